```python
import jax, jax.numpy as jnp
from jax import lax
import numpy as np

D_MODEL = 1024
BATCH = 8
SEQ = 4096
DEPTH = 2

CHUNK = 64
N_MIXERS = 2
N_RET = (DEPTH + 1) // 2
N_MLA = DEPTH // 2
ROPE_THETA = 10000.0
EPS = 1e-6
RET_HEADS = 4
RET_DK = D_MODEL // RET_HEADS
RET_DV = 2 * RET_DK
RET_QK_W = RET_HEADS * RET_DK
RET_V_W = RET_HEADS * RET_DV
RET_IN = 2 * RET_QK_W + 2 * RET_V_W
MLA_HEADS = 8
MLA_NOPE = 128
MLA_ROPE = 64
MLA_QKD = MLA_NOPE + MLA_ROPE
MLA_VD = 128
MLA_Q_RANK = 384
MLA_KV_RANK = 256
MLA_IN = MLA_Q_RANK + MLA_KV_RANK + MLA_ROPE
Q_BLOCK = 128
D_FF = 4 * D_MODEL
PLE_DIM = 256

kernel_name = "hybrid_retention_mla_trunk"


def _rmsnorm(x, g):
    xf = x.astype(jnp.float32)
    y = xf * lax.rsqrt(jnp.mean(xf * xf, axis=-1, keepdims=True) + EPS)
    return (y * g.astype(jnp.float32)).astype(x.dtype)


def _rope_tables(seq, dim, dtype):
    inv = 1.0 / (ROPE_THETA ** (jnp.arange(0, dim, 2, dtype=jnp.float32) / dim))
    ang = jnp.arange(seq, dtype=jnp.float32)[:, None] * inv[None, :]
    return jnp.cos(ang)[:, None, :].astype(dtype), jnp.sin(ang)[:, None, :].astype(dtype)


def _rope(x, cos, sin):
    x1, x2 = jnp.split(x, 2, axis=-1)
    return jnp.concatenate([x1 * cos - x2 * sin, x2 * cos + x1 * sin], axis=-1)


def _retention(h, w_in, gn_gain, w_out):
    B, S, _ = h.shape
    nc = S // CHUNK
    proj = h @ w_in
    q, k, v, g = jnp.split(proj, [RET_QK_W, 2 * RET_QK_W, 2 * RET_QK_W + RET_V_W], axis=-1)
    cos, sin = _rope_tables(S, RET_DK, h.dtype)
    q = _rope(q.reshape(B, S, RET_HEADS, RET_DK), cos, sin)
    k = _rope(k.reshape(B, S, RET_HEADS, RET_DK), cos, sin) * (RET_DK ** -0.5)
    v = v.reshape(B, S, RET_HEADS, RET_DV)

    def to_chunks(t):
        return t.reshape(B, nc, CHUNK, RET_HEADS, t.shape[-1]).transpose(1, 0, 3, 2, 4)

    qc, kc, vc = to_chunks(q), to_chunks(k), to_chunks(v)
    log_gamma = jnp.log(1.0 - 2.0 ** (-5.0 - jnp.arange(RET_HEADS, dtype=jnp.float32)))
    idx = jnp.arange(CHUNK, dtype=jnp.float32)
    dist = jnp.abs(idx[:, None] - idx[None, :])
    intra = jnp.exp(log_gamma[:, None, None] * dist).astype(h.dtype)
    q_decay = jnp.exp(log_gamma[:, None] * (idx + 1.0))[None, :, :, None].astype(h.dtype)
    k_decay = jnp.exp(log_gamma[:, None] * (CHUNK - 1.0 - idx))[None, :, :, None].astype(h.dtype)
    chunk_decay = jnp.exp(log_gamma * CHUNK)[None, :, None, None].astype(h.dtype)

    def step(state, xs):
        qi, ki, vi = xs
        scores = jnp.einsum('bhnd,bhmd->bhnm', qi, ki) * intra
        inner = jnp.einsum('bhnm,bhmv->bhnv', scores, vi)
        cross = jnp.einsum('bhnd,bhdv->bhnv', qi * q_decay, state)
        new_state = state * chunk_decay + jnp.einsum('bhmd,bhmv->bhdv', ki * k_decay, vi)
        return new_state, inner + cross

    state0 = jnp.zeros((B, RET_HEADS, RET_DK, RET_DV), h.dtype)
    _, out = lax.scan(step, state0, (qc, kc, vc))
    out = out.transpose(1, 0, 3, 2, 4).reshape(B, S, RET_HEADS, RET_DV)
    out = _rmsnorm(out, gn_gain).reshape(B, S, RET_V_W)
    return (jax.nn.silu(g) * out) @ w_out


def _mla(h, w_in, q_a_gain, kv_a_gain, w_uq, w_ukv, q_gain, k_gain, w_out):
    B, S, _ = h.shape
    proj = h @ w_in
    c_q, c_kv, k_r = jnp.split(proj, [MLA_Q_RANK, MLA_Q_RANK + MLA_KV_RANK], axis=-1)
    q = (_rmsnorm(c_q, q_a_gain) @ w_uq).reshape(B, S, MLA_HEADS, MLA_QKD)
    kv = (_rmsnorm(c_kv, kv_a_gain) @ w_ukv).reshape(B, S, MLA_HEADS, MLA_NOPE + MLA_VD)
    k_nope, v = jnp.split(kv, [MLA_NOPE], axis=-1)
    k = jnp.concatenate([k_nope, jnp.broadcast_to(k_r[:, :, None, :], (B, S, MLA_HEADS, MLA_ROPE))], axis=-1)
    q = _rmsnorm(q, q_gain)
    k = _rmsnorm(k, k_gain)
    cos, sin = _rope_tables(S, MLA_ROPE, h.dtype)
    q = jnp.concatenate([q[..., :MLA_NOPE], _rope(q[..., MLA_NOPE:], cos, sin)], axis=-1)
    k = jnp.concatenate([k[..., :MLA_NOPE], _rope(k[..., MLA_NOPE:], cos, sin)], axis=-1)
    q = q.transpose(0, 2, 1, 3)
    k = k.transpose(0, 2, 1, 3)
    v = v.transpose(0, 2, 1, 3)
    scale = MLA_QKD ** -0.5
    outs = []
    for blk in range(S // Q_BLOCK):
        q0 = blk * Q_BLOCK
        kend = q0 + Q_BLOCK
        qb = q[:, :, q0:kend]
        kb = k[:, :, :kend]
        vb = v[:, :, :kend]
        s = jnp.einsum('bhqd,bhkd->bhqk', qb, kb).astype(jnp.float32) * scale
        q_chunk = (q0 + jnp.arange(Q_BLOCK)) // CHUNK
        k_chunk = jnp.arange(kend) // CHUNK
        s = jnp.where(k_chunk[None, :] <= q_chunk[:, None], s, -1e30)
        pr = jax.nn.softmax(s, axis=-1).astype(vb.dtype)
        outs.append(jnp.einsum('bhqk,bhkd->bhqd', pr, vb))
    o = jnp.concatenate(outs, axis=2).transpose(0, 2, 1, 3).reshape(B, S, MLA_HEADS * MLA_VD)
    return o @ w_out


def _dense(k, shape, fan_in):
    return jax.random.normal(k, shape, jnp.float32) * (fan_in ** -0.5)


def _gain(k, shape):
    return 1.0 + 0.05 * jax.random.normal(k, shape, jnp.float32)


def _fwd_setup_inputs(seed: int = 0) -> dict:
    key = jax.random.key(seed)
    ks = jax.random.split(key, 20)
    return {
        "x": jax.random.normal(ks[0], (BATCH, SEQ, D_MODEL), jnp.float32),
        "p": jax.random.normal(ks[1], (DEPTH, BATCH, SEQ, PLE_DIM), jnp.float32),
        "mix_norm": _gain(ks[2], (DEPTH, D_MODEL)),
        "ret_w_in": _dense(ks[3], (N_RET, D_MODEL, RET_IN), D_MODEL),
        "ret_gn": _gain(ks[4], (N_RET, RET_HEADS, RET_DV)),
        "ret_w_out": _dense(ks[5], (N_RET, RET_V_W, D_MODEL), RET_V_W),
        "mla_w_in": _dense(ks[6], (N_MLA, D_MODEL, MLA_IN), D_MODEL),
        "mla_q_a_norm": _gain(ks[7], (N_MLA, MLA_Q_RANK)),
        "mla_kv_a_norm": _gain(ks[8], (N_MLA, MLA_KV_RANK)),
        "mla_w_uq": _dense(ks[9], (N_MLA, MLA_Q_RANK, MLA_HEADS * MLA_QKD), MLA_Q_RANK),
        "mla_w_ukv": _dense(ks[10], (N_MLA, MLA_KV_RANK, MLA_HEADS * (MLA_NOPE + MLA_VD)), MLA_KV_RANK),
        "mla_q_norm": _gain(ks[11], (N_MLA, MLA_QKD)),
        "mla_k_norm": _gain(ks[12], (N_MLA, MLA_QKD)),
        "mla_w_out": _dense(ks[13], (N_MLA, MLA_HEADS * MLA_VD, D_MODEL), MLA_HEADS * MLA_VD),
        "mlp_norm": _gain(ks[14], (DEPTH, D_MODEL)),
        "mlp_w1": _dense(ks[15], (DEPTH, D_MODEL, D_FF), D_MODEL),
        "mlp_w2": _dense(ks[16], (DEPTH, D_FF, D_MODEL), D_FF),
        "ple_norm": _gain(ks[17], (DEPTH, D_MODEL)),
        "ple_gate_w": _dense(ks[18], (DEPTH, D_MODEL, D_MODEL), D_MODEL),
        "ple_proj_w": _dense(ks[19], (DEPTH, PLE_DIM, D_MODEL), PLE_DIM),
    }


def _fwd_reference(x, p, mix_norm, ret_w_in, ret_gn, ret_w_out, mla_w_in, mla_q_a_norm, mla_kv_a_norm,
              mla_w_uq, mla_w_ukv, mla_q_norm, mla_k_norm, mla_w_out, mlp_norm, mlp_w1, mlp_w2,
              ple_norm, ple_gate_w, ple_proj_w):
    h = x
    for i in range(DEPTH):
        j = i // N_MIXERS
        hn = _rmsnorm(h, mix_norm[i])
        if i % N_MIXERS == 0:
            mixed = _retention(hn, ret_w_in[j], ret_gn[j], ret_w_out[j])
        else:
            mixed = _mla(hn, mla_w_in[j], mla_q_a_norm[j], mla_kv_a_norm[j], mla_w_uq[j], mla_w_ukv[j],
                         mla_q_norm[j], mla_k_norm[j], mla_w_out[j])
        h = h + mixed
        hn = _rmsnorm(h, mlp_norm[i])
        h = h + jnp.square(jax.nn.relu(hn @ mlp_w1[i])) @ mlp_w2[i]
        gate = jax.nn.sigmoid(_rmsnorm(h, ple_norm[i]) @ ple_gate_w[i])
        h = h + gate * (p[i] @ ple_proj_w[i])
    return h


import jax as _jax
import jax.numpy as _jnp

TWIN_FORMAT = 'train_step'
FWD_PARAMS = ['x', 'p', 'mix_norm', 'ret_w_in', 'ret_gn', 'ret_w_out', 'mla_w_in', 'mla_q_a_norm', 'mla_kv_a_norm', 'mla_w_uq', 'mla_w_ukv', 'mla_q_norm', 'mla_k_norm', 'mla_w_out', 'mlp_norm', 'mlp_w1', 'mlp_w2', 'ple_norm', 'ple_gate_w', 'ple_proj_w']
TWIN_WEIGHTS = ['mix_norm', 'ret_w_in', 'ret_gn', 'ret_w_out', 'mla_w_in', 'mla_q_a_norm', 'mla_kv_a_norm', 'mla_w_uq', 'mla_w_ukv', 'mla_q_norm', 'mla_k_norm', 'mla_w_out', 'mlp_norm', 'mlp_w1', 'mlp_w2', 'ple_norm', 'ple_gate_w', 'ple_proj_w']
TWIN_DIFF_INPUT = 'x'
TWIN_INPUTS = ['x', 'p', 'mix_norm', 'ret_w_in', 'ret_gn', 'ret_w_out', 'mla_w_in', 'mla_q_a_norm', 'mla_kv_a_norm', 'mla_w_uq', 'mla_w_ukv', 'mla_q_norm', 'mla_k_norm', 'mla_w_out', 'mlp_norm', 'mlp_w1', 'mlp_w2', 'ple_norm', 'ple_gate_w', 'ple_proj_w', 'loss_target', 'm_mix_norm', 'm_ret_w_in', 'm_ret_gn', 'm_ret_w_out', 'm_mla_w_in', 'm_mla_q_a_norm', 'm_mla_kv_a_norm', 'm_mla_w_uq', 'm_mla_w_ukv', 'm_mla_q_norm', 'm_mla_k_norm', 'm_mla_w_out', 'm_mlp_norm', 'm_mlp_w1', 'm_mlp_w2', 'm_ple_norm', 'm_ple_gate_w', 'm_ple_proj_w', 'v_mix_norm', 'v_ret_w_in', 'v_ret_gn', 'v_ret_w_out', 'v_mla_w_in', 'v_mla_q_a_norm', 'v_mla_kv_a_norm', 'v_mla_w_uq', 'v_mla_w_ukv', 'v_mla_q_norm', 'v_mla_k_norm', 'v_mla_w_out', 'v_mlp_norm', 'v_mlp_w1', 'v_mlp_w2', 'v_ple_norm', 'v_ple_gate_w', 'v_ple_proj_w']
TWIN_OUTPUTS = ['loss', 'grad_x', 'grad_mix_norm', 'grad_ret_w_in', 'grad_ret_gn', 'grad_ret_w_out', 'grad_mla_w_in', 'grad_mla_q_a_norm', 'grad_mla_kv_a_norm', 'grad_mla_w_uq', 'grad_mla_w_ukv', 'grad_mla_q_norm', 'grad_mla_k_norm', 'grad_mla_w_out', 'grad_mlp_norm', 'grad_mlp_w1', 'grad_mlp_w2', 'grad_ple_norm', 'grad_ple_gate_w', 'grad_ple_proj_w', 'delta_mix_norm', 'delta_ret_w_in', 'delta_ret_gn', 'delta_ret_w_out', 'delta_mla_w_in', 'delta_mla_q_a_norm', 'delta_mla_kv_a_norm', 'delta_mla_w_uq', 'delta_mla_w_ukv', 'delta_mla_q_norm', 'delta_mla_k_norm', 'delta_mla_w_out', 'delta_mlp_norm', 'delta_mlp_w1', 'delta_mlp_w2', 'delta_ple_norm', 'delta_ple_gate_w', 'delta_ple_proj_w', 'new_m_mix_norm', 'new_m_ret_w_in', 'new_m_ret_gn', 'new_m_ret_w_out', 'new_m_mla_w_in', 'new_m_mla_q_a_norm', 'new_m_mla_kv_a_norm', 'new_m_mla_w_uq', 'new_m_mla_w_ukv', 'new_m_mla_q_norm', 'new_m_mla_k_norm', 'new_m_mla_w_out', 'new_m_mlp_norm', 'new_m_mlp_w1', 'new_m_mlp_w2', 'new_m_ple_norm', 'new_m_ple_gate_w', 'new_m_ple_proj_w', 'new_v_mix_norm', 'new_v_ret_w_in', 'new_v_ret_gn', 'new_v_ret_w_out', 'new_v_mla_w_in', 'new_v_mla_q_a_norm', 'new_v_mla_kv_a_norm', 'new_v_mla_w_uq', 'new_v_mla_w_ukv', 'new_v_mla_q_norm', 'new_v_mla_k_norm', 'new_v_mla_w_out', 'new_v_mlp_norm', 'new_v_mlp_w1', 'new_v_mlp_w2', 'new_v_ple_norm', 'new_v_ple_gate_w', 'new_v_ple_proj_w']
TWIN_LEAF_KINDS = {'loss': 'loss', 'grad_x': 'grad_x', 'grad_mix_norm': 'grad_w', 'grad_ret_w_in': 'grad_w', 'grad_ret_gn': 'grad_w', 'grad_ret_w_out': 'grad_w', 'grad_mla_w_in': 'grad_w', 'grad_mla_q_a_norm': 'grad_w', 'grad_mla_kv_a_norm': 'grad_w', 'grad_mla_w_uq': 'grad_w', 'grad_mla_w_ukv': 'grad_w', 'grad_mla_q_norm': 'grad_w', 'grad_mla_k_norm': 'grad_w', 'grad_mla_w_out': 'grad_w', 'grad_mlp_norm': 'grad_w', 'grad_mlp_w1': 'grad_w', 'grad_mlp_w2': 'grad_w', 'grad_ple_norm': 'grad_w', 'grad_ple_gate_w': 'grad_w', 'grad_ple_proj_w': 'grad_w', 'delta_mix_norm': 'delta_w', 'delta_ret_w_in': 'delta_w', 'delta_ret_gn': 'delta_w', 'delta_ret_w_out': 'delta_w', 'delta_mla_w_in': 'delta_w', 'delta_mla_q_a_norm': 'delta_w', 'delta_mla_kv_a_norm': 'delta_w', 'delta_mla_w_uq': 'delta_w', 'delta_mla_w_ukv': 'delta_w', 'delta_mla_q_norm': 'delta_w', 'delta_mla_k_norm': 'delta_w', 'delta_mla_w_out': 'delta_w', 'delta_mlp_norm': 'delta_w', 'delta_mlp_w1': 'delta_w', 'delta_mlp_w2': 'delta_w', 'delta_ple_norm': 'delta_w', 'delta_ple_gate_w': 'delta_w', 'delta_ple_proj_w': 'delta_w', 'new_m_mix_norm': 'new_m', 'new_m_ret_w_in': 'new_m', 'new_m_ret_gn': 'new_m', 'new_m_ret_w_out': 'new_m', 'new_m_mla_w_in': 'new_m', 'new_m_mla_q_a_norm': 'new_m', 'new_m_mla_kv_a_norm': 'new_m', 'new_m_mla_w_uq': 'new_m', 'new_m_mla_w_ukv': 'new_m', 'new_m_mla_q_norm': 'new_m', 'new_m_mla_k_norm': 'new_m', 'new_m_mla_w_out': 'new_m', 'new_m_mlp_norm': 'new_m', 'new_m_mlp_w1': 'new_m', 'new_m_mlp_w2': 'new_m', 'new_m_ple_norm': 'new_m', 'new_m_ple_gate_w': 'new_m', 'new_m_ple_proj_w': 'new_m', 'new_v_mix_norm': 'new_v', 'new_v_ret_w_in': 'new_v', 'new_v_ret_gn': 'new_v', 'new_v_ret_w_out': 'new_v', 'new_v_mla_w_in': 'new_v', 'new_v_mla_q_a_norm': 'new_v', 'new_v_mla_kv_a_norm': 'new_v', 'new_v_mla_w_uq': 'new_v', 'new_v_mla_w_ukv': 'new_v', 'new_v_mla_q_norm': 'new_v', 'new_v_mla_k_norm': 'new_v', 'new_v_mla_w_out': 'new_v', 'new_v_mlp_norm': 'new_v', 'new_v_mlp_w1': 'new_v', 'new_v_mlp_w2': 'new_v', 'new_v_ple_norm': 'new_v', 'new_v_ple_gate_w': 'new_v', 'new_v_ple_proj_w': 'new_v'}


def _forward(args):
    return _fwd_reference(*[args[k] for k in FWD_PARAMS])


def _output_shape():
    out = _jax.eval_shape(lambda: _forward(_fwd_setup_inputs(0)))
    return out.shape, out.dtype

N_MICROBATCH = 1
ADAM_LR = 0.001
ADAM_B1 = 0.9
ADAM_B2 = 0.999
ADAM_EPS = 1e-08
ADAM_WD = 0.01
ADAM_STEP = 10
PER_EXAMPLE_BATCH_AXIS = {'x': 0, 'p': 1, 'loss_target': 0}
SHARED_INPUTS = []
_WEIGHT_DTYPES = {'mix_norm': _jnp.float32, 'ret_w_in': _jnp.float32, 'ret_gn': _jnp.float32, 'ret_w_out': _jnp.float32, 'mla_w_in': _jnp.float32, 'mla_q_a_norm': _jnp.float32, 'mla_kv_a_norm': _jnp.float32, 'mla_w_uq': _jnp.float32, 'mla_w_ukv': _jnp.float32, 'mla_q_norm': _jnp.float32, 'mla_k_norm': _jnp.float32, 'mla_w_out': _jnp.float32, 'mlp_norm': _jnp.float32, 'mlp_w1': _jnp.float32, 'mlp_w2': _jnp.float32, 'ple_norm': _jnp.float32, 'ple_gate_w': _jnp.float32, 'ple_proj_w': _jnp.float32}
MOMENT_SCALE = {'mix_norm': 1.144277e+01, 'ret_w_in': 5.816688e-01, 'ret_gn': 5.214144e+00, 'ret_w_out': 7.713739e-01, 'mla_w_in': 1.368744e+01, 'mla_q_a_norm': 6.927924e-01, 'mla_kv_a_norm': 2.733757e+01, 'mla_w_uq': 3.605616e-01, 'mla_w_ukv': 6.930558e+00, 'mla_q_norm': 1.117003e+00, 'mla_k_norm': 1.107318e+00, 'mla_w_out': 9.427353e+00, 'mlp_norm': 9.860116e+01, 'mlp_w1': 4.077198e+00, 'mlp_w2': 1.871608e+01, 'ple_norm': 1.401976e+00, 'ple_gate_w': 9.950256e-01, 'ple_proj_w': 5.021345e-01}


def _to_microbatches(a, axis):
    t = _jnp.moveaxis(a, axis, 0)
    t = t.reshape((N_MICROBATCH, t.shape[0] // N_MICROBATCH) + t.shape[1:])
    return _jnp.moveaxis(t, 1, axis + 1)


def setup_inputs(seed: int = 0) -> dict:
    inp = _fwd_setup_inputs(seed)
    key = _jax.random.fold_in(_jax.random.key(seed), 7919)
    shape, _ = _output_shape()
    out = dict(inp)
    out["loss_target"] = _jax.random.normal(_jax.random.fold_in(key, 0), shape, _jnp.float32)
    for i, name in enumerate(TWIN_WEIGHTS):
        w = inp[name].astype(_jnp.float32)
        if MOMENT_SCALE is None:
            s = _jnp.sqrt(_jnp.mean(_jnp.square(w)) + 1e-30)
        else:
            s = MOMENT_SCALE[name]
        km, kv = _jax.random.split(_jax.random.fold_in(key, i + 1))
        out[name] = w
        out["m_" + name] = s * _jax.random.normal(km, w.shape, _jnp.float32)
        out["v_" + name] = (s * s) * _jax.random.uniform(kv, w.shape, _jnp.float32, 0.5, 1.5)
    if N_MICROBATCH > 1:
        for name, axis in PER_EXAMPLE_BATCH_AXIS.items():
            out[name] = _to_microbatches(out[name], axis)
    return {'x': out['x'], 'p': out['p'], 'mix_norm': out['mix_norm'], 'ret_w_in': out['ret_w_in'], 'ret_gn': out['ret_gn'], 'ret_w_out': out['ret_w_out'], 'mla_w_in': out['mla_w_in'], 'mla_q_a_norm': out['mla_q_a_norm'], 'mla_kv_a_norm': out['mla_kv_a_norm'], 'mla_w_uq': out['mla_w_uq'], 'mla_w_ukv': out['mla_w_ukv'], 'mla_q_norm': out['mla_q_norm'], 'mla_k_norm': out['mla_k_norm'], 'mla_w_out': out['mla_w_out'], 'mlp_norm': out['mlp_norm'], 'mlp_w1': out['mlp_w1'], 'mlp_w2': out['mlp_w2'], 'ple_norm': out['ple_norm'], 'ple_gate_w': out['ple_gate_w'], 'ple_proj_w': out['ple_proj_w'], 'loss_target': out['loss_target'], 'm_mix_norm': out['m_mix_norm'], 'm_ret_w_in': out['m_ret_w_in'], 'm_ret_gn': out['m_ret_gn'], 'm_ret_w_out': out['m_ret_w_out'], 'm_mla_w_in': out['m_mla_w_in'], 'm_mla_q_a_norm': out['m_mla_q_a_norm'], 'm_mla_kv_a_norm': out['m_mla_kv_a_norm'], 'm_mla_w_uq': out['m_mla_w_uq'], 'm_mla_w_ukv': out['m_mla_w_ukv'], 'm_mla_q_norm': out['m_mla_q_norm'], 'm_mla_k_norm': out['m_mla_k_norm'], 'm_mla_w_out': out['m_mla_w_out'], 'm_mlp_norm': out['m_mlp_norm'], 'm_mlp_w1': out['m_mlp_w1'], 'm_mlp_w2': out['m_mlp_w2'], 'm_ple_norm': out['m_ple_norm'], 'm_ple_gate_w': out['m_ple_gate_w'], 'm_ple_proj_w': out['m_ple_proj_w'], 'v_mix_norm': out['v_mix_norm'], 'v_ret_w_in': out['v_ret_w_in'], 'v_ret_gn': out['v_ret_gn'], 'v_ret_w_out': out['v_ret_w_out'], 'v_mla_w_in': out['v_mla_w_in'], 'v_mla_q_a_norm': out['v_mla_q_a_norm'], 'v_mla_kv_a_norm': out['v_mla_kv_a_norm'], 'v_mla_w_uq': out['v_mla_w_uq'], 'v_mla_w_ukv': out['v_mla_w_ukv'], 'v_mla_q_norm': out['v_mla_q_norm'], 'v_mla_k_norm': out['v_mla_k_norm'], 'v_mla_w_out': out['v_mla_w_out'], 'v_mlp_norm': out['v_mlp_norm'], 'v_mlp_w1': out['v_mlp_w1'], 'v_mlp_w2': out['v_mlp_w2'], 'v_ple_norm': out['v_ple_norm'], 'v_ple_gate_w': out['v_ple_gate_w'], 'v_ple_proj_w': out['v_ple_proj_w']}


def _loss(weights, diff, rest, loss_target):
    with _jax.named_scope("forward"):
        args = {**rest, TWIN_DIFF_INPUT: diff, **{k: w.astype(_WEIGHT_DTYPES[k]) for k, w in weights.items()}}
        y = _forward(args)
    with _jax.named_scope("loss_head"):
        err = _jnp.square(y.astype(_jnp.float32) - loss_target)
        return 0.5 * _jnp.sum(_jnp.mean(err, axis=-1)) if err.ndim else 0.5 * err


def _adamw(w, g, m, v):
    m = ADAM_B1 * m + (1.0 - ADAM_B1) * g
    v = ADAM_B2 * v + (1.0 - ADAM_B2) * _jnp.square(g)
    m_hat = m / (1.0 - ADAM_B1 ** ADAM_STEP)
    v_hat = v / (1.0 - ADAM_B2 ** ADAM_STEP)
    delta = -ADAM_LR * (m_hat / (_jnp.sqrt(v_hat) + ADAM_EPS) + ADAM_WD * w)
    return delta, m, v


def reference(x, p, mix_norm, ret_w_in, ret_gn, ret_w_out, mla_w_in, mla_q_a_norm, mla_kv_a_norm, mla_w_uq, mla_w_ukv, mla_q_norm, mla_k_norm, mla_w_out, mlp_norm, mlp_w1, mlp_w2, ple_norm, ple_gate_w, ple_proj_w, loss_target, m_mix_norm, m_ret_w_in, m_ret_gn, m_ret_w_out, m_mla_w_in, m_mla_q_a_norm, m_mla_kv_a_norm, m_mla_w_uq, m_mla_w_ukv, m_mla_q_norm, m_mla_k_norm, m_mla_w_out, m_mlp_norm, m_mlp_w1, m_mlp_w2, m_ple_norm, m_ple_gate_w, m_ple_proj_w, v_mix_norm, v_ret_w_in, v_ret_gn, v_ret_w_out, v_mla_w_in, v_mla_q_a_norm, v_mla_kv_a_norm, v_mla_w_uq, v_mla_w_ukv, v_mla_q_norm, v_mla_k_norm, v_mla_w_out, v_mlp_norm, v_mlp_w1, v_mlp_w2, v_ple_norm, v_ple_gate_w, v_ple_proj_w):
    given = dict(x=x, p=p, mix_norm=mix_norm, ret_w_in=ret_w_in, ret_gn=ret_gn, ret_w_out=ret_w_out, mla_w_in=mla_w_in, mla_q_a_norm=mla_q_a_norm, mla_kv_a_norm=mla_kv_a_norm, mla_w_uq=mla_w_uq, mla_w_ukv=mla_w_ukv, mla_q_norm=mla_q_norm, mla_k_norm=mla_k_norm, mla_w_out=mla_w_out, mlp_norm=mlp_norm, mlp_w1=mlp_w1, mlp_w2=mlp_w2, ple_norm=ple_norm, ple_gate_w=ple_gate_w, ple_proj_w=ple_proj_w, loss_target=loss_target, m_mix_norm=m_mix_norm, m_ret_w_in=m_ret_w_in, m_ret_gn=m_ret_gn, m_ret_w_out=m_ret_w_out, m_mla_w_in=m_mla_w_in, m_mla_q_a_norm=m_mla_q_a_norm, m_mla_kv_a_norm=m_mla_kv_a_norm, m_mla_w_uq=m_mla_w_uq, m_mla_w_ukv=m_mla_w_ukv, m_mla_q_norm=m_mla_q_norm, m_mla_k_norm=m_mla_k_norm, m_mla_w_out=m_mla_w_out, m_mlp_norm=m_mlp_norm, m_mlp_w1=m_mlp_w1, m_mlp_w2=m_mlp_w2, m_ple_norm=m_ple_norm, m_ple_gate_w=m_ple_gate_w, m_ple_proj_w=m_ple_proj_w, v_mix_norm=v_mix_norm, v_ret_w_in=v_ret_w_in, v_ret_gn=v_ret_gn, v_ret_w_out=v_ret_w_out, v_mla_w_in=v_mla_w_in, v_mla_q_a_norm=v_mla_q_a_norm, v_mla_kv_a_norm=v_mla_kv_a_norm, v_mla_w_uq=v_mla_w_uq, v_mla_w_ukv=v_mla_w_ukv, v_mla_q_norm=v_mla_q_norm, v_mla_k_norm=v_mla_k_norm, v_mla_w_out=v_mla_w_out, v_mlp_norm=v_mlp_norm, v_mlp_w1=v_mlp_w1, v_mlp_w2=v_mlp_w2, v_ple_norm=v_ple_norm, v_ple_gate_w=v_ple_gate_w, v_ple_proj_w=v_ple_proj_w)
    weights = {n: given[n] for n in TWIN_WEIGHTS}
    shared = {n: given[n] for n in SHARED_INPUTS}
    per_example = {n: given[n] for n in ['x', 'p']}
    grad_fn = _jax.value_and_grad(_loss, argnums=(0, 1))

    def one_microbatch(ex, loss_target):
        ex = dict(ex)
        diff = ex.pop(TWIN_DIFF_INPUT)
        return grad_fn(weights, diff, {**shared, **ex}, loss_target)

    if N_MICROBATCH == 1:
        loss, (grad_w, grad_x) = one_microbatch(per_example, given["loss_target"])
    else:
        def body(carry, xs):
            loss_sum, grad_sum = carry
            l_k, (gw_k, gx_k) = one_microbatch(xs[0], xs[1])
            with _jax.named_scope("update"):
                return (loss_sum + l_k, _jax.tree.map(_jnp.add, grad_sum, gw_k)), gx_k

        init = (_jnp.zeros((), _jnp.float32), _jax.tree.map(_jnp.zeros_like, weights))
        (loss, grad_w), grad_x = _jax.lax.scan(body, init, (per_example, given["loss_target"]))
    with _jax.named_scope("update"):
        delta_w, new_m, new_v = {}, {}, {}
        for n in TWIN_WEIGHTS:
            delta_w[n], new_m[n], new_v[n] = _adamw(weights[n], grad_w[n], given["m_" + n], given["v_" + n])
    return (loss, grad_x, *[grad_w[n] for n in TWIN_WEIGHTS], *[delta_w[n] for n in TWIN_WEIGHTS],
            *[new_m[n] for n in TWIN_WEIGHTS], *[new_v[n] for n in TWIN_WEIGHTS])
```

```python
import functools
import math

import jax
import jax.numpy as jnp
from jax import lax
from jax.experimental import pallas as pl
from jax.experimental.pallas import tpu as pltpu

F32 = jnp.float32
BF16 = jnp.bfloat16
MESH = pl.DeviceIdType.MESH
ANY = pl.BlockSpec(memory_space=pl.ANY)

N_DEV = 8
D_MODEL = 1024
CHUNK = 64
EPS = 1e-6
ROPE_THETA = 10000.0
RET_HEADS = 4
RET_DK = 256
RET_DV = 512
RET_QK_W = RET_HEADS * RET_DK
RET_V_W = RET_HEADS * RET_DV
RET_IN = 2 * RET_QK_W + 2 * RET_V_W
MLA_HEADS = 8
MLA_NOPE = 128
MLA_ROPE = 64
MLA_QKD = MLA_NOPE + MLA_ROPE
MLA_VD = 128
MLA_Q_RANK = 384
MLA_KV_RANK = 256
MLA_IN = MLA_Q_RANK + MLA_KV_RANK + MLA_ROPE
MLA_IN_PAD = 768
MLA_HD_PAD = 256
D_FF = 4096
PLE_DIM = 256
ATT_SCALE = MLA_QKD ** -0.5

ADAM_LR = 0.001
ADAM_B1 = 0.9
ADAM_B2 = 0.999
ADAM_EPS = 1e-08
ADAM_WD = 0.01
ADAM_STEP = 10

VMEM_LIMIT = 52 * 1024 * 1024
ROW_TILE = 1024
RET_ROWS = 256
ATT_BLOCK = 256

WEIGHTS = ['mix_norm', 'ret_w_in', 'ret_gn', 'ret_w_out', 'mla_w_in', 'mla_q_a_norm', 'mla_kv_a_norm',
           'mla_w_uq', 'mla_w_ukv', 'mla_q_norm', 'mla_k_norm', 'mla_w_out', 'mlp_norm', 'mlp_w1', 'mlp_w2',
           'ple_norm', 'ple_gate_w', 'ple_proj_w']
BIG = ['ret_w_in', 'ret_w_out', 'mla_w_in', 'mla_w_uq', 'mla_w_ukv', 'mla_w_out', 'mlp_w1', 'mlp_w2',
       'ple_gate_w', 'ple_proj_w']
SMALL = [w for w in WEIGHTS if w not in BIG]


def _cparams(sem=None):
    return pltpu.CompilerParams(dimension_semantics=sem, vmem_limit_bytes=VMEM_LIMIT)


def _dot(a, b, ca, cb):
    return lax.dot_general(a, b, (((ca,), (cb,)), ((), ())), preferred_element_type=F32)


def _bf(v):
    return v if v.dtype == BF16 else v.astype(BF16)


def _sigmoid(z):
    return 1.0 / (1.0 + jnp.exp(-z))


def _mm(name, grid, a, a_spec, b, b_spec, contract, outs, extras=(), epi=None):
    nk = grid[2]
    n_ex, n_out = len(extras), len(outs)
    acc_shape = tuple(d for d in outs[0][1].block_shape if d is not None)

    def body(*refs):
        a_ref, b_ref = refs[:2]
        ex_refs = refs[2:2 + n_ex]
        out_refs = refs[2 + n_ex:2 + n_ex + n_out]

        def product():
            return _dot(_bf(a_ref[...]), _bf(b_ref[...]), contract[0], contract[1])

        def finish(acc):
            res = epi(acc, *[r[...] for r in ex_refs]) if epi is not None else (acc,)
            for o, r in zip(out_refs, res):
                o[...] = r.astype(o.dtype)

        if nk == 1:
            finish(product())
        else:
            acc_ref = refs[-1]
            k = pl.program_id(2)

            @pl.when(k == 0)
            def _():
                acc_ref[...] = jnp.zeros_like(acc_ref)

            acc_ref[...] += product()

            @pl.when(k == nk - 1)
            def _():
                finish(acc_ref[...])

    return pl.pallas_call(
        body, name=name, grid=grid,
        in_specs=[a_spec, b_spec] + [s for _, s in extras],
        out_specs=[s for _, s in outs],
        out_shape=[s for s, _ in outs],
        scratch_shapes=[pltpu.VMEM(acc_shape, F32)] if nk > 1 else [],
        compiler_params=_cparams(("parallel", "parallel", "arbitrary")),
    )(a, b, *[x for x, _ in extras])


def _sds(shape, dtype):
    return jax.ShapeDtypeStruct(shape, dtype)


def _row_tile(t, cap=ROW_TILE):
    return min(cap, t)


def _rms_fwd(name, x, g):
    t, d = x.shape
    tm = _row_tile(t)

    def body(x_ref, g_ref, o_ref):
        xv = x_ref[...]
        r = lax.rsqrt(jnp.mean(xv * xv, axis=-1, keepdims=True) + EPS)
        o_ref[...] = (xv * r * g_ref[...]).astype(o_ref.dtype)

    return pl.pallas_call(
        body, name=name, grid=(t // tm,),
        in_specs=[pl.BlockSpec((tm, d), lambda i: (i, 0)), pl.BlockSpec((1, d), lambda i: (0, 0))],
        out_specs=pl.BlockSpec((tm, d), lambda i: (i, 0)),
        out_shape=_sds((t, d), BF16),
        compiler_params=_cparams(("parallel",)),
    )(x, g)


def _rms_bwd_rows(dy, xv, g, n):
    r = lax.rsqrt(jnp.sum(xv * xv, axis=-1, keepdims=True) / n + EPS)
    xh = xv * r
    dxh = dy * g
    dx = r * (dxh - xh * (jnp.sum(dxh * xh, axis=-1, keepdims=True) / n))
    return dx, dy * xh


def _rms_bwd(name, dy, x, g, res):
    t, d = x.shape
    tm = _row_tile(t, 512)

    def body(dy_ref, x_ref, g_ref, res_ref, dx_ref, dg_ref):
        @pl.when(pl.program_id(0) == 0)
        def _():
            dg_ref[...] = jnp.zeros_like(dg_ref)

        dx, dgr = _rms_bwd_rows(dy_ref[...], x_ref[...], g_ref[...], d)
        dx_ref[...] = res_ref[...] + dx
        dg_ref[...] += jnp.sum(dgr, axis=0, keepdims=True)

    row = pl.BlockSpec((tm, d), lambda i: (i, 0))
    vec = pl.BlockSpec((1, d), lambda i: (0, 0))
    return pl.pallas_call(
        body, name=name, grid=(t // tm,),
        in_specs=[row, row, vec, row], out_specs=[row, vec],
        out_shape=[_sds((t, d), F32), _sds((1, d), F32)],
        compiler_params=_cparams(("arbitrary",)),
    )(dy, x, g, res)


def _loss_head(y, target):
    t, d = y.shape
    tm = _row_tile(t)

    def body(y_ref, t_ref, dy_ref, l_ref):
        @pl.when(pl.program_id(0) == 0)
        def _():
            l_ref[...] = jnp.zeros_like(l_ref)

        e = y_ref[...] - t_ref[...]
        dy_ref[...] = e / d
        l_ref[...] += jnp.sum(jnp.sum(e * e, axis=-1, keepdims=True), axis=0, keepdims=True)

    row = pl.BlockSpec((tm, d), lambda i: (i, 0))
    return pl.pallas_call(
        body, name="loss_head", grid=(t // tm,),
        in_specs=[row, row], out_specs=[row, pl.BlockSpec((8, 128), lambda i: (0, 0))],
        out_shape=[_sds((t, d), F32), _sds((8, 128), F32)],
        compiler_params=_cparams(("arbitrary",)),
    )(y, target)


def _ple_gate_bwd(name, dh, gate, e):
    t, d = dh.shape
    tm = _row_tile(t)

    def body(dh_ref, g_ref, e_ref, de_ref, dz_ref):
        dh_v, gt = dh_ref[...], g_ref[...]
        de_ref[...] = (dh_v * gt).astype(BF16)
        dz_ref[...] = (dh_v * e_ref[...] * (gt * (1.0 - gt))).astype(BF16)

    row = pl.BlockSpec((tm, d), lambda i: (i, 0))
    return pl.pallas_call(
        body, name=name, grid=(t // tm,), in_specs=[row, row, row], out_specs=[row, row],
        out_shape=[_sds((t, d), BF16), _sds((t, d), BF16)],
        compiler_params=_cparams(("parallel",)),
    )(dh, gate, e)


def _rope_half(v, cos, sin):
    half = v.shape[-1] // 2
    v1, v2 = v[:, :half], v[:, half:]
    return jnp.concatenate([v1 * cos - v2 * sin, v2 * cos + v1 * sin], axis=-1)


def _ret_consts():
    lg = jnp.log(1.0 - 2.0 ** (-5.0 - jnp.arange(RET_HEADS, dtype=F32)))
    idx = jnp.arange(CHUNK, dtype=F32)
    intra = jnp.exp(lg[:, None, None] * jnp.abs(idx[:, None] - idx[None, :]))
    qdec = jnp.exp(lg[:, None] * (idx + 1.0))
    kdec = jnp.exp(lg[:, None] * (CHUNK - 1.0 - idx))
    cdec = jnp.exp(lg * CHUNK)
    qdec = jnp.broadcast_to(qdec[:, :, None], (RET_HEADS, CHUNK, RET_DK))
    kdec = jnp.broadcast_to(kdec[:, :, None], (RET_HEADS, CHUNK, RET_DK))
    cdec = jnp.broadcast_to(cdec[:, None, None], (RET_HEADS, 1, RET_DV))
    return intra, qdec, kdec, cdec


def _ret_specs(rb, rev_nb=None):
    blk = (lambda i: i) if rev_nb is None else (lambda i: rev_nb - 1 - i)
    full = lambda shape: pl.BlockSpec(shape, lambda i: (0,) * len(shape))
    return dict(
        proj=pl.BlockSpec((rb, RET_IN), lambda i: (blk(i), 0)),
        tab=pl.BlockSpec((rb, RET_DK // 2), lambda i: (blk(i), 0)),
        vw=pl.BlockSpec((rb, RET_V_W), lambda i: (blk(i), 0)),
        st=pl.BlockSpec((rb // CHUNK, RET_HEADS, RET_DK, RET_DV), lambda i: (blk(i), 0, 0, 0)),
        gn=full((RET_HEADS, 1, RET_DV)),
        intra=full((RET_HEADS, CHUNK, CHUNK)),
        dec=full((RET_HEADS, CHUNK, RET_DK)),
        cdec=full((RET_HEADS, 1, RET_DV)),
    )


def _ret_fwd(proj, cos, sin, gn):
    t = proj.shape[0]
    rb = min(RET_ROWS, t)
    cpb = rb // CHUNK
    intra, qdec, kdec, cdec = _ret_consts()
    sp = _ret_specs(rb)

    def body(proj_ref, cos_ref, sin_ref, gn_ref, intra_ref, qd_ref, kd_ref, cd_ref,
             gated_ref, outp_ref, st_ref, s_ref):
        @pl.when(pl.program_id(0) == 0)
        def _():
            s_ref[...] = jnp.zeros_like(s_ref)

        def chunk(c, carry):
            rows = pl.ds(pl.multiple_of(c * CHUNK, CHUNK), CHUNK)
            cs, sn = cos_ref[rows, :], sin_ref[rows, :]
            for h in range(RET_HEADS):
                q = proj_ref[rows, h * RET_DK:(h + 1) * RET_DK]
                k = proj_ref[rows, RET_QK_W + h * RET_DK:RET_QK_W + (h + 1) * RET_DK]
                v = proj_ref[rows, 2 * RET_QK_W + h * RET_DV:2 * RET_QK_W + (h + 1) * RET_DV]
                g = proj_ref[rows, 2 * RET_QK_W + RET_V_W + h * RET_DV:2 * RET_QK_W + RET_V_W + (h + 1) * RET_DV]
                qr = _rope_half(q, cs, sn)
                kr = _rope_half(k, cs, sn) * (RET_DK ** -0.5)
                qb, kb, vb = qr.astype(BF16), kr.astype(BF16), v.astype(BF16)
                sc = _dot(qb, kb, 1, 1) * intra_ref[h]
                inner = _dot(sc.astype(BF16), vb, 1, 0)
                s_old = s_ref[h]
                sb = s_old.astype(BF16)
                st_ref[c, h] = sb
                cross = _dot((qr * qd_ref[h]).astype(BF16), sb, 1, 0)
                out = inner + cross
                s_ref[h] = s_old * cd_ref[h] + _dot((kr * kd_ref[h]).astype(BF16), vb, 0, 0)
                r = lax.rsqrt(jnp.mean(out * out, axis=-1, keepdims=True) + EPS)
                y = out * r * gn_ref[h]
                cols = slice(h * RET_DV, (h + 1) * RET_DV)
                gated_ref[rows, cols] = (g * _sigmoid(g) * y).astype(BF16)
                outp_ref[rows, cols] = out
            return carry

        lax.fori_loop(0, cpb, chunk, 0)

    return pl.pallas_call(
        body, name="ret_fwd", grid=(t // rb,),
        in_specs=[sp['proj'], sp['tab'], sp['tab'], sp['gn'], sp['intra'], sp['dec'], sp['dec'], sp['cdec']],
        out_specs=[sp['vw'], sp['vw'], sp['st']],
        out_shape=[_sds((t, RET_V_W), BF16), _sds((t, RET_V_W), F32),
                   _sds((t // CHUNK, RET_HEADS, RET_DK, RET_DV), BF16)],
        scratch_shapes=[pltpu.VMEM((RET_HEADS, RET_DK, RET_DV), F32)],
        compiler_params=_cparams(("arbitrary",)),
    )(proj, cos, sin, gn.reshape(RET_HEADS, 1, RET_DV), intra, qdec, kdec, cdec)


def _ret_bwd(proj, cos, sin, gn, outp, states, dgated):
    t = proj.shape[0]
    rb = min(RET_ROWS, t)
    cpb = rb // CHUNK
    nb = t // rb
    intra, qdec, kdec, cdec = _ret_consts()
    sp = _ret_specs(rb, rev_nb=nb)

    def body(proj_ref, cos_ref, sin_ref, gn_ref, intra_ref, qd_ref, kd_ref, cd_ref, outp_ref, st_ref, dgt_ref,
             dproj_ref, dgn_ref, ds_ref):
        @pl.when(pl.program_id(0) == 0)
        def _():
            ds_ref[...] = jnp.zeros_like(ds_ref)
            dgn_ref[...] = jnp.zeros_like(dgn_ref)

        def chunk(cc, carry):
            c = cpb - 1 - cc
            rows = pl.ds(pl.multiple_of(c * CHUNK, CHUNK), CHUNK)
            cs, sn = cos_ref[rows, :], sin_ref[rows, :]
            for h in range(RET_HEADS):
                q = proj_ref[rows, h * RET_DK:(h + 1) * RET_DK]
                k = proj_ref[rows, RET_QK_W + h * RET_DK:RET_QK_W + (h + 1) * RET_DK]
                v = proj_ref[rows, 2 * RET_QK_W + h * RET_DV:2 * RET_QK_W + (h + 1) * RET_DV]
                g = proj_ref[rows, 2 * RET_QK_W + RET_V_W + h * RET_DV:2 * RET_QK_W + RET_V_W + (h + 1) * RET_DV]
                cols = slice(h * RET_DV, (h + 1) * RET_DV)
                qr = _rope_half(q, cs, sn)
                kr = _rope_half(k, cs, sn) * (RET_DK ** -0.5)
                qb, kb, vb = qr.astype(BF16), kr.astype(BF16), v.astype(BF16)
                qdb = (qr * qd_ref[h]).astype(BF16)
                kdb = (kr * kd_ref[h]).astype(BF16)
                out = outp_ref[rows, cols]
                dgt = dgt_ref[rows, cols]
                gnh = gn_ref[h]
                r = lax.rsqrt(jnp.mean(out * out, axis=-1, keepdims=True) + EPS)
                xh = out * r
                sg = _sigmoid(g)
                dgate = dgt * (xh * gnh) * (sg * (1.0 + g * (1.0 - sg)))
                dy = dgt * (g * sg)
                dgn_ref[h] += jnp.sum(dy * xh, axis=0, keepdims=True)
                dxh = dy * gnh
                dout = r * (dxh - xh * jnp.mean(dxh * xh, axis=-1, keepdims=True))
                doutb = dout.astype(BF16)
                itr = intra_ref[h]
                pb = (_dot(qb, kb, 1, 1) * itr).astype(BF16)
                dv = _dot(pb, doutb, 0, 0)
                dsc = (_dot(doutb, vb, 1, 1) * itr).astype(BF16)
                dq = _dot(dsc, kb, 1, 0)
                dk = _dot(dsc, qb, 0, 0)
                dq = dq + _dot(doutb, st_ref[c, h], 1, 1) * qd_ref[h]
                ds_new = ds_ref[h]
                dsb = ds_new.astype(BF16)
                dk = dk + _dot(vb, dsb, 1, 1) * kd_ref[h]
                dv = dv + _dot(kdb, dsb, 1, 0)
                ds_ref[h] = ds_new * cd_ref[h] + _dot(qdb, doutb, 0, 0)
                dproj_ref[rows, h * RET_DK:(h + 1) * RET_DK] = _rope_half(dq, cs, -sn).astype(BF16)
                dproj_ref[rows, RET_QK_W + h * RET_DK:RET_QK_W + (h + 1) * RET_DK] = (
                    _rope_half(dk * (RET_DK ** -0.5), cs, -sn).astype(BF16))
                dproj_ref[rows, 2 * RET_QK_W + h * RET_DV:2 * RET_QK_W + (h + 1) * RET_DV] = dv.astype(BF16)
                dproj_ref[rows, 2 * RET_QK_W + RET_V_W + h * RET_DV:
                          2 * RET_QK_W + RET_V_W + (h + 1) * RET_DV] = dgate.astype(BF16)
            return carry

        lax.fori_loop(0, cpb, chunk, 0)

    return pl.pallas_call(
        body, name="ret_bwd", grid=(nb,),
        in_specs=[sp['proj'], sp['tab'], sp['tab'], sp['gn'], sp['intra'], sp['dec'], sp['dec'], sp['cdec'],
                  sp['vw'], sp['st'], sp['vw']],
        out_specs=[sp['proj'], sp['gn']],
        out_shape=[_sds((t, RET_IN), BF16), _sds((RET_HEADS, 1, RET_DV), F32)],
        scratch_shapes=[pltpu.VMEM((RET_HEADS, RET_DK, RET_DV), F32)],
        compiler_params=_cparams(("arbitrary",)),
    )(proj, cos, sin, gn.reshape(RET_HEADS, 1, RET_DV), intra, qdec, kdec, cdec, outp, states, dgated)


def _mla_tables(t):
    half = MLA_ROPE // 2
    inv = 1.0 / (ROPE_THETA ** (jnp.arange(0, MLA_ROPE, 2, dtype=F32) / MLA_ROPE))
    ang = jnp.arange(t, dtype=F32)[:, None] * inv[None, :]
    cos, sin = jnp.cos(ang), jnp.sin(ang)
    z = jnp.zeros((t, half), F32)
    c = jnp.concatenate([cos, cos, z, z], axis=1)
    s1 = jnp.concatenate([-sin, z, z, z], axis=1)
    s2 = jnp.concatenate([z, sin, z, z], axis=1)
    return c, s1, s2


def _rope_tile(r, c, s1, s2):
    return r * c + pltpu.roll(r, 96, 1) * s1 + pltpu.roll(r, 32, 1) * s2


def _mla_mid(proj2, qa, kva):
    t = proj2.shape[0]
    tm = _row_tile(t)

    def body(p_ref, qa_ref, kva_ref, cq_ref, ckv_ref):
        cq = p_ref[:, :MLA_Q_RANK]
        ckv = p_ref[:, MLA_Q_RANK:MLA_Q_RANK + MLA_KV_RANK]
        rq = lax.rsqrt(jnp.mean(cq * cq, axis=-1, keepdims=True) + EPS)
        rkv = lax.rsqrt(jnp.mean(ckv * ckv, axis=-1, keepdims=True) + EPS)
        cq_ref[...] = (cq * rq * qa_ref[...]).astype(BF16)
        ckv_ref[...] = (ckv * rkv * kva_ref[...]).astype(BF16)

    return pl.pallas_call(
        body, name="mla_mid", grid=(t // tm,),
        in_specs=[pl.BlockSpec((tm, MLA_IN_PAD), lambda i: (i, 0)),
                  pl.BlockSpec((1, MLA_Q_RANK), lambda i: (0, 0)),
                  pl.BlockSpec((1, MLA_KV_RANK), lambda i: (0, 0))],
        out_specs=[pl.BlockSpec((tm, MLA_Q_RANK), lambda i: (i, 0)),
                   pl.BlockSpec((tm, MLA_KV_RANK), lambda i: (i, 0))],
        out_shape=[_sds((t, MLA_Q_RANK), BF16), _sds((t, MLA_KV_RANK), BF16)],
        compiler_params=_cparams(("parallel",)),
    )(proj2, qa, kva)


def _mla_mid_bwd(proj2, qa, kva, dcq, dckv, dkr):
    t = proj2.shape[0]
    tm = _row_tile(t)

    def body(p_ref, qa_ref, kva_ref, dcq_ref, dckv_ref, dkr_ref, dp_ref, dqa_ref, dkva_ref):
        @pl.when(pl.program_id(0) == 0)
        def _():
            dqa_ref[...] = jnp.zeros_like(dqa_ref)
            dkva_ref[...] = jnp.zeros_like(dkva_ref)

        dxq, dgq = _rms_bwd_rows(dcq_ref[...], p_ref[:, :MLA_Q_RANK], qa_ref[...], MLA_Q_RANK)
        dxk, dgk = _rms_bwd_rows(dckv_ref[...], p_ref[:, MLA_Q_RANK:MLA_Q_RANK + MLA_KV_RANK], kva_ref[...],
                                 MLA_KV_RANK)
        dp_ref[:, :MLA_Q_RANK] = dxq.astype(BF16)
        dp_ref[:, MLA_Q_RANK:MLA_Q_RANK + MLA_KV_RANK] = dxk.astype(BF16)
        dp_ref[:, MLA_Q_RANK + MLA_KV_RANK:] = dkr_ref[...].astype(BF16)
        dqa_ref[...] += jnp.sum(dgq, axis=0, keepdims=True)
        dkva_ref[...] += jnp.sum(dgk, axis=0, keepdims=True)

    return pl.pallas_call(
        body, name="mla_mid_bwd", grid=(t // tm,),
        in_specs=[pl.BlockSpec((tm, MLA_IN_PAD), lambda i: (i, 0)),
                  pl.BlockSpec((1, MLA_Q_RANK), lambda i: (0, 0)),
                  pl.BlockSpec((1, MLA_KV_RANK), lambda i: (0, 0)),
                  pl.BlockSpec((tm, MLA_Q_RANK), lambda i: (i, 0)),
                  pl.BlockSpec((tm, MLA_KV_RANK), lambda i: (i, 0)),
                  pl.BlockSpec((tm, 128), lambda i: (i, 0))],
        out_specs=[pl.BlockSpec((tm, MLA_IN_PAD), lambda i: (i, 0)),
                   pl.BlockSpec((1, MLA_Q_RANK), lambda i: (0, 0)),
                   pl.BlockSpec((1, MLA_KV_RANK), lambda i: (0, 0))],
        out_shape=[_sds((t, MLA_IN_PAD), BF16), _sds((1, MLA_Q_RANK), F32), _sds((1, MLA_KV_RANK), F32)],
        compiler_params=_cparams(("arbitrary",)),
    )(proj2, qa, kva, dcq, dckv, dkr)


def _mla_prep_specs(t, tm):
    head = lambda w: pl.BlockSpec((None, tm, w), lambda i, h: (h, i, 0))
    return dict(
        head256=head(MLA_HD_PAD), head128=head(MLA_VD),
        kr=pl.BlockSpec((tm, 128), lambda i, h: (i, (MLA_Q_RANK + MLA_KV_RANK) // 128)),
        gain=pl.BlockSpec((1, MLA_HD_PAD), lambda i, h: (0, 0)),
        tab=pl.BlockSpec((tm, 128), lambda i, h: (i, 0)),
    )


def _mla_prep(q, kv, proj2, gq, gk, tabs):
    t = q.shape[1]
    tm = _row_tile(t)
    sp = _mla_prep_specs(t, tm)

    def body(q_ref, kv_ref, kr_ref, gq_ref, gk_ref, c_ref, s1_ref, s2_ref, qh_ref, kh_ref, vh_ref):
        c, s1, s2 = c_ref[...], s1_ref[...], s2_ref[...]

        def norm_rope(xv, gain):
            r = lax.rsqrt(jnp.sum(xv * xv, axis=-1, keepdims=True) / MLA_QKD + EPS)
            y = xv * r * gain
            return jnp.concatenate([y[:, :MLA_NOPE], _rope_tile(y[:, MLA_NOPE:], c, s1, s2)], axis=-1)

        kvv = kv_ref[...]
        qh_ref[...] = norm_rope(q_ref[...], gq_ref[...]).astype(BF16)
        kf = jnp.concatenate([kvv[:, :MLA_NOPE], kr_ref[...]], axis=-1)
        kh_ref[...] = norm_rope(kf, gk_ref[...]).astype(BF16)
        vh_ref[...] = kvv[:, MLA_NOPE:].astype(BF16)

    return pl.pallas_call(
        body, name="mla_prep", grid=(t // tm, MLA_HEADS),
        in_specs=[sp['head256'], sp['head256'], sp['kr'], sp['gain'], sp['gain'], sp['tab'], sp['tab'], sp['tab']],
        out_specs=[sp['head256'], sp['head256'], sp['head128']],
        out_shape=[_sds((MLA_HEADS, t, MLA_HD_PAD), BF16), _sds((MLA_HEADS, t, MLA_HD_PAD), BF16),
                   _sds((MLA_HEADS, t, MLA_VD), BF16)],
        compiler_params=_cparams(("parallel", "arbitrary")),
    )(q, kv, proj2, gq, gk, *tabs)


def _mla_prep_bwd(q, kv, proj2, gq, gk, tabs, dqh, dkh, dvh):
    t = q.shape[1]
    tm = _row_tile(t)
    sp = _mla_prep_specs(t, tm)

    def body(q_ref, kv_ref, kr_ref, gq_ref, gk_ref, c_ref, s1_ref, s2_ref, dqh_ref, dkh_ref, dvh_ref,
             dq_ref, dkv_ref, dkr_ref, dgq_ref, dgk_ref):
        i, h = pl.program_id(0), pl.program_id(1)

        @pl.when((i == 0) & (h == 0))
        def _():
            dgq_ref[...] = jnp.zeros_like(dgq_ref)
            dgk_ref[...] = jnp.zeros_like(dgk_ref)

        @pl.when(h == 0)
        def _():
            dkr_ref[...] = jnp.zeros_like(dkr_ref)

        c, s1, s2 = c_ref[...], s1_ref[...], s2_ref[...]

        def back(xv, gain, dout):
            dy = jnp.concatenate([dout[:, :MLA_NOPE], _rope_tile(dout[:, MLA_NOPE:], c, -s1, -s2)], axis=-1)
            return _rms_bwd_rows(dy, xv, gain, MLA_QKD)

        kvv = kv_ref[...]
        dxq, dgq = back(q_ref[...], gq_ref[...], dqh_ref[...])
        kf = jnp.concatenate([kvv[:, :MLA_NOPE], kr_ref[...]], axis=-1)
        dxk, dgk = back(kf, gk_ref[...], dkh_ref[...])
        dq_ref[...] = dxq.astype(BF16)
        dkv_ref[...] = jnp.concatenate([dxk[:, :MLA_NOPE], dvh_ref[...]], axis=-1).astype(BF16)
        dkr_ref[...] += dxk[:, MLA_NOPE:]
        dgq_ref[...] += jnp.sum(dgq, axis=0, keepdims=True)
        dgk_ref[...] += jnp.sum(dgk, axis=0, keepdims=True)

    return pl.pallas_call(
        body, name="mla_prep_bwd", grid=(t // tm, MLA_HEADS),
        in_specs=[sp['head256'], sp['head256'], sp['kr'], sp['gain'], sp['gain'], sp['tab'], sp['tab'], sp['tab'],
                  sp['head256'], sp['head256'], sp['head128']],
        out_specs=[sp['head256'], sp['head256'], sp['tab'], sp['gain'], sp['gain']],
        out_shape=[_sds((MLA_HEADS, t, MLA_HD_PAD), BF16), _sds((MLA_HEADS, t, MLA_HD_PAD), BF16),
                   _sds((t, 128), F32), _sds((1, MLA_HD_PAD), F32), _sds((1, MLA_HD_PAD), F32)],
        compiler_params=_cparams(("arbitrary", "arbitrary")),
    )(q, kv, proj2, gq, gk, *tabs, dqh, dkh, dvh)


def _chunk_mask(n):
    rq = lax.shift_right_logical(lax.broadcasted_iota(jnp.int32, (n, n), 0), 6)
    ck = lax.shift_right_logical(lax.broadcasted_iota(jnp.int32, (n, n), 1), 6)
    return ck <= rq


def _attn_fwd(qh, kh, vh):
    t = qh.shape[1]
    ab = min(ATT_BLOCK, t)

    def body(q_ref, k_ref, v_ref, o_ref, lse_ref):
        i = pl.program_id(1)
        q = q_ref[...]

        def step(kb, carry, masked):
            m, l, acc = carry
            rows = pl.ds(pl.multiple_of(kb * ab, ab), ab)
            s = _dot(q, k_ref[rows, :], 1, 1) * ATT_SCALE
            if masked:
                s = jnp.where(_chunk_mask(ab), s, -1e30)
            m_new = jnp.maximum(m, jnp.max(s, axis=-1, keepdims=True))
            p = jnp.exp(s - m_new)
            alpha = jnp.exp(m - m_new)
            l = alpha * l + jnp.sum(p, axis=-1, keepdims=True)
            acc = alpha * acc + _dot(p.astype(BF16), v_ref[rows, :], 1, 0)
            return m_new, l, acc

        init = (jnp.full((ab, 1), -1e30, F32), jnp.zeros((ab, 1), F32), jnp.zeros((ab, MLA_VD), F32))
        carry = lax.fori_loop(0, i, lambda kb, cr: step(kb, cr, False), init)
        m, l, acc = step(i, carry, True)
        o_ref[...] = acc / l
        lse_ref[...] = jnp.broadcast_to(m + jnp.log(l), (ab, 128))

    return pl.pallas_call(
        body, name="mla_attn", grid=(MLA_HEADS, t // ab),
        in_specs=[pl.BlockSpec((None, ab, MLA_HD_PAD), lambda h, i: (h, i, 0)),
                  pl.BlockSpec((None, t, MLA_HD_PAD), lambda h, i: (h, 0, 0)),
                  pl.BlockSpec((None, t, MLA_VD), lambda h, i: (h, 0, 0))],
        out_specs=[pl.BlockSpec((ab, MLA_VD), lambda h, i: (i, h)),
                   pl.BlockSpec((None, ab, 128), lambda h, i: (h, i, 0))],
        out_shape=[_sds((t, MLA_HEADS * MLA_VD), F32), _sds((MLA_HEADS, t, 128), F32)],
        compiler_params=_cparams(("parallel", "arbitrary")),
    )(qh, kh, vh)


def _attn_delta(do, o):
    t = do.shape[0]
    tm = _row_tile(t)

    def body(do_ref, o_ref, d_ref):
        d_ref[...] = jnp.broadcast_to(jnp.sum(do_ref[...] * o_ref[...], axis=-1, keepdims=True), (tm, 128))

    col = pl.BlockSpec((tm, MLA_VD), lambda i, h: (i, h))
    return pl.pallas_call(
        body, name="mla_delta", grid=(t // tm, MLA_HEADS), in_specs=[col, col],
        out_specs=pl.BlockSpec((None, tm, 128), lambda i, h: (h, i, 0)),
        out_shape=_sds((MLA_HEADS, t, 128), F32),
        compiler_params=_cparams(("parallel", "parallel")),
    )(do, o)


def _attn_bwd(qh, kh, vh, do, lse, delta):
    t = qh.shape[1]
    ab = min(ATT_BLOCK, t)
    nq = t // ab

    def body(q_ref, k_ref, v_ref, do_ref, lse_ref, dl_ref, dq_ref, dk_ref, dv_ref):
        j = pl.program_id(1)

        @pl.when(j == 0)
        def _():
            dq_ref[...] = jnp.zeros_like(dq_ref)

        k, v = k_ref[...], v_ref[...]

        def step(qb, carry, masked):
            dk, dv = carry
            rows = pl.ds(pl.multiple_of(qb * ab, ab), ab)
            q = q_ref[rows, :]
            dob = do_ref[rows, :].astype(BF16)
            s = _dot(q, k, 1, 1) * ATT_SCALE
            if masked:
                s = jnp.where(_chunk_mask(ab), s, -1e30)
            p = jnp.exp(s - jnp.tile(lse_ref[rows, :], (1, ab // 128)))
            dp = _dot(dob, v, 1, 1)
            ds = (p * (dp - jnp.tile(dl_ref[rows, :], (1, ab // 128))) * ATT_SCALE).astype(BF16)
            dv = dv + _dot(p.astype(BF16), dob, 0, 0)
            dk = dk + _dot(ds, q, 0, 0)
            dq_ref[rows, :] += _dot(ds, k, 1, 0)
            return dk, dv

        carry = step(j, (jnp.zeros((ab, MLA_HD_PAD), F32), jnp.zeros((ab, MLA_VD), F32)), True)
        dk, dv = lax.fori_loop(j + 1, nq, lambda qb, cr: step(qb, cr, False), carry)
        dk_ref[...] = dk
        dv_ref[...] = dv

    whole = lambda w: pl.BlockSpec((None, t, w), lambda h, j: (h, 0, 0))
    blk = lambda w: pl.BlockSpec((None, ab, w), lambda h, j: (h, j, 0))
    return pl.pallas_call(
        body, name="mla_attn_bwd", grid=(MLA_HEADS, nq),
        in_specs=[whole(MLA_HD_PAD), blk(MLA_HD_PAD), blk(MLA_VD),
                  pl.BlockSpec((t, MLA_VD), lambda h, j: (0, h)), whole(128), whole(128)],
        out_specs=[whole(MLA_HD_PAD), blk(MLA_HD_PAD), blk(MLA_VD)],
        out_shape=[_sds((MLA_HEADS, t, MLA_HD_PAD), F32), _sds((MLA_HEADS, t, MLA_HD_PAD), F32),
                   _sds((MLA_HEADS, t, MLA_VD), F32)],
        compiler_params=_cparams(("parallel", "arbitrary")),
    )(qh, kh, vh, do, lse, delta)


def _mlp_fwd(l, h, norm_g, w1g, w2g):
    t = h.shape[0]
    tm = _row_tile(t)
    nsh, _, _, wsh = w1g.shape
    hn = _rms_fwd(f"mlp_norm{l}", h, norm_g)

    def relu2(acc):
        r = jnp.maximum(acc, 0.0)
        return r, r * r

    tile = pl.BlockSpec((tm, wsh), lambda i, j, k: (i, j))
    r, u = _mm(f"mlp_up{l}", (t // tm, nsh, 1),
               hn, pl.BlockSpec((tm, D_MODEL), lambda i, j, k: (i, 0)),
               w1g, pl.BlockSpec((None, None, D_MODEL, wsh), lambda i, j, k: (j, l, 0, 0)), (1, 0),
               [(_sds((t, D_FF), BF16), tile), (_sds((t, D_FF), BF16), tile)], epi=relu2)
    row = pl.BlockSpec((tm, D_MODEL), lambda i, j, k: (i, 0))
    (h2,) = _mm(f"mlp_down{l}", (t // tm, 1, nsh),
                u, pl.BlockSpec((tm, wsh), lambda i, j, k: (i, k)),
                w2g, pl.BlockSpec((None, None, wsh, D_MODEL), lambda i, j, k: (k, l, 0, 0)), (1, 0),
                [(_sds((t, D_MODEL), F32), row)], extras=[(h, row)], epi=lambda acc, hv: (acc + hv,))
    return h2, (h, hn, r, u)


def _mlp_bwd(l, dh, saved, norm_g, w1g, w2g):
    h, hn, r, u = saved
    t = h.shape[0]
    tm = _row_tile(t)
    tk = _row_tile(t)
    nsh, _, _, wsh = w1g.shape
    tile = pl.BlockSpec((tm, wsh), lambda i, j, k: (i, j))
    (da,) = _mm(f"mlp_du{l}", (t // tm, nsh, 1),
                dh, pl.BlockSpec((tm, D_MODEL), lambda i, j, k: (i, 0)),
                w2g, pl.BlockSpec((None, None, wsh, D_MODEL), lambda i, j, k: (j, l, 0, 0)), (1, 1),
                [(_sds((t, D_FF), BF16), tile)], extras=[(r, tile)],
                epi=lambda acc, rv: (2.0 * rv.astype(F32) * acc,))
    (dw2,) = _mm(f"mlp_dw2{l}", (nsh, 1, t // tk),
                 u, pl.BlockSpec((tk, wsh), lambda i, j, k: (k, i)),
                 dh, pl.BlockSpec((tk, D_MODEL), lambda i, j, k: (k, 0)), (0, 0),
                 [(_sds((nsh, wsh, D_MODEL), BF16), pl.BlockSpec((None, wsh, D_MODEL), lambda i, j, k: (i, 0, 0)))])
    (dw1,) = _mm(f"mlp_dw1{l}", (1, nsh, t // tk),
                 hn, pl.BlockSpec((tk, D_MODEL), lambda i, j, k: (k, 0)),
                 da, pl.BlockSpec((tk, wsh), lambda i, j, k: (k, j)), (0, 0),
                 [(_sds((nsh, D_MODEL, wsh), BF16), pl.BlockSpec((None, D_MODEL, wsh), lambda i, j, k: (j, 0, 0)))])
    (dhn,) = _mm(f"mlp_dhn{l}", (t // tm, 1, nsh),
                 da, pl.BlockSpec((tm, wsh), lambda i, j, k: (i, k)),
                 w1g, pl.BlockSpec((None, None, D_MODEL, wsh), lambda i, j, k: (k, l, 0, 0)), (1, 1),
                 [(_sds((t, D_MODEL), F32), pl.BlockSpec((tm, D_MODEL), lambda i, j, k: (i, 0)))])
    dh_in, dg = _rms_bwd(f"mlp_norm_bwd{l}", dhn, h, norm_g, dh)
    return dh_in, dg, dw1, dw2


def _ple_fwd(l, h, p, norm_g, wg, wp):
    t = h.shape[0]
    tm = _row_tile(t, 512)
    hn = _rms_fwd(f"ple_norm{l}", h, norm_g)
    row = pl.BlockSpec((tm, D_MODEL), lambda i, j, k: (i, 0))
    full = lambda r: pl.BlockSpec((r, D_MODEL), lambda i, j, k: (0, 0))
    (e,) = _mm(f"ple_proj{l}", (t // tm, 1, 1),
               p, pl.BlockSpec((None, None, tm, PLE_DIM), lambda i, j, k: (l, 0, i, 0)),
               wp, full(PLE_DIM), (1, 0), [(_sds((t, D_MODEL), F32), row)])

    def gate_epi(acc, hv, ev):
        gt = _sigmoid(acc)
        return hv + gt * ev, gt

    h_out, gate = _mm(f"ple_gate{l}", (t // tm, 1, 1), hn, row, wg, full(D_MODEL), (1, 0),
                      [(_sds((t, D_MODEL), F32), row), (_sds((t, D_MODEL), F32), row)],
                      extras=[(h, row), (e, row)], epi=gate_epi)
    return h_out, (h, hn, gate, e)


def _ple_bwd(l, dh, saved, p, norm_g, wg):
    h, hn, gate, e = saved
    t = h.shape[0]
    tm = _row_tile(t)
    tk = _row_tile(t, 512)
    de, dz = _ple_gate_bwd(f"ple_gate_bwd{l}", dh, gate, e)
    full = lambda r: pl.BlockSpec((r, D_MODEL), lambda i, j, k: (0, 0))
    rowk = pl.BlockSpec((tk, D_MODEL), lambda i, j, k: (k, 0))
    (dwp,) = _mm(f"ple_dwp{l}", (1, 1, t // tk),
                 p, pl.BlockSpec((None, None, tk, PLE_DIM), lambda i, j, k: (l, 0, k, 0)),
                 de, rowk, (0, 0), [(_sds((PLE_DIM, D_MODEL), BF16), full(PLE_DIM))])
    (dwg,) = _mm(f"ple_dwg{l}", (1, 1, t // tk), hn, rowk, dz, rowk, (0, 0),
                 [(_sds((D_MODEL, D_MODEL), BF16), full(D_MODEL))])
    row = pl.BlockSpec((tm, D_MODEL), lambda i, j, k: (i, 0))
    (dhn,) = _mm(f"ple_dhn{l}", (t // tm, 1, 1), dz, row, wg, full(D_MODEL), (1, 1),
                 [(_sds((t, D_MODEL), F32), row)])
    dh_in, dg = _rms_bwd(f"ple_norm_bwd{l}", dhn, h, norm_g, dh)
    return dh_in, dg, dwg, dwp


def _ret_layer_fwd(x, norm_g, wri, wro, gn, cos, sin):
    t = x.shape[0]
    tm = _row_tile(t)
    nsh, _, wsh = wri.shape
    hn = _rms_fwd("mix_norm0", x, norm_g)
    (proj,) = _mm("ret_in", (t // tm, nsh, 1),
                  hn, pl.BlockSpec((tm, D_MODEL), lambda i, j, k: (i, 0)),
                  wri, pl.BlockSpec((None, D_MODEL, wsh), lambda i, j, k: (j, 0, 0)), (1, 0),
                  [(_sds((t, RET_IN), F32), pl.BlockSpec((tm, wsh), lambda i, j, k: (i, j)))])
    gated, outp, states = _ret_fwd(proj, cos, sin, gn)
    row = pl.BlockSpec((tm, D_MODEL), lambda i, j, k: (i, 0))
    kt = 512
    (h1,) = _mm("ret_out", (t // tm, 1, RET_V_W // kt),
                gated, pl.BlockSpec((tm, kt), lambda i, j, k: (i, k)),
                wro, pl.BlockSpec((kt, D_MODEL), lambda i, j, k: (k, 0)), (1, 0),
                [(_sds((t, D_MODEL), F32), row)], extras=[(x, row)], epi=lambda acc, xv: (acc + xv,))
    return h1, (x, hn, proj, gated, outp, states)


def _ret_layer_bwd(dh, saved, norm_g, wri, wro, gn, cos, sin):
    x, hn, proj, gated, outp, states = saved
    t = x.shape[0]
    tm = _row_tile(t)
    tk = _row_tile(t, 512)
    nsh, _, wsh = wri.shape
    (dgated,) = _mm("ret_dgated", (t // tm, RET_V_W // D_MODEL, 1),
                    dh, pl.BlockSpec((tm, D_MODEL), lambda i, j, k: (i, 0)),
                    wro, pl.BlockSpec((D_MODEL, D_MODEL), lambda i, j, k: (j, 0)), (1, 1),
                    [(_sds((t, RET_V_W), F32), pl.BlockSpec((tm, D_MODEL), lambda i, j, k: (i, j)))])
    kt = 512
    (dwro,) = _mm("ret_dwro", (RET_V_W // kt, 1, t // tk),
                  gated, pl.BlockSpec((tk, kt), lambda i, j, k: (k, i)),
                  dh, pl.BlockSpec((tk, D_MODEL), lambda i, j, k: (k, 0)), (0, 0),
                  [(_sds((RET_V_W, D_MODEL), BF16), pl.BlockSpec((kt, D_MODEL), lambda i, j, k: (i, 0)))])
    dproj, dgn = _ret_bwd(proj, cos, sin, gn, outp, states, dgated)
    (dwri,) = _mm("ret_dwri", (1, nsh, t // tk),
                  hn, pl.BlockSpec((tk, D_MODEL), lambda i, j, k: (k, 0)),
                  dproj, pl.BlockSpec((tk, wsh), lambda i, j, k: (k, j)), (0, 0),
                  [(_sds((nsh, D_MODEL, wsh), BF16), pl.BlockSpec((None, D_MODEL, wsh), lambda i, j, k: (j, 0, 0)))])
    (dhn,) = _mm("ret_dhn", (t // tm, 1, nsh),
                 dproj, pl.BlockSpec((tm, wsh), lambda i, j, k: (i, k)),
                 wri, pl.BlockSpec((None, D_MODEL, wsh), lambda i, j, k: (k, 0, 0)), (1, 1),
                 [(_sds((t, D_MODEL), F32), pl.BlockSpec((tm, D_MODEL), lambda i, j, k: (i, 0)))])
    dx, dg = _rms_bwd("mix_norm_bwd0", dhn, x, norm_g, dh)
    return dx, dg, dwri, dwro, dgn.reshape(RET_HEADS, RET_DV)


def _mla_layer_fwd(h, norm_g, wmi, qa, kva, wuq, wukv, gq, gk, wmo, tabs):
    t = h.shape[0]
    tm = _row_tile(t)
    hn = _rms_fwd("mix_norm1", h, norm_g)
    row = pl.BlockSpec((tm, D_MODEL), lambda i, j, k: (i, 0))
    (proj2,) = _mm("mla_in", (t // tm, 1, 1), hn, row,
                   wmi, pl.BlockSpec((D_MODEL, MLA_IN_PAD), lambda i, j, k: (0, 0)), (1, 0),
                   [(_sds((t, MLA_IN_PAD), F32), pl.BlockSpec((tm, MLA_IN_PAD), lambda i, j, k: (i, 0)))])
    cq, ckv = _mla_mid(proj2, qa, kva)
    head = pl.BlockSpec((None, tm, MLA_HD_PAD), lambda i, j, k: (j, i, 0))
    (q,) = _mm("mla_uq", (t // tm, MLA_HEADS, 1),
               cq, pl.BlockSpec((tm, MLA_Q_RANK), lambda i, j, k: (i, 0)),
               wuq, pl.BlockSpec((None, MLA_Q_RANK, MLA_HD_PAD), lambda i, j, k: (j, 0, 0)), (1, 0),
               [(_sds((MLA_HEADS, t, MLA_HD_PAD), F32), head)])
    (kv,) = _mm("mla_ukv", (t // tm, MLA_HEADS, 1),
                ckv, pl.BlockSpec((tm, MLA_KV_RANK), lambda i, j, k: (i, 0)),
                wukv, pl.BlockSpec((None, MLA_KV_RANK, MLA_HD_PAD), lambda i, j, k: (j, 0, 0)), (1, 0),
                [(_sds((MLA_HEADS, t, MLA_HD_PAD), F32), head)])
    qh, kh, vh = _mla_prep(q, kv, proj2, gq, gk, tabs)
    o, lse = _attn_fwd(qh, kh, vh)
    (h_out,) = _mm("mla_out", (t // tm, 1, 1), o, row,
                   wmo, pl.BlockSpec((D_MODEL, D_MODEL), lambda i, j, k: (0, 0)), (1, 0),
                   [(_sds((t, D_MODEL), F32), row)], extras=[(h, row)], epi=lambda acc, hv: (acc + hv,))
    return h_out, (h, hn, proj2, cq, ckv, q, kv, qh, kh, vh, o, lse)


def _mla_layer_bwd(dh, saved, norm_g, wmi, qa, kva, wuq, wukv, gq, gk, wmo, tabs):
    h, hn, proj2, cq, ckv, q, kv, qh, kh, vh, o, lse = saved
    t = h.shape[0]
    tm = _row_tile(t)
    tk = _row_tile(t, 512)
    row = pl.BlockSpec((tm, D_MODEL), lambda i, j, k: (i, 0))
    rowk = pl.BlockSpec((tk, D_MODEL), lambda i, j, k: (k, 0))
    sq = pl.BlockSpec((D_MODEL, D_MODEL), lambda i, j, k: (0, 0))
    (do,) = _mm("mla_do", (t // tm, 1, 1), dh, row, wmo, sq, (1, 1), [(_sds((t, D_MODEL), F32), row)])
    (dwmo,) = _mm("mla_dwo", (1, 1, t // tk), o, rowk, dh, rowk, (0, 0), [(_sds((D_MODEL, D_MODEL), BF16), sq)])
    delta = _attn_delta(do, o)
    dqh, dkh, dvh = _attn_bwd(qh, kh, vh, do, lse, delta)
    dq, dkv, dkr, dgq, dgk = _mla_prep_bwd(q, kv, proj2, gq, gk, tabs, dqh, dkh, dvh)

    headk = pl.BlockSpec((None, tk, MLA_HD_PAD), lambda i, j, k: (j, k, 0))
    (dwuq,) = _mm("mla_dwuq", (1, MLA_HEADS, t // tk),
                  cq, pl.BlockSpec((tk, MLA_Q_RANK), lambda i, j, k: (k, 0)), dq, headk, (0, 0),
                  [(_sds((MLA_HEADS, MLA_Q_RANK, MLA_HD_PAD), BF16),
                    pl.BlockSpec((None, MLA_Q_RANK, MLA_HD_PAD), lambda i, j, k: (j, 0, 0)))])
    (dwukv,) = _mm("mla_dwukv", (1, MLA_HEADS, t // tk),
                   ckv, pl.BlockSpec((tk, MLA_KV_RANK), lambda i, j, k: (k, 0)), dkv, headk, (0, 0),
                   [(_sds((MLA_HEADS, MLA_KV_RANK, MLA_HD_PAD), BF16),
                     pl.BlockSpec((None, MLA_KV_RANK, MLA_HD_PAD), lambda i, j, k: (j, 0, 0)))])
    headi = pl.BlockSpec((None, tm, MLA_HD_PAD), lambda i, j, k: (k, i, 0))
    (dcq,) = _mm("mla_dcq", (t // tm, 1, MLA_HEADS), dq, headi,
                 wuq, pl.BlockSpec((None, MLA_Q_RANK, MLA_HD_PAD), lambda i, j, k: (k, 0, 0)), (1, 1),
                 [(_sds((t, MLA_Q_RANK), F32), pl.BlockSpec((tm, MLA_Q_RANK), lambda i, j, k: (i, 0)))])
    (dckv,) = _mm("mla_dckv", (t // tm, 1, MLA_HEADS), dkv, headi,
                  wukv, pl.BlockSpec((None, MLA_KV_RANK, MLA_HD_PAD), lambda i, j, k: (k, 0, 0)), (1, 1),
                  [(_sds((t, MLA_KV_RANK), F32), pl.BlockSpec((tm, MLA_KV_RANK), lambda i, j, k: (i, 0)))])
    dproj2, dqa, dkva = _mla_mid_bwd(proj2, qa, kva, dcq, dckv, dkr)
    win = pl.BlockSpec((D_MODEL, MLA_IN_PAD), lambda i, j, k: (0, 0))
    (dwmi,) = _mm("mla_dwin", (1, 1, t // tk), hn, rowk,
                  dproj2, pl.BlockSpec((tk, MLA_IN_PAD), lambda i, j, k: (k, 0)), (0, 0),
                  [(_sds((D_MODEL, MLA_IN_PAD), BF16), win)])
    (dhn,) = _mm("mla_dhn", (t // tm, 1, 1),
                 dproj2, pl.BlockSpec((tm, MLA_IN_PAD), lambda i, j, k: (i, 0)), wmi, win, (1, 1),
                 [(_sds((t, D_MODEL), F32), row)])
    dh_in, dg = _rms_bwd("mix_norm_bwd1", dhn, h, norm_g, dh)
    return dh_in, dict(mix=dg, wmi=dwmi, qa=dqa, kva=dkva, wuq=dwuq, wukv=dwukv, gq=dgq, gk=dgk, wmo=dwmo)


def _local_step(x, p, target, w):
    t = x.shape[0]
    inv = 1.0 / (ROPE_THETA ** (jnp.arange(0, RET_DK, 2, dtype=F32) / RET_DK))
    ang = jnp.arange(t, dtype=F32)[:, None] * inv[None, :]
    cos_r, sin_r = jnp.cos(ang), jnp.sin(ang)
    tabs = _mla_tables(t)
    row = lambda a, i: a[i:i + 1]

    h1, s_ret = _ret_layer_fwd(x, row(w['mix_norm'], 0), w['ret_w_in'], w['ret_w_out'], w['ret_gn'], cos_r, sin_r)
    h2, s_mlp0 = _mlp_fwd(0, h1, row(w['mlp_norm'], 0), w['mlp_w1'], w['mlp_w2'])
    h3, s_ple0 = _ple_fwd(0, h2, p, row(w['ple_norm'], 0), w['ple_gate_w'][0], w['ple_proj_w'][0])
    mla_w = (w['mla_w_in'], w['mla_q_a_norm'], w['mla_kv_a_norm'], w['mla_w_uq'], w['mla_w_ukv'],
             w['mla_q_norm'], w['mla_k_norm'], w['mla_w_out'], tabs)
    h4, s_mla = _mla_layer_fwd(h3, row(w['mix_norm'], 1), *mla_w)
    h5, s_mlp1 = _mlp_fwd(1, h4, row(w['mlp_norm'], 1), w['mlp_w1'], w['mlp_w2'])
    y, s_ple1 = _ple_fwd(1, h5, p, row(w['ple_norm'], 1), w['ple_gate_w'][1], w['ple_proj_w'][1])

    dy, sq_err = _loss_head(y, target)

    dh5, dg_ple1, dwg1, dwp1 = _ple_bwd(1, dy, s_ple1, p, row(w['ple_norm'], 1), w['ple_gate_w'][1])
    dh4, dg_mlp1, dw1_1, dw2_1 = _mlp_bwd(1, dh5, s_mlp1, row(w['mlp_norm'], 1), w['mlp_w1'], w['mlp_w2'])
    dh3, gm = _mla_layer_bwd(dh4, s_mla, row(w['mix_norm'], 1), *mla_w)
    dh2, dg_ple0, dwg0, dwp0 = _ple_bwd(0, dh3, s_ple0, p, row(w['ple_norm'], 0), w['ple_gate_w'][0])
    dh1, dg_mlp0, dw1_0, dw2_0 = _mlp_bwd(0, dh2, s_mlp0, row(w['mlp_norm'], 0), w['mlp_w1'], w['mlp_w2'])
    dx, dg_mix0, dwri, dwro, dgn = _ret_layer_bwd(dh1, s_ret, row(w['mix_norm'], 0), w['ret_w_in'],
                                                  w['ret_w_out'], w['ret_gn'], cos_r, sin_r)

    n = N_DEV
    colsh = lambda a: a.reshape(a.shape[0], n, a.shape[1] // n).transpose(1, 0, 2)
    rowsh = lambda a: a.reshape(n, a.shape[0] // n, a.shape[1])
    big = dict(
        ret_w_in=dwri,
        ret_w_out=rowsh(dwro),
        mla_w_in=rowsh(gm['wmi'][:, :MLA_IN]),
        mla_w_uq=gm['wuq'][:, :, :MLA_QKD],
        mla_w_ukv=gm['wukv'],
        mla_w_out=rowsh(gm['wmo']),
        mlp_w1=jnp.stack([dw1_0, dw1_1], axis=1),
        mlp_w2=jnp.stack([dw2_0, dw2_1], axis=1),
        ple_gate_w=jnp.stack([rowsh(dwg0), rowsh(dwg1)], axis=1),
        ple_proj_w=jnp.stack([colsh(dwp0), colsh(dwp1)], axis=1),
    )
    small = dict(
        mix_norm=jnp.concatenate([dg_mix0, gm['mix']], axis=0),
        mlp_norm=jnp.concatenate([dg_mlp0, dg_mlp1], axis=0),
        ple_norm=jnp.concatenate([dg_ple0, dg_ple1], axis=0),
        ret_gn=dgn,
        mla_q_a_norm=gm['qa'],
        mla_kv_a_norm=gm['kva'],
        mla_q_norm=gm['gq'][:, :MLA_QKD],
        mla_k_norm=gm['gk'][:, :MLA_QKD],
    )
    return sq_err, dx, big, small


def _my_place():
    x, y, c = lax.axis_index("x"), lax.axis_index("y"), lax.axis_index("c")
    return x, y, c


def _flat(px, py, pc):
    return 4 * px + 2 * py + pc


def _peer(x, y, c, r):
    return (1 - x if r & 4 else x, 1 - y if r & 2 else y, 1 - c if r & 1 else c)


def _all_gather(arrays):
    n = len(arrays)

    def body(*refs):
        ins, outs = refs[:n], refs[n:2 * n]
        send_sems, recv_sems, local_sems = refs[2 * n:]
        x, y, c = _my_place()
        me, sibling = (x, y, c), (x, y, 1 - c)
        chips = [(1 - x, y), (x, 1 - y), (1 - x, 1 - y)]

        def copy(a, k, block, to, src=None):
            slot = outs[a].at[_flat(*block)]
            return pltpu.make_async_remote_copy(
                src_ref=slot if src is None else src, dst_ref=slot,
                send_sem=send_sems.at[a, k], recv_sem=recv_sems.at[a, k], device_id=to, device_id_type=MESH)

        mine = [pltpu.make_async_copy(ins[a], outs[a].at[_flat(*me)], local_sems.at[a]) for a in range(n)]
        for cp in mine:
            cp.start()
        first = []
        for a in range(n):
            first.append(copy(a, 0, me, sibling, src=ins[a]))
            first += [copy(a, 1 + j, me, (*chip, c), src=ins[a]) for j, chip in enumerate(chips)]
        for cp in first:
            cp.start()
        passed = []
        for a in range(n):
            for j, chip in enumerate(chips):
                copy(a, 1 + j, (*chip, c), me).wait_recv()
                passed.append(copy(a, 4 + j, (*chip, c), sibling))
                passed[-1].start()
        for a in range(n):
            copy(a, 0, sibling, me).wait_recv()
            for j, chip in enumerate(chips):
                copy(a, 4 + j, (*chip, 1 - c), me).wait_recv()
        for cp in first + passed:
            cp.wait_send()
        for cp in mine:
            cp.wait()

    return pl.pallas_call(
        body, name="all_gather_weights",
        in_specs=[ANY] * n, out_specs=[ANY] * n,
        out_shape=[_sds((N_DEV,) + a.shape, a.dtype) for a in arrays],
        scratch_shapes=[pltpu.SemaphoreType.DMA((n, 7)), pltpu.SemaphoreType.DMA((n, 7)),
                        pltpu.SemaphoreType.DMA((n,))],
    )(*arrays)


def _reduce_scatter(arrays):
    n = len(arrays)

    def body(*refs):
        ins, outs = refs[:n], refs[n:2 * n]
        send_sems, recv_sems, local_sems = refs[2 * n:]
        x, y, c = _my_place()
        mine = [pltpu.make_async_copy(ins[a].at[_flat(x, y, c)], outs[a].at[0], local_sems.at[a]) for a in range(n)]
        for cp in mine:
            cp.start()
        copies = []
        for a in range(n):
            for r in range(1, N_DEV):
                peer = _peer(x, y, c, r)
                copies.append(pltpu.make_async_remote_copy(
                    src_ref=ins[a].at[_flat(*peer)], dst_ref=outs[a].at[r],
                    send_sem=send_sems.at[a, r - 1], recv_sem=recv_sems.at[a, r - 1],
                    device_id=peer, device_id_type=MESH))
        for cp in copies:
            cp.start()
        for cp in copies:
            cp.wait()
        for cp in mine:
            cp.wait()

    return pl.pallas_call(
        body, name="reduce_scatter_grads",
        in_specs=[ANY] * n, out_specs=[ANY] * n,
        out_shape=[_sds(a.shape, a.dtype) for a in arrays],
        scratch_shapes=[pltpu.SemaphoreType.DMA((n, 7)), pltpu.SemaphoreType.DMA((n, 7)),
                        pltpu.SemaphoreType.DMA((n,))],
    )(*arrays)


def _all_reduce_small(pack):
    def body(in_ref, out_ref, buf, send_sems, recv_sems):
        x, y, c = _my_place()
        buf[_flat(x, y, c)] = in_ref[...]
        copies = []
        for r in range(1, N_DEV):
            peer = _peer(x, y, c, r)
            send = pltpu.make_async_remote_copy(
                src_ref=in_ref, dst_ref=buf.at[_flat(x, y, c)],
                send_sem=send_sems.at[r - 1], recv_sem=recv_sems.at[r - 1], device_id=peer, device_id_type=MESH)
            send.start()
            recv = pltpu.make_async_remote_copy(
                src_ref=in_ref, dst_ref=buf.at[_flat(*peer)],
                send_sem=send_sems.at[r - 1], recv_sem=recv_sems.at[r - 1], device_id=peer, device_id_type=MESH)
            copies.append((send, recv))
        for send, recv in copies:
            send.wait_send()
            recv.wait_recv()
        acc = buf[0]
        for s in range(1, N_DEV):
            acc = acc + buf[s]
        out_ref[...] = acc

    vm = pl.BlockSpec(memory_space=pltpu.VMEM)
    return pl.pallas_call(
        body, name="all_reduce_small", in_specs=[vm], out_specs=vm,
        out_shape=_sds(pack.shape, pack.dtype),
        scratch_shapes=[pltpu.VMEM((N_DEV,) + pack.shape, pack.dtype),
                        pltpu.SemaphoreType.DMA((7,)), pltpu.SemaphoreType.DMA((7,))],
    )(pack)


def _adamw_math(w, g, m, v):
    m = ADAM_B1 * m + (1.0 - ADAM_B1) * g
    v = ADAM_B2 * v + (1.0 - ADAM_B2) * (g * g)
    m_hat = m / (1.0 - ADAM_B1 ** ADAM_STEP)
    v_hat = v / (1.0 - ADAM_B2 ** ADAM_STEP)
    delta = -ADAM_LR * (m_hat / (jnp.sqrt(v_hat) + ADAM_EPS) + ADAM_WD * w)
    return delta, m, v


def _adamw_big(name, w, m, v, parts):
    rows, cols = w.shape
    tr = next(cand for cand in (256, 128, 64, 32, 16, 8) if rows % cand == 0)

    def body(w_ref, m_ref, v_ref, p_ref, g_ref, d_ref, mo_ref, vo_ref):
        g = p_ref[0].astype(F32)
        for s in range(1, N_DEV):
            g = g + p_ref[s].astype(F32)
        delta, mn, vn = _adamw_math(w_ref[...], g, m_ref[...], v_ref[...])
        g_ref[...] = g
        d_ref[...] = delta
        mo_ref[...] = mn
        vo_ref[...] = vn

    blk = pl.BlockSpec((tr, cols), lambda i: (i, 0))
    return pl.pallas_call(
        body, name=name, grid=(rows // tr,),
        in_specs=[blk, blk, blk, pl.BlockSpec((N_DEV, tr, cols), lambda i: (0, i, 0))],
        out_specs=[blk] * 4, out_shape=[_sds((rows, cols), F32)] * 4,
        compiler_params=_cparams(("parallel",)),
    )(w, m, v, parts)


def _adamw_small(ws, gs, ms, vs):
    n = len(ws)

    def body(*refs):
        w_refs, g_refs, m_refs, v_refs = (refs[i * n:(i + 1) * n] for i in range(4))
        d_out, m_out, v_out = (refs[(4 + i) * n:(5 + i) * n] for i in range(3))
        for i in range(n):
            delta, mn, vn = _adamw_math(w_refs[i][...], g_refs[i][...], m_refs[i][...], v_refs[i][...])
            d_out[i][...] = delta
            m_out[i][...] = mn
            v_out[i][...] = vn

    vm = pl.BlockSpec(memory_space=pltpu.VMEM)
    outs = pl.pallas_call(
        body, name="adamw_small", in_specs=[vm] * (4 * n), out_specs=[vm] * (3 * n),
        out_shape=[_sds(a.shape, F32) for a in ws] * 3,
    )(*ws, *gs, *ms, *vs)
    return outs[:n], outs[n:2 * n], outs[2 * n:]


SMALL_ROWS = 16


def _pad_to(a, rows, cols):
    return jnp.pad(a, ((0, rows - a.shape[0]), (0, cols - a.shape[1])))


def _prepare_weights(p):
    gn_pack = jnp.concatenate([
        _pad_to(p['ret_gn'][0], RET_HEADS, 128), _pad_to(p['mla_q_a_norm'], 1, 128),
        _pad_to(p['mla_kv_a_norm'], 1, 128), jnp.zeros((2, 128), F32)], axis=0)
    shards = [gn_pack,
              p['ret_w_in'][0].astype(BF16), p['ret_w_out'][0].astype(BF16),
              p['mlp_w1'].astype(BF16), p['mlp_w2'].astype(BF16),
              p['ple_gate_w'].astype(BF16), p['ple_proj_w'].astype(BF16),
              p['mla_w_in'][0].astype(BF16), p['mla_w_uq'][0].astype(BF16), p['mla_w_ukv'][0].astype(BF16),
              p['mla_w_out'][0].astype(BF16)]
    pack, wri, wro, w1, w2, wg, wp, wmi, wuq, wukv, wmo = _all_gather(shards)
    n = N_DEV
    w = {k: p[k] for k in ('mix_norm', 'mlp_norm', 'ple_norm')}
    w['ret_gn'] = pack[:, :RET_HEADS, :RET_DV // n].transpose(1, 0, 2).reshape(RET_HEADS, RET_DV)
    w['mla_q_a_norm'] = pack[:, RET_HEADS, :MLA_Q_RANK // n].reshape(1, MLA_Q_RANK)
    w['mla_kv_a_norm'] = pack[:, RET_HEADS + 1, :MLA_KV_RANK // n].reshape(1, MLA_KV_RANK)
    w['ret_w_in'] = wri
    w['ret_w_out'] = wro.reshape(RET_V_W, D_MODEL)
    w['mlp_w1'] = w1
    w['mlp_w2'] = w2
    w['ple_gate_w'] = [wg[:, l].reshape(D_MODEL, D_MODEL) for l in range(2)]
    w['ple_proj_w'] = [wp[:, l].transpose(1, 0, 2).reshape(PLE_DIM, D_MODEL) for l in range(2)]
    w['mla_w_in'] = jnp.pad(wmi.reshape(D_MODEL, MLA_IN), ((0, 0), (0, MLA_IN_PAD - MLA_IN)))
    w['mla_w_uq'] = jnp.pad(wuq, ((0, 0), (0, 0), (0, MLA_HD_PAD - MLA_QKD)))
    w['mla_w_ukv'] = wukv
    w['mla_q_norm'] = _pad_to(p['mla_q_norm'], 1, MLA_HD_PAD)
    w['mla_k_norm'] = _pad_to(p['mla_k_norm'], 1, MLA_HD_PAD)
    w['mla_w_out'] = wmo.reshape(D_MODEL, D_MODEL)
    return w


def _small_grads(small):
    wide = lambda a: _pad_to(a, a.shape[0], D_MODEL)
    pack = jnp.concatenate([
        small['mix_norm'], small['mlp_norm'], small['ple_norm'], wide(small['ret_gn']),
        wide(small['mla_q_a_norm']), wide(small['mla_kv_a_norm']), wide(small['mla_q_norm']),
        wide(small['mla_k_norm']), jnp.zeros((2, D_MODEL), F32)], axis=0)
    gs = _all_reduce_small(pack)
    me = _flat(*_my_place())
    n = N_DEV
    return dict(
        mix_norm=gs[0:2], mlp_norm=gs[2:4], ple_norm=gs[4:6],
        ret_gn=lax.dynamic_slice(gs, (6, me * (RET_DV // n)), (RET_HEADS, RET_DV // n)),
        mla_q_a_norm=lax.dynamic_slice(gs, (10, me * (MLA_Q_RANK // n)), (1, MLA_Q_RANK // n)),
        mla_kv_a_norm=lax.dynamic_slice(gs, (11, me * (MLA_KV_RANK // n)), (1, MLA_KV_RANK // n)),
        mla_q_norm=gs[12:13, :MLA_QKD], mla_k_norm=gs[13:14, :MLA_QKD])


def kernel(x, p, mix_norm, ret_w_in, ret_gn, ret_w_out, mla_w_in, mla_q_a_norm, mla_kv_a_norm, mla_w_uq, mla_w_ukv, mla_q_norm, mla_k_norm, mla_w_out, mlp_norm, mlp_w1, mlp_w2, ple_norm, ple_gate_w, ple_proj_w, loss_target, m_mix_norm, m_ret_w_in, m_ret_gn, m_ret_w_out, m_mla_w_in, m_mla_q_a_norm, m_mla_kv_a_norm, m_mla_w_uq, m_mla_w_ukv, m_mla_q_norm, m_mla_k_norm, m_mla_w_out, m_mlp_norm, m_mlp_w1, m_mlp_w2, m_ple_norm, m_ple_gate_w, m_ple_proj_w, v_mix_norm, v_ret_w_in, v_ret_gn, v_ret_w_out, v_mla_w_in, v_mla_q_a_norm, v_mla_kv_a_norm, v_mla_w_uq, v_mla_w_ukv, v_mla_q_norm, v_mla_k_norm, v_mla_w_out, v_mlp_norm, v_mlp_w1, v_mlp_w2, v_ple_norm, v_ple_gate_w, v_ple_proj_w):
    given = dict(locals())
    params = {n: given[n] for n in WEIGHTS}
    w = _prepare_weights(params)
    sq_err, grad_x, big, small = _local_step(x[0], p, loss_target[0], w)
    loss = lax.psum(0.5 / D_MODEL * sq_err[0, 0], ("x", "y", "c"))

    grads, deltas, new_m, new_v = {}, {}, {}, {}
    parts = _reduce_scatter([big[n] for n in BIG])
    for n, part in zip(BIG, parts):
        shape = params[n].shape
        flat = lambda a: a.reshape(-1, shape[-1])
        out = _adamw_big("adamw_" + n, flat(params[n]), flat(given["m_" + n]), flat(given["v_" + n]),
                         part.reshape(N_DEV, -1, shape[-1]))
        grads[n], deltas[n], new_m[n], new_v[n] = (o.reshape(shape) for o in out)

    sg = _small_grads(small)
    two_d = lambda a: a.reshape(-1, a.shape[-1])
    d_s, m_s, v_s = _adamw_small([two_d(params[n]) for n in SMALL], [sg[n] for n in SMALL],
                                 [two_d(given["m_" + n]) for n in SMALL], [two_d(given["v_" + n]) for n in SMALL])
    for i, n in enumerate(SMALL):
        shape = params[n].shape
        grads[n], deltas[n], new_m[n], new_v[n] = (a.reshape(shape) for a in (sg[n], d_s[i], m_s[i], v_s[i]))

    return (loss, grad_x[None], *[grads[n] for n in WEIGHTS], *[deltas[n] for n in WEIGHTS],
            *[new_m[n] for n in WEIGHTS], *[new_v[n] for n in WEIGHTS])
```

```python
import functools
import math

import jax
import jax.numpy as jnp
from jax import lax
from jax.experimental import pallas as pl
from jax.experimental.pallas import tpu as pltpu

F32 = jnp.float32
BF16 = jnp.bfloat16
MESH = pl.DeviceIdType.MESH
ANY = pl.BlockSpec(memory_space=pl.ANY)

N_DEV = 8
D_MODEL = 1024
CHUNK = 64
EPS = 1e-6
ROPE_THETA = 10000.0
RET_HEADS = 4
RET_DK = 256
RET_DV = 512
RET_QK_W = RET_HEADS * RET_DK
RET_V_W = RET_HEADS * RET_DV
RET_IN = 2 * RET_QK_W + 2 * RET_V_W
MLA_HEADS = 8
MLA_NOPE = 128
MLA_ROPE = 64
MLA_QKD = MLA_NOPE + MLA_ROPE
MLA_VD = 128
MLA_Q_RANK = 384
MLA_KV_RANK = 256
MLA_IN = MLA_Q_RANK + MLA_KV_RANK + MLA_ROPE
MLA_IN_PAD = 768
MLA_HD_PAD = 256
D_FF = 4096
PLE_DIM = 256
ATT_SCALE = MLA_QKD ** -0.5
LOG2E = 1.4426950408889634
ATT_EXP2 = ATT_SCALE * LOG2E

ADAM_LR = 0.001
ADAM_B1 = 0.9
ADAM_B2 = 0.999
ADAM_EPS = 1e-08
ADAM_WD = 0.01
ADAM_STEP = 10

VMEM_LIMIT = 52 * 1024 * 1024
ROW_TILE = 1024
RET_ROWS = 256
ATT_BLOCK = 256
ATT_QROWS = 512
ATT_HEADS = 2

WEIGHTS = ['mix_norm', 'ret_w_in', 'ret_gn', 'ret_w_out', 'mla_w_in', 'mla_q_a_norm', 'mla_kv_a_norm',
           'mla_w_uq', 'mla_w_ukv', 'mla_q_norm', 'mla_k_norm', 'mla_w_out', 'mlp_norm', 'mlp_w1', 'mlp_w2',
           'ple_norm', 'ple_gate_w', 'ple_proj_w']
BIG = ['ret_w_in', 'ret_w_out', 'mla_w_in', 'mla_w_uq', 'mla_w_ukv', 'mla_w_out', 'mlp_w1', 'mlp_w2',
       'ple_gate_w', 'ple_proj_w']
SMALL = [w for w in WEIGHTS if w not in BIG]


def _cparams(sem=None):
    return pltpu.CompilerParams(dimension_semantics=sem, vmem_limit_bytes=VMEM_LIMIT)


def _dot(a, b, ca, cb):
    return lax.dot_general(a, b, (((ca,), (cb,)), ((), ())), preferred_element_type=F32)


def _bf(v):
    return v if v.dtype == BF16 else v.astype(BF16)


def _sigmoid(z):
    return 1.0 / (1.0 + jnp.exp(-z))


def _mm(name, grid, a, a_spec, b, b_spec, contract, outs, extras=(), epi=None):
    nk = grid[2]
    n_ex, n_out = len(extras), len(outs)
    acc_shape = tuple(d for d in outs[0][1].block_shape if d is not None)

    def body(*refs):
        a_ref, b_ref = refs[:2]
        ex_refs = refs[2:2 + n_ex]
        out_refs = refs[2 + n_ex:2 + n_ex + n_out]

        def product():
            return _dot(_bf(a_ref[...]), _bf(b_ref[...]), contract[0], contract[1])

        def finish(acc):
            res = epi(acc, *[r[...] for r in ex_refs]) if epi is not None else (acc,)
            for o, r in zip(out_refs, res):
                o[...] = r.astype(o.dtype)

        if nk == 1:
            finish(product())
        else:
            acc_ref = refs[-1]
            k = pl.program_id(2)

            @pl.when(k == 0)
            def _():
                acc_ref[...] = jnp.zeros_like(acc_ref)

            acc_ref[...] += product()

            @pl.when(k == nk - 1)
            def _():
                finish(acc_ref[...])

    return pl.pallas_call(
        body, name=name, grid=grid,
        in_specs=[a_spec, b_spec] + [s for _, s in extras],
        out_specs=[s for _, s in outs],
        out_shape=[s for s, _ in outs],
        scratch_shapes=[pltpu.VMEM(acc_shape, F32)] if nk > 1 else [],
        compiler_params=_cparams(("parallel", "parallel", "arbitrary")),
    )(a, b, *[x for x, _ in extras])


def _sds(shape, dtype):
    return jax.ShapeDtypeStruct(shape, dtype)


def _row_tile(t, cap=ROW_TILE):
    return min(cap, t)


def _rms_fwd(name, x, g):
    t, d = x.shape
    tm = _row_tile(t)

    def body(x_ref, g_ref, o_ref):
        xv = x_ref[...]
        r = lax.rsqrt(jnp.mean(xv * xv, axis=-1, keepdims=True) + EPS)
        o_ref[...] = (xv * r * g_ref[...]).astype(o_ref.dtype)

    return pl.pallas_call(
        body, name=name, grid=(t // tm,),
        in_specs=[pl.BlockSpec((tm, d), lambda i: (i, 0)), pl.BlockSpec((1, d), lambda i: (0, 0))],
        out_specs=pl.BlockSpec((tm, d), lambda i: (i, 0)),
        out_shape=_sds((t, d), BF16),
        compiler_params=_cparams(("parallel",)),
    )(x, g)


def _rms_bwd_rows(dy, xv, g, n):
    r = lax.rsqrt(jnp.sum(xv * xv, axis=-1, keepdims=True) / n + EPS)
    xh = xv * r
    dxh = dy * g
    dx = r * (dxh - xh * (jnp.sum(dxh * xh, axis=-1, keepdims=True) / n))
    return dx, dy * xh


def _rms_bwd(name, dy, x, g, res):
    t, d = x.shape
    tm = _row_tile(t, 512)

    def body(dy_ref, x_ref, g_ref, res_ref, dx_ref, dg_ref):
        @pl.when(pl.program_id(0) == 0)
        def _():
            dg_ref[...] = jnp.zeros_like(dg_ref)

        dx, dgr = _rms_bwd_rows(dy_ref[...], x_ref[...], g_ref[...], d)
        dx_ref[...] = res_ref[...] + dx
        dg_ref[...] += jnp.sum(dgr, axis=0, keepdims=True)

    row = pl.BlockSpec((tm, d), lambda i: (i, 0))
    vec = pl.BlockSpec((1, d), lambda i: (0, 0))
    return pl.pallas_call(
        body, name=name, grid=(t // tm,),
        in_specs=[row, row, vec, row], out_specs=[row, vec],
        out_shape=[_sds((t, d), F32), _sds((1, d), F32)],
        compiler_params=_cparams(("arbitrary",)),
    )(dy, x, g, res)


def _loss_head(y, target):
    t, d = y.shape
    tm = _row_tile(t)

    def body(y_ref, t_ref, dy_ref, l_ref):
        @pl.when(pl.program_id(0) == 0)
        def _():
            l_ref[...] = jnp.zeros_like(l_ref)

        e = y_ref[...] - t_ref[...]
        dy_ref[...] = e / d
        l_ref[...] += jnp.sum(jnp.sum(e * e, axis=-1, keepdims=True), axis=0, keepdims=True)

    row = pl.BlockSpec((tm, d), lambda i: (i, 0))
    return pl.pallas_call(
        body, name="loss_head", grid=(t // tm,),
        in_specs=[row, row], out_specs=[row, pl.BlockSpec((8, 128), lambda i: (0, 0))],
        out_shape=[_sds((t, d), F32), _sds((8, 128), F32)],
        compiler_params=_cparams(("arbitrary",)),
    )(y, target)


def _ple_gate_bwd(name, dh, gate, e):
    t, d = dh.shape
    tm = _row_tile(t)

    def body(dh_ref, g_ref, e_ref, de_ref, dz_ref):
        dh_v, gt = dh_ref[...], g_ref[...]
        de_ref[...] = (dh_v * gt).astype(BF16)
        dz_ref[...] = (dh_v * e_ref[...] * (gt * (1.0 - gt))).astype(BF16)

    row = pl.BlockSpec((tm, d), lambda i: (i, 0))
    return pl.pallas_call(
        body, name=name, grid=(t // tm,), in_specs=[row, row, row], out_specs=[row, row],
        out_shape=[_sds((t, d), BF16), _sds((t, d), BF16)],
        compiler_params=_cparams(("parallel",)),
    )(dh, gate, e)


def _rope_half(v, cos, sin):
    half = v.shape[-1] // 2
    v1, v2 = v[:, :half], v[:, half:]
    return jnp.concatenate([v1 * cos - v2 * sin, v2 * cos + v1 * sin], axis=-1)


def _ret_consts():
    lg = jnp.log(1.0 - 2.0 ** (-5.0 - jnp.arange(RET_HEADS, dtype=F32)))
    idx = jnp.arange(CHUNK, dtype=F32)
    intra = jnp.exp(lg[:, None, None] * jnp.abs(idx[:, None] - idx[None, :]))
    qdec = jnp.exp(lg[:, None] * (idx + 1.0))
    kdec = jnp.exp(lg[:, None] * (CHUNK - 1.0 - idx))
    cdec = jnp.exp(lg * CHUNK)
    qdec = jnp.broadcast_to(qdec[:, :, None], (RET_HEADS, CHUNK, RET_DK))
    kdec = jnp.broadcast_to(kdec[:, :, None], (RET_HEADS, CHUNK, RET_DK))
    cdec = jnp.broadcast_to(cdec[:, None, None], (RET_HEADS, 1, RET_DV))
    return intra, qdec, kdec, cdec


def _ret_specs(rb, rev_nb=None):
    blk = (lambda i: i) if rev_nb is None else (lambda i: rev_nb - 1 - i)
    full = lambda shape: pl.BlockSpec(shape, lambda i: (0,) * len(shape))
    return dict(
        proj=pl.BlockSpec((rb, RET_IN), lambda i: (blk(i), 0)),
        tab=pl.BlockSpec((rb, RET_DK // 2), lambda i: (blk(i), 0)),
        vw=pl.BlockSpec((rb, RET_V_W), lambda i: (blk(i), 0)),
        st=pl.BlockSpec((rb // CHUNK, RET_HEADS, RET_DK, RET_DV), lambda i: (blk(i), 0, 0, 0)),
        gn=full((RET_HEADS, 1, RET_DV)),
        intra=full((RET_HEADS, CHUNK, CHUNK)),
        dec=full((RET_HEADS, CHUNK, RET_DK)),
        cdec=full((RET_HEADS, 1, RET_DV)),
    )


def _ret_fwd(proj, cos, sin, gn):
    t = proj.shape[0]
    rb = min(RET_ROWS, t)
    cpb = rb // CHUNK
    intra, qdec, kdec, cdec = _ret_consts()
    sp = _ret_specs(rb)

    def body(proj_ref, cos_ref, sin_ref, gn_ref, intra_ref, qd_ref, kd_ref, cd_ref,
             gated_ref, outp_ref, st_ref, s_ref):
        @pl.when(pl.program_id(0) == 0)
        def _():
            s_ref[...] = jnp.zeros_like(s_ref)

        def chunk(c, carry):
            rows = pl.ds(pl.multiple_of(c * CHUNK, CHUNK), CHUNK)
            cs, sn = cos_ref[rows, :], sin_ref[rows, :]
            for h in range(RET_HEADS):
                q = proj_ref[rows, h * RET_DK:(h + 1) * RET_DK]
                k = proj_ref[rows, RET_QK_W + h * RET_DK:RET_QK_W + (h + 1) * RET_DK]
                v = proj_ref[rows, 2 * RET_QK_W + h * RET_DV:2 * RET_QK_W + (h + 1) * RET_DV]
                g = proj_ref[rows, 2 * RET_QK_W + RET_V_W + h * RET_DV:2 * RET_QK_W + RET_V_W + (h + 1) * RET_DV]
                qr = _rope_half(q, cs, sn)
                kr = _rope_half(k, cs, sn) * (RET_DK ** -0.5)
                qb, kb, vb = qr.astype(BF16), kr.astype(BF16), v.astype(BF16)
                sc = _dot(qb, kb, 1, 1) * intra_ref[h]
                inner = _dot(sc.astype(BF16), vb, 1, 0)
                s_old = s_ref[h]
                sb = s_old.astype(BF16)
                st_ref[c, h] = sb
                cross = _dot((qr * qd_ref[h]).astype(BF16), sb, 1, 0)
                out = inner + cross
                s_ref[h] = s_old * cd_ref[h] + _dot((kr * kd_ref[h]).astype(BF16), vb, 0, 0)
                r = lax.rsqrt(jnp.mean(out * out, axis=-1, keepdims=True) + EPS)
                y = out * r * gn_ref[h]
                cols = slice(h * RET_DV, (h + 1) * RET_DV)
                gated_ref[rows, cols] = (g * _sigmoid(g) * y).astype(BF16)
                outp_ref[rows, cols] = out
            return carry

        lax.fori_loop(0, cpb, chunk, 0)

    return pl.pallas_call(
        body, name="ret_fwd", grid=(t // rb,),
        in_specs=[sp['proj'], sp['tab'], sp['tab'], sp['gn'], sp['intra'], sp['dec'], sp['dec'], sp['cdec']],
        out_specs=[sp['vw'], sp['vw'], sp['st']],
        out_shape=[_sds((t, RET_V_W), BF16), _sds((t, RET_V_W), F32),
                   _sds((t // CHUNK, RET_HEADS, RET_DK, RET_DV), BF16)],
        scratch_shapes=[pltpu.VMEM((RET_HEADS, RET_DK, RET_DV), F32)],
        compiler_params=_cparams(("arbitrary",)),
    )(proj, cos, sin, gn.reshape(RET_HEADS, 1, RET_DV), intra, qdec, kdec, cdec)


def _ret_bwd(proj, cos, sin, gn, outp, states, dgated):
    t = proj.shape[0]
    rb = min(RET_ROWS, t)
    cpb = rb // CHUNK
    nb = t // rb
    intra, qdec, kdec, cdec = _ret_consts()
    sp = _ret_specs(rb, rev_nb=nb)

    def body(proj_ref, cos_ref, sin_ref, gn_ref, intra_ref, qd_ref, kd_ref, cd_ref, outp_ref, st_ref, dgt_ref,
             dproj_ref, dgn_ref, ds_ref):
        @pl.when(pl.program_id(0) == 0)
        def _():
            ds_ref[...] = jnp.zeros_like(ds_ref)
            dgn_ref[...] = jnp.zeros_like(dgn_ref)

        def chunk(cc, carry):
            c = cpb - 1 - cc
            rows = pl.ds(pl.multiple_of(c * CHUNK, CHUNK), CHUNK)
            cs, sn = cos_ref[rows, :], sin_ref[rows, :]
            for h in range(RET_HEADS):
                q = proj_ref[rows, h * RET_DK:(h + 1) * RET_DK]
                k = proj_ref[rows, RET_QK_W + h * RET_DK:RET_QK_W + (h + 1) * RET_DK]
                v = proj_ref[rows, 2 * RET_QK_W + h * RET_DV:2 * RET_QK_W + (h + 1) * RET_DV]
                g = proj_ref[rows, 2 * RET_QK_W + RET_V_W + h * RET_DV:2 * RET_QK_W + RET_V_W + (h + 1) * RET_DV]
                cols = slice(h * RET_DV, (h + 1) * RET_DV)
                qr = _rope_half(q, cs, sn)
                kr = _rope_half(k, cs, sn) * (RET_DK ** -0.5)
                qb, kb, vb = qr.astype(BF16), kr.astype(BF16), v.astype(BF16)
                qdb = (qr * qd_ref[h]).astype(BF16)
                kdb = (kr * kd_ref[h]).astype(BF16)
                out = outp_ref[rows, cols]
                dgt = dgt_ref[rows, cols]
                gnh = gn_ref[h]
                r = lax.rsqrt(jnp.mean(out * out, axis=-1, keepdims=True) + EPS)
                xh = out * r
                sg = _sigmoid(g)
                dgate = dgt * (xh * gnh) * (sg * (1.0 + g * (1.0 - sg)))
                dy = dgt * (g * sg)
                dgn_ref[h] += jnp.sum(dy * xh, axis=0, keepdims=True)
                dxh = dy * gnh
                dout = r * (dxh - xh * jnp.mean(dxh * xh, axis=-1, keepdims=True))
                doutb = dout.astype(BF16)
                itr = intra_ref[h]
                pb = (_dot(qb, kb, 1, 1) * itr).astype(BF16)
                dv = _dot(pb, doutb, 0, 0)
                dsc = (_dot(doutb, vb, 1, 1) * itr).astype(BF16)
                dq = _dot(dsc, kb, 1, 0)
                dk = _dot(dsc, qb, 0, 0)
                dq = dq + _dot(doutb, st_ref[c, h], 1, 1) * qd_ref[h]
                ds_new = ds_ref[h]
                dsb = ds_new.astype(BF16)
                dk = dk + _dot(vb, dsb, 1, 1) * kd_ref[h]
                dv = dv + _dot(kdb, dsb, 1, 0)
                ds_ref[h] = ds_new * cd_ref[h] + _dot(qdb, doutb, 0, 0)
                dproj_ref[rows, h * RET_DK:(h + 1) * RET_DK] = _rope_half(dq, cs, -sn).astype(BF16)
                dproj_ref[rows, RET_QK_W + h * RET_DK:RET_QK_W + (h + 1) * RET_DK] = (
                    _rope_half(dk * (RET_DK ** -0.5), cs, -sn).astype(BF16))
                dproj_ref[rows, 2 * RET_QK_W + h * RET_DV:2 * RET_QK_W + (h + 1) * RET_DV] = dv.astype(BF16)
                dproj_ref[rows, 2 * RET_QK_W + RET_V_W + h * RET_DV:
                          2 * RET_QK_W + RET_V_W + (h + 1) * RET_DV] = dgate.astype(BF16)
            return carry

        lax.fori_loop(0, cpb, chunk, 0)

    return pl.pallas_call(
        body, name="ret_bwd", grid=(nb,),
        in_specs=[sp['proj'], sp['tab'], sp['tab'], sp['gn'], sp['intra'], sp['dec'], sp['dec'], sp['cdec'],
                  sp['vw'], sp['st'], sp['vw']],
        out_specs=[sp['proj'], sp['gn']],
        out_shape=[_sds((t, RET_IN), BF16), _sds((RET_HEADS, 1, RET_DV), F32)],
        scratch_shapes=[pltpu.VMEM((RET_HEADS, RET_DK, RET_DV), F32)],
        compiler_params=_cparams(("arbitrary",)),
    )(proj, cos, sin, gn.reshape(RET_HEADS, 1, RET_DV), intra, qdec, kdec, cdec, outp, states, dgated)


def _mla_tables(t):
    half = MLA_ROPE // 2
    inv = 1.0 / (ROPE_THETA ** (jnp.arange(0, MLA_ROPE, 2, dtype=F32) / MLA_ROPE))
    ang = jnp.arange(t, dtype=F32)[:, None] * inv[None, :]
    cos, sin = jnp.cos(ang), jnp.sin(ang)
    z = jnp.zeros((t, half), F32)
    c = jnp.concatenate([cos, cos, z, z], axis=1)
    s1 = jnp.concatenate([-sin, z, z, z], axis=1)
    s2 = jnp.concatenate([z, sin, z, z], axis=1)
    return c, s1, s2


def _rope_tile(r, c, s1, s2):
    return r * c + pltpu.roll(r, 96, 1) * s1 + pltpu.roll(r, 32, 1) * s2


def _mla_mid(proj2, qa, kva):
    t = proj2.shape[0]
    tm = _row_tile(t)

    def body(p_ref, qa_ref, kva_ref, cq_ref, ckv_ref):
        cq = p_ref[:, :MLA_Q_RANK]
        ckv = p_ref[:, MLA_Q_RANK:MLA_Q_RANK + MLA_KV_RANK]
        rq = lax.rsqrt(jnp.mean(cq * cq, axis=-1, keepdims=True) + EPS)
        rkv = lax.rsqrt(jnp.mean(ckv * ckv, axis=-1, keepdims=True) + EPS)
        cq_ref[...] = (cq * rq * qa_ref[...]).astype(BF16)
        ckv_ref[...] = (ckv * rkv * kva_ref[...]).astype(BF16)

    return pl.pallas_call(
        body, name="mla_mid", grid=(t // tm,),
        in_specs=[pl.BlockSpec((tm, MLA_IN_PAD), lambda i: (i, 0)),
                  pl.BlockSpec((1, MLA_Q_RANK), lambda i: (0, 0)),
                  pl.BlockSpec((1, MLA_KV_RANK), lambda i: (0, 0))],
        out_specs=[pl.BlockSpec((tm, MLA_Q_RANK), lambda i: (i, 0)),
                   pl.BlockSpec((tm, MLA_KV_RANK), lambda i: (i, 0))],
        out_shape=[_sds((t, MLA_Q_RANK), BF16), _sds((t, MLA_KV_RANK), BF16)],
        compiler_params=_cparams(("parallel",)),
    )(proj2, qa, kva)


def _mla_mid_bwd(proj2, qa, kva, dcq, dckv, dkr):
    t = proj2.shape[0]
    tm = _row_tile(t)

    def body(p_ref, qa_ref, kva_ref, dcq_ref, dckv_ref, dkr_ref, dp_ref, dqa_ref, dkva_ref):
        @pl.when(pl.program_id(0) == 0)
        def _():
            dqa_ref[...] = jnp.zeros_like(dqa_ref)
            dkva_ref[...] = jnp.zeros_like(dkva_ref)

        dxq, dgq = _rms_bwd_rows(dcq_ref[...], p_ref[:, :MLA_Q_RANK], qa_ref[...], MLA_Q_RANK)
        dxk, dgk = _rms_bwd_rows(dckv_ref[...], p_ref[:, MLA_Q_RANK:MLA_Q_RANK + MLA_KV_RANK], kva_ref[...],
                                 MLA_KV_RANK)
        dp_ref[:, :MLA_Q_RANK] = dxq.astype(BF16)
        dp_ref[:, MLA_Q_RANK:MLA_Q_RANK + MLA_KV_RANK] = dxk.astype(BF16)
        dp_ref[:, MLA_Q_RANK + MLA_KV_RANK:] = dkr_ref[...].astype(BF16)
        dqa_ref[...] += jnp.sum(dgq, axis=0, keepdims=True)
        dkva_ref[...] += jnp.sum(dgk, axis=0, keepdims=True)

    return pl.pallas_call(
        body, name="mla_mid_bwd", grid=(t // tm,),
        in_specs=[pl.BlockSpec((tm, MLA_IN_PAD), lambda i: (i, 0)),
                  pl.BlockSpec((1, MLA_Q_RANK), lambda i: (0, 0)),
                  pl.BlockSpec((1, MLA_KV_RANK), lambda i: (0, 0)),
                  pl.BlockSpec((tm, MLA_Q_RANK), lambda i: (i, 0)),
                  pl.BlockSpec((tm, MLA_KV_RANK), lambda i: (i, 0)),
                  pl.BlockSpec((tm, 128), lambda i: (i, 0))],
        out_specs=[pl.BlockSpec((tm, MLA_IN_PAD), lambda i: (i, 0)),
                   pl.BlockSpec((1, MLA_Q_RANK), lambda i: (0, 0)),
                   pl.BlockSpec((1, MLA_KV_RANK), lambda i: (0, 0))],
        out_shape=[_sds((t, MLA_IN_PAD), BF16), _sds((1, MLA_Q_RANK), F32), _sds((1, MLA_KV_RANK), F32)],
        compiler_params=_cparams(("arbitrary",)),
    )(proj2, qa, kva, dcq, dckv, dkr)


def _mla_prep_specs(t, tm):
    head = lambda w: pl.BlockSpec((None, tm, w), lambda i, h: (h, i, 0))
    return dict(
        head256=head(MLA_HD_PAD), head128=head(MLA_VD),
        kr=pl.BlockSpec((tm, 128), lambda i, h: (i, (MLA_Q_RANK + MLA_KV_RANK) // 128)),
        gain=pl.BlockSpec((1, MLA_HD_PAD), lambda i, h: (0, 0)),
        tab=pl.BlockSpec((tm, 128), lambda i, h: (i, 0)),
    )


def _mla_prep(q, kv, proj2, gq, gk, tabs):
    t = q.shape[1]
    tm = _row_tile(t)
    sp = _mla_prep_specs(t, tm)

    def body(q_ref, kv_ref, kr_ref, gq_ref, gk_ref, c_ref, s1_ref, s2_ref, qh_ref, kh_ref, vh_ref):
        c, s1, s2 = c_ref[...], s1_ref[...], s2_ref[...]

        def norm_rope(xv, gain):
            r = lax.rsqrt(jnp.sum(xv * xv, axis=-1, keepdims=True) / MLA_QKD + EPS)
            y = xv * r * gain
            return jnp.concatenate([y[:, :MLA_NOPE], _rope_tile(y[:, MLA_NOPE:], c, s1, s2)], axis=-1)

        kvv = kv_ref[...]
        qh_ref[...] = norm_rope(q_ref[...], gq_ref[...]).astype(BF16)
        kf = jnp.concatenate([kvv[:, :MLA_NOPE], kr_ref[...]], axis=-1)
        kh_ref[...] = norm_rope(kf, gk_ref[...]).astype(BF16)
        vh_ref[...] = jnp.concatenate([kvv[:, MLA_NOPE:], jnp.ones((tm, MLA_VD), F32)], axis=-1).astype(BF16)

    return pl.pallas_call(
        body, name="mla_prep", grid=(t // tm, MLA_HEADS),
        in_specs=[sp['head256'], sp['head256'], sp['kr'], sp['gain'], sp['gain'], sp['tab'], sp['tab'], sp['tab']],
        out_specs=[sp['head256'], sp['head256'], sp['head256']],
        out_shape=[_sds((MLA_HEADS, t, MLA_HD_PAD), BF16), _sds((MLA_HEADS, t, MLA_HD_PAD), BF16),
                   _sds((MLA_HEADS, t, 2 * MLA_VD), BF16)],
        compiler_params=_cparams(("parallel", "arbitrary")),
    )(q, kv, proj2, gq, gk, *tabs)


def _mla_prep_bwd(q, kv, proj2, gq, gk, tabs, dqt, dkh, dvh):
    t = q.shape[1]
    tm = _row_tile(t)
    ab = dqt.shape[-1]
    sp = _mla_prep_specs(t, tm)

    def body(q_ref, kv_ref, kr_ref, gq_ref, gk_ref, c_ref, s1_ref, s2_ref, dqt_ref, dkh_ref, dvh_ref,
             dq_ref, dkv_ref, dkr_ref, dgq_ref, dgk_ref):
        dqh = jnp.concatenate([dqt_ref[b].T for b in range(tm // ab)], axis=0)
        i, h = pl.program_id(0), pl.program_id(1)

        @pl.when((i == 0) & (h == 0))
        def _():
            dgq_ref[...] = jnp.zeros_like(dgq_ref)
            dgk_ref[...] = jnp.zeros_like(dgk_ref)

        @pl.when(h == 0)
        def _():
            dkr_ref[...] = jnp.zeros_like(dkr_ref)

        c, s1, s2 = c_ref[...], s1_ref[...], s2_ref[...]

        def back(xv, gain, dout):
            dy = jnp.concatenate([dout[:, :MLA_NOPE], _rope_tile(dout[:, MLA_NOPE:], c, -s1, -s2)], axis=-1)
            return _rms_bwd_rows(dy, xv, gain, MLA_QKD)

        kvv = kv_ref[...]
        dxq, dgq = back(q_ref[...], gq_ref[...], dqh)
        kf = jnp.concatenate([kvv[:, :MLA_NOPE], kr_ref[...]], axis=-1)
        dxk, dgk = back(kf, gk_ref[...], dkh_ref[...])
        dq_ref[...] = dxq.astype(BF16)
        dkv_ref[...] = jnp.concatenate([dxk[:, :MLA_NOPE], dvh_ref[...]], axis=-1).astype(BF16)
        dkr_ref[...] += dxk[:, MLA_NOPE:]
        dgq_ref[...] += jnp.sum(dgq, axis=0, keepdims=True)
        dgk_ref[...] += jnp.sum(dgk, axis=0, keepdims=True)

    return pl.pallas_call(
        body, name="mla_prep_bwd", grid=(t // tm, MLA_HEADS),
        in_specs=[sp['head256'], sp['head256'], sp['kr'], sp['gain'], sp['gain'], sp['tab'], sp['tab'], sp['tab'],
                  pl.BlockSpec((None, tm // ab, MLA_HD_PAD, ab), lambda i, h: (h, i, 0, 0)),
                  sp['head256'], sp['head128']],
        out_specs=[sp['head256'], sp['head256'], sp['tab'], sp['gain'], sp['gain']],
        out_shape=[_sds((MLA_HEADS, t, MLA_HD_PAD), BF16), _sds((MLA_HEADS, t, MLA_HD_PAD), BF16),
                   _sds((t, 128), F32), _sds((1, MLA_HD_PAD), F32), _sds((1, MLA_HD_PAD), F32)],
        compiler_params=_cparams(("arbitrary", "arbitrary")),
    )(q, kv, proj2, gq, gk, *tabs, dqt, dkh, dvh)


def _chunk_visible(rows, cols, row_off, col_off):
    rq = lax.shift_right_logical(lax.broadcasted_iota(jnp.int32, (rows, cols), 0) + row_off, 6)
    ck = lax.shift_right_logical(lax.broadcasted_iota(jnp.int32, (rows, cols), 1) + col_off, 6)
    return ck <= rq


def _rows_to_lanes(col):
    return col.T[:8, :]


def _attn_fwd(qh, kh, vh):
    t = qh.shape[1]
    ab = min(ATT_BLOCK, t)
    tq = min(ATT_QROWS, t)
    r = tq // ab
    hg = ATT_HEADS

    def body(q_ref, k_ref, v_ref, o_ref, lse_ref):
        n_un = pl.program_id(1) * r

        def step(b, state, diag):
            rows = pl.ds(pl.multiple_of(b * ab, ab), ab)
            ms, accs = [], []
            for hh in range(hg):
                m, acc = state[0][hh], state[1][hh]
                s = _dot(q_ref[hh], k_ref[hh, rows, :], 1, 1)
                if diag is not None:
                    s = jnp.where(_chunk_visible(tq, ab, 0, diag * ab), s, -1e30)
                m_new = jnp.maximum(m, jnp.max(s, axis=-1, keepdims=True))
                p = jnp.exp2((s - m_new) * ATT_EXP2).astype(BF16)
                accs.append(jnp.exp2((m - m_new) * ATT_EXP2) * acc + _dot(p, v_ref[hh, rows, :], 1, 0))
                ms.append(m_new)
            return tuple(ms), tuple(accs)

        heads = lambda v: tuple(v for _ in range(hg))
        state = (heads(jnp.full((tq, 1), -1e30, F32)), heads(jnp.zeros((tq, 2 * MLA_VD), F32)))
        state = lax.fori_loop(0, n_un, lambda b, st: step(b, st, None), state)
        for d in range(r):
            state = step(n_un + d, state, d)
        ms, accs = state
        for hh in range(hg):
            l = accs[hh][:, MLA_VD:]
            o_ref[:, hh * MLA_VD:(hh + 1) * MLA_VD] = accs[hh][:, :MLA_VD] / l
            lse_t = _rows_to_lanes(ms[hh] * ATT_EXP2 + jnp.log(l) * LOG2E)
            for d in range(r):
                lse_ref[hh, d] = lse_t[:, d * ab:(d + 1) * ab]

    return pl.pallas_call(
        body, name="mla_attn", grid=(MLA_HEADS // hg, t // tq),
        in_specs=[pl.BlockSpec((hg, tq, MLA_HD_PAD), lambda g, i: (g, i, 0)),
                  pl.BlockSpec((hg, t, MLA_HD_PAD), lambda g, i: (g, 0, 0)),
                  pl.BlockSpec((hg, t, 2 * MLA_VD), lambda g, i: (g, 0, 0))],
        out_specs=[pl.BlockSpec((tq, hg * MLA_VD), lambda g, i: (i, g)),
                   pl.BlockSpec((hg, r, 8, ab), lambda g, i: (g, i, 0, 0))],
        out_shape=[_sds((t, MLA_HEADS * MLA_VD), F32), _sds((MLA_HEADS, t // ab, 8, ab), F32)],
        compiler_params=_cparams(("parallel", "arbitrary")),
    )(qh, kh, vh)


def _attn_delta(do, o, ab):
    t = do.shape[0]
    tm = _row_tile(t)

    def body(do_ref, o_ref, d_ref):
        d = jnp.sum(do_ref[...] * o_ref[...], axis=-1, keepdims=True)
        d_t = _rows_to_lanes(jnp.broadcast_to(d, (tm, 128)))
        for b in range(tm // ab):
            d_ref[b] = d_t[:, b * ab:(b + 1) * ab]

    col = pl.BlockSpec((tm, MLA_VD), lambda i, h: (i, h))
    return pl.pallas_call(
        body, name="mla_delta", grid=(t // tm, MLA_HEADS), in_specs=[col, col],
        out_specs=pl.BlockSpec((None, tm // ab, 8, ab), lambda i, h: (h, i, 0, 0)),
        out_shape=_sds((MLA_HEADS, t // ab, 8, ab), F32),
        compiler_params=_cparams(("parallel", "parallel")),
    )(do, o)


def _attn_bwd(qh, kh, vh, dob, lse_t, dl_t):
    t = qh.shape[1]
    ab = min(ATT_BLOCK, t)
    nq = t // ab
    hg = ATT_HEADS

    def body(q_ref, k_ref, v_ref, do_ref, lse_ref, dl_ref, dqt_ref, dk_ref, dv_ref):
        j = pl.program_id(1)

        @pl.when(j == 0)
        def _():
            dqt_ref[...] = jnp.zeros_like(dqt_ref)

        ks = [k_ref[hh] for hh in range(hg)]
        vs = [v_ref[hh, :, :MLA_VD] for hh in range(hg)]
        kts = [k.T for k in ks]

        def step(b, grads, masked):
            rows = pl.ds(pl.multiple_of(b * ab, ab), ab)
            out = []
            for hh in range(hg):
                dk, dv = grads[hh]
                q = q_ref[hh, rows, :]
                do = do_ref[rows, hh * MLA_VD:(hh + 1) * MLA_VD]
                s_t = _dot(ks[hh], q, 1, 1)
                if masked:
                    key_chunk = lax.shift_right_logical(lax.broadcasted_iota(jnp.int32, (ab, ab), 0), 6)
                    query_chunk = lax.shift_right_logical(lax.broadcasted_iota(jnp.int32, (ab, ab), 1), 6)
                    s_t = jnp.where(key_chunk <= query_chunk, s_t, -1e30)
                p_t = jnp.exp2(s_t * ATT_EXP2 - lse_ref[hh, b][0:1, :])
                dp_t = _dot(vs[hh], do, 1, 1)
                ds_t = (p_t * (dp_t - dl_ref[hh, b][0:1, :]) * ATT_SCALE).astype(BF16)
                dqt_ref[hh, b] += _dot(kts[hh], ds_t, 1, 0)
                out.append((dk + _dot(ds_t, q, 1, 0), dv + _dot(p_t.astype(BF16), do, 1, 0)))
            return tuple(out)

        grads = tuple((jnp.zeros((ab, MLA_HD_PAD), F32), jnp.zeros((ab, MLA_VD), F32)) for _ in range(hg))
        grads = step(j, grads, True)
        grads = lax.fori_loop(j + 1, nq, lambda b, g: step(b, g, False), grads)
        for hh in range(hg):
            dk_ref[hh] = grads[hh][0]
            dv_ref[hh] = grads[hh][1]

    whole = lambda w: pl.BlockSpec((hg, t, w), lambda g, j: (g, 0, 0))
    blk = lambda w: pl.BlockSpec((hg, ab, w), lambda g, j: (g, j, 0))
    stat = pl.BlockSpec((hg, nq, 8, ab), lambda g, j: (g, 0, 0, 0))
    return pl.pallas_call(
        body, name="mla_attn_bwd", grid=(MLA_HEADS // hg, nq),
        in_specs=[whole(MLA_HD_PAD), blk(MLA_HD_PAD), blk(2 * MLA_VD),
                  pl.BlockSpec((t, hg * MLA_VD), lambda g, j: (0, g)), stat, stat],
        out_specs=[pl.BlockSpec((hg, nq, MLA_HD_PAD, ab), lambda g, j: (g, 0, 0, 0)), blk(MLA_HD_PAD), blk(MLA_VD)],
        out_shape=[_sds((MLA_HEADS, nq, MLA_HD_PAD, ab), F32), _sds((MLA_HEADS, t, MLA_HD_PAD), F32),
                   _sds((MLA_HEADS, t, MLA_VD), F32)],
        compiler_params=_cparams(("parallel", "arbitrary")),
    )(qh, kh, vh, dob, lse_t, dl_t)


def _mlp_fwd(l, h, norm_g, w1g, w2g):
    t = h.shape[0]
    tm = _row_tile(t)
    nsh, _, _, wsh = w1g.shape
    hn = _rms_fwd(f"mlp_norm{l}", h, norm_g)

    def relu2(acc):
        r = jnp.maximum(acc, 0.0)
        return r, r * r

    tile = pl.BlockSpec((tm, wsh), lambda i, j, k: (i, j))
    r, u = _mm(f"mlp_up{l}", (t // tm, nsh, 1),
               hn, pl.BlockSpec((tm, D_MODEL), lambda i, j, k: (i, 0)),
               w1g, pl.BlockSpec((None, None, D_MODEL, wsh), lambda i, j, k: (j, l, 0, 0)), (1, 0),
               [(_sds((t, D_FF), BF16), tile), (_sds((t, D_FF), BF16), tile)], epi=relu2)
    row = pl.BlockSpec((tm, D_MODEL), lambda i, j, k: (i, 0))
    (h2,) = _mm(f"mlp_down{l}", (t // tm, 1, nsh),
                u, pl.BlockSpec((tm, wsh), lambda i, j, k: (i, k)),
                w2g, pl.BlockSpec((None, None, wsh, D_MODEL), lambda i, j, k: (k, l, 0, 0)), (1, 0),
                [(_sds((t, D_MODEL), F32), row)], extras=[(h, row)], epi=lambda acc, hv: (acc + hv,))
    return h2, (h, hn, r, u)


def _mlp_bwd(l, dh, saved, norm_g, w1g, w2g):
    h, hn, r, u = saved
    t = h.shape[0]
    tm = _row_tile(t)
    tk = _row_tile(t)
    nsh, _, _, wsh = w1g.shape
    tile = pl.BlockSpec((tm, wsh), lambda i, j, k: (i, j))
    (da,) = _mm(f"mlp_du{l}", (t // tm, nsh, 1),
                dh, pl.BlockSpec((tm, D_MODEL), lambda i, j, k: (i, 0)),
                w2g, pl.BlockSpec((None, None, wsh, D_MODEL), lambda i, j, k: (j, l, 0, 0)), (1, 1),
                [(_sds((t, D_FF), BF16), tile)], extras=[(r, tile)],
                epi=lambda acc, rv: (2.0 * rv.astype(F32) * acc,))
    (dw2,) = _mm(f"mlp_dw2{l}", (nsh, 1, t // tk),
                 u, pl.BlockSpec((tk, wsh), lambda i, j, k: (k, i)),
                 dh, pl.BlockSpec((tk, D_MODEL), lambda i, j, k: (k, 0)), (0, 0),
                 [(_sds((nsh, wsh, D_MODEL), BF16), pl.BlockSpec((None, wsh, D_MODEL), lambda i, j, k: (i, 0, 0)))])
    (dw1,) = _mm(f"mlp_dw1{l}", (1, nsh, t // tk),
                 hn, pl.BlockSpec((tk, D_MODEL), lambda i, j, k: (k, 0)),
                 da, pl.BlockSpec((tk, wsh), lambda i, j, k: (k, j)), (0, 0),
                 [(_sds((nsh, D_MODEL, wsh), BF16), pl.BlockSpec((None, D_MODEL, wsh), lambda i, j, k: (j, 0, 0)))])
    (dhn,) = _mm(f"mlp_dhn{l}", (t // tm, 1, nsh),
                 da, pl.BlockSpec((tm, wsh), lambda i, j, k: (i, k)),
                 w1g, pl.BlockSpec((None, None, D_MODEL, wsh), lambda i, j, k: (k, l, 0, 0)), (1, 1),
                 [(_sds((t, D_MODEL), F32), pl.BlockSpec((tm, D_MODEL), lambda i, j, k: (i, 0)))])
    dh_in, dg = _rms_bwd(f"mlp_norm_bwd{l}", dhn, h, norm_g, dh)
    return dh_in, dg, dw1, dw2


def _ple_fwd(l, h, p, norm_g, wg, wp):
    t = h.shape[0]
    tm = _row_tile(t, 512)
    hn = _rms_fwd(f"ple_norm{l}", h, norm_g)
    row = pl.BlockSpec((tm, D_MODEL), lambda i, j, k: (i, 0))
    full = lambda r: pl.BlockSpec((r, D_MODEL), lambda i, j, k: (0, 0))
    (e,) = _mm(f"ple_proj{l}", (t // tm, 1, 1),
               p, pl.BlockSpec((None, None, tm, PLE_DIM), lambda i, j, k: (l, 0, i, 0)),
               wp, full(PLE_DIM), (1, 0), [(_sds((t, D_MODEL), F32), row)])

    def gate_epi(acc, hv, ev):
        gt = _sigmoid(acc)
        return hv + gt * ev, gt

    h_out, gate = _mm(f"ple_gate{l}", (t // tm, 1, 1), hn, row, wg, full(D_MODEL), (1, 0),
                      [(_sds((t, D_MODEL), F32), row), (_sds((t, D_MODEL), F32), row)],
                      extras=[(h, row), (e, row)], epi=gate_epi)
    return h_out, (h, hn, gate, e)


def _ple_bwd(l, dh, saved, p, norm_g, wg):
    h, hn, gate, e = saved
    t = h.shape[0]
    tm = _row_tile(t)
    tk = _row_tile(t, 512)
    de, dz = _ple_gate_bwd(f"ple_gate_bwd{l}", dh, gate, e)
    full = lambda r: pl.BlockSpec((r, D_MODEL), lambda i, j, k: (0, 0))
    rowk = pl.BlockSpec((tk, D_MODEL), lambda i, j, k: (k, 0))
    (dwp,) = _mm(f"ple_dwp{l}", (1, 1, t // tk),
                 p, pl.BlockSpec((None, None, tk, PLE_DIM), lambda i, j, k: (l, 0, k, 0)),
                 de, rowk, (0, 0), [(_sds((PLE_DIM, D_MODEL), BF16), full(PLE_DIM))])
    (dwg,) = _mm(f"ple_dwg{l}", (1, 1, t // tk), hn, rowk, dz, rowk, (0, 0),
                 [(_sds((D_MODEL, D_MODEL), BF16), full(D_MODEL))])
    row = pl.BlockSpec((tm, D_MODEL), lambda i, j, k: (i, 0))
    (dhn,) = _mm(f"ple_dhn{l}", (t // tm, 1, 1), dz, row, wg, full(D_MODEL), (1, 1),
                 [(_sds((t, D_MODEL), F32), row)])
    dh_in, dg = _rms_bwd(f"ple_norm_bwd{l}", dhn, h, norm_g, dh)
    return dh_in, dg, dwg, dwp


def _ret_layer_fwd(x, norm_g, wri, wro, gn, cos, sin):
    t = x.shape[0]
    tm = _row_tile(t)
    nsh, _, wsh = wri.shape
    hn = _rms_fwd("mix_norm0", x, norm_g)
    (proj,) = _mm("ret_in", (t // tm, nsh, 1),
                  hn, pl.BlockSpec((tm, D_MODEL), lambda i, j, k: (i, 0)),
                  wri, pl.BlockSpec((None, D_MODEL, wsh), lambda i, j, k: (j, 0, 0)), (1, 0),
                  [(_sds((t, RET_IN), F32), pl.BlockSpec((tm, wsh), lambda i, j, k: (i, j)))])
    gated, outp, states = _ret_fwd(proj, cos, sin, gn)
    row = pl.BlockSpec((tm, D_MODEL), lambda i, j, k: (i, 0))
    kt = 512
    (h1,) = _mm("ret_out", (t // tm, 1, RET_V_W // kt),
                gated, pl.BlockSpec((tm, kt), lambda i, j, k: (i, k)),
                wro, pl.BlockSpec((kt, D_MODEL), lambda i, j, k: (k, 0)), (1, 0),
                [(_sds((t, D_MODEL), F32), row)], extras=[(x, row)], epi=lambda acc, xv: (acc + xv,))
    return h1, (x, hn, proj, gated, outp, states)


def _ret_layer_bwd(dh, saved, norm_g, wri, wro, gn, cos, sin):
    x, hn, proj, gated, outp, states = saved
    t = x.shape[0]
    tm = _row_tile(t)
    tk = _row_tile(t, 512)
    nsh, _, wsh = wri.shape
    (dgated,) = _mm("ret_dgated", (t // tm, RET_V_W // D_MODEL, 1),
                    dh, pl.BlockSpec((tm, D_MODEL), lambda i, j, k: (i, 0)),
                    wro, pl.BlockSpec((D_MODEL, D_MODEL), lambda i, j, k: (j, 0)), (1, 1),
                    [(_sds((t, RET_V_W), F32), pl.BlockSpec((tm, D_MODEL), lambda i, j, k: (i, j)))])
    kt = 512
    (dwro,) = _mm("ret_dwro", (RET_V_W // kt, 1, t // tk),
                  gated, pl.BlockSpec((tk, kt), lambda i, j, k: (k, i)),
                  dh, pl.BlockSpec((tk, D_MODEL), lambda i, j, k: (k, 0)), (0, 0),
                  [(_sds((RET_V_W, D_MODEL), BF16), pl.BlockSpec((kt, D_MODEL), lambda i, j, k: (i, 0)))])
    dproj, dgn = _ret_bwd(proj, cos, sin, gn, outp, states, dgated)
    (dwri,) = _mm("ret_dwri", (1, nsh, t // tk),
                  hn, pl.BlockSpec((tk, D_MODEL), lambda i, j, k: (k, 0)),
                  dproj, pl.BlockSpec((tk, wsh), lambda i, j, k: (k, j)), (0, 0),
                  [(_sds((nsh, D_MODEL, wsh), BF16), pl.BlockSpec((None, D_MODEL, wsh), lambda i, j, k: (j, 0, 0)))])
    (dhn,) = _mm("ret_dhn", (t // tm, 1, nsh),
                 dproj, pl.BlockSpec((tm, wsh), lambda i, j, k: (i, k)),
                 wri, pl.BlockSpec((None, D_MODEL, wsh), lambda i, j, k: (k, 0, 0)), (1, 1),
                 [(_sds((t, D_MODEL), F32), pl.BlockSpec((tm, D_MODEL), lambda i, j, k: (i, 0)))])
    dx, dg = _rms_bwd("mix_norm_bwd0", dhn, x, norm_g, dh)
    return dx, dg, dwri, dwro, dgn.reshape(RET_HEADS, RET_DV)


def _mla_layer_fwd(h, norm_g, wmi, qa, kva, wuq, wukv, gq, gk, wmo, tabs):
    t = h.shape[0]
    tm = _row_tile(t)
    hn = _rms_fwd("mix_norm1", h, norm_g)
    row = pl.BlockSpec((tm, D_MODEL), lambda i, j, k: (i, 0))
    (proj2,) = _mm("mla_in", (t // tm, 1, 1), hn, row,
                   wmi, pl.BlockSpec((D_MODEL, MLA_IN_PAD), lambda i, j, k: (0, 0)), (1, 0),
                   [(_sds((t, MLA_IN_PAD), F32), pl.BlockSpec((tm, MLA_IN_PAD), lambda i, j, k: (i, 0)))])
    cq, ckv = _mla_mid(proj2, qa, kva)
    head = pl.BlockSpec((None, tm, MLA_HD_PAD), lambda i, j, k: (j, i, 0))
    (q,) = _mm("mla_uq", (t // tm, MLA_HEADS, 1),
               cq, pl.BlockSpec((tm, MLA_Q_RANK), lambda i, j, k: (i, 0)),
               wuq, pl.BlockSpec((None, MLA_Q_RANK, MLA_HD_PAD), lambda i, j, k: (j, 0, 0)), (1, 0),
               [(_sds((MLA_HEADS, t, MLA_HD_PAD), F32), head)])
    (kv,) = _mm("mla_ukv", (t // tm, MLA_HEADS, 1),
                ckv, pl.BlockSpec((tm, MLA_KV_RANK), lambda i, j, k: (i, 0)),
                wukv, pl.BlockSpec((None, MLA_KV_RANK, MLA_HD_PAD), lambda i, j, k: (j, 0, 0)), (1, 0),
                [(_sds((MLA_HEADS, t, MLA_HD_PAD), F32), head)])
    qh, kh, vh = _mla_prep(q, kv, proj2, gq, gk, tabs)
    o, lse = _attn_fwd(qh, kh, vh)
    (h_out,) = _mm("mla_out", (t // tm, 1, 1), o, row,
                   wmo, pl.BlockSpec((D_MODEL, D_MODEL), lambda i, j, k: (0, 0)), (1, 0),
                   [(_sds((t, D_MODEL), F32), row)], extras=[(h, row)], epi=lambda acc, hv: (acc + hv,))
    return h_out, (h, hn, proj2, cq, ckv, q, kv, qh, kh, vh, o, lse)


def _mla_layer_bwd(dh, saved, norm_g, wmi, qa, kva, wuq, wukv, gq, gk, wmo, tabs):
    h, hn, proj2, cq, ckv, q, kv, qh, kh, vh, o, lse = saved
    t = h.shape[0]
    tm = _row_tile(t)
    tk = _row_tile(t, 512)
    row = pl.BlockSpec((tm, D_MODEL), lambda i, j, k: (i, 0))
    rowk = pl.BlockSpec((tk, D_MODEL), lambda i, j, k: (k, 0))
    sq = pl.BlockSpec((D_MODEL, D_MODEL), lambda i, j, k: (0, 0))
    do, dob = _mm("mla_do", (t // tm, 1, 1), dh, row, wmo, sq, (1, 1),
                  [(_sds((t, D_MODEL), F32), row), (_sds((t, D_MODEL), BF16), row)], epi=lambda acc: (acc, acc))
    (dwmo,) = _mm("mla_dwo", (1, 1, t // tk), o, rowk, dh, rowk, (0, 0), [(_sds((D_MODEL, D_MODEL), BF16), sq)])
    delta = _attn_delta(do, o, lse.shape[-1])
    dqt, dkh, dvh = _attn_bwd(qh, kh, vh, dob, lse, delta)
    dq, dkv, dkr, dgq, dgk = _mla_prep_bwd(q, kv, proj2, gq, gk, tabs, dqt, dkh, dvh)

    headk = pl.BlockSpec((None, tk, MLA_HD_PAD), lambda i, j, k: (j, k, 0))
    (dwuq,) = _mm("mla_dwuq", (1, MLA_HEADS, t // tk),
                  cq, pl.BlockSpec((tk, MLA_Q_RANK), lambda i, j, k: (k, 0)), dq, headk, (0, 0),
                  [(_sds((MLA_HEADS, MLA_Q_RANK, MLA_HD_PAD), BF16),
                    pl.BlockSpec((None, MLA_Q_RANK, MLA_HD_PAD), lambda i, j, k: (j, 0, 0)))])
    (dwukv,) = _mm("mla_dwukv", (1, MLA_HEADS, t // tk),
                   ckv, pl.BlockSpec((tk, MLA_KV_RANK), lambda i, j, k: (k, 0)), dkv, headk, (0, 0),
                   [(_sds((MLA_HEADS, MLA_KV_RANK, MLA_HD_PAD), BF16),
                     pl.BlockSpec((None, MLA_KV_RANK, MLA_HD_PAD), lambda i, j, k: (j, 0, 0)))])
    headi = pl.BlockSpec((None, tm, MLA_HD_PAD), lambda i, j, k: (k, i, 0))
    (dcq,) = _mm("mla_dcq", (t // tm, 1, MLA_HEADS), dq, headi,
                 wuq, pl.BlockSpec((None, MLA_Q_RANK, MLA_HD_PAD), lambda i, j, k: (k, 0, 0)), (1, 1),
                 [(_sds((t, MLA_Q_RANK), F32), pl.BlockSpec((tm, MLA_Q_RANK), lambda i, j, k: (i, 0)))])
    (dckv,) = _mm("mla_dckv", (t // tm, 1, MLA_HEADS), dkv, headi,
                  wukv, pl.BlockSpec((None, MLA_KV_RANK, MLA_HD_PAD), lambda i, j, k: (k, 0, 0)), (1, 1),
                  [(_sds((t, MLA_KV_RANK), F32), pl.BlockSpec((tm, MLA_KV_RANK), lambda i, j, k: (i, 0)))])
    dproj2, dqa, dkva = _mla_mid_bwd(proj2, qa, kva, dcq, dckv, dkr)
    win = pl.BlockSpec((D_MODEL, MLA_IN_PAD), lambda i, j, k: (0, 0))
    (dwmi,) = _mm("mla_dwin", (1, 1, t // tk), hn, rowk,
                  dproj2, pl.BlockSpec((tk, MLA_IN_PAD), lambda i, j, k: (k, 0)), (0, 0),
                  [(_sds((D_MODEL, MLA_IN_PAD), BF16), win)])
    (dhn,) = _mm("mla_dhn", (t // tm, 1, 1),
                 dproj2, pl.BlockSpec((tm, MLA_IN_PAD), lambda i, j, k: (i, 0)), wmi, win, (1, 1),
                 [(_sds((t, D_MODEL), F32), row)])
    dh_in, dg = _rms_bwd("mix_norm_bwd1", dhn, h, norm_g, dh)
    return dh_in, dict(mix=dg, wmi=dwmi, qa=dqa, kva=dkva, wuq=dwuq, wukv=dwukv, gq=dgq, gk=dgk, wmo=dwmo)


def _local_step(x, p, target, w):
    t = x.shape[0]
    inv = 1.0 / (ROPE_THETA ** (jnp.arange(0, RET_DK, 2, dtype=F32) / RET_DK))
    ang = jnp.arange(t, dtype=F32)[:, None] * inv[None, :]
    cos_r, sin_r = jnp.cos(ang), jnp.sin(ang)
    tabs = _mla_tables(t)
    row = lambda a, i: a[i:i + 1]

    h1, s_ret = _ret_layer_fwd(x, row(w['mix_norm'], 0), w['ret_w_in'], w['ret_w_out'], w['ret_gn'], cos_r, sin_r)
    h2, s_mlp0 = _mlp_fwd(0, h1, row(w['mlp_norm'], 0), w['mlp_w1'], w['mlp_w2'])
    h3, s_ple0 = _ple_fwd(0, h2, p, row(w['ple_norm'], 0), w['ple_gate_w'][0], w['ple_proj_w'][0])
    mla_w = (w['mla_w_in'], w['mla_q_a_norm'], w['mla_kv_a_norm'], w['mla_w_uq'], w['mla_w_ukv'],
             w['mla_q_norm'], w['mla_k_norm'], w['mla_w_out'], tabs)
    h4, s_mla = _mla_layer_fwd(h3, row(w['mix_norm'], 1), *mla_w)
    h5, s_mlp1 = _mlp_fwd(1, h4, row(w['mlp_norm'], 1), w['mlp_w1'], w['mlp_w2'])
    y, s_ple1 = _ple_fwd(1, h5, p, row(w['ple_norm'], 1), w['ple_gate_w'][1], w['ple_proj_w'][1])

    dy, sq_err = _loss_head(y, target)

    dh5, dg_ple1, dwg1, dwp1 = _ple_bwd(1, dy, s_ple1, p, row(w['ple_norm'], 1), w['ple_gate_w'][1])
    dh4, dg_mlp1, dw1_1, dw2_1 = _mlp_bwd(1, dh5, s_mlp1, row(w['mlp_norm'], 1), w['mlp_w1'], w['mlp_w2'])
    dh3, gm = _mla_layer_bwd(dh4, s_mla, row(w['mix_norm'], 1), *mla_w)
    dh2, dg_ple0, dwg0, dwp0 = _ple_bwd(0, dh3, s_ple0, p, row(w['ple_norm'], 0), w['ple_gate_w'][0])
    dh1, dg_mlp0, dw1_0, dw2_0 = _mlp_bwd(0, dh2, s_mlp0, row(w['mlp_norm'], 0), w['mlp_w1'], w['mlp_w2'])
    dx, dg_mix0, dwri, dwro, dgn = _ret_layer_bwd(dh1, s_ret, row(w['mix_norm'], 0), w['ret_w_in'],
                                                  w['ret_w_out'], w['ret_gn'], cos_r, sin_r)

    n = N_DEV
    colsh = lambda a: a.reshape(a.shape[0], n, a.shape[1] // n).transpose(1, 0, 2)
    rowsh = lambda a: a.reshape(n, a.shape[0] // n, a.shape[1])
    big = dict(
        ret_w_in=dwri,
        ret_w_out=rowsh(dwro),
        mla_w_in=rowsh(gm['wmi'][:, :MLA_IN]),
        mla_w_uq=gm['wuq'][:, :, :MLA_QKD],
        mla_w_ukv=gm['wukv'],
        mla_w_out=rowsh(gm['wmo']),
        mlp_w1=jnp.stack([dw1_0, dw1_1], axis=1),
        mlp_w2=jnp.stack([dw2_0, dw2_1], axis=1),
        ple_gate_w=jnp.stack([rowsh(dwg0), rowsh(dwg1)], axis=1),
        ple_proj_w=jnp.stack([colsh(dwp0), colsh(dwp1)], axis=1),
    )
    small = dict(
        mix_norm=jnp.concatenate([dg_mix0, gm['mix']], axis=0),
        mlp_norm=jnp.concatenate([dg_mlp0, dg_mlp1], axis=0),
        ple_norm=jnp.concatenate([dg_ple0, dg_ple1], axis=0),
        ret_gn=dgn,
        mla_q_a_norm=gm['qa'],
        mla_kv_a_norm=gm['kva'],
        mla_q_norm=gm['gq'][:, :MLA_QKD],
        mla_k_norm=gm['gk'][:, :MLA_QKD],
    )
    return sq_err, dx, big, small


def _my_place():
    x, y, c = lax.axis_index("x"), lax.axis_index("y"), lax.axis_index("c")
    return x, y, c


def _flat(px, py, pc):
    return 4 * px + 2 * py + pc


def _peer(x, y, c, r):
    return (1 - x if r & 4 else x, 1 - y if r & 2 else y, 1 - c if r & 1 else c)


def _all_gather(arrays):
    n = len(arrays)

    def body(*refs):
        ins, outs = refs[:n], refs[n:2 * n]
        send_sems, recv_sems, local_sems = refs[2 * n:]
        x, y, c = _my_place()
        me, sibling = (x, y, c), (x, y, 1 - c)
        chips = [(1 - x, y), (x, 1 - y), (1 - x, 1 - y)]

        def copy(a, k, block, to, src=None):
            slot = outs[a].at[_flat(*block)]
            return pltpu.make_async_remote_copy(
                src_ref=slot if src is None else src, dst_ref=slot,
                send_sem=send_sems.at[a, k], recv_sem=recv_sems.at[a, k], device_id=to, device_id_type=MESH)

        mine = [pltpu.make_async_copy(ins[a], outs[a].at[_flat(*me)], local_sems.at[a]) for a in range(n)]
        for cp in mine:
            cp.start()
        first = []
        for a in range(n):
            first.append(copy(a, 0, me, sibling, src=ins[a]))
            first += [copy(a, 1 + j, me, (*chip, c), src=ins[a]) for j, chip in enumerate(chips)]
        for cp in first:
            cp.start()
        passed = []
        for a in range(n):
            for j, chip in enumerate(chips):
                copy(a, 1 + j, (*chip, c), me).wait_recv()
                passed.append(copy(a, 4 + j, (*chip, c), sibling))
                passed[-1].start()
        for a in range(n):
            copy(a, 0, sibling, me).wait_recv()
            for j, chip in enumerate(chips):
                copy(a, 4 + j, (*chip, 1 - c), me).wait_recv()
        for cp in first + passed:
            cp.wait_send()
        for cp in mine:
            cp.wait()

    return pl.pallas_call(
        body, name="all_gather_weights",
        in_specs=[ANY] * n, out_specs=[ANY] * n,
        out_shape=[_sds((N_DEV,) + a.shape, a.dtype) for a in arrays],
        scratch_shapes=[pltpu.SemaphoreType.DMA((n, 7)), pltpu.SemaphoreType.DMA((n, 7)),
                        pltpu.SemaphoreType.DMA((n,))],
    )(*arrays)


def _reduce_scatter(arrays):
    n = len(arrays)

    def body(*refs):
        ins, outs = refs[:n], refs[n:2 * n]
        send_sems, recv_sems, local_sems = refs[2 * n:]
        x, y, c = _my_place()
        mine = [pltpu.make_async_copy(ins[a].at[_flat(x, y, c)], outs[a].at[0], local_sems.at[a]) for a in range(n)]
        for cp in mine:
            cp.start()
        copies = []
        for a in range(n):
            for r in range(1, N_DEV):
                peer = _peer(x, y, c, r)
                copies.append(pltpu.make_async_remote_copy(
                    src_ref=ins[a].at[_flat(*peer)], dst_ref=outs[a].at[r],
                    send_sem=send_sems.at[a, r - 1], recv_sem=recv_sems.at[a, r - 1],
                    device_id=peer, device_id_type=MESH))
        for cp in copies:
            cp.start()
        for cp in copies:
            cp.wait()
        for cp in mine:
            cp.wait()

    return pl.pallas_call(
        body, name="reduce_scatter_grads",
        in_specs=[ANY] * n, out_specs=[ANY] * n,
        out_shape=[_sds(a.shape, a.dtype) for a in arrays],
        scratch_shapes=[pltpu.SemaphoreType.DMA((n, 7)), pltpu.SemaphoreType.DMA((n, 7)),
                        pltpu.SemaphoreType.DMA((n,))],
    )(*arrays)


def _all_reduce_small(pack):
    def body(in_ref, out_ref, buf, send_sems, recv_sems):
        x, y, c = _my_place()
        buf[_flat(x, y, c)] = in_ref[...]
        copies = []
        for r in range(1, N_DEV):
            peer = _peer(x, y, c, r)
            send = pltpu.make_async_remote_copy(
                src_ref=in_ref, dst_ref=buf.at[_flat(x, y, c)],
                send_sem=send_sems.at[r - 1], recv_sem=recv_sems.at[r - 1], device_id=peer, device_id_type=MESH)
            send.start()
            recv = pltpu.make_async_remote_copy(
                src_ref=in_ref, dst_ref=buf.at[_flat(*peer)],
                send_sem=send_sems.at[r - 1], recv_sem=recv_sems.at[r - 1], device_id=peer, device_id_type=MESH)
            copies.append((send, recv))
        for send, recv in copies:
            send.wait_send()
            recv.wait_recv()
        acc = buf[0]
        for s in range(1, N_DEV):
            acc = acc + buf[s]
        out_ref[...] = acc

    vm = pl.BlockSpec(memory_space=pltpu.VMEM)
    return pl.pallas_call(
        body, name="all_reduce_small", in_specs=[vm], out_specs=vm,
        out_shape=_sds(pack.shape, pack.dtype),
        scratch_shapes=[pltpu.VMEM((N_DEV,) + pack.shape, pack.dtype),
                        pltpu.SemaphoreType.DMA((7,)), pltpu.SemaphoreType.DMA((7,))],
    )(pack)


def _adamw_math(w, g, m, v):
    m = ADAM_B1 * m + (1.0 - ADAM_B1) * g
    v = ADAM_B2 * v + (1.0 - ADAM_B2) * (g * g)
    m_hat = m / (1.0 - ADAM_B1 ** ADAM_STEP)
    v_hat = v / (1.0 - ADAM_B2 ** ADAM_STEP)
    delta = -ADAM_LR * (m_hat / (jnp.sqrt(v_hat) + ADAM_EPS) + ADAM_WD * w)
    return delta, m, v


def _adamw_big(name, w, m, v, parts):
    rows, cols = w.shape
    tr = next(cand for cand in (256, 128, 64, 32, 16, 8) if rows % cand == 0)

    def body(w_ref, m_ref, v_ref, p_ref, g_ref, d_ref, mo_ref, vo_ref):
        g = p_ref[0].astype(F32)
        for s in range(1, N_DEV):
            g = g + p_ref[s].astype(F32)
        delta, mn, vn = _adamw_math(w_ref[...], g, m_ref[...], v_ref[...])
        g_ref[...] = g
        d_ref[...] = delta
        mo_ref[...] = mn
        vo_ref[...] = vn

    blk = pl.BlockSpec((tr, cols), lambda i: (i, 0))
    return pl.pallas_call(
        body, name=name, grid=(rows // tr,),
        in_specs=[blk, blk, blk, pl.BlockSpec((N_DEV, tr, cols), lambda i: (0, i, 0))],
        out_specs=[blk] * 4, out_shape=[_sds((rows, cols), F32)] * 4,
        compiler_params=_cparams(("parallel",)),
    )(w, m, v, parts)


def _adamw_small(ws, gs, ms, vs):
    n = len(ws)

    def body(*refs):
        w_refs, g_refs, m_refs, v_refs = (refs[i * n:(i + 1) * n] for i in range(4))
        d_out, m_out, v_out = (refs[(4 + i) * n:(5 + i) * n] for i in range(3))
        for i in range(n):
            delta, mn, vn = _adamw_math(w_refs[i][...], g_refs[i][...], m_refs[i][...], v_refs[i][...])
            d_out[i][...] = delta
            m_out[i][...] = mn
            v_out[i][...] = vn

    vm = pl.BlockSpec(memory_space=pltpu.VMEM)
    outs = pl.pallas_call(
        body, name="adamw_small", in_specs=[vm] * (4 * n), out_specs=[vm] * (3 * n),
        out_shape=[_sds(a.shape, F32) for a in ws] * 3,
    )(*ws, *gs, *ms, *vs)
    return outs[:n], outs[n:2 * n], outs[2 * n:]


SMALL_ROWS = 16


def _pad_to(a, rows, cols):
    return jnp.pad(a, ((0, rows - a.shape[0]), (0, cols - a.shape[1])))


def _prepare_weights(p):
    gn_pack = jnp.concatenate([
        _pad_to(p['ret_gn'][0], RET_HEADS, 128), _pad_to(p['mla_q_a_norm'], 1, 128),
        _pad_to(p['mla_kv_a_norm'], 1, 128), jnp.zeros((2, 128), F32)], axis=0)
    shards = [gn_pack,
              p['ret_w_in'][0].astype(BF16), p['ret_w_out'][0].astype(BF16),
              p['mlp_w1'].astype(BF16), p['mlp_w2'].astype(BF16),
              p['ple_gate_w'].astype(BF16), p['ple_proj_w'].astype(BF16),
              p['mla_w_in'][0].astype(BF16), p['mla_w_uq'][0].astype(BF16), p['mla_w_ukv'][0].astype(BF16),
              p['mla_w_out'][0].astype(BF16)]
    pack, wri, wro, w1, w2, wg, wp, wmi, wuq, wukv, wmo = _all_gather(shards)
    n = N_DEV
    w = {k: p[k] for k in ('mix_norm', 'mlp_norm', 'ple_norm')}
    w['ret_gn'] = pack[:, :RET_HEADS, :RET_DV // n].transpose(1, 0, 2).reshape(RET_HEADS, RET_DV)
    w['mla_q_a_norm'] = pack[:, RET_HEADS, :MLA_Q_RANK // n].reshape(1, MLA_Q_RANK)
    w['mla_kv_a_norm'] = pack[:, RET_HEADS + 1, :MLA_KV_RANK // n].reshape(1, MLA_KV_RANK)
    w['ret_w_in'] = wri
    w['ret_w_out'] = wro.reshape(RET_V_W, D_MODEL)
    w['mlp_w1'] = w1
    w['mlp_w2'] = w2
    w['ple_gate_w'] = [wg[:, l].reshape(D_MODEL, D_MODEL) for l in range(2)]
    w['ple_proj_w'] = [wp[:, l].transpose(1, 0, 2).reshape(PLE_DIM, D_MODEL) for l in range(2)]
    w['mla_w_in'] = jnp.pad(wmi.reshape(D_MODEL, MLA_IN), ((0, 0), (0, MLA_IN_PAD - MLA_IN)))
    w['mla_w_uq'] = jnp.pad(wuq, ((0, 0), (0, 0), (0, MLA_HD_PAD - MLA_QKD)))
    w['mla_w_ukv'] = wukv
    w['mla_q_norm'] = _pad_to(p['mla_q_norm'], 1, MLA_HD_PAD)
    w['mla_k_norm'] = _pad_to(p['mla_k_norm'], 1, MLA_HD_PAD)
    w['mla_w_out'] = wmo.reshape(D_MODEL, D_MODEL)
    return w


def _small_grads(small):
    wide = lambda a: _pad_to(a, a.shape[0], D_MODEL)
    pack = jnp.concatenate([
        small['mix_norm'], small['mlp_norm'], small['ple_norm'], wide(small['ret_gn']),
        wide(small['mla_q_a_norm']), wide(small['mla_kv_a_norm']), wide(small['mla_q_norm']),
        wide(small['mla_k_norm']), jnp.zeros((2, D_MODEL), F32)], axis=0)
    gs = _all_reduce_small(pack)
    me = _flat(*_my_place())
    n = N_DEV
    return dict(
        mix_norm=gs[0:2], mlp_norm=gs[2:4], ple_norm=gs[4:6],
        ret_gn=lax.dynamic_slice(gs, (6, me * (RET_DV // n)), (RET_HEADS, RET_DV // n)),
        mla_q_a_norm=lax.dynamic_slice(gs, (10, me * (MLA_Q_RANK // n)), (1, MLA_Q_RANK // n)),
        mla_kv_a_norm=lax.dynamic_slice(gs, (11, me * (MLA_KV_RANK // n)), (1, MLA_KV_RANK // n)),
        mla_q_norm=gs[12:13, :MLA_QKD], mla_k_norm=gs[13:14, :MLA_QKD])


def kernel(x, p, mix_norm, ret_w_in, ret_gn, ret_w_out, mla_w_in, mla_q_a_norm, mla_kv_a_norm, mla_w_uq, mla_w_ukv, mla_q_norm, mla_k_norm, mla_w_out, mlp_norm, mlp_w1, mlp_w2, ple_norm, ple_gate_w, ple_proj_w, loss_target, m_mix_norm, m_ret_w_in, m_ret_gn, m_ret_w_out, m_mla_w_in, m_mla_q_a_norm, m_mla_kv_a_norm, m_mla_w_uq, m_mla_w_ukv, m_mla_q_norm, m_mla_k_norm, m_mla_w_out, m_mlp_norm, m_mlp_w1, m_mlp_w2, m_ple_norm, m_ple_gate_w, m_ple_proj_w, v_mix_norm, v_ret_w_in, v_ret_gn, v_ret_w_out, v_mla_w_in, v_mla_q_a_norm, v_mla_kv_a_norm, v_mla_w_uq, v_mla_w_ukv, v_mla_q_norm, v_mla_k_norm, v_mla_w_out, v_mlp_norm, v_mlp_w1, v_mlp_w2, v_ple_norm, v_ple_gate_w, v_ple_proj_w):
    given = dict(locals())
    params = {n: given[n] for n in WEIGHTS}
    w = _prepare_weights(params)
    sq_err, grad_x, big, small = _local_step(x[0], p, loss_target[0], w)
    loss = lax.psum(0.5 / D_MODEL * sq_err[0, 0], ("x", "y", "c"))

    grads, deltas, new_m, new_v = {}, {}, {}, {}
    parts = _reduce_scatter([big[n] for n in BIG])
    for n, part in zip(BIG, parts):
        shape = params[n].shape
        flat = lambda a: a.reshape(-1, shape[-1])
        out = _adamw_big("adamw_" + n, flat(params[n]), flat(given["m_" + n]), flat(given["v_" + n]),
                         part.reshape(N_DEV, -1, shape[-1]))
        grads[n], deltas[n], new_m[n], new_v[n] = (o.reshape(shape) for o in out)

    sg = _small_grads(small)
    two_d = lambda a: a.reshape(-1, a.shape[-1])
    d_s, m_s, v_s = _adamw_small([two_d(params[n]) for n in SMALL], [sg[n] for n in SMALL],
                                 [two_d(given["m_" + n]) for n in SMALL], [two_d(given["v_" + n]) for n in SMALL])
    for i, n in enumerate(SMALL):
        shape = params[n].shape
        grads[n], deltas[n], new_m[n], new_v[n] = (a.reshape(shape) for a in (sg[n], d_s[i], m_s[i], v_s[i]))

    return (loss, grad_x[None], *[grads[n] for n in WEIGHTS], *[deltas[n] for n in WEIGHTS],
            *[new_m[n] for n in WEIGHTS], *[new_v[n] for n in WEIGHTS])
```

```python
import functools
import math

import jax
import jax.numpy as jnp
from jax import lax
from jax.experimental import pallas as pl
from jax.experimental.pallas import tpu as pltpu

F32 = jnp.float32
BF16 = jnp.bfloat16
MESH = pl.DeviceIdType.MESH
ANY = pl.BlockSpec(memory_space=pl.ANY)

N_DEV = 8
D_MODEL = 1024
CHUNK = 64
EPS = 1e-6
ROPE_THETA = 10000.0
RET_HEADS = 4
RET_DK = 256
RET_DV = 512
RET_QK_W = RET_HEADS * RET_DK
RET_V_W = RET_HEADS * RET_DV
RET_IN = 2 * RET_QK_W + 2 * RET_V_W
MLA_HEADS = 8
MLA_NOPE = 128
MLA_ROPE = 64
MLA_QKD = MLA_NOPE + MLA_ROPE
MLA_VD = 128
MLA_Q_RANK = 384
MLA_KV_RANK = 256
MLA_IN = MLA_Q_RANK + MLA_KV_RANK + MLA_ROPE
MLA_IN_PAD = 768
MLA_HD_PAD = 256
D_FF = 4096
PLE_DIM = 256
ATT_SCALE = MLA_QKD ** -0.5
LOG2E = 1.4426950408889634
ATT_EXP2 = ATT_SCALE * LOG2E

ADAM_LR = 0.001
ADAM_B1 = 0.9
ADAM_B2 = 0.999
ADAM_EPS = 1e-08
ADAM_WD = 0.01
ADAM_STEP = 10

VMEM_LIMIT = 52 * 1024 * 1024
ROW_TILE = 1024
RET_ROWS = 256
ATT_BLOCK = 256
ATT_QROWS = 512
ATT_HEADS = 2

WEIGHTS = ['mix_norm', 'ret_w_in', 'ret_gn', 'ret_w_out', 'mla_w_in', 'mla_q_a_norm', 'mla_kv_a_norm',
           'mla_w_uq', 'mla_w_ukv', 'mla_q_norm', 'mla_k_norm', 'mla_w_out', 'mlp_norm', 'mlp_w1', 'mlp_w2',
           'ple_norm', 'ple_gate_w', 'ple_proj_w']
BIG = ['ret_w_in', 'ret_w_out', 'mla_w_in', 'mla_w_uq', 'mla_w_ukv', 'mla_w_out', 'mlp_w1', 'mlp_w2',
       'ple_gate_w', 'ple_proj_w']
SMALL = [w for w in WEIGHTS if w not in BIG]


def _cparams(sem=None):
    return pltpu.CompilerParams(dimension_semantics=sem, vmem_limit_bytes=VMEM_LIMIT)


def _dot(a, b, ca, cb):
    return lax.dot_general(a, b, (((ca,), (cb,)), ((), ())), preferred_element_type=F32)


def _bf(v):
    return v if v.dtype == BF16 else v.astype(BF16)


def _sigmoid(z):
    return 1.0 / (1.0 + jnp.exp(-z))


def _mm(name, grid, a, a_spec, b, b_spec, contract, outs, extras=(), epi=None, deps=()):
    nk = grid[2]
    n_ex, n_out, n_dep = len(extras), len(outs), len(deps)
    acc_shape = tuple(d for d in outs[0][1].block_shape if d is not None)

    def body(*refs):
        a_ref, b_ref = refs[:2]
        ex_refs = refs[2:2 + n_ex]
        out_refs = refs[2 + n_ex + n_dep:2 + n_ex + n_dep + n_out]

        def product():
            return _dot(_bf(a_ref[...]), _bf(b_ref[...]), contract[0], contract[1])

        def finish(acc):
            res = epi(acc, *[r[...] for r in ex_refs]) if epi is not None else (acc,)
            for o, r in zip(out_refs, res):
                o[...] = r.astype(o.dtype)

        if nk == 1:
            finish(product())
        else:
            acc_ref = refs[-1]
            k = pl.program_id(2)

            @pl.when(k == 0)
            def _():
                acc_ref[...] = jnp.zeros_like(acc_ref)

            acc_ref[...] += product()

            @pl.when(k == nk - 1)
            def _():
                finish(acc_ref[...])

    return pl.pallas_call(
        body, name=name, grid=grid,
        in_specs=[a_spec, b_spec] + [s for _, s in extras] + [ANY] * n_dep,
        out_specs=[s for _, s in outs],
        out_shape=[s for s, _ in outs],
        scratch_shapes=[pltpu.VMEM(acc_shape, F32)] if nk > 1 else [],
        compiler_params=_cparams(("parallel", "parallel", "arbitrary")),
    )(a, b, *[x for x, _ in extras], *deps)


def _sds(shape, dtype):
    return jax.ShapeDtypeStruct(shape, dtype)


def _row_tile(t, cap=ROW_TILE):
    return min(cap, t)


def _rms_fwd(name, x, g):
    t, d = x.shape
    tm = _row_tile(t)

    def body(x_ref, g_ref, o_ref):
        xv = x_ref[...]
        r = lax.rsqrt(jnp.mean(xv * xv, axis=-1, keepdims=True) + EPS)
        o_ref[...] = (xv * r * g_ref[...]).astype(o_ref.dtype)

    return pl.pallas_call(
        body, name=name, grid=(t // tm,),
        in_specs=[pl.BlockSpec((tm, d), lambda i: (i, 0)), pl.BlockSpec((1, d), lambda i: (0, 0))],
        out_specs=pl.BlockSpec((tm, d), lambda i: (i, 0)),
        out_shape=_sds((t, d), BF16),
        compiler_params=_cparams(("parallel",)),
    )(x, g)


def _rms_bwd_rows(dy, xv, g, n):
    r = lax.rsqrt(jnp.sum(xv * xv, axis=-1, keepdims=True) / n + EPS)
    xh = xv * r
    dxh = dy * g
    dx = r * (dxh - xh * (jnp.sum(dxh * xh, axis=-1, keepdims=True) / n))
    return dx, dy * xh


def _rms_bwd(name, dy, x, g, res):
    t, d = x.shape
    tm = _row_tile(t, 512)

    def body(dy_ref, x_ref, g_ref, res_ref, dx_ref, dg_ref):
        @pl.when(pl.program_id(0) == 0)
        def _():
            dg_ref[...] = jnp.zeros_like(dg_ref)

        dx, dgr = _rms_bwd_rows(dy_ref[...], x_ref[...], g_ref[...], d)
        dx_ref[...] = res_ref[...] + dx
        dg_ref[...] += jnp.sum(dgr, axis=0, keepdims=True)

    row = pl.BlockSpec((tm, d), lambda i: (i, 0))
    vec = pl.BlockSpec((1, d), lambda i: (0, 0))
    return pl.pallas_call(
        body, name=name, grid=(t // tm,),
        in_specs=[row, row, vec, row], out_specs=[row, vec],
        out_shape=[_sds((t, d), F32), _sds((1, d), F32)],
        compiler_params=_cparams(("arbitrary",)),
    )(dy, x, g, res)


def _loss_head(y, target):
    t, d = y.shape
    tm = _row_tile(t)

    def body(y_ref, t_ref, dy_ref, l_ref):
        @pl.when(pl.program_id(0) == 0)
        def _():
            l_ref[...] = jnp.zeros_like(l_ref)

        e = y_ref[...] - t_ref[...]
        dy_ref[...] = e / d
        l_ref[...] += jnp.sum(jnp.sum(e * e, axis=-1, keepdims=True), axis=0, keepdims=True)

    row = pl.BlockSpec((tm, d), lambda i: (i, 0))
    return pl.pallas_call(
        body, name="loss_head", grid=(t // tm,),
        in_specs=[row, row], out_specs=[row, pl.BlockSpec((8, 128), lambda i: (0, 0))],
        out_shape=[_sds((t, d), F32), _sds((8, 128), F32)],
        compiler_params=_cparams(("arbitrary",)),
    )(y, target)


def _ple_gate_bwd(name, dh, gate, e):
    t, d = dh.shape
    tm = _row_tile(t)

    def body(dh_ref, g_ref, e_ref, de_ref, dz_ref):
        dh_v, gt = dh_ref[...], g_ref[...]
        de_ref[...] = (dh_v * gt).astype(BF16)
        dz_ref[...] = (dh_v * e_ref[...] * (gt * (1.0 - gt))).astype(BF16)

    row = pl.BlockSpec((tm, d), lambda i: (i, 0))
    return pl.pallas_call(
        body, name=name, grid=(t // tm,), in_specs=[row, row, row], out_specs=[row, row],
        out_shape=[_sds((t, d), BF16), _sds((t, d), BF16)],
        compiler_params=_cparams(("parallel",)),
    )(dh, gate, e)


def _rope_half(v, cos, sin):
    half = v.shape[-1] // 2
    v1, v2 = v[:, :half], v[:, half:]
    return jnp.concatenate([v1 * cos - v2 * sin, v2 * cos + v1 * sin], axis=-1)


def _ret_consts():
    lg = jnp.log(1.0 - 2.0 ** (-5.0 - jnp.arange(RET_HEADS, dtype=F32)))
    idx = jnp.arange(CHUNK, dtype=F32)
    intra = jnp.exp(lg[:, None, None] * jnp.abs(idx[:, None] - idx[None, :]))
    qdec = jnp.exp(lg[:, None] * (idx + 1.0))
    kdec = jnp.exp(lg[:, None] * (CHUNK - 1.0 - idx))
    cdec = jnp.exp(lg * CHUNK)
    qdec = jnp.broadcast_to(qdec[:, :, None], (RET_HEADS, CHUNK, RET_DK))
    kdec = jnp.broadcast_to(kdec[:, :, None], (RET_HEADS, CHUNK, RET_DK))
    cdec = jnp.broadcast_to(cdec[:, None, None], (RET_HEADS, 1, RET_DV))
    return intra, qdec, kdec, cdec


def _ret_specs(rb, rev_nb=None):
    blk = (lambda i: i) if rev_nb is None else (lambda i: rev_nb - 1 - i)
    full = lambda shape: pl.BlockSpec(shape, lambda i: (0,) * len(shape))
    return dict(
        proj=pl.BlockSpec((rb, RET_IN), lambda i: (blk(i), 0)),
        tab=pl.BlockSpec((rb, RET_DK // 2), lambda i: (blk(i), 0)),
        vw=pl.BlockSpec((rb, RET_V_W), lambda i: (blk(i), 0)),
        st=pl.BlockSpec((rb // CHUNK, RET_HEADS, RET_DK, RET_DV), lambda i: (blk(i), 0, 0, 0)),
        gn=full((RET_HEADS, 1, RET_DV)),
        intra=full((RET_HEADS, CHUNK, CHUNK)),
        dec=full((RET_HEADS, CHUNK, RET_DK)),
        cdec=full((RET_HEADS, 1, RET_DV)),
    )


def _ret_fwd(proj, cos, sin, gn):
    t = proj.shape[0]
    rb = min(RET_ROWS, t)
    cpb = rb // CHUNK
    intra, qdec, kdec, cdec = _ret_consts()
    sp = _ret_specs(rb)

    def body(proj_ref, cos_ref, sin_ref, gn_ref, intra_ref, qd_ref, kd_ref, cd_ref,
             gated_ref, outp_ref, st_ref, s_ref):
        @pl.when(pl.program_id(0) == 0)
        def _():
            s_ref[...] = jnp.zeros_like(s_ref)

        def chunk(c, carry):
            rows = pl.ds(pl.multiple_of(c * CHUNK, CHUNK), CHUNK)
            cs, sn = cos_ref[rows, :], sin_ref[rows, :]
            for h in range(RET_HEADS):
                q = proj_ref[rows, h * RET_DK:(h + 1) * RET_DK]
                k = proj_ref[rows, RET_QK_W + h * RET_DK:RET_QK_W + (h + 1) * RET_DK]
                v = proj_ref[rows, 2 * RET_QK_W + h * RET_DV:2 * RET_QK_W + (h + 1) * RET_DV]
                g = proj_ref[rows, 2 * RET_QK_W + RET_V_W + h * RET_DV:2 * RET_QK_W + RET_V_W + (h + 1) * RET_DV]
                qr = _rope_half(q, cs, sn)
                kr = _rope_half(k, cs, sn) * (RET_DK ** -0.5)
                qb, kb, vb = qr.astype(BF16), kr.astype(BF16), v.astype(BF16)
                sc = _dot(qb, kb, 1, 1) * intra_ref[h]
                inner = _dot(sc.astype(BF16), vb, 1, 0)
                s_old = s_ref[h]
                sb = s_old.astype(BF16)
                st_ref[c, h] = sb
                cross = _dot((qr * qd_ref[h]).astype(BF16), sb, 1, 0)
                out = inner + cross
                s_ref[h] = s_old * cd_ref[h] + _dot((kr * kd_ref[h]).astype(BF16), vb, 0, 0)
                r = lax.rsqrt(jnp.mean(out * out, axis=-1, keepdims=True) + EPS)
                y = out * r * gn_ref[h]
                cols = slice(h * RET_DV, (h + 1) * RET_DV)
                gated_ref[rows, cols] = (g * _sigmoid(g) * y).astype(BF16)
                outp_ref[rows, cols] = out
            return carry

        lax.fori_loop(0, cpb, chunk, 0)

    return pl.pallas_call(
        body, name="ret_fwd", grid=(t // rb,),
        in_specs=[sp['proj'], sp['tab'], sp['tab'], sp['gn'], sp['intra'], sp['dec'], sp['dec'], sp['cdec']],
        out_specs=[sp['vw'], sp['vw'], sp['st']],
        out_shape=[_sds((t, RET_V_W), BF16), _sds((t, RET_V_W), F32),
                   _sds((t // CHUNK, RET_HEADS, RET_DK, RET_DV), BF16)],
        scratch_shapes=[pltpu.VMEM((RET_HEADS, RET_DK, RET_DV), F32)],
        compiler_params=_cparams(("arbitrary",)),
    )(proj, cos, sin, gn.reshape(RET_HEADS, 1, RET_DV), intra, qdec, kdec, cdec)


def _ret_bwd(proj, cos, sin, gn, outp, states, dgated):
    t = proj.shape[0]
    rb = min(RET_ROWS, t)
    cpb = rb // CHUNK
    nb = t // rb
    intra, qdec, kdec, cdec = _ret_consts()
    sp = _ret_specs(rb, rev_nb=nb)

    def body(proj_ref, cos_ref, sin_ref, gn_ref, intra_ref, qd_ref, kd_ref, cd_ref, outp_ref, st_ref, dgt_ref,
             dproj_ref, dgn_ref, ds_ref):
        @pl.when(pl.program_id(0) == 0)
        def _():
            ds_ref[...] = jnp.zeros_like(ds_ref)
            dgn_ref[...] = jnp.zeros_like(dgn_ref)

        def chunk(cc, carry):
            c = cpb - 1 - cc
            rows = pl.ds(pl.multiple_of(c * CHUNK, CHUNK), CHUNK)
            cs, sn = cos_ref[rows, :], sin_ref[rows, :]
            for h in range(RET_HEADS):
                q = proj_ref[rows, h * RET_DK:(h + 1) * RET_DK]
                k = proj_ref[rows, RET_QK_W + h * RET_DK:RET_QK_W + (h + 1) * RET_DK]
                v = proj_ref[rows, 2 * RET_QK_W + h * RET_DV:2 * RET_QK_W + (h + 1) * RET_DV]
                g = proj_ref[rows, 2 * RET_QK_W + RET_V_W + h * RET_DV:2 * RET_QK_W + RET_V_W + (h + 1) * RET_DV]
                cols = slice(h * RET_DV, (h + 1) * RET_DV)
                qr = _rope_half(q, cs, sn)
                kr = _rope_half(k, cs, sn) * (RET_DK ** -0.5)
                qb, kb, vb = qr.astype(BF16), kr.astype(BF16), v.astype(BF16)
                qdb = (qr * qd_ref[h]).astype(BF16)
                kdb = (kr * kd_ref[h]).astype(BF16)
                out = outp_ref[rows, cols]
                dgt = dgt_ref[rows, cols]
                gnh = gn_ref[h]
                r = lax.rsqrt(jnp.mean(out * out, axis=-1, keepdims=True) + EPS)
                xh = out * r
                sg = _sigmoid(g)
                dgate = dgt * (xh * gnh) * (sg * (1.0 + g * (1.0 - sg)))
                dy = dgt * (g * sg)
                dgn_ref[h] += jnp.sum(dy * xh, axis=0, keepdims=True)
                dxh = dy * gnh
                dout = r * (dxh - xh * jnp.mean(dxh * xh, axis=-1, keepdims=True))
                doutb = dout.astype(BF16)
                itr = intra_ref[h]
                pb = (_dot(qb, kb, 1, 1) * itr).astype(BF16)
                dv = _dot(pb, doutb, 0, 0)
                dsc = (_dot(doutb, vb, 1, 1) * itr).astype(BF16)
                dq = _dot(dsc, kb, 1, 0)
                dk = _dot(dsc, qb, 0, 0)
                dq = dq + _dot(doutb, st_ref[c, h], 1, 1) * qd_ref[h]
                ds_new = ds_ref[h]
                dsb = ds_new.astype(BF16)
                dk = dk + _dot(vb, dsb, 1, 1) * kd_ref[h]
                dv = dv + _dot(kdb, dsb, 1, 0)
                ds_ref[h] = ds_new * cd_ref[h] + _dot(qdb, doutb, 0, 0)
                dproj_ref[rows, h * RET_DK:(h + 1) * RET_DK] = _rope_half(dq, cs, -sn).astype(BF16)
                dproj_ref[rows, RET_QK_W + h * RET_DK:RET_QK_W + (h + 1) * RET_DK] = (
                    _rope_half(dk * (RET_DK ** -0.5), cs, -sn).astype(BF16))
                dproj_ref[rows, 2 * RET_QK_W + h * RET_DV:2 * RET_QK_W + (h + 1) * RET_DV] = dv.astype(BF16)
                dproj_ref[rows, 2 * RET_QK_W + RET_V_W + h * RET_DV:
                          2 * RET_QK_W + RET_V_W + (h + 1) * RET_DV] = dgate.astype(BF16)
            return carry

        lax.fori_loop(0, cpb, chunk, 0)

    return pl.pallas_call(
        body, name="ret_bwd", grid=(nb,),
        in_specs=[sp['proj'], sp['tab'], sp['tab'], sp['gn'], sp['intra'], sp['dec'], sp['dec'], sp['cdec'],
                  sp['vw'], sp['st'], sp['vw']],
        out_specs=[sp['proj'], sp['gn']],
        out_shape=[_sds((t, RET_IN), BF16), _sds((RET_HEADS, 1, RET_DV), F32)],
        scratch_shapes=[pltpu.VMEM((RET_HEADS, RET_DK, RET_DV), F32)],
        compiler_params=_cparams(("arbitrary",)),
    )(proj, cos, sin, gn.reshape(RET_HEADS, 1, RET_DV), intra, qdec, kdec, cdec, outp, states, dgated)


def _mla_tables(t):
    half = MLA_ROPE // 2
    inv = 1.0 / (ROPE_THETA ** (jnp.arange(0, MLA_ROPE, 2, dtype=F32) / MLA_ROPE))
    ang = jnp.arange(t, dtype=F32)[:, None] * inv[None, :]
    cos, sin = jnp.cos(ang), jnp.sin(ang)
    z = jnp.zeros((t, half), F32)
    c = jnp.concatenate([cos, cos, z, z], axis=1)
    s1 = jnp.concatenate([-sin, z, z, z], axis=1)
    s2 = jnp.concatenate([z, sin, z, z], axis=1)
    return c, s1, s2


def _rope_tile(r, c, s1, s2):
    return r * c + pltpu.roll(r, 96, 1) * s1 + pltpu.roll(r, 32, 1) * s2


def _mla_mid(proj2, qa, kva):
    t = proj2.shape[0]
    tm = _row_tile(t)

    def body(p_ref, qa_ref, kva_ref, cq_ref, ckv_ref):
        cq = p_ref[:, :MLA_Q_RANK]
        ckv = p_ref[:, MLA_Q_RANK:MLA_Q_RANK + MLA_KV_RANK]
        rq = lax.rsqrt(jnp.mean(cq * cq, axis=-1, keepdims=True) + EPS)
        rkv = lax.rsqrt(jnp.mean(ckv * ckv, axis=-1, keepdims=True) + EPS)
        cq_ref[...] = (cq * rq * qa_ref[...]).astype(BF16)
        ckv_ref[...] = (ckv * rkv * kva_ref[...]).astype(BF16)

    return pl.pallas_call(
        body, name="mla_mid", grid=(t // tm,),
        in_specs=[pl.BlockSpec((tm, MLA_IN_PAD), lambda i: (i, 0)),
                  pl.BlockSpec((1, MLA_Q_RANK), lambda i: (0, 0)),
                  pl.BlockSpec((1, MLA_KV_RANK), lambda i: (0, 0))],
        out_specs=[pl.BlockSpec((tm, MLA_Q_RANK), lambda i: (i, 0)),
                   pl.BlockSpec((tm, MLA_KV_RANK), lambda i: (i, 0))],
        out_shape=[_sds((t, MLA_Q_RANK), BF16), _sds((t, MLA_KV_RANK), BF16)],
        compiler_params=_cparams(("parallel",)),
    )(proj2, qa, kva)


def _mla_mid_bwd(proj2, qa, kva, dcq, dckv, dkr):
    t = proj2.shape[0]
    tm = _row_tile(t)

    def body(p_ref, qa_ref, kva_ref, dcq_ref, dckv_ref, dkr_ref, dp_ref, dqa_ref, dkva_ref):
        @pl.when(pl.program_id(0) == 0)
        def _():
            dqa_ref[...] = jnp.zeros_like(dqa_ref)
            dkva_ref[...] = jnp.zeros_like(dkva_ref)

        dxq, dgq = _rms_bwd_rows(dcq_ref[...], p_ref[:, :MLA_Q_RANK], qa_ref[...], MLA_Q_RANK)
        dxk, dgk = _rms_bwd_rows(dckv_ref[...], p_ref[:, MLA_Q_RANK:MLA_Q_RANK + MLA_KV_RANK], kva_ref[...],
                                 MLA_KV_RANK)
        dp_ref[:, :MLA_Q_RANK] = dxq.astype(BF16)
        dp_ref[:, MLA_Q_RANK:MLA_Q_RANK + MLA_KV_RANK] = dxk.astype(BF16)
        dp_ref[:, MLA_Q_RANK + MLA_KV_RANK:] = dkr_ref[...].astype(BF16)
        dqa_ref[...] += jnp.sum(dgq, axis=0, keepdims=True)
        dkva_ref[...] += jnp.sum(dgk, axis=0, keepdims=True)

    return pl.pallas_call(
        body, name="mla_mid_bwd", grid=(t // tm,),
        in_specs=[pl.BlockSpec((tm, MLA_IN_PAD), lambda i: (i, 0)),
                  pl.BlockSpec((1, MLA_Q_RANK), lambda i: (0, 0)),
                  pl.BlockSpec((1, MLA_KV_RANK), lambda i: (0, 0)),
                  pl.BlockSpec((tm, MLA_Q_RANK), lambda i: (i, 0)),
                  pl.BlockSpec((tm, MLA_KV_RANK), lambda i: (i, 0)),
                  pl.BlockSpec((tm, 128), lambda i: (i, 0))],
        out_specs=[pl.BlockSpec((tm, MLA_IN_PAD), lambda i: (i, 0)),
                   pl.BlockSpec((1, MLA_Q_RANK), lambda i: (0, 0)),
                   pl.BlockSpec((1, MLA_KV_RANK), lambda i: (0, 0))],
        out_shape=[_sds((t, MLA_IN_PAD), BF16), _sds((1, MLA_Q_RANK), F32), _sds((1, MLA_KV_RANK), F32)],
        compiler_params=_cparams(("arbitrary",)),
    )(proj2, qa, kva, dcq, dckv, dkr)


def _mla_prep_specs(t, tm):
    head = lambda w: pl.BlockSpec((None, tm, w), lambda i, h: (h, i, 0))
    return dict(
        head256=head(MLA_HD_PAD), head128=head(MLA_VD),
        kr=pl.BlockSpec((tm, 128), lambda i, h: (i, (MLA_Q_RANK + MLA_KV_RANK) // 128)),
        gain=pl.BlockSpec((1, MLA_HD_PAD), lambda i, h: (0, 0)),
        tab=pl.BlockSpec((tm, 128), lambda i, h: (i, 0)),
    )


def _mla_prep(q, kv, proj2, gq, gk, tabs):
    t = q.shape[1]
    tm = _row_tile(t)
    sp = _mla_prep_specs(t, tm)

    def body(q_ref, kv_ref, kr_ref, gq_ref, gk_ref, c_ref, s1_ref, s2_ref, qh_ref, kh_ref, vh_ref):
        c, s1, s2 = c_ref[...], s1_ref[...], s2_ref[...]

        def norm_rope(xv, gain):
            r = lax.rsqrt(jnp.sum(xv * xv, axis=-1, keepdims=True) / MLA_QKD + EPS)
            y = xv * r * gain
            return jnp.concatenate([y[:, :MLA_NOPE], _rope_tile(y[:, MLA_NOPE:], c, s1, s2)], axis=-1)

        kvv = kv_ref[...]
        qh_ref[...] = norm_rope(q_ref[...], gq_ref[...]).astype(BF16)
        kf = jnp.concatenate([kvv[:, :MLA_NOPE], kr_ref[...]], axis=-1)
        kh_ref[...] = norm_rope(kf, gk_ref[...]).astype(BF16)
        vh_ref[...] = jnp.concatenate([kvv[:, MLA_NOPE:], jnp.ones((tm, MLA_VD), F32)], axis=-1).astype(BF16)

    return pl.pallas_call(
        body, name="mla_prep", grid=(t // tm, MLA_HEADS),
        in_specs=[sp['head256'], sp['head256'], sp['kr'], sp['gain'], sp['gain'], sp['tab'], sp['tab'], sp['tab']],
        out_specs=[sp['head256'], sp['head256'], sp['head256']],
        out_shape=[_sds((MLA_HEADS, t, MLA_HD_PAD), BF16), _sds((MLA_HEADS, t, MLA_HD_PAD), BF16),
                   _sds((MLA_HEADS, t, 2 * MLA_VD), BF16)],
        compiler_params=_cparams(("parallel", "arbitrary")),
    )(q, kv, proj2, gq, gk, *tabs)


def _mla_prep_bwd(q, kv, proj2, gq, gk, tabs, dqt, dkh, dvh):
    t = q.shape[1]
    tm = _row_tile(t)
    ab = dqt.shape[-1]
    sp = _mla_prep_specs(t, tm)

    def body(q_ref, kv_ref, kr_ref, gq_ref, gk_ref, c_ref, s1_ref, s2_ref, dqt_ref, dkh_ref, dvh_ref,
             dq_ref, dkv_ref, dkr_ref, dgq_ref, dgk_ref):
        dqh = jnp.concatenate([dqt_ref[b].T for b in range(tm // ab)], axis=0)
        i, h = pl.program_id(0), pl.program_id(1)

        @pl.when((i == 0) & (h == 0))
        def _():
            dgq_ref[...] = jnp.zeros_like(dgq_ref)
            dgk_ref[...] = jnp.zeros_like(dgk_ref)

        @pl.when(h == 0)
        def _():
            dkr_ref[...] = jnp.zeros_like(dkr_ref)

        c, s1, s2 = c_ref[...], s1_ref[...], s2_ref[...]

        def back(xv, gain, dout):
            dy = jnp.concatenate([dout[:, :MLA_NOPE], _rope_tile(dout[:, MLA_NOPE:], c, -s1, -s2)], axis=-1)
            return _rms_bwd_rows(dy, xv, gain, MLA_QKD)

        kvv = kv_ref[...]
        dxq, dgq = back(q_ref[...], gq_ref[...], dqh)
        kf = jnp.concatenate([kvv[:, :MLA_NOPE], kr_ref[...]], axis=-1)
        dxk, dgk = back(kf, gk_ref[...], dkh_ref[...])
        dq_ref[...] = dxq.astype(BF16)
        dkv_ref[...] = jnp.concatenate([dxk[:, :MLA_NOPE], dvh_ref[...]], axis=-1).astype(BF16)
        dkr_ref[...] += dxk[:, MLA_NOPE:]
        dgq_ref[...] += jnp.sum(dgq, axis=0, keepdims=True)
        dgk_ref[...] += jnp.sum(dgk, axis=0, keepdims=True)

    return pl.pallas_call(
        body, name="mla_prep_bwd", grid=(t // tm, MLA_HEADS),
        in_specs=[sp['head256'], sp['head256'], sp['kr'], sp['gain'], sp['gain'], sp['tab'], sp['tab'], sp['tab'],
                  pl.BlockSpec((None, tm // ab, MLA_HD_PAD, ab), lambda i, h: (h, i, 0, 0)),
                  sp['head256'], sp['head128']],
        out_specs=[sp['head256'], sp['head256'], sp['tab'], sp['gain'], sp['gain']],
        out_shape=[_sds((MLA_HEADS, t, MLA_HD_PAD), BF16), _sds((MLA_HEADS, t, MLA_HD_PAD), BF16),
                   _sds((t, 128), F32), _sds((1, MLA_HD_PAD), F32), _sds((1, MLA_HD_PAD), F32)],
        compiler_params=_cparams(("arbitrary", "arbitrary")),
    )(q, kv, proj2, gq, gk, *tabs, dqt, dkh, dvh)


def _chunk_visible(rows, cols, row_off, col_off):
    rq = lax.shift_right_logical(lax.broadcasted_iota(jnp.int32, (rows, cols), 0) + row_off, 6)
    ck = lax.shift_right_logical(lax.broadcasted_iota(jnp.int32, (rows, cols), 1) + col_off, 6)
    return ck <= rq


def _rows_to_lanes(col):
    return col.T[:8, :]


def _attn_fwd(qh, kh, vh):
    t = qh.shape[1]
    ab = min(ATT_BLOCK, t)
    tq = min(ATT_QROWS, t)
    r = tq // ab
    hg = ATT_HEADS

    def body(q_ref, k_ref, v_ref, o_ref, lse_ref):
        n_un = pl.program_id(1) * r

        def step(b, state, diag):
            rows = pl.ds(pl.multiple_of(b * ab, ab), ab)
            ms, accs = [], []
            for hh in range(hg):
                m, acc = state[0][hh], state[1][hh]
                s = _dot(q_ref[hh], k_ref[hh, rows, :], 1, 1)
                if diag is not None:
                    s = jnp.where(_chunk_visible(tq, ab, 0, diag * ab), s, -1e30)
                m_new = jnp.maximum(m, jnp.max(s, axis=-1, keepdims=True))
                p = jnp.exp2((s - m_new) * ATT_EXP2).astype(BF16)
                accs.append(jnp.exp2((m - m_new) * ATT_EXP2) * acc + _dot(p, v_ref[hh, rows, :], 1, 0))
                ms.append(m_new)
            return tuple(ms), tuple(accs)

        heads = lambda v: tuple(v for _ in range(hg))
        state = (heads(jnp.full((tq, 1), -1e30, F32)), heads(jnp.zeros((tq, 2 * MLA_VD), F32)))
        state = lax.fori_loop(0, n_un, lambda b, st: step(b, st, None), state)
        for d in range(r):
            state = step(n_un + d, state, d)
        ms, accs = state
        for hh in range(hg):
            l = accs[hh][:, MLA_VD:]
            o_ref[:, hh * MLA_VD:(hh + 1) * MLA_VD] = accs[hh][:, :MLA_VD] / l
            lse_t = _rows_to_lanes(ms[hh] * ATT_EXP2 + jnp.log(l) * LOG2E)
            for d in range(r):
                lse_ref[hh, d] = lse_t[:, d * ab:(d + 1) * ab]

    return pl.pallas_call(
        body, name="mla_attn", grid=(MLA_HEADS // hg, t // tq),
        in_specs=[pl.BlockSpec((hg, tq, MLA_HD_PAD), lambda g, i: (g, i, 0)),
                  pl.BlockSpec((hg, t, MLA_HD_PAD), lambda g, i: (g, 0, 0)),
                  pl.BlockSpec((hg, t, 2 * MLA_VD), lambda g, i: (g, 0, 0))],
        out_specs=[pl.BlockSpec((tq, hg * MLA_VD), lambda g, i: (i, g)),
                   pl.BlockSpec((hg, r, 8, ab), lambda g, i: (g, i, 0, 0))],
        out_shape=[_sds((t, MLA_HEADS * MLA_VD), F32), _sds((MLA_HEADS, t // ab, 8, ab), F32)],
        compiler_params=_cparams(("parallel", "arbitrary")),
    )(qh, kh, vh)


def _attn_delta(do, o, ab):
    t = do.shape[0]
    tm = _row_tile(t)

    def body(do_ref, o_ref, d_ref):
        d = jnp.sum(do_ref[...] * o_ref[...], axis=-1, keepdims=True)
        d_t = _rows_to_lanes(jnp.broadcast_to(d, (tm, 128)))
        for b in range(tm // ab):
            d_ref[b] = d_t[:, b * ab:(b + 1) * ab]

    col = pl.BlockSpec((tm, MLA_VD), lambda i, h: (i, h))
    return pl.pallas_call(
        body, name="mla_delta", grid=(t // tm, MLA_HEADS), in_specs=[col, col],
        out_specs=pl.BlockSpec((None, tm // ab, 8, ab), lambda i, h: (h, i, 0, 0)),
        out_shape=_sds((MLA_HEADS, t // ab, 8, ab), F32),
        compiler_params=_cparams(("parallel", "parallel")),
    )(do, o)


def _attn_bwd(qh, kh, vh, dob, lse_t, dl_t):
    t = qh.shape[1]
    ab = min(ATT_BLOCK, t)
    nq = t // ab
    hg = ATT_HEADS

    def body(q_ref, k_ref, v_ref, do_ref, lse_ref, dl_ref, dqt_ref, dk_ref, dv_ref):
        j = pl.program_id(1)

        @pl.when(j == 0)
        def _():
            dqt_ref[...] = jnp.zeros_like(dqt_ref)

        ks = [k_ref[hh] for hh in range(hg)]
        vs = [v_ref[hh, :, :MLA_VD] for hh in range(hg)]
        kts = [k.T for k in ks]

        def step(b, grads, masked):
            rows = pl.ds(pl.multiple_of(b * ab, ab), ab)
            out = []
            for hh in range(hg):
                dk, dv = grads[hh]
                q = q_ref[hh, rows, :]
                do = do_ref[rows, hh * MLA_VD:(hh + 1) * MLA_VD]
                s_t = _dot(ks[hh], q, 1, 1)
                if masked:
                    key_chunk = lax.shift_right_logical(lax.broadcasted_iota(jnp.int32, (ab, ab), 0), 6)
                    query_chunk = lax.shift_right_logical(lax.broadcasted_iota(jnp.int32, (ab, ab), 1), 6)
                    s_t = jnp.where(key_chunk <= query_chunk, s_t, -1e30)
                p_t = jnp.exp2(s_t * ATT_EXP2 - lse_ref[hh, b][0:1, :])
                dp_t = _dot(vs[hh], do, 1, 1)
                ds_t = (p_t * (dp_t - dl_ref[hh, b][0:1, :]) * ATT_SCALE).astype(BF16)
                dqt_ref[hh, b] += _dot(kts[hh], ds_t, 1, 0)
                out.append((dk + _dot(ds_t, q, 1, 0), dv + _dot(p_t.astype(BF16), do, 1, 0)))
            return tuple(out)

        grads = tuple((jnp.zeros((ab, MLA_HD_PAD), F32), jnp.zeros((ab, MLA_VD), F32)) for _ in range(hg))
        grads = step(j, grads, True)
        grads = lax.fori_loop(j + 1, nq, lambda b, g: step(b, g, False), grads)
        for hh in range(hg):
            dk_ref[hh] = grads[hh][0]
            dv_ref[hh] = grads[hh][1]

    whole = lambda w: pl.BlockSpec((hg, t, w), lambda g, j: (g, 0, 0))
    blk = lambda w: pl.BlockSpec((hg, ab, w), lambda g, j: (g, j, 0))
    stat = pl.BlockSpec((hg, nq, 8, ab), lambda g, j: (g, 0, 0, 0))
    return pl.pallas_call(
        body, name="mla_attn_bwd", grid=(MLA_HEADS // hg, nq),
        in_specs=[whole(MLA_HD_PAD), blk(MLA_HD_PAD), blk(2 * MLA_VD),
                  pl.BlockSpec((t, hg * MLA_VD), lambda g, j: (0, g)), stat, stat],
        out_specs=[pl.BlockSpec((hg, nq, MLA_HD_PAD, ab), lambda g, j: (g, 0, 0, 0)), blk(MLA_HD_PAD), blk(MLA_VD)],
        out_shape=[_sds((MLA_HEADS, nq, MLA_HD_PAD, ab), F32), _sds((MLA_HEADS, t, MLA_HD_PAD), F32),
                   _sds((MLA_HEADS, t, MLA_VD), F32)],
        compiler_params=_cparams(("parallel", "arbitrary")),
    )(qh, kh, vh, dob, lse_t, dl_t)


def _mlp_fwd(l, h, norm_g, w1g, w2g):
    t = h.shape[0]
    tm = _row_tile(t)
    nsh, _, _, wsh = w1g.shape
    hn = _rms_fwd(f"mlp_norm{l}", h, norm_g)

    def relu2(acc):
        r = jnp.maximum(acc, 0.0)
        return r, r * r

    tile = pl.BlockSpec((tm, wsh), lambda i, j, k: (i, j))
    r, u = _mm(f"mlp_up{l}", (t // tm, nsh, 1),
               hn, pl.BlockSpec((tm, D_MODEL), lambda i, j, k: (i, 0)),
               w1g, pl.BlockSpec((None, None, D_MODEL, wsh), lambda i, j, k: (j, l, 0, 0)), (1, 0),
               [(_sds((t, D_FF), BF16), tile), (_sds((t, D_FF), BF16), tile)], epi=relu2)
    row = pl.BlockSpec((tm, D_MODEL), lambda i, j, k: (i, 0))
    (h2,) = _mm(f"mlp_down{l}", (t // tm, 1, nsh),
                u, pl.BlockSpec((tm, wsh), lambda i, j, k: (i, k)),
                w2g, pl.BlockSpec((None, None, wsh, D_MODEL), lambda i, j, k: (k, l, 0, 0)), (1, 0),
                [(_sds((t, D_MODEL), F32), row)], extras=[(h, row)], epi=lambda acc, hv: (acc + hv,))
    return h2, (h, hn, r, u)


def _mlp_bwd(l, dh, saved, norm_g, w1g, w2g):
    h, hn, r, u = saved
    t = h.shape[0]
    tm = _row_tile(t)
    tk = _row_tile(t)
    nsh, _, _, wsh = w1g.shape
    tile = pl.BlockSpec((tm, wsh), lambda i, j, k: (i, j))
    (da,) = _mm(f"mlp_du{l}", (t // tm, nsh, 1),
                dh, pl.BlockSpec((tm, D_MODEL), lambda i, j, k: (i, 0)),
                w2g, pl.BlockSpec((None, None, wsh, D_MODEL), lambda i, j, k: (j, l, 0, 0)), (1, 1),
                [(_sds((t, D_FF), BF16), tile)], extras=[(r, tile)],
                epi=lambda acc, rv: (2.0 * rv.astype(F32) * acc,))
    (dw2,) = _mm(f"mlp_dw2{l}", (nsh, 1, t // tk),
                 u, pl.BlockSpec((tk, wsh), lambda i, j, k: (k, i)),
                 dh, pl.BlockSpec((tk, D_MODEL), lambda i, j, k: (k, 0)), (0, 0),
                 [(_sds((nsh, wsh, D_MODEL), BF16), pl.BlockSpec((None, wsh, D_MODEL), lambda i, j, k: (i, 0, 0)))])
    (dw1,) = _mm(f"mlp_dw1{l}", (1, nsh, t // tk),
                 hn, pl.BlockSpec((tk, D_MODEL), lambda i, j, k: (k, 0)),
                 da, pl.BlockSpec((tk, wsh), lambda i, j, k: (k, j)), (0, 0),
                 [(_sds((nsh, D_MODEL, wsh), BF16), pl.BlockSpec((None, D_MODEL, wsh), lambda i, j, k: (j, 0, 0)))])
    (dhn,) = _mm(f"mlp_dhn{l}", (t // tm, 1, nsh),
                 da, pl.BlockSpec((tm, wsh), lambda i, j, k: (i, k)),
                 w1g, pl.BlockSpec((None, None, D_MODEL, wsh), lambda i, j, k: (k, l, 0, 0)), (1, 1),
                 [(_sds((t, D_MODEL), F32), pl.BlockSpec((tm, D_MODEL), lambda i, j, k: (i, 0)))])
    dh_in, dg = _rms_bwd(f"mlp_norm_bwd{l}", dhn, h, norm_g, dh)
    return dh_in, dg, dw1, dw2


def _ple_fwd(l, h, p, norm_g, wg, wp):
    t = h.shape[0]
    tm = _row_tile(t, 512)
    hn = _rms_fwd(f"ple_norm{l}", h, norm_g)
    row = pl.BlockSpec((tm, D_MODEL), lambda i, j, k: (i, 0))
    full = lambda r: pl.BlockSpec((r, D_MODEL), lambda i, j, k: (0, 0))
    (e,) = _mm(f"ple_proj{l}", (t // tm, 1, 1),
               p, pl.BlockSpec((None, None, tm, PLE_DIM), lambda i, j, k: (l, 0, i, 0)),
               wp, full(PLE_DIM), (1, 0), [(_sds((t, D_MODEL), F32), row)])

    def gate_epi(acc, hv, ev):
        gt = _sigmoid(acc)
        return hv + gt * ev, gt

    h_out, gate = _mm(f"ple_gate{l}", (t // tm, 1, 1), hn, row, wg, full(D_MODEL), (1, 0),
                      [(_sds((t, D_MODEL), F32), row), (_sds((t, D_MODEL), F32), row)],
                      extras=[(h, row), (e, row)], epi=gate_epi)
    return h_out, (h, hn, gate, e)


def _ple_bwd(l, dh, saved, p, norm_g, wg, deps=()):
    h, hn, gate, e = saved
    t = h.shape[0]
    tm = _row_tile(t)
    tk = _row_tile(t, 512)
    de, dz = _ple_gate_bwd(f"ple_gate_bwd{l}", dh, gate, e)
    full = lambda r: pl.BlockSpec((r, D_MODEL), lambda i, j, k: (0, 0))
    rowk = pl.BlockSpec((tk, D_MODEL), lambda i, j, k: (k, 0))
    (dwp,) = _mm(f"ple_dwp{l}", (1, 1, t // tk),
                 p, pl.BlockSpec((None, None, tk, PLE_DIM), lambda i, j, k: (l, 0, k, 0)),
                 de, rowk, (0, 0), [(_sds((PLE_DIM, D_MODEL), BF16), full(PLE_DIM))], deps=deps)
    (dwg,) = _mm(f"ple_dwg{l}", (1, 1, t // tk), hn, rowk, dz, rowk, (0, 0),
                 [(_sds((D_MODEL, D_MODEL), BF16), full(D_MODEL))])
    row = pl.BlockSpec((tm, D_MODEL), lambda i, j, k: (i, 0))
    (dhn,) = _mm(f"ple_dhn{l}", (t // tm, 1, 1), dz, row, wg, full(D_MODEL), (1, 1),
                 [(_sds((t, D_MODEL), F32), row)])
    dh_in, dg = _rms_bwd(f"ple_norm_bwd{l}", dhn, h, norm_g, dh)
    return dh_in, dg, dwg, dwp


def _ret_layer_fwd(x, norm_g, wri, wro, gn, cos, sin):
    t = x.shape[0]
    tm = _row_tile(t)
    nsh, _, wsh = wri.shape
    hn = _rms_fwd("mix_norm0", x, norm_g)
    (proj,) = _mm("ret_in", (t // tm, nsh, 1),
                  hn, pl.BlockSpec((tm, D_MODEL), lambda i, j, k: (i, 0)),
                  wri, pl.BlockSpec((None, D_MODEL, wsh), lambda i, j, k: (j, 0, 0)), (1, 0),
                  [(_sds((t, RET_IN), F32), pl.BlockSpec((tm, wsh), lambda i, j, k: (i, j)))])
    gated, outp, states = _ret_fwd(proj, cos, sin, gn)
    row = pl.BlockSpec((tm, D_MODEL), lambda i, j, k: (i, 0))
    kt = 512
    (h1,) = _mm("ret_out", (t // tm, 1, RET_V_W // kt),
                gated, pl.BlockSpec((tm, kt), lambda i, j, k: (i, k)),
                wro, pl.BlockSpec((kt, D_MODEL), lambda i, j, k: (k, 0)), (1, 0),
                [(_sds((t, D_MODEL), F32), row)], extras=[(x, row)], epi=lambda acc, xv: (acc + xv,))
    return h1, (x, hn, proj, gated, outp, states)


def _ret_layer_bwd(dh, saved, norm_g, wri, wro, gn, cos, sin, emit, deps=()):
    x, hn, proj, gated, outp, states = saved
    t = x.shape[0]
    tm = _row_tile(t)
    tk = _row_tile(t, 512)
    nsh, _, wsh = wri.shape
    (dgated,) = _mm("ret_dgated", (t // tm, RET_V_W // D_MODEL, 1),
                    dh, pl.BlockSpec((tm, D_MODEL), lambda i, j, k: (i, 0)),
                    wro, pl.BlockSpec((D_MODEL, D_MODEL), lambda i, j, k: (j, 0)), (1, 1),
                    [(_sds((t, RET_V_W), F32), pl.BlockSpec((tm, D_MODEL), lambda i, j, k: (i, j)))], deps=deps)
    kt = 512
    (dwro,) = _mm("ret_dwro", (RET_V_W // kt, 1, t // tk),
                  gated, pl.BlockSpec((tk, kt), lambda i, j, k: (k, i)),
                  dh, pl.BlockSpec((tk, D_MODEL), lambda i, j, k: (k, 0)), (0, 0),
                  [(_sds((RET_V_W, D_MODEL), BF16), pl.BlockSpec((kt, D_MODEL), lambda i, j, k: (i, 0)))])
    dproj, dgn = _ret_bwd(proj, cos, sin, gn, outp, states, dgated)
    (dwri,) = _mm("ret_dwri", (1, nsh, t // tk),
                  hn, pl.BlockSpec((tk, D_MODEL), lambda i, j, k: (k, 0)),
                  dproj, pl.BlockSpec((tk, wsh), lambda i, j, k: (k, j)), (0, 0),
                  [(_sds((nsh, D_MODEL, wsh), BF16), pl.BlockSpec((None, D_MODEL, wsh), lambda i, j, k: (j, 0, 0)))])
    deps = emit(dwro, dwri)
    (dhn,) = _mm("ret_dhn", (t // tm, 1, nsh),
                 dproj, pl.BlockSpec((tm, wsh), lambda i, j, k: (i, k)),
                 wri, pl.BlockSpec((None, D_MODEL, wsh), lambda i, j, k: (k, 0, 0)), (1, 1),
                 [(_sds((t, D_MODEL), F32), pl.BlockSpec((tm, D_MODEL), lambda i, j, k: (i, 0)))], deps=deps)
    dx, dg = _rms_bwd("mix_norm_bwd0", dhn, x, norm_g, dh)
    return dx, dg, dgn.reshape(RET_HEADS, RET_DV)


def _mla_layer_fwd(h, norm_g, wmi, qa, kva, wuq, wukv, gq, gk, wmo, tabs):
    t = h.shape[0]
    tm = _row_tile(t)
    hn = _rms_fwd("mix_norm1", h, norm_g)
    row = pl.BlockSpec((tm, D_MODEL), lambda i, j, k: (i, 0))
    (proj2,) = _mm("mla_in", (t // tm, 1, 1), hn, row,
                   wmi, pl.BlockSpec((D_MODEL, MLA_IN_PAD), lambda i, j, k: (0, 0)), (1, 0),
                   [(_sds((t, MLA_IN_PAD), F32), pl.BlockSpec((tm, MLA_IN_PAD), lambda i, j, k: (i, 0)))])
    cq, ckv = _mla_mid(proj2, qa, kva)
    head = pl.BlockSpec((None, tm, MLA_HD_PAD), lambda i, j, k: (j, i, 0))
    (q,) = _mm("mla_uq", (t // tm, MLA_HEADS, 1),
               cq, pl.BlockSpec((tm, MLA_Q_RANK), lambda i, j, k: (i, 0)),
               wuq, pl.BlockSpec((None, MLA_Q_RANK, MLA_HD_PAD), lambda i, j, k: (j, 0, 0)), (1, 0),
               [(_sds((MLA_HEADS, t, MLA_HD_PAD), F32), head)])
    (kv,) = _mm("mla_ukv", (t // tm, MLA_HEADS, 1),
                ckv, pl.BlockSpec((tm, MLA_KV_RANK), lambda i, j, k: (i, 0)),
                wukv, pl.BlockSpec((None, MLA_KV_RANK, MLA_HD_PAD), lambda i, j, k: (j, 0, 0)), (1, 0),
                [(_sds((MLA_HEADS, t, MLA_HD_PAD), F32), head)])
    qh, kh, vh = _mla_prep(q, kv, proj2, gq, gk, tabs)
    o, lse = _attn_fwd(qh, kh, vh)
    (h_out,) = _mm("mla_out", (t // tm, 1, 1), o, row,
                   wmo, pl.BlockSpec((D_MODEL, D_MODEL), lambda i, j, k: (0, 0)), (1, 0),
                   [(_sds((t, D_MODEL), F32), row)], extras=[(h, row)], epi=lambda acc, hv: (acc + hv,))
    return h_out, (h, hn, proj2, cq, ckv, q, kv, qh, kh, vh, o, lse)


def _mla_layer_bwd(dh, saved, norm_g, wmi, qa, kva, wuq, wukv, gq, gk, wmo, tabs, deps=()):
    h, hn, proj2, cq, ckv, q, kv, qh, kh, vh, o, lse = saved
    t = h.shape[0]
    tm = _row_tile(t)
    tk = _row_tile(t, 512)
    row = pl.BlockSpec((tm, D_MODEL), lambda i, j, k: (i, 0))
    rowk = pl.BlockSpec((tk, D_MODEL), lambda i, j, k: (k, 0))
    sq = pl.BlockSpec((D_MODEL, D_MODEL), lambda i, j, k: (0, 0))
    do, dob = _mm("mla_do", (t // tm, 1, 1), dh, row, wmo, sq, (1, 1),
                  [(_sds((t, D_MODEL), F32), row), (_sds((t, D_MODEL), BF16), row)], epi=lambda acc: (acc, acc),
                  deps=deps)
    (dwmo,) = _mm("mla_dwo", (1, 1, t // tk), o, rowk, dh, rowk, (0, 0), [(_sds((D_MODEL, D_MODEL), BF16), sq)])
    delta = _attn_delta(do, o, lse.shape[-1])
    dqt, dkh, dvh = _attn_bwd(qh, kh, vh, dob, lse, delta)
    dq, dkv, dkr, dgq, dgk = _mla_prep_bwd(q, kv, proj2, gq, gk, tabs, dqt, dkh, dvh)

    headk = pl.BlockSpec((None, tk, MLA_HD_PAD), lambda i, j, k: (j, k, 0))
    (dwuq,) = _mm("mla_dwuq", (1, MLA_HEADS, t // tk),
                  cq, pl.BlockSpec((tk, MLA_Q_RANK), lambda i, j, k: (k, 0)), dq, headk, (0, 0),
                  [(_sds((MLA_HEADS, MLA_Q_RANK, MLA_HD_PAD), BF16),
                    pl.BlockSpec((None, MLA_Q_RANK, MLA_HD_PAD), lambda i, j, k: (j, 0, 0)))])
    (dwukv,) = _mm("mla_dwukv", (1, MLA_HEADS, t // tk),
                   ckv, pl.BlockSpec((tk, MLA_KV_RANK), lambda i, j, k: (k, 0)), dkv, headk, (0, 0),
                   [(_sds((MLA_HEADS, MLA_KV_RANK, MLA_HD_PAD), BF16),
                     pl.BlockSpec((None, MLA_KV_RANK, MLA_HD_PAD), lambda i, j, k: (j, 0, 0)))])
    headi = pl.BlockSpec((None, tm, MLA_HD_PAD), lambda i, j, k: (k, i, 0))
    (dcq,) = _mm("mla_dcq", (t // tm, 1, MLA_HEADS), dq, headi,
                 wuq, pl.BlockSpec((None, MLA_Q_RANK, MLA_HD_PAD), lambda i, j, k: (k, 0, 0)), (1, 1),
                 [(_sds((t, MLA_Q_RANK), F32), pl.BlockSpec((tm, MLA_Q_RANK), lambda i, j, k: (i, 0)))])
    (dckv,) = _mm("mla_dckv", (t // tm, 1, MLA_HEADS), dkv, headi,
                  wukv, pl.BlockSpec((None, MLA_KV_RANK, MLA_HD_PAD), lambda i, j, k: (k, 0, 0)), (1, 1),
                  [(_sds((t, MLA_KV_RANK), F32), pl.BlockSpec((tm, MLA_KV_RANK), lambda i, j, k: (i, 0)))])
    dproj2, dqa, dkva = _mla_mid_bwd(proj2, qa, kva, dcq, dckv, dkr)
    win = pl.BlockSpec((D_MODEL, MLA_IN_PAD), lambda i, j, k: (0, 0))
    (dwmi,) = _mm("mla_dwin", (1, 1, t // tk), hn, rowk,
                  dproj2, pl.BlockSpec((tk, MLA_IN_PAD), lambda i, j, k: (k, 0)), (0, 0),
                  [(_sds((D_MODEL, MLA_IN_PAD), BF16), win)])
    (dhn,) = _mm("mla_dhn", (t // tm, 1, 1),
                 dproj2, pl.BlockSpec((tm, MLA_IN_PAD), lambda i, j, k: (i, 0)), wmi, win, (1, 1),
                 [(_sds((t, D_MODEL), F32), row)])
    dh_in, dg = _rms_bwd("mix_norm_bwd1", dhn, h, norm_g, dh)
    return dh_in, dict(mix=dg, wmi=dwmi, qa=dqa, kva=dkva, wuq=dwuq, wukv=dwukv, gq=dgq, gk=dgk, wmo=dwmo)


def _local_step(x, p, target, w, emit=lambda group: ()):
    t = x.shape[0]
    inv = 1.0 / (ROPE_THETA ** (jnp.arange(0, RET_DK, 2, dtype=F32) / RET_DK))
    ang = jnp.arange(t, dtype=F32)[:, None] * inv[None, :]
    cos_r, sin_r = jnp.cos(ang), jnp.sin(ang)
    tabs = _mla_tables(t)
    row = lambda a, i: a[i:i + 1]

    h1, s_ret = _ret_layer_fwd(x, row(w['mix_norm'], 0), w['ret_w_in'], w['ret_w_out'], w['ret_gn'], cos_r, sin_r)
    h2, s_mlp0 = _mlp_fwd(0, h1, row(w['mlp_norm'], 0), w['mlp_w1'], w['mlp_w2'])
    h3, s_ple0 = _ple_fwd(0, h2, p, row(w['ple_norm'], 0), w['ple_gate_w'][0], w['ple_proj_w'][0])
    mla_w = (w['mla_w_in'], w['mla_q_a_norm'], w['mla_kv_a_norm'], w['mla_w_uq'], w['mla_w_ukv'],
             w['mla_q_norm'], w['mla_k_norm'], w['mla_w_out'], tabs)
    h4, s_mla = _mla_layer_fwd(h3, row(w['mix_norm'], 1), *mla_w)
    h5, s_mlp1 = _mlp_fwd(1, h4, row(w['mlp_norm'], 1), w['mlp_w1'], w['mlp_w2'])
    y, s_ple1 = _ple_fwd(1, h5, p, row(w['ple_norm'], 1), w['ple_gate_w'][1], w['ple_proj_w'][1])

    dy, sq_err = _loss_head(y, target)

    n = N_DEV
    colsh = lambda a: a.reshape(a.shape[0], n, a.shape[1] // n).transpose(1, 0, 2)
    rowsh = lambda a: a.reshape(n, a.shape[0] // n, a.shape[1])
    big = {}

    def emit_group(group):
        big.update(group)
        return emit(group)

    dh5, dg_ple1, dwg1, dwp1 = _ple_bwd(1, dy, s_ple1, p, row(w['ple_norm'], 1), w['ple_gate_w'][1])
    dh4, dg_mlp1, dw1_1, dw2_1 = _mlp_bwd(1, dh5, s_mlp1, row(w['mlp_norm'], 1), w['mlp_w1'], w['mlp_w2'])
    deps = emit_group({('ple_gate_w', 1): rowsh(dwg1), ('ple_proj_w', 1): colsh(dwp1),
                       ('mlp_w2', 1): dw2_1, ('mlp_w1', 1): dw1_1})
    dh3, gm = _mla_layer_bwd(dh4, s_mla, row(w['mix_norm'], 1), *mla_w, deps=deps)
    deps = emit_group({('mla_w_out', 0): rowsh(gm['wmo']), ('mla_w_uq', 0): gm['wuq'][:, :, :MLA_QKD],
                       ('mla_w_ukv', 0): gm['wukv'], ('mla_w_in', 0): rowsh(gm['wmi'][:, :MLA_IN])})
    dh2, dg_ple0, dwg0, dwp0 = _ple_bwd(0, dh3, s_ple0, p, row(w['ple_norm'], 0), w['ple_gate_w'][0], deps=deps)
    dh1, dg_mlp0, dw1_0, dw2_0 = _mlp_bwd(0, dh2, s_mlp0, row(w['mlp_norm'], 0), w['mlp_w1'], w['mlp_w2'])
    deps = emit_group({('ple_gate_w', 0): rowsh(dwg0), ('ple_proj_w', 0): colsh(dwp0),
                       ('mlp_w2', 0): dw2_0, ('mlp_w1', 0): dw1_0})
    dx, dg_mix0, dgn = _ret_layer_bwd(
        dh1, s_ret, row(w['mix_norm'], 0), w['ret_w_in'], w['ret_w_out'], w['ret_gn'], cos_r, sin_r,
        lambda dwro, dwri: emit_group({('ret_w_out', 0): rowsh(dwro), ('ret_w_in', 0): dwri}), deps=deps)

    small = dict(
        mix_norm=[dg_mix0, gm['mix']], mlp_norm=[dg_mlp0, dg_mlp1], ple_norm=[dg_ple0, dg_ple1],
        ret_gn=dgn, mla_q_a_norm=gm['qa'], mla_kv_a_norm=gm['kva'], mla_q_norm=gm['gq'], mla_k_norm=gm['gk'],
    )
    return sq_err, dx, big, small


def _my_place():
    x, y, c = lax.axis_index("x"), lax.axis_index("y"), lax.axis_index("c")
    return x, y, c


def _flat(px, py, pc):
    return 4 * px + 2 * py + pc


def _peer(x, y, c, r):
    return (1 - x if r & 4 else x, 1 - y if r & 2 else y, 1 - c if r & 1 else c)


def _all_gather(arrays):
    n = len(arrays)

    def body(*refs):
        ins, outs = refs[:n], refs[n:2 * n]
        send_sems, recv_sems, local_sems = refs[2 * n:]
        x, y, c = _my_place()
        me, sibling = (x, y, c), (x, y, 1 - c)
        chips = [(1 - x, y), (x, 1 - y), (1 - x, 1 - y)]

        def copy(a, k, block, to, src=None):
            slot = outs[a].at[_flat(*block)]
            return pltpu.make_async_remote_copy(
                src_ref=slot if src is None else src, dst_ref=slot,
                send_sem=send_sems.at[a, k], recv_sem=recv_sems.at[a, k], device_id=to, device_id_type=MESH)

        mine = [pltpu.make_async_copy(ins[a], outs[a].at[_flat(*me)], local_sems.at[a]) for a in range(n)]
        for cp in mine:
            cp.start()
        first = []
        for a in range(n):
            first.append(copy(a, 0, me, sibling, src=ins[a]))
            first += [copy(a, 1 + j, me, (*chip, c), src=ins[a]) for j, chip in enumerate(chips)]
        for cp in first:
            cp.start()
        passed = []
        for a in range(n):
            for j, chip in enumerate(chips):
                copy(a, 1 + j, (*chip, c), me).wait_recv()
                passed.append(copy(a, 4 + j, (*chip, c), sibling))
                passed[-1].start()
        for a in range(n):
            copy(a, 0, sibling, me).wait_recv()
            for j, chip in enumerate(chips):
                copy(a, 4 + j, (*chip, 1 - c), me).wait_recv()
        for cp in first + passed:
            cp.wait_send()
        for cp in mine:
            cp.wait()

    return pl.pallas_call(
        body, name="all_gather_weights",
        in_specs=[ANY] * n, out_specs=[ANY] * n,
        out_shape=[_sds((N_DEV,) + a.shape, a.dtype) for a in arrays],
        scratch_shapes=[pltpu.SemaphoreType.DMA((n, 7)), pltpu.SemaphoreType.DMA((n, 7)),
                        pltpu.SemaphoreType.DMA((n,))],
    )(*arrays)


HBM = pl.BlockSpec(memory_space=pltpu.HBM)
SEMS = pl.BlockSpec(memory_space=pltpu.SEMAPHORE)
SIDE_EFFECT = pltpu.SideEffectType.DATAFLOW_SIDE_EFFECTING


def _rs_copies(x, y, c, srcs, lands, send_sems, recv_sems):
    copies = []
    for a in range(len(srcs)):
        for r in range(1, N_DEV):
            peer = _peer(x, y, c, r)
            k = a * (N_DEV - 1) + r - 1
            copies.append(pltpu.make_async_remote_copy(
                src_ref=srcs[a].at[_flat(*peer)], dst_ref=lands[a].at[r - 1],
                send_sem=send_sems.at[k], recv_sem=recv_sems.at[k], device_id=peer, device_id_type=MESH))
    return copies


def _rs_start(name, arrays):
    n = len(arrays)
    hbm = lambda a: pltpu.with_memory_space_constraint(a, pltpu.HBM)
    lands = [hbm(lax.empty((N_DEV - 1,) + a.shape[1:], a.dtype)) for a in arrays]

    def body(*refs):
        srcs, lnd = refs[:n], refs[n:2 * n]
        send_sems, recv_sems = refs[2 * n], refs[2 * n + 1]
        token = refs[-1]
        for cp in _rs_copies(*_my_place(), srcs, lnd, send_sems, recv_sems):
            cp.start()
        token[...] = jnp.zeros_like(token)

    outs = pl.pallas_call(
        body, name=name,
        in_specs=[HBM] * (2 * n),
        out_specs=[SEMS, SEMS] + [HBM] * (2 * n) + [pl.BlockSpec(memory_space=pltpu.VMEM)],
        out_shape=[pltpu.SemaphoreType.DMA((n * (N_DEV - 1),)), pltpu.SemaphoreType.DMA((n * (N_DEV - 1),))]
        + [pltpu.HBM(a.shape, a.dtype) for a in arrays] + [pltpu.HBM(l.shape, l.dtype) for l in lands]
        + [_sds((8, 128), F32)],
        input_output_aliases={i: 2 + i for i in range(2 * n)},
        compiler_params=pltpu.CompilerParams(has_side_effects=SIDE_EFFECT),
    )(*[hbm(a) for a in arrays], *lands)
    return outs[0], outs[1], outs[2:2 + n], outs[2 + n:2 + 2 * n], outs[-1]


def _rs_wait(name, send_sems, recv_sems, srcs, lands, after):
    n = len(srcs)

    def body(*refs):
        src_refs, lnd = refs[:n], refs[n:2 * n]
        send, recv = refs[2 * n], refs[2 * n + 1]
        for cp in _rs_copies(*_my_place(), src_refs, lnd, send, recv):
            cp.wait_send()
            cp.wait_recv()

    outs = pl.pallas_call(
        body, name=name,
        in_specs=[HBM] * (2 * n) + [SEMS, SEMS] + [ANY] * len(after),
        out_specs=[HBM] * (2 * n),
        out_shape=[pltpu.HBM(a.shape, a.dtype) for a in list(srcs) + list(lands)],
        input_output_aliases={i: i for i in range(2 * n)},
        compiler_params=pltpu.CompilerParams(has_side_effects=SIDE_EFFECT),
    )(*srcs, *lands, send_sems, recv_sems, *after)
    return outs[:n], outs[n:]


SMALL_PACK_ROWS = 16


def _all_reduce_small(rows):
    n = len(rows)

    def body(*refs):
        ins = refs[:n]
        out_ref, mine, buf, send_sems, recv_sems = refs[n:]
        x, y, c = _my_place()
        mine[...] = jnp.zeros_like(mine)
        for (r0, a), ref in zip(rows, ins):
            mine[r0:r0 + a.shape[0], 0:a.shape[1]] = ref[...]
        buf[_flat(x, y, c)] = mine[...]
        copies = []
        for r in range(1, N_DEV):
            peer = _peer(x, y, c, r)
            send = pltpu.make_async_remote_copy(
                src_ref=mine, dst_ref=buf.at[_flat(x, y, c)],
                send_sem=send_sems.at[r - 1], recv_sem=recv_sems.at[r - 1], device_id=peer, device_id_type=MESH)
            send.start()
            recv = pltpu.make_async_remote_copy(
                src_ref=mine, dst_ref=buf.at[_flat(*peer)],
                send_sem=send_sems.at[r - 1], recv_sem=recv_sems.at[r - 1], device_id=peer, device_id_type=MESH)
            copies.append((send, recv))
        for send, recv in copies:
            send.wait_send()
            recv.wait_recv()
        acc = buf[0]
        for s in range(1, N_DEV):
            acc = acc + buf[s]
        out_ref[...] = acc

    vm = pl.BlockSpec(memory_space=pltpu.VMEM)
    shape = (SMALL_PACK_ROWS, D_MODEL)
    return pl.pallas_call(
        body, name="all_reduce_small", in_specs=[vm] * n, out_specs=vm,
        out_shape=_sds(shape, F32),
        scratch_shapes=[pltpu.VMEM(shape, F32), pltpu.VMEM((N_DEV,) + shape, F32),
                        pltpu.SemaphoreType.DMA((7,)), pltpu.SemaphoreType.DMA((7,))],
    )(*[a for _, a in rows])


def _adamw_math(w, g, m, v):
    m = ADAM_B1 * m + (1.0 - ADAM_B1) * g
    v = ADAM_B2 * v + (1.0 - ADAM_B2) * (g * g)
    m_hat = m / (1.0 - ADAM_B1 ** ADAM_STEP)
    v_hat = v / (1.0 - ADAM_B2 ** ADAM_STEP)
    delta = -ADAM_LR * (m_hat / (jnp.sqrt(v_hat) + ADAM_EPS) + ADAM_WD * w)
    return delta, m, v


def _adamw_big(name, w, m, v, srcs, lands, me):
    nl, rows, cols = w.shape
    tr = next(cand for cand in (256, 128, 64, 32, 16, 8) if rows % cand == 0)

    def body(me_ref, w_ref, m_ref, v_ref, *rest):
        src_refs, land_refs = rest[:nl], rest[nl:2 * nl]
        g_ref, d_ref, mo_ref, vo_ref = rest[2 * nl:]
        for layer in range(nl):
            @pl.when(pl.program_id(0) == layer)
            def _():
                g = src_refs[layer][...].astype(F32)
                for s in range(N_DEV - 1):
                    g = g + land_refs[layer][s].astype(F32)
                delta, mn, vn = _adamw_math(w_ref[...], g, m_ref[...], v_ref[...])
                g_ref[...] = g
                d_ref[...] = delta
                mo_ref[...] = mn
                vo_ref[...] = vn

    blk = pl.BlockSpec((None, tr, cols), lambda l, i, me_ref: (l, i, 0))
    own = pl.BlockSpec((None, tr, cols), lambda l, i, me_ref: (me_ref[0], i, 0))
    peers = pl.BlockSpec((N_DEV - 1, tr, cols), lambda l, i, me_ref: (0, i, 0))
    return pl.pallas_call(
        body, name=name,
        grid_spec=pltpu.PrefetchScalarGridSpec(
            num_scalar_prefetch=1, grid=(nl, rows // tr),
            in_specs=[blk, blk, blk] + [own] * nl + [peers] * nl, out_specs=[blk] * 4),
        out_shape=[_sds((nl, rows, cols), F32)] * 4,
        compiler_params=_cparams(("arbitrary", "arbitrary")),
    )(me, w, m, v, *srcs, *lands)


def _adamw_small(ws, gs, ms, vs):
    n = len(ws)

    def body(*refs):
        w_refs, g_refs, m_refs, v_refs = (refs[i * n:(i + 1) * n] for i in range(4))
        d_out, m_out, v_out = (refs[(4 + i) * n:(5 + i) * n] for i in range(3))
        for i in range(n):
            delta, mn, vn = _adamw_math(w_refs[i][...], g_refs[i][...], m_refs[i][...], v_refs[i][...])
            d_out[i][...] = delta
            m_out[i][...] = mn
            v_out[i][...] = vn

    vm = pl.BlockSpec(memory_space=pltpu.VMEM)
    outs = pl.pallas_call(
        body, name="adamw_small", in_specs=[vm] * (4 * n), out_specs=[vm] * (3 * n),
        out_shape=[_sds(a.shape, F32) for a in ws] * 3,
    )(*ws, *gs, *ms, *vs)
    return outs[:n], outs[n:2 * n], outs[2 * n:]


SMALL_ROWS = 16


def _pad_to(a, rows, cols):
    return jnp.pad(a, ((0, rows - a.shape[0]), (0, cols - a.shape[1])))


def _prepare_weights(p):
    gn_pack = jnp.concatenate([
        _pad_to(p['ret_gn'][0], RET_HEADS, 128), _pad_to(p['mla_q_a_norm'], 1, 128),
        _pad_to(p['mla_kv_a_norm'], 1, 128), jnp.zeros((2, 128), F32)], axis=0)
    shards = [gn_pack,
              p['ret_w_in'][0].astype(BF16), p['ret_w_out'][0].astype(BF16),
              p['mlp_w1'].astype(BF16), p['mlp_w2'].astype(BF16),
              p['ple_gate_w'].astype(BF16), p['ple_proj_w'].astype(BF16),
              p['mla_w_in'][0].astype(BF16), p['mla_w_uq'][0].astype(BF16), p['mla_w_ukv'][0].astype(BF16),
              p['mla_w_out'][0].astype(BF16)]
    pack, wri, wro, w1, w2, wg, wp, wmi, wuq, wukv, wmo = _all_gather(shards)
    n = N_DEV
    w = {k: p[k] for k in ('mix_norm', 'mlp_norm', 'ple_norm')}
    w['ret_gn'] = pack[:, :RET_HEADS, :RET_DV // n].transpose(1, 0, 2).reshape(RET_HEADS, RET_DV)
    w['mla_q_a_norm'] = pack[:, RET_HEADS, :MLA_Q_RANK // n].reshape(1, MLA_Q_RANK)
    w['mla_kv_a_norm'] = pack[:, RET_HEADS + 1, :MLA_KV_RANK // n].reshape(1, MLA_KV_RANK)
    w['ret_w_in'] = wri
    w['ret_w_out'] = wro.reshape(RET_V_W, D_MODEL)
    w['mlp_w1'] = w1
    w['mlp_w2'] = w2
    w['ple_gate_w'] = [wg[:, l].reshape(D_MODEL, D_MODEL) for l in range(2)]
    w['ple_proj_w'] = [wp[:, l].transpose(1, 0, 2).reshape(PLE_DIM, D_MODEL) for l in range(2)]
    w['mla_w_in'] = jnp.pad(wmi.reshape(D_MODEL, MLA_IN), ((0, 0), (0, MLA_IN_PAD - MLA_IN)))
    w['mla_w_uq'] = jnp.pad(wuq, ((0, 0), (0, 0), (0, MLA_HD_PAD - MLA_QKD)))
    w['mla_w_ukv'] = wukv
    w['mla_q_norm'] = _pad_to(p['mla_q_norm'], 1, MLA_HD_PAD)
    w['mla_k_norm'] = _pad_to(p['mla_k_norm'], 1, MLA_HD_PAD)
    w['mla_w_out'] = wmo.reshape(D_MODEL, D_MODEL)
    return w


def _small_grads(small):
    rows = [(0, small['mix_norm'][0]), (1, small['mix_norm'][1]), (2, small['mlp_norm'][0]),
            (3, small['mlp_norm'][1]), (4, small['ple_norm'][0]), (5, small['ple_norm'][1]),
            (6, small['ret_gn']), (10, small['mla_q_a_norm']), (11, small['mla_kv_a_norm']),
            (12, small['mla_q_norm']), (13, small['mla_k_norm'])]
    gs = _all_reduce_small(rows)
    me = _flat(*_my_place())
    n = N_DEV
    return dict(
        mix_norm=gs[0:2], mlp_norm=gs[2:4], ple_norm=gs[4:6],
        ret_gn=lax.dynamic_slice(gs, (6, me * (RET_DV // n)), (RET_HEADS, RET_DV // n)),
        mla_q_a_norm=lax.dynamic_slice(gs, (10, me * (MLA_Q_RANK // n)), (1, MLA_Q_RANK // n)),
        mla_kv_a_norm=lax.dynamic_slice(gs, (11, me * (MLA_KV_RANK // n)), (1, MLA_KV_RANK // n)),
        mla_q_norm=gs[12:13, :MLA_QKD], mla_k_norm=gs[13:14, :MLA_QKD])


def kernel(x, p, mix_norm, ret_w_in, ret_gn, ret_w_out, mla_w_in, mla_q_a_norm, mla_kv_a_norm, mla_w_uq, mla_w_ukv, mla_q_norm, mla_k_norm, mla_w_out, mlp_norm, mlp_w1, mlp_w2, ple_norm, ple_gate_w, ple_proj_w, loss_target, m_mix_norm, m_ret_w_in, m_ret_gn, m_ret_w_out, m_mla_w_in, m_mla_q_a_norm, m_mla_kv_a_norm, m_mla_w_uq, m_mla_w_ukv, m_mla_q_norm, m_mla_k_norm, m_mla_w_out, m_mlp_norm, m_mlp_w1, m_mlp_w2, m_ple_norm, m_ple_gate_w, m_ple_proj_w, v_mix_norm, v_ret_w_in, v_ret_gn, v_ret_w_out, v_mla_w_in, v_mla_q_a_norm, v_mla_kv_a_norm, v_mla_w_uq, v_mla_w_ukv, v_mla_q_norm, v_mla_k_norm, v_mla_w_out, v_mlp_norm, v_mlp_w1, v_mlp_w2, v_ple_norm, v_ple_gate_w, v_ple_proj_w):
    given = dict(locals())
    params = {n: given[n] for n in WEIGHTS}
    w = _prepare_weights(params)

    started = []

    def emit(group):
        keys = list(group)
        send, recv, srcs, lands, token = _rs_start(f"rs_start{len(started)}", [group[k] for k in keys])
        started.append((keys, send, recv, srcs, lands))
        return (token,)

    sq_err, grad_x, _, small = _local_step(x[0], p, loss_target[0], w, emit)
    loss = lax.psum(0.5 / D_MODEL * sq_err[0, 0], ("x", "y", "c"))

    grads, deltas, new_m, new_v = {}, {}, {}, {}
    sg = _small_grads(small)
    two_d = lambda a: a.reshape(-1, a.shape[-1])
    d_s, m_s, v_s = _adamw_small([two_d(params[n]) for n in SMALL], [sg[n] for n in SMALL],
                                 [two_d(given["m_" + n]) for n in SMALL], [two_d(given["v_" + n]) for n in SMALL])
    for i, n in enumerate(SMALL):
        shape = params[n].shape
        grads[n], deltas[n], new_m[n], new_v[n] = (a.reshape(shape) for a in (sg[n], d_s[i], m_s[i], v_s[i]))

    me = _flat(*_my_place()).astype(jnp.int32).reshape(1)
    after = (grad_x, d_s[0])
    src_of, land_of = {}, {}
    for gi, (keys, send, recv, srcs, lands) in enumerate(started):
        srcs, lands = _rs_wait(f"rs_wait{gi}", send, recv, srcs, lands, after)
        for k, s, l in zip(keys, srcs, lands):
            src_of[k], land_of[k] = s, l
        done = [n for n in BIG if n not in grads and all((n, l) in src_of for l in range(params[n].shape[0]))]
        for n in done:
            layers = range(params[n].shape[0])
            grads[n], deltas[n], new_m[n], new_v[n] = _adamw_big(
                "adamw_" + n, params[n], given["m_" + n], given["v_" + n],
                [src_of[(n, l)] for l in layers], [land_of[(n, l)] for l in layers], me)
        if done:
            after = (deltas[done[-1]],)

    return (loss, grad_x[None], *[grads[n] for n in WEIGHTS], *[deltas[n] for n in WEIGHTS],
            *[new_m[n] for n in WEIGHTS], *[new_v[n] for n in WEIGHTS])
```

```python
import functools
import math

import jax
import jax.numpy as jnp
from jax import lax
from jax.experimental import pallas as pl
from jax.experimental.pallas import tpu as pltpu

F32 = jnp.float32
BF16 = jnp.bfloat16
MESH = pl.DeviceIdType.MESH
ANY = pl.BlockSpec(memory_space=pl.ANY)

N_DEV = 8
D_MODEL = 1024
CHUNK = 64
EPS = 1e-6
ROPE_THETA = 10000.0
RET_HEADS = 4
RET_DK = 256
RET_DV = 512
RET_QK_W = RET_HEADS * RET_DK
RET_V_W = RET_HEADS * RET_DV
RET_IN = 2 * RET_QK_W + 2 * RET_V_W
MLA_HEADS = 8
MLA_NOPE = 128
MLA_ROPE = 64
MLA_QKD = MLA_NOPE + MLA_ROPE
MLA_VD = 128
MLA_Q_RANK = 384
MLA_KV_RANK = 256
MLA_IN = MLA_Q_RANK + MLA_KV_RANK + MLA_ROPE
MLA_IN_PAD = 768
MLA_HD_PAD = 256
D_FF = 4096
PLE_DIM = 256
ATT_SCALE = MLA_QKD ** -0.5
LOG2E = 1.4426950408889634
ATT_EXP2 = ATT_SCALE * LOG2E

ADAM_LR = 0.001
ADAM_B1 = 0.9
ADAM_B2 = 0.999
ADAM_EPS = 1e-08
ADAM_WD = 0.01
ADAM_STEP = 10

VMEM_LIMIT = 52 * 1024 * 1024
ROW_TILE = 1024
RET_ROWS = 256
ATT_BLOCK = 256
ATT_QROWS = 512
ATT_HEADS = 2

WEIGHTS = ['mix_norm', 'ret_w_in', 'ret_gn', 'ret_w_out', 'mla_w_in', 'mla_q_a_norm', 'mla_kv_a_norm',
           'mla_w_uq', 'mla_w_ukv', 'mla_q_norm', 'mla_k_norm', 'mla_w_out', 'mlp_norm', 'mlp_w1', 'mlp_w2',
           'ple_norm', 'ple_gate_w', 'ple_proj_w']
BIG = ['ret_w_in', 'ret_w_out', 'mla_w_in', 'mla_w_uq', 'mla_w_ukv', 'mla_w_out', 'mlp_w1', 'mlp_w2',
       'ple_gate_w', 'ple_proj_w']
SMALL = [w for w in WEIGHTS if w not in BIG]


def _cparams(sem=None):
    return pltpu.CompilerParams(dimension_semantics=sem, vmem_limit_bytes=VMEM_LIMIT)


def _dot(a, b, ca, cb):
    return lax.dot_general(a, b, (((ca,), (cb,)), ((), ())), preferred_element_type=F32)


def _bf(v):
    return v if v.dtype == BF16 else v.astype(BF16)


def _sigmoid(z):
    return 1.0 / (1.0 + jnp.exp(-z))


def _mm(name, grid, a, a_spec, b, b_spec, contract, outs, extras=(), epi=None, deps=()):
    nk = grid[2]
    n_ex, n_out, n_dep = len(extras), len(outs), len(deps)
    acc_shape = tuple(d for d in outs[0][1].block_shape if d is not None)

    def body(*refs):
        a_ref, b_ref = refs[:2]
        ex_refs = refs[2:2 + n_ex]
        out_refs = refs[2 + n_ex + n_dep:2 + n_ex + n_dep + n_out]

        def product():
            return _dot(_bf(a_ref[...]), _bf(b_ref[...]), contract[0], contract[1])

        def finish(acc):
            res = epi(acc, *[r[...] for r in ex_refs]) if epi is not None else (acc,)
            for o, r in zip(out_refs, res):
                o[...] = r.astype(o.dtype)

        if nk == 1:
            finish(product())
        else:
            acc_ref = refs[-1]
            k = pl.program_id(2)

            @pl.when(k == 0)
            def _():
                acc_ref[...] = jnp.zeros_like(acc_ref)

            acc_ref[...] += product()

            @pl.when(k == nk - 1)
            def _():
                finish(acc_ref[...])

    return pl.pallas_call(
        body, name=name, grid=grid,
        in_specs=[a_spec, b_spec] + [s for _, s in extras] + [ANY] * n_dep,
        out_specs=[s for _, s in outs],
        out_shape=[s for s, _ in outs],
        scratch_shapes=[pltpu.VMEM(acc_shape, F32)] if nk > 1 else [],
        compiler_params=_cparams(("parallel", "parallel", "arbitrary")),
    )(a, b, *[x for x, _ in extras], *deps)


def _sds(shape, dtype):
    return jax.ShapeDtypeStruct(shape, dtype)


def _row_tile(t, cap=ROW_TILE):
    return min(cap, t)


def _rms_fwd(name, x, g):
    t, d = x.shape
    tm = _row_tile(t)

    def body(x_ref, g_ref, o_ref):
        xv = x_ref[...]
        r = lax.rsqrt(jnp.mean(xv * xv, axis=-1, keepdims=True) + EPS)
        o_ref[...] = (xv * r * g_ref[...]).astype(o_ref.dtype)

    return pl.pallas_call(
        body, name=name, grid=(t // tm,),
        in_specs=[pl.BlockSpec((tm, d), lambda i: (i, 0)), pl.BlockSpec((1, d), lambda i: (0, 0))],
        out_specs=pl.BlockSpec((tm, d), lambda i: (i, 0)),
        out_shape=_sds((t, d), BF16),
        compiler_params=_cparams(("parallel",)),
    )(x, g)


def _rms_bwd_rows(dy, xv, g, n):
    r = lax.rsqrt(jnp.sum(xv * xv, axis=-1, keepdims=True) / n + EPS)
    xh = xv * r
    dxh = dy * g
    dx = r * (dxh - xh * (jnp.sum(dxh * xh, axis=-1, keepdims=True) / n))
    return dx, dy * xh


def _rms_bwd(name, dy, x, g, res):
    t, d = x.shape
    tm = _row_tile(t, 512)

    def body(dy_ref, x_ref, g_ref, res_ref, dx_ref, dg_ref):
        @pl.when(pl.program_id(0) == 0)
        def _():
            dg_ref[...] = jnp.zeros_like(dg_ref)

        dx, dgr = _rms_bwd_rows(dy_ref[...], x_ref[...], g_ref[...], d)
        dx_ref[...] = res_ref[...] + dx
        dg_ref[...] += jnp.sum(dgr, axis=0, keepdims=True)

    row = pl.BlockSpec((tm, d), lambda i: (i, 0))
    vec = pl.BlockSpec((1, d), lambda i: (0, 0))
    return pl.pallas_call(
        body, name=name, grid=(t // tm,),
        in_specs=[row, row, vec, row], out_specs=[row, vec],
        out_shape=[_sds((t, d), F32), _sds((1, d), F32)],
        compiler_params=_cparams(("arbitrary",)),
    )(dy, x, g, res)


def _loss_head(y, target):
    t, d = y.shape
    tm = _row_tile(t)

    def body(y_ref, t_ref, dy_ref, l_ref):
        @pl.when(pl.program_id(0) == 0)
        def _():
            l_ref[...] = jnp.zeros_like(l_ref)

        e = y_ref[...] - t_ref[...]
        dy_ref[...] = e / d
        l_ref[...] += jnp.sum(jnp.sum(e * e, axis=-1, keepdims=True), axis=0, keepdims=True)

    row = pl.BlockSpec((tm, d), lambda i: (i, 0))
    return pl.pallas_call(
        body, name="loss_head", grid=(t // tm,),
        in_specs=[row, row], out_specs=[row, pl.BlockSpec((8, 128), lambda i: (0, 0))],
        out_shape=[_sds((t, d), F32), _sds((8, 128), F32)],
        compiler_params=_cparams(("arbitrary",)),
    )(y, target)


def _ple_gate_bwd(name, dh, gate, e):
    t, d = dh.shape
    tm = _row_tile(t)

    def body(dh_ref, g_ref, e_ref, de_ref, dz_ref):
        dh_v, gt = dh_ref[...], g_ref[...]
        de_ref[...] = (dh_v * gt).astype(BF16)
        dz_ref[...] = (dh_v * e_ref[...] * (gt * (1.0 - gt))).astype(BF16)

    row = pl.BlockSpec((tm, d), lambda i: (i, 0))
    return pl.pallas_call(
        body, name=name, grid=(t // tm,), in_specs=[row, row, row], out_specs=[row, row],
        out_shape=[_sds((t, d), BF16), _sds((t, d), BF16)],
        compiler_params=_cparams(("parallel",)),
    )(dh, gate, e)


def _rope_half(v, cos, sin):
    half = v.shape[-1] // 2
    v1, v2 = v[:, :half], v[:, half:]
    return jnp.concatenate([v1 * cos - v2 * sin, v2 * cos + v1 * sin], axis=-1)


def _ret_consts():
    lg = jnp.log(1.0 - 2.0 ** (-5.0 - jnp.arange(RET_HEADS, dtype=F32)))
    idx = jnp.arange(CHUNK, dtype=F32)
    intra = jnp.exp(lg[:, None, None] * jnp.abs(idx[:, None] - idx[None, :]))
    qdec = jnp.exp(lg[:, None] * (idx + 1.0))
    kdec = jnp.exp(lg[:, None] * (CHUNK - 1.0 - idx))
    cdec = jnp.exp(lg * CHUNK)
    qdec = jnp.broadcast_to(qdec[:, :, None], (RET_HEADS, CHUNK, RET_DK))
    kdec = jnp.broadcast_to(kdec[:, :, None], (RET_HEADS, CHUNK, RET_DK))
    cdec = jnp.broadcast_to(cdec[:, None, None], (RET_HEADS, 1, RET_DV))
    return intra, qdec, kdec, cdec


def _ret_specs(rb, rev_nb=None):
    blk = (lambda i: i) if rev_nb is None else (lambda i: rev_nb - 1 - i)
    full = lambda shape: pl.BlockSpec(shape, lambda i: (0,) * len(shape))
    return dict(
        proj=pl.BlockSpec((rb, RET_IN), lambda i: (blk(i), 0)),
        tab=pl.BlockSpec((rb, RET_DK // 2), lambda i: (blk(i), 0)),
        vw=pl.BlockSpec((rb, RET_V_W), lambda i: (blk(i), 0)),
        st=pl.BlockSpec((rb // CHUNK, RET_HEADS, RET_DK, RET_DV), lambda i: (blk(i), 0, 0, 0)),
        gn=full((RET_HEADS, 1, RET_DV)),
        intra=full((RET_HEADS, CHUNK, CHUNK)),
        dec=full((RET_HEADS, CHUNK, RET_DK)),
        cdec=full((RET_HEADS, 1, RET_DV)),
    )


def _ret_fwd(proj, cos, sin, gn):
    t = proj.shape[0]
    rb = min(RET_ROWS, t)
    cpb = rb // CHUNK
    intra, qdec, kdec, cdec = _ret_consts()
    sp = _ret_specs(rb)

    def body(proj_ref, cos_ref, sin_ref, gn_ref, intra_ref, qd_ref, kd_ref, cd_ref,
             gated_ref, outp_ref, st_ref, s_ref):
        @pl.when(pl.program_id(0) == 0)
        def _():
            s_ref[...] = jnp.zeros_like(s_ref)

        def chunk(c, carry):
            rows = pl.ds(pl.multiple_of(c * CHUNK, CHUNK), CHUNK)
            cs, sn = cos_ref[rows, :], sin_ref[rows, :]
            for h in range(RET_HEADS):
                q = proj_ref[rows, h * RET_DK:(h + 1) * RET_DK]
                k = proj_ref[rows, RET_QK_W + h * RET_DK:RET_QK_W + (h + 1) * RET_DK]
                v = proj_ref[rows, 2 * RET_QK_W + h * RET_DV:2 * RET_QK_W + (h + 1) * RET_DV]
                g = proj_ref[rows, 2 * RET_QK_W + RET_V_W + h * RET_DV:2 * RET_QK_W + RET_V_W + (h + 1) * RET_DV]
                qr = _rope_half(q, cs, sn)
                kr = _rope_half(k, cs, sn) * (RET_DK ** -0.5)
                qb, kb, vb = qr.astype(BF16), kr.astype(BF16), v.astype(BF16)
                sc = _dot(qb, kb, 1, 1) * intra_ref[h]
                inner = _dot(sc.astype(BF16), vb, 1, 0)
                s_old = s_ref[h]
                sb = s_old.astype(BF16)
                st_ref[c, h] = sb
                cross = _dot((qr * qd_ref[h]).astype(BF16), sb, 1, 0)
                out = inner + cross
                s_ref[h] = s_old * cd_ref[h] + _dot((kr * kd_ref[h]).astype(BF16), vb, 0, 0)
                r = lax.rsqrt(jnp.mean(out * out, axis=-1, keepdims=True) + EPS)
                y = out * r * gn_ref[h]
                cols = slice(h * RET_DV, (h + 1) * RET_DV)
                gated_ref[rows, cols] = (g * _sigmoid(g) * y).astype(BF16)
                outp_ref[rows, cols] = out
            return carry

        lax.fori_loop(0, cpb, chunk, 0)

    return pl.pallas_call(
        body, name="ret_fwd", grid=(t // rb,),
        in_specs=[sp['proj'], sp['tab'], sp['tab'], sp['gn'], sp['intra'], sp['dec'], sp['dec'], sp['cdec']],
        out_specs=[sp['vw'], sp['vw'], sp['st']],
        out_shape=[_sds((t, RET_V_W), BF16), _sds((t, RET_V_W), F32),
                   _sds((t // CHUNK, RET_HEADS, RET_DK, RET_DV), BF16)],
        scratch_shapes=[pltpu.VMEM((RET_HEADS, RET_DK, RET_DV), F32)],
        compiler_params=_cparams(("arbitrary",)),
    )(proj, cos, sin, gn.reshape(RET_HEADS, 1, RET_DV), intra, qdec, kdec, cdec)


def _ret_bwd(proj, cos, sin, gn, outp, states, dgated):
    t = proj.shape[0]
    rb = min(RET_ROWS, t)
    cpb = rb // CHUNK
    nb = t // rb
    intra, qdec, kdec, cdec = _ret_consts()
    sp = _ret_specs(rb, rev_nb=nb)

    def body(proj_ref, cos_ref, sin_ref, gn_ref, intra_ref, qd_ref, kd_ref, cd_ref, outp_ref, st_ref, dgt_ref,
             dproj_ref, dgn_ref, ds_ref):
        @pl.when(pl.program_id(0) == 0)
        def _():
            ds_ref[...] = jnp.zeros_like(ds_ref)
            dgn_ref[...] = jnp.zeros_like(dgn_ref)

        def chunk(cc, carry):
            c = cpb - 1 - cc
            rows = pl.ds(pl.multiple_of(c * CHUNK, CHUNK), CHUNK)
            cs, sn = cos_ref[rows, :], sin_ref[rows, :]
            for h in range(RET_HEADS):
                q = proj_ref[rows, h * RET_DK:(h + 1) * RET_DK]
                k = proj_ref[rows, RET_QK_W + h * RET_DK:RET_QK_W + (h + 1) * RET_DK]
                v = proj_ref[rows, 2 * RET_QK_W + h * RET_DV:2 * RET_QK_W + (h + 1) * RET_DV]
                g = proj_ref[rows, 2 * RET_QK_W + RET_V_W + h * RET_DV:2 * RET_QK_W + RET_V_W + (h + 1) * RET_DV]
                cols = slice(h * RET_DV, (h + 1) * RET_DV)
                qr = _rope_half(q, cs, sn)
                kr = _rope_half(k, cs, sn) * (RET_DK ** -0.5)
                qb, kb, vb = qr.astype(BF16), kr.astype(BF16), v.astype(BF16)
                qdb = (qr * qd_ref[h]).astype(BF16)
                kdb = (kr * kd_ref[h]).astype(BF16)
                out = outp_ref[rows, cols]
                dgt = dgt_ref[rows, cols]
                gnh = gn_ref[h]
                r = lax.rsqrt(jnp.mean(out * out, axis=-1, keepdims=True) + EPS)
                xh = out * r
                sg = _sigmoid(g)
                dgate = dgt * (xh * gnh) * (sg * (1.0 + g * (1.0 - sg)))
                dy = dgt * (g * sg)
                dgn_ref[h] += jnp.sum(dy * xh, axis=0, keepdims=True)
                dxh = dy * gnh
                dout = r * (dxh - xh * jnp.mean(dxh * xh, axis=-1, keepdims=True))
                doutb = dout.astype(BF16)
                itr = intra_ref[h]
                pb = (_dot(qb, kb, 1, 1) * itr).astype(BF16)
                dv = _dot(pb, doutb, 0, 0)
                dsc = (_dot(doutb, vb, 1, 1) * itr).astype(BF16)
                dq = _dot(dsc, kb, 1, 0)
                dk = _dot(dsc, qb, 0, 0)
                dq = dq + _dot(doutb, st_ref[c, h], 1, 1) * qd_ref[h]
                ds_new = ds_ref[h]
                dsb = ds_new.astype(BF16)
                dk = dk + _dot(vb, dsb, 1, 1) * kd_ref[h]
                dv = dv + _dot(kdb, dsb, 1, 0)
                ds_ref[h] = ds_new * cd_ref[h] + _dot(qdb, doutb, 0, 0)
                dproj_ref[rows, h * RET_DK:(h + 1) * RET_DK] = _rope_half(dq, cs, -sn).astype(BF16)
                dproj_ref[rows, RET_QK_W + h * RET_DK:RET_QK_W + (h + 1) * RET_DK] = (
                    _rope_half(dk * (RET_DK ** -0.5), cs, -sn).astype(BF16))
                dproj_ref[rows, 2 * RET_QK_W + h * RET_DV:2 * RET_QK_W + (h + 1) * RET_DV] = dv.astype(BF16)
                dproj_ref[rows, 2 * RET_QK_W + RET_V_W + h * RET_DV:
                          2 * RET_QK_W + RET_V_W + (h + 1) * RET_DV] = dgate.astype(BF16)
            return carry

        lax.fori_loop(0, cpb, chunk, 0)

    return pl.pallas_call(
        body, name="ret_bwd", grid=(nb,),
        in_specs=[sp['proj'], sp['tab'], sp['tab'], sp['gn'], sp['intra'], sp['dec'], sp['dec'], sp['cdec'],
                  sp['vw'], sp['st'], sp['vw']],
        out_specs=[sp['proj'], sp['gn']],
        out_shape=[_sds((t, RET_IN), BF16), _sds((RET_HEADS, 1, RET_DV), F32)],
        scratch_shapes=[pltpu.VMEM((RET_HEADS, RET_DK, RET_DV), F32)],
        compiler_params=_cparams(("arbitrary",)),
    )(proj, cos, sin, gn.reshape(RET_HEADS, 1, RET_DV), intra, qdec, kdec, cdec, outp, states, dgated)


def _mla_tables(t):
    half = MLA_ROPE // 2
    inv = 1.0 / (ROPE_THETA ** (jnp.arange(0, MLA_ROPE, 2, dtype=F32) / MLA_ROPE))
    ang = jnp.arange(t, dtype=F32)[:, None] * inv[None, :]
    cos, sin = jnp.cos(ang), jnp.sin(ang)
    z = jnp.zeros((t, half), F32)
    c = jnp.concatenate([cos, cos, z, z], axis=1)
    s1 = jnp.concatenate([-sin, z, z, z], axis=1)
    s2 = jnp.concatenate([z, sin, z, z], axis=1)
    return c, s1, s2


def _rope_tile(r, c, s1, s2):
    return r * c + pltpu.roll(r, 96, 1) * s1 + pltpu.roll(r, 32, 1) * s2


def _mla_mid(proj2, qa, kva):
    t = proj2.shape[0]
    tm = _row_tile(t)

    def body(p_ref, qa_ref, kva_ref, cq_ref, ckv_ref):
        cq = p_ref[:, :MLA_Q_RANK]
        ckv = p_ref[:, MLA_Q_RANK:MLA_Q_RANK + MLA_KV_RANK]
        rq = lax.rsqrt(jnp.mean(cq * cq, axis=-1, keepdims=True) + EPS)
        rkv = lax.rsqrt(jnp.mean(ckv * ckv, axis=-1, keepdims=True) + EPS)
        cq_ref[...] = (cq * rq * qa_ref[...]).astype(BF16)
        ckv_ref[...] = (ckv * rkv * kva_ref[...]).astype(BF16)

    return pl.pallas_call(
        body, name="mla_mid", grid=(t // tm,),
        in_specs=[pl.BlockSpec((tm, MLA_IN_PAD), lambda i: (i, 0)),
                  pl.BlockSpec((1, MLA_Q_RANK), lambda i: (0, 0)),
                  pl.BlockSpec((1, MLA_KV_RANK), lambda i: (0, 0))],
        out_specs=[pl.BlockSpec((tm, MLA_Q_RANK), lambda i: (i, 0)),
                   pl.BlockSpec((tm, MLA_KV_RANK), lambda i: (i, 0))],
        out_shape=[_sds((t, MLA_Q_RANK), BF16), _sds((t, MLA_KV_RANK), BF16)],
        compiler_params=_cparams(("parallel",)),
    )(proj2, qa, kva)


def _mla_mid_bwd(proj2, qa, kva, dcq, dckv, dkr):
    t = proj2.shape[0]
    tm = _row_tile(t)

    def body(p_ref, qa_ref, kva_ref, dcq_ref, dckv_ref, dkr_ref, dp_ref, dqa_ref, dkva_ref):
        @pl.when(pl.program_id(0) == 0)
        def _():
            dqa_ref[...] = jnp.zeros_like(dqa_ref)
            dkva_ref[...] = jnp.zeros_like(dkva_ref)

        dxq, dgq = _rms_bwd_rows(dcq_ref[...], p_ref[:, :MLA_Q_RANK], qa_ref[...], MLA_Q_RANK)
        dxk, dgk = _rms_bwd_rows(dckv_ref[...], p_ref[:, MLA_Q_RANK:MLA_Q_RANK + MLA_KV_RANK], kva_ref[...],
                                 MLA_KV_RANK)
        dp_ref[:, :MLA_Q_RANK] = dxq.astype(BF16)
        dp_ref[:, MLA_Q_RANK:MLA_Q_RANK + MLA_KV_RANK] = dxk.astype(BF16)
        dp_ref[:, MLA_Q_RANK + MLA_KV_RANK:] = dkr_ref[...].astype(BF16)
        dqa_ref[...] += jnp.sum(dgq, axis=0, keepdims=True)
        dkva_ref[...] += jnp.sum(dgk, axis=0, keepdims=True)

    return pl.pallas_call(
        body, name="mla_mid_bwd", grid=(t // tm,),
        in_specs=[pl.BlockSpec((tm, MLA_IN_PAD), lambda i: (i, 0)),
                  pl.BlockSpec((1, MLA_Q_RANK), lambda i: (0, 0)),
                  pl.BlockSpec((1, MLA_KV_RANK), lambda i: (0, 0)),
                  pl.BlockSpec((tm, MLA_Q_RANK), lambda i: (i, 0)),
                  pl.BlockSpec((tm, MLA_KV_RANK), lambda i: (i, 0)),
                  pl.BlockSpec((tm, 128), lambda i: (i, 0))],
        out_specs=[pl.BlockSpec((tm, MLA_IN_PAD), lambda i: (i, 0)),
                   pl.BlockSpec((1, MLA_Q_RANK), lambda i: (0, 0)),
                   pl.BlockSpec((1, MLA_KV_RANK), lambda i: (0, 0))],
        out_shape=[_sds((t, MLA_IN_PAD), BF16), _sds((1, MLA_Q_RANK), F32), _sds((1, MLA_KV_RANK), F32)],
        compiler_params=_cparams(("arbitrary",)),
    )(proj2, qa, kva, dcq, dckv, dkr)


def _mla_prep_specs(t, tm):
    head = lambda w: pl.BlockSpec((None, tm, w), lambda i, h: (h, i, 0))
    return dict(
        head256=head(MLA_HD_PAD), head128=head(MLA_VD),
        kr=pl.BlockSpec((tm, 128), lambda i, h: (i, (MLA_Q_RANK + MLA_KV_RANK) // 128)),
        gain=pl.BlockSpec((1, MLA_HD_PAD), lambda i, h: (0, 0)),
        tab=pl.BlockSpec((tm, 128), lambda i, h: (i, 0)),
    )


def _mla_prep(q, kv, proj2, gq, gk, tabs):
    t = q.shape[1]
    tm = _row_tile(t)
    sp = _mla_prep_specs(t, tm)

    def body(q_ref, kv_ref, kr_ref, gq_ref, gk_ref, c_ref, s1_ref, s2_ref, qh_ref, kh_ref, vh_ref):
        c, s1, s2 = c_ref[...], s1_ref[...], s2_ref[...]

        def norm_rope(xv, gain):
            r = lax.rsqrt(jnp.sum(xv * xv, axis=-1, keepdims=True) / MLA_QKD + EPS)
            y = xv * r * gain
            return jnp.concatenate([y[:, :MLA_NOPE], _rope_tile(y[:, MLA_NOPE:], c, s1, s2)], axis=-1)

        kvv = kv_ref[...]
        qh_ref[...] = norm_rope(q_ref[...], gq_ref[...]).astype(BF16)
        kf = jnp.concatenate([kvv[:, :MLA_NOPE], kr_ref[...]], axis=-1)
        kh_ref[...] = norm_rope(kf, gk_ref[...]).astype(BF16)
        vh_ref[...] = jnp.concatenate([kvv[:, MLA_NOPE:], jnp.ones((tm, MLA_VD), F32)], axis=-1).astype(BF16)

    return pl.pallas_call(
        body, name="mla_prep", grid=(t // tm, MLA_HEADS),
        in_specs=[sp['head256'], sp['head256'], sp['kr'], sp['gain'], sp['gain'], sp['tab'], sp['tab'], sp['tab']],
        out_specs=[sp['head256'], sp['head256'], sp['head256']],
        out_shape=[_sds((MLA_HEADS, t, MLA_HD_PAD), BF16), _sds((MLA_HEADS, t, MLA_HD_PAD), BF16),
                   _sds((MLA_HEADS, t, 2 * MLA_VD), BF16)],
        compiler_params=_cparams(("parallel", "arbitrary")),
    )(q, kv, proj2, gq, gk, *tabs)


def _mla_prep_bwd(q, kv, proj2, gq, gk, tabs, dqt, dkh, dvh):
    t = q.shape[1]
    tm = _row_tile(t)
    ab = dqt.shape[-1]
    sp = _mla_prep_specs(t, tm)

    def body(q_ref, kv_ref, kr_ref, gq_ref, gk_ref, c_ref, s1_ref, s2_ref, dqt_ref, dkh_ref, dvh_ref,
             dq_ref, dkv_ref, dkr_ref, dgq_ref, dgk_ref):
        dqh = jnp.concatenate([dqt_ref[b].T for b in range(tm // ab)], axis=0)
        i, h = pl.program_id(0), pl.program_id(1)

        @pl.when((i == 0) & (h == 0))
        def _():
            dgq_ref[...] = jnp.zeros_like(dgq_ref)
            dgk_ref[...] = jnp.zeros_like(dgk_ref)

        @pl.when(h == 0)
        def _():
            dkr_ref[...] = jnp.zeros_like(dkr_ref)

        c, s1, s2 = c_ref[...], s1_ref[...], s2_ref[...]

        def back(xv, gain, dout):
            dy = jnp.concatenate([dout[:, :MLA_NOPE], _rope_tile(dout[:, MLA_NOPE:], c, -s1, -s2)], axis=-1)
            return _rms_bwd_rows(dy, xv, gain, MLA_QKD)

        kvv = kv_ref[...]
        dxq, dgq = back(q_ref[...], gq_ref[...], dqh)
        kf = jnp.concatenate([kvv[:, :MLA_NOPE], kr_ref[...]], axis=-1)
        dxk, dgk = back(kf, gk_ref[...], dkh_ref[...])
        dq_ref[...] = dxq.astype(BF16)
        dkv_ref[...] = jnp.concatenate([dxk[:, :MLA_NOPE], dvh_ref[...]], axis=-1).astype(BF16)
        dkr_ref[...] += dxk[:, MLA_NOPE:]
        dgq_ref[...] += jnp.sum(dgq, axis=0, keepdims=True)
        dgk_ref[...] += jnp.sum(dgk, axis=0, keepdims=True)

    return pl.pallas_call(
        body, name="mla_prep_bwd", grid=(t // tm, MLA_HEADS),
        in_specs=[sp['head256'], sp['head256'], sp['kr'], sp['gain'], sp['gain'], sp['tab'], sp['tab'], sp['tab'],
                  pl.BlockSpec((None, tm // ab, MLA_HD_PAD, ab), lambda i, h: (h, i, 0, 0)),
                  sp['head256'], sp['head128']],
        out_specs=[sp['head256'], sp['head256'], sp['tab'], sp['gain'], sp['gain']],
        out_shape=[_sds((MLA_HEADS, t, MLA_HD_PAD), BF16), _sds((MLA_HEADS, t, MLA_HD_PAD), BF16),
                   _sds((t, 128), F32), _sds((1, MLA_HD_PAD), F32), _sds((1, MLA_HD_PAD), F32)],
        compiler_params=_cparams(("arbitrary", "arbitrary")),
    )(q, kv, proj2, gq, gk, *tabs, dqt, dkh, dvh)


def _chunk_visible(rows, cols, row_off, col_off):
    rq = lax.shift_right_logical(lax.broadcasted_iota(jnp.int32, (rows, cols), 0) + row_off, 6)
    ck = lax.shift_right_logical(lax.broadcasted_iota(jnp.int32, (rows, cols), 1) + col_off, 6)
    return ck <= rq


def _rows_to_lanes(col):
    return col.T[:8, :]


def _attn_fwd(qh, kh, vh):
    t = qh.shape[1]
    ab = min(ATT_BLOCK, t)
    tq = min(ATT_QROWS, t)
    r = tq // ab
    hg = ATT_HEADS

    def body(q_ref, k_ref, v_ref, o_ref, lse_ref):
        n_un = pl.program_id(1) * r

        def step(b, state, diag):
            rows = pl.ds(pl.multiple_of(b * ab, ab), ab)
            ms, accs = [], []
            for hh in range(hg):
                m, acc = state[0][hh], state[1][hh]
                s = _dot(q_ref[hh], k_ref[hh, rows, :], 1, 1)
                if diag is not None:
                    s = jnp.where(_chunk_visible(tq, ab, 0, diag * ab), s, -1e30)
                m_new = jnp.maximum(m, jnp.max(s, axis=-1, keepdims=True))
                p = jnp.exp2((s - m_new) * ATT_EXP2).astype(BF16)
                accs.append(jnp.exp2((m - m_new) * ATT_EXP2) * acc + _dot(p, v_ref[hh, rows, :], 1, 0))
                ms.append(m_new)
            return tuple(ms), tuple(accs)

        heads = lambda v: tuple(v for _ in range(hg))
        state = (heads(jnp.full((tq, 1), -1e30, F32)), heads(jnp.zeros((tq, 2 * MLA_VD), F32)))
        state = lax.fori_loop(0, n_un, lambda b, st: step(b, st, None), state)
        for d in range(r):
            state = step(n_un + d, state, d)
        ms, accs = state
        for hh in range(hg):
            l = accs[hh][:, MLA_VD:]
            o_ref[:, hh * MLA_VD:(hh + 1) * MLA_VD] = accs[hh][:, :MLA_VD] / l
            lse_t = _rows_to_lanes(ms[hh] * ATT_EXP2 + jnp.log(l) * LOG2E)
            for d in range(r):
                lse_ref[hh, d] = lse_t[:, d * ab:(d + 1) * ab]

    return pl.pallas_call(
        body, name="mla_attn", grid=(MLA_HEADS // hg, t // tq),
        in_specs=[pl.BlockSpec((hg, tq, MLA_HD_PAD), lambda g, i: (g, i, 0)),
                  pl.BlockSpec((hg, t, MLA_HD_PAD), lambda g, i: (g, 0, 0)),
                  pl.BlockSpec((hg, t, 2 * MLA_VD), lambda g, i: (g, 0, 0))],
        out_specs=[pl.BlockSpec((tq, hg * MLA_VD), lambda g, i: (i, g)),
                   pl.BlockSpec((hg, r, 8, ab), lambda g, i: (g, i, 0, 0))],
        out_shape=[_sds((t, MLA_HEADS * MLA_VD), F32), _sds((MLA_HEADS, t // ab, 8, ab), F32)],
        compiler_params=_cparams(("parallel", "arbitrary")),
    )(qh, kh, vh)


def _attn_delta(do, o, ab):
    t = do.shape[0]
    tm = _row_tile(t)

    def body(do_ref, o_ref, d_ref):
        d = jnp.sum(do_ref[...] * o_ref[...], axis=-1, keepdims=True)
        d_t = _rows_to_lanes(jnp.broadcast_to(d, (tm, 128)))
        for b in range(tm // ab):
            d_ref[b] = d_t[:, b * ab:(b + 1) * ab]

    col = pl.BlockSpec((tm, MLA_VD), lambda i, h: (i, h))
    return pl.pallas_call(
        body, name="mla_delta", grid=(t // tm, MLA_HEADS), in_specs=[col, col],
        out_specs=pl.BlockSpec((None, tm // ab, 8, ab), lambda i, h: (h, i, 0, 0)),
        out_shape=_sds((MLA_HEADS, t // ab, 8, ab), F32),
        compiler_params=_cparams(("parallel", "parallel")),
    )(do, o)


def _attn_bwd(qh, kh, vh, dob, lse_t, dl_t):
    t = qh.shape[1]
    ab = min(ATT_BLOCK, t)
    nq = t // ab
    hg = ATT_HEADS

    def body(q_ref, k_ref, v_ref, do_ref, lse_ref, dl_ref, dqt_ref, dk_ref, dv_ref):
        j = pl.program_id(1)

        @pl.when(j == 0)
        def _():
            dqt_ref[...] = jnp.zeros_like(dqt_ref)

        ks = [k_ref[hh] for hh in range(hg)]
        vs = [v_ref[hh, :, :MLA_VD] for hh in range(hg)]
        kts = [k.T for k in ks]

        def step(b, grads, masked):
            rows = pl.ds(pl.multiple_of(b * ab, ab), ab)
            out = []
            for hh in range(hg):
                dk, dv = grads[hh]
                q = q_ref[hh, rows, :]
                do = do_ref[rows, hh * MLA_VD:(hh + 1) * MLA_VD]
                s_t = _dot(ks[hh], q, 1, 1)
                if masked:
                    key_chunk = lax.shift_right_logical(lax.broadcasted_iota(jnp.int32, (ab, ab), 0), 6)
                    query_chunk = lax.shift_right_logical(lax.broadcasted_iota(jnp.int32, (ab, ab), 1), 6)
                    s_t = jnp.where(key_chunk <= query_chunk, s_t, -1e30)
                p_t = jnp.exp2(s_t * ATT_EXP2 - lse_ref[hh, b][0:1, :])
                dp_t = _dot(vs[hh], do, 1, 1)
                ds_t = (p_t * (dp_t - dl_ref[hh, b][0:1, :]) * ATT_SCALE).astype(BF16)
                dqt_ref[hh, b] += _dot(kts[hh], ds_t, 1, 0)
                out.append((dk + _dot(ds_t, q, 1, 0), dv + _dot(p_t.astype(BF16), do, 1, 0)))
            return tuple(out)

        grads = tuple((jnp.zeros((ab, MLA_HD_PAD), F32), jnp.zeros((ab, MLA_VD), F32)) for _ in range(hg))
        grads = step(j, grads, True)
        grads = lax.fori_loop(j + 1, nq, lambda b, g: step(b, g, False), grads)
        for hh in range(hg):
            dk_ref[hh] = grads[hh][0]
            dv_ref[hh] = grads[hh][1]

    whole = lambda w: pl.BlockSpec((hg, t, w), lambda g, j: (g, 0, 0))
    blk = lambda w: pl.BlockSpec((hg, ab, w), lambda g, j: (g, j, 0))
    stat = pl.BlockSpec((hg, nq, 8, ab), lambda g, j: (g, 0, 0, 0))
    return pl.pallas_call(
        body, name="mla_attn_bwd", grid=(MLA_HEADS // hg, nq),
        in_specs=[whole(MLA_HD_PAD), blk(MLA_HD_PAD), blk(2 * MLA_VD),
                  pl.BlockSpec((t, hg * MLA_VD), lambda g, j: (0, g)), stat, stat],
        out_specs=[pl.BlockSpec((hg, nq, MLA_HD_PAD, ab), lambda g, j: (g, 0, 0, 0)), blk(MLA_HD_PAD), blk(MLA_VD)],
        out_shape=[_sds((MLA_HEADS, nq, MLA_HD_PAD, ab), F32), _sds((MLA_HEADS, t, MLA_HD_PAD), F32),
                   _sds((MLA_HEADS, t, MLA_VD), F32)],
        compiler_params=_cparams(("parallel", "arbitrary")),
    )(qh, kh, vh, dob, lse_t, dl_t)


def _mlp_fwd(l, h, norm_g, w1g, w2g):
    t = h.shape[0]
    tm = _row_tile(t)
    nsh, _, wsh = w1g.shape
    hn = _rms_fwd(f"mlp_norm{l}", h, norm_g)

    def relu2(acc):
        r = jnp.maximum(acc, 0.0)
        return r, r * r

    tile = pl.BlockSpec((tm, wsh), lambda i, j, k: (i, j))
    r, u = _mm(f"mlp_up{l}", (t // tm, nsh, 1),
               hn, pl.BlockSpec((tm, D_MODEL), lambda i, j, k: (i, 0)),
               w1g, pl.BlockSpec((None, D_MODEL, wsh), lambda i, j, k: (j, 0, 0)), (1, 0),
               [(_sds((t, D_FF), BF16), tile), (_sds((t, D_FF), BF16), tile)], epi=relu2)
    row = pl.BlockSpec((tm, D_MODEL), lambda i, j, k: (i, 0))
    (h2,) = _mm(f"mlp_down{l}", (t // tm, 1, nsh),
                u, pl.BlockSpec((tm, wsh), lambda i, j, k: (i, k)),
                w2g, pl.BlockSpec((None, wsh, D_MODEL), lambda i, j, k: (k, 0, 0)), (1, 0),
                [(_sds((t, D_MODEL), F32), row)], extras=[(h, row)], epi=lambda acc, hv: (acc + hv,))
    return h2, (h, hn, r, u)


def _mlp_bwd(l, dh, saved, norm_g, w1g, w2g):
    h, hn, r, u = saved
    t = h.shape[0]
    tm = _row_tile(t)
    tk = _row_tile(t)
    nsh, _, wsh = w1g.shape
    tile = pl.BlockSpec((tm, wsh), lambda i, j, k: (i, j))
    (da,) = _mm(f"mlp_du{l}", (t // tm, nsh, 1),
                dh, pl.BlockSpec((tm, D_MODEL), lambda i, j, k: (i, 0)),
                w2g, pl.BlockSpec((None, wsh, D_MODEL), lambda i, j, k: (j, 0, 0)), (1, 1),
                [(_sds((t, D_FF), BF16), tile)], extras=[(r, tile)],
                epi=lambda acc, rv: (2.0 * rv.astype(F32) * acc,))
    (dw2,) = _mm(f"mlp_dw2{l}", (nsh, 1, t // tk),
                 u, pl.BlockSpec((tk, wsh), lambda i, j, k: (k, i)),
                 dh, pl.BlockSpec((tk, D_MODEL), lambda i, j, k: (k, 0)), (0, 0),
                 [(_sds((nsh, wsh, D_MODEL), BF16), pl.BlockSpec((None, wsh, D_MODEL), lambda i, j, k: (i, 0, 0)))])
    (dw1,) = _mm(f"mlp_dw1{l}", (1, nsh, t // tk),
                 hn, pl.BlockSpec((tk, D_MODEL), lambda i, j, k: (k, 0)),
                 da, pl.BlockSpec((tk, wsh), lambda i, j, k: (k, j)), (0, 0),
                 [(_sds((nsh, D_MODEL, wsh), BF16), pl.BlockSpec((None, D_MODEL, wsh), lambda i, j, k: (j, 0, 0)))])
    (dhn,) = _mm(f"mlp_dhn{l}", (t // tm, 1, nsh),
                 da, pl.BlockSpec((tm, wsh), lambda i, j, k: (i, k)),
                 w1g, pl.BlockSpec((None, D_MODEL, wsh), lambda i, j, k: (k, 0, 0)), (1, 1),
                 [(_sds((t, D_MODEL), F32), pl.BlockSpec((tm, D_MODEL), lambda i, j, k: (i, 0)))])
    dh_in, dg = _rms_bwd(f"mlp_norm_bwd{l}", dhn, h, norm_g, dh)
    return dh_in, dg, dw1, dw2


def _ple_fwd(l, h, p, norm_g, wg, wp):
    t = h.shape[0]
    tm = _row_tile(t, 512)
    hn = _rms_fwd(f"ple_norm{l}", h, norm_g)
    row = pl.BlockSpec((tm, D_MODEL), lambda i, j, k: (i, 0))
    full = lambda r: pl.BlockSpec((r, D_MODEL), lambda i, j, k: (0, 0))
    (e,) = _mm(f"ple_proj{l}", (t // tm, 1, 1),
               p, pl.BlockSpec((None, None, tm, PLE_DIM), lambda i, j, k: (l, 0, i, 0)),
               wp, full(PLE_DIM), (1, 0), [(_sds((t, D_MODEL), F32), row)])

    def gate_epi(acc, hv, ev):
        gt = _sigmoid(acc)
        return hv + gt * ev, gt

    h_out, gate = _mm(f"ple_gate{l}", (t // tm, 1, 1), hn, row, wg, full(D_MODEL), (1, 0),
                      [(_sds((t, D_MODEL), F32), row), (_sds((t, D_MODEL), F32), row)],
                      extras=[(h, row), (e, row)], epi=gate_epi)
    return h_out, (h, hn, gate, e)


def _ple_bwd(l, dh, saved, p, norm_g, wg, deps=()):
    h, hn, gate, e = saved
    t = h.shape[0]
    tm = _row_tile(t)
    tk = _row_tile(t, 512)
    de, dz = _ple_gate_bwd(f"ple_gate_bwd{l}", dh, gate, e)
    full = lambda r: pl.BlockSpec((r, D_MODEL), lambda i, j, k: (0, 0))
    rowk = pl.BlockSpec((tk, D_MODEL), lambda i, j, k: (k, 0))
    (dwp,) = _mm(f"ple_dwp{l}", (1, 1, t // tk),
                 p, pl.BlockSpec((None, None, tk, PLE_DIM), lambda i, j, k: (l, 0, k, 0)),
                 de, rowk, (0, 0), [(_sds((PLE_DIM, D_MODEL), BF16), full(PLE_DIM))], deps=deps)
    (dwg,) = _mm(f"ple_dwg{l}", (1, 1, t // tk), hn, rowk, dz, rowk, (0, 0),
                 [(_sds((D_MODEL, D_MODEL), BF16), full(D_MODEL))])
    row = pl.BlockSpec((tm, D_MODEL), lambda i, j, k: (i, 0))
    (dhn,) = _mm(f"ple_dhn{l}", (t // tm, 1, 1), dz, row, wg, full(D_MODEL), (1, 1),
                 [(_sds((t, D_MODEL), F32), row)])
    dh_in, dg = _rms_bwd(f"ple_norm_bwd{l}", dhn, h, norm_g, dh)
    return dh_in, dg, dwg, dwp


def _ret_layer_fwd(x, norm_g, wri, wro, gn, cos, sin, deps=()):
    t = x.shape[0]
    tm = _row_tile(t)
    nsh, _, wsh = wri.shape
    hn = _rms_fwd("mix_norm0", x, norm_g)
    (proj,) = _mm("ret_in", (t // tm, nsh, 1),
                  hn, pl.BlockSpec((tm, D_MODEL), lambda i, j, k: (i, 0)),
                  wri, pl.BlockSpec((None, D_MODEL, wsh), lambda i, j, k: (j, 0, 0)), (1, 0),
                  [(_sds((t, RET_IN), F32), pl.BlockSpec((tm, wsh), lambda i, j, k: (i, j)))], deps=deps)
    gated, outp, states = _ret_fwd(proj, cos, sin, gn)
    row = pl.BlockSpec((tm, D_MODEL), lambda i, j, k: (i, 0))
    kt = 512
    (h1,) = _mm("ret_out", (t // tm, 1, RET_V_W // kt),
                gated, pl.BlockSpec((tm, kt), lambda i, j, k: (i, k)),
                wro, pl.BlockSpec((kt, D_MODEL), lambda i, j, k: (k, 0)), (1, 0),
                [(_sds((t, D_MODEL), F32), row)], extras=[(x, row)], epi=lambda acc, xv: (acc + xv,))
    return h1, (x, hn, proj, gated, outp, states)


def _ret_layer_bwd(dh, saved, norm_g, wri, wro, gn, cos, sin, emit, deps=()):
    x, hn, proj, gated, outp, states = saved
    t = x.shape[0]
    tm = _row_tile(t)
    tk = _row_tile(t, 512)
    nsh, _, wsh = wri.shape
    (dgated,) = _mm("ret_dgated", (t // tm, RET_V_W // D_MODEL, 1),
                    dh, pl.BlockSpec((tm, D_MODEL), lambda i, j, k: (i, 0)),
                    wro, pl.BlockSpec((D_MODEL, D_MODEL), lambda i, j, k: (j, 0)), (1, 1),
                    [(_sds((t, RET_V_W), F32), pl.BlockSpec((tm, D_MODEL), lambda i, j, k: (i, j)))], deps=deps)
    kt = 512
    (dwro,) = _mm("ret_dwro", (RET_V_W // kt, 1, t // tk),
                  gated, pl.BlockSpec((tk, kt), lambda i, j, k: (k, i)),
                  dh, pl.BlockSpec((tk, D_MODEL), lambda i, j, k: (k, 0)), (0, 0),
                  [(_sds((RET_V_W, D_MODEL), BF16), pl.BlockSpec((kt, D_MODEL), lambda i, j, k: (i, 0)))])
    dproj, dgn = _ret_bwd(proj, cos, sin, gn, outp, states, dgated)
    (dwri,) = _mm("ret_dwri", (1, nsh, t // tk),
                  hn, pl.BlockSpec((tk, D_MODEL), lambda i, j, k: (k, 0)),
                  dproj, pl.BlockSpec((tk, wsh), lambda i, j, k: (k, j)), (0, 0),
                  [(_sds((nsh, D_MODEL, wsh), BF16), pl.BlockSpec((None, D_MODEL, wsh), lambda i, j, k: (j, 0, 0)))])
    deps = emit(dwro, dwri)
    (dhn,) = _mm("ret_dhn", (t // tm, 1, nsh),
                 dproj, pl.BlockSpec((tm, wsh), lambda i, j, k: (i, k)),
                 wri, pl.BlockSpec((None, D_MODEL, wsh), lambda i, j, k: (k, 0, 0)), (1, 1),
                 [(_sds((t, D_MODEL), F32), pl.BlockSpec((tm, D_MODEL), lambda i, j, k: (i, 0)))], deps=deps)
    dx, dg = _rms_bwd("mix_norm_bwd0", dhn, x, norm_g, dh)
    return dx, dg, dgn.reshape(RET_HEADS, RET_DV)


def _mla_layer_fwd(h, norm_g, wmi, qa, kva, wuq, wukv, gq, gk, wmo, tabs):
    t = h.shape[0]
    tm = _row_tile(t)
    hn = _rms_fwd("mix_norm1", h, norm_g)
    row = pl.BlockSpec((tm, D_MODEL), lambda i, j, k: (i, 0))
    (proj2,) = _mm("mla_in", (t // tm, 1, 1), hn, row,
                   wmi, pl.BlockSpec((D_MODEL, MLA_IN_PAD), lambda i, j, k: (0, 0)), (1, 0),
                   [(_sds((t, MLA_IN_PAD), F32), pl.BlockSpec((tm, MLA_IN_PAD), lambda i, j, k: (i, 0)))])
    cq, ckv = _mla_mid(proj2, qa, kva)
    head = pl.BlockSpec((None, tm, MLA_HD_PAD), lambda i, j, k: (j, i, 0))
    (q,) = _mm("mla_uq", (t // tm, MLA_HEADS, 1),
               cq, pl.BlockSpec((tm, MLA_Q_RANK), lambda i, j, k: (i, 0)),
               wuq, pl.BlockSpec((None, MLA_Q_RANK, MLA_HD_PAD), lambda i, j, k: (j, 0, 0)), (1, 0),
               [(_sds((MLA_HEADS, t, MLA_HD_PAD), F32), head)])
    (kv,) = _mm("mla_ukv", (t // tm, MLA_HEADS, 1),
                ckv, pl.BlockSpec((tm, MLA_KV_RANK), lambda i, j, k: (i, 0)),
                wukv, pl.BlockSpec((None, MLA_KV_RANK, MLA_HD_PAD), lambda i, j, k: (j, 0, 0)), (1, 0),
                [(_sds((MLA_HEADS, t, MLA_HD_PAD), F32), head)])
    qh, kh, vh = _mla_prep(q, kv, proj2, gq, gk, tabs)
    o, lse = _attn_fwd(qh, kh, vh)
    (h_out,) = _mm("mla_out", (t // tm, 1, 1), o, row,
                   wmo, pl.BlockSpec((D_MODEL, D_MODEL), lambda i, j, k: (0, 0)), (1, 0),
                   [(_sds((t, D_MODEL), F32), row)], extras=[(h, row)], epi=lambda acc, hv: (acc + hv,))
    return h_out, (h, hn, proj2, cq, ckv, q, kv, qh, kh, vh, o, lse)


def _mla_layer_bwd(dh, saved, norm_g, wmi, qa, kva, wuq, wukv, gq, gk, wmo, tabs, deps=()):
    h, hn, proj2, cq, ckv, q, kv, qh, kh, vh, o, lse = saved
    t = h.shape[0]
    tm = _row_tile(t)
    tk = _row_tile(t, 512)
    row = pl.BlockSpec((tm, D_MODEL), lambda i, j, k: (i, 0))
    rowk = pl.BlockSpec((tk, D_MODEL), lambda i, j, k: (k, 0))
    sq = pl.BlockSpec((D_MODEL, D_MODEL), lambda i, j, k: (0, 0))
    do, dob = _mm("mla_do", (t // tm, 1, 1), dh, row, wmo, sq, (1, 1),
                  [(_sds((t, D_MODEL), F32), row), (_sds((t, D_MODEL), BF16), row)], epi=lambda acc: (acc, acc),
                  deps=deps)
    (dwmo,) = _mm("mla_dwo", (1, 1, t // tk), o, rowk, dh, rowk, (0, 0), [(_sds((D_MODEL, D_MODEL), BF16), sq)])
    delta = _attn_delta(do, o, lse.shape[-1])
    dqt, dkh, dvh = _attn_bwd(qh, kh, vh, dob, lse, delta)
    dq, dkv, dkr, dgq, dgk = _mla_prep_bwd(q, kv, proj2, gq, gk, tabs, dqt, dkh, dvh)

    headk = pl.BlockSpec((None, tk, MLA_HD_PAD), lambda i, j, k: (j, k, 0))
    (dwuq,) = _mm("mla_dwuq", (1, MLA_HEADS, t // tk),
                  cq, pl.BlockSpec((tk, MLA_Q_RANK), lambda i, j, k: (k, 0)), dq, headk, (0, 0),
                  [(_sds((MLA_HEADS, MLA_Q_RANK, MLA_HD_PAD), BF16),
                    pl.BlockSpec((None, MLA_Q_RANK, MLA_HD_PAD), lambda i, j, k: (j, 0, 0)))])
    (dwukv,) = _mm("mla_dwukv", (1, MLA_HEADS, t // tk),
                   ckv, pl.BlockSpec((tk, MLA_KV_RANK), lambda i, j, k: (k, 0)), dkv, headk, (0, 0),
                   [(_sds((MLA_HEADS, MLA_KV_RANK, MLA_HD_PAD), BF16),
                     pl.BlockSpec((None, MLA_KV_RANK, MLA_HD_PAD), lambda i, j, k: (j, 0, 0)))])
    headi = pl.BlockSpec((None, tm, MLA_HD_PAD), lambda i, j, k: (k, i, 0))
    (dcq,) = _mm("mla_dcq", (t // tm, 1, MLA_HEADS), dq, headi,
                 wuq, pl.BlockSpec((None, MLA_Q_RANK, MLA_HD_PAD), lambda i, j, k: (k, 0, 0)), (1, 1),
                 [(_sds((t, MLA_Q_RANK), F32), pl.BlockSpec((tm, MLA_Q_RANK), lambda i, j, k: (i, 0)))])
    (dckv,) = _mm("mla_dckv", (t // tm, 1, MLA_HEADS), dkv, headi,
                  wukv, pl.BlockSpec((None, MLA_KV_RANK, MLA_HD_PAD), lambda i, j, k: (k, 0, 0)), (1, 1),
                  [(_sds((t, MLA_KV_RANK), F32), pl.BlockSpec((tm, MLA_KV_RANK), lambda i, j, k: (i, 0)))])
    dproj2, dqa, dkva = _mla_mid_bwd(proj2, qa, kva, dcq, dckv, dkr)
    win = pl.BlockSpec((D_MODEL, MLA_IN_PAD), lambda i, j, k: (0, 0))
    (dwmi,) = _mm("mla_dwin", (1, 1, t // tk), hn, rowk,
                  dproj2, pl.BlockSpec((tk, MLA_IN_PAD), lambda i, j, k: (k, 0)), (0, 0),
                  [(_sds((D_MODEL, MLA_IN_PAD), BF16), win)])
    (dhn,) = _mm("mla_dhn", (t // tm, 1, 1),
                 dproj2, pl.BlockSpec((tm, MLA_IN_PAD), lambda i, j, k: (i, 0)), wmi, win, (1, 1),
                 [(_sds((t, D_MODEL), F32), row)])
    dh_in, dg = _rms_bwd("mix_norm_bwd1", dhn, h, norm_g, dh)
    return dh_in, dict(mix=dg, wmi=dwmi, qa=dqa, kva=dkva, wuq=dwuq, wukv=dwukv, gq=dgq, gk=dgk, wmo=dwmo)


def _local_step(x, p, target, w, fetch, emit=lambda group: ()):
    t = x.shape[0]
    inv = 1.0 / (ROPE_THETA ** (jnp.arange(0, RET_DK, 2, dtype=F32) / RET_DK))
    ang = jnp.arange(t, dtype=F32)[:, None] * inv[None, :]
    cos_r, sin_r = jnp.cos(ang), jnp.sin(ang)
    tabs = _mla_tables(t)
    row = lambda a, i: a[i:i + 1]

    h1, s_ret = _ret_layer_fwd(x, row(w['mix_norm'], 0), w['ret_w_in'], w['ret_w_out'], w['ret_gn'], cos_r, sin_r,
                               deps=w['deps'])
    w0 = fetch('layer0', (h1,))
    h2, s_mlp0 = _mlp_fwd(0, h1, row(w['mlp_norm'], 0), w0['mlp_w1'], w0['mlp_w2'])
    h3, s_ple0 = _ple_fwd(0, h2, p, row(w['ple_norm'], 0), w0['ple_gate_w'], w0['ple_proj_w'])
    wm = fetch('mla', (h3,))
    mla_w = (wm['mla_w_in'], w['mla_q_a_norm'], w['mla_kv_a_norm'], wm['mla_w_uq'], wm['mla_w_ukv'],
             w['mla_q_norm'], w['mla_k_norm'], wm['mla_w_out'], tabs)
    h4, s_mla = _mla_layer_fwd(h3, row(w['mix_norm'], 1), *mla_w)
    w1 = fetch('layer1', (h4,))
    h5, s_mlp1 = _mlp_fwd(1, h4, row(w['mlp_norm'], 1), w1['mlp_w1'], w1['mlp_w2'])
    y, s_ple1 = _ple_fwd(1, h5, p, row(w['ple_norm'], 1), w1['ple_gate_w'], w1['ple_proj_w'])

    dy, sq_err = _loss_head(y, target)

    n = N_DEV
    colsh = lambda a: a.reshape(a.shape[0], n, a.shape[1] // n).transpose(1, 0, 2)
    rowsh = lambda a: a.reshape(n, a.shape[0] // n, a.shape[1])
    big = {}

    def emit_group(group):
        big.update(group)
        return emit(group)

    dh5, dg_ple1, dwg1, dwp1 = _ple_bwd(1, dy, s_ple1, p, row(w['ple_norm'], 1), w1['ple_gate_w'])
    dh4, dg_mlp1, dw1_1, dw2_1 = _mlp_bwd(1, dh5, s_mlp1, row(w['mlp_norm'], 1), w1['mlp_w1'], w1['mlp_w2'])
    deps = emit_group({('ple_gate_w', 1): rowsh(dwg1), ('ple_proj_w', 1): colsh(dwp1),
                       ('mlp_w2', 1): dw2_1, ('mlp_w1', 1): dw1_1})
    dh3, gm = _mla_layer_bwd(dh4, s_mla, row(w['mix_norm'], 1), *mla_w, deps=deps)
    deps = emit_group({('mla_w_out', 0): rowsh(gm['wmo']), ('mla_w_uq', 0): gm['wuq'][:, :, :MLA_QKD],
                       ('mla_w_ukv', 0): gm['wukv'], ('mla_w_in', 0): rowsh(gm['wmi'][:, :MLA_IN])})
    dh2, dg_ple0, dwg0, dwp0 = _ple_bwd(0, dh3, s_ple0, p, row(w['ple_norm'], 0), w0['ple_gate_w'], deps=deps)
    dh1, dg_mlp0, dw1_0, dw2_0 = _mlp_bwd(0, dh2, s_mlp0, row(w['mlp_norm'], 0), w0['mlp_w1'], w0['mlp_w2'])
    deps = emit_group({('ple_gate_w', 0): rowsh(dwg0), ('ple_proj_w', 0): colsh(dwp0),
                       ('mlp_w2', 0): dw2_0, ('mlp_w1', 0): dw1_0})
    dx, dg_mix0, dgn = _ret_layer_bwd(
        dh1, s_ret, row(w['mix_norm'], 0), w['ret_w_in'], w['ret_w_out'], w['ret_gn'], cos_r, sin_r,
        lambda dwro, dwri: emit_group({('ret_w_out', 0): rowsh(dwro), ('ret_w_in', 0): dwri}), deps=deps)

    small = dict(
        mix_norm=[dg_mix0, gm['mix']], mlp_norm=[dg_mlp0, dg_mlp1], ple_norm=[dg_ple0, dg_ple1],
        ret_gn=dgn, mla_q_a_norm=gm['qa'], mla_kv_a_norm=gm['kva'], mla_q_norm=gm['gq'], mla_k_norm=gm['gk'],
    )
    return sq_err, dx, big, small


def _my_place():
    x, y, c = lax.axis_index("x"), lax.axis_index("y"), lax.axis_index("c")
    return x, y, c


def _flat(px, py, pc):
    return 4 * px + 2 * py + pc


def _peer(x, y, c, r):
    return (1 - x if r & 4 else x, 1 - y if r & 2 else y, 1 - c if r & 1 else c)


def _all_gather(arrays):
    n = len(arrays)

    def body(*refs):
        ins, outs = refs[:n], refs[n:2 * n]
        send_sems, recv_sems, local_sems = refs[2 * n:]
        x, y, c = _my_place()
        me, sibling = (x, y, c), (x, y, 1 - c)
        chips = [(1 - x, y), (x, 1 - y), (1 - x, 1 - y)]

        def copy(a, k, block, to, src=None):
            slot = outs[a].at[_flat(*block)]
            return pltpu.make_async_remote_copy(
                src_ref=slot if src is None else src, dst_ref=slot,
                send_sem=send_sems.at[a, k], recv_sem=recv_sems.at[a, k], device_id=to, device_id_type=MESH)

        mine = [pltpu.make_async_copy(ins[a], outs[a].at[_flat(*me)], local_sems.at[a]) for a in range(n)]
        for cp in mine:
            cp.start()
        first = []
        for a in range(n):
            first.append(copy(a, 0, me, sibling, src=ins[a]))
            first += [copy(a, 1 + j, me, (*chip, c), src=ins[a]) for j, chip in enumerate(chips)]
        for cp in first:
            cp.start()
        passed = []
        for a in range(n):
            for j, chip in enumerate(chips):
                copy(a, 1 + j, (*chip, c), me).wait_recv()
                passed.append(copy(a, 4 + j, (*chip, c), sibling))
                passed[-1].start()
        for a in range(n):
            copy(a, 0, sibling, me).wait_recv()
            for j, chip in enumerate(chips):
                copy(a, 4 + j, (*chip, 1 - c), me).wait_recv()
        for cp in first + passed:
            cp.wait_send()
        for cp in mine:
            cp.wait()

    return pl.pallas_call(
        body, name="all_gather_weights",
        in_specs=[ANY] * n, out_specs=[ANY] * n,
        out_shape=[_sds((N_DEV,) + a.shape, a.dtype) for a in arrays],
        scratch_shapes=[pltpu.SemaphoreType.DMA((n, 7)), pltpu.SemaphoreType.DMA((n, 7)),
                        pltpu.SemaphoreType.DMA((n,))],
    )(*arrays)


HBM = pl.BlockSpec(memory_space=pltpu.HBM)
SEMS = pl.BlockSpec(memory_space=pltpu.SEMAPHORE)
SIDE_EFFECT = pltpu.SideEffectType.DATAFLOW_SIDE_EFFECTING


def _rs_copies(x, y, c, srcs, lands, send_sems, recv_sems):
    copies = []
    for a in range(len(srcs)):
        for r in range(1, N_DEV):
            peer = _peer(x, y, c, r)
            k = a * (N_DEV - 1) + r - 1
            copies.append(pltpu.make_async_remote_copy(
                src_ref=srcs[a].at[_flat(*peer)], dst_ref=lands[a].at[r - 1],
                send_sem=send_sems.at[k], recv_sem=recv_sems.at[k], device_id=peer, device_id_type=MESH))
    return copies


def _rs_start(name, arrays):
    n = len(arrays)
    hbm = lambda a: pltpu.with_memory_space_constraint(a, pltpu.HBM)
    lands = [hbm(lax.empty((N_DEV - 1,) + a.shape[1:], a.dtype)) for a in arrays]

    def body(*refs):
        srcs, lnd = refs[:n], refs[n:2 * n]
        send_sems, recv_sems = refs[2 * n], refs[2 * n + 1]
        token = refs[-1]
        for cp in _rs_copies(*_my_place(), srcs, lnd, send_sems, recv_sems):
            cp.start()
        token[...] = jnp.zeros_like(token)

    outs = pl.pallas_call(
        body, name=name,
        in_specs=[HBM] * (2 * n),
        out_specs=[SEMS, SEMS] + [HBM] * (2 * n) + [pl.BlockSpec(memory_space=pltpu.VMEM)],
        out_shape=[pltpu.SemaphoreType.DMA((n * (N_DEV - 1),)), pltpu.SemaphoreType.DMA((n * (N_DEV - 1),))]
        + [pltpu.HBM(a.shape, a.dtype) for a in arrays] + [pltpu.HBM(l.shape, l.dtype) for l in lands]
        + [_sds((8, 128), F32)],
        input_output_aliases={i: 2 + i for i in range(2 * n)},
        compiler_params=pltpu.CompilerParams(has_side_effects=SIDE_EFFECT),
    )(*[hbm(a) for a in arrays], *lands)
    return outs[0], outs[1], outs[2:2 + n], outs[2 + n:2 + 2 * n], outs[-1]


def _rs_wait(name, send_sems, recv_sems, srcs, lands, after):
    n = len(srcs)

    def body(*refs):
        src_refs, lnd = refs[:n], refs[n:2 * n]
        send, recv = refs[2 * n], refs[2 * n + 1]
        for cp in _rs_copies(*_my_place(), src_refs, lnd, send, recv):
            cp.wait_send()
            cp.wait_recv()

    outs = pl.pallas_call(
        body, name=name,
        in_specs=[HBM] * (2 * n) + [SEMS, SEMS] + [ANY] * len(after),
        out_specs=[HBM] * (2 * n),
        out_shape=[pltpu.HBM(a.shape, a.dtype) for a in list(srcs) + list(lands)],
        input_output_aliases={i: i for i in range(2 * n)},
        compiler_params=pltpu.CompilerParams(has_side_effects=SIDE_EFFECT),
    )(*srcs, *lands, send_sems, recv_sems, *after)
    return outs[:n], outs[n:]


SMALL_PACK_ROWS = 16


def _all_reduce_small(rows):
    n = len(rows)

    def body(*refs):
        ins = refs[:n]
        out_ref, mine, buf, send_sems, recv_sems = refs[n:]
        x, y, c = _my_place()
        mine[...] = jnp.zeros_like(mine)
        for (r0, a), ref in zip(rows, ins):
            mine[r0:r0 + a.shape[0], 0:a.shape[1]] = ref[...]
        buf[_flat(x, y, c)] = mine[...]
        copies = []
        for r in range(1, N_DEV):
            peer = _peer(x, y, c, r)
            send = pltpu.make_async_remote_copy(
                src_ref=mine, dst_ref=buf.at[_flat(x, y, c)],
                send_sem=send_sems.at[r - 1], recv_sem=recv_sems.at[r - 1], device_id=peer, device_id_type=MESH)
            send.start()
            recv = pltpu.make_async_remote_copy(
                src_ref=mine, dst_ref=buf.at[_flat(*peer)],
                send_sem=send_sems.at[r - 1], recv_sem=recv_sems.at[r - 1], device_id=peer, device_id_type=MESH)
            copies.append((send, recv))
        for send, recv in copies:
            send.wait_send()
            recv.wait_recv()
        acc = buf[0]
        for s in range(1, N_DEV):
            acc = acc + buf[s]
        out_ref[...] = acc

    vm = pl.BlockSpec(memory_space=pltpu.VMEM)
    shape = (SMALL_PACK_ROWS, D_MODEL)
    return pl.pallas_call(
        body, name="all_reduce_small", in_specs=[vm] * n, out_specs=vm,
        out_shape=_sds(shape, F32),
        scratch_shapes=[pltpu.VMEM(shape, F32), pltpu.VMEM((N_DEV,) + shape, F32),
                        pltpu.SemaphoreType.DMA((7,)), pltpu.SemaphoreType.DMA((7,))],
    )(*[a for _, a in rows])


def _adamw_math(w, g, m, v):
    m = ADAM_B1 * m + (1.0 - ADAM_B1) * g
    v = ADAM_B2 * v + (1.0 - ADAM_B2) * (g * g)
    m_hat = m / (1.0 - ADAM_B1 ** ADAM_STEP)
    v_hat = v / (1.0 - ADAM_B2 ** ADAM_STEP)
    delta = -ADAM_LR * (m_hat / (jnp.sqrt(v_hat) + ADAM_EPS) + ADAM_WD * w)
    return delta, m, v


def _adamw_big(name, w, m, v, srcs, lands, me):
    nl, rows, cols = w.shape
    tr = next(cand for cand in (256, 128, 64, 32, 16, 8) if rows % cand == 0)

    def body(me_ref, w_ref, m_ref, v_ref, *rest):
        src_refs, land_refs = rest[:nl], rest[nl:2 * nl]
        g_ref, d_ref, mo_ref, vo_ref = rest[2 * nl:]
        for layer in range(nl):
            @pl.when(pl.program_id(0) == layer)
            def _():
                g = src_refs[layer][...].astype(F32)
                for s in range(N_DEV - 1):
                    g = g + land_refs[layer][s].astype(F32)
                delta, mn, vn = _adamw_math(w_ref[...], g, m_ref[...], v_ref[...])
                g_ref[...] = g
                d_ref[...] = delta
                mo_ref[...] = mn
                vo_ref[...] = vn

    blk = pl.BlockSpec((None, tr, cols), lambda l, i, me_ref: (l, i, 0))
    own = pl.BlockSpec((None, tr, cols), lambda l, i, me_ref: (me_ref[0], i, 0))
    peers = pl.BlockSpec((N_DEV - 1, tr, cols), lambda l, i, me_ref: (0, i, 0))
    return pl.pallas_call(
        body, name=name,
        grid_spec=pltpu.PrefetchScalarGridSpec(
            num_scalar_prefetch=1, grid=(nl, rows // tr),
            in_specs=[blk, blk, blk] + [own] * nl + [peers] * nl, out_specs=[blk] * 4),
        out_shape=[_sds((nl, rows, cols), F32)] * 4,
        compiler_params=_cparams(("arbitrary", "arbitrary")),
    )(me, w, m, v, *srcs, *lands)


def _adamw_small(ws, gs, ms, vs):
    n = len(ws)

    def body(*refs):
        w_refs, g_refs, m_refs, v_refs = (refs[i * n:(i + 1) * n] for i in range(4))
        d_out, m_out, v_out = (refs[(4 + i) * n:(5 + i) * n] for i in range(3))
        for i in range(n):
            delta, mn, vn = _adamw_math(w_refs[i][...], g_refs[i][...], m_refs[i][...], v_refs[i][...])
            d_out[i][...] = delta
            m_out[i][...] = mn
            v_out[i][...] = vn

    vm = pl.BlockSpec(memory_space=pltpu.VMEM)
    outs = pl.pallas_call(
        body, name="adamw_small", in_specs=[vm] * (4 * n), out_specs=[vm] * (3 * n),
        out_shape=[_sds(a.shape, F32) for a in ws] * 3,
    )(*ws, *gs, *ms, *vs)
    return outs[:n], outs[n:2 * n], outs[2 * n:]


SMALL_ROWS = 16


def _pad_to(a, rows, cols):
    return jnp.pad(a, ((0, rows - a.shape[0]), (0, cols - a.shape[1])))


def _place_own(blocks):
    n = len(blocks)

    def body(*refs):
        ins, outs, sems = refs[:n], refs[n:2 * n], refs[2 * n]
        me = _flat(*_my_place())
        copies = [pltpu.make_async_copy(ins[a], outs[a].at[me], sems.at[a]) for a in range(n)]
        for cp in copies:
            cp.start()
        for cp in copies:
            cp.wait()

    return pl.pallas_call(
        body, name="gather_place_own", in_specs=[ANY] * n, out_specs=[ANY] * n,
        out_shape=[_sds((N_DEV,) + b.shape, b.dtype) for b in blocks],
        scratch_shapes=[pltpu.SemaphoreType.DMA((n,))],
    )(*blocks)


def _ag_copies(x, y, c, blocks, bufs, send_sems, recv_sems):
    sends, recvs = [], []
    for a in range(len(blocks)):
        for r in range(1, N_DEV):
            peer = _peer(x, y, c, r)
            k = a * (N_DEV - 1) + r - 1
            make = lambda place: pltpu.make_async_remote_copy(
                src_ref=blocks[a], dst_ref=bufs[a].at[_flat(*place)],
                send_sem=send_sems.at[k], recv_sem=recv_sems.at[k], device_id=peer, device_id_type=MESH)
            sends.append(make((x, y, c)))
            recvs.append(make(peer))
    return sends, recvs


def _ag_start(groups, after):
    flat = [pair for g in groups for pair in g]
    n, ng = len(flat), len(groups)
    hbm = lambda a: pltpu.with_memory_space_constraint(a, pltpu.HBM)

    def body(*refs):
        blocks, bufs = refs[:n], refs[n:2 * n]
        sems = refs[2 * n + len(after):2 * n + len(after) + 2 * ng]
        x, y, c = _my_place()
        at = 0
        for gi, g in enumerate(groups):
            sends, _ = _ag_copies(x, y, c, blocks[at:at + len(g)], bufs[at:at + len(g)], sems[2 * gi], sems[2 * gi + 1])
            for cp in sends:
                cp.start()
            at += len(g)
        refs[-1][...] = jnp.zeros_like(refs[-1])

    sem_shapes = [pltpu.SemaphoreType.DMA((len(g) * (N_DEV - 1),)) for g in groups for _ in range(2)]
    outs = pl.pallas_call(
        body, name="gather_start",
        in_specs=[HBM] * (2 * n) + [ANY] * len(after),
        out_specs=[SEMS] * (2 * ng) + [HBM] * (2 * n) + [pl.BlockSpec(memory_space=pltpu.VMEM)],
        out_shape=sem_shapes + [pltpu.HBM(b.shape, b.dtype) for b, _ in flat]
        + [pltpu.HBM(u.shape, u.dtype) for _, u in flat] + [_sds((8, 128), F32)],
        input_output_aliases={i: 2 * ng + i for i in range(2 * n)},
        compiler_params=pltpu.CompilerParams(has_side_effects=SIDE_EFFECT),
    )(*[hbm(b) for b, _ in flat], *[hbm(u) for _, u in flat], *after)
    blocks_thru, bufs_thru = outs[2 * ng:2 * ng + n], outs[2 * ng + n:2 * ng + 2 * n]
    started, at = [], 0
    for gi, g in enumerate(groups):
        started.append((outs[2 * gi], outs[2 * gi + 1], blocks_thru[at:at + len(g)], bufs_thru[at:at + len(g)]))
        at += len(g)
    return started, outs[-1]


def _ag_wait(name, send_sems, recv_sems, blocks, bufs, after):
    n = len(blocks)

    def body(*refs):
        sends, recvs = _ag_copies(*_my_place(), refs[:n], refs[n:2 * n], refs[2 * n], refs[2 * n + 1])
        for s, r in zip(sends, recvs):
            s.wait_send()
            r.wait_recv()

    outs = pl.pallas_call(
        body, name=name,
        in_specs=[HBM] * (2 * n) + [SEMS, SEMS] + [ANY] * len(after),
        out_specs=[HBM] * (2 * n),
        out_shape=[pltpu.HBM(a.shape, a.dtype) for a in list(blocks) + list(bufs)],
        input_output_aliases={i: i for i in range(2 * n)},
        compiler_params=pltpu.CompilerParams(has_side_effects=SIDE_EFFECT),
    )(*blocks, *bufs, send_sems, recv_sems, *after)
    return outs[n:]


def _prepare_weights(p):
    n = N_DEV
    bf = lambda a: a.astype(BF16)
    gn_pack = jnp.concatenate([
        _pad_to(p['ret_gn'][0], RET_HEADS, 128), _pad_to(p['mla_q_a_norm'], 1, 128),
        _pad_to(p['mla_kv_a_norm'], 1, 128), jnp.zeros((2, 128), F32)], axis=0)
    layer = lambda l: [bf(p['mlp_w1'][l]), bf(p['mlp_w2'][l]), bf(p['ple_gate_w'][l]), bf(p['ple_proj_w'][l])]
    later = [layer(0), [bf(p['mla_w_in'][0]), bf(p['mla_w_uq'][0]), bf(p['mla_w_ukv'][0]), bf(p['mla_w_out'][0])],
             layer(1)]
    bufs = _place_own([b for g in later for b in g])
    pack, wri, wro = _all_gather([gn_pack, bf(p['ret_w_in'][0]), bf(p['ret_w_out'][0])])
    groups, at = [], 0
    for g in later:
        groups.append(list(zip(g, bufs[at:at + len(g)])))
        at += len(g)
    started, token = _ag_start(groups, (wri,))

    w = {k: p[k] for k in ('mix_norm', 'mlp_norm', 'ple_norm')}
    w['ret_gn'] = pack[:, :RET_HEADS, :RET_DV // n].transpose(1, 0, 2).reshape(RET_HEADS, RET_DV)
    w['mla_q_a_norm'] = pack[:, RET_HEADS, :MLA_Q_RANK // n].reshape(1, MLA_Q_RANK)
    w['mla_kv_a_norm'] = pack[:, RET_HEADS + 1, :MLA_KV_RANK // n].reshape(1, MLA_KV_RANK)
    w['ret_w_in'] = wri
    w['ret_w_out'] = wro.reshape(RET_V_W, D_MODEL)
    w['mla_q_norm'] = _pad_to(p['mla_q_norm'], 1, MLA_HD_PAD)
    w['mla_k_norm'] = _pad_to(p['mla_k_norm'], 1, MLA_HD_PAD)
    w['deps'] = (token,)

    def fetch(name, after):
        gi = ('layer0', 'mla', 'layer1').index(name)
        got = _ag_wait("gather_wait_" + name, *started[gi], after)
        if name == 'mla':
            wmi, wuq, wukv, wmo = got
            return dict(mla_w_in=jnp.pad(wmi.reshape(D_MODEL, MLA_IN), ((0, 0), (0, MLA_IN_PAD - MLA_IN))),
                        mla_w_uq=jnp.pad(wuq, ((0, 0), (0, 0), (0, MLA_HD_PAD - MLA_QKD))),
                        mla_w_ukv=wukv, mla_w_out=wmo.reshape(D_MODEL, D_MODEL))
        w1, w2, wg, wp = got
        return dict(mlp_w1=w1, mlp_w2=w2, ple_gate_w=wg.reshape(D_MODEL, D_MODEL),
                    ple_proj_w=wp.transpose(1, 0, 2).reshape(PLE_DIM, D_MODEL))

    return w, fetch


def _small_grads(small):
    rows = [(0, small['mix_norm'][0]), (1, small['mix_norm'][1]), (2, small['mlp_norm'][0]),
            (3, small['mlp_norm'][1]), (4, small['ple_norm'][0]), (5, small['ple_norm'][1]),
            (6, small['ret_gn']), (10, small['mla_q_a_norm']), (11, small['mla_kv_a_norm']),
            (12, small['mla_q_norm']), (13, small['mla_k_norm'])]
    gs = _all_reduce_small(rows)
    me = _flat(*_my_place())
    n = N_DEV
    return dict(
        mix_norm=gs[0:2], mlp_norm=gs[2:4], ple_norm=gs[4:6],
        ret_gn=lax.dynamic_slice(gs, (6, me * (RET_DV // n)), (RET_HEADS, RET_DV // n)),
        mla_q_a_norm=lax.dynamic_slice(gs, (10, me * (MLA_Q_RANK // n)), (1, MLA_Q_RANK // n)),
        mla_kv_a_norm=lax.dynamic_slice(gs, (11, me * (MLA_KV_RANK // n)), (1, MLA_KV_RANK // n)),
        mla_q_norm=gs[12:13, :MLA_QKD], mla_k_norm=gs[13:14, :MLA_QKD])


def kernel(x, p, mix_norm, ret_w_in, ret_gn, ret_w_out, mla_w_in, mla_q_a_norm, mla_kv_a_norm, mla_w_uq, mla_w_ukv, mla_q_norm, mla_k_norm, mla_w_out, mlp_norm, mlp_w1, mlp_w2, ple_norm, ple_gate_w, ple_proj_w, loss_target, m_mix_norm, m_ret_w_in, m_ret_gn, m_ret_w_out, m_mla_w_in, m_mla_q_a_norm, m_mla_kv_a_norm, m_mla_w_uq, m_mla_w_ukv, m_mla_q_norm, m_mla_k_norm, m_mla_w_out, m_mlp_norm, m_mlp_w1, m_mlp_w2, m_ple_norm, m_ple_gate_w, m_ple_proj_w, v_mix_norm, v_ret_w_in, v_ret_gn, v_ret_w_out, v_mla_w_in, v_mla_q_a_norm, v_mla_kv_a_norm, v_mla_w_uq, v_mla_w_ukv, v_mla_q_norm, v_mla_k_norm, v_mla_w_out, v_mlp_norm, v_mlp_w1, v_mlp_w2, v_ple_norm, v_ple_gate_w, v_ple_proj_w):
    given = dict(locals())
    params = {n: given[n] for n in WEIGHTS}
    w, fetch = _prepare_weights(params)

    started = []

    def emit(group):
        keys = list(group)
        send, recv, srcs, lands, token = _rs_start(f"rs_start{len(started)}", [group[k] for k in keys])
        started.append((keys, send, recv, srcs, lands))
        return (token,)

    sq_err, grad_x, _, small = _local_step(x[0], p, loss_target[0], w, fetch, emit)
    loss = lax.psum(0.5 / D_MODEL * sq_err[0, 0], ("x", "y", "c"))

    grads, deltas, new_m, new_v = {}, {}, {}, {}
    sg = _small_grads(small)
    two_d = lambda a: a.reshape(-1, a.shape[-1])
    d_s, m_s, v_s = _adamw_small([two_d(params[n]) for n in SMALL], [sg[n] for n in SMALL],
                                 [two_d(given["m_" + n]) for n in SMALL], [two_d(given["v_" + n]) for n in SMALL])
    for i, n in enumerate(SMALL):
        shape = params[n].shape
        grads[n], deltas[n], new_m[n], new_v[n] = (a.reshape(shape) for a in (sg[n], d_s[i], m_s[i], v_s[i]))

    me = _flat(*_my_place()).astype(jnp.int32).reshape(1)
    after = (grad_x, d_s[0])
    src_of, land_of = {}, {}
    for gi, (keys, send, recv, srcs, lands) in enumerate(started):
        srcs, lands = _rs_wait(f"rs_wait{gi}", send, recv, srcs, lands, after)
        for k, s, l in zip(keys, srcs, lands):
            src_of[k], land_of[k] = s, l
        done = [n for n in BIG if n not in grads and all((n, l) in src_of for l in range(params[n].shape[0]))]
        for n in done:
            layers = range(params[n].shape[0])
            grads[n], deltas[n], new_m[n], new_v[n] = _adamw_big(
                "adamw_" + n, params[n], given["m_" + n], given["v_" + n],
                [src_of[(n, l)] for l in layers], [land_of[(n, l)] for l in layers], me)
        if done:
            after = (deltas[done[-1]],)

    return (loss, grad_x[None], *[grads[n] for n in WEIGHTS], *[deltas[n] for n in WEIGHTS],
            *[new_m[n] for n in WEIGHTS], *[new_v[n] for n in WEIGHTS])
```

```python
import functools
import math

import jax
import jax.numpy as jnp
from jax import lax
from jax.experimental import pallas as pl
from jax.experimental.pallas import tpu as pltpu

F32 = jnp.float32
BF16 = jnp.bfloat16
MESH = pl.DeviceIdType.MESH
ANY = pl.BlockSpec(memory_space=pl.ANY)

N_DEV = 8
D_MODEL = 1024
CHUNK = 64
EPS = 1e-6
ROPE_THETA = 10000.0
RET_HEADS = 4
RET_DK = 256
RET_DV = 512
RET_QK_W = RET_HEADS * RET_DK
RET_V_W = RET_HEADS * RET_DV
RET_IN = 2 * RET_QK_W + 2 * RET_V_W
MLA_HEADS = 8
MLA_NOPE = 128
MLA_ROPE = 64
MLA_QKD = MLA_NOPE + MLA_ROPE
MLA_VD = 128
MLA_Q_RANK = 384
MLA_KV_RANK = 256
MLA_IN = MLA_Q_RANK + MLA_KV_RANK + MLA_ROPE
MLA_IN_PAD = 768
MLA_HD_PAD = 256
D_FF = 4096
PLE_DIM = 256
ATT_SCALE = MLA_QKD ** -0.5
LOG2E = 1.4426950408889634
ATT_EXP2 = ATT_SCALE * LOG2E

ADAM_LR = 0.001
ADAM_B1 = 0.9
ADAM_B2 = 0.999
ADAM_EPS = 1e-08
ADAM_WD = 0.01
ADAM_STEP = 10

VMEM_LIMIT = 52 * 1024 * 1024
ROW_TILE = 1024
RET_ROWS = 256
ATT_BLOCK = 256
ATT_QROWS = 512
ATT_HEADS = 2

WEIGHTS = ['mix_norm', 'ret_w_in', 'ret_gn', 'ret_w_out', 'mla_w_in', 'mla_q_a_norm', 'mla_kv_a_norm',
           'mla_w_uq', 'mla_w_ukv', 'mla_q_norm', 'mla_k_norm', 'mla_w_out', 'mlp_norm', 'mlp_w1', 'mlp_w2',
           'ple_norm', 'ple_gate_w', 'ple_proj_w']
BIG = ['ret_w_in', 'ret_w_out', 'mla_w_in', 'mla_w_uq', 'mla_w_ukv', 'mla_w_out', 'mlp_w1', 'mlp_w2',
       'ple_gate_w', 'ple_proj_w']
SMALL = [w for w in WEIGHTS if w not in BIG]


def _cparams(sem=None):
    return pltpu.CompilerParams(dimension_semantics=sem, vmem_limit_bytes=VMEM_LIMIT)


def _dot(a, b, ca, cb):
    return lax.dot_general(a, b, (((ca,), (cb,)), ((), ())), preferred_element_type=F32)


def _bf(v):
    return v if v.dtype == BF16 else v.astype(BF16)


def _sigmoid(z):
    return 1.0 / (1.0 + jnp.exp(-z))


def _mm(name, grid, a, a_spec, b, b_spec, contract, outs, extras=(), epi=None, deps=()):
    nk = grid[2]
    n_ex, n_out, n_dep = len(extras), len(outs), len(deps)
    acc_shape = tuple(d for d in outs[0][1].block_shape if d is not None)

    def body(*refs):
        a_ref, b_ref = refs[:2]
        ex_refs = refs[2:2 + n_ex]
        out_refs = refs[2 + n_ex + n_dep:2 + n_ex + n_dep + n_out]

        def product():
            return _dot(_bf(a_ref[...]), _bf(b_ref[...]), contract[0], contract[1])

        def finish(acc):
            res = epi(acc, *[r[...] for r in ex_refs]) if epi is not None else (acc,)
            for o, r in zip(out_refs, res):
                o[...] = r.astype(o.dtype)

        if nk == 1:
            finish(product())
        else:
            acc_ref = refs[-1]
            k = pl.program_id(2)

            @pl.when(k == 0)
            def _():
                acc_ref[...] = jnp.zeros_like(acc_ref)

            acc_ref[...] += product()

            @pl.when(k == nk - 1)
            def _():
                finish(acc_ref[...])

    return pl.pallas_call(
        body, name=name, grid=grid,
        in_specs=[a_spec, b_spec] + [s for _, s in extras] + [ANY] * n_dep,
        out_specs=[s for _, s in outs],
        out_shape=[s for s, _ in outs],
        scratch_shapes=[pltpu.VMEM(acc_shape, F32)] if nk > 1 else [],
        compiler_params=_cparams(("parallel", "parallel", "arbitrary")),
    )(a, b, *[x for x, _ in extras], *deps)


def _sds(shape, dtype):
    return jax.ShapeDtypeStruct(shape, dtype)


def _row_tile(t, cap=ROW_TILE):
    return min(cap, t)


def _rms_fwd(name, x, g):
    t, d = x.shape
    tm = _row_tile(t)

    def body(x_ref, g_ref, o_ref):
        xv = x_ref[...]
        r = lax.rsqrt(jnp.mean(xv * xv, axis=-1, keepdims=True) + EPS)
        o_ref[...] = (xv * r * g_ref[...]).astype(o_ref.dtype)

    return pl.pallas_call(
        body, name=name, grid=(t // tm,),
        in_specs=[pl.BlockSpec((tm, d), lambda i: (i, 0)), pl.BlockSpec((1, d), lambda i: (0, 0))],
        out_specs=pl.BlockSpec((tm, d), lambda i: (i, 0)),
        out_shape=_sds((t, d), BF16),
        compiler_params=_cparams(("parallel",)),
    )(x, g)


def _rms_bwd_rows(dy, xv, g, n):
    r = lax.rsqrt(jnp.sum(xv * xv, axis=-1, keepdims=True) / n + EPS)
    xh = xv * r
    dxh = dy * g
    dx = r * (dxh - xh * (jnp.sum(dxh * xh, axis=-1, keepdims=True) / n))
    return dx, dy * xh


def _rms_bwd(name, dy, x, g, res):
    t, d = x.shape
    tm = _row_tile(t, 512)

    def body(dy_ref, x_ref, g_ref, res_ref, dx_ref, dg_ref):
        @pl.when(pl.program_id(0) == 0)
        def _():
            dg_ref[...] = jnp.zeros_like(dg_ref)

        dx, dgr = _rms_bwd_rows(dy_ref[...], x_ref[...], g_ref[...], d)
        dx_ref[...] = res_ref[...] + dx
        dg_ref[...] += jnp.sum(dgr, axis=0, keepdims=True)

    row = pl.BlockSpec((tm, d), lambda i: (i, 0))
    vec = pl.BlockSpec((1, d), lambda i: (0, 0))
    return pl.pallas_call(
        body, name=name, grid=(t // tm,),
        in_specs=[row, row, vec, row], out_specs=[row, vec],
        out_shape=[_sds((t, d), F32), _sds((1, d), F32)],
        compiler_params=_cparams(("arbitrary",)),
    )(dy, x, g, res)


def _loss_head(y, target):
    t, d = y.shape
    tm = _row_tile(t)

    def body(y_ref, t_ref, dy_ref, l_ref):
        @pl.when(pl.program_id(0) == 0)
        def _():
            l_ref[...] = jnp.zeros_like(l_ref)

        e = y_ref[...] - t_ref[...]
        dy_ref[...] = e / d
        l_ref[...] += jnp.sum(jnp.sum(e * e, axis=-1, keepdims=True), axis=0, keepdims=True)

    row = pl.BlockSpec((tm, d), lambda i: (i, 0))
    return pl.pallas_call(
        body, name="loss_head", grid=(t // tm,),
        in_specs=[row, row], out_specs=[row, pl.BlockSpec((8, 128), lambda i: (0, 0))],
        out_shape=[_sds((t, d), F32), _sds((8, 128), F32)],
        compiler_params=_cparams(("arbitrary",)),
    )(y, target)


def _ple_gate_bwd(name, dh, gate, e):
    t, d = dh.shape
    tm = _row_tile(t)

    def body(dh_ref, g_ref, e_ref, de_ref, dz_ref):
        dh_v, gt = dh_ref[...], g_ref[...]
        de_ref[...] = (dh_v * gt).astype(BF16)
        dz_ref[...] = (dh_v * e_ref[...] * (gt * (1.0 - gt))).astype(BF16)

    row = pl.BlockSpec((tm, d), lambda i: (i, 0))
    return pl.pallas_call(
        body, name=name, grid=(t // tm,), in_specs=[row, row, row], out_specs=[row, row],
        out_shape=[_sds((t, d), BF16), _sds((t, d), BF16)],
        compiler_params=_cparams(("parallel",)),
    )(dh, gate, e)


def _rope_half(v, cos, sin):
    half = v.shape[-1] // 2
    v1, v2 = v[:, :half], v[:, half:]
    return jnp.concatenate([v1 * cos - v2 * sin, v2 * cos + v1 * sin], axis=-1)


def _ret_consts():
    lg = jnp.log(1.0 - 2.0 ** (-5.0 - jnp.arange(RET_HEADS, dtype=F32)))
    idx = jnp.arange(CHUNK, dtype=F32)
    intra = jnp.exp(lg[:, None, None] * jnp.abs(idx[:, None] - idx[None, :]))
    qdec = jnp.exp(lg[:, None] * (idx + 1.0))
    kdec = jnp.exp(lg[:, None] * (CHUNK - 1.0 - idx))
    cdec = jnp.exp(lg * CHUNK)
    qdec = jnp.broadcast_to(qdec[:, :, None], (RET_HEADS, CHUNK, RET_DK))
    kdec = jnp.broadcast_to(kdec[:, :, None], (RET_HEADS, CHUNK, RET_DK))
    cdec = jnp.broadcast_to(cdec[:, None, None], (RET_HEADS, 1, RET_DV))
    return intra, qdec, kdec, cdec


def _ret_specs(rb, rev_nb=None):
    blk = (lambda i: i) if rev_nb is None else (lambda i: rev_nb - 1 - i)
    full = lambda shape: pl.BlockSpec(shape, lambda i: (0,) * len(shape))
    return dict(
        proj=pl.BlockSpec((rb, RET_IN), lambda i: (blk(i), 0)),
        tab=pl.BlockSpec((rb, RET_DK // 2), lambda i: (blk(i), 0)),
        vw=pl.BlockSpec((rb, RET_V_W), lambda i: (blk(i), 0)),
        st=pl.BlockSpec((rb // CHUNK, RET_HEADS, RET_DK, RET_DV), lambda i: (blk(i), 0, 0, 0)),
        gn=full((RET_HEADS, 1, RET_DV)),
        intra=full((RET_HEADS, CHUNK, CHUNK)),
        dec=full((RET_HEADS, CHUNK, RET_DK)),
        cdec=full((RET_HEADS, 1, RET_DV)),
    )


def _ret_fwd(proj, cos, sin, gn):
    t = proj.shape[0]
    rb = min(RET_ROWS, t)
    cpb = rb // CHUNK
    intra, qdec, kdec, cdec = _ret_consts()
    sp = _ret_specs(rb)

    def body(proj_ref, cos_ref, sin_ref, gn_ref, intra_ref, qd_ref, kd_ref, cd_ref,
             gated_ref, outp_ref, st_ref, s_ref):
        @pl.when(pl.program_id(0) == 0)
        def _():
            s_ref[...] = jnp.zeros_like(s_ref)

        def chunk(c, carry):
            rows = pl.ds(pl.multiple_of(c * CHUNK, CHUNK), CHUNK)
            cs, sn = cos_ref[rows, :], sin_ref[rows, :]
            for h in range(RET_HEADS):
                q = proj_ref[rows, h * RET_DK:(h + 1) * RET_DK]
                k = proj_ref[rows, RET_QK_W + h * RET_DK:RET_QK_W + (h + 1) * RET_DK]
                v = proj_ref[rows, 2 * RET_QK_W + h * RET_DV:2 * RET_QK_W + (h + 1) * RET_DV]
                g = proj_ref[rows, 2 * RET_QK_W + RET_V_W + h * RET_DV:2 * RET_QK_W + RET_V_W + (h + 1) * RET_DV]
                qr = _rope_half(q, cs, sn)
                kr = _rope_half(k, cs, sn) * (RET_DK ** -0.5)
                qb, kb, vb = qr.astype(BF16), kr.astype(BF16), v.astype(BF16)
                sc = _dot(qb, kb, 1, 1) * intra_ref[h]
                inner = _dot(sc.astype(BF16), vb, 1, 0)
                s_old = s_ref[h]
                sb = s_old.astype(BF16)
                st_ref[c, h] = sb
                cross = _dot((qr * qd_ref[h]).astype(BF16), sb, 1, 0)
                out = inner + cross
                s_ref[h] = s_old * cd_ref[h] + _dot((kr * kd_ref[h]).astype(BF16), vb, 0, 0)
                r = lax.rsqrt(jnp.mean(out * out, axis=-1, keepdims=True) + EPS)
                y = out * r * gn_ref[h]
                cols = slice(h * RET_DV, (h + 1) * RET_DV)
                gated_ref[rows, cols] = (g * _sigmoid(g) * y).astype(BF16)
                outp_ref[rows, cols] = out
            return carry

        lax.fori_loop(0, cpb, chunk, 0)

    return pl.pallas_call(
        body, name="ret_fwd", grid=(t // rb,),
        in_specs=[sp['proj'], sp['tab'], sp['tab'], sp['gn'], sp['intra'], sp['dec'], sp['dec'], sp['cdec']],
        out_specs=[sp['vw'], sp['vw'], sp['st']],
        out_shape=[_sds((t, RET_V_W), BF16), _sds((t, RET_V_W), F32),
                   _sds((t // CHUNK, RET_HEADS, RET_DK, RET_DV), BF16)],
        scratch_shapes=[pltpu.VMEM((RET_HEADS, RET_DK, RET_DV), F32)],
        compiler_params=_cparams(("arbitrary",)),
    )(proj, cos, sin, gn.reshape(RET_HEADS, 1, RET_DV), intra, qdec, kdec, cdec)


def _ret_bwd(proj, cos, sin, gn, outp, states, dgated):
    t = proj.shape[0]
    rb = min(RET_ROWS, t)
    cpb = rb // CHUNK
    nb = t // rb
    intra, qdec, kdec, cdec = _ret_consts()
    sp = _ret_specs(rb, rev_nb=nb)

    def body(proj_ref, cos_ref, sin_ref, gn_ref, intra_ref, qd_ref, kd_ref, cd_ref, outp_ref, st_ref, dgt_ref,
             dproj_ref, dgn_ref, ds_ref):
        @pl.when(pl.program_id(0) == 0)
        def _():
            ds_ref[...] = jnp.zeros_like(ds_ref)
            dgn_ref[...] = jnp.zeros_like(dgn_ref)

        def chunk(cc, carry):
            c = cpb - 1 - cc
            rows = pl.ds(pl.multiple_of(c * CHUNK, CHUNK), CHUNK)
            cs, sn = cos_ref[rows, :], sin_ref[rows, :]
            for h in range(RET_HEADS):
                q = proj_ref[rows, h * RET_DK:(h + 1) * RET_DK]
                k = proj_ref[rows, RET_QK_W + h * RET_DK:RET_QK_W + (h + 1) * RET_DK]
                v = proj_ref[rows, 2 * RET_QK_W + h * RET_DV:2 * RET_QK_W + (h + 1) * RET_DV]
                g = proj_ref[rows, 2 * RET_QK_W + RET_V_W + h * RET_DV:2 * RET_QK_W + RET_V_W + (h + 1) * RET_DV]
                cols = slice(h * RET_DV, (h + 1) * RET_DV)
                qr = _rope_half(q, cs, sn)
                kr = _rope_half(k, cs, sn) * (RET_DK ** -0.5)
                qb, kb, vb = qr.astype(BF16), kr.astype(BF16), v.astype(BF16)
                qdb = (qr * qd_ref[h]).astype(BF16)
                kdb = (kr * kd_ref[h]).astype(BF16)
                out = outp_ref[rows, cols]
                dgt = dgt_ref[rows, cols]
                gnh = gn_ref[h]
                r = lax.rsqrt(jnp.mean(out * out, axis=-1, keepdims=True) + EPS)
                xh = out * r
                sg = _sigmoid(g)
                dgate = dgt * (xh * gnh) * (sg * (1.0 + g * (1.0 - sg)))
                dy = dgt * (g * sg)
                dgn_ref[h] += jnp.sum(dy * xh, axis=0, keepdims=True)
                dxh = dy * gnh
                dout = r * (dxh - xh * jnp.mean(dxh * xh, axis=-1, keepdims=True))
                doutb = dout.astype(BF16)
                itr = intra_ref[h]
                pb = (_dot(qb, kb, 1, 1) * itr).astype(BF16)
                dv = _dot(pb, doutb, 0, 0)
                dsc = (_dot(doutb, vb, 1, 1) * itr).astype(BF16)
                dq = _dot(dsc, kb, 1, 0)
                dk = _dot(dsc, qb, 0, 0)
                dq = dq + _dot(doutb, st_ref[c, h], 1, 1) * qd_ref[h]
                ds_new = ds_ref[h]
                dsb = ds_new.astype(BF16)
                dk = dk + _dot(vb, dsb, 1, 1) * kd_ref[h]
                dv = dv + _dot(kdb, dsb, 1, 0)
                ds_ref[h] = ds_new * cd_ref[h] + _dot(qdb, doutb, 0, 0)
                dproj_ref[rows, h * RET_DK:(h + 1) * RET_DK] = _rope_half(dq, cs, -sn).astype(BF16)
                dproj_ref[rows, RET_QK_W + h * RET_DK:RET_QK_W + (h + 1) * RET_DK] = (
                    _rope_half(dk * (RET_DK ** -0.5), cs, -sn).astype(BF16))
                dproj_ref[rows, 2 * RET_QK_W + h * RET_DV:2 * RET_QK_W + (h + 1) * RET_DV] = dv.astype(BF16)
                dproj_ref[rows, 2 * RET_QK_W + RET_V_W + h * RET_DV:
                          2 * RET_QK_W + RET_V_W + (h + 1) * RET_DV] = dgate.astype(BF16)
            return carry

        lax.fori_loop(0, cpb, chunk, 0)

    return pl.pallas_call(
        body, name="ret_bwd", grid=(nb,),
        in_specs=[sp['proj'], sp['tab'], sp['tab'], sp['gn'], sp['intra'], sp['dec'], sp['dec'], sp['cdec'],
                  sp['vw'], sp['st'], sp['vw']],
        out_specs=[sp['proj'], sp['gn']],
        out_shape=[_sds((t, RET_IN), BF16), _sds((RET_HEADS, 1, RET_DV), F32)],
        scratch_shapes=[pltpu.VMEM((RET_HEADS, RET_DK, RET_DV), F32)],
        compiler_params=_cparams(("arbitrary",)),
    )(proj, cos, sin, gn.reshape(RET_HEADS, 1, RET_DV), intra, qdec, kdec, cdec, outp, states, dgated)


def _mla_tables(t):
    half = MLA_ROPE // 2
    inv = 1.0 / (ROPE_THETA ** (jnp.arange(0, MLA_ROPE, 2, dtype=F32) / MLA_ROPE))
    ang = jnp.arange(t, dtype=F32)[:, None] * inv[None, :]
    cos, sin = jnp.cos(ang), jnp.sin(ang)
    z = jnp.zeros((t, half), F32)
    c = jnp.concatenate([cos, cos, z, z], axis=1)
    s1 = jnp.concatenate([-sin, z, z, z], axis=1)
    s2 = jnp.concatenate([z, sin, z, z], axis=1)
    return c, s1, s2


def _rope_tile(r, c, s1, s2):
    return r * c + pltpu.roll(r, 96, 1) * s1 + pltpu.roll(r, 32, 1) * s2


def _mla_mid(proj2, qa, kva):
    t = proj2.shape[0]
    tm = _row_tile(t)

    def body(p_ref, qa_ref, kva_ref, cq_ref, ckv_ref):
        cq = p_ref[:, :MLA_Q_RANK]
        ckv = p_ref[:, MLA_Q_RANK:MLA_Q_RANK + MLA_KV_RANK]
        rq = lax.rsqrt(jnp.mean(cq * cq, axis=-1, keepdims=True) + EPS)
        rkv = lax.rsqrt(jnp.mean(ckv * ckv, axis=-1, keepdims=True) + EPS)
        cq_ref[...] = (cq * rq * qa_ref[...]).astype(BF16)
        ckv_ref[...] = (ckv * rkv * kva_ref[...]).astype(BF16)

    return pl.pallas_call(
        body, name="mla_mid", grid=(t // tm,),
        in_specs=[pl.BlockSpec((tm, MLA_IN_PAD), lambda i: (i, 0)),
                  pl.BlockSpec((1, MLA_Q_RANK), lambda i: (0, 0)),
                  pl.BlockSpec((1, MLA_KV_RANK), lambda i: (0, 0))],
        out_specs=[pl.BlockSpec((tm, MLA_Q_RANK), lambda i: (i, 0)),
                   pl.BlockSpec((tm, MLA_KV_RANK), lambda i: (i, 0))],
        out_shape=[_sds((t, MLA_Q_RANK), BF16), _sds((t, MLA_KV_RANK), BF16)],
        compiler_params=_cparams(("parallel",)),
    )(proj2, qa, kva)


def _mla_mid_bwd(proj2, qa, kva, dcq, dckv, dkr):
    t = proj2.shape[0]
    tm = _row_tile(t)

    def body(p_ref, qa_ref, kva_ref, dcq_ref, dckv_ref, dkr_ref, dp_ref, dqa_ref, dkva_ref):
        @pl.when(pl.program_id(0) == 0)
        def _():
            dqa_ref[...] = jnp.zeros_like(dqa_ref)
            dkva_ref[...] = jnp.zeros_like(dkva_ref)

        dxq, dgq = _rms_bwd_rows(dcq_ref[...], p_ref[:, :MLA_Q_RANK], qa_ref[...], MLA_Q_RANK)
        dxk, dgk = _rms_bwd_rows(dckv_ref[...], p_ref[:, MLA_Q_RANK:MLA_Q_RANK + MLA_KV_RANK], kva_ref[...],
                                 MLA_KV_RANK)
        dp_ref[:, :MLA_Q_RANK] = dxq.astype(BF16)
        dp_ref[:, MLA_Q_RANK:MLA_Q_RANK + MLA_KV_RANK] = dxk.astype(BF16)
        dp_ref[:, MLA_Q_RANK + MLA_KV_RANK:] = dkr_ref[...].astype(BF16)
        dqa_ref[...] += jnp.sum(dgq, axis=0, keepdims=True)
        dkva_ref[...] += jnp.sum(dgk, axis=0, keepdims=True)

    return pl.pallas_call(
        body, name="mla_mid_bwd", grid=(t // tm,),
        in_specs=[pl.BlockSpec((tm, MLA_IN_PAD), lambda i: (i, 0)),
                  pl.BlockSpec((1, MLA_Q_RANK), lambda i: (0, 0)),
                  pl.BlockSpec((1, MLA_KV_RANK), lambda i: (0, 0)),
                  pl.BlockSpec((tm, MLA_Q_RANK), lambda i: (i, 0)),
                  pl.BlockSpec((tm, MLA_KV_RANK), lambda i: (i, 0)),
                  pl.BlockSpec((tm, 128), lambda i: (i, 0))],
        out_specs=[pl.BlockSpec((tm, MLA_IN_PAD), lambda i: (i, 0)),
                   pl.BlockSpec((1, MLA_Q_RANK), lambda i: (0, 0)),
                   pl.BlockSpec((1, MLA_KV_RANK), lambda i: (0, 0))],
        out_shape=[_sds((t, MLA_IN_PAD), BF16), _sds((1, MLA_Q_RANK), F32), _sds((1, MLA_KV_RANK), F32)],
        compiler_params=_cparams(("arbitrary",)),
    )(proj2, qa, kva, dcq, dckv, dkr)


def _mla_prep_specs(t, tm):
    head = lambda w: pl.BlockSpec((None, tm, w), lambda i, h: (h, i, 0))
    return dict(
        head256=head(MLA_HD_PAD), head128=head(MLA_VD),
        kr=pl.BlockSpec((tm, 128), lambda i, h: (i, (MLA_Q_RANK + MLA_KV_RANK) // 128)),
        gain=pl.BlockSpec((1, MLA_HD_PAD), lambda i, h: (0, 0)),
        tab=pl.BlockSpec((tm, 128), lambda i, h: (i, 0)),
    )


def _mla_prep(q, kv, proj2, gq, gk, tabs):
    t = q.shape[1]
    tm = _row_tile(t)
    sp = _mla_prep_specs(t, tm)

    def body(q_ref, kv_ref, kr_ref, gq_ref, gk_ref, c_ref, s1_ref, s2_ref, qh_ref, kh_ref, vh_ref):
        c, s1, s2 = c_ref[...], s1_ref[...], s2_ref[...]

        def norm_rope(xv, gain):
            r = lax.rsqrt(jnp.sum(xv * xv, axis=-1, keepdims=True) / MLA_QKD + EPS)
            y = xv * r * gain
            return jnp.concatenate([y[:, :MLA_NOPE], _rope_tile(y[:, MLA_NOPE:], c, s1, s2)], axis=-1)

        kvv = kv_ref[...]
        qh_ref[...] = norm_rope(q_ref[...], gq_ref[...]).astype(BF16)
        kf = jnp.concatenate([kvv[:, :MLA_NOPE], kr_ref[...]], axis=-1)
        kh_ref[...] = norm_rope(kf, gk_ref[...]).astype(BF16)
        vh_ref[...] = jnp.concatenate([kvv[:, MLA_NOPE:], jnp.ones((tm, MLA_VD), F32)], axis=-1).astype(BF16)

    return pl.pallas_call(
        body, name="mla_prep", grid=(t // tm, MLA_HEADS),
        in_specs=[sp['head256'], sp['head256'], sp['kr'], sp['gain'], sp['gain'], sp['tab'], sp['tab'], sp['tab']],
        out_specs=[sp['head256'], sp['head256'], sp['head256']],
        out_shape=[_sds((MLA_HEADS, t, MLA_HD_PAD), BF16), _sds((MLA_HEADS, t, MLA_HD_PAD), BF16),
                   _sds((MLA_HEADS, t, 2 * MLA_VD), BF16)],
        compiler_params=_cparams(("parallel", "arbitrary")),
    )(q, kv, proj2, gq, gk, *tabs)


def _mla_prep_bwd(q, kv, proj2, gq, gk, tabs, dqt, dkh, dvh):
    t = q.shape[1]
    tm = _row_tile(t)
    ab = dqt.shape[-1]
    sp = _mla_prep_specs(t, tm)

    def body(q_ref, kv_ref, kr_ref, gq_ref, gk_ref, c_ref, s1_ref, s2_ref, dqt_ref, dkh_ref, dvh_ref,
             dq_ref, dkv_ref, dkr_ref, dgq_ref, dgk_ref):
        dqh = jnp.concatenate([dqt_ref[b].T for b in range(tm // ab)], axis=0)
        i, h = pl.program_id(0), pl.program_id(1)

        @pl.when((i == 0) & (h == 0))
        def _():
            dgq_ref[...] = jnp.zeros_like(dgq_ref)
            dgk_ref[...] = jnp.zeros_like(dgk_ref)

        @pl.when(h == 0)
        def _():
            dkr_ref[...] = jnp.zeros_like(dkr_ref)

        c, s1, s2 = c_ref[...], s1_ref[...], s2_ref[...]

        def back(xv, gain, dout):
            dy = jnp.concatenate([dout[:, :MLA_NOPE], _rope_tile(dout[:, MLA_NOPE:], c, -s1, -s2)], axis=-1)
            return _rms_bwd_rows(dy, xv, gain, MLA_QKD)

        kvv = kv_ref[...]
        dxq, dgq = back(q_ref[...], gq_ref[...], dqh)
        kf = jnp.concatenate([kvv[:, :MLA_NOPE], kr_ref[...]], axis=-1)
        dxk, dgk = back(kf, gk_ref[...], dkh_ref[...])
        dq_ref[...] = dxq.astype(BF16)
        dkv_ref[...] = jnp.concatenate([dxk[:, :MLA_NOPE], dvh_ref[...]], axis=-1).astype(BF16)
        dkr_ref[...] += dxk[:, MLA_NOPE:]
        dgq_ref[...] += jnp.sum(dgq, axis=0, keepdims=True)
        dgk_ref[...] += jnp.sum(dgk, axis=0, keepdims=True)

    return pl.pallas_call(
        body, name="mla_prep_bwd", grid=(t // tm, MLA_HEADS),
        in_specs=[sp['head256'], sp['head256'], sp['kr'], sp['gain'], sp['gain'], sp['tab'], sp['tab'], sp['tab'],
                  pl.BlockSpec((None, tm // ab, MLA_HD_PAD, ab), lambda i, h: (h, i, 0, 0)),
                  sp['head256'], sp['head128']],
        out_specs=[sp['head256'], sp['head256'], sp['tab'], sp['gain'], sp['gain']],
        out_shape=[_sds((MLA_HEADS, t, MLA_HD_PAD), BF16), _sds((MLA_HEADS, t, MLA_HD_PAD), BF16),
                   _sds((t, 128), F32), _sds((1, MLA_HD_PAD), F32), _sds((1, MLA_HD_PAD), F32)],
        compiler_params=_cparams(("arbitrary", "arbitrary")),
    )(q, kv, proj2, gq, gk, *tabs, dqt, dkh, dvh)


def _chunk_visible(rows, cols, row_off, col_off):
    rq = lax.shift_right_logical(lax.broadcasted_iota(jnp.int32, (rows, cols), 0) + row_off, 6)
    ck = lax.shift_right_logical(lax.broadcasted_iota(jnp.int32, (rows, cols), 1) + col_off, 6)
    return ck <= rq


def _rows_to_lanes(col):
    return col.T[:8, :]


def _attn_fwd(qh, kh, vh):
    t = qh.shape[1]
    ab = min(ATT_BLOCK, t)
    tq = min(ATT_QROWS, t)
    r = tq // ab
    hg = ATT_HEADS

    def body(q_ref, k_ref, v_ref, o_ref, lse_ref):
        n_un = pl.program_id(1) * r

        def step(b, state, diag):
            rows = pl.ds(pl.multiple_of(b * ab, ab), ab)
            ms, accs = [], []
            for hh in range(hg):
                m, acc = state[0][hh], state[1][hh]
                s = _dot(q_ref[hh], k_ref[hh, rows, :], 1, 1)
                if diag is not None:
                    s = jnp.where(_chunk_visible(tq, ab, 0, diag * ab), s, -1e30)
                m_new = jnp.maximum(m, jnp.max(s, axis=-1, keepdims=True))
                p = jnp.exp2((s - m_new) * ATT_EXP2).astype(BF16)
                accs.append(jnp.exp2((m - m_new) * ATT_EXP2) * acc + _dot(p, v_ref[hh, rows, :], 1, 0))
                ms.append(m_new)
            return tuple(ms), tuple(accs)

        heads = lambda v: tuple(v for _ in range(hg))
        state = (heads(jnp.full((tq, 1), -1e30, F32)), heads(jnp.zeros((tq, 2 * MLA_VD), F32)))
        state = lax.fori_loop(0, n_un, lambda b, st: step(b, st, None), state)
        for d in range(r):
            state = step(n_un + d, state, d)
        ms, accs = state
        for hh in range(hg):
            l = accs[hh][:, MLA_VD:]
            o_ref[:, hh * MLA_VD:(hh + 1) * MLA_VD] = accs[hh][:, :MLA_VD] / l
            lse_t = _rows_to_lanes(ms[hh] * ATT_EXP2 + jnp.log(l) * LOG2E)
            for d in range(r):
                lse_ref[hh, d] = lse_t[:, d * ab:(d + 1) * ab]

    return pl.pallas_call(
        body, name="mla_attn", grid=(MLA_HEADS // hg, t // tq),
        in_specs=[pl.BlockSpec((hg, tq, MLA_HD_PAD), lambda g, i: (g, i, 0)),
                  pl.BlockSpec((hg, t, MLA_HD_PAD), lambda g, i: (g, 0, 0)),
                  pl.BlockSpec((hg, t, 2 * MLA_VD), lambda g, i: (g, 0, 0))],
        out_specs=[pl.BlockSpec((tq, hg * MLA_VD), lambda g, i: (i, g)),
                   pl.BlockSpec((hg, r, 8, ab), lambda g, i: (g, i, 0, 0))],
        out_shape=[_sds((t, MLA_HEADS * MLA_VD), F32), _sds((MLA_HEADS, t // ab, 8, ab), F32)],
        compiler_params=_cparams(("parallel", "arbitrary")),
    )(qh, kh, vh)


def _attn_delta(do, o, ab):
    t = do.shape[0]
    tm = _row_tile(t)

    def body(do_ref, o_ref, d_ref):
        d = jnp.sum(do_ref[...] * o_ref[...], axis=-1, keepdims=True)
        d_t = _rows_to_lanes(jnp.broadcast_to(d, (tm, 128)))
        for b in range(tm // ab):
            d_ref[b] = d_t[:, b * ab:(b + 1) * ab]

    col = pl.BlockSpec((tm, MLA_VD), lambda i, h: (i, h))
    return pl.pallas_call(
        body, name="mla_delta", grid=(t // tm, MLA_HEADS), in_specs=[col, col],
        out_specs=pl.BlockSpec((None, tm // ab, 8, ab), lambda i, h: (h, i, 0, 0)),
        out_shape=_sds((MLA_HEADS, t // ab, 8, ab), F32),
        compiler_params=_cparams(("parallel", "parallel")),
    )(do, o)


def _attn_bwd(qh, kh, vh, dob, lse_t, dl_t):
    t = qh.shape[1]
    ab = min(ATT_BLOCK, t)
    nq = t // ab
    hg = ATT_HEADS

    def body(q_ref, k_ref, v_ref, do_ref, lse_ref, dl_ref, dqt_ref, dk_ref, dv_ref):
        j = pl.program_id(1)

        @pl.when(j == 0)
        def _():
            dqt_ref[...] = jnp.zeros_like(dqt_ref)

        ks = [k_ref[hh] for hh in range(hg)]
        vs = [v_ref[hh, :, :MLA_VD] for hh in range(hg)]
        kts = [k.T for k in ks]

        def step(b, grads, masked):
            rows = pl.ds(pl.multiple_of(b * ab, ab), ab)
            out = []
            for hh in range(hg):
                dk, dv = grads[hh]
                q = q_ref[hh, rows, :]
                do = do_ref[rows, hh * MLA_VD:(hh + 1) * MLA_VD]
                s_t = _dot(ks[hh], q, 1, 1)
                if masked:
                    key_chunk = lax.shift_right_logical(lax.broadcasted_iota(jnp.int32, (ab, ab), 0), 6)
                    query_chunk = lax.shift_right_logical(lax.broadcasted_iota(jnp.int32, (ab, ab), 1), 6)
                    s_t = jnp.where(key_chunk <= query_chunk, s_t, -1e30)
                p_t = jnp.exp2(s_t * ATT_EXP2 - lse_ref[hh, b][0:1, :])
                dp_t = _dot(vs[hh], do, 1, 1)
                ds_t = (p_t * (dp_t - dl_ref[hh, b][0:1, :]) * ATT_SCALE).astype(BF16)
                dqt_ref[hh, b] += _dot(kts[hh], ds_t, 1, 0)
                out.append((dk + _dot(ds_t, q, 1, 0), dv + _dot(p_t.astype(BF16), do, 1, 0)))
            return tuple(out)

        grads = tuple((jnp.zeros((ab, MLA_HD_PAD), F32), jnp.zeros((ab, MLA_VD), F32)) for _ in range(hg))
        grads = step(j, grads, True)
        grads = lax.fori_loop(j + 1, nq, lambda b, g: step(b, g, False), grads)
        for hh in range(hg):
            dk_ref[hh] = grads[hh][0]
            dv_ref[hh] = grads[hh][1]

    whole = lambda w: pl.BlockSpec((hg, t, w), lambda g, j: (g, 0, 0))
    blk = lambda w: pl.BlockSpec((hg, ab, w), lambda g, j: (g, j, 0))
    stat = pl.BlockSpec((hg, nq, 8, ab), lambda g, j: (g, 0, 0, 0))
    return pl.pallas_call(
        body, name="mla_attn_bwd", grid=(MLA_HEADS // hg, nq),
        in_specs=[whole(MLA_HD_PAD), blk(MLA_HD_PAD), blk(2 * MLA_VD),
                  pl.BlockSpec((t, hg * MLA_VD), lambda g, j: (0, g)), stat, stat],
        out_specs=[pl.BlockSpec((hg, nq, MLA_HD_PAD, ab), lambda g, j: (g, 0, 0, 0)), blk(MLA_HD_PAD), blk(MLA_VD)],
        out_shape=[_sds((MLA_HEADS, nq, MLA_HD_PAD, ab), F32), _sds((MLA_HEADS, t, MLA_HD_PAD), F32),
                   _sds((MLA_HEADS, t, MLA_VD), F32)],
        compiler_params=_cparams(("parallel", "arbitrary")),
    )(qh, kh, vh, dob, lse_t, dl_t)


def _mlp_fwd(l, h, norm_g, w1g, w2g):
    t = h.shape[0]
    tm = _row_tile(t)
    nsh, _, wsh = w1g.shape
    hn = _rms_fwd(f"mlp_norm{l}", h, norm_g)

    def relu2(acc):
        r = jnp.maximum(acc, 0.0)
        return r, r * r

    tile = pl.BlockSpec((tm, wsh), lambda i, j, k: (i, j))
    r, u = _mm(f"mlp_up{l}", (t // tm, nsh, 1),
               hn, pl.BlockSpec((tm, D_MODEL), lambda i, j, k: (i, 0)),
               w1g, pl.BlockSpec((None, D_MODEL, wsh), lambda i, j, k: (j, 0, 0)), (1, 0),
               [(_sds((t, D_FF), BF16), tile), (_sds((t, D_FF), BF16), tile)], epi=relu2)
    row = pl.BlockSpec((tm, D_MODEL), lambda i, j, k: (i, 0))
    (h2,) = _mm(f"mlp_down{l}", (t // tm, 1, nsh),
                u, pl.BlockSpec((tm, wsh), lambda i, j, k: (i, k)),
                w2g, pl.BlockSpec((None, wsh, D_MODEL), lambda i, j, k: (k, 0, 0)), (1, 0),
                [(_sds((t, D_MODEL), F32), row)], extras=[(h, row)], epi=lambda acc, hv: (acc + hv,))
    return h2, (h, hn, r, u)


def _mlp_bwd(l, dh, saved, norm_g, w1g, w2g):
    h, hn, r, u = saved
    t = h.shape[0]
    tm = _row_tile(t)
    tk = _row_tile(t)
    nsh, _, wsh = w1g.shape
    tile = pl.BlockSpec((tm, wsh), lambda i, j, k: (i, j))
    (da,) = _mm(f"mlp_du{l}", (t // tm, nsh, 1),
                dh, pl.BlockSpec((tm, D_MODEL), lambda i, j, k: (i, 0)),
                w2g, pl.BlockSpec((None, wsh, D_MODEL), lambda i, j, k: (j, 0, 0)), (1, 1),
                [(_sds((t, D_FF), BF16), tile)], extras=[(r, tile)],
                epi=lambda acc, rv: (2.0 * rv.astype(F32) * acc,))
    (dw2,) = _mm(f"mlp_dw2{l}", (nsh, 1, t // tk),
                 u, pl.BlockSpec((tk, wsh), lambda i, j, k: (k, i)),
                 dh, pl.BlockSpec((tk, D_MODEL), lambda i, j, k: (k, 0)), (0, 0),
                 [(_sds((nsh, wsh, D_MODEL), BF16), pl.BlockSpec((None, wsh, D_MODEL), lambda i, j, k: (i, 0, 0)))])
    (dw1,) = _mm(f"mlp_dw1{l}", (1, nsh, t // tk),
                 hn, pl.BlockSpec((tk, D_MODEL), lambda i, j, k: (k, 0)),
                 da, pl.BlockSpec((tk, wsh), lambda i, j, k: (k, j)), (0, 0),
                 [(_sds((nsh, D_MODEL, wsh), BF16), pl.BlockSpec((None, D_MODEL, wsh), lambda i, j, k: (j, 0, 0)))])
    (dhn,) = _mm(f"mlp_dhn{l}", (t // tm, 1, nsh),
                 da, pl.BlockSpec((tm, wsh), lambda i, j, k: (i, k)),
                 w1g, pl.BlockSpec((None, D_MODEL, wsh), lambda i, j, k: (k, 0, 0)), (1, 1),
                 [(_sds((t, D_MODEL), F32), pl.BlockSpec((tm, D_MODEL), lambda i, j, k: (i, 0)))])
    dh_in, dg = _rms_bwd(f"mlp_norm_bwd{l}", dhn, h, norm_g, dh)
    return dh_in, dg, dw1, dw2


def _ple_fwd(l, h, p, norm_g, wg, wp):
    t = h.shape[0]
    tm = _row_tile(t, 512)
    hn = _rms_fwd(f"ple_norm{l}", h, norm_g)
    row = pl.BlockSpec((tm, D_MODEL), lambda i, j, k: (i, 0))
    full = lambda r: pl.BlockSpec((r, D_MODEL), lambda i, j, k: (0, 0))
    (e,) = _mm(f"ple_proj{l}", (t // tm, 1, 1),
               p, pl.BlockSpec((None, None, tm, PLE_DIM), lambda i, j, k: (l, 0, i, 0)),
               wp, full(PLE_DIM), (1, 0), [(_sds((t, D_MODEL), F32), row)])

    def gate_epi(acc, hv, ev):
        gt = _sigmoid(acc)
        return hv + gt * ev, gt

    h_out, gate = _mm(f"ple_gate{l}", (t // tm, 1, 1), hn, row, wg, full(D_MODEL), (1, 0),
                      [(_sds((t, D_MODEL), F32), row), (_sds((t, D_MODEL), F32), row)],
                      extras=[(h, row), (e, row)], epi=gate_epi)
    return h_out, (h, hn, gate, e)


def _ple_bwd(l, dh, saved, p, norm_g, wg, deps=()):
    h, hn, gate, e = saved
    t = h.shape[0]
    tm = _row_tile(t)
    tk = _row_tile(t, 512)
    de, dz = _ple_gate_bwd(f"ple_gate_bwd{l}", dh, gate, e)
    full = lambda r: pl.BlockSpec((r, D_MODEL), lambda i, j, k: (0, 0))
    rowk = pl.BlockSpec((tk, D_MODEL), lambda i, j, k: (k, 0))
    (dwp,) = _mm(f"ple_dwp{l}", (1, 1, t // tk),
                 p, pl.BlockSpec((None, None, tk, PLE_DIM), lambda i, j, k: (l, 0, k, 0)),
                 de, rowk, (0, 0), [(_sds((PLE_DIM, D_MODEL), BF16), full(PLE_DIM))], deps=deps)
    (dwg,) = _mm(f"ple_dwg{l}", (1, 1, t // tk), hn, rowk, dz, rowk, (0, 0),
                 [(_sds((D_MODEL, D_MODEL), BF16), full(D_MODEL))])
    row = pl.BlockSpec((tm, D_MODEL), lambda i, j, k: (i, 0))
    (dhn,) = _mm(f"ple_dhn{l}", (t // tm, 1, 1), dz, row, wg, full(D_MODEL), (1, 1),
                 [(_sds((t, D_MODEL), F32), row)])
    dh_in, dg = _rms_bwd(f"ple_norm_bwd{l}", dhn, h, norm_g, dh)
    return dh_in, dg, dwg, dwp


def _ret_layer_fwd(x, norm_g, wri, wro, gn, cos, sin, deps=()):
    t = x.shape[0]
    tm = _row_tile(t)
    nsh, _, wsh = wri.shape
    hn = _rms_fwd("mix_norm0", x, norm_g)
    (proj,) = _mm("ret_in", (t // tm, nsh, 1),
                  hn, pl.BlockSpec((tm, D_MODEL), lambda i, j, k: (i, 0)),
                  wri, pl.BlockSpec((None, D_MODEL, wsh), lambda i, j, k: (j, 0, 0)), (1, 0),
                  [(_sds((t, RET_IN), F32), pl.BlockSpec((tm, wsh), lambda i, j, k: (i, j)))], deps=deps)
    gated, outp, states = _ret_fwd(proj, cos, sin, gn)
    row = pl.BlockSpec((tm, D_MODEL), lambda i, j, k: (i, 0))
    kt = 512
    (h1,) = _mm("ret_out", (t // tm, 1, RET_V_W // kt),
                gated, pl.BlockSpec((tm, kt), lambda i, j, k: (i, k)),
                wro, pl.BlockSpec((kt, D_MODEL), lambda i, j, k: (k, 0)), (1, 0),
                [(_sds((t, D_MODEL), F32), row)], extras=[(x, row)], epi=lambda acc, xv: (acc + xv,))
    return h1, (x, hn, proj, gated, outp, states)


def _ret_layer_bwd(dh, saved, norm_g, wri, wro, gn, cos, sin, emit, deps=()):
    x, hn, proj, gated, outp, states = saved
    t = x.shape[0]
    tm = _row_tile(t)
    tk = _row_tile(t, 512)
    nsh, _, wsh = wri.shape
    (dgated,) = _mm("ret_dgated", (t // tm, RET_V_W // D_MODEL, 1),
                    dh, pl.BlockSpec((tm, D_MODEL), lambda i, j, k: (i, 0)),
                    wro, pl.BlockSpec((D_MODEL, D_MODEL), lambda i, j, k: (j, 0)), (1, 1),
                    [(_sds((t, RET_V_W), F32), pl.BlockSpec((tm, D_MODEL), lambda i, j, k: (i, j)))], deps=deps)
    kt = 512
    (dwro,) = _mm("ret_dwro", (RET_V_W // kt, 1, t // tk),
                  gated, pl.BlockSpec((tk, kt), lambda i, j, k: (k, i)),
                  dh, pl.BlockSpec((tk, D_MODEL), lambda i, j, k: (k, 0)), (0, 0),
                  [(_sds((RET_V_W, D_MODEL), BF16), pl.BlockSpec((kt, D_MODEL), lambda i, j, k: (i, 0)))])
    dproj, dgn = _ret_bwd(proj, cos, sin, gn, outp, states, dgated)
    (dwri,) = _mm("ret_dwri", (1, nsh, t // tk),
                  hn, pl.BlockSpec((tk, D_MODEL), lambda i, j, k: (k, 0)),
                  dproj, pl.BlockSpec((tk, wsh), lambda i, j, k: (k, j)), (0, 0),
                  [(_sds((nsh, D_MODEL, wsh), BF16), pl.BlockSpec((None, D_MODEL, wsh), lambda i, j, k: (j, 0, 0)))])
    deps = emit(dwro, dwri)
    (dhn,) = _mm("ret_dhn", (t // tm, 1, nsh),
                 dproj, pl.BlockSpec((tm, wsh), lambda i, j, k: (i, k)),
                 wri, pl.BlockSpec((None, D_MODEL, wsh), lambda i, j, k: (k, 0, 0)), (1, 1),
                 [(_sds((t, D_MODEL), F32), pl.BlockSpec((tm, D_MODEL), lambda i, j, k: (i, 0)))], deps=deps)
    dx, dg = _rms_bwd("mix_norm_bwd0", dhn, x, norm_g, dh)
    return dx, dg, dgn.reshape(RET_HEADS, RET_DV)


def _mla_layer_fwd(h, norm_g, wmi, qa, kva, wuq, wukv, gq, gk, wmo, tabs):
    t = h.shape[0]
    tm = _row_tile(t)
    hn = _rms_fwd("mix_norm1", h, norm_g)
    row = pl.BlockSpec((tm, D_MODEL), lambda i, j, k: (i, 0))
    (proj2,) = _mm("mla_in", (t // tm, 1, 1), hn, row,
                   wmi, pl.BlockSpec((D_MODEL, MLA_IN_PAD), lambda i, j, k: (0, 0)), (1, 0),
                   [(_sds((t, MLA_IN_PAD), F32), pl.BlockSpec((tm, MLA_IN_PAD), lambda i, j, k: (i, 0)))])
    cq, ckv = _mla_mid(proj2, qa, kva)
    head = pl.BlockSpec((None, tm, MLA_HD_PAD), lambda i, j, k: (j, i, 0))
    (q,) = _mm("mla_uq", (t // tm, MLA_HEADS, 1),
               cq, pl.BlockSpec((tm, MLA_Q_RANK), lambda i, j, k: (i, 0)),
               wuq, pl.BlockSpec((None, MLA_Q_RANK, MLA_HD_PAD), lambda i, j, k: (j, 0, 0)), (1, 0),
               [(_sds((MLA_HEADS, t, MLA_HD_PAD), F32), head)])
    (kv,) = _mm("mla_ukv", (t // tm, MLA_HEADS, 1),
                ckv, pl.BlockSpec((tm, MLA_KV_RANK), lambda i, j, k: (i, 0)),
                wukv, pl.BlockSpec((None, MLA_KV_RANK, MLA_HD_PAD), lambda i, j, k: (j, 0, 0)), (1, 0),
                [(_sds((MLA_HEADS, t, MLA_HD_PAD), F32), head)])
    qh, kh, vh = _mla_prep(q, kv, proj2, gq, gk, tabs)
    o, lse = _attn_fwd(qh, kh, vh)
    (h_out,) = _mm("mla_out", (t // tm, 1, 1), o, row,
                   wmo, pl.BlockSpec((D_MODEL, D_MODEL), lambda i, j, k: (0, 0)), (1, 0),
                   [(_sds((t, D_MODEL), F32), row)], extras=[(h, row)], epi=lambda acc, hv: (acc + hv,))
    return h_out, (h, hn, proj2, cq, ckv, q, kv, qh, kh, vh, o, lse)


def _mla_layer_bwd(dh, saved, norm_g, wmi, qa, kva, wuq, wukv, gq, gk, wmo, tabs, deps=()):
    h, hn, proj2, cq, ckv, q, kv, qh, kh, vh, o, lse = saved
    t = h.shape[0]
    tm = _row_tile(t)
    tk = _row_tile(t, 512)
    row = pl.BlockSpec((tm, D_MODEL), lambda i, j, k: (i, 0))
    rowk = pl.BlockSpec((tk, D_MODEL), lambda i, j, k: (k, 0))
    sq = pl.BlockSpec((D_MODEL, D_MODEL), lambda i, j, k: (0, 0))
    do, dob = _mm("mla_do", (t // tm, 1, 1), dh, row, wmo, sq, (1, 1),
                  [(_sds((t, D_MODEL), F32), row), (_sds((t, D_MODEL), BF16), row)], epi=lambda acc: (acc, acc),
                  deps=deps)
    (dwmo,) = _mm("mla_dwo", (1, 1, t // tk), o, rowk, dh, rowk, (0, 0), [(_sds((D_MODEL, D_MODEL), BF16), sq)])
    delta = _attn_delta(do, o, lse.shape[-1])
    dqt, dkh, dvh = _attn_bwd(qh, kh, vh, dob, lse, delta)
    dq, dkv, dkr, dgq, dgk = _mla_prep_bwd(q, kv, proj2, gq, gk, tabs, dqt, dkh, dvh)

    headk = pl.BlockSpec((None, tk, MLA_HD_PAD), lambda i, j, k: (j, k, 0))
    (dwuq,) = _mm("mla_dwuq", (1, MLA_HEADS, t // tk),
                  cq, pl.BlockSpec((tk, MLA_Q_RANK), lambda i, j, k: (k, 0)), dq, headk, (0, 0),
                  [(_sds((MLA_HEADS, MLA_Q_RANK, MLA_HD_PAD), BF16),
                    pl.BlockSpec((None, MLA_Q_RANK, MLA_HD_PAD), lambda i, j, k: (j, 0, 0)))])
    (dwukv,) = _mm("mla_dwukv", (1, MLA_HEADS, t // tk),
                   ckv, pl.BlockSpec((tk, MLA_KV_RANK), lambda i, j, k: (k, 0)), dkv, headk, (0, 0),
                   [(_sds((MLA_HEADS, MLA_KV_RANK, MLA_HD_PAD), BF16),
                     pl.BlockSpec((None, MLA_KV_RANK, MLA_HD_PAD), lambda i, j, k: (j, 0, 0)))])
    headi = pl.BlockSpec((None, tm, MLA_HD_PAD), lambda i, j, k: (k, i, 0))
    (dcq,) = _mm("mla_dcq", (t // tm, 1, MLA_HEADS), dq, headi,
                 wuq, pl.BlockSpec((None, MLA_Q_RANK, MLA_HD_PAD), lambda i, j, k: (k, 0, 0)), (1, 1),
                 [(_sds((t, MLA_Q_RANK), F32), pl.BlockSpec((tm, MLA_Q_RANK), lambda i, j, k: (i, 0)))])
    (dckv,) = _mm("mla_dckv", (t // tm, 1, MLA_HEADS), dkv, headi,
                  wukv, pl.BlockSpec((None, MLA_KV_RANK, MLA_HD_PAD), lambda i, j, k: (k, 0, 0)), (1, 1),
                  [(_sds((t, MLA_KV_RANK), F32), pl.BlockSpec((tm, MLA_KV_RANK), lambda i, j, k: (i, 0)))])
    dproj2, dqa, dkva = _mla_mid_bwd(proj2, qa, kva, dcq, dckv, dkr)
    win = pl.BlockSpec((D_MODEL, MLA_IN_PAD), lambda i, j, k: (0, 0))
    (dwmi,) = _mm("mla_dwin", (1, 1, t // tk), hn, rowk,
                  dproj2, pl.BlockSpec((tk, MLA_IN_PAD), lambda i, j, k: (k, 0)), (0, 0),
                  [(_sds((D_MODEL, MLA_IN_PAD), BF16), win)])
    (dhn,) = _mm("mla_dhn", (t // tm, 1, 1),
                 dproj2, pl.BlockSpec((tm, MLA_IN_PAD), lambda i, j, k: (i, 0)), wmi, win, (1, 1),
                 [(_sds((t, D_MODEL), F32), row)])
    dh_in, dg = _rms_bwd("mix_norm_bwd1", dhn, h, norm_g, dh)
    return dh_in, dict(mix=dg, wmi=dwmi, qa=dqa, kva=dkva, wuq=dwuq, wukv=dwukv, gq=dgq, gk=dgk, wmo=dwmo)


def _local_step(x, p, target, w, fetch, emit=lambda group: ()):
    t = x.shape[0]
    inv = 1.0 / (ROPE_THETA ** (jnp.arange(0, RET_DK, 2, dtype=F32) / RET_DK))
    ang = jnp.arange(t, dtype=F32)[:, None] * inv[None, :]
    cos_r, sin_r = jnp.cos(ang), jnp.sin(ang)
    tabs = _mla_tables(t)
    row = lambda a, i: a[i:i + 1]

    h1, s_ret = _ret_layer_fwd(x, row(w['mix_norm'], 0), w['ret_w_in'], w['ret_w_out'], w['ret_gn'], cos_r, sin_r,
                               deps=w['deps'])
    w0 = fetch('layer0', (h1,))
    h2, s_mlp0 = _mlp_fwd(0, h1, row(w['mlp_norm'], 0), w0['mlp_w1'], w0['mlp_w2'])
    h3, s_ple0 = _ple_fwd(0, h2, p, row(w['ple_norm'], 0), w0['ple_gate_w'], w0['ple_proj_w'])
    wm = fetch('mla', (h3,))
    mla_w = (wm['mla_w_in'], w['mla_q_a_norm'], w['mla_kv_a_norm'], wm['mla_w_uq'], wm['mla_w_ukv'],
             w['mla_q_norm'], w['mla_k_norm'], wm['mla_w_out'], tabs)
    h4, s_mla = _mla_layer_fwd(h3, row(w['mix_norm'], 1), *mla_w)
    w1 = fetch('layer1', (h4,))
    h5, s_mlp1 = _mlp_fwd(1, h4, row(w['mlp_norm'], 1), w1['mlp_w1'], w1['mlp_w2'])
    y, s_ple1 = _ple_fwd(1, h5, p, row(w['ple_norm'], 1), w1['ple_gate_w'], w1['ple_proj_w'])

    dy, sq_err = _loss_head(y, target)

    n = N_DEV
    colsh = lambda a: a.reshape(a.shape[0], n, a.shape[1] // n).transpose(1, 0, 2)
    rowsh = lambda a: a.reshape(n, a.shape[0] // n, a.shape[1])
    big = {}

    def emit_group(group):
        big.update(group)
        return emit(group)

    dh5, dg_ple1, dwg1, dwp1 = _ple_bwd(1, dy, s_ple1, p, row(w['ple_norm'], 1), w1['ple_gate_w'])
    dh4, dg_mlp1, dw1_1, dw2_1 = _mlp_bwd(1, dh5, s_mlp1, row(w['mlp_norm'], 1), w1['mlp_w1'], w1['mlp_w2'])
    deps = emit_group({('ple_gate_w', 1): rowsh(dwg1), ('ple_proj_w', 1): colsh(dwp1),
                       ('mlp_w2', 1): dw2_1, ('mlp_w1', 1): dw1_1})
    dh3, gm = _mla_layer_bwd(dh4, s_mla, row(w['mix_norm'], 1), *mla_w, deps=deps)
    deps = emit_group({('mla_w_out', 0): rowsh(gm['wmo']), ('mla_w_uq', 0): gm['wuq'][:, :, :MLA_QKD],
                       ('mla_w_ukv', 0): gm['wukv'], ('mla_w_in', 0): rowsh(gm['wmi'][:, :MLA_IN])})
    dh2, dg_ple0, dwg0, dwp0 = _ple_bwd(0, dh3, s_ple0, p, row(w['ple_norm'], 0), w0['ple_gate_w'], deps=deps)
    dh1, dg_mlp0, dw1_0, dw2_0 = _mlp_bwd(0, dh2, s_mlp0, row(w['mlp_norm'], 0), w0['mlp_w1'], w0['mlp_w2'])
    deps = emit_group({('ple_gate_w', 0): rowsh(dwg0), ('ple_proj_w', 0): colsh(dwp0),
                       ('mlp_w2', 0): dw2_0, ('mlp_w1', 0): dw1_0})
    dx, dg_mix0, dgn = _ret_layer_bwd(
        dh1, s_ret, row(w['mix_norm'], 0), w['ret_w_in'], w['ret_w_out'], w['ret_gn'], cos_r, sin_r,
        lambda dwro, dwri: emit_group({('ret_w_out', 0): rowsh(dwro), ('ret_w_in', 0): dwri}), deps=deps)

    small = dict(
        mix_norm=[dg_mix0, gm['mix']], mlp_norm=[dg_mlp0, dg_mlp1], ple_norm=[dg_ple0, dg_ple1],
        ret_gn=dgn, mla_q_a_norm=gm['qa'], mla_kv_a_norm=gm['kva'], mla_q_norm=gm['gq'], mla_k_norm=gm['gk'],
    )
    return sq_err, dx, big, small


def _my_place():
    x, y, c = lax.axis_index("x"), lax.axis_index("y"), lax.axis_index("c")
    return x, y, c


def _flat(px, py, pc):
    return 4 * px + 2 * py + pc


def _peer(x, y, c, r):
    return (1 - x if r & 4 else x, 1 - y if r & 2 else y, 1 - c if r & 1 else c)


def _all_gather(arrays):
    n = len(arrays)

    def body(*refs):
        ins, outs = refs[:n], refs[n:2 * n]
        send_sems, recv_sems, local_sems = refs[2 * n:]
        x, y, c = _my_place()
        me, sibling = (x, y, c), (x, y, 1 - c)
        chips = [(1 - x, y), (x, 1 - y), (1 - x, 1 - y)]

        def copy(a, k, block, to, src=None):
            slot = outs[a].at[_flat(*block)]
            return pltpu.make_async_remote_copy(
                src_ref=slot if src is None else src, dst_ref=slot,
                send_sem=send_sems.at[a, k], recv_sem=recv_sems.at[a, k], device_id=to, device_id_type=MESH)

        mine = [pltpu.make_async_copy(ins[a], outs[a].at[_flat(*me)], local_sems.at[a]) for a in range(n)]
        for cp in mine:
            cp.start()
        first = []
        for a in range(n):
            first.append(copy(a, 0, me, sibling, src=ins[a]))
            first += [copy(a, 1 + j, me, (*chip, c), src=ins[a]) for j, chip in enumerate(chips)]
        for cp in first:
            cp.start()
        passed = []
        for a in range(n):
            for j, chip in enumerate(chips):
                copy(a, 1 + j, (*chip, c), me).wait_recv()
                passed.append(copy(a, 4 + j, (*chip, c), sibling))
                passed[-1].start()
        for a in range(n):
            copy(a, 0, sibling, me).wait_recv()
            for j, chip in enumerate(chips):
                copy(a, 4 + j, (*chip, 1 - c), me).wait_recv()
        for cp in first + passed:
            cp.wait_send()
        for cp in mine:
            cp.wait()

    return pl.pallas_call(
        body, name="all_gather_weights",
        in_specs=[ANY] * n, out_specs=[ANY] * n,
        out_shape=[_sds((N_DEV,) + a.shape, a.dtype) for a in arrays],
        scratch_shapes=[pltpu.SemaphoreType.DMA((n, 7)), pltpu.SemaphoreType.DMA((n, 7)),
                        pltpu.SemaphoreType.DMA((n,))],
    )(*arrays)


HBM = pl.BlockSpec(memory_space=pltpu.HBM)
SEMS = pl.BlockSpec(memory_space=pltpu.SEMAPHORE)
SIDE_EFFECT = pltpu.SideEffectType.DATAFLOW_SIDE_EFFECTING


def _rs_copies(x, y, c, srcs, lands, send_sems, recv_sems):
    copies = []
    for a in range(len(srcs)):
        for r in range(1, N_DEV):
            peer = _peer(x, y, c, r)
            k = a * (N_DEV - 1) + r - 1
            copies.append(pltpu.make_async_remote_copy(
                src_ref=srcs[a].at[_flat(*peer)], dst_ref=lands[a].at[r - 1],
                send_sem=send_sems.at[k], recv_sem=recv_sems.at[k], device_id=peer, device_id_type=MESH))
    return copies


def _rs_start(name, arrays):
    n = len(arrays)
    hbm = lambda a: pltpu.with_memory_space_constraint(a, pltpu.HBM)
    lands = [hbm(lax.empty((N_DEV - 1,) + a.shape[1:], a.dtype)) for a in arrays]

    def body(*refs):
        srcs, lnd = refs[:n], refs[n:2 * n]
        send_sems, recv_sems = refs[2 * n], refs[2 * n + 1]
        token = refs[-1]
        for cp in _rs_copies(*_my_place(), srcs, lnd, send_sems, recv_sems):
            cp.start()
        token[...] = jnp.zeros_like(token)

    outs = pl.pallas_call(
        body, name=name,
        in_specs=[HBM] * (2 * n),
        out_specs=[SEMS, SEMS] + [HBM] * (2 * n) + [pl.BlockSpec(memory_space=pltpu.VMEM)],
        out_shape=[pltpu.SemaphoreType.DMA((n * (N_DEV - 1),)), pltpu.SemaphoreType.DMA((n * (N_DEV - 1),))]
        + [pltpu.HBM(a.shape, a.dtype) for a in arrays] + [pltpu.HBM(l.shape, l.dtype) for l in lands]
        + [_sds((8, 128), F32)],
        input_output_aliases={i: 2 + i for i in range(2 * n)},
        compiler_params=pltpu.CompilerParams(has_side_effects=SIDE_EFFECT),
    )(*[hbm(a) for a in arrays], *lands)
    return outs[0], outs[1], outs[2:2 + n], outs[2 + n:2 + 2 * n], outs[-1]


def _rs_wait(name, send_sems, recv_sems, srcs, lands, after):
    n = len(srcs)

    def body(*refs):
        src_refs, lnd = refs[:n], refs[n:2 * n]
        send, recv = refs[2 * n], refs[2 * n + 1]
        for cp in _rs_copies(*_my_place(), src_refs, lnd, send, recv):
            cp.wait_send()
            cp.wait_recv()

    outs = pl.pallas_call(
        body, name=name,
        in_specs=[HBM] * (2 * n) + [SEMS, SEMS] + [ANY] * len(after),
        out_specs=[HBM] * (2 * n),
        out_shape=[pltpu.HBM(a.shape, a.dtype) for a in list(srcs) + list(lands)],
        input_output_aliases={i: i for i in range(2 * n)},
        compiler_params=pltpu.CompilerParams(has_side_effects=SIDE_EFFECT),
    )(*srcs, *lands, send_sems, recv_sems, *after)
    return outs[:n], outs[n:]


SMALL_PACK_ROWS = 16


def _all_reduce_small(rows):
    n = len(rows)

    def body(*refs):
        ins = refs[:n]
        out_ref, mine, buf, send_sems, recv_sems = refs[n:]
        x, y, c = _my_place()
        mine[...] = jnp.zeros_like(mine)
        for (r0, a), ref in zip(rows, ins):
            mine[r0:r0 + a.shape[0], 0:a.shape[1]] = ref[...]
        buf[_flat(x, y, c)] = mine[...]
        copies = []
        for r in range(1, N_DEV):
            peer = _peer(x, y, c, r)
            send = pltpu.make_async_remote_copy(
                src_ref=mine, dst_ref=buf.at[_flat(x, y, c)],
                send_sem=send_sems.at[r - 1], recv_sem=recv_sems.at[r - 1], device_id=peer, device_id_type=MESH)
            send.start()
            recv = pltpu.make_async_remote_copy(
                src_ref=mine, dst_ref=buf.at[_flat(*peer)],
                send_sem=send_sems.at[r - 1], recv_sem=recv_sems.at[r - 1], device_id=peer, device_id_type=MESH)
            copies.append((send, recv))
        for send, recv in copies:
            send.wait_send()
            recv.wait_recv()
        acc = buf[0]
        for s in range(1, N_DEV):
            acc = acc + buf[s]
        out_ref[...] = acc

    vm = pl.BlockSpec(memory_space=pltpu.VMEM)
    shape = (SMALL_PACK_ROWS, D_MODEL)
    return pl.pallas_call(
        body, name="all_reduce_small", in_specs=[vm] * n, out_specs=vm,
        out_shape=_sds(shape, F32),
        scratch_shapes=[pltpu.VMEM(shape, F32), pltpu.VMEM((N_DEV,) + shape, F32),
                        pltpu.SemaphoreType.DMA((7,)), pltpu.SemaphoreType.DMA((7,))],
    )(*[a for _, a in rows])


def _adamw_math(w, g, m, v):
    m = ADAM_B1 * m + (1.0 - ADAM_B1) * g
    v = ADAM_B2 * v + (1.0 - ADAM_B2) * (g * g)
    m_hat = m / (1.0 - ADAM_B1 ** ADAM_STEP)
    v_hat = v / (1.0 - ADAM_B2 ** ADAM_STEP)
    delta = -ADAM_LR * (m_hat / (jnp.sqrt(v_hat) + ADAM_EPS) + ADAM_WD * w)
    return delta, m, v


def _adamw_big(name, w, m, v, srcs, lands, me):
    nl, rows, cols = w.shape
    tr = next(cand for cand in (256, 128, 64, 32, 16, 8) if rows % cand == 0)

    def body(me_ref, w_ref, m_ref, v_ref, *rest):
        src_refs, land_refs = rest[:nl], rest[nl:2 * nl]
        g_ref, d_ref, mo_ref, vo_ref = rest[2 * nl:]
        for layer in range(nl):
            @pl.when(pl.program_id(0) == layer)
            def _():
                g = src_refs[layer][...].astype(F32)
                for s in range(N_DEV - 1):
                    g = g + land_refs[layer][s].astype(F32)
                delta, mn, vn = _adamw_math(w_ref[...], g, m_ref[...], v_ref[...])
                g_ref[...] = g
                d_ref[...] = delta
                mo_ref[...] = mn
                vo_ref[...] = vn

    blk = pl.BlockSpec((None, tr, cols), lambda l, i, me_ref: (l, i, 0))
    own = pl.BlockSpec((None, tr, cols), lambda l, i, me_ref: (me_ref[0], i, 0))
    peers = pl.BlockSpec((N_DEV - 1, tr, cols), lambda l, i, me_ref: (0, i, 0))
    return pl.pallas_call(
        body, name=name,
        grid_spec=pltpu.PrefetchScalarGridSpec(
            num_scalar_prefetch=1, grid=(nl, rows // tr),
            in_specs=[blk, blk, blk] + [own] * nl + [peers] * nl, out_specs=[blk] * 4),
        out_shape=[_sds((nl, rows, cols), F32)] * 4,
        compiler_params=_cparams(("arbitrary", "arbitrary")),
    )(me, w, m, v, *srcs, *lands)


def _adamw_small(ws, gs, ms, vs):
    n = len(ws)

    def body(*refs):
        w_refs, g_refs, m_refs, v_refs = (refs[i * n:(i + 1) * n] for i in range(4))
        d_out, m_out, v_out = (refs[(4 + i) * n:(5 + i) * n] for i in range(3))
        for i in range(n):
            delta, mn, vn = _adamw_math(w_refs[i][...], g_refs[i][...], m_refs[i][...], v_refs[i][...])
            d_out[i][...] = delta
            m_out[i][...] = mn
            v_out[i][...] = vn

    vm = pl.BlockSpec(memory_space=pltpu.VMEM)
    outs = pl.pallas_call(
        body, name="adamw_small", in_specs=[vm] * (4 * n), out_specs=[vm] * (3 * n),
        out_shape=[_sds(a.shape, F32) for a in ws] * 3,
    )(*ws, *gs, *ms, *vs)
    return outs[:n], outs[n:2 * n], outs[2 * n:]


SMALL_ROWS = 16


def _pad_to(a, rows, cols):
    return jnp.pad(a, ((0, rows - a.shape[0]), (0, cols - a.shape[1])))


def _place_own(blocks):
    me = _flat(*_my_place())
    return [lax.dynamic_update_slice(lax.empty((N_DEV,) + b.shape, b.dtype), b[None], (me,) + (0,) * b.ndim)
            for b in blocks]


def _ag_copies(x, y, c, blocks, bufs, send_sems, recv_sems):
    sends, recvs = [], []
    for a in range(len(blocks)):
        for r in range(1, N_DEV):
            peer = _peer(x, y, c, r)
            k = a * (N_DEV - 1) + r - 1
            make = lambda place: pltpu.make_async_remote_copy(
                src_ref=blocks[a], dst_ref=bufs[a].at[_flat(*place)],
                send_sem=send_sems.at[k], recv_sem=recv_sems.at[k], device_id=peer, device_id_type=MESH)
            sends.append(make((x, y, c)))
            recvs.append(make(peer))
    return sends, recvs


def _ag_start(groups, after):
    flat = [pair for g in groups for pair in g]
    n, ng = len(flat), len(groups)
    hbm = lambda a: pltpu.with_memory_space_constraint(a, pltpu.HBM)

    def body(*refs):
        blocks, bufs = refs[:n], refs[n:2 * n]
        sems = refs[2 * n + len(after):2 * n + len(after) + 2 * ng]
        x, y, c = _my_place()
        at = 0
        for gi, g in enumerate(groups):
            sends, _ = _ag_copies(x, y, c, blocks[at:at + len(g)], bufs[at:at + len(g)], sems[2 * gi], sems[2 * gi + 1])
            for cp in sends:
                cp.start()
            at += len(g)
        refs[-1][...] = jnp.zeros_like(refs[-1])

    sem_shapes = [pltpu.SemaphoreType.DMA((len(g) * (N_DEV - 1),)) for g in groups for _ in range(2)]
    outs = pl.pallas_call(
        body, name="gather_start",
        in_specs=[HBM] * (2 * n) + [ANY] * len(after),
        out_specs=[SEMS] * (2 * ng) + [HBM] * (2 * n) + [pl.BlockSpec(memory_space=pltpu.VMEM)],
        out_shape=sem_shapes + [pltpu.HBM(b.shape, b.dtype) for b, _ in flat]
        + [pltpu.HBM(u.shape, u.dtype) for _, u in flat] + [_sds((8, 128), F32)],
        input_output_aliases={i: 2 * ng + i for i in range(2 * n)},
        compiler_params=pltpu.CompilerParams(has_side_effects=SIDE_EFFECT),
    )(*[hbm(b) for b, _ in flat], *[hbm(u) for _, u in flat], *after)
    blocks_thru, bufs_thru = outs[2 * ng:2 * ng + n], outs[2 * ng + n:2 * ng + 2 * n]
    started, at = [], 0
    for gi, g in enumerate(groups):
        started.append((outs[2 * gi], outs[2 * gi + 1], blocks_thru[at:at + len(g)], bufs_thru[at:at + len(g)]))
        at += len(g)
    return started, outs[-1]


def _ag_wait(name, send_sems, recv_sems, blocks, bufs, after):
    n = len(blocks)

    def body(*refs):
        sends, recvs = _ag_copies(*_my_place(), refs[:n], refs[n:2 * n], refs[2 * n], refs[2 * n + 1])
        for s, r in zip(sends, recvs):
            s.wait_send()
            r.wait_recv()

    outs = pl.pallas_call(
        body, name=name,
        in_specs=[HBM] * (2 * n) + [SEMS, SEMS] + [ANY] * len(after),
        out_specs=[HBM] * (2 * n),
        out_shape=[pltpu.HBM(a.shape, a.dtype) for a in list(blocks) + list(bufs)],
        input_output_aliases={i: i for i in range(2 * n)},
        compiler_params=pltpu.CompilerParams(has_side_effects=SIDE_EFFECT),
    )(*blocks, *bufs, send_sems, recv_sems, *after)
    return outs[n:]


def _prepare_weights(p):
    n = N_DEV
    bf = lambda a: a.astype(BF16)
    gn_pack = jnp.concatenate([
        _pad_to(p['ret_gn'][0], RET_HEADS, 128), _pad_to(p['mla_q_a_norm'], 1, 128),
        _pad_to(p['mla_kv_a_norm'], 1, 128), jnp.zeros((2, 128), F32)], axis=0)
    layer = lambda l: [bf(p['mlp_w1'][l]), bf(p['mlp_w2'][l]), bf(p['ple_gate_w'][l]), bf(p['ple_proj_w'][l])]
    later = [layer(0), [bf(p['mla_w_in'][0]), bf(p['mla_w_uq'][0]), bf(p['mla_w_ukv'][0]), bf(p['mla_w_out'][0])],
             layer(1)]
    bufs = _place_own([b for g in later for b in g])
    pack, wri, wro = _all_gather([gn_pack, bf(p['ret_w_in'][0]), bf(p['ret_w_out'][0])])
    groups, at = [], 0
    for g in later:
        groups.append(list(zip(g, bufs[at:at + len(g)])))
        at += len(g)
    started, token = _ag_start(groups, (wri,))

    w = {k: p[k] for k in ('mix_norm', 'mlp_norm', 'ple_norm')}
    w['ret_gn'] = pack[:, :RET_HEADS, :RET_DV // n].transpose(1, 0, 2).reshape(RET_HEADS, RET_DV)
    w['mla_q_a_norm'] = pack[:, RET_HEADS, :MLA_Q_RANK // n].reshape(1, MLA_Q_RANK)
    w['mla_kv_a_norm'] = pack[:, RET_HEADS + 1, :MLA_KV_RANK // n].reshape(1, MLA_KV_RANK)
    w['ret_w_in'] = wri
    w['ret_w_out'] = wro.reshape(RET_V_W, D_MODEL)
    w['mla_q_norm'] = _pad_to(p['mla_q_norm'], 1, MLA_HD_PAD)
    w['mla_k_norm'] = _pad_to(p['mla_k_norm'], 1, MLA_HD_PAD)
    w['deps'] = (token,)

    def fetch(name, after):
        gi = ('layer0', 'mla', 'layer1').index(name)
        got = _ag_wait("gather_wait_" + name, *started[gi], after)
        if name == 'mla':
            wmi, wuq, wukv, wmo = got
            return dict(mla_w_in=jnp.pad(wmi.reshape(D_MODEL, MLA_IN), ((0, 0), (0, MLA_IN_PAD - MLA_IN))),
                        mla_w_uq=jnp.pad(wuq, ((0, 0), (0, 0), (0, MLA_HD_PAD - MLA_QKD))),
                        mla_w_ukv=wukv, mla_w_out=wmo.reshape(D_MODEL, D_MODEL))
        w1, w2, wg, wp = got
        return dict(mlp_w1=w1, mlp_w2=w2, ple_gate_w=wg.reshape(D_MODEL, D_MODEL),
                    ple_proj_w=wp.transpose(1, 0, 2).reshape(PLE_DIM, D_MODEL))

    return w, fetch


def _small_grads(small):
    rows = [(0, small['mix_norm'][0]), (1, small['mix_norm'][1]), (2, small['mlp_norm'][0]),
            (3, small['mlp_norm'][1]), (4, small['ple_norm'][0]), (5, small['ple_norm'][1]),
            (6, small['ret_gn']), (10, small['mla_q_a_norm']), (11, small['mla_kv_a_norm']),
            (12, small['mla_q_norm']), (13, small['mla_k_norm'])]
    gs = _all_reduce_small(rows)
    me = _flat(*_my_place())
    n = N_DEV
    return dict(
        mix_norm=gs[0:2], mlp_norm=gs[2:4], ple_norm=gs[4:6],
        ret_gn=lax.dynamic_slice(gs, (6, me * (RET_DV // n)), (RET_HEADS, RET_DV // n)),
        mla_q_a_norm=lax.dynamic_slice(gs, (10, me * (MLA_Q_RANK // n)), (1, MLA_Q_RANK // n)),
        mla_kv_a_norm=lax.dynamic_slice(gs, (11, me * (MLA_KV_RANK // n)), (1, MLA_KV_RANK // n)),
        mla_q_norm=gs[12:13, :MLA_QKD], mla_k_norm=gs[13:14, :MLA_QKD])


def kernel(x, p, mix_norm, ret_w_in, ret_gn, ret_w_out, mla_w_in, mla_q_a_norm, mla_kv_a_norm, mla_w_uq, mla_w_ukv, mla_q_norm, mla_k_norm, mla_w_out, mlp_norm, mlp_w1, mlp_w2, ple_norm, ple_gate_w, ple_proj_w, loss_target, m_mix_norm, m_ret_w_in, m_ret_gn, m_ret_w_out, m_mla_w_in, m_mla_q_a_norm, m_mla_kv_a_norm, m_mla_w_uq, m_mla_w_ukv, m_mla_q_norm, m_mla_k_norm, m_mla_w_out, m_mlp_norm, m_mlp_w1, m_mlp_w2, m_ple_norm, m_ple_gate_w, m_ple_proj_w, v_mix_norm, v_ret_w_in, v_ret_gn, v_ret_w_out, v_mla_w_in, v_mla_q_a_norm, v_mla_kv_a_norm, v_mla_w_uq, v_mla_w_ukv, v_mla_q_norm, v_mla_k_norm, v_mla_w_out, v_mlp_norm, v_mlp_w1, v_mlp_w2, v_ple_norm, v_ple_gate_w, v_ple_proj_w):
    given = dict(locals())
    params = {n: given[n] for n in WEIGHTS}
    w, fetch = _prepare_weights(params)

    started = []

    def emit(group):
        keys = list(group)
        send, recv, srcs, lands, token = _rs_start(f"rs_start{len(started)}", [group[k] for k in keys])
        started.append((keys, send, recv, srcs, lands))
        return (token,)

    sq_err, grad_x, _, small = _local_step(x[0], p, loss_target[0], w, fetch, emit)
    loss = lax.psum(0.5 / D_MODEL * sq_err[0, 0], ("x", "y", "c"))

    grads, deltas, new_m, new_v = {}, {}, {}, {}
    sg = _small_grads(small)
    two_d = lambda a: a.reshape(-1, a.shape[-1])
    d_s, m_s, v_s = _adamw_small([two_d(params[n]) for n in SMALL], [sg[n] for n in SMALL],
                                 [two_d(given["m_" + n]) for n in SMALL], [two_d(given["v_" + n]) for n in SMALL])
    for i, n in enumerate(SMALL):
        shape = params[n].shape
        grads[n], deltas[n], new_m[n], new_v[n] = (a.reshape(shape) for a in (sg[n], d_s[i], m_s[i], v_s[i]))

    me = _flat(*_my_place()).astype(jnp.int32).reshape(1)
    after = (grad_x, d_s[0])
    src_of, land_of = {}, {}
    for gi, (keys, send, recv, srcs, lands) in enumerate(started):
        srcs, lands = _rs_wait(f"rs_wait{gi}", send, recv, srcs, lands, after)
        for k, s, l in zip(keys, srcs, lands):
            src_of[k], land_of[k] = s, l
        done = [n for n in BIG if n not in grads and all((n, l) in src_of for l in range(params[n].shape[0]))]
        for n in done:
            layers = range(params[n].shape[0])
            grads[n], deltas[n], new_m[n], new_v[n] = _adamw_big(
                "adamw_" + n, params[n], given["m_" + n], given["v_" + n],
                [src_of[(n, l)] for l in layers], [land_of[(n, l)] for l in layers], me)
        if done:
            after = (deltas[done[-1]],)

    return (loss, grad_x[None], *[grads[n] for n in WEIGHTS], *[deltas[n] for n in WEIGHTS],
            *[new_m[n] for n in WEIGHTS], *[new_v[n] for n in WEIGHTS])
```

```python
import functools
import math

import jax
import jax.numpy as jnp
from jax import lax
from jax.experimental import pallas as pl
from jax.experimental.pallas import tpu as pltpu

F32 = jnp.float32
BF16 = jnp.bfloat16
MESH = pl.DeviceIdType.MESH
ANY = pl.BlockSpec(memory_space=pl.ANY)

N_DEV = 8
D_MODEL = 1024
CHUNK = 64
EPS = 1e-6
ROPE_THETA = 10000.0
RET_HEADS = 4
RET_DK = 256
RET_DV = 512
RET_QK_W = RET_HEADS * RET_DK
RET_V_W = RET_HEADS * RET_DV
RET_IN = 2 * RET_QK_W + 2 * RET_V_W
MLA_HEADS = 8
MLA_NOPE = 128
MLA_ROPE = 64
MLA_QKD = MLA_NOPE + MLA_ROPE
MLA_VD = 128
MLA_Q_RANK = 384
MLA_KV_RANK = 256
MLA_IN = MLA_Q_RANK + MLA_KV_RANK + MLA_ROPE
MLA_IN_PAD = 768
MLA_HD_PAD = 256
D_FF = 4096
PLE_DIM = 256
ATT_SCALE = MLA_QKD ** -0.5
LOG2E = 1.4426950408889634
ATT_EXP2 = ATT_SCALE * LOG2E

ADAM_LR = 0.001
ADAM_B1 = 0.9
ADAM_B2 = 0.999
ADAM_EPS = 1e-08
ADAM_WD = 0.01
ADAM_STEP = 10

VMEM_LIMIT = 52 * 1024 * 1024
ROW_TILE = 1024
RET_ROWS = 256
ATT_BLOCK = 256
ATT_QROWS = 512
ATT_HEADS = 2

WEIGHTS = ['mix_norm', 'ret_w_in', 'ret_gn', 'ret_w_out', 'mla_w_in', 'mla_q_a_norm', 'mla_kv_a_norm',
           'mla_w_uq', 'mla_w_ukv', 'mla_q_norm', 'mla_k_norm', 'mla_w_out', 'mlp_norm', 'mlp_w1', 'mlp_w2',
           'ple_norm', 'ple_gate_w', 'ple_proj_w']
BIG = ['ret_w_in', 'ret_w_out', 'mla_w_in', 'mla_w_uq', 'mla_w_ukv', 'mla_w_out', 'mlp_w1', 'mlp_w2',
       'ple_gate_w', 'ple_proj_w']
SMALL = [w for w in WEIGHTS if w not in BIG]


def _cparams(sem=None):
    return pltpu.CompilerParams(dimension_semantics=sem, vmem_limit_bytes=VMEM_LIMIT)


def _dot(a, b, ca, cb):
    return lax.dot_general(a, b, (((ca,), (cb,)), ((), ())), preferred_element_type=F32)


def _bf(v):
    return v if v.dtype == BF16 else v.astype(BF16)


def _sigmoid(z):
    return 1.0 / (1.0 + jnp.exp(-z))


def _mm(name, grid, a, a_spec, b, b_spec, contract, outs, extras=(), epi=None, deps=()):
    nk = grid[2]
    n_ex, n_out, n_dep = len(extras), len(outs), len(deps)
    acc_shape = tuple(d for d in outs[0][1].block_shape if d is not None)

    def body(*refs):
        a_ref, b_ref = refs[:2]
        ex_refs = refs[2:2 + n_ex]
        out_refs = refs[2 + n_ex + n_dep:2 + n_ex + n_dep + n_out]

        def product():
            return _dot(_bf(a_ref[...]), _bf(b_ref[...]), contract[0], contract[1])

        def finish(acc):
            res = epi(acc, *[r[...] for r in ex_refs]) if epi is not None else (acc,)
            for o, r in zip(out_refs, res):
                o[...] = r.astype(o.dtype)

        if nk == 1:
            finish(product())
        else:
            acc_ref = refs[-1]
            k = pl.program_id(2)

            @pl.when(k == 0)
            def _():
                acc_ref[...] = jnp.zeros_like(acc_ref)

            acc_ref[...] += product()

            @pl.when(k == nk - 1)
            def _():
                finish(acc_ref[...])

    return pl.pallas_call(
        body, name=name, grid=grid,
        in_specs=[a_spec, b_spec] + [s for _, s in extras] + [ANY] * n_dep,
        out_specs=[s for _, s in outs],
        out_shape=[s for s, _ in outs],
        scratch_shapes=[pltpu.VMEM(acc_shape, F32)] if nk > 1 else [],
        compiler_params=_cparams(("parallel", "parallel", "arbitrary")),
    )(a, b, *[x for x, _ in extras], *deps)


def _sds(shape, dtype):
    return jax.ShapeDtypeStruct(shape, dtype)


def _row_tile(t, cap=ROW_TILE):
    return min(cap, t)


def _rms_fwd(name, x, g):
    t, d = x.shape
    tm = _row_tile(t)

    def body(x_ref, g_ref, o_ref):
        xv = x_ref[...]
        r = lax.rsqrt(jnp.mean(xv * xv, axis=-1, keepdims=True) + EPS)
        o_ref[...] = (xv * r * g_ref[...]).astype(o_ref.dtype)

    return pl.pallas_call(
        body, name=name, grid=(t // tm,),
        in_specs=[pl.BlockSpec((tm, d), lambda i: (i, 0)), pl.BlockSpec((1, d), lambda i: (0, 0))],
        out_specs=pl.BlockSpec((tm, d), lambda i: (i, 0)),
        out_shape=_sds((t, d), BF16),
        compiler_params=_cparams(("parallel",)),
    )(x, g)


def _rms_bwd_rows(dy, xv, g, n):
    r = lax.rsqrt(jnp.sum(xv * xv, axis=-1, keepdims=True) / n + EPS)
    xh = xv * r
    dxh = dy * g
    dx = r * (dxh - xh * (jnp.sum(dxh * xh, axis=-1, keepdims=True) / n))
    return dx, dy * xh


def _rms_bwd(name, dy, x, g, res):
    t, d = x.shape
    tm = _row_tile(t, 512)

    def body(dy_ref, x_ref, g_ref, res_ref, dx_ref, dg_ref):
        @pl.when(pl.program_id(0) == 0)
        def _():
            dg_ref[...] = jnp.zeros_like(dg_ref)

        dx, dgr = _rms_bwd_rows(dy_ref[...], x_ref[...], g_ref[...], d)
        dx_ref[...] = res_ref[...] + dx
        dg_ref[...] += jnp.sum(dgr, axis=0, keepdims=True)

    row = pl.BlockSpec((tm, d), lambda i: (i, 0))
    vec = pl.BlockSpec((1, d), lambda i: (0, 0))
    return pl.pallas_call(
        body, name=name, grid=(t // tm,),
        in_specs=[row, row, vec, row], out_specs=[row, vec],
        out_shape=[_sds((t, d), F32), _sds((1, d), F32)],
        compiler_params=_cparams(("arbitrary",)),
    )(dy, x, g, res)


def _loss_head(y, target):
    t, d = y.shape
    tm = _row_tile(t)

    def body(y_ref, t_ref, dy_ref, l_ref):
        @pl.when(pl.program_id(0) == 0)
        def _():
            l_ref[...] = jnp.zeros_like(l_ref)

        e = y_ref[...] - t_ref[...]
        dy_ref[...] = e / d
        l_ref[...] += jnp.sum(jnp.sum(e * e, axis=-1, keepdims=True), axis=0, keepdims=True)

    row = pl.BlockSpec((tm, d), lambda i: (i, 0))
    return pl.pallas_call(
        body, name="loss_head", grid=(t // tm,),
        in_specs=[row, row], out_specs=[row, pl.BlockSpec((8, 128), lambda i: (0, 0))],
        out_shape=[_sds((t, d), F32), _sds((8, 128), F32)],
        compiler_params=_cparams(("arbitrary",)),
    )(y, target)


def _ple_gate_bwd(name, dh, gate, e):
    t, d = dh.shape
    tm = _row_tile(t)

    def body(dh_ref, g_ref, e_ref, de_ref, dz_ref):
        dh_v, gt = dh_ref[...], g_ref[...]
        de_ref[...] = (dh_v * gt).astype(BF16)
        dz_ref[...] = (dh_v * e_ref[...] * (gt * (1.0 - gt))).astype(BF16)

    row = pl.BlockSpec((tm, d), lambda i: (i, 0))
    return pl.pallas_call(
        body, name=name, grid=(t // tm,), in_specs=[row, row, row], out_specs=[row, row],
        out_shape=[_sds((t, d), BF16), _sds((t, d), BF16)],
        compiler_params=_cparams(("parallel",)),
    )(dh, gate, e)


def _rope_half(v, cos, sin):
    half = v.shape[-1] // 2
    v1, v2 = v[:, :half], v[:, half:]
    return jnp.concatenate([v1 * cos - v2 * sin, v2 * cos + v1 * sin], axis=-1)


def _ret_consts():
    lg = jnp.log(1.0 - 2.0 ** (-5.0 - jnp.arange(RET_HEADS, dtype=F32)))
    idx = jnp.arange(CHUNK, dtype=F32)
    intra = jnp.exp(lg[:, None, None] * jnp.abs(idx[:, None] - idx[None, :]))
    qdec = jnp.exp(lg[:, None] * (idx + 1.0))
    kdec = jnp.exp(lg[:, None] * (CHUNK - 1.0 - idx))
    cdec = jnp.exp(lg * CHUNK)
    qdec = jnp.broadcast_to(qdec[:, :, None], (RET_HEADS, CHUNK, RET_DK))
    kdec = jnp.broadcast_to(kdec[:, :, None], (RET_HEADS, CHUNK, RET_DK))
    cdec = jnp.broadcast_to(cdec[:, None, None], (RET_HEADS, 1, RET_DV))
    return intra, qdec, kdec, cdec


def _ret_specs(rb, rev_nb=None):
    blk = (lambda i: i) if rev_nb is None else (lambda i: rev_nb - 1 - i)
    full = lambda shape: pl.BlockSpec(shape, lambda i: (0,) * len(shape))
    return dict(
        proj=pl.BlockSpec((rb, RET_IN), lambda i: (blk(i), 0)),
        tab=pl.BlockSpec((rb, RET_DK // 2), lambda i: (blk(i), 0)),
        vw=pl.BlockSpec((rb, RET_V_W), lambda i: (blk(i), 0)),
        st=pl.BlockSpec((rb // CHUNK, RET_HEADS, RET_DK, RET_DV), lambda i: (blk(i), 0, 0, 0)),
        gn=full((RET_HEADS, 1, RET_DV)),
        intra=full((RET_HEADS, CHUNK, CHUNK)),
        dec=full((RET_HEADS, CHUNK, RET_DK)),
        cdec=full((RET_HEADS, 1, RET_DV)),
    )


def _ret_fwd(proj, cos, sin, gn):
    t = proj.shape[0]
    rb = min(RET_ROWS, t)
    cpb = rb // CHUNK
    intra, qdec, kdec, cdec = _ret_consts()
    sp = _ret_specs(rb)

    def body(proj_ref, cos_ref, sin_ref, gn_ref, intra_ref, qd_ref, kd_ref, cd_ref,
             gated_ref, outp_ref, st_ref, s_ref):
        @pl.when(pl.program_id(0) == 0)
        def _():
            s_ref[...] = jnp.zeros_like(s_ref)

        def chunk(c, carry):
            rows = pl.ds(pl.multiple_of(c * CHUNK, CHUNK), CHUNK)
            cs, sn = cos_ref[rows, :], sin_ref[rows, :]
            for h in range(RET_HEADS):
                q = proj_ref[rows, h * RET_DK:(h + 1) * RET_DK]
                k = proj_ref[rows, RET_QK_W + h * RET_DK:RET_QK_W + (h + 1) * RET_DK]
                v = proj_ref[rows, 2 * RET_QK_W + h * RET_DV:2 * RET_QK_W + (h + 1) * RET_DV]
                g = proj_ref[rows, 2 * RET_QK_W + RET_V_W + h * RET_DV:2 * RET_QK_W + RET_V_W + (h + 1) * RET_DV]
                qr = _rope_half(q, cs, sn)
                kr = _rope_half(k, cs, sn) * (RET_DK ** -0.5)
                qb, kb, vb = qr.astype(BF16), kr.astype(BF16), v.astype(BF16)
                sc = _dot(qb, kb, 1, 1) * intra_ref[h]
                inner = _dot(sc.astype(BF16), vb, 1, 0)
                s_old = s_ref[h]
                sb = s_old.astype(BF16)
                st_ref[c, h] = sb
                cross = _dot((qr * qd_ref[h]).astype(BF16), sb, 1, 0)
                out = inner + cross
                s_ref[h] = s_old * cd_ref[h] + _dot((kr * kd_ref[h]).astype(BF16), vb, 0, 0)
                r = lax.rsqrt(jnp.mean(out * out, axis=-1, keepdims=True) + EPS)
                y = out * r * gn_ref[h]
                cols = slice(h * RET_DV, (h + 1) * RET_DV)
                gated_ref[rows, cols] = (g * _sigmoid(g) * y).astype(BF16)
                outp_ref[rows, cols] = out
            return carry

        lax.fori_loop(0, cpb, chunk, 0)

    return pl.pallas_call(
        body, name="ret_fwd", grid=(t // rb,),
        in_specs=[sp['proj'], sp['tab'], sp['tab'], sp['gn'], sp['intra'], sp['dec'], sp['dec'], sp['cdec']],
        out_specs=[sp['vw'], sp['vw'], sp['st']],
        out_shape=[_sds((t, RET_V_W), BF16), _sds((t, RET_V_W), F32),
                   _sds((t // CHUNK, RET_HEADS, RET_DK, RET_DV), BF16)],
        scratch_shapes=[pltpu.VMEM((RET_HEADS, RET_DK, RET_DV), F32)],
        compiler_params=_cparams(("arbitrary",)),
    )(proj, cos, sin, gn.reshape(RET_HEADS, 1, RET_DV), intra, qdec, kdec, cdec)


def _ret_bwd(proj, cos, sin, gn, outp, states, dgated):
    t = proj.shape[0]
    rb = min(RET_ROWS, t)
    cpb = rb // CHUNK
    nb = t // rb
    intra, qdec, kdec, cdec = _ret_consts()
    sp = _ret_specs(rb, rev_nb=nb)

    def body(proj_ref, cos_ref, sin_ref, gn_ref, intra_ref, qd_ref, kd_ref, cd_ref, outp_ref, st_ref, dgt_ref,
             dproj_ref, dgn_ref, ds_ref):
        @pl.when(pl.program_id(0) == 0)
        def _():
            ds_ref[...] = jnp.zeros_like(ds_ref)
            dgn_ref[...] = jnp.zeros_like(dgn_ref)

        def chunk(cc, carry):
            c = cpb - 1 - cc
            rows = pl.ds(pl.multiple_of(c * CHUNK, CHUNK), CHUNK)
            cs, sn = cos_ref[rows, :], sin_ref[rows, :]
            for h in range(RET_HEADS):
                q = proj_ref[rows, h * RET_DK:(h + 1) * RET_DK]
                k = proj_ref[rows, RET_QK_W + h * RET_DK:RET_QK_W + (h + 1) * RET_DK]
                v = proj_ref[rows, 2 * RET_QK_W + h * RET_DV:2 * RET_QK_W + (h + 1) * RET_DV]
                g = proj_ref[rows, 2 * RET_QK_W + RET_V_W + h * RET_DV:2 * RET_QK_W + RET_V_W + (h + 1) * RET_DV]
                cols = slice(h * RET_DV, (h + 1) * RET_DV)
                qr = _rope_half(q, cs, sn)
                kr = _rope_half(k, cs, sn) * (RET_DK ** -0.5)
                qb, kb, vb = qr.astype(BF16), kr.astype(BF16), v.astype(BF16)
                qdb = (qr * qd_ref[h]).astype(BF16)
                kdb = (kr * kd_ref[h]).astype(BF16)
                out = outp_ref[rows, cols]
                dgt = dgt_ref[rows, cols]
                gnh = gn_ref[h]
                r = lax.rsqrt(jnp.mean(out * out, axis=-1, keepdims=True) + EPS)
                xh = out * r
                sg = _sigmoid(g)
                dgate = dgt * (xh * gnh) * (sg * (1.0 + g * (1.0 - sg)))
                dy = dgt * (g * sg)
                dgn_ref[h] += jnp.sum(dy * xh, axis=0, keepdims=True)
                dxh = dy * gnh
                dout = r * (dxh - xh * jnp.mean(dxh * xh, axis=-1, keepdims=True))
                doutb = dout.astype(BF16)
                itr = intra_ref[h]
                pb = (_dot(qb, kb, 1, 1) * itr).astype(BF16)
                dv = _dot(pb, doutb, 0, 0)
                dsc = (_dot(doutb, vb, 1, 1) * itr).astype(BF16)
                dq = _dot(dsc, kb, 1, 0)
                dk = _dot(dsc, qb, 0, 0)
                dq = dq + _dot(doutb, st_ref[c, h], 1, 1) * qd_ref[h]
                ds_new = ds_ref[h]
                dsb = ds_new.astype(BF16)
                dk = dk + _dot(vb, dsb, 1, 1) * kd_ref[h]
                dv = dv + _dot(kdb, dsb, 1, 0)
                ds_ref[h] = ds_new * cd_ref[h] + _dot(qdb, doutb, 0, 0)
                dproj_ref[rows, h * RET_DK:(h + 1) * RET_DK] = _rope_half(dq, cs, -sn).astype(BF16)
                dproj_ref[rows, RET_QK_W + h * RET_DK:RET_QK_W + (h + 1) * RET_DK] = (
                    _rope_half(dk * (RET_DK ** -0.5), cs, -sn).astype(BF16))
                dproj_ref[rows, 2 * RET_QK_W + h * RET_DV:2 * RET_QK_W + (h + 1) * RET_DV] = dv.astype(BF16)
                dproj_ref[rows, 2 * RET_QK_W + RET_V_W + h * RET_DV:
                          2 * RET_QK_W + RET_V_W + (h + 1) * RET_DV] = dgate.astype(BF16)
            return carry

        lax.fori_loop(0, cpb, chunk, 0)

    return pl.pallas_call(
        body, name="ret_bwd", grid=(nb,),
        in_specs=[sp['proj'], sp['tab'], sp['tab'], sp['gn'], sp['intra'], sp['dec'], sp['dec'], sp['cdec'],
                  sp['vw'], sp['st'], sp['vw']],
        out_specs=[sp['proj'], sp['gn']],
        out_shape=[_sds((t, RET_IN), BF16), _sds((RET_HEADS, 1, RET_DV), F32)],
        scratch_shapes=[pltpu.VMEM((RET_HEADS, RET_DK, RET_DV), F32)],
        compiler_params=_cparams(("arbitrary",)),
    )(proj, cos, sin, gn.reshape(RET_HEADS, 1, RET_DV), intra, qdec, kdec, cdec, outp, states, dgated)


def _mla_tables(t):
    half = MLA_ROPE // 2
    inv = 1.0 / (ROPE_THETA ** (jnp.arange(0, MLA_ROPE, 2, dtype=F32) / MLA_ROPE))
    ang = jnp.arange(t, dtype=F32)[:, None] * inv[None, :]
    cos, sin = jnp.cos(ang), jnp.sin(ang)
    z = jnp.zeros((t, half), F32)
    c = jnp.concatenate([cos, cos, z, z], axis=1)
    s1 = jnp.concatenate([-sin, z, z, z], axis=1)
    s2 = jnp.concatenate([z, sin, z, z], axis=1)
    return c, s1, s2


def _rope_tile(r, c, s1, s2):
    return r * c + pltpu.roll(r, 96, 1) * s1 + pltpu.roll(r, 32, 1) * s2


def _mla_mid(proj2, qa, kva):
    t = proj2.shape[0]
    tm = _row_tile(t)

    def body(p_ref, qa_ref, kva_ref, cq_ref, ckv_ref):
        cq = p_ref[:, :MLA_Q_RANK]
        ckv = p_ref[:, MLA_Q_RANK:MLA_Q_RANK + MLA_KV_RANK]
        rq = lax.rsqrt(jnp.mean(cq * cq, axis=-1, keepdims=True) + EPS)
        rkv = lax.rsqrt(jnp.mean(ckv * ckv, axis=-1, keepdims=True) + EPS)
        cq_ref[...] = (cq * rq * qa_ref[...]).astype(BF16)
        ckv_ref[...] = (ckv * rkv * kva_ref[...]).astype(BF16)

    return pl.pallas_call(
        body, name="mla_mid", grid=(t // tm,),
        in_specs=[pl.BlockSpec((tm, MLA_IN_PAD), lambda i: (i, 0)),
                  pl.BlockSpec((1, MLA_Q_RANK), lambda i: (0, 0)),
                  pl.BlockSpec((1, MLA_KV_RANK), lambda i: (0, 0))],
        out_specs=[pl.BlockSpec((tm, MLA_Q_RANK), lambda i: (i, 0)),
                   pl.BlockSpec((tm, MLA_KV_RANK), lambda i: (i, 0))],
        out_shape=[_sds((t, MLA_Q_RANK), BF16), _sds((t, MLA_KV_RANK), BF16)],
        compiler_params=_cparams(("parallel",)),
    )(proj2, qa, kva)


def _mla_mid_bwd(proj2, qa, kva, dcq, dckv, dkr):
    t = proj2.shape[0]
    tm = _row_tile(t)

    def body(p_ref, qa_ref, kva_ref, dcq_ref, dckv_ref, dkr_ref, dp_ref, dqa_ref, dkva_ref):
        @pl.when(pl.program_id(0) == 0)
        def _():
            dqa_ref[...] = jnp.zeros_like(dqa_ref)
            dkva_ref[...] = jnp.zeros_like(dkva_ref)

        dxq, dgq = _rms_bwd_rows(dcq_ref[...], p_ref[:, :MLA_Q_RANK], qa_ref[...], MLA_Q_RANK)
        dxk, dgk = _rms_bwd_rows(dckv_ref[...], p_ref[:, MLA_Q_RANK:MLA_Q_RANK + MLA_KV_RANK], kva_ref[...],
                                 MLA_KV_RANK)
        dp_ref[:, :MLA_Q_RANK] = dxq.astype(BF16)
        dp_ref[:, MLA_Q_RANK:MLA_Q_RANK + MLA_KV_RANK] = dxk.astype(BF16)
        dp_ref[:, MLA_Q_RANK + MLA_KV_RANK:] = dkr_ref[...].astype(BF16)
        dqa_ref[...] += jnp.sum(dgq, axis=0, keepdims=True)
        dkva_ref[...] += jnp.sum(dgk, axis=0, keepdims=True)

    return pl.pallas_call(
        body, name="mla_mid_bwd", grid=(t // tm,),
        in_specs=[pl.BlockSpec((tm, MLA_IN_PAD), lambda i: (i, 0)),
                  pl.BlockSpec((1, MLA_Q_RANK), lambda i: (0, 0)),
                  pl.BlockSpec((1, MLA_KV_RANK), lambda i: (0, 0)),
                  pl.BlockSpec((tm, MLA_Q_RANK), lambda i: (i, 0)),
                  pl.BlockSpec((tm, MLA_KV_RANK), lambda i: (i, 0)),
                  pl.BlockSpec((tm, 128), lambda i: (i, 0))],
        out_specs=[pl.BlockSpec((tm, MLA_IN_PAD), lambda i: (i, 0)),
                   pl.BlockSpec((1, MLA_Q_RANK), lambda i: (0, 0)),
                   pl.BlockSpec((1, MLA_KV_RANK), lambda i: (0, 0))],
        out_shape=[_sds((t, MLA_IN_PAD), BF16), _sds((1, MLA_Q_RANK), F32), _sds((1, MLA_KV_RANK), F32)],
        compiler_params=_cparams(("arbitrary",)),
    )(proj2, qa, kva, dcq, dckv, dkr)


def _mla_prep_specs(t, tm):
    head = lambda w: pl.BlockSpec((None, tm, w), lambda i, h: (h, i, 0))
    return dict(
        head256=head(MLA_HD_PAD), head128=head(MLA_VD),
        kr=pl.BlockSpec((tm, 128), lambda i, h: (i, (MLA_Q_RANK + MLA_KV_RANK) // 128)),
        gain=pl.BlockSpec((1, MLA_HD_PAD), lambda i, h: (0, 0)),
        tab=pl.BlockSpec((tm, 128), lambda i, h: (i, 0)),
    )


def _mla_prep(q, kv, proj2, gq, gk, tabs):
    t = q.shape[1]
    tm = _row_tile(t)
    sp = _mla_prep_specs(t, tm)

    def body(q_ref, kv_ref, kr_ref, gq_ref, gk_ref, c_ref, s1_ref, s2_ref, qh_ref, kh_ref, vh_ref):
        c, s1, s2 = c_ref[...], s1_ref[...], s2_ref[...]

        def norm_rope(xv, gain):
            r = lax.rsqrt(jnp.sum(xv * xv, axis=-1, keepdims=True) / MLA_QKD + EPS)
            y = xv * r * gain
            return jnp.concatenate([y[:, :MLA_NOPE], _rope_tile(y[:, MLA_NOPE:], c, s1, s2)], axis=-1)

        kvv = kv_ref[...]
        qh_ref[...] = norm_rope(q_ref[...], gq_ref[...]).astype(BF16)
        kf = jnp.concatenate([kvv[:, :MLA_NOPE], kr_ref[...]], axis=-1)
        kh_ref[...] = norm_rope(kf, gk_ref[...]).astype(BF16)
        vh_ref[...] = jnp.concatenate([kvv[:, MLA_NOPE:], jnp.ones((tm, MLA_VD), F32)], axis=-1).astype(BF16)

    return pl.pallas_call(
        body, name="mla_prep", grid=(t // tm, MLA_HEADS),
        in_specs=[sp['head256'], sp['head256'], sp['kr'], sp['gain'], sp['gain'], sp['tab'], sp['tab'], sp['tab']],
        out_specs=[sp['head256'], sp['head256'], sp['head256']],
        out_shape=[_sds((MLA_HEADS, t, MLA_HD_PAD), BF16), _sds((MLA_HEADS, t, MLA_HD_PAD), BF16),
                   _sds((MLA_HEADS, t, 2 * MLA_VD), BF16)],
        compiler_params=_cparams(("parallel", "arbitrary")),
    )(q, kv, proj2, gq, gk, *tabs)


def _mla_prep_bwd(q, kv, proj2, gq, gk, tabs, dqt, dkh, dvh):
    t = q.shape[1]
    tm = _row_tile(t)
    ab = dqt.shape[-1]
    sp = _mla_prep_specs(t, tm)

    def body(q_ref, kv_ref, kr_ref, gq_ref, gk_ref, c_ref, s1_ref, s2_ref, dqt_ref, dkh_ref, dvh_ref,
             dq_ref, dkv_ref, dkr_ref, dgq_ref, dgk_ref):
        dqh = jnp.concatenate([dqt_ref[b].T for b in range(tm // ab)], axis=0)
        i, h = pl.program_id(0), pl.program_id(1)

        @pl.when((i == 0) & (h == 0))
        def _():
            dgq_ref[...] = jnp.zeros_like(dgq_ref)
            dgk_ref[...] = jnp.zeros_like(dgk_ref)

        @pl.when(h == 0)
        def _():
            dkr_ref[...] = jnp.zeros_like(dkr_ref)

        c, s1, s2 = c_ref[...], s1_ref[...], s2_ref[...]

        def back(xv, gain, dout):
            dy = jnp.concatenate([dout[:, :MLA_NOPE], _rope_tile(dout[:, MLA_NOPE:], c, -s1, -s2)], axis=-1)
            return _rms_bwd_rows(dy, xv, gain, MLA_QKD)

        kvv = kv_ref[...]
        dxq, dgq = back(q_ref[...], gq_ref[...], dqh)
        kf = jnp.concatenate([kvv[:, :MLA_NOPE], kr_ref[...]], axis=-1)
        dxk, dgk = back(kf, gk_ref[...], dkh_ref[...])
        dq_ref[...] = dxq.astype(BF16)
        dkv_ref[...] = jnp.concatenate([dxk[:, :MLA_NOPE], dvh_ref[...]], axis=-1).astype(BF16)
        dkr_ref[...] += dxk[:, MLA_NOPE:]
        dgq_ref[...] += jnp.sum(dgq, axis=0, keepdims=True)
        dgk_ref[...] += jnp.sum(dgk, axis=0, keepdims=True)

    return pl.pallas_call(
        body, name="mla_prep_bwd", grid=(t // tm, MLA_HEADS),
        in_specs=[sp['head256'], sp['head256'], sp['kr'], sp['gain'], sp['gain'], sp['tab'], sp['tab'], sp['tab'],
                  pl.BlockSpec((None, tm // ab, MLA_HD_PAD, ab), lambda i, h: (h, i, 0, 0)),
                  sp['head256'], sp['head128']],
        out_specs=[sp['head256'], sp['head256'], sp['tab'], sp['gain'], sp['gain']],
        out_shape=[_sds((MLA_HEADS, t, MLA_HD_PAD), BF16), _sds((MLA_HEADS, t, MLA_HD_PAD), BF16),
                   _sds((t, 128), F32), _sds((1, MLA_HD_PAD), F32), _sds((1, MLA_HD_PAD), F32)],
        compiler_params=_cparams(("arbitrary", "arbitrary")),
    )(q, kv, proj2, gq, gk, *tabs, dqt, dkh, dvh)


def _chunk_visible(rows, cols, row_off, col_off):
    rq = lax.shift_right_logical(lax.broadcasted_iota(jnp.int32, (rows, cols), 0) + row_off, 6)
    ck = lax.shift_right_logical(lax.broadcasted_iota(jnp.int32, (rows, cols), 1) + col_off, 6)
    return ck <= rq


def _rows_to_lanes(col):
    return col.T[:8, :]


def _attn_fwd(qh, kh, vh):
    t = qh.shape[1]
    ab = min(ATT_BLOCK, t)
    tq = min(ATT_QROWS, t)
    r = tq // ab
    hg = ATT_HEADS

    def body(q_ref, k_ref, v_ref, o_ref, lse_ref):
        n_un = pl.program_id(1) * r

        def step(b, state, diag):
            rows = pl.ds(pl.multiple_of(b * ab, ab), ab)
            ms, accs = [], []
            for hh in range(hg):
                m, acc = state[0][hh], state[1][hh]
                s = _dot(q_ref[hh], k_ref[hh, rows, :], 1, 1)
                if diag is not None:
                    s = jnp.where(_chunk_visible(tq, ab, 0, diag * ab), s, -1e30)
                m_new = jnp.maximum(m, jnp.max(s, axis=-1, keepdims=True))
                p = jnp.exp2((s - m_new) * ATT_EXP2).astype(BF16)
                accs.append(jnp.exp2((m - m_new) * ATT_EXP2) * acc + _dot(p, v_ref[hh, rows, :], 1, 0))
                ms.append(m_new)
            return tuple(ms), tuple(accs)

        heads = lambda v: tuple(v for _ in range(hg))
        state = (heads(jnp.full((tq, 1), -1e30, F32)), heads(jnp.zeros((tq, 2 * MLA_VD), F32)))
        state = lax.fori_loop(0, n_un, lambda b, st: step(b, st, None), state)
        for d in range(r):
            state = step(n_un + d, state, d)
        ms, accs = state
        for hh in range(hg):
            l = accs[hh][:, MLA_VD:]
            o_ref[:, hh * MLA_VD:(hh + 1) * MLA_VD] = accs[hh][:, :MLA_VD] / l
            lse_t = _rows_to_lanes(ms[hh] * ATT_EXP2 + jnp.log(l) * LOG2E)
            for d in range(r):
                lse_ref[hh, d] = lse_t[:, d * ab:(d + 1) * ab]

    return pl.pallas_call(
        body, name="mla_attn", grid=(MLA_HEADS // hg, t // tq),
        in_specs=[pl.BlockSpec((hg, tq, MLA_HD_PAD), lambda g, i: (g, i, 0)),
                  pl.BlockSpec((hg, t, MLA_HD_PAD), lambda g, i: (g, 0, 0)),
                  pl.BlockSpec((hg, t, 2 * MLA_VD), lambda g, i: (g, 0, 0))],
        out_specs=[pl.BlockSpec((tq, hg * MLA_VD), lambda g, i: (i, g)),
                   pl.BlockSpec((hg, r, 8, ab), lambda g, i: (g, i, 0, 0))],
        out_shape=[_sds((t, MLA_HEADS * MLA_VD), F32), _sds((MLA_HEADS, t // ab, 8, ab), F32)],
        compiler_params=_cparams(("parallel", "arbitrary")),
    )(qh, kh, vh)


def _attn_delta(do, o, ab):
    t = do.shape[0]
    tm = _row_tile(t)

    def body(do_ref, o_ref, d_ref):
        d = jnp.sum(do_ref[...] * o_ref[...], axis=-1, keepdims=True)
        d_t = _rows_to_lanes(jnp.broadcast_to(d, (tm, 128)))
        for b in range(tm // ab):
            d_ref[b] = d_t[:, b * ab:(b + 1) * ab]

    col = pl.BlockSpec((tm, MLA_VD), lambda i, h: (i, h))
    return pl.pallas_call(
        body, name="mla_delta", grid=(t // tm, MLA_HEADS), in_specs=[col, col],
        out_specs=pl.BlockSpec((None, tm // ab, 8, ab), lambda i, h: (h, i, 0, 0)),
        out_shape=_sds((MLA_HEADS, t // ab, 8, ab), F32),
        compiler_params=_cparams(("parallel", "parallel")),
    )(do, o)


def _attn_bwd(qh, kh, vh, dob, lse_t, dl_t):
    t = qh.shape[1]
    ab = min(ATT_BLOCK, t)
    nq = t // ab
    hg = ATT_HEADS

    def body(q_ref, k_ref, v_ref, do_ref, lse_ref, dl_ref, dqt_ref, dk_ref, dv_ref):
        j = pl.program_id(1)

        @pl.when(j == 0)
        def _():
            dqt_ref[...] = jnp.zeros_like(dqt_ref)

        ks = [k_ref[hh] for hh in range(hg)]
        vs = [v_ref[hh, :, :MLA_VD] for hh in range(hg)]
        kts = [k.T for k in ks]

        def step(b, grads, masked):
            rows = pl.ds(pl.multiple_of(b * ab, ab), ab)
            out = []
            for hh in range(hg):
                dk, dv = grads[hh]
                q = q_ref[hh, rows, :]
                do = do_ref[rows, hh * MLA_VD:(hh + 1) * MLA_VD]
                s_t = _dot(ks[hh], q, 1, 1)
                if masked:
                    key_chunk = lax.shift_right_logical(lax.broadcasted_iota(jnp.int32, (ab, ab), 0), 6)
                    query_chunk = lax.shift_right_logical(lax.broadcasted_iota(jnp.int32, (ab, ab), 1), 6)
                    s_t = jnp.where(key_chunk <= query_chunk, s_t, -1e30)
                p_t = jnp.exp2(s_t * ATT_EXP2 - lse_ref[hh, b][0:1, :])
                dp_t = _dot(vs[hh], do, 1, 1)
                ds_t = (p_t * (dp_t - dl_ref[hh, b][0:1, :]) * ATT_SCALE).astype(BF16)
                dqt_ref[hh, b] += _dot(kts[hh], ds_t, 1, 0)
                out.append((dk + _dot(ds_t, q, 1, 0), dv + _dot(p_t.astype(BF16), do, 1, 0)))
            return tuple(out)

        grads = tuple((jnp.zeros((ab, MLA_HD_PAD), F32), jnp.zeros((ab, MLA_VD), F32)) for _ in range(hg))
        grads = step(j, grads, True)
        grads = lax.fori_loop(j + 1, nq, lambda b, g: step(b, g, False), grads)
        for hh in range(hg):
            dk_ref[hh] = grads[hh][0]
            dv_ref[hh] = grads[hh][1]

    whole = lambda w: pl.BlockSpec((hg, t, w), lambda g, j: (g, 0, 0))
    blk = lambda w: pl.BlockSpec((hg, ab, w), lambda g, j: (g, j, 0))
    stat = pl.BlockSpec((hg, nq, 8, ab), lambda g, j: (g, 0, 0, 0))
    return pl.pallas_call(
        body, name="mla_attn_bwd", grid=(MLA_HEADS // hg, nq),
        in_specs=[whole(MLA_HD_PAD), blk(MLA_HD_PAD), blk(2 * MLA_VD),
                  pl.BlockSpec((t, hg * MLA_VD), lambda g, j: (0, g)), stat, stat],
        out_specs=[pl.BlockSpec((hg, nq, MLA_HD_PAD, ab), lambda g, j: (g, 0, 0, 0)), blk(MLA_HD_PAD), blk(MLA_VD)],
        out_shape=[_sds((MLA_HEADS, nq, MLA_HD_PAD, ab), F32), _sds((MLA_HEADS, t, MLA_HD_PAD), F32),
                   _sds((MLA_HEADS, t, MLA_VD), F32)],
        compiler_params=_cparams(("parallel", "arbitrary")),
    )(qh, kh, vh, dob, lse_t, dl_t)


VEC = pl.BlockSpec((1, D_MODEL), lambda i, j, k: (0, 0))


def _residual_epi(next_gain):
    if next_gain is None:
        return [], lambda acc, hv: (acc + hv,)

    def epi(acc, hv, g):
        h_new = acc + hv
        r = lax.rsqrt(jnp.mean(h_new * h_new, axis=-1, keepdims=True) + EPS)
        return h_new, h_new * r * g

    return [(next_gain, VEC)], epi


def _residual_outs(t, row, next_gain):
    outs = [(_sds((t, D_MODEL), F32), row)]
    return outs + ([(_sds((t, D_MODEL), BF16), row)] if next_gain is not None else [])


def _mlp_fwd(l, h, hn, w1g, w2g, next_gain):
    t = h.shape[0]
    tm = _row_tile(t)
    nsh, _, wsh = w1g.shape

    def relu2(acc):
        r = jnp.maximum(acc, 0.0)
        return r, r * r

    tile = pl.BlockSpec((tm, wsh), lambda i, j, k: (i, j))
    r, u = _mm(f"mlp_up{l}", (t // tm, nsh, 1),
               hn, pl.BlockSpec((tm, D_MODEL), lambda i, j, k: (i, 0)),
               w1g, pl.BlockSpec((None, D_MODEL, wsh), lambda i, j, k: (j, 0, 0)), (1, 0),
               [(_sds((t, D_FF), BF16), tile), (_sds((t, D_FF), BF16), tile)], epi=relu2)
    row = pl.BlockSpec((tm, D_MODEL), lambda i, j, k: (i, 0))
    more, epi = _residual_epi(next_gain)
    h2, hn_next = _mm(f"mlp_down{l}", (t // tm, 1, nsh),
                      u, pl.BlockSpec((tm, wsh), lambda i, j, k: (i, k)),
                      w2g, pl.BlockSpec((None, wsh, D_MODEL), lambda i, j, k: (k, 0, 0)), (1, 0),
                      _residual_outs(t, row, next_gain), extras=[(h, row)] + more, epi=epi)
    return h2, hn_next, (h, hn, r, u)


def _mlp_bwd(l, dh, saved, norm_g, w1g, w2g):
    h, hn, r, u = saved
    t = h.shape[0]
    tm = _row_tile(t)
    tk = _row_tile(t)
    nsh, _, wsh = w1g.shape
    tile = pl.BlockSpec((tm, wsh), lambda i, j, k: (i, j))
    (da,) = _mm(f"mlp_du{l}", (t // tm, nsh, 1),
                dh, pl.BlockSpec((tm, D_MODEL), lambda i, j, k: (i, 0)),
                w2g, pl.BlockSpec((None, wsh, D_MODEL), lambda i, j, k: (j, 0, 0)), (1, 1),
                [(_sds((t, D_FF), BF16), tile)], extras=[(r, tile)],
                epi=lambda acc, rv: (2.0 * rv.astype(F32) * acc,))
    (dw2,) = _mm(f"mlp_dw2{l}", (nsh, 1, t // tk),
                 u, pl.BlockSpec((tk, wsh), lambda i, j, k: (k, i)),
                 dh, pl.BlockSpec((tk, D_MODEL), lambda i, j, k: (k, 0)), (0, 0),
                 [(_sds((nsh, wsh, D_MODEL), BF16), pl.BlockSpec((None, wsh, D_MODEL), lambda i, j, k: (i, 0, 0)))])
    (dw1,) = _mm(f"mlp_dw1{l}", (1, nsh, t // tk),
                 hn, pl.BlockSpec((tk, D_MODEL), lambda i, j, k: (k, 0)),
                 da, pl.BlockSpec((tk, wsh), lambda i, j, k: (k, j)), (0, 0),
                 [(_sds((nsh, D_MODEL, wsh), BF16), pl.BlockSpec((None, D_MODEL, wsh), lambda i, j, k: (j, 0, 0)))])
    (dhn,) = _mm(f"mlp_dhn{l}", (t // tm, 1, nsh),
                 da, pl.BlockSpec((tm, wsh), lambda i, j, k: (i, k)),
                 w1g, pl.BlockSpec((None, D_MODEL, wsh), lambda i, j, k: (k, 0, 0)), (1, 1),
                 [(_sds((t, D_MODEL), F32), pl.BlockSpec((tm, D_MODEL), lambda i, j, k: (i, 0)))])
    dh_in, dg = _rms_bwd(f"mlp_norm_bwd{l}", dhn, h, norm_g, dh)
    return dh_in, dg, dw1, dw2


def _ple_fwd(l, h, hn, p, wg, wp, next_gain):
    t = h.shape[0]
    tm = _row_tile(t, 512)
    row = pl.BlockSpec((tm, D_MODEL), lambda i, j, k: (i, 0))
    full = lambda r: pl.BlockSpec((r, D_MODEL), lambda i, j, k: (0, 0))
    (e,) = _mm(f"ple_proj{l}", (t // tm, 1, 1),
               p, pl.BlockSpec((None, None, tm, PLE_DIM), lambda i, j, k: (l, 0, i, 0)),
               wp, full(PLE_DIM), (1, 0), [(_sds((t, D_MODEL), F32), row)])

    def gate_epi(acc, hv, ev, *gain):
        gt = _sigmoid(acc)
        h_new = hv + gt * ev
        if not gain:
            return h_new, gt
        r = lax.rsqrt(jnp.mean(h_new * h_new, axis=-1, keepdims=True) + EPS)
        return h_new, gt, h_new * r * gain[0]

    f32_row, bf_row = (_sds((t, D_MODEL), F32), row), (_sds((t, D_MODEL), BF16), row)
    res = _mm(f"ple_gate{l}", (t // tm, 1, 1), hn, row, wg, full(D_MODEL), (1, 0),
              [f32_row, f32_row] + ([bf_row] if next_gain is not None else []),
              extras=[(h, row), (e, row)] + ([(next_gain, VEC)] if next_gain is not None else []), epi=gate_epi)
    h_out, gate = res[0], res[1]
    return h_out, (res[2] if next_gain is not None else None), (h, hn, gate, e)


def _ple_bwd(l, dh, saved, p, norm_g, wg, deps=()):
    h, hn, gate, e = saved
    t = h.shape[0]
    tm = _row_tile(t)
    tk = _row_tile(t, 512)
    de, dz = _ple_gate_bwd(f"ple_gate_bwd{l}", dh, gate, e)
    full = lambda r: pl.BlockSpec((r, D_MODEL), lambda i, j, k: (0, 0))
    rowk = pl.BlockSpec((tk, D_MODEL), lambda i, j, k: (k, 0))
    (dwp,) = _mm(f"ple_dwp{l}", (1, 1, t // tk),
                 p, pl.BlockSpec((None, None, tk, PLE_DIM), lambda i, j, k: (l, 0, k, 0)),
                 de, rowk, (0, 0), [(_sds((PLE_DIM, D_MODEL), BF16), full(PLE_DIM))], deps=deps)
    (dwg,) = _mm(f"ple_dwg{l}", (1, 1, t // tk), hn, rowk, dz, rowk, (0, 0),
                 [(_sds((D_MODEL, D_MODEL), BF16), full(D_MODEL))])
    row = pl.BlockSpec((tm, D_MODEL), lambda i, j, k: (i, 0))
    (dhn,) = _mm(f"ple_dhn{l}", (t // tm, 1, 1), dz, row, wg, full(D_MODEL), (1, 1),
                 [(_sds((t, D_MODEL), F32), row)])
    dh_in, dg = _rms_bwd(f"ple_norm_bwd{l}", dhn, h, norm_g, dh)
    return dh_in, dg, dwg, dwp


def _ret_layer_fwd(x, norm_g, wri, fetch_wro, gn, cos, sin, next_gain, deps=()):
    t = x.shape[0]
    tm = _row_tile(t)
    nsh, _, wsh = wri.shape
    hn = _rms_fwd("mix_norm0", x, norm_g)
    (proj,) = _mm("ret_in", (t // tm, nsh, 1),
                  hn, pl.BlockSpec((tm, D_MODEL), lambda i, j, k: (i, 0)),
                  wri, pl.BlockSpec((None, D_MODEL, wsh), lambda i, j, k: (j, 0, 0)), (1, 0),
                  [(_sds((t, RET_IN), F32), pl.BlockSpec((tm, wsh), lambda i, j, k: (i, j)))], deps=deps)
    gated, outp, states = _ret_fwd(proj, cos, sin, gn)
    wro = fetch_wro((gated,))
    row = pl.BlockSpec((tm, D_MODEL), lambda i, j, k: (i, 0))
    kt = 512
    more, epi = _residual_epi(next_gain)
    h1, hn_next = _mm("ret_out", (t // tm, 1, RET_V_W // kt),
                      gated, pl.BlockSpec((tm, kt), lambda i, j, k: (i, k)),
                      wro, pl.BlockSpec((kt, D_MODEL), lambda i, j, k: (k, 0)), (1, 0),
                      _residual_outs(t, row, next_gain), extras=[(x, row)] + more, epi=epi)
    return h1, hn_next, (x, hn, proj, gated, outp, states, wro)


def _ret_layer_bwd(dh, saved, norm_g, wri, gn, cos, sin, emit, deps=()):
    x, hn, proj, gated, outp, states, wro = saved
    t = x.shape[0]
    tm = _row_tile(t)
    tk = _row_tile(t, 512)
    nsh, _, wsh = wri.shape
    (dgated,) = _mm("ret_dgated", (t // tm, RET_V_W // D_MODEL, 1),
                    dh, pl.BlockSpec((tm, D_MODEL), lambda i, j, k: (i, 0)),
                    wro, pl.BlockSpec((D_MODEL, D_MODEL), lambda i, j, k: (j, 0)), (1, 1),
                    [(_sds((t, RET_V_W), F32), pl.BlockSpec((tm, D_MODEL), lambda i, j, k: (i, j)))], deps=deps)
    kt = 512
    (dwro,) = _mm("ret_dwro", (RET_V_W // kt, 1, t // tk),
                  gated, pl.BlockSpec((tk, kt), lambda i, j, k: (k, i)),
                  dh, pl.BlockSpec((tk, D_MODEL), lambda i, j, k: (k, 0)), (0, 0),
                  [(_sds((RET_V_W, D_MODEL), BF16), pl.BlockSpec((kt, D_MODEL), lambda i, j, k: (i, 0)))])
    dproj, dgn = _ret_bwd(proj, cos, sin, gn, outp, states, dgated)
    (dwri,) = _mm("ret_dwri", (1, nsh, t // tk),
                  hn, pl.BlockSpec((tk, D_MODEL), lambda i, j, k: (k, 0)),
                  dproj, pl.BlockSpec((tk, wsh), lambda i, j, k: (k, j)), (0, 0),
                  [(_sds((nsh, D_MODEL, wsh), BF16), pl.BlockSpec((None, D_MODEL, wsh), lambda i, j, k: (j, 0, 0)))])
    deps = emit(dwro, dwri)
    (dhn,) = _mm("ret_dhn", (t // tm, 1, nsh),
                 dproj, pl.BlockSpec((tm, wsh), lambda i, j, k: (i, k)),
                 wri, pl.BlockSpec((None, D_MODEL, wsh), lambda i, j, k: (k, 0, 0)), (1, 1),
                 [(_sds((t, D_MODEL), F32), pl.BlockSpec((tm, D_MODEL), lambda i, j, k: (i, 0)))], deps=deps)
    dx, dg = _rms_bwd("mix_norm_bwd0", dhn, x, norm_g, dh)
    return dx, dg, dgn.reshape(RET_HEADS, RET_DV)


def _mla_layer_fwd(h, hn, wmi, qa, kva, wuq, wukv, gq, gk, wmo, tabs, next_gain):
    t = h.shape[0]
    tm = _row_tile(t)
    row = pl.BlockSpec((tm, D_MODEL), lambda i, j, k: (i, 0))
    (proj2,) = _mm("mla_in", (t // tm, 1, 1), hn, row,
                   wmi, pl.BlockSpec((D_MODEL, MLA_IN_PAD), lambda i, j, k: (0, 0)), (1, 0),
                   [(_sds((t, MLA_IN_PAD), F32), pl.BlockSpec((tm, MLA_IN_PAD), lambda i, j, k: (i, 0)))])
    cq, ckv = _mla_mid(proj2, qa, kva)
    head = pl.BlockSpec((None, tm, MLA_HD_PAD), lambda i, j, k: (j, i, 0))
    (q,) = _mm("mla_uq", (t // tm, MLA_HEADS, 1),
               cq, pl.BlockSpec((tm, MLA_Q_RANK), lambda i, j, k: (i, 0)),
               wuq, pl.BlockSpec((None, MLA_Q_RANK, MLA_HD_PAD), lambda i, j, k: (j, 0, 0)), (1, 0),
               [(_sds((MLA_HEADS, t, MLA_HD_PAD), F32), head)])
    (kv,) = _mm("mla_ukv", (t // tm, MLA_HEADS, 1),
                ckv, pl.BlockSpec((tm, MLA_KV_RANK), lambda i, j, k: (i, 0)),
                wukv, pl.BlockSpec((None, MLA_KV_RANK, MLA_HD_PAD), lambda i, j, k: (j, 0, 0)), (1, 0),
                [(_sds((MLA_HEADS, t, MLA_HD_PAD), F32), head)])
    qh, kh, vh = _mla_prep(q, kv, proj2, gq, gk, tabs)
    o, lse = _attn_fwd(qh, kh, vh)
    more, epi = _residual_epi(next_gain)
    h_out, hn_next = _mm("mla_out", (t // tm, 1, 1), o, row,
                         wmo, pl.BlockSpec((D_MODEL, D_MODEL), lambda i, j, k: (0, 0)), (1, 0),
                         _residual_outs(t, row, next_gain), extras=[(h, row)] + more, epi=epi)
    return h_out, hn_next, (h, hn, proj2, cq, ckv, q, kv, qh, kh, vh, o, lse)


def _mla_layer_bwd(dh, saved, norm_g, wmi, qa, kva, wuq, wukv, gq, gk, wmo, tabs, deps=()):
    h, hn, proj2, cq, ckv, q, kv, qh, kh, vh, o, lse = saved
    t = h.shape[0]
    tm = _row_tile(t)
    tk = _row_tile(t, 512)
    row = pl.BlockSpec((tm, D_MODEL), lambda i, j, k: (i, 0))
    rowk = pl.BlockSpec((tk, D_MODEL), lambda i, j, k: (k, 0))
    sq = pl.BlockSpec((D_MODEL, D_MODEL), lambda i, j, k: (0, 0))
    do, dob = _mm("mla_do", (t // tm, 1, 1), dh, row, wmo, sq, (1, 1),
                  [(_sds((t, D_MODEL), F32), row), (_sds((t, D_MODEL), BF16), row)], epi=lambda acc: (acc, acc),
                  deps=deps)
    (dwmo,) = _mm("mla_dwo", (1, 1, t // tk), o, rowk, dh, rowk, (0, 0), [(_sds((D_MODEL, D_MODEL), BF16), sq)])
    delta = _attn_delta(do, o, lse.shape[-1])
    dqt, dkh, dvh = _attn_bwd(qh, kh, vh, dob, lse, delta)
    dq, dkv, dkr, dgq, dgk = _mla_prep_bwd(q, kv, proj2, gq, gk, tabs, dqt, dkh, dvh)

    headk = pl.BlockSpec((None, tk, MLA_HD_PAD), lambda i, j, k: (j, k, 0))
    (dwuq,) = _mm("mla_dwuq", (1, MLA_HEADS, t // tk),
                  cq, pl.BlockSpec((tk, MLA_Q_RANK), lambda i, j, k: (k, 0)), dq, headk, (0, 0),
                  [(_sds((MLA_HEADS, MLA_Q_RANK, MLA_HD_PAD), BF16),
                    pl.BlockSpec((None, MLA_Q_RANK, MLA_HD_PAD), lambda i, j, k: (j, 0, 0)))])
    (dwukv,) = _mm("mla_dwukv", (1, MLA_HEADS, t // tk),
                   ckv, pl.BlockSpec((tk, MLA_KV_RANK), lambda i, j, k: (k, 0)), dkv, headk, (0, 0),
                   [(_sds((MLA_HEADS, MLA_KV_RANK, MLA_HD_PAD), BF16),
                     pl.BlockSpec((None, MLA_KV_RANK, MLA_HD_PAD), lambda i, j, k: (j, 0, 0)))])
    headi = pl.BlockSpec((None, tm, MLA_HD_PAD), lambda i, j, k: (k, i, 0))
    (dcq,) = _mm("mla_dcq", (t // tm, 1, MLA_HEADS), dq, headi,
                 wuq, pl.BlockSpec((None, MLA_Q_RANK, MLA_HD_PAD), lambda i, j, k: (k, 0, 0)), (1, 1),
                 [(_sds((t, MLA_Q_RANK), F32), pl.BlockSpec((tm, MLA_Q_RANK), lambda i, j, k: (i, 0)))])
    (dckv,) = _mm("mla_dckv", (t // tm, 1, MLA_HEADS), dkv, headi,
                  wukv, pl.BlockSpec((None, MLA_KV_RANK, MLA_HD_PAD), lambda i, j, k: (k, 0, 0)), (1, 1),
                  [(_sds((t, MLA_KV_RANK), F32), pl.BlockSpec((tm, MLA_KV_RANK), lambda i, j, k: (i, 0)))])
    dproj2, dqa, dkva = _mla_mid_bwd(proj2, qa, kva, dcq, dckv, dkr)
    win = pl.BlockSpec((D_MODEL, MLA_IN_PAD), lambda i, j, k: (0, 0))
    (dwmi,) = _mm("mla_dwin", (1, 1, t // tk), hn, rowk,
                  dproj2, pl.BlockSpec((tk, MLA_IN_PAD), lambda i, j, k: (k, 0)), (0, 0),
                  [(_sds((D_MODEL, MLA_IN_PAD), BF16), win)])
    (dhn,) = _mm("mla_dhn", (t // tm, 1, 1),
                 dproj2, pl.BlockSpec((tm, MLA_IN_PAD), lambda i, j, k: (i, 0)), wmi, win, (1, 1),
                 [(_sds((t, D_MODEL), F32), row)])
    dh_in, dg = _rms_bwd("mix_norm_bwd1", dhn, h, norm_g, dh)
    return dh_in, dict(mix=dg, wmi=dwmi, qa=dqa, kva=dkva, wuq=dwuq, wukv=dwukv, gq=dgq, gk=dgk, wmo=dwmo)


def _local_step(x, p, target, w, fetch, emit=lambda group: ()):
    t = x.shape[0]
    inv = 1.0 / (ROPE_THETA ** (jnp.arange(0, RET_DK, 2, dtype=F32) / RET_DK))
    ang = jnp.arange(t, dtype=F32)[:, None] * inv[None, :]
    cos_r, sin_r = jnp.cos(ang), jnp.sin(ang)
    tabs = _mla_tables(t)
    row = lambda a, i: a[i:i + 1]

    h1, hn1, s_ret = _ret_layer_fwd(x, row(w['mix_norm'], 0), w['ret_w_in'],
                                    lambda after: fetch('ret_out', after)['ret_w_out'], w['ret_gn'], cos_r, sin_r,
                                    row(w['mlp_norm'], 0), deps=w['deps'])
    w0 = fetch('layer0', (h1,))
    h2, hn2, s_mlp0 = _mlp_fwd(0, h1, hn1, w0['mlp_w1'], w0['mlp_w2'], row(w['ple_norm'], 0))
    h3, hn3, s_ple0 = _ple_fwd(0, h2, hn2, p, w0['ple_gate_w'], w0['ple_proj_w'], row(w['mix_norm'], 1))
    wm = fetch('mla', (h3,))
    mla_w = (wm['mla_w_in'], w['mla_q_a_norm'], w['mla_kv_a_norm'], wm['mla_w_uq'], wm['mla_w_ukv'],
             w['mla_q_norm'], w['mla_k_norm'], wm['mla_w_out'], tabs)
    h4, hn4, s_mla = _mla_layer_fwd(h3, hn3, *mla_w, row(w['mlp_norm'], 1))
    w1 = fetch('layer1', (h4,))
    h5, hn5, s_mlp1 = _mlp_fwd(1, h4, hn4, w1['mlp_w1'], w1['mlp_w2'], row(w['ple_norm'], 1))
    y, _, s_ple1 = _ple_fwd(1, h5, hn5, p, w1['ple_gate_w'], w1['ple_proj_w'], None)

    dy, sq_err = _loss_head(y, target)

    n = N_DEV
    colsh = lambda a: a.reshape(a.shape[0], n, a.shape[1] // n).transpose(1, 0, 2)
    rowsh = lambda a: a.reshape(n, a.shape[0] // n, a.shape[1])
    big = {}

    def emit_group(group):
        big.update(group)
        return emit(group)

    dh5, dg_ple1, dwg1, dwp1 = _ple_bwd(1, dy, s_ple1, p, row(w['ple_norm'], 1), w1['ple_gate_w'])
    dh4, dg_mlp1, dw1_1, dw2_1 = _mlp_bwd(1, dh5, s_mlp1, row(w['mlp_norm'], 1), w1['mlp_w1'], w1['mlp_w2'])
    deps = emit_group({('ple_gate_w', 1): rowsh(dwg1), ('ple_proj_w', 1): colsh(dwp1),
                       ('mlp_w2', 1): dw2_1, ('mlp_w1', 1): dw1_1})
    dh3, gm = _mla_layer_bwd(dh4, s_mla, row(w['mix_norm'], 1), *mla_w, deps=deps)
    deps = emit_group({('mla_w_out', 0): rowsh(gm['wmo']), ('mla_w_uq', 0): gm['wuq'][:, :, :MLA_QKD],
                       ('mla_w_ukv', 0): gm['wukv'], ('mla_w_in', 0): rowsh(gm['wmi'][:, :MLA_IN])})
    dh2, dg_ple0, dwg0, dwp0 = _ple_bwd(0, dh3, s_ple0, p, row(w['ple_norm'], 0), w0['ple_gate_w'], deps=deps)
    dh1, dg_mlp0, dw1_0, dw2_0 = _mlp_bwd(0, dh2, s_mlp0, row(w['mlp_norm'], 0), w0['mlp_w1'], w0['mlp_w2'])
    deps = emit_group({('ple_gate_w', 0): rowsh(dwg0), ('ple_proj_w', 0): colsh(dwp0),
                       ('mlp_w2', 0): dw2_0, ('mlp_w1', 0): dw1_0})
    dx, dg_mix0, dgn = _ret_layer_bwd(
        dh1, s_ret, row(w['mix_norm'], 0), w['ret_w_in'], w['ret_gn'], cos_r, sin_r,
        lambda dwro, dwri: emit_group({('ret_w_out', 0): rowsh(dwro), ('ret_w_in', 0): dwri}), deps=deps)

    small = dict(
        mix_norm=[dg_mix0, gm['mix']], mlp_norm=[dg_mlp0, dg_mlp1], ple_norm=[dg_ple0, dg_ple1],
        ret_gn=dgn, mla_q_a_norm=gm['qa'], mla_kv_a_norm=gm['kva'], mla_q_norm=gm['gq'], mla_k_norm=gm['gk'],
    )
    return sq_err, dx, big, small


def _my_place():
    x, y, c = lax.axis_index("x"), lax.axis_index("y"), lax.axis_index("c")
    return x, y, c


def _flat(px, py, pc):
    return 4 * px + 2 * py + pc


def _peer(x, y, c, r):
    return (1 - x if r & 4 else x, 1 - y if r & 2 else y, 1 - c if r & 1 else c)


def _all_gather(arrays):
    n = len(arrays)

    def body(*refs):
        ins, outs = refs[:n], refs[n:2 * n]
        send_sems, recv_sems, local_sems = refs[2 * n:]
        x, y, c = _my_place()
        me, sibling = (x, y, c), (x, y, 1 - c)
        chips = [(1 - x, y), (x, 1 - y), (1 - x, 1 - y)]

        def copy(a, k, block, to, src=None):
            slot = outs[a].at[_flat(*block)]
            return pltpu.make_async_remote_copy(
                src_ref=slot if src is None else src, dst_ref=slot,
                send_sem=send_sems.at[a, k], recv_sem=recv_sems.at[a, k], device_id=to, device_id_type=MESH)

        mine = [pltpu.make_async_copy(ins[a], outs[a].at[_flat(*me)], local_sems.at[a]) for a in range(n)]
        for cp in mine:
            cp.start()
        first = []
        for a in range(n):
            first.append(copy(a, 0, me, sibling, src=ins[a]))
            first += [copy(a, 1 + j, me, (*chip, c), src=ins[a]) for j, chip in enumerate(chips)]
        for cp in first:
            cp.start()
        passed = []
        for a in range(n):
            for j, chip in enumerate(chips):
                copy(a, 1 + j, (*chip, c), me).wait_recv()
                passed.append(copy(a, 4 + j, (*chip, c), sibling))
                passed[-1].start()
        for a in range(n):
            copy(a, 0, sibling, me).wait_recv()
            for j, chip in enumerate(chips):
                copy(a, 4 + j, (*chip, 1 - c), me).wait_recv()
        for cp in first + passed:
            cp.wait_send()
        for cp in mine:
            cp.wait()

    return pl.pallas_call(
        body, name="all_gather_weights",
        in_specs=[ANY] * n, out_specs=[ANY] * n,
        out_shape=[_sds((N_DEV,) + a.shape, a.dtype) for a in arrays],
        scratch_shapes=[pltpu.SemaphoreType.DMA((n, 7)), pltpu.SemaphoreType.DMA((n, 7)),
                        pltpu.SemaphoreType.DMA((n,))],
    )(*arrays)


HBM = pl.BlockSpec(memory_space=pltpu.HBM)
SEMS = pl.BlockSpec(memory_space=pltpu.SEMAPHORE)
SIDE_EFFECT = pltpu.SideEffectType.DATAFLOW_SIDE_EFFECTING


def _rs_copies(x, y, c, srcs, lands, send_sems, recv_sems):
    copies = []
    for a in range(len(srcs)):
        for r in range(1, N_DEV):
            peer = _peer(x, y, c, r)
            k = a * (N_DEV - 1) + r - 1
            copies.append(pltpu.make_async_remote_copy(
                src_ref=srcs[a].at[_flat(*peer)], dst_ref=lands[a].at[r - 1],
                send_sem=send_sems.at[k], recv_sem=recv_sems.at[k], device_id=peer, device_id_type=MESH))
    return copies


def _rs_start(name, arrays):
    n = len(arrays)
    hbm = lambda a: pltpu.with_memory_space_constraint(a, pltpu.HBM)
    lands = [hbm(lax.empty((N_DEV - 1,) + a.shape[1:], a.dtype)) for a in arrays]

    def body(*refs):
        srcs, lnd = refs[:n], refs[n:2 * n]
        send_sems, recv_sems = refs[2 * n], refs[2 * n + 1]
        token = refs[-1]
        for cp in _rs_copies(*_my_place(), srcs, lnd, send_sems, recv_sems):
            cp.start()
        token[...] = jnp.zeros_like(token)

    outs = pl.pallas_call(
        body, name=name,
        in_specs=[HBM] * (2 * n),
        out_specs=[SEMS, SEMS] + [HBM] * (2 * n) + [pl.BlockSpec(memory_space=pltpu.VMEM)],
        out_shape=[pltpu.SemaphoreType.DMA((n * (N_DEV - 1),)), pltpu.SemaphoreType.DMA((n * (N_DEV - 1),))]
        + [pltpu.HBM(a.shape, a.dtype) for a in arrays] + [pltpu.HBM(l.shape, l.dtype) for l in lands]
        + [_sds((8, 128), F32)],
        input_output_aliases={i: 2 + i for i in range(2 * n)},
        compiler_params=pltpu.CompilerParams(has_side_effects=SIDE_EFFECT),
    )(*[hbm(a) for a in arrays], *lands)
    return outs[0], outs[1], outs[2:2 + n], outs[2 + n:2 + 2 * n], outs[-1]


def _rs_wait(name, send_sems, recv_sems, srcs, lands, after):
    n = len(srcs)

    def body(*refs):
        src_refs, lnd = refs[:n], refs[n:2 * n]
        send, recv = refs[2 * n], refs[2 * n + 1]
        for cp in _rs_copies(*_my_place(), src_refs, lnd, send, recv):
            cp.wait_send()
            cp.wait_recv()

    outs = pl.pallas_call(
        body, name=name,
        in_specs=[HBM] * (2 * n) + [SEMS, SEMS] + [ANY] * len(after),
        out_specs=[HBM] * (2 * n),
        out_shape=[pltpu.HBM(a.shape, a.dtype) for a in list(srcs) + list(lands)],
        input_output_aliases={i: i for i in range(2 * n)},
        compiler_params=pltpu.CompilerParams(has_side_effects=SIDE_EFFECT),
    )(*srcs, *lands, send_sems, recv_sems, *after)
    return outs[:n], outs[n:]


SMALL_PACK_ROWS = 16


def _all_reduce_small(rows, deps=()):
    n = len(rows)

    def body(*refs):
        ins = refs[:n]
        out_ref, mine, buf, send_sems, recv_sems = refs[n + len(deps):]
        x, y, c = _my_place()
        mine[...] = jnp.zeros_like(mine)
        for (r0, a), ref in zip(rows, ins):
            mine[r0:r0 + a.shape[0], 0:a.shape[1]] = ref[...]
        buf[_flat(x, y, c)] = mine[...]
        copies = []
        for r in range(1, N_DEV):
            peer = _peer(x, y, c, r)
            send = pltpu.make_async_remote_copy(
                src_ref=mine, dst_ref=buf.at[_flat(x, y, c)],
                send_sem=send_sems.at[r - 1], recv_sem=recv_sems.at[r - 1], device_id=peer, device_id_type=MESH)
            send.start()
            recv = pltpu.make_async_remote_copy(
                src_ref=mine, dst_ref=buf.at[_flat(*peer)],
                send_sem=send_sems.at[r - 1], recv_sem=recv_sems.at[r - 1], device_id=peer, device_id_type=MESH)
            copies.append((send, recv))
        for send, recv in copies:
            send.wait_send()
            recv.wait_recv()
        acc = buf[0]
        for s in range(1, N_DEV):
            acc = acc + buf[s]
        out_ref[...] = acc

    vm = pl.BlockSpec(memory_space=pltpu.VMEM)
    shape = (SMALL_PACK_ROWS, D_MODEL)
    return pl.pallas_call(
        body, name="all_reduce_small", in_specs=[vm] * n + [ANY] * len(deps), out_specs=vm,
        out_shape=_sds(shape, F32),
        scratch_shapes=[pltpu.VMEM(shape, F32), pltpu.VMEM((N_DEV,) + shape, F32),
                        pltpu.SemaphoreType.DMA((7,)), pltpu.SemaphoreType.DMA((7,))],
    )(*[a for _, a in rows], *deps)


def _adamw_math(w, g, m, v):
    m = ADAM_B1 * m + (1.0 - ADAM_B1) * g
    v = ADAM_B2 * v + (1.0 - ADAM_B2) * (g * g)
    m_hat = m / (1.0 - ADAM_B1 ** ADAM_STEP)
    v_hat = v / (1.0 - ADAM_B2 ** ADAM_STEP)
    delta = -ADAM_LR * (m_hat / (jnp.sqrt(v_hat) + ADAM_EPS) + ADAM_WD * w)
    return delta, m, v


def _adamw_big(name, w, m, v, srcs, lands, me):
    nl, rows, cols = w.shape
    tr = next(cand for cand in (256, 128, 64, 32, 16, 8) if rows % cand == 0)

    def body(me_ref, w_ref, m_ref, v_ref, *rest):
        src_refs, land_refs = rest[:nl], rest[nl:2 * nl]
        g_ref, d_ref, mo_ref, vo_ref = rest[2 * nl:]
        for layer in range(nl):
            @pl.when(pl.program_id(0) == layer)
            def _():
                g = src_refs[layer][...].astype(F32)
                for s in range(N_DEV - 1):
                    g = g + land_refs[layer][s].astype(F32)
                delta, mn, vn = _adamw_math(w_ref[...], g, m_ref[...], v_ref[...])
                g_ref[...] = g
                d_ref[...] = delta
                mo_ref[...] = mn
                vo_ref[...] = vn

    blk = pl.BlockSpec((None, tr, cols), lambda l, i, me_ref: (l, i, 0))
    own = pl.BlockSpec((None, tr, cols), lambda l, i, me_ref: (me_ref[0], i, 0))
    peers = pl.BlockSpec((N_DEV - 1, tr, cols), lambda l, i, me_ref: (0, i, 0))
    return pl.pallas_call(
        body, name=name,
        grid_spec=pltpu.PrefetchScalarGridSpec(
            num_scalar_prefetch=1, grid=(nl, rows // tr),
            in_specs=[blk, blk, blk] + [own] * nl + [peers] * nl, out_specs=[blk] * 4),
        out_shape=[_sds((nl, rows, cols), F32)] * 4,
        compiler_params=_cparams(("arbitrary", "arbitrary")),
    )(me, w, m, v, *srcs, *lands)


def _adamw_small(ws, gs, ms, vs):
    n = len(ws)

    def body(*refs):
        w_refs, g_refs, m_refs, v_refs = (refs[i * n:(i + 1) * n] for i in range(4))
        d_out, m_out, v_out = (refs[(4 + i) * n:(5 + i) * n] for i in range(3))
        for i in range(n):
            delta, mn, vn = _adamw_math(w_refs[i][...], g_refs[i][...], m_refs[i][...], v_refs[i][...])
            d_out[i][...] = delta
            m_out[i][...] = mn
            v_out[i][...] = vn

    vm = pl.BlockSpec(memory_space=pltpu.VMEM)
    outs = pl.pallas_call(
        body, name="adamw_small", in_specs=[vm] * (4 * n), out_specs=[vm] * (3 * n),
        out_shape=[_sds(a.shape, F32) for a in ws] * 3,
    )(*ws, *gs, *ms, *vs)
    return outs[:n], outs[n:2 * n], outs[2 * n:]


SMALL_ROWS = 16


def _pad_to(a, rows, cols):
    return jnp.pad(a, ((0, rows - a.shape[0]), (0, cols - a.shape[1])))


def _place_own(blocks):
    me = _flat(*_my_place())
    return [lax.dynamic_update_slice(lax.empty((N_DEV,) + b.shape, b.dtype), b[None], (me,) + (0,) * b.ndim)
            for b in blocks]


def _ag_copies(x, y, c, blocks, bufs, send_sems, recv_sems):
    sends, recvs = [], []
    for a in range(len(blocks)):
        for r in range(1, N_DEV):
            peer = _peer(x, y, c, r)
            k = a * (N_DEV - 1) + r - 1
            make = lambda place: pltpu.make_async_remote_copy(
                src_ref=blocks[a], dst_ref=bufs[a].at[_flat(*place)],
                send_sem=send_sems.at[k], recv_sem=recv_sems.at[k], device_id=peer, device_id_type=MESH)
            sends.append(make((x, y, c)))
            recvs.append(make(peer))
    return sends, recvs


def _ag_start(groups, after):
    flat = [pair for g in groups for pair in g]
    n, ng = len(flat), len(groups)
    hbm = lambda a: pltpu.with_memory_space_constraint(a, pltpu.HBM)

    def body(*refs):
        blocks, bufs = refs[:n], refs[n:2 * n]
        sems = refs[2 * n + len(after):2 * n + len(after) + 2 * ng]
        x, y, c = _my_place()
        at = 0
        for gi, g in enumerate(groups):
            sends, _ = _ag_copies(x, y, c, blocks[at:at + len(g)], bufs[at:at + len(g)], sems[2 * gi], sems[2 * gi + 1])
            for cp in sends:
                cp.start()
            at += len(g)
        refs[-1][...] = jnp.zeros_like(refs[-1])

    sem_shapes = [pltpu.SemaphoreType.DMA((len(g) * (N_DEV - 1),)) for g in groups for _ in range(2)]
    outs = pl.pallas_call(
        body, name="gather_start",
        in_specs=[HBM] * (2 * n) + [ANY] * len(after),
        out_specs=[SEMS] * (2 * ng) + [HBM] * (2 * n) + [pl.BlockSpec(memory_space=pltpu.VMEM)],
        out_shape=sem_shapes + [pltpu.HBM(b.shape, b.dtype) for b, _ in flat]
        + [pltpu.HBM(u.shape, u.dtype) for _, u in flat] + [_sds((8, 128), F32)],
        input_output_aliases={i: 2 * ng + i for i in range(2 * n)},
        compiler_params=pltpu.CompilerParams(has_side_effects=SIDE_EFFECT),
    )(*[hbm(b) for b, _ in flat], *[hbm(u) for _, u in flat], *after)
    blocks_thru, bufs_thru = outs[2 * ng:2 * ng + n], outs[2 * ng + n:2 * ng + 2 * n]
    started, at = [], 0
    for gi, g in enumerate(groups):
        started.append((outs[2 * gi], outs[2 * gi + 1], blocks_thru[at:at + len(g)], bufs_thru[at:at + len(g)]))
        at += len(g)
    return started, outs[-1]


def _ag_wait(name, send_sems, recv_sems, blocks, bufs, after):
    n = len(blocks)

    def body(*refs):
        sends, recvs = _ag_copies(*_my_place(), refs[:n], refs[n:2 * n], refs[2 * n], refs[2 * n + 1])
        for s, r in zip(sends, recvs):
            s.wait_send()
            r.wait_recv()

    outs = pl.pallas_call(
        body, name=name,
        in_specs=[HBM] * (2 * n) + [SEMS, SEMS] + [ANY] * len(after),
        out_specs=[HBM] * (2 * n),
        out_shape=[pltpu.HBM(a.shape, a.dtype) for a in list(blocks) + list(bufs)],
        input_output_aliases={i: i for i in range(2 * n)},
        compiler_params=pltpu.CompilerParams(has_side_effects=SIDE_EFFECT),
    )(*blocks, *bufs, send_sems, recv_sems, *after)
    return outs[n:]


def _prepare_weights(p):
    n = N_DEV
    bf = lambda a: a.astype(BF16)
    gn_pack = jnp.concatenate([
        _pad_to(p['ret_gn'][0], RET_HEADS, 128), _pad_to(p['mla_q_a_norm'], 1, 128),
        _pad_to(p['mla_kv_a_norm'], 1, 128), jnp.zeros((2, 128), F32)], axis=0)
    layer = lambda l: [bf(p['mlp_w1'][l]), bf(p['mlp_w2'][l]), bf(p['ple_gate_w'][l]), bf(p['ple_proj_w'][l])]
    later = [[bf(p['ret_w_out'][0])], layer(0),
             [bf(p['mla_w_in'][0]), bf(p['mla_w_uq'][0]), bf(p['mla_w_ukv'][0]), bf(p['mla_w_out'][0])], layer(1)]
    bufs = _place_own([b for g in later for b in g])
    pack, wri = _all_gather([gn_pack, bf(p['ret_w_in'][0])])
    groups, at = [], 0
    for g in later:
        groups.append(list(zip(g, bufs[at:at + len(g)])))
        at += len(g)
    started, token = _ag_start(groups, (wri,))

    w = {k: p[k] for k in ('mix_norm', 'mlp_norm', 'ple_norm')}
    w['ret_gn'] = pack[:, :RET_HEADS, :RET_DV // n].transpose(1, 0, 2).reshape(RET_HEADS, RET_DV)
    w['mla_q_a_norm'] = pack[:, RET_HEADS, :MLA_Q_RANK // n].reshape(1, MLA_Q_RANK)
    w['mla_kv_a_norm'] = pack[:, RET_HEADS + 1, :MLA_KV_RANK // n].reshape(1, MLA_KV_RANK)
    w['ret_w_in'] = wri
    w['mla_q_norm'] = _pad_to(p['mla_q_norm'], 1, MLA_HD_PAD)
    w['mla_k_norm'] = _pad_to(p['mla_k_norm'], 1, MLA_HD_PAD)
    w['deps'] = (token,)

    def fetch(name, after):
        gi = ('ret_out', 'layer0', 'mla', 'layer1').index(name)
        got = _ag_wait("gather_wait_" + name, *started[gi], after)
        if name == 'ret_out':
            return dict(ret_w_out=got[0].reshape(RET_V_W, D_MODEL))
        if name == 'mla':
            wmi, wuq, wukv, wmo = got
            return dict(mla_w_in=jnp.pad(wmi.reshape(D_MODEL, MLA_IN), ((0, 0), (0, MLA_IN_PAD - MLA_IN))),
                        mla_w_uq=jnp.pad(wuq, ((0, 0), (0, 0), (0, MLA_HD_PAD - MLA_QKD))),
                        mla_w_ukv=wukv, mla_w_out=wmo.reshape(D_MODEL, D_MODEL))
        w1, w2, wg, wp = got
        return dict(mlp_w1=w1, mlp_w2=w2, ple_gate_w=wg.reshape(D_MODEL, D_MODEL),
                    ple_proj_w=wp.transpose(1, 0, 2).reshape(PLE_DIM, D_MODEL))

    return w, fetch


def _small_grads(small, after):
    rows = [(0, small['mix_norm'][0]), (1, small['mix_norm'][1]), (2, small['mlp_norm'][0]),
            (3, small['mlp_norm'][1]), (4, small['ple_norm'][0]), (5, small['ple_norm'][1]),
            (6, small['ret_gn']), (10, small['mla_q_a_norm']), (11, small['mla_kv_a_norm']),
            (12, small['mla_q_norm']), (13, small['mla_k_norm'])]
    gs = _all_reduce_small(rows, after)
    me = _flat(*_my_place())
    n = N_DEV
    return dict(
        mix_norm=gs[0:2], mlp_norm=gs[2:4], ple_norm=gs[4:6],
        ret_gn=lax.dynamic_slice(gs, (6, me * (RET_DV // n)), (RET_HEADS, RET_DV // n)),
        mla_q_a_norm=lax.dynamic_slice(gs, (10, me * (MLA_Q_RANK // n)), (1, MLA_Q_RANK // n)),
        mla_kv_a_norm=lax.dynamic_slice(gs, (11, me * (MLA_KV_RANK // n)), (1, MLA_KV_RANK // n)),
        mla_q_norm=gs[12:13, :MLA_QKD], mla_k_norm=gs[13:14, :MLA_QKD])


def kernel(x, p, mix_norm, ret_w_in, ret_gn, ret_w_out, mla_w_in, mla_q_a_norm, mla_kv_a_norm, mla_w_uq, mla_w_ukv, mla_q_norm, mla_k_norm, mla_w_out, mlp_norm, mlp_w1, mlp_w2, ple_norm, ple_gate_w, ple_proj_w, loss_target, m_mix_norm, m_ret_w_in, m_ret_gn, m_ret_w_out, m_mla_w_in, m_mla_q_a_norm, m_mla_kv_a_norm, m_mla_w_uq, m_mla_w_ukv, m_mla_q_norm, m_mla_k_norm, m_mla_w_out, m_mlp_norm, m_mlp_w1, m_mlp_w2, m_ple_norm, m_ple_gate_w, m_ple_proj_w, v_mix_norm, v_ret_w_in, v_ret_gn, v_ret_w_out, v_mla_w_in, v_mla_q_a_norm, v_mla_kv_a_norm, v_mla_w_uq, v_mla_w_ukv, v_mla_q_norm, v_mla_k_norm, v_mla_w_out, v_mlp_norm, v_mlp_w1, v_mlp_w2, v_ple_norm, v_ple_gate_w, v_ple_proj_w):
    given = dict(locals())
    params = {n: given[n] for n in WEIGHTS}
    w, fetch = _prepare_weights(params)

    started = []

    def emit(group):
        keys = list(group)
        send, recv, srcs, lands, token = _rs_start(f"rs_start{len(started)}", [group[k] for k in keys])
        started.append((keys, send, recv, srcs, lands))
        return (token,)

    sq_err, grad_x, _, small = _local_step(x[0], p, loss_target[0], w, fetch, emit)
    loss = lax.psum(0.5 / D_MODEL * sq_err[0, 0], ("x", "y", "c"))

    grads, deltas, new_m, new_v = {}, {}, {}, {}

    def small_updates(after):
        sg = _small_grads(small, after)
        two_d = lambda a: a.reshape(-1, a.shape[-1])
        d_s, m_s, v_s = _adamw_small(
            [two_d(params[n]) for n in SMALL], [sg[n] for n in SMALL],
            [two_d(given["m_" + n]) for n in SMALL], [two_d(given["v_" + n]) for n in SMALL])
        for i, n in enumerate(SMALL):
            shape = params[n].shape
            grads[n], deltas[n], new_m[n], new_v[n] = (a.reshape(shape) for a in (sg[n], d_s[i], m_s[i], v_s[i]))
        return (d_s[0],)

    me = _flat(*_my_place()).astype(jnp.int32).reshape(1)
    after = (grad_x,)
    src_of, land_of = {}, {}
    for gi, (keys, send, recv, srcs, lands) in enumerate(started):
        if gi == len(started) - 1:
            after = small_updates(after)
        srcs, lands = _rs_wait(f"rs_wait{gi}", send, recv, srcs, lands, after)
        for k, s, l in zip(keys, srcs, lands):
            src_of[k], land_of[k] = s, l
        done = [n for n in BIG if n not in grads and all((n, l) in src_of for l in range(params[n].shape[0]))]
        for n in done:
            layers = range(params[n].shape[0])
            grads[n], deltas[n], new_m[n], new_v[n] = _adamw_big(
                "adamw_" + n, params[n], given["m_" + n], given["v_" + n],
                [src_of[(n, l)] for l in layers], [land_of[(n, l)] for l in layers], me)
        if done:
            after = (deltas[done[-1]],)

    return (loss, grad_x[None], *[grads[n] for n in WEIGHTS], *[deltas[n] for n in WEIGHTS],
            *[new_m[n] for n in WEIGHTS], *[new_v[n] for n in WEIGHTS])
```

```python
import functools
import math

import jax
import jax.numpy as jnp
from jax import lax
from jax.experimental import pallas as pl
from jax.experimental.pallas import tpu as pltpu

F32 = jnp.float32
BF16 = jnp.bfloat16
MESH = pl.DeviceIdType.MESH
ANY = pl.BlockSpec(memory_space=pl.ANY)

N_DEV = 8
D_MODEL = 1024
CHUNK = 64
EPS = 1e-6
ROPE_THETA = 10000.0
RET_HEADS = 4
RET_DK = 256
RET_DV = 512
RET_QK_W = RET_HEADS * RET_DK
RET_V_W = RET_HEADS * RET_DV
RET_IN = 2 * RET_QK_W + 2 * RET_V_W
MLA_HEADS = 8
MLA_NOPE = 128
MLA_ROPE = 64
MLA_QKD = MLA_NOPE + MLA_ROPE
MLA_VD = 128
MLA_Q_RANK = 384
MLA_KV_RANK = 256
MLA_IN = MLA_Q_RANK + MLA_KV_RANK + MLA_ROPE
MLA_IN_PAD = 768
MLA_HD_PAD = 256
D_FF = 4096
PLE_DIM = 256
ATT_SCALE = MLA_QKD ** -0.5
LOG2E = 1.4426950408889634
ATT_EXP2 = ATT_SCALE * LOG2E

ADAM_LR = 0.001
ADAM_B1 = 0.9
ADAM_B2 = 0.999
ADAM_EPS = 1e-08
ADAM_WD = 0.01
ADAM_STEP = 10

VMEM_LIMIT = 52 * 1024 * 1024
ROW_TILE = 1024
RET_ROWS = 256
ATT_BLOCK = 256
ATT_QROWS = 512
ATT_HEADS = 2

WEIGHTS = ['mix_norm', 'ret_w_in', 'ret_gn', 'ret_w_out', 'mla_w_in', 'mla_q_a_norm', 'mla_kv_a_norm',
           'mla_w_uq', 'mla_w_ukv', 'mla_q_norm', 'mla_k_norm', 'mla_w_out', 'mlp_norm', 'mlp_w1', 'mlp_w2',
           'ple_norm', 'ple_gate_w', 'ple_proj_w']
BIG = ['ret_w_in', 'ret_w_out', 'mla_w_in', 'mla_w_uq', 'mla_w_ukv', 'mla_w_out', 'mlp_w1', 'mlp_w2',
       'ple_gate_w', 'ple_proj_w']
SMALL = [w for w in WEIGHTS if w not in BIG]


def _cparams(sem=None):
    return pltpu.CompilerParams(dimension_semantics=sem, vmem_limit_bytes=VMEM_LIMIT)


def _dot(a, b, ca, cb):
    return lax.dot_general(a, b, (((ca,), (cb,)), ((), ())), preferred_element_type=F32)


def _bf(v):
    return v if v.dtype == BF16 else v.astype(BF16)


def _sigmoid(z):
    return 1.0 / (1.0 + jnp.exp(-z))


def _mm(name, grid, a, a_spec, b, b_spec, contract, outs, extras=(), epi=None, deps=(), split=None):
    nk = grid[2]
    n_ex, n_out, n_dep = len(extras), len(outs), len(deps)
    acc_shape = tuple(d for d in outs[0][1].block_shape if d is not None)
    if split is not None:
        acc_shape = (acc_shape[1], acc_shape[0] * split)

    def body(*refs):
        a_ref, b_ref = refs[:2]
        ex_refs = refs[2:2 + n_ex]
        out_refs = refs[2 + n_ex + n_dep:2 + n_ex + n_dep + n_out]

        def product():
            return _dot(_bf(a_ref[...]), _bf(b_ref[...]), contract[0], contract[1])

        def finish(acc):
            if split is not None:
                for j in range(acc_shape[1] // split):
                    out_refs[0][j] = acc[:, j * split:(j + 1) * split].astype(out_refs[0].dtype)
                return
            acc = acc[...]
            res = epi(acc, *[r[...] for r in ex_refs]) if epi is not None else (acc,)
            for o, r in zip(out_refs, res):
                o[...] = r.astype(o.dtype)

        if nk == 1:
            finish(product())
        else:
            acc_ref = refs[-1]
            k = pl.program_id(2)

            @pl.when(k == 0)
            def _():
                acc_ref[...] = jnp.zeros_like(acc_ref)

            acc_ref[...] += product()

            @pl.when(k == nk - 1)
            def _():
                finish(acc_ref)

    return pl.pallas_call(
        body, name=name, grid=grid,
        in_specs=[a_spec, b_spec] + [s for _, s in extras] + [ANY] * n_dep,
        out_specs=[s for _, s in outs],
        out_shape=[s for s, _ in outs],
        scratch_shapes=[pltpu.VMEM(acc_shape, F32)] if nk > 1 else [],
        compiler_params=_cparams(("parallel", "parallel", "arbitrary")),
    )(a, b, *[x for x, _ in extras], *deps)


def _sds(shape, dtype):
    return jax.ShapeDtypeStruct(shape, dtype)


def _row_tile(t, cap=ROW_TILE):
    return min(cap, t)


def _rms_fwd(name, x, g):
    t, d = x.shape
    tm = _row_tile(t)

    def body(x_ref, g_ref, o_ref):
        xv = x_ref[...]
        r = lax.rsqrt(jnp.mean(xv * xv, axis=-1, keepdims=True) + EPS)
        o_ref[...] = (xv * r * g_ref[...]).astype(o_ref.dtype)

    return pl.pallas_call(
        body, name=name, grid=(t // tm,),
        in_specs=[pl.BlockSpec((tm, d), lambda i: (i, 0)), pl.BlockSpec((1, d), lambda i: (0, 0))],
        out_specs=pl.BlockSpec((tm, d), lambda i: (i, 0)),
        out_shape=_sds((t, d), BF16),
        compiler_params=_cparams(("parallel",)),
    )(x, g)


def _rms_bwd_rows(dy, xv, g, n):
    r = lax.rsqrt(jnp.sum(xv * xv, axis=-1, keepdims=True) / n + EPS)
    xh = xv * r
    dxh = dy * g
    dx = r * (dxh - xh * (jnp.sum(dxh * xh, axis=-1, keepdims=True) / n))
    return dx, dy * xh


def _rms_bwd(name, dy, x, g, res):
    t, d = x.shape
    tm = _row_tile(t, 512)

    def body(dy_ref, x_ref, g_ref, res_ref, dx_ref, dg_ref):
        @pl.when(pl.program_id(0) == 0)
        def _():
            dg_ref[...] = jnp.zeros_like(dg_ref)

        dx, dgr = _rms_bwd_rows(dy_ref[...], x_ref[...], g_ref[...], d)
        dx_ref[...] = res_ref[...] + dx
        dg_ref[...] += jnp.sum(dgr, axis=0, keepdims=True)

    row = pl.BlockSpec((tm, d), lambda i: (i, 0))
    vec = pl.BlockSpec((1, d), lambda i: (0, 0))
    return pl.pallas_call(
        body, name=name, grid=(t // tm,),
        in_specs=[row, row, vec, row], out_specs=[row, vec],
        out_shape=[_sds((t, d), F32), _sds((1, d), F32)],
        compiler_params=_cparams(("arbitrary",)),
    )(dy, x, g, res)


def _loss_head(y, target):
    t, d = y.shape
    tm = _row_tile(t)

    def body(y_ref, t_ref, dy_ref, l_ref):
        @pl.when(pl.program_id(0) == 0)
        def _():
            l_ref[...] = jnp.zeros_like(l_ref)

        e = y_ref[...] - t_ref[...]
        dy_ref[...] = e / d
        l_ref[...] += jnp.sum(jnp.sum(e * e, axis=-1, keepdims=True), axis=0, keepdims=True)

    row = pl.BlockSpec((tm, d), lambda i: (i, 0))
    return pl.pallas_call(
        body, name="loss_head", grid=(t // tm,),
        in_specs=[row, row], out_specs=[row, pl.BlockSpec((8, 128), lambda i: (0, 0))],
        out_shape=[_sds((t, d), F32), _sds((8, 128), F32)],
        compiler_params=_cparams(("arbitrary",)),
    )(y, target)


def _ple_gate_bwd(name, dh, gate, e):
    t, d = dh.shape
    tm = _row_tile(t)

    def body(dh_ref, g_ref, e_ref, de_ref, dz_ref):
        dh_v, gt = dh_ref[...], g_ref[...]
        de_ref[...] = (dh_v * gt).astype(BF16)
        dz_ref[...] = (dh_v * e_ref[...] * (gt * (1.0 - gt))).astype(BF16)

    row = pl.BlockSpec((tm, d), lambda i: (i, 0))
    return pl.pallas_call(
        body, name=name, grid=(t // tm,), in_specs=[row, row, row], out_specs=[row, row],
        out_shape=[_sds((t, d), BF16), _sds((t, d), BF16)],
        compiler_params=_cparams(("parallel",)),
    )(dh, gate, e)


def _rope_half(v, cos, sin):
    half = v.shape[-1] // 2
    v1, v2 = v[:, :half], v[:, half:]
    return jnp.concatenate([v1 * cos - v2 * sin, v2 * cos + v1 * sin], axis=-1)


def _ret_consts():
    lg = jnp.log(1.0 - 2.0 ** (-5.0 - jnp.arange(RET_HEADS, dtype=F32)))
    idx = jnp.arange(CHUNK, dtype=F32)
    intra = jnp.exp(lg[:, None, None] * jnp.abs(idx[:, None] - idx[None, :]))
    qdec = jnp.exp(lg[:, None] * (idx + 1.0))
    kdec = jnp.exp(lg[:, None] * (CHUNK - 1.0 - idx))
    cdec = jnp.exp(lg * CHUNK)
    qdec = jnp.broadcast_to(qdec[:, :, None], (RET_HEADS, CHUNK, RET_DK))
    kdec = jnp.broadcast_to(kdec[:, :, None], (RET_HEADS, CHUNK, RET_DK))
    cdec = jnp.broadcast_to(cdec[:, None, None], (RET_HEADS, 1, RET_DV))
    return intra, qdec, kdec, cdec


def _ret_specs(rb, rev_nb=None):
    blk = (lambda i: i) if rev_nb is None else (lambda i: rev_nb - 1 - i)
    full = lambda shape: pl.BlockSpec(shape, lambda i: (0,) * len(shape))
    return dict(
        proj=pl.BlockSpec((rb, RET_IN), lambda i: (blk(i), 0)),
        tab=pl.BlockSpec((rb, RET_DK // 2), lambda i: (blk(i), 0)),
        vw=pl.BlockSpec((rb, RET_V_W), lambda i: (blk(i), 0)),
        st=pl.BlockSpec((rb // CHUNK, RET_HEADS, RET_DK, RET_DV), lambda i: (blk(i), 0, 0, 0)),
        gn=full((RET_HEADS, 1, RET_DV)),
        intra=full((RET_HEADS, CHUNK, CHUNK)),
        dec=full((RET_HEADS, CHUNK, RET_DK)),
        cdec=full((RET_HEADS, 1, RET_DV)),
    )


def _ret_fwd(proj, cos, sin, gn):
    t = proj.shape[0]
    rb = min(RET_ROWS, t)
    cpb = rb // CHUNK
    intra, qdec, kdec, cdec = _ret_consts()
    sp = _ret_specs(rb)

    def body(proj_ref, cos_ref, sin_ref, gn_ref, intra_ref, qd_ref, kd_ref, cd_ref,
             gated_ref, outp_ref, st_ref, s_ref):
        @pl.when(pl.program_id(0) == 0)
        def _():
            s_ref[...] = jnp.zeros_like(s_ref)

        def chunk(c, carry):
            rows = pl.ds(pl.multiple_of(c * CHUNK, CHUNK), CHUNK)
            cs, sn = cos_ref[rows, :], sin_ref[rows, :]
            for h in range(RET_HEADS):
                q = proj_ref[rows, h * RET_DK:(h + 1) * RET_DK]
                k = proj_ref[rows, RET_QK_W + h * RET_DK:RET_QK_W + (h + 1) * RET_DK]
                v = proj_ref[rows, 2 * RET_QK_W + h * RET_DV:2 * RET_QK_W + (h + 1) * RET_DV]
                g = proj_ref[rows, 2 * RET_QK_W + RET_V_W + h * RET_DV:2 * RET_QK_W + RET_V_W + (h + 1) * RET_DV]
                qr = _rope_half(q, cs, sn)
                kr = _rope_half(k, cs, sn) * (RET_DK ** -0.5)
                qb, kb, vb = qr.astype(BF16), kr.astype(BF16), v.astype(BF16)
                sc = _dot(qb, kb, 1, 1) * intra_ref[h]
                inner = _dot(sc.astype(BF16), vb, 1, 0)
                s_old = s_ref[h]
                sb = s_old.astype(BF16)
                st_ref[c, h] = sb
                cross = _dot((qr * qd_ref[h]).astype(BF16), sb, 1, 0)
                out = inner + cross
                s_ref[h] = s_old * cd_ref[h] + _dot((kr * kd_ref[h]).astype(BF16), vb, 0, 0)
                r = lax.rsqrt(jnp.mean(out * out, axis=-1, keepdims=True) + EPS)
                y = out * r * gn_ref[h]
                cols = slice(h * RET_DV, (h + 1) * RET_DV)
                gated_ref[rows, cols] = (g * _sigmoid(g) * y).astype(BF16)
                outp_ref[rows, cols] = out
            return carry

        lax.fori_loop(0, cpb, chunk, 0)

    return pl.pallas_call(
        body, name="ret_fwd", grid=(t // rb,),
        in_specs=[sp['proj'], sp['tab'], sp['tab'], sp['gn'], sp['intra'], sp['dec'], sp['dec'], sp['cdec']],
        out_specs=[sp['vw'], sp['vw'], sp['st']],
        out_shape=[_sds((t, RET_V_W), BF16), _sds((t, RET_V_W), F32),
                   _sds((t // CHUNK, RET_HEADS, RET_DK, RET_DV), BF16)],
        scratch_shapes=[pltpu.VMEM((RET_HEADS, RET_DK, RET_DV), F32)],
        compiler_params=_cparams(("arbitrary",)),
    )(proj, cos, sin, gn.reshape(RET_HEADS, 1, RET_DV), intra, qdec, kdec, cdec)


def _ret_bwd(proj, cos, sin, gn, outp, states, dgated):
    t = proj.shape[0]
    rb = min(RET_ROWS, t)
    cpb = rb // CHUNK
    nb = t // rb
    intra, qdec, kdec, cdec = _ret_consts()
    sp = _ret_specs(rb, rev_nb=nb)

    def body(proj_ref, cos_ref, sin_ref, gn_ref, intra_ref, qd_ref, kd_ref, cd_ref, outp_ref, st_ref, dgt_ref,
             dproj_ref, dgn_ref, ds_ref):
        @pl.when(pl.program_id(0) == 0)
        def _():
            ds_ref[...] = jnp.zeros_like(ds_ref)
            dgn_ref[...] = jnp.zeros_like(dgn_ref)

        def chunk(cc, carry):
            c = cpb - 1 - cc
            rows = pl.ds(pl.multiple_of(c * CHUNK, CHUNK), CHUNK)
            cs, sn = cos_ref[rows, :], sin_ref[rows, :]
            for h in range(RET_HEADS):
                q = proj_ref[rows, h * RET_DK:(h + 1) * RET_DK]
                k = proj_ref[rows, RET_QK_W + h * RET_DK:RET_QK_W + (h + 1) * RET_DK]
                v = proj_ref[rows, 2 * RET_QK_W + h * RET_DV:2 * RET_QK_W + (h + 1) * RET_DV]
                g = proj_ref[rows, 2 * RET_QK_W + RET_V_W + h * RET_DV:2 * RET_QK_W + RET_V_W + (h + 1) * RET_DV]
                cols = slice(h * RET_DV, (h + 1) * RET_DV)
                qr = _rope_half(q, cs, sn)
                kr = _rope_half(k, cs, sn) * (RET_DK ** -0.5)
                qb, kb, vb = qr.astype(BF16), kr.astype(BF16), v.astype(BF16)
                qdb = (qr * qd_ref[h]).astype(BF16)
                kdb = (kr * kd_ref[h]).astype(BF16)
                out = outp_ref[rows, cols]
                dgt = dgt_ref[rows, cols]
                gnh = gn_ref[h]
                r = lax.rsqrt(jnp.mean(out * out, axis=-1, keepdims=True) + EPS)
                xh = out * r
                sg = _sigmoid(g)
                dgate = dgt * (xh * gnh) * (sg * (1.0 + g * (1.0 - sg)))
                dy = dgt * (g * sg)
                dgn_ref[h] += jnp.sum(dy * xh, axis=0, keepdims=True)
                dxh = dy * gnh
                dout = r * (dxh - xh * jnp.mean(dxh * xh, axis=-1, keepdims=True))
                doutb = dout.astype(BF16)
                itr = intra_ref[h]
                pb = (_dot(qb, kb, 1, 1) * itr).astype(BF16)
                dv = _dot(pb, doutb, 0, 0)
                dsc = (_dot(doutb, vb, 1, 1) * itr).astype(BF16)
                dq = _dot(dsc, kb, 1, 0)
                dk = _dot(dsc, qb, 0, 0)
                dq = dq + _dot(doutb, st_ref[c, h], 1, 1) * qd_ref[h]
                ds_new = ds_ref[h]
                dsb = ds_new.astype(BF16)
                dk = dk + _dot(vb, dsb, 1, 1) * kd_ref[h]
                dv = dv + _dot(kdb, dsb, 1, 0)
                ds_ref[h] = ds_new * cd_ref[h] + _dot(qdb, doutb, 0, 0)
                dproj_ref[rows, h * RET_DK:(h + 1) * RET_DK] = _rope_half(dq, cs, -sn).astype(BF16)
                dproj_ref[rows, RET_QK_W + h * RET_DK:RET_QK_W + (h + 1) * RET_DK] = (
                    _rope_half(dk * (RET_DK ** -0.5), cs, -sn).astype(BF16))
                dproj_ref[rows, 2 * RET_QK_W + h * RET_DV:2 * RET_QK_W + (h + 1) * RET_DV] = dv.astype(BF16)
                dproj_ref[rows, 2 * RET_QK_W + RET_V_W + h * RET_DV:
                          2 * RET_QK_W + RET_V_W + (h + 1) * RET_DV] = dgate.astype(BF16)
            return carry

        lax.fori_loop(0, cpb, chunk, 0)

    return pl.pallas_call(
        body, name="ret_bwd", grid=(nb,),
        in_specs=[sp['proj'], sp['tab'], sp['tab'], sp['gn'], sp['intra'], sp['dec'], sp['dec'], sp['cdec'],
                  sp['vw'], sp['st'], sp['vw']],
        out_specs=[sp['proj'], sp['gn']],
        out_shape=[_sds((t, RET_IN), BF16), _sds((RET_HEADS, 1, RET_DV), F32)],
        scratch_shapes=[pltpu.VMEM((RET_HEADS, RET_DK, RET_DV), F32)],
        compiler_params=_cparams(("arbitrary",)),
    )(proj, cos, sin, gn.reshape(RET_HEADS, 1, RET_DV), intra, qdec, kdec, cdec, outp, states, dgated)


def _mla_tables(t):
    half = MLA_ROPE // 2
    inv = 1.0 / (ROPE_THETA ** (jnp.arange(0, MLA_ROPE, 2, dtype=F32) / MLA_ROPE))
    ang = jnp.arange(t, dtype=F32)[:, None] * inv[None, :]
    cos, sin = jnp.cos(ang), jnp.sin(ang)
    z = jnp.zeros((t, half), F32)
    c = jnp.concatenate([cos, cos, z, z], axis=1)
    s1 = jnp.concatenate([-sin, z, z, z], axis=1)
    s2 = jnp.concatenate([z, sin, z, z], axis=1)
    return c, s1, s2


def _rope_tile(r, c, s1, s2):
    return r * c + pltpu.roll(r, 96, 1) * s1 + pltpu.roll(r, 32, 1) * s2


def _mla_mid(proj2, qa, kva):
    t = proj2.shape[0]
    tm = _row_tile(t)

    def body(p_ref, qa_ref, kva_ref, cq_ref, ckv_ref):
        cq = p_ref[:, :MLA_Q_RANK]
        ckv = p_ref[:, MLA_Q_RANK:MLA_Q_RANK + MLA_KV_RANK]
        rq = lax.rsqrt(jnp.mean(cq * cq, axis=-1, keepdims=True) + EPS)
        rkv = lax.rsqrt(jnp.mean(ckv * ckv, axis=-1, keepdims=True) + EPS)
        cq_ref[...] = (cq * rq * qa_ref[...]).astype(BF16)
        ckv_ref[...] = (ckv * rkv * kva_ref[...]).astype(BF16)

    return pl.pallas_call(
        body, name="mla_mid", grid=(t // tm,),
        in_specs=[pl.BlockSpec((tm, MLA_IN_PAD), lambda i: (i, 0)),
                  pl.BlockSpec((1, MLA_Q_RANK), lambda i: (0, 0)),
                  pl.BlockSpec((1, MLA_KV_RANK), lambda i: (0, 0))],
        out_specs=[pl.BlockSpec((tm, MLA_Q_RANK), lambda i: (i, 0)),
                   pl.BlockSpec((tm, MLA_KV_RANK), lambda i: (i, 0))],
        out_shape=[_sds((t, MLA_Q_RANK), BF16), _sds((t, MLA_KV_RANK), BF16)],
        compiler_params=_cparams(("parallel",)),
    )(proj2, qa, kva)


def _mla_mid_bwd(proj2, qa, kva, dcq, dckv, dkr):
    t = proj2.shape[0]
    tm = _row_tile(t)

    def body(p_ref, qa_ref, kva_ref, dcq_ref, dckv_ref, dkr_ref, dp_ref, dqa_ref, dkva_ref):
        @pl.when(pl.program_id(0) == 0)
        def _():
            dqa_ref[...] = jnp.zeros_like(dqa_ref)
            dkva_ref[...] = jnp.zeros_like(dkva_ref)

        dxq, dgq = _rms_bwd_rows(dcq_ref[...], p_ref[:, :MLA_Q_RANK], qa_ref[...], MLA_Q_RANK)
        dxk, dgk = _rms_bwd_rows(dckv_ref[...], p_ref[:, MLA_Q_RANK:MLA_Q_RANK + MLA_KV_RANK], kva_ref[...],
                                 MLA_KV_RANK)
        dp_ref[:, :MLA_Q_RANK] = dxq.astype(BF16)
        dp_ref[:, MLA_Q_RANK:MLA_Q_RANK + MLA_KV_RANK] = dxk.astype(BF16)
        dp_ref[:, MLA_Q_RANK + MLA_KV_RANK:] = dkr_ref[...].astype(BF16)
        dqa_ref[...] += jnp.sum(dgq, axis=0, keepdims=True)
        dkva_ref[...] += jnp.sum(dgk, axis=0, keepdims=True)

    return pl.pallas_call(
        body, name="mla_mid_bwd", grid=(t // tm,),
        in_specs=[pl.BlockSpec((tm, MLA_IN_PAD), lambda i: (i, 0)),
                  pl.BlockSpec((1, MLA_Q_RANK), lambda i: (0, 0)),
                  pl.BlockSpec((1, MLA_KV_RANK), lambda i: (0, 0)),
                  pl.BlockSpec((tm, MLA_Q_RANK), lambda i: (i, 0)),
                  pl.BlockSpec((tm, MLA_KV_RANK), lambda i: (i, 0)),
                  pl.BlockSpec((tm, 128), lambda i: (i, 0))],
        out_specs=[pl.BlockSpec((tm, MLA_IN_PAD), lambda i: (i, 0)),
                   pl.BlockSpec((1, MLA_Q_RANK), lambda i: (0, 0)),
                   pl.BlockSpec((1, MLA_KV_RANK), lambda i: (0, 0))],
        out_shape=[_sds((t, MLA_IN_PAD), BF16), _sds((1, MLA_Q_RANK), F32), _sds((1, MLA_KV_RANK), F32)],
        compiler_params=_cparams(("arbitrary",)),
    )(proj2, qa, kva, dcq, dckv, dkr)


def _mla_prep_specs(t, tm):
    head = lambda w: pl.BlockSpec((None, tm, w), lambda i, h: (h, i, 0))
    return dict(
        head256=head(MLA_HD_PAD), head128=head(MLA_VD),
        kr=pl.BlockSpec((tm, 128), lambda i, h: (i, (MLA_Q_RANK + MLA_KV_RANK) // 128)),
        gain=pl.BlockSpec((1, MLA_HD_PAD), lambda i, h: (0, 0)),
        tab=pl.BlockSpec((tm, 128), lambda i, h: (i, 0)),
    )


def _mla_prep(q, kv, proj2, gq, gk, tabs):
    t = q.shape[1]
    tm = _row_tile(t)
    sp = _mla_prep_specs(t, tm)

    def body(q_ref, kv_ref, kr_ref, gq_ref, gk_ref, c_ref, s1_ref, s2_ref, qh_ref, kh_ref, vh_ref):
        c, s1, s2 = c_ref[...], s1_ref[...], s2_ref[...]

        def norm_rope(xv, gain):
            r = lax.rsqrt(jnp.sum(xv * xv, axis=-1, keepdims=True) / MLA_QKD + EPS)
            y = xv * r * gain
            return jnp.concatenate([y[:, :MLA_NOPE], _rope_tile(y[:, MLA_NOPE:], c, s1, s2)], axis=-1)

        kvv = kv_ref[...]
        qh_ref[...] = norm_rope(q_ref[...], gq_ref[...]).astype(BF16)
        kf = jnp.concatenate([kvv[:, :MLA_NOPE], kr_ref[...]], axis=-1)
        kh_ref[...] = norm_rope(kf, gk_ref[...]).astype(BF16)
        vh_ref[...] = jnp.concatenate([kvv[:, MLA_NOPE:], jnp.ones((tm, MLA_VD), F32)], axis=-1).astype(BF16)

    return pl.pallas_call(
        body, name="mla_prep", grid=(t // tm, MLA_HEADS),
        in_specs=[sp['head256'], sp['head256'], sp['kr'], sp['gain'], sp['gain'], sp['tab'], sp['tab'], sp['tab']],
        out_specs=[sp['head256'], sp['head256'], sp['head256']],
        out_shape=[_sds((MLA_HEADS, t, MLA_HD_PAD), BF16), _sds((MLA_HEADS, t, MLA_HD_PAD), BF16),
                   _sds((MLA_HEADS, t, 2 * MLA_VD), BF16)],
        compiler_params=_cparams(("parallel", "arbitrary")),
    )(q, kv, proj2, gq, gk, *tabs)


def _mla_prep_bwd(q, kv, proj2, gq, gk, tabs, dqt, dkh, dvh):
    t = q.shape[1]
    tm = _row_tile(t)
    ab = dqt.shape[-1]
    sp = _mla_prep_specs(t, tm)

    def body(q_ref, kv_ref, kr_ref, gq_ref, gk_ref, c_ref, s1_ref, s2_ref, dqt_ref, dkh_ref, dvh_ref,
             dq_ref, dkv_ref, dkr_ref, dgq_ref, dgk_ref):
        dqh = jnp.concatenate([dqt_ref[b].T for b in range(tm // ab)], axis=0)
        i, h = pl.program_id(0), pl.program_id(1)

        @pl.when((i == 0) & (h == 0))
        def _():
            dgq_ref[...] = jnp.zeros_like(dgq_ref)
            dgk_ref[...] = jnp.zeros_like(dgk_ref)

        @pl.when(h == 0)
        def _():
            dkr_ref[...] = jnp.zeros_like(dkr_ref)

        c, s1, s2 = c_ref[...], s1_ref[...], s2_ref[...]

        def back(xv, gain, dout):
            dy = jnp.concatenate([dout[:, :MLA_NOPE], _rope_tile(dout[:, MLA_NOPE:], c, -s1, -s2)], axis=-1)
            return _rms_bwd_rows(dy, xv, gain, MLA_QKD)

        kvv = kv_ref[...]
        dxq, dgq = back(q_ref[...], gq_ref[...], dqh)
        kf = jnp.concatenate([kvv[:, :MLA_NOPE], kr_ref[...]], axis=-1)
        dxk, dgk = back(kf, gk_ref[...], dkh_ref[...])
        dq_ref[...] = dxq.astype(BF16)
        dkv_ref[...] = jnp.concatenate([dxk[:, :MLA_NOPE], dvh_ref[...]], axis=-1).astype(BF16)
        dkr_ref[...] += dxk[:, MLA_NOPE:]
        dgq_ref[...] += jnp.sum(dgq, axis=0, keepdims=True)
        dgk_ref[...] += jnp.sum(dgk, axis=0, keepdims=True)

    return pl.pallas_call(
        body, name="mla_prep_bwd", grid=(t // tm, MLA_HEADS),
        in_specs=[sp['head256'], sp['head256'], sp['kr'], sp['gain'], sp['gain'], sp['tab'], sp['tab'], sp['tab'],
                  pl.BlockSpec((None, tm // ab, MLA_HD_PAD, ab), lambda i, h: (h, i, 0, 0)),
                  sp['head256'], sp['head128']],
        out_specs=[sp['head256'], sp['head256'], sp['tab'], sp['gain'], sp['gain']],
        out_shape=[_sds((MLA_HEADS, t, MLA_HD_PAD), BF16), _sds((MLA_HEADS, t, MLA_HD_PAD), BF16),
                   _sds((t, 128), F32), _sds((1, MLA_HD_PAD), F32), _sds((1, MLA_HD_PAD), F32)],
        compiler_params=_cparams(("arbitrary", "arbitrary")),
    )(q, kv, proj2, gq, gk, *tabs, dqt, dkh, dvh)


def _chunk_visible(rows, cols, row_off, col_off):
    rq = lax.shift_right_logical(lax.broadcasted_iota(jnp.int32, (rows, cols), 0) + row_off, 6)
    ck = lax.shift_right_logical(lax.broadcasted_iota(jnp.int32, (rows, cols), 1) + col_off, 6)
    return ck <= rq


def _rows_to_lanes(col):
    return col.T[:8, :]


def _attn_fwd(qh, kh, vh):
    t = qh.shape[1]
    ab = min(ATT_BLOCK, t)
    tq = min(ATT_QROWS, t)
    r = tq // ab
    hg = ATT_HEADS

    def body(q_ref, k_ref, v_ref, o_ref, lse_ref):
        n_un = pl.program_id(1) * r

        def step(b, state, diag):
            rows = pl.ds(pl.multiple_of(b * ab, ab), ab)
            ms, accs = [], []
            for hh in range(hg):
                m, acc = state[0][hh], state[1][hh]
                s = _dot(q_ref[hh], k_ref[hh, rows, :], 1, 1)
                if diag is not None:
                    s = jnp.where(_chunk_visible(tq, ab, 0, diag * ab), s, -1e30)
                m_new = jnp.maximum(m, jnp.max(s, axis=-1, keepdims=True))
                p = jnp.exp2((s - m_new) * ATT_EXP2).astype(BF16)
                accs.append(jnp.exp2((m - m_new) * ATT_EXP2) * acc + _dot(p, v_ref[hh, rows, :], 1, 0))
                ms.append(m_new)
            return tuple(ms), tuple(accs)

        heads = lambda v: tuple(v for _ in range(hg))
        state = (heads(jnp.full((tq, 1), -1e30, F32)), heads(jnp.zeros((tq, 2 * MLA_VD), F32)))
        state = lax.fori_loop(0, n_un, lambda b, st: step(b, st, None), state)
        for d in range(r):
            state = step(n_un + d, state, d)
        ms, accs = state
        for hh in range(hg):
            l = accs[hh][:, MLA_VD:]
            o_ref[:, hh * MLA_VD:(hh + 1) * MLA_VD] = accs[hh][:, :MLA_VD] / l
            lse_t = _rows_to_lanes(ms[hh] * ATT_EXP2 + jnp.log(l) * LOG2E)
            for d in range(r):
                lse_ref[hh, d] = lse_t[:, d * ab:(d + 1) * ab]

    return pl.pallas_call(
        body, name="mla_attn", grid=(MLA_HEADS // hg, t // tq),
        in_specs=[pl.BlockSpec((hg, tq, MLA_HD_PAD), lambda g, i: (g, i, 0)),
                  pl.BlockSpec((hg, t, MLA_HD_PAD), lambda g, i: (g, 0, 0)),
                  pl.BlockSpec((hg, t, 2 * MLA_VD), lambda g, i: (g, 0, 0))],
        out_specs=[pl.BlockSpec((tq, hg * MLA_VD), lambda g, i: (i, g)),
                   pl.BlockSpec((hg, r, 8, ab), lambda g, i: (g, i, 0, 0))],
        out_shape=[_sds((t, MLA_HEADS * MLA_VD), F32), _sds((MLA_HEADS, t // ab, 8, ab), F32)],
        compiler_params=_cparams(("parallel", "arbitrary")),
    )(qh, kh, vh)


def _attn_delta(do, o, ab):
    t = do.shape[0]
    tm = _row_tile(t)

    def body(do_ref, o_ref, d_ref):
        d = jnp.sum(do_ref[...] * o_ref[...], axis=-1, keepdims=True)
        d_t = _rows_to_lanes(jnp.broadcast_to(d, (tm, 128)))
        for b in range(tm // ab):
            d_ref[b] = d_t[:, b * ab:(b + 1) * ab]

    col = pl.BlockSpec((tm, MLA_VD), lambda i, h: (i, h))
    return pl.pallas_call(
        body, name="mla_delta", grid=(t // tm, MLA_HEADS), in_specs=[col, col],
        out_specs=pl.BlockSpec((None, tm // ab, 8, ab), lambda i, h: (h, i, 0, 0)),
        out_shape=_sds((MLA_HEADS, t // ab, 8, ab), F32),
        compiler_params=_cparams(("parallel", "parallel")),
    )(do, o)


def _attn_bwd(qh, kh, vh, dob, lse_t, dl_t):
    t = qh.shape[1]
    ab = min(ATT_BLOCK, t)
    nq = t // ab
    hg = ATT_HEADS

    def body(q_ref, k_ref, v_ref, do_ref, lse_ref, dl_ref, dqt_ref, dk_ref, dv_ref):
        j = pl.program_id(1)

        @pl.when(j == 0)
        def _():
            dqt_ref[...] = jnp.zeros_like(dqt_ref)

        ks = [k_ref[hh] for hh in range(hg)]
        vs = [v_ref[hh, :, :MLA_VD] for hh in range(hg)]
        kts = [k.T for k in ks]

        def step(b, grads, masked):
            rows = pl.ds(pl.multiple_of(b * ab, ab), ab)
            out = []
            for hh in range(hg):
                dk, dv = grads[hh]
                q = q_ref[hh, rows, :]
                do = do_ref[rows, hh * MLA_VD:(hh + 1) * MLA_VD]
                s_t = _dot(ks[hh], q, 1, 1)
                if masked:
                    key_chunk = lax.shift_right_logical(lax.broadcasted_iota(jnp.int32, (ab, ab), 0), 6)
                    query_chunk = lax.shift_right_logical(lax.broadcasted_iota(jnp.int32, (ab, ab), 1), 6)
                    s_t = jnp.where(key_chunk <= query_chunk, s_t, -1e30)
                p_t = jnp.exp2(s_t * ATT_EXP2 - lse_ref[hh, b][0:1, :])
                dp_t = _dot(vs[hh], do, 1, 1)
                ds_t = (p_t * (dp_t - dl_ref[hh, b][0:1, :]) * ATT_SCALE).astype(BF16)
                dqt_ref[hh, b] += _dot(kts[hh], ds_t, 1, 0)
                out.append((dk + _dot(ds_t, q, 1, 0), dv + _dot(p_t.astype(BF16), do, 1, 0)))
            return tuple(out)

        grads = tuple((jnp.zeros((ab, MLA_HD_PAD), F32), jnp.zeros((ab, MLA_VD), F32)) for _ in range(hg))
        grads = step(j, grads, True)
        grads = lax.fori_loop(j + 1, nq, lambda b, g: step(b, g, False), grads)
        for hh in range(hg):
            dk_ref[hh] = grads[hh][0]
            dv_ref[hh] = grads[hh][1]

    whole = lambda w: pl.BlockSpec((hg, t, w), lambda g, j: (g, 0, 0))
    blk = lambda w: pl.BlockSpec((hg, ab, w), lambda g, j: (g, j, 0))
    stat = pl.BlockSpec((hg, nq, 8, ab), lambda g, j: (g, 0, 0, 0))
    return pl.pallas_call(
        body, name="mla_attn_bwd", grid=(MLA_HEADS // hg, nq),
        in_specs=[whole(MLA_HD_PAD), blk(MLA_HD_PAD), blk(2 * MLA_VD),
                  pl.BlockSpec((t, hg * MLA_VD), lambda g, j: (0, g)), stat, stat],
        out_specs=[pl.BlockSpec((hg, nq, MLA_HD_PAD, ab), lambda g, j: (g, 0, 0, 0)), blk(MLA_HD_PAD), blk(MLA_VD)],
        out_shape=[_sds((MLA_HEADS, nq, MLA_HD_PAD, ab), F32), _sds((MLA_HEADS, t, MLA_HD_PAD), F32),
                   _sds((MLA_HEADS, t, MLA_VD), F32)],
        compiler_params=_cparams(("parallel", "arbitrary")),
    )(qh, kh, vh, dob, lse_t, dl_t)


VEC = pl.BlockSpec((1, D_MODEL), lambda i, j, k: (0, 0))


def _residual_epi(next_gain):
    if next_gain is None:
        return [], lambda acc, hv: (acc + hv,)

    def epi(acc, hv, g):
        h_new = acc + hv
        r = lax.rsqrt(jnp.mean(h_new * h_new, axis=-1, keepdims=True) + EPS)
        return h_new, h_new * r * g

    return [(next_gain, VEC)], epi


def _residual_outs(t, row, next_gain):
    outs = [(_sds((t, D_MODEL), F32), row)]
    return outs + ([(_sds((t, D_MODEL), BF16), row)] if next_gain is not None else [])


def _mlp_fwd(l, h, hn, w1g, fetch_w2, next_gain):
    t = h.shape[0]
    tm = _row_tile(t)
    nsh, _, wsh = w1g.shape

    def relu2(acc):
        r = jnp.maximum(acc, 0.0)
        return r, r * r

    tile = pl.BlockSpec((tm, wsh), lambda i, j, k: (i, j))
    r, u = _mm(f"mlp_up{l}", (t // tm, nsh, 1),
               hn, pl.BlockSpec((tm, D_MODEL), lambda i, j, k: (i, 0)),
               w1g, pl.BlockSpec((None, D_MODEL, wsh), lambda i, j, k: (j, 0, 0)), (1, 0),
               [(_sds((t, D_FF), BF16), tile), (_sds((t, D_FF), BF16), tile)], epi=relu2)
    w2g = fetch_w2((u,))
    row = pl.BlockSpec((tm, D_MODEL), lambda i, j, k: (i, 0))
    more, epi = _residual_epi(next_gain)
    h2, hn_next = _mm(f"mlp_down{l}", (t // tm, 1, nsh),
                      u, pl.BlockSpec((tm, wsh), lambda i, j, k: (i, k)),
                      w2g, pl.BlockSpec((None, wsh, D_MODEL), lambda i, j, k: (k, 0, 0)), (1, 0),
                      _residual_outs(t, row, next_gain), extras=[(h, row)] + more, epi=epi)
    return h2, hn_next, (h, hn, r, u, w1g, w2g)


def _mlp_bwd(l, dh, saved, norm_g):
    h, hn, r, u, w1g, w2g = saved
    t = h.shape[0]
    tm = _row_tile(t)
    nsh, _, wsh = w1g.shape
    tile = pl.BlockSpec((tm, wsh), lambda i, j, k: (i, j))
    (da,) = _mm(f"mlp_du{l}", (t // tm, nsh, 1),
                dh, pl.BlockSpec((tm, D_MODEL), lambda i, j, k: (i, 0)),
                w2g, pl.BlockSpec((None, wsh, D_MODEL), lambda i, j, k: (j, 0, 0)), (1, 1),
                [(_sds((t, D_FF), BF16), tile)], extras=[(r, tile)],
                epi=lambda acc, rv: (2.0 * rv.astype(F32) * acc,))
    tw = _row_tile(t, 512)
    (dw2,) = _mm(f"mlp_dw2{l}", (1, 1, t // tw),
                 u, pl.BlockSpec((tw, D_FF), lambda i, j, k: (k, 0)),
                 dh, pl.BlockSpec((tw, D_MODEL), lambda i, j, k: (k, 0)), (0, 0),
                 [(_sds((D_FF, D_MODEL), BF16), pl.BlockSpec((D_FF, D_MODEL), lambda i, j, k: (0, 0)))])
    dw2 = dw2.reshape(nsh, wsh, D_MODEL)
    (dw1,) = _mm(f"mlp_dw1{l}", (1, 1, t // tw),
                 hn, pl.BlockSpec((tw, D_MODEL), lambda i, j, k: (k, 0)),
                 da, pl.BlockSpec((tw, D_FF), lambda i, j, k: (k, 0)), (0, 0),
                 [(_sds((nsh, D_MODEL, wsh), BF16), pl.BlockSpec((nsh, D_MODEL, wsh), lambda i, j, k: (0, 0, 0)))],
                 split=wsh)
    (dhn,) = _mm(f"mlp_dhn{l}", (t // tm, 1, nsh),
                 da, pl.BlockSpec((tm, wsh), lambda i, j, k: (i, k)),
                 w1g, pl.BlockSpec((None, D_MODEL, wsh), lambda i, j, k: (k, 0, 0)), (1, 1),
                 [(_sds((t, D_MODEL), F32), pl.BlockSpec((tm, D_MODEL), lambda i, j, k: (i, 0)))])
    dh_in, dg = _rms_bwd(f"mlp_norm_bwd{l}", dhn, h, norm_g, dh)
    return dh_in, dg, dw1, dw2


def _ple_fwd(l, h, hn, p, wg, wp, next_gain):
    t = h.shape[0]
    tm = _row_tile(t, 512)
    row = pl.BlockSpec((tm, D_MODEL), lambda i, j, k: (i, 0))
    full = lambda r: pl.BlockSpec((r, D_MODEL), lambda i, j, k: (0, 0))
    (e,) = _mm(f"ple_proj{l}", (t // tm, 1, 1),
               p, pl.BlockSpec((None, None, tm, PLE_DIM), lambda i, j, k: (l, 0, i, 0)),
               wp, full(PLE_DIM), (1, 0), [(_sds((t, D_MODEL), F32), row)])

    def gate_epi(acc, hv, ev, *gain):
        gt = _sigmoid(acc)
        h_new = hv + gt * ev
        if not gain:
            return h_new, gt
        r = lax.rsqrt(jnp.mean(h_new * h_new, axis=-1, keepdims=True) + EPS)
        return h_new, gt, h_new * r * gain[0]

    f32_row, bf_row = (_sds((t, D_MODEL), F32), row), (_sds((t, D_MODEL), BF16), row)
    res = _mm(f"ple_gate{l}", (t // tm, 1, 1), hn, row, wg, full(D_MODEL), (1, 0),
              [f32_row, f32_row] + ([bf_row] if next_gain is not None else []),
              extras=[(h, row), (e, row)] + ([(next_gain, VEC)] if next_gain is not None else []), epi=gate_epi)
    h_out, gate = res[0], res[1]
    return h_out, (res[2] if next_gain is not None else None), (h, hn, gate, e)


def _ple_bwd(l, dh, saved, p, norm_g, wg, deps=()):
    h, hn, gate, e = saved
    t = h.shape[0]
    tm = _row_tile(t)
    tk = _row_tile(t, 512)
    de, dz = _ple_gate_bwd(f"ple_gate_bwd{l}", dh, gate, e)
    full = lambda r: pl.BlockSpec((r, D_MODEL), lambda i, j, k: (0, 0))
    rowk = pl.BlockSpec((tk, D_MODEL), lambda i, j, k: (k, 0))
    (dwp,) = _mm(f"ple_dwp{l}", (1, 1, t // tk),
                 p, pl.BlockSpec((None, None, tk, PLE_DIM), lambda i, j, k: (l, 0, k, 0)),
                 de, rowk, (0, 0), [(_sds((PLE_DIM, D_MODEL), BF16), full(PLE_DIM))], deps=deps)
    (dwg,) = _mm(f"ple_dwg{l}", (1, 1, t // tk), hn, rowk, dz, rowk, (0, 0),
                 [(_sds((D_MODEL, D_MODEL), BF16), full(D_MODEL))])
    row = pl.BlockSpec((tm, D_MODEL), lambda i, j, k: (i, 0))
    (dhn,) = _mm(f"ple_dhn{l}", (t // tm, 1, 1), dz, row, wg, full(D_MODEL), (1, 1),
                 [(_sds((t, D_MODEL), F32), row)])
    dh_in, dg = _rms_bwd(f"ple_norm_bwd{l}", dhn, h, norm_g, dh)
    return dh_in, dg, dwg, dwp


def _ret_layer_fwd(x, norm_g, wri, fetch_wro, gn, cos, sin, next_gain, deps=()):
    t = x.shape[0]
    tm = _row_tile(t)
    nsh, _, wsh = wri.shape
    hn = _rms_fwd("mix_norm0", x, norm_g)
    (proj,) = _mm("ret_in", (t // tm, nsh, 1),
                  hn, pl.BlockSpec((tm, D_MODEL), lambda i, j, k: (i, 0)),
                  wri, pl.BlockSpec((None, D_MODEL, wsh), lambda i, j, k: (j, 0, 0)), (1, 0),
                  [(_sds((t, RET_IN), F32), pl.BlockSpec((tm, wsh), lambda i, j, k: (i, j)))], deps=deps)
    gated, outp, states = _ret_fwd(proj, cos, sin, gn)
    wro = fetch_wro((gated,))
    row = pl.BlockSpec((tm, D_MODEL), lambda i, j, k: (i, 0))
    kt = 512
    more, epi = _residual_epi(next_gain)
    h1, hn_next = _mm("ret_out", (t // tm, 1, RET_V_W // kt),
                      gated, pl.BlockSpec((tm, kt), lambda i, j, k: (i, k)),
                      wro, pl.BlockSpec((kt, D_MODEL), lambda i, j, k: (k, 0)), (1, 0),
                      _residual_outs(t, row, next_gain), extras=[(x, row)] + more, epi=epi)
    return h1, hn_next, (x, hn, proj, gated, outp, states, wro)


def _ret_layer_bwd(dh, saved, norm_g, wri, gn, cos, sin, emit, deps=()):
    x, hn, proj, gated, outp, states, wro = saved
    t = x.shape[0]
    tm = _row_tile(t)
    tk = _row_tile(t, 512)
    nsh, _, wsh = wri.shape
    (dgated,) = _mm("ret_dgated", (t // tm, RET_V_W // D_MODEL, 1),
                    dh, pl.BlockSpec((tm, D_MODEL), lambda i, j, k: (i, 0)),
                    wro, pl.BlockSpec((D_MODEL, D_MODEL), lambda i, j, k: (j, 0)), (1, 1),
                    [(_sds((t, RET_V_W), F32), pl.BlockSpec((tm, D_MODEL), lambda i, j, k: (i, j)))], deps=deps)
    (dwro,) = _mm("ret_dwro", (1, 1, t // tk),
                  gated, pl.BlockSpec((tk, RET_V_W), lambda i, j, k: (k, 0)),
                  dh, pl.BlockSpec((tk, D_MODEL), lambda i, j, k: (k, 0)), (0, 0),
                  [(_sds((RET_V_W, D_MODEL), BF16), pl.BlockSpec((RET_V_W, D_MODEL), lambda i, j, k: (0, 0)))])
    dproj, dgn = _ret_bwd(proj, cos, sin, gn, outp, states, dgated)
    half = nsh // 2
    (dwri,) = _mm("ret_dwri", (2, 1, t // tk),
                  hn, pl.BlockSpec((tk, D_MODEL), lambda i, j, k: (k, 0)),
                  dproj, pl.BlockSpec((tk, half * wsh), lambda i, j, k: (k, i)), (0, 0),
                  [(_sds((nsh, D_MODEL, wsh), BF16), pl.BlockSpec((half, D_MODEL, wsh), lambda i, j, k: (i, 0, 0)))],
                  split=wsh)
    deps = emit(dwro, dwri)
    (dhn,) = _mm("ret_dhn", (t // tm, 1, nsh),
                 dproj, pl.BlockSpec((tm, wsh), lambda i, j, k: (i, k)),
                 wri, pl.BlockSpec((None, D_MODEL, wsh), lambda i, j, k: (k, 0, 0)), (1, 1),
                 [(_sds((t, D_MODEL), F32), pl.BlockSpec((tm, D_MODEL), lambda i, j, k: (i, 0)))], deps=deps)
    dx, dg = _rms_bwd("mix_norm_bwd0", dhn, x, norm_g, dh)
    return dx, dg, dgn.reshape(RET_HEADS, RET_DV)


def _mla_layer_fwd(h, hn, wmi, qa, kva, wuq, wukv, gq, gk, wmo, tabs, next_gain):
    t = h.shape[0]
    tm = _row_tile(t)
    row = pl.BlockSpec((tm, D_MODEL), lambda i, j, k: (i, 0))
    (proj2,) = _mm("mla_in", (t // tm, 1, 1), hn, row,
                   wmi, pl.BlockSpec((D_MODEL, MLA_IN_PAD), lambda i, j, k: (0, 0)), (1, 0),
                   [(_sds((t, MLA_IN_PAD), F32), pl.BlockSpec((tm, MLA_IN_PAD), lambda i, j, k: (i, 0)))])
    cq, ckv = _mla_mid(proj2, qa, kva)
    head = pl.BlockSpec((None, tm, MLA_HD_PAD), lambda i, j, k: (j, i, 0))
    (q,) = _mm("mla_uq", (t // tm, MLA_HEADS, 1),
               cq, pl.BlockSpec((tm, MLA_Q_RANK), lambda i, j, k: (i, 0)),
               wuq, pl.BlockSpec((None, MLA_Q_RANK, MLA_HD_PAD), lambda i, j, k: (j, 0, 0)), (1, 0),
               [(_sds((MLA_HEADS, t, MLA_HD_PAD), F32), head)])
    (kv,) = _mm("mla_ukv", (t // tm, MLA_HEADS, 1),
                ckv, pl.BlockSpec((tm, MLA_KV_RANK), lambda i, j, k: (i, 0)),
                wukv, pl.BlockSpec((None, MLA_KV_RANK, MLA_HD_PAD), lambda i, j, k: (j, 0, 0)), (1, 0),
                [(_sds((MLA_HEADS, t, MLA_HD_PAD), F32), head)])
    qh, kh, vh = _mla_prep(q, kv, proj2, gq, gk, tabs)
    o, lse = _attn_fwd(qh, kh, vh)
    more, epi = _residual_epi(next_gain)
    h_out, hn_next = _mm("mla_out", (t // tm, 1, 1), o, row,
                         wmo, pl.BlockSpec((D_MODEL, D_MODEL), lambda i, j, k: (0, 0)), (1, 0),
                         _residual_outs(t, row, next_gain), extras=[(h, row)] + more, epi=epi)
    return h_out, hn_next, (h, hn, proj2, cq, ckv, q, kv, qh, kh, vh, o, lse)


def _mla_layer_bwd(dh, saved, norm_g, wmi, qa, kva, wuq, wukv, gq, gk, wmo, tabs, deps=()):
    h, hn, proj2, cq, ckv, q, kv, qh, kh, vh, o, lse = saved
    t = h.shape[0]
    tm = _row_tile(t)
    tk = _row_tile(t, 512)
    row = pl.BlockSpec((tm, D_MODEL), lambda i, j, k: (i, 0))
    rowk = pl.BlockSpec((tk, D_MODEL), lambda i, j, k: (k, 0))
    sq = pl.BlockSpec((D_MODEL, D_MODEL), lambda i, j, k: (0, 0))
    do, dob = _mm("mla_do", (t // tm, 1, 1), dh, row, wmo, sq, (1, 1),
                  [(_sds((t, D_MODEL), F32), row), (_sds((t, D_MODEL), BF16), row)], epi=lambda acc: (acc, acc),
                  deps=deps)
    (dwmo,) = _mm("mla_dwo", (1, 1, t // tk), o, rowk, dh, rowk, (0, 0), [(_sds((D_MODEL, D_MODEL), BF16), sq)])
    delta = _attn_delta(do, o, lse.shape[-1])
    dqt, dkh, dvh = _attn_bwd(qh, kh, vh, dob, lse, delta)
    dq, dkv, dkr, dgq, dgk = _mla_prep_bwd(q, kv, proj2, gq, gk, tabs, dqt, dkh, dvh)

    headk = pl.BlockSpec((None, tk, MLA_HD_PAD), lambda i, j, k: (j, k, 0))
    (dwuq,) = _mm("mla_dwuq", (1, MLA_HEADS, t // tk),
                  cq, pl.BlockSpec((tk, MLA_Q_RANK), lambda i, j, k: (k, 0)), dq, headk, (0, 0),
                  [(_sds((MLA_HEADS, MLA_Q_RANK, MLA_HD_PAD), BF16),
                    pl.BlockSpec((None, MLA_Q_RANK, MLA_HD_PAD), lambda i, j, k: (j, 0, 0)))])
    (dwukv,) = _mm("mla_dwukv", (1, MLA_HEADS, t // tk),
                   ckv, pl.BlockSpec((tk, MLA_KV_RANK), lambda i, j, k: (k, 0)), dkv, headk, (0, 0),
                   [(_sds((MLA_HEADS, MLA_KV_RANK, MLA_HD_PAD), BF16),
                     pl.BlockSpec((None, MLA_KV_RANK, MLA_HD_PAD), lambda i, j, k: (j, 0, 0)))])
    headi = pl.BlockSpec((None, tm, MLA_HD_PAD), lambda i, j, k: (k, i, 0))
    (dcq,) = _mm("mla_dcq", (t // tm, 1, MLA_HEADS), dq, headi,
                 wuq, pl.BlockSpec((None, MLA_Q_RANK, MLA_HD_PAD), lambda i, j, k: (k, 0, 0)), (1, 1),
                 [(_sds((t, MLA_Q_RANK), F32), pl.BlockSpec((tm, MLA_Q_RANK), lambda i, j, k: (i, 0)))])
    (dckv,) = _mm("mla_dckv", (t // tm, 1, MLA_HEADS), dkv, headi,
                  wukv, pl.BlockSpec((None, MLA_KV_RANK, MLA_HD_PAD), lambda i, j, k: (k, 0, 0)), (1, 1),
                  [(_sds((t, MLA_KV_RANK), F32), pl.BlockSpec((tm, MLA_KV_RANK), lambda i, j, k: (i, 0)))])
    dproj2, dqa, dkva = _mla_mid_bwd(proj2, qa, kva, dcq, dckv, dkr)
    win = pl.BlockSpec((D_MODEL, MLA_IN_PAD), lambda i, j, k: (0, 0))
    (dwmi,) = _mm("mla_dwin", (1, 1, t // tk), hn, rowk,
                  dproj2, pl.BlockSpec((tk, MLA_IN_PAD), lambda i, j, k: (k, 0)), (0, 0),
                  [(_sds((D_MODEL, MLA_IN_PAD), BF16), win)])
    (dhn,) = _mm("mla_dhn", (t // tm, 1, 1),
                 dproj2, pl.BlockSpec((tm, MLA_IN_PAD), lambda i, j, k: (i, 0)), wmi, win, (1, 1),
                 [(_sds((t, D_MODEL), F32), row)])
    dh_in, dg = _rms_bwd("mix_norm_bwd1", dhn, h, norm_g, dh)
    return dh_in, dict(mix=dg, wmi=dwmi, qa=dqa, kva=dkva, wuq=dwuq, wukv=dwukv, gq=dgq, gk=dgk, wmo=dwmo)


def _local_step(x, p, target, w, fetch, emit=lambda group: ()):
    t = x.shape[0]
    inv = 1.0 / (ROPE_THETA ** (jnp.arange(0, RET_DK, 2, dtype=F32) / RET_DK))
    ang = jnp.arange(t, dtype=F32)[:, None] * inv[None, :]
    cos_r, sin_r = jnp.cos(ang), jnp.sin(ang)
    tabs = _mla_tables(t)
    row = lambda a, i: a[i:i + 1]

    h1, hn1, s_ret = _ret_layer_fwd(x, row(w['mix_norm'], 0), w['ret_w_in'],
                                    lambda after: fetch('ret_out', after)['ret_w_out'], w['ret_gn'], cos_r, sin_r,
                                    row(w['mlp_norm'], 0), deps=w['deps'])
    h2, hn2, s_mlp0 = _mlp_fwd(0, h1, hn1, fetch('mlp_w1_0', (h1,))['mlp_w1'],
                               lambda after: fetch('mlp_w2_0', after)['mlp_w2'], row(w['ple_norm'], 0))
    w0 = fetch('ple_0', (h2,))
    h3, hn3, s_ple0 = _ple_fwd(0, h2, hn2, p, w0['ple_gate_w'], w0['ple_proj_w'], row(w['mix_norm'], 1))
    wm = fetch('mla', (h3,))
    mla_w = (wm['mla_w_in'], w['mla_q_a_norm'], w['mla_kv_a_norm'], wm['mla_w_uq'], wm['mla_w_ukv'],
             w['mla_q_norm'], w['mla_k_norm'], wm['mla_w_out'], tabs)
    h4, hn4, s_mla = _mla_layer_fwd(h3, hn3, *mla_w, row(w['mlp_norm'], 1))
    w1 = fetch('layer_1', (h4,))
    h5, hn5, s_mlp1 = _mlp_fwd(1, h4, hn4, w1['mlp_w1'], lambda after: w1['mlp_w2'], row(w['ple_norm'], 1))
    y, _, s_ple1 = _ple_fwd(1, h5, hn5, p, w1['ple_gate_w'], w1['ple_proj_w'], None)

    dy, sq_err = _loss_head(y, target)

    n = N_DEV
    colsh = lambda a: a.reshape(a.shape[0], n, a.shape[1] // n).transpose(1, 0, 2)
    rowsh = lambda a: a.reshape(n, a.shape[0] // n, a.shape[1])
    big = {}

    def emit_group(group):
        big.update(group)
        return emit(group)

    dh5, dg_ple1, dwg1, dwp1 = _ple_bwd(1, dy, s_ple1, p, row(w['ple_norm'], 1), w1['ple_gate_w'])
    dh4, dg_mlp1, dw1_1, dw2_1 = _mlp_bwd(1, dh5, s_mlp1, row(w['mlp_norm'], 1))
    deps = emit_group({('ple_gate_w', 1): rowsh(dwg1), ('ple_proj_w', 1): colsh(dwp1),
                       ('mlp_w2', 1): dw2_1, ('mlp_w1', 1): dw1_1})
    dh3, gm = _mla_layer_bwd(dh4, s_mla, row(w['mix_norm'], 1), *mla_w, deps=deps)
    deps = emit_group({('mla_w_out', 0): rowsh(gm['wmo']), ('mla_w_uq', 0): gm['wuq'][:, :, :MLA_QKD],
                       ('mla_w_ukv', 0): gm['wukv'], ('mla_w_in', 0): rowsh(gm['wmi'][:, :MLA_IN])})
    dh2, dg_ple0, dwg0, dwp0 = _ple_bwd(0, dh3, s_ple0, p, row(w['ple_norm'], 0), w0['ple_gate_w'], deps=deps)
    dh1, dg_mlp0, dw1_0, dw2_0 = _mlp_bwd(0, dh2, s_mlp0, row(w['mlp_norm'], 0))
    deps = emit_group({('ple_gate_w', 0): rowsh(dwg0), ('ple_proj_w', 0): colsh(dwp0),
                       ('mlp_w2', 0): dw2_0, ('mlp_w1', 0): dw1_0})
    dx, dg_mix0, dgn = _ret_layer_bwd(
        dh1, s_ret, row(w['mix_norm'], 0), w['ret_w_in'], w['ret_gn'], cos_r, sin_r,
        lambda dwro, dwri: emit_group({('ret_w_out', 0): rowsh(dwro), ('ret_w_in', 0): dwri}), deps=deps)

    small = dict(
        mix_norm=[dg_mix0, gm['mix']], mlp_norm=[dg_mlp0, dg_mlp1], ple_norm=[dg_ple0, dg_ple1],
        ret_gn=dgn, mla_q_a_norm=gm['qa'], mla_kv_a_norm=gm['kva'], mla_q_norm=gm['gq'], mla_k_norm=gm['gk'],
    )
    return sq_err, dx, big, small


def _my_place():
    x, y, c = lax.axis_index("x"), lax.axis_index("y"), lax.axis_index("c")
    return x, y, c


def _flat(px, py, pc):
    return 4 * px + 2 * py + pc


def _peer(x, y, c, r):
    return (1 - x if r & 4 else x, 1 - y if r & 2 else y, 1 - c if r & 1 else c)


def _all_gather(arrays):
    n = len(arrays)

    def body(*refs):
        ins, outs = refs[:n], refs[n:2 * n]
        send_sems, recv_sems, local_sems = refs[2 * n:]
        x, y, c = _my_place()
        me, sibling = (x, y, c), (x, y, 1 - c)
        chips = [(1 - x, y), (x, 1 - y), (1 - x, 1 - y)]

        def copy(a, k, block, to, src=None):
            slot = outs[a].at[_flat(*block)]
            return pltpu.make_async_remote_copy(
                src_ref=slot if src is None else src, dst_ref=slot,
                send_sem=send_sems.at[a, k], recv_sem=recv_sems.at[a, k], device_id=to, device_id_type=MESH)

        mine = [pltpu.make_async_copy(ins[a], outs[a].at[_flat(*me)], local_sems.at[a]) for a in range(n)]
        for cp in mine:
            cp.start()
        first = []
        for a in range(n):
            first.append(copy(a, 0, me, sibling, src=ins[a]))
            first += [copy(a, 1 + j, me, (*chip, c), src=ins[a]) for j, chip in enumerate(chips)]
        for cp in first:
            cp.start()
        passed = []
        for a in range(n):
            for j, chip in enumerate(chips):
                copy(a, 1 + j, (*chip, c), me).wait_recv()
                passed.append(copy(a, 4 + j, (*chip, c), sibling))
                passed[-1].start()
        for a in range(n):
            copy(a, 0, sibling, me).wait_recv()
            for j, chip in enumerate(chips):
                copy(a, 4 + j, (*chip, 1 - c), me).wait_recv()
        for cp in first + passed:
            cp.wait_send()
        for cp in mine:
            cp.wait()

    return pl.pallas_call(
        body, name="all_gather_weights",
        in_specs=[ANY] * n, out_specs=[ANY] * n,
        out_shape=[_sds((N_DEV,) + a.shape, a.dtype) for a in arrays],
        scratch_shapes=[pltpu.SemaphoreType.DMA((n, 7)), pltpu.SemaphoreType.DMA((n, 7)),
                        pltpu.SemaphoreType.DMA((n,))],
    )(*arrays)


HBM = pl.BlockSpec(memory_space=pltpu.HBM)
SEMS = pl.BlockSpec(memory_space=pltpu.SEMAPHORE)
SIDE_EFFECT = pltpu.SideEffectType.DATAFLOW_SIDE_EFFECTING


def _rs_copies(x, y, c, srcs, lands, send_sems, recv_sems):
    copies = []
    for a in range(len(srcs)):
        for r in range(1, N_DEV):
            peer = _peer(x, y, c, r)
            k = a * (N_DEV - 1) + r - 1
            copies.append(pltpu.make_async_remote_copy(
                src_ref=srcs[a].at[_flat(*peer)], dst_ref=lands[a].at[r - 1],
                send_sem=send_sems.at[k], recv_sem=recv_sems.at[k], device_id=peer, device_id_type=MESH))
    return copies


def _rs_start(name, arrays):
    n = len(arrays)
    hbm = lambda a: pltpu.with_memory_space_constraint(a, pltpu.HBM)
    lands = [hbm(lax.empty((N_DEV - 1,) + a.shape[1:], a.dtype)) for a in arrays]

    def body(*refs):
        srcs, lnd = refs[:n], refs[n:2 * n]
        send_sems, recv_sems = refs[2 * n], refs[2 * n + 1]
        token = refs[-1]
        for cp in _rs_copies(*_my_place(), srcs, lnd, send_sems, recv_sems):
            cp.start()
        token[...] = jnp.zeros_like(token)

    outs = pl.pallas_call(
        body, name=name,
        in_specs=[HBM] * (2 * n),
        out_specs=[SEMS, SEMS] + [HBM] * (2 * n) + [pl.BlockSpec(memory_space=pltpu.VMEM)],
        out_shape=[pltpu.SemaphoreType.DMA((n * (N_DEV - 1),)), pltpu.SemaphoreType.DMA((n * (N_DEV - 1),))]
        + [pltpu.HBM(a.shape, a.dtype) for a in arrays] + [pltpu.HBM(l.shape, l.dtype) for l in lands]
        + [_sds((8, 128), F32)],
        input_output_aliases={i: 2 + i for i in range(2 * n)},
        compiler_params=pltpu.CompilerParams(has_side_effects=SIDE_EFFECT),
    )(*[hbm(a) for a in arrays], *lands)
    return outs[0], outs[1], outs[2:2 + n], outs[2 + n:2 + 2 * n], outs[-1]


def _rs_wait(name, send_sems, recv_sems, srcs, lands, after):
    n = len(srcs)

    def body(*refs):
        src_refs, lnd = refs[:n], refs[n:2 * n]
        send, recv = refs[2 * n], refs[2 * n + 1]
        for cp in _rs_copies(*_my_place(), src_refs, lnd, send, recv):
            cp.wait_send()
            cp.wait_recv()

    outs = pl.pallas_call(
        body, name=name,
        in_specs=[HBM] * (2 * n) + [SEMS, SEMS] + [ANY] * len(after),
        out_specs=[HBM] * (2 * n),
        out_shape=[pltpu.HBM(a.shape, a.dtype) for a in list(srcs) + list(lands)],
        input_output_aliases={i: i for i in range(2 * n)},
        compiler_params=pltpu.CompilerParams(has_side_effects=SIDE_EFFECT),
    )(*srcs, *lands, send_sems, recv_sems, *after)
    return outs[:n], outs[n:]


SMALL_PACK_ROWS = 16


def _all_reduce_small(rows, deps=()):
    n = len(rows)

    def body(*refs):
        ins = refs[:n]
        out_ref, mine, buf, send_sems, recv_sems = refs[n + len(deps):]
        x, y, c = _my_place()
        mine[...] = jnp.zeros_like(mine)
        for (r0, a), ref in zip(rows, ins):
            mine[r0:r0 + a.shape[0], 0:a.shape[1]] = ref[...]
        buf[_flat(x, y, c)] = mine[...]
        copies = []
        for r in range(1, N_DEV):
            peer = _peer(x, y, c, r)
            send = pltpu.make_async_remote_copy(
                src_ref=mine, dst_ref=buf.at[_flat(x, y, c)],
                send_sem=send_sems.at[r - 1], recv_sem=recv_sems.at[r - 1], device_id=peer, device_id_type=MESH)
            send.start()
            recv = pltpu.make_async_remote_copy(
                src_ref=mine, dst_ref=buf.at[_flat(*peer)],
                send_sem=send_sems.at[r - 1], recv_sem=recv_sems.at[r - 1], device_id=peer, device_id_type=MESH)
            copies.append((send, recv))
        for send, recv in copies:
            send.wait_send()
            recv.wait_recv()
        acc = buf[0]
        for s in range(1, N_DEV):
            acc = acc + buf[s]
        out_ref[...] = acc

    vm = pl.BlockSpec(memory_space=pltpu.VMEM)
    shape = (SMALL_PACK_ROWS, D_MODEL)
    return pl.pallas_call(
        body, name="all_reduce_small", in_specs=[vm] * n + [ANY] * len(deps), out_specs=vm,
        out_shape=_sds(shape, F32),
        scratch_shapes=[pltpu.VMEM(shape, F32), pltpu.VMEM((N_DEV,) + shape, F32),
                        pltpu.SemaphoreType.DMA((7,)), pltpu.SemaphoreType.DMA((7,))],
    )(*[a for _, a in rows], *deps)


def _adamw_math(w, g, m, v):
    m = ADAM_B1 * m + (1.0 - ADAM_B1) * g
    v = ADAM_B2 * v + (1.0 - ADAM_B2) * (g * g)
    m_hat = m / (1.0 - ADAM_B1 ** ADAM_STEP)
    v_hat = v / (1.0 - ADAM_B2 ** ADAM_STEP)
    delta = -ADAM_LR * (m_hat / (jnp.sqrt(v_hat) + ADAM_EPS) + ADAM_WD * w)
    return delta, m, v


def _adamw_big(name, w, m, v, srcs, lands, me):
    nl, rows, cols = w.shape
    tr = next(cand for cand in (256, 128, 64, 32, 16, 8) if rows % cand == 0)

    def body(me_ref, w_ref, m_ref, v_ref, *rest):
        src_refs, land_refs = rest[:nl], rest[nl:2 * nl]
        g_ref, d_ref, mo_ref, vo_ref = rest[2 * nl:]
        for layer in range(nl):
            @pl.when(pl.program_id(0) == layer)
            def _():
                g = src_refs[layer][...].astype(F32)
                for s in range(N_DEV - 1):
                    g = g + land_refs[layer][s].astype(F32)
                delta, mn, vn = _adamw_math(w_ref[...], g, m_ref[...], v_ref[...])
                g_ref[...] = g
                d_ref[...] = delta
                mo_ref[...] = mn
                vo_ref[...] = vn

    blk = pl.BlockSpec((None, tr, cols), lambda l, i, me_ref: (l, i, 0))
    own = pl.BlockSpec((None, tr, cols), lambda l, i, me_ref: (me_ref[0], i, 0))
    peers = pl.BlockSpec((N_DEV - 1, tr, cols), lambda l, i, me_ref: (0, i, 0))
    return pl.pallas_call(
        body, name=name,
        grid_spec=pltpu.PrefetchScalarGridSpec(
            num_scalar_prefetch=1, grid=(nl, rows // tr),
            in_specs=[blk, blk, blk] + [own] * nl + [peers] * nl, out_specs=[blk] * 4),
        out_shape=[_sds((nl, rows, cols), F32)] * 4,
        compiler_params=_cparams(("arbitrary", "arbitrary")),
    )(me, w, m, v, *srcs, *lands)


def _adamw_small(ws, gs, ms, vs):
    n = len(ws)

    def body(*refs):
        w_refs, g_refs, m_refs, v_refs = (refs[i * n:(i + 1) * n] for i in range(4))
        d_out, m_out, v_out = (refs[(4 + i) * n:(5 + i) * n] for i in range(3))
        for i in range(n):
            delta, mn, vn = _adamw_math(w_refs[i][...], g_refs[i][...], m_refs[i][...], v_refs[i][...])
            d_out[i][...] = delta
            m_out[i][...] = mn
            v_out[i][...] = vn

    vm = pl.BlockSpec(memory_space=pltpu.VMEM)
    outs = pl.pallas_call(
        body, name="adamw_small", in_specs=[vm] * (4 * n), out_specs=[vm] * (3 * n),
        out_shape=[_sds(a.shape, F32) for a in ws] * 3,
    )(*ws, *gs, *ms, *vs)
    return outs[:n], outs[n:2 * n], outs[2 * n:]


SMALL_ROWS = 16


def _pad_to(a, rows, cols):
    return jnp.pad(a, ((0, rows - a.shape[0]), (0, cols - a.shape[1])))


def _place_own(blocks):
    me = _flat(*_my_place())
    return [lax.dynamic_update_slice(lax.empty((N_DEV,) + b.shape, b.dtype), b[None], (me,) + (0,) * b.ndim)
            for b in blocks]


def _ag_copies(x, y, c, blocks, bufs, send_sems, recv_sems):
    sends, recvs = [], []
    for a in range(len(blocks)):
        for r in range(1, N_DEV):
            peer = _peer(x, y, c, r)
            k = a * (N_DEV - 1) + r - 1
            make = lambda place: pltpu.make_async_remote_copy(
                src_ref=blocks[a], dst_ref=bufs[a].at[_flat(*place)],
                send_sem=send_sems.at[k], recv_sem=recv_sems.at[k], device_id=peer, device_id_type=MESH)
            sends.append(make((x, y, c)))
            recvs.append(make(peer))
    return sends, recvs


def _ag_start(groups, after):
    flat = [pair for g in groups for pair in g]
    n, ng = len(flat), len(groups)
    hbm = lambda a: pltpu.with_memory_space_constraint(a, pltpu.HBM)

    def body(*refs):
        blocks, bufs = refs[:n], refs[n:2 * n]
        sems = refs[2 * n + len(after):2 * n + len(after) + 2 * ng]
        x, y, c = _my_place()
        at = 0
        for gi, g in enumerate(groups):
            sends, _ = _ag_copies(x, y, c, blocks[at:at + len(g)], bufs[at:at + len(g)], sems[2 * gi], sems[2 * gi + 1])
            for cp in sends:
                cp.start()
            at += len(g)
        refs[-1][...] = jnp.zeros_like(refs[-1])

    sem_shapes = [pltpu.SemaphoreType.DMA((len(g) * (N_DEV - 1),)) for g in groups for _ in range(2)]
    outs = pl.pallas_call(
        body, name="gather_start",
        in_specs=[HBM] * (2 * n) + [ANY] * len(after),
        out_specs=[SEMS] * (2 * ng) + [HBM] * (2 * n) + [pl.BlockSpec(memory_space=pltpu.VMEM)],
        out_shape=sem_shapes + [pltpu.HBM(b.shape, b.dtype) for b, _ in flat]
        + [pltpu.HBM(u.shape, u.dtype) for _, u in flat] + [_sds((8, 128), F32)],
        input_output_aliases={i: 2 * ng + i for i in range(2 * n)},
        compiler_params=pltpu.CompilerParams(has_side_effects=SIDE_EFFECT),
    )(*[hbm(b) for b, _ in flat], *[hbm(u) for _, u in flat], *after)
    blocks_thru, bufs_thru = outs[2 * ng:2 * ng + n], outs[2 * ng + n:2 * ng + 2 * n]
    started, at = [], 0
    for gi, g in enumerate(groups):
        started.append((outs[2 * gi], outs[2 * gi + 1], blocks_thru[at:at + len(g)], bufs_thru[at:at + len(g)]))
        at += len(g)
    return started, outs[-1]


def _ag_wait(name, send_sems, recv_sems, blocks, bufs, after):
    n = len(blocks)

    def body(*refs):
        sends, recvs = _ag_copies(*_my_place(), refs[:n], refs[n:2 * n], refs[2 * n], refs[2 * n + 1])
        for s, r in zip(sends, recvs):
            s.wait_send()
            r.wait_recv()

    outs = pl.pallas_call(
        body, name=name,
        in_specs=[HBM] * (2 * n) + [SEMS, SEMS] + [ANY] * len(after),
        out_specs=[HBM] * (2 * n),
        out_shape=[pltpu.HBM(a.shape, a.dtype) for a in list(blocks) + list(bufs)],
        input_output_aliases={i: i for i in range(2 * n)},
        compiler_params=pltpu.CompilerParams(has_side_effects=SIDE_EFFECT),
    )(*blocks, *bufs, send_sems, recv_sems, *after)
    return outs[n:]


def _prepare_weights(p):
    n = N_DEV
    bf = lambda a: a.astype(BF16)
    gn_pack = jnp.concatenate([
        _pad_to(p['ret_gn'][0], RET_HEADS, 128), _pad_to(p['mla_q_a_norm'], 1, 128),
        _pad_to(p['mla_kv_a_norm'], 1, 128), jnp.zeros((2, 128), F32)], axis=0)
    ple = lambda l: [bf(p['ple_gate_w'][l]), bf(p['ple_proj_w'][l])]
    names = ('ret_out', 'mlp_w1_0', 'mlp_w2_0', 'ple_0', 'mla', 'layer_1')
    later = [[bf(p['ret_w_out'][0])], [bf(p['mlp_w1'][0])], [bf(p['mlp_w2'][0])], ple(0),
             [bf(p['mla_w_in'][0]), bf(p['mla_w_uq'][0]), bf(p['mla_w_ukv'][0]), bf(p['mla_w_out'][0])],
             [bf(p['mlp_w1'][1]), bf(p['mlp_w2'][1])] + ple(1)]
    bufs = _place_own([b for g in later for b in g])
    pack, wri = _all_gather([gn_pack, bf(p['ret_w_in'][0])])
    groups, at = [], 0
    for g in later:
        groups.append(list(zip(g, bufs[at:at + len(g)])))
        at += len(g)
    started, token = _ag_start(groups, (wri,))

    w = {k: p[k] for k in ('mix_norm', 'mlp_norm', 'ple_norm')}
    w['ret_gn'] = pack[:, :RET_HEADS, :RET_DV // n].transpose(1, 0, 2).reshape(RET_HEADS, RET_DV)
    w['mla_q_a_norm'] = pack[:, RET_HEADS, :MLA_Q_RANK // n].reshape(1, MLA_Q_RANK)
    w['mla_kv_a_norm'] = pack[:, RET_HEADS + 1, :MLA_KV_RANK // n].reshape(1, MLA_KV_RANK)
    w['ret_w_in'] = wri
    w['mla_q_norm'] = _pad_to(p['mla_q_norm'], 1, MLA_HD_PAD)
    w['mla_k_norm'] = _pad_to(p['mla_k_norm'], 1, MLA_HD_PAD)
    w['deps'] = (token,)

    def fetch(name, after):
        got = list(_ag_wait("gather_wait_" + name, *started[names.index(name)], after))
        if name == 'ret_out':
            return dict(ret_w_out=got[0].reshape(RET_V_W, D_MODEL))
        if name == 'mla':
            wmi, wuq, wukv, wmo = got
            return dict(mla_w_in=jnp.pad(wmi.reshape(D_MODEL, MLA_IN), ((0, 0), (0, MLA_IN_PAD - MLA_IN))),
                        mla_w_uq=jnp.pad(wuq, ((0, 0), (0, 0), (0, MLA_HD_PAD - MLA_QKD))),
                        mla_w_ukv=wukv, mla_w_out=wmo.reshape(D_MODEL, D_MODEL))
        out = {}
        if name in ('mlp_w1_0', 'layer_1'):
            out['mlp_w1'] = got.pop(0)
        if name in ('mlp_w2_0', 'layer_1'):
            out['mlp_w2'] = got.pop(0)
        if name in ('ple_0', 'layer_1'):
            out['ple_gate_w'] = got[0].reshape(D_MODEL, D_MODEL)
            out['ple_proj_w'] = got[1].transpose(1, 0, 2).reshape(PLE_DIM, D_MODEL)
        return out

    return w, fetch


def _small_grads(small, after):
    rows = [(0, small['mix_norm'][0]), (1, small['mix_norm'][1]), (2, small['mlp_norm'][0]),
            (3, small['mlp_norm'][1]), (4, small['ple_norm'][0]), (5, small['ple_norm'][1]),
            (6, small['ret_gn']), (10, small['mla_q_a_norm']), (11, small['mla_kv_a_norm']),
            (12, small['mla_q_norm']), (13, small['mla_k_norm'])]
    gs = _all_reduce_small(rows, after)
    me = _flat(*_my_place())
    n = N_DEV
    return dict(
        mix_norm=gs[0:2], mlp_norm=gs[2:4], ple_norm=gs[4:6],
        ret_gn=lax.dynamic_slice(gs, (6, me * (RET_DV // n)), (RET_HEADS, RET_DV // n)),
        mla_q_a_norm=lax.dynamic_slice(gs, (10, me * (MLA_Q_RANK // n)), (1, MLA_Q_RANK // n)),
        mla_kv_a_norm=lax.dynamic_slice(gs, (11, me * (MLA_KV_RANK // n)), (1, MLA_KV_RANK // n)),
        mla_q_norm=gs[12:13, :MLA_QKD], mla_k_norm=gs[13:14, :MLA_QKD])


def kernel(x, p, mix_norm, ret_w_in, ret_gn, ret_w_out, mla_w_in, mla_q_a_norm, mla_kv_a_norm, mla_w_uq, mla_w_ukv, mla_q_norm, mla_k_norm, mla_w_out, mlp_norm, mlp_w1, mlp_w2, ple_norm, ple_gate_w, ple_proj_w, loss_target, m_mix_norm, m_ret_w_in, m_ret_gn, m_ret_w_out, m_mla_w_in, m_mla_q_a_norm, m_mla_kv_a_norm, m_mla_w_uq, m_mla_w_ukv, m_mla_q_norm, m_mla_k_norm, m_mla_w_out, m_mlp_norm, m_mlp_w1, m_mlp_w2, m_ple_norm, m_ple_gate_w, m_ple_proj_w, v_mix_norm, v_ret_w_in, v_ret_gn, v_ret_w_out, v_mla_w_in, v_mla_q_a_norm, v_mla_kv_a_norm, v_mla_w_uq, v_mla_w_ukv, v_mla_q_norm, v_mla_k_norm, v_mla_w_out, v_mlp_norm, v_mlp_w1, v_mlp_w2, v_ple_norm, v_ple_gate_w, v_ple_proj_w):
    given = dict(locals())
    params = {n: given[n] for n in WEIGHTS}
    w, fetch = _prepare_weights(params)

    started = []

    def emit(group):
        keys = list(group)
        send, recv, srcs, lands, token = _rs_start(f"rs_start{len(started)}", [group[k] for k in keys])
        started.append((keys, send, recv, srcs, lands))
        return (token,)

    sq_err, grad_x, _, small = _local_step(x[0], p, loss_target[0], w, fetch, emit)
    loss = lax.psum(0.5 / D_MODEL * sq_err[0, 0], ("x", "y", "c"))

    grads, deltas, new_m, new_v = {}, {}, {}, {}

    def small_updates(after):
        sg = _small_grads(small, after)
        two_d = lambda a: a.reshape(-1, a.shape[-1])
        d_s, m_s, v_s = _adamw_small(
            [two_d(params[n]) for n in SMALL], [sg[n] for n in SMALL],
            [two_d(given["m_" + n]) for n in SMALL], [two_d(given["v_" + n]) for n in SMALL])
        for i, n in enumerate(SMALL):
            shape = params[n].shape
            grads[n], deltas[n], new_m[n], new_v[n] = (a.reshape(shape) for a in (sg[n], d_s[i], m_s[i], v_s[i]))
        return (d_s[0],)

    me = _flat(*_my_place()).astype(jnp.int32).reshape(1)
    after = (grad_x,)
    src_of, land_of = {}, {}
    for gi, (keys, send, recv, srcs, lands) in enumerate(started):
        if gi == len(started) - 1:
            after = small_updates(after)
        srcs, lands = _rs_wait(f"rs_wait{gi}", send, recv, srcs, lands, after)
        for k, s, l in zip(keys, srcs, lands):
            src_of[k], land_of[k] = s, l
        done = [n for n in BIG if n not in grads and all((n, l) in src_of for l in range(params[n].shape[0]))]
        for n in done:
            layers = range(params[n].shape[0])
            grads[n], deltas[n], new_m[n], new_v[n] = _adamw_big(
                "adamw_" + n, params[n], given["m_" + n], given["v_" + n],
                [src_of[(n, l)] for l in layers], [land_of[(n, l)] for l in layers], me)
        if done:
            after = (deltas[done[-1]],)

    return (loss, grad_x[None], *[grads[n] for n in WEIGHTS], *[deltas[n] for n in WEIGHTS],
            *[new_m[n] for n in WEIGHTS], *[new_v[n] for n in WEIGHTS])
```

```python
import functools
import math

import jax
import jax.numpy as jnp
from jax import lax
from jax.experimental import pallas as pl
from jax.experimental.pallas import tpu as pltpu

F32 = jnp.float32
BF16 = jnp.bfloat16
MESH = pl.DeviceIdType.MESH
ANY = pl.BlockSpec(memory_space=pl.ANY)

N_DEV = 8
D_MODEL = 1024
CHUNK = 64
EPS = 1e-6
ROPE_THETA = 10000.0
RET_HEADS = 4
RET_DK = 256
RET_DV = 512
RET_QK_W = RET_HEADS * RET_DK
RET_V_W = RET_HEADS * RET_DV
RET_IN = 2 * RET_QK_W + 2 * RET_V_W
MLA_HEADS = 8
MLA_NOPE = 128
MLA_ROPE = 64
MLA_QKD = MLA_NOPE + MLA_ROPE
MLA_VD = 128
MLA_Q_RANK = 384
MLA_KV_RANK = 256
MLA_IN = MLA_Q_RANK + MLA_KV_RANK + MLA_ROPE
MLA_IN_PAD = 768
MLA_HD_PAD = 256
D_FF = 4096
PLE_DIM = 256
ATT_SCALE = MLA_QKD ** -0.5
LOG2E = 1.4426950408889634
ATT_EXP2 = ATT_SCALE * LOG2E

ADAM_LR = 0.001
ADAM_B1 = 0.9
ADAM_B2 = 0.999
ADAM_EPS = 1e-08
ADAM_WD = 0.01
ADAM_STEP = 10

VMEM_LIMIT = 52 * 1024 * 1024
ROW_TILE = 1024
RET_ROWS = 256
ATT_BLOCK = 256
ATT_QROWS = 512
ATT_HEADS = 2

WEIGHTS = ['mix_norm', 'ret_w_in', 'ret_gn', 'ret_w_out', 'mla_w_in', 'mla_q_a_norm', 'mla_kv_a_norm',
           'mla_w_uq', 'mla_w_ukv', 'mla_q_norm', 'mla_k_norm', 'mla_w_out', 'mlp_norm', 'mlp_w1', 'mlp_w2',
           'ple_norm', 'ple_gate_w', 'ple_proj_w']
BIG = ['ret_w_in', 'ret_w_out', 'mla_w_in', 'mla_w_uq', 'mla_w_ukv', 'mla_w_out', 'mlp_w1', 'mlp_w2',
       'ple_gate_w', 'ple_proj_w']
SMALL = [w for w in WEIGHTS if w not in BIG]


def _cparams(sem=None):
    return pltpu.CompilerParams(dimension_semantics=sem, vmem_limit_bytes=VMEM_LIMIT)


def _dot(a, b, ca, cb):
    return lax.dot_general(a, b, (((ca,), (cb,)), ((), ())), preferred_element_type=F32)


def _bf(v):
    return v if v.dtype == BF16 else v.astype(BF16)


def _sigmoid(z):
    return 1.0 / (1.0 + jnp.exp(-z))


def _mm(name, grid, a, a_spec, b, b_spec, contract, outs, extras=(), epi=None, deps=(), split=None):
    nk = grid[2]
    n_ex, n_out, n_dep = len(extras), len(outs), len(deps)
    acc_shape = tuple(d for d in outs[0][1].block_shape if d is not None)
    if split is not None:
        acc_shape = (acc_shape[1], acc_shape[0] * split)

    def body(*refs):
        a_ref, b_ref = refs[:2]
        ex_refs = refs[2:2 + n_ex]
        out_refs = refs[2 + n_ex + n_dep:2 + n_ex + n_dep + n_out]

        def product():
            return _dot(_bf(a_ref[...]), _bf(b_ref[...]), contract[0], contract[1])

        def finish(acc):
            if split is not None:
                for j in range(acc_shape[1] // split):
                    out_refs[0][j] = acc[:, j * split:(j + 1) * split].astype(out_refs[0].dtype)
                return
            acc = acc[...]
            res = epi(acc, *[r[...] for r in ex_refs]) if epi is not None else (acc,)
            for o, r in zip(out_refs, res):
                o[...] = r.astype(o.dtype)

        if nk == 1:
            finish(product())
        else:
            acc_ref = refs[-1]
            k = pl.program_id(2)

            @pl.when(k == 0)
            def _():
                acc_ref[...] = jnp.zeros_like(acc_ref)

            acc_ref[...] += product()

            @pl.when(k == nk - 1)
            def _():
                finish(acc_ref)

    return pl.pallas_call(
        body, name=name, grid=grid,
        in_specs=[a_spec, b_spec] + [s for _, s in extras] + [ANY] * n_dep,
        out_specs=[s for _, s in outs],
        out_shape=[s for s, _ in outs],
        scratch_shapes=[pltpu.VMEM(acc_shape, F32)] if nk > 1 else [],
        compiler_params=_cparams(("parallel", "parallel", "arbitrary")),
    )(a, b, *[x for x, _ in extras], *deps)


def _sds(shape, dtype):
    return jax.ShapeDtypeStruct(shape, dtype)


def _row_tile(t, cap=ROW_TILE):
    return min(cap, t)


def _rms_fwd(name, x, g):
    t, d = x.shape
    tm = _row_tile(t)

    def body(x_ref, g_ref, o_ref):
        xv = x_ref[...]
        r = lax.rsqrt(jnp.mean(xv * xv, axis=-1, keepdims=True) + EPS)
        o_ref[...] = (xv * r * g_ref[...]).astype(o_ref.dtype)

    return pl.pallas_call(
        body, name=name, grid=(t // tm,),
        in_specs=[pl.BlockSpec((tm, d), lambda i: (i, 0)), pl.BlockSpec((1, d), lambda i: (0, 0))],
        out_specs=pl.BlockSpec((tm, d), lambda i: (i, 0)),
        out_shape=_sds((t, d), BF16),
        compiler_params=_cparams(("parallel",)),
    )(x, g)


def _rms_bwd_rows(dy, xv, g, n):
    r = lax.rsqrt(jnp.sum(xv * xv, axis=-1, keepdims=True) / n + EPS)
    xh = xv * r
    dxh = dy * g
    dx = r * (dxh - xh * (jnp.sum(dxh * xh, axis=-1, keepdims=True) / n))
    return dx, dy * xh


def _rms_bwd(name, dy, x, g, res):
    t, d = x.shape
    tm = _row_tile(t, 512)

    def body(dy_ref, x_ref, g_ref, res_ref, dx_ref, dg_ref):
        @pl.when(pl.program_id(0) == 0)
        def _():
            dg_ref[...] = jnp.zeros_like(dg_ref)

        dx, dgr = _rms_bwd_rows(dy_ref[...], x_ref[...], g_ref[...], d)
        dx_ref[...] = res_ref[...] + dx
        dg_ref[...] += jnp.sum(dgr, axis=0, keepdims=True)

    row = pl.BlockSpec((tm, d), lambda i: (i, 0))
    vec = pl.BlockSpec((1, d), lambda i: (0, 0))
    return pl.pallas_call(
        body, name=name, grid=(t // tm,),
        in_specs=[row, row, vec, row], out_specs=[row, vec],
        out_shape=[_sds((t, d), F32), _sds((1, d), F32)],
        compiler_params=_cparams(("arbitrary",)),
    )(dy, x, g, res)


def _loss_head(y, target):
    t, d = y.shape
    tm = _row_tile(t)

    def body(y_ref, t_ref, dy_ref, l_ref):
        @pl.when(pl.program_id(0) == 0)
        def _():
            l_ref[...] = jnp.zeros_like(l_ref)

        e = y_ref[...] - t_ref[...]
        dy_ref[...] = e / d
        l_ref[...] += jnp.sum(jnp.sum(e * e, axis=-1, keepdims=True), axis=0, keepdims=True)

    row = pl.BlockSpec((tm, d), lambda i: (i, 0))
    return pl.pallas_call(
        body, name="loss_head", grid=(t // tm,),
        in_specs=[row, row], out_specs=[row, pl.BlockSpec((8, 128), lambda i: (0, 0))],
        out_shape=[_sds((t, d), F32), _sds((8, 128), F32)],
        compiler_params=_cparams(("arbitrary",)),
    )(y, target)


def _ple_gate_bwd(name, dh, gate, e):
    t, d = dh.shape
    tm = _row_tile(t)

    def body(dh_ref, g_ref, e_ref, de_ref, dz_ref):
        dh_v, gt = dh_ref[...], g_ref[...]
        de_ref[...] = (dh_v * gt).astype(BF16)
        dz_ref[...] = (dh_v * e_ref[...] * (gt * (1.0 - gt))).astype(BF16)

    row = pl.BlockSpec((tm, d), lambda i: (i, 0))
    return pl.pallas_call(
        body, name=name, grid=(t // tm,), in_specs=[row, row, row], out_specs=[row, row],
        out_shape=[_sds((t, d), BF16), _sds((t, d), BF16)],
        compiler_params=_cparams(("parallel",)),
    )(dh, gate, e)


def _rope_half(v, cos, sin):
    half = v.shape[-1] // 2
    v1, v2 = v[:, :half], v[:, half:]
    return jnp.concatenate([v1 * cos - v2 * sin, v2 * cos + v1 * sin], axis=-1)


def _ret_consts():
    lg = jnp.log(1.0 - 2.0 ** (-5.0 - jnp.arange(RET_HEADS, dtype=F32)))
    idx = jnp.arange(CHUNK, dtype=F32)
    intra = jnp.exp(lg[:, None, None] * jnp.abs(idx[:, None] - idx[None, :]))
    qdec = jnp.exp(lg[:, None] * (idx + 1.0))
    kdec = jnp.exp(lg[:, None] * (CHUNK - 1.0 - idx))
    cdec = jnp.exp(lg * CHUNK)
    qdec = jnp.broadcast_to(qdec[:, :, None], (RET_HEADS, CHUNK, RET_DK))
    kdec = jnp.broadcast_to(kdec[:, :, None], (RET_HEADS, CHUNK, RET_DK))
    cdec = jnp.broadcast_to(cdec[:, None, None], (RET_HEADS, 1, RET_DV))
    return intra, qdec, kdec, cdec


def _ret_specs(rb, rev_nb=None):
    blk = (lambda i: i) if rev_nb is None else (lambda i: rev_nb - 1 - i)
    full = lambda shape: pl.BlockSpec(shape, lambda i: (0,) * len(shape))
    return dict(
        proj=pl.BlockSpec((rb, RET_IN), lambda i: (blk(i), 0)),
        tab=pl.BlockSpec((rb, RET_DK // 2), lambda i: (blk(i), 0)),
        vw=pl.BlockSpec((rb, RET_V_W), lambda i: (blk(i), 0)),
        st=pl.BlockSpec((rb // CHUNK, RET_HEADS, RET_DK, RET_DV), lambda i: (blk(i), 0, 0, 0)),
        gn=full((RET_HEADS, 1, RET_DV)),
        intra=full((RET_HEADS, CHUNK, CHUNK)),
        dec=full((RET_HEADS, CHUNK, RET_DK)),
        cdec=full((RET_HEADS, 1, RET_DV)),
    )


def _ret_fwd(proj, cos, sin, gn):
    t = proj.shape[0]
    rb = min(RET_ROWS, t)
    cpb = rb // CHUNK
    intra, qdec, kdec, cdec = _ret_consts()
    sp = _ret_specs(rb)

    def body(proj_ref, cos_ref, sin_ref, gn_ref, intra_ref, qd_ref, kd_ref, cd_ref,
             gated_ref, outp_ref, st_ref, s_ref):
        @pl.when(pl.program_id(0) == 0)
        def _():
            s_ref[...] = jnp.zeros_like(s_ref)

        def chunk(c, carry):
            rows = pl.ds(pl.multiple_of(c * CHUNK, CHUNK), CHUNK)
            cs, sn = cos_ref[rows, :], sin_ref[rows, :]
            for h in range(RET_HEADS):
                q = proj_ref[rows, h * RET_DK:(h + 1) * RET_DK]
                k = proj_ref[rows, RET_QK_W + h * RET_DK:RET_QK_W + (h + 1) * RET_DK]
                v = proj_ref[rows, 2 * RET_QK_W + h * RET_DV:2 * RET_QK_W + (h + 1) * RET_DV]
                g = proj_ref[rows, 2 * RET_QK_W + RET_V_W + h * RET_DV:2 * RET_QK_W + RET_V_W + (h + 1) * RET_DV]
                qr = _rope_half(q, cs, sn)
                kr = _rope_half(k, cs, sn) * (RET_DK ** -0.5)
                qb, kb, vb = qr.astype(BF16), kr.astype(BF16), v.astype(BF16)
                sc = _dot(qb, kb, 1, 1) * intra_ref[h]
                inner = _dot(sc.astype(BF16), vb, 1, 0)
                s_old = s_ref[h]
                sb = s_old.astype(BF16)
                st_ref[c, h] = sb
                cross = _dot((qr * qd_ref[h]).astype(BF16), sb, 1, 0)
                out = inner + cross
                s_ref[h] = s_old * cd_ref[h] + _dot((kr * kd_ref[h]).astype(BF16), vb, 0, 0)
                r = lax.rsqrt(jnp.mean(out * out, axis=-1, keepdims=True) + EPS)
                y = out * r * gn_ref[h]
                cols = slice(h * RET_DV, (h + 1) * RET_DV)
                gated_ref[rows, cols] = (g * _sigmoid(g) * y).astype(BF16)
                outp_ref[rows, cols] = out
            return carry

        lax.fori_loop(0, cpb, chunk, 0)

    return pl.pallas_call(
        body, name="ret_fwd", grid=(t // rb,),
        in_specs=[sp['proj'], sp['tab'], sp['tab'], sp['gn'], sp['intra'], sp['dec'], sp['dec'], sp['cdec']],
        out_specs=[sp['vw'], sp['vw'], sp['st']],
        out_shape=[_sds((t, RET_V_W), BF16), _sds((t, RET_V_W), F32),
                   _sds((t // CHUNK, RET_HEADS, RET_DK, RET_DV), BF16)],
        scratch_shapes=[pltpu.VMEM((RET_HEADS, RET_DK, RET_DV), F32)],
        compiler_params=_cparams(("arbitrary",)),
    )(proj, cos, sin, gn.reshape(RET_HEADS, 1, RET_DV), intra, qdec, kdec, cdec)


def _ret_bwd(proj, cos, sin, gn, outp, states, dgated):
    t = proj.shape[0]
    rb = min(RET_ROWS, t)
    cpb = rb // CHUNK
    nb = t // rb
    intra, qdec, kdec, cdec = _ret_consts()
    sp = _ret_specs(rb, rev_nb=nb)

    def body(proj_ref, cos_ref, sin_ref, gn_ref, intra_ref, qd_ref, kd_ref, cd_ref, outp_ref, st_ref, dgt_ref,
             dproj_ref, dgn_ref, ds_ref):
        @pl.when(pl.program_id(0) == 0)
        def _():
            ds_ref[...] = jnp.zeros_like(ds_ref)
            dgn_ref[...] = jnp.zeros_like(dgn_ref)

        def chunk(cc, carry):
            c = cpb - 1 - cc
            rows = pl.ds(pl.multiple_of(c * CHUNK, CHUNK), CHUNK)
            cs, sn = cos_ref[rows, :], sin_ref[rows, :]
            for h in range(RET_HEADS):
                q = proj_ref[rows, h * RET_DK:(h + 1) * RET_DK]
                k = proj_ref[rows, RET_QK_W + h * RET_DK:RET_QK_W + (h + 1) * RET_DK]
                v = proj_ref[rows, 2 * RET_QK_W + h * RET_DV:2 * RET_QK_W + (h + 1) * RET_DV]
                g = proj_ref[rows, 2 * RET_QK_W + RET_V_W + h * RET_DV:2 * RET_QK_W + RET_V_W + (h + 1) * RET_DV]
                cols = slice(h * RET_DV, (h + 1) * RET_DV)
                qr = _rope_half(q, cs, sn)
                kr = _rope_half(k, cs, sn) * (RET_DK ** -0.5)
                qb, kb, vb = qr.astype(BF16), kr.astype(BF16), v.astype(BF16)
                qdb = (qr * qd_ref[h]).astype(BF16)
                kdb = (kr * kd_ref[h]).astype(BF16)
                out = outp_ref[rows, cols]
                dgt = dgt_ref[rows, cols]
                gnh = gn_ref[h]
                r = lax.rsqrt(jnp.mean(out * out, axis=-1, keepdims=True) + EPS)
                xh = out * r
                sg = _sigmoid(g)
                dgate = dgt * (xh * gnh) * (sg * (1.0 + g * (1.0 - sg)))
                dy = dgt * (g * sg)
                dgn_ref[h] += jnp.sum(dy * xh, axis=0, keepdims=True)
                dxh = dy * gnh
                dout = r * (dxh - xh * jnp.mean(dxh * xh, axis=-1, keepdims=True))
                doutb = dout.astype(BF16)
                itr = intra_ref[h]
                pb = (_dot(qb, kb, 1, 1) * itr).astype(BF16)
                dv = _dot(pb, doutb, 0, 0)
                dsc = (_dot(doutb, vb, 1, 1) * itr).astype(BF16)
                dq = _dot(dsc, kb, 1, 0)
                dk = _dot(dsc, qb, 0, 0)
                dq = dq + _dot(doutb, st_ref[c, h], 1, 1) * qd_ref[h]
                ds_new = ds_ref[h]
                dsb = ds_new.astype(BF16)
                dk = dk + _dot(vb, dsb, 1, 1) * kd_ref[h]
                dv = dv + _dot(kdb, dsb, 1, 0)
                ds_ref[h] = ds_new * cd_ref[h] + _dot(qdb, doutb, 0, 0)
                dproj_ref[rows, h * RET_DK:(h + 1) * RET_DK] = _rope_half(dq, cs, -sn).astype(BF16)
                dproj_ref[rows, RET_QK_W + h * RET_DK:RET_QK_W + (h + 1) * RET_DK] = (
                    _rope_half(dk * (RET_DK ** -0.5), cs, -sn).astype(BF16))
                dproj_ref[rows, 2 * RET_QK_W + h * RET_DV:2 * RET_QK_W + (h + 1) * RET_DV] = dv.astype(BF16)
                dproj_ref[rows, 2 * RET_QK_W + RET_V_W + h * RET_DV:
                          2 * RET_QK_W + RET_V_W + (h + 1) * RET_DV] = dgate.astype(BF16)
            return carry

        lax.fori_loop(0, cpb, chunk, 0)

    return pl.pallas_call(
        body, name="ret_bwd", grid=(nb,),
        in_specs=[sp['proj'], sp['tab'], sp['tab'], sp['gn'], sp['intra'], sp['dec'], sp['dec'], sp['cdec'],
                  sp['vw'], sp['st'], sp['vw']],
        out_specs=[sp['proj'], sp['gn']],
        out_shape=[_sds((t, RET_IN), BF16), _sds((RET_HEADS, 1, RET_DV), F32)],
        scratch_shapes=[pltpu.VMEM((RET_HEADS, RET_DK, RET_DV), F32)],
        compiler_params=_cparams(("arbitrary",)),
    )(proj, cos, sin, gn.reshape(RET_HEADS, 1, RET_DV), intra, qdec, kdec, cdec, outp, states, dgated)


def _mla_tables(t):
    half = MLA_ROPE // 2
    inv = 1.0 / (ROPE_THETA ** (jnp.arange(0, MLA_ROPE, 2, dtype=F32) / MLA_ROPE))
    ang = jnp.arange(t, dtype=F32)[:, None] * inv[None, :]
    cos, sin = jnp.cos(ang), jnp.sin(ang)
    z = jnp.zeros((t, half), F32)
    c = jnp.concatenate([cos, cos, z, z], axis=1)
    s1 = jnp.concatenate([-sin, z, z, z], axis=1)
    s2 = jnp.concatenate([z, sin, z, z], axis=1)
    return c, s1, s2


def _rope_tile(r, c, s1, s2):
    return r * c + pltpu.roll(r, 96, 1) * s1 + pltpu.roll(r, 32, 1) * s2


def _mla_mid(proj2, qa, kva):
    t = proj2.shape[0]
    tm = _row_tile(t)

    def body(p_ref, qa_ref, kva_ref, cq_ref, ckv_ref):
        cq = p_ref[:, :MLA_Q_RANK]
        ckv = p_ref[:, MLA_Q_RANK:MLA_Q_RANK + MLA_KV_RANK]
        rq = lax.rsqrt(jnp.mean(cq * cq, axis=-1, keepdims=True) + EPS)
        rkv = lax.rsqrt(jnp.mean(ckv * ckv, axis=-1, keepdims=True) + EPS)
        cq_ref[...] = (cq * rq * qa_ref[...]).astype(BF16)
        ckv_ref[...] = (ckv * rkv * kva_ref[...]).astype(BF16)

    return pl.pallas_call(
        body, name="mla_mid", grid=(t // tm,),
        in_specs=[pl.BlockSpec((tm, MLA_IN_PAD), lambda i: (i, 0)),
                  pl.BlockSpec((1, MLA_Q_RANK), lambda i: (0, 0)),
                  pl.BlockSpec((1, MLA_KV_RANK), lambda i: (0, 0))],
        out_specs=[pl.BlockSpec((tm, MLA_Q_RANK), lambda i: (i, 0)),
                   pl.BlockSpec((tm, MLA_KV_RANK), lambda i: (i, 0))],
        out_shape=[_sds((t, MLA_Q_RANK), BF16), _sds((t, MLA_KV_RANK), BF16)],
        compiler_params=_cparams(("parallel",)),
    )(proj2, qa, kva)


def _mla_mid_bwd(proj2, qa, kva, dcq, dckv, dkr):
    t = proj2.shape[0]
    tm = _row_tile(t)

    def body(p_ref, qa_ref, kva_ref, dcq_ref, dckv_ref, dkr_ref, dp_ref, dqa_ref, dkva_ref):
        @pl.when(pl.program_id(0) == 0)
        def _():
            dqa_ref[...] = jnp.zeros_like(dqa_ref)
            dkva_ref[...] = jnp.zeros_like(dkva_ref)

        dxq, dgq = _rms_bwd_rows(dcq_ref[...], p_ref[:, :MLA_Q_RANK], qa_ref[...], MLA_Q_RANK)
        dxk, dgk = _rms_bwd_rows(dckv_ref[...], p_ref[:, MLA_Q_RANK:MLA_Q_RANK + MLA_KV_RANK], kva_ref[...],
                                 MLA_KV_RANK)
        dp_ref[:, :MLA_Q_RANK] = dxq.astype(BF16)
        dp_ref[:, MLA_Q_RANK:MLA_Q_RANK + MLA_KV_RANK] = dxk.astype(BF16)
        dp_ref[:, MLA_Q_RANK + MLA_KV_RANK:] = dkr_ref[...].astype(BF16)
        dqa_ref[...] += jnp.sum(dgq, axis=0, keepdims=True)
        dkva_ref[...] += jnp.sum(dgk, axis=0, keepdims=True)

    return pl.pallas_call(
        body, name="mla_mid_bwd", grid=(t // tm,),
        in_specs=[pl.BlockSpec((tm, MLA_IN_PAD), lambda i: (i, 0)),
                  pl.BlockSpec((1, MLA_Q_RANK), lambda i: (0, 0)),
                  pl.BlockSpec((1, MLA_KV_RANK), lambda i: (0, 0)),
                  pl.BlockSpec((tm, MLA_Q_RANK), lambda i: (i, 0)),
                  pl.BlockSpec((tm, MLA_KV_RANK), lambda i: (i, 0)),
                  pl.BlockSpec((tm, 128), lambda i: (i, 0))],
        out_specs=[pl.BlockSpec((tm, MLA_IN_PAD), lambda i: (i, 0)),
                   pl.BlockSpec((1, MLA_Q_RANK), lambda i: (0, 0)),
                   pl.BlockSpec((1, MLA_KV_RANK), lambda i: (0, 0))],
        out_shape=[_sds((t, MLA_IN_PAD), BF16), _sds((1, MLA_Q_RANK), F32), _sds((1, MLA_KV_RANK), F32)],
        compiler_params=_cparams(("arbitrary",)),
    )(proj2, qa, kva, dcq, dckv, dkr)


def _mla_prep_specs(t, tm):
    head = lambda w: pl.BlockSpec((None, tm, w), lambda i, h: (h, i, 0))
    return dict(
        head256=head(MLA_HD_PAD), head128=head(MLA_VD),
        cols256=pl.BlockSpec((tm, MLA_HD_PAD), lambda i, h: (i, h)),
        kr=pl.BlockSpec((tm, 128), lambda i, h: (i, (MLA_Q_RANK + MLA_KV_RANK) // 128)),
        gain=pl.BlockSpec((1, MLA_HD_PAD), lambda i, h: (0, 0)),
        tab=pl.BlockSpec((tm, 128), lambda i, h: (i, 0)),
    )


def _mla_prep(q, kv, proj2, gq, gk, tabs):
    t = q.shape[1]
    tm = _row_tile(t)
    sp = _mla_prep_specs(t, tm)

    def body(q_ref, kv_ref, kr_ref, gq_ref, gk_ref, c_ref, s1_ref, s2_ref, qh_ref, kh_ref, vh_ref):
        c, s1, s2 = c_ref[...], s1_ref[...], s2_ref[...]

        def norm_rope(xv, gain):
            r = lax.rsqrt(jnp.sum(xv * xv, axis=-1, keepdims=True) / MLA_QKD + EPS)
            y = xv * r * gain
            return jnp.concatenate([y[:, :MLA_NOPE], _rope_tile(y[:, MLA_NOPE:], c, s1, s2)], axis=-1)

        kvv = kv_ref[...]
        qh_ref[...] = norm_rope(q_ref[...], gq_ref[...]).astype(BF16)
        kf = jnp.concatenate([kvv[:, :MLA_NOPE], kr_ref[...]], axis=-1)
        kh_ref[...] = norm_rope(kf, gk_ref[...]).astype(BF16)
        vh_ref[...] = jnp.concatenate([kvv[:, MLA_NOPE:], jnp.ones((tm, MLA_VD), F32)], axis=-1).astype(BF16)

    return pl.pallas_call(
        body, name="mla_prep", grid=(t // tm, MLA_HEADS),
        in_specs=[sp['head256'], sp['head256'], sp['kr'], sp['gain'], sp['gain'], sp['tab'], sp['tab'], sp['tab']],
        out_specs=[sp['head256'], sp['head256'], sp['head256']],
        out_shape=[_sds((MLA_HEADS, t, MLA_HD_PAD), BF16), _sds((MLA_HEADS, t, MLA_HD_PAD), BF16),
                   _sds((MLA_HEADS, t, 2 * MLA_VD), BF16)],
        compiler_params=_cparams(("parallel", "arbitrary")),
    )(q, kv, proj2, gq, gk, *tabs)


def _mla_prep_bwd(q, kv, proj2, gq, gk, tabs, dqt, dkh, dvh):
    t = q.shape[1]
    tm = _row_tile(t)
    ab = dqt.shape[-1]
    sp = _mla_prep_specs(t, tm)

    def body(q_ref, kv_ref, kr_ref, gq_ref, gk_ref, c_ref, s1_ref, s2_ref, dqt_ref, dkh_ref, dvh_ref,
             dq_ref, dkv_ref, dkr_ref, dgq_ref, dgk_ref):
        dqh = jnp.concatenate([dqt_ref[b].T for b in range(tm // ab)], axis=0)
        i, h = pl.program_id(0), pl.program_id(1)

        @pl.when((i == 0) & (h == 0))
        def _():
            dgq_ref[...] = jnp.zeros_like(dgq_ref)
            dgk_ref[...] = jnp.zeros_like(dgk_ref)

        @pl.when(h == 0)
        def _():
            dkr_ref[...] = jnp.zeros_like(dkr_ref)

        c, s1, s2 = c_ref[...], s1_ref[...], s2_ref[...]

        def back(xv, gain, dout):
            dy = jnp.concatenate([dout[:, :MLA_NOPE], _rope_tile(dout[:, MLA_NOPE:], c, -s1, -s2)], axis=-1)
            return _rms_bwd_rows(dy, xv, gain, MLA_QKD)

        kvv = kv_ref[...]
        dxq, dgq = back(q_ref[...], gq_ref[...], dqh)
        kf = jnp.concatenate([kvv[:, :MLA_NOPE], kr_ref[...]], axis=-1)
        dxk, dgk = back(kf, gk_ref[...], dkh_ref[...])
        dq_ref[...] = dxq.astype(BF16)
        dkv_ref[...] = jnp.concatenate([dxk[:, :MLA_NOPE], dvh_ref[...]], axis=-1).astype(BF16)
        dkr_ref[...] += dxk[:, MLA_NOPE:]
        dgq_ref[...] += jnp.sum(dgq, axis=0, keepdims=True)
        dgk_ref[...] += jnp.sum(dgk, axis=0, keepdims=True)

    return pl.pallas_call(
        body, name="mla_prep_bwd", grid=(t // tm, MLA_HEADS),
        in_specs=[sp['head256'], sp['head256'], sp['kr'], sp['gain'], sp['gain'], sp['tab'], sp['tab'], sp['tab'],
                  pl.BlockSpec((None, tm // ab, MLA_HD_PAD, ab), lambda i, h: (h, i, 0, 0)),
                  sp['head256'], sp['head128']],
        out_specs=[sp['cols256'], sp['cols256'], sp['tab'], sp['gain'], sp['gain']],
        out_shape=[_sds((t, MLA_HEADS * MLA_HD_PAD), BF16), _sds((t, MLA_HEADS * MLA_HD_PAD), BF16),
                   _sds((t, 128), F32), _sds((1, MLA_HD_PAD), F32), _sds((1, MLA_HD_PAD), F32)],
        compiler_params=_cparams(("arbitrary", "arbitrary")),
    )(q, kv, proj2, gq, gk, *tabs, dqt, dkh, dvh)


def _chunk_visible(rows, cols, row_off, col_off):
    rq = lax.shift_right_logical(lax.broadcasted_iota(jnp.int32, (rows, cols), 0) + row_off, 6)
    ck = lax.shift_right_logical(lax.broadcasted_iota(jnp.int32, (rows, cols), 1) + col_off, 6)
    return ck <= rq


def _rows_to_lanes(col):
    return col.T[:8, :]


def _attn_fwd(qh, kh, vh):
    t = qh.shape[1]
    ab = min(ATT_BLOCK, t)
    tq = min(ATT_QROWS, t)
    r = tq // ab
    hg = ATT_HEADS

    def body(q_ref, k_ref, v_ref, o_ref, lse_ref):
        n_un = pl.program_id(1) * r

        def step(b, state, diag):
            rows = pl.ds(pl.multiple_of(b * ab, ab), ab)
            ms, accs = [], []
            for hh in range(hg):
                m, acc = state[0][hh], state[1][hh]
                s = _dot(q_ref[hh], k_ref[hh, rows, :], 1, 1)
                if diag is not None:
                    s = jnp.where(_chunk_visible(tq, ab, 0, diag * ab), s, -1e30)
                m_new = jnp.maximum(m, jnp.max(s, axis=-1, keepdims=True))
                p = jnp.exp2((s - m_new) * ATT_EXP2).astype(BF16)
                accs.append(jnp.exp2((m - m_new) * ATT_EXP2) * acc + _dot(p, v_ref[hh, rows, :], 1, 0))
                ms.append(m_new)
            return tuple(ms), tuple(accs)

        heads = lambda v: tuple(v for _ in range(hg))
        state = (heads(jnp.full((tq, 1), -1e30, F32)), heads(jnp.zeros((tq, 2 * MLA_VD), F32)))
        state = lax.fori_loop(0, n_un, lambda b, st: step(b, st, None), state)
        for d in range(r):
            state = step(n_un + d, state, d)
        ms, accs = state
        for hh in range(hg):
            l = accs[hh][:, MLA_VD:]
            o_ref[:, hh * MLA_VD:(hh + 1) * MLA_VD] = accs[hh][:, :MLA_VD] / l
            lse_t = _rows_to_lanes(ms[hh] * ATT_EXP2 + jnp.log(l) * LOG2E)
            for d in range(r):
                lse_ref[hh, d] = lse_t[:, d * ab:(d + 1) * ab]

    return pl.pallas_call(
        body, name="mla_attn", grid=(MLA_HEADS // hg, t // tq),
        in_specs=[pl.BlockSpec((hg, tq, MLA_HD_PAD), lambda g, i: (g, i, 0)),
                  pl.BlockSpec((hg, t, MLA_HD_PAD), lambda g, i: (g, 0, 0)),
                  pl.BlockSpec((hg, t, 2 * MLA_VD), lambda g, i: (g, 0, 0))],
        out_specs=[pl.BlockSpec((tq, hg * MLA_VD), lambda g, i: (i, g)),
                   pl.BlockSpec((hg, r, 8, ab), lambda g, i: (g, i, 0, 0))],
        out_shape=[_sds((t, MLA_HEADS * MLA_VD), F32), _sds((MLA_HEADS, t // ab, 8, ab), F32)],
        compiler_params=_cparams(("parallel", "arbitrary")),
    )(qh, kh, vh)


def _attn_delta(do, o, ab):
    t = do.shape[0]
    tm = _row_tile(t)

    def body(do_ref, o_ref, d_ref):
        d = jnp.sum(do_ref[...] * o_ref[...], axis=-1, keepdims=True)
        d_t = _rows_to_lanes(jnp.broadcast_to(d, (tm, 128)))
        for b in range(tm // ab):
            d_ref[b] = d_t[:, b * ab:(b + 1) * ab]

    col = pl.BlockSpec((tm, MLA_VD), lambda i, h: (i, h))
    return pl.pallas_call(
        body, name="mla_delta", grid=(t // tm, MLA_HEADS), in_specs=[col, col],
        out_specs=pl.BlockSpec((None, tm // ab, 8, ab), lambda i, h: (h, i, 0, 0)),
        out_shape=_sds((MLA_HEADS, t // ab, 8, ab), F32),
        compiler_params=_cparams(("parallel", "parallel")),
    )(do, o)


def _attn_bwd(qh, kh, vh, dob, lse_t, dl_t):
    t = qh.shape[1]
    ab = min(ATT_BLOCK, t)
    nq = t // ab
    hg = ATT_HEADS

    def body(q_ref, k_ref, v_ref, do_ref, lse_ref, dl_ref, dqt_ref, dk_ref, dv_ref):
        j = pl.program_id(1)

        @pl.when(j == 0)
        def _():
            dqt_ref[...] = jnp.zeros_like(dqt_ref)

        ks = [k_ref[hh] for hh in range(hg)]
        vs = [v_ref[hh, :, :MLA_VD] for hh in range(hg)]
        kts = [k.T for k in ks]

        def step(b, grads, masked):
            rows = pl.ds(pl.multiple_of(b * ab, ab), ab)
            out = []
            for hh in range(hg):
                dk, dv = grads[hh]
                q = q_ref[hh, rows, :]
                do = do_ref[rows, hh * MLA_VD:(hh + 1) * MLA_VD]
                s_t = _dot(ks[hh], q, 1, 1)
                if masked:
                    key_chunk = lax.shift_right_logical(lax.broadcasted_iota(jnp.int32, (ab, ab), 0), 6)
                    query_chunk = lax.shift_right_logical(lax.broadcasted_iota(jnp.int32, (ab, ab), 1), 6)
                    s_t = jnp.where(key_chunk <= query_chunk, s_t, -1e30)
                p_t = jnp.exp2(s_t * ATT_EXP2 - lse_ref[hh, b][0:1, :])
                dp_t = _dot(vs[hh], do, 1, 1)
                ds_t = (p_t * (dp_t - dl_ref[hh, b][0:1, :]) * ATT_SCALE).astype(BF16)
                dqt_ref[hh, b] += _dot(kts[hh], ds_t, 1, 0)
                out.append((dk + _dot(ds_t, q, 1, 0), dv + _dot(p_t.astype(BF16), do, 1, 0)))
            return tuple(out)

        grads = tuple((jnp.zeros((ab, MLA_HD_PAD), F32), jnp.zeros((ab, MLA_VD), F32)) for _ in range(hg))
        grads = step(j, grads, True)
        grads = lax.fori_loop(j + 1, nq, lambda b, g: step(b, g, False), grads)
        for hh in range(hg):
            dk_ref[hh] = grads[hh][0]
            dv_ref[hh] = grads[hh][1]

    whole = lambda w: pl.BlockSpec((hg, t, w), lambda g, j: (g, 0, 0))
    blk = lambda w: pl.BlockSpec((hg, ab, w), lambda g, j: (g, j, 0))
    stat = pl.BlockSpec((hg, nq, 8, ab), lambda g, j: (g, 0, 0, 0))
    return pl.pallas_call(
        body, name="mla_attn_bwd", grid=(MLA_HEADS // hg, nq),
        in_specs=[whole(MLA_HD_PAD), blk(MLA_HD_PAD), blk(2 * MLA_VD),
                  pl.BlockSpec((t, hg * MLA_VD), lambda g, j: (0, g)), stat, stat],
        out_specs=[pl.BlockSpec((hg, nq, MLA_HD_PAD, ab), lambda g, j: (g, 0, 0, 0)), blk(MLA_HD_PAD), blk(MLA_VD)],
        out_shape=[_sds((MLA_HEADS, nq, MLA_HD_PAD, ab), F32), _sds((MLA_HEADS, t, MLA_HD_PAD), F32),
                   _sds((MLA_HEADS, t, MLA_VD), F32)],
        compiler_params=_cparams(("parallel", "arbitrary")),
    )(qh, kh, vh, dob, lse_t, dl_t)


VEC = pl.BlockSpec((1, D_MODEL), lambda i, j, k: (0, 0))


def _residual_epi(next_gain):
    if next_gain is None:
        return [], lambda acc, hv: (acc + hv,)

    def epi(acc, hv, g):
        h_new = acc + hv
        r = lax.rsqrt(jnp.mean(h_new * h_new, axis=-1, keepdims=True) + EPS)
        return h_new, h_new * r * g

    return [(next_gain, VEC)], epi


def _residual_outs(t, row, next_gain):
    outs = [(_sds((t, D_MODEL), F32), row)]
    return outs + ([(_sds((t, D_MODEL), BF16), row)] if next_gain is not None else [])


def _mlp_fwd(l, h, hn, w1g, fetch_w2, next_gain):
    t = h.shape[0]
    tm = _row_tile(t)
    nsh, _, wsh = w1g.shape

    def relu2(acc):
        r = jnp.maximum(acc, 0.0)
        return (r * r,)

    tu = _row_tile(t, 2 * ROW_TILE)
    tile = pl.BlockSpec((tu, wsh), lambda i, j, k: (i, j))
    (u,) = _mm(f"mlp_up{l}", (t // tu, nsh, 1),
               hn, pl.BlockSpec((tu, D_MODEL), lambda i, j, k: (i, 0)),
               w1g, pl.BlockSpec((None, D_MODEL, wsh), lambda i, j, k: (j, 0, 0)), (1, 0),
               [(_sds((t, D_FF), BF16), tile)], epi=relu2)
    w2g = fetch_w2((u,))
    row = pl.BlockSpec((tm, D_MODEL), lambda i, j, k: (i, 0))
    more, epi = _residual_epi(next_gain)
    h2, hn_next = _mm(f"mlp_down{l}", (t // tm, 1, nsh),
                      u, pl.BlockSpec((tm, wsh), lambda i, j, k: (i, k)),
                      w2g, pl.BlockSpec((None, wsh, D_MODEL), lambda i, j, k: (k, 0, 0)), (1, 0),
                      _residual_outs(t, row, next_gain), extras=[(h, row)] + more, epi=epi)
    return h2, hn_next, (h, hn, u, w1g, w2g)


def _norm_bwd_outs(t, tm):
    return [(_sds((t, D_MODEL), F32), pl.BlockSpec((tm, D_MODEL), lambda i, j, k: (i, 0))),
            (_sds((t // tm, 1, D_MODEL), F32), pl.BlockSpec((None, 1, D_MODEL), lambda i, j, k: (i, 0, 0)))]


def _norm_bwd_epi(acc, xv, res, g):
    dx, dgr = _rms_bwd_rows(acc, xv, g, D_MODEL)
    return res + dx, jnp.sum(dgr, axis=0, keepdims=True)


def _mlp_bwd(l, dh, saved, norm_g):
    h, hn, u, w1g, w2g = saved
    t = h.shape[0]
    tm = _row_tile(t)
    nsh, _, wsh = w1g.shape
    tu = _row_tile(t, 2 * ROW_TILE)
    tile = pl.BlockSpec((tu, wsh), lambda i, j, k: (i, j))
    (da,) = _mm(f"mlp_du{l}", (t // tu, nsh, 1),
                dh, pl.BlockSpec((tu, D_MODEL), lambda i, j, k: (i, 0)),
                w2g, pl.BlockSpec((None, wsh, D_MODEL), lambda i, j, k: (j, 0, 0)), (1, 1),
                [(_sds((t, D_FF), BF16), tile)], extras=[(u, tile)],
                epi=lambda acc, uv: (2.0 * jnp.sqrt(uv.astype(F32)) * acc,))
    tw = _row_tile(t, 512)
    (dw2,) = _mm(f"mlp_dw2{l}", (1, 1, t // tw),
                 u, pl.BlockSpec((tw, D_FF), lambda i, j, k: (k, 0)),
                 dh, pl.BlockSpec((tw, D_MODEL), lambda i, j, k: (k, 0)), (0, 0),
                 [(_sds((D_FF, D_MODEL), BF16), pl.BlockSpec((D_FF, D_MODEL), lambda i, j, k: (0, 0)))])
    dw2 = dw2.reshape(nsh, wsh, D_MODEL)
    (dw1,) = _mm(f"mlp_dw1{l}", (1, 1, t // tw),
                 hn, pl.BlockSpec((tw, D_MODEL), lambda i, j, k: (k, 0)),
                 da, pl.BlockSpec((tw, D_FF), lambda i, j, k: (k, 0)), (0, 0),
                 [(_sds((nsh, D_MODEL, wsh), BF16), pl.BlockSpec((nsh, D_MODEL, wsh), lambda i, j, k: (0, 0, 0)))],
                 split=wsh)
    row = pl.BlockSpec((tm, D_MODEL), lambda i, j, k: (i, 0))
    dh_in, dg = _mm(f"mlp_dhn{l}", (t // tm, 1, nsh),
                    da, pl.BlockSpec((tm, wsh), lambda i, j, k: (i, k)),
                    w1g, pl.BlockSpec((None, D_MODEL, wsh), lambda i, j, k: (k, 0, 0)), (1, 1),
                    _norm_bwd_outs(t, tm), extras=[(h, row), (dh, row), (norm_g, VEC)], epi=_norm_bwd_epi)
    return dh_in, jnp.sum(dg, axis=0), dw1, dw2


def _ple_fwd(l, h, hn, p, wg, wp, next_gain):
    t = h.shape[0]
    tm = _row_tile(t, 512)
    row = pl.BlockSpec((tm, D_MODEL), lambda i, j, k: (i, 0))
    full = lambda r: pl.BlockSpec((r, D_MODEL), lambda i, j, k: (0, 0))
    (e,) = _mm(f"ple_proj{l}", (t // tm, 1, 1),
               p, pl.BlockSpec((None, None, tm, PLE_DIM), lambda i, j, k: (l, 0, i, 0)),
               wp, full(PLE_DIM), (1, 0), [(_sds((t, D_MODEL), F32), row)])

    def gate_epi(acc, hv, ev, *gain):
        gt = _sigmoid(acc)
        h_new = hv + gt * ev
        if not gain:
            return h_new, gt
        r = lax.rsqrt(jnp.mean(h_new * h_new, axis=-1, keepdims=True) + EPS)
        return h_new, gt, h_new * r * gain[0]

    f32_row, bf_row = (_sds((t, D_MODEL), F32), row), (_sds((t, D_MODEL), BF16), row)
    res = _mm(f"ple_gate{l}", (t // tm, 1, 1), hn, row, wg, full(D_MODEL), (1, 0),
              [f32_row, f32_row] + ([bf_row] if next_gain is not None else []),
              extras=[(h, row), (e, row)] + ([(next_gain, VEC)] if next_gain is not None else []), epi=gate_epi)
    h_out, gate = res[0], res[1]
    return h_out, (res[2] if next_gain is not None else None), (h, hn, gate, e)


def _ple_bwd(l, dh, saved, p, norm_g, wg, deps=()):
    h, hn, gate, e = saved
    t = h.shape[0]
    tm = _row_tile(t)
    tk = _row_tile(t, 512)
    de, dz = _ple_gate_bwd(f"ple_gate_bwd{l}", dh, gate, e)
    full = lambda r: pl.BlockSpec((r, D_MODEL), lambda i, j, k: (0, 0))
    rowk = pl.BlockSpec((tk, D_MODEL), lambda i, j, k: (k, 0))
    (dwp,) = _mm(f"ple_dwp{l}", (1, 1, t // tk),
                 p, pl.BlockSpec((None, None, tk, PLE_DIM), lambda i, j, k: (l, 0, k, 0)),
                 de, rowk, (0, 0), [(_sds((PLE_DIM, D_MODEL), BF16), full(PLE_DIM))], deps=deps)
    (dwg,) = _mm(f"ple_dwg{l}", (1, 1, t // tk), hn, rowk, dz, rowk, (0, 0),
                 [(_sds((D_MODEL, D_MODEL), BF16), full(D_MODEL))])
    row = pl.BlockSpec((tm, D_MODEL), lambda i, j, k: (i, 0))
    dh_in, dg = _mm(f"ple_dhn{l}", (t // tm, 1, 1), dz, row, wg, full(D_MODEL), (1, 1),
                    _norm_bwd_outs(t, tm), extras=[(h, row), (dh, row), (norm_g, VEC)], epi=_norm_bwd_epi)
    return dh_in, jnp.sum(dg, axis=0), dwg, dwp


def _ret_layer_fwd(x, norm_g, wri, fetch_wro, gn, cos, sin, next_gain, deps=()):
    t = x.shape[0]
    tm = _row_tile(t)
    nsh, _, wsh = wri.shape
    hn = _rms_fwd("mix_norm0", x, norm_g)
    (proj,) = _mm("ret_in", (t // tm, nsh, 1),
                  hn, pl.BlockSpec((tm, D_MODEL), lambda i, j, k: (i, 0)),
                  wri, pl.BlockSpec((None, D_MODEL, wsh), lambda i, j, k: (j, 0, 0)), (1, 0),
                  [(_sds((t, RET_IN), F32), pl.BlockSpec((tm, wsh), lambda i, j, k: (i, j)))], deps=deps)
    gated, outp, states = _ret_fwd(proj, cos, sin, gn)
    wro = fetch_wro((gated,))
    row = pl.BlockSpec((tm, D_MODEL), lambda i, j, k: (i, 0))
    kt = 512
    more, epi = _residual_epi(next_gain)
    h1, hn_next = _mm("ret_out", (t // tm, 1, RET_V_W // kt),
                      gated, pl.BlockSpec((tm, kt), lambda i, j, k: (i, k)),
                      wro, pl.BlockSpec((kt, D_MODEL), lambda i, j, k: (k, 0)), (1, 0),
                      _residual_outs(t, row, next_gain), extras=[(x, row)] + more, epi=epi)
    return h1, hn_next, (x, hn, proj, gated, outp, states, wro)


def _ret_layer_bwd(dh, saved, norm_g, wri, gn, cos, sin, emit, deps=()):
    x, hn, proj, gated, outp, states, wro = saved
    t = x.shape[0]
    tm = _row_tile(t)
    tk = _row_tile(t, 512)
    nsh, _, wsh = wri.shape
    (dgated,) = _mm("ret_dgated", (t // tm, RET_V_W // D_MODEL, 1),
                    dh, pl.BlockSpec((tm, D_MODEL), lambda i, j, k: (i, 0)),
                    wro, pl.BlockSpec((D_MODEL, D_MODEL), lambda i, j, k: (j, 0)), (1, 1),
                    [(_sds((t, RET_V_W), F32), pl.BlockSpec((tm, D_MODEL), lambda i, j, k: (i, j)))], deps=deps)
    (dwro,) = _mm("ret_dwro", (1, 1, t // tk),
                  gated, pl.BlockSpec((tk, RET_V_W), lambda i, j, k: (k, 0)),
                  dh, pl.BlockSpec((tk, D_MODEL), lambda i, j, k: (k, 0)), (0, 0),
                  [(_sds((RET_V_W, D_MODEL), BF16), pl.BlockSpec((RET_V_W, D_MODEL), lambda i, j, k: (0, 0)))])
    dproj, dgn = _ret_bwd(proj, cos, sin, gn, outp, states, dgated)
    half = nsh // 2
    (dwri,) = _mm("ret_dwri", (2, 1, t // tk),
                  hn, pl.BlockSpec((tk, D_MODEL), lambda i, j, k: (k, 0)),
                  dproj, pl.BlockSpec((tk, half * wsh), lambda i, j, k: (k, i)), (0, 0),
                  [(_sds((nsh, D_MODEL, wsh), BF16), pl.BlockSpec((half, D_MODEL, wsh), lambda i, j, k: (i, 0, 0)))],
                  split=wsh)
    deps = emit(dwro, dwri)
    row = pl.BlockSpec((tm, D_MODEL), lambda i, j, k: (i, 0))
    dx, dg = _mm("ret_dhn", (t // tm, 1, nsh),
                 dproj, pl.BlockSpec((tm, wsh), lambda i, j, k: (i, k)),
                 wri, pl.BlockSpec((None, D_MODEL, wsh), lambda i, j, k: (k, 0, 0)), (1, 1),
                 _norm_bwd_outs(t, tm), extras=[(x, row), (dh, row), (norm_g, VEC)], epi=_norm_bwd_epi, deps=deps)
    return dx, jnp.sum(dg, axis=0), dgn.reshape(RET_HEADS, RET_DV)


def _mla_layer_fwd(h, hn, wmi, qa, kva, wuq, wukv, gq, gk, wmo, tabs, next_gain):
    t = h.shape[0]
    tm = _row_tile(t)
    row = pl.BlockSpec((tm, D_MODEL), lambda i, j, k: (i, 0))
    (proj2,) = _mm("mla_in", (t // tm, 1, 1), hn, row,
                   wmi, pl.BlockSpec((D_MODEL, MLA_IN_PAD), lambda i, j, k: (0, 0)), (1, 0),
                   [(_sds((t, MLA_IN_PAD), F32), pl.BlockSpec((tm, MLA_IN_PAD), lambda i, j, k: (i, 0)))])
    cq, ckv = _mla_mid(proj2, qa, kva)
    head = pl.BlockSpec((None, tm, MLA_HD_PAD), lambda i, j, k: (j, i, 0))
    (q,) = _mm("mla_uq", (t // tm, MLA_HEADS, 1),
               cq, pl.BlockSpec((tm, MLA_Q_RANK), lambda i, j, k: (i, 0)),
               wuq, pl.BlockSpec((None, MLA_Q_RANK, MLA_HD_PAD), lambda i, j, k: (j, 0, 0)), (1, 0),
               [(_sds((MLA_HEADS, t, MLA_HD_PAD), F32), head)])
    (kv,) = _mm("mla_ukv", (t // tm, MLA_HEADS, 1),
                ckv, pl.BlockSpec((tm, MLA_KV_RANK), lambda i, j, k: (i, 0)),
                wukv, pl.BlockSpec((None, MLA_KV_RANK, MLA_HD_PAD), lambda i, j, k: (j, 0, 0)), (1, 0),
                [(_sds((MLA_HEADS, t, MLA_HD_PAD), F32), head)])
    qh, kh, vh = _mla_prep(q, kv, proj2, gq, gk, tabs)
    o, lse = _attn_fwd(qh, kh, vh)
    more, epi = _residual_epi(next_gain)
    h_out, hn_next = _mm("mla_out", (t // tm, 1, 1), o, row,
                         wmo, pl.BlockSpec((D_MODEL, D_MODEL), lambda i, j, k: (0, 0)), (1, 0),
                         _residual_outs(t, row, next_gain), extras=[(h, row)] + more, epi=epi)
    return h_out, hn_next, (h, hn, proj2, cq, ckv, q, kv, qh, kh, vh, o, lse)


def _mla_layer_bwd(dh, saved, norm_g, wmi, qa, kva, wuq, wukv, gq, gk, wmo, tabs, deps=()):
    h, hn, proj2, cq, ckv, q, kv, qh, kh, vh, o, lse = saved
    t = h.shape[0]
    tm = _row_tile(t)
    tk = _row_tile(t, 512)
    row = pl.BlockSpec((tm, D_MODEL), lambda i, j, k: (i, 0))
    rowk = pl.BlockSpec((tk, D_MODEL), lambda i, j, k: (k, 0))
    sq = pl.BlockSpec((D_MODEL, D_MODEL), lambda i, j, k: (0, 0))
    do, dob = _mm("mla_do", (t // tm, 1, 1), dh, row, wmo, sq, (1, 1),
                  [(_sds((t, D_MODEL), F32), row), (_sds((t, D_MODEL), BF16), row)], epi=lambda acc: (acc, acc),
                  deps=deps)
    (dwmo,) = _mm("mla_dwo", (1, 1, t // tk), o, rowk, dh, rowk, (0, 0), [(_sds((D_MODEL, D_MODEL), BF16), sq)])
    delta = _attn_delta(do, o, lse.shape[-1])
    dqt, dkh, dvh = _attn_bwd(qh, kh, vh, dob, lse, delta)
    dq, dkv, dkr, dgq, dgk = _mla_prep_bwd(q, kv, proj2, gq, gk, tabs, dqt, dkh, dvh)

    wide = MLA_HEADS * MLA_HD_PAD
    widek = pl.BlockSpec((tk, wide), lambda i, j, k: (k, 0))
    (dwuq,) = _mm("mla_dwuq", (1, 1, t // tk),
                  cq, pl.BlockSpec((tk, MLA_Q_RANK), lambda i, j, k: (k, 0)), dq, widek, (0, 0),
                  [(_sds((MLA_HEADS, MLA_Q_RANK, MLA_HD_PAD), BF16),
                    pl.BlockSpec((MLA_HEADS, MLA_Q_RANK, MLA_HD_PAD), lambda i, j, k: (0, 0, 0)))], split=MLA_HD_PAD)
    (dwukv,) = _mm("mla_dwukv", (1, 1, t // tk),
                   ckv, pl.BlockSpec((tk, MLA_KV_RANK), lambda i, j, k: (k, 0)), dkv, widek, (0, 0),
                   [(_sds((MLA_HEADS, MLA_KV_RANK, MLA_HD_PAD), BF16),
                     pl.BlockSpec((MLA_HEADS, MLA_KV_RANK, MLA_HD_PAD), lambda i, j, k: (0, 0, 0)))],
                   split=MLA_HD_PAD)
    side_by_side = lambda wg: wg.transpose(1, 0, 2).reshape(wg.shape[1], wide)
    widei = pl.BlockSpec((tm, wide), lambda i, j, k: (i, 0))
    (dcq,) = _mm("mla_dcq", (t // tm, 1, 1), dq, widei,
                 side_by_side(wuq), pl.BlockSpec((MLA_Q_RANK, wide), lambda i, j, k: (0, 0)), (1, 1),
                 [(_sds((t, MLA_Q_RANK), F32), pl.BlockSpec((tm, MLA_Q_RANK), lambda i, j, k: (i, 0)))])
    (dckv,) = _mm("mla_dckv", (t // tm, 1, 1), dkv, widei,
                  side_by_side(wukv), pl.BlockSpec((MLA_KV_RANK, wide), lambda i, j, k: (0, 0)), (1, 1),
                  [(_sds((t, MLA_KV_RANK), F32), pl.BlockSpec((tm, MLA_KV_RANK), lambda i, j, k: (i, 0)))])
    dproj2, dqa, dkva = _mla_mid_bwd(proj2, qa, kva, dcq, dckv, dkr)
    win = pl.BlockSpec((D_MODEL, MLA_IN_PAD), lambda i, j, k: (0, 0))
    (dwmi,) = _mm("mla_dwin", (1, 1, t // tk), hn, rowk,
                  dproj2, pl.BlockSpec((tk, MLA_IN_PAD), lambda i, j, k: (k, 0)), (0, 0),
                  [(_sds((D_MODEL, MLA_IN_PAD), BF16), win)])
    dh_in, dg = _mm("mla_dhn", (t // tm, 1, 1),
                    dproj2, pl.BlockSpec((tm, MLA_IN_PAD), lambda i, j, k: (i, 0)), wmi, win, (1, 1),
                    _norm_bwd_outs(t, tm), extras=[(h, row), (dh, row), (norm_g, VEC)], epi=_norm_bwd_epi)
    return dh_in, dict(mix=jnp.sum(dg, axis=0), wmi=dwmi, qa=dqa, kva=dkva, wuq=dwuq, wukv=dwukv, gq=dgq, gk=dgk,
                       wmo=dwmo)


def _local_step(x, p, target, w, fetch, emit=lambda group: ()):
    t = x.shape[0]
    inv = 1.0 / (ROPE_THETA ** (jnp.arange(0, RET_DK, 2, dtype=F32) / RET_DK))
    ang = jnp.arange(t, dtype=F32)[:, None] * inv[None, :]
    cos_r, sin_r = jnp.cos(ang), jnp.sin(ang)
    tabs = _mla_tables(t)
    row = lambda a, i: a[i:i + 1]

    h1, hn1, s_ret = _ret_layer_fwd(x, row(w['mix_norm'], 0), w['ret_w_in'],
                                    lambda after: fetch('ret_out', after)['ret_w_out'], w['ret_gn'], cos_r, sin_r,
                                    row(w['mlp_norm'], 0), deps=w['deps'])
    h2, hn2, s_mlp0 = _mlp_fwd(0, h1, hn1, fetch('mlp_w1_0', (h1,))['mlp_w1'],
                               lambda after: fetch('mlp_w2_0', after)['mlp_w2'], row(w['ple_norm'], 0))
    w0 = fetch('ple_0', (h2,))
    h3, hn3, s_ple0 = _ple_fwd(0, h2, hn2, p, w0['ple_gate_w'], w0['ple_proj_w'], row(w['mix_norm'], 1))
    wm = fetch('mla', (h3,))
    mla_w = (wm['mla_w_in'], w['mla_q_a_norm'], w['mla_kv_a_norm'], wm['mla_w_uq'], wm['mla_w_ukv'],
             w['mla_q_norm'], w['mla_k_norm'], wm['mla_w_out'], tabs)
    h4, hn4, s_mla = _mla_layer_fwd(h3, hn3, *mla_w, row(w['mlp_norm'], 1))
    w1 = fetch('layer_1', (h4,))
    h5, hn5, s_mlp1 = _mlp_fwd(1, h4, hn4, w1['mlp_w1'], lambda after: w1['mlp_w2'], row(w['ple_norm'], 1))
    y, _, s_ple1 = _ple_fwd(1, h5, hn5, p, w1['ple_gate_w'], w1['ple_proj_w'], None)

    dy, sq_err = _loss_head(y, target)

    n = N_DEV
    colsh = lambda a: a.reshape(a.shape[0], n, a.shape[1] // n).transpose(1, 0, 2)
    rowsh = lambda a: a.reshape(n, a.shape[0] // n, a.shape[1])
    big = {}

    def emit_group(group):
        big.update(group)
        return emit(group)

    dh5, dg_ple1, dwg1, dwp1 = _ple_bwd(1, dy, s_ple1, p, row(w['ple_norm'], 1), w1['ple_gate_w'])
    dh4, dg_mlp1, dw1_1, dw2_1 = _mlp_bwd(1, dh5, s_mlp1, row(w['mlp_norm'], 1))
    deps = emit_group({('ple_gate_w', 1): rowsh(dwg1), ('ple_proj_w', 1): colsh(dwp1),
                       ('mlp_w2', 1): dw2_1, ('mlp_w1', 1): dw1_1})
    dh3, gm = _mla_layer_bwd(dh4, s_mla, row(w['mix_norm'], 1), *mla_w, deps=deps)
    deps = emit_group({('mla_w_out', 0): rowsh(gm['wmo']), ('mla_w_uq', 0): gm['wuq'][:, :, :MLA_QKD],
                       ('mla_w_ukv', 0): gm['wukv'], ('mla_w_in', 0): rowsh(gm['wmi'][:, :MLA_IN])})
    dh2, dg_ple0, dwg0, dwp0 = _ple_bwd(0, dh3, s_ple0, p, row(w['ple_norm'], 0), w0['ple_gate_w'], deps=deps)
    dh1, dg_mlp0, dw1_0, dw2_0 = _mlp_bwd(0, dh2, s_mlp0, row(w['mlp_norm'], 0))
    deps = emit_group({('ple_gate_w', 0): rowsh(dwg0), ('ple_proj_w', 0): colsh(dwp0),
                       ('mlp_w2', 0): dw2_0, ('mlp_w1', 0): dw1_0})
    dx, dg_mix0, dgn = _ret_layer_bwd(
        dh1, s_ret, row(w['mix_norm'], 0), w['ret_w_in'], w['ret_gn'], cos_r, sin_r,
        lambda dwro, dwri: emit_group({('ret_w_out', 0): rowsh(dwro), ('ret_w_in', 0): dwri}), deps=deps)

    small = dict(
        mix_norm=[dg_mix0, gm['mix']], mlp_norm=[dg_mlp0, dg_mlp1], ple_norm=[dg_ple0, dg_ple1],
        ret_gn=dgn, mla_q_a_norm=gm['qa'], mla_kv_a_norm=gm['kva'], mla_q_norm=gm['gq'], mla_k_norm=gm['gk'],
    )
    return sq_err, dx, big, small


def _my_place():
    x, y, c = lax.axis_index("x"), lax.axis_index("y"), lax.axis_index("c")
    return x, y, c


def _flat(px, py, pc):
    return 4 * px + 2 * py + pc


def _peer(x, y, c, r):
    return (1 - x if r & 4 else x, 1 - y if r & 2 else y, 1 - c if r & 1 else c)


def _all_gather(arrays):
    n = len(arrays)

    def body(*refs):
        ins, outs = refs[:n], refs[n:2 * n]
        send_sems, recv_sems, local_sems = refs[2 * n:]
        x, y, c = _my_place()
        me, sibling = (x, y, c), (x, y, 1 - c)
        chips = [(1 - x, y), (x, 1 - y), (1 - x, 1 - y)]

        def copy(a, k, block, to, src=None):
            slot = outs[a].at[_flat(*block)]
            return pltpu.make_async_remote_copy(
                src_ref=slot if src is None else src, dst_ref=slot,
                send_sem=send_sems.at[a, k], recv_sem=recv_sems.at[a, k], device_id=to, device_id_type=MESH)

        mine = [pltpu.make_async_copy(ins[a], outs[a].at[_flat(*me)], local_sems.at[a]) for a in range(n)]
        for cp in mine:
            cp.start()
        first = []
        for a in range(n):
            first.append(copy(a, 0, me, sibling, src=ins[a]))
            first += [copy(a, 1 + j, me, (*chip, c), src=ins[a]) for j, chip in enumerate(chips)]
        for cp in first:
            cp.start()
        passed = []
        for a in range(n):
            for j, chip in enumerate(chips):
                copy(a, 1 + j, (*chip, c), me).wait_recv()
                passed.append(copy(a, 4 + j, (*chip, c), sibling))
                passed[-1].start()
        for a in range(n):
            copy(a, 0, sibling, me).wait_recv()
            for j, chip in enumerate(chips):
                copy(a, 4 + j, (*chip, 1 - c), me).wait_recv()
        for cp in first + passed:
            cp.wait_send()
        for cp in mine:
            cp.wait()

    return pl.pallas_call(
        body, name="all_gather_weights",
        in_specs=[ANY] * n, out_specs=[ANY] * n,
        out_shape=[_sds((N_DEV,) + a.shape, a.dtype) for a in arrays],
        scratch_shapes=[pltpu.SemaphoreType.DMA((n, 7)), pltpu.SemaphoreType.DMA((n, 7)),
                        pltpu.SemaphoreType.DMA((n,))],
    )(*arrays)


HBM = pl.BlockSpec(memory_space=pltpu.HBM)
SEMS = pl.BlockSpec(memory_space=pltpu.SEMAPHORE)
SIDE_EFFECT = pltpu.SideEffectType.DATAFLOW_SIDE_EFFECTING


def _rs_copies(x, y, c, srcs, lands, send_sems, recv_sems):
    copies = []
    for a in range(len(srcs)):
        for r in range(1, N_DEV):
            peer = _peer(x, y, c, r)
            k = a * (N_DEV - 1) + r - 1
            copies.append(pltpu.make_async_remote_copy(
                src_ref=srcs[a].at[_flat(*peer)], dst_ref=lands[a].at[r - 1],
                send_sem=send_sems.at[k], recv_sem=recv_sems.at[k], device_id=peer, device_id_type=MESH))
    return copies


def _rs_start(name, arrays):
    n = len(arrays)
    hbm = lambda a: pltpu.with_memory_space_constraint(a, pltpu.HBM)
    lands = [hbm(lax.empty((N_DEV - 1,) + a.shape[1:], a.dtype)) for a in arrays]

    def body(*refs):
        srcs, lnd = refs[:n], refs[n:2 * n]
        send_sems, recv_sems = refs[2 * n], refs[2 * n + 1]
        token = refs[-1]
        for cp in _rs_copies(*_my_place(), srcs, lnd, send_sems, recv_sems):
            cp.start()
        token[...] = jnp.zeros_like(token)

    outs = pl.pallas_call(
        body, name=name,
        in_specs=[HBM] * (2 * n),
        out_specs=[SEMS, SEMS] + [HBM] * (2 * n) + [pl.BlockSpec(memory_space=pltpu.VMEM)],
        out_shape=[pltpu.SemaphoreType.DMA((n * (N_DEV - 1),)), pltpu.SemaphoreType.DMA((n * (N_DEV - 1),))]
        + [pltpu.HBM(a.shape, a.dtype) for a in arrays] + [pltpu.HBM(l.shape, l.dtype) for l in lands]
        + [_sds((8, 128), F32)],
        input_output_aliases={i: 2 + i for i in range(2 * n)},
        compiler_params=pltpu.CompilerParams(has_side_effects=SIDE_EFFECT),
    )(*[hbm(a) for a in arrays], *lands)
    return outs[0], outs[1], outs[2:2 + n], outs[2 + n:2 + 2 * n], outs[-1]


def _rs_wait(name, send_sems, recv_sems, srcs, lands, after):
    n = len(srcs)

    def body(*refs):
        src_refs, lnd = refs[:n], refs[n:2 * n]
        send, recv = refs[2 * n], refs[2 * n + 1]
        for cp in _rs_copies(*_my_place(), src_refs, lnd, send, recv):
            cp.wait_send()
            cp.wait_recv()

    outs = pl.pallas_call(
        body, name=name,
        in_specs=[HBM] * (2 * n) + [SEMS, SEMS] + [ANY] * len(after),
        out_specs=[HBM] * (2 * n),
        out_shape=[pltpu.HBM(a.shape, a.dtype) for a in list(srcs) + list(lands)],
        input_output_aliases={i: i for i in range(2 * n)},
        compiler_params=pltpu.CompilerParams(has_side_effects=SIDE_EFFECT),
    )(*srcs, *lands, send_sems, recv_sems, *after)
    return outs[:n], outs[n:]


SMALL_PACK_ROWS = 16


def _all_reduce_small(rows, deps=()):
    n = len(rows)

    def body(*refs):
        ins = refs[:n]
        out_ref, mine, buf, send_sems, recv_sems = refs[n + len(deps):]
        x, y, c = _my_place()
        mine[...] = jnp.zeros_like(mine)
        for (r0, a), ref in zip(rows, ins):
            mine[r0:r0 + a.shape[0], 0:a.shape[1]] = ref[...]
        buf[_flat(x, y, c)] = mine[...]
        copies = []
        for r in range(1, N_DEV):
            peer = _peer(x, y, c, r)
            send = pltpu.make_async_remote_copy(
                src_ref=mine, dst_ref=buf.at[_flat(x, y, c)],
                send_sem=send_sems.at[r - 1], recv_sem=recv_sems.at[r - 1], device_id=peer, device_id_type=MESH)
            send.start()
            recv = pltpu.make_async_remote_copy(
                src_ref=mine, dst_ref=buf.at[_flat(*peer)],
                send_sem=send_sems.at[r - 1], recv_sem=recv_sems.at[r - 1], device_id=peer, device_id_type=MESH)
            copies.append((send, recv))
        for send, recv in copies:
            send.wait_send()
            recv.wait_recv()
        acc = buf[0]
        for s in range(1, N_DEV):
            acc = acc + buf[s]
        out_ref[...] = acc

    vm = pl.BlockSpec(memory_space=pltpu.VMEM)
    shape = (SMALL_PACK_ROWS, D_MODEL)
    return pl.pallas_call(
        body, name="all_reduce_small", in_specs=[vm] * n + [ANY] * len(deps), out_specs=vm,
        out_shape=_sds(shape, F32),
        scratch_shapes=[pltpu.VMEM(shape, F32), pltpu.VMEM((N_DEV,) + shape, F32),
                        pltpu.SemaphoreType.DMA((7,)), pltpu.SemaphoreType.DMA((7,))],
    )(*[a for _, a in rows], *deps)


def _adamw_math(w, g, m, v):
    m = ADAM_B1 * m + (1.0 - ADAM_B1) * g
    v = ADAM_B2 * v + (1.0 - ADAM_B2) * (g * g)
    m_hat = m / (1.0 - ADAM_B1 ** ADAM_STEP)
    v_hat = v / (1.0 - ADAM_B2 ** ADAM_STEP)
    delta = -ADAM_LR * (m_hat / (jnp.sqrt(v_hat) + ADAM_EPS) + ADAM_WD * w)
    return delta, m, v


def _adamw_big(name, w, m, v, srcs, lands, me):
    nl, rows, cols = w.shape
    tr = next(cand for cand in (256, 128, 64, 32, 16, 8) if rows % cand == 0)

    def body(me_ref, w_ref, m_ref, v_ref, *rest):
        src_refs, land_refs = rest[:nl], rest[nl:2 * nl]
        g_ref, d_ref, mo_ref, vo_ref = rest[2 * nl:]
        for layer in range(nl):
            @pl.when(pl.program_id(0) == layer)
            def _():
                g = src_refs[layer][...].astype(F32)
                for s in range(N_DEV - 1):
                    g = g + land_refs[layer][s].astype(F32)
                delta, mn, vn = _adamw_math(w_ref[...], g, m_ref[...], v_ref[...])
                g_ref[...] = g
                d_ref[...] = delta
                mo_ref[...] = mn
                vo_ref[...] = vn

    blk = pl.BlockSpec((None, tr, cols), lambda l, i, me_ref: (l, i, 0))
    own = pl.BlockSpec((None, tr, cols), lambda l, i, me_ref: (me_ref[0], i, 0))
    peers = pl.BlockSpec((N_DEV - 1, tr, cols), lambda l, i, me_ref: (0, i, 0))
    return pl.pallas_call(
        body, name=name,
        grid_spec=pltpu.PrefetchScalarGridSpec(
            num_scalar_prefetch=1, grid=(nl, rows // tr),
            in_specs=[blk, blk, blk] + [own] * nl + [peers] * nl, out_specs=[blk] * 4),
        out_shape=[_sds((nl, rows, cols), F32)] * 4,
        compiler_params=_cparams(("arbitrary", "arbitrary")),
    )(me, w, m, v, *srcs, *lands)


def _adamw_small(ws, gs, ms, vs):
    n = len(ws)

    def body(*refs):
        w_refs, g_refs, m_refs, v_refs = (refs[i * n:(i + 1) * n] for i in range(4))
        d_out, m_out, v_out = (refs[(4 + i) * n:(5 + i) * n] for i in range(3))
        for i in range(n):
            delta, mn, vn = _adamw_math(w_refs[i][...], g_refs[i][...], m_refs[i][...], v_refs[i][...])
            d_out[i][...] = delta
            m_out[i][...] = mn
            v_out[i][...] = vn

    vm = pl.BlockSpec(memory_space=pltpu.VMEM)
    outs = pl.pallas_call(
        body, name="adamw_small", in_specs=[vm] * (4 * n), out_specs=[vm] * (3 * n),
        out_shape=[_sds(a.shape, F32) for a in ws] * 3,
    )(*ws, *gs, *ms, *vs)
    return outs[:n], outs[n:2 * n], outs[2 * n:]


SMALL_ROWS = 16


def _pad_to(a, rows, cols):
    return jnp.pad(a, ((0, rows - a.shape[0]), (0, cols - a.shape[1])))


def _place_own(blocks):
    me = _flat(*_my_place())
    return [lax.dynamic_update_slice(lax.empty((N_DEV,) + b.shape, b.dtype), b[None], (me,) + (0,) * b.ndim)
            for b in blocks]


def _ag_copies(x, y, c, blocks, bufs, send_sems, recv_sems):
    sends, recvs = [], []
    for a in range(len(blocks)):
        for r in range(1, N_DEV):
            peer = _peer(x, y, c, r)
            k = a * (N_DEV - 1) + r - 1
            make = lambda place: pltpu.make_async_remote_copy(
                src_ref=blocks[a], dst_ref=bufs[a].at[_flat(*place)],
                send_sem=send_sems.at[k], recv_sem=recv_sems.at[k], device_id=peer, device_id_type=MESH)
            sends.append(make((x, y, c)))
            recvs.append(make(peer))
    return sends, recvs


def _ag_start(groups, after):
    flat = [pair for g in groups for pair in g]
    n, ng = len(flat), len(groups)
    hbm = lambda a: pltpu.with_memory_space_constraint(a, pltpu.HBM)

    def body(*refs):
        blocks, bufs = refs[:n], refs[n:2 * n]
        sems = refs[2 * n + len(after):2 * n + len(after) + 2 * ng]
        x, y, c = _my_place()
        at = 0
        for gi, g in enumerate(groups):
            sends, _ = _ag_copies(x, y, c, blocks[at:at + len(g)], bufs[at:at + len(g)], sems[2 * gi], sems[2 * gi + 1])
            for cp in sends:
                cp.start()
            at += len(g)
        refs[-1][...] = jnp.zeros_like(refs[-1])

    sem_shapes = [pltpu.SemaphoreType.DMA((len(g) * (N_DEV - 1),)) for g in groups for _ in range(2)]
    outs = pl.pallas_call(
        body, name="gather_start",
        in_specs=[HBM] * (2 * n) + [ANY] * len(after),
        out_specs=[SEMS] * (2 * ng) + [HBM] * (2 * n) + [pl.BlockSpec(memory_space=pltpu.VMEM)],
        out_shape=sem_shapes + [pltpu.HBM(b.shape, b.dtype) for b, _ in flat]
        + [pltpu.HBM(u.shape, u.dtype) for _, u in flat] + [_sds((8, 128), F32)],
        input_output_aliases={i: 2 * ng + i for i in range(2 * n)},
        compiler_params=pltpu.CompilerParams(has_side_effects=SIDE_EFFECT),
    )(*[hbm(b) for b, _ in flat], *[hbm(u) for _, u in flat], *after)
    blocks_thru, bufs_thru = outs[2 * ng:2 * ng + n], outs[2 * ng + n:2 * ng + 2 * n]
    started, at = [], 0
    for gi, g in enumerate(groups):
        started.append((outs[2 * gi], outs[2 * gi + 1], blocks_thru[at:at + len(g)], bufs_thru[at:at + len(g)]))
        at += len(g)
    return started, outs[-1]


def _ag_wait(name, send_sems, recv_sems, blocks, bufs, after):
    n = len(blocks)

    def body(*refs):
        sends, recvs = _ag_copies(*_my_place(), refs[:n], refs[n:2 * n], refs[2 * n], refs[2 * n + 1])
        for s, r in zip(sends, recvs):
            s.wait_send()
            r.wait_recv()

    outs = pl.pallas_call(
        body, name=name,
        in_specs=[HBM] * (2 * n) + [SEMS, SEMS] + [ANY] * len(after),
        out_specs=[HBM] * (2 * n),
        out_shape=[pltpu.HBM(a.shape, a.dtype) for a in list(blocks) + list(bufs)],
        input_output_aliases={i: i for i in range(2 * n)},
        compiler_params=pltpu.CompilerParams(has_side_effects=SIDE_EFFECT),
    )(*blocks, *bufs, send_sems, recv_sems, *after)
    return outs[n:]


def _prepare_weights(p):
    n = N_DEV
    bf = lambda a: a.astype(BF16)
    gn_pack = jnp.concatenate([
        _pad_to(p['ret_gn'][0], RET_HEADS, 128), _pad_to(p['mla_q_a_norm'], 1, 128),
        _pad_to(p['mla_kv_a_norm'], 1, 128), jnp.zeros((2, 128), F32)], axis=0)
    ple = lambda l: [bf(p['ple_gate_w'][l]), bf(p['ple_proj_w'][l])]
    names = ('ret_out', 'mlp_w1_0', 'mlp_w2_0', 'ple_0', 'mla', 'layer_1')
    later = [[bf(p['ret_w_out'][0])], [bf(p['mlp_w1'][0])], [bf(p['mlp_w2'][0])], ple(0),
             [bf(p['mla_w_in'][0]), bf(p['mla_w_uq'][0]), bf(p['mla_w_ukv'][0]), bf(p['mla_w_out'][0])],
             [bf(p['mlp_w1'][1]), bf(p['mlp_w2'][1])] + ple(1)]
    bufs = _place_own([b for g in later for b in g])
    pack, wri = _all_gather([gn_pack, bf(p['ret_w_in'][0])])
    groups, at = [], 0
    for g in later:
        groups.append(list(zip(g, bufs[at:at + len(g)])))
        at += len(g)
    started, token = _ag_start(groups, (wri,))

    w = {k: p[k] for k in ('mix_norm', 'mlp_norm', 'ple_norm')}
    w['ret_gn'] = pack[:, :RET_HEADS, :RET_DV // n].transpose(1, 0, 2).reshape(RET_HEADS, RET_DV)
    w['mla_q_a_norm'] = pack[:, RET_HEADS, :MLA_Q_RANK // n].reshape(1, MLA_Q_RANK)
    w['mla_kv_a_norm'] = pack[:, RET_HEADS + 1, :MLA_KV_RANK // n].reshape(1, MLA_KV_RANK)
    w['ret_w_in'] = wri
    w['mla_q_norm'] = _pad_to(p['mla_q_norm'], 1, MLA_HD_PAD)
    w['mla_k_norm'] = _pad_to(p['mla_k_norm'], 1, MLA_HD_PAD)
    w['deps'] = (token,)

    def fetch(name, after):
        got = list(_ag_wait("gather_wait_" + name, *started[names.index(name)], after))
        if name == 'ret_out':
            return dict(ret_w_out=got[0].reshape(RET_V_W, D_MODEL))
        if name == 'mla':
            wmi, wuq, wukv, wmo = got
            return dict(mla_w_in=jnp.pad(wmi.reshape(D_MODEL, MLA_IN), ((0, 0), (0, MLA_IN_PAD - MLA_IN))),
                        mla_w_uq=jnp.pad(wuq, ((0, 0), (0, 0), (0, MLA_HD_PAD - MLA_QKD))),
                        mla_w_ukv=wukv, mla_w_out=wmo.reshape(D_MODEL, D_MODEL))
        out = {}
        if name in ('mlp_w1_0', 'layer_1'):
            out['mlp_w1'] = got.pop(0)
        if name in ('mlp_w2_0', 'layer_1'):
            out['mlp_w2'] = got.pop(0)
        if name in ('ple_0', 'layer_1'):
            out['ple_gate_w'] = got[0].reshape(D_MODEL, D_MODEL)
            out['ple_proj_w'] = got[1].transpose(1, 0, 2).reshape(PLE_DIM, D_MODEL)
        return out

    return w, fetch


def _small_grads(small, after):
    rows = [(0, small['mix_norm'][0]), (1, small['mix_norm'][1]), (2, small['mlp_norm'][0]),
            (3, small['mlp_norm'][1]), (4, small['ple_norm'][0]), (5, small['ple_norm'][1]),
            (6, small['ret_gn']), (10, small['mla_q_a_norm']), (11, small['mla_kv_a_norm']),
            (12, small['mla_q_norm']), (13, small['mla_k_norm'])]
    gs = _all_reduce_small(rows, after)
    me = _flat(*_my_place())
    n = N_DEV
    return dict(
        mix_norm=gs[0:2], mlp_norm=gs[2:4], ple_norm=gs[4:6],
        ret_gn=lax.dynamic_slice(gs, (6, me * (RET_DV // n)), (RET_HEADS, RET_DV // n)),
        mla_q_a_norm=lax.dynamic_slice(gs, (10, me * (MLA_Q_RANK // n)), (1, MLA_Q_RANK // n)),
        mla_kv_a_norm=lax.dynamic_slice(gs, (11, me * (MLA_KV_RANK // n)), (1, MLA_KV_RANK // n)),
        mla_q_norm=gs[12:13, :MLA_QKD], mla_k_norm=gs[13:14, :MLA_QKD])


def kernel(x, p, mix_norm, ret_w_in, ret_gn, ret_w_out, mla_w_in, mla_q_a_norm, mla_kv_a_norm, mla_w_uq, mla_w_ukv, mla_q_norm, mla_k_norm, mla_w_out, mlp_norm, mlp_w1, mlp_w2, ple_norm, ple_gate_w, ple_proj_w, loss_target, m_mix_norm, m_ret_w_in, m_ret_gn, m_ret_w_out, m_mla_w_in, m_mla_q_a_norm, m_mla_kv_a_norm, m_mla_w_uq, m_mla_w_ukv, m_mla_q_norm, m_mla_k_norm, m_mla_w_out, m_mlp_norm, m_mlp_w1, m_mlp_w2, m_ple_norm, m_ple_gate_w, m_ple_proj_w, v_mix_norm, v_ret_w_in, v_ret_gn, v_ret_w_out, v_mla_w_in, v_mla_q_a_norm, v_mla_kv_a_norm, v_mla_w_uq, v_mla_w_ukv, v_mla_q_norm, v_mla_k_norm, v_mla_w_out, v_mlp_norm, v_mlp_w1, v_mlp_w2, v_ple_norm, v_ple_gate_w, v_ple_proj_w):
    given = dict(locals())
    params = {n: given[n] for n in WEIGHTS}
    w, fetch = _prepare_weights(params)

    started = []

    def emit(group):
        keys = list(group)
        send, recv, srcs, lands, token = _rs_start(f"rs_start{len(started)}", [group[k] for k in keys])
        started.append((keys, send, recv, srcs, lands))
        return (token,)

    sq_err, grad_x, _, small = _local_step(x[0], p, loss_target[0], w, fetch, emit)
    loss = lax.psum(0.5 / D_MODEL * sq_err[0, 0], ("x", "y", "c"))

    grads, deltas, new_m, new_v = {}, {}, {}, {}

    def small_updates(after):
        sg = _small_grads(small, after)
        two_d = lambda a: a.reshape(-1, a.shape[-1])
        d_s, m_s, v_s = _adamw_small(
            [two_d(params[n]) for n in SMALL], [sg[n] for n in SMALL],
            [two_d(given["m_" + n]) for n in SMALL], [two_d(given["v_" + n]) for n in SMALL])
        for i, n in enumerate(SMALL):
            shape = params[n].shape
            grads[n], deltas[n], new_m[n], new_v[n] = (a.reshape(shape) for a in (sg[n], d_s[i], m_s[i], v_s[i]))
        return (d_s[0],)

    me = _flat(*_my_place()).astype(jnp.int32).reshape(1)
    after = (grad_x,)
    src_of, land_of = {}, {}
    for gi, (keys, send, recv, srcs, lands) in enumerate(started):
        if gi == len(started) - 1:
            after = small_updates(after)
        srcs, lands = _rs_wait(f"rs_wait{gi}", send, recv, srcs, lands, after)
        for k, s, l in zip(keys, srcs, lands):
            src_of[k], land_of[k] = s, l
        done = [n for n in BIG if n not in grads and all((n, l) in src_of for l in range(params[n].shape[0]))]
        for n in done:
            layers = range(params[n].shape[0])
            grads[n], deltas[n], new_m[n], new_v[n] = _adamw_big(
                "adamw_" + n, params[n], given["m_" + n], given["v_" + n],
                [src_of[(n, l)] for l in layers], [land_of[(n, l)] for l in layers], me)
        if done:
            after = (deltas[done[-1]],)

    return (loss, grad_x[None], *[grads[n] for n in WEIGHTS], *[deltas[n] for n in WEIGHTS],
            *[new_m[n] for n in WEIGHTS], *[new_v[n] for n in WEIGHTS])
```

```python
import functools
import math

import jax
import jax.numpy as jnp
from jax import lax
from jax.experimental import pallas as pl
from jax.experimental.pallas import tpu as pltpu

F32 = jnp.float32
BF16 = jnp.bfloat16
MESH = pl.DeviceIdType.MESH
ANY = pl.BlockSpec(memory_space=pl.ANY)

N_DEV = 8
D_MODEL = 1024
CHUNK = 64
EPS = 1e-6
ROPE_THETA = 10000.0
RET_HEADS = 4
RET_DK = 256
RET_DV = 512
RET_QK_W = RET_HEADS * RET_DK
RET_V_W = RET_HEADS * RET_DV
RET_IN = 2 * RET_QK_W + 2 * RET_V_W
MLA_HEADS = 8
MLA_NOPE = 128
MLA_ROPE = 64
MLA_QKD = MLA_NOPE + MLA_ROPE
MLA_VD = 128
MLA_Q_RANK = 384
MLA_KV_RANK = 256
MLA_IN = MLA_Q_RANK + MLA_KV_RANK + MLA_ROPE
MLA_IN_PAD = 768
MLA_HD_PAD = 256
D_FF = 4096
PLE_DIM = 256
ATT_SCALE = MLA_QKD ** -0.5
LOG2E = 1.4426950408889634
ATT_EXP2 = ATT_SCALE * LOG2E

ADAM_LR = 0.001
ADAM_B1 = 0.9
ADAM_B2 = 0.999
ADAM_EPS = 1e-08
ADAM_WD = 0.01
ADAM_STEP = 10

VMEM_LIMIT = 52 * 1024 * 1024
ROW_TILE = 1024
RET_ROWS = 256
ATT_BLOCK = 256
ATT_QROWS = 512
ATT_KROWS = 512
ATT_HEADS = 2

WEIGHTS = ['mix_norm', 'ret_w_in', 'ret_gn', 'ret_w_out', 'mla_w_in', 'mla_q_a_norm', 'mla_kv_a_norm',
           'mla_w_uq', 'mla_w_ukv', 'mla_q_norm', 'mla_k_norm', 'mla_w_out', 'mlp_norm', 'mlp_w1', 'mlp_w2',
           'ple_norm', 'ple_gate_w', 'ple_proj_w']
BIG = ['ret_w_in', 'ret_w_out', 'mla_w_in', 'mla_w_uq', 'mla_w_ukv', 'mla_w_out', 'mlp_w1', 'mlp_w2',
       'ple_gate_w', 'ple_proj_w']
SMALL = [w for w in WEIGHTS if w not in BIG]


def _cparams(sem=None):
    return pltpu.CompilerParams(dimension_semantics=sem, vmem_limit_bytes=VMEM_LIMIT)


def _dot(a, b, ca, cb):
    return lax.dot_general(a, b, (((ca,), (cb,)), ((), ())), preferred_element_type=F32)


def _bf(v):
    return v if v.dtype == BF16 else v.astype(BF16)


def _sigmoid(z):
    return 1.0 / (1.0 + jnp.exp(-z))


def _mm(name, grid, a, a_spec, b, b_spec, contract, outs, extras=(), epi=None, deps=(), split=None):
    nk = grid[2]
    n_ex, n_out, n_dep = len(extras), len(outs), len(deps)
    acc_shape = tuple(d for d in outs[0][1].block_shape if d is not None)
    if split is not None:
        acc_shape = (acc_shape[1], acc_shape[0] * split)

    def body(*refs):
        a_ref, b_ref = refs[:2]
        ex_refs = refs[2:2 + n_ex]
        out_refs = refs[2 + n_ex + n_dep:2 + n_ex + n_dep + n_out]

        def product():
            return _dot(_bf(a_ref[...]), _bf(b_ref[...]), contract[0], contract[1])

        def finish(acc):
            if split is not None:
                for j in range(acc_shape[1] // split):
                    out_refs[0][j] = acc[:, j * split:(j + 1) * split].astype(out_refs[0].dtype)
                return
            acc = acc[...]
            res = epi(acc, *[r[...] for r in ex_refs]) if epi is not None else (acc,)
            for o, r in zip(out_refs, res):
                o[...] = r.astype(o.dtype)

        if nk == 1:
            finish(product())
        else:
            acc_ref = refs[-1]
            k = pl.program_id(2)

            @pl.when(k == 0)
            def _():
                acc_ref[...] = jnp.zeros_like(acc_ref)

            acc_ref[...] += product()

            @pl.when(k == nk - 1)
            def _():
                finish(acc_ref)

    return pl.pallas_call(
        body, name=name, grid=grid,
        in_specs=[a_spec, b_spec] + [s for _, s in extras] + [ANY] * n_dep,
        out_specs=[s for _, s in outs],
        out_shape=[s for s, _ in outs],
        scratch_shapes=[pltpu.VMEM(acc_shape, F32)] if nk > 1 else [],
        compiler_params=_cparams(("parallel", "parallel", "arbitrary")),
    )(a, b, *[x for x, _ in extras], *deps)


def _sds(shape, dtype):
    return jax.ShapeDtypeStruct(shape, dtype)


def _row_tile(t, cap=ROW_TILE):
    return min(cap, t)


def _rms_fwd(name, x, g):
    t, d = x.shape
    tm = _row_tile(t)

    def body(x_ref, g_ref, o_ref):
        xv = x_ref[...]
        r = lax.rsqrt(jnp.mean(xv * xv, axis=-1, keepdims=True) + EPS)
        o_ref[...] = (xv * r * g_ref[...]).astype(o_ref.dtype)

    return pl.pallas_call(
        body, name=name, grid=(t // tm,),
        in_specs=[pl.BlockSpec((tm, d), lambda i: (i, 0)), pl.BlockSpec((1, d), lambda i: (0, 0))],
        out_specs=pl.BlockSpec((tm, d), lambda i: (i, 0)),
        out_shape=_sds((t, d), BF16),
        compiler_params=_cparams(("parallel",)),
    )(x, g)


def _rms_bwd_rows(dy, xv, g, n):
    r = lax.rsqrt(jnp.sum(xv * xv, axis=-1, keepdims=True) / n + EPS)
    xh = xv * r
    dxh = dy * g
    dx = r * (dxh - xh * (jnp.sum(dxh * xh, axis=-1, keepdims=True) / n))
    return dx, dy * xh


def _rms_bwd(name, dy, x, g, res):
    t, d = x.shape
    tm = _row_tile(t, 512)

    def body(dy_ref, x_ref, g_ref, res_ref, dx_ref, dg_ref):
        @pl.when(pl.program_id(0) == 0)
        def _():
            dg_ref[...] = jnp.zeros_like(dg_ref)

        dx, dgr = _rms_bwd_rows(dy_ref[...], x_ref[...], g_ref[...], d)
        dx_ref[...] = res_ref[...] + dx
        dg_ref[...] += jnp.sum(dgr, axis=0, keepdims=True)

    row = pl.BlockSpec((tm, d), lambda i: (i, 0))
    vec = pl.BlockSpec((1, d), lambda i: (0, 0))
    return pl.pallas_call(
        body, name=name, grid=(t // tm,),
        in_specs=[row, row, vec, row], out_specs=[row, vec],
        out_shape=[_sds((t, d), F32), _sds((1, d), F32)],
        compiler_params=_cparams(("arbitrary",)),
    )(dy, x, g, res)


def _loss_head(y, target):
    t, d = y.shape
    tm = _row_tile(t)

    def body(y_ref, t_ref, dy_ref, l_ref):
        @pl.when(pl.program_id(0) == 0)
        def _():
            l_ref[...] = jnp.zeros_like(l_ref)

        e = y_ref[...] - t_ref[...]
        dy_ref[...] = e / d
        l_ref[...] += jnp.sum(jnp.sum(e * e, axis=-1, keepdims=True), axis=0, keepdims=True)

    row = pl.BlockSpec((tm, d), lambda i: (i, 0))
    return pl.pallas_call(
        body, name="loss_head", grid=(t // tm,),
        in_specs=[row, row], out_specs=[row, pl.BlockSpec((8, 128), lambda i: (0, 0))],
        out_shape=[_sds((t, d), F32), _sds((8, 128), F32)],
        compiler_params=_cparams(("arbitrary",)),
    )(y, target)


def _ple_gate_bwd(name, dh, gate, e):
    t, d = dh.shape
    tm = _row_tile(t)

    def body(dh_ref, g_ref, e_ref, de_ref, dz_ref):
        dh_v, gt = dh_ref[...], g_ref[...]
        de_ref[...] = (dh_v * gt).astype(BF16)
        dz_ref[...] = (dh_v * e_ref[...] * (gt * (1.0 - gt))).astype(BF16)

    row = pl.BlockSpec((tm, d), lambda i: (i, 0))
    return pl.pallas_call(
        body, name=name, grid=(t // tm,), in_specs=[row, row, row], out_specs=[row, row],
        out_shape=[_sds((t, d), BF16), _sds((t, d), BF16)],
        compiler_params=_cparams(("parallel",)),
    )(dh, gate, e)


def _rope_half(v, cos, sin):
    half = v.shape[-1] // 2
    v1, v2 = v[:, :half], v[:, half:]
    return jnp.concatenate([v1 * cos - v2 * sin, v2 * cos + v1 * sin], axis=-1)


def _ret_consts():
    lg = jnp.log(1.0 - 2.0 ** (-5.0 - jnp.arange(RET_HEADS, dtype=F32)))
    idx = jnp.arange(CHUNK, dtype=F32)
    intra = jnp.exp(lg[:, None, None] * jnp.abs(idx[:, None] - idx[None, :]))
    qdec = jnp.exp(lg[:, None] * (idx + 1.0))
    kdec = jnp.exp(lg[:, None] * (CHUNK - 1.0 - idx))
    cdec = jnp.exp(lg * CHUNK)
    qdec = jnp.broadcast_to(qdec[:, :, None], (RET_HEADS, CHUNK, RET_DK))
    kdec = jnp.broadcast_to(kdec[:, :, None], (RET_HEADS, CHUNK, RET_DK))
    cdec = jnp.broadcast_to(cdec[:, None, None], (RET_HEADS, 1, RET_DV))
    return intra, qdec, kdec, cdec


def _ret_specs(rb, rev_nb=None):
    blk = (lambda i: i) if rev_nb is None else (lambda i: rev_nb - 1 - i)
    full = lambda shape: pl.BlockSpec(shape, lambda i: (0,) * len(shape))
    return dict(
        proj=pl.BlockSpec((rb, RET_IN), lambda i: (blk(i), 0)),
        tab=pl.BlockSpec((rb, RET_DK // 2), lambda i: (blk(i), 0)),
        vw=pl.BlockSpec((rb, RET_V_W), lambda i: (blk(i), 0)),
        st=pl.BlockSpec((rb // CHUNK, RET_HEADS, RET_DK, RET_DV), lambda i: (blk(i), 0, 0, 0)),
        gn=full((RET_HEADS, 1, RET_DV)),
        intra=full((RET_HEADS, CHUNK, CHUNK)),
        dec=full((RET_HEADS, CHUNK, RET_DK)),
        cdec=full((RET_HEADS, 1, RET_DV)),
    )


def _ret_fwd(proj, cos, sin, gn):
    t = proj.shape[0]
    rb = min(RET_ROWS, t)
    cpb = rb // CHUNK
    intra, qdec, kdec, cdec = _ret_consts()
    sp = _ret_specs(rb)

    def body(proj_ref, cos_ref, sin_ref, gn_ref, intra_ref, qd_ref, kd_ref, cd_ref,
             gated_ref, outp_ref, st_ref, s_ref):
        @pl.when(pl.program_id(0) == 0)
        def _():
            s_ref[...] = jnp.zeros_like(s_ref)

        def chunk(c, carry):
            rows = pl.ds(pl.multiple_of(c * CHUNK, CHUNK), CHUNK)
            cs, sn = cos_ref[rows, :], sin_ref[rows, :]
            for h in range(RET_HEADS):
                q = proj_ref[rows, h * RET_DK:(h + 1) * RET_DK]
                k = proj_ref[rows, RET_QK_W + h * RET_DK:RET_QK_W + (h + 1) * RET_DK]
                v = proj_ref[rows, 2 * RET_QK_W + h * RET_DV:2 * RET_QK_W + (h + 1) * RET_DV]
                g = proj_ref[rows, 2 * RET_QK_W + RET_V_W + h * RET_DV:2 * RET_QK_W + RET_V_W + (h + 1) * RET_DV]
                qr = _rope_half(q, cs, sn)
                kr = _rope_half(k, cs, sn) * (RET_DK ** -0.5)
                qb, kb, vb = qr.astype(BF16), kr.astype(BF16), v.astype(BF16)
                sc = _dot(qb, kb, 1, 1) * intra_ref[h]
                inner = _dot(sc.astype(BF16), vb, 1, 0)
                s_old = s_ref[h]
                sb = s_old.astype(BF16)
                st_ref[c, h] = sb
                cross = _dot((qr * qd_ref[h]).astype(BF16), sb, 1, 0)
                out = inner + cross
                s_ref[h] = s_old * cd_ref[h] + _dot((kr * kd_ref[h]).astype(BF16), vb, 0, 0)
                r = lax.rsqrt(jnp.mean(out * out, axis=-1, keepdims=True) + EPS)
                y = out * r * gn_ref[h]
                cols = slice(h * RET_DV, (h + 1) * RET_DV)
                gated_ref[rows, cols] = (g * _sigmoid(g) * y).astype(BF16)
                outp_ref[rows, cols] = out
            return carry

        lax.fori_loop(0, cpb, chunk, 0)

    return pl.pallas_call(
        body, name="ret_fwd", grid=(t // rb,),
        in_specs=[sp['proj'], sp['tab'], sp['tab'], sp['gn'], sp['intra'], sp['dec'], sp['dec'], sp['cdec']],
        out_specs=[sp['vw'], sp['vw'], sp['st']],
        out_shape=[_sds((t, RET_V_W), BF16), _sds((t, RET_V_W), F32),
                   _sds((t // CHUNK, RET_HEADS, RET_DK, RET_DV), BF16)],
        scratch_shapes=[pltpu.VMEM((RET_HEADS, RET_DK, RET_DV), F32)],
        compiler_params=_cparams(("arbitrary",)),
    )(proj, cos, sin, gn.reshape(RET_HEADS, 1, RET_DV), intra, qdec, kdec, cdec)


def _ret_bwd(proj, cos, sin, gn, outp, states, dgated):
    t = proj.shape[0]
    rb = min(RET_ROWS, t)
    cpb = rb // CHUNK
    nb = t // rb
    intra, qdec, kdec, cdec = _ret_consts()
    sp = _ret_specs(rb, rev_nb=nb)

    def body(proj_ref, cos_ref, sin_ref, gn_ref, intra_ref, qd_ref, kd_ref, cd_ref, outp_ref, st_ref, dgt_ref,
             dproj_ref, dgn_ref, ds_ref):
        @pl.when(pl.program_id(0) == 0)
        def _():
            ds_ref[...] = jnp.zeros_like(ds_ref)
            dgn_ref[...] = jnp.zeros_like(dgn_ref)

        def chunk(cc, carry):
            c = cpb - 1 - cc
            rows = pl.ds(pl.multiple_of(c * CHUNK, CHUNK), CHUNK)
            cs, sn = cos_ref[rows, :], sin_ref[rows, :]
            for h in range(RET_HEADS):
                q = proj_ref[rows, h * RET_DK:(h + 1) * RET_DK]
                k = proj_ref[rows, RET_QK_W + h * RET_DK:RET_QK_W + (h + 1) * RET_DK]
                v = proj_ref[rows, 2 * RET_QK_W + h * RET_DV:2 * RET_QK_W + (h + 1) * RET_DV]
                g = proj_ref[rows, 2 * RET_QK_W + RET_V_W + h * RET_DV:2 * RET_QK_W + RET_V_W + (h + 1) * RET_DV]
                cols = slice(h * RET_DV, (h + 1) * RET_DV)
                qr = _rope_half(q, cs, sn)
                kr = _rope_half(k, cs, sn) * (RET_DK ** -0.5)
                qb, kb, vb = qr.astype(BF16), kr.astype(BF16), v.astype(BF16)
                qdb = (qr * qd_ref[h]).astype(BF16)
                kdb = (kr * kd_ref[h]).astype(BF16)
                out = outp_ref[rows, cols]
                dgt = dgt_ref[rows, cols]
                gnh = gn_ref[h]
                r = lax.rsqrt(jnp.mean(out * out, axis=-1, keepdims=True) + EPS)
                xh = out * r
                sg = _sigmoid(g)
                dgate = dgt * (xh * gnh) * (sg * (1.0 + g * (1.0 - sg)))
                dy = dgt * (g * sg)
                dgn_ref[h] += jnp.sum(dy * xh, axis=0, keepdims=True)
                dxh = dy * gnh
                dout = r * (dxh - xh * jnp.mean(dxh * xh, axis=-1, keepdims=True))
                doutb = dout.astype(BF16)
                itr = intra_ref[h]
                pb = (_dot(qb, kb, 1, 1) * itr).astype(BF16)
                dv = _dot(pb, doutb, 0, 0)
                dsc = (_dot(doutb, vb, 1, 1) * itr).astype(BF16)
                dq = _dot(dsc, kb, 1, 0)
                dk = _dot(dsc, qb, 0, 0)
                dq = dq + _dot(doutb, st_ref[c, h], 1, 1) * qd_ref[h]
                ds_new = ds_ref[h]
                dsb = ds_new.astype(BF16)
                dk = dk + _dot(vb, dsb, 1, 1) * kd_ref[h]
                dv = dv + _dot(kdb, dsb, 1, 0)
                ds_ref[h] = ds_new * cd_ref[h] + _dot(qdb, doutb, 0, 0)
                dproj_ref[rows, h * RET_DK:(h + 1) * RET_DK] = _rope_half(dq, cs, -sn).astype(BF16)
                dproj_ref[rows, RET_QK_W + h * RET_DK:RET_QK_W + (h + 1) * RET_DK] = (
                    _rope_half(dk * (RET_DK ** -0.5), cs, -sn).astype(BF16))
                dproj_ref[rows, 2 * RET_QK_W + h * RET_DV:2 * RET_QK_W + (h + 1) * RET_DV] = dv.astype(BF16)
                dproj_ref[rows, 2 * RET_QK_W + RET_V_W + h * RET_DV:
                          2 * RET_QK_W + RET_V_W + (h + 1) * RET_DV] = dgate.astype(BF16)
            return carry

        lax.fori_loop(0, cpb, chunk, 0)

    return pl.pallas_call(
        body, name="ret_bwd", grid=(nb,),
        in_specs=[sp['proj'], sp['tab'], sp['tab'], sp['gn'], sp['intra'], sp['dec'], sp['dec'], sp['cdec'],
                  sp['vw'], sp['st'], sp['vw']],
        out_specs=[sp['proj'], sp['gn']],
        out_shape=[_sds((t, RET_IN), BF16), _sds((RET_HEADS, 1, RET_DV), F32)],
        scratch_shapes=[pltpu.VMEM((RET_HEADS, RET_DK, RET_DV), F32)],
        compiler_params=_cparams(("arbitrary",)),
    )(proj, cos, sin, gn.reshape(RET_HEADS, 1, RET_DV), intra, qdec, kdec, cdec, outp, states, dgated)


def _mla_tables(t):
    half = MLA_ROPE // 2
    inv = 1.0 / (ROPE_THETA ** (jnp.arange(0, MLA_ROPE, 2, dtype=F32) / MLA_ROPE))
    ang = jnp.arange(t, dtype=F32)[:, None] * inv[None, :]
    cos, sin = jnp.cos(ang), jnp.sin(ang)
    z = jnp.zeros((t, half), F32)
    c = jnp.concatenate([cos, cos, z, z], axis=1)
    s1 = jnp.concatenate([-sin, z, z, z], axis=1)
    s2 = jnp.concatenate([z, sin, z, z], axis=1)
    return c, s1, s2


def _rope_tile(r, c, s1, s2):
    return r * c + pltpu.roll(r, 96, 1) * s1 + pltpu.roll(r, 32, 1) * s2


def _mla_mid(proj2, qa, kva):
    t = proj2.shape[0]
    tm = _row_tile(t)

    def body(p_ref, qa_ref, kva_ref, cq_ref, ckv_ref):
        cq = p_ref[:, :MLA_Q_RANK]
        ckv = p_ref[:, MLA_Q_RANK:MLA_Q_RANK + MLA_KV_RANK]
        rq = lax.rsqrt(jnp.mean(cq * cq, axis=-1, keepdims=True) + EPS)
        rkv = lax.rsqrt(jnp.mean(ckv * ckv, axis=-1, keepdims=True) + EPS)
        cq_ref[...] = (cq * rq * qa_ref[...]).astype(BF16)
        ckv_ref[...] = (ckv * rkv * kva_ref[...]).astype(BF16)

    return pl.pallas_call(
        body, name="mla_mid", grid=(t // tm,),
        in_specs=[pl.BlockSpec((tm, MLA_IN_PAD), lambda i: (i, 0)),
                  pl.BlockSpec((1, MLA_Q_RANK), lambda i: (0, 0)),
                  pl.BlockSpec((1, MLA_KV_RANK), lambda i: (0, 0))],
        out_specs=[pl.BlockSpec((tm, MLA_Q_RANK), lambda i: (i, 0)),
                   pl.BlockSpec((tm, MLA_KV_RANK), lambda i: (i, 0))],
        out_shape=[_sds((t, MLA_Q_RANK), BF16), _sds((t, MLA_KV_RANK), BF16)],
        compiler_params=_cparams(("parallel",)),
    )(proj2, qa, kva)


def _mla_mid_bwd(proj2, qa, kva, dcq, dckv, dkr):
    t = proj2.shape[0]
    tm = _row_tile(t)

    def body(p_ref, qa_ref, kva_ref, dcq_ref, dckv_ref, dkr_ref, dp_ref, dqa_ref, dkva_ref):
        @pl.when(pl.program_id(0) == 0)
        def _():
            dqa_ref[...] = jnp.zeros_like(dqa_ref)
            dkva_ref[...] = jnp.zeros_like(dkva_ref)

        dxq, dgq = _rms_bwd_rows(dcq_ref[...], p_ref[:, :MLA_Q_RANK], qa_ref[...], MLA_Q_RANK)
        dxk, dgk = _rms_bwd_rows(dckv_ref[...], p_ref[:, MLA_Q_RANK:MLA_Q_RANK + MLA_KV_RANK], kva_ref[...],
                                 MLA_KV_RANK)
        dp_ref[:, :MLA_Q_RANK] = dxq.astype(BF16)
        dp_ref[:, MLA_Q_RANK:MLA_Q_RANK + MLA_KV_RANK] = dxk.astype(BF16)
        dp_ref[:, MLA_Q_RANK + MLA_KV_RANK:] = dkr_ref[...].astype(BF16)
        dqa_ref[...] += jnp.sum(dgq, axis=0, keepdims=True)
        dkva_ref[...] += jnp.sum(dgk, axis=0, keepdims=True)

    return pl.pallas_call(
        body, name="mla_mid_bwd", grid=(t // tm,),
        in_specs=[pl.BlockSpec((tm, MLA_IN_PAD), lambda i: (i, 0)),
                  pl.BlockSpec((1, MLA_Q_RANK), lambda i: (0, 0)),
                  pl.BlockSpec((1, MLA_KV_RANK), lambda i: (0, 0)),
                  pl.BlockSpec((tm, MLA_Q_RANK), lambda i: (i, 0)),
                  pl.BlockSpec((tm, MLA_KV_RANK), lambda i: (i, 0)),
                  pl.BlockSpec((tm, 128), lambda i: (i, 0))],
        out_specs=[pl.BlockSpec((tm, MLA_IN_PAD), lambda i: (i, 0)),
                   pl.BlockSpec((1, MLA_Q_RANK), lambda i: (0, 0)),
                   pl.BlockSpec((1, MLA_KV_RANK), lambda i: (0, 0))],
        out_shape=[_sds((t, MLA_IN_PAD), BF16), _sds((1, MLA_Q_RANK), F32), _sds((1, MLA_KV_RANK), F32)],
        compiler_params=_cparams(("arbitrary",)),
    )(proj2, qa, kva, dcq, dckv, dkr)


def _mla_prep_specs(t, tm):
    head = lambda w: pl.BlockSpec((None, tm, w), lambda i, h: (h, i, 0))
    return dict(
        head256=head(MLA_HD_PAD), head128=head(MLA_VD),
        cols256=pl.BlockSpec((tm, MLA_HD_PAD), lambda i, h: (i, h)),
        kr=pl.BlockSpec((tm, 128), lambda i, h: (i, (MLA_Q_RANK + MLA_KV_RANK) // 128)),
        gain=pl.BlockSpec((1, MLA_HD_PAD), lambda i, h: (0, 0)),
        tab=pl.BlockSpec((tm, 128), lambda i, h: (i, 0)),
    )


def _mla_prep(q, kv, proj2, gq, gk, tabs):
    t = q.shape[1]
    tm = _row_tile(t)
    sp = _mla_prep_specs(t, tm)

    def body(q_ref, kv_ref, kr_ref, gq_ref, gk_ref, c_ref, s1_ref, s2_ref, qh_ref, kh_ref, vh_ref):
        c, s1, s2 = c_ref[...], s1_ref[...], s2_ref[...]

        def norm_rope(xv, gain):
            r = lax.rsqrt(jnp.sum(xv * xv, axis=-1, keepdims=True) / MLA_QKD + EPS)
            y = xv * r * gain
            return jnp.concatenate([y[:, :MLA_NOPE], _rope_tile(y[:, MLA_NOPE:], c, s1, s2)], axis=-1)

        kvv = kv_ref[...]
        qh_ref[...] = norm_rope(q_ref[...], gq_ref[...]).astype(BF16)
        kf = jnp.concatenate([kvv[:, :MLA_NOPE], kr_ref[...]], axis=-1)
        kh_ref[...] = norm_rope(kf, gk_ref[...]).astype(BF16)
        vh_ref[...] = jnp.concatenate([kvv[:, MLA_NOPE:], jnp.ones((tm, MLA_VD), F32)], axis=-1).astype(BF16)

    return pl.pallas_call(
        body, name="mla_prep", grid=(t // tm, MLA_HEADS),
        in_specs=[sp['head256'], sp['head256'], sp['kr'], sp['gain'], sp['gain'], sp['tab'], sp['tab'], sp['tab']],
        out_specs=[sp['head256'], sp['head256'], sp['head256']],
        out_shape=[_sds((MLA_HEADS, t, MLA_HD_PAD), BF16), _sds((MLA_HEADS, t, MLA_HD_PAD), BF16),
                   _sds((MLA_HEADS, t, 2 * MLA_VD), BF16)],
        compiler_params=_cparams(("parallel", "arbitrary")),
    )(q, kv, proj2, gq, gk, *tabs)


def _mla_prep_bwd(q, kv, proj2, gq, gk, tabs, dqt, dkh, dvh):
    t = q.shape[1]
    tm = _row_tile(t)
    ab = dqt.shape[-1]
    sp = _mla_prep_specs(t, tm)

    def body(q_ref, kv_ref, kr_ref, gq_ref, gk_ref, c_ref, s1_ref, s2_ref, dqt_ref, dkh_ref, dvh_ref,
             dq_ref, dkv_ref, dkr_ref, dgq_ref, dgk_ref):
        dqh = jnp.concatenate([dqt_ref[b].T for b in range(tm // ab)], axis=0)
        i, h = pl.program_id(0), pl.program_id(1)

        @pl.when((i == 0) & (h == 0))
        def _():
            dgq_ref[...] = jnp.zeros_like(dgq_ref)
            dgk_ref[...] = jnp.zeros_like(dgk_ref)

        @pl.when(h == 0)
        def _():
            dkr_ref[...] = jnp.zeros_like(dkr_ref)

        c, s1, s2 = c_ref[...], s1_ref[...], s2_ref[...]

        def back(xv, gain, dout):
            dy = jnp.concatenate([dout[:, :MLA_NOPE], _rope_tile(dout[:, MLA_NOPE:], c, -s1, -s2)], axis=-1)
            return _rms_bwd_rows(dy, xv, gain, MLA_QKD)

        kvv = kv_ref[...]
        dxq, dgq = back(q_ref[...], gq_ref[...], dqh)
        kf = jnp.concatenate([kvv[:, :MLA_NOPE], kr_ref[...]], axis=-1)
        dxk, dgk = back(kf, gk_ref[...], dkh_ref[...])
        dq_ref[...] = dxq.astype(BF16)
        dkv_ref[...] = jnp.concatenate([dxk[:, :MLA_NOPE], dvh_ref[...]], axis=-1).astype(BF16)
        dkr_ref[...] += dxk[:, MLA_NOPE:]
        dgq_ref[...] += jnp.sum(dgq, axis=0, keepdims=True)
        dgk_ref[...] += jnp.sum(dgk, axis=0, keepdims=True)

    return pl.pallas_call(
        body, name="mla_prep_bwd", grid=(t // tm, MLA_HEADS),
        in_specs=[sp['head256'], sp['head256'], sp['kr'], sp['gain'], sp['gain'], sp['tab'], sp['tab'], sp['tab'],
                  pl.BlockSpec((None, tm // ab, MLA_HD_PAD, ab), lambda i, h: (h, i, 0, 0)),
                  sp['head256'], sp['head128']],
        out_specs=[sp['cols256'], sp['cols256'], sp['tab'], sp['gain'], sp['gain']],
        out_shape=[_sds((t, MLA_HEADS * MLA_HD_PAD), BF16), _sds((t, MLA_HEADS * MLA_HD_PAD), BF16),
                   _sds((t, 128), F32), _sds((1, MLA_HD_PAD), F32), _sds((1, MLA_HD_PAD), F32)],
        compiler_params=_cparams(("arbitrary", "arbitrary")),
    )(q, kv, proj2, gq, gk, *tabs, dqt, dkh, dvh)


def _chunk_visible(rows, cols, row_off, col_off):
    rq = lax.shift_right_logical(lax.broadcasted_iota(jnp.int32, (rows, cols), 0) + row_off, 6)
    ck = lax.shift_right_logical(lax.broadcasted_iota(jnp.int32, (rows, cols), 1) + col_off, 6)
    return ck <= rq


def _rows_to_lanes(col):
    return col.T[:8, :]


def _attn_fwd(qh, kh, vh):
    t = qh.shape[1]
    ab = min(ATT_BLOCK, t)
    tq = min(ATT_QROWS, t)
    r = tq // ab
    hg = ATT_HEADS

    def body(q_ref, k_ref, v_ref, o_ref, lse_ref):
        n_un = pl.program_id(1) * r

        def step(b, state, diag):
            rows = pl.ds(pl.multiple_of(b * ab, ab), ab)
            ms, accs = [], []
            for hh in range(hg):
                m, acc = state[0][hh], state[1][hh]
                s = _dot(q_ref[hh], k_ref[hh, rows, :], 1, 1)
                if diag is not None:
                    s = jnp.where(_chunk_visible(tq, ab, 0, diag * ab), s, -1e30)
                m_new = jnp.maximum(m, jnp.max(s, axis=-1, keepdims=True))
                p = jnp.exp2((s - m_new) * ATT_EXP2).astype(BF16)
                accs.append(jnp.exp2((m - m_new) * ATT_EXP2) * acc + _dot(p, v_ref[hh, rows, :], 1, 0))
                ms.append(m_new)
            return tuple(ms), tuple(accs)

        heads = lambda v: tuple(v for _ in range(hg))
        state = (heads(jnp.full((tq, 1), -1e30, F32)), heads(jnp.zeros((tq, 2 * MLA_VD), F32)))
        state = lax.fori_loop(0, n_un, lambda b, st: step(b, st, None), state)
        for d in range(r):
            state = step(n_un + d, state, d)
        ms, accs = state
        for hh in range(hg):
            l = accs[hh][:, MLA_VD:]
            o_ref[:, hh * MLA_VD:(hh + 1) * MLA_VD] = accs[hh][:, :MLA_VD] / l
            lse_t = _rows_to_lanes(ms[hh] * ATT_EXP2 + jnp.log(l) * LOG2E)
            for d in range(r):
                lse_ref[hh, d] = lse_t[:, d * ab:(d + 1) * ab]

    return pl.pallas_call(
        body, name="mla_attn", grid=(MLA_HEADS // hg, t // tq),
        in_specs=[pl.BlockSpec((hg, tq, MLA_HD_PAD), lambda g, i: (g, i, 0)),
                  pl.BlockSpec((hg, t, MLA_HD_PAD), lambda g, i: (g, 0, 0)),
                  pl.BlockSpec((hg, t, 2 * MLA_VD), lambda g, i: (g, 0, 0))],
        out_specs=[pl.BlockSpec((tq, hg * MLA_VD), lambda g, i: (i, g)),
                   pl.BlockSpec((hg, r, 8, ab), lambda g, i: (g, i, 0, 0))],
        out_shape=[_sds((t, MLA_HEADS * MLA_VD), F32), _sds((MLA_HEADS, t // ab, 8, ab), F32)],
        compiler_params=_cparams(("parallel", "arbitrary")),
    )(qh, kh, vh)


def _attn_delta(do, o, ab):
    t = do.shape[0]
    tm = _row_tile(t)

    def body(do_ref, o_ref, d_ref):
        d = jnp.sum(do_ref[...] * o_ref[...], axis=-1, keepdims=True)
        d_t = _rows_to_lanes(jnp.broadcast_to(d, (tm, 128)))
        for b in range(tm // ab):
            d_ref[b] = d_t[:, b * ab:(b + 1) * ab]

    col = pl.BlockSpec((tm, MLA_VD), lambda i, h: (i, h))
    return pl.pallas_call(
        body, name="mla_delta", grid=(t // tm, MLA_HEADS), in_specs=[col, col],
        out_specs=pl.BlockSpec((None, tm // ab, 8, ab), lambda i, h: (h, i, 0, 0)),
        out_shape=_sds((MLA_HEADS, t // ab, 8, ab), F32),
        compiler_params=_cparams(("parallel", "parallel")),
    )(do, o)


def _attn_bwd(qh, kh, vh, dob, lse_t, dl_t):
    t = qh.shape[1]
    ab = min(ATT_BLOCK, t)
    kb = min(ATT_KROWS, t)
    r = kb // ab
    nq = t // ab
    hg = ATT_HEADS

    def body(q_ref, k_ref, v_ref, do_ref, lse_ref, dl_ref, dqt_ref, dk_ref, dv_ref):
        j = pl.program_id(1)

        @pl.when(j == 0)
        def _():
            dqt_ref[...] = jnp.zeros_like(dqt_ref)

        ks = [k_ref[hh] for hh in range(hg)]
        vs = [v_ref[hh, :, :MLA_VD] for hh in range(hg)]
        kts = [k.T for k in ks]

        def step(b, grads, diag):
            rows = pl.ds(pl.multiple_of(b * ab, ab), ab)
            out = []
            for hh in range(hg):
                dk, dv = grads[hh]
                q = q_ref[hh, rows, :]
                do = do_ref[rows, hh * MLA_VD:(hh + 1) * MLA_VD]
                s_t = _dot(ks[hh], q, 1, 1)
                if diag is not None:
                    key_chunk = lax.shift_right_logical(lax.broadcasted_iota(jnp.int32, (kb, ab), 0), 6)
                    query_chunk = lax.shift_right_logical(
                        lax.broadcasted_iota(jnp.int32, (kb, ab), 1) + diag * ab, 6)
                    s_t = jnp.where(key_chunk <= query_chunk, s_t, -1e30)
                p_t = jnp.exp2(s_t * ATT_EXP2 - lse_ref[hh, b][0:1, :])
                dp_t = _dot(vs[hh], do, 1, 1)
                ds_t = (p_t * (dp_t - dl_ref[hh, b][0:1, :]) * ATT_SCALE).astype(BF16)
                dqt_ref[hh, b] += _dot(kts[hh], ds_t, 1, 0)
                out.append((dk + _dot(ds_t, q, 1, 0), dv + _dot(p_t.astype(BF16), do, 1, 0)))
            return tuple(out)

        grads = tuple((jnp.zeros((kb, MLA_HD_PAD), F32), jnp.zeros((kb, MLA_VD), F32)) for _ in range(hg))
        for d in range(r):
            grads = step(j * r + d, grads, d)
        grads = lax.fori_loop((j + 1) * r, nq, lambda b, g: step(b, g, None), grads)
        for hh in range(hg):
            dk_ref[hh] = grads[hh][0]
            dv_ref[hh] = grads[hh][1]

    whole = lambda w: pl.BlockSpec((hg, t, w), lambda g, j: (g, 0, 0))
    blk = lambda w: pl.BlockSpec((hg, kb, w), lambda g, j: (g, j, 0))
    stat = pl.BlockSpec((hg, nq, 8, ab), lambda g, j: (g, 0, 0, 0))
    return pl.pallas_call(
        body, name="mla_attn_bwd", grid=(MLA_HEADS // hg, t // kb),
        in_specs=[whole(MLA_HD_PAD), blk(MLA_HD_PAD), blk(2 * MLA_VD),
                  pl.BlockSpec((t, hg * MLA_VD), lambda g, j: (0, g)), stat, stat],
        out_specs=[pl.BlockSpec((hg, nq, MLA_HD_PAD, ab), lambda g, j: (g, 0, 0, 0)), blk(MLA_HD_PAD), blk(MLA_VD)],
        out_shape=[_sds((MLA_HEADS, nq, MLA_HD_PAD, ab), F32), _sds((MLA_HEADS, t, MLA_HD_PAD), F32),
                   _sds((MLA_HEADS, t, MLA_VD), F32)],
        compiler_params=_cparams(("parallel", "arbitrary")),
    )(qh, kh, vh, dob, lse_t, dl_t)


VEC = pl.BlockSpec((1, D_MODEL), lambda i, j, k: (0, 0))


def _residual_epi(next_gain):
    if next_gain is None:
        return [], lambda acc, hv: (acc + hv,)

    def epi(acc, hv, g):
        h_new = acc + hv
        r = lax.rsqrt(jnp.mean(h_new * h_new, axis=-1, keepdims=True) + EPS)
        return h_new, h_new * r * g

    return [(next_gain, VEC)], epi


def _residual_outs(t, row, next_gain):
    outs = [(_sds((t, D_MODEL), F32), row)]
    return outs + ([(_sds((t, D_MODEL), BF16), row)] if next_gain is not None else [])


def _mlp_fwd(l, h, hn, w1g, fetch_w2, next_gain):
    t = h.shape[0]
    tm = _row_tile(t)
    nsh, _, wsh = w1g.shape

    def relu2(acc):
        r = jnp.maximum(acc, 0.0)
        return (r * r,)

    tu = _row_tile(t, 2 * ROW_TILE)
    tile = pl.BlockSpec((tu, wsh), lambda i, j, k: (i, j))
    (u,) = _mm(f"mlp_up{l}", (t // tu, nsh, 1),
               hn, pl.BlockSpec((tu, D_MODEL), lambda i, j, k: (i, 0)),
               w1g, pl.BlockSpec((None, D_MODEL, wsh), lambda i, j, k: (j, 0, 0)), (1, 0),
               [(_sds((t, D_FF), BF16), tile)], epi=relu2)
    w2g = fetch_w2((u,))
    row = pl.BlockSpec((tm, D_MODEL), lambda i, j, k: (i, 0))
    more, epi = _residual_epi(next_gain)
    h2, hn_next = _mm(f"mlp_down{l}", (t // tm, 1, nsh),
                      u, pl.BlockSpec((tm, wsh), lambda i, j, k: (i, k)),
                      w2g, pl.BlockSpec((None, wsh, D_MODEL), lambda i, j, k: (k, 0, 0)), (1, 0),
                      _residual_outs(t, row, next_gain), extras=[(h, row)] + more, epi=epi)
    return h2, hn_next, (h, hn, u, w1g, w2g)


def _norm_bwd_outs(t, tm):
    return [(_sds((t, D_MODEL), F32), pl.BlockSpec((tm, D_MODEL), lambda i, j, k: (i, 0))),
            (_sds((t // tm, 1, D_MODEL), F32), pl.BlockSpec((None, 1, D_MODEL), lambda i, j, k: (i, 0, 0)))]


def _norm_bwd_epi(acc, xv, res, g):
    dx, dgr = _rms_bwd_rows(acc, xv, g, D_MODEL)
    return res + dx, jnp.sum(dgr, axis=0, keepdims=True)


def _mlp_bwd(l, dh, saved, norm_g):
    h, hn, u, w1g, w2g = saved
    t = h.shape[0]
    tm = _row_tile(t)
    nsh, _, wsh = w1g.shape
    tu = _row_tile(t, 2 * ROW_TILE)
    tile = pl.BlockSpec((tu, wsh), lambda i, j, k: (i, j))
    (da,) = _mm(f"mlp_du{l}", (t // tu, nsh, 1),
                dh, pl.BlockSpec((tu, D_MODEL), lambda i, j, k: (i, 0)),
                w2g, pl.BlockSpec((None, wsh, D_MODEL), lambda i, j, k: (j, 0, 0)), (1, 1),
                [(_sds((t, D_FF), BF16), tile)], extras=[(u, tile)],
                epi=lambda acc, uv: (2.0 * jnp.sqrt(uv.astype(F32)) * acc,))
    tw = _row_tile(t, 512)
    (dw2,) = _mm(f"mlp_dw2{l}", (1, 1, t // tw),
                 u, pl.BlockSpec((tw, D_FF), lambda i, j, k: (k, 0)),
                 dh, pl.BlockSpec((tw, D_MODEL), lambda i, j, k: (k, 0)), (0, 0),
                 [(_sds((D_FF, D_MODEL), BF16), pl.BlockSpec((D_FF, D_MODEL), lambda i, j, k: (0, 0)))])
    dw2 = dw2.reshape(nsh, wsh, D_MODEL)
    (dw1,) = _mm(f"mlp_dw1{l}", (1, 1, t // tw),
                 hn, pl.BlockSpec((tw, D_MODEL), lambda i, j, k: (k, 0)),
                 da, pl.BlockSpec((tw, D_FF), lambda i, j, k: (k, 0)), (0, 0),
                 [(_sds((nsh, D_MODEL, wsh), BF16), pl.BlockSpec((nsh, D_MODEL, wsh), lambda i, j, k: (0, 0, 0)))],
                 split=wsh)
    row = pl.BlockSpec((tm, D_MODEL), lambda i, j, k: (i, 0))
    dh_in, dg = _mm(f"mlp_dhn{l}", (t // tm, 1, nsh),
                    da, pl.BlockSpec((tm, wsh), lambda i, j, k: (i, k)),
                    w1g, pl.BlockSpec((None, D_MODEL, wsh), lambda i, j, k: (k, 0, 0)), (1, 1),
                    _norm_bwd_outs(t, tm), extras=[(h, row), (dh, row), (norm_g, VEC)], epi=_norm_bwd_epi)
    return dh_in, jnp.sum(dg, axis=0), dw1, dw2


def _ple_fwd(l, h, hn, p, wg, wp, next_gain):
    t = h.shape[0]
    tm = _row_tile(t, 512)
    row = pl.BlockSpec((tm, D_MODEL), lambda i, j, k: (i, 0))
    full = lambda r: pl.BlockSpec((r, D_MODEL), lambda i, j, k: (0, 0))
    (e,) = _mm(f"ple_proj{l}", (t // tm, 1, 1),
               p, pl.BlockSpec((None, None, tm, PLE_DIM), lambda i, j, k: (l, 0, i, 0)),
               wp, full(PLE_DIM), (1, 0), [(_sds((t, D_MODEL), F32), row)])

    def gate_epi(acc, hv, ev, *gain):
        gt = _sigmoid(acc)
        h_new = hv + gt * ev
        if not gain:
            return h_new, gt
        r = lax.rsqrt(jnp.mean(h_new * h_new, axis=-1, keepdims=True) + EPS)
        return h_new, gt, h_new * r * gain[0]

    f32_row, bf_row = (_sds((t, D_MODEL), F32), row), (_sds((t, D_MODEL), BF16), row)
    res = _mm(f"ple_gate{l}", (t // tm, 1, 1), hn, row, wg, full(D_MODEL), (1, 0),
              [f32_row, f32_row] + ([bf_row] if next_gain is not None else []),
              extras=[(h, row), (e, row)] + ([(next_gain, VEC)] if next_gain is not None else []), epi=gate_epi)
    h_out, gate = res[0], res[1]
    return h_out, (res[2] if next_gain is not None else None), (h, hn, gate, e)


def _ple_bwd(l, dh, saved, p, norm_g, wg, deps=()):
    h, hn, gate, e = saved
    t = h.shape[0]
    tm = _row_tile(t)
    tk = _row_tile(t, 512)
    de, dz = _ple_gate_bwd(f"ple_gate_bwd{l}", dh, gate, e)
    full = lambda r: pl.BlockSpec((r, D_MODEL), lambda i, j, k: (0, 0))
    rowk = pl.BlockSpec((tk, D_MODEL), lambda i, j, k: (k, 0))
    (dwp,) = _mm(f"ple_dwp{l}", (1, 1, t // tk),
                 p, pl.BlockSpec((None, None, tk, PLE_DIM), lambda i, j, k: (l, 0, k, 0)),
                 de, rowk, (0, 0), [(_sds((PLE_DIM, D_MODEL), BF16), full(PLE_DIM))], deps=deps)
    (dwg,) = _mm(f"ple_dwg{l}", (1, 1, t // tk), hn, rowk, dz, rowk, (0, 0),
                 [(_sds((D_MODEL, D_MODEL), BF16), full(D_MODEL))])
    row = pl.BlockSpec((tm, D_MODEL), lambda i, j, k: (i, 0))
    dh_in, dg = _mm(f"ple_dhn{l}", (t // tm, 1, 1), dz, row, wg, full(D_MODEL), (1, 1),
                    _norm_bwd_outs(t, tm), extras=[(h, row), (dh, row), (norm_g, VEC)], epi=_norm_bwd_epi)
    return dh_in, jnp.sum(dg, axis=0), dwg, dwp


def _ret_layer_fwd(x, norm_g, wri, fetch_wro, gn, cos, sin, next_gain, deps=()):
    t = x.shape[0]
    tm = _row_tile(t)
    nsh, _, wsh = wri.shape
    hn = _rms_fwd("mix_norm0", x, norm_g)
    (proj,) = _mm("ret_in", (t // tm, nsh, 1),
                  hn, pl.BlockSpec((tm, D_MODEL), lambda i, j, k: (i, 0)),
                  wri, pl.BlockSpec((None, D_MODEL, wsh), lambda i, j, k: (j, 0, 0)), (1, 0),
                  [(_sds((t, RET_IN), F32), pl.BlockSpec((tm, wsh), lambda i, j, k: (i, j)))], deps=deps)
    gated, outp, states = _ret_fwd(proj, cos, sin, gn)
    wro = fetch_wro((gated,))
    row = pl.BlockSpec((tm, D_MODEL), lambda i, j, k: (i, 0))
    kt = 512
    more, epi = _residual_epi(next_gain)
    h1, hn_next = _mm("ret_out", (t // tm, 1, RET_V_W // kt),
                      gated, pl.BlockSpec((tm, kt), lambda i, j, k: (i, k)),
                      wro, pl.BlockSpec((kt, D_MODEL), lambda i, j, k: (k, 0)), (1, 0),
                      _residual_outs(t, row, next_gain), extras=[(x, row)] + more, epi=epi)
    return h1, hn_next, (x, hn, proj, gated, outp, states, wro)


def _ret_layer_bwd(dh, saved, norm_g, wri, gn, cos, sin, emit, deps=()):
    x, hn, proj, gated, outp, states, wro = saved
    t = x.shape[0]
    tm = _row_tile(t)
    tk = _row_tile(t, 512)
    nsh, _, wsh = wri.shape
    (dgated,) = _mm("ret_dgated", (t // tm, RET_V_W // D_MODEL, 1),
                    dh, pl.BlockSpec((tm, D_MODEL), lambda i, j, k: (i, 0)),
                    wro, pl.BlockSpec((D_MODEL, D_MODEL), lambda i, j, k: (j, 0)), (1, 1),
                    [(_sds((t, RET_V_W), F32), pl.BlockSpec((tm, D_MODEL), lambda i, j, k: (i, j)))], deps=deps)
    (dwro,) = _mm("ret_dwro", (1, 1, t // tk),
                  gated, pl.BlockSpec((tk, RET_V_W), lambda i, j, k: (k, 0)),
                  dh, pl.BlockSpec((tk, D_MODEL), lambda i, j, k: (k, 0)), (0, 0),
                  [(_sds((RET_V_W, D_MODEL), BF16), pl.BlockSpec((RET_V_W, D_MODEL), lambda i, j, k: (0, 0)))])
    dproj, dgn = _ret_bwd(proj, cos, sin, gn, outp, states, dgated)
    half = nsh // 2
    (dwri,) = _mm("ret_dwri", (2, 1, t // tk),
                  hn, pl.BlockSpec((tk, D_MODEL), lambda i, j, k: (k, 0)),
                  dproj, pl.BlockSpec((tk, half * wsh), lambda i, j, k: (k, i)), (0, 0),
                  [(_sds((nsh, D_MODEL, wsh), BF16), pl.BlockSpec((half, D_MODEL, wsh), lambda i, j, k: (i, 0, 0)))],
                  split=wsh)
    deps = emit(dwro, dwri)
    row = pl.BlockSpec((tm, D_MODEL), lambda i, j, k: (i, 0))
    dx, dg = _mm("ret_dhn", (t // tm, 1, nsh),
                 dproj, pl.BlockSpec((tm, wsh), lambda i, j, k: (i, k)),
                 wri, pl.BlockSpec((None, D_MODEL, wsh), lambda i, j, k: (k, 0, 0)), (1, 1),
                 _norm_bwd_outs(t, tm), extras=[(x, row), (dh, row), (norm_g, VEC)], epi=_norm_bwd_epi, deps=deps)
    return dx, jnp.sum(dg, axis=0), dgn.reshape(RET_HEADS, RET_DV)


def _mla_layer_fwd(h, hn, wmi, qa, kva, wuq, wukv, gq, gk, wmo, tabs, next_gain):
    t = h.shape[0]
    tm = _row_tile(t)
    row = pl.BlockSpec((tm, D_MODEL), lambda i, j, k: (i, 0))
    (proj2,) = _mm("mla_in", (t // tm, 1, 1), hn, row,
                   wmi, pl.BlockSpec((D_MODEL, MLA_IN_PAD), lambda i, j, k: (0, 0)), (1, 0),
                   [(_sds((t, MLA_IN_PAD), F32), pl.BlockSpec((tm, MLA_IN_PAD), lambda i, j, k: (i, 0)))])
    cq, ckv = _mla_mid(proj2, qa, kva)
    head = pl.BlockSpec((None, tm, MLA_HD_PAD), lambda i, j, k: (j, i, 0))
    (q,) = _mm("mla_uq", (t // tm, MLA_HEADS, 1),
               cq, pl.BlockSpec((tm, MLA_Q_RANK), lambda i, j, k: (i, 0)),
               wuq, pl.BlockSpec((None, MLA_Q_RANK, MLA_HD_PAD), lambda i, j, k: (j, 0, 0)), (1, 0),
               [(_sds((MLA_HEADS, t, MLA_HD_PAD), F32), head)])
    (kv,) = _mm("mla_ukv", (t // tm, MLA_HEADS, 1),
                ckv, pl.BlockSpec((tm, MLA_KV_RANK), lambda i, j, k: (i, 0)),
                wukv, pl.BlockSpec((None, MLA_KV_RANK, MLA_HD_PAD), lambda i, j, k: (j, 0, 0)), (1, 0),
                [(_sds((MLA_HEADS, t, MLA_HD_PAD), F32), head)])
    qh, kh, vh = _mla_prep(q, kv, proj2, gq, gk, tabs)
    o, lse = _attn_fwd(qh, kh, vh)
    more, epi = _residual_epi(next_gain)
    h_out, hn_next = _mm("mla_out", (t // tm, 1, 1), o, row,
                         wmo, pl.BlockSpec((D_MODEL, D_MODEL), lambda i, j, k: (0, 0)), (1, 0),
                         _residual_outs(t, row, next_gain), extras=[(h, row)] + more, epi=epi)
    return h_out, hn_next, (h, hn, proj2, cq, ckv, q, kv, qh, kh, vh, o, lse)


def _mla_layer_bwd(dh, saved, norm_g, wmi, qa, kva, wuq, wukv, gq, gk, wmo, tabs, deps=()):
    h, hn, proj2, cq, ckv, q, kv, qh, kh, vh, o, lse = saved
    t = h.shape[0]
    tm = _row_tile(t)
    tk = _row_tile(t, 512)
    row = pl.BlockSpec((tm, D_MODEL), lambda i, j, k: (i, 0))
    rowk = pl.BlockSpec((tk, D_MODEL), lambda i, j, k: (k, 0))
    sq = pl.BlockSpec((D_MODEL, D_MODEL), lambda i, j, k: (0, 0))
    do, dob = _mm("mla_do", (t // tm, 1, 1), dh, row, wmo, sq, (1, 1),
                  [(_sds((t, D_MODEL), F32), row), (_sds((t, D_MODEL), BF16), row)], epi=lambda acc: (acc, acc),
                  deps=deps)
    (dwmo,) = _mm("mla_dwo", (1, 1, t // tk), o, rowk, dh, rowk, (0, 0), [(_sds((D_MODEL, D_MODEL), BF16), sq)])
    delta = _attn_delta(do, o, lse.shape[-1])
    dqt, dkh, dvh = _attn_bwd(qh, kh, vh, dob, lse, delta)
    dq, dkv, dkr, dgq, dgk = _mla_prep_bwd(q, kv, proj2, gq, gk, tabs, dqt, dkh, dvh)

    wide = MLA_HEADS * MLA_HD_PAD
    widek = pl.BlockSpec((tk, wide), lambda i, j, k: (k, 0))
    (dwuq,) = _mm("mla_dwuq", (1, 1, t // tk),
                  cq, pl.BlockSpec((tk, MLA_Q_RANK), lambda i, j, k: (k, 0)), dq, widek, (0, 0),
                  [(_sds((MLA_HEADS, MLA_Q_RANK, MLA_HD_PAD), BF16),
                    pl.BlockSpec((MLA_HEADS, MLA_Q_RANK, MLA_HD_PAD), lambda i, j, k: (0, 0, 0)))], split=MLA_HD_PAD)
    (dwukv,) = _mm("mla_dwukv", (1, 1, t // tk),
                   ckv, pl.BlockSpec((tk, MLA_KV_RANK), lambda i, j, k: (k, 0)), dkv, widek, (0, 0),
                   [(_sds((MLA_HEADS, MLA_KV_RANK, MLA_HD_PAD), BF16),
                     pl.BlockSpec((MLA_HEADS, MLA_KV_RANK, MLA_HD_PAD), lambda i, j, k: (0, 0, 0)))],
                   split=MLA_HD_PAD)
    side_by_side = lambda wg: wg.transpose(1, 0, 2).reshape(wg.shape[1], wide)
    widei = pl.BlockSpec((tm, wide), lambda i, j, k: (i, 0))
    (dcq,) = _mm("mla_dcq", (t // tm, 1, 1), dq, widei,
                 side_by_side(wuq), pl.BlockSpec((MLA_Q_RANK, wide), lambda i, j, k: (0, 0)), (1, 1),
                 [(_sds((t, MLA_Q_RANK), F32), pl.BlockSpec((tm, MLA_Q_RANK), lambda i, j, k: (i, 0)))])
    (dckv,) = _mm("mla_dckv", (t // tm, 1, 1), dkv, widei,
                  side_by_side(wukv), pl.BlockSpec((MLA_KV_RANK, wide), lambda i, j, k: (0, 0)), (1, 1),
                  [(_sds((t, MLA_KV_RANK), F32), pl.BlockSpec((tm, MLA_KV_RANK), lambda i, j, k: (i, 0)))])
    dproj2, dqa, dkva = _mla_mid_bwd(proj2, qa, kva, dcq, dckv, dkr)
    win = pl.BlockSpec((D_MODEL, MLA_IN_PAD), lambda i, j, k: (0, 0))
    (dwmi,) = _mm("mla_dwin", (1, 1, t // tk), hn, rowk,
                  dproj2, pl.BlockSpec((tk, MLA_IN_PAD), lambda i, j, k: (k, 0)), (0, 0),
                  [(_sds((D_MODEL, MLA_IN_PAD), BF16), win)])
    dh_in, dg = _mm("mla_dhn", (t // tm, 1, 1),
                    dproj2, pl.BlockSpec((tm, MLA_IN_PAD), lambda i, j, k: (i, 0)), wmi, win, (1, 1),
                    _norm_bwd_outs(t, tm), extras=[(h, row), (dh, row), (norm_g, VEC)], epi=_norm_bwd_epi)
    return dh_in, dict(mix=jnp.sum(dg, axis=0), wmi=dwmi, qa=dqa, kva=dkva, wuq=dwuq, wukv=dwukv, gq=dgq, gk=dgk,
                       wmo=dwmo)


def _local_step(x, p, target, w, fetch, emit=lambda group: ()):
    t = x.shape[0]
    inv = 1.0 / (ROPE_THETA ** (jnp.arange(0, RET_DK, 2, dtype=F32) / RET_DK))
    ang = jnp.arange(t, dtype=F32)[:, None] * inv[None, :]
    cos_r, sin_r = jnp.cos(ang), jnp.sin(ang)
    tabs = _mla_tables(t)
    row = lambda a, i: a[i:i + 1]

    h1, hn1, s_ret = _ret_layer_fwd(x, row(w['mix_norm'], 0), w['ret_w_in'],
                                    lambda after: fetch('ret_out', after)['ret_w_out'], w['ret_gn'], cos_r, sin_r,
                                    row(w['mlp_norm'], 0), deps=w['deps'])
    h2, hn2, s_mlp0 = _mlp_fwd(0, h1, hn1, fetch('mlp_w1_0', (h1,))['mlp_w1'],
                               lambda after: fetch('mlp_w2_0', after)['mlp_w2'], row(w['ple_norm'], 0))
    w0 = fetch('ple_0', (h2,))
    h3, hn3, s_ple0 = _ple_fwd(0, h2, hn2, p, w0['ple_gate_w'], w0['ple_proj_w'], row(w['mix_norm'], 1))
    wm = fetch('mla', (h3,))
    mla_w = (wm['mla_w_in'], w['mla_q_a_norm'], w['mla_kv_a_norm'], wm['mla_w_uq'], wm['mla_w_ukv'],
             w['mla_q_norm'], w['mla_k_norm'], wm['mla_w_out'], tabs)
    h4, hn4, s_mla = _mla_layer_fwd(h3, hn3, *mla_w, row(w['mlp_norm'], 1))
    w1 = fetch('layer_1', (h4,))
    h5, hn5, s_mlp1 = _mlp_fwd(1, h4, hn4, w1['mlp_w1'], lambda after: w1['mlp_w2'], row(w['ple_norm'], 1))
    y, _, s_ple1 = _ple_fwd(1, h5, hn5, p, w1['ple_gate_w'], w1['ple_proj_w'], None)

    dy, sq_err = _loss_head(y, target)

    n = N_DEV
    colsh = lambda a: a.reshape(a.shape[0], n, a.shape[1] // n).transpose(1, 0, 2)
    rowsh = lambda a: a.reshape(n, a.shape[0] // n, a.shape[1])
    big = {}

    def emit_group(group):
        big.update(group)
        return emit(group)

    dh5, dg_ple1, dwg1, dwp1 = _ple_bwd(1, dy, s_ple1, p, row(w['ple_norm'], 1), w1['ple_gate_w'])
    dh4, dg_mlp1, dw1_1, dw2_1 = _mlp_bwd(1, dh5, s_mlp1, row(w['mlp_norm'], 1))
    deps = emit_group({('ple_gate_w', 1): rowsh(dwg1), ('ple_proj_w', 1): colsh(dwp1),
                       ('mlp_w2', 1): dw2_1, ('mlp_w1', 1): dw1_1})
    dh3, gm = _mla_layer_bwd(dh4, s_mla, row(w['mix_norm'], 1), *mla_w, deps=deps)
    deps = emit_group({('mla_w_out', 0): rowsh(gm['wmo']), ('mla_w_uq', 0): gm['wuq'][:, :, :MLA_QKD],
                       ('mla_w_ukv', 0): gm['wukv'], ('mla_w_in', 0): rowsh(gm['wmi'][:, :MLA_IN])})
    dh2, dg_ple0, dwg0, dwp0 = _ple_bwd(0, dh3, s_ple0, p, row(w['ple_norm'], 0), w0['ple_gate_w'], deps=deps)
    dh1, dg_mlp0, dw1_0, dw2_0 = _mlp_bwd(0, dh2, s_mlp0, row(w['mlp_norm'], 0))
    deps = emit_group({('ple_gate_w', 0): rowsh(dwg0), ('ple_proj_w', 0): colsh(dwp0),
                       ('mlp_w2', 0): dw2_0, ('mlp_w1', 0): dw1_0})
    dx, dg_mix0, dgn = _ret_layer_bwd(
        dh1, s_ret, row(w['mix_norm'], 0), w['ret_w_in'], w['ret_gn'], cos_r, sin_r,
        lambda dwro, dwri: emit_group({('ret_w_out', 0): rowsh(dwro), ('ret_w_in', 0): dwri}), deps=deps)

    small = dict(
        mix_norm=[dg_mix0, gm['mix']], mlp_norm=[dg_mlp0, dg_mlp1], ple_norm=[dg_ple0, dg_ple1],
        ret_gn=dgn, mla_q_a_norm=gm['qa'], mla_kv_a_norm=gm['kva'], mla_q_norm=gm['gq'], mla_k_norm=gm['gk'],
    )
    return sq_err, dx, big, small


def _my_place():
    x, y, c = lax.axis_index("x"), lax.axis_index("y"), lax.axis_index("c")
    return x, y, c


def _flat(px, py, pc):
    return 4 * px + 2 * py + pc


def _peer(x, y, c, r):
    return (1 - x if r & 4 else x, 1 - y if r & 2 else y, 1 - c if r & 1 else c)


def _all_gather(arrays):
    n = len(arrays)

    def body(*refs):
        ins, outs = refs[:n], refs[n:2 * n]
        send_sems, recv_sems, local_sems = refs[2 * n:]
        x, y, c = _my_place()
        me, sibling = (x, y, c), (x, y, 1 - c)
        chips = [(1 - x, y), (x, 1 - y), (1 - x, 1 - y)]

        def copy(a, k, block, to, src=None):
            slot = outs[a].at[_flat(*block)]
            return pltpu.make_async_remote_copy(
                src_ref=slot if src is None else src, dst_ref=slot,
                send_sem=send_sems.at[a, k], recv_sem=recv_sems.at[a, k], device_id=to, device_id_type=MESH)

        mine = [pltpu.make_async_copy(ins[a], outs[a].at[_flat(*me)], local_sems.at[a]) for a in range(n)]
        for cp in mine:
            cp.start()
        first = []
        for a in range(n):
            first.append(copy(a, 0, me, sibling, src=ins[a]))
            first += [copy(a, 1 + j, me, (*chip, c), src=ins[a]) for j, chip in enumerate(chips)]
        for cp in first:
            cp.start()
        passed = []
        for a in range(n):
            for j, chip in enumerate(chips):
                copy(a, 1 + j, (*chip, c), me).wait_recv()
                passed.append(copy(a, 4 + j, (*chip, c), sibling))
                passed[-1].start()
        for a in range(n):
            copy(a, 0, sibling, me).wait_recv()
            for j, chip in enumerate(chips):
                copy(a, 4 + j, (*chip, 1 - c), me).wait_recv()
        for cp in first + passed:
            cp.wait_send()
        for cp in mine:
            cp.wait()

    return pl.pallas_call(
        body, name="all_gather_weights",
        in_specs=[ANY] * n, out_specs=[ANY] * n,
        out_shape=[_sds((N_DEV,) + a.shape, a.dtype) for a in arrays],
        scratch_shapes=[pltpu.SemaphoreType.DMA((n, 7)), pltpu.SemaphoreType.DMA((n, 7)),
                        pltpu.SemaphoreType.DMA((n,))],
    )(*arrays)


HBM = pl.BlockSpec(memory_space=pltpu.HBM)
SEMS = pl.BlockSpec(memory_space=pltpu.SEMAPHORE)
SIDE_EFFECT = pltpu.SideEffectType.DATAFLOW_SIDE_EFFECTING


def _rs_copies(x, y, c, srcs, lands, send_sems, recv_sems):
    copies = []
    for a in range(len(srcs)):
        for r in range(1, N_DEV):
            peer = _peer(x, y, c, r)
            k = a * (N_DEV - 1) + r - 1
            copies.append(pltpu.make_async_remote_copy(
                src_ref=srcs[a].at[_flat(*peer)], dst_ref=lands[a].at[r - 1],
                send_sem=send_sems.at[k], recv_sem=recv_sems.at[k], device_id=peer, device_id_type=MESH))
    return copies


def _rs_start(name, arrays):
    n = len(arrays)
    hbm = lambda a: pltpu.with_memory_space_constraint(a, pltpu.HBM)
    lands = [hbm(lax.empty((N_DEV - 1,) + a.shape[1:], a.dtype)) for a in arrays]

    def body(*refs):
        srcs, lnd = refs[:n], refs[n:2 * n]
        send_sems, recv_sems = refs[2 * n], refs[2 * n + 1]
        token = refs[-1]
        for cp in _rs_copies(*_my_place(), srcs, lnd, send_sems, recv_sems):
            cp.start()
        token[...] = jnp.zeros_like(token)

    outs = pl.pallas_call(
        body, name=name,
        in_specs=[HBM] * (2 * n),
        out_specs=[SEMS, SEMS] + [HBM] * (2 * n) + [pl.BlockSpec(memory_space=pltpu.VMEM)],
        out_shape=[pltpu.SemaphoreType.DMA((n * (N_DEV - 1),)), pltpu.SemaphoreType.DMA((n * (N_DEV - 1),))]
        + [pltpu.HBM(a.shape, a.dtype) for a in arrays] + [pltpu.HBM(l.shape, l.dtype) for l in lands]
        + [_sds((8, 128), F32)],
        input_output_aliases={i: 2 + i for i in range(2 * n)},
        compiler_params=pltpu.CompilerParams(has_side_effects=SIDE_EFFECT),
    )(*[hbm(a) for a in arrays], *lands)
    return outs[0], outs[1], outs[2:2 + n], outs[2 + n:2 + 2 * n], outs[-1]


def _rs_wait(name, send_sems, recv_sems, srcs, lands, after):
    n = len(srcs)

    def body(*refs):
        src_refs, lnd = refs[:n], refs[n:2 * n]
        send, recv = refs[2 * n], refs[2 * n + 1]
        for cp in _rs_copies(*_my_place(), src_refs, lnd, send, recv):
            cp.wait_send()
            cp.wait_recv()

    outs = pl.pallas_call(
        body, name=name,
        in_specs=[HBM] * (2 * n) + [SEMS, SEMS] + [ANY] * len(after),
        out_specs=[HBM] * (2 * n),
        out_shape=[pltpu.HBM(a.shape, a.dtype) for a in list(srcs) + list(lands)],
        input_output_aliases={i: i for i in range(2 * n)},
        compiler_params=pltpu.CompilerParams(has_side_effects=SIDE_EFFECT),
    )(*srcs, *lands, send_sems, recv_sems, *after)
    return outs[:n], outs[n:]


SMALL_PACK_ROWS = 16


def _all_reduce_small(rows, deps=()):
    n = len(rows)

    def body(*refs):
        ins = refs[:n]
        out_ref, mine, buf, send_sems, recv_sems = refs[n + len(deps):]
        x, y, c = _my_place()
        mine[...] = jnp.zeros_like(mine)
        for (r0, a), ref in zip(rows, ins):
            mine[r0:r0 + a.shape[0], 0:a.shape[1]] = ref[...]
        buf[_flat(x, y, c)] = mine[...]
        copies = []
        for r in range(1, N_DEV):
            peer = _peer(x, y, c, r)
            send = pltpu.make_async_remote_copy(
                src_ref=mine, dst_ref=buf.at[_flat(x, y, c)],
                send_sem=send_sems.at[r - 1], recv_sem=recv_sems.at[r - 1], device_id=peer, device_id_type=MESH)
            send.start()
            recv = pltpu.make_async_remote_copy(
                src_ref=mine, dst_ref=buf.at[_flat(*peer)],
                send_sem=send_sems.at[r - 1], recv_sem=recv_sems.at[r - 1], device_id=peer, device_id_type=MESH)
            copies.append((send, recv))
        for send, recv in copies:
            send.wait_send()
            recv.wait_recv()
        acc = buf[0]
        for s in range(1, N_DEV):
            acc = acc + buf[s]
        out_ref[...] = acc

    vm = pl.BlockSpec(memory_space=pltpu.VMEM)
    shape = (SMALL_PACK_ROWS, D_MODEL)
    return pl.pallas_call(
        body, name="all_reduce_small", in_specs=[vm] * n + [ANY] * len(deps), out_specs=vm,
        out_shape=_sds(shape, F32),
        scratch_shapes=[pltpu.VMEM(shape, F32), pltpu.VMEM((N_DEV,) + shape, F32),
                        pltpu.SemaphoreType.DMA((7,)), pltpu.SemaphoreType.DMA((7,))],
    )(*[a for _, a in rows], *deps)


def _adamw_math(w, g, m, v):
    m = ADAM_B1 * m + (1.0 - ADAM_B1) * g
    v = ADAM_B2 * v + (1.0 - ADAM_B2) * (g * g)
    m_hat = m / (1.0 - ADAM_B1 ** ADAM_STEP)
    v_hat = v / (1.0 - ADAM_B2 ** ADAM_STEP)
    delta = -ADAM_LR * (m_hat / (jnp.sqrt(v_hat) + ADAM_EPS) + ADAM_WD * w)
    return delta, m, v


def _adamw_big(name, w, m, v, srcs, lands, me):
    nl, rows, cols = w.shape
    tr = next(cand for cand in (256, 128, 64, 32, 16, 8) if rows % cand == 0)

    def body(me_ref, w_ref, m_ref, v_ref, *rest):
        src_refs, land_refs = rest[:nl], rest[nl:2 * nl]
        g_ref, d_ref, mo_ref, vo_ref = rest[2 * nl:]
        for layer in range(nl):
            @pl.when(pl.program_id(0) == layer)
            def _():
                g = src_refs[layer][...].astype(F32)
                for s in range(N_DEV - 1):
                    g = g + land_refs[layer][s].astype(F32)
                delta, mn, vn = _adamw_math(w_ref[...], g, m_ref[...], v_ref[...])
                g_ref[...] = g
                d_ref[...] = delta
                mo_ref[...] = mn
                vo_ref[...] = vn

    blk = pl.BlockSpec((None, tr, cols), lambda l, i, me_ref: (l, i, 0))
    own = pl.BlockSpec((None, tr, cols), lambda l, i, me_ref: (me_ref[0], i, 0))
    peers = pl.BlockSpec((N_DEV - 1, tr, cols), lambda l, i, me_ref: (0, i, 0))
    return pl.pallas_call(
        body, name=name,
        grid_spec=pltpu.PrefetchScalarGridSpec(
            num_scalar_prefetch=1, grid=(nl, rows // tr),
            in_specs=[blk, blk, blk] + [own] * nl + [peers] * nl, out_specs=[blk] * 4),
        out_shape=[_sds((nl, rows, cols), F32)] * 4,
        compiler_params=_cparams(("arbitrary", "arbitrary")),
    )(me, w, m, v, *srcs, *lands)


def _adamw_small(ws, gs, ms, vs):
    n = len(ws)

    def body(*refs):
        w_refs, g_refs, m_refs, v_refs = (refs[i * n:(i + 1) * n] for i in range(4))
        d_out, m_out, v_out = (refs[(4 + i) * n:(5 + i) * n] for i in range(3))
        for i in range(n):
            delta, mn, vn = _adamw_math(w_refs[i][...], g_refs[i][...], m_refs[i][...], v_refs[i][...])
            d_out[i][...] = delta
            m_out[i][...] = mn
            v_out[i][...] = vn

    vm = pl.BlockSpec(memory_space=pltpu.VMEM)
    outs = pl.pallas_call(
        body, name="adamw_small", in_specs=[vm] * (4 * n), out_specs=[vm] * (3 * n),
        out_shape=[_sds(a.shape, F32) for a in ws] * 3,
    )(*ws, *gs, *ms, *vs)
    return outs[:n], outs[n:2 * n], outs[2 * n:]


SMALL_ROWS = 16


def _pad_to(a, rows, cols):
    return jnp.pad(a, ((0, rows - a.shape[0]), (0, cols - a.shape[1])))


def _place_own(blocks):
    me = _flat(*_my_place())
    return [lax.dynamic_update_slice(lax.empty((N_DEV,) + b.shape, b.dtype), b[None], (me,) + (0,) * b.ndim)
            for b in blocks]


def _ag_copies(x, y, c, blocks, bufs, send_sems, recv_sems):
    sends, recvs = [], []
    for a in range(len(blocks)):
        for r in range(1, N_DEV):
            peer = _peer(x, y, c, r)
            k = a * (N_DEV - 1) + r - 1
            make = lambda place: pltpu.make_async_remote_copy(
                src_ref=blocks[a], dst_ref=bufs[a].at[_flat(*place)],
                send_sem=send_sems.at[k], recv_sem=recv_sems.at[k], device_id=peer, device_id_type=MESH)
            sends.append(make((x, y, c)))
            recvs.append(make(peer))
    return sends, recvs


def _ag_start(groups, after):
    flat = [pair for g in groups for pair in g]
    n, ng = len(flat), len(groups)
    hbm = lambda a: pltpu.with_memory_space_constraint(a, pltpu.HBM)

    def body(*refs):
        blocks, bufs = refs[:n], refs[n:2 * n]
        sems = refs[2 * n + len(after):2 * n + len(after) + 2 * ng]
        x, y, c = _my_place()
        at = 0
        for gi, g in enumerate(groups):
            sends, _ = _ag_copies(x, y, c, blocks[at:at + len(g)], bufs[at:at + len(g)], sems[2 * gi], sems[2 * gi + 1])
            for cp in sends:
                cp.start()
            at += len(g)
        refs[-1][...] = jnp.zeros_like(refs[-1])

    sem_shapes = [pltpu.SemaphoreType.DMA((len(g) * (N_DEV - 1),)) for g in groups for _ in range(2)]
    outs = pl.pallas_call(
        body, name="gather_start",
        in_specs=[HBM] * (2 * n) + [ANY] * len(after),
        out_specs=[SEMS] * (2 * ng) + [HBM] * (2 * n) + [pl.BlockSpec(memory_space=pltpu.VMEM)],
        out_shape=sem_shapes + [pltpu.HBM(b.shape, b.dtype) for b, _ in flat]
        + [pltpu.HBM(u.shape, u.dtype) for _, u in flat] + [_sds((8, 128), F32)],
        input_output_aliases={i: 2 * ng + i for i in range(2 * n)},
        compiler_params=pltpu.CompilerParams(has_side_effects=SIDE_EFFECT),
    )(*[hbm(b) for b, _ in flat], *[hbm(u) for _, u in flat], *after)
    blocks_thru, bufs_thru = outs[2 * ng:2 * ng + n], outs[2 * ng + n:2 * ng + 2 * n]
    started, at = [], 0
    for gi, g in enumerate(groups):
        started.append((outs[2 * gi], outs[2 * gi + 1], blocks_thru[at:at + len(g)], bufs_thru[at:at + len(g)]))
        at += len(g)
    return started, outs[-1]


def _ag_wait(name, send_sems, recv_sems, blocks, bufs, after):
    n = len(blocks)

    def body(*refs):
        sends, recvs = _ag_copies(*_my_place(), refs[:n], refs[n:2 * n], refs[2 * n], refs[2 * n + 1])
        for s, r in zip(sends, recvs):
            s.wait_send()
            r.wait_recv()

    outs = pl.pallas_call(
        body, name=name,
        in_specs=[HBM] * (2 * n) + [SEMS, SEMS] + [ANY] * len(after),
        out_specs=[HBM] * (2 * n),
        out_shape=[pltpu.HBM(a.shape, a.dtype) for a in list(blocks) + list(bufs)],
        input_output_aliases={i: i for i in range(2 * n)},
        compiler_params=pltpu.CompilerParams(has_side_effects=SIDE_EFFECT),
    )(*blocks, *bufs, send_sems, recv_sems, *after)
    return outs[n:]


def _prepare_weights(p):
    n = N_DEV
    bf = lambda a: a.astype(BF16)
    gn_pack = jnp.concatenate([
        _pad_to(p['ret_gn'][0], RET_HEADS, 128), _pad_to(p['mla_q_a_norm'], 1, 128),
        _pad_to(p['mla_kv_a_norm'], 1, 128), jnp.zeros((2, 128), F32)], axis=0)
    ple = lambda l: [bf(p['ple_gate_w'][l]), bf(p['ple_proj_w'][l])]
    names = ('ret_out', 'mlp_w1_0', 'mlp_w2_0', 'ple_0', 'mla', 'layer_1')
    later = [[bf(p['ret_w_out'][0])], [bf(p['mlp_w1'][0])], [bf(p['mlp_w2'][0])], ple(0),
             [bf(p['mla_w_in'][0]), bf(p['mla_w_uq'][0]), bf(p['mla_w_ukv'][0]), bf(p['mla_w_out'][0])],
             [bf(p['mlp_w1'][1]), bf(p['mlp_w2'][1])] + ple(1)]
    bufs = _place_own([b for g in later for b in g])
    pack, wri = _all_gather([gn_pack, bf(p['ret_w_in'][0])])
    groups, at = [], 0
    for g in later:
        groups.append(list(zip(g, bufs[at:at + len(g)])))
        at += len(g)
    started, token = _ag_start(groups, (wri,))

    w = {k: p[k] for k in ('mix_norm', 'mlp_norm', 'ple_norm')}
    w['ret_gn'] = pack[:, :RET_HEADS, :RET_DV // n].transpose(1, 0, 2).reshape(RET_HEADS, RET_DV)
    w['mla_q_a_norm'] = pack[:, RET_HEADS, :MLA_Q_RANK // n].reshape(1, MLA_Q_RANK)
    w['mla_kv_a_norm'] = pack[:, RET_HEADS + 1, :MLA_KV_RANK // n].reshape(1, MLA_KV_RANK)
    w['ret_w_in'] = wri
    w['mla_q_norm'] = _pad_to(p['mla_q_norm'], 1, MLA_HD_PAD)
    w['mla_k_norm'] = _pad_to(p['mla_k_norm'], 1, MLA_HD_PAD)
    w['deps'] = (token,)

    def fetch(name, after):
        got = list(_ag_wait("gather_wait_" + name, *started[names.index(name)], after))
        if name == 'ret_out':
            return dict(ret_w_out=got[0].reshape(RET_V_W, D_MODEL))
        if name == 'mla':
            wmi, wuq, wukv, wmo = got
            return dict(mla_w_in=jnp.pad(wmi.reshape(D_MODEL, MLA_IN), ((0, 0), (0, MLA_IN_PAD - MLA_IN))),
                        mla_w_uq=jnp.pad(wuq, ((0, 0), (0, 0), (0, MLA_HD_PAD - MLA_QKD))),
                        mla_w_ukv=wukv, mla_w_out=wmo.reshape(D_MODEL, D_MODEL))
        out = {}
        if name in ('mlp_w1_0', 'layer_1'):
            out['mlp_w1'] = got.pop(0)
        if name in ('mlp_w2_0', 'layer_1'):
            out['mlp_w2'] = got.pop(0)
        if name in ('ple_0', 'layer_1'):
            out['ple_gate_w'] = got[0].reshape(D_MODEL, D_MODEL)
            out['ple_proj_w'] = got[1].transpose(1, 0, 2).reshape(PLE_DIM, D_MODEL)
        return out

    return w, fetch


def _small_grads(small, after):
    rows = [(0, small['mix_norm'][0]), (1, small['mix_norm'][1]), (2, small['mlp_norm'][0]),
            (3, small['mlp_norm'][1]), (4, small['ple_norm'][0]), (5, small['ple_norm'][1]),
            (6, small['ret_gn']), (10, small['mla_q_a_norm']), (11, small['mla_kv_a_norm']),
            (12, small['mla_q_norm']), (13, small['mla_k_norm'])]
    gs = _all_reduce_small(rows, after)
    me = _flat(*_my_place())
    n = N_DEV
    return dict(
        mix_norm=gs[0:2], mlp_norm=gs[2:4], ple_norm=gs[4:6],
        ret_gn=lax.dynamic_slice(gs, (6, me * (RET_DV // n)), (RET_HEADS, RET_DV // n)),
        mla_q_a_norm=lax.dynamic_slice(gs, (10, me * (MLA_Q_RANK // n)), (1, MLA_Q_RANK // n)),
        mla_kv_a_norm=lax.dynamic_slice(gs, (11, me * (MLA_KV_RANK // n)), (1, MLA_KV_RANK // n)),
        mla_q_norm=gs[12:13, :MLA_QKD], mla_k_norm=gs[13:14, :MLA_QKD])


def kernel(x, p, mix_norm, ret_w_in, ret_gn, ret_w_out, mla_w_in, mla_q_a_norm, mla_kv_a_norm, mla_w_uq, mla_w_ukv, mla_q_norm, mla_k_norm, mla_w_out, mlp_norm, mlp_w1, mlp_w2, ple_norm, ple_gate_w, ple_proj_w, loss_target, m_mix_norm, m_ret_w_in, m_ret_gn, m_ret_w_out, m_mla_w_in, m_mla_q_a_norm, m_mla_kv_a_norm, m_mla_w_uq, m_mla_w_ukv, m_mla_q_norm, m_mla_k_norm, m_mla_w_out, m_mlp_norm, m_mlp_w1, m_mlp_w2, m_ple_norm, m_ple_gate_w, m_ple_proj_w, v_mix_norm, v_ret_w_in, v_ret_gn, v_ret_w_out, v_mla_w_in, v_mla_q_a_norm, v_mla_kv_a_norm, v_mla_w_uq, v_mla_w_ukv, v_mla_q_norm, v_mla_k_norm, v_mla_w_out, v_mlp_norm, v_mlp_w1, v_mlp_w2, v_ple_norm, v_ple_gate_w, v_ple_proj_w):
    given = dict(locals())
    params = {n: given[n] for n in WEIGHTS}
    w, fetch = _prepare_weights(params)

    started = []

    def emit(group):
        keys = list(group)
        send, recv, srcs, lands, token = _rs_start(f"rs_start{len(started)}", [group[k] for k in keys])
        started.append((keys, send, recv, srcs, lands))
        return (token,)

    sq_err, grad_x, _, small = _local_step(x[0], p, loss_target[0], w, fetch, emit)
    loss = lax.psum(0.5 / D_MODEL * sq_err[0, 0], ("x", "y", "c"))

    grads, deltas, new_m, new_v = {}, {}, {}, {}

    def small_updates(after):
        sg = _small_grads(small, after)
        two_d = lambda a: a.reshape(-1, a.shape[-1])
        d_s, m_s, v_s = _adamw_small(
            [two_d(params[n]) for n in SMALL], [sg[n] for n in SMALL],
            [two_d(given["m_" + n]) for n in SMALL], [two_d(given["v_" + n]) for n in SMALL])
        for i, n in enumerate(SMALL):
            shape = params[n].shape
            grads[n], deltas[n], new_m[n], new_v[n] = (a.reshape(shape) for a in (sg[n], d_s[i], m_s[i], v_s[i]))
        return (d_s[0],)

    me = _flat(*_my_place()).astype(jnp.int32).reshape(1)
    after = (grad_x,)
    src_of, land_of = {}, {}
    for gi, (keys, send, recv, srcs, lands) in enumerate(started):
        if gi == len(started) - 1:
            after = small_updates(after)
        srcs, lands = _rs_wait(f"rs_wait{gi}", send, recv, srcs, lands, after)
        for k, s, l in zip(keys, srcs, lands):
            src_of[k], land_of[k] = s, l
        done = [n for n in BIG if n not in grads and all((n, l) in src_of for l in range(params[n].shape[0]))]
        for n in done:
            layers = range(params[n].shape[0])
            grads[n], deltas[n], new_m[n], new_v[n] = _adamw_big(
                "adamw_" + n, params[n], given["m_" + n], given["v_" + n],
                [src_of[(n, l)] for l in layers], [land_of[(n, l)] for l in layers], me)
        if done:
            after = (deltas[done[-1]],)

    return (loss, grad_x[None], *[grads[n] for n in WEIGHTS], *[deltas[n] for n in WEIGHTS],
            *[new_m[n] for n in WEIGHTS], *[new_v[n] for n in WEIGHTS])
```

```python
import functools
import math

import jax
import jax.numpy as jnp
from jax import lax
from jax.experimental import pallas as pl
from jax.experimental.pallas import tpu as pltpu

F32 = jnp.float32
BF16 = jnp.bfloat16
MESH = pl.DeviceIdType.MESH
ANY = pl.BlockSpec(memory_space=pl.ANY)

N_DEV = 8
D_MODEL = 1024
CHUNK = 64
EPS = 1e-6
ROPE_THETA = 10000.0
RET_HEADS = 4
RET_DK = 256
RET_DV = 512
RET_QK_W = RET_HEADS * RET_DK
RET_V_W = RET_HEADS * RET_DV
RET_IN = 2 * RET_QK_W + 2 * RET_V_W
MLA_HEADS = 8
MLA_NOPE = 128
MLA_ROPE = 64
MLA_QKD = MLA_NOPE + MLA_ROPE
MLA_VD = 128
MLA_Q_RANK = 384
MLA_KV_RANK = 256
MLA_IN = MLA_Q_RANK + MLA_KV_RANK + MLA_ROPE
MLA_IN_PAD = 768
MLA_HD_PAD = 256
D_FF = 4096
PLE_DIM = 256
ATT_SCALE = MLA_QKD ** -0.5
LOG2E = 1.4426950408889634
ATT_EXP2 = ATT_SCALE * LOG2E

ADAM_LR = 0.001
ADAM_B1 = 0.9
ADAM_B2 = 0.999
ADAM_EPS = 1e-08
ADAM_WD = 0.01
ADAM_STEP = 10

VMEM_LIMIT = 52 * 1024 * 1024
ROW_TILE = 1024
RET_ROWS = 256
ATT_BLOCK = 256
ATT_QROWS = 1024
ATT_KROWS = 1024
ATT_HEADS = 2

WEIGHTS = ['mix_norm', 'ret_w_in', 'ret_gn', 'ret_w_out', 'mla_w_in', 'mla_q_a_norm', 'mla_kv_a_norm',
           'mla_w_uq', 'mla_w_ukv', 'mla_q_norm', 'mla_k_norm', 'mla_w_out', 'mlp_norm', 'mlp_w1', 'mlp_w2',
           'ple_norm', 'ple_gate_w', 'ple_proj_w']
BIG = ['ret_w_in', 'ret_w_out', 'mla_w_in', 'mla_w_uq', 'mla_w_ukv', 'mla_w_out', 'mlp_w1', 'mlp_w2',
       'ple_gate_w', 'ple_proj_w']
SMALL = [w for w in WEIGHTS if w not in BIG]


def _cparams(sem=None):
    return pltpu.CompilerParams(dimension_semantics=sem, vmem_limit_bytes=VMEM_LIMIT)


def _dot(a, b, ca, cb):
    return lax.dot_general(a, b, (((ca,), (cb,)), ((), ())), preferred_element_type=F32)


def _bf(v):
    return v if v.dtype == BF16 else v.astype(BF16)


def _sigmoid(z):
    return 1.0 / (1.0 + jnp.exp(-z))


def _mm(name, grid, a, a_spec, b, b_spec, contract, outs, extras=(), epi=None, deps=(), split=None):
    nk = grid[2]
    n_ex, n_out, n_dep = len(extras), len(outs), len(deps)
    acc_shape = tuple(d for d in outs[0][1].block_shape if d is not None)
    if split is not None:
        acc_shape = (acc_shape[1], acc_shape[0] * split)

    def body(*refs):
        a_ref, b_ref = refs[:2]
        ex_refs = refs[2:2 + n_ex]
        out_refs = refs[2 + n_ex + n_dep:2 + n_ex + n_dep + n_out]

        def product():
            return _dot(_bf(a_ref[...]), _bf(b_ref[...]), contract[0], contract[1])

        def finish(acc):
            if split is not None:
                for j in range(acc_shape[1] // split):
                    out_refs[0][j] = acc[:, j * split:(j + 1) * split].astype(out_refs[0].dtype)
                return
            acc = acc[...]
            res = epi(acc, *[r[...] for r in ex_refs]) if epi is not None else (acc,)
            for o, r in zip(out_refs, res):
                o[...] = r.astype(o.dtype)

        if nk == 1:
            finish(product())
        else:
            acc_ref = refs[-1]
            k = pl.program_id(2)

            @pl.when(k == 0)
            def _():
                acc_ref[...] = jnp.zeros_like(acc_ref)

            acc_ref[...] += product()

            @pl.when(k == nk - 1)
            def _():
                finish(acc_ref)

    return pl.pallas_call(
        body, name=name, grid=grid,
        in_specs=[a_spec, b_spec] + [s for _, s in extras] + [ANY] * n_dep,
        out_specs=[s for _, s in outs],
        out_shape=[s for s, _ in outs],
        scratch_shapes=[pltpu.VMEM(acc_shape, F32)] if nk > 1 else [],
        compiler_params=_cparams(("parallel", "parallel", "arbitrary")),
    )(a, b, *[x for x, _ in extras], *deps)


def _sds(shape, dtype):
    return jax.ShapeDtypeStruct(shape, dtype)


def _row_tile(t, cap=ROW_TILE):
    return min(cap, t)


def _rms_fwd(name, x, g):
    t, d = x.shape
    tm = _row_tile(t)

    def body(x_ref, g_ref, o_ref):
        xv = x_ref[...]
        r = lax.rsqrt(jnp.mean(xv * xv, axis=-1, keepdims=True) + EPS)
        o_ref[...] = (xv * r * g_ref[...]).astype(o_ref.dtype)

    return pl.pallas_call(
        body, name=name, grid=(t // tm,),
        in_specs=[pl.BlockSpec((tm, d), lambda i: (i, 0)), pl.BlockSpec((1, d), lambda i: (0, 0))],
        out_specs=pl.BlockSpec((tm, d), lambda i: (i, 0)),
        out_shape=_sds((t, d), BF16),
        compiler_params=_cparams(("parallel",)),
    )(x, g)


def _rms_bwd_rows(dy, xv, g, n):
    r = lax.rsqrt(jnp.sum(xv * xv, axis=-1, keepdims=True) / n + EPS)
    xh = xv * r
    dxh = dy * g
    dx = r * (dxh - xh * (jnp.sum(dxh * xh, axis=-1, keepdims=True) / n))
    return dx, dy * xh


def _rms_bwd(name, dy, x, g, res):
    t, d = x.shape
    tm = _row_tile(t, 512)

    def body(dy_ref, x_ref, g_ref, res_ref, dx_ref, dg_ref):
        @pl.when(pl.program_id(0) == 0)
        def _():
            dg_ref[...] = jnp.zeros_like(dg_ref)

        dx, dgr = _rms_bwd_rows(dy_ref[...], x_ref[...], g_ref[...], d)
        dx_ref[...] = res_ref[...] + dx
        dg_ref[...] += jnp.sum(dgr, axis=0, keepdims=True)

    row = pl.BlockSpec((tm, d), lambda i: (i, 0))
    vec = pl.BlockSpec((1, d), lambda i: (0, 0))
    return pl.pallas_call(
        body, name=name, grid=(t // tm,),
        in_specs=[row, row, vec, row], out_specs=[row, vec],
        out_shape=[_sds((t, d), F32), _sds((1, d), F32)],
        compiler_params=_cparams(("arbitrary",)),
    )(dy, x, g, res)


def _loss_head(y, target):
    t, d = y.shape
    tm = _row_tile(t)

    def body(y_ref, t_ref, dy_ref, l_ref):
        @pl.when(pl.program_id(0) == 0)
        def _():
            l_ref[...] = jnp.zeros_like(l_ref)

        e = y_ref[...] - t_ref[...]
        dy_ref[...] = e / d
        l_ref[...] += jnp.sum(jnp.sum(e * e, axis=-1, keepdims=True), axis=0, keepdims=True)

    row = pl.BlockSpec((tm, d), lambda i: (i, 0))
    return pl.pallas_call(
        body, name="loss_head", grid=(t // tm,),
        in_specs=[row, row], out_specs=[row, pl.BlockSpec((8, 128), lambda i: (0, 0))],
        out_shape=[_sds((t, d), F32), _sds((8, 128), F32)],
        compiler_params=_cparams(("arbitrary",)),
    )(y, target)


def _ple_gate_bwd(name, dh, gate, e):
    t, d = dh.shape
    tm = _row_tile(t)

    def body(dh_ref, g_ref, e_ref, de_ref, dz_ref):
        dh_v, gt = dh_ref[...], g_ref[...]
        de_ref[...] = (dh_v * gt).astype(BF16)
        dz_ref[...] = (dh_v * e_ref[...] * (gt * (1.0 - gt))).astype(BF16)

    row = pl.BlockSpec((tm, d), lambda i: (i, 0))
    return pl.pallas_call(
        body, name=name, grid=(t // tm,), in_specs=[row, row, row], out_specs=[row, row],
        out_shape=[_sds((t, d), BF16), _sds((t, d), BF16)],
        compiler_params=_cparams(("parallel",)),
    )(dh, gate, e)


def _rope_half(v, cos, sin):
    half = v.shape[-1] // 2
    v1, v2 = v[:, :half], v[:, half:]
    return jnp.concatenate([v1 * cos - v2 * sin, v2 * cos + v1 * sin], axis=-1)


def _ret_consts():
    lg = jnp.log(1.0 - 2.0 ** (-5.0 - jnp.arange(RET_HEADS, dtype=F32)))
    idx = jnp.arange(CHUNK, dtype=F32)
    intra = jnp.exp(lg[:, None, None] * jnp.abs(idx[:, None] - idx[None, :]))
    qdec = jnp.exp(lg[:, None] * (idx + 1.0))
    kdec = jnp.exp(lg[:, None] * (CHUNK - 1.0 - idx))
    cdec = jnp.exp(lg * CHUNK)
    qdec = jnp.broadcast_to(qdec[:, :, None], (RET_HEADS, CHUNK, RET_DK))
    kdec = jnp.broadcast_to(kdec[:, :, None], (RET_HEADS, CHUNK, RET_DK))
    cdec = jnp.broadcast_to(cdec[:, None, None], (RET_HEADS, 1, RET_DV))
    return intra, qdec, kdec, cdec


def _ret_specs(rb, rev_nb=None):
    blk = (lambda i: i) if rev_nb is None else (lambda i: rev_nb - 1 - i)
    full = lambda shape: pl.BlockSpec(shape, lambda i: (0,) * len(shape))
    return dict(
        proj=pl.BlockSpec((rb, RET_IN), lambda i: (blk(i), 0)),
        tab=pl.BlockSpec((rb, RET_DK // 2), lambda i: (blk(i), 0)),
        vw=pl.BlockSpec((rb, RET_V_W), lambda i: (blk(i), 0)),
        st=pl.BlockSpec((rb // CHUNK, RET_HEADS, RET_DK, RET_DV), lambda i: (blk(i), 0, 0, 0)),
        gn=full((RET_HEADS, 1, RET_DV)),
        intra=full((RET_HEADS, CHUNK, CHUNK)),
        dec=full((RET_HEADS, CHUNK, RET_DK)),
        cdec=full((RET_HEADS, 1, RET_DV)),
    )


def _ret_fwd(proj, cos, sin, gn):
    t = proj.shape[0]
    rb = min(RET_ROWS, t)
    cpb = rb // CHUNK
    intra, qdec, kdec, cdec = _ret_consts()
    sp = _ret_specs(rb)

    def body(proj_ref, cos_ref, sin_ref, gn_ref, intra_ref, qd_ref, kd_ref, cd_ref,
             gated_ref, outp_ref, st_ref, s_ref):
        @pl.when(pl.program_id(0) == 0)
        def _():
            s_ref[...] = jnp.zeros_like(s_ref)

        def chunk(c, carry):
            rows = pl.ds(pl.multiple_of(c * CHUNK, CHUNK), CHUNK)
            cs, sn = cos_ref[rows, :], sin_ref[rows, :]
            for h in range(RET_HEADS):
                q = proj_ref[rows, h * RET_DK:(h + 1) * RET_DK]
                k = proj_ref[rows, RET_QK_W + h * RET_DK:RET_QK_W + (h + 1) * RET_DK]
                v = proj_ref[rows, 2 * RET_QK_W + h * RET_DV:2 * RET_QK_W + (h + 1) * RET_DV]
                g = proj_ref[rows, 2 * RET_QK_W + RET_V_W + h * RET_DV:2 * RET_QK_W + RET_V_W + (h + 1) * RET_DV]
                qr = _rope_half(q, cs, sn)
                kr = _rope_half(k, cs, sn) * (RET_DK ** -0.5)
                qb, kb, vb = qr.astype(BF16), kr.astype(BF16), v.astype(BF16)
                sc = _dot(qb, kb, 1, 1) * intra_ref[h]
                inner = _dot(sc.astype(BF16), vb, 1, 0)
                s_old = s_ref[h]
                sb = s_old.astype(BF16)
                st_ref[c, h] = sb
                cross = _dot((qr * qd_ref[h]).astype(BF16), sb, 1, 0)
                out = inner + cross
                s_ref[h] = s_old * cd_ref[h] + _dot((kr * kd_ref[h]).astype(BF16), vb, 0, 0)
                r = lax.rsqrt(jnp.mean(out * out, axis=-1, keepdims=True) + EPS)
                y = out * r * gn_ref[h]
                cols = slice(h * RET_DV, (h + 1) * RET_DV)
                gated_ref[rows, cols] = (g * _sigmoid(g) * y).astype(BF16)
                outp_ref[rows, cols] = out
            return carry

        lax.fori_loop(0, cpb, chunk, 0)

    return pl.pallas_call(
        body, name="ret_fwd", grid=(t // rb,),
        in_specs=[sp['proj'], sp['tab'], sp['tab'], sp['gn'], sp['intra'], sp['dec'], sp['dec'], sp['cdec']],
        out_specs=[sp['vw'], sp['vw'], sp['st']],
        out_shape=[_sds((t, RET_V_W), BF16), _sds((t, RET_V_W), F32),
                   _sds((t // CHUNK, RET_HEADS, RET_DK, RET_DV), BF16)],
        scratch_shapes=[pltpu.VMEM((RET_HEADS, RET_DK, RET_DV), F32)],
        compiler_params=_cparams(("arbitrary",)),
    )(proj, cos, sin, gn.reshape(RET_HEADS, 1, RET_DV), intra, qdec, kdec, cdec)


def _ret_bwd(proj, cos, sin, gn, outp, states, dgated):
    t = proj.shape[0]
    rb = min(RET_ROWS, t)
    cpb = rb // CHUNK
    nb = t // rb
    intra, qdec, kdec, cdec = _ret_consts()
    sp = _ret_specs(rb, rev_nb=nb)

    def body(proj_ref, cos_ref, sin_ref, gn_ref, intra_ref, qd_ref, kd_ref, cd_ref, outp_ref, st_ref, dgt_ref,
             dproj_ref, dgn_ref, ds_ref):
        @pl.when(pl.program_id(0) == 0)
        def _():
            ds_ref[...] = jnp.zeros_like(ds_ref)
            dgn_ref[...] = jnp.zeros_like(dgn_ref)

        def chunk(cc, carry):
            c = cpb - 1 - cc
            rows = pl.ds(pl.multiple_of(c * CHUNK, CHUNK), CHUNK)
            cs, sn = cos_ref[rows, :], sin_ref[rows, :]
            for h in range(RET_HEADS):
                q = proj_ref[rows, h * RET_DK:(h + 1) * RET_DK]
                k = proj_ref[rows, RET_QK_W + h * RET_DK:RET_QK_W + (h + 1) * RET_DK]
                v = proj_ref[rows, 2 * RET_QK_W + h * RET_DV:2 * RET_QK_W + (h + 1) * RET_DV]
                g = proj_ref[rows, 2 * RET_QK_W + RET_V_W + h * RET_DV:2 * RET_QK_W + RET_V_W + (h + 1) * RET_DV]
                cols = slice(h * RET_DV, (h + 1) * RET_DV)
                qr = _rope_half(q, cs, sn)
                kr = _rope_half(k, cs, sn) * (RET_DK ** -0.5)
                qb, kb, vb = qr.astype(BF16), kr.astype(BF16), v.astype(BF16)
                qdb = (qr * qd_ref[h]).astype(BF16)
                kdb = (kr * kd_ref[h]).astype(BF16)
                out = outp_ref[rows, cols]
                dgt = dgt_ref[rows, cols]
                gnh = gn_ref[h]
                r = lax.rsqrt(jnp.mean(out * out, axis=-1, keepdims=True) + EPS)
                xh = out * r
                sg = _sigmoid(g)
                dgate = dgt * (xh * gnh) * (sg * (1.0 + g * (1.0 - sg)))
                dy = dgt * (g * sg)
                dgn_ref[h] += jnp.sum(dy * xh, axis=0, keepdims=True)
                dxh = dy * gnh
                dout = r * (dxh - xh * jnp.mean(dxh * xh, axis=-1, keepdims=True))
                doutb = dout.astype(BF16)
                itr = intra_ref[h]
                pb = (_dot(qb, kb, 1, 1) * itr).astype(BF16)
                dv = _dot(pb, doutb, 0, 0)
                dsc = (_dot(doutb, vb, 1, 1) * itr).astype(BF16)
                dq = _dot(dsc, kb, 1, 0)
                dk = _dot(dsc, qb, 0, 0)
                dq = dq + _dot(doutb, st_ref[c, h], 1, 1) * qd_ref[h]
                ds_new = ds_ref[h]
                dsb = ds_new.astype(BF16)
                dk = dk + _dot(vb, dsb, 1, 1) * kd_ref[h]
                dv = dv + _dot(kdb, dsb, 1, 0)
                ds_ref[h] = ds_new * cd_ref[h] + _dot(qdb, doutb, 0, 0)
                dproj_ref[rows, h * RET_DK:(h + 1) * RET_DK] = _rope_half(dq, cs, -sn).astype(BF16)
                dproj_ref[rows, RET_QK_W + h * RET_DK:RET_QK_W + (h + 1) * RET_DK] = (
                    _rope_half(dk * (RET_DK ** -0.5), cs, -sn).astype(BF16))
                dproj_ref[rows, 2 * RET_QK_W + h * RET_DV:2 * RET_QK_W + (h + 1) * RET_DV] = dv.astype(BF16)
                dproj_ref[rows, 2 * RET_QK_W + RET_V_W + h * RET_DV:
                          2 * RET_QK_W + RET_V_W + (h + 1) * RET_DV] = dgate.astype(BF16)
            return carry

        lax.fori_loop(0, cpb, chunk, 0)

    return pl.pallas_call(
        body, name="ret_bwd", grid=(nb,),
        in_specs=[sp['proj'], sp['tab'], sp['tab'], sp['gn'], sp['intra'], sp['dec'], sp['dec'], sp['cdec'],
                  sp['vw'], sp['st'], sp['vw']],
        out_specs=[sp['proj'], sp['gn']],
        out_shape=[_sds((t, RET_IN), BF16), _sds((RET_HEADS, 1, RET_DV), F32)],
        scratch_shapes=[pltpu.VMEM((RET_HEADS, RET_DK, RET_DV), F32)],
        compiler_params=_cparams(("arbitrary",)),
    )(proj, cos, sin, gn.reshape(RET_HEADS, 1, RET_DV), intra, qdec, kdec, cdec, outp, states, dgated)


def _mla_tables(t):
    half = MLA_ROPE // 2
    inv = 1.0 / (ROPE_THETA ** (jnp.arange(0, MLA_ROPE, 2, dtype=F32) / MLA_ROPE))
    ang = jnp.arange(t, dtype=F32)[:, None] * inv[None, :]
    cos, sin = jnp.cos(ang), jnp.sin(ang)
    z = jnp.zeros((t, half), F32)
    c = jnp.concatenate([cos, cos, z, z], axis=1)
    s1 = jnp.concatenate([-sin, z, z, z], axis=1)
    s2 = jnp.concatenate([z, sin, z, z], axis=1)
    return c, s1, s2


def _rope_tile(r, c, s1, s2):
    return r * c + pltpu.roll(r, 96, 1) * s1 + pltpu.roll(r, 32, 1) * s2


def _mla_mid(proj2, qa, kva):
    t = proj2.shape[0]
    tm = _row_tile(t)

    def body(p_ref, qa_ref, kva_ref, cq_ref, ckv_ref):
        cq = p_ref[:, :MLA_Q_RANK]
        ckv = p_ref[:, MLA_Q_RANK:MLA_Q_RANK + MLA_KV_RANK]
        rq = lax.rsqrt(jnp.mean(cq * cq, axis=-1, keepdims=True) + EPS)
        rkv = lax.rsqrt(jnp.mean(ckv * ckv, axis=-1, keepdims=True) + EPS)
        cq_ref[...] = (cq * rq * qa_ref[...]).astype(BF16)
        ckv_ref[...] = (ckv * rkv * kva_ref[...]).astype(BF16)

    return pl.pallas_call(
        body, name="mla_mid", grid=(t // tm,),
        in_specs=[pl.BlockSpec((tm, MLA_IN_PAD), lambda i: (i, 0)),
                  pl.BlockSpec((1, MLA_Q_RANK), lambda i: (0, 0)),
                  pl.BlockSpec((1, MLA_KV_RANK), lambda i: (0, 0))],
        out_specs=[pl.BlockSpec((tm, MLA_Q_RANK), lambda i: (i, 0)),
                   pl.BlockSpec((tm, MLA_KV_RANK), lambda i: (i, 0))],
        out_shape=[_sds((t, MLA_Q_RANK), BF16), _sds((t, MLA_KV_RANK), BF16)],
        compiler_params=_cparams(("parallel",)),
    )(proj2, qa, kva)


def _mla_mid_bwd(proj2, qa, kva, dcq, dckv, dkr):
    t = proj2.shape[0]
    tm = _row_tile(t)

    def body(p_ref, qa_ref, kva_ref, dcq_ref, dckv_ref, dkr_ref, dp_ref, dqa_ref, dkva_ref):
        @pl.when(pl.program_id(0) == 0)
        def _():
            dqa_ref[...] = jnp.zeros_like(dqa_ref)
            dkva_ref[...] = jnp.zeros_like(dkva_ref)

        dxq, dgq = _rms_bwd_rows(dcq_ref[...], p_ref[:, :MLA_Q_RANK], qa_ref[...], MLA_Q_RANK)
        dxk, dgk = _rms_bwd_rows(dckv_ref[...], p_ref[:, MLA_Q_RANK:MLA_Q_RANK + MLA_KV_RANK], kva_ref[...],
                                 MLA_KV_RANK)
        dp_ref[:, :MLA_Q_RANK] = dxq.astype(BF16)
        dp_ref[:, MLA_Q_RANK:MLA_Q_RANK + MLA_KV_RANK] = dxk.astype(BF16)
        dp_ref[:, MLA_Q_RANK + MLA_KV_RANK:] = dkr_ref[...].astype(BF16)
        dqa_ref[...] += jnp.sum(dgq, axis=0, keepdims=True)
        dkva_ref[...] += jnp.sum(dgk, axis=0, keepdims=True)

    return pl.pallas_call(
        body, name="mla_mid_bwd", grid=(t // tm,),
        in_specs=[pl.BlockSpec((tm, MLA_IN_PAD), lambda i: (i, 0)),
                  pl.BlockSpec((1, MLA_Q_RANK), lambda i: (0, 0)),
                  pl.BlockSpec((1, MLA_KV_RANK), lambda i: (0, 0)),
                  pl.BlockSpec((tm, MLA_Q_RANK), lambda i: (i, 0)),
                  pl.BlockSpec((tm, MLA_KV_RANK), lambda i: (i, 0)),
                  pl.BlockSpec((tm, 128), lambda i: (i, 0))],
        out_specs=[pl.BlockSpec((tm, MLA_IN_PAD), lambda i: (i, 0)),
                   pl.BlockSpec((1, MLA_Q_RANK), lambda i: (0, 0)),
                   pl.BlockSpec((1, MLA_KV_RANK), lambda i: (0, 0))],
        out_shape=[_sds((t, MLA_IN_PAD), BF16), _sds((1, MLA_Q_RANK), F32), _sds((1, MLA_KV_RANK), F32)],
        compiler_params=_cparams(("arbitrary",)),
    )(proj2, qa, kva, dcq, dckv, dkr)


def _mla_prep_specs(t, tm):
    head = lambda w: pl.BlockSpec((None, tm, w), lambda i, h: (h, i, 0))
    return dict(
        head256=head(MLA_HD_PAD), head128=head(MLA_VD),
        cols256=pl.BlockSpec((tm, MLA_HD_PAD), lambda i, h: (i, h)),
        kr=pl.BlockSpec((tm, 128), lambda i, h: (i, (MLA_Q_RANK + MLA_KV_RANK) // 128)),
        gain=pl.BlockSpec((1, MLA_HD_PAD), lambda i, h: (0, 0)),
        tab=pl.BlockSpec((tm, 128), lambda i, h: (i, 0)),
    )


def _mla_prep(q, kv, proj2, gq, gk, tabs):
    t = q.shape[1]
    tm = _row_tile(t)
    sp = _mla_prep_specs(t, tm)

    def body(q_ref, kv_ref, kr_ref, gq_ref, gk_ref, c_ref, s1_ref, s2_ref, qh_ref, kh_ref, vh_ref):
        c, s1, s2 = c_ref[...], s1_ref[...], s2_ref[...]

        def norm_rope(xv, gain):
            r = lax.rsqrt(jnp.sum(xv * xv, axis=-1, keepdims=True) / MLA_QKD + EPS)
            y = xv * r * gain
            return jnp.concatenate([y[:, :MLA_NOPE], _rope_tile(y[:, MLA_NOPE:], c, s1, s2)], axis=-1)

        kvv = kv_ref[...]
        qh_ref[...] = norm_rope(q_ref[...], gq_ref[...]).astype(BF16)
        kf = jnp.concatenate([kvv[:, :MLA_NOPE], kr_ref[...]], axis=-1)
        kh_ref[...] = norm_rope(kf, gk_ref[...]).astype(BF16)
        vh_ref[...] = jnp.concatenate([kvv[:, MLA_NOPE:], jnp.ones((tm, MLA_VD), F32)], axis=-1).astype(BF16)

    return pl.pallas_call(
        body, name="mla_prep", grid=(t // tm, MLA_HEADS),
        in_specs=[sp['head256'], sp['head256'], sp['kr'], sp['gain'], sp['gain'], sp['tab'], sp['tab'], sp['tab']],
        out_specs=[sp['head256'], sp['head256'], sp['head256']],
        out_shape=[_sds((MLA_HEADS, t, MLA_HD_PAD), BF16), _sds((MLA_HEADS, t, MLA_HD_PAD), BF16),
                   _sds((MLA_HEADS, t, 2 * MLA_VD), BF16)],
        compiler_params=_cparams(("parallel", "arbitrary")),
    )(q, kv, proj2, gq, gk, *tabs)


def _mla_prep_bwd(q, kv, proj2, gq, gk, tabs, dqt, dkh, dvh):
    t = q.shape[1]
    tm = _row_tile(t)
    ab = dqt.shape[-1]
    sp = _mla_prep_specs(t, tm)

    def body(q_ref, kv_ref, kr_ref, gq_ref, gk_ref, c_ref, s1_ref, s2_ref, dqt_ref, dkh_ref, dvh_ref,
             dq_ref, dkv_ref, dkr_ref, dgq_ref, dgk_ref):
        dqh = jnp.concatenate([dqt_ref[b].T for b in range(tm // ab)], axis=0)
        i, h = pl.program_id(0), pl.program_id(1)

        @pl.when((i == 0) & (h == 0))
        def _():
            dgq_ref[...] = jnp.zeros_like(dgq_ref)
            dgk_ref[...] = jnp.zeros_like(dgk_ref)

        @pl.when(h == 0)
        def _():
            dkr_ref[...] = jnp.zeros_like(dkr_ref)

        c, s1, s2 = c_ref[...], s1_ref[...], s2_ref[...]

        def back(xv, gain, dout):
            dy = jnp.concatenate([dout[:, :MLA_NOPE], _rope_tile(dout[:, MLA_NOPE:], c, -s1, -s2)], axis=-1)
            return _rms_bwd_rows(dy, xv, gain, MLA_QKD)

        kvv = kv_ref[...]
        dxq, dgq = back(q_ref[...], gq_ref[...], dqh)
        kf = jnp.concatenate([kvv[:, :MLA_NOPE], kr_ref[...]], axis=-1)
        dxk, dgk = back(kf, gk_ref[...], dkh_ref[...])
        dq_ref[...] = dxq.astype(BF16)
        dkv_ref[...] = jnp.concatenate([dxk[:, :MLA_NOPE], dvh_ref[...]], axis=-1).astype(BF16)
        dkr_ref[...] += dxk[:, MLA_NOPE:]
        dgq_ref[...] += jnp.sum(dgq, axis=0, keepdims=True)
        dgk_ref[...] += jnp.sum(dgk, axis=0, keepdims=True)

    return pl.pallas_call(
        body, name="mla_prep_bwd", grid=(t // tm, MLA_HEADS),
        in_specs=[sp['head256'], sp['head256'], sp['kr'], sp['gain'], sp['gain'], sp['tab'], sp['tab'], sp['tab'],
                  pl.BlockSpec((None, tm // ab, MLA_HD_PAD, ab), lambda i, h: (h, i, 0, 0)),
                  sp['head256'], sp['head128']],
        out_specs=[sp['cols256'], sp['cols256'], sp['tab'], sp['gain'], sp['gain']],
        out_shape=[_sds((t, MLA_HEADS * MLA_HD_PAD), BF16), _sds((t, MLA_HEADS * MLA_HD_PAD), BF16),
                   _sds((t, 128), F32), _sds((1, MLA_HD_PAD), F32), _sds((1, MLA_HD_PAD), F32)],
        compiler_params=_cparams(("arbitrary", "arbitrary")),
    )(q, kv, proj2, gq, gk, *tabs, dqt, dkh, dvh)


def _chunk_visible(rows, cols, row_off, col_off):
    rq = lax.shift_right_logical(lax.broadcasted_iota(jnp.int32, (rows, cols), 0) + row_off, 6)
    ck = lax.shift_right_logical(lax.broadcasted_iota(jnp.int32, (rows, cols), 1) + col_off, 6)
    return ck <= rq


def _rows_to_lanes(col):
    return col.T[:8, :]


def _attn_fwd(qh, kh, vh):
    t = qh.shape[1]
    ab = min(ATT_BLOCK, t)
    tq = min(ATT_QROWS, t)
    r = tq // ab
    hg = ATT_HEADS

    def body(q_ref, k_ref, v_ref, o_ref, lse_ref):
        n_un = pl.program_id(1) * r

        def step(b, state, diag):
            rows = pl.ds(pl.multiple_of(b * ab, ab), ab)
            ms, accs = [], []
            for hh in range(hg):
                m, acc = state[0][hh], state[1][hh]
                s = _dot(q_ref[hh], k_ref[hh, rows, :], 1, 1)
                if diag is not None:
                    s = jnp.where(_chunk_visible(tq, ab, 0, diag * ab), s, -1e30)
                m_new = jnp.maximum(m, jnp.max(s, axis=-1, keepdims=True))
                p = jnp.exp2((s - m_new) * ATT_EXP2).astype(BF16)
                accs.append(jnp.exp2((m - m_new) * ATT_EXP2) * acc + _dot(p, v_ref[hh, rows, :], 1, 0))
                ms.append(m_new)
            return tuple(ms), tuple(accs)

        heads = lambda v: tuple(v for _ in range(hg))
        state = (heads(jnp.full((tq, 1), -1e30, F32)), heads(jnp.zeros((tq, 2 * MLA_VD), F32)))
        state = lax.fori_loop(0, n_un, lambda b, st: step(b, st, None), state)
        for d in range(r):
            state = step(n_un + d, state, d)
        ms, accs = state
        for hh in range(hg):
            l = accs[hh][:, MLA_VD:]
            o_ref[:, hh * MLA_VD:(hh + 1) * MLA_VD] = accs[hh][:, :MLA_VD] / l
            lse_t = _rows_to_lanes(ms[hh] * ATT_EXP2 + jnp.log(l) * LOG2E)
            for d in range(r):
                lse_ref[hh, d] = lse_t[:, d * ab:(d + 1) * ab]

    return pl.pallas_call(
        body, name="mla_attn", grid=(MLA_HEADS // hg, t // tq),
        in_specs=[pl.BlockSpec((hg, tq, MLA_HD_PAD), lambda g, i: (g, i, 0)),
                  pl.BlockSpec((hg, t, MLA_HD_PAD), lambda g, i: (g, 0, 0)),
                  pl.BlockSpec((hg, t, 2 * MLA_VD), lambda g, i: (g, 0, 0))],
        out_specs=[pl.BlockSpec((tq, hg * MLA_VD), lambda g, i: (i, g)),
                   pl.BlockSpec((hg, r, 8, ab), lambda g, i: (g, i, 0, 0))],
        out_shape=[_sds((t, MLA_HEADS * MLA_VD), F32), _sds((MLA_HEADS, t // ab, 8, ab), F32)],
        compiler_params=_cparams(("parallel", "arbitrary")),
    )(qh, kh, vh)


def _attn_delta(do, o, ab):
    t = do.shape[0]
    tm = _row_tile(t)

    def body(do_ref, o_ref, d_ref):
        d = jnp.sum(do_ref[...] * o_ref[...], axis=-1, keepdims=True)
        d_t = _rows_to_lanes(jnp.broadcast_to(d, (tm, 128)))
        for b in range(tm // ab):
            d_ref[b] = d_t[:, b * ab:(b + 1) * ab]

    col = pl.BlockSpec((tm, MLA_VD), lambda i, h: (i, h))
    return pl.pallas_call(
        body, name="mla_delta", grid=(t // tm, MLA_HEADS), in_specs=[col, col],
        out_specs=pl.BlockSpec((None, tm // ab, 8, ab), lambda i, h: (h, i, 0, 0)),
        out_shape=_sds((MLA_HEADS, t // ab, 8, ab), F32),
        compiler_params=_cparams(("parallel", "parallel")),
    )(do, o)


def _attn_bwd(qh, kh, vh, dob, lse_t, dl_t):
    t = qh.shape[1]
    ab = min(ATT_BLOCK, t)
    kb = min(ATT_KROWS, t)
    r = kb // ab
    nq = t // ab
    hg = ATT_HEADS

    def body(q_ref, k_ref, v_ref, do_ref, lse_ref, dl_ref, dqt_ref, dk_ref, dv_ref):
        j = pl.program_id(1)

        @pl.when(j == 0)
        def _():
            dqt_ref[...] = jnp.zeros_like(dqt_ref)

        ks = [k_ref[hh] for hh in range(hg)]
        vs = [v_ref[hh, :, :MLA_VD] for hh in range(hg)]
        kts = [k.T for k in ks]

        def step(b, grads, diag):
            rows = pl.ds(pl.multiple_of(b * ab, ab), ab)
            out = []
            for hh in range(hg):
                dk, dv = grads[hh]
                q = q_ref[hh, rows, :]
                do = do_ref[rows, hh * MLA_VD:(hh + 1) * MLA_VD]
                s_t = _dot(ks[hh], q, 1, 1)
                if diag is not None:
                    key_chunk = lax.shift_right_logical(lax.broadcasted_iota(jnp.int32, (kb, ab), 0), 6)
                    query_chunk = lax.shift_right_logical(
                        lax.broadcasted_iota(jnp.int32, (kb, ab), 1) + diag * ab, 6)
                    s_t = jnp.where(key_chunk <= query_chunk, s_t, -1e30)
                p_t = jnp.exp2(s_t * ATT_EXP2 - lse_ref[hh, b][0:1, :])
                dp_t = _dot(vs[hh], do, 1, 1)
                ds_t = (p_t * (dp_t - dl_ref[hh, b][0:1, :]) * ATT_SCALE).astype(BF16)
                dqt_ref[hh, b] += _dot(kts[hh], ds_t, 1, 0)
                out.append((dk + _dot(ds_t, q, 1, 0), dv + _dot(p_t.astype(BF16), do, 1, 0)))
            return tuple(out)

        grads = tuple((jnp.zeros((kb, MLA_HD_PAD), F32), jnp.zeros((kb, MLA_VD), F32)) for _ in range(hg))
        for d in range(r):
            grads = step(j * r + d, grads, d)
        grads = lax.fori_loop((j + 1) * r, nq, lambda b, g: step(b, g, None), grads)
        for hh in range(hg):
            dk_ref[hh] = grads[hh][0]
            dv_ref[hh] = grads[hh][1]

    whole = lambda w: pl.BlockSpec((hg, t, w), lambda g, j: (g, 0, 0))
    blk = lambda w: pl.BlockSpec((hg, kb, w), lambda g, j: (g, j, 0))
    stat = pl.BlockSpec((hg, nq, 8, ab), lambda g, j: (g, 0, 0, 0))
    return pl.pallas_call(
        body, name="mla_attn_bwd", grid=(MLA_HEADS // hg, t // kb),
        in_specs=[whole(MLA_HD_PAD), blk(MLA_HD_PAD), blk(2 * MLA_VD),
                  pl.BlockSpec((t, hg * MLA_VD), lambda g, j: (0, g)), stat, stat],
        out_specs=[pl.BlockSpec((hg, nq, MLA_HD_PAD, ab), lambda g, j: (g, 0, 0, 0)), blk(MLA_HD_PAD), blk(MLA_VD)],
        out_shape=[_sds((MLA_HEADS, nq, MLA_HD_PAD, ab), F32), _sds((MLA_HEADS, t, MLA_HD_PAD), F32),
                   _sds((MLA_HEADS, t, MLA_VD), F32)],
        compiler_params=_cparams(("parallel", "arbitrary")),
    )(qh, kh, vh, dob, lse_t, dl_t)


VEC = pl.BlockSpec((1, D_MODEL), lambda i, j, k: (0, 0))


def _residual_epi(next_gain):
    if next_gain is None:
        return [], lambda acc, hv: (acc + hv,)

    def epi(acc, hv, g):
        h_new = acc + hv
        r = lax.rsqrt(jnp.mean(h_new * h_new, axis=-1, keepdims=True) + EPS)
        return h_new, h_new * r * g

    return [(next_gain, VEC)], epi


def _residual_outs(t, row, next_gain):
    outs = [(_sds((t, D_MODEL), F32), row)]
    return outs + ([(_sds((t, D_MODEL), BF16), row)] if next_gain is not None else [])


def _mlp_fwd(l, h, hn, w1g, fetch_w2, next_gain):
    t = h.shape[0]
    tm = _row_tile(t)
    nsh, _, wsh = w1g.shape

    def relu2(acc):
        r = jnp.maximum(acc, 0.0)
        return (r * r,)

    tu = _row_tile(t, 2 * ROW_TILE)
    tile = pl.BlockSpec((tu, wsh), lambda i, j, k: (i, j))
    (u,) = _mm(f"mlp_up{l}", (t // tu, nsh, 1),
               hn, pl.BlockSpec((tu, D_MODEL), lambda i, j, k: (i, 0)),
               w1g, pl.BlockSpec((None, D_MODEL, wsh), lambda i, j, k: (j, 0, 0)), (1, 0),
               [(_sds((t, D_FF), BF16), tile)], epi=relu2)
    w2g = fetch_w2((u,))
    row = pl.BlockSpec((tm, D_MODEL), lambda i, j, k: (i, 0))
    more, epi = _residual_epi(next_gain)
    h2, hn_next = _mm(f"mlp_down{l}", (t // tm, 1, nsh),
                      u, pl.BlockSpec((tm, wsh), lambda i, j, k: (i, k)),
                      w2g, pl.BlockSpec((None, wsh, D_MODEL), lambda i, j, k: (k, 0, 0)), (1, 0),
                      _residual_outs(t, row, next_gain), extras=[(h, row)] + more, epi=epi)
    return h2, hn_next, (h, hn, u, w1g, w2g)


def _norm_bwd_outs(t, tm):
    return [(_sds((t, D_MODEL), F32), pl.BlockSpec((tm, D_MODEL), lambda i, j, k: (i, 0))),
            (_sds((t // tm, 1, D_MODEL), F32), pl.BlockSpec((None, 1, D_MODEL), lambda i, j, k: (i, 0, 0)))]


def _norm_bwd_epi(acc, xv, res, g):
    dx, dgr = _rms_bwd_rows(acc, xv, g, D_MODEL)
    return res + dx, jnp.sum(dgr, axis=0, keepdims=True)


def _mlp_bwd(l, dh, saved, norm_g):
    h, hn, u, w1g, w2g = saved
    t = h.shape[0]
    tm = _row_tile(t)
    nsh, _, wsh = w1g.shape
    tu = _row_tile(t, 2 * ROW_TILE)
    tile = pl.BlockSpec((tu, wsh), lambda i, j, k: (i, j))
    (da,) = _mm(f"mlp_du{l}", (t // tu, nsh, 1),
                dh, pl.BlockSpec((tu, D_MODEL), lambda i, j, k: (i, 0)),
                w2g, pl.BlockSpec((None, wsh, D_MODEL), lambda i, j, k: (j, 0, 0)), (1, 1),
                [(_sds((t, D_FF), BF16), tile)], extras=[(u, tile)],
                epi=lambda acc, uv: (2.0 * jnp.sqrt(uv.astype(F32)) * acc,))
    tw = _row_tile(t, 512)
    (dw2,) = _mm(f"mlp_dw2{l}", (1, 1, t // tw),
                 u, pl.BlockSpec((tw, D_FF), lambda i, j, k: (k, 0)),
                 dh, pl.BlockSpec((tw, D_MODEL), lambda i, j, k: (k, 0)), (0, 0),
                 [(_sds((D_FF, D_MODEL), BF16), pl.BlockSpec((D_FF, D_MODEL), lambda i, j, k: (0, 0)))])
    dw2 = dw2.reshape(nsh, wsh, D_MODEL)
    (dw1,) = _mm(f"mlp_dw1{l}", (1, 1, t // tw),
                 hn, pl.BlockSpec((tw, D_MODEL), lambda i, j, k: (k, 0)),
                 da, pl.BlockSpec((tw, D_FF), lambda i, j, k: (k, 0)), (0, 0),
                 [(_sds((nsh, D_MODEL, wsh), BF16), pl.BlockSpec((nsh, D_MODEL, wsh), lambda i, j, k: (0, 0, 0)))],
                 split=wsh)
    row = pl.BlockSpec((tm, D_MODEL), lambda i, j, k: (i, 0))
    dh_in, dg = _mm(f"mlp_dhn{l}", (t // tm, 1, nsh),
                    da, pl.BlockSpec((tm, wsh), lambda i, j, k: (i, k)),
                    w1g, pl.BlockSpec((None, D_MODEL, wsh), lambda i, j, k: (k, 0, 0)), (1, 1),
                    _norm_bwd_outs(t, tm), extras=[(h, row), (dh, row), (norm_g, VEC)], epi=_norm_bwd_epi)
    return dh_in, jnp.sum(dg, axis=0), dw1, dw2


def _ple_fwd(l, h, hn, p, wg, wp, next_gain):
    t = h.shape[0]
    tm = _row_tile(t, 512)
    row = pl.BlockSpec((tm, D_MODEL), lambda i, j, k: (i, 0))
    full = lambda r: pl.BlockSpec((r, D_MODEL), lambda i, j, k: (0, 0))
    (e,) = _mm(f"ple_proj{l}", (t // tm, 1, 1),
               p, pl.BlockSpec((None, None, tm, PLE_DIM), lambda i, j, k: (l, 0, i, 0)),
               wp, full(PLE_DIM), (1, 0), [(_sds((t, D_MODEL), F32), row)])

    def gate_epi(acc, hv, ev, *gain):
        gt = _sigmoid(acc)
        h_new = hv + gt * ev
        if not gain:
            return h_new, gt
        r = lax.rsqrt(jnp.mean(h_new * h_new, axis=-1, keepdims=True) + EPS)
        return h_new, gt, h_new * r * gain[0]

    f32_row, bf_row = (_sds((t, D_MODEL), F32), row), (_sds((t, D_MODEL), BF16), row)
    res = _mm(f"ple_gate{l}", (t // tm, 1, 1), hn, row, wg, full(D_MODEL), (1, 0),
              [f32_row, f32_row] + ([bf_row] if next_gain is not None else []),
              extras=[(h, row), (e, row)] + ([(next_gain, VEC)] if next_gain is not None else []), epi=gate_epi)
    h_out, gate = res[0], res[1]
    return h_out, (res[2] if next_gain is not None else None), (h, hn, gate, e)


def _ple_bwd(l, dh, saved, p, norm_g, wg, deps=()):
    h, hn, gate, e = saved
    t = h.shape[0]
    tm = _row_tile(t)
    tk = _row_tile(t, 512)
    de, dz = _ple_gate_bwd(f"ple_gate_bwd{l}", dh, gate, e)
    full = lambda r: pl.BlockSpec((r, D_MODEL), lambda i, j, k: (0, 0))
    rowk = pl.BlockSpec((tk, D_MODEL), lambda i, j, k: (k, 0))
    (dwp,) = _mm(f"ple_dwp{l}", (1, 1, t // tk),
                 p, pl.BlockSpec((None, None, tk, PLE_DIM), lambda i, j, k: (l, 0, k, 0)),
                 de, rowk, (0, 0), [(_sds((PLE_DIM, D_MODEL), BF16), full(PLE_DIM))], deps=deps)
    (dwg,) = _mm(f"ple_dwg{l}", (1, 1, t // tk), hn, rowk, dz, rowk, (0, 0),
                 [(_sds((D_MODEL, D_MODEL), BF16), full(D_MODEL))])
    row = pl.BlockSpec((tm, D_MODEL), lambda i, j, k: (i, 0))
    dh_in, dg = _mm(f"ple_dhn{l}", (t // tm, 1, 1), dz, row, wg, full(D_MODEL), (1, 1),
                    _norm_bwd_outs(t, tm), extras=[(h, row), (dh, row), (norm_g, VEC)], epi=_norm_bwd_epi)
    return dh_in, jnp.sum(dg, axis=0), dwg, dwp


def _ret_layer_fwd(x, norm_g, wri, fetch_wro, gn, cos, sin, next_gain, deps=()):
    t = x.shape[0]
    tm = _row_tile(t)
    nsh, _, wsh = wri.shape
    hn = _rms_fwd("mix_norm0", x, norm_g)
    (proj,) = _mm("ret_in", (t // tm, nsh, 1),
                  hn, pl.BlockSpec((tm, D_MODEL), lambda i, j, k: (i, 0)),
                  wri, pl.BlockSpec((None, D_MODEL, wsh), lambda i, j, k: (j, 0, 0)), (1, 0),
                  [(_sds((t, RET_IN), F32), pl.BlockSpec((tm, wsh), lambda i, j, k: (i, j)))], deps=deps)
    gated, outp, states = _ret_fwd(proj, cos, sin, gn)
    wro = fetch_wro((gated,))
    row = pl.BlockSpec((tm, D_MODEL), lambda i, j, k: (i, 0))
    kt = 512
    more, epi = _residual_epi(next_gain)
    h1, hn_next = _mm("ret_out", (t // tm, 1, RET_V_W // kt),
                      gated, pl.BlockSpec((tm, kt), lambda i, j, k: (i, k)),
                      wro, pl.BlockSpec((kt, D_MODEL), lambda i, j, k: (k, 0)), (1, 0),
                      _residual_outs(t, row, next_gain), extras=[(x, row)] + more, epi=epi)
    return h1, hn_next, (x, hn, proj, gated, outp, states, wro)


def _ret_layer_bwd(dh, saved, norm_g, wri, gn, cos, sin, emit, deps=()):
    x, hn, proj, gated, outp, states, wro = saved
    t = x.shape[0]
    tm = _row_tile(t)
    tk = _row_tile(t, 512)
    nsh, _, wsh = wri.shape
    (dgated,) = _mm("ret_dgated", (t // tm, RET_V_W // D_MODEL, 1),
                    dh, pl.BlockSpec((tm, D_MODEL), lambda i, j, k: (i, 0)),
                    wro, pl.BlockSpec((D_MODEL, D_MODEL), lambda i, j, k: (j, 0)), (1, 1),
                    [(_sds((t, RET_V_W), F32), pl.BlockSpec((tm, D_MODEL), lambda i, j, k: (i, j)))], deps=deps)
    (dwro,) = _mm("ret_dwro", (1, 1, t // tk),
                  gated, pl.BlockSpec((tk, RET_V_W), lambda i, j, k: (k, 0)),
                  dh, pl.BlockSpec((tk, D_MODEL), lambda i, j, k: (k, 0)), (0, 0),
                  [(_sds((RET_V_W, D_MODEL), BF16), pl.BlockSpec((RET_V_W, D_MODEL), lambda i, j, k: (0, 0)))])
    dproj, dgn = _ret_bwd(proj, cos, sin, gn, outp, states, dgated)
    half = nsh // 2
    (dwri,) = _mm("ret_dwri", (2, 1, t // tk),
                  hn, pl.BlockSpec((tk, D_MODEL), lambda i, j, k: (k, 0)),
                  dproj, pl.BlockSpec((tk, half * wsh), lambda i, j, k: (k, i)), (0, 0),
                  [(_sds((nsh, D_MODEL, wsh), BF16), pl.BlockSpec((half, D_MODEL, wsh), lambda i, j, k: (i, 0, 0)))],
                  split=wsh)
    deps = emit(dwro, dwri)
    row = pl.BlockSpec((tm, D_MODEL), lambda i, j, k: (i, 0))
    dx, dg = _mm("ret_dhn", (t // tm, 1, nsh),
                 dproj, pl.BlockSpec((tm, wsh), lambda i, j, k: (i, k)),
                 wri, pl.BlockSpec((None, D_MODEL, wsh), lambda i, j, k: (k, 0, 0)), (1, 1),
                 _norm_bwd_outs(t, tm), extras=[(x, row), (dh, row), (norm_g, VEC)], epi=_norm_bwd_epi, deps=deps)
    return dx, jnp.sum(dg, axis=0), dgn.reshape(RET_HEADS, RET_DV)


def _mla_layer_fwd(h, hn, wmi, qa, kva, wuq, wukv, gq, gk, wmo, tabs, next_gain):
    t = h.shape[0]
    tm = _row_tile(t)
    row = pl.BlockSpec((tm, D_MODEL), lambda i, j, k: (i, 0))
    (proj2,) = _mm("mla_in", (t // tm, 1, 1), hn, row,
                   wmi, pl.BlockSpec((D_MODEL, MLA_IN_PAD), lambda i, j, k: (0, 0)), (1, 0),
                   [(_sds((t, MLA_IN_PAD), F32), pl.BlockSpec((tm, MLA_IN_PAD), lambda i, j, k: (i, 0)))])
    cq, ckv = _mla_mid(proj2, qa, kva)
    head = pl.BlockSpec((None, tm, MLA_HD_PAD), lambda i, j, k: (j, i, 0))
    (q,) = _mm("mla_uq", (t // tm, MLA_HEADS, 1),
               cq, pl.BlockSpec((tm, MLA_Q_RANK), lambda i, j, k: (i, 0)),
               wuq, pl.BlockSpec((None, MLA_Q_RANK, MLA_HD_PAD), lambda i, j, k: (j, 0, 0)), (1, 0),
               [(_sds((MLA_HEADS, t, MLA_HD_PAD), F32), head)])
    (kv,) = _mm("mla_ukv", (t // tm, MLA_HEADS, 1),
                ckv, pl.BlockSpec((tm, MLA_KV_RANK), lambda i, j, k: (i, 0)),
                wukv, pl.BlockSpec((None, MLA_KV_RANK, MLA_HD_PAD), lambda i, j, k: (j, 0, 0)), (1, 0),
                [(_sds((MLA_HEADS, t, MLA_HD_PAD), F32), head)])
    qh, kh, vh = _mla_prep(q, kv, proj2, gq, gk, tabs)
    o, lse = _attn_fwd(qh, kh, vh)
    more, epi = _residual_epi(next_gain)
    h_out, hn_next = _mm("mla_out", (t // tm, 1, 1), o, row,
                         wmo, pl.BlockSpec((D_MODEL, D_MODEL), lambda i, j, k: (0, 0)), (1, 0),
                         _residual_outs(t, row, next_gain), extras=[(h, row)] + more, epi=epi)
    return h_out, hn_next, (h, hn, proj2, cq, ckv, q, kv, qh, kh, vh, o, lse)


def _mla_layer_bwd(dh, saved, norm_g, wmi, qa, kva, wuq, wukv, gq, gk, wmo, tabs, deps=()):
    h, hn, proj2, cq, ckv, q, kv, qh, kh, vh, o, lse = saved
    t = h.shape[0]
    tm = _row_tile(t)
    tk = _row_tile(t, 512)
    row = pl.BlockSpec((tm, D_MODEL), lambda i, j, k: (i, 0))
    rowk = pl.BlockSpec((tk, D_MODEL), lambda i, j, k: (k, 0))
    sq = pl.BlockSpec((D_MODEL, D_MODEL), lambda i, j, k: (0, 0))
    do, dob = _mm("mla_do", (t // tm, 1, 1), dh, row, wmo, sq, (1, 1),
                  [(_sds((t, D_MODEL), F32), row), (_sds((t, D_MODEL), BF16), row)], epi=lambda acc: (acc, acc),
                  deps=deps)
    (dwmo,) = _mm("mla_dwo", (1, 1, t // tk), o, rowk, dh, rowk, (0, 0), [(_sds((D_MODEL, D_MODEL), BF16), sq)])
    delta = _attn_delta(do, o, lse.shape[-1])
    dqt, dkh, dvh = _attn_bwd(qh, kh, vh, dob, lse, delta)
    dq, dkv, dkr, dgq, dgk = _mla_prep_bwd(q, kv, proj2, gq, gk, tabs, dqt, dkh, dvh)

    wide = MLA_HEADS * MLA_HD_PAD
    widek = pl.BlockSpec((tk, wide), lambda i, j, k: (k, 0))
    (dwuq,) = _mm("mla_dwuq", (1, 1, t // tk),
                  cq, pl.BlockSpec((tk, MLA_Q_RANK), lambda i, j, k: (k, 0)), dq, widek, (0, 0),
                  [(_sds((MLA_HEADS, MLA_Q_RANK, MLA_HD_PAD), BF16),
                    pl.BlockSpec((MLA_HEADS, MLA_Q_RANK, MLA_HD_PAD), lambda i, j, k: (0, 0, 0)))], split=MLA_HD_PAD)
    (dwukv,) = _mm("mla_dwukv", (1, 1, t // tk),
                   ckv, pl.BlockSpec((tk, MLA_KV_RANK), lambda i, j, k: (k, 0)), dkv, widek, (0, 0),
                   [(_sds((MLA_HEADS, MLA_KV_RANK, MLA_HD_PAD), BF16),
                     pl.BlockSpec((MLA_HEADS, MLA_KV_RANK, MLA_HD_PAD), lambda i, j, k: (0, 0, 0)))],
                   split=MLA_HD_PAD)
    side_by_side = lambda wg: wg.transpose(1, 0, 2).reshape(wg.shape[1], wide)
    widei = pl.BlockSpec((tm, wide), lambda i, j, k: (i, 0))
    (dcq,) = _mm("mla_dcq", (t // tm, 1, 1), dq, widei,
                 side_by_side(wuq), pl.BlockSpec((MLA_Q_RANK, wide), lambda i, j, k: (0, 0)), (1, 1),
                 [(_sds((t, MLA_Q_RANK), F32), pl.BlockSpec((tm, MLA_Q_RANK), lambda i, j, k: (i, 0)))])
    (dckv,) = _mm("mla_dckv", (t // tm, 1, 1), dkv, widei,
                  side_by_side(wukv), pl.BlockSpec((MLA_KV_RANK, wide), lambda i, j, k: (0, 0)), (1, 1),
                  [(_sds((t, MLA_KV_RANK), F32), pl.BlockSpec((tm, MLA_KV_RANK), lambda i, j, k: (i, 0)))])
    dproj2, dqa, dkva = _mla_mid_bwd(proj2, qa, kva, dcq, dckv, dkr)
    win = pl.BlockSpec((D_MODEL, MLA_IN_PAD), lambda i, j, k: (0, 0))
    (dwmi,) = _mm("mla_dwin", (1, 1, t // tk), hn, rowk,
                  dproj2, pl.BlockSpec((tk, MLA_IN_PAD), lambda i, j, k: (k, 0)), (0, 0),
                  [(_sds((D_MODEL, MLA_IN_PAD), BF16), win)])
    dh_in, dg = _mm("mla_dhn", (t // tm, 1, 1),
                    dproj2, pl.BlockSpec((tm, MLA_IN_PAD), lambda i, j, k: (i, 0)), wmi, win, (1, 1),
                    _norm_bwd_outs(t, tm), extras=[(h, row), (dh, row), (norm_g, VEC)], epi=_norm_bwd_epi)
    return dh_in, dict(mix=jnp.sum(dg, axis=0), wmi=dwmi, qa=dqa, kva=dkva, wuq=dwuq, wukv=dwukv, gq=dgq, gk=dgk,
                       wmo=dwmo)


def _local_step(x, p, target, w, fetch, emit=lambda group: ()):
    t = x.shape[0]
    inv = 1.0 / (ROPE_THETA ** (jnp.arange(0, RET_DK, 2, dtype=F32) / RET_DK))
    ang = jnp.arange(t, dtype=F32)[:, None] * inv[None, :]
    cos_r, sin_r = jnp.cos(ang), jnp.sin(ang)
    tabs = _mla_tables(t)
    row = lambda a, i: a[i:i + 1]

    h1, hn1, s_ret = _ret_layer_fwd(x, row(w['mix_norm'], 0), w['ret_w_in'],
                                    lambda after: fetch('ret_out', after)['ret_w_out'], w['ret_gn'], cos_r, sin_r,
                                    row(w['mlp_norm'], 0), deps=w['deps'])
    h2, hn2, s_mlp0 = _mlp_fwd(0, h1, hn1, fetch('mlp_w1_0', (h1,))['mlp_w1'],
                               lambda after: fetch('mlp_w2_0', after)['mlp_w2'], row(w['ple_norm'], 0))
    w0 = fetch('ple_0', (h2,))
    h3, hn3, s_ple0 = _ple_fwd(0, h2, hn2, p, w0['ple_gate_w'], w0['ple_proj_w'], row(w['mix_norm'], 1))
    wm = fetch('mla', (h3,))
    mla_w = (wm['mla_w_in'], w['mla_q_a_norm'], w['mla_kv_a_norm'], wm['mla_w_uq'], wm['mla_w_ukv'],
             w['mla_q_norm'], w['mla_k_norm'], wm['mla_w_out'], tabs)
    h4, hn4, s_mla = _mla_layer_fwd(h3, hn3, *mla_w, row(w['mlp_norm'], 1))
    w1 = fetch('layer_1', (h4,))
    h5, hn5, s_mlp1 = _mlp_fwd(1, h4, hn4, w1['mlp_w1'], lambda after: w1['mlp_w2'], row(w['ple_norm'], 1))
    y, _, s_ple1 = _ple_fwd(1, h5, hn5, p, w1['ple_gate_w'], w1['ple_proj_w'], None)

    dy, sq_err = _loss_head(y, target)

    n = N_DEV
    colsh = lambda a: a.reshape(a.shape[0], n, a.shape[1] // n).transpose(1, 0, 2)
    rowsh = lambda a: a.reshape(n, a.shape[0] // n, a.shape[1])
    big = {}

    def emit_group(group):
        big.update(group)
        return emit(group)

    dh5, dg_ple1, dwg1, dwp1 = _ple_bwd(1, dy, s_ple1, p, row(w['ple_norm'], 1), w1['ple_gate_w'])
    dh4, dg_mlp1, dw1_1, dw2_1 = _mlp_bwd(1, dh5, s_mlp1, row(w['mlp_norm'], 1))
    deps = emit_group({('ple_gate_w', 1): rowsh(dwg1), ('ple_proj_w', 1): colsh(dwp1),
                       ('mlp_w2', 1): dw2_1, ('mlp_w1', 1): dw1_1})
    dh3, gm = _mla_layer_bwd(dh4, s_mla, row(w['mix_norm'], 1), *mla_w, deps=deps)
    deps = emit_group({('mla_w_out', 0): rowsh(gm['wmo']), ('mla_w_uq', 0): gm['wuq'][:, :, :MLA_QKD],
                       ('mla_w_ukv', 0): gm['wukv'], ('mla_w_in', 0): rowsh(gm['wmi'][:, :MLA_IN])})
    dh2, dg_ple0, dwg0, dwp0 = _ple_bwd(0, dh3, s_ple0, p, row(w['ple_norm'], 0), w0['ple_gate_w'], deps=deps)
    dh1, dg_mlp0, dw1_0, dw2_0 = _mlp_bwd(0, dh2, s_mlp0, row(w['mlp_norm'], 0))
    deps = emit_group({('ple_gate_w', 0): rowsh(dwg0), ('ple_proj_w', 0): colsh(dwp0),
                       ('mlp_w2', 0): dw2_0, ('mlp_w1', 0): dw1_0})
    dx, dg_mix0, dgn = _ret_layer_bwd(
        dh1, s_ret, row(w['mix_norm'], 0), w['ret_w_in'], w['ret_gn'], cos_r, sin_r,
        lambda dwro, dwri: emit_group({('ret_w_out', 0): rowsh(dwro), ('ret_w_in', 0): dwri}), deps=deps)

    small = dict(
        mix_norm=[dg_mix0, gm['mix']], mlp_norm=[dg_mlp0, dg_mlp1], ple_norm=[dg_ple0, dg_ple1],
        ret_gn=dgn, mla_q_a_norm=gm['qa'], mla_kv_a_norm=gm['kva'], mla_q_norm=gm['gq'], mla_k_norm=gm['gk'],
    )
    return sq_err, dx, big, small


def _my_place():
    x, y, c = lax.axis_index("x"), lax.axis_index("y"), lax.axis_index("c")
    return x, y, c


def _flat(px, py, pc):
    return 4 * px + 2 * py + pc


def _peer(x, y, c, r):
    return (1 - x if r & 4 else x, 1 - y if r & 2 else y, 1 - c if r & 1 else c)


def _all_gather(arrays):
    n = len(arrays)

    def body(*refs):
        ins, outs = refs[:n], refs[n:2 * n]
        send_sems, recv_sems, local_sems = refs[2 * n:]
        x, y, c = _my_place()
        me, sibling = (x, y, c), (x, y, 1 - c)
        chips = [(1 - x, y), (x, 1 - y), (1 - x, 1 - y)]

        def copy(a, k, block, to, src=None):
            slot = outs[a].at[_flat(*block)]
            return pltpu.make_async_remote_copy(
                src_ref=slot if src is None else src, dst_ref=slot,
                send_sem=send_sems.at[a, k], recv_sem=recv_sems.at[a, k], device_id=to, device_id_type=MESH)

        mine = [pltpu.make_async_copy(ins[a], outs[a].at[_flat(*me)], local_sems.at[a]) for a in range(n)]
        for cp in mine:
            cp.start()
        first = []
        for a in range(n):
            first.append(copy(a, 0, me, sibling, src=ins[a]))
            first += [copy(a, 1 + j, me, (*chip, c), src=ins[a]) for j, chip in enumerate(chips)]
        for cp in first:
            cp.start()
        passed = []
        for a in range(n):
            for j, chip in enumerate(chips):
                copy(a, 1 + j, (*chip, c), me).wait_recv()
                passed.append(copy(a, 4 + j, (*chip, c), sibling))
                passed[-1].start()
        for a in range(n):
            copy(a, 0, sibling, me).wait_recv()
            for j, chip in enumerate(chips):
                copy(a, 4 + j, (*chip, 1 - c), me).wait_recv()
        for cp in first + passed:
            cp.wait_send()
        for cp in mine:
            cp.wait()

    return pl.pallas_call(
        body, name="all_gather_weights",
        in_specs=[ANY] * n, out_specs=[ANY] * n,
        out_shape=[_sds((N_DEV,) + a.shape, a.dtype) for a in arrays],
        scratch_shapes=[pltpu.SemaphoreType.DMA((n, 7)), pltpu.SemaphoreType.DMA((n, 7)),
                        pltpu.SemaphoreType.DMA((n,))],
    )(*arrays)


HBM = pl.BlockSpec(memory_space=pltpu.HBM)
SEMS = pl.BlockSpec(memory_space=pltpu.SEMAPHORE)
SIDE_EFFECT = pltpu.SideEffectType.DATAFLOW_SIDE_EFFECTING


def _rs_copies(x, y, c, srcs, lands, send_sems, recv_sems):
    copies = []
    for a in range(len(srcs)):
        for r in range(1, N_DEV):
            peer = _peer(x, y, c, r)
            k = a * (N_DEV - 1) + r - 1
            copies.append(pltpu.make_async_remote_copy(
                src_ref=srcs[a].at[_flat(*peer)], dst_ref=lands[a].at[r - 1],
                send_sem=send_sems.at[k], recv_sem=recv_sems.at[k], device_id=peer, device_id_type=MESH))
    return copies


def _rs_start(name, arrays):
    n = len(arrays)
    hbm = lambda a: pltpu.with_memory_space_constraint(a, pltpu.HBM)
    lands = [hbm(lax.empty((N_DEV - 1,) + a.shape[1:], a.dtype)) for a in arrays]

    def body(*refs):
        srcs, lnd = refs[:n], refs[n:2 * n]
        send_sems, recv_sems = refs[2 * n], refs[2 * n + 1]
        token = refs[-1]
        for cp in _rs_copies(*_my_place(), srcs, lnd, send_sems, recv_sems):
            cp.start()
        token[...] = jnp.zeros_like(token)

    outs = pl.pallas_call(
        body, name=name,
        in_specs=[HBM] * (2 * n),
        out_specs=[SEMS, SEMS] + [HBM] * (2 * n) + [pl.BlockSpec(memory_space=pltpu.VMEM)],
        out_shape=[pltpu.SemaphoreType.DMA((n * (N_DEV - 1),)), pltpu.SemaphoreType.DMA((n * (N_DEV - 1),))]
        + [pltpu.HBM(a.shape, a.dtype) for a in arrays] + [pltpu.HBM(l.shape, l.dtype) for l in lands]
        + [_sds((8, 128), F32)],
        input_output_aliases={i: 2 + i for i in range(2 * n)},
        compiler_params=pltpu.CompilerParams(has_side_effects=SIDE_EFFECT),
    )(*[hbm(a) for a in arrays], *lands)
    return outs[0], outs[1], outs[2:2 + n], outs[2 + n:2 + 2 * n], outs[-1]


def _rs_wait(name, send_sems, recv_sems, srcs, lands, after):
    n = len(srcs)

    def body(*refs):
        src_refs, lnd = refs[:n], refs[n:2 * n]
        send, recv = refs[2 * n], refs[2 * n + 1]
        for cp in _rs_copies(*_my_place(), src_refs, lnd, send, recv):
            cp.wait_send()
            cp.wait_recv()

    outs = pl.pallas_call(
        body, name=name,
        in_specs=[HBM] * (2 * n) + [SEMS, SEMS] + [ANY] * len(after),
        out_specs=[HBM] * (2 * n),
        out_shape=[pltpu.HBM(a.shape, a.dtype) for a in list(srcs) + list(lands)],
        input_output_aliases={i: i for i in range(2 * n)},
        compiler_params=pltpu.CompilerParams(has_side_effects=SIDE_EFFECT),
    )(*srcs, *lands, send_sems, recv_sems, *after)
    return outs[:n], outs[n:]


SMALL_PACK_ROWS = 16


def _all_reduce_small(rows, deps=()):
    n = len(rows)

    def body(*refs):
        ins = refs[:n]
        out_ref, mine, buf, send_sems, recv_sems = refs[n + len(deps):]
        x, y, c = _my_place()
        mine[...] = jnp.zeros_like(mine)
        for (r0, a), ref in zip(rows, ins):
            mine[r0:r0 + a.shape[0], 0:a.shape[1]] = ref[...]
        buf[_flat(x, y, c)] = mine[...]
        copies = []
        for r in range(1, N_DEV):
            peer = _peer(x, y, c, r)
            send = pltpu.make_async_remote_copy(
                src_ref=mine, dst_ref=buf.at[_flat(x, y, c)],
                send_sem=send_sems.at[r - 1], recv_sem=recv_sems.at[r - 1], device_id=peer, device_id_type=MESH)
            send.start()
            recv = pltpu.make_async_remote_copy(
                src_ref=mine, dst_ref=buf.at[_flat(*peer)],
                send_sem=send_sems.at[r - 1], recv_sem=recv_sems.at[r - 1], device_id=peer, device_id_type=MESH)
            copies.append((send, recv))
        for send, recv in copies:
            send.wait_send()
            recv.wait_recv()
        acc = buf[0]
        for s in range(1, N_DEV):
            acc = acc + buf[s]
        out_ref[...] = acc

    vm = pl.BlockSpec(memory_space=pltpu.VMEM)
    shape = (SMALL_PACK_ROWS, D_MODEL)
    return pl.pallas_call(
        body, name="all_reduce_small", in_specs=[vm] * n + [ANY] * len(deps), out_specs=vm,
        out_shape=_sds(shape, F32),
        scratch_shapes=[pltpu.VMEM(shape, F32), pltpu.VMEM((N_DEV,) + shape, F32),
                        pltpu.SemaphoreType.DMA((7,)), pltpu.SemaphoreType.DMA((7,))],
    )(*[a for _, a in rows], *deps)


def _adamw_math(w, g, m, v):
    m = ADAM_B1 * m + (1.0 - ADAM_B1) * g
    v = ADAM_B2 * v + (1.0 - ADAM_B2) * (g * g)
    m_hat = m / (1.0 - ADAM_B1 ** ADAM_STEP)
    v_hat = v / (1.0 - ADAM_B2 ** ADAM_STEP)
    delta = -ADAM_LR * (m_hat / (jnp.sqrt(v_hat) + ADAM_EPS) + ADAM_WD * w)
    return delta, m, v


def _adamw_big(name, w, m, v, srcs, lands, me):
    nl, rows, cols = w.shape
    tr = next(cand for cand in (256, 128, 64, 32, 16, 8) if rows % cand == 0)

    def body(me_ref, w_ref, m_ref, v_ref, *rest):
        src_refs, land_refs = rest[:nl], rest[nl:2 * nl]
        g_ref, d_ref, mo_ref, vo_ref = rest[2 * nl:]
        for layer in range(nl):
            @pl.when(pl.program_id(0) == layer)
            def _():
                g = src_refs[layer][...].astype(F32)
                for s in range(N_DEV - 1):
                    g = g + land_refs[layer][s].astype(F32)
                delta, mn, vn = _adamw_math(w_ref[...], g, m_ref[...], v_ref[...])
                g_ref[...] = g
                d_ref[...] = delta
                mo_ref[...] = mn
                vo_ref[...] = vn

    blk = pl.BlockSpec((None, tr, cols), lambda l, i, me_ref: (l, i, 0))
    own = pl.BlockSpec((None, tr, cols), lambda l, i, me_ref: (me_ref[0], i, 0))
    peers = pl.BlockSpec((N_DEV - 1, tr, cols), lambda l, i, me_ref: (0, i, 0))
    return pl.pallas_call(
        body, name=name,
        grid_spec=pltpu.PrefetchScalarGridSpec(
            num_scalar_prefetch=1, grid=(nl, rows // tr),
            in_specs=[blk, blk, blk] + [own] * nl + [peers] * nl, out_specs=[blk] * 4),
        out_shape=[_sds((nl, rows, cols), F32)] * 4,
        compiler_params=_cparams(("arbitrary", "arbitrary")),
    )(me, w, m, v, *srcs, *lands)


def _adamw_small(ws, gs, ms, vs):
    n = len(ws)

    def body(*refs):
        w_refs, g_refs, m_refs, v_refs = (refs[i * n:(i + 1) * n] for i in range(4))
        d_out, m_out, v_out = (refs[(4 + i) * n:(5 + i) * n] for i in range(3))
        for i in range(n):
            delta, mn, vn = _adamw_math(w_refs[i][...], g_refs[i][...], m_refs[i][...], v_refs[i][...])
            d_out[i][...] = delta
            m_out[i][...] = mn
            v_out[i][...] = vn

    vm = pl.BlockSpec(memory_space=pltpu.VMEM)
    outs = pl.pallas_call(
        body, name="adamw_small", in_specs=[vm] * (4 * n), out_specs=[vm] * (3 * n),
        out_shape=[_sds(a.shape, F32) for a in ws] * 3,
    )(*ws, *gs, *ms, *vs)
    return outs[:n], outs[n:2 * n], outs[2 * n:]


SMALL_ROWS = 16


def _pad_to(a, rows, cols):
    return jnp.pad(a, ((0, rows - a.shape[0]), (0, cols - a.shape[1])))


def _place_own(blocks):
    me = _flat(*_my_place())
    return [lax.dynamic_update_slice(lax.empty((N_DEV,) + b.shape, b.dtype), b[None], (me,) + (0,) * b.ndim)
            for b in blocks]


def _ag_copies(x, y, c, blocks, bufs, send_sems, recv_sems):
    sends, recvs = [], []
    for a in range(len(blocks)):
        for r in range(1, N_DEV):
            peer = _peer(x, y, c, r)
            k = a * (N_DEV - 1) + r - 1
            make = lambda place: pltpu.make_async_remote_copy(
                src_ref=blocks[a], dst_ref=bufs[a].at[_flat(*place)],
                send_sem=send_sems.at[k], recv_sem=recv_sems.at[k], device_id=peer, device_id_type=MESH)
            sends.append(make((x, y, c)))
            recvs.append(make(peer))
    return sends, recvs


def _ag_start(groups, after):
    flat = [pair for g in groups for pair in g]
    n, ng = len(flat), len(groups)
    hbm = lambda a: pltpu.with_memory_space_constraint(a, pltpu.HBM)

    def body(*refs):
        blocks, bufs = refs[:n], refs[n:2 * n]
        sems = refs[2 * n + len(after):2 * n + len(after) + 2 * ng]
        x, y, c = _my_place()
        at = 0
        for gi, g in enumerate(groups):
            sends, _ = _ag_copies(x, y, c, blocks[at:at + len(g)], bufs[at:at + len(g)], sems[2 * gi], sems[2 * gi + 1])
            for cp in sends:
                cp.start()
            at += len(g)
        refs[-1][...] = jnp.zeros_like(refs[-1])

    sem_shapes = [pltpu.SemaphoreType.DMA((len(g) * (N_DEV - 1),)) for g in groups for _ in range(2)]
    outs = pl.pallas_call(
        body, name="gather_start",
        in_specs=[HBM] * (2 * n) + [ANY] * len(after),
        out_specs=[SEMS] * (2 * ng) + [HBM] * (2 * n) + [pl.BlockSpec(memory_space=pltpu.VMEM)],
        out_shape=sem_shapes + [pltpu.HBM(b.shape, b.dtype) for b, _ in flat]
        + [pltpu.HBM(u.shape, u.dtype) for _, u in flat] + [_sds((8, 128), F32)],
        input_output_aliases={i: 2 * ng + i for i in range(2 * n)},
        compiler_params=pltpu.CompilerParams(has_side_effects=SIDE_EFFECT),
    )(*[hbm(b) for b, _ in flat], *[hbm(u) for _, u in flat], *after)
    blocks_thru, bufs_thru = outs[2 * ng:2 * ng + n], outs[2 * ng + n:2 * ng + 2 * n]
    started, at = [], 0
    for gi, g in enumerate(groups):
        started.append((outs[2 * gi], outs[2 * gi + 1], blocks_thru[at:at + len(g)], bufs_thru[at:at + len(g)]))
        at += len(g)
    return started, outs[-1]


def _ag_wait(name, send_sems, recv_sems, blocks, bufs, after):
    n = len(blocks)

    def body(*refs):
        sends, recvs = _ag_copies(*_my_place(), refs[:n], refs[n:2 * n], refs[2 * n], refs[2 * n + 1])
        for s, r in zip(sends, recvs):
            s.wait_send()
            r.wait_recv()

    outs = pl.pallas_call(
        body, name=name,
        in_specs=[HBM] * (2 * n) + [SEMS, SEMS] + [ANY] * len(after),
        out_specs=[HBM] * (2 * n),
        out_shape=[pltpu.HBM(a.shape, a.dtype) for a in list(blocks) + list(bufs)],
        input_output_aliases={i: i for i in range(2 * n)},
        compiler_params=pltpu.CompilerParams(has_side_effects=SIDE_EFFECT),
    )(*blocks, *bufs, send_sems, recv_sems, *after)
    return outs[n:]


def _prepare_weights(p):
    n = N_DEV
    bf = lambda a: a.astype(BF16)
    gn_pack = jnp.concatenate([
        _pad_to(p['ret_gn'][0], RET_HEADS, 128), _pad_to(p['mla_q_a_norm'], 1, 128),
        _pad_to(p['mla_kv_a_norm'], 1, 128), jnp.zeros((2, 128), F32)], axis=0)
    ple = lambda l: [bf(p['ple_gate_w'][l]), bf(p['ple_proj_w'][l])]
    names = ('ret_out', 'mlp_w1_0', 'mlp_w2_0', 'ple_0', 'mla', 'layer_1')
    later = [[bf(p['ret_w_out'][0])], [bf(p['mlp_w1'][0])], [bf(p['mlp_w2'][0])], ple(0),
             [bf(p['mla_w_in'][0]), bf(p['mla_w_uq'][0]), bf(p['mla_w_ukv'][0]), bf(p['mla_w_out'][0])],
             [bf(p['mlp_w1'][1]), bf(p['mlp_w2'][1])] + ple(1)]
    bufs = _place_own([b for g in later for b in g])
    pack, wri = _all_gather([gn_pack, bf(p['ret_w_in'][0])])
    groups, at = [], 0
    for g in later:
        groups.append(list(zip(g, bufs[at:at + len(g)])))
        at += len(g)
    started, token = _ag_start(groups, (wri,))

    w = {k: p[k] for k in ('mix_norm', 'mlp_norm', 'ple_norm')}
    w['ret_gn'] = pack[:, :RET_HEADS, :RET_DV // n].transpose(1, 0, 2).reshape(RET_HEADS, RET_DV)
    w['mla_q_a_norm'] = pack[:, RET_HEADS, :MLA_Q_RANK // n].reshape(1, MLA_Q_RANK)
    w['mla_kv_a_norm'] = pack[:, RET_HEADS + 1, :MLA_KV_RANK // n].reshape(1, MLA_KV_RANK)
    w['ret_w_in'] = wri
    w['mla_q_norm'] = _pad_to(p['mla_q_norm'], 1, MLA_HD_PAD)
    w['mla_k_norm'] = _pad_to(p['mla_k_norm'], 1, MLA_HD_PAD)
    w['deps'] = (token,)

    def fetch(name, after):
        got = list(_ag_wait("gather_wait_" + name, *started[names.index(name)], after))
        if name == 'ret_out':
            return dict(ret_w_out=got[0].reshape(RET_V_W, D_MODEL))
        if name == 'mla':
            wmi, wuq, wukv, wmo = got
            return dict(mla_w_in=jnp.pad(wmi.reshape(D_MODEL, MLA_IN), ((0, 0), (0, MLA_IN_PAD - MLA_IN))),
                        mla_w_uq=jnp.pad(wuq, ((0, 0), (0, 0), (0, MLA_HD_PAD - MLA_QKD))),
                        mla_w_ukv=wukv, mla_w_out=wmo.reshape(D_MODEL, D_MODEL))
        out = {}
        if name in ('mlp_w1_0', 'layer_1'):
            out['mlp_w1'] = got.pop(0)
        if name in ('mlp_w2_0', 'layer_1'):
            out['mlp_w2'] = got.pop(0)
        if name in ('ple_0', 'layer_1'):
            out['ple_gate_w'] = got[0].reshape(D_MODEL, D_MODEL)
            out['ple_proj_w'] = got[1].transpose(1, 0, 2).reshape(PLE_DIM, D_MODEL)
        return out

    return w, fetch


def _small_grads(small, after):
    rows = [(0, small['mix_norm'][0]), (1, small['mix_norm'][1]), (2, small['mlp_norm'][0]),
            (3, small['mlp_norm'][1]), (4, small['ple_norm'][0]), (5, small['ple_norm'][1]),
            (6, small['ret_gn']), (10, small['mla_q_a_norm']), (11, small['mla_kv_a_norm']),
            (12, small['mla_q_norm']), (13, small['mla_k_norm'])]
    gs = _all_reduce_small(rows, after)
    me = _flat(*_my_place())
    n = N_DEV
    return dict(
        mix_norm=gs[0:2], mlp_norm=gs[2:4], ple_norm=gs[4:6],
        ret_gn=lax.dynamic_slice(gs, (6, me * (RET_DV // n)), (RET_HEADS, RET_DV // n)),
        mla_q_a_norm=lax.dynamic_slice(gs, (10, me * (MLA_Q_RANK // n)), (1, MLA_Q_RANK // n)),
        mla_kv_a_norm=lax.dynamic_slice(gs, (11, me * (MLA_KV_RANK // n)), (1, MLA_KV_RANK // n)),
        mla_q_norm=gs[12:13, :MLA_QKD], mla_k_norm=gs[13:14, :MLA_QKD])


def kernel(x, p, mix_norm, ret_w_in, ret_gn, ret_w_out, mla_w_in, mla_q_a_norm, mla_kv_a_norm, mla_w_uq, mla_w_ukv, mla_q_norm, mla_k_norm, mla_w_out, mlp_norm, mlp_w1, mlp_w2, ple_norm, ple_gate_w, ple_proj_w, loss_target, m_mix_norm, m_ret_w_in, m_ret_gn, m_ret_w_out, m_mla_w_in, m_mla_q_a_norm, m_mla_kv_a_norm, m_mla_w_uq, m_mla_w_ukv, m_mla_q_norm, m_mla_k_norm, m_mla_w_out, m_mlp_norm, m_mlp_w1, m_mlp_w2, m_ple_norm, m_ple_gate_w, m_ple_proj_w, v_mix_norm, v_ret_w_in, v_ret_gn, v_ret_w_out, v_mla_w_in, v_mla_q_a_norm, v_mla_kv_a_norm, v_mla_w_uq, v_mla_w_ukv, v_mla_q_norm, v_mla_k_norm, v_mla_w_out, v_mlp_norm, v_mlp_w1, v_mlp_w2, v_ple_norm, v_ple_gate_w, v_ple_proj_w):
    given = dict(locals())
    params = {n: given[n] for n in WEIGHTS}
    w, fetch = _prepare_weights(params)

    started = []

    def emit(group):
        keys = list(group)
        send, recv, srcs, lands, token = _rs_start(f"rs_start{len(started)}", [group[k] for k in keys])
        started.append((keys, send, recv, srcs, lands))
        return (token,)

    sq_err, grad_x, _, small = _local_step(x[0], p, loss_target[0], w, fetch, emit)
    loss = lax.psum(0.5 / D_MODEL * sq_err[0, 0], ("x", "y", "c"))

    grads, deltas, new_m, new_v = {}, {}, {}, {}

    def small_updates(after):
        sg = _small_grads(small, after)
        two_d = lambda a: a.reshape(-1, a.shape[-1])
        d_s, m_s, v_s = _adamw_small(
            [two_d(params[n]) for n in SMALL], [sg[n] for n in SMALL],
            [two_d(given["m_" + n]) for n in SMALL], [two_d(given["v_" + n]) for n in SMALL])
        for i, n in enumerate(SMALL):
            shape = params[n].shape
            grads[n], deltas[n], new_m[n], new_v[n] = (a.reshape(shape) for a in (sg[n], d_s[i], m_s[i], v_s[i]))
        return (d_s[0],)

    me = _flat(*_my_place()).astype(jnp.int32).reshape(1)
    after = (grad_x,)
    src_of, land_of = {}, {}
    for gi, (keys, send, recv, srcs, lands) in enumerate(started):
        if gi == len(started) - 1:
            after = small_updates(after)
        srcs, lands = _rs_wait(f"rs_wait{gi}", send, recv, srcs, lands, after)
        for k, s, l in zip(keys, srcs, lands):
            src_of[k], land_of[k] = s, l
        done = [n for n in BIG if n not in grads and all((n, l) in src_of for l in range(params[n].shape[0]))]
        for n in done:
            layers = range(params[n].shape[0])
            grads[n], deltas[n], new_m[n], new_v[n] = _adamw_big(
                "adamw_" + n, params[n], given["m_" + n], given["v_" + n],
                [src_of[(n, l)] for l in layers], [land_of[(n, l)] for l in layers], me)
        if done:
            after = (deltas[done[-1]],)

    return (loss, grad_x[None], *[grads[n] for n in WEIGHTS], *[deltas[n] for n in WEIGHTS],
            *[new_m[n] for n in WEIGHTS], *[new_v[n] for n in WEIGHTS])
```

```python
import functools
import math

import jax
import jax.numpy as jnp
from jax import lax
from jax.experimental import pallas as pl
from jax.experimental.pallas import tpu as pltpu

F32 = jnp.float32
BF16 = jnp.bfloat16
MESH = pl.DeviceIdType.MESH
ANY = pl.BlockSpec(memory_space=pl.ANY)

N_DEV = 8
D_MODEL = 1024
CHUNK = 64
EPS = 1e-6
ROPE_THETA = 10000.0
RET_HEADS = 4
RET_DK = 256
RET_DV = 512
RET_QK_W = RET_HEADS * RET_DK
RET_V_W = RET_HEADS * RET_DV
RET_IN = 2 * RET_QK_W + 2 * RET_V_W
MLA_HEADS = 8
MLA_NOPE = 128
MLA_ROPE = 64
MLA_QKD = MLA_NOPE + MLA_ROPE
MLA_VD = 128
MLA_Q_RANK = 384
MLA_KV_RANK = 256
MLA_IN = MLA_Q_RANK + MLA_KV_RANK + MLA_ROPE
MLA_IN_PAD = 768
MLA_HD_PAD = 256
D_FF = 4096
PLE_DIM = 256
ATT_SCALE = MLA_QKD ** -0.5
LOG2E = 1.4426950408889634
ATT_EXP2 = ATT_SCALE * LOG2E

ADAM_LR = 0.001
ADAM_B1 = 0.9
ADAM_B2 = 0.999
ADAM_EPS = 1e-08
ADAM_WD = 0.01
ADAM_STEP = 10

VMEM_LIMIT = 52 * 1024 * 1024
ROW_TILE = 1024
RET_ROWS = 256
ATT_BLOCK = 256
ATT_QROWS = 1024
ATT_KROWS = 1024
ATT_HEADS = 2

WEIGHTS = ['mix_norm', 'ret_w_in', 'ret_gn', 'ret_w_out', 'mla_w_in', 'mla_q_a_norm', 'mla_kv_a_norm',
           'mla_w_uq', 'mla_w_ukv', 'mla_q_norm', 'mla_k_norm', 'mla_w_out', 'mlp_norm', 'mlp_w1', 'mlp_w2',
           'ple_norm', 'ple_gate_w', 'ple_proj_w']
BIG = ['ret_w_in', 'ret_w_out', 'mla_w_in', 'mla_w_uq', 'mla_w_ukv', 'mla_w_out', 'mlp_w1', 'mlp_w2',
       'ple_gate_w', 'ple_proj_w']
SMALL = [w for w in WEIGHTS if w not in BIG]


def _cparams(sem=None):
    return pltpu.CompilerParams(dimension_semantics=sem, vmem_limit_bytes=VMEM_LIMIT)


def _dot(a, b, ca, cb):
    return lax.dot_general(a, b, (((ca,), (cb,)), ((), ())), preferred_element_type=F32)


def _bf(v):
    return v if v.dtype == BF16 else v.astype(BF16)


def _sigmoid(z):
    return 1.0 / (1.0 + jnp.exp(-z))


def _mm(name, grid, a, a_spec, b, b_spec, contract, outs, extras=(), epi=None, deps=(), split=None):
    nk = grid[2]
    n_ex, n_out, n_dep = len(extras), len(outs), len(deps)
    acc_shape = tuple(d for d in outs[0][1].block_shape if d is not None)
    if split is not None:
        acc_shape = (acc_shape[1], acc_shape[0] * split)

    def body(*refs):
        a_ref, b_ref = refs[:2]
        ex_refs = refs[2:2 + n_ex]
        out_refs = refs[2 + n_ex + n_dep:2 + n_ex + n_dep + n_out]

        def product():
            return _dot(_bf(a_ref[...]), _bf(b_ref[...]), contract[0], contract[1])

        def finish(acc):
            if split is not None:
                for j in range(acc_shape[1] // split):
                    out_refs[0][j] = acc[:, j * split:(j + 1) * split].astype(out_refs[0].dtype)
                return
            acc = acc[...]
            res = epi(acc, *[r[...] for r in ex_refs]) if epi is not None else (acc,)
            for o, r in zip(out_refs, res):
                o[...] = r.astype(o.dtype)

        if nk == 1:
            finish(product())
        else:
            acc_ref = refs[-1]
            k = pl.program_id(2)

            @pl.when(k == 0)
            def _():
                acc_ref[...] = jnp.zeros_like(acc_ref)

            acc_ref[...] += product()

            @pl.when(k == nk - 1)
            def _():
                finish(acc_ref)

    return pl.pallas_call(
        body, name=name, grid=grid,
        in_specs=[a_spec, b_spec] + [s for _, s in extras] + [ANY] * n_dep,
        out_specs=[s for _, s in outs],
        out_shape=[s for s, _ in outs],
        scratch_shapes=[pltpu.VMEM(acc_shape, F32)] if nk > 1 else [],
        compiler_params=_cparams(("parallel", "parallel", "arbitrary")),
    )(a, b, *[x for x, _ in extras], *deps)


def _sds(shape, dtype):
    return jax.ShapeDtypeStruct(shape, dtype)


def _row_tile(t, cap=ROW_TILE):
    return min(cap, t)


def _rms_fwd(name, x, g):
    t, d = x.shape
    tm = _row_tile(t)

    def body(x_ref, g_ref, o_ref):
        xv = x_ref[...]
        r = lax.rsqrt(jnp.mean(xv * xv, axis=-1, keepdims=True) + EPS)
        o_ref[...] = (xv * r * g_ref[...]).astype(o_ref.dtype)

    return pl.pallas_call(
        body, name=name, grid=(t // tm,),
        in_specs=[pl.BlockSpec((tm, d), lambda i: (i, 0)), pl.BlockSpec((1, d), lambda i: (0, 0))],
        out_specs=pl.BlockSpec((tm, d), lambda i: (i, 0)),
        out_shape=_sds((t, d), BF16),
        compiler_params=_cparams(("parallel",)),
    )(x, g)


def _rms_bwd_rows(dy, xv, g, n):
    r = lax.rsqrt(jnp.sum(xv * xv, axis=-1, keepdims=True) / n + EPS)
    xh = xv * r
    dxh = dy * g
    dx = r * (dxh - xh * (jnp.sum(dxh * xh, axis=-1, keepdims=True) / n))
    return dx, dy * xh


def _rms_bwd(name, dy, x, g, res):
    t, d = x.shape
    tm = _row_tile(t, 512)

    def body(dy_ref, x_ref, g_ref, res_ref, dx_ref, dg_ref):
        @pl.when(pl.program_id(0) == 0)
        def _():
            dg_ref[...] = jnp.zeros_like(dg_ref)

        dx, dgr = _rms_bwd_rows(dy_ref[...], x_ref[...], g_ref[...], d)
        dx_ref[...] = res_ref[...] + dx
        dg_ref[...] += jnp.sum(dgr, axis=0, keepdims=True)

    row = pl.BlockSpec((tm, d), lambda i: (i, 0))
    vec = pl.BlockSpec((1, d), lambda i: (0, 0))
    return pl.pallas_call(
        body, name=name, grid=(t // tm,),
        in_specs=[row, row, vec, row], out_specs=[row, vec],
        out_shape=[_sds((t, d), F32), _sds((1, d), F32)],
        compiler_params=_cparams(("arbitrary",)),
    )(dy, x, g, res)


def _loss_head(y, target):
    t, d = y.shape
    tm = _row_tile(t)

    def body(y_ref, t_ref, dy_ref, l_ref):
        @pl.when(pl.program_id(0) == 0)
        def _():
            l_ref[...] = jnp.zeros_like(l_ref)

        e = y_ref[...] - t_ref[...]
        dy_ref[...] = e / d
        l_ref[...] += jnp.sum(jnp.sum(e * e, axis=-1, keepdims=True), axis=0, keepdims=True)

    row = pl.BlockSpec((tm, d), lambda i: (i, 0))
    return pl.pallas_call(
        body, name="loss_head", grid=(t // tm,),
        in_specs=[row, row], out_specs=[row, pl.BlockSpec((8, 128), lambda i: (0, 0))],
        out_shape=[_sds((t, d), F32), _sds((8, 128), F32)],
        compiler_params=_cparams(("arbitrary",)),
    )(y, target)


def _ple_gate_bwd(name, dh, gate, e):
    t, d = dh.shape
    tm = _row_tile(t)

    def body(dh_ref, g_ref, e_ref, de_ref, dz_ref):
        dh_v, gt = dh_ref[...], g_ref[...]
        de_ref[...] = (dh_v * gt).astype(BF16)
        dz_ref[...] = (dh_v * e_ref[...] * (gt * (1.0 - gt))).astype(BF16)

    row = pl.BlockSpec((tm, d), lambda i: (i, 0))
    return pl.pallas_call(
        body, name=name, grid=(t // tm,), in_specs=[row, row, row], out_specs=[row, row],
        out_shape=[_sds((t, d), BF16), _sds((t, d), BF16)],
        compiler_params=_cparams(("parallel",)),
    )(dh, gate, e)


def _rope_half(v, cos, sin):
    half = v.shape[-1] // 2
    v1, v2 = v[:, :half], v[:, half:]
    return jnp.concatenate([v1 * cos - v2 * sin, v2 * cos + v1 * sin], axis=-1)


def _ret_consts():
    lg = jnp.log(1.0 - 2.0 ** (-5.0 - jnp.arange(RET_HEADS, dtype=F32)))
    idx = jnp.arange(CHUNK, dtype=F32)
    intra = jnp.exp(lg[:, None, None] * jnp.abs(idx[:, None] - idx[None, :]))
    qdec = jnp.exp(lg[:, None] * (idx + 1.0))
    kdec = jnp.exp(lg[:, None] * (CHUNK - 1.0 - idx))
    cdec = jnp.exp(lg * CHUNK)
    qdec = jnp.broadcast_to(qdec[:, :, None], (RET_HEADS, CHUNK, RET_DK))
    kdec = jnp.broadcast_to(kdec[:, :, None], (RET_HEADS, CHUNK, RET_DK))
    cdec = jnp.broadcast_to(cdec[:, None, None], (RET_HEADS, 1, RET_DV))
    return intra, qdec, kdec, cdec


def _ret_specs(rb, rev_nb=None):
    blk = (lambda i: i) if rev_nb is None else (lambda i: rev_nb - 1 - i)
    full = lambda shape: pl.BlockSpec(shape, lambda i: (0,) * len(shape))
    return dict(
        proj=pl.BlockSpec((rb, RET_IN), lambda i: (blk(i), 0)),
        tab=pl.BlockSpec((rb, RET_DK // 2), lambda i: (blk(i), 0)),
        vw=pl.BlockSpec((rb, RET_V_W), lambda i: (blk(i), 0)),
        st=pl.BlockSpec((rb // CHUNK, RET_HEADS, RET_DK, RET_DV), lambda i: (blk(i), 0, 0, 0)),
        gn=full((RET_HEADS, 1, RET_DV)),
        intra=full((RET_HEADS, CHUNK, CHUNK)),
        dec=full((RET_HEADS, CHUNK, RET_DK)),
        cdec=full((RET_HEADS, 1, RET_DV)),
    )


def _ret_fwd(proj, cos, sin, gn):
    t = proj.shape[0]
    rb = min(RET_ROWS, t)
    cpb = rb // CHUNK
    intra, qdec, kdec, cdec = _ret_consts()
    sp = _ret_specs(rb)

    def body(proj_ref, cos_ref, sin_ref, gn_ref, intra_ref, qd_ref, kd_ref, cd_ref,
             gated_ref, outp_ref, st_ref, s_ref):
        @pl.when(pl.program_id(0) == 0)
        def _():
            s_ref[...] = jnp.zeros_like(s_ref)

        def chunk(c, carry):
            rows = pl.ds(pl.multiple_of(c * CHUNK, CHUNK), CHUNK)
            cs, sn = cos_ref[rows, :], sin_ref[rows, :]
            for h in range(RET_HEADS):
                q = proj_ref[rows, h * RET_DK:(h + 1) * RET_DK]
                k = proj_ref[rows, RET_QK_W + h * RET_DK:RET_QK_W + (h + 1) * RET_DK]
                v = proj_ref[rows, 2 * RET_QK_W + h * RET_DV:2 * RET_QK_W + (h + 1) * RET_DV]
                g = proj_ref[rows, 2 * RET_QK_W + RET_V_W + h * RET_DV:2 * RET_QK_W + RET_V_W + (h + 1) * RET_DV]
                qr = _rope_half(q, cs, sn)
                kr = _rope_half(k, cs, sn) * (RET_DK ** -0.5)
                qb, kb, vb = qr.astype(BF16), kr.astype(BF16), v.astype(BF16)
                sc = _dot(qb, kb, 1, 1) * intra_ref[h]
                inner = _dot(sc.astype(BF16), vb, 1, 0)
                s_old = s_ref[h]
                sb = s_old.astype(BF16)
                st_ref[c, h] = sb
                cross = _dot((qr * qd_ref[h]).astype(BF16), sb, 1, 0)
                out = inner + cross
                s_ref[h] = s_old * cd_ref[h] + _dot((kr * kd_ref[h]).astype(BF16), vb, 0, 0)
                r = lax.rsqrt(jnp.mean(out * out, axis=-1, keepdims=True) + EPS)
                y = out * r * gn_ref[h]
                cols = slice(h * RET_DV, (h + 1) * RET_DV)
                gated_ref[rows, cols] = (g * _sigmoid(g) * y).astype(BF16)
                outp_ref[rows, cols] = out
            return carry

        lax.fori_loop(0, cpb, chunk, 0)

    return pl.pallas_call(
        body, name="ret_fwd", grid=(t // rb,),
        in_specs=[sp['proj'], sp['tab'], sp['tab'], sp['gn'], sp['intra'], sp['dec'], sp['dec'], sp['cdec']],
        out_specs=[sp['vw'], sp['vw'], sp['st']],
        out_shape=[_sds((t, RET_V_W), BF16), _sds((t, RET_V_W), F32),
                   _sds((t // CHUNK, RET_HEADS, RET_DK, RET_DV), BF16)],
        scratch_shapes=[pltpu.VMEM((RET_HEADS, RET_DK, RET_DV), F32)],
        compiler_params=_cparams(("arbitrary",)),
    )(proj, cos, sin, gn.reshape(RET_HEADS, 1, RET_DV), intra, qdec, kdec, cdec)


def _ret_bwd(proj, cos, sin, gn, outp, states, dgated):
    t = proj.shape[0]
    rb = min(RET_ROWS, t)
    cpb = rb // CHUNK
    nb = t // rb
    intra, qdec, kdec, cdec = _ret_consts()
    sp = _ret_specs(rb, rev_nb=nb)

    def body(proj_ref, cos_ref, sin_ref, gn_ref, intra_ref, qd_ref, kd_ref, cd_ref, outp_ref, st_ref, dgt_ref,
             dproj_ref, dgn_ref, ds_ref):
        @pl.when(pl.program_id(0) == 0)
        def _():
            ds_ref[...] = jnp.zeros_like(ds_ref)
            dgn_ref[...] = jnp.zeros_like(dgn_ref)

        def chunk(cc, carry):
            c = cpb - 1 - cc
            rows = pl.ds(pl.multiple_of(c * CHUNK, CHUNK), CHUNK)
            cs, sn = cos_ref[rows, :], sin_ref[rows, :]
            for h in range(RET_HEADS):
                q = proj_ref[rows, h * RET_DK:(h + 1) * RET_DK]
                k = proj_ref[rows, RET_QK_W + h * RET_DK:RET_QK_W + (h + 1) * RET_DK]
                v = proj_ref[rows, 2 * RET_QK_W + h * RET_DV:2 * RET_QK_W + (h + 1) * RET_DV]
                g = proj_ref[rows, 2 * RET_QK_W + RET_V_W + h * RET_DV:2 * RET_QK_W + RET_V_W + (h + 1) * RET_DV]
                cols = slice(h * RET_DV, (h + 1) * RET_DV)
                qr = _rope_half(q, cs, sn)
                kr = _rope_half(k, cs, sn) * (RET_DK ** -0.5)
                qb, kb, vb = qr.astype(BF16), kr.astype(BF16), v.astype(BF16)
                qdb = (qr * qd_ref[h]).astype(BF16)
                kdb = (kr * kd_ref[h]).astype(BF16)
                out = outp_ref[rows, cols]
                dgt = dgt_ref[rows, cols]
                gnh = gn_ref[h]
                r = lax.rsqrt(jnp.mean(out * out, axis=-1, keepdims=True) + EPS)
                xh = out * r
                sg = _sigmoid(g)
                dgate = dgt * (xh * gnh) * (sg * (1.0 + g * (1.0 - sg)))
                dy = dgt * (g * sg)
                dgn_ref[h] += jnp.sum(dy * xh, axis=0, keepdims=True)
                dxh = dy * gnh
                dout = r * (dxh - xh * jnp.mean(dxh * xh, axis=-1, keepdims=True))
                doutb = dout.astype(BF16)
                itr = intra_ref[h]
                pb = (_dot(qb, kb, 1, 1) * itr).astype(BF16)
                dv = _dot(pb, doutb, 0, 0)
                dsc = (_dot(doutb, vb, 1, 1) * itr).astype(BF16)
                dq = _dot(dsc, kb, 1, 0)
                dk = _dot(dsc, qb, 0, 0)
                dq = dq + _dot(doutb, st_ref[c, h], 1, 1) * qd_ref[h]
                ds_new = ds_ref[h]
                dsb = ds_new.astype(BF16)
                dk = dk + _dot(vb, dsb, 1, 1) * kd_ref[h]
                dv = dv + _dot(kdb, dsb, 1, 0)
                ds_ref[h] = ds_new * cd_ref[h] + _dot(qdb, doutb, 0, 0)
                dproj_ref[rows, h * RET_DK:(h + 1) * RET_DK] = _rope_half(dq, cs, -sn).astype(BF16)
                dproj_ref[rows, RET_QK_W + h * RET_DK:RET_QK_W + (h + 1) * RET_DK] = (
                    _rope_half(dk * (RET_DK ** -0.5), cs, -sn).astype(BF16))
                dproj_ref[rows, 2 * RET_QK_W + h * RET_DV:2 * RET_QK_W + (h + 1) * RET_DV] = dv.astype(BF16)
                dproj_ref[rows, 2 * RET_QK_W + RET_V_W + h * RET_DV:
                          2 * RET_QK_W + RET_V_W + (h + 1) * RET_DV] = dgate.astype(BF16)
            return carry

        lax.fori_loop(0, cpb, chunk, 0)

    return pl.pallas_call(
        body, name="ret_bwd", grid=(nb,),
        in_specs=[sp['proj'], sp['tab'], sp['tab'], sp['gn'], sp['intra'], sp['dec'], sp['dec'], sp['cdec'],
                  sp['vw'], sp['st'], sp['vw']],
        out_specs=[sp['proj'], sp['gn']],
        out_shape=[_sds((t, RET_IN), BF16), _sds((RET_HEADS, 1, RET_DV), F32)],
        scratch_shapes=[pltpu.VMEM((RET_HEADS, RET_DK, RET_DV), F32)],
        compiler_params=_cparams(("arbitrary",)),
    )(proj, cos, sin, gn.reshape(RET_HEADS, 1, RET_DV), intra, qdec, kdec, cdec, outp, states, dgated)


def _mla_tables(t):
    half = MLA_ROPE // 2
    inv = 1.0 / (ROPE_THETA ** (jnp.arange(0, MLA_ROPE, 2, dtype=F32) / MLA_ROPE))
    ang = jnp.arange(t, dtype=F32)[:, None] * inv[None, :]
    cos, sin = jnp.cos(ang), jnp.sin(ang)
    z = jnp.zeros((t, half), F32)
    c = jnp.concatenate([cos, cos, z, z], axis=1)
    s1 = jnp.concatenate([-sin, z, z, z], axis=1)
    s2 = jnp.concatenate([z, sin, z, z], axis=1)
    return c, s1, s2


def _rope_tile(r, c, s1, s2):
    return r * c + pltpu.roll(r, 96, 1) * s1 + pltpu.roll(r, 32, 1) * s2


def _mla_mid(proj2, qa, kva):
    t = proj2.shape[0]
    tm = _row_tile(t)

    def body(p_ref, qa_ref, kva_ref, cq_ref, ckv_ref):
        cq = p_ref[:, :MLA_Q_RANK]
        ckv = p_ref[:, MLA_Q_RANK:MLA_Q_RANK + MLA_KV_RANK]
        rq = lax.rsqrt(jnp.mean(cq * cq, axis=-1, keepdims=True) + EPS)
        rkv = lax.rsqrt(jnp.mean(ckv * ckv, axis=-1, keepdims=True) + EPS)
        cq_ref[...] = (cq * rq * qa_ref[...]).astype(BF16)
        ckv_ref[...] = (ckv * rkv * kva_ref[...]).astype(BF16)

    return pl.pallas_call(
        body, name="mla_mid", grid=(t // tm,),
        in_specs=[pl.BlockSpec((tm, MLA_IN_PAD), lambda i: (i, 0)),
                  pl.BlockSpec((1, MLA_Q_RANK), lambda i: (0, 0)),
                  pl.BlockSpec((1, MLA_KV_RANK), lambda i: (0, 0))],
        out_specs=[pl.BlockSpec((tm, MLA_Q_RANK), lambda i: (i, 0)),
                   pl.BlockSpec((tm, MLA_KV_RANK), lambda i: (i, 0))],
        out_shape=[_sds((t, MLA_Q_RANK), BF16), _sds((t, MLA_KV_RANK), BF16)],
        compiler_params=_cparams(("parallel",)),
    )(proj2, qa, kva)


def _mla_mid_bwd(proj2, qa, kva, dcq, dckv, dkr):
    t = proj2.shape[0]
    tm = _row_tile(t)

    def body(p_ref, qa_ref, kva_ref, dcq_ref, dckv_ref, dkr_ref, dp_ref, dqa_ref, dkva_ref):
        @pl.when(pl.program_id(0) == 0)
        def _():
            dqa_ref[...] = jnp.zeros_like(dqa_ref)
            dkva_ref[...] = jnp.zeros_like(dkva_ref)

        dxq, dgq = _rms_bwd_rows(dcq_ref[...], p_ref[:, :MLA_Q_RANK], qa_ref[...], MLA_Q_RANK)
        dxk, dgk = _rms_bwd_rows(dckv_ref[...], p_ref[:, MLA_Q_RANK:MLA_Q_RANK + MLA_KV_RANK], kva_ref[...],
                                 MLA_KV_RANK)
        dp_ref[:, :MLA_Q_RANK] = dxq.astype(BF16)
        dp_ref[:, MLA_Q_RANK:MLA_Q_RANK + MLA_KV_RANK] = dxk.astype(BF16)
        dp_ref[:, MLA_Q_RANK + MLA_KV_RANK:] = dkr_ref[...].astype(BF16)
        dqa_ref[...] += jnp.sum(dgq, axis=0, keepdims=True)
        dkva_ref[...] += jnp.sum(dgk, axis=0, keepdims=True)

    return pl.pallas_call(
        body, name="mla_mid_bwd", grid=(t // tm,),
        in_specs=[pl.BlockSpec((tm, MLA_IN_PAD), lambda i: (i, 0)),
                  pl.BlockSpec((1, MLA_Q_RANK), lambda i: (0, 0)),
                  pl.BlockSpec((1, MLA_KV_RANK), lambda i: (0, 0)),
                  pl.BlockSpec((tm, MLA_Q_RANK), lambda i: (i, 0)),
                  pl.BlockSpec((tm, MLA_KV_RANK), lambda i: (i, 0)),
                  pl.BlockSpec((tm, 128), lambda i: (i, 0))],
        out_specs=[pl.BlockSpec((tm, MLA_IN_PAD), lambda i: (i, 0)),
                   pl.BlockSpec((1, MLA_Q_RANK), lambda i: (0, 0)),
                   pl.BlockSpec((1, MLA_KV_RANK), lambda i: (0, 0))],
        out_shape=[_sds((t, MLA_IN_PAD), BF16), _sds((1, MLA_Q_RANK), F32), _sds((1, MLA_KV_RANK), F32)],
        compiler_params=_cparams(("arbitrary",)),
    )(proj2, qa, kva, dcq, dckv, dkr)


def _mla_prep_specs(t, tm):
    head = lambda w: pl.BlockSpec((None, tm, w), lambda i, h: (h, i, 0))
    return dict(
        head256=head(MLA_HD_PAD), head128=head(MLA_VD),
        cols256=pl.BlockSpec((tm, MLA_HD_PAD), lambda i, h: (i, h)),
        cq=pl.BlockSpec((tm, MLA_Q_RANK), lambda i, h: (i, 0)),
        ckv=pl.BlockSpec((tm, MLA_KV_RANK), lambda i, h: (i, 0)),
        wuq=pl.BlockSpec((None, MLA_Q_RANK, MLA_HD_PAD), lambda i, h: (h, 0, 0)),
        wukv=pl.BlockSpec((None, MLA_KV_RANK, MLA_HD_PAD), lambda i, h: (h, 0, 0)),
        kr=pl.BlockSpec((tm, 128), lambda i, h: (i, (MLA_Q_RANK + MLA_KV_RANK) // 128)),
        gain=pl.BlockSpec((1, MLA_HD_PAD), lambda i, h: (0, 0)),
        tab=pl.BlockSpec((tm, 128), lambda i, h: (i, 0)),
    )


def _mla_prep(cq, ckv, wuq, wukv, proj2, gq, gk, tabs):
    t = cq.shape[0]
    tm = _row_tile(t)
    sp = _mla_prep_specs(t, tm)

    def body(cq_ref, ckv_ref, wuq_ref, wukv_ref, kr_ref, gq_ref, gk_ref, c_ref, s1_ref, s2_ref,
             qh_ref, kh_ref, vh_ref):
        c, s1, s2 = c_ref[...], s1_ref[...], s2_ref[...]

        def norm_rope(xv, gain):
            r = lax.rsqrt(jnp.sum(xv * xv, axis=-1, keepdims=True) / MLA_QKD + EPS)
            y = xv * r * gain
            return jnp.concatenate([y[:, :MLA_NOPE], _rope_tile(y[:, MLA_NOPE:], c, s1, s2)], axis=-1)

        kvv = _dot(ckv_ref[...], wukv_ref[...], 1, 0)
        qh_ref[...] = norm_rope(_dot(cq_ref[...], wuq_ref[...], 1, 0), gq_ref[...]).astype(BF16)
        kf = jnp.concatenate([kvv[:, :MLA_NOPE], kr_ref[...]], axis=-1)
        kh_ref[...] = norm_rope(kf, gk_ref[...]).astype(BF16)
        vh_ref[...] = jnp.concatenate([kvv[:, MLA_NOPE:], jnp.ones((tm, MLA_VD), F32)], axis=-1).astype(BF16)

    return pl.pallas_call(
        body, name="mla_prep", grid=(t // tm, MLA_HEADS),
        in_specs=[sp['cq'], sp['ckv'], sp['wuq'], sp['wukv'], sp['kr'], sp['gain'], sp['gain'],
                  sp['tab'], sp['tab'], sp['tab']],
        out_specs=[sp['head256'], sp['head256'], sp['head256']],
        out_shape=[_sds((MLA_HEADS, t, MLA_HD_PAD), BF16), _sds((MLA_HEADS, t, MLA_HD_PAD), BF16),
                   _sds((MLA_HEADS, t, 2 * MLA_VD), BF16)],
        compiler_params=_cparams(("parallel", "arbitrary")),
    )(cq, ckv, wuq, wukv, proj2, gq, gk, *tabs)


def _mla_prep_bwd(cq, ckv, wuq, wukv, proj2, gq, gk, tabs, dqt, dkh, dvh):
    t = cq.shape[0]
    tm = _row_tile(t)
    ab = dqt.shape[-1]
    sp = _mla_prep_specs(t, tm)

    def body(cq_ref, ckv_ref, wuq_ref, wukv_ref, kr_ref, gq_ref, gk_ref, c_ref, s1_ref, s2_ref,
             dqt_ref, dkh_ref, dvh_ref, dq_ref, dkv_ref, dkr_ref, dgq_ref, dgk_ref):
        dqh = jnp.concatenate([dqt_ref[b].T for b in range(tm // ab)], axis=0)
        i, h = pl.program_id(0), pl.program_id(1)

        @pl.when((i == 0) & (h == 0))
        def _():
            dgq_ref[...] = jnp.zeros_like(dgq_ref)
            dgk_ref[...] = jnp.zeros_like(dgk_ref)

        @pl.when(h == 0)
        def _():
            dkr_ref[...] = jnp.zeros_like(dkr_ref)

        c, s1, s2 = c_ref[...], s1_ref[...], s2_ref[...]

        def back(xv, gain, dout):
            dy = jnp.concatenate([dout[:, :MLA_NOPE], _rope_tile(dout[:, MLA_NOPE:], c, -s1, -s2)], axis=-1)
            return _rms_bwd_rows(dy, xv, gain, MLA_QKD)

        kvv = _dot(ckv_ref[...], wukv_ref[...], 1, 0)
        dxq, dgq = back(_dot(cq_ref[...], wuq_ref[...], 1, 0), gq_ref[...], dqh)
        kf = jnp.concatenate([kvv[:, :MLA_NOPE], kr_ref[...]], axis=-1)
        dxk, dgk = back(kf, gk_ref[...], dkh_ref[...])
        dq_ref[...] = dxq.astype(BF16)
        dkv_ref[...] = jnp.concatenate([dxk[:, :MLA_NOPE], dvh_ref[...]], axis=-1).astype(BF16)
        dkr_ref[...] += dxk[:, MLA_NOPE:]
        dgq_ref[...] += jnp.sum(dgq, axis=0, keepdims=True)
        dgk_ref[...] += jnp.sum(dgk, axis=0, keepdims=True)

    return pl.pallas_call(
        body, name="mla_prep_bwd", grid=(t // tm, MLA_HEADS),
        in_specs=[sp['cq'], sp['ckv'], sp['wuq'], sp['wukv'], sp['kr'], sp['gain'], sp['gain'],
                  sp['tab'], sp['tab'], sp['tab'],
                  pl.BlockSpec((None, tm // ab, MLA_HD_PAD, ab), lambda i, h: (h, i, 0, 0)),
                  sp['head256'], sp['head128']],
        out_specs=[sp['cols256'], sp['cols256'], sp['tab'], sp['gain'], sp['gain']],
        out_shape=[_sds((t, MLA_HEADS * MLA_HD_PAD), BF16), _sds((t, MLA_HEADS * MLA_HD_PAD), BF16),
                   _sds((t, 128), F32), _sds((1, MLA_HD_PAD), F32), _sds((1, MLA_HD_PAD), F32)],
        compiler_params=_cparams(("arbitrary", "arbitrary")),
    )(cq, ckv, wuq, wukv, proj2, gq, gk, *tabs, dqt, dkh, dvh)


def _chunk_visible(rows, cols, row_off, col_off):
    rq = lax.shift_right_logical(lax.broadcasted_iota(jnp.int32, (rows, cols), 0) + row_off, 6)
    ck = lax.shift_right_logical(lax.broadcasted_iota(jnp.int32, (rows, cols), 1) + col_off, 6)
    return ck <= rq


def _rows_to_lanes(col):
    return col.T[:8, :]


def _attn_fwd(qh, kh, vh):
    t = qh.shape[1]
    ab = min(ATT_BLOCK, t)
    tq = min(ATT_QROWS, t)
    r = tq // ab
    hg = ATT_HEADS

    def body(q_ref, k_ref, v_ref, o_ref, lse_ref):
        n_un = pl.program_id(1) * r

        def step(b, state, diag):
            rows = pl.ds(pl.multiple_of(b * ab, ab), ab)
            ms, accs = [], []
            for hh in range(hg):
                m, acc = state[0][hh], state[1][hh]
                s = _dot(q_ref[hh], k_ref[hh, rows, :], 1, 1)
                if diag is not None:
                    s = jnp.where(_chunk_visible(tq, ab, 0, diag * ab), s, -1e30)
                m_new = jnp.maximum(m, jnp.max(s, axis=-1, keepdims=True))
                p = jnp.exp2((s - m_new) * ATT_EXP2).astype(BF16)
                accs.append(jnp.exp2((m - m_new) * ATT_EXP2) * acc + _dot(p, v_ref[hh, rows, :], 1, 0))
                ms.append(m_new)
            return tuple(ms), tuple(accs)

        heads = lambda v: tuple(v for _ in range(hg))
        state = (heads(jnp.full((tq, 1), -1e30, F32)), heads(jnp.zeros((tq, 2 * MLA_VD), F32)))
        state = lax.fori_loop(0, n_un, lambda b, st: step(b, st, None), state)
        for d in range(r):
            state = step(n_un + d, state, d)
        ms, accs = state
        for hh in range(hg):
            l = accs[hh][:, MLA_VD:]
            o_ref[:, hh * MLA_VD:(hh + 1) * MLA_VD] = accs[hh][:, :MLA_VD] / l
            lse_t = _rows_to_lanes(ms[hh] * ATT_EXP2 + jnp.log(l) * LOG2E)
            for d in range(r):
                lse_ref[hh, d] = lse_t[:, d * ab:(d + 1) * ab]

    return pl.pallas_call(
        body, name="mla_attn", grid=(MLA_HEADS // hg, t // tq),
        in_specs=[pl.BlockSpec((hg, tq, MLA_HD_PAD), lambda g, i: (g, i, 0)),
                  pl.BlockSpec((hg, t, MLA_HD_PAD), lambda g, i: (g, 0, 0)),
                  pl.BlockSpec((hg, t, 2 * MLA_VD), lambda g, i: (g, 0, 0))],
        out_specs=[pl.BlockSpec((tq, hg * MLA_VD), lambda g, i: (i, g)),
                   pl.BlockSpec((hg, r, 8, ab), lambda g, i: (g, i, 0, 0))],
        out_shape=[_sds((t, MLA_HEADS * MLA_VD), F32), _sds((MLA_HEADS, t // ab, 8, ab), F32)],
        compiler_params=_cparams(("parallel", "arbitrary")),
    )(qh, kh, vh)


def _attn_delta(do, o, ab):
    t = do.shape[0]
    tm = _row_tile(t)

    def body(do_ref, o_ref, d_ref):
        d = jnp.sum(do_ref[...] * o_ref[...], axis=-1, keepdims=True)
        d_t = _rows_to_lanes(jnp.broadcast_to(d, (tm, 128)))
        for b in range(tm // ab):
            d_ref[b] = d_t[:, b * ab:(b + 1) * ab]

    col = pl.BlockSpec((tm, MLA_VD), lambda i, h: (i, h))
    return pl.pallas_call(
        body, name="mla_delta", grid=(t // tm, MLA_HEADS), in_specs=[col, col],
        out_specs=pl.BlockSpec((None, tm // ab, 8, ab), lambda i, h: (h, i, 0, 0)),
        out_shape=_sds((MLA_HEADS, t // ab, 8, ab), F32),
        compiler_params=_cparams(("parallel", "parallel")),
    )(do, o)


def _attn_bwd(qh, kh, vh, dob, lse_t, dl_t):
    t = qh.shape[1]
    ab = min(ATT_BLOCK, t)
    kb = min(ATT_KROWS, t)
    r = kb // ab
    nq = t // ab
    hg = ATT_HEADS

    def body(q_ref, k_ref, v_ref, do_ref, lse_ref, dl_ref, dqt_ref, dk_ref, dv_ref):
        j = pl.program_id(1)

        @pl.when(j == 0)
        def _():
            dqt_ref[...] = jnp.zeros_like(dqt_ref)

        ks = [k_ref[hh] for hh in range(hg)]
        vs = [v_ref[hh, :, :MLA_VD] for hh in range(hg)]
        kts = [k.T for k in ks]

        def step(b, grads, diag):
            rows = pl.ds(pl.multiple_of(b * ab, ab), ab)
            out = []
            for hh in range(hg):
                dk, dv = grads[hh]
                q = q_ref[hh, rows, :]
                do = do_ref[rows, hh * MLA_VD:(hh + 1) * MLA_VD]
                s_t = _dot(ks[hh], q, 1, 1)
                if diag is not None:
                    key_chunk = lax.shift_right_logical(lax.broadcasted_iota(jnp.int32, (kb, ab), 0), 6)
                    query_chunk = lax.shift_right_logical(
                        lax.broadcasted_iota(jnp.int32, (kb, ab), 1) + diag * ab, 6)
                    s_t = jnp.where(key_chunk <= query_chunk, s_t, -1e30)
                p_t = jnp.exp2(s_t * ATT_EXP2 - lse_ref[hh, b][0:1, :])
                dp_t = _dot(vs[hh], do, 1, 1)
                ds_t = (p_t * (dp_t - dl_ref[hh, b][0:1, :]) * ATT_SCALE).astype(BF16)
                dqt_ref[hh, b] += _dot(kts[hh], ds_t, 1, 0)
                out.append((dk + _dot(ds_t, q, 1, 0), dv + _dot(p_t.astype(BF16), do, 1, 0)))
            return tuple(out)

        grads = tuple((jnp.zeros((kb, MLA_HD_PAD), F32), jnp.zeros((kb, MLA_VD), F32)) for _ in range(hg))
        for d in range(r):
            grads = step(j * r + d, grads, d)
        grads = lax.fori_loop((j + 1) * r, nq, lambda b, g: step(b, g, None), grads)
        for hh in range(hg):
            dk_ref[hh] = grads[hh][0]
            dv_ref[hh] = grads[hh][1]

    whole = lambda w: pl.BlockSpec((hg, t, w), lambda g, j: (g, 0, 0))
    blk = lambda w: pl.BlockSpec((hg, kb, w), lambda g, j: (g, j, 0))
    stat = pl.BlockSpec((hg, nq, 8, ab), lambda g, j: (g, 0, 0, 0))
    return pl.pallas_call(
        body, name="mla_attn_bwd", grid=(MLA_HEADS // hg, t // kb),
        in_specs=[whole(MLA_HD_PAD), blk(MLA_HD_PAD), blk(2 * MLA_VD),
                  pl.BlockSpec((t, hg * MLA_VD), lambda g, j: (0, g)), stat, stat],
        out_specs=[pl.BlockSpec((hg, nq, MLA_HD_PAD, ab), lambda g, j: (g, 0, 0, 0)), blk(MLA_HD_PAD), blk(MLA_VD)],
        out_shape=[_sds((MLA_HEADS, nq, MLA_HD_PAD, ab), F32), _sds((MLA_HEADS, t, MLA_HD_PAD), F32),
                   _sds((MLA_HEADS, t, MLA_VD), F32)],
        compiler_params=_cparams(("parallel", "arbitrary")),
    )(qh, kh, vh, dob, lse_t, dl_t)


VEC = pl.BlockSpec((1, D_MODEL), lambda i, j, k: (0, 0))


def _residual_epi(next_gain):
    if next_gain is None:
        return [], lambda acc, hv: (acc + hv,)

    def epi(acc, hv, g):
        h_new = acc + hv
        r = lax.rsqrt(jnp.mean(h_new * h_new, axis=-1, keepdims=True) + EPS)
        return h_new, h_new * r * g

    return [(next_gain, VEC)], epi


def _residual_outs(t, row, next_gain):
    outs = [(_sds((t, D_MODEL), F32), row)]
    return outs + ([(_sds((t, D_MODEL), BF16), row)] if next_gain is not None else [])


def _mlp_fwd(l, h, hn, w1g, fetch_w2, next_gain):
    t = h.shape[0]
    tm = _row_tile(t)
    nsh, _, wsh = w1g.shape

    def relu2(acc):
        r = jnp.maximum(acc, 0.0)
        return (r * r,)

    tu = _row_tile(t, 2 * ROW_TILE)
    tile = pl.BlockSpec((tu, wsh), lambda i, j, k: (i, j))
    (u,) = _mm(f"mlp_up{l}", (t // tu, nsh, 1),
               hn, pl.BlockSpec((tu, D_MODEL), lambda i, j, k: (i, 0)),
               w1g, pl.BlockSpec((None, D_MODEL, wsh), lambda i, j, k: (j, 0, 0)), (1, 0),
               [(_sds((t, D_FF), BF16), tile)], epi=relu2)
    w2g = fetch_w2((u,))
    row = pl.BlockSpec((tm, D_MODEL), lambda i, j, k: (i, 0))
    more, epi = _residual_epi(next_gain)
    h2, hn_next = _mm(f"mlp_down{l}", (t // tm, 1, nsh),
                      u, pl.BlockSpec((tm, wsh), lambda i, j, k: (i, k)),
                      w2g, pl.BlockSpec((None, wsh, D_MODEL), lambda i, j, k: (k, 0, 0)), (1, 0),
                      _residual_outs(t, row, next_gain), extras=[(h, row)] + more, epi=epi)
    return h2, hn_next, (h, hn, u, w1g, w2g)


def _norm_bwd_outs(t, tm):
    return [(_sds((t, D_MODEL), F32), pl.BlockSpec((tm, D_MODEL), lambda i, j, k: (i, 0))),
            (_sds((t // tm, 1, D_MODEL), F32), pl.BlockSpec((None, 1, D_MODEL), lambda i, j, k: (i, 0, 0)))]


def _norm_bwd_epi(acc, xv, res, g):
    dx, dgr = _rms_bwd_rows(acc, xv, g, D_MODEL)
    return res + dx, jnp.sum(dgr, axis=0, keepdims=True)


def _mlp_bwd(l, dh, saved, norm_g):
    h, hn, u, w1g, w2g = saved
    t = h.shape[0]
    tm = _row_tile(t)
    nsh, _, wsh = w1g.shape
    tu = _row_tile(t, 2 * ROW_TILE)
    tile = pl.BlockSpec((tu, wsh), lambda i, j, k: (i, j))
    (da,) = _mm(f"mlp_du{l}", (t // tu, nsh, 1),
                dh, pl.BlockSpec((tu, D_MODEL), lambda i, j, k: (i, 0)),
                w2g, pl.BlockSpec((None, wsh, D_MODEL), lambda i, j, k: (j, 0, 0)), (1, 1),
                [(_sds((t, D_FF), BF16), tile)], extras=[(u, tile)],
                epi=lambda acc, uv: (2.0 * jnp.sqrt(uv.astype(F32)) * acc,))
    tw = _row_tile(t, 512)
    (dw2,) = _mm(f"mlp_dw2{l}", (1, 1, t // tw),
                 u, pl.BlockSpec((tw, D_FF), lambda i, j, k: (k, 0)),
                 dh, pl.BlockSpec((tw, D_MODEL), lambda i, j, k: (k, 0)), (0, 0),
                 [(_sds((D_FF, D_MODEL), BF16), pl.BlockSpec((D_FF, D_MODEL), lambda i, j, k: (0, 0)))])
    dw2 = dw2.reshape(nsh, wsh, D_MODEL)
    (dw1,) = _mm(f"mlp_dw1{l}", (1, 1, t // tw),
                 hn, pl.BlockSpec((tw, D_MODEL), lambda i, j, k: (k, 0)),
                 da, pl.BlockSpec((tw, D_FF), lambda i, j, k: (k, 0)), (0, 0),
                 [(_sds((nsh, D_MODEL, wsh), BF16), pl.BlockSpec((nsh, D_MODEL, wsh), lambda i, j, k: (0, 0, 0)))],
                 split=wsh)
    row = pl.BlockSpec((tm, D_MODEL), lambda i, j, k: (i, 0))
    dh_in, dg = _mm(f"mlp_dhn{l}", (t // tm, 1, nsh),
                    da, pl.BlockSpec((tm, wsh), lambda i, j, k: (i, k)),
                    w1g, pl.BlockSpec((None, D_MODEL, wsh), lambda i, j, k: (k, 0, 0)), (1, 1),
                    _norm_bwd_outs(t, tm), extras=[(h, row), (dh, row), (norm_g, VEC)], epi=_norm_bwd_epi)
    return dh_in, jnp.sum(dg, axis=0), dw1, dw2


def _ple_fwd(l, h, hn, p, wg, wp, next_gain, target=None):
    t = h.shape[0]
    tm = _row_tile(t, 512)
    row = pl.BlockSpec((tm, D_MODEL), lambda i, j, k: (i, 0))
    full = lambda r: pl.BlockSpec((r, D_MODEL), lambda i, j, k: (0, 0))
    (e,) = _mm(f"ple_proj{l}", (t // tm, 1, 1),
               p, pl.BlockSpec((None, None, tm, PLE_DIM), lambda i, j, k: (l, 0, i, 0)),
               wp, full(PLE_DIM), (1, 0), [(_sds((t, D_MODEL), F32), row)])

    f32_row, bf_row = (_sds((t, D_MODEL), F32), row), (_sds((t, D_MODEL), BF16), row)
    if target is not None:
        def loss_epi(acc, hv, ev, tv):
            gt = _sigmoid(acc)
            err = hv + gt * ev - tv
            sq = jnp.sum(jnp.sum(err * err, axis=-1, keepdims=True), axis=0, keepdims=True)
            return err / D_MODEL, gt, jnp.broadcast_to(sq, (8, 128))

        dy, gate, sq = _mm(f"ple_gate{l}", (t // tm, 1, 1), hn, row, wg, full(D_MODEL), (1, 0),
                           [f32_row, f32_row, (_sds((t // tm, 8, 128), F32),
                                               pl.BlockSpec((None, 8, 128), lambda i, j, k: (i, 0, 0)))],
                           extras=[(h, row), (e, row), (target, row)], epi=loss_epi)
        return dy, jnp.sum(sq, axis=0), (h, hn, gate, e)

    def gate_epi(acc, hv, ev, *gain):
        gt = _sigmoid(acc)
        h_new = hv + gt * ev
        if not gain:
            return h_new, gt
        r = lax.rsqrt(jnp.mean(h_new * h_new, axis=-1, keepdims=True) + EPS)
        return h_new, gt, h_new * r * gain[0]

    res = _mm(f"ple_gate{l}", (t // tm, 1, 1), hn, row, wg, full(D_MODEL), (1, 0),
              [f32_row, f32_row] + ([bf_row] if next_gain is not None else []),
              extras=[(h, row), (e, row)] + ([(next_gain, VEC)] if next_gain is not None else []), epi=gate_epi)
    h_out, gate = res[0], res[1]
    return h_out, (res[2] if next_gain is not None else None), (h, hn, gate, e)


def _ple_bwd(l, dh, saved, p, norm_g, wg, deps=()):
    h, hn, gate, e = saved
    t = h.shape[0]
    tm = _row_tile(t)
    tk = _row_tile(t, 512)
    de, dz = _ple_gate_bwd(f"ple_gate_bwd{l}", dh, gate, e)
    full = lambda r: pl.BlockSpec((r, D_MODEL), lambda i, j, k: (0, 0))
    rowk = pl.BlockSpec((tk, D_MODEL), lambda i, j, k: (k, 0))
    (dwp,) = _mm(f"ple_dwp{l}", (1, 1, t // tk),
                 p, pl.BlockSpec((None, None, tk, PLE_DIM), lambda i, j, k: (l, 0, k, 0)),
                 de, rowk, (0, 0), [(_sds((PLE_DIM, D_MODEL), BF16), full(PLE_DIM))], deps=deps)
    (dwg,) = _mm(f"ple_dwg{l}", (1, 1, t // tk), hn, rowk, dz, rowk, (0, 0),
                 [(_sds((D_MODEL, D_MODEL), BF16), full(D_MODEL))])
    row = pl.BlockSpec((tm, D_MODEL), lambda i, j, k: (i, 0))
    dh_in, dg = _mm(f"ple_dhn{l}", (t // tm, 1, 1), dz, row, wg, full(D_MODEL), (1, 1),
                    _norm_bwd_outs(t, tm), extras=[(h, row), (dh, row), (norm_g, VEC)], epi=_norm_bwd_epi)
    return dh_in, jnp.sum(dg, axis=0), dwg, dwp


def _ret_layer_fwd(x, norm_g, wri, fetch_wro, gn, cos, sin, next_gain, deps=()):
    t = x.shape[0]
    tm = _row_tile(t)
    nsh, _, wsh = wri.shape
    hn = _rms_fwd("mix_norm0", x, norm_g)
    (proj,) = _mm("ret_in", (t // tm, nsh, 1),
                  hn, pl.BlockSpec((tm, D_MODEL), lambda i, j, k: (i, 0)),
                  wri, pl.BlockSpec((None, D_MODEL, wsh), lambda i, j, k: (j, 0, 0)), (1, 0),
                  [(_sds((t, RET_IN), F32), pl.BlockSpec((tm, wsh), lambda i, j, k: (i, j)))], deps=deps)
    gated, outp, states = _ret_fwd(proj, cos, sin, gn)
    wro = fetch_wro((gated,))
    row = pl.BlockSpec((tm, D_MODEL), lambda i, j, k: (i, 0))
    kt = 512
    more, epi = _residual_epi(next_gain)
    h1, hn_next = _mm("ret_out", (t // tm, 1, RET_V_W // kt),
                      gated, pl.BlockSpec((tm, kt), lambda i, j, k: (i, k)),
                      wro, pl.BlockSpec((kt, D_MODEL), lambda i, j, k: (k, 0)), (1, 0),
                      _residual_outs(t, row, next_gain), extras=[(x, row)] + more, epi=epi)
    return h1, hn_next, (x, hn, proj, gated, outp, states, wro)


def _ret_layer_bwd(dh, saved, norm_g, wri, gn, cos, sin, emit, deps=()):
    x, hn, proj, gated, outp, states, wro = saved
    t = x.shape[0]
    tm = _row_tile(t)
    tk = _row_tile(t, 512)
    nsh, _, wsh = wri.shape
    (dgated,) = _mm("ret_dgated", (t // tm, RET_V_W // D_MODEL, 1),
                    dh, pl.BlockSpec((tm, D_MODEL), lambda i, j, k: (i, 0)),
                    wro, pl.BlockSpec((D_MODEL, D_MODEL), lambda i, j, k: (j, 0)), (1, 1),
                    [(_sds((t, RET_V_W), F32), pl.BlockSpec((tm, D_MODEL), lambda i, j, k: (i, j)))], deps=deps)
    (dwro,) = _mm("ret_dwro", (1, 1, t // tk),
                  gated, pl.BlockSpec((tk, RET_V_W), lambda i, j, k: (k, 0)),
                  dh, pl.BlockSpec((tk, D_MODEL), lambda i, j, k: (k, 0)), (0, 0),
                  [(_sds((RET_V_W, D_MODEL), BF16), pl.BlockSpec((RET_V_W, D_MODEL), lambda i, j, k: (0, 0)))])
    dproj, dgn = _ret_bwd(proj, cos, sin, gn, outp, states, dgated)
    half = nsh // 2
    (dwri,) = _mm("ret_dwri", (2, 1, t // tk),
                  hn, pl.BlockSpec((tk, D_MODEL), lambda i, j, k: (k, 0)),
                  dproj, pl.BlockSpec((tk, half * wsh), lambda i, j, k: (k, i)), (0, 0),
                  [(_sds((nsh, D_MODEL, wsh), BF16), pl.BlockSpec((half, D_MODEL, wsh), lambda i, j, k: (i, 0, 0)))],
                  split=wsh)
    deps = emit(dwro, dwri)
    row = pl.BlockSpec((tm, D_MODEL), lambda i, j, k: (i, 0))
    dx, dg = _mm("ret_dhn", (t // tm, 1, nsh),
                 dproj, pl.BlockSpec((tm, wsh), lambda i, j, k: (i, k)),
                 wri, pl.BlockSpec((None, D_MODEL, wsh), lambda i, j, k: (k, 0, 0)), (1, 1),
                 _norm_bwd_outs(t, tm), extras=[(x, row), (dh, row), (norm_g, VEC)], epi=_norm_bwd_epi, deps=deps)
    return dx, jnp.sum(dg, axis=0), dgn.reshape(RET_HEADS, RET_DV)


def _mla_layer_fwd(h, hn, wmi, qa, kva, wuq, wukv, gq, gk, wmo, tabs, next_gain):
    t = h.shape[0]
    tm = _row_tile(t)
    row = pl.BlockSpec((tm, D_MODEL), lambda i, j, k: (i, 0))
    (proj2,) = _mm("mla_in", (t // tm, 1, 1), hn, row,
                   wmi, pl.BlockSpec((D_MODEL, MLA_IN_PAD), lambda i, j, k: (0, 0)), (1, 0),
                   [(_sds((t, MLA_IN_PAD), F32), pl.BlockSpec((tm, MLA_IN_PAD), lambda i, j, k: (i, 0)))])
    cq, ckv = _mla_mid(proj2, qa, kva)
    qh, kh, vh = _mla_prep(cq, ckv, wuq, wukv, proj2, gq, gk, tabs)
    o, lse = _attn_fwd(qh, kh, vh)
    more, epi = _residual_epi(next_gain)
    h_out, hn_next = _mm("mla_out", (t // tm, 1, 1), o, row,
                         wmo, pl.BlockSpec((D_MODEL, D_MODEL), lambda i, j, k: (0, 0)), (1, 0),
                         _residual_outs(t, row, next_gain), extras=[(h, row)] + more, epi=epi)
    return h_out, hn_next, (h, hn, proj2, cq, ckv, qh, kh, vh, o, lse)


def _mla_layer_bwd(dh, saved, norm_g, wmi, qa, kva, wuq, wukv, gq, gk, wmo, tabs, deps=()):
    h, hn, proj2, cq, ckv, qh, kh, vh, o, lse = saved
    t = h.shape[0]
    tm = _row_tile(t)
    tk = _row_tile(t, 512)
    row = pl.BlockSpec((tm, D_MODEL), lambda i, j, k: (i, 0))
    rowk = pl.BlockSpec((tk, D_MODEL), lambda i, j, k: (k, 0))
    sq = pl.BlockSpec((D_MODEL, D_MODEL), lambda i, j, k: (0, 0))
    do, dob = _mm("mla_do", (t // tm, 1, 1), dh, row, wmo, sq, (1, 1),
                  [(_sds((t, D_MODEL), F32), row), (_sds((t, D_MODEL), BF16), row)], epi=lambda acc: (acc, acc),
                  deps=deps)
    (dwmo,) = _mm("mla_dwo", (1, 1, t // tk), o, rowk, dh, rowk, (0, 0), [(_sds((D_MODEL, D_MODEL), BF16), sq)])
    delta = _attn_delta(do, o, lse.shape[-1])
    dqt, dkh, dvh = _attn_bwd(qh, kh, vh, dob, lse, delta)
    dq, dkv, dkr, dgq, dgk = _mla_prep_bwd(cq, ckv, wuq, wukv, proj2, gq, gk, tabs, dqt, dkh, dvh)

    wide = MLA_HEADS * MLA_HD_PAD
    widek = pl.BlockSpec((tk, wide), lambda i, j, k: (k, 0))
    (dwuq,) = _mm("mla_dwuq", (1, 1, t // tk),
                  cq, pl.BlockSpec((tk, MLA_Q_RANK), lambda i, j, k: (k, 0)), dq, widek, (0, 0),
                  [(_sds((MLA_HEADS, MLA_Q_RANK, MLA_HD_PAD), BF16),
                    pl.BlockSpec((MLA_HEADS, MLA_Q_RANK, MLA_HD_PAD), lambda i, j, k: (0, 0, 0)))], split=MLA_HD_PAD)
    (dwukv,) = _mm("mla_dwukv", (1, 1, t // tk),
                   ckv, pl.BlockSpec((tk, MLA_KV_RANK), lambda i, j, k: (k, 0)), dkv, widek, (0, 0),
                   [(_sds((MLA_HEADS, MLA_KV_RANK, MLA_HD_PAD), BF16),
                     pl.BlockSpec((MLA_HEADS, MLA_KV_RANK, MLA_HD_PAD), lambda i, j, k: (0, 0, 0)))],
                   split=MLA_HD_PAD)
    side_by_side = lambda wg: wg.transpose(1, 0, 2).reshape(wg.shape[1], wide)
    widei = pl.BlockSpec((tm, wide), lambda i, j, k: (i, 0))
    (dcq,) = _mm("mla_dcq", (t // tm, 1, 1), dq, widei,
                 side_by_side(wuq), pl.BlockSpec((MLA_Q_RANK, wide), lambda i, j, k: (0, 0)), (1, 1),
                 [(_sds((t, MLA_Q_RANK), F32), pl.BlockSpec((tm, MLA_Q_RANK), lambda i, j, k: (i, 0)))])
    (dckv,) = _mm("mla_dckv", (t // tm, 1, 1), dkv, widei,
                  side_by_side(wukv), pl.BlockSpec((MLA_KV_RANK, wide), lambda i, j, k: (0, 0)), (1, 1),
                  [(_sds((t, MLA_KV_RANK), F32), pl.BlockSpec((tm, MLA_KV_RANK), lambda i, j, k: (i, 0)))])
    dproj2, dqa, dkva = _mla_mid_bwd(proj2, qa, kva, dcq, dckv, dkr)
    win = pl.BlockSpec((D_MODEL, MLA_IN_PAD), lambda i, j, k: (0, 0))
    (dwmi,) = _mm("mla_dwin", (1, 1, t // tk), hn, rowk,
                  dproj2, pl.BlockSpec((tk, MLA_IN_PAD), lambda i, j, k: (k, 0)), (0, 0),
                  [(_sds((D_MODEL, MLA_IN_PAD), BF16), win)])
    dh_in, dg = _mm("mla_dhn", (t // tm, 1, 1),
                    dproj2, pl.BlockSpec((tm, MLA_IN_PAD), lambda i, j, k: (i, 0)), wmi, win, (1, 1),
                    _norm_bwd_outs(t, tm), extras=[(h, row), (dh, row), (norm_g, VEC)], epi=_norm_bwd_epi)
    return dh_in, dict(mix=jnp.sum(dg, axis=0), wmi=dwmi, qa=dqa, kva=dkva, wuq=dwuq, wukv=dwukv, gq=dgq, gk=dgk,
                       wmo=dwmo)


def _local_step(x, p, target, w, fetch, emit=lambda group: ()):
    t = x.shape[0]
    inv = 1.0 / (ROPE_THETA ** (jnp.arange(0, RET_DK, 2, dtype=F32) / RET_DK))
    ang = jnp.arange(t, dtype=F32)[:, None] * inv[None, :]
    cos_r, sin_r = jnp.cos(ang), jnp.sin(ang)
    tabs = _mla_tables(t)
    row = lambda a, i: a[i:i + 1]

    h1, hn1, s_ret = _ret_layer_fwd(x, row(w['mix_norm'], 0), w['ret_w_in'],
                                    lambda after: fetch('ret_out', after)['ret_w_out'], w['ret_gn'], cos_r, sin_r,
                                    row(w['mlp_norm'], 0), deps=w['deps'])
    h2, hn2, s_mlp0 = _mlp_fwd(0, h1, hn1, fetch('mlp_w1_0', (h1,))['mlp_w1'],
                               lambda after: fetch('mlp_w2_0', after)['mlp_w2'], row(w['ple_norm'], 0))
    w0 = fetch('ple_0', (h2,))
    h3, hn3, s_ple0 = _ple_fwd(0, h2, hn2, p, w0['ple_gate_w'], w0['ple_proj_w'], row(w['mix_norm'], 1))
    wm = fetch('mla', (h3,))
    mla_w = (wm['mla_w_in'], w['mla_q_a_norm'], w['mla_kv_a_norm'], wm['mla_w_uq'], wm['mla_w_ukv'],
             w['mla_q_norm'], w['mla_k_norm'], wm['mla_w_out'], tabs)
    h4, hn4, s_mla = _mla_layer_fwd(h3, hn3, *mla_w, row(w['mlp_norm'], 1))
    w1 = fetch('layer_1', (h4,))
    h5, hn5, s_mlp1 = _mlp_fwd(1, h4, hn4, w1['mlp_w1'], lambda after: w1['mlp_w2'], row(w['ple_norm'], 1))
    dy, sq_err, s_ple1 = _ple_fwd(1, h5, hn5, p, w1['ple_gate_w'], w1['ple_proj_w'], None, target)

    n = N_DEV
    colsh = lambda a: a.reshape(a.shape[0], n, a.shape[1] // n).transpose(1, 0, 2)
    rowsh = lambda a: a.reshape(n, a.shape[0] // n, a.shape[1])
    big = {}

    def emit_group(group):
        big.update(group)
        return emit(group)

    dh5, dg_ple1, dwg1, dwp1 = _ple_bwd(1, dy, s_ple1, p, row(w['ple_norm'], 1), w1['ple_gate_w'])
    dh4, dg_mlp1, dw1_1, dw2_1 = _mlp_bwd(1, dh5, s_mlp1, row(w['mlp_norm'], 1))
    deps = emit_group({('ple_gate_w', 1): rowsh(dwg1), ('ple_proj_w', 1): colsh(dwp1),
                       ('mlp_w2', 1): dw2_1, ('mlp_w1', 1): dw1_1})
    dh3, gm = _mla_layer_bwd(dh4, s_mla, row(w['mix_norm'], 1), *mla_w, deps=deps)
    deps = emit_group({('mla_w_out', 0): rowsh(gm['wmo']), ('mla_w_uq', 0): gm['wuq'][:, :, :MLA_QKD],
                       ('mla_w_ukv', 0): gm['wukv'], ('mla_w_in', 0): rowsh(gm['wmi'][:, :MLA_IN])})
    dh2, dg_ple0, dwg0, dwp0 = _ple_bwd(0, dh3, s_ple0, p, row(w['ple_norm'], 0), w0['ple_gate_w'], deps=deps)
    dh1, dg_mlp0, dw1_0, dw2_0 = _mlp_bwd(0, dh2, s_mlp0, row(w['mlp_norm'], 0))
    deps = emit_group({('ple_gate_w', 0): rowsh(dwg0), ('ple_proj_w', 0): colsh(dwp0),
                       ('mlp_w2', 0): dw2_0, ('mlp_w1', 0): dw1_0})
    dx, dg_mix0, dgn = _ret_layer_bwd(
        dh1, s_ret, row(w['mix_norm'], 0), w['ret_w_in'], w['ret_gn'], cos_r, sin_r,
        lambda dwro, dwri: emit_group({('ret_w_out', 0): rowsh(dwro), ('ret_w_in', 0): dwri}), deps=deps)

    small = dict(
        mix_norm=[dg_mix0, gm['mix']], mlp_norm=[dg_mlp0, dg_mlp1], ple_norm=[dg_ple0, dg_ple1],
        ret_gn=dgn, mla_q_a_norm=gm['qa'], mla_kv_a_norm=gm['kva'], mla_q_norm=gm['gq'], mla_k_norm=gm['gk'],
    )
    return sq_err, dx, big, small


def _my_place():
    x, y, c = lax.axis_index("x"), lax.axis_index("y"), lax.axis_index("c")
    return x, y, c


def _flat(px, py, pc):
    return 4 * px + 2 * py + pc


def _peer(x, y, c, r):
    return (1 - x if r & 4 else x, 1 - y if r & 2 else y, 1 - c if r & 1 else c)


def _all_gather(arrays):
    n = len(arrays)

    def body(*refs):
        ins, outs = refs[:n], refs[n:2 * n]
        send_sems, recv_sems, local_sems = refs[2 * n:]
        x, y, c = _my_place()
        me, sibling = (x, y, c), (x, y, 1 - c)
        chips = [(1 - x, y), (x, 1 - y), (1 - x, 1 - y)]

        def copy(a, k, block, to, src=None):
            slot = outs[a].at[_flat(*block)]
            return pltpu.make_async_remote_copy(
                src_ref=slot if src is None else src, dst_ref=slot,
                send_sem=send_sems.at[a, k], recv_sem=recv_sems.at[a, k], device_id=to, device_id_type=MESH)

        mine = [pltpu.make_async_copy(ins[a], outs[a].at[_flat(*me)], local_sems.at[a]) for a in range(n)]
        for cp in mine:
            cp.start()
        first = []
        for a in range(n):
            first.append(copy(a, 0, me, sibling, src=ins[a]))
            first += [copy(a, 1 + j, me, (*chip, c), src=ins[a]) for j, chip in enumerate(chips)]
        for cp in first:
            cp.start()
        passed = []
        for a in range(n):
            for j, chip in enumerate(chips):
                copy(a, 1 + j, (*chip, c), me).wait_recv()
                passed.append(copy(a, 4 + j, (*chip, c), sibling))
                passed[-1].start()
        for a in range(n):
            copy(a, 0, sibling, me).wait_recv()
            for j, chip in enumerate(chips):
                copy(a, 4 + j, (*chip, 1 - c), me).wait_recv()
        for cp in first + passed:
            cp.wait_send()
        for cp in mine:
            cp.wait()

    return pl.pallas_call(
        body, name="all_gather_weights",
        in_specs=[ANY] * n, out_specs=[ANY] * n,
        out_shape=[_sds((N_DEV,) + a.shape, a.dtype) for a in arrays],
        scratch_shapes=[pltpu.SemaphoreType.DMA((n, 7)), pltpu.SemaphoreType.DMA((n, 7)),
                        pltpu.SemaphoreType.DMA((n,))],
    )(*arrays)


HBM = pl.BlockSpec(memory_space=pltpu.HBM)
SEMS = pl.BlockSpec(memory_space=pltpu.SEMAPHORE)
SIDE_EFFECT = pltpu.SideEffectType.DATAFLOW_SIDE_EFFECTING


def _rs_copies(x, y, c, srcs, lands, send_sems, recv_sems):
    copies = []
    for a in range(len(srcs)):
        for r in range(1, N_DEV):
            peer = _peer(x, y, c, r)
            k = a * (N_DEV - 1) + r - 1
            copies.append(pltpu.make_async_remote_copy(
                src_ref=srcs[a].at[_flat(*peer)], dst_ref=lands[a].at[r - 1],
                send_sem=send_sems.at[k], recv_sem=recv_sems.at[k], device_id=peer, device_id_type=MESH))
    return copies


def _rs_start(name, arrays):
    n = len(arrays)
    hbm = lambda a: pltpu.with_memory_space_constraint(a, pltpu.HBM)
    lands = [hbm(lax.empty((N_DEV - 1,) + a.shape[1:], a.dtype)) for a in arrays]

    def body(*refs):
        srcs, lnd = refs[:n], refs[n:2 * n]
        send_sems, recv_sems = refs[2 * n], refs[2 * n + 1]
        token = refs[-1]
        for cp in _rs_copies(*_my_place(), srcs, lnd, send_sems, recv_sems):
            cp.start()
        token[...] = jnp.zeros_like(token)

    outs = pl.pallas_call(
        body, name=name,
        in_specs=[HBM] * (2 * n),
        out_specs=[SEMS, SEMS] + [HBM] * (2 * n) + [pl.BlockSpec(memory_space=pltpu.VMEM)],
        out_shape=[pltpu.SemaphoreType.DMA((n * (N_DEV - 1),)), pltpu.SemaphoreType.DMA((n * (N_DEV - 1),))]
        + [pltpu.HBM(a.shape, a.dtype) for a in arrays] + [pltpu.HBM(l.shape, l.dtype) for l in lands]
        + [_sds((8, 128), F32)],
        input_output_aliases={i: 2 + i for i in range(2 * n)},
        compiler_params=pltpu.CompilerParams(has_side_effects=SIDE_EFFECT),
    )(*[hbm(a) for a in arrays], *lands)
    return outs[0], outs[1], outs[2:2 + n], outs[2 + n:2 + 2 * n], outs[-1]


def _rs_wait(name, send_sems, recv_sems, srcs, lands, after):
    n = len(srcs)

    def body(*refs):
        src_refs, lnd = refs[:n], refs[n:2 * n]
        send, recv = refs[2 * n], refs[2 * n + 1]
        for cp in _rs_copies(*_my_place(), src_refs, lnd, send, recv):
            cp.wait_send()
            cp.wait_recv()

    outs = pl.pallas_call(
        body, name=name,
        in_specs=[HBM] * (2 * n) + [SEMS, SEMS] + [ANY] * len(after),
        out_specs=[HBM] * (2 * n),
        out_shape=[pltpu.HBM(a.shape, a.dtype) for a in list(srcs) + list(lands)],
        input_output_aliases={i: i for i in range(2 * n)},
        compiler_params=pltpu.CompilerParams(has_side_effects=SIDE_EFFECT),
    )(*srcs, *lands, send_sems, recv_sems, *after)
    return outs[:n], outs[n:]


SMALL_PACK_ROWS = 16


def _all_reduce_small(rows, deps=()):
    n = len(rows)

    def body(*refs):
        ins = refs[:n]
        out_ref, mine, buf, send_sems, recv_sems = refs[n + len(deps):]
        x, y, c = _my_place()
        mine[...] = jnp.zeros_like(mine)
        for (r0, a), ref in zip(rows, ins):
            mine[r0:r0 + a.shape[0], 0:a.shape[1]] = ref[...]
        buf[_flat(x, y, c)] = mine[...]
        copies = []
        for r in range(1, N_DEV):
            peer = _peer(x, y, c, r)
            send = pltpu.make_async_remote_copy(
                src_ref=mine, dst_ref=buf.at[_flat(x, y, c)],
                send_sem=send_sems.at[r - 1], recv_sem=recv_sems.at[r - 1], device_id=peer, device_id_type=MESH)
            send.start()
            recv = pltpu.make_async_remote_copy(
                src_ref=mine, dst_ref=buf.at[_flat(*peer)],
                send_sem=send_sems.at[r - 1], recv_sem=recv_sems.at[r - 1], device_id=peer, device_id_type=MESH)
            copies.append((send, recv))
        for send, recv in copies:
            send.wait_send()
            recv.wait_recv()
        acc = buf[0]
        for s in range(1, N_DEV):
            acc = acc + buf[s]
        out_ref[...] = acc

    vm = pl.BlockSpec(memory_space=pltpu.VMEM)
    shape = (SMALL_PACK_ROWS, D_MODEL)
    return pl.pallas_call(
        body, name="all_reduce_small", in_specs=[vm] * n + [ANY] * len(deps), out_specs=vm,
        out_shape=_sds(shape, F32),
        scratch_shapes=[pltpu.VMEM(shape, F32), pltpu.VMEM((N_DEV,) + shape, F32),
                        pltpu.SemaphoreType.DMA((7,)), pltpu.SemaphoreType.DMA((7,))],
    )(*[a for _, a in rows], *deps)


def _adamw_math(w, g, m, v):
    m = ADAM_B1 * m + (1.0 - ADAM_B1) * g
    v = ADAM_B2 * v + (1.0 - ADAM_B2) * (g * g)
    m_hat = m / (1.0 - ADAM_B1 ** ADAM_STEP)
    v_hat = v / (1.0 - ADAM_B2 ** ADAM_STEP)
    delta = -ADAM_LR * (m_hat / (jnp.sqrt(v_hat) + ADAM_EPS) + ADAM_WD * w)
    return delta, m, v


def _adamw_big(name, w, m, v, srcs, lands, me):
    nl, rows, cols = w.shape
    tr = next(cand for cand in (256, 128, 64, 32, 16, 8) if rows % cand == 0)

    def body(me_ref, w_ref, m_ref, v_ref, *rest):
        src_refs, land_refs = rest[:nl], rest[nl:2 * nl]
        g_ref, d_ref, mo_ref, vo_ref = rest[2 * nl:]
        for layer in range(nl):
            @pl.when(pl.program_id(0) == layer)
            def _():
                g = src_refs[layer][...].astype(F32)
                for s in range(N_DEV - 1):
                    g = g + land_refs[layer][s].astype(F32)
                delta, mn, vn = _adamw_math(w_ref[...], g, m_ref[...], v_ref[...])
                g_ref[...] = g
                d_ref[...] = delta
                mo_ref[...] = mn
                vo_ref[...] = vn

    blk = pl.BlockSpec((None, tr, cols), lambda l, i, me_ref: (l, i, 0))
    own = pl.BlockSpec((None, tr, cols), lambda l, i, me_ref: (me_ref[0], i, 0))
    peers = pl.BlockSpec((N_DEV - 1, tr, cols), lambda l, i, me_ref: (0, i, 0))
    return pl.pallas_call(
        body, name=name,
        grid_spec=pltpu.PrefetchScalarGridSpec(
            num_scalar_prefetch=1, grid=(nl, rows // tr),
            in_specs=[blk, blk, blk] + [own] * nl + [peers] * nl, out_specs=[blk] * 4),
        out_shape=[_sds((nl, rows, cols), F32)] * 4,
        compiler_params=_cparams(("arbitrary", "arbitrary")),
    )(me, w, m, v, *srcs, *lands)


def _adamw_small(ws, gs, ms, vs):
    n = len(ws)

    def body(*refs):
        w_refs, g_refs, m_refs, v_refs = (refs[i * n:(i + 1) * n] for i in range(4))
        d_out, m_out, v_out = (refs[(4 + i) * n:(5 + i) * n] for i in range(3))
        for i in range(n):
            delta, mn, vn = _adamw_math(w_refs[i][...], g_refs[i][...], m_refs[i][...], v_refs[i][...])
            d_out[i][...] = delta
            m_out[i][...] = mn
            v_out[i][...] = vn

    vm = pl.BlockSpec(memory_space=pltpu.VMEM)
    outs = pl.pallas_call(
        body, name="adamw_small", in_specs=[vm] * (4 * n), out_specs=[vm] * (3 * n),
        out_shape=[_sds(a.shape, F32) for a in ws] * 3,
    )(*ws, *gs, *ms, *vs)
    return outs[:n], outs[n:2 * n], outs[2 * n:]


SMALL_ROWS = 16


def _pad_to(a, rows, cols):
    return jnp.pad(a, ((0, rows - a.shape[0]), (0, cols - a.shape[1])))


def _place_own(blocks):
    me = _flat(*_my_place())
    return [lax.dynamic_update_slice(lax.empty((N_DEV,) + b.shape, b.dtype), b[None], (me,) + (0,) * b.ndim)
            for b in blocks]


def _ag_copies(x, y, c, blocks, bufs, send_sems, recv_sems):
    sends, recvs = [], []
    for a in range(len(blocks)):
        for r in range(1, N_DEV):
            peer = _peer(x, y, c, r)
            k = a * (N_DEV - 1) + r - 1
            make = lambda place: pltpu.make_async_remote_copy(
                src_ref=blocks[a], dst_ref=bufs[a].at[_flat(*place)],
                send_sem=send_sems.at[k], recv_sem=recv_sems.at[k], device_id=peer, device_id_type=MESH)
            sends.append(make((x, y, c)))
            recvs.append(make(peer))
    return sends, recvs


def _ag_start(groups, after):
    flat = [pair for g in groups for pair in g]
    n, ng = len(flat), len(groups)
    hbm = lambda a: pltpu.with_memory_space_constraint(a, pltpu.HBM)

    def body(*refs):
        blocks, bufs = refs[:n], refs[n:2 * n]
        sems = refs[2 * n + len(after):2 * n + len(after) + 2 * ng]
        x, y, c = _my_place()
        at = 0
        for gi, g in enumerate(groups):
            sends, _ = _ag_copies(x, y, c, blocks[at:at + len(g)], bufs[at:at + len(g)], sems[2 * gi], sems[2 * gi + 1])
            for cp in sends:
                cp.start()
            at += len(g)
        refs[-1][...] = jnp.zeros_like(refs[-1])

    sem_shapes = [pltpu.SemaphoreType.DMA((len(g) * (N_DEV - 1),)) for g in groups for _ in range(2)]
    outs = pl.pallas_call(
        body, name="gather_start",
        in_specs=[HBM] * (2 * n) + [ANY] * len(after),
        out_specs=[SEMS] * (2 * ng) + [HBM] * (2 * n) + [pl.BlockSpec(memory_space=pltpu.VMEM)],
        out_shape=sem_shapes + [pltpu.HBM(b.shape, b.dtype) for b, _ in flat]
        + [pltpu.HBM(u.shape, u.dtype) for _, u in flat] + [_sds((8, 128), F32)],
        input_output_aliases={i: 2 * ng + i for i in range(2 * n)},
        compiler_params=pltpu.CompilerParams(has_side_effects=SIDE_EFFECT),
    )(*[hbm(b) for b, _ in flat], *[hbm(u) for _, u in flat], *after)
    blocks_thru, bufs_thru = outs[2 * ng:2 * ng + n], outs[2 * ng + n:2 * ng + 2 * n]
    started, at = [], 0
    for gi, g in enumerate(groups):
        started.append((outs[2 * gi], outs[2 * gi + 1], blocks_thru[at:at + len(g)], bufs_thru[at:at + len(g)]))
        at += len(g)
    return started, outs[-1]


def _ag_wait(name, send_sems, recv_sems, blocks, bufs, after):
    n = len(blocks)

    def body(*refs):
        sends, recvs = _ag_copies(*_my_place(), refs[:n], refs[n:2 * n], refs[2 * n], refs[2 * n + 1])
        for s, r in zip(sends, recvs):
            s.wait_send()
            r.wait_recv()

    outs = pl.pallas_call(
        body, name=name,
        in_specs=[HBM] * (2 * n) + [SEMS, SEMS] + [ANY] * len(after),
        out_specs=[HBM] * (2 * n),
        out_shape=[pltpu.HBM(a.shape, a.dtype) for a in list(blocks) + list(bufs)],
        input_output_aliases={i: i for i in range(2 * n)},
        compiler_params=pltpu.CompilerParams(has_side_effects=SIDE_EFFECT),
    )(*blocks, *bufs, send_sems, recv_sems, *after)
    return outs[n:]


def _prepare_weights(p):
    n = N_DEV
    bf = lambda a: a.astype(BF16)
    gn_pack = jnp.concatenate([
        _pad_to(p['ret_gn'][0], RET_HEADS, 128), _pad_to(p['mla_q_a_norm'], 1, 128),
        _pad_to(p['mla_kv_a_norm'], 1, 128), jnp.zeros((2, 128), F32)], axis=0)
    ple = lambda l: [bf(p['ple_gate_w'][l]), bf(p['ple_proj_w'][l])]
    names = ('ret_out', 'mlp_w1_0', 'mlp_w2_0', 'ple_0', 'mla', 'layer_1')
    later = [[bf(p['ret_w_out'][0])], [bf(p['mlp_w1'][0])], [bf(p['mlp_w2'][0])], ple(0),
             [bf(p['mla_w_in'][0]), bf(p['mla_w_uq'][0]), bf(p['mla_w_ukv'][0]), bf(p['mla_w_out'][0])],
             [bf(p['mlp_w1'][1]), bf(p['mlp_w2'][1])] + ple(1)]
    bufs = _place_own([b for g in later for b in g])
    pack, wri = _all_gather([gn_pack, bf(p['ret_w_in'][0])])
    groups, at = [], 0
    for g in later:
        groups.append(list(zip(g, bufs[at:at + len(g)])))
        at += len(g)
    started, token = _ag_start(groups, (wri,))

    w = {k: p[k] for k in ('mix_norm', 'mlp_norm', 'ple_norm')}
    w['ret_gn'] = pack[:, :RET_HEADS, :RET_DV // n].transpose(1, 0, 2).reshape(RET_HEADS, RET_DV)
    w['mla_q_a_norm'] = pack[:, RET_HEADS, :MLA_Q_RANK // n].reshape(1, MLA_Q_RANK)
    w['mla_kv_a_norm'] = pack[:, RET_HEADS + 1, :MLA_KV_RANK // n].reshape(1, MLA_KV_RANK)
    w['ret_w_in'] = wri
    w['mla_q_norm'] = _pad_to(p['mla_q_norm'], 1, MLA_HD_PAD)
    w['mla_k_norm'] = _pad_to(p['mla_k_norm'], 1, MLA_HD_PAD)
    w['deps'] = (token,)

    def fetch(name, after):
        got = list(_ag_wait("gather_wait_" + name, *started[names.index(name)], after))
        if name == 'ret_out':
            return dict(ret_w_out=got[0].reshape(RET_V_W, D_MODEL))
        if name == 'mla':
            wmi, wuq, wukv, wmo = got
            return dict(mla_w_in=jnp.pad(wmi.reshape(D_MODEL, MLA_IN), ((0, 0), (0, MLA_IN_PAD - MLA_IN))),
                        mla_w_uq=jnp.pad(wuq, ((0, 0), (0, 0), (0, MLA_HD_PAD - MLA_QKD))),
                        mla_w_ukv=wukv, mla_w_out=wmo.reshape(D_MODEL, D_MODEL))
        out = {}
        if name in ('mlp_w1_0', 'layer_1'):
            out['mlp_w1'] = got.pop(0)
        if name in ('mlp_w2_0', 'layer_1'):
            out['mlp_w2'] = got.pop(0)
        if name in ('ple_0', 'layer_1'):
            out['ple_gate_w'] = got[0].reshape(D_MODEL, D_MODEL)
            out['ple_proj_w'] = got[1].transpose(1, 0, 2).reshape(PLE_DIM, D_MODEL)
        return out

    return w, fetch


def _small_grads(small, after):
    rows = [(0, small['mix_norm'][0]), (1, small['mix_norm'][1]), (2, small['mlp_norm'][0]),
            (3, small['mlp_norm'][1]), (4, small['ple_norm'][0]), (5, small['ple_norm'][1]),
            (6, small['ret_gn']), (10, small['mla_q_a_norm']), (11, small['mla_kv_a_norm']),
            (12, small['mla_q_norm']), (13, small['mla_k_norm'])]
    gs = _all_reduce_small(rows, after)
    me = _flat(*_my_place())
    n = N_DEV
    return dict(
        mix_norm=gs[0:2], mlp_norm=gs[2:4], ple_norm=gs[4:6],
        ret_gn=lax.dynamic_slice(gs, (6, me * (RET_DV // n)), (RET_HEADS, RET_DV // n)),
        mla_q_a_norm=lax.dynamic_slice(gs, (10, me * (MLA_Q_RANK // n)), (1, MLA_Q_RANK // n)),
        mla_kv_a_norm=lax.dynamic_slice(gs, (11, me * (MLA_KV_RANK // n)), (1, MLA_KV_RANK // n)),
        mla_q_norm=gs[12:13, :MLA_QKD], mla_k_norm=gs[13:14, :MLA_QKD])


def kernel(x, p, mix_norm, ret_w_in, ret_gn, ret_w_out, mla_w_in, mla_q_a_norm, mla_kv_a_norm, mla_w_uq, mla_w_ukv, mla_q_norm, mla_k_norm, mla_w_out, mlp_norm, mlp_w1, mlp_w2, ple_norm, ple_gate_w, ple_proj_w, loss_target, m_mix_norm, m_ret_w_in, m_ret_gn, m_ret_w_out, m_mla_w_in, m_mla_q_a_norm, m_mla_kv_a_norm, m_mla_w_uq, m_mla_w_ukv, m_mla_q_norm, m_mla_k_norm, m_mla_w_out, m_mlp_norm, m_mlp_w1, m_mlp_w2, m_ple_norm, m_ple_gate_w, m_ple_proj_w, v_mix_norm, v_ret_w_in, v_ret_gn, v_ret_w_out, v_mla_w_in, v_mla_q_a_norm, v_mla_kv_a_norm, v_mla_w_uq, v_mla_w_ukv, v_mla_q_norm, v_mla_k_norm, v_mla_w_out, v_mlp_norm, v_mlp_w1, v_mlp_w2, v_ple_norm, v_ple_gate_w, v_ple_proj_w):
    given = dict(locals())
    params = {n: given[n] for n in WEIGHTS}
    w, fetch = _prepare_weights(params)

    started = []

    def emit(group):
        keys = list(group)
        send, recv, srcs, lands, token = _rs_start(f"rs_start{len(started)}", [group[k] for k in keys])
        started.append((keys, send, recv, srcs, lands))
        return (token,)

    sq_err, grad_x, _, small = _local_step(x[0], p, loss_target[0], w, fetch, emit)
    loss = lax.psum(0.5 / D_MODEL * sq_err[0, 0], ("x", "y", "c"))

    grads, deltas, new_m, new_v = {}, {}, {}, {}

    def small_updates(after):
        sg = _small_grads(small, after)
        two_d = lambda a: a.reshape(-1, a.shape[-1])
        d_s, m_s, v_s = _adamw_small(
            [two_d(params[n]) for n in SMALL], [sg[n] for n in SMALL],
            [two_d(given["m_" + n]) for n in SMALL], [two_d(given["v_" + n]) for n in SMALL])
        for i, n in enumerate(SMALL):
            shape = params[n].shape
            grads[n], deltas[n], new_m[n], new_v[n] = (a.reshape(shape) for a in (sg[n], d_s[i], m_s[i], v_s[i]))
        return (d_s[0],)

    me = _flat(*_my_place()).astype(jnp.int32).reshape(1)
    after = (grad_x,)
    src_of, land_of = {}, {}
    for gi, (keys, send, recv, srcs, lands) in enumerate(started):
        if gi == len(started) - 1:
            after = small_updates(after)
        srcs, lands = _rs_wait(f"rs_wait{gi}", send, recv, srcs, lands, after)
        for k, s, l in zip(keys, srcs, lands):
            src_of[k], land_of[k] = s, l
        done = [n for n in BIG if n not in grads and all((n, l) in src_of for l in range(params[n].shape[0]))]
        for n in done:
            layers = range(params[n].shape[0])
            grads[n], deltas[n], new_m[n], new_v[n] = _adamw_big(
                "adamw_" + n, params[n], given["m_" + n], given["v_" + n],
                [src_of[(n, l)] for l in layers], [land_of[(n, l)] for l in layers], me)
        if done:
            after = (deltas[done[-1]],)

    return (loss, grad_x[None], *[grads[n] for n in WEIGHTS], *[deltas[n] for n in WEIGHTS],
            *[new_m[n] for n in WEIGHTS], *[new_v[n] for n in WEIGHTS])
```

```python
import functools
import math

import jax
import jax.numpy as jnp
from jax import lax
from jax.experimental import pallas as pl
from jax.experimental.pallas import tpu as pltpu

F32 = jnp.float32
BF16 = jnp.bfloat16
MESH = pl.DeviceIdType.MESH
ANY = pl.BlockSpec(memory_space=pl.ANY)

N_DEV = 8
D_MODEL = 1024
CHUNK = 64
EPS = 1e-6
ROPE_THETA = 10000.0
RET_HEADS = 4
RET_DK = 256
RET_DV = 512
RET_QK_W = RET_HEADS * RET_DK
RET_V_W = RET_HEADS * RET_DV
RET_IN = 2 * RET_QK_W + 2 * RET_V_W
MLA_HEADS = 8
MLA_NOPE = 128
MLA_ROPE = 64
MLA_QKD = MLA_NOPE + MLA_ROPE
MLA_VD = 128
MLA_Q_RANK = 384
MLA_KV_RANK = 256
MLA_IN = MLA_Q_RANK + MLA_KV_RANK + MLA_ROPE
MLA_IN_PAD = 768
MLA_HD_PAD = 256
D_FF = 4096
PLE_DIM = 256
ATT_SCALE = MLA_QKD ** -0.5
LOG2E = 1.4426950408889634
ATT_EXP2 = ATT_SCALE * LOG2E

ADAM_LR = 0.001
ADAM_B1 = 0.9
ADAM_B2 = 0.999
ADAM_EPS = 1e-08
ADAM_WD = 0.01
ADAM_STEP = 10

VMEM_LIMIT = 52 * 1024 * 1024
ROW_TILE = 1024
RET_ROWS = 256
ATT_BLOCK = 256
ATT_QROWS = 1024
ATT_KROWS = 1024
ATT_HEADS = 2

WEIGHTS = ['mix_norm', 'ret_w_in', 'ret_gn', 'ret_w_out', 'mla_w_in', 'mla_q_a_norm', 'mla_kv_a_norm',
           'mla_w_uq', 'mla_w_ukv', 'mla_q_norm', 'mla_k_norm', 'mla_w_out', 'mlp_norm', 'mlp_w1', 'mlp_w2',
           'ple_norm', 'ple_gate_w', 'ple_proj_w']
BIG = ['ret_w_in', 'ret_w_out', 'mla_w_in', 'mla_w_uq', 'mla_w_ukv', 'mla_w_out', 'mlp_w1', 'mlp_w2',
       'ple_gate_w', 'ple_proj_w']
SMALL = [w for w in WEIGHTS if w not in BIG]


def _cparams(sem=None):
    return pltpu.CompilerParams(dimension_semantics=sem, vmem_limit_bytes=VMEM_LIMIT)


def _dot(a, b, ca, cb):
    return lax.dot_general(a, b, (((ca,), (cb,)), ((), ())), preferred_element_type=F32)


def _bf(v):
    return v if v.dtype == BF16 else v.astype(BF16)


def _sigmoid(z):
    return 1.0 / (1.0 + jnp.exp(-z))


def _mm(name, grid, a, a_spec, b, b_spec, contract, outs, extras=(), epi=None, deps=(), split=None):
    nk = grid[2]
    n_ex, n_out, n_dep = len(extras), len(outs), len(deps)
    acc_shape = tuple(d for d in outs[0][1].block_shape if d is not None)
    if split is not None:
        acc_shape = (acc_shape[1], acc_shape[0] * split)

    def body(*refs):
        a_ref, b_ref = refs[:2]
        ex_refs = refs[2:2 + n_ex]
        out_refs = refs[2 + n_ex + n_dep:2 + n_ex + n_dep + n_out]

        def product():
            return _dot(_bf(a_ref[...]), _bf(b_ref[...]), contract[0], contract[1])

        def finish(acc):
            if split is not None:
                for j in range(acc_shape[1] // split):
                    out_refs[0][j] = acc[:, j * split:(j + 1) * split].astype(out_refs[0].dtype)
                return
            acc = acc[...]
            res = epi(acc, *[r[...] for r in ex_refs]) if epi is not None else (acc,)
            for o, r in zip(out_refs, res):
                o[...] = r.astype(o.dtype)

        if nk == 1:
            finish(product())
        else:
            acc_ref = refs[-1]
            k = pl.program_id(2)

            @pl.when(k == 0)
            def _():
                acc_ref[...] = jnp.zeros_like(acc_ref)

            acc_ref[...] += product()

            @pl.when(k == nk - 1)
            def _():
                finish(acc_ref)

    return pl.pallas_call(
        body, name=name, grid=grid,
        in_specs=[a_spec, b_spec] + [s for _, s in extras] + [ANY] * n_dep,
        out_specs=[s for _, s in outs],
        out_shape=[s for s, _ in outs],
        scratch_shapes=[pltpu.VMEM(acc_shape, F32)] if nk > 1 else [],
        compiler_params=_cparams(("parallel", "parallel", "arbitrary")),
    )(a, b, *[x for x, _ in extras], *deps)


def _sds(shape, dtype):
    return jax.ShapeDtypeStruct(shape, dtype)


def _row_tile(t, cap=ROW_TILE):
    return min(cap, t)


def _rms_fwd(name, x, g):
    t, d = x.shape
    tm = _row_tile(t)

    def body(x_ref, g_ref, o_ref):
        xv = x_ref[...]
        r = lax.rsqrt(jnp.mean(xv * xv, axis=-1, keepdims=True) + EPS)
        o_ref[...] = (xv * r * g_ref[...]).astype(o_ref.dtype)

    return pl.pallas_call(
        body, name=name, grid=(t // tm,),
        in_specs=[pl.BlockSpec((tm, d), lambda i: (i, 0)), pl.BlockSpec((1, d), lambda i: (0, 0))],
        out_specs=pl.BlockSpec((tm, d), lambda i: (i, 0)),
        out_shape=_sds((t, d), BF16),
        compiler_params=_cparams(("parallel",)),
    )(x, g)


def _rms_bwd_rows(dy, xv, g, n):
    r = lax.rsqrt(jnp.sum(xv * xv, axis=-1, keepdims=True) / n + EPS)
    xh = xv * r
    dxh = dy * g
    dx = r * (dxh - xh * (jnp.sum(dxh * xh, axis=-1, keepdims=True) / n))
    return dx, dy * xh


def _rms_bwd(name, dy, x, g, res):
    t, d = x.shape
    tm = _row_tile(t, 512)

    def body(dy_ref, x_ref, g_ref, res_ref, dx_ref, dg_ref):
        @pl.when(pl.program_id(0) == 0)
        def _():
            dg_ref[...] = jnp.zeros_like(dg_ref)

        dx, dgr = _rms_bwd_rows(dy_ref[...], x_ref[...], g_ref[...], d)
        dx_ref[...] = res_ref[...] + dx
        dg_ref[...] += jnp.sum(dgr, axis=0, keepdims=True)

    row = pl.BlockSpec((tm, d), lambda i: (i, 0))
    vec = pl.BlockSpec((1, d), lambda i: (0, 0))
    return pl.pallas_call(
        body, name=name, grid=(t // tm,),
        in_specs=[row, row, vec, row], out_specs=[row, vec],
        out_shape=[_sds((t, d), F32), _sds((1, d), F32)],
        compiler_params=_cparams(("arbitrary",)),
    )(dy, x, g, res)


def _loss_head(y, target):
    t, d = y.shape
    tm = _row_tile(t)

    def body(y_ref, t_ref, dy_ref, l_ref):
        @pl.when(pl.program_id(0) == 0)
        def _():
            l_ref[...] = jnp.zeros_like(l_ref)

        e = y_ref[...] - t_ref[...]
        dy_ref[...] = e / d
        l_ref[...] += jnp.sum(jnp.sum(e * e, axis=-1, keepdims=True), axis=0, keepdims=True)

    row = pl.BlockSpec((tm, d), lambda i: (i, 0))
    return pl.pallas_call(
        body, name="loss_head", grid=(t // tm,),
        in_specs=[row, row], out_specs=[row, pl.BlockSpec((8, 128), lambda i: (0, 0))],
        out_shape=[_sds((t, d), F32), _sds((8, 128), F32)],
        compiler_params=_cparams(("arbitrary",)),
    )(y, target)


def _ple_gate_bwd(name, dh, gate, e):
    t, d = dh.shape
    tm = _row_tile(t)

    def body(dh_ref, g_ref, e_ref, de_ref, dz_ref):
        dh_v, gt = dh_ref[...], g_ref[...]
        de_ref[...] = (dh_v * gt).astype(BF16)
        dz_ref[...] = (dh_v * e_ref[...] * (gt * (1.0 - gt))).astype(BF16)

    row = pl.BlockSpec((tm, d), lambda i: (i, 0))
    return pl.pallas_call(
        body, name=name, grid=(t // tm,), in_specs=[row, row, row], out_specs=[row, row],
        out_shape=[_sds((t, d), BF16), _sds((t, d), BF16)],
        compiler_params=_cparams(("parallel",)),
    )(dh, gate, e)


def _rope_half(v, cos, sin):
    half = v.shape[-1] // 2
    v1, v2 = v[:, :half], v[:, half:]
    return jnp.concatenate([v1 * cos - v2 * sin, v2 * cos + v1 * sin], axis=-1)


def _ret_consts():
    lg = jnp.log(1.0 - 2.0 ** (-5.0 - jnp.arange(RET_HEADS, dtype=F32)))
    idx = jnp.arange(CHUNK, dtype=F32)
    intra = jnp.exp(lg[:, None, None] * jnp.abs(idx[:, None] - idx[None, :]))
    qdec = jnp.exp(lg[:, None] * (idx + 1.0))
    kdec = jnp.exp(lg[:, None] * (CHUNK - 1.0 - idx))
    cdec = jnp.exp(lg * CHUNK)
    qdec = jnp.broadcast_to(qdec[:, :, None], (RET_HEADS, CHUNK, RET_DK))
    kdec = jnp.broadcast_to(kdec[:, :, None], (RET_HEADS, CHUNK, RET_DK))
    cdec = jnp.broadcast_to(cdec[:, None, None], (RET_HEADS, 1, RET_DV))
    return intra, qdec, kdec, cdec


def _ret_specs(rb, rev_nb=None):
    blk = (lambda i: i) if rev_nb is None else (lambda i: rev_nb - 1 - i)
    full = lambda shape: pl.BlockSpec(shape, lambda i: (0,) * len(shape))
    return dict(
        proj=pl.BlockSpec((rb, RET_IN), lambda i: (blk(i), 0)),
        tab=pl.BlockSpec((rb, RET_DK // 2), lambda i: (blk(i), 0)),
        vw=pl.BlockSpec((rb, RET_V_W), lambda i: (blk(i), 0)),
        st=pl.BlockSpec((rb // CHUNK, RET_HEADS, RET_DK, RET_DV), lambda i: (blk(i), 0, 0, 0)),
        gn=full((RET_HEADS, 1, RET_DV)),
        intra=full((RET_HEADS, CHUNK, CHUNK)),
        dec=full((RET_HEADS, CHUNK, RET_DK)),
        cdec=full((RET_HEADS, 1, RET_DV)),
    )


def _ret_fwd(proj, cos, sin, gn):
    t = proj.shape[0]
    rb = min(RET_ROWS, t)
    cpb = rb // CHUNK
    intra, qdec, kdec, cdec = _ret_consts()
    sp = _ret_specs(rb)

    def body(proj_ref, cos_ref, sin_ref, gn_ref, intra_ref, qd_ref, kd_ref, cd_ref,
             gated_ref, outp_ref, st_ref, s_ref):
        @pl.when(pl.program_id(0) == 0)
        def _():
            s_ref[...] = jnp.zeros_like(s_ref)

        def chunk(c, carry):
            rows = pl.ds(pl.multiple_of(c * CHUNK, CHUNK), CHUNK)
            cs, sn = cos_ref[rows, :], sin_ref[rows, :]
            for h in range(RET_HEADS):
                q = proj_ref[rows, h * RET_DK:(h + 1) * RET_DK]
                k = proj_ref[rows, RET_QK_W + h * RET_DK:RET_QK_W + (h + 1) * RET_DK]
                v = proj_ref[rows, 2 * RET_QK_W + h * RET_DV:2 * RET_QK_W + (h + 1) * RET_DV]
                g = proj_ref[rows, 2 * RET_QK_W + RET_V_W + h * RET_DV:2 * RET_QK_W + RET_V_W + (h + 1) * RET_DV]
                qr = _rope_half(q, cs, sn)
                kr = _rope_half(k, cs, sn) * (RET_DK ** -0.5)
                qb, kb, vb = qr.astype(BF16), kr.astype(BF16), v.astype(BF16)
                sc = _dot(qb, kb, 1, 1) * intra_ref[h]
                inner = _dot(sc.astype(BF16), vb, 1, 0)
                s_old = s_ref[h]
                sb = s_old.astype(BF16)
                st_ref[c, h] = sb
                cross = _dot((qr * qd_ref[h]).astype(BF16), sb, 1, 0)
                out = inner + cross
                s_ref[h] = s_old * cd_ref[h] + _dot((kr * kd_ref[h]).astype(BF16), vb, 0, 0)
                r = lax.rsqrt(jnp.mean(out * out, axis=-1, keepdims=True) + EPS)
                y = out * r * gn_ref[h]
                cols = slice(h * RET_DV, (h + 1) * RET_DV)
                gated_ref[rows, cols] = (g * _sigmoid(g) * y).astype(BF16)
                outp_ref[rows, cols] = out
            return carry

        lax.fori_loop(0, cpb, chunk, 0)

    return pl.pallas_call(
        body, name="ret_fwd", grid=(t // rb,),
        in_specs=[sp['proj'], sp['tab'], sp['tab'], sp['gn'], sp['intra'], sp['dec'], sp['dec'], sp['cdec']],
        out_specs=[sp['vw'], sp['vw'], sp['st']],
        out_shape=[_sds((t, RET_V_W), BF16), _sds((t, RET_V_W), F32),
                   _sds((t // CHUNK, RET_HEADS, RET_DK, RET_DV), BF16)],
        scratch_shapes=[pltpu.VMEM((RET_HEADS, RET_DK, RET_DV), F32)],
        compiler_params=_cparams(("arbitrary",)),
    )(proj, cos, sin, gn.reshape(RET_HEADS, 1, RET_DV), intra, qdec, kdec, cdec)


def _ret_bwd(proj, cos, sin, gn, outp, states, dgated):
    t = proj.shape[0]
    rb = min(RET_ROWS, t)
    cpb = rb // CHUNK
    nb = t // rb
    intra, qdec, kdec, cdec = _ret_consts()
    sp = _ret_specs(rb, rev_nb=nb)

    def body(proj_ref, cos_ref, sin_ref, gn_ref, intra_ref, qd_ref, kd_ref, cd_ref, outp_ref, st_ref, dgt_ref,
             dproj_ref, dgn_ref, ds_ref):
        @pl.when(pl.program_id(0) == 0)
        def _():
            ds_ref[...] = jnp.zeros_like(ds_ref)
            dgn_ref[...] = jnp.zeros_like(dgn_ref)

        def chunk(cc, carry):
            c = cpb - 1 - cc
            rows = pl.ds(pl.multiple_of(c * CHUNK, CHUNK), CHUNK)
            cs, sn = cos_ref[rows, :], sin_ref[rows, :]
            for h in range(RET_HEADS):
                q = proj_ref[rows, h * RET_DK:(h + 1) * RET_DK]
                k = proj_ref[rows, RET_QK_W + h * RET_DK:RET_QK_W + (h + 1) * RET_DK]
                v = proj_ref[rows, 2 * RET_QK_W + h * RET_DV:2 * RET_QK_W + (h + 1) * RET_DV]
                g = proj_ref[rows, 2 * RET_QK_W + RET_V_W + h * RET_DV:2 * RET_QK_W + RET_V_W + (h + 1) * RET_DV]
                cols = slice(h * RET_DV, (h + 1) * RET_DV)
                qr = _rope_half(q, cs, sn)
                kr = _rope_half(k, cs, sn) * (RET_DK ** -0.5)
                qb, kb, vb = qr.astype(BF16), kr.astype(BF16), v.astype(BF16)
                qdb = (qr * qd_ref[h]).astype(BF16)
                kdb = (kr * kd_ref[h]).astype(BF16)
                out = outp_ref[rows, cols]
                dgt = dgt_ref[rows, cols]
                gnh = gn_ref[h]
                r = lax.rsqrt(jnp.mean(out * out, axis=-1, keepdims=True) + EPS)
                xh = out * r
                sg = _sigmoid(g)
                dgate = dgt * (xh * gnh) * (sg * (1.0 + g * (1.0 - sg)))
                dy = dgt * (g * sg)
                dgn_ref[h] += jnp.sum(dy * xh, axis=0, keepdims=True)
                dxh = dy * gnh
                dout = r * (dxh - xh * jnp.mean(dxh * xh, axis=-1, keepdims=True))
                doutb = dout.astype(BF16)
                itr = intra_ref[h]
                pb = (_dot(qb, kb, 1, 1) * itr).astype(BF16)
                dv = _dot(pb, doutb, 0, 0)
                dsc = (_dot(doutb, vb, 1, 1) * itr).astype(BF16)
                dq = _dot(dsc, kb, 1, 0)
                dk = _dot(dsc, qb, 0, 0)
                dq = dq + _dot(doutb, st_ref[c, h], 1, 1) * qd_ref[h]
                ds_new = ds_ref[h]
                dsb = ds_new.astype(BF16)
                dk = dk + _dot(vb, dsb, 1, 1) * kd_ref[h]
                dv = dv + _dot(kdb, dsb, 1, 0)
                ds_ref[h] = ds_new * cd_ref[h] + _dot(qdb, doutb, 0, 0)
                dproj_ref[rows, h * RET_DK:(h + 1) * RET_DK] = _rope_half(dq, cs, -sn).astype(BF16)
                dproj_ref[rows, RET_QK_W + h * RET_DK:RET_QK_W + (h + 1) * RET_DK] = (
                    _rope_half(dk * (RET_DK ** -0.5), cs, -sn).astype(BF16))
                dproj_ref[rows, 2 * RET_QK_W + h * RET_DV:2 * RET_QK_W + (h + 1) * RET_DV] = dv.astype(BF16)
                dproj_ref[rows, 2 * RET_QK_W + RET_V_W + h * RET_DV:
                          2 * RET_QK_W + RET_V_W + (h + 1) * RET_DV] = dgate.astype(BF16)
            return carry

        lax.fori_loop(0, cpb, chunk, 0)

    return pl.pallas_call(
        body, name="ret_bwd", grid=(nb,),
        in_specs=[sp['proj'], sp['tab'], sp['tab'], sp['gn'], sp['intra'], sp['dec'], sp['dec'], sp['cdec'],
                  sp['vw'], sp['st'], sp['vw']],
        out_specs=[sp['proj'], sp['gn']],
        out_shape=[_sds((t, RET_IN), BF16), _sds((RET_HEADS, 1, RET_DV), F32)],
        scratch_shapes=[pltpu.VMEM((RET_HEADS, RET_DK, RET_DV), F32)],
        compiler_params=_cparams(("arbitrary",)),
    )(proj, cos, sin, gn.reshape(RET_HEADS, 1, RET_DV), intra, qdec, kdec, cdec, outp, states, dgated)


def _mla_tables(t):
    half = MLA_ROPE // 2
    inv = 1.0 / (ROPE_THETA ** (jnp.arange(0, MLA_ROPE, 2, dtype=F32) / MLA_ROPE))
    ang = jnp.arange(t, dtype=F32)[:, None] * inv[None, :]
    cos, sin = jnp.cos(ang), jnp.sin(ang)
    z = jnp.zeros((t, half), F32)
    c = jnp.concatenate([cos, cos, z, z], axis=1)
    s1 = jnp.concatenate([-sin, z, z, z], axis=1)
    s2 = jnp.concatenate([z, sin, z, z], axis=1)
    return c, s1, s2


def _rope_tile(r, c, s1, s2):
    return r * c + pltpu.roll(r, 96, 1) * s1 + pltpu.roll(r, 32, 1) * s2


def _mla_mid(proj2, qa, kva):
    t = proj2.shape[0]
    tm = _row_tile(t)

    def body(p_ref, qa_ref, kva_ref, cq_ref, ckv_ref):
        cq = p_ref[:, :MLA_Q_RANK]
        ckv = p_ref[:, MLA_Q_RANK:MLA_Q_RANK + MLA_KV_RANK]
        rq = lax.rsqrt(jnp.mean(cq * cq, axis=-1, keepdims=True) + EPS)
        rkv = lax.rsqrt(jnp.mean(ckv * ckv, axis=-1, keepdims=True) + EPS)
        cq_ref[...] = (cq * rq * qa_ref[...]).astype(BF16)
        ckv_ref[...] = (ckv * rkv * kva_ref[...]).astype(BF16)

    return pl.pallas_call(
        body, name="mla_mid", grid=(t // tm,),
        in_specs=[pl.BlockSpec((tm, MLA_IN_PAD), lambda i: (i, 0)),
                  pl.BlockSpec((1, MLA_Q_RANK), lambda i: (0, 0)),
                  pl.BlockSpec((1, MLA_KV_RANK), lambda i: (0, 0))],
        out_specs=[pl.BlockSpec((tm, MLA_Q_RANK), lambda i: (i, 0)),
                   pl.BlockSpec((tm, MLA_KV_RANK), lambda i: (i, 0))],
        out_shape=[_sds((t, MLA_Q_RANK), BF16), _sds((t, MLA_KV_RANK), BF16)],
        compiler_params=_cparams(("parallel",)),
    )(proj2, qa, kva)


def _mla_mid_bwd(proj2, qa, kva, dcq, dckv, dkr):
    t = proj2.shape[0]
    tm = _row_tile(t)

    def body(p_ref, qa_ref, kva_ref, dcq_ref, dckv_ref, dkr_ref, dp_ref, dqa_ref, dkva_ref):
        @pl.when(pl.program_id(0) == 0)
        def _():
            dqa_ref[...] = jnp.zeros_like(dqa_ref)
            dkva_ref[...] = jnp.zeros_like(dkva_ref)

        dxq, dgq = _rms_bwd_rows(dcq_ref[...], p_ref[:, :MLA_Q_RANK], qa_ref[...], MLA_Q_RANK)
        dxk, dgk = _rms_bwd_rows(dckv_ref[...], p_ref[:, MLA_Q_RANK:MLA_Q_RANK + MLA_KV_RANK], kva_ref[...],
                                 MLA_KV_RANK)
        dp_ref[:, :MLA_Q_RANK] = dxq.astype(BF16)
        dp_ref[:, MLA_Q_RANK:MLA_Q_RANK + MLA_KV_RANK] = dxk.astype(BF16)
        dp_ref[:, MLA_Q_RANK + MLA_KV_RANK:] = dkr_ref[...].astype(BF16)
        dqa_ref[...] += jnp.sum(dgq, axis=0, keepdims=True)
        dkva_ref[...] += jnp.sum(dgk, axis=0, keepdims=True)

    return pl.pallas_call(
        body, name="mla_mid_bwd", grid=(t // tm,),
        in_specs=[pl.BlockSpec((tm, MLA_IN_PAD), lambda i: (i, 0)),
                  pl.BlockSpec((1, MLA_Q_RANK), lambda i: (0, 0)),
                  pl.BlockSpec((1, MLA_KV_RANK), lambda i: (0, 0)),
                  pl.BlockSpec((tm, MLA_Q_RANK), lambda i: (i, 0)),
                  pl.BlockSpec((tm, MLA_KV_RANK), lambda i: (i, 0)),
                  pl.BlockSpec((tm, 128), lambda i: (i, 0))],
        out_specs=[pl.BlockSpec((tm, MLA_IN_PAD), lambda i: (i, 0)),
                   pl.BlockSpec((1, MLA_Q_RANK), lambda i: (0, 0)),
                   pl.BlockSpec((1, MLA_KV_RANK), lambda i: (0, 0))],
        out_shape=[_sds((t, MLA_IN_PAD), BF16), _sds((1, MLA_Q_RANK), F32), _sds((1, MLA_KV_RANK), F32)],
        compiler_params=_cparams(("arbitrary",)),
    )(proj2, qa, kva, dcq, dckv, dkr)


def _mla_prep_specs(t, tm):
    head = lambda w: pl.BlockSpec((None, tm, w), lambda i, h: (h, i, 0))
    return dict(
        head256=head(MLA_HD_PAD), head128=head(MLA_VD),
        cols256=pl.BlockSpec((tm, MLA_HD_PAD), lambda i, h: (i, h)),
        cq=pl.BlockSpec((tm, MLA_Q_RANK), lambda i, h: (i, 0)),
        ckv=pl.BlockSpec((tm, MLA_KV_RANK), lambda i, h: (i, 0)),
        wuq=pl.BlockSpec((None, MLA_Q_RANK, MLA_HD_PAD), lambda i, h: (h, 0, 0)),
        wukv=pl.BlockSpec((None, MLA_KV_RANK, MLA_HD_PAD), lambda i, h: (h, 0, 0)),
        kr=pl.BlockSpec((tm, 128), lambda i, h: (i, (MLA_Q_RANK + MLA_KV_RANK) // 128)),
        gain=pl.BlockSpec((1, MLA_HD_PAD), lambda i, h: (0, 0)),
        tab=pl.BlockSpec((tm, 128), lambda i, h: (i, 0)),
    )


def _mla_prep(cq, ckv, wuq, wukv, proj2, gq, gk, tabs):
    t = cq.shape[0]
    tm = _row_tile(t)
    sp = _mla_prep_specs(t, tm)

    def body(cq_ref, ckv_ref, wuq_ref, wukv_ref, kr_ref, gq_ref, gk_ref, c_ref, s1_ref, s2_ref,
             qh_ref, kh_ref, vh_ref):
        c, s1, s2 = c_ref[...], s1_ref[...], s2_ref[...]

        def norm_rope(xv, gain):
            r = lax.rsqrt(jnp.sum(xv * xv, axis=-1, keepdims=True) / MLA_QKD + EPS)
            y = xv * r * gain
            return jnp.concatenate([y[:, :MLA_NOPE], _rope_tile(y[:, MLA_NOPE:], c, s1, s2)], axis=-1)

        kvv = _dot(ckv_ref[...], wukv_ref[...], 1, 0)
        qh_ref[...] = norm_rope(_dot(cq_ref[...], wuq_ref[...], 1, 0), gq_ref[...]).astype(BF16)
        kf = jnp.concatenate([kvv[:, :MLA_NOPE], kr_ref[...]], axis=-1)
        kh_ref[...] = norm_rope(kf, gk_ref[...]).astype(BF16)
        vh_ref[...] = jnp.concatenate([kvv[:, MLA_NOPE:], jnp.ones((tm, MLA_VD), F32)], axis=-1).astype(BF16)

    return pl.pallas_call(
        body, name="mla_prep", grid=(t // tm, MLA_HEADS),
        in_specs=[sp['cq'], sp['ckv'], sp['wuq'], sp['wukv'], sp['kr'], sp['gain'], sp['gain'],
                  sp['tab'], sp['tab'], sp['tab']],
        out_specs=[sp['head256'], sp['head256'], sp['head256']],
        out_shape=[_sds((MLA_HEADS, t, MLA_HD_PAD), BF16), _sds((MLA_HEADS, t, MLA_HD_PAD), BF16),
                   _sds((MLA_HEADS, t, 2 * MLA_VD), BF16)],
        compiler_params=_cparams(("parallel", "arbitrary")),
    )(cq, ckv, wuq, wukv, proj2, gq, gk, *tabs)


def _mla_prep_bwd(cq, ckv, wuq, wukv, proj2, gq, gk, tabs, dqt, dkh, dvh):
    t = cq.shape[0]
    tm = _row_tile(t)
    ab = dqt.shape[-1]
    sp = _mla_prep_specs(t, tm)

    def body(cq_ref, ckv_ref, wuq_ref, wukv_ref, kr_ref, gq_ref, gk_ref, c_ref, s1_ref, s2_ref,
             dqt_ref, dkh_ref, dvh_ref, dq_ref, dkv_ref, dkr_ref, dgq_ref, dgk_ref):
        dqh = jnp.concatenate([dqt_ref[b].T for b in range(tm // ab)], axis=0)
        i, h = pl.program_id(0), pl.program_id(1)

        @pl.when((i == 0) & (h == 0))
        def _():
            dgq_ref[...] = jnp.zeros_like(dgq_ref)
            dgk_ref[...] = jnp.zeros_like(dgk_ref)

        @pl.when(h == 0)
        def _():
            dkr_ref[...] = jnp.zeros_like(dkr_ref)

        c, s1, s2 = c_ref[...], s1_ref[...], s2_ref[...]

        def back(xv, gain, dout):
            dy = jnp.concatenate([dout[:, :MLA_NOPE], _rope_tile(dout[:, MLA_NOPE:], c, -s1, -s2)], axis=-1)
            return _rms_bwd_rows(dy, xv, gain, MLA_QKD)

        kvv = _dot(ckv_ref[...], wukv_ref[...], 1, 0)
        dxq, dgq = back(_dot(cq_ref[...], wuq_ref[...], 1, 0), gq_ref[...], dqh)
        kf = jnp.concatenate([kvv[:, :MLA_NOPE], kr_ref[...]], axis=-1)
        dxk, dgk = back(kf, gk_ref[...], dkh_ref[...])
        dq_ref[...] = dxq.astype(BF16)
        dkv_ref[...] = jnp.concatenate([dxk[:, :MLA_NOPE], dvh_ref[...]], axis=-1).astype(BF16)
        dkr_ref[...] += dxk[:, MLA_NOPE:]
        dgq_ref[...] += jnp.sum(dgq, axis=0, keepdims=True)
        dgk_ref[...] += jnp.sum(dgk, axis=0, keepdims=True)

    return pl.pallas_call(
        body, name="mla_prep_bwd", grid=(t // tm, MLA_HEADS),
        in_specs=[sp['cq'], sp['ckv'], sp['wuq'], sp['wukv'], sp['kr'], sp['gain'], sp['gain'],
                  sp['tab'], sp['tab'], sp['tab'],
                  pl.BlockSpec((None, tm // ab, MLA_HD_PAD, ab), lambda i, h: (h, i, 0, 0)),
                  sp['head256'], sp['head128']],
        out_specs=[sp['cols256'], sp['cols256'], sp['tab'], sp['gain'], sp['gain']],
        out_shape=[_sds((t, MLA_HEADS * MLA_HD_PAD), BF16), _sds((t, MLA_HEADS * MLA_HD_PAD), BF16),
                   _sds((t, 128), F32), _sds((1, MLA_HD_PAD), F32), _sds((1, MLA_HD_PAD), F32)],
        compiler_params=_cparams(("arbitrary", "arbitrary")),
    )(cq, ckv, wuq, wukv, proj2, gq, gk, *tabs, dqt, dkh, dvh)


def _chunk_visible(rows, cols, row_off, col_off):
    rq = lax.shift_right_logical(lax.broadcasted_iota(jnp.int32, (rows, cols), 0) + row_off, 6)
    ck = lax.shift_right_logical(lax.broadcasted_iota(jnp.int32, (rows, cols), 1) + col_off, 6)
    return ck <= rq


def _rows_to_lanes(col):
    return col.T[:8, :]


def _attn_fwd(qh, kh, vh):
    t = qh.shape[1]
    ab = min(ATT_BLOCK, t)
    tq = min(ATT_QROWS, t)
    r = tq // ab
    hg = ATT_HEADS

    def body(q_ref, k_ref, v_ref, o_ref, lse_ref):
        n_un = pl.program_id(1) * r

        def step(b, state, diag):
            rows = pl.ds(pl.multiple_of(b * ab, ab), ab)
            ms, accs = [], []
            for hh in range(hg):
                m, acc = state[0][hh], state[1][hh]
                s = _dot(q_ref[hh], k_ref[hh, rows, :], 1, 1)
                if diag is not None:
                    s = jnp.where(_chunk_visible(tq, ab, 0, diag * ab), s, -1e30)
                m_new = jnp.maximum(m, jnp.max(s, axis=-1, keepdims=True))
                p = jnp.exp2((s - m_new) * ATT_EXP2).astype(BF16)
                accs.append(jnp.exp2((m - m_new) * ATT_EXP2) * acc + _dot(p, v_ref[hh, rows, :], 1, 0))
                ms.append(m_new)
            return tuple(ms), tuple(accs)

        heads = lambda v: tuple(v for _ in range(hg))
        state = (heads(jnp.full((tq, 1), -1e30, F32)), heads(jnp.zeros((tq, 2 * MLA_VD), F32)))
        state = lax.fori_loop(0, n_un, lambda b, st: step(b, st, None), state)
        for d in range(r):
            state = step(n_un + d, state, d)
        ms, accs = state
        for hh in range(hg):
            l = accs[hh][:, MLA_VD:]
            o_ref[:, hh * MLA_VD:(hh + 1) * MLA_VD] = accs[hh][:, :MLA_VD] / l
            lse_t = _rows_to_lanes(ms[hh] * ATT_EXP2 + jnp.log(l) * LOG2E)
            for d in range(r):
                lse_ref[hh, d] = lse_t[:, d * ab:(d + 1) * ab]

    return pl.pallas_call(
        body, name="mla_attn", grid=(MLA_HEADS // hg, t // tq),
        in_specs=[pl.BlockSpec((hg, tq, MLA_HD_PAD), lambda g, i: (g, i, 0)),
                  pl.BlockSpec((hg, t, MLA_HD_PAD), lambda g, i: (g, 0, 0)),
                  pl.BlockSpec((hg, t, 2 * MLA_VD), lambda g, i: (g, 0, 0))],
        out_specs=[pl.BlockSpec((tq, hg * MLA_VD), lambda g, i: (i, g)),
                   pl.BlockSpec((hg, r, 8, ab), lambda g, i: (g, i, 0, 0))],
        out_shape=[_sds((t, MLA_HEADS * MLA_VD), F32), _sds((MLA_HEADS, t // ab, 8, ab), F32)],
        compiler_params=_cparams(("parallel", "arbitrary")),
    )(qh, kh, vh)


def _attn_delta(do, o, ab):
    t = do.shape[0]
    tm = _row_tile(t)

    def body(do_ref, o_ref, d_ref):
        d = jnp.sum(do_ref[...] * o_ref[...], axis=-1, keepdims=True)
        d_t = _rows_to_lanes(jnp.broadcast_to(d, (tm, 128)))
        for b in range(tm // ab):
            d_ref[b] = d_t[:, b * ab:(b + 1) * ab]

    col = pl.BlockSpec((tm, MLA_VD), lambda i, h: (i, h))
    return pl.pallas_call(
        body, name="mla_delta", grid=(t // tm, MLA_HEADS), in_specs=[col, col],
        out_specs=pl.BlockSpec((None, tm // ab, 8, ab), lambda i, h: (h, i, 0, 0)),
        out_shape=_sds((MLA_HEADS, t // ab, 8, ab), F32),
        compiler_params=_cparams(("parallel", "parallel")),
    )(do, o)


def _attn_bwd(qh, kh, vh, dob, lse_t, dl_t):
    t = qh.shape[1]
    ab = min(ATT_BLOCK, t)
    kb = min(ATT_KROWS, t)
    r = kb // ab
    nq = t // ab
    hg = ATT_HEADS

    def body(q_ref, k_ref, v_ref, do_ref, lse_ref, dl_ref, dqt_ref, dk_ref, dv_ref):
        j = pl.program_id(1)

        @pl.when(j == 0)
        def _():
            dqt_ref[...] = jnp.zeros_like(dqt_ref)

        ks = [k_ref[hh] for hh in range(hg)]
        vs = [v_ref[hh, :, :MLA_VD] for hh in range(hg)]
        kts = [k.T for k in ks]

        def step(b, grads, diag):
            rows = pl.ds(pl.multiple_of(b * ab, ab), ab)
            out = []
            for hh in range(hg):
                dk, dv = grads[hh]
                q = q_ref[hh, rows, :]
                do = do_ref[rows, hh * MLA_VD:(hh + 1) * MLA_VD]
                s_t = _dot(ks[hh], q, 1, 1)
                if diag is not None:
                    key_chunk = lax.shift_right_logical(lax.broadcasted_iota(jnp.int32, (kb, ab), 0), 6)
                    query_chunk = lax.shift_right_logical(
                        lax.broadcasted_iota(jnp.int32, (kb, ab), 1) + diag * ab, 6)
                    s_t = jnp.where(key_chunk <= query_chunk, s_t, -1e30)
                p_t = jnp.exp2(s_t * ATT_EXP2 - lse_ref[hh, b][0:1, :])
                dp_t = _dot(vs[hh], do, 1, 1)
                ds_t = (p_t * (dp_t - dl_ref[hh, b][0:1, :]) * ATT_SCALE).astype(BF16)
                dqt_ref[hh, b] += _dot(kts[hh], ds_t, 1, 0)
                out.append((dk + _dot(ds_t, q, 1, 0), dv + _dot(p_t.astype(BF16), do, 1, 0)))
            return tuple(out)

        grads = tuple((jnp.zeros((kb, MLA_HD_PAD), F32), jnp.zeros((kb, MLA_VD), F32)) for _ in range(hg))
        for d in range(r):
            grads = step(j * r + d, grads, d)
        grads = lax.fori_loop((j + 1) * r, nq, lambda b, g: step(b, g, None), grads)
        for hh in range(hg):
            dk_ref[hh] = grads[hh][0]
            dv_ref[hh] = grads[hh][1]

    whole = lambda w: pl.BlockSpec((hg, t, w), lambda g, j: (g, 0, 0))
    blk = lambda w: pl.BlockSpec((hg, kb, w), lambda g, j: (g, j, 0))
    stat = pl.BlockSpec((hg, nq, 8, ab), lambda g, j: (g, 0, 0, 0))
    return pl.pallas_call(
        body, name="mla_attn_bwd", grid=(MLA_HEADS // hg, t // kb),
        in_specs=[whole(MLA_HD_PAD), blk(MLA_HD_PAD), blk(2 * MLA_VD),
                  pl.BlockSpec((t, hg * MLA_VD), lambda g, j: (0, g)), stat, stat],
        out_specs=[pl.BlockSpec((hg, nq, MLA_HD_PAD, ab), lambda g, j: (g, 0, 0, 0)), blk(MLA_HD_PAD), blk(MLA_VD)],
        out_shape=[_sds((MLA_HEADS, nq, MLA_HD_PAD, ab), F32), _sds((MLA_HEADS, t, MLA_HD_PAD), F32),
                   _sds((MLA_HEADS, t, MLA_VD), F32)],
        compiler_params=_cparams(("parallel", "arbitrary")),
    )(qh, kh, vh, dob, lse_t, dl_t)


VEC = pl.BlockSpec((1, D_MODEL), lambda i, j, k: (0, 0))


def _residual_epi(next_gain):
    if next_gain is None:
        return [], lambda acc, hv: (acc + hv,)

    def epi(acc, hv, g):
        h_new = acc + hv
        r = lax.rsqrt(jnp.mean(h_new * h_new, axis=-1, keepdims=True) + EPS)
        return h_new, h_new * r * g

    return [(next_gain, VEC)], epi


def _residual_outs(t, row, next_gain):
    outs = [(_sds((t, D_MODEL), F32), row)]
    return outs + ([(_sds((t, D_MODEL), BF16), row)] if next_gain is not None else [])


def _mlp_fwd(l, h, hn, w1g, fetch_w2, next_gain):
    t = h.shape[0]
    tm = _row_tile(t)
    nsh, _, wsh = w1g.shape

    def relu2(acc):
        r = jnp.maximum(acc, 0.0)
        return (r * r,)

    tu = _row_tile(t, 2 * ROW_TILE)
    tile = pl.BlockSpec((tu, wsh), lambda i, j, k: (i, j))
    (u,) = _mm(f"mlp_up{l}", (t // tu, nsh, 1),
               hn, pl.BlockSpec((tu, D_MODEL), lambda i, j, k: (i, 0)),
               w1g, pl.BlockSpec((None, D_MODEL, wsh), lambda i, j, k: (j, 0, 0)), (1, 0),
               [(_sds((t, D_FF), BF16), tile)], epi=relu2)
    w2g = fetch_w2((u,))
    row = pl.BlockSpec((tm, D_MODEL), lambda i, j, k: (i, 0))
    more, epi = _residual_epi(next_gain)
    h2, hn_next = _mm(f"mlp_down{l}", (t // tm, 1, nsh),
                      u, pl.BlockSpec((tm, wsh), lambda i, j, k: (i, k)),
                      w2g, pl.BlockSpec((None, wsh, D_MODEL), lambda i, j, k: (k, 0, 0)), (1, 0),
                      _residual_outs(t, row, next_gain), extras=[(h, row)] + more, epi=epi)
    return h2, hn_next, (h, hn, u, w1g, w2g)


def _norm_bwd_outs(t, tm):
    return [(_sds((t, D_MODEL), F32), pl.BlockSpec((tm, D_MODEL), lambda i, j, k: (i, 0))),
            (_sds((t // tm, 1, D_MODEL), F32), pl.BlockSpec((None, 1, D_MODEL), lambda i, j, k: (i, 0, 0)))]


def _norm_bwd_epi(acc, xv, res, g):
    dx, dgr = _rms_bwd_rows(acc, xv, g, D_MODEL)
    return res + dx, jnp.sum(dgr, axis=0, keepdims=True)


def _mlp_bwd(l, dh, saved, norm_g):
    h, hn, u, w1g, w2g = saved
    t = h.shape[0]
    tm = _row_tile(t)
    nsh, _, wsh = w1g.shape
    tu = _row_tile(t, 2 * ROW_TILE)
    tile = pl.BlockSpec((tu, wsh), lambda i, j, k: (i, j))
    (da,) = _mm(f"mlp_du{l}", (t // tu, nsh, 1),
                dh, pl.BlockSpec((tu, D_MODEL), lambda i, j, k: (i, 0)),
                w2g, pl.BlockSpec((None, wsh, D_MODEL), lambda i, j, k: (j, 0, 0)), (1, 1),
                [(_sds((t, D_FF), BF16), tile)], extras=[(u, tile)],
                epi=lambda acc, uv: (2.0 * jnp.sqrt(uv.astype(F32)) * acc,))
    tw = _row_tile(t, 512)
    (dw2,) = _mm(f"mlp_dw2{l}", (1, 1, t // tw),
                 u, pl.BlockSpec((tw, D_FF), lambda i, j, k: (k, 0)),
                 dh, pl.BlockSpec((tw, D_MODEL), lambda i, j, k: (k, 0)), (0, 0),
                 [(_sds((D_FF, D_MODEL), BF16), pl.BlockSpec((D_FF, D_MODEL), lambda i, j, k: (0, 0)))])
    dw2 = dw2.reshape(nsh, wsh, D_MODEL)
    (dw1,) = _mm(f"mlp_dw1{l}", (1, 1, t // tw),
                 hn, pl.BlockSpec((tw, D_MODEL), lambda i, j, k: (k, 0)),
                 da, pl.BlockSpec((tw, D_FF), lambda i, j, k: (k, 0)), (0, 0),
                 [(_sds((nsh, D_MODEL, wsh), BF16), pl.BlockSpec((nsh, D_MODEL, wsh), lambda i, j, k: (0, 0, 0)))],
                 split=wsh)
    row = pl.BlockSpec((tm, D_MODEL), lambda i, j, k: (i, 0))
    dh_in, dg = _mm(f"mlp_dhn{l}", (t // tm, 1, nsh),
                    da, pl.BlockSpec((tm, wsh), lambda i, j, k: (i, k)),
                    w1g, pl.BlockSpec((None, D_MODEL, wsh), lambda i, j, k: (k, 0, 0)), (1, 1),
                    _norm_bwd_outs(t, tm), extras=[(h, row), (dh, row), (norm_g, VEC)], epi=_norm_bwd_epi)
    return dh_in, jnp.sum(dg, axis=0), dw1, dw2


def _ple_fwd(l, h, hn, p, wg, wp, next_gain, target=None):
    t = h.shape[0]
    tm = _row_tile(t, 512)
    row = pl.BlockSpec((tm, D_MODEL), lambda i, j, k: (i, 0))
    full = lambda r: pl.BlockSpec((r, D_MODEL), lambda i, j, k: (0, 0))
    (e,) = _mm(f"ple_proj{l}", (t // tm, 1, 1),
               p, pl.BlockSpec((None, None, tm, PLE_DIM), lambda i, j, k: (l, 0, i, 0)),
               wp, full(PLE_DIM), (1, 0), [(_sds((t, D_MODEL), F32), row)])

    f32_row, bf_row = (_sds((t, D_MODEL), F32), row), (_sds((t, D_MODEL), BF16), row)
    if target is not None:
        def loss_epi(acc, hv, ev, tv):
            gt = _sigmoid(acc)
            err = hv + gt * ev - tv
            sq = jnp.sum(jnp.sum(err * err, axis=-1, keepdims=True), axis=0, keepdims=True)
            return err / D_MODEL, gt, jnp.broadcast_to(sq, (8, 128))

        dy, gate, sq = _mm(f"ple_gate{l}", (t // tm, 1, 1), hn, row, wg, full(D_MODEL), (1, 0),
                           [f32_row, f32_row, (_sds((t // tm, 8, 128), F32),
                                               pl.BlockSpec((None, 8, 128), lambda i, j, k: (i, 0, 0)))],
                           extras=[(h, row), (e, row), (target, row)], epi=loss_epi)
        return dy, jnp.sum(sq, axis=0), (h, hn, gate, e)

    def gate_epi(acc, hv, ev, *gain):
        gt = _sigmoid(acc)
        h_new = hv + gt * ev
        if not gain:
            return h_new, gt
        r = lax.rsqrt(jnp.mean(h_new * h_new, axis=-1, keepdims=True) + EPS)
        return h_new, gt, h_new * r * gain[0]

    res = _mm(f"ple_gate{l}", (t // tm, 1, 1), hn, row, wg, full(D_MODEL), (1, 0),
              [f32_row, f32_row] + ([bf_row] if next_gain is not None else []),
              extras=[(h, row), (e, row)] + ([(next_gain, VEC)] if next_gain is not None else []), epi=gate_epi)
    h_out, gate = res[0], res[1]
    return h_out, (res[2] if next_gain is not None else None), (h, hn, gate, e)


def _ple_bwd(l, dh, saved, p, norm_g, wg, deps=()):
    h, hn, gate, e = saved
    t = h.shape[0]
    tm = _row_tile(t)
    tk = _row_tile(t, 512)
    de, dz = _ple_gate_bwd(f"ple_gate_bwd{l}", dh, gate, e)
    full = lambda r: pl.BlockSpec((r, D_MODEL), lambda i, j, k: (0, 0))
    rowk = pl.BlockSpec((tk, D_MODEL), lambda i, j, k: (k, 0))
    (dwp,) = _mm(f"ple_dwp{l}", (1, 1, t // tk),
                 p, pl.BlockSpec((None, None, tk, PLE_DIM), lambda i, j, k: (l, 0, k, 0)),
                 de, rowk, (0, 0), [(_sds((PLE_DIM, D_MODEL), BF16), full(PLE_DIM))], deps=deps)
    (dwg,) = _mm(f"ple_dwg{l}", (1, 1, t // tk), hn, rowk, dz, rowk, (0, 0),
                 [(_sds((D_MODEL, D_MODEL), BF16), full(D_MODEL))])
    row = pl.BlockSpec((tm, D_MODEL), lambda i, j, k: (i, 0))
    dh_in, dg = _mm(f"ple_dhn{l}", (t // tm, 1, 1), dz, row, wg, full(D_MODEL), (1, 1),
                    _norm_bwd_outs(t, tm), extras=[(h, row), (dh, row), (norm_g, VEC)], epi=_norm_bwd_epi)
    return dh_in, jnp.sum(dg, axis=0), dwg, dwp


def _ret_layer_fwd(x, norm_g, wri, fetch_wro, gn, cos, sin, next_gain, deps=()):
    t = x.shape[0]
    tm = _row_tile(t)
    nsh, _, wsh = wri.shape
    hn = _rms_fwd("mix_norm0", x, norm_g)
    (proj,) = _mm("ret_in", (t // tm, nsh, 1),
                  hn, pl.BlockSpec((tm, D_MODEL), lambda i, j, k: (i, 0)),
                  wri, pl.BlockSpec((None, D_MODEL, wsh), lambda i, j, k: (j, 0, 0)), (1, 0),
                  [(_sds((t, RET_IN), F32), pl.BlockSpec((tm, wsh), lambda i, j, k: (i, j)))], deps=deps)
    gated, outp, states = _ret_fwd(proj, cos, sin, gn)
    wro = fetch_wro((gated,))
    row = pl.BlockSpec((tm, D_MODEL), lambda i, j, k: (i, 0))
    kt = 512
    more, epi = _residual_epi(next_gain)
    h1, hn_next = _mm("ret_out", (t // tm, 1, RET_V_W // kt),
                      gated, pl.BlockSpec((tm, kt), lambda i, j, k: (i, k)),
                      wro, pl.BlockSpec((kt, D_MODEL), lambda i, j, k: (k, 0)), (1, 0),
                      _residual_outs(t, row, next_gain), extras=[(x, row)] + more, epi=epi)
    return h1, hn_next, (x, hn, proj, gated, outp, states, wro)


def _ret_layer_bwd(dh, saved, norm_g, wri, gn, cos, sin, emit, deps=()):
    x, hn, proj, gated, outp, states, wro = saved
    t = x.shape[0]
    tm = _row_tile(t)
    tk = _row_tile(t, 512)
    nsh, _, wsh = wri.shape
    (dgated,) = _mm("ret_dgated", (t // tm, RET_V_W // D_MODEL, 1),
                    dh, pl.BlockSpec((tm, D_MODEL), lambda i, j, k: (i, 0)),
                    wro, pl.BlockSpec((D_MODEL, D_MODEL), lambda i, j, k: (j, 0)), (1, 1),
                    [(_sds((t, RET_V_W), F32), pl.BlockSpec((tm, D_MODEL), lambda i, j, k: (i, j)))], deps=deps)
    (dwro,) = _mm("ret_dwro", (1, 1, t // tk),
                  gated, pl.BlockSpec((tk, RET_V_W), lambda i, j, k: (k, 0)),
                  dh, pl.BlockSpec((tk, D_MODEL), lambda i, j, k: (k, 0)), (0, 0),
                  [(_sds((RET_V_W, D_MODEL), BF16), pl.BlockSpec((RET_V_W, D_MODEL), lambda i, j, k: (0, 0)))])
    dproj, dgn = _ret_bwd(proj, cos, sin, gn, outp, states, dgated)
    half = nsh // 2
    (dwri,) = _mm("ret_dwri", (2, 1, t // tk),
                  hn, pl.BlockSpec((tk, D_MODEL), lambda i, j, k: (k, 0)),
                  dproj, pl.BlockSpec((tk, half * wsh), lambda i, j, k: (k, i)), (0, 0),
                  [(_sds((nsh, D_MODEL, wsh), BF16), pl.BlockSpec((half, D_MODEL, wsh), lambda i, j, k: (i, 0, 0)))],
                  split=wsh)
    deps = emit(dwro, dwri)
    row = pl.BlockSpec((tm, D_MODEL), lambda i, j, k: (i, 0))
    dx, dg = _mm("ret_dhn", (t // tm, 1, nsh),
                 dproj, pl.BlockSpec((tm, wsh), lambda i, j, k: (i, k)),
                 wri, pl.BlockSpec((None, D_MODEL, wsh), lambda i, j, k: (k, 0, 0)), (1, 1),
                 _norm_bwd_outs(t, tm), extras=[(x, row), (dh, row), (norm_g, VEC)], epi=_norm_bwd_epi, deps=deps)
    return dx, jnp.sum(dg, axis=0), dgn.reshape(RET_HEADS, RET_DV)


def _mla_layer_fwd(h, hn, wmi, qa, kva, wuq, wukv, gq, gk, wmo, tabs, next_gain):
    t = h.shape[0]
    tm = _row_tile(t)
    row = pl.BlockSpec((tm, D_MODEL), lambda i, j, k: (i, 0))
    (proj2,) = _mm("mla_in", (t // tm, 1, 1), hn, row,
                   wmi, pl.BlockSpec((D_MODEL, MLA_IN_PAD), lambda i, j, k: (0, 0)), (1, 0),
                   [(_sds((t, MLA_IN_PAD), F32), pl.BlockSpec((tm, MLA_IN_PAD), lambda i, j, k: (i, 0)))])
    cq, ckv = _mla_mid(proj2, qa, kva)
    qh, kh, vh = _mla_prep(cq, ckv, wuq, wukv, proj2, gq, gk, tabs)
    o, lse = _attn_fwd(qh, kh, vh)
    more, epi = _residual_epi(next_gain)
    h_out, hn_next = _mm("mla_out", (t // tm, 1, 1), o, row,
                         wmo, pl.BlockSpec((D_MODEL, D_MODEL), lambda i, j, k: (0, 0)), (1, 0),
                         _residual_outs(t, row, next_gain), extras=[(h, row)] + more, epi=epi)
    return h_out, hn_next, (h, hn, proj2, cq, ckv, qh, kh, vh, o, lse)


def _mla_layer_bwd(dh, saved, norm_g, wmi, qa, kva, wuq, wukv, gq, gk, wmo, tabs, deps=()):
    h, hn, proj2, cq, ckv, qh, kh, vh, o, lse = saved
    t = h.shape[0]
    tm = _row_tile(t)
    tk = _row_tile(t, 512)
    row = pl.BlockSpec((tm, D_MODEL), lambda i, j, k: (i, 0))
    rowk = pl.BlockSpec((tk, D_MODEL), lambda i, j, k: (k, 0))
    sq = pl.BlockSpec((D_MODEL, D_MODEL), lambda i, j, k: (0, 0))
    do, dob = _mm("mla_do", (t // tm, 1, 1), dh, row, wmo, sq, (1, 1),
                  [(_sds((t, D_MODEL), F32), row), (_sds((t, D_MODEL), BF16), row)], epi=lambda acc: (acc, acc),
                  deps=deps)
    (dwmo,) = _mm("mla_dwo", (1, 1, t // tk), o, rowk, dh, rowk, (0, 0), [(_sds((D_MODEL, D_MODEL), BF16), sq)])
    delta = _attn_delta(do, o, lse.shape[-1])
    dqt, dkh, dvh = _attn_bwd(qh, kh, vh, dob, lse, delta)
    dq, dkv, dkr, dgq, dgk = _mla_prep_bwd(cq, ckv, wuq, wukv, proj2, gq, gk, tabs, dqt, dkh, dvh)

    wide = MLA_HEADS * MLA_HD_PAD
    widek = pl.BlockSpec((tk, wide), lambda i, j, k: (k, 0))
    (dwuq,) = _mm("mla_dwuq", (1, 1, t // tk),
                  cq, pl.BlockSpec((tk, MLA_Q_RANK), lambda i, j, k: (k, 0)), dq, widek, (0, 0),
                  [(_sds((MLA_HEADS, MLA_Q_RANK, MLA_HD_PAD), BF16),
                    pl.BlockSpec((MLA_HEADS, MLA_Q_RANK, MLA_HD_PAD), lambda i, j, k: (0, 0, 0)))], split=MLA_HD_PAD)
    (dwukv,) = _mm("mla_dwukv", (1, 1, t // tk),
                   ckv, pl.BlockSpec((tk, MLA_KV_RANK), lambda i, j, k: (k, 0)), dkv, widek, (0, 0),
                   [(_sds((MLA_HEADS, MLA_KV_RANK, MLA_HD_PAD), BF16),
                     pl.BlockSpec((MLA_HEADS, MLA_KV_RANK, MLA_HD_PAD), lambda i, j, k: (0, 0, 0)))],
                   split=MLA_HD_PAD)
    side_by_side = lambda wg: wg.transpose(1, 0, 2).reshape(wg.shape[1], wide)
    widei = pl.BlockSpec((tm, wide), lambda i, j, k: (i, 0))
    (dcq,) = _mm("mla_dcq", (t // tm, 1, 1), dq, widei,
                 side_by_side(wuq), pl.BlockSpec((MLA_Q_RANK, wide), lambda i, j, k: (0, 0)), (1, 1),
                 [(_sds((t, MLA_Q_RANK), F32), pl.BlockSpec((tm, MLA_Q_RANK), lambda i, j, k: (i, 0)))])
    (dckv,) = _mm("mla_dckv", (t // tm, 1, 1), dkv, widei,
                  side_by_side(wukv), pl.BlockSpec((MLA_KV_RANK, wide), lambda i, j, k: (0, 0)), (1, 1),
                  [(_sds((t, MLA_KV_RANK), F32), pl.BlockSpec((tm, MLA_KV_RANK), lambda i, j, k: (i, 0)))])
    dproj2, dqa, dkva = _mla_mid_bwd(proj2, qa, kva, dcq, dckv, dkr)
    win = pl.BlockSpec((D_MODEL, MLA_IN_PAD), lambda i, j, k: (0, 0))
    (dwmi,) = _mm("mla_dwin", (1, 1, t // tk), hn, rowk,
                  dproj2, pl.BlockSpec((tk, MLA_IN_PAD), lambda i, j, k: (k, 0)), (0, 0),
                  [(_sds((D_MODEL, MLA_IN_PAD), BF16), win)])
    dh_in, dg = _mm("mla_dhn", (t // tm, 1, 1),
                    dproj2, pl.BlockSpec((tm, MLA_IN_PAD), lambda i, j, k: (i, 0)), wmi, win, (1, 1),
                    _norm_bwd_outs(t, tm), extras=[(h, row), (dh, row), (norm_g, VEC)], epi=_norm_bwd_epi)
    return dh_in, dict(mix=jnp.sum(dg, axis=0), wmi=dwmi, qa=dqa, kva=dkva, wuq=dwuq, wukv=dwukv, gq=dgq, gk=dgk,
                       wmo=dwmo)


def _local_step(x, p, target, w, fetch, emit=lambda group: ()):
    t = x.shape[0]
    inv = 1.0 / (ROPE_THETA ** (jnp.arange(0, RET_DK, 2, dtype=F32) / RET_DK))
    ang = jnp.arange(t, dtype=F32)[:, None] * inv[None, :]
    cos_r, sin_r = jnp.cos(ang), jnp.sin(ang)
    tabs = _mla_tables(t)
    row = lambda a, i: a[i:i + 1]

    h1, hn1, s_ret = _ret_layer_fwd(x, row(w['mix_norm'], 0), w['ret_w_in'],
                                    lambda after: fetch('ret_out', after)['ret_w_out'], w['ret_gn'], cos_r, sin_r,
                                    row(w['mlp_norm'], 0), deps=w['deps'])
    h2, hn2, s_mlp0 = _mlp_fwd(0, h1, hn1, fetch('mlp_w1_0', (h1,))['mlp_w1'],
                               lambda after: fetch('mlp_w2_0', after)['mlp_w2'], row(w['ple_norm'], 0))
    w0 = fetch('ple_0', (h2,))
    h3, hn3, s_ple0 = _ple_fwd(0, h2, hn2, p, w0['ple_gate_w'], w0['ple_proj_w'], row(w['mix_norm'], 1))
    wm = fetch('mla', (h3,))
    mla_w = (wm['mla_w_in'], w['mla_q_a_norm'], w['mla_kv_a_norm'], wm['mla_w_uq'], wm['mla_w_ukv'],
             w['mla_q_norm'], w['mla_k_norm'], wm['mla_w_out'], tabs)
    h4, hn4, s_mla = _mla_layer_fwd(h3, hn3, *mla_w, row(w['mlp_norm'], 1))
    w1 = fetch('layer_1', (h4,))
    h5, hn5, s_mlp1 = _mlp_fwd(1, h4, hn4, w1['mlp_w1'], lambda after: w1['mlp_w2'], row(w['ple_norm'], 1))
    dy, sq_err, s_ple1 = _ple_fwd(1, h5, hn5, p, w1['ple_gate_w'], w1['ple_proj_w'], None, target)

    n = N_DEV
    colsh = lambda a: a.reshape(a.shape[0], n, a.shape[1] // n).transpose(1, 0, 2)
    rowsh = lambda a: a.reshape(n, a.shape[0] // n, a.shape[1])
    big = {}

    def emit_group(group):
        big.update(group)
        return emit(group)

    dh5, dg_ple1, dwg1, dwp1 = _ple_bwd(1, dy, s_ple1, p, row(w['ple_norm'], 1), w1['ple_gate_w'])
    dh4, dg_mlp1, dw1_1, dw2_1 = _mlp_bwd(1, dh5, s_mlp1, row(w['mlp_norm'], 1))
    deps = emit_group({('ple_gate_w', 1): rowsh(dwg1), ('ple_proj_w', 1): colsh(dwp1),
                       ('mlp_w2', 1): dw2_1, ('mlp_w1', 1): dw1_1})
    dh3, gm = _mla_layer_bwd(dh4, s_mla, row(w['mix_norm'], 1), *mla_w, deps=deps)
    deps = emit_group({('mla_w_out', 0): rowsh(gm['wmo']), ('mla_w_uq', 0): gm['wuq'][:, :, :MLA_QKD],
                       ('mla_w_ukv', 0): gm['wukv'], ('mla_w_in', 0): rowsh(gm['wmi'][:, :MLA_IN])})
    dh2, dg_ple0, dwg0, dwp0 = _ple_bwd(0, dh3, s_ple0, p, row(w['ple_norm'], 0), w0['ple_gate_w'], deps=deps)
    dh1, dg_mlp0, dw1_0, dw2_0 = _mlp_bwd(0, dh2, s_mlp0, row(w['mlp_norm'], 0))
    deps = emit_group({('ple_gate_w', 0): rowsh(dwg0), ('ple_proj_w', 0): colsh(dwp0),
                       ('mlp_w2', 0): dw2_0, ('mlp_w1', 0): dw1_0})
    dx, dg_mix0, dgn = _ret_layer_bwd(
        dh1, s_ret, row(w['mix_norm'], 0), w['ret_w_in'], w['ret_gn'], cos_r, sin_r,
        lambda dwro, dwri: emit_group({('ret_w_out', 0): rowsh(dwro), ('ret_w_in', 0): dwri}), deps=deps)

    small = dict(
        mix_norm=[dg_mix0, gm['mix']], mlp_norm=[dg_mlp0, dg_mlp1], ple_norm=[dg_ple0, dg_ple1],
        ret_gn=dgn, mla_q_a_norm=gm['qa'], mla_kv_a_norm=gm['kva'], mla_q_norm=gm['gq'], mla_k_norm=gm['gk'],
    )
    return sq_err, dx, big, small


def _my_place():
    x, y, c = lax.axis_index("x"), lax.axis_index("y"), lax.axis_index("c")
    return x, y, c


def _flat(px, py, pc):
    return 4 * px + 2 * py + pc


def _peer(x, y, c, r):
    return (1 - x if r & 4 else x, 1 - y if r & 2 else y, 1 - c if r & 1 else c)


def _all_gather(arrays):
    n = len(arrays)

    def body(*refs):
        ins, outs = refs[:n], refs[n:2 * n]
        send_sems, recv_sems, local_sems = refs[2 * n:]
        x, y, c = _my_place()
        me, sibling = (x, y, c), (x, y, 1 - c)
        chips = [(1 - x, y), (x, 1 - y), (1 - x, 1 - y)]

        def copy(a, k, block, to, src=None):
            slot = outs[a].at[_flat(*block)]
            return pltpu.make_async_remote_copy(
                src_ref=slot if src is None else src, dst_ref=slot,
                send_sem=send_sems.at[a, k], recv_sem=recv_sems.at[a, k], device_id=to, device_id_type=MESH)

        mine = [pltpu.make_async_copy(ins[a], outs[a].at[_flat(*me)], local_sems.at[a]) for a in range(n)]
        for cp in mine:
            cp.start()
        first = []
        for a in range(n):
            first.append(copy(a, 0, me, sibling, src=ins[a]))
            first += [copy(a, 1 + j, me, (*chip, c), src=ins[a]) for j, chip in enumerate(chips)]
        for cp in first:
            cp.start()
        passed = []
        for a in range(n):
            for j, chip in enumerate(chips):
                copy(a, 1 + j, (*chip, c), me).wait_recv()
                passed.append(copy(a, 4 + j, (*chip, c), sibling))
                passed[-1].start()
        for a in range(n):
            copy(a, 0, sibling, me).wait_recv()
            for j, chip in enumerate(chips):
                copy(a, 4 + j, (*chip, 1 - c), me).wait_recv()
        for cp in first + passed:
            cp.wait_send()
        for cp in mine:
            cp.wait()

    return pl.pallas_call(
        body, name="all_gather_weights",
        in_specs=[ANY] * n, out_specs=[ANY] * n,
        out_shape=[_sds((N_DEV,) + a.shape, a.dtype) for a in arrays],
        scratch_shapes=[pltpu.SemaphoreType.DMA((n, 7)), pltpu.SemaphoreType.DMA((n, 7)),
                        pltpu.SemaphoreType.DMA((n,))],
    )(*arrays)


HBM = pl.BlockSpec(memory_space=pltpu.HBM)
SEMS = pl.BlockSpec(memory_space=pltpu.SEMAPHORE)
SIDE_EFFECT = pltpu.SideEffectType.DATAFLOW_SIDE_EFFECTING


def _rs_copies(x, y, c, srcs, lands, send_sems, recv_sems):
    copies = []
    for a in range(len(srcs)):
        for r in range(1, N_DEV):
            peer = _peer(x, y, c, r)
            k = a * (N_DEV - 1) + r - 1
            copies.append(pltpu.make_async_remote_copy(
                src_ref=srcs[a].at[_flat(*peer)], dst_ref=lands[a].at[r - 1],
                send_sem=send_sems.at[k], recv_sem=recv_sems.at[k], device_id=peer, device_id_type=MESH))
    return copies


def _rs_start(name, arrays):
    n = len(arrays)
    hbm = lambda a: pltpu.with_memory_space_constraint(a, pltpu.HBM)
    lands = [hbm(lax.empty((N_DEV - 1,) + a.shape[1:], a.dtype)) for a in arrays]

    def body(*refs):
        srcs, lnd = refs[:n], refs[n:2 * n]
        send_sems, recv_sems = refs[2 * n], refs[2 * n + 1]
        token = refs[-1]
        for cp in _rs_copies(*_my_place(), srcs, lnd, send_sems, recv_sems):
            cp.start()
        token[...] = jnp.zeros_like(token)

    outs = pl.pallas_call(
        body, name=name,
        in_specs=[HBM] * (2 * n),
        out_specs=[SEMS, SEMS] + [HBM] * (2 * n) + [pl.BlockSpec(memory_space=pltpu.VMEM)],
        out_shape=[pltpu.SemaphoreType.DMA((n * (N_DEV - 1),)), pltpu.SemaphoreType.DMA((n * (N_DEV - 1),))]
        + [pltpu.HBM(a.shape, a.dtype) for a in arrays] + [pltpu.HBM(l.shape, l.dtype) for l in lands]
        + [_sds((8, 128), F32)],
        input_output_aliases={i: 2 + i for i in range(2 * n)},
        compiler_params=pltpu.CompilerParams(has_side_effects=SIDE_EFFECT),
    )(*[hbm(a) for a in arrays], *lands)
    return outs[0], outs[1], outs[2:2 + n], outs[2 + n:2 + 2 * n], outs[-1]


def _rs_wait(name, send_sems, recv_sems, srcs, lands, after):
    n = len(srcs)

    def body(*refs):
        src_refs, lnd = refs[:n], refs[n:2 * n]
        send, recv = refs[2 * n], refs[2 * n + 1]
        for cp in _rs_copies(*_my_place(), src_refs, lnd, send, recv):
            cp.wait_send()
            cp.wait_recv()

    outs = pl.pallas_call(
        body, name=name,
        in_specs=[HBM] * (2 * n) + [SEMS, SEMS] + [ANY] * len(after),
        out_specs=[HBM] * (2 * n),
        out_shape=[pltpu.HBM(a.shape, a.dtype) for a in list(srcs) + list(lands)],
        input_output_aliases={i: i for i in range(2 * n)},
        compiler_params=pltpu.CompilerParams(has_side_effects=SIDE_EFFECT),
    )(*srcs, *lands, send_sems, recv_sems, *after)
    return outs[:n], outs[n:]


SMALL_PACK_ROWS = 16


def _all_reduce_small(rows, deps=()):
    n = len(rows)

    def body(*refs):
        ins = refs[:n]
        out_ref, mine, buf, send_sems, recv_sems = refs[n + len(deps):]
        x, y, c = _my_place()
        mine[...] = jnp.zeros_like(mine)
        for (r0, a), ref in zip(rows, ins):
            mine[r0:r0 + a.shape[0], 0:a.shape[1]] = ref[...]
        buf[_flat(x, y, c)] = mine[...]
        copies = []
        for r in range(1, N_DEV):
            peer = _peer(x, y, c, r)
            send = pltpu.make_async_remote_copy(
                src_ref=mine, dst_ref=buf.at[_flat(x, y, c)],
                send_sem=send_sems.at[r - 1], recv_sem=recv_sems.at[r - 1], device_id=peer, device_id_type=MESH)
            send.start()
            recv = pltpu.make_async_remote_copy(
                src_ref=mine, dst_ref=buf.at[_flat(*peer)],
                send_sem=send_sems.at[r - 1], recv_sem=recv_sems.at[r - 1], device_id=peer, device_id_type=MESH)
            copies.append((send, recv))
        for send, recv in copies:
            send.wait_send()
            recv.wait_recv()
        acc = buf[0]
        for s in range(1, N_DEV):
            acc = acc + buf[s]
        out_ref[...] = acc

    vm = pl.BlockSpec(memory_space=pltpu.VMEM)
    shape = (SMALL_PACK_ROWS, D_MODEL)
    return pl.pallas_call(
        body, name="all_reduce_small", in_specs=[vm] * n + [ANY] * len(deps), out_specs=vm,
        out_shape=_sds(shape, F32),
        scratch_shapes=[pltpu.VMEM(shape, F32), pltpu.VMEM((N_DEV,) + shape, F32),
                        pltpu.SemaphoreType.DMA((7,)), pltpu.SemaphoreType.DMA((7,))],
    )(*[a for _, a in rows], *deps)


def _adamw_math(w, g, m, v):
    m = ADAM_B1 * m + (1.0 - ADAM_B1) * g
    v = ADAM_B2 * v + (1.0 - ADAM_B2) * (g * g)
    m_hat = m / (1.0 - ADAM_B1 ** ADAM_STEP)
    v_hat = v / (1.0 - ADAM_B2 ** ADAM_STEP)
    delta = -ADAM_LR * (m_hat / (jnp.sqrt(v_hat) + ADAM_EPS) + ADAM_WD * w)
    return delta, m, v


def _adamw_big(name, w, m, v, srcs, lands, me):
    nl, rows, cols = w.shape
    tr = next(cand for cand in (256, 128, 64, 32, 16, 8) if rows % cand == 0)

    def body(me_ref, w_ref, m_ref, v_ref, *rest):
        src_refs, land_refs = rest[:nl], rest[nl:2 * nl]
        g_ref, d_ref, mo_ref, vo_ref = rest[2 * nl:]
        for layer in range(nl):
            @pl.when(pl.program_id(0) == layer)
            def _():
                g = src_refs[layer][...].astype(F32)
                for s in range(N_DEV - 1):
                    g = g + land_refs[layer][s].astype(F32)
                delta, mn, vn = _adamw_math(w_ref[...], g, m_ref[...], v_ref[...])
                g_ref[...] = g
                d_ref[...] = delta
                mo_ref[...] = mn
                vo_ref[...] = vn

    blk = pl.BlockSpec((None, tr, cols), lambda l, i, me_ref: (l, i, 0))
    own = pl.BlockSpec((None, tr, cols), lambda l, i, me_ref: (me_ref[0], i, 0))
    peers = pl.BlockSpec((N_DEV - 1, tr, cols), lambda l, i, me_ref: (0, i, 0))
    return pl.pallas_call(
        body, name=name,
        grid_spec=pltpu.PrefetchScalarGridSpec(
            num_scalar_prefetch=1, grid=(nl, rows // tr),
            in_specs=[blk, blk, blk] + [own] * nl + [peers] * nl, out_specs=[blk] * 4),
        out_shape=[_sds((nl, rows, cols), F32)] * 4,
        compiler_params=_cparams(("arbitrary", "arbitrary")),
    )(me, w, m, v, *srcs, *lands)


def _adamw_small(ws, gs, ms, vs):
    n = len(ws)

    def body(*refs):
        w_refs, g_refs, m_refs, v_refs = (refs[i * n:(i + 1) * n] for i in range(4))
        d_out, m_out, v_out = (refs[(4 + i) * n:(5 + i) * n] for i in range(3))
        for i in range(n):
            delta, mn, vn = _adamw_math(w_refs[i][...], g_refs[i][...], m_refs[i][...], v_refs[i][...])
            d_out[i][...] = delta
            m_out[i][...] = mn
            v_out[i][...] = vn

    vm = pl.BlockSpec(memory_space=pltpu.VMEM)
    outs = pl.pallas_call(
        body, name="adamw_small", in_specs=[vm] * (4 * n), out_specs=[vm] * (3 * n),
        out_shape=[_sds(a.shape, F32) for a in ws] * 3,
    )(*ws, *gs, *ms, *vs)
    return outs[:n], outs[n:2 * n], outs[2 * n:]


SMALL_ROWS = 16


def _pad_to(a, rows, cols):
    return jnp.pad(a, ((0, rows - a.shape[0]), (0, cols - a.shape[1])))


def _place_own(blocks):
    me = _flat(*_my_place())
    return [lax.dynamic_update_slice(lax.empty((N_DEV,) + b.shape, b.dtype), b[None], (me,) + (0,) * b.ndim)
            for b in blocks]


def _ag_copies(x, y, c, blocks, bufs, send_sems, recv_sems):
    sends, recvs = [], []
    for a in range(len(blocks)):
        for r in range(1, N_DEV):
            peer = _peer(x, y, c, r)
            k = a * (N_DEV - 1) + r - 1
            make = lambda place: pltpu.make_async_remote_copy(
                src_ref=blocks[a], dst_ref=bufs[a].at[_flat(*place)],
                send_sem=send_sems.at[k], recv_sem=recv_sems.at[k], device_id=peer, device_id_type=MESH)
            sends.append(make((x, y, c)))
            recvs.append(make(peer))
    return sends, recvs


def _ag_start(groups, after):
    flat = [pair for g in groups for pair in g]
    n, ng = len(flat), len(groups)
    hbm = lambda a: pltpu.with_memory_space_constraint(a, pltpu.HBM)

    def body(*refs):
        blocks, bufs = refs[:n], refs[n:2 * n]
        sems = refs[2 * n + len(after):2 * n + len(after) + 2 * ng]
        x, y, c = _my_place()
        at = 0
        for gi, g in enumerate(groups):
            sends, _ = _ag_copies(x, y, c, blocks[at:at + len(g)], bufs[at:at + len(g)], sems[2 * gi], sems[2 * gi + 1])
            for cp in sends:
                cp.start()
            at += len(g)
        refs[-1][...] = jnp.zeros_like(refs[-1])

    sem_shapes = [pltpu.SemaphoreType.DMA((len(g) * (N_DEV - 1),)) for g in groups for _ in range(2)]
    outs = pl.pallas_call(
        body, name="gather_start",
        in_specs=[HBM] * (2 * n) + [ANY] * len(after),
        out_specs=[SEMS] * (2 * ng) + [HBM] * (2 * n) + [pl.BlockSpec(memory_space=pltpu.VMEM)],
        out_shape=sem_shapes + [pltpu.HBM(b.shape, b.dtype) for b, _ in flat]
        + [pltpu.HBM(u.shape, u.dtype) for _, u in flat] + [_sds((8, 128), F32)],
        input_output_aliases={i: 2 * ng + i for i in range(2 * n)},
        compiler_params=pltpu.CompilerParams(has_side_effects=SIDE_EFFECT),
    )(*[hbm(b) for b, _ in flat], *[hbm(u) for _, u in flat], *after)
    blocks_thru, bufs_thru = outs[2 * ng:2 * ng + n], outs[2 * ng + n:2 * ng + 2 * n]
    started, at = [], 0
    for gi, g in enumerate(groups):
        started.append((outs[2 * gi], outs[2 * gi + 1], blocks_thru[at:at + len(g)], bufs_thru[at:at + len(g)]))
        at += len(g)
    return started, outs[-1]


def _ag_wait(name, send_sems, recv_sems, blocks, bufs, after):
    n = len(blocks)

    def body(*refs):
        sends, recvs = _ag_copies(*_my_place(), refs[:n], refs[n:2 * n], refs[2 * n], refs[2 * n + 1])
        for s, r in zip(sends, recvs):
            s.wait_send()
            r.wait_recv()

    outs = pl.pallas_call(
        body, name=name,
        in_specs=[HBM] * (2 * n) + [SEMS, SEMS] + [ANY] * len(after),
        out_specs=[HBM] * (2 * n),
        out_shape=[pltpu.HBM(a.shape, a.dtype) for a in list(blocks) + list(bufs)],
        input_output_aliases={i: i for i in range(2 * n)},
        compiler_params=pltpu.CompilerParams(has_side_effects=SIDE_EFFECT),
    )(*blocks, *bufs, send_sems, recv_sems, *after)
    return outs[n:]


def _prepare_weights(p):
    n = N_DEV
    bf = lambda a: a.astype(BF16)
    gn_pack = jnp.concatenate([
        _pad_to(p['ret_gn'][0], RET_HEADS, 128), _pad_to(p['mla_q_a_norm'], 1, 128),
        _pad_to(p['mla_kv_a_norm'], 1, 128), jnp.zeros((2, 128), F32)], axis=0)
    ple = lambda l: [bf(p['ple_gate_w'][l]), bf(p['ple_proj_w'][l])]
    names = ('ret_out', 'mlp_w1_0', 'mlp_w2_0', 'ple_0', 'mla', 'layer_1')
    later = [[bf(p['ret_w_out'][0])], [bf(p['mlp_w1'][0])], [bf(p['mlp_w2'][0])], ple(0),
             [bf(p['mla_w_in'][0]), bf(p['mla_w_uq'][0]), bf(p['mla_w_ukv'][0]), bf(p['mla_w_out'][0])],
             [bf(p['mlp_w1'][1]), bf(p['mlp_w2'][1])] + ple(1)]
    bufs = _place_own([b for g in later for b in g])
    pack, wri = _all_gather([gn_pack, bf(p['ret_w_in'][0])])
    groups, at = [], 0
    for g in later:
        groups.append(list(zip(g, bufs[at:at + len(g)])))
        at += len(g)
    started, token = _ag_start(groups, (wri,))

    w = {k: p[k] for k in ('mix_norm', 'mlp_norm', 'ple_norm')}
    w['ret_gn'] = pack[:, :RET_HEADS, :RET_DV // n].transpose(1, 0, 2).reshape(RET_HEADS, RET_DV)
    w['mla_q_a_norm'] = pack[:, RET_HEADS, :MLA_Q_RANK // n].reshape(1, MLA_Q_RANK)
    w['mla_kv_a_norm'] = pack[:, RET_HEADS + 1, :MLA_KV_RANK // n].reshape(1, MLA_KV_RANK)
    w['ret_w_in'] = wri
    w['mla_q_norm'] = _pad_to(p['mla_q_norm'], 1, MLA_HD_PAD)
    w['mla_k_norm'] = _pad_to(p['mla_k_norm'], 1, MLA_HD_PAD)
    w['deps'] = (token,)

    def fetch(name, after):
        got = list(_ag_wait("gather_wait_" + name, *started[names.index(name)], after))
        if name == 'ret_out':
            return dict(ret_w_out=got[0].reshape(RET_V_W, D_MODEL))
        if name == 'mla':
            wmi, wuq, wukv, wmo = got
            return dict(mla_w_in=jnp.pad(wmi.reshape(D_MODEL, MLA_IN), ((0, 0), (0, MLA_IN_PAD - MLA_IN))),
                        mla_w_uq=jnp.pad(wuq, ((0, 0), (0, 0), (0, MLA_HD_PAD - MLA_QKD))),
                        mla_w_ukv=wukv, mla_w_out=wmo.reshape(D_MODEL, D_MODEL))
        out = {}
        if name in ('mlp_w1_0', 'layer_1'):
            out['mlp_w1'] = got.pop(0)
        if name in ('mlp_w2_0', 'layer_1'):
            out['mlp_w2'] = got.pop(0)
        if name in ('ple_0', 'layer_1'):
            out['ple_gate_w'] = got[0].reshape(D_MODEL, D_MODEL)
            out['ple_proj_w'] = got[1].transpose(1, 0, 2).reshape(PLE_DIM, D_MODEL)
        return out

    return w, fetch


def _small_grads(small, after):
    rows = [(0, small['mix_norm'][0]), (1, small['mix_norm'][1]), (2, small['mlp_norm'][0]),
            (3, small['mlp_norm'][1]), (4, small['ple_norm'][0]), (5, small['ple_norm'][1]),
            (6, small['ret_gn']), (10, small['mla_q_a_norm']), (11, small['mla_kv_a_norm']),
            (12, small['mla_q_norm']), (13, small['mla_k_norm'])]
    gs = _all_reduce_small(rows, after)
    me = _flat(*_my_place())
    n = N_DEV
    return dict(
        mix_norm=gs[0:2], mlp_norm=gs[2:4], ple_norm=gs[4:6],
        ret_gn=lax.dynamic_slice(gs, (6, me * (RET_DV // n)), (RET_HEADS, RET_DV // n)),
        mla_q_a_norm=lax.dynamic_slice(gs, (10, me * (MLA_Q_RANK // n)), (1, MLA_Q_RANK // n)),
        mla_kv_a_norm=lax.dynamic_slice(gs, (11, me * (MLA_KV_RANK // n)), (1, MLA_KV_RANK // n)),
        mla_q_norm=gs[12:13, :MLA_QKD], mla_k_norm=gs[13:14, :MLA_QKD])


def kernel(x, p, mix_norm, ret_w_in, ret_gn, ret_w_out, mla_w_in, mla_q_a_norm, mla_kv_a_norm, mla_w_uq, mla_w_ukv, mla_q_norm, mla_k_norm, mla_w_out, mlp_norm, mlp_w1, mlp_w2, ple_norm, ple_gate_w, ple_proj_w, loss_target, m_mix_norm, m_ret_w_in, m_ret_gn, m_ret_w_out, m_mla_w_in, m_mla_q_a_norm, m_mla_kv_a_norm, m_mla_w_uq, m_mla_w_ukv, m_mla_q_norm, m_mla_k_norm, m_mla_w_out, m_mlp_norm, m_mlp_w1, m_mlp_w2, m_ple_norm, m_ple_gate_w, m_ple_proj_w, v_mix_norm, v_ret_w_in, v_ret_gn, v_ret_w_out, v_mla_w_in, v_mla_q_a_norm, v_mla_kv_a_norm, v_mla_w_uq, v_mla_w_ukv, v_mla_q_norm, v_mla_k_norm, v_mla_w_out, v_mlp_norm, v_mlp_w1, v_mlp_w2, v_ple_norm, v_ple_gate_w, v_ple_proj_w):
    given = dict(locals())
    params = {n: given[n] for n in WEIGHTS}
    w, fetch = _prepare_weights(params)

    started = []

    def emit(group):
        keys = list(group)
        send, recv, srcs, lands, token = _rs_start(f"rs_start{len(started)}", [group[k] for k in keys])
        started.append((keys, send, recv, srcs, lands))
        return (token,)

    sq_err, grad_x, _, small = _local_step(x[0], p, loss_target[0], w, fetch, emit)
    loss = lax.psum(0.5 / D_MODEL * sq_err[0, 0], ("x", "y", "c"))

    grads, deltas, new_m, new_v = {}, {}, {}, {}

    def small_updates(after):
        sg = _small_grads(small, after)
        two_d = lambda a: a.reshape(-1, a.shape[-1])
        d_s, m_s, v_s = _adamw_small(
            [two_d(params[n]) for n in SMALL], [sg[n] for n in SMALL],
            [two_d(given["m_" + n]) for n in SMALL], [two_d(given["v_" + n]) for n in SMALL])
        for i, n in enumerate(SMALL):
            shape = params[n].shape
            grads[n], deltas[n], new_m[n], new_v[n] = (a.reshape(shape) for a in (sg[n], d_s[i], m_s[i], v_s[i]))
        return (d_s[0],)

    me = _flat(*_my_place()).astype(jnp.int32).reshape(1)
    after = (grad_x,)
    src_of, land_of = {}, {}
    for gi, (keys, send, recv, srcs, lands) in enumerate(started):
        if gi == len(started) - 1:
            after = small_updates(after)
        srcs, lands = _rs_wait(f"rs_wait{gi}", send, recv, srcs, lands, after)
        for k, s, l in zip(keys, srcs, lands):
            src_of[k], land_of[k] = s, l
        done = [n for n in BIG if n not in grads and all((n, l) in src_of for l in range(params[n].shape[0]))]
        for n in done:
            layers = range(params[n].shape[0])
            grads[n], deltas[n], new_m[n], new_v[n] = _adamw_big(
                "adamw_" + n, params[n], given["m_" + n], given["v_" + n],
                [src_of[(n, l)] for l in layers], [land_of[(n, l)] for l in layers], me)
        if done:
            after = tuple(deltas[n] for n in done)

    return (loss, grad_x[None], *[grads[n] for n in WEIGHTS], *[deltas[n] for n in WEIGHTS],
            *[new_m[n] for n in WEIGHTS], *[new_v[n] for n in WEIGHTS])
```

```python
import functools
import math

import jax
import jax.numpy as jnp
from jax import lax
from jax.experimental import pallas as pl
from jax.experimental.pallas import tpu as pltpu

F32 = jnp.float32
BF16 = jnp.bfloat16
MESH = pl.DeviceIdType.MESH
ANY = pl.BlockSpec(memory_space=pl.ANY)

N_DEV = 8
D_MODEL = 1024
CHUNK = 64
EPS = 1e-6
ROPE_THETA = 10000.0
RET_HEADS = 4
RET_DK = 256
RET_DV = 512
RET_QK_W = RET_HEADS * RET_DK
RET_V_W = RET_HEADS * RET_DV
RET_IN = 2 * RET_QK_W + 2 * RET_V_W
MLA_HEADS = 8
MLA_NOPE = 128
MLA_ROPE = 64
MLA_QKD = MLA_NOPE + MLA_ROPE
MLA_VD = 128
MLA_Q_RANK = 384
MLA_KV_RANK = 256
MLA_IN = MLA_Q_RANK + MLA_KV_RANK + MLA_ROPE
MLA_IN_PAD = 768
MLA_HD_PAD = 256
D_FF = 4096
PLE_DIM = 256
ATT_SCALE = MLA_QKD ** -0.5
LOG2E = 1.4426950408889634
ATT_EXP2 = ATT_SCALE * LOG2E

ADAM_LR = 0.001
ADAM_B1 = 0.9
ADAM_B2 = 0.999
ADAM_EPS = 1e-08
ADAM_WD = 0.01
ADAM_STEP = 10

VMEM_LIMIT = 52 * 1024 * 1024
ROW_TILE = 1024
RET_ROWS = 256
ATT_BLOCK = 256
ATT_QROWS = 1024
ATT_KROWS = 1024
ATT_HEADS = 2

WEIGHTS = ['mix_norm', 'ret_w_in', 'ret_gn', 'ret_w_out', 'mla_w_in', 'mla_q_a_norm', 'mla_kv_a_norm',
           'mla_w_uq', 'mla_w_ukv', 'mla_q_norm', 'mla_k_norm', 'mla_w_out', 'mlp_norm', 'mlp_w1', 'mlp_w2',
           'ple_norm', 'ple_gate_w', 'ple_proj_w']
BIG = ['ret_w_in', 'ret_w_out', 'mla_w_in', 'mla_w_uq', 'mla_w_ukv', 'mla_w_out', 'mlp_w1', 'mlp_w2',
       'ple_gate_w', 'ple_proj_w']
SMALL = [w for w in WEIGHTS if w not in BIG]


def _cparams(sem=None):
    return pltpu.CompilerParams(dimension_semantics=sem, vmem_limit_bytes=VMEM_LIMIT)


def _dot(a, b, ca, cb):
    return lax.dot_general(a, b, (((ca,), (cb,)), ((), ())), preferred_element_type=F32)


def _bf(v):
    return v if v.dtype == BF16 else v.astype(BF16)


def _sigmoid(z):
    return 1.0 / (1.0 + jnp.exp(-z))


def _mm(name, grid, a, a_spec, b, b_spec, contract, outs, extras=(), epi=None, deps=(), split=None):
    nk = grid[2]
    n_ex, n_out, n_dep = len(extras), len(outs), len(deps)
    acc_shape = tuple(d for d in outs[0][1].block_shape if d is not None)
    if split is not None:
        acc_shape = (acc_shape[1], acc_shape[0] * split)

    def body(*refs):
        a_ref, b_ref = refs[:2]
        ex_refs = refs[2:2 + n_ex]
        out_refs = refs[2 + n_ex + n_dep:2 + n_ex + n_dep + n_out]

        def product():
            return _dot(_bf(a_ref[...]), _bf(b_ref[...]), contract[0], contract[1])

        def finish(acc):
            if split is not None:
                for j in range(acc_shape[1] // split):
                    out_refs[0][j] = acc[:, j * split:(j + 1) * split].astype(out_refs[0].dtype)
                return
            acc = acc[...]
            res = epi(acc, *[r[...] for r in ex_refs]) if epi is not None else (acc,)
            for o, r in zip(out_refs, res):
                o[...] = r.astype(o.dtype)

        if nk == 1:
            finish(product())
        else:
            acc_ref = refs[-1]
            k = pl.program_id(2)

            @pl.when(k == 0)
            def _():
                acc_ref[...] = jnp.zeros_like(acc_ref)

            acc_ref[...] += product()

            @pl.when(k == nk - 1)
            def _():
                finish(acc_ref)

    return pl.pallas_call(
        body, name=name, grid=grid,
        in_specs=[a_spec, b_spec] + [s for _, s in extras] + [ANY] * n_dep,
        out_specs=[s for _, s in outs],
        out_shape=[s for s, _ in outs],
        scratch_shapes=[pltpu.VMEM(acc_shape, F32)] if nk > 1 else [],
        compiler_params=_cparams(("parallel", "parallel", "arbitrary")),
    )(a, b, *[x for x, _ in extras], *deps)


def _mm_rows(name, tm, a, w, mode, outs, extras=(), epi=None, deps=()):
    n_sh, rows, cols = w.shape
    n_ex, n_out, n_dep = len(extras), len(outs), len(deps)
    by_cols = mode in ('nn_cols', 'nt_rows')
    width = cols if mode == 'nn_cols' else rows

    def body(*refs):
        a_ref, w_ref = refs[:2]
        ex_refs = refs[2:2 + n_ex]
        out_refs = refs[2 + n_ex + n_dep:2 + n_ex + n_dep + n_out]
        if by_cols:
            av = _bf(a_ref[...])
            for s in range(n_sh):
                cs = slice(s * width, (s + 1) * width)
                acc = _dot(av, w_ref[s], 1, 0 if mode == 'nn_cols' else 1)
                res = epi(acc, *[r[:, cs] for r in ex_refs]) if epi is not None else (acc,)
                for o, r in zip(out_refs, res):
                    o[:, cs] = r.astype(o.dtype)
        else:
            chunk = rows if mode == 'nn_rows' else cols
            acc = None
            for s in range(n_sh):
                part = _dot(_bf(a_ref[:, s * chunk:(s + 1) * chunk]), w_ref[s], 1, 0 if mode == 'nn_rows' else 1)
                acc = part if acc is None else acc + part
            res = epi(acc, *[r[...] for r in ex_refs]) if epi is not None else (acc,)
            for o, r in zip(out_refs, res):
                o[...] = r.astype(o.dtype)

    t, ka = a.shape
    return pl.pallas_call(
        body, name=name, grid=(t // tm, 1, 1),
        in_specs=[pl.BlockSpec((tm, ka), lambda i, j, k: (i, 0)),
                  pl.BlockSpec((n_sh, rows, cols), lambda i, j, k: (0, 0, 0))] + [s for _, s in extras] + [ANY] * n_dep,
        out_specs=[s for _, s in outs],
        out_shape=[s for s, _ in outs],
        compiler_params=_cparams(("parallel", "arbitrary", "arbitrary")),
    )(a, w, *[x for x, _ in extras], *deps)


def _sds(shape, dtype):
    return jax.ShapeDtypeStruct(shape, dtype)


def _row_tile(t, cap=ROW_TILE):
    return min(cap, t)


def _rms_fwd(name, x, g):
    t, d = x.shape
    tm = _row_tile(t)

    def body(x_ref, g_ref, o_ref):
        xv = x_ref[...]
        r = lax.rsqrt(jnp.mean(xv * xv, axis=-1, keepdims=True) + EPS)
        o_ref[...] = (xv * r * g_ref[...]).astype(o_ref.dtype)

    return pl.pallas_call(
        body, name=name, grid=(t // tm,),
        in_specs=[pl.BlockSpec((tm, d), lambda i: (i, 0)), pl.BlockSpec((1, d), lambda i: (0, 0))],
        out_specs=pl.BlockSpec((tm, d), lambda i: (i, 0)),
        out_shape=_sds((t, d), BF16),
        compiler_params=_cparams(("parallel",)),
    )(x, g)


def _rms_bwd_rows(dy, xv, g, n):
    r = lax.rsqrt(jnp.sum(xv * xv, axis=-1, keepdims=True) / n + EPS)
    xh = xv * r
    dxh = dy * g
    dx = r * (dxh - xh * (jnp.sum(dxh * xh, axis=-1, keepdims=True) / n))
    return dx, dy * xh


def _rms_bwd(name, dy, x, g, res):
    t, d = x.shape
    tm = _row_tile(t, 512)

    def body(dy_ref, x_ref, g_ref, res_ref, dx_ref, dg_ref):
        @pl.when(pl.program_id(0) == 0)
        def _():
            dg_ref[...] = jnp.zeros_like(dg_ref)

        dx, dgr = _rms_bwd_rows(dy_ref[...], x_ref[...], g_ref[...], d)
        dx_ref[...] = res_ref[...] + dx
        dg_ref[...] += jnp.sum(dgr, axis=0, keepdims=True)

    row = pl.BlockSpec((tm, d), lambda i: (i, 0))
    vec = pl.BlockSpec((1, d), lambda i: (0, 0))
    return pl.pallas_call(
        body, name=name, grid=(t // tm,),
        in_specs=[row, row, vec, row], out_specs=[row, vec],
        out_shape=[_sds((t, d), F32), _sds((1, d), F32)],
        compiler_params=_cparams(("arbitrary",)),
    )(dy, x, g, res)


def _loss_head(y, target):
    t, d = y.shape
    tm = _row_tile(t)

    def body(y_ref, t_ref, dy_ref, l_ref):
        @pl.when(pl.program_id(0) == 0)
        def _():
            l_ref[...] = jnp.zeros_like(l_ref)

        e = y_ref[...] - t_ref[...]
        dy_ref[...] = e / d
        l_ref[...] += jnp.sum(jnp.sum(e * e, axis=-1, keepdims=True), axis=0, keepdims=True)

    row = pl.BlockSpec((tm, d), lambda i: (i, 0))
    return pl.pallas_call(
        body, name="loss_head", grid=(t // tm,),
        in_specs=[row, row], out_specs=[row, pl.BlockSpec((8, 128), lambda i: (0, 0))],
        out_shape=[_sds((t, d), F32), _sds((8, 128), F32)],
        compiler_params=_cparams(("arbitrary",)),
    )(y, target)


def _ple_gate_bwd(name, dh, gate, e):
    t, d = dh.shape
    tm = _row_tile(t)

    def body(dh_ref, g_ref, e_ref, de_ref, dz_ref):
        dh_v, gt = dh_ref[...], g_ref[...]
        de_ref[...] = (dh_v * gt).astype(BF16)
        dz_ref[...] = (dh_v * e_ref[...] * (gt * (1.0 - gt))).astype(BF16)

    row = pl.BlockSpec((tm, d), lambda i: (i, 0))
    return pl.pallas_call(
        body, name=name, grid=(t // tm,), in_specs=[row, row, row], out_specs=[row, row],
        out_shape=[_sds((t, d), BF16), _sds((t, d), BF16)],
        compiler_params=_cparams(("parallel",)),
    )(dh, gate, e)


def _rope_half(v, cos, sin):
    half = v.shape[-1] // 2
    v1, v2 = v[:, :half], v[:, half:]
    return jnp.concatenate([v1 * cos - v2 * sin, v2 * cos + v1 * sin], axis=-1)


def _ret_consts():
    lg = jnp.log(1.0 - 2.0 ** (-5.0 - jnp.arange(RET_HEADS, dtype=F32)))
    idx = jnp.arange(CHUNK, dtype=F32)
    intra = jnp.exp(lg[:, None, None] * jnp.abs(idx[:, None] - idx[None, :]))
    qdec = jnp.exp(lg[:, None] * (idx + 1.0))
    kdec = jnp.exp(lg[:, None] * (CHUNK - 1.0 - idx))
    cdec = jnp.exp(lg * CHUNK)
    qdec = jnp.broadcast_to(qdec[:, :, None], (RET_HEADS, CHUNK, RET_DK))
    kdec = jnp.broadcast_to(kdec[:, :, None], (RET_HEADS, CHUNK, RET_DK))
    cdec = jnp.broadcast_to(cdec[:, None, None], (RET_HEADS, 1, RET_DV))
    return intra, qdec, kdec, cdec


def _ret_specs(rb, rev_nb=None):
    blk = (lambda i: i) if rev_nb is None else (lambda i: rev_nb - 1 - i)
    full = lambda shape: pl.BlockSpec(shape, lambda i: (0,) * len(shape))
    return dict(
        proj=pl.BlockSpec((rb, RET_IN), lambda i: (blk(i), 0)),
        tab=pl.BlockSpec((rb, RET_DK // 2), lambda i: (blk(i), 0)),
        vw=pl.BlockSpec((rb, RET_V_W), lambda i: (blk(i), 0)),
        st=pl.BlockSpec((rb // CHUNK, RET_HEADS, RET_DK, RET_DV), lambda i: (blk(i), 0, 0, 0)),
        gn=full((RET_HEADS, 1, RET_DV)),
        intra=full((RET_HEADS, CHUNK, CHUNK)),
        dec=full((RET_HEADS, CHUNK, RET_DK)),
        cdec=full((RET_HEADS, 1, RET_DV)),
    )


def _ret_fwd(proj, cos, sin, gn):
    t = proj.shape[0]
    rb = min(RET_ROWS, t)
    cpb = rb // CHUNK
    intra, qdec, kdec, cdec = _ret_consts()
    sp = _ret_specs(rb)

    def body(proj_ref, cos_ref, sin_ref, gn_ref, intra_ref, qd_ref, kd_ref, cd_ref,
             gated_ref, outp_ref, st_ref, s_ref):
        @pl.when(pl.program_id(0) == 0)
        def _():
            s_ref[...] = jnp.zeros_like(s_ref)

        def chunk(c, carry):
            rows = pl.ds(pl.multiple_of(c * CHUNK, CHUNK), CHUNK)
            cs, sn = cos_ref[rows, :], sin_ref[rows, :]
            for h in range(RET_HEADS):
                q = proj_ref[rows, h * RET_DK:(h + 1) * RET_DK].astype(F32)
                k = proj_ref[rows, RET_QK_W + h * RET_DK:RET_QK_W + (h + 1) * RET_DK].astype(F32)
                v = proj_ref[rows, 2 * RET_QK_W + h * RET_DV:2 * RET_QK_W + (h + 1) * RET_DV]
                g = proj_ref[rows, 2 * RET_QK_W + RET_V_W + h * RET_DV:
                             2 * RET_QK_W + RET_V_W + (h + 1) * RET_DV].astype(F32)
                qr = _rope_half(q, cs, sn)
                kr = _rope_half(k, cs, sn) * (RET_DK ** -0.5)
                qb, kb, vb = qr.astype(BF16), kr.astype(BF16), v
                sc = _dot(qb, kb, 1, 1) * intra_ref[h]
                inner = _dot(sc.astype(BF16), vb, 1, 0)
                s_old = s_ref[h]
                sb = s_old.astype(BF16)
                st_ref[c, h] = sb
                cross = _dot((qr * qd_ref[h]).astype(BF16), sb, 1, 0)
                out = inner + cross
                s_ref[h] = s_old * cd_ref[h] + _dot((kr * kd_ref[h]).astype(BF16), vb, 0, 0)
                r = lax.rsqrt(jnp.mean(out * out, axis=-1, keepdims=True) + EPS)
                y = out * r * gn_ref[h]
                cols = slice(h * RET_DV, (h + 1) * RET_DV)
                gated_ref[rows, cols] = (g * _sigmoid(g) * y).astype(BF16)
                outp_ref[rows, cols] = out
            return carry

        lax.fori_loop(0, cpb, chunk, 0)

    return pl.pallas_call(
        body, name="ret_fwd", grid=(t // rb,),
        in_specs=[sp['proj'], sp['tab'], sp['tab'], sp['gn'], sp['intra'], sp['dec'], sp['dec'], sp['cdec']],
        out_specs=[sp['vw'], sp['vw'], sp['st']],
        out_shape=[_sds((t, RET_V_W), BF16), _sds((t, RET_V_W), F32),
                   _sds((t // CHUNK, RET_HEADS, RET_DK, RET_DV), BF16)],
        scratch_shapes=[pltpu.VMEM((RET_HEADS, RET_DK, RET_DV), F32)],
        compiler_params=_cparams(("arbitrary",)),
    )(proj, cos, sin, gn.reshape(RET_HEADS, 1, RET_DV), intra, qdec, kdec, cdec)


def _ret_bwd(proj, cos, sin, gn, outp, states, dgated):
    t = proj.shape[0]
    rb = min(RET_ROWS, t)
    cpb = rb // CHUNK
    nb = t // rb
    intra, qdec, kdec, cdec = _ret_consts()
    sp = _ret_specs(rb, rev_nb=nb)

    def body(proj_ref, cos_ref, sin_ref, gn_ref, intra_ref, qd_ref, kd_ref, cd_ref, outp_ref, st_ref, dgt_ref,
             dproj_ref, dgn_ref, ds_ref):
        @pl.when(pl.program_id(0) == 0)
        def _():
            ds_ref[...] = jnp.zeros_like(ds_ref)
            dgn_ref[...] = jnp.zeros_like(dgn_ref)

        def chunk(cc, carry):
            c = cpb - 1 - cc
            rows = pl.ds(pl.multiple_of(c * CHUNK, CHUNK), CHUNK)
            cs, sn = cos_ref[rows, :], sin_ref[rows, :]
            for h in range(RET_HEADS):
                q = proj_ref[rows, h * RET_DK:(h + 1) * RET_DK].astype(F32)
                k = proj_ref[rows, RET_QK_W + h * RET_DK:RET_QK_W + (h + 1) * RET_DK].astype(F32)
                v = proj_ref[rows, 2 * RET_QK_W + h * RET_DV:2 * RET_QK_W + (h + 1) * RET_DV]
                g = proj_ref[rows, 2 * RET_QK_W + RET_V_W + h * RET_DV:
                             2 * RET_QK_W + RET_V_W + (h + 1) * RET_DV].astype(F32)
                cols = slice(h * RET_DV, (h + 1) * RET_DV)
                qr = _rope_half(q, cs, sn)
                kr = _rope_half(k, cs, sn) * (RET_DK ** -0.5)
                qb, kb, vb = qr.astype(BF16), kr.astype(BF16), v
                qdb = (qr * qd_ref[h]).astype(BF16)
                kdb = (kr * kd_ref[h]).astype(BF16)
                out = outp_ref[rows, cols]
                dgt = dgt_ref[rows, cols]
                gnh = gn_ref[h]
                r = lax.rsqrt(jnp.mean(out * out, axis=-1, keepdims=True) + EPS)
                xh = out * r
                sg = _sigmoid(g)
                dgate = dgt * (xh * gnh) * (sg * (1.0 + g * (1.0 - sg)))
                dy = dgt * (g * sg)
                dgn_ref[h] += jnp.sum(dy * xh, axis=0, keepdims=True)
                dxh = dy * gnh
                dout = r * (dxh - xh * jnp.mean(dxh * xh, axis=-1, keepdims=True))
                doutb = dout.astype(BF16)
                itr = intra_ref[h]
                pb = (_dot(qb, kb, 1, 1) * itr).astype(BF16)
                dv = _dot(pb, doutb, 0, 0)
                dsc = (_dot(doutb, vb, 1, 1) * itr).astype(BF16)
                dq = _dot(dsc, kb, 1, 0)
                dk = _dot(dsc, qb, 0, 0)
                dq = dq + _dot(doutb, st_ref[c, h], 1, 1) * qd_ref[h]
                ds_new = ds_ref[h]
                dsb = ds_new.astype(BF16)
                dk = dk + _dot(vb, dsb, 1, 1) * kd_ref[h]
                dv = dv + _dot(kdb, dsb, 1, 0)
                ds_ref[h] = ds_new * cd_ref[h] + _dot(qdb, doutb, 0, 0)
                dproj_ref[rows, h * RET_DK:(h + 1) * RET_DK] = _rope_half(dq, cs, -sn).astype(BF16)
                dproj_ref[rows, RET_QK_W + h * RET_DK:RET_QK_W + (h + 1) * RET_DK] = (
                    _rope_half(dk * (RET_DK ** -0.5), cs, -sn).astype(BF16))
                dproj_ref[rows, 2 * RET_QK_W + h * RET_DV:2 * RET_QK_W + (h + 1) * RET_DV] = dv.astype(BF16)
                dproj_ref[rows, 2 * RET_QK_W + RET_V_W + h * RET_DV:
                          2 * RET_QK_W + RET_V_W + (h + 1) * RET_DV] = dgate.astype(BF16)
            return carry

        lax.fori_loop(0, cpb, chunk, 0)

    return pl.pallas_call(
        body, name="ret_bwd", grid=(nb,),
        in_specs=[sp['proj'], sp['tab'], sp['tab'], sp['gn'], sp['intra'], sp['dec'], sp['dec'], sp['cdec'],
                  sp['vw'], sp['st'], sp['vw']],
        out_specs=[sp['proj'], sp['gn']],
        out_shape=[_sds((t, RET_IN), BF16), _sds((RET_HEADS, 1, RET_DV), F32)],
        scratch_shapes=[pltpu.VMEM((RET_HEADS, RET_DK, RET_DV), F32)],
        compiler_params=_cparams(("arbitrary",)),
    )(proj, cos, sin, gn.reshape(RET_HEADS, 1, RET_DV), intra, qdec, kdec, cdec, outp, states, dgated)


def _mla_tables(t):
    half = MLA_ROPE // 2
    inv = 1.0 / (ROPE_THETA ** (jnp.arange(0, MLA_ROPE, 2, dtype=F32) / MLA_ROPE))
    ang = jnp.arange(t, dtype=F32)[:, None] * inv[None, :]
    cos, sin = jnp.cos(ang), jnp.sin(ang)
    z = jnp.zeros((t, half), F32)
    c = jnp.concatenate([cos, cos, z, z], axis=1)
    s1 = jnp.concatenate([-sin, z, z, z], axis=1)
    s2 = jnp.concatenate([z, sin, z, z], axis=1)
    return c, s1, s2


def _rope_tile(r, c, s1, s2):
    return r * c + pltpu.roll(r, 96, 1) * s1 + pltpu.roll(r, 32, 1) * s2


def _mla_mid(proj2, qa, kva):
    t = proj2.shape[0]
    tm = _row_tile(t)

    def body(p_ref, qa_ref, kva_ref, cq_ref, ckv_ref):
        cq = p_ref[:, :MLA_Q_RANK]
        ckv = p_ref[:, MLA_Q_RANK:MLA_Q_RANK + MLA_KV_RANK]
        rq = lax.rsqrt(jnp.mean(cq * cq, axis=-1, keepdims=True) + EPS)
        rkv = lax.rsqrt(jnp.mean(ckv * ckv, axis=-1, keepdims=True) + EPS)
        cq_ref[...] = (cq * rq * qa_ref[...]).astype(BF16)
        ckv_ref[...] = (ckv * rkv * kva_ref[...]).astype(BF16)

    return pl.pallas_call(
        body, name="mla_mid", grid=(t // tm,),
        in_specs=[pl.BlockSpec((tm, MLA_IN_PAD), lambda i: (i, 0)),
                  pl.BlockSpec((1, MLA_Q_RANK), lambda i: (0, 0)),
                  pl.BlockSpec((1, MLA_KV_RANK), lambda i: (0, 0))],
        out_specs=[pl.BlockSpec((tm, MLA_Q_RANK), lambda i: (i, 0)),
                   pl.BlockSpec((tm, MLA_KV_RANK), lambda i: (i, 0))],
        out_shape=[_sds((t, MLA_Q_RANK), BF16), _sds((t, MLA_KV_RANK), BF16)],
        compiler_params=_cparams(("parallel",)),
    )(proj2, qa, kva)


def _mla_mid_bwd(proj2, qa, kva, dcq, dckv, dkr):
    t = proj2.shape[0]
    tm = _row_tile(t)

    def body(p_ref, qa_ref, kva_ref, dcq_ref, dckv_ref, dkr_ref, dp_ref, dqa_ref, dkva_ref):
        @pl.when(pl.program_id(0) == 0)
        def _():
            dqa_ref[...] = jnp.zeros_like(dqa_ref)
            dkva_ref[...] = jnp.zeros_like(dkva_ref)

        dxq, dgq = _rms_bwd_rows(dcq_ref[...], p_ref[:, :MLA_Q_RANK], qa_ref[...], MLA_Q_RANK)
        dxk, dgk = _rms_bwd_rows(dckv_ref[...], p_ref[:, MLA_Q_RANK:MLA_Q_RANK + MLA_KV_RANK], kva_ref[...],
                                 MLA_KV_RANK)
        dp_ref[:, :MLA_Q_RANK] = dxq.astype(BF16)
        dp_ref[:, MLA_Q_RANK:MLA_Q_RANK + MLA_KV_RANK] = dxk.astype(BF16)
        dp_ref[:, MLA_Q_RANK + MLA_KV_RANK:] = dkr_ref[...].astype(BF16)
        dqa_ref[...] += jnp.sum(dgq, axis=0, keepdims=True)
        dkva_ref[...] += jnp.sum(dgk, axis=0, keepdims=True)

    return pl.pallas_call(
        body, name="mla_mid_bwd", grid=(t // tm,),
        in_specs=[pl.BlockSpec((tm, MLA_IN_PAD), lambda i: (i, 0)),
                  pl.BlockSpec((1, MLA_Q_RANK), lambda i: (0, 0)),
                  pl.BlockSpec((1, MLA_KV_RANK), lambda i: (0, 0)),
                  pl.BlockSpec((tm, MLA_Q_RANK), lambda i: (i, 0)),
                  pl.BlockSpec((tm, MLA_KV_RANK), lambda i: (i, 0)),
                  pl.BlockSpec((tm, 128), lambda i: (i, 0))],
        out_specs=[pl.BlockSpec((tm, MLA_IN_PAD), lambda i: (i, 0)),
                   pl.BlockSpec((1, MLA_Q_RANK), lambda i: (0, 0)),
                   pl.BlockSpec((1, MLA_KV_RANK), lambda i: (0, 0))],
        out_shape=[_sds((t, MLA_IN_PAD), BF16), _sds((1, MLA_Q_RANK), F32), _sds((1, MLA_KV_RANK), F32)],
        compiler_params=_cparams(("arbitrary",)),
    )(proj2, qa, kva, dcq, dckv, dkr)


def _mla_prep_specs(t, tm):
    head = lambda w: pl.BlockSpec((None, tm, w), lambda i, h: (h, i, 0))
    return dict(
        head256=head(MLA_HD_PAD), head128=head(MLA_VD),
        cols256=pl.BlockSpec((tm, MLA_HD_PAD), lambda i, h: (i, h)),
        cq=pl.BlockSpec((tm, MLA_Q_RANK), lambda i, h: (i, 0)),
        ckv=pl.BlockSpec((tm, MLA_KV_RANK), lambda i, h: (i, 0)),
        wuq=pl.BlockSpec((None, MLA_Q_RANK, MLA_HD_PAD), lambda i, h: (h, 0, 0)),
        wukv=pl.BlockSpec((None, MLA_KV_RANK, MLA_HD_PAD), lambda i, h: (h, 0, 0)),
        kr=pl.BlockSpec((tm, 128), lambda i, h: (i, (MLA_Q_RANK + MLA_KV_RANK) // 128)),
        gain=pl.BlockSpec((1, MLA_HD_PAD), lambda i, h: (0, 0)),
        tab=pl.BlockSpec((tm, 128), lambda i, h: (i, 0)),
    )


def _mla_prep(cq, ckv, wuq, wukv, proj2, gq, gk, tabs):
    t = cq.shape[0]
    tm = _row_tile(t)
    sp = _mla_prep_specs(t, tm)

    def body(cq_ref, ckv_ref, wuq_ref, wukv_ref, kr_ref, gq_ref, gk_ref, c_ref, s1_ref, s2_ref,
             qh_ref, kh_ref, vh_ref):
        c, s1, s2 = c_ref[...], s1_ref[...], s2_ref[...]

        def norm_rope(xv, gain):
            r = lax.rsqrt(jnp.sum(xv * xv, axis=-1, keepdims=True) / MLA_QKD + EPS)
            y = xv * r * gain
            return jnp.concatenate([y[:, :MLA_NOPE], _rope_tile(y[:, MLA_NOPE:], c, s1, s2)], axis=-1)

        kvv = _dot(ckv_ref[...], wukv_ref[...], 1, 0)
        qh_ref[...] = norm_rope(_dot(cq_ref[...], wuq_ref[...], 1, 0), gq_ref[...]).astype(BF16)
        kf = jnp.concatenate([kvv[:, :MLA_NOPE], kr_ref[...]], axis=-1)
        kh_ref[...] = norm_rope(kf, gk_ref[...]).astype(BF16)
        vh_ref[...] = jnp.concatenate([kvv[:, MLA_NOPE:], jnp.ones((tm, MLA_VD), F32)], axis=-1).astype(BF16)

    return pl.pallas_call(
        body, name="mla_prep", grid=(t // tm, MLA_HEADS),
        in_specs=[sp['cq'], sp['ckv'], sp['wuq'], sp['wukv'], sp['kr'], sp['gain'], sp['gain'],
                  sp['tab'], sp['tab'], sp['tab']],
        out_specs=[sp['head256'], sp['head256'], sp['head256']],
        out_shape=[_sds((MLA_HEADS, t, MLA_HD_PAD), BF16), _sds((MLA_HEADS, t, MLA_HD_PAD), BF16),
                   _sds((MLA_HEADS, t, 2 * MLA_VD), BF16)],
        compiler_params=_cparams(("parallel", "arbitrary")),
    )(cq, ckv, wuq, wukv, proj2, gq, gk, *tabs)


def _mla_prep_bwd(cq, ckv, wuq, wukv, proj2, gq, gk, tabs, dqt, dkh, dvh):
    t = cq.shape[0]
    tm = _row_tile(t)
    ab = dqt.shape[-1]
    sp = _mla_prep_specs(t, tm)

    def body(cq_ref, ckv_ref, wuq_ref, wukv_ref, kr_ref, gq_ref, gk_ref, c_ref, s1_ref, s2_ref,
             dqt_ref, dkh_ref, dvh_ref, dq_ref, dkv_ref, dkr_ref, dgq_ref, dgk_ref):
        dqh = jnp.concatenate([dqt_ref[b].T for b in range(tm // ab)], axis=0)
        i, h = pl.program_id(0), pl.program_id(1)

        @pl.when((i == 0) & (h == 0))
        def _():
            dgq_ref[...] = jnp.zeros_like(dgq_ref)
            dgk_ref[...] = jnp.zeros_like(dgk_ref)

        @pl.when(h == 0)
        def _():
            dkr_ref[...] = jnp.zeros_like(dkr_ref)

        c, s1, s2 = c_ref[...], s1_ref[...], s2_ref[...]

        def back(xv, gain, dout):
            dy = jnp.concatenate([dout[:, :MLA_NOPE], _rope_tile(dout[:, MLA_NOPE:], c, -s1, -s2)], axis=-1)
            return _rms_bwd_rows(dy, xv, gain, MLA_QKD)

        kvv = _dot(ckv_ref[...], wukv_ref[...], 1, 0)
        dxq, dgq = back(_dot(cq_ref[...], wuq_ref[...], 1, 0), gq_ref[...], dqh)
        kf = jnp.concatenate([kvv[:, :MLA_NOPE], kr_ref[...]], axis=-1)
        dxk, dgk = back(kf, gk_ref[...], dkh_ref[...])
        dq_ref[...] = dxq.astype(BF16)
        dkv_ref[...] = jnp.concatenate([dxk[:, :MLA_NOPE], dvh_ref[...]], axis=-1).astype(BF16)
        dkr_ref[...] += dxk[:, MLA_NOPE:]
        dgq_ref[...] += jnp.sum(dgq, axis=0, keepdims=True)
        dgk_ref[...] += jnp.sum(dgk, axis=0, keepdims=True)

    return pl.pallas_call(
        body, name="mla_prep_bwd", grid=(t // tm, MLA_HEADS),
        in_specs=[sp['cq'], sp['ckv'], sp['wuq'], sp['wukv'], sp['kr'], sp['gain'], sp['gain'],
                  sp['tab'], sp['tab'], sp['tab'],
                  pl.BlockSpec((None, tm // ab, MLA_HD_PAD, ab), lambda i, h: (h, i, 0, 0)),
                  sp['head256'], sp['head128']],
        out_specs=[sp['cols256'], sp['cols256'], sp['tab'], sp['gain'], sp['gain']],
        out_shape=[_sds((t, MLA_HEADS * MLA_HD_PAD), BF16), _sds((t, MLA_HEADS * MLA_HD_PAD), BF16),
                   _sds((t, 128), F32), _sds((1, MLA_HD_PAD), F32), _sds((1, MLA_HD_PAD), F32)],
        compiler_params=_cparams(("arbitrary", "arbitrary")),
    )(cq, ckv, wuq, wukv, proj2, gq, gk, *tabs, dqt, dkh, dvh)


def _chunk_visible(rows, cols, row_off, col_off):
    rq = lax.shift_right_logical(lax.broadcasted_iota(jnp.int32, (rows, cols), 0) + row_off, 6)
    ck = lax.shift_right_logical(lax.broadcasted_iota(jnp.int32, (rows, cols), 1) + col_off, 6)
    return ck <= rq


def _rows_to_lanes(col):
    return col.T[:8, :]


def _attn_fwd(qh, kh, vh):
    t = qh.shape[1]
    ab = min(ATT_BLOCK, t)
    tq = min(ATT_QROWS, t)
    r = tq // ab
    hg = ATT_HEADS

    def body(q_ref, k_ref, v_ref, o_ref, lse_ref):
        n_un = pl.program_id(1) * r

        def step(b, state, diag):
            rows = pl.ds(pl.multiple_of(b * ab, ab), ab)
            ms, accs = [], []
            for hh in range(hg):
                m, acc = state[0][hh], state[1][hh]
                s = _dot(q_ref[hh], k_ref[hh, rows, :], 1, 1)
                if diag is not None:
                    s = jnp.where(_chunk_visible(tq, ab, 0, diag * ab), s, -1e30)
                m_new = jnp.maximum(m, jnp.max(s, axis=-1, keepdims=True))
                p = jnp.exp2((s - m_new) * ATT_EXP2).astype(BF16)
                accs.append(jnp.exp2((m - m_new) * ATT_EXP2) * acc + _dot(p, v_ref[hh, rows, :], 1, 0))
                ms.append(m_new)
            return tuple(ms), tuple(accs)

        heads = lambda v: tuple(v for _ in range(hg))
        state = (heads(jnp.full((tq, 1), -1e30, F32)), heads(jnp.zeros((tq, 2 * MLA_VD), F32)))
        state = lax.fori_loop(0, n_un, lambda b, st: step(b, st, None), state)
        for d in range(r):
            state = step(n_un + d, state, d)
        ms, accs = state
        for hh in range(hg):
            l = accs[hh][:, MLA_VD:]
            o_ref[:, hh * MLA_VD:(hh + 1) * MLA_VD] = accs[hh][:, :MLA_VD] / l
            lse_t = _rows_to_lanes(ms[hh] * ATT_EXP2 + jnp.log(l) * LOG2E)
            for d in range(r):
                lse_ref[hh, d] = lse_t[:, d * ab:(d + 1) * ab]

    return pl.pallas_call(
        body, name="mla_attn", grid=(MLA_HEADS // hg, t // tq),
        in_specs=[pl.BlockSpec((hg, tq, MLA_HD_PAD), lambda g, i: (g, i, 0)),
                  pl.BlockSpec((hg, t, MLA_HD_PAD), lambda g, i: (g, 0, 0)),
                  pl.BlockSpec((hg, t, 2 * MLA_VD), lambda g, i: (g, 0, 0))],
        out_specs=[pl.BlockSpec((tq, hg * MLA_VD), lambda g, i: (i, g)),
                   pl.BlockSpec((hg, r, 8, ab), lambda g, i: (g, i, 0, 0))],
        out_shape=[_sds((t, MLA_HEADS * MLA_VD), F32), _sds((MLA_HEADS, t // ab, 8, ab), F32)],
        compiler_params=_cparams(("parallel", "arbitrary")),
    )(qh, kh, vh)


def _attn_delta(do, o, ab):
    t = do.shape[0]
    tm = _row_tile(t)

    def body(do_ref, o_ref, d_ref):
        d = jnp.sum(do_ref[...] * o_ref[...], axis=-1, keepdims=True)
        d_t = _rows_to_lanes(jnp.broadcast_to(d, (tm, 128)))
        for b in range(tm // ab):
            d_ref[b] = d_t[:, b * ab:(b + 1) * ab]

    col = pl.BlockSpec((tm, MLA_VD), lambda i, h: (i, h))
    return pl.pallas_call(
        body, name="mla_delta", grid=(t // tm, MLA_HEADS), in_specs=[col, col],
        out_specs=pl.BlockSpec((None, tm // ab, 8, ab), lambda i, h: (h, i, 0, 0)),
        out_shape=_sds((MLA_HEADS, t // ab, 8, ab), F32),
        compiler_params=_cparams(("parallel", "parallel")),
    )(do, o)


def _attn_bwd(qh, kh, vh, dob, lse_t, dl_t):
    t = qh.shape[1]
    ab = min(ATT_BLOCK, t)
    kb = min(ATT_KROWS, t)
    r = kb // ab
    nq = t // ab
    hg = ATT_HEADS

    def body(q_ref, k_ref, v_ref, do_ref, lse_ref, dl_ref, dqt_ref, dk_ref, dv_ref):
        j = pl.program_id(1)

        @pl.when(j == 0)
        def _():
            dqt_ref[...] = jnp.zeros_like(dqt_ref)

        ks = [k_ref[hh] for hh in range(hg)]
        vs = [v_ref[hh, :, :MLA_VD] for hh in range(hg)]
        kts = [k.T for k in ks]

        def step(b, grads, diag):
            rows = pl.ds(pl.multiple_of(b * ab, ab), ab)
            out = []
            for hh in range(hg):
                dk, dv = grads[hh]
                q = q_ref[hh, rows, :]
                do = do_ref[rows, hh * MLA_VD:(hh + 1) * MLA_VD]
                s_t = _dot(ks[hh], q, 1, 1)
                if diag is not None:
                    key_chunk = lax.shift_right_logical(lax.broadcasted_iota(jnp.int32, (kb, ab), 0), 6)
                    query_chunk = lax.shift_right_logical(
                        lax.broadcasted_iota(jnp.int32, (kb, ab), 1) + diag * ab, 6)
                    s_t = jnp.where(key_chunk <= query_chunk, s_t, -1e30)
                p_t = jnp.exp2(s_t * ATT_EXP2 - lse_ref[hh, b][0:1, :])
                dp_t = _dot(vs[hh], do, 1, 1)
                ds_t = (p_t * (dp_t - dl_ref[hh, b][0:1, :]) * ATT_SCALE).astype(BF16)
                dqt_ref[hh, b] += _dot(kts[hh], ds_t, 1, 0)
                out.append((dk + _dot(ds_t, q, 1, 0), dv + _dot(p_t.astype(BF16), do, 1, 0)))
            return tuple(out)

        grads = tuple((jnp.zeros((kb, MLA_HD_PAD), F32), jnp.zeros((kb, MLA_VD), F32)) for _ in range(hg))
        for d in range(r):
            grads = step(j * r + d, grads, d)
        grads = lax.fori_loop((j + 1) * r, nq, lambda b, g: step(b, g, None), grads)
        for hh in range(hg):
            dk_ref[hh] = grads[hh][0]
            dv_ref[hh] = grads[hh][1]

    whole = lambda w: pl.BlockSpec((hg, t, w), lambda g, j: (g, 0, 0))
    blk = lambda w: pl.BlockSpec((hg, kb, w), lambda g, j: (g, j, 0))
    stat = pl.BlockSpec((hg, nq, 8, ab), lambda g, j: (g, 0, 0, 0))
    return pl.pallas_call(
        body, name="mla_attn_bwd", grid=(MLA_HEADS // hg, t // kb),
        in_specs=[whole(MLA_HD_PAD), blk(MLA_HD_PAD), blk(2 * MLA_VD),
                  pl.BlockSpec((t, hg * MLA_VD), lambda g, j: (0, g)), stat, stat],
        out_specs=[pl.BlockSpec((hg, nq, MLA_HD_PAD, ab), lambda g, j: (g, 0, 0, 0)), blk(MLA_HD_PAD), blk(MLA_VD)],
        out_shape=[_sds((MLA_HEADS, nq, MLA_HD_PAD, ab), F32), _sds((MLA_HEADS, t, MLA_HD_PAD), F32),
                   _sds((MLA_HEADS, t, MLA_VD), F32)],
        compiler_params=_cparams(("parallel", "arbitrary")),
    )(qh, kh, vh, dob, lse_t, dl_t)


VEC = pl.BlockSpec((1, D_MODEL), lambda i, j, k: (0, 0))


def _rows(tm, width):
    return pl.BlockSpec((tm, width), lambda i, j, k: (i, 0))


def _residual_epi(next_gain):
    if next_gain is None:
        return [], lambda acc, hv: (acc + hv,)

    def epi(acc, hv, g):
        h_new = acc + hv
        r = lax.rsqrt(jnp.mean(h_new * h_new, axis=-1, keepdims=True) + EPS)
        return h_new, h_new * r * g

    return [(next_gain, VEC)], epi


def _residual_outs(t, row, next_gain):
    outs = [(_sds((t, D_MODEL), F32), row)]
    return outs + ([(_sds((t, D_MODEL), BF16), row)] if next_gain is not None else [])


def _mlp_fwd(l, h, hn, w1g, fetch_w2, next_gain):
    t = h.shape[0]
    tm = _row_tile(t, 512)

    def relu2(acc):
        r = jnp.maximum(acc, 0.0)
        return (r * r,)

    (u,) = _mm_rows(f"mlp_up{l}", tm, hn, w1g, 'nn_cols', [(_sds((t, D_FF), BF16), _rows(tm, D_FF))], epi=relu2)
    w2g = fetch_w2((u,))
    row = _rows(tm, D_MODEL)
    more, epi = _residual_epi(next_gain)
    h2, hn_next = _mm_rows(f"mlp_down{l}", tm, u, w2g, 'nn_rows', _residual_outs(t, row, next_gain),
                           extras=[(h, row)] + more, epi=epi)
    return h2, hn_next, (h, hn, u, w1g, w2g)


def _norm_bwd_outs(t, tm):
    return [(_sds((t, D_MODEL), F32), pl.BlockSpec((tm, D_MODEL), lambda i, j, k: (i, 0))),
            (_sds((t // tm, 1, D_MODEL), F32), pl.BlockSpec((None, 1, D_MODEL), lambda i, j, k: (i, 0, 0)))]


def _norm_bwd_epi(acc, xv, res, g):
    dx, dgr = _rms_bwd_rows(acc, xv, g, D_MODEL)
    return res + dx, jnp.sum(dgr, axis=0, keepdims=True)


def _mlp_bwd(l, dh, saved, norm_g):
    h, hn, u, w1g, w2g = saved
    t = h.shape[0]
    tm = _row_tile(t, 512)
    nsh, _, wsh = w1g.shape
    wide = _rows(tm, D_FF)
    (da,) = _mm_rows(f"mlp_du{l}", tm, dh, w2g, 'nt_rows', [(_sds((t, D_FF), BF16), wide)], extras=[(u, wide)],
                     epi=lambda acc, uv: (2.0 * jnp.sqrt(uv.astype(F32)) * acc,))
    tw = _row_tile(t, 512)
    (dw2,) = _mm(f"mlp_dw2{l}", (1, 1, t // tw),
                 u, pl.BlockSpec((tw, D_FF), lambda i, j, k: (k, 0)),
                 dh, pl.BlockSpec((tw, D_MODEL), lambda i, j, k: (k, 0)), (0, 0),
                 [(_sds((D_FF, D_MODEL), BF16), pl.BlockSpec((D_FF, D_MODEL), lambda i, j, k: (0, 0)))])
    dw2 = dw2.reshape(nsh, wsh, D_MODEL)
    (dw1,) = _mm(f"mlp_dw1{l}", (1, 1, t // tw),
                 hn, pl.BlockSpec((tw, D_MODEL), lambda i, j, k: (k, 0)),
                 da, pl.BlockSpec((tw, D_FF), lambda i, j, k: (k, 0)), (0, 0),
                 [(_sds((nsh, D_MODEL, wsh), BF16), pl.BlockSpec((nsh, D_MODEL, wsh), lambda i, j, k: (0, 0, 0)))],
                 split=wsh)
    row = _rows(tm, D_MODEL)
    dh_in, dg = _mm_rows(f"mlp_dhn{l}", tm, da, w1g, 'nt_cols', _norm_bwd_outs(t, tm),
                         extras=[(h, row), (dh, row), (norm_g, VEC)], epi=_norm_bwd_epi)
    return dh_in, jnp.sum(dg, axis=0), dw1, dw2


def _ple_fwd(l, h, hn, p, wg, wp, next_gain, target=None):
    t = h.shape[0]
    tm = _row_tile(t, 512)
    row = pl.BlockSpec((tm, D_MODEL), lambda i, j, k: (i, 0))
    full = lambda r: pl.BlockSpec((r, D_MODEL), lambda i, j, k: (0, 0))
    (e,) = _mm(f"ple_proj{l}", (t // tm, 1, 1),
               p, pl.BlockSpec((None, None, tm, PLE_DIM), lambda i, j, k: (l, 0, i, 0)),
               wp, full(PLE_DIM), (1, 0), [(_sds((t, D_MODEL), F32), row)])

    f32_row, bf_row = (_sds((t, D_MODEL), F32), row), (_sds((t, D_MODEL), BF16), row)
    if target is not None:
        def loss_epi(acc, hv, ev, tv):
            gt = _sigmoid(acc)
            err = hv + gt * ev - tv
            sq = jnp.sum(jnp.sum(err * err, axis=-1, keepdims=True), axis=0, keepdims=True)
            return err / D_MODEL, gt, jnp.broadcast_to(sq, (8, 128))

        dy, gate, sq = _mm(f"ple_gate{l}", (t // tm, 1, 1), hn, row, wg, full(D_MODEL), (1, 0),
                           [f32_row, f32_row, (_sds((t // tm, 8, 128), F32),
                                               pl.BlockSpec((None, 8, 128), lambda i, j, k: (i, 0, 0)))],
                           extras=[(h, row), (e, row), (target, row)], epi=loss_epi)
        return dy, jnp.sum(sq, axis=0), (h, hn, gate, e)

    def gate_epi(acc, hv, ev, *gain):
        gt = _sigmoid(acc)
        h_new = hv + gt * ev
        if not gain:
            return h_new, gt
        r = lax.rsqrt(jnp.mean(h_new * h_new, axis=-1, keepdims=True) + EPS)
        return h_new, gt, h_new * r * gain[0]

    res = _mm(f"ple_gate{l}", (t // tm, 1, 1), hn, row, wg, full(D_MODEL), (1, 0),
              [f32_row, f32_row] + ([bf_row] if next_gain is not None else []),
              extras=[(h, row), (e, row)] + ([(next_gain, VEC)] if next_gain is not None else []), epi=gate_epi)
    h_out, gate = res[0], res[1]
    return h_out, (res[2] if next_gain is not None else None), (h, hn, gate, e)


def _ple_bwd(l, dh, saved, p, norm_g, wg, deps=()):
    h, hn, gate, e = saved
    t = h.shape[0]
    tm = _row_tile(t)
    tk = _row_tile(t, 512)
    de, dz = _ple_gate_bwd(f"ple_gate_bwd{l}", dh, gate, e)
    full = lambda r: pl.BlockSpec((r, D_MODEL), lambda i, j, k: (0, 0))
    rowk = pl.BlockSpec((tk, D_MODEL), lambda i, j, k: (k, 0))
    (dwp,) = _mm(f"ple_dwp{l}", (1, 1, t // tk),
                 p, pl.BlockSpec((None, None, tk, PLE_DIM), lambda i, j, k: (l, 0, k, 0)),
                 de, rowk, (0, 0), [(_sds((PLE_DIM, D_MODEL), BF16), full(PLE_DIM))], deps=deps)
    (dwg,) = _mm(f"ple_dwg{l}", (1, 1, t // tk), hn, rowk, dz, rowk, (0, 0),
                 [(_sds((D_MODEL, D_MODEL), BF16), full(D_MODEL))])
    row = pl.BlockSpec((tm, D_MODEL), lambda i, j, k: (i, 0))
    dh_in, dg = _mm(f"ple_dhn{l}", (t // tm, 1, 1), dz, row, wg, full(D_MODEL), (1, 1),
                    _norm_bwd_outs(t, tm), extras=[(h, row), (dh, row), (norm_g, VEC)], epi=_norm_bwd_epi)
    return dh_in, jnp.sum(dg, axis=0), dwg, dwp


def _ret_layer_fwd(x, norm_g, wri, fetch_wro, gn, cos, sin, next_gain, deps=()):
    t = x.shape[0]
    tm = _row_tile(t)
    nsh, _, wsh = wri.shape
    hn = _rms_fwd("mix_norm0", x, norm_g)
    tp = _row_tile(t, 512)
    (proj,) = _mm_rows("ret_in", tp, hn, wri, 'nn_cols', [(_sds((t, RET_IN), BF16), _rows(tp, RET_IN))], deps=deps)
    gated, outp, states = _ret_fwd(proj, cos, sin, gn)
    wro = fetch_wro((gated,))
    row = pl.BlockSpec((tm, D_MODEL), lambda i, j, k: (i, 0))
    kt = 512
    more, epi = _residual_epi(next_gain)
    h1, hn_next = _mm("ret_out", (t // tm, 1, RET_V_W // kt),
                      gated, pl.BlockSpec((tm, kt), lambda i, j, k: (i, k)),
                      wro, pl.BlockSpec((kt, D_MODEL), lambda i, j, k: (k, 0)), (1, 0),
                      _residual_outs(t, row, next_gain), extras=[(x, row)] + more, epi=epi)
    return h1, hn_next, (x, hn, proj, gated, outp, states, wro)


def _ret_layer_bwd(dh, saved, norm_g, wri, gn, cos, sin, emit, deps=()):
    x, hn, proj, gated, outp, states, wro = saved
    t = x.shape[0]
    tm = _row_tile(t)
    tk = _row_tile(t, 512)
    nsh, _, wsh = wri.shape
    (dgated,) = _mm("ret_dgated", (t // tm, RET_V_W // D_MODEL, 1),
                    dh, pl.BlockSpec((tm, D_MODEL), lambda i, j, k: (i, 0)),
                    wro, pl.BlockSpec((D_MODEL, D_MODEL), lambda i, j, k: (j, 0)), (1, 1),
                    [(_sds((t, RET_V_W), F32), pl.BlockSpec((tm, D_MODEL), lambda i, j, k: (i, j)))], deps=deps)
    (dwro,) = _mm("ret_dwro", (1, 1, t // tk),
                  gated, pl.BlockSpec((tk, RET_V_W), lambda i, j, k: (k, 0)),
                  dh, pl.BlockSpec((tk, D_MODEL), lambda i, j, k: (k, 0)), (0, 0),
                  [(_sds((RET_V_W, D_MODEL), BF16), pl.BlockSpec((RET_V_W, D_MODEL), lambda i, j, k: (0, 0)))])
    dproj, dgn = _ret_bwd(proj, cos, sin, gn, outp, states, dgated)
    half = nsh // 2
    (dwri,) = _mm("ret_dwri", (2, 1, t // tk),
                  hn, pl.BlockSpec((tk, D_MODEL), lambda i, j, k: (k, 0)),
                  dproj, pl.BlockSpec((tk, half * wsh), lambda i, j, k: (k, i)), (0, 0),
                  [(_sds((nsh, D_MODEL, wsh), BF16), pl.BlockSpec((half, D_MODEL, wsh), lambda i, j, k: (i, 0, 0)))],
                  split=wsh)
    deps = emit(dwro, dwri)
    td = _row_tile(t, 256)
    row = _rows(td, D_MODEL)
    dx, dg = _mm_rows("ret_dhn", td, dproj, wri, 'nt_cols', _norm_bwd_outs(t, td),
                      extras=[(x, row), (dh, row), (norm_g, VEC)], epi=_norm_bwd_epi, deps=deps)
    return dx, jnp.sum(dg, axis=0), dgn.reshape(RET_HEADS, RET_DV)


def _mla_layer_fwd(h, hn, wmi, qa, kva, wuq, wukv, gq, gk, wmo, tabs, next_gain):
    t = h.shape[0]
    tm = _row_tile(t)
    row = pl.BlockSpec((tm, D_MODEL), lambda i, j, k: (i, 0))
    (proj2,) = _mm("mla_in", (t // tm, 1, 1), hn, row,
                   wmi, pl.BlockSpec((D_MODEL, MLA_IN_PAD), lambda i, j, k: (0, 0)), (1, 0),
                   [(_sds((t, MLA_IN_PAD), F32), pl.BlockSpec((tm, MLA_IN_PAD), lambda i, j, k: (i, 0)))])
    cq, ckv = _mla_mid(proj2, qa, kva)
    qh, kh, vh = _mla_prep(cq, ckv, wuq, wukv, proj2, gq, gk, tabs)
    o, lse = _attn_fwd(qh, kh, vh)
    more, epi = _residual_epi(next_gain)
    h_out, hn_next = _mm("mla_out", (t // tm, 1, 1), o, row,
                         wmo, pl.BlockSpec((D_MODEL, D_MODEL), lambda i, j, k: (0, 0)), (1, 0),
                         _residual_outs(t, row, next_gain), extras=[(h, row)] + more, epi=epi)
    return h_out, hn_next, (h, hn, proj2, cq, ckv, qh, kh, vh, o, lse)


def _mla_layer_bwd(dh, saved, norm_g, wmi, qa, kva, wuq, wukv, gq, gk, wmo, tabs, deps=()):
    h, hn, proj2, cq, ckv, qh, kh, vh, o, lse = saved
    t = h.shape[0]
    tm = _row_tile(t)
    tk = _row_tile(t, 512)
    row = pl.BlockSpec((tm, D_MODEL), lambda i, j, k: (i, 0))
    rowk = pl.BlockSpec((tk, D_MODEL), lambda i, j, k: (k, 0))
    sq = pl.BlockSpec((D_MODEL, D_MODEL), lambda i, j, k: (0, 0))
    do, dob = _mm("mla_do", (t // tm, 1, 1), dh, row, wmo, sq, (1, 1),
                  [(_sds((t, D_MODEL), F32), row), (_sds((t, D_MODEL), BF16), row)], epi=lambda acc: (acc, acc),
                  deps=deps)
    (dwmo,) = _mm("mla_dwo", (1, 1, t // tk), o, rowk, dh, rowk, (0, 0), [(_sds((D_MODEL, D_MODEL), BF16), sq)])
    delta = _attn_delta(do, o, lse.shape[-1])
    dqt, dkh, dvh = _attn_bwd(qh, kh, vh, dob, lse, delta)
    dq, dkv, dkr, dgq, dgk = _mla_prep_bwd(cq, ckv, wuq, wukv, proj2, gq, gk, tabs, dqt, dkh, dvh)

    wide = MLA_HEADS * MLA_HD_PAD
    widek = pl.BlockSpec((tk, wide), lambda i, j, k: (k, 0))
    (dwuq,) = _mm("mla_dwuq", (1, 1, t // tk),
                  cq, pl.BlockSpec((tk, MLA_Q_RANK), lambda i, j, k: (k, 0)), dq, widek, (0, 0),
                  [(_sds((MLA_HEADS, MLA_Q_RANK, MLA_HD_PAD), BF16),
                    pl.BlockSpec((MLA_HEADS, MLA_Q_RANK, MLA_HD_PAD), lambda i, j, k: (0, 0, 0)))], split=MLA_HD_PAD)
    (dwukv,) = _mm("mla_dwukv", (1, 1, t // tk),
                   ckv, pl.BlockSpec((tk, MLA_KV_RANK), lambda i, j, k: (k, 0)), dkv, widek, (0, 0),
                   [(_sds((MLA_HEADS, MLA_KV_RANK, MLA_HD_PAD), BF16),
                     pl.BlockSpec((MLA_HEADS, MLA_KV_RANK, MLA_HD_PAD), lambda i, j, k: (0, 0, 0)))],
                   split=MLA_HD_PAD)
    side_by_side = lambda wg: wg.transpose(1, 0, 2).reshape(wg.shape[1], wide)
    widei = pl.BlockSpec((tm, wide), lambda i, j, k: (i, 0))
    (dcq,) = _mm("mla_dcq", (t // tm, 1, 1), dq, widei,
                 side_by_side(wuq), pl.BlockSpec((MLA_Q_RANK, wide), lambda i, j, k: (0, 0)), (1, 1),
                 [(_sds((t, MLA_Q_RANK), F32), pl.BlockSpec((tm, MLA_Q_RANK), lambda i, j, k: (i, 0)))])
    (dckv,) = _mm("mla_dckv", (t // tm, 1, 1), dkv, widei,
                  side_by_side(wukv), pl.BlockSpec((MLA_KV_RANK, wide), lambda i, j, k: (0, 0)), (1, 1),
                  [(_sds((t, MLA_KV_RANK), F32), pl.BlockSpec((tm, MLA_KV_RANK), lambda i, j, k: (i, 0)))])
    dproj2, dqa, dkva = _mla_mid_bwd(proj2, qa, kva, dcq, dckv, dkr)
    win = pl.BlockSpec((D_MODEL, MLA_IN_PAD), lambda i, j, k: (0, 0))
    (dwmi,) = _mm("mla_dwin", (1, 1, t // tk), hn, rowk,
                  dproj2, pl.BlockSpec((tk, MLA_IN_PAD), lambda i, j, k: (k, 0)), (0, 0),
                  [(_sds((D_MODEL, MLA_IN_PAD), BF16), win)])
    dh_in, dg = _mm("mla_dhn", (t // tm, 1, 1),
                    dproj2, pl.BlockSpec((tm, MLA_IN_PAD), lambda i, j, k: (i, 0)), wmi, win, (1, 1),
                    _norm_bwd_outs(t, tm), extras=[(h, row), (dh, row), (norm_g, VEC)], epi=_norm_bwd_epi)
    return dh_in, dict(mix=jnp.sum(dg, axis=0), wmi=dwmi, qa=dqa, kva=dkva, wuq=dwuq, wukv=dwukv, gq=dgq, gk=dgk,
                       wmo=dwmo)


def _local_step(x, p, target, w, fetch, emit=lambda group: ()):
    t = x.shape[0]
    inv = 1.0 / (ROPE_THETA ** (jnp.arange(0, RET_DK, 2, dtype=F32) / RET_DK))
    ang = jnp.arange(t, dtype=F32)[:, None] * inv[None, :]
    cos_r, sin_r = jnp.cos(ang), jnp.sin(ang)
    tabs = _mla_tables(t)
    row = lambda a, i: a[i:i + 1]

    h1, hn1, s_ret = _ret_layer_fwd(x, row(w['mix_norm'], 0), w['ret_w_in'],
                                    lambda after: fetch('ret_out', after)['ret_w_out'], w['ret_gn'], cos_r, sin_r,
                                    row(w['mlp_norm'], 0), deps=w['deps'])
    h2, hn2, s_mlp0 = _mlp_fwd(0, h1, hn1, fetch('mlp_w1_0', (h1,))['mlp_w1'],
                               lambda after: fetch('mlp_w2_0', after)['mlp_w2'], row(w['ple_norm'], 0))
    w0 = fetch('ple_0', (h2,))
    h3, hn3, s_ple0 = _ple_fwd(0, h2, hn2, p, w0['ple_gate_w'], w0['ple_proj_w'], row(w['mix_norm'], 1))
    wm = fetch('mla', (h3,))
    mla_w = (wm['mla_w_in'], w['mla_q_a_norm'], w['mla_kv_a_norm'], wm['mla_w_uq'], wm['mla_w_ukv'],
             w['mla_q_norm'], w['mla_k_norm'], wm['mla_w_out'], tabs)
    h4, hn4, s_mla = _mla_layer_fwd(h3, hn3, *mla_w, row(w['mlp_norm'], 1))
    w1 = fetch('layer_1', (h4,))
    h5, hn5, s_mlp1 = _mlp_fwd(1, h4, hn4, w1['mlp_w1'], lambda after: w1['mlp_w2'], row(w['ple_norm'], 1))
    dy, sq_err, s_ple1 = _ple_fwd(1, h5, hn5, p, w1['ple_gate_w'], w1['ple_proj_w'], None, target)

    n = N_DEV
    colsh = lambda a: a.reshape(a.shape[0], n, a.shape[1] // n).transpose(1, 0, 2)
    rowsh = lambda a: a.reshape(n, a.shape[0] // n, a.shape[1])
    big = {}

    def emit_group(group):
        big.update(group)
        return emit(group)

    dh5, dg_ple1, dwg1, dwp1 = _ple_bwd(1, dy, s_ple1, p, row(w['ple_norm'], 1), w1['ple_gate_w'])
    dh4, dg_mlp1, dw1_1, dw2_1 = _mlp_bwd(1, dh5, s_mlp1, row(w['mlp_norm'], 1))
    deps = emit_group({('ple_gate_w', 1): rowsh(dwg1), ('ple_proj_w', 1): colsh(dwp1),
                       ('mlp_w2', 1): dw2_1, ('mlp_w1', 1): dw1_1})
    dh3, gm = _mla_layer_bwd(dh4, s_mla, row(w['mix_norm'], 1), *mla_w, deps=deps)
    deps = emit_group({('mla_w_out', 0): rowsh(gm['wmo']), ('mla_w_uq', 0): gm['wuq'][:, :, :MLA_QKD],
                       ('mla_w_ukv', 0): gm['wukv'], ('mla_w_in', 0): rowsh(gm['wmi'][:, :MLA_IN])})
    dh2, dg_ple0, dwg0, dwp0 = _ple_bwd(0, dh3, s_ple0, p, row(w['ple_norm'], 0), w0['ple_gate_w'], deps=deps)
    dh1, dg_mlp0, dw1_0, dw2_0 = _mlp_bwd(0, dh2, s_mlp0, row(w['mlp_norm'], 0))
    deps = emit_group({('ple_gate_w', 0): rowsh(dwg0), ('ple_proj_w', 0): colsh(dwp0),
                       ('mlp_w2', 0): dw2_0, ('mlp_w1', 0): dw1_0})
    dx, dg_mix0, dgn = _ret_layer_bwd(
        dh1, s_ret, row(w['mix_norm'], 0), w['ret_w_in'], w['ret_gn'], cos_r, sin_r,
        lambda dwro, dwri: emit_group({('ret_w_out', 0): rowsh(dwro), ('ret_w_in', 0): dwri}), deps=deps)

    small = dict(
        mix_norm=[dg_mix0, gm['mix']], mlp_norm=[dg_mlp0, dg_mlp1], ple_norm=[dg_ple0, dg_ple1],
        ret_gn=dgn, mla_q_a_norm=gm['qa'], mla_kv_a_norm=gm['kva'], mla_q_norm=gm['gq'], mla_k_norm=gm['gk'],
    )
    return sq_err, dx, big, small


def _my_place():
    x, y, c = lax.axis_index("x"), lax.axis_index("y"), lax.axis_index("c")
    return x, y, c


def _flat(px, py, pc):
    return 4 * px + 2 * py + pc


def _peer(x, y, c, r):
    return (1 - x if r & 4 else x, 1 - y if r & 2 else y, 1 - c if r & 1 else c)


def _all_gather(arrays):
    n = len(arrays)

    def body(*refs):
        ins, outs = refs[:n], refs[n:2 * n]
        send_sems, recv_sems, local_sems = refs[2 * n:]
        x, y, c = _my_place()
        me, sibling = (x, y, c), (x, y, 1 - c)
        chips = [(1 - x, y), (x, 1 - y), (1 - x, 1 - y)]

        def copy(a, k, block, to, src=None):
            slot = outs[a].at[_flat(*block)]
            return pltpu.make_async_remote_copy(
                src_ref=slot if src is None else src, dst_ref=slot,
                send_sem=send_sems.at[a, k], recv_sem=recv_sems.at[a, k], device_id=to, device_id_type=MESH)

        mine = [pltpu.make_async_copy(ins[a], outs[a].at[_flat(*me)], local_sems.at[a]) for a in range(n)]
        for cp in mine:
            cp.start()
        first = []
        for a in range(n):
            first.append(copy(a, 0, me, sibling, src=ins[a]))
            first += [copy(a, 1 + j, me, (*chip, c), src=ins[a]) for j, chip in enumerate(chips)]
        for cp in first:
            cp.start()
        passed = []
        for a in range(n):
            for j, chip in enumerate(chips):
                copy(a, 1 + j, (*chip, c), me).wait_recv()
                passed.append(copy(a, 4 + j, (*chip, c), sibling))
                passed[-1].start()
        for a in range(n):
            copy(a, 0, sibling, me).wait_recv()
            for j, chip in enumerate(chips):
                copy(a, 4 + j, (*chip, 1 - c), me).wait_recv()
        for cp in first + passed:
            cp.wait_send()
        for cp in mine:
            cp.wait()

    return pl.pallas_call(
        body, name="all_gather_weights",
        in_specs=[ANY] * n, out_specs=[ANY] * n,
        out_shape=[_sds((N_DEV,) + a.shape, a.dtype) for a in arrays],
        scratch_shapes=[pltpu.SemaphoreType.DMA((n, 7)), pltpu.SemaphoreType.DMA((n, 7)),
                        pltpu.SemaphoreType.DMA((n,))],
    )(*arrays)


HBM = pl.BlockSpec(memory_space=pltpu.HBM)
SEMS = pl.BlockSpec(memory_space=pltpu.SEMAPHORE)
SIDE_EFFECT = pltpu.SideEffectType.DATAFLOW_SIDE_EFFECTING


def _rs_copies(x, y, c, srcs, lands, send_sems, recv_sems):
    copies = []
    for a in range(len(srcs)):
        for r in range(1, N_DEV):
            peer = _peer(x, y, c, r)
            k = a * (N_DEV - 1) + r - 1
            copies.append(pltpu.make_async_remote_copy(
                src_ref=srcs[a].at[_flat(*peer)], dst_ref=lands[a].at[r - 1],
                send_sem=send_sems.at[k], recv_sem=recv_sems.at[k], device_id=peer, device_id_type=MESH))
    return copies


def _rs_start(name, arrays):
    n = len(arrays)
    hbm = lambda a: pltpu.with_memory_space_constraint(a, pltpu.HBM)
    lands = [hbm(lax.empty((N_DEV - 1,) + a.shape[1:], a.dtype)) for a in arrays]

    def body(*refs):
        srcs, lnd = refs[:n], refs[n:2 * n]
        send_sems, recv_sems = refs[2 * n], refs[2 * n + 1]
        token = refs[-1]
        for cp in _rs_copies(*_my_place(), srcs, lnd, send_sems, recv_sems):
            cp.start()
        token[...] = jnp.zeros_like(token)

    outs = pl.pallas_call(
        body, name=name,
        in_specs=[HBM] * (2 * n),
        out_specs=[SEMS, SEMS] + [HBM] * (2 * n) + [pl.BlockSpec(memory_space=pltpu.VMEM)],
        out_shape=[pltpu.SemaphoreType.DMA((n * (N_DEV - 1),)), pltpu.SemaphoreType.DMA((n * (N_DEV - 1),))]
        + [pltpu.HBM(a.shape, a.dtype) for a in arrays] + [pltpu.HBM(l.shape, l.dtype) for l in lands]
        + [_sds((8, 128), F32)],
        input_output_aliases={i: 2 + i for i in range(2 * n)},
        compiler_params=pltpu.CompilerParams(has_side_effects=SIDE_EFFECT),
    )(*[hbm(a) for a in arrays], *lands)
    return outs[0], outs[1], outs[2:2 + n], outs[2 + n:2 + 2 * n], outs[-1]


def _rs_wait(name, send_sems, recv_sems, srcs, lands, after):
    n = len(srcs)

    def body(*refs):
        src_refs, lnd = refs[:n], refs[n:2 * n]
        send, recv = refs[2 * n], refs[2 * n + 1]
        for cp in _rs_copies(*_my_place(), src_refs, lnd, send, recv):
            cp.wait_send()
            cp.wait_recv()

    outs = pl.pallas_call(
        body, name=name,
        in_specs=[HBM] * (2 * n) + [SEMS, SEMS] + [ANY] * len(after),
        out_specs=[HBM] * (2 * n),
        out_shape=[pltpu.HBM(a.shape, a.dtype) for a in list(srcs) + list(lands)],
        input_output_aliases={i: i for i in range(2 * n)},
        compiler_params=pltpu.CompilerParams(has_side_effects=SIDE_EFFECT),
    )(*srcs, *lands, send_sems, recv_sems, *after)
    return outs[:n], outs[n:]


SMALL_PACK_ROWS = 16


def _all_reduce_small(rows, deps=()):
    n = len(rows)

    def body(*refs):
        ins = refs[:n]
        out_ref, mine, buf, send_sems, recv_sems = refs[n + len(deps):]
        x, y, c = _my_place()
        mine[...] = jnp.zeros_like(mine)
        for (r0, a), ref in zip(rows, ins):
            mine[r0:r0 + a.shape[0], 0:a.shape[1]] = ref[...]
        buf[_flat(x, y, c)] = mine[...]
        copies = []
        for r in range(1, N_DEV):
            peer = _peer(x, y, c, r)
            send = pltpu.make_async_remote_copy(
                src_ref=mine, dst_ref=buf.at[_flat(x, y, c)],
                send_sem=send_sems.at[r - 1], recv_sem=recv_sems.at[r - 1], device_id=peer, device_id_type=MESH)
            send.start()
            recv = pltpu.make_async_remote_copy(
                src_ref=mine, dst_ref=buf.at[_flat(*peer)],
                send_sem=send_sems.at[r - 1], recv_sem=recv_sems.at[r - 1], device_id=peer, device_id_type=MESH)
            copies.append((send, recv))
        for send, recv in copies:
            send.wait_send()
            recv.wait_recv()
        acc = buf[0]
        for s in range(1, N_DEV):
            acc = acc + buf[s]
        out_ref[...] = acc

    vm = pl.BlockSpec(memory_space=pltpu.VMEM)
    shape = (SMALL_PACK_ROWS, D_MODEL)
    return pl.pallas_call(
        body, name="all_reduce_small", in_specs=[vm] * n + [ANY] * len(deps), out_specs=vm,
        out_shape=_sds(shape, F32),
        scratch_shapes=[pltpu.VMEM(shape, F32), pltpu.VMEM((N_DEV,) + shape, F32),
                        pltpu.SemaphoreType.DMA((7,)), pltpu.SemaphoreType.DMA((7,))],
    )(*[a for _, a in rows], *deps)


def _adamw_math(w, g, m, v):
    m = ADAM_B1 * m + (1.0 - ADAM_B1) * g
    v = ADAM_B2 * v + (1.0 - ADAM_B2) * (g * g)
    m_hat = m / (1.0 - ADAM_B1 ** ADAM_STEP)
    v_hat = v / (1.0 - ADAM_B2 ** ADAM_STEP)
    delta = -ADAM_LR * (m_hat / (jnp.sqrt(v_hat) + ADAM_EPS) + ADAM_WD * w)
    return delta, m, v


def _adamw_big(name, w, m, v, srcs, lands, me):
    nl, rows, cols = w.shape
    tr = next(cand for cand in (256, 128, 64, 32, 16, 8) if rows % cand == 0)

    def body(me_ref, w_ref, m_ref, v_ref, *rest):
        src_refs, land_refs = rest[:nl], rest[nl:2 * nl]
        g_ref, d_ref, mo_ref, vo_ref = rest[2 * nl:]
        for layer in range(nl):
            @pl.when(pl.program_id(0) == layer)
            def _():
                g = src_refs[layer][...].astype(F32)
                for s in range(N_DEV - 1):
                    g = g + land_refs[layer][s].astype(F32)
                delta, mn, vn = _adamw_math(w_ref[...], g, m_ref[...], v_ref[...])
                g_ref[...] = g
                d_ref[...] = delta
                mo_ref[...] = mn
                vo_ref[...] = vn

    blk = pl.BlockSpec((None, tr, cols), lambda l, i, me_ref: (l, i, 0))
    own = pl.BlockSpec((None, tr, cols), lambda l, i, me_ref: (me_ref[0], i, 0))
    peers = pl.BlockSpec((N_DEV - 1, tr, cols), lambda l, i, me_ref: (0, i, 0))
    return pl.pallas_call(
        body, name=name,
        grid_spec=pltpu.PrefetchScalarGridSpec(
            num_scalar_prefetch=1, grid=(nl, rows // tr),
            in_specs=[blk, blk, blk] + [own] * nl + [peers] * nl, out_specs=[blk] * 4),
        out_shape=[_sds((nl, rows, cols), F32)] * 4,
        compiler_params=_cparams(("arbitrary", "arbitrary")),
    )(me, w, m, v, *srcs, *lands)


def _adamw_small(ws, gs, ms, vs):
    n = len(ws)

    def body(*refs):
        w_refs, g_refs, m_refs, v_refs = (refs[i * n:(i + 1) * n] for i in range(4))
        d_out, m_out, v_out = (refs[(4 + i) * n:(5 + i) * n] for i in range(3))
        for i in range(n):
            delta, mn, vn = _adamw_math(w_refs[i][...], g_refs[i][...], m_refs[i][...], v_refs[i][...])
            d_out[i][...] = delta
            m_out[i][...] = mn
            v_out[i][...] = vn

    vm = pl.BlockSpec(memory_space=pltpu.VMEM)
    outs = pl.pallas_call(
        body, name="adamw_small", in_specs=[vm] * (4 * n), out_specs=[vm] * (3 * n),
        out_shape=[_sds(a.shape, F32) for a in ws] * 3,
    )(*ws, *gs, *ms, *vs)
    return outs[:n], outs[n:2 * n], outs[2 * n:]


SMALL_ROWS = 16


def _pad_to(a, rows, cols):
    return jnp.pad(a, ((0, rows - a.shape[0]), (0, cols - a.shape[1])))


def _place_own(blocks):
    me = _flat(*_my_place())
    return [lax.dynamic_update_slice(lax.empty((N_DEV,) + b.shape, b.dtype), b[None], (me,) + (0,) * b.ndim)
            for b in blocks]


def _ag_copies(x, y, c, blocks, bufs, send_sems, recv_sems):
    sends, recvs = [], []
    for a in range(len(blocks)):
        for r in range(1, N_DEV):
            peer = _peer(x, y, c, r)
            k = a * (N_DEV - 1) + r - 1
            make = lambda place: pltpu.make_async_remote_copy(
                src_ref=blocks[a], dst_ref=bufs[a].at[_flat(*place)],
                send_sem=send_sems.at[k], recv_sem=recv_sems.at[k], device_id=peer, device_id_type=MESH)
            sends.append(make((x, y, c)))
            recvs.append(make(peer))
    return sends, recvs


def _ag_start(groups, after):
    flat = [pair for g in groups for pair in g]
    n, ng = len(flat), len(groups)
    hbm = lambda a: pltpu.with_memory_space_constraint(a, pltpu.HBM)

    def body(*refs):
        blocks, bufs = refs[:n], refs[n:2 * n]
        sems = refs[2 * n + len(after):2 * n + len(after) + 2 * ng]
        x, y, c = _my_place()
        at = 0
        for gi, g in enumerate(groups):
            sends, _ = _ag_copies(x, y, c, blocks[at:at + len(g)], bufs[at:at + len(g)], sems[2 * gi], sems[2 * gi + 1])
            for cp in sends:
                cp.start()
            at += len(g)
        refs[-1][...] = jnp.zeros_like(refs[-1])

    sem_shapes = [pltpu.SemaphoreType.DMA((len(g) * (N_DEV - 1),)) for g in groups for _ in range(2)]
    outs = pl.pallas_call(
        body, name="gather_start",
        in_specs=[HBM] * (2 * n) + [ANY] * len(after),
        out_specs=[SEMS] * (2 * ng) + [HBM] * (2 * n) + [pl.BlockSpec(memory_space=pltpu.VMEM)],
        out_shape=sem_shapes + [pltpu.HBM(b.shape, b.dtype) for b, _ in flat]
        + [pltpu.HBM(u.shape, u.dtype) for _, u in flat] + [_sds((8, 128), F32)],
        input_output_aliases={i: 2 * ng + i for i in range(2 * n)},
        compiler_params=pltpu.CompilerParams(has_side_effects=SIDE_EFFECT),
    )(*[hbm(b) for b, _ in flat], *[hbm(u) for _, u in flat], *after)
    blocks_thru, bufs_thru = outs[2 * ng:2 * ng + n], outs[2 * ng + n:2 * ng + 2 * n]
    started, at = [], 0
    for gi, g in enumerate(groups):
        started.append((outs[2 * gi], outs[2 * gi + 1], blocks_thru[at:at + len(g)], bufs_thru[at:at + len(g)]))
        at += len(g)
    return started, outs[-1]


def _ag_wait(name, send_sems, recv_sems, blocks, bufs, after):
    n = len(blocks)

    def body(*refs):
        sends, recvs = _ag_copies(*_my_place(), refs[:n], refs[n:2 * n], refs[2 * n], refs[2 * n + 1])
        for s, r in zip(sends, recvs):
            s.wait_send()
            r.wait_recv()

    outs = pl.pallas_call(
        body, name=name,
        in_specs=[HBM] * (2 * n) + [SEMS, SEMS] + [ANY] * len(after),
        out_specs=[HBM] * (2 * n),
        out_shape=[pltpu.HBM(a.shape, a.dtype) for a in list(blocks) + list(bufs)],
        input_output_aliases={i: i for i in range(2 * n)},
        compiler_params=pltpu.CompilerParams(has_side_effects=SIDE_EFFECT),
    )(*blocks, *bufs, send_sems, recv_sems, *after)
    return outs[n:]


def _prepare_weights(p):
    n = N_DEV
    bf = lambda a: a.astype(BF16)
    gn_pack = jnp.concatenate([
        _pad_to(p['ret_gn'][0], RET_HEADS, 128), _pad_to(p['mla_q_a_norm'], 1, 128),
        _pad_to(p['mla_kv_a_norm'], 1, 128), jnp.zeros((2, 128), F32)], axis=0)
    ple = lambda l: [bf(p['ple_gate_w'][l]), bf(p['ple_proj_w'][l])]
    names = ('ret_out', 'mlp_w1_0', 'mlp_w2_0', 'ple_0', 'mla', 'layer_1')
    later = [[bf(p['ret_w_out'][0])], [bf(p['mlp_w1'][0])], [bf(p['mlp_w2'][0])], ple(0),
             [bf(p['mla_w_in'][0]), bf(p['mla_w_uq'][0]), bf(p['mla_w_ukv'][0]), bf(p['mla_w_out'][0])],
             [bf(p['mlp_w1'][1]), bf(p['mlp_w2'][1])] + ple(1)]
    bufs = _place_own([b for g in later for b in g])
    pack, wri = _all_gather([gn_pack, bf(p['ret_w_in'][0])])
    groups, at = [], 0
    for g in later:
        groups.append(list(zip(g, bufs[at:at + len(g)])))
        at += len(g)
    started, token = _ag_start(groups, (wri,))

    w = {k: p[k] for k in ('mix_norm', 'mlp_norm', 'ple_norm')}
    w['ret_gn'] = pack[:, :RET_HEADS, :RET_DV // n].transpose(1, 0, 2).reshape(RET_HEADS, RET_DV)
    w['mla_q_a_norm'] = pack[:, RET_HEADS, :MLA_Q_RANK // n].reshape(1, MLA_Q_RANK)
    w['mla_kv_a_norm'] = pack[:, RET_HEADS + 1, :MLA_KV_RANK // n].reshape(1, MLA_KV_RANK)
    w['ret_w_in'] = wri
    w['mla_q_norm'] = _pad_to(p['mla_q_norm'], 1, MLA_HD_PAD)
    w['mla_k_norm'] = _pad_to(p['mla_k_norm'], 1, MLA_HD_PAD)
    w['deps'] = (token,)

    def fetch(name, after):
        got = list(_ag_wait("gather_wait_" + name, *started[names.index(name)], after))
        if name == 'ret_out':
            return dict(ret_w_out=got[0].reshape(RET_V_W, D_MODEL))
        if name == 'mla':
            wmi, wuq, wukv, wmo = got
            return dict(mla_w_in=jnp.pad(wmi.reshape(D_MODEL, MLA_IN), ((0, 0), (0, MLA_IN_PAD - MLA_IN))),
                        mla_w_uq=jnp.pad(wuq, ((0, 0), (0, 0), (0, MLA_HD_PAD - MLA_QKD))),
                        mla_w_ukv=wukv, mla_w_out=wmo.reshape(D_MODEL, D_MODEL))
        out = {}
        if name in ('mlp_w1_0', 'layer_1'):
            out['mlp_w1'] = got.pop(0)
        if name in ('mlp_w2_0', 'layer_1'):
            out['mlp_w2'] = got.pop(0)
        if name in ('ple_0', 'layer_1'):
            out['ple_gate_w'] = got[0].reshape(D_MODEL, D_MODEL)
            out['ple_proj_w'] = got[1].transpose(1, 0, 2).reshape(PLE_DIM, D_MODEL)
        return out

    return w, fetch


def _small_grads(small, after):
    rows = [(0, small['mix_norm'][0]), (1, small['mix_norm'][1]), (2, small['mlp_norm'][0]),
            (3, small['mlp_norm'][1]), (4, small['ple_norm'][0]), (5, small['ple_norm'][1]),
            (6, small['ret_gn']), (10, small['mla_q_a_norm']), (11, small['mla_kv_a_norm']),
            (12, small['mla_q_norm']), (13, small['mla_k_norm'])]
    gs = _all_reduce_small(rows, after)
    me = _flat(*_my_place())
    n = N_DEV
    return dict(
        mix_norm=gs[0:2], mlp_norm=gs[2:4], ple_norm=gs[4:6],
        ret_gn=lax.dynamic_slice(gs, (6, me * (RET_DV // n)), (RET_HEADS, RET_DV // n)),
        mla_q_a_norm=lax.dynamic_slice(gs, (10, me * (MLA_Q_RANK // n)), (1, MLA_Q_RANK // n)),
        mla_kv_a_norm=lax.dynamic_slice(gs, (11, me * (MLA_KV_RANK // n)), (1, MLA_KV_RANK // n)),
        mla_q_norm=gs[12:13, :MLA_QKD], mla_k_norm=gs[13:14, :MLA_QKD])


def kernel(x, p, mix_norm, ret_w_in, ret_gn, ret_w_out, mla_w_in, mla_q_a_norm, mla_kv_a_norm, mla_w_uq, mla_w_ukv, mla_q_norm, mla_k_norm, mla_w_out, mlp_norm, mlp_w1, mlp_w2, ple_norm, ple_gate_w, ple_proj_w, loss_target, m_mix_norm, m_ret_w_in, m_ret_gn, m_ret_w_out, m_mla_w_in, m_mla_q_a_norm, m_mla_kv_a_norm, m_mla_w_uq, m_mla_w_ukv, m_mla_q_norm, m_mla_k_norm, m_mla_w_out, m_mlp_norm, m_mlp_w1, m_mlp_w2, m_ple_norm, m_ple_gate_w, m_ple_proj_w, v_mix_norm, v_ret_w_in, v_ret_gn, v_ret_w_out, v_mla_w_in, v_mla_q_a_norm, v_mla_kv_a_norm, v_mla_w_uq, v_mla_w_ukv, v_mla_q_norm, v_mla_k_norm, v_mla_w_out, v_mlp_norm, v_mlp_w1, v_mlp_w2, v_ple_norm, v_ple_gate_w, v_ple_proj_w):
    given = dict(locals())
    params = {n: given[n] for n in WEIGHTS}
    w, fetch = _prepare_weights(params)

    started = []

    def emit(group):
        keys = list(group)
        send, recv, srcs, lands, token = _rs_start(f"rs_start{len(started)}", [group[k] for k in keys])
        started.append((keys, send, recv, srcs, lands))
        return (token,)

    sq_err, grad_x, _, small = _local_step(x[0], p, loss_target[0], w, fetch, emit)
    loss = lax.psum(0.5 / D_MODEL * sq_err[0, 0], ("x", "y", "c"))

    grads, deltas, new_m, new_v = {}, {}, {}, {}

    def small_updates(after):
        sg = _small_grads(small, after)
        two_d = lambda a: a.reshape(-1, a.shape[-1])
        d_s, m_s, v_s = _adamw_small(
            [two_d(params[n]) for n in SMALL], [sg[n] for n in SMALL],
            [two_d(given["m_" + n]) for n in SMALL], [two_d(given["v_" + n]) for n in SMALL])
        for i, n in enumerate(SMALL):
            shape = params[n].shape
            grads[n], deltas[n], new_m[n], new_v[n] = (a.reshape(shape) for a in (sg[n], d_s[i], m_s[i], v_s[i]))
        return (d_s[0],)

    me = _flat(*_my_place()).astype(jnp.int32).reshape(1)
    after = (grad_x,)
    src_of, land_of = {}, {}
    for gi, (keys, send, recv, srcs, lands) in enumerate(started):
        if gi == len(started) - 1:
            after = small_updates(after)
        srcs, lands = _rs_wait(f"rs_wait{gi}", send, recv, srcs, lands, after)
        for k, s, l in zip(keys, srcs, lands):
            src_of[k], land_of[k] = s, l
        done = [n for n in BIG if n not in grads and all((n, l) in src_of for l in range(params[n].shape[0]))]
        for n in done:
            layers = range(params[n].shape[0])
            grads[n], deltas[n], new_m[n], new_v[n] = _adamw_big(
                "adamw_" + n, params[n], given["m_" + n], given["v_" + n],
                [src_of[(n, l)] for l in layers], [land_of[(n, l)] for l in layers], me)
        if done:
            after = tuple(deltas[n] for n in done)

    return (loss, grad_x[None], *[grads[n] for n in WEIGHTS], *[deltas[n] for n in WEIGHTS],
            *[new_m[n] for n in WEIGHTS], *[new_v[n] for n in WEIGHTS])
```

```python
import functools
import math

import jax
import jax.numpy as jnp
from jax import lax
from jax.experimental import pallas as pl
from jax.experimental.pallas import tpu as pltpu

F32 = jnp.float32
BF16 = jnp.bfloat16
MESH = pl.DeviceIdType.MESH
ANY = pl.BlockSpec(memory_space=pl.ANY)

N_DEV = 8
D_MODEL = 1024
CHUNK = 64
EPS = 1e-6
ROPE_THETA = 10000.0
RET_HEADS = 4
RET_DK = 256
RET_DV = 512
RET_QK_W = RET_HEADS * RET_DK
RET_V_W = RET_HEADS * RET_DV
RET_IN = 2 * RET_QK_W + 2 * RET_V_W
MLA_HEADS = 8
MLA_NOPE = 128
MLA_ROPE = 64
MLA_QKD = MLA_NOPE + MLA_ROPE
MLA_VD = 128
MLA_Q_RANK = 384
MLA_KV_RANK = 256
MLA_IN = MLA_Q_RANK + MLA_KV_RANK + MLA_ROPE
MLA_IN_PAD = 768
MLA_HD_PAD = 256
D_FF = 4096
PLE_DIM = 256
ATT_SCALE = MLA_QKD ** -0.5
LOG2E = 1.4426950408889634
ATT_EXP2 = ATT_SCALE * LOG2E

ADAM_LR = 0.001
ADAM_B1 = 0.9
ADAM_B2 = 0.999
ADAM_EPS = 1e-08
ADAM_WD = 0.01
ADAM_STEP = 10

VMEM_LIMIT = 52 * 1024 * 1024
ROW_TILE = 1024
RET_ROWS = 256
ATT_BLOCK = 256
ATT_QROWS = 1024
ATT_KROWS = 1024
ATT_HEADS = 2

WEIGHTS = ['mix_norm', 'ret_w_in', 'ret_gn', 'ret_w_out', 'mla_w_in', 'mla_q_a_norm', 'mla_kv_a_norm',
           'mla_w_uq', 'mla_w_ukv', 'mla_q_norm', 'mla_k_norm', 'mla_w_out', 'mlp_norm', 'mlp_w1', 'mlp_w2',
           'ple_norm', 'ple_gate_w', 'ple_proj_w']
BIG = ['ret_w_in', 'ret_w_out', 'mla_w_in', 'mla_w_uq', 'mla_w_ukv', 'mla_w_out', 'mlp_w1', 'mlp_w2',
       'ple_gate_w', 'ple_proj_w']
SMALL = [w for w in WEIGHTS if w not in BIG]


def _cparams(sem=None):
    return pltpu.CompilerParams(dimension_semantics=sem, vmem_limit_bytes=VMEM_LIMIT)


def _dot(a, b, ca, cb):
    return lax.dot_general(a, b, (((ca,), (cb,)), ((), ())), preferred_element_type=F32)


def _bf(v):
    return v if v.dtype == BF16 else v.astype(BF16)


def _sigmoid(z):
    return 1.0 / (1.0 + jnp.exp(-z))


def _mm(name, grid, a, a_spec, b, b_spec, contract, outs, extras=(), epi=None, deps=(), split=None):
    nk = grid[2]
    n_ex, n_out, n_dep = len(extras), len(outs), len(deps)
    acc_shape = tuple(d for d in outs[0][1].block_shape if d is not None)
    if split is not None:
        acc_shape = (acc_shape[1], acc_shape[0] * split)

    def body(*refs):
        a_ref, b_ref = refs[:2]
        ex_refs = refs[2:2 + n_ex]
        out_refs = refs[2 + n_ex + n_dep:2 + n_ex + n_dep + n_out]

        def product():
            return _dot(_bf(a_ref[...]), _bf(b_ref[...]), contract[0], contract[1])

        def finish(acc):
            if split is not None:
                for j in range(acc_shape[1] // split):
                    out_refs[0][j] = acc[:, j * split:(j + 1) * split].astype(out_refs[0].dtype)
                return
            acc = acc[...]
            res = epi(acc, *[r[...] for r in ex_refs]) if epi is not None else (acc,)
            for o, r in zip(out_refs, res):
                o[...] = r.astype(o.dtype)

        if nk == 1:
            finish(product())
        else:
            acc_ref = refs[-1]
            k = pl.program_id(2)

            @pl.when(k == 0)
            def _():
                acc_ref[...] = jnp.zeros_like(acc_ref)

            acc_ref[...] += product()

            @pl.when(k == nk - 1)
            def _():
                finish(acc_ref)

    return pl.pallas_call(
        body, name=name, grid=grid,
        in_specs=[a_spec, b_spec] + [s for _, s in extras] + [ANY] * n_dep,
        out_specs=[s for _, s in outs],
        out_shape=[s for s, _ in outs],
        scratch_shapes=[pltpu.VMEM(acc_shape, F32)] if nk > 1 else [],
        compiler_params=_cparams(("parallel", "parallel", "arbitrary")),
    )(a, b, *[x for x, _ in extras], *deps)


def _mm_rows(name, tm, a, w, mode, outs, extras=(), epi=None, deps=()):
    n_sh, rows, cols = w.shape
    n_ex, n_out, n_dep = len(extras), len(outs), len(deps)
    by_cols = mode in ('nn_cols', 'nt_rows')
    width = cols if mode == 'nn_cols' else rows

    def body(*refs):
        a_ref, w_ref = refs[:2]
        ex_refs = refs[2:2 + n_ex]
        out_refs = refs[2 + n_ex + n_dep:2 + n_ex + n_dep + n_out]
        if by_cols:
            av = _bf(a_ref[...])
            for s in range(n_sh):
                cs = slice(s * width, (s + 1) * width)
                acc = _dot(av, w_ref[s], 1, 0 if mode == 'nn_cols' else 1)
                res = epi(acc, *[r[:, cs] for r in ex_refs]) if epi is not None else (acc,)
                for o, r in zip(out_refs, res):
                    o[:, cs] = r.astype(o.dtype)
        else:
            chunk = rows if mode == 'nn_rows' else cols
            acc = None
            for s in range(n_sh):
                part = _dot(_bf(a_ref[:, s * chunk:(s + 1) * chunk]), w_ref[s], 1, 0 if mode == 'nn_rows' else 1)
                acc = part if acc is None else acc + part
            res = epi(acc, *[r[...] for r in ex_refs]) if epi is not None else (acc,)
            for o, r in zip(out_refs, res):
                o[...] = r.astype(o.dtype)

    t, ka = a.shape
    return pl.pallas_call(
        body, name=name, grid=(t // tm, 1, 1),
        in_specs=[pl.BlockSpec((tm, ka), lambda i, j, k: (i, 0)),
                  pl.BlockSpec((n_sh, rows, cols), lambda i, j, k: (0, 0, 0))] + [s for _, s in extras] + [ANY] * n_dep,
        out_specs=[s for _, s in outs],
        out_shape=[s for s, _ in outs],
        compiler_params=_cparams(("parallel", "arbitrary", "arbitrary")),
    )(a, w, *[x for x, _ in extras], *deps)


def _sds(shape, dtype):
    return jax.ShapeDtypeStruct(shape, dtype)


def _row_tile(t, cap=ROW_TILE):
    return min(cap, t)


def _rms_fwd(name, x, g):
    t, d = x.shape
    tm = _row_tile(t)

    def body(x_ref, g_ref, o_ref):
        xv = x_ref[...]
        r = lax.rsqrt(jnp.mean(xv * xv, axis=-1, keepdims=True) + EPS)
        o_ref[...] = (xv * r * g_ref[...]).astype(o_ref.dtype)

    return pl.pallas_call(
        body, name=name, grid=(t // tm,),
        in_specs=[pl.BlockSpec((tm, d), lambda i: (i, 0)), pl.BlockSpec((1, d), lambda i: (0, 0))],
        out_specs=pl.BlockSpec((tm, d), lambda i: (i, 0)),
        out_shape=_sds((t, d), BF16),
        compiler_params=_cparams(("parallel",)),
    )(x, g)


def _rms_bwd_rows(dy, xv, g, n):
    r = lax.rsqrt(jnp.sum(xv * xv, axis=-1, keepdims=True) / n + EPS)
    xh = xv * r
    dxh = dy * g
    dx = r * (dxh - xh * (jnp.sum(dxh * xh, axis=-1, keepdims=True) / n))
    return dx, dy * xh


def _rms_bwd(name, dy, x, g, res):
    t, d = x.shape
    tm = _row_tile(t, 512)

    def body(dy_ref, x_ref, g_ref, res_ref, dx_ref, dg_ref):
        @pl.when(pl.program_id(0) == 0)
        def _():
            dg_ref[...] = jnp.zeros_like(dg_ref)

        dx, dgr = _rms_bwd_rows(dy_ref[...], x_ref[...], g_ref[...], d)
        dx_ref[...] = res_ref[...] + dx
        dg_ref[...] += jnp.sum(dgr, axis=0, keepdims=True)

    row = pl.BlockSpec((tm, d), lambda i: (i, 0))
    vec = pl.BlockSpec((1, d), lambda i: (0, 0))
    return pl.pallas_call(
        body, name=name, grid=(t // tm,),
        in_specs=[row, row, vec, row], out_specs=[row, vec],
        out_shape=[_sds((t, d), F32), _sds((1, d), F32)],
        compiler_params=_cparams(("arbitrary",)),
    )(dy, x, g, res)


def _loss_head(y, target):
    t, d = y.shape
    tm = _row_tile(t)

    def body(y_ref, t_ref, dy_ref, l_ref):
        @pl.when(pl.program_id(0) == 0)
        def _():
            l_ref[...] = jnp.zeros_like(l_ref)

        e = y_ref[...] - t_ref[...]
        dy_ref[...] = e / d
        l_ref[...] += jnp.sum(jnp.sum(e * e, axis=-1, keepdims=True), axis=0, keepdims=True)

    row = pl.BlockSpec((tm, d), lambda i: (i, 0))
    return pl.pallas_call(
        body, name="loss_head", grid=(t // tm,),
        in_specs=[row, row], out_specs=[row, pl.BlockSpec((8, 128), lambda i: (0, 0))],
        out_shape=[_sds((t, d), F32), _sds((8, 128), F32)],
        compiler_params=_cparams(("arbitrary",)),
    )(y, target)


def _ple_gate_bwd(name, dh, gate, e):
    t, d = dh.shape
    tm = _row_tile(t)

    def body(dh_ref, g_ref, e_ref, de_ref, dz_ref):
        dh_v, gt = dh_ref[...], g_ref[...]
        de_ref[...] = (dh_v * gt).astype(BF16)
        dz_ref[...] = (dh_v * e_ref[...] * (gt * (1.0 - gt))).astype(BF16)

    row = pl.BlockSpec((tm, d), lambda i: (i, 0))
    return pl.pallas_call(
        body, name=name, grid=(t // tm,), in_specs=[row, row, row], out_specs=[row, row],
        out_shape=[_sds((t, d), BF16), _sds((t, d), BF16)],
        compiler_params=_cparams(("parallel",)),
    )(dh, gate, e)


def _rope_half(v, cos, sin):
    half = v.shape[-1] // 2
    v1, v2 = v[:, :half], v[:, half:]
    return jnp.concatenate([v1 * cos - v2 * sin, v2 * cos + v1 * sin], axis=-1)


def _ret_consts():
    lg = jnp.log(1.0 - 2.0 ** (-5.0 - jnp.arange(RET_HEADS, dtype=F32)))
    idx = jnp.arange(CHUNK, dtype=F32)
    intra = jnp.exp(lg[:, None, None] * jnp.abs(idx[:, None] - idx[None, :]))
    qdec = jnp.exp(lg[:, None] * (idx + 1.0))
    kdec = jnp.exp(lg[:, None] * (CHUNK - 1.0 - idx))
    cdec = jnp.exp(lg * CHUNK)
    qdec = jnp.broadcast_to(qdec[:, :, None], (RET_HEADS, CHUNK, RET_DK))
    kdec = jnp.broadcast_to(kdec[:, :, None], (RET_HEADS, CHUNK, RET_DK))
    cdec = jnp.broadcast_to(cdec[:, None, None], (RET_HEADS, 1, RET_DV))
    return intra, qdec, kdec, cdec


def _ret_specs(rb, rev_nb=None):
    blk = (lambda i: i) if rev_nb is None else (lambda i: rev_nb - 1 - i)
    full = lambda shape: pl.BlockSpec(shape, lambda i: (0,) * len(shape))
    return dict(
        proj=pl.BlockSpec((rb, RET_IN), lambda i: (blk(i), 0)),
        tab=pl.BlockSpec((rb, RET_DK // 2), lambda i: (blk(i), 0)),
        vw=pl.BlockSpec((rb, RET_V_W), lambda i: (blk(i), 0)),
        st=pl.BlockSpec((rb // CHUNK, RET_HEADS, RET_DK, RET_DV), lambda i: (blk(i), 0, 0, 0)),
        gn=full((RET_HEADS, 1, RET_DV)),
        intra=full((RET_HEADS, CHUNK, CHUNK)),
        dec=full((RET_HEADS, CHUNK, RET_DK)),
        cdec=full((RET_HEADS, 1, RET_DV)),
    )


def _ret_fwd(proj, cos, sin, gn):
    t = proj.shape[0]
    rb = min(RET_ROWS, t)
    cpb = rb // CHUNK
    intra, qdec, kdec, cdec = _ret_consts()
    sp = _ret_specs(rb)

    def body(proj_ref, cos_ref, sin_ref, gn_ref, intra_ref, qd_ref, kd_ref, cd_ref,
             gated_ref, outp_ref, st_ref, s_ref):
        @pl.when(pl.program_id(0) == 0)
        def _():
            s_ref[...] = jnp.zeros_like(s_ref)

        def chunk(c, carry):
            rows = pl.ds(pl.multiple_of(c * CHUNK, CHUNK), CHUNK)
            cs, sn = cos_ref[rows, :], sin_ref[rows, :]
            for h in range(RET_HEADS):
                q = proj_ref[rows, h * RET_DK:(h + 1) * RET_DK].astype(F32)
                k = proj_ref[rows, RET_QK_W + h * RET_DK:RET_QK_W + (h + 1) * RET_DK].astype(F32)
                v = proj_ref[rows, 2 * RET_QK_W + h * RET_DV:2 * RET_QK_W + (h + 1) * RET_DV]
                g = proj_ref[rows, 2 * RET_QK_W + RET_V_W + h * RET_DV:
                             2 * RET_QK_W + RET_V_W + (h + 1) * RET_DV].astype(F32)
                qr = _rope_half(q, cs, sn)
                kr = _rope_half(k, cs, sn) * (RET_DK ** -0.5)
                qb, kb, vb = qr.astype(BF16), kr.astype(BF16), v
                sc = _dot(qb, kb, 1, 1) * intra_ref[h]
                inner = _dot(sc.astype(BF16), vb, 1, 0)
                s_old = s_ref[h]
                sb = s_old.astype(BF16)
                st_ref[c, h] = sb
                cross = _dot((qr * qd_ref[h]).astype(BF16), sb, 1, 0)
                out = inner + cross
                s_ref[h] = s_old * cd_ref[h] + _dot((kr * kd_ref[h]).astype(BF16), vb, 0, 0)
                r = lax.rsqrt(jnp.mean(out * out, axis=-1, keepdims=True) + EPS)
                y = out * r * gn_ref[h]
                cols = slice(h * RET_DV, (h + 1) * RET_DV)
                gated_ref[rows, cols] = (g * _sigmoid(g) * y).astype(BF16)
                outp_ref[rows, cols] = out
            return carry

        lax.fori_loop(0, cpb, chunk, 0)

    return pl.pallas_call(
        body, name="ret_fwd", grid=(t // rb,),
        in_specs=[sp['proj'], sp['tab'], sp['tab'], sp['gn'], sp['intra'], sp['dec'], sp['dec'], sp['cdec']],
        out_specs=[sp['vw'], sp['vw'], sp['st']],
        out_shape=[_sds((t, RET_V_W), BF16), _sds((t, RET_V_W), F32),
                   _sds((t // CHUNK, RET_HEADS, RET_DK, RET_DV), BF16)],
        scratch_shapes=[pltpu.VMEM((RET_HEADS, RET_DK, RET_DV), F32)],
        compiler_params=_cparams(("arbitrary",)),
    )(proj, cos, sin, gn.reshape(RET_HEADS, 1, RET_DV), intra, qdec, kdec, cdec)


def _ret_bwd(proj, cos, sin, gn, outp, states, dgated, deps=()):
    t = proj.shape[0]
    rb = min(RET_ROWS, t)
    cpb = rb // CHUNK
    nb = t // rb
    intra, qdec, kdec, cdec = _ret_consts()
    sp = _ret_specs(rb, rev_nb=nb)

    def body(proj_ref, cos_ref, sin_ref, gn_ref, intra_ref, qd_ref, kd_ref, cd_ref, outp_ref, st_ref, dgt_ref, *rest):
        dproj_ref, dgn_ref, ds_ref = rest[len(deps):]
        @pl.when(pl.program_id(0) == 0)
        def _():
            ds_ref[...] = jnp.zeros_like(ds_ref)
            dgn_ref[...] = jnp.zeros_like(dgn_ref)

        def chunk(cc, carry):
            c = cpb - 1 - cc
            rows = pl.ds(pl.multiple_of(c * CHUNK, CHUNK), CHUNK)
            cs, sn = cos_ref[rows, :], sin_ref[rows, :]
            for h in range(RET_HEADS):
                q = proj_ref[rows, h * RET_DK:(h + 1) * RET_DK].astype(F32)
                k = proj_ref[rows, RET_QK_W + h * RET_DK:RET_QK_W + (h + 1) * RET_DK].astype(F32)
                v = proj_ref[rows, 2 * RET_QK_W + h * RET_DV:2 * RET_QK_W + (h + 1) * RET_DV]
                g = proj_ref[rows, 2 * RET_QK_W + RET_V_W + h * RET_DV:
                             2 * RET_QK_W + RET_V_W + (h + 1) * RET_DV].astype(F32)
                cols = slice(h * RET_DV, (h + 1) * RET_DV)
                qr = _rope_half(q, cs, sn)
                kr = _rope_half(k, cs, sn) * (RET_DK ** -0.5)
                qb, kb, vb = qr.astype(BF16), kr.astype(BF16), v
                qdb = (qr * qd_ref[h]).astype(BF16)
                kdb = (kr * kd_ref[h]).astype(BF16)
                out = outp_ref[rows, cols]
                dgt = dgt_ref[rows, cols]
                gnh = gn_ref[h]
                r = lax.rsqrt(jnp.mean(out * out, axis=-1, keepdims=True) + EPS)
                xh = out * r
                sg = _sigmoid(g)
                dgate = dgt * (xh * gnh) * (sg * (1.0 + g * (1.0 - sg)))
                dy = dgt * (g * sg)
                dgn_ref[h] += jnp.sum(dy * xh, axis=0, keepdims=True)
                dxh = dy * gnh
                dout = r * (dxh - xh * jnp.mean(dxh * xh, axis=-1, keepdims=True))
                doutb = dout.astype(BF16)
                itr = intra_ref[h]
                pb = (_dot(qb, kb, 1, 1) * itr).astype(BF16)
                dv = _dot(pb, doutb, 0, 0)
                dsc = (_dot(doutb, vb, 1, 1) * itr).astype(BF16)
                dq = _dot(dsc, kb, 1, 0)
                dk = _dot(dsc, qb, 0, 0)
                dq = dq + _dot(doutb, st_ref[c, h], 1, 1) * qd_ref[h]
                ds_new = ds_ref[h]
                dsb = ds_new.astype(BF16)
                dk = dk + _dot(vb, dsb, 1, 1) * kd_ref[h]
                dv = dv + _dot(kdb, dsb, 1, 0)
                ds_ref[h] = ds_new * cd_ref[h] + _dot(qdb, doutb, 0, 0)
                dproj_ref[rows, h * RET_DK:(h + 1) * RET_DK] = _rope_half(dq, cs, -sn).astype(BF16)
                dproj_ref[rows, RET_QK_W + h * RET_DK:RET_QK_W + (h + 1) * RET_DK] = (
                    _rope_half(dk * (RET_DK ** -0.5), cs, -sn).astype(BF16))
                dproj_ref[rows, 2 * RET_QK_W + h * RET_DV:2 * RET_QK_W + (h + 1) * RET_DV] = dv.astype(BF16)
                dproj_ref[rows, 2 * RET_QK_W + RET_V_W + h * RET_DV:
                          2 * RET_QK_W + RET_V_W + (h + 1) * RET_DV] = dgate.astype(BF16)
            return carry

        lax.fori_loop(0, cpb, chunk, 0)

    return pl.pallas_call(
        body, name="ret_bwd", grid=(nb,),
        in_specs=[sp['proj'], sp['tab'], sp['tab'], sp['gn'], sp['intra'], sp['dec'], sp['dec'], sp['cdec'],
                  sp['vw'], sp['st'], sp['vw']] + [ANY] * len(deps),
        out_specs=[sp['proj'], sp['gn']],
        out_shape=[_sds((t, RET_IN), BF16), _sds((RET_HEADS, 1, RET_DV), F32)],
        scratch_shapes=[pltpu.VMEM((RET_HEADS, RET_DK, RET_DV), F32)],
        compiler_params=_cparams(("arbitrary",)),
    )(proj, cos, sin, gn.reshape(RET_HEADS, 1, RET_DV), intra, qdec, kdec, cdec, outp, states, dgated, *deps)


def _mla_tables(t):
    half = MLA_ROPE // 2
    inv = 1.0 / (ROPE_THETA ** (jnp.arange(0, MLA_ROPE, 2, dtype=F32) / MLA_ROPE))
    ang = jnp.arange(t, dtype=F32)[:, None] * inv[None, :]
    cos, sin = jnp.cos(ang), jnp.sin(ang)
    z = jnp.zeros((t, half), F32)
    c = jnp.concatenate([cos, cos, z, z], axis=1)
    s1 = jnp.concatenate([-sin, z, z, z], axis=1)
    s2 = jnp.concatenate([z, sin, z, z], axis=1)
    return c, s1, s2


def _rope_tile(r, c, s1, s2):
    return r * c + pltpu.roll(r, 96, 1) * s1 + pltpu.roll(r, 32, 1) * s2


def _mla_mid(proj2, qa, kva):
    t = proj2.shape[0]
    tm = _row_tile(t)

    def body(p_ref, qa_ref, kva_ref, cq_ref, ckv_ref):
        cq = p_ref[:, :MLA_Q_RANK]
        ckv = p_ref[:, MLA_Q_RANK:MLA_Q_RANK + MLA_KV_RANK]
        rq = lax.rsqrt(jnp.mean(cq * cq, axis=-1, keepdims=True) + EPS)
        rkv = lax.rsqrt(jnp.mean(ckv * ckv, axis=-1, keepdims=True) + EPS)
        cq_ref[...] = (cq * rq * qa_ref[...]).astype(BF16)
        ckv_ref[...] = (ckv * rkv * kva_ref[...]).astype(BF16)

    return pl.pallas_call(
        body, name="mla_mid", grid=(t // tm,),
        in_specs=[pl.BlockSpec((tm, MLA_IN_PAD), lambda i: (i, 0)),
                  pl.BlockSpec((1, MLA_Q_RANK), lambda i: (0, 0)),
                  pl.BlockSpec((1, MLA_KV_RANK), lambda i: (0, 0))],
        out_specs=[pl.BlockSpec((tm, MLA_Q_RANK), lambda i: (i, 0)),
                   pl.BlockSpec((tm, MLA_KV_RANK), lambda i: (i, 0))],
        out_shape=[_sds((t, MLA_Q_RANK), BF16), _sds((t, MLA_KV_RANK), BF16)],
        compiler_params=_cparams(("parallel",)),
    )(proj2, qa, kva)


def _mla_mid_bwd(proj2, qa, kva, dcq, dckv, dkr):
    t = proj2.shape[0]
    tm = _row_tile(t)

    def body(p_ref, qa_ref, kva_ref, dcq_ref, dckv_ref, dkr_ref, dp_ref, dqa_ref, dkva_ref):
        @pl.when(pl.program_id(0) == 0)
        def _():
            dqa_ref[...] = jnp.zeros_like(dqa_ref)
            dkva_ref[...] = jnp.zeros_like(dkva_ref)

        dxq, dgq = _rms_bwd_rows(dcq_ref[...], p_ref[:, :MLA_Q_RANK], qa_ref[...], MLA_Q_RANK)
        dxk, dgk = _rms_bwd_rows(dckv_ref[...], p_ref[:, MLA_Q_RANK:MLA_Q_RANK + MLA_KV_RANK], kva_ref[...],
                                 MLA_KV_RANK)
        dp_ref[:, :MLA_Q_RANK] = dxq.astype(BF16)
        dp_ref[:, MLA_Q_RANK:MLA_Q_RANK + MLA_KV_RANK] = dxk.astype(BF16)
        dp_ref[:, MLA_Q_RANK + MLA_KV_RANK:] = dkr_ref[...].astype(BF16)
        dqa_ref[...] += jnp.sum(dgq, axis=0, keepdims=True)
        dkva_ref[...] += jnp.sum(dgk, axis=0, keepdims=True)

    return pl.pallas_call(
        body, name="mla_mid_bwd", grid=(t // tm,),
        in_specs=[pl.BlockSpec((tm, MLA_IN_PAD), lambda i: (i, 0)),
                  pl.BlockSpec((1, MLA_Q_RANK), lambda i: (0, 0)),
                  pl.BlockSpec((1, MLA_KV_RANK), lambda i: (0, 0)),
                  pl.BlockSpec((tm, MLA_Q_RANK), lambda i: (i, 0)),
                  pl.BlockSpec((tm, MLA_KV_RANK), lambda i: (i, 0)),
                  pl.BlockSpec((tm, 128), lambda i: (i, 0))],
        out_specs=[pl.BlockSpec((tm, MLA_IN_PAD), lambda i: (i, 0)),
                   pl.BlockSpec((1, MLA_Q_RANK), lambda i: (0, 0)),
                   pl.BlockSpec((1, MLA_KV_RANK), lambda i: (0, 0))],
        out_shape=[_sds((t, MLA_IN_PAD), BF16), _sds((1, MLA_Q_RANK), F32), _sds((1, MLA_KV_RANK), F32)],
        compiler_params=_cparams(("arbitrary",)),
    )(proj2, qa, kva, dcq, dckv, dkr)


def _mla_prep_specs(t, tm):
    head = lambda w: pl.BlockSpec((None, tm, w), lambda i, h: (h, i, 0))
    return dict(
        head256=head(MLA_HD_PAD), head128=head(MLA_VD),
        cols256=pl.BlockSpec((tm, MLA_HD_PAD), lambda i, h: (i, h)),
        cq=pl.BlockSpec((tm, MLA_Q_RANK), lambda i, h: (i, 0)),
        ckv=pl.BlockSpec((tm, MLA_KV_RANK), lambda i, h: (i, 0)),
        wuq=pl.BlockSpec((None, MLA_Q_RANK, MLA_HD_PAD), lambda i, h: (h, 0, 0)),
        wukv=pl.BlockSpec((None, MLA_KV_RANK, MLA_HD_PAD), lambda i, h: (h, 0, 0)),
        kr=pl.BlockSpec((tm, 128), lambda i, h: (i, (MLA_Q_RANK + MLA_KV_RANK) // 128)),
        gain=pl.BlockSpec((1, MLA_HD_PAD), lambda i, h: (0, 0)),
        tab=pl.BlockSpec((tm, 128), lambda i, h: (i, 0)),
    )


def _mla_prep(cq, ckv, wuq, wukv, proj2, gq, gk, tabs):
    t = cq.shape[0]
    tm = _row_tile(t)
    sp = _mla_prep_specs(t, tm)

    def body(cq_ref, ckv_ref, wuq_ref, wukv_ref, kr_ref, gq_ref, gk_ref, c_ref, s1_ref, s2_ref,
             qh_ref, kh_ref, vh_ref):
        c, s1, s2 = c_ref[...], s1_ref[...], s2_ref[...]

        def norm_rope(xv, gain):
            r = lax.rsqrt(jnp.sum(xv * xv, axis=-1, keepdims=True) / MLA_QKD + EPS)
            y = xv * r * gain
            return jnp.concatenate([y[:, :MLA_NOPE], _rope_tile(y[:, MLA_NOPE:], c, s1, s2)], axis=-1)

        kvv = _dot(ckv_ref[...], wukv_ref[...], 1, 0)
        qh_ref[...] = norm_rope(_dot(cq_ref[...], wuq_ref[...], 1, 0), gq_ref[...]).astype(BF16)
        kf = jnp.concatenate([kvv[:, :MLA_NOPE], kr_ref[...]], axis=-1)
        kh_ref[...] = norm_rope(kf, gk_ref[...]).astype(BF16)
        vh_ref[...] = jnp.concatenate([kvv[:, MLA_NOPE:], jnp.ones((tm, MLA_VD), F32)], axis=-1).astype(BF16)

    return pl.pallas_call(
        body, name="mla_prep", grid=(t // tm, MLA_HEADS),
        in_specs=[sp['cq'], sp['ckv'], sp['wuq'], sp['wukv'], sp['kr'], sp['gain'], sp['gain'],
                  sp['tab'], sp['tab'], sp['tab']],
        out_specs=[sp['head256'], sp['head256'], sp['head256']],
        out_shape=[_sds((MLA_HEADS, t, MLA_HD_PAD), BF16), _sds((MLA_HEADS, t, MLA_HD_PAD), BF16),
                   _sds((MLA_HEADS, t, 2 * MLA_VD), BF16)],
        compiler_params=_cparams(("parallel", "arbitrary")),
    )(cq, ckv, wuq, wukv, proj2, gq, gk, *tabs)


def _mla_prep_bwd(cq, ckv, wuq, wukv, proj2, gq, gk, tabs, dqt, dkh, dvh):
    t = cq.shape[0]
    tm = _row_tile(t)
    ab = dqt.shape[-1]
    sp = _mla_prep_specs(t, tm)

    def body(cq_ref, ckv_ref, wuq_ref, wukv_ref, kr_ref, gq_ref, gk_ref, c_ref, s1_ref, s2_ref,
             dqt_ref, dkh_ref, dvh_ref, dq_ref, dkv_ref, dkr_ref, dgq_ref, dgk_ref):
        dqh = jnp.concatenate([dqt_ref[b].T for b in range(tm // ab)], axis=0)
        i, h = pl.program_id(0), pl.program_id(1)

        @pl.when((i == 0) & (h == 0))
        def _():
            dgq_ref[...] = jnp.zeros_like(dgq_ref)
            dgk_ref[...] = jnp.zeros_like(dgk_ref)

        @pl.when(h == 0)
        def _():
            dkr_ref[...] = jnp.zeros_like(dkr_ref)

        c, s1, s2 = c_ref[...], s1_ref[...], s2_ref[...]

        def back(xv, gain, dout):
            dy = jnp.concatenate([dout[:, :MLA_NOPE], _rope_tile(dout[:, MLA_NOPE:], c, -s1, -s2)], axis=-1)
            return _rms_bwd_rows(dy, xv, gain, MLA_QKD)

        kvv = _dot(ckv_ref[...], wukv_ref[...], 1, 0)
        dxq, dgq = back(_dot(cq_ref[...], wuq_ref[...], 1, 0), gq_ref[...], dqh)
        kf = jnp.concatenate([kvv[:, :MLA_NOPE], kr_ref[...]], axis=-1)
        dxk, dgk = back(kf, gk_ref[...], dkh_ref[...])
        dq_ref[...] = dxq.astype(BF16)
        dkv_ref[...] = jnp.concatenate([dxk[:, :MLA_NOPE], dvh_ref[...]], axis=-1).astype(BF16)
        dkr_ref[...] += dxk[:, MLA_NOPE:]
        dgq_ref[...] += jnp.sum(dgq, axis=0, keepdims=True)
        dgk_ref[...] += jnp.sum(dgk, axis=0, keepdims=True)

    return pl.pallas_call(
        body, name="mla_prep_bwd", grid=(t // tm, MLA_HEADS),
        in_specs=[sp['cq'], sp['ckv'], sp['wuq'], sp['wukv'], sp['kr'], sp['gain'], sp['gain'],
                  sp['tab'], sp['tab'], sp['tab'],
                  pl.BlockSpec((None, tm // ab, MLA_HD_PAD, ab), lambda i, h: (h, i, 0, 0)),
                  sp['head256'], sp['head128']],
        out_specs=[sp['cols256'], sp['cols256'], sp['tab'], sp['gain'], sp['gain']],
        out_shape=[_sds((t, MLA_HEADS * MLA_HD_PAD), BF16), _sds((t, MLA_HEADS * MLA_HD_PAD), BF16),
                   _sds((t, 128), F32), _sds((1, MLA_HD_PAD), F32), _sds((1, MLA_HD_PAD), F32)],
        compiler_params=_cparams(("arbitrary", "arbitrary")),
    )(cq, ckv, wuq, wukv, proj2, gq, gk, *tabs, dqt, dkh, dvh)


def _chunk_visible(rows, cols, row_off, col_off):
    rq = lax.shift_right_logical(lax.broadcasted_iota(jnp.int32, (rows, cols), 0) + row_off, 6)
    ck = lax.shift_right_logical(lax.broadcasted_iota(jnp.int32, (rows, cols), 1) + col_off, 6)
    return ck <= rq


def _rows_to_lanes(col):
    return col.T[:8, :]


def _attn_fwd(qh, kh, vh):
    t = qh.shape[1]
    ab = min(ATT_BLOCK, t)
    tq = min(ATT_QROWS, t)
    r = tq // ab
    hg = ATT_HEADS

    def body(q_ref, k_ref, v_ref, o_ref, lse_ref):
        n_un = pl.program_id(1) * r

        def step(b, state, diag):
            rows = pl.ds(pl.multiple_of(b * ab, ab), ab)
            lo = 0 if diag is None else diag * ab
            ms, accs = [], []
            for hh in range(hg):
                m_all, acc_all = state[0][hh], state[1][hh]
                m, acc = m_all[lo:], acc_all[lo:]
                s = _dot(q_ref[hh, lo:, :], k_ref[hh, rows, :], 1, 1)
                if diag is not None:
                    s = jnp.where(_chunk_visible(tq - lo, ab, 0, 0), s, -1e30)
                m_new = jnp.maximum(m, jnp.max(s, axis=-1, keepdims=True))
                p = jnp.exp2((s - m_new) * ATT_EXP2).astype(BF16)
                acc = jnp.exp2((m - m_new) * ATT_EXP2) * acc + _dot(p, v_ref[hh, rows, :], 1, 0)
                if lo:
                    m_new = jnp.concatenate([m_all[:lo], m_new], axis=0)
                    acc = jnp.concatenate([acc_all[:lo], acc], axis=0)
                accs.append(acc)
                ms.append(m_new)
            return tuple(ms), tuple(accs)

        heads = lambda v: tuple(v for _ in range(hg))
        state = (heads(jnp.full((tq, 1), -1e30, F32)), heads(jnp.zeros((tq, 2 * MLA_VD), F32)))
        state = lax.fori_loop(0, n_un, lambda b, st: step(b, st, None), state)
        for d in range(r):
            state = step(n_un + d, state, d)
        ms, accs = state
        for hh in range(hg):
            l = accs[hh][:, MLA_VD:]
            o_ref[:, hh * MLA_VD:(hh + 1) * MLA_VD] = accs[hh][:, :MLA_VD] / l
            lse_t = _rows_to_lanes(ms[hh] * ATT_EXP2 + jnp.log(l) * LOG2E)
            for d in range(r):
                lse_ref[hh, d] = lse_t[:, d * ab:(d + 1) * ab]

    return pl.pallas_call(
        body, name="mla_attn", grid=(MLA_HEADS // hg, t // tq),
        in_specs=[pl.BlockSpec((hg, tq, MLA_HD_PAD), lambda g, i: (g, i, 0)),
                  pl.BlockSpec((hg, t, MLA_HD_PAD), lambda g, i: (g, 0, 0)),
                  pl.BlockSpec((hg, t, 2 * MLA_VD), lambda g, i: (g, 0, 0))],
        out_specs=[pl.BlockSpec((tq, hg * MLA_VD), lambda g, i: (i, g)),
                   pl.BlockSpec((hg, r, 8, ab), lambda g, i: (g, i, 0, 0))],
        out_shape=[_sds((t, MLA_HEADS * MLA_VD), F32), _sds((MLA_HEADS, t // ab, 8, ab), F32)],
        compiler_params=_cparams(("parallel", "arbitrary")),
    )(qh, kh, vh)


def _attn_delta(do, o, ab):
    t = do.shape[0]
    tm = _row_tile(t)

    def body(do_ref, o_ref, d_ref):
        d = jnp.sum(do_ref[...] * o_ref[...], axis=-1, keepdims=True)
        d_t = _rows_to_lanes(jnp.broadcast_to(d, (tm, 128)))
        for b in range(tm // ab):
            d_ref[b] = d_t[:, b * ab:(b + 1) * ab]

    col = pl.BlockSpec((tm, MLA_VD), lambda i, h: (i, h))
    return pl.pallas_call(
        body, name="mla_delta", grid=(t // tm, MLA_HEADS), in_specs=[col, col],
        out_specs=pl.BlockSpec((None, tm // ab, 8, ab), lambda i, h: (h, i, 0, 0)),
        out_shape=_sds((MLA_HEADS, t // ab, 8, ab), F32),
        compiler_params=_cparams(("parallel", "parallel")),
    )(do, o)


def _attn_bwd(qh, kh, vh, dob, lse_t, dl_t):
    t = qh.shape[1]
    ab = min(ATT_BLOCK, t)
    kb = min(ATT_KROWS, t)
    r = kb // ab
    nq = t // ab
    hg = ATT_HEADS

    def body(q_ref, k_ref, v_ref, do_ref, lse_ref, dl_ref, dqt_ref, dk_ref, dv_ref):
        j = pl.program_id(1)

        @pl.when(j == 0)
        def _():
            dqt_ref[...] = jnp.zeros_like(dqt_ref)

        ks = [k_ref[hh] for hh in range(hg)]
        vs = [v_ref[hh, :, :MLA_VD] for hh in range(hg)]
        kts = [k.T for k in ks]

        def step(b, grads, diag):
            rows = pl.ds(pl.multiple_of(b * ab, ab), ab)
            hi = kb if diag is None else (diag + 1) * ab
            out = []
            for hh in range(hg):
                dk_all, dv_all = grads[hh]
                q = q_ref[hh, rows, :]
                do = do_ref[rows, hh * MLA_VD:(hh + 1) * MLA_VD]
                s_t = _dot(ks[hh][:hi], q, 1, 1)
                if diag is not None:
                    key_chunk = lax.shift_right_logical(lax.broadcasted_iota(jnp.int32, (hi, ab), 0), 6)
                    query_chunk = lax.shift_right_logical(
                        lax.broadcasted_iota(jnp.int32, (hi, ab), 1) + diag * ab, 6)
                    s_t = jnp.where(key_chunk <= query_chunk, s_t, -1e30)
                p_t = jnp.exp2(s_t * ATT_EXP2 - lse_ref[hh, b][0:1, :])
                dp_t = _dot(vs[hh][:hi], do, 1, 1)
                ds_t = (p_t * (dp_t - dl_ref[hh, b][0:1, :]) * ATT_SCALE).astype(BF16)
                dqt_ref[hh, b] += _dot(kts[hh][:, :hi], ds_t, 1, 0)
                dk = dk_all[:hi] + _dot(ds_t, q, 1, 0)
                dv = dv_all[:hi] + _dot(p_t.astype(BF16), do, 1, 0)
                if hi < kb:
                    dk = jnp.concatenate([dk, dk_all[hi:]], axis=0)
                    dv = jnp.concatenate([dv, dv_all[hi:]], axis=0)
                out.append((dk, dv))
            return tuple(out)

        grads = tuple((jnp.zeros((kb, MLA_HD_PAD), F32), jnp.zeros((kb, MLA_VD), F32)) for _ in range(hg))
        for d in range(r):
            grads = step(j * r + d, grads, d)
        grads = lax.fori_loop((j + 1) * r, nq, lambda b, g: step(b, g, None), grads)
        for hh in range(hg):
            dk_ref[hh] = grads[hh][0]
            dv_ref[hh] = grads[hh][1]

    whole = lambda w: pl.BlockSpec((hg, t, w), lambda g, j: (g, 0, 0))
    blk = lambda w: pl.BlockSpec((hg, kb, w), lambda g, j: (g, j, 0))
    stat = pl.BlockSpec((hg, nq, 8, ab), lambda g, j: (g, 0, 0, 0))
    return pl.pallas_call(
        body, name="mla_attn_bwd", grid=(MLA_HEADS // hg, t // kb),
        in_specs=[whole(MLA_HD_PAD), blk(MLA_HD_PAD), blk(2 * MLA_VD),
                  pl.BlockSpec((t, hg * MLA_VD), lambda g, j: (0, g)), stat, stat],
        out_specs=[pl.BlockSpec((hg, nq, MLA_HD_PAD, ab), lambda g, j: (g, 0, 0, 0)), blk(MLA_HD_PAD), blk(MLA_VD)],
        out_shape=[_sds((MLA_HEADS, nq, MLA_HD_PAD, ab), F32), _sds((MLA_HEADS, t, MLA_HD_PAD), F32),
                   _sds((MLA_HEADS, t, MLA_VD), F32)],
        compiler_params=_cparams(("parallel", "arbitrary")),
    )(qh, kh, vh, dob, lse_t, dl_t)


VEC = pl.BlockSpec((1, D_MODEL), lambda i, j, k: (0, 0))


def _rows(tm, width):
    return pl.BlockSpec((tm, width), lambda i, j, k: (i, 0))


def _residual_epi(next_gain):
    if next_gain is None:
        return [], lambda acc, hv: (acc + hv,)

    def epi(acc, hv, g):
        h_new = acc + hv
        r = lax.rsqrt(jnp.mean(h_new * h_new, axis=-1, keepdims=True) + EPS)
        return h_new, h_new * r * g

    return [(next_gain, VEC)], epi


def _residual_outs(t, row, next_gain):
    outs = [(_sds((t, D_MODEL), F32), row)]
    return outs + ([(_sds((t, D_MODEL), BF16), row)] if next_gain is not None else [])


def _mlp_fwd(l, h, hn, w1g, fetch_w2, next_gain):
    t = h.shape[0]
    tm = _row_tile(t, 512)

    def relu2(acc):
        r = jnp.maximum(acc, 0.0)
        return (r * r,)

    (u,) = _mm_rows(f"mlp_up{l}", tm, hn, w1g, 'nn_cols', [(_sds((t, D_FF), BF16), _rows(tm, D_FF))], epi=relu2)
    w2g = fetch_w2((u,))
    row = _rows(tm, D_MODEL)
    more, epi = _residual_epi(next_gain)
    h2, hn_next = _mm_rows(f"mlp_down{l}", tm, u, w2g, 'nn_rows', _residual_outs(t, row, next_gain),
                           extras=[(h, row)] + more, epi=epi)
    return h2, hn_next, (h, hn, u, w1g, w2g)


def _norm_bwd_outs(t, tm):
    return [(_sds((t, D_MODEL), F32), pl.BlockSpec((tm, D_MODEL), lambda i, j, k: (i, 0))),
            (_sds((t // tm, 1, D_MODEL), F32), pl.BlockSpec((None, 1, D_MODEL), lambda i, j, k: (i, 0, 0)))]


def _norm_bwd_epi(acc, xv, res, g):
    dx, dgr = _rms_bwd_rows(acc, xv, g, D_MODEL)
    return res + dx, jnp.sum(dgr, axis=0, keepdims=True)


def _mlp_bwd(l, dh, saved, norm_g):
    h, hn, u, w1g, w2g = saved
    t = h.shape[0]
    tm = _row_tile(t, 512)
    nsh, _, wsh = w1g.shape
    wide = _rows(tm, D_FF)
    (da,) = _mm_rows(f"mlp_du{l}", tm, dh, w2g, 'nt_rows', [(_sds((t, D_FF), BF16), wide)], extras=[(u, wide)],
                     epi=lambda acc, uv: (2.0 * jnp.sqrt(uv.astype(F32)) * acc,))
    tw = _row_tile(t, 512)
    (dw2,) = _mm(f"mlp_dw2{l}", (1, 1, t // tw),
                 u, pl.BlockSpec((tw, D_FF), lambda i, j, k: (k, 0)),
                 dh, pl.BlockSpec((tw, D_MODEL), lambda i, j, k: (k, 0)), (0, 0),
                 [(_sds((D_FF, D_MODEL), BF16), pl.BlockSpec((D_FF, D_MODEL), lambda i, j, k: (0, 0)))])
    dw2 = dw2.reshape(nsh, wsh, D_MODEL)
    (dw1,) = _mm(f"mlp_dw1{l}", (1, 1, t // tw),
                 hn, pl.BlockSpec((tw, D_MODEL), lambda i, j, k: (k, 0)),
                 da, pl.BlockSpec((tw, D_FF), lambda i, j, k: (k, 0)), (0, 0),
                 [(_sds((nsh, D_MODEL, wsh), BF16), pl.BlockSpec((nsh, D_MODEL, wsh), lambda i, j, k: (0, 0, 0)))],
                 split=wsh)
    row = _rows(tm, D_MODEL)
    dh_in, dg = _mm_rows(f"mlp_dhn{l}", tm, da, w1g, 'nt_cols', _norm_bwd_outs(t, tm),
                         extras=[(h, row), (dh, row), (norm_g, VEC)], epi=_norm_bwd_epi)
    return dh_in, jnp.sum(dg, axis=0), dw1, dw2


def _ple_fwd(l, h, hn, p, wg, wp, next_gain, target=None):
    t = h.shape[0]
    tm = _row_tile(t, 512)
    row = pl.BlockSpec((tm, D_MODEL), lambda i, j, k: (i, 0))
    full = lambda r: pl.BlockSpec((r, D_MODEL), lambda i, j, k: (0, 0))
    (e,) = _mm(f"ple_proj{l}", (t // tm, 1, 1),
               p, pl.BlockSpec((None, None, tm, PLE_DIM), lambda i, j, k: (l, 0, i, 0)),
               wp, full(PLE_DIM), (1, 0), [(_sds((t, D_MODEL), F32), row)])

    f32_row, bf_row = (_sds((t, D_MODEL), F32), row), (_sds((t, D_MODEL), BF16), row)
    if target is not None:
        def loss_epi(acc, hv, ev, tv):
            gt = _sigmoid(acc)
            err = hv + gt * ev - tv
            sq = jnp.sum(jnp.sum(err * err, axis=-1, keepdims=True), axis=0, keepdims=True)
            return err / D_MODEL, gt, jnp.broadcast_to(sq, (8, 128))

        dy, gate, sq = _mm(f"ple_gate{l}", (t // tm, 1, 1), hn, row, wg, full(D_MODEL), (1, 0),
                           [f32_row, f32_row, (_sds((t // tm, 8, 128), F32),
                                               pl.BlockSpec((None, 8, 128), lambda i, j, k: (i, 0, 0)))],
                           extras=[(h, row), (e, row), (target, row)], epi=loss_epi)
        return dy, jnp.sum(sq, axis=0), (h, hn, gate, e)

    def gate_epi(acc, hv, ev, *gain):
        gt = _sigmoid(acc)
        h_new = hv + gt * ev
        if not gain:
            return h_new, gt
        r = lax.rsqrt(jnp.mean(h_new * h_new, axis=-1, keepdims=True) + EPS)
        return h_new, gt, h_new * r * gain[0]

    res = _mm(f"ple_gate{l}", (t // tm, 1, 1), hn, row, wg, full(D_MODEL), (1, 0),
              [f32_row, f32_row] + ([bf_row] if next_gain is not None else []),
              extras=[(h, row), (e, row)] + ([(next_gain, VEC)] if next_gain is not None else []), epi=gate_epi)
    h_out, gate = res[0], res[1]
    return h_out, (res[2] if next_gain is not None else None), (h, hn, gate, e)


def _ple_bwd(l, dh, saved, p, norm_g, wg, deps=()):
    h, hn, gate, e = saved
    t = h.shape[0]
    tm = _row_tile(t)
    tk = _row_tile(t, 512)
    de, dz = _ple_gate_bwd(f"ple_gate_bwd{l}", dh, gate, e)
    full = lambda r: pl.BlockSpec((r, D_MODEL), lambda i, j, k: (0, 0))
    rowk = pl.BlockSpec((tk, D_MODEL), lambda i, j, k: (k, 0))
    (dwp,) = _mm(f"ple_dwp{l}", (1, 1, t // tk),
                 p, pl.BlockSpec((None, None, tk, PLE_DIM), lambda i, j, k: (l, 0, k, 0)),
                 de, rowk, (0, 0), [(_sds((PLE_DIM, D_MODEL), BF16), full(PLE_DIM))], deps=deps)
    (dwg,) = _mm(f"ple_dwg{l}", (1, 1, t // tk), hn, rowk, dz, rowk, (0, 0),
                 [(_sds((D_MODEL, D_MODEL), BF16), full(D_MODEL))])
    row = pl.BlockSpec((tm, D_MODEL), lambda i, j, k: (i, 0))
    dh_in, dg = _mm(f"ple_dhn{l}", (t // tm, 1, 1), dz, row, wg, full(D_MODEL), (1, 1),
                    _norm_bwd_outs(t, tm), extras=[(h, row), (dh, row), (norm_g, VEC)], epi=_norm_bwd_epi)
    return dh_in, jnp.sum(dg, axis=0), dwg, dwp


def _ret_layer_fwd(x, norm_g, wri, fetch_wro, gn, cos, sin, next_gain, deps=()):
    t = x.shape[0]
    tm = _row_tile(t)
    nsh, _, wsh = wri.shape
    hn = _rms_fwd("mix_norm0", x, norm_g)
    tp = _row_tile(t, 512)
    (proj,) = _mm_rows("ret_in", tp, hn, wri, 'nn_cols', [(_sds((t, RET_IN), BF16), _rows(tp, RET_IN))], deps=deps)
    gated, outp, states = _ret_fwd(proj, cos, sin, gn)
    wro = fetch_wro((gated,))
    row = pl.BlockSpec((tm, D_MODEL), lambda i, j, k: (i, 0))
    kt = 512
    more, epi = _residual_epi(next_gain)
    h1, hn_next = _mm("ret_out", (t // tm, 1, RET_V_W // kt),
                      gated, pl.BlockSpec((tm, kt), lambda i, j, k: (i, k)),
                      wro, pl.BlockSpec((kt, D_MODEL), lambda i, j, k: (k, 0)), (1, 0),
                      _residual_outs(t, row, next_gain), extras=[(x, row)] + more, epi=epi)
    return h1, hn_next, (x, hn, proj, gated, outp, states, wro)


def _ret_layer_bwd(dh, saved, norm_g, wri, gn, cos, sin, emit_out, emit_in, deps=()):
    x, hn, proj, gated, outp, states, wro = saved
    t = x.shape[0]
    tm = _row_tile(t)
    tk = _row_tile(t, 512)
    nsh, _, wsh = wri.shape
    (dgated,) = _mm("ret_dgated", (t // tm, RET_V_W // D_MODEL, 1),
                    dh, pl.BlockSpec((tm, D_MODEL), lambda i, j, k: (i, 0)),
                    wro, pl.BlockSpec((D_MODEL, D_MODEL), lambda i, j, k: (j, 0)), (1, 1),
                    [(_sds((t, RET_V_W), F32), pl.BlockSpec((tm, D_MODEL), lambda i, j, k: (i, j)))], deps=deps)
    (dwro,) = _mm("ret_dwro", (1, 1, t // tk),
                  gated, pl.BlockSpec((tk, RET_V_W), lambda i, j, k: (k, 0)),
                  dh, pl.BlockSpec((tk, D_MODEL), lambda i, j, k: (k, 0)), (0, 0),
                  [(_sds((RET_V_W, D_MODEL), BF16), pl.BlockSpec((RET_V_W, D_MODEL), lambda i, j, k: (0, 0)))])
    dproj, dgn = _ret_bwd(proj, cos, sin, gn, outp, states, dgated, deps=emit_out(dwro))
    half = nsh // 2
    (dwri,) = _mm("ret_dwri", (2, 1, t // tk),
                  hn, pl.BlockSpec((tk, D_MODEL), lambda i, j, k: (k, 0)),
                  dproj, pl.BlockSpec((tk, half * wsh), lambda i, j, k: (k, i)), (0, 0),
                  [(_sds((nsh, D_MODEL, wsh), BF16), pl.BlockSpec((half, D_MODEL, wsh), lambda i, j, k: (i, 0, 0)))],
                  split=wsh)
    deps = emit_in(dwri)
    td = _row_tile(t, 256)
    row = _rows(td, D_MODEL)
    dx, dg = _mm_rows("ret_dhn", td, dproj, wri, 'nt_cols', _norm_bwd_outs(t, td),
                      extras=[(x, row), (dh, row), (norm_g, VEC)], epi=_norm_bwd_epi, deps=deps)
    return dx, jnp.sum(dg, axis=0), dgn.reshape(RET_HEADS, RET_DV)


def _mla_layer_fwd(h, hn, wmi, qa, kva, wuq, wukv, gq, gk, wmo, tabs, next_gain):
    t = h.shape[0]
    tm = _row_tile(t)
    row = pl.BlockSpec((tm, D_MODEL), lambda i, j, k: (i, 0))
    (proj2,) = _mm("mla_in", (t // tm, 1, 1), hn, row,
                   wmi, pl.BlockSpec((D_MODEL, MLA_IN_PAD), lambda i, j, k: (0, 0)), (1, 0),
                   [(_sds((t, MLA_IN_PAD), F32), pl.BlockSpec((tm, MLA_IN_PAD), lambda i, j, k: (i, 0)))])
    cq, ckv = _mla_mid(proj2, qa, kva)
    qh, kh, vh = _mla_prep(cq, ckv, wuq, wukv, proj2, gq, gk, tabs)
    o, lse = _attn_fwd(qh, kh, vh)
    more, epi = _residual_epi(next_gain)
    h_out, hn_next = _mm("mla_out", (t // tm, 1, 1), o, row,
                         wmo, pl.BlockSpec((D_MODEL, D_MODEL), lambda i, j, k: (0, 0)), (1, 0),
                         _residual_outs(t, row, next_gain), extras=[(h, row)] + more, epi=epi)
    return h_out, hn_next, (h, hn, proj2, cq, ckv, qh, kh, vh, o, lse)


def _mla_layer_bwd(dh, saved, norm_g, wmi, qa, kva, wuq, wukv, gq, gk, wmo, tabs, deps=()):
    h, hn, proj2, cq, ckv, qh, kh, vh, o, lse = saved
    t = h.shape[0]
    tm = _row_tile(t)
    tk = _row_tile(t, 512)
    row = pl.BlockSpec((tm, D_MODEL), lambda i, j, k: (i, 0))
    rowk = pl.BlockSpec((tk, D_MODEL), lambda i, j, k: (k, 0))
    sq = pl.BlockSpec((D_MODEL, D_MODEL), lambda i, j, k: (0, 0))
    do, dob = _mm("mla_do", (t // tm, 1, 1), dh, row, wmo, sq, (1, 1),
                  [(_sds((t, D_MODEL), F32), row), (_sds((t, D_MODEL), BF16), row)], epi=lambda acc: (acc, acc),
                  deps=deps)
    (dwmo,) = _mm("mla_dwo", (1, 1, t // tk), o, rowk, dh, rowk, (0, 0), [(_sds((D_MODEL, D_MODEL), BF16), sq)])
    delta = _attn_delta(do, o, lse.shape[-1])
    dqt, dkh, dvh = _attn_bwd(qh, kh, vh, dob, lse, delta)
    dq, dkv, dkr, dgq, dgk = _mla_prep_bwd(cq, ckv, wuq, wukv, proj2, gq, gk, tabs, dqt, dkh, dvh)

    wide = MLA_HEADS * MLA_HD_PAD
    widek = pl.BlockSpec((tk, wide), lambda i, j, k: (k, 0))
    (dwuq,) = _mm("mla_dwuq", (1, 1, t // tk),
                  cq, pl.BlockSpec((tk, MLA_Q_RANK), lambda i, j, k: (k, 0)), dq, widek, (0, 0),
                  [(_sds((MLA_HEADS, MLA_Q_RANK, MLA_HD_PAD), BF16),
                    pl.BlockSpec((MLA_HEADS, MLA_Q_RANK, MLA_HD_PAD), lambda i, j, k: (0, 0, 0)))], split=MLA_HD_PAD)
    (dwukv,) = _mm("mla_dwukv", (1, 1, t // tk),
                   ckv, pl.BlockSpec((tk, MLA_KV_RANK), lambda i, j, k: (k, 0)), dkv, widek, (0, 0),
                   [(_sds((MLA_HEADS, MLA_KV_RANK, MLA_HD_PAD), BF16),
                     pl.BlockSpec((MLA_HEADS, MLA_KV_RANK, MLA_HD_PAD), lambda i, j, k: (0, 0, 0)))],
                   split=MLA_HD_PAD)
    side_by_side = lambda wg: wg.transpose(1, 0, 2).reshape(wg.shape[1], wide)
    widei = pl.BlockSpec((tm, wide), lambda i, j, k: (i, 0))
    (dcq,) = _mm("mla_dcq", (t // tm, 1, 1), dq, widei,
                 side_by_side(wuq), pl.BlockSpec((MLA_Q_RANK, wide), lambda i, j, k: (0, 0)), (1, 1),
                 [(_sds((t, MLA_Q_RANK), F32), pl.BlockSpec((tm, MLA_Q_RANK), lambda i, j, k: (i, 0)))])
    (dckv,) = _mm("mla_dckv", (t // tm, 1, 1), dkv, widei,
                  side_by_side(wukv), pl.BlockSpec((MLA_KV_RANK, wide), lambda i, j, k: (0, 0)), (1, 1),
                  [(_sds((t, MLA_KV_RANK), F32), pl.BlockSpec((tm, MLA_KV_RANK), lambda i, j, k: (i, 0)))])
    dproj2, dqa, dkva = _mla_mid_bwd(proj2, qa, kva, dcq, dckv, dkr)
    win = pl.BlockSpec((D_MODEL, MLA_IN_PAD), lambda i, j, k: (0, 0))
    (dwmi,) = _mm("mla_dwin", (1, 1, t // tk), hn, rowk,
                  dproj2, pl.BlockSpec((tk, MLA_IN_PAD), lambda i, j, k: (k, 0)), (0, 0),
                  [(_sds((D_MODEL, MLA_IN_PAD), BF16), win)])
    dh_in, dg = _mm("mla_dhn", (t // tm, 1, 1),
                    dproj2, pl.BlockSpec((tm, MLA_IN_PAD), lambda i, j, k: (i, 0)), wmi, win, (1, 1),
                    _norm_bwd_outs(t, tm), extras=[(h, row), (dh, row), (norm_g, VEC)], epi=_norm_bwd_epi)
    return dh_in, dict(mix=jnp.sum(dg, axis=0), wmi=dwmi, qa=dqa, kva=dkva, wuq=dwuq, wukv=dwukv, gq=dgq, gk=dgk,
                       wmo=dwmo)


def _local_step(x, p, target, w, fetch, emit=lambda group: ()):
    t = x.shape[0]
    inv = 1.0 / (ROPE_THETA ** (jnp.arange(0, RET_DK, 2, dtype=F32) / RET_DK))
    ang = jnp.arange(t, dtype=F32)[:, None] * inv[None, :]
    cos_r, sin_r = jnp.cos(ang), jnp.sin(ang)
    tabs = _mla_tables(t)
    row = lambda a, i: a[i:i + 1]

    h1, hn1, s_ret = _ret_layer_fwd(x, row(w['mix_norm'], 0), w['ret_w_in'],
                                    lambda after: fetch('ret_out', after)['ret_w_out'], w['ret_gn'], cos_r, sin_r,
                                    row(w['mlp_norm'], 0), deps=w['deps'])
    h2, hn2, s_mlp0 = _mlp_fwd(0, h1, hn1, fetch('mlp_w1_0', (h1,))['mlp_w1'],
                               lambda after: fetch('mlp_w2_0', after)['mlp_w2'], row(w['ple_norm'], 0))
    w0 = fetch('ple_0', (h2,))
    h3, hn3, s_ple0 = _ple_fwd(0, h2, hn2, p, w0['ple_gate_w'], w0['ple_proj_w'], row(w['mix_norm'], 1))
    wm = fetch('mla', (h3,))
    mla_w = (wm['mla_w_in'], w['mla_q_a_norm'], w['mla_kv_a_norm'], wm['mla_w_uq'], wm['mla_w_ukv'],
             w['mla_q_norm'], w['mla_k_norm'], wm['mla_w_out'], tabs)
    h4, hn4, s_mla = _mla_layer_fwd(h3, hn3, *mla_w, row(w['mlp_norm'], 1))
    w1 = fetch('layer_1', (h4,))
    h5, hn5, s_mlp1 = _mlp_fwd(1, h4, hn4, w1['mlp_w1'], lambda after: w1['mlp_w2'], row(w['ple_norm'], 1))
    dy, sq_err, s_ple1 = _ple_fwd(1, h5, hn5, p, w1['ple_gate_w'], w1['ple_proj_w'], None, target)

    n = N_DEV
    colsh = lambda a: a.reshape(a.shape[0], n, a.shape[1] // n).transpose(1, 0, 2)
    rowsh = lambda a: a.reshape(n, a.shape[0] // n, a.shape[1])
    big = {}

    def emit_group(group):
        big.update(group)
        return emit(group)

    dh5, dg_ple1, dwg1, dwp1 = _ple_bwd(1, dy, s_ple1, p, row(w['ple_norm'], 1), w1['ple_gate_w'])
    dh4, dg_mlp1, dw1_1, dw2_1 = _mlp_bwd(1, dh5, s_mlp1, row(w['mlp_norm'], 1))
    deps = emit_group({('ple_gate_w', 1): rowsh(dwg1), ('ple_proj_w', 1): colsh(dwp1),
                       ('mlp_w2', 1): dw2_1, ('mlp_w1', 1): dw1_1})
    dh3, gm = _mla_layer_bwd(dh4, s_mla, row(w['mix_norm'], 1), *mla_w, deps=deps)
    deps = emit_group({('mla_w_out', 0): rowsh(gm['wmo']), ('mla_w_uq', 0): gm['wuq'][:, :, :MLA_QKD],
                       ('mla_w_ukv', 0): gm['wukv'], ('mla_w_in', 0): rowsh(gm['wmi'][:, :MLA_IN])})
    dh2, dg_ple0, dwg0, dwp0 = _ple_bwd(0, dh3, s_ple0, p, row(w['ple_norm'], 0), w0['ple_gate_w'], deps=deps)
    dh1, dg_mlp0, dw1_0, dw2_0 = _mlp_bwd(0, dh2, s_mlp0, row(w['mlp_norm'], 0))
    deps = emit_group({('ple_gate_w', 0): rowsh(dwg0), ('ple_proj_w', 0): colsh(dwp0),
                       ('mlp_w2', 0): dw2_0, ('mlp_w1', 0): dw1_0})
    dx, dg_mix0, dgn = _ret_layer_bwd(
        dh1, s_ret, row(w['mix_norm'], 0), w['ret_w_in'], w['ret_gn'], cos_r, sin_r,
        lambda dwro: emit_group({('ret_w_out', 0): rowsh(dwro)}),
        lambda dwri: emit_group({('ret_w_in', 0): dwri}), deps=deps)

    small = dict(
        mix_norm=[dg_mix0, gm['mix']], mlp_norm=[dg_mlp0, dg_mlp1], ple_norm=[dg_ple0, dg_ple1],
        ret_gn=dgn, mla_q_a_norm=gm['qa'], mla_kv_a_norm=gm['kva'], mla_q_norm=gm['gq'], mla_k_norm=gm['gk'],
    )
    return sq_err, dx, big, small


def _my_place():
    x, y, c = lax.axis_index("x"), lax.axis_index("y"), lax.axis_index("c")
    return x, y, c


def _flat(px, py, pc):
    return 4 * px + 2 * py + pc


def _peer(x, y, c, r):
    return (1 - x if r & 4 else x, 1 - y if r & 2 else y, 1 - c if r & 1 else c)


def _all_gather(arrays):
    n = len(arrays)

    def body(*refs):
        ins, outs = refs[:n], refs[n:2 * n]
        send_sems, recv_sems, local_sems = refs[2 * n:]
        x, y, c = _my_place()
        me, sibling = (x, y, c), (x, y, 1 - c)
        chips = [(1 - x, y), (x, 1 - y), (1 - x, 1 - y)]

        def copy(a, k, block, to, src=None):
            slot = outs[a].at[_flat(*block)]
            return pltpu.make_async_remote_copy(
                src_ref=slot if src is None else src, dst_ref=slot,
                send_sem=send_sems.at[a, k], recv_sem=recv_sems.at[a, k], device_id=to, device_id_type=MESH)

        mine = [pltpu.make_async_copy(ins[a], outs[a].at[_flat(*me)], local_sems.at[a]) for a in range(n)]
        for cp in mine:
            cp.start()
        first = []
        for a in range(n):
            first.append(copy(a, 0, me, sibling, src=ins[a]))
            first += [copy(a, 1 + j, me, (*chip, c), src=ins[a]) for j, chip in enumerate(chips)]
        for cp in first:
            cp.start()
        passed = []
        for a in range(n):
            for j, chip in enumerate(chips):
                copy(a, 1 + j, (*chip, c), me).wait_recv()
                passed.append(copy(a, 4 + j, (*chip, c), sibling))
                passed[-1].start()
        for a in range(n):
            copy(a, 0, sibling, me).wait_recv()
            for j, chip in enumerate(chips):
                copy(a, 4 + j, (*chip, 1 - c), me).wait_recv()
        for cp in first + passed:
            cp.wait_send()
        for cp in mine:
            cp.wait()

    return pl.pallas_call(
        body, name="all_gather_weights",
        in_specs=[ANY] * n, out_specs=[ANY] * n,
        out_shape=[_sds((N_DEV,) + a.shape, a.dtype) for a in arrays],
        scratch_shapes=[pltpu.SemaphoreType.DMA((n, 7)), pltpu.SemaphoreType.DMA((n, 7)),
                        pltpu.SemaphoreType.DMA((n,))],
    )(*arrays)


HBM = pl.BlockSpec(memory_space=pltpu.HBM)
SEMS = pl.BlockSpec(memory_space=pltpu.SEMAPHORE)
SIDE_EFFECT = pltpu.SideEffectType.DATAFLOW_SIDE_EFFECTING


def _rs_copies(x, y, c, srcs, lands, send_sems, recv_sems):
    copies = []
    for a in range(len(srcs)):
        for r in range(1, N_DEV):
            peer = _peer(x, y, c, r)
            k = a * (N_DEV - 1) + r - 1
            copies.append(pltpu.make_async_remote_copy(
                src_ref=srcs[a].at[_flat(*peer)], dst_ref=lands[a].at[r - 1],
                send_sem=send_sems.at[k], recv_sem=recv_sems.at[k], device_id=peer, device_id_type=MESH))
    return copies


def _rs_start(name, arrays):
    n = len(arrays)
    hbm = lambda a: pltpu.with_memory_space_constraint(a, pltpu.HBM)
    lands = [hbm(lax.empty((N_DEV - 1,) + a.shape[1:], a.dtype)) for a in arrays]

    def body(*refs):
        srcs, lnd = refs[:n], refs[n:2 * n]
        send_sems, recv_sems = refs[2 * n], refs[2 * n + 1]
        token = refs[-1]
        for cp in _rs_copies(*_my_place(), srcs, lnd, send_sems, recv_sems):
            cp.start()
        token[...] = jnp.zeros_like(token)

    outs = pl.pallas_call(
        body, name=name,
        in_specs=[HBM] * (2 * n),
        out_specs=[SEMS, SEMS] + [HBM] * (2 * n) + [pl.BlockSpec(memory_space=pltpu.VMEM)],
        out_shape=[pltpu.SemaphoreType.DMA((n * (N_DEV - 1),)), pltpu.SemaphoreType.DMA((n * (N_DEV - 1),))]
        + [pltpu.HBM(a.shape, a.dtype) for a in arrays] + [pltpu.HBM(l.shape, l.dtype) for l in lands]
        + [_sds((8, 128), F32)],
        input_output_aliases={i: 2 + i for i in range(2 * n)},
        compiler_params=pltpu.CompilerParams(has_side_effects=SIDE_EFFECT),
    )(*[hbm(a) for a in arrays], *lands)
    return outs[0], outs[1], outs[2:2 + n], outs[2 + n:2 + 2 * n], outs[-1]


def _rs_wait(name, send_sems, recv_sems, srcs, lands, after):
    n = len(srcs)

    def body(*refs):
        src_refs, lnd = refs[:n], refs[n:2 * n]
        send, recv = refs[2 * n], refs[2 * n + 1]
        for cp in _rs_copies(*_my_place(), src_refs, lnd, send, recv):
            cp.wait_send()
            cp.wait_recv()

    outs = pl.pallas_call(
        body, name=name,
        in_specs=[HBM] * (2 * n) + [SEMS, SEMS] + [ANY] * len(after),
        out_specs=[HBM] * (2 * n),
        out_shape=[pltpu.HBM(a.shape, a.dtype) for a in list(srcs) + list(lands)],
        input_output_aliases={i: i for i in range(2 * n)},
        compiler_params=pltpu.CompilerParams(has_side_effects=SIDE_EFFECT),
    )(*srcs, *lands, send_sems, recv_sems, *after)
    return outs[:n], outs[n:]


SMALL_PACK_ROWS = 16


def _all_reduce_small(rows, deps=()):
    n = len(rows)

    def body(*refs):
        ins = refs[:n]
        out_ref, mine, buf, send_sems, recv_sems = refs[n + len(deps):]
        x, y, c = _my_place()
        mine[...] = jnp.zeros_like(mine)
        for (r0, a), ref in zip(rows, ins):
            mine[r0:r0 + a.shape[0], 0:a.shape[1]] = ref[...]
        buf[_flat(x, y, c)] = mine[...]
        copies = []
        for r in range(1, N_DEV):
            peer = _peer(x, y, c, r)
            send = pltpu.make_async_remote_copy(
                src_ref=mine, dst_ref=buf.at[_flat(x, y, c)],
                send_sem=send_sems.at[r - 1], recv_sem=recv_sems.at[r - 1], device_id=peer, device_id_type=MESH)
            send.start()
            recv = pltpu.make_async_remote_copy(
                src_ref=mine, dst_ref=buf.at[_flat(*peer)],
                send_sem=send_sems.at[r - 1], recv_sem=recv_sems.at[r - 1], device_id=peer, device_id_type=MESH)
            copies.append((send, recv))
        for send, recv in copies:
            send.wait_send()
            recv.wait_recv()
        acc = buf[0]
        for s in range(1, N_DEV):
            acc = acc + buf[s]
        out_ref[...] = acc

    vm = pl.BlockSpec(memory_space=pltpu.VMEM)
    shape = (SMALL_PACK_ROWS, D_MODEL)
    return pl.pallas_call(
        body, name="all_reduce_small", in_specs=[vm] * n + [ANY] * len(deps), out_specs=vm,
        out_shape=_sds(shape, F32),
        scratch_shapes=[pltpu.VMEM(shape, F32), pltpu.VMEM((N_DEV,) + shape, F32),
                        pltpu.SemaphoreType.DMA((7,)), pltpu.SemaphoreType.DMA((7,))],
    )(*[a for _, a in rows], *deps)


def _adamw_math(w, g, m, v):
    m = ADAM_B1 * m + (1.0 - ADAM_B1) * g
    v = ADAM_B2 * v + (1.0 - ADAM_B2) * (g * g)
    m_hat = m / (1.0 - ADAM_B1 ** ADAM_STEP)
    v_hat = v / (1.0 - ADAM_B2 ** ADAM_STEP)
    delta = -ADAM_LR * (m_hat / (jnp.sqrt(v_hat) + ADAM_EPS) + ADAM_WD * w)
    return delta, m, v


def _adamw_big(name, w, m, v, srcs, lands, me):
    nl, rows, cols = w.shape
    tr = next(cand for cand in (256, 128, 64, 32, 16, 8) if rows % cand == 0)

    def body(me_ref, w_ref, m_ref, v_ref, *rest):
        src_refs, land_refs = rest[:nl], rest[nl:2 * nl]
        g_ref, d_ref, mo_ref, vo_ref = rest[2 * nl:]
        for layer in range(nl):
            @pl.when(pl.program_id(0) == layer)
            def _():
                g = src_refs[layer][...].astype(F32)
                for s in range(N_DEV - 1):
                    g = g + land_refs[layer][s].astype(F32)
                delta, mn, vn = _adamw_math(w_ref[...], g, m_ref[...], v_ref[...])
                g_ref[...] = g
                d_ref[...] = delta
                mo_ref[...] = mn
                vo_ref[...] = vn

    blk = pl.BlockSpec((None, tr, cols), lambda l, i, me_ref: (l, i, 0))
    at = lambda layer, l, i: jnp.where(l == layer, i, 0)
    own = [pl.BlockSpec((None, tr, cols), functools.partial(lambda layer, l, i, me_ref: (me_ref[0], at(layer, l, i), 0),
                                                            layer)) for layer in range(nl)]
    peers = [pl.BlockSpec((N_DEV - 1, tr, cols), functools.partial(lambda layer, l, i, me_ref: (0, at(layer, l, i), 0),
                                                                   layer)) for layer in range(nl)]
    return pl.pallas_call(
        body, name=name,
        grid_spec=pltpu.PrefetchScalarGridSpec(
            num_scalar_prefetch=1, grid=(nl, rows // tr),
            in_specs=[blk, blk, blk] + own + peers, out_specs=[blk] * 4),
        out_shape=[_sds((nl, rows, cols), F32)] * 4,
        compiler_params=_cparams(("arbitrary", "arbitrary")),
    )(me, w, m, v, *srcs, *lands)


def _adamw_small(ws, gs, ms, vs):
    n = len(ws)

    def body(*refs):
        w_refs, g_refs, m_refs, v_refs = (refs[i * n:(i + 1) * n] for i in range(4))
        d_out, m_out, v_out = (refs[(4 + i) * n:(5 + i) * n] for i in range(3))
        for i in range(n):
            delta, mn, vn = _adamw_math(w_refs[i][...], g_refs[i][...], m_refs[i][...], v_refs[i][...])
            d_out[i][...] = delta
            m_out[i][...] = mn
            v_out[i][...] = vn

    vm = pl.BlockSpec(memory_space=pltpu.VMEM)
    outs = pl.pallas_call(
        body, name="adamw_small", in_specs=[vm] * (4 * n), out_specs=[vm] * (3 * n),
        out_shape=[_sds(a.shape, F32) for a in ws] * 3,
    )(*ws, *gs, *ms, *vs)
    return outs[:n], outs[n:2 * n], outs[2 * n:]


SMALL_ROWS = 16


def _pad_to(a, rows, cols):
    return jnp.pad(a, ((0, rows - a.shape[0]), (0, cols - a.shape[1])))


def _place_own(blocks):
    me = _flat(*_my_place())
    return [lax.dynamic_update_slice(lax.empty((N_DEV,) + b.shape, b.dtype), b[None], (me,) + (0,) * b.ndim)
            for b in blocks]


def _ag_copies(x, y, c, blocks, bufs, send_sems, recv_sems):
    sends, recvs = [], []
    for a in range(len(blocks)):
        for r in range(1, N_DEV):
            peer = _peer(x, y, c, r)
            k = a * (N_DEV - 1) + r - 1
            make = lambda place: pltpu.make_async_remote_copy(
                src_ref=blocks[a], dst_ref=bufs[a].at[_flat(*place)],
                send_sem=send_sems.at[k], recv_sem=recv_sems.at[k], device_id=peer, device_id_type=MESH)
            sends.append(make((x, y, c)))
            recvs.append(make(peer))
    return sends, recvs


def _ag_start(groups, after):
    flat = [pair for g in groups for pair in g]
    n, ng = len(flat), len(groups)
    hbm = lambda a: pltpu.with_memory_space_constraint(a, pltpu.HBM)

    def body(*refs):
        blocks, bufs = refs[:n], refs[n:2 * n]
        sems = refs[2 * n + len(after):2 * n + len(after) + 2 * ng]
        x, y, c = _my_place()
        at = 0
        for gi, g in enumerate(groups):
            sends, _ = _ag_copies(x, y, c, blocks[at:at + len(g)], bufs[at:at + len(g)], sems[2 * gi], sems[2 * gi + 1])
            for cp in sends:
                cp.start()
            at += len(g)
        refs[-1][...] = jnp.zeros_like(refs[-1])

    sem_shapes = [pltpu.SemaphoreType.DMA((len(g) * (N_DEV - 1),)) for g in groups for _ in range(2)]
    outs = pl.pallas_call(
        body, name="gather_start",
        in_specs=[HBM] * (2 * n) + [ANY] * len(after),
        out_specs=[SEMS] * (2 * ng) + [HBM] * (2 * n) + [pl.BlockSpec(memory_space=pltpu.VMEM)],
        out_shape=sem_shapes + [pltpu.HBM(b.shape, b.dtype) for b, _ in flat]
        + [pltpu.HBM(u.shape, u.dtype) for _, u in flat] + [_sds((8, 128), F32)],
        input_output_aliases={i: 2 * ng + i for i in range(2 * n)},
        compiler_params=pltpu.CompilerParams(has_side_effects=SIDE_EFFECT),
    )(*[hbm(b) for b, _ in flat], *[hbm(u) for _, u in flat], *after)
    blocks_thru, bufs_thru = outs[2 * ng:2 * ng + n], outs[2 * ng + n:2 * ng + 2 * n]
    started, at = [], 0
    for gi, g in enumerate(groups):
        started.append((outs[2 * gi], outs[2 * gi + 1], blocks_thru[at:at + len(g)], bufs_thru[at:at + len(g)]))
        at += len(g)
    return started, outs[-1]


def _ag_wait(name, send_sems, recv_sems, blocks, bufs, after):
    n = len(blocks)

    def body(*refs):
        sends, recvs = _ag_copies(*_my_place(), refs[:n], refs[n:2 * n], refs[2 * n], refs[2 * n + 1])
        for s, r in zip(sends, recvs):
            s.wait_send()
            r.wait_recv()

    outs = pl.pallas_call(
        body, name=name,
        in_specs=[HBM] * (2 * n) + [SEMS, SEMS] + [ANY] * len(after),
        out_specs=[HBM] * (2 * n),
        out_shape=[pltpu.HBM(a.shape, a.dtype) for a in list(blocks) + list(bufs)],
        input_output_aliases={i: i for i in range(2 * n)},
        compiler_params=pltpu.CompilerParams(has_side_effects=SIDE_EFFECT),
    )(*blocks, *bufs, send_sems, recv_sems, *after)
    return outs[n:]


def _prepare_weights(p):
    n = N_DEV
    bf = lambda a: a.astype(BF16)
    gn_pack = jnp.concatenate([
        _pad_to(p['ret_gn'][0], RET_HEADS, 128), _pad_to(p['mla_q_a_norm'], 1, 128),
        _pad_to(p['mla_kv_a_norm'], 1, 128), jnp.zeros((2, 128), F32)], axis=0)
    ple = lambda l: [bf(p['ple_gate_w'][l]), bf(p['ple_proj_w'][l])]
    names = ('ret_out', 'mlp_w1_0', 'mlp_w2_0', 'ple_0', 'mla', 'layer_1')
    later = [[bf(p['ret_w_out'][0])], [bf(p['mlp_w1'][0])], [bf(p['mlp_w2'][0])], ple(0),
             [bf(p['mla_w_in'][0]), bf(p['mla_w_uq'][0]), bf(p['mla_w_ukv'][0]), bf(p['mla_w_out'][0])],
             [bf(p['mlp_w1'][1]), bf(p['mlp_w2'][1])] + ple(1)]
    bufs = _place_own([b for g in later for b in g])
    pack, wri = _all_gather([gn_pack, bf(p['ret_w_in'][0])])
    groups, at = [], 0
    for g in later:
        groups.append(list(zip(g, bufs[at:at + len(g)])))
        at += len(g)
    started, token = _ag_start(groups, (wri,))

    w = {k: p[k] for k in ('mix_norm', 'mlp_norm', 'ple_norm')}
    w['ret_gn'] = pack[:, :RET_HEADS, :RET_DV // n].transpose(1, 0, 2).reshape(RET_HEADS, RET_DV)
    w['mla_q_a_norm'] = pack[:, RET_HEADS, :MLA_Q_RANK // n].reshape(1, MLA_Q_RANK)
    w['mla_kv_a_norm'] = pack[:, RET_HEADS + 1, :MLA_KV_RANK // n].reshape(1, MLA_KV_RANK)
    w['ret_w_in'] = wri
    w['mla_q_norm'] = _pad_to(p['mla_q_norm'], 1, MLA_HD_PAD)
    w['mla_k_norm'] = _pad_to(p['mla_k_norm'], 1, MLA_HD_PAD)
    w['deps'] = (token,)

    def fetch(name, after):
        got = list(_ag_wait("gather_wait_" + name, *started[names.index(name)], after))
        if name == 'ret_out':
            return dict(ret_w_out=got[0].reshape(RET_V_W, D_MODEL))
        if name == 'mla':
            wmi, wuq, wukv, wmo = got
            return dict(mla_w_in=jnp.pad(wmi.reshape(D_MODEL, MLA_IN), ((0, 0), (0, MLA_IN_PAD - MLA_IN))),
                        mla_w_uq=jnp.pad(wuq, ((0, 0), (0, 0), (0, MLA_HD_PAD - MLA_QKD))),
                        mla_w_ukv=wukv, mla_w_out=wmo.reshape(D_MODEL, D_MODEL))
        out = {}
        if name in ('mlp_w1_0', 'layer_1'):
            out['mlp_w1'] = got.pop(0)
        if name in ('mlp_w2_0', 'layer_1'):
            out['mlp_w2'] = got.pop(0)
        if name in ('ple_0', 'layer_1'):
            out['ple_gate_w'] = got[0].reshape(D_MODEL, D_MODEL)
            out['ple_proj_w'] = got[1].transpose(1, 0, 2).reshape(PLE_DIM, D_MODEL)
        return out

    return w, fetch


def _small_grads(small, after):
    rows = [(0, small['mix_norm'][0]), (1, small['mix_norm'][1]), (2, small['mlp_norm'][0]),
            (3, small['mlp_norm'][1]), (4, small['ple_norm'][0]), (5, small['ple_norm'][1]),
            (6, small['ret_gn']), (10, small['mla_q_a_norm']), (11, small['mla_kv_a_norm']),
            (12, small['mla_q_norm']), (13, small['mla_k_norm'])]
    gs = _all_reduce_small(rows, after)
    me = _flat(*_my_place())
    n = N_DEV
    return dict(
        mix_norm=gs[0:2], mlp_norm=gs[2:4], ple_norm=gs[4:6],
        ret_gn=lax.dynamic_slice(gs, (6, me * (RET_DV // n)), (RET_HEADS, RET_DV // n)),
        mla_q_a_norm=lax.dynamic_slice(gs, (10, me * (MLA_Q_RANK // n)), (1, MLA_Q_RANK // n)),
        mla_kv_a_norm=lax.dynamic_slice(gs, (11, me * (MLA_KV_RANK // n)), (1, MLA_KV_RANK // n)),
        mla_q_norm=gs[12:13, :MLA_QKD], mla_k_norm=gs[13:14, :MLA_QKD])


def kernel(x, p, mix_norm, ret_w_in, ret_gn, ret_w_out, mla_w_in, mla_q_a_norm, mla_kv_a_norm, mla_w_uq, mla_w_ukv, mla_q_norm, mla_k_norm, mla_w_out, mlp_norm, mlp_w1, mlp_w2, ple_norm, ple_gate_w, ple_proj_w, loss_target, m_mix_norm, m_ret_w_in, m_ret_gn, m_ret_w_out, m_mla_w_in, m_mla_q_a_norm, m_mla_kv_a_norm, m_mla_w_uq, m_mla_w_ukv, m_mla_q_norm, m_mla_k_norm, m_mla_w_out, m_mlp_norm, m_mlp_w1, m_mlp_w2, m_ple_norm, m_ple_gate_w, m_ple_proj_w, v_mix_norm, v_ret_w_in, v_ret_gn, v_ret_w_out, v_mla_w_in, v_mla_q_a_norm, v_mla_kv_a_norm, v_mla_w_uq, v_mla_w_ukv, v_mla_q_norm, v_mla_k_norm, v_mla_w_out, v_mlp_norm, v_mlp_w1, v_mlp_w2, v_ple_norm, v_ple_gate_w, v_ple_proj_w):
    given = dict(locals())
    params = {n: given[n] for n in WEIGHTS}
    w, fetch = _prepare_weights(params)

    started = []

    def emit(group):
        keys = list(group)
        send, recv, srcs, lands, token = _rs_start(f"rs_start{len(started)}", [group[k] for k in keys])
        started.append((keys, send, recv, srcs, lands))
        return (token,)

    sq_err, grad_x, _, small = _local_step(x[0], p, loss_target[0], w, fetch, emit)
    loss = lax.psum(0.5 / D_MODEL * sq_err[0, 0], ("x", "y", "c"))

    grads, deltas, new_m, new_v = {}, {}, {}, {}

    def small_updates(after):
        sg = _small_grads(small, after)
        two_d = lambda a: a.reshape(-1, a.shape[-1])
        d_s, m_s, v_s = _adamw_small(
            [two_d(params[n]) for n in SMALL], [sg[n] for n in SMALL],
            [two_d(given["m_" + n]) for n in SMALL], [two_d(given["v_" + n]) for n in SMALL])
        for i, n in enumerate(SMALL):
            shape = params[n].shape
            grads[n], deltas[n], new_m[n], new_v[n] = (a.reshape(shape) for a in (sg[n], d_s[i], m_s[i], v_s[i]))
        return (d_s[0],)

    me = _flat(*_my_place()).astype(jnp.int32).reshape(1)
    after = (grad_x,)
    src_of, land_of = {}, {}
    for gi, (keys, send, recv, srcs, lands) in enumerate(started):
        if gi == len(started) - 1:
            after = small_updates(after)
        srcs, lands = _rs_wait(f"rs_wait{gi}", send, recv, srcs, lands, after)
        for k, s, l in zip(keys, srcs, lands):
            src_of[k], land_of[k] = s, l
        done = [n for n in BIG if n not in grads and all((n, l) in src_of for l in range(params[n].shape[0]))]
        for n in done:
            layers = range(params[n].shape[0])
            grads[n], deltas[n], new_m[n], new_v[n] = _adamw_big(
                "adamw_" + n, params[n], given["m_" + n], given["v_" + n],
                [src_of[(n, l)] for l in layers], [land_of[(n, l)] for l in layers], me)
        if done:
            after = tuple(deltas[n] for n in done)

    return (loss, grad_x[None], *[grads[n] for n in WEIGHTS], *[deltas[n] for n in WEIGHTS],
            *[new_m[n] for n in WEIGHTS], *[new_v[n] for n in WEIGHTS])
```

```python
import functools
import math

import jax
import jax.numpy as jnp
from jax import lax
from jax.experimental import pallas as pl
from jax.experimental.pallas import tpu as pltpu

F32 = jnp.float32
BF16 = jnp.bfloat16
MESH = pl.DeviceIdType.MESH
ANY = pl.BlockSpec(memory_space=pl.ANY)

N_DEV = 8
D_MODEL = 1024
CHUNK = 64
EPS = 1e-6
ROPE_THETA = 10000.0
RET_HEADS = 4
RET_DK = 256
RET_DV = 512
RET_QK_W = RET_HEADS * RET_DK
RET_V_W = RET_HEADS * RET_DV
RET_IN = 2 * RET_QK_W + 2 * RET_V_W
MLA_HEADS = 8
MLA_NOPE = 128
MLA_ROPE = 64
MLA_QKD = MLA_NOPE + MLA_ROPE
MLA_VD = 128
MLA_Q_RANK = 384
MLA_KV_RANK = 256
MLA_IN = MLA_Q_RANK + MLA_KV_RANK + MLA_ROPE
MLA_IN_PAD = 768
MLA_HD_PAD = 256
D_FF = 4096
PLE_DIM = 256
ATT_SCALE = MLA_QKD ** -0.5
LOG2E = 1.4426950408889634
ATT_EXP2 = ATT_SCALE * LOG2E

ADAM_LR = 0.001
ADAM_B1 = 0.9
ADAM_B2 = 0.999
ADAM_EPS = 1e-08
ADAM_WD = 0.01
ADAM_STEP = 10

VMEM_LIMIT = 52 * 1024 * 1024
ROW_TILE = 1024
RET_ROWS = 256
ATT_BLOCK = 256
ATT_QROWS = 1024
ATT_KROWS = 1024
ATT_HEADS = 2

WEIGHTS = ['mix_norm', 'ret_w_in', 'ret_gn', 'ret_w_out', 'mla_w_in', 'mla_q_a_norm', 'mla_kv_a_norm',
           'mla_w_uq', 'mla_w_ukv', 'mla_q_norm', 'mla_k_norm', 'mla_w_out', 'mlp_norm', 'mlp_w1', 'mlp_w2',
           'ple_norm', 'ple_gate_w', 'ple_proj_w']
BIG = ['ret_w_in', 'ret_w_out', 'mla_w_in', 'mla_w_uq', 'mla_w_ukv', 'mla_w_out', 'mlp_w1', 'mlp_w2',
       'ple_gate_w', 'ple_proj_w']
SMALL = [w for w in WEIGHTS if w not in BIG]


def _cparams(sem=None):
    return pltpu.CompilerParams(dimension_semantics=sem, vmem_limit_bytes=VMEM_LIMIT)


def _dot(a, b, ca, cb):
    return lax.dot_general(a, b, (((ca,), (cb,)), ((), ())), preferred_element_type=F32)


def _bf(v):
    return v if v.dtype == BF16 else v.astype(BF16)


def _sigmoid(z):
    return 1.0 / (1.0 + jnp.exp(-z))


def _mm(name, grid, a, a_spec, b, b_spec, contract, outs, extras=(), epi=None, deps=(), split=None):
    nk = grid[2]
    n_ex, n_out, n_dep = len(extras), len(outs), len(deps)
    acc_shape = tuple(d for d in outs[0][1].block_shape if d is not None)
    if split is not None:
        acc_shape = (acc_shape[1], acc_shape[0] * split)

    def body(*refs):
        a_ref, b_ref = refs[:2]
        ex_refs = refs[2:2 + n_ex]
        out_refs = refs[2 + n_ex + n_dep:2 + n_ex + n_dep + n_out]

        def product():
            return _dot(_bf(a_ref[...]), _bf(b_ref[...]), contract[0], contract[1])

        def finish(acc):
            if split is not None:
                for j in range(acc_shape[1] // split):
                    out_refs[0][j] = acc[:, j * split:(j + 1) * split].astype(out_refs[0].dtype)
                return
            acc = acc[...]
            res = epi(acc, *[r[...] for r in ex_refs]) if epi is not None else (acc,)
            for o, r in zip(out_refs, res):
                o[...] = r.astype(o.dtype)

        if nk == 1:
            finish(product())
        else:
            acc_ref = refs[-1]
            k = pl.program_id(2)

            @pl.when(k == 0)
            def _():
                acc_ref[...] = jnp.zeros_like(acc_ref)

            acc_ref[...] += product()

            @pl.when(k == nk - 1)
            def _():
                finish(acc_ref)

    return pl.pallas_call(
        body, name=name, grid=grid,
        in_specs=[a_spec, b_spec] + [s for _, s in extras] + [ANY] * n_dep,
        out_specs=[s for _, s in outs],
        out_shape=[s for s, _ in outs],
        scratch_shapes=[pltpu.VMEM(acc_shape, F32)] if nk > 1 else [],
        compiler_params=_cparams(("parallel", "parallel", "arbitrary")),
    )(a, b, *[x for x, _ in extras], *deps)


def _mm_rows(name, tm, a, w, mode, outs, extras=(), epi=None, deps=()):
    n_sh, rows, cols = w.shape
    n_ex, n_out, n_dep = len(extras), len(outs), len(deps)
    by_cols = mode in ('nn_cols', 'nt_rows')
    width = cols if mode == 'nn_cols' else rows

    def body(*refs):
        a_ref, w_ref = refs[:2]
        ex_refs = refs[2:2 + n_ex]
        out_refs = refs[2 + n_ex + n_dep:2 + n_ex + n_dep + n_out]
        if by_cols:
            av = _bf(a_ref[...])
            for s in range(n_sh):
                cs = slice(s * width, (s + 1) * width)
                acc = _dot(av, w_ref[s], 1, 0 if mode == 'nn_cols' else 1)
                res = epi(acc, *[r[:, cs] for r in ex_refs]) if epi is not None else (acc,)
                for o, r in zip(out_refs, res):
                    o[:, cs] = r.astype(o.dtype)
        else:
            chunk = rows if mode == 'nn_rows' else cols
            acc = None
            for s in range(n_sh):
                part = _dot(_bf(a_ref[:, s * chunk:(s + 1) * chunk]), w_ref[s], 1, 0 if mode == 'nn_rows' else 1)
                acc = part if acc is None else acc + part
            res = epi(acc, *[r[...] for r in ex_refs]) if epi is not None else (acc,)
            for o, r in zip(out_refs, res):
                o[...] = r.astype(o.dtype)

    t, ka = a.shape
    return pl.pallas_call(
        body, name=name, grid=(t // tm, 1, 1),
        in_specs=[pl.BlockSpec((tm, ka), lambda i, j, k: (i, 0)),
                  pl.BlockSpec((n_sh, rows, cols), lambda i, j, k: (0, 0, 0))] + [s for _, s in extras] + [ANY] * n_dep,
        out_specs=[s for _, s in outs],
        out_shape=[s for s, _ in outs],
        compiler_params=_cparams(("parallel", "arbitrary", "arbitrary")),
    )(a, w, *[x for x, _ in extras], *deps)


def _sds(shape, dtype):
    return jax.ShapeDtypeStruct(shape, dtype)


def _row_tile(t, cap=ROW_TILE):
    return min(cap, t)


def _rms_fwd(name, x, g):
    t, d = x.shape
    tm = _row_tile(t)

    def body(x_ref, g_ref, o_ref):
        xv = x_ref[...]
        r = lax.rsqrt(jnp.mean(xv * xv, axis=-1, keepdims=True) + EPS)
        o_ref[...] = (xv * r * g_ref[...]).astype(o_ref.dtype)

    return pl.pallas_call(
        body, name=name, grid=(t // tm,),
        in_specs=[pl.BlockSpec((tm, d), lambda i: (i, 0)), pl.BlockSpec((1, d), lambda i: (0, 0))],
        out_specs=pl.BlockSpec((tm, d), lambda i: (i, 0)),
        out_shape=_sds((t, d), BF16),
        compiler_params=_cparams(("parallel",)),
    )(x, g)


def _rms_bwd_rows(dy, xv, g, n):
    r = lax.rsqrt(jnp.sum(xv * xv, axis=-1, keepdims=True) / n + EPS)
    xh = xv * r
    dxh = dy * g
    dx = r * (dxh - xh * (jnp.sum(dxh * xh, axis=-1, keepdims=True) / n))
    return dx, dy * xh


def _ple_gate_bwd(name, dh, gate, e):
    t, d = dh.shape
    tm = _row_tile(t)

    def body(dh_ref, g_ref, e_ref, de_ref, dz_ref):
        dh_v, gt = dh_ref[...], g_ref[...].astype(F32)
        de_ref[...] = (dh_v * gt).astype(BF16)
        dz_ref[...] = (dh_v * e_ref[...].astype(F32) * (gt * (1.0 - gt))).astype(BF16)

    row = pl.BlockSpec((tm, d), lambda i: (i, 0))
    return pl.pallas_call(
        body, name=name, grid=(t // tm,), in_specs=[row, row, row], out_specs=[row, row],
        out_shape=[_sds((t, d), BF16), _sds((t, d), BF16)],
        compiler_params=_cparams(("parallel",)),
    )(dh, gate, e)


def _rope_half(v, cos, sin):
    half = v.shape[-1] // 2
    v1, v2 = v[:, :half], v[:, half:]
    return jnp.concatenate([v1 * cos - v2 * sin, v2 * cos + v1 * sin], axis=-1)


def _ret_consts():
    lg = jnp.log(1.0 - 2.0 ** (-5.0 - jnp.arange(RET_HEADS, dtype=F32)))
    idx = jnp.arange(CHUNK, dtype=F32)
    intra = jnp.exp(lg[:, None, None] * jnp.abs(idx[:, None] - idx[None, :]))
    qdec = jnp.exp(lg[:, None] * (idx + 1.0))
    kdec = jnp.exp(lg[:, None] * (CHUNK - 1.0 - idx))
    cdec = jnp.exp(lg * CHUNK)
    qdec = jnp.broadcast_to(qdec[:, :, None], (RET_HEADS, CHUNK, RET_DK))
    kdec = jnp.broadcast_to(kdec[:, :, None], (RET_HEADS, CHUNK, RET_DK))
    cdec = jnp.broadcast_to(cdec[:, None, None], (RET_HEADS, 1, RET_DV))
    return intra, qdec, kdec, cdec


def _ret_specs(rb, rev_nb=None):
    blk = (lambda i: i) if rev_nb is None else (lambda i: rev_nb - 1 - i)
    full = lambda shape: pl.BlockSpec(shape, lambda i: (0,) * len(shape))
    return dict(
        proj=pl.BlockSpec((rb, RET_IN), lambda i: (blk(i), 0)),
        tab=pl.BlockSpec((rb, RET_DK // 2), lambda i: (blk(i), 0)),
        vw=pl.BlockSpec((rb, RET_V_W), lambda i: (blk(i), 0)),
        st=pl.BlockSpec((rb // CHUNK, RET_HEADS, RET_DK, RET_DV), lambda i: (blk(i), 0, 0, 0)),
        gn=full((RET_HEADS, 1, RET_DV)),
        intra=full((RET_HEADS, CHUNK, CHUNK)),
        dec=full((RET_HEADS, CHUNK, RET_DK)),
        cdec=full((RET_HEADS, 1, RET_DV)),
    )


def _ret_fwd(proj, cos, sin, gn):
    t = proj.shape[0]
    rb = min(RET_ROWS, t)
    cpb = rb // CHUNK
    intra, qdec, kdec, cdec = _ret_consts()
    sp = _ret_specs(rb)

    def body(proj_ref, cos_ref, sin_ref, gn_ref, intra_ref, qd_ref, kd_ref, cd_ref,
             gated_ref, outp_ref, st_ref, s_ref):
        @pl.when(pl.program_id(0) == 0)
        def _():
            s_ref[...] = jnp.zeros_like(s_ref)

        def chunk(c, carry):
            rows = pl.ds(pl.multiple_of(c * CHUNK, CHUNK), CHUNK)
            cs, sn = cos_ref[rows, :], sin_ref[rows, :]
            for h in range(RET_HEADS):
                q = proj_ref[rows, h * RET_DK:(h + 1) * RET_DK].astype(F32)
                k = proj_ref[rows, RET_QK_W + h * RET_DK:RET_QK_W + (h + 1) * RET_DK].astype(F32)
                v = proj_ref[rows, 2 * RET_QK_W + h * RET_DV:2 * RET_QK_W + (h + 1) * RET_DV]
                g = proj_ref[rows, 2 * RET_QK_W + RET_V_W + h * RET_DV:
                             2 * RET_QK_W + RET_V_W + (h + 1) * RET_DV].astype(F32)
                qr = _rope_half(q, cs, sn)
                kr = _rope_half(k, cs, sn) * (RET_DK ** -0.5)
                qb, kb, vb = qr.astype(BF16), kr.astype(BF16), v
                sc = _dot(qb, kb, 1, 1) * intra_ref[h]
                inner = _dot(sc.astype(BF16), vb, 1, 0)
                s_old = s_ref[h]
                sb = s_old.astype(BF16)
                st_ref[c, h] = sb
                cross = _dot((qr * qd_ref[h]).astype(BF16), sb, 1, 0)
                out = inner + cross
                s_ref[h] = s_old * cd_ref[h] + _dot((kr * kd_ref[h]).astype(BF16), vb, 0, 0)
                r = lax.rsqrt(jnp.mean(out * out, axis=-1, keepdims=True) + EPS)
                y = out * r * gn_ref[h]
                cols = slice(h * RET_DV, (h + 1) * RET_DV)
                gated_ref[rows, cols] = (g * _sigmoid(g) * y).astype(BF16)
                outp_ref[rows, cols] = out
            return carry

        lax.fori_loop(0, cpb, chunk, 0)

    return pl.pallas_call(
        body, name="ret_fwd", grid=(t // rb,),
        in_specs=[sp['proj'], sp['tab'], sp['tab'], sp['gn'], sp['intra'], sp['dec'], sp['dec'], sp['cdec']],
        out_specs=[sp['vw'], sp['vw'], sp['st']],
        out_shape=[_sds((t, RET_V_W), BF16), _sds((t, RET_V_W), F32),
                   _sds((t // CHUNK, RET_HEADS, RET_DK, RET_DV), BF16)],
        scratch_shapes=[pltpu.VMEM((RET_HEADS, RET_DK, RET_DV), F32)],
        compiler_params=_cparams(("arbitrary",)),
    )(proj, cos, sin, gn.reshape(RET_HEADS, 1, RET_DV), intra, qdec, kdec, cdec)


def _ret_bwd(proj, cos, sin, gn, outp, states, dgated, deps=()):
    t = proj.shape[0]
    rb = min(RET_ROWS, t)
    cpb = rb // CHUNK
    nb = t // rb
    intra, qdec, kdec, cdec = _ret_consts()
    sp = _ret_specs(rb, rev_nb=nb)

    def body(proj_ref, cos_ref, sin_ref, gn_ref, intra_ref, qd_ref, kd_ref, cd_ref, outp_ref, st_ref, dgt_ref, *rest):
        dproj_ref, dgn_ref, ds_ref = rest[len(deps):]
        @pl.when(pl.program_id(0) == 0)
        def _():
            ds_ref[...] = jnp.zeros_like(ds_ref)
            dgn_ref[...] = jnp.zeros_like(dgn_ref)

        def chunk(cc, carry):
            c = cpb - 1 - cc
            rows = pl.ds(pl.multiple_of(c * CHUNK, CHUNK), CHUNK)
            cs, sn = cos_ref[rows, :], sin_ref[rows, :]
            for h in range(RET_HEADS):
                q = proj_ref[rows, h * RET_DK:(h + 1) * RET_DK].astype(F32)
                k = proj_ref[rows, RET_QK_W + h * RET_DK:RET_QK_W + (h + 1) * RET_DK].astype(F32)
                v = proj_ref[rows, 2 * RET_QK_W + h * RET_DV:2 * RET_QK_W + (h + 1) * RET_DV]
                g = proj_ref[rows, 2 * RET_QK_W + RET_V_W + h * RET_DV:
                             2 * RET_QK_W + RET_V_W + (h + 1) * RET_DV].astype(F32)
                cols = slice(h * RET_DV, (h + 1) * RET_DV)
                qr = _rope_half(q, cs, sn)
                kr = _rope_half(k, cs, sn) * (RET_DK ** -0.5)
                qb, kb, vb = qr.astype(BF16), kr.astype(BF16), v
                qdb = (qr * qd_ref[h]).astype(BF16)
                kdb = (kr * kd_ref[h]).astype(BF16)
                out = outp_ref[rows, cols]
                dgt = dgt_ref[rows, cols]
                gnh = gn_ref[h]
                r = lax.rsqrt(jnp.mean(out * out, axis=-1, keepdims=True) + EPS)
                xh = out * r
                sg = _sigmoid(g)
                dgate = dgt * (xh * gnh) * (sg * (1.0 + g * (1.0 - sg)))
                dy = dgt * (g * sg)
                dgn_ref[h] += jnp.sum(dy * xh, axis=0, keepdims=True)
                dxh = dy * gnh
                dout = r * (dxh - xh * jnp.mean(dxh * xh, axis=-1, keepdims=True))
                doutb = dout.astype(BF16)
                itr = intra_ref[h]
                pb = (_dot(qb, kb, 1, 1) * itr).astype(BF16)
                dv = _dot(pb, doutb, 0, 0)
                dsc = (_dot(doutb, vb, 1, 1) * itr).astype(BF16)
                dq = _dot(dsc, kb, 1, 0)
                dk = _dot(dsc, qb, 0, 0)
                dq = dq + _dot(doutb, st_ref[c, h], 1, 1) * qd_ref[h]
                ds_new = ds_ref[h]
                dsb = ds_new.astype(BF16)
                dk = dk + _dot(vb, dsb, 1, 1) * kd_ref[h]
                dv = dv + _dot(kdb, dsb, 1, 0)
                ds_ref[h] = ds_new * cd_ref[h] + _dot(qdb, doutb, 0, 0)
                dproj_ref[rows, h * RET_DK:(h + 1) * RET_DK] = _rope_half(dq, cs, -sn).astype(BF16)
                dproj_ref[rows, RET_QK_W + h * RET_DK:RET_QK_W + (h + 1) * RET_DK] = (
                    _rope_half(dk * (RET_DK ** -0.5), cs, -sn).astype(BF16))
                dproj_ref[rows, 2 * RET_QK_W + h * RET_DV:2 * RET_QK_W + (h + 1) * RET_DV] = dv.astype(BF16)
                dproj_ref[rows, 2 * RET_QK_W + RET_V_W + h * RET_DV:
                          2 * RET_QK_W + RET_V_W + (h + 1) * RET_DV] = dgate.astype(BF16)
            return carry

        lax.fori_loop(0, cpb, chunk, 0)

    return pl.pallas_call(
        body, name="ret_bwd", grid=(nb,),
        in_specs=[sp['proj'], sp['tab'], sp['tab'], sp['gn'], sp['intra'], sp['dec'], sp['dec'], sp['cdec'],
                  sp['vw'], sp['st'], sp['vw']] + [ANY] * len(deps),
        out_specs=[sp['proj'], sp['gn']],
        out_shape=[_sds((t, RET_IN), BF16), _sds((RET_HEADS, 1, RET_DV), F32)],
        scratch_shapes=[pltpu.VMEM((RET_HEADS, RET_DK, RET_DV), F32)],
        compiler_params=_cparams(("arbitrary",)),
    )(proj, cos, sin, gn.reshape(RET_HEADS, 1, RET_DV), intra, qdec, kdec, cdec, outp, states, dgated, *deps)


def _mla_tables(t):
    half = MLA_ROPE // 2
    inv = 1.0 / (ROPE_THETA ** (jnp.arange(0, MLA_ROPE, 2, dtype=F32) / MLA_ROPE))
    ang = jnp.arange(t, dtype=F32)[:, None] * inv[None, :]
    cos, sin = jnp.cos(ang), jnp.sin(ang)
    z = jnp.zeros((t, half), F32)
    c = jnp.concatenate([cos, cos, z, z], axis=1)
    s1 = jnp.concatenate([-sin, z, z, z], axis=1)
    s2 = jnp.concatenate([z, sin, z, z], axis=1)
    return c, s1, s2


def _rope_tile(r, c, s1, s2):
    return r * c + pltpu.roll(r, 96, 1) * s1 + pltpu.roll(r, 32, 1) * s2


def _mla_mid(proj2, qa, kva):
    t = proj2.shape[0]
    tm = _row_tile(t)

    def body(p_ref, qa_ref, kva_ref, cq_ref, ckv_ref):
        cq = p_ref[:, :MLA_Q_RANK]
        ckv = p_ref[:, MLA_Q_RANK:MLA_Q_RANK + MLA_KV_RANK]
        rq = lax.rsqrt(jnp.mean(cq * cq, axis=-1, keepdims=True) + EPS)
        rkv = lax.rsqrt(jnp.mean(ckv * ckv, axis=-1, keepdims=True) + EPS)
        cq_ref[...] = (cq * rq * qa_ref[...]).astype(BF16)
        ckv_ref[...] = (ckv * rkv * kva_ref[...]).astype(BF16)

    return pl.pallas_call(
        body, name="mla_mid", grid=(t // tm,),
        in_specs=[pl.BlockSpec((tm, MLA_IN_PAD), lambda i: (i, 0)),
                  pl.BlockSpec((1, MLA_Q_RANK), lambda i: (0, 0)),
                  pl.BlockSpec((1, MLA_KV_RANK), lambda i: (0, 0))],
        out_specs=[pl.BlockSpec((tm, MLA_Q_RANK), lambda i: (i, 0)),
                   pl.BlockSpec((tm, MLA_KV_RANK), lambda i: (i, 0))],
        out_shape=[_sds((t, MLA_Q_RANK), BF16), _sds((t, MLA_KV_RANK), BF16)],
        compiler_params=_cparams(("parallel",)),
    )(proj2, qa, kva)


def _mla_mid_bwd(proj2, qa, kva, dcq, dckv, dkr):
    t = proj2.shape[0]
    tm = _row_tile(t)

    def body(p_ref, qa_ref, kva_ref, dcq_ref, dckv_ref, dkr_ref, dp_ref, dqa_ref, dkva_ref):
        @pl.when(pl.program_id(0) == 0)
        def _():
            dqa_ref[...] = jnp.zeros_like(dqa_ref)
            dkva_ref[...] = jnp.zeros_like(dkva_ref)

        dxq, dgq = _rms_bwd_rows(dcq_ref[...], p_ref[:, :MLA_Q_RANK], qa_ref[...], MLA_Q_RANK)
        dxk, dgk = _rms_bwd_rows(dckv_ref[...], p_ref[:, MLA_Q_RANK:MLA_Q_RANK + MLA_KV_RANK], kva_ref[...],
                                 MLA_KV_RANK)
        dp_ref[:, :MLA_Q_RANK] = dxq.astype(BF16)
        dp_ref[:, MLA_Q_RANK:MLA_Q_RANK + MLA_KV_RANK] = dxk.astype(BF16)
        dp_ref[:, MLA_Q_RANK + MLA_KV_RANK:] = dkr_ref[...].astype(BF16)
        dqa_ref[...] += jnp.sum(dgq, axis=0, keepdims=True)
        dkva_ref[...] += jnp.sum(dgk, axis=0, keepdims=True)

    return pl.pallas_call(
        body, name="mla_mid_bwd", grid=(t // tm,),
        in_specs=[pl.BlockSpec((tm, MLA_IN_PAD), lambda i: (i, 0)),
                  pl.BlockSpec((1, MLA_Q_RANK), lambda i: (0, 0)),
                  pl.BlockSpec((1, MLA_KV_RANK), lambda i: (0, 0)),
                  pl.BlockSpec((tm, MLA_Q_RANK), lambda i: (i, 0)),
                  pl.BlockSpec((tm, MLA_KV_RANK), lambda i: (i, 0)),
                  pl.BlockSpec((tm, 128), lambda i: (i, 0))],
        out_specs=[pl.BlockSpec((tm, MLA_IN_PAD), lambda i: (i, 0)),
                   pl.BlockSpec((1, MLA_Q_RANK), lambda i: (0, 0)),
                   pl.BlockSpec((1, MLA_KV_RANK), lambda i: (0, 0))],
        out_shape=[_sds((t, MLA_IN_PAD), BF16), _sds((1, MLA_Q_RANK), F32), _sds((1, MLA_KV_RANK), F32)],
        compiler_params=_cparams(("arbitrary",)),
    )(proj2, qa, kva, dcq, dckv, dkr)


def _mla_prep_specs(t, tm):
    head = lambda w: pl.BlockSpec((None, tm, w), lambda i, h: (h, i, 0))
    return dict(
        head256=head(MLA_HD_PAD), head128=head(MLA_VD),
        cols256=pl.BlockSpec((tm, MLA_HD_PAD), lambda i, h: (i, h)),
        cq=pl.BlockSpec((tm, MLA_Q_RANK), lambda i, h: (i, 0)),
        ckv=pl.BlockSpec((tm, MLA_KV_RANK), lambda i, h: (i, 0)),
        wuq=pl.BlockSpec((None, MLA_Q_RANK, MLA_HD_PAD), lambda i, h: (h, 0, 0)),
        wukv=pl.BlockSpec((None, MLA_KV_RANK, MLA_HD_PAD), lambda i, h: (h, 0, 0)),
        kr=pl.BlockSpec((tm, 128), lambda i, h: (i, (MLA_Q_RANK + MLA_KV_RANK) // 128)),
        gain=pl.BlockSpec((1, MLA_HD_PAD), lambda i, h: (0, 0)),
        tab=pl.BlockSpec((tm, 128), lambda i, h: (i, 0)),
    )


def _mla_prep(cq, ckv, wuq, wukv, proj2, gq, gk, tabs):
    t = cq.shape[0]
    tm = _row_tile(t)
    sp = _mla_prep_specs(t, tm)

    def body(cq_ref, ckv_ref, wuq_ref, wukv_ref, kr_ref, gq_ref, gk_ref, c_ref, s1_ref, s2_ref,
             qh_ref, kh_ref, vh_ref):
        c, s1, s2 = c_ref[...], s1_ref[...], s2_ref[...]

        def norm_rope(xv, gain):
            r = lax.rsqrt(jnp.sum(xv * xv, axis=-1, keepdims=True) / MLA_QKD + EPS)
            y = xv * r * gain
            return jnp.concatenate([y[:, :MLA_NOPE], _rope_tile(y[:, MLA_NOPE:], c, s1, s2)], axis=-1)

        kvv = _dot(ckv_ref[...], wukv_ref[...], 1, 0)
        qh_ref[...] = norm_rope(_dot(cq_ref[...], wuq_ref[...], 1, 0), gq_ref[...]).astype(BF16)
        kf = jnp.concatenate([kvv[:, :MLA_NOPE], kr_ref[...]], axis=-1)
        kh_ref[...] = norm_rope(kf, gk_ref[...]).astype(BF16)
        vh_ref[...] = jnp.concatenate([kvv[:, MLA_NOPE:], jnp.ones((tm, MLA_VD), F32)], axis=-1).astype(BF16)

    return pl.pallas_call(
        body, name="mla_prep", grid=(t // tm, MLA_HEADS),
        in_specs=[sp['cq'], sp['ckv'], sp['wuq'], sp['wukv'], sp['kr'], sp['gain'], sp['gain'],
                  sp['tab'], sp['tab'], sp['tab']],
        out_specs=[sp['head256'], sp['head256'], sp['head256']],
        out_shape=[_sds((MLA_HEADS, t, MLA_HD_PAD), BF16), _sds((MLA_HEADS, t, MLA_HD_PAD), BF16),
                   _sds((MLA_HEADS, t, 2 * MLA_VD), BF16)],
        compiler_params=_cparams(("parallel", "arbitrary")),
    )(cq, ckv, wuq, wukv, proj2, gq, gk, *tabs)


def _mla_prep_bwd(cq, ckv, wuq, wukv, proj2, gq, gk, tabs, dqt, dkh, dvh):
    t = cq.shape[0]
    tm = _row_tile(t)
    ab = dqt.shape[-1]
    sp = _mla_prep_specs(t, tm)

    def body(cq_ref, ckv_ref, wuq_ref, wukv_ref, kr_ref, gq_ref, gk_ref, c_ref, s1_ref, s2_ref,
             dqt_ref, dkh_ref, dvh_ref, dq_ref, dkv_ref, dkr_ref, dgq_ref, dgk_ref):
        dqh = jnp.concatenate([dqt_ref[b].T for b in range(tm // ab)], axis=0)
        i, h = pl.program_id(0), pl.program_id(1)

        @pl.when((i == 0) & (h == 0))
        def _():
            dgq_ref[...] = jnp.zeros_like(dgq_ref)
            dgk_ref[...] = jnp.zeros_like(dgk_ref)

        @pl.when(h == 0)
        def _():
            dkr_ref[...] = jnp.zeros_like(dkr_ref)

        c, s1, s2 = c_ref[...], s1_ref[...], s2_ref[...]

        def back(xv, gain, dout):
            dy = jnp.concatenate([dout[:, :MLA_NOPE], _rope_tile(dout[:, MLA_NOPE:], c, -s1, -s2)], axis=-1)
            return _rms_bwd_rows(dy, xv, gain, MLA_QKD)

        kvv = _dot(ckv_ref[...], wukv_ref[...], 1, 0)
        dxq, dgq = back(_dot(cq_ref[...], wuq_ref[...], 1, 0), gq_ref[...], dqh)
        kf = jnp.concatenate([kvv[:, :MLA_NOPE], kr_ref[...]], axis=-1)
        dxk, dgk = back(kf, gk_ref[...], dkh_ref[...])
        dq_ref[...] = dxq.astype(BF16)
        dkv_ref[...] = jnp.concatenate([dxk[:, :MLA_NOPE], dvh_ref[...]], axis=-1).astype(BF16)
        dkr_ref[...] += dxk[:, MLA_NOPE:]
        dgq_ref[...] += jnp.sum(dgq, axis=0, keepdims=True)
        dgk_ref[...] += jnp.sum(dgk, axis=0, keepdims=True)

    return pl.pallas_call(
        body, name="mla_prep_bwd", grid=(t // tm, MLA_HEADS),
        in_specs=[sp['cq'], sp['ckv'], sp['wuq'], sp['wukv'], sp['kr'], sp['gain'], sp['gain'],
                  sp['tab'], sp['tab'], sp['tab'],
                  pl.BlockSpec((None, tm // ab, MLA_HD_PAD, ab), lambda i, h: (h, i, 0, 0)),
                  sp['head256'], sp['head128']],
        out_specs=[sp['cols256'], sp['cols256'], sp['tab'], sp['gain'], sp['gain']],
        out_shape=[_sds((t, MLA_HEADS * MLA_HD_PAD), BF16), _sds((t, MLA_HEADS * MLA_HD_PAD), BF16),
                   _sds((t, 128), F32), _sds((1, MLA_HD_PAD), F32), _sds((1, MLA_HD_PAD), F32)],
        compiler_params=_cparams(("arbitrary", "arbitrary")),
    )(cq, ckv, wuq, wukv, proj2, gq, gk, *tabs, dqt, dkh, dvh)


def _chunk_visible(rows, cols, row_off, col_off):
    rq = lax.shift_right_logical(lax.broadcasted_iota(jnp.int32, (rows, cols), 0) + row_off, 6)
    ck = lax.shift_right_logical(lax.broadcasted_iota(jnp.int32, (rows, cols), 1) + col_off, 6)
    return ck <= rq


def _rows_to_lanes(col):
    return col.T[:8, :]


def _attn_fwd(qh, kh, vh):
    t = qh.shape[1]
    ab = min(ATT_BLOCK, t)
    tq = min(ATT_QROWS, t)
    r = tq // ab
    hg = ATT_HEADS

    def body(q_ref, k_ref, v_ref, o_ref, lse_ref):
        n_un = pl.program_id(1) * r

        def step(b, state, diag):
            rows = pl.ds(pl.multiple_of(b * ab, ab), ab)
            ms, accs = [], []
            for hh in range(hg):
                m, acc = state[0][hh], state[1][hh]
                s = _dot(q_ref[hh], k_ref[hh, rows, :], 1, 1)
                if diag is not None:
                    s = jnp.where(_chunk_visible(tq, ab, 0, diag * ab), s, -1e30)
                m_new = jnp.maximum(m, jnp.max(s, axis=-1, keepdims=True))
                p = jnp.exp2((s - m_new) * ATT_EXP2).astype(BF16)
                accs.append(jnp.exp2((m - m_new) * ATT_EXP2) * acc + _dot(p, v_ref[hh, rows, :], 1, 0))
                ms.append(m_new)
            return tuple(ms), tuple(accs)

        heads = lambda v: tuple(v for _ in range(hg))
        state = (heads(jnp.full((tq, 1), -1e30, F32)), heads(jnp.zeros((tq, 2 * MLA_VD), F32)))
        state = lax.fori_loop(0, n_un, lambda b, st: step(b, st, None), state)
        for d in range(r):
            state = step(n_un + d, state, d)
        ms, accs = state
        for hh in range(hg):
            l = accs[hh][:, MLA_VD:]
            o_ref[:, hh * MLA_VD:(hh + 1) * MLA_VD] = accs[hh][:, :MLA_VD] / l
            lse_t = _rows_to_lanes(ms[hh] * ATT_EXP2 + jnp.log(l) * LOG2E)
            for d in range(r):
                lse_ref[hh, d] = lse_t[:, d * ab:(d + 1) * ab]

    return pl.pallas_call(
        body, name="mla_attn", grid=(MLA_HEADS // hg, t // tq),
        in_specs=[pl.BlockSpec((hg, tq, MLA_HD_PAD), lambda g, i: (g, i, 0)),
                  pl.BlockSpec((hg, t, MLA_HD_PAD), lambda g, i: (g, 0, 0)),
                  pl.BlockSpec((hg, t, 2 * MLA_VD), lambda g, i: (g, 0, 0))],
        out_specs=[pl.BlockSpec((tq, hg * MLA_VD), lambda g, i: (i, g)),
                   pl.BlockSpec((hg, r, 8, ab), lambda g, i: (g, i, 0, 0))],
        out_shape=[_sds((t, MLA_HEADS * MLA_VD), F32), _sds((MLA_HEADS, t // ab, 8, ab), F32)],
        compiler_params=_cparams(("parallel", "arbitrary")),
    )(qh, kh, vh)


def _attn_delta(do, o, ab):
    t = do.shape[0]
    tm = _row_tile(t)

    def body(do_ref, o_ref, d_ref):
        d = jnp.sum(do_ref[...] * o_ref[...], axis=-1, keepdims=True)
        d_t = _rows_to_lanes(jnp.broadcast_to(d, (tm, 128)))
        for b in range(tm // ab):
            d_ref[b] = d_t[:, b * ab:(b + 1) * ab]

    col = pl.BlockSpec((tm, MLA_VD), lambda i, h: (i, h))
    return pl.pallas_call(
        body, name="mla_delta", grid=(t // tm, MLA_HEADS), in_specs=[col, col],
        out_specs=pl.BlockSpec((None, tm // ab, 8, ab), lambda i, h: (h, i, 0, 0)),
        out_shape=_sds((MLA_HEADS, t // ab, 8, ab), F32),
        compiler_params=_cparams(("parallel", "parallel")),
    )(do, o)


def _attn_bwd(qh, kh, vh, dob, lse_t, dl_t):
    t = qh.shape[1]
    ab = min(ATT_BLOCK, t)
    kb = min(ATT_KROWS, t)
    r = kb // ab
    nq = t // ab
    hg = ATT_HEADS

    def body(q_ref, k_ref, v_ref, do_ref, lse_ref, dl_ref, dqt_ref, dk_ref, dv_ref):
        j = pl.program_id(1)

        @pl.when(j == 0)
        def _():
            dqt_ref[...] = jnp.zeros_like(dqt_ref)

        ks = [k_ref[hh] for hh in range(hg)]
        vs = [v_ref[hh, :, :MLA_VD] for hh in range(hg)]
        kts = [k.T for k in ks]

        def step(b, grads, diag):
            rows = pl.ds(pl.multiple_of(b * ab, ab), ab)
            hi = kb if diag is None else (diag + 1) * ab
            out = []
            for hh in range(hg):
                dk_all, dv_all = grads[hh]
                q = q_ref[hh, rows, :]
                do = do_ref[rows, hh * MLA_VD:(hh + 1) * MLA_VD]
                s_t = _dot(ks[hh][:hi], q, 1, 1)
                if diag is not None:
                    key_chunk = lax.shift_right_logical(lax.broadcasted_iota(jnp.int32, (hi, ab), 0), 6)
                    query_chunk = lax.shift_right_logical(
                        lax.broadcasted_iota(jnp.int32, (hi, ab), 1) + diag * ab, 6)
                    s_t = jnp.where(key_chunk <= query_chunk, s_t, -1e30)
                p_t = jnp.exp2(s_t * ATT_EXP2 - lse_ref[hh, b][0:1, :])
                dp_t = _dot(vs[hh][:hi], do, 1, 1)
                ds_t = (p_t * (dp_t - dl_ref[hh, b][0:1, :]) * ATT_SCALE).astype(BF16)
                dqt_ref[hh, b] += _dot(kts[hh][:, :hi], ds_t, 1, 0)
                dk = dk_all[:hi] + _dot(ds_t, q, 1, 0)
                dv = dv_all[:hi] + _dot(p_t.astype(BF16), do, 1, 0)
                if hi < kb:
                    dk = jnp.concatenate([dk, dk_all[hi:]], axis=0)
                    dv = jnp.concatenate([dv, dv_all[hi:]], axis=0)
                out.append((dk, dv))
            return tuple(out)

        grads = tuple((jnp.zeros((kb, MLA_HD_PAD), F32), jnp.zeros((kb, MLA_VD), F32)) for _ in range(hg))
        for d in range(r):
            grads = step(j * r + d, grads, d)
        grads = lax.fori_loop((j + 1) * r, nq, lambda b, g: step(b, g, None), grads)
        for hh in range(hg):
            dk_ref[hh] = grads[hh][0]
            dv_ref[hh] = grads[hh][1]

    whole = lambda w: pl.BlockSpec((hg, t, w), lambda g, j: (g, 0, 0))
    blk = lambda w: pl.BlockSpec((hg, kb, w), lambda g, j: (g, j, 0))
    stat = pl.BlockSpec((hg, nq, 8, ab), lambda g, j: (g, 0, 0, 0))
    return pl.pallas_call(
        body, name="mla_attn_bwd", grid=(MLA_HEADS // hg, t // kb),
        in_specs=[whole(MLA_HD_PAD), blk(MLA_HD_PAD), blk(2 * MLA_VD),
                  pl.BlockSpec((t, hg * MLA_VD), lambda g, j: (0, g)), stat, stat],
        out_specs=[pl.BlockSpec((hg, nq, MLA_HD_PAD, ab), lambda g, j: (g, 0, 0, 0)), blk(MLA_HD_PAD), blk(MLA_VD)],
        out_shape=[_sds((MLA_HEADS, nq, MLA_HD_PAD, ab), F32), _sds((MLA_HEADS, t, MLA_HD_PAD), F32),
                   _sds((MLA_HEADS, t, MLA_VD), F32)],
        compiler_params=_cparams(("parallel", "arbitrary")),
    )(qh, kh, vh, dob, lse_t, dl_t)


VEC = pl.BlockSpec((1, D_MODEL), lambda i, j, k: (0, 0))


def _rows(tm, width):
    return pl.BlockSpec((tm, width), lambda i, j, k: (i, 0))


def _residual_epi(next_gain):
    if next_gain is None:
        return [], lambda acc, hv: (acc + hv,)

    def epi(acc, hv, g):
        h_new = acc + hv
        r = lax.rsqrt(jnp.mean(h_new * h_new, axis=-1, keepdims=True) + EPS)
        return h_new, h_new * r * g

    return [(next_gain, VEC)], epi


def _residual_outs(t, row, next_gain):
    outs = [(_sds((t, D_MODEL), F32), row)]
    return outs + ([(_sds((t, D_MODEL), BF16), row)] if next_gain is not None else [])


def _mlp_fwd(l, h, hn, w1g, fetch_w2, next_gain):
    t = h.shape[0]
    tm = _row_tile(t, 512)

    def relu2(acc):
        r = jnp.maximum(acc, 0.0)
        return (r * r,)

    (u,) = _mm_rows(f"mlp_up{l}", tm, hn, w1g, 'nn_cols', [(_sds((t, D_FF), BF16), _rows(tm, D_FF))], epi=relu2)
    w2g = fetch_w2((u,))
    row = _rows(tm, D_MODEL)
    more, epi = _residual_epi(next_gain)
    h2, hn_next = _mm_rows(f"mlp_down{l}", tm, u, w2g, 'nn_rows', _residual_outs(t, row, next_gain),
                           extras=[(h, row)] + more, epi=epi)
    return h2, hn_next, (h, hn, u, w1g, w2g)


def _norm_bwd_outs(t, tm):
    return [(_sds((t, D_MODEL), F32), pl.BlockSpec((tm, D_MODEL), lambda i, j, k: (i, 0))),
            (_sds((t // tm, 1, D_MODEL), F32), pl.BlockSpec((None, 1, D_MODEL), lambda i, j, k: (i, 0, 0)))]


def _norm_bwd_epi(acc, xv, res, g):
    dx, dgr = _rms_bwd_rows(acc, xv, g, D_MODEL)
    return res + dx, jnp.sum(dgr, axis=0, keepdims=True)


def _mlp_bwd(l, dh, saved, norm_g):
    h, hn, u, w1g, w2g = saved
    t = h.shape[0]
    tm = _row_tile(t, 512)
    nsh, _, wsh = w1g.shape
    wide = _rows(tm, D_FF)
    (da,) = _mm_rows(f"mlp_du{l}", tm, dh, w2g, 'nt_rows', [(_sds((t, D_FF), BF16), wide)], extras=[(u, wide)],
                     epi=lambda acc, uv: (2.0 * jnp.sqrt(uv.astype(F32)) * acc,))
    tw = _row_tile(t, 512)
    (dw2,) = _mm(f"mlp_dw2{l}", (1, 1, t // tw),
                 u, pl.BlockSpec((tw, D_FF), lambda i, j, k: (k, 0)),
                 dh, pl.BlockSpec((tw, D_MODEL), lambda i, j, k: (k, 0)), (0, 0),
                 [(_sds((D_FF, D_MODEL), BF16), pl.BlockSpec((D_FF, D_MODEL), lambda i, j, k: (0, 0)))])
    dw2 = dw2.reshape(nsh, wsh, D_MODEL)
    (dw1,) = _mm(f"mlp_dw1{l}", (1, 1, t // tw),
                 hn, pl.BlockSpec((tw, D_MODEL), lambda i, j, k: (k, 0)),
                 da, pl.BlockSpec((tw, D_FF), lambda i, j, k: (k, 0)), (0, 0),
                 [(_sds((nsh, D_MODEL, wsh), BF16), pl.BlockSpec((nsh, D_MODEL, wsh), lambda i, j, k: (0, 0, 0)))],
                 split=wsh)
    row = _rows(tm, D_MODEL)
    dh_in, dg = _mm_rows(f"mlp_dhn{l}", tm, da, w1g, 'nt_cols', _norm_bwd_outs(t, tm),
                         extras=[(h, row), (dh, row), (norm_g, VEC)], epi=_norm_bwd_epi)
    return dh_in, jnp.sum(dg, axis=0), dw1, dw2


def _ple_fwd(l, h, hn, p, wg, wp, next_gain, target=None):
    t = h.shape[0]
    tm = _row_tile(t, 512)
    row = pl.BlockSpec((tm, D_MODEL), lambda i, j, k: (i, 0))
    full = lambda r: pl.BlockSpec((r, D_MODEL), lambda i, j, k: (0, 0))
    f32_row, bf_row = (_sds((t, D_MODEL), F32), row), (_sds((t, D_MODEL), BF16), row)
    common = [(h, row), (p, pl.BlockSpec((None, None, tm, PLE_DIM), lambda i, j, k: (l, 0, i, 0))),
              (wp, full(PLE_DIM))]
    if target is not None:
        def loss_epi(acc, hv, pv, wpv, tv):
            gt = _sigmoid(acc)
            ev = _dot(_bf(pv), wpv, 1, 0)
            err = hv + gt * ev - tv
            sq = jnp.sum(jnp.sum(err * err, axis=-1, keepdims=True), axis=0, keepdims=True)
            return err / D_MODEL, gt, ev, jnp.broadcast_to(sq, (8, 128))

        dy, gate, e, sq = _mm(f"ple_gate{l}", (t // tm, 1, 1), hn, row, wg, full(D_MODEL), (1, 0),
                              [f32_row, bf_row, bf_row, (_sds((t // tm, 8, 128), F32),
                                                         pl.BlockSpec((None, 8, 128), lambda i, j, k: (i, 0, 0)))],
                              extras=common + [(target, row)], epi=loss_epi)
        return dy, jnp.sum(sq, axis=0), (h, hn, gate, e)

    def gate_epi(acc, hv, pv, wpv, *gain):
        gt = _sigmoid(acc)
        ev = _dot(_bf(pv), wpv, 1, 0)
        h_new = hv + gt * ev
        if not gain:
            return h_new, gt, ev
        r = lax.rsqrt(jnp.mean(h_new * h_new, axis=-1, keepdims=True) + EPS)
        return h_new, gt, ev, h_new * r * gain[0]

    res = _mm(f"ple_gate{l}", (t // tm, 1, 1), hn, row, wg, full(D_MODEL), (1, 0),
              [f32_row, bf_row, bf_row] + ([bf_row] if next_gain is not None else []),
              extras=common + ([(next_gain, VEC)] if next_gain is not None else []), epi=gate_epi)
    h_out, gate, e = res[0], res[1], res[2]
    return h_out, (res[3] if next_gain is not None else None), (h, hn, gate, e)


def _ple_bwd(l, dh, saved, p, norm_g, wg, deps=()):
    h, hn, gate, e = saved
    t = h.shape[0]
    tm = _row_tile(t)
    tk = _row_tile(t, 512)
    de, dz = _ple_gate_bwd(f"ple_gate_bwd{l}", dh, gate, e)
    full = lambda r: pl.BlockSpec((r, D_MODEL), lambda i, j, k: (0, 0))
    rowk = pl.BlockSpec((tk, D_MODEL), lambda i, j, k: (k, 0))
    (dwp,) = _mm(f"ple_dwp{l}", (1, 1, t // tk),
                 p, pl.BlockSpec((None, None, tk, PLE_DIM), lambda i, j, k: (l, 0, k, 0)),
                 de, rowk, (0, 0), [(_sds((PLE_DIM, D_MODEL), BF16), full(PLE_DIM))], deps=deps)
    (dwg,) = _mm(f"ple_dwg{l}", (1, 1, t // tk), hn, rowk, dz, rowk, (0, 0),
                 [(_sds((D_MODEL, D_MODEL), BF16), full(D_MODEL))])
    row = pl.BlockSpec((tm, D_MODEL), lambda i, j, k: (i, 0))
    dh_in, dg = _mm(f"ple_dhn{l}", (t // tm, 1, 1), dz, row, wg, full(D_MODEL), (1, 1),
                    _norm_bwd_outs(t, tm), extras=[(h, row), (dh, row), (norm_g, VEC)], epi=_norm_bwd_epi)
    return dh_in, jnp.sum(dg, axis=0), dwg, dwp


def _ret_layer_fwd(x, norm_g, wri, fetch_wro, gn, cos, sin, next_gain, deps=()):
    t = x.shape[0]
    tm = _row_tile(t)
    nsh, _, wsh = wri.shape
    hn = _rms_fwd("mix_norm0", x, norm_g)
    tp = _row_tile(t, 512)
    (proj,) = _mm_rows("ret_in", tp, hn, wri, 'nn_cols', [(_sds((t, RET_IN), BF16), _rows(tp, RET_IN))], deps=deps)
    gated, outp, states = _ret_fwd(proj, cos, sin, gn)
    wro = fetch_wro((gated,))
    row = pl.BlockSpec((tm, D_MODEL), lambda i, j, k: (i, 0))
    kt = 512
    more, epi = _residual_epi(next_gain)
    h1, hn_next = _mm("ret_out", (t // tm, 1, RET_V_W // kt),
                      gated, pl.BlockSpec((tm, kt), lambda i, j, k: (i, k)),
                      wro, pl.BlockSpec((kt, D_MODEL), lambda i, j, k: (k, 0)), (1, 0),
                      _residual_outs(t, row, next_gain), extras=[(x, row)] + more, epi=epi)
    return h1, hn_next, (x, hn, proj, gated, outp, states, wro)


def _ret_layer_bwd(dh, saved, norm_g, wri, gn, cos, sin, emit_out, emit_in, deps=()):
    x, hn, proj, gated, outp, states, wro = saved
    t = x.shape[0]
    tm = _row_tile(t)
    tk = _row_tile(t, 512)
    nsh, _, wsh = wri.shape
    (dgated,) = _mm("ret_dgated", (t // tm, RET_V_W // D_MODEL, 1),
                    dh, pl.BlockSpec((tm, D_MODEL), lambda i, j, k: (i, 0)),
                    wro, pl.BlockSpec((D_MODEL, D_MODEL), lambda i, j, k: (j, 0)), (1, 1),
                    [(_sds((t, RET_V_W), F32), pl.BlockSpec((tm, D_MODEL), lambda i, j, k: (i, j)))], deps=deps)
    (dwro,) = _mm("ret_dwro", (1, 1, t // tk),
                  gated, pl.BlockSpec((tk, RET_V_W), lambda i, j, k: (k, 0)),
                  dh, pl.BlockSpec((tk, D_MODEL), lambda i, j, k: (k, 0)), (0, 0),
                  [(_sds((RET_V_W, D_MODEL), BF16), pl.BlockSpec((RET_V_W, D_MODEL), lambda i, j, k: (0, 0)))])
    dproj, dgn = _ret_bwd(proj, cos, sin, gn, outp, states, dgated, deps=emit_out(dwro))
    half = nsh // 2
    (dwri,) = _mm("ret_dwri", (2, 1, t // tk),
                  hn, pl.BlockSpec((tk, D_MODEL), lambda i, j, k: (k, 0)),
                  dproj, pl.BlockSpec((tk, half * wsh), lambda i, j, k: (k, i)), (0, 0),
                  [(_sds((nsh, D_MODEL, wsh), BF16), pl.BlockSpec((half, D_MODEL, wsh), lambda i, j, k: (i, 0, 0)))],
                  split=wsh)
    deps = emit_in(dwri)
    td = _row_tile(t, 256)
    row = _rows(td, D_MODEL)
    dx, dg = _mm_rows("ret_dhn", td, dproj, wri, 'nt_cols', _norm_bwd_outs(t, td),
                      extras=[(x, row), (dh, row), (norm_g, VEC)], epi=_norm_bwd_epi, deps=deps)
    return dx, jnp.sum(dg, axis=0), dgn.reshape(RET_HEADS, RET_DV)


def _mla_layer_fwd(h, hn, wmi, qa, kva, wuq, wukv, gq, gk, wmo, tabs, next_gain):
    t = h.shape[0]
    tm = _row_tile(t)
    row = pl.BlockSpec((tm, D_MODEL), lambda i, j, k: (i, 0))
    (proj2,) = _mm("mla_in", (t // tm, 1, 1), hn, row,
                   wmi, pl.BlockSpec((D_MODEL, MLA_IN_PAD), lambda i, j, k: (0, 0)), (1, 0),
                   [(_sds((t, MLA_IN_PAD), F32), pl.BlockSpec((tm, MLA_IN_PAD), lambda i, j, k: (i, 0)))])
    cq, ckv = _mla_mid(proj2, qa, kva)
    qh, kh, vh = _mla_prep(cq, ckv, wuq, wukv, proj2, gq, gk, tabs)
    o, lse = _attn_fwd(qh, kh, vh)
    more, epi = _residual_epi(next_gain)
    h_out, hn_next = _mm("mla_out", (t // tm, 1, 1), o, row,
                         wmo, pl.BlockSpec((D_MODEL, D_MODEL), lambda i, j, k: (0, 0)), (1, 0),
                         _residual_outs(t, row, next_gain), extras=[(h, row)] + more, epi=epi)
    return h_out, hn_next, (h, hn, proj2, cq, ckv, qh, kh, vh, o, lse)


def _mla_layer_bwd(dh, saved, norm_g, wmi, qa, kva, wuq, wukv, gq, gk, wmo, tabs, deps=()):
    h, hn, proj2, cq, ckv, qh, kh, vh, o, lse = saved
    t = h.shape[0]
    tm = _row_tile(t)
    tk = _row_tile(t, 512)
    row = pl.BlockSpec((tm, D_MODEL), lambda i, j, k: (i, 0))
    rowk = pl.BlockSpec((tk, D_MODEL), lambda i, j, k: (k, 0))
    sq = pl.BlockSpec((D_MODEL, D_MODEL), lambda i, j, k: (0, 0))
    do, dob = _mm("mla_do", (t // tm, 1, 1), dh, row, wmo, sq, (1, 1),
                  [(_sds((t, D_MODEL), F32), row), (_sds((t, D_MODEL), BF16), row)], epi=lambda acc: (acc, acc),
                  deps=deps)
    (dwmo,) = _mm("mla_dwo", (1, 1, t // tk), o, rowk, dh, rowk, (0, 0), [(_sds((D_MODEL, D_MODEL), BF16), sq)])
    delta = _attn_delta(do, o, lse.shape[-1])
    dqt, dkh, dvh = _attn_bwd(qh, kh, vh, dob, lse, delta)
    dq, dkv, dkr, dgq, dgk = _mla_prep_bwd(cq, ckv, wuq, wukv, proj2, gq, gk, tabs, dqt, dkh, dvh)

    wide = MLA_HEADS * MLA_HD_PAD
    widek = pl.BlockSpec((tk, wide), lambda i, j, k: (k, 0))
    (dwuq,) = _mm("mla_dwuq", (1, 1, t // tk),
                  cq, pl.BlockSpec((tk, MLA_Q_RANK), lambda i, j, k: (k, 0)), dq, widek, (0, 0),
                  [(_sds((MLA_HEADS, MLA_Q_RANK, MLA_HD_PAD), BF16),
                    pl.BlockSpec((MLA_HEADS, MLA_Q_RANK, MLA_HD_PAD), lambda i, j, k: (0, 0, 0)))], split=MLA_HD_PAD)
    (dwukv,) = _mm("mla_dwukv", (1, 1, t // tk),
                   ckv, pl.BlockSpec((tk, MLA_KV_RANK), lambda i, j, k: (k, 0)), dkv, widek, (0, 0),
                   [(_sds((MLA_HEADS, MLA_KV_RANK, MLA_HD_PAD), BF16),
                     pl.BlockSpec((MLA_HEADS, MLA_KV_RANK, MLA_HD_PAD), lambda i, j, k: (0, 0, 0)))],
                   split=MLA_HD_PAD)
    side_by_side = lambda wg: wg.transpose(1, 0, 2).reshape(wg.shape[1], wide)
    widei = pl.BlockSpec((tm, wide), lambda i, j, k: (i, 0))
    (dcq,) = _mm("mla_dcq", (t // tm, 1, 1), dq, widei,
                 side_by_side(wuq), pl.BlockSpec((MLA_Q_RANK, wide), lambda i, j, k: (0, 0)), (1, 1),
                 [(_sds((t, MLA_Q_RANK), F32), pl.BlockSpec((tm, MLA_Q_RANK), lambda i, j, k: (i, 0)))])
    (dckv,) = _mm("mla_dckv", (t // tm, 1, 1), dkv, widei,
                  side_by_side(wukv), pl.BlockSpec((MLA_KV_RANK, wide), lambda i, j, k: (0, 0)), (1, 1),
                  [(_sds((t, MLA_KV_RANK), F32), pl.BlockSpec((tm, MLA_KV_RANK), lambda i, j, k: (i, 0)))])
    dproj2, dqa, dkva = _mla_mid_bwd(proj2, qa, kva, dcq, dckv, dkr)
    win = pl.BlockSpec((D_MODEL, MLA_IN_PAD), lambda i, j, k: (0, 0))
    (dwmi,) = _mm("mla_dwin", (1, 1, t // tk), hn, rowk,
                  dproj2, pl.BlockSpec((tk, MLA_IN_PAD), lambda i, j, k: (k, 0)), (0, 0),
                  [(_sds((D_MODEL, MLA_IN_PAD), BF16), win)])
    dh_in, dg = _mm("mla_dhn", (t // tm, 1, 1),
                    dproj2, pl.BlockSpec((tm, MLA_IN_PAD), lambda i, j, k: (i, 0)), wmi, win, (1, 1),
                    _norm_bwd_outs(t, tm), extras=[(h, row), (dh, row), (norm_g, VEC)], epi=_norm_bwd_epi)
    return dh_in, dict(mix=jnp.sum(dg, axis=0), wmi=dwmi, qa=dqa, kva=dkva, wuq=dwuq, wukv=dwukv, gq=dgq, gk=dgk,
                       wmo=dwmo)


def _local_step(x, p, target, w, fetch, emit=lambda group: ()):
    t = x.shape[0]
    inv = 1.0 / (ROPE_THETA ** (jnp.arange(0, RET_DK, 2, dtype=F32) / RET_DK))
    ang = jnp.arange(t, dtype=F32)[:, None] * inv[None, :]
    cos_r, sin_r = jnp.cos(ang), jnp.sin(ang)
    tabs = _mla_tables(t)
    row = lambda a, i: a[i:i + 1]

    h1, hn1, s_ret = _ret_layer_fwd(x, row(w['mix_norm'], 0), w['ret_w_in'],
                                    lambda after: fetch('ret_out', after)['ret_w_out'], w['ret_gn'], cos_r, sin_r,
                                    row(w['mlp_norm'], 0), deps=w['deps'])
    h2, hn2, s_mlp0 = _mlp_fwd(0, h1, hn1, fetch('mlp_w1_0', (h1,))['mlp_w1'],
                               lambda after: fetch('mlp_w2_0', after)['mlp_w2'], row(w['ple_norm'], 0))
    w0 = fetch('ple_0', (h2,))
    h3, hn3, s_ple0 = _ple_fwd(0, h2, hn2, p, w0['ple_gate_w'], w0['ple_proj_w'], row(w['mix_norm'], 1))
    wm = fetch('mla', (h3,))
    mla_w = (wm['mla_w_in'], w['mla_q_a_norm'], w['mla_kv_a_norm'], wm['mla_w_uq'], wm['mla_w_ukv'],
             w['mla_q_norm'], w['mla_k_norm'], wm['mla_w_out'], tabs)
    h4, hn4, s_mla = _mla_layer_fwd(h3, hn3, *mla_w, row(w['mlp_norm'], 1))
    w1 = fetch('layer_1', (h4,))
    h5, hn5, s_mlp1 = _mlp_fwd(1, h4, hn4, w1['mlp_w1'], lambda after: w1['mlp_w2'], row(w['ple_norm'], 1))
    dy, sq_err, s_ple1 = _ple_fwd(1, h5, hn5, p, w1['ple_gate_w'], w1['ple_proj_w'], None, target)

    n = N_DEV
    colsh = lambda a: a.reshape(a.shape[0], n, a.shape[1] // n).transpose(1, 0, 2)
    rowsh = lambda a: a.reshape(n, a.shape[0] // n, a.shape[1])
    big = {}

    def emit_group(group):
        big.update(group)
        return emit(group)

    dh5, dg_ple1, dwg1, dwp1 = _ple_bwd(1, dy, s_ple1, p, row(w['ple_norm'], 1), w1['ple_gate_w'])
    dh4, dg_mlp1, dw1_1, dw2_1 = _mlp_bwd(1, dh5, s_mlp1, row(w['mlp_norm'], 1))
    deps = emit_group({('ple_gate_w', 1): rowsh(dwg1), ('ple_proj_w', 1): colsh(dwp1),
                       ('mlp_w2', 1): dw2_1, ('mlp_w1', 1): dw1_1})
    dh3, gm = _mla_layer_bwd(dh4, s_mla, row(w['mix_norm'], 1), *mla_w, deps=deps)
    deps = emit_group({('mla_w_out', 0): rowsh(gm['wmo']), ('mla_w_uq', 0): gm['wuq'][:, :, :MLA_QKD],
                       ('mla_w_ukv', 0): gm['wukv'], ('mla_w_in', 0): rowsh(gm['wmi'][:, :MLA_IN])})
    dh2, dg_ple0, dwg0, dwp0 = _ple_bwd(0, dh3, s_ple0, p, row(w['ple_norm'], 0), w0['ple_gate_w'], deps=deps)
    dh1, dg_mlp0, dw1_0, dw2_0 = _mlp_bwd(0, dh2, s_mlp0, row(w['mlp_norm'], 0))
    deps = emit_group({('ple_gate_w', 0): rowsh(dwg0), ('ple_proj_w', 0): colsh(dwp0),
                       ('mlp_w2', 0): dw2_0, ('mlp_w1', 0): dw1_0})
    dx, dg_mix0, dgn = _ret_layer_bwd(
        dh1, s_ret, row(w['mix_norm'], 0), w['ret_w_in'], w['ret_gn'], cos_r, sin_r,
        lambda dwro: emit_group({('ret_w_out', 0): rowsh(dwro)}),
        lambda dwri: emit_group({('ret_w_in', 0): dwri}), deps=deps)

    small = dict(
        mix_norm=[dg_mix0, gm['mix']], mlp_norm=[dg_mlp0, dg_mlp1], ple_norm=[dg_ple0, dg_ple1],
        ret_gn=dgn, mla_q_a_norm=gm['qa'], mla_kv_a_norm=gm['kva'], mla_q_norm=gm['gq'], mla_k_norm=gm['gk'],
    )
    return sq_err, dx, big, small


def _my_place():
    x, y, c = lax.axis_index("x"), lax.axis_index("y"), lax.axis_index("c")
    return x, y, c


def _flat(px, py, pc):
    return 4 * px + 2 * py + pc


def _peer(x, y, c, r):
    return (1 - x if r & 4 else x, 1 - y if r & 2 else y, 1 - c if r & 1 else c)


def _all_gather(arrays):
    n = len(arrays)

    def body(*refs):
        ins, outs = refs[:n], refs[n:2 * n]
        send_sems, recv_sems, local_sems = refs[2 * n:]
        x, y, c = _my_place()
        me, sibling = (x, y, c), (x, y, 1 - c)
        chips = [(1 - x, y), (x, 1 - y), (1 - x, 1 - y)]

        def copy(a, k, block, to, src=None):
            slot = outs[a].at[_flat(*block)]
            return pltpu.make_async_remote_copy(
                src_ref=slot if src is None else src, dst_ref=slot,
                send_sem=send_sems.at[a, k], recv_sem=recv_sems.at[a, k], device_id=to, device_id_type=MESH)

        mine = [pltpu.make_async_copy(ins[a], outs[a].at[_flat(*me)], local_sems.at[a]) for a in range(n)]
        for cp in mine:
            cp.start()
        first = []
        for a in range(n):
            first.append(copy(a, 0, me, sibling, src=ins[a]))
            first += [copy(a, 1 + j, me, (*chip, c), src=ins[a]) for j, chip in enumerate(chips)]
        for cp in first:
            cp.start()
        passed = []
        for a in range(n):
            for j, chip in enumerate(chips):
                copy(a, 1 + j, (*chip, c), me).wait_recv()
                passed.append(copy(a, 4 + j, (*chip, c), sibling))
                passed[-1].start()
        for a in range(n):
            copy(a, 0, sibling, me).wait_recv()
            for j, chip in enumerate(chips):
                copy(a, 4 + j, (*chip, 1 - c), me).wait_recv()
        for cp in first + passed:
            cp.wait_send()
        for cp in mine:
            cp.wait()

    return pl.pallas_call(
        body, name="all_gather_weights",
        in_specs=[ANY] * n, out_specs=[ANY] * n,
        out_shape=[_sds((N_DEV,) + a.shape, a.dtype) for a in arrays],
        scratch_shapes=[pltpu.SemaphoreType.DMA((n, 7)), pltpu.SemaphoreType.DMA((n, 7)),
                        pltpu.SemaphoreType.DMA((n,))],
    )(*arrays)


HBM = pl.BlockSpec(memory_space=pltpu.HBM)
SEMS = pl.BlockSpec(memory_space=pltpu.SEMAPHORE)
SIDE_EFFECT = pltpu.SideEffectType.DATAFLOW_SIDE_EFFECTING


def _rs_copies(x, y, c, srcs, lands, send_sems, recv_sems):
    copies = []
    for a in range(len(srcs)):
        for r in range(1, N_DEV):
            peer = _peer(x, y, c, r)
            k = a * (N_DEV - 1) + r - 1
            copies.append(pltpu.make_async_remote_copy(
                src_ref=srcs[a].at[_flat(*peer)], dst_ref=lands[a].at[r - 1],
                send_sem=send_sems.at[k], recv_sem=recv_sems.at[k], device_id=peer, device_id_type=MESH))
    return copies


def _rs_start(name, arrays):
    n = len(arrays)
    hbm = lambda a: pltpu.with_memory_space_constraint(a, pltpu.HBM)
    lands = [hbm(lax.empty((N_DEV - 1,) + a.shape[1:], a.dtype)) for a in arrays]

    def body(*refs):
        srcs, lnd = refs[:n], refs[n:2 * n]
        send_sems, recv_sems = refs[2 * n], refs[2 * n + 1]
        token = refs[-1]
        for cp in _rs_copies(*_my_place(), srcs, lnd, send_sems, recv_sems):
            cp.start()
        token[...] = jnp.zeros_like(token)

    outs = pl.pallas_call(
        body, name=name,
        in_specs=[HBM] * (2 * n),
        out_specs=[SEMS, SEMS] + [HBM] * (2 * n) + [pl.BlockSpec(memory_space=pltpu.VMEM)],
        out_shape=[pltpu.SemaphoreType.DMA((n * (N_DEV - 1),)), pltpu.SemaphoreType.DMA((n * (N_DEV - 1),))]
        + [pltpu.HBM(a.shape, a.dtype) for a in arrays] + [pltpu.HBM(l.shape, l.dtype) for l in lands]
        + [_sds((8, 128), F32)],
        input_output_aliases={i: 2 + i for i in range(2 * n)},
        compiler_params=pltpu.CompilerParams(has_side_effects=SIDE_EFFECT),
    )(*[hbm(a) for a in arrays], *lands)
    return outs[0], outs[1], outs[2:2 + n], outs[2 + n:2 + 2 * n], outs[-1]


def _rs_wait(name, send_sems, recv_sems, srcs, lands, after):
    n = len(srcs)

    def body(*refs):
        src_refs, lnd = refs[:n], refs[n:2 * n]
        send, recv = refs[2 * n], refs[2 * n + 1]
        for cp in _rs_copies(*_my_place(), src_refs, lnd, send, recv):
            cp.wait_send()
            cp.wait_recv()

    outs = pl.pallas_call(
        body, name=name,
        in_specs=[HBM] * (2 * n) + [SEMS, SEMS] + [ANY] * len(after),
        out_specs=[HBM] * (2 * n),
        out_shape=[pltpu.HBM(a.shape, a.dtype) for a in list(srcs) + list(lands)],
        input_output_aliases={i: i for i in range(2 * n)},
        compiler_params=pltpu.CompilerParams(has_side_effects=SIDE_EFFECT),
    )(*srcs, *lands, send_sems, recv_sems, *after)
    return outs[:n], outs[n:]


SMALL_PACK_ROWS = 16


def _all_reduce_small(rows, deps=()):
    n = len(rows)

    def body(*refs):
        ins = refs[:n]
        out_ref, mine, buf, send_sems, recv_sems = refs[n + len(deps):]
        x, y, c = _my_place()
        mine[...] = jnp.zeros_like(mine)
        for (r0, a), ref in zip(rows, ins):
            mine[r0:r0 + a.shape[0], 0:a.shape[1]] = ref[...]
        buf[_flat(x, y, c)] = mine[...]
        copies = []
        for r in range(1, N_DEV):
            peer = _peer(x, y, c, r)
            send = pltpu.make_async_remote_copy(
                src_ref=mine, dst_ref=buf.at[_flat(x, y, c)],
                send_sem=send_sems.at[r - 1], recv_sem=recv_sems.at[r - 1], device_id=peer, device_id_type=MESH)
            send.start()
            recv = pltpu.make_async_remote_copy(
                src_ref=mine, dst_ref=buf.at[_flat(*peer)],
                send_sem=send_sems.at[r - 1], recv_sem=recv_sems.at[r - 1], device_id=peer, device_id_type=MESH)
            copies.append((send, recv))
        for send, recv in copies:
            send.wait_send()
            recv.wait_recv()
        acc = buf[0]
        for s in range(1, N_DEV):
            acc = acc + buf[s]
        out_ref[...] = acc

    vm = pl.BlockSpec(memory_space=pltpu.VMEM)
    shape = (SMALL_PACK_ROWS, D_MODEL)
    return pl.pallas_call(
        body, name="all_reduce_small", in_specs=[vm] * n + [ANY] * len(deps), out_specs=vm,
        out_shape=_sds(shape, F32),
        scratch_shapes=[pltpu.VMEM(shape, F32), pltpu.VMEM((N_DEV,) + shape, F32),
                        pltpu.SemaphoreType.DMA((7,)), pltpu.SemaphoreType.DMA((7,))],
    )(*[a for _, a in rows], *deps)


def _adamw_math(w, g, m, v):
    m = ADAM_B1 * m + (1.0 - ADAM_B1) * g
    v = ADAM_B2 * v + (1.0 - ADAM_B2) * (g * g)
    m_hat = m / (1.0 - ADAM_B1 ** ADAM_STEP)
    v_hat = v / (1.0 - ADAM_B2 ** ADAM_STEP)
    delta = -ADAM_LR * (m_hat / (jnp.sqrt(v_hat) + ADAM_EPS) + ADAM_WD * w)
    return delta, m, v


def _adamw_big(name, w, m, v, srcs, lands, me):
    nl, rows, cols = w.shape
    tr = next(cand for cand in (256, 128, 64, 32, 16, 8) if rows % cand == 0)

    def body(me_ref, w_ref, m_ref, v_ref, *rest):
        src_refs, land_refs = rest[:nl], rest[nl:2 * nl]
        g_ref, d_ref, mo_ref, vo_ref = rest[2 * nl:]
        for layer in range(nl):
            @pl.when(pl.program_id(0) == layer)
            def _():
                g = src_refs[layer][...].astype(F32)
                for s in range(N_DEV - 1):
                    g = g + land_refs[layer][s].astype(F32)
                delta, mn, vn = _adamw_math(w_ref[...], g, m_ref[...], v_ref[...])
                g_ref[...] = g
                d_ref[...] = delta
                mo_ref[...] = mn
                vo_ref[...] = vn

    blk = pl.BlockSpec((None, tr, cols), lambda l, i, me_ref: (l, i, 0))
    at = lambda layer, l, i: jnp.where(l == layer, i, 0)
    own = [pl.BlockSpec((None, tr, cols), functools.partial(lambda layer, l, i, me_ref: (me_ref[0], at(layer, l, i), 0),
                                                            layer)) for layer in range(nl)]
    peers = [pl.BlockSpec((N_DEV - 1, tr, cols), functools.partial(lambda layer, l, i, me_ref: (0, at(layer, l, i), 0),
                                                                   layer)) for layer in range(nl)]
    return pl.pallas_call(
        body, name=name,
        grid_spec=pltpu.PrefetchScalarGridSpec(
            num_scalar_prefetch=1, grid=(nl, rows // tr),
            in_specs=[blk, blk, blk] + own + peers, out_specs=[blk] * 4),
        out_shape=[_sds((nl, rows, cols), F32)] * 4,
        compiler_params=_cparams(("arbitrary", "arbitrary")),
    )(me, w, m, v, *srcs, *lands)


def _adamw_small(ws, gs, ms, vs):
    n = len(ws)

    def body(*refs):
        w_refs, g_refs, m_refs, v_refs = (refs[i * n:(i + 1) * n] for i in range(4))
        d_out, m_out, v_out = (refs[(4 + i) * n:(5 + i) * n] for i in range(3))
        for i in range(n):
            delta, mn, vn = _adamw_math(w_refs[i][...], g_refs[i][...], m_refs[i][...], v_refs[i][...])
            d_out[i][...] = delta
            m_out[i][...] = mn
            v_out[i][...] = vn

    vm = pl.BlockSpec(memory_space=pltpu.VMEM)
    outs = pl.pallas_call(
        body, name="adamw_small", in_specs=[vm] * (4 * n), out_specs=[vm] * (3 * n),
        out_shape=[_sds(a.shape, F32) for a in ws] * 3,
    )(*ws, *gs, *ms, *vs)
    return outs[:n], outs[n:2 * n], outs[2 * n:]


SMALL_ROWS = 16


def _pad_to(a, rows, cols):
    return jnp.pad(a, ((0, rows - a.shape[0]), (0, cols - a.shape[1])))


def _place_own(blocks):
    me = _flat(*_my_place())
    return [lax.dynamic_update_slice(lax.empty((N_DEV,) + b.shape, b.dtype), b[None], (me,) + (0,) * b.ndim)
            for b in blocks]


def _ag_copies(x, y, c, blocks, bufs, send_sems, recv_sems):
    sends, recvs = [], []
    for a in range(len(blocks)):
        for r in range(1, N_DEV):
            peer = _peer(x, y, c, r)
            k = a * (N_DEV - 1) + r - 1
            make = lambda place: pltpu.make_async_remote_copy(
                src_ref=blocks[a], dst_ref=bufs[a].at[_flat(*place)],
                send_sem=send_sems.at[k], recv_sem=recv_sems.at[k], device_id=peer, device_id_type=MESH)
            sends.append(make((x, y, c)))
            recvs.append(make(peer))
    return sends, recvs


def _ag_start(groups, after):
    flat = [pair for g in groups for pair in g]
    n, ng = len(flat), len(groups)
    hbm = lambda a: pltpu.with_memory_space_constraint(a, pltpu.HBM)

    def body(*refs):
        blocks, bufs = refs[:n], refs[n:2 * n]
        sems = refs[2 * n + len(after):2 * n + len(after) + 2 * ng]
        x, y, c = _my_place()
        at = 0
        for gi, g in enumerate(groups):
            sends, _ = _ag_copies(x, y, c, blocks[at:at + len(g)], bufs[at:at + len(g)], sems[2 * gi], sems[2 * gi + 1])
            for cp in sends:
                cp.start()
            at += len(g)
        refs[-1][...] = jnp.zeros_like(refs[-1])

    sem_shapes = [pltpu.SemaphoreType.DMA((len(g) * (N_DEV - 1),)) for g in groups for _ in range(2)]
    outs = pl.pallas_call(
        body, name="gather_start",
        in_specs=[HBM] * (2 * n) + [ANY] * len(after),
        out_specs=[SEMS] * (2 * ng) + [HBM] * (2 * n) + [pl.BlockSpec(memory_space=pltpu.VMEM)],
        out_shape=sem_shapes + [pltpu.HBM(b.shape, b.dtype) for b, _ in flat]
        + [pltpu.HBM(u.shape, u.dtype) for _, u in flat] + [_sds((8, 128), F32)],
        input_output_aliases={i: 2 * ng + i for i in range(2 * n)},
        compiler_params=pltpu.CompilerParams(has_side_effects=SIDE_EFFECT),
    )(*[hbm(b) for b, _ in flat], *[hbm(u) for _, u in flat], *after)
    blocks_thru, bufs_thru = outs[2 * ng:2 * ng + n], outs[2 * ng + n:2 * ng + 2 * n]
    started, at = [], 0
    for gi, g in enumerate(groups):
        started.append((outs[2 * gi], outs[2 * gi + 1], blocks_thru[at:at + len(g)], bufs_thru[at:at + len(g)]))
        at += len(g)
    return started, outs[-1]


def _ag_wait(name, send_sems, recv_sems, blocks, bufs, after):
    n = len(blocks)

    def body(*refs):
        sends, recvs = _ag_copies(*_my_place(), refs[:n], refs[n:2 * n], refs[2 * n], refs[2 * n + 1])
        for s, r in zip(sends, recvs):
            s.wait_send()
            r.wait_recv()

    outs = pl.pallas_call(
        body, name=name,
        in_specs=[HBM] * (2 * n) + [SEMS, SEMS] + [ANY] * len(after),
        out_specs=[HBM] * (2 * n),
        out_shape=[pltpu.HBM(a.shape, a.dtype) for a in list(blocks) + list(bufs)],
        input_output_aliases={i: i for i in range(2 * n)},
        compiler_params=pltpu.CompilerParams(has_side_effects=SIDE_EFFECT),
    )(*blocks, *bufs, send_sems, recv_sems, *after)
    return outs[n:]


def _prepare_weights(p):
    n = N_DEV
    bf = lambda a: a.astype(BF16)
    gn_pack = jnp.concatenate([
        _pad_to(p['ret_gn'][0], RET_HEADS, 128), _pad_to(p['mla_q_a_norm'], 1, 128),
        _pad_to(p['mla_kv_a_norm'], 1, 128), jnp.zeros((2, 128), F32)], axis=0)
    ple = lambda l: [bf(p['ple_gate_w'][l]), bf(p['ple_proj_w'][l])]
    names = ('ret_out', 'mlp_w1_0', 'mlp_w2_0', 'ple_0', 'mla', 'layer_1')
    later = [[bf(p['ret_w_out'][0])], [bf(p['mlp_w1'][0])], [bf(p['mlp_w2'][0])], ple(0),
             [bf(p['mla_w_in'][0]), bf(p['mla_w_uq'][0]), bf(p['mla_w_ukv'][0]), bf(p['mla_w_out'][0])],
             [bf(p['mlp_w1'][1]), bf(p['mlp_w2'][1])] + ple(1)]
    bufs = _place_own([b for g in later for b in g])
    pack, wri = _all_gather([gn_pack, bf(p['ret_w_in'][0])])
    groups, at = [], 0
    for g in later:
        groups.append(list(zip(g, bufs[at:at + len(g)])))
        at += len(g)
    started, token = _ag_start(groups, (wri,))

    w = {k: p[k] for k in ('mix_norm', 'mlp_norm', 'ple_norm')}
    w['ret_gn'] = pack[:, :RET_HEADS, :RET_DV // n].transpose(1, 0, 2).reshape(RET_HEADS, RET_DV)
    w['mla_q_a_norm'] = pack[:, RET_HEADS, :MLA_Q_RANK // n].reshape(1, MLA_Q_RANK)
    w['mla_kv_a_norm'] = pack[:, RET_HEADS + 1, :MLA_KV_RANK // n].reshape(1, MLA_KV_RANK)
    w['ret_w_in'] = wri
    w['mla_q_norm'] = _pad_to(p['mla_q_norm'], 1, MLA_HD_PAD)
    w['mla_k_norm'] = _pad_to(p['mla_k_norm'], 1, MLA_HD_PAD)
    w['deps'] = (token,)

    def fetch(name, after):
        got = list(_ag_wait("gather_wait_" + name, *started[names.index(name)], after))
        if name == 'ret_out':
            return dict(ret_w_out=got[0].reshape(RET_V_W, D_MODEL))
        if name == 'mla':
            wmi, wuq, wukv, wmo = got
            return dict(mla_w_in=jnp.pad(wmi.reshape(D_MODEL, MLA_IN), ((0, 0), (0, MLA_IN_PAD - MLA_IN))),
                        mla_w_uq=jnp.pad(wuq, ((0, 0), (0, 0), (0, MLA_HD_PAD - MLA_QKD))),
                        mla_w_ukv=wukv, mla_w_out=wmo.reshape(D_MODEL, D_MODEL))
        out = {}
        if name in ('mlp_w1_0', 'layer_1'):
            out['mlp_w1'] = got.pop(0)
        if name in ('mlp_w2_0', 'layer_1'):
            out['mlp_w2'] = got.pop(0)
        if name in ('ple_0', 'layer_1'):
            out['ple_gate_w'] = got[0].reshape(D_MODEL, D_MODEL)
            out['ple_proj_w'] = got[1].transpose(1, 0, 2).reshape(PLE_DIM, D_MODEL)
        return out

    return w, fetch


def _small_grads(small, after):
    rows = [(0, small['mix_norm'][0]), (1, small['mix_norm'][1]), (2, small['mlp_norm'][0]),
            (3, small['mlp_norm'][1]), (4, small['ple_norm'][0]), (5, small['ple_norm'][1]),
            (6, small['ret_gn']), (10, small['mla_q_a_norm']), (11, small['mla_kv_a_norm']),
            (12, small['mla_q_norm']), (13, small['mla_k_norm'])]
    gs = _all_reduce_small(rows, after)
    me = _flat(*_my_place())
    n = N_DEV
    return dict(
        mix_norm=gs[0:2], mlp_norm=gs[2:4], ple_norm=gs[4:6],
        ret_gn=lax.dynamic_slice(gs, (6, me * (RET_DV // n)), (RET_HEADS, RET_DV // n)),
        mla_q_a_norm=lax.dynamic_slice(gs, (10, me * (MLA_Q_RANK // n)), (1, MLA_Q_RANK // n)),
        mla_kv_a_norm=lax.dynamic_slice(gs, (11, me * (MLA_KV_RANK // n)), (1, MLA_KV_RANK // n)),
        mla_q_norm=gs[12:13, :MLA_QKD], mla_k_norm=gs[13:14, :MLA_QKD])


def kernel(x, p, mix_norm, ret_w_in, ret_gn, ret_w_out, mla_w_in, mla_q_a_norm, mla_kv_a_norm, mla_w_uq, mla_w_ukv, mla_q_norm, mla_k_norm, mla_w_out, mlp_norm, mlp_w1, mlp_w2, ple_norm, ple_gate_w, ple_proj_w, loss_target, m_mix_norm, m_ret_w_in, m_ret_gn, m_ret_w_out, m_mla_w_in, m_mla_q_a_norm, m_mla_kv_a_norm, m_mla_w_uq, m_mla_w_ukv, m_mla_q_norm, m_mla_k_norm, m_mla_w_out, m_mlp_norm, m_mlp_w1, m_mlp_w2, m_ple_norm, m_ple_gate_w, m_ple_proj_w, v_mix_norm, v_ret_w_in, v_ret_gn, v_ret_w_out, v_mla_w_in, v_mla_q_a_norm, v_mla_kv_a_norm, v_mla_w_uq, v_mla_w_ukv, v_mla_q_norm, v_mla_k_norm, v_mla_w_out, v_mlp_norm, v_mlp_w1, v_mlp_w2, v_ple_norm, v_ple_gate_w, v_ple_proj_w):
    given = dict(locals())
    params = {n: given[n] for n in WEIGHTS}
    w, fetch = _prepare_weights(params)

    started = []

    def emit(group):
        keys = list(group)
        send, recv, srcs, lands, token = _rs_start(f"rs_start{len(started)}", [group[k] for k in keys])
        started.append((keys, send, recv, srcs, lands))
        return (token,)

    sq_err, grad_x, _, small = _local_step(x[0], p, loss_target[0], w, fetch, emit)
    loss = lax.psum(0.5 / D_MODEL * sq_err[0, 0], ("x", "y", "c"))

    grads, deltas, new_m, new_v = {}, {}, {}, {}

    def small_updates(after):
        sg = _small_grads(small, after)
        two_d = lambda a: a.reshape(-1, a.shape[-1])
        d_s, m_s, v_s = _adamw_small(
            [two_d(params[n]) for n in SMALL], [sg[n] for n in SMALL],
            [two_d(given["m_" + n]) for n in SMALL], [two_d(given["v_" + n]) for n in SMALL])
        for i, n in enumerate(SMALL):
            shape = params[n].shape
            grads[n], deltas[n], new_m[n], new_v[n] = (a.reshape(shape) for a in (sg[n], d_s[i], m_s[i], v_s[i]))
        return (d_s[0],)

    me = _flat(*_my_place()).astype(jnp.int32).reshape(1)
    after = (grad_x,)
    src_of, land_of = {}, {}
    for gi, (keys, send, recv, srcs, lands) in enumerate(started):
        if gi == len(started) - 1:
            after = small_updates(after)
        srcs, lands = _rs_wait(f"rs_wait{gi}", send, recv, srcs, lands, after)
        for k, s, l in zip(keys, srcs, lands):
            src_of[k], land_of[k] = s, l
        done = [n for n in BIG if n not in grads and all((n, l) in src_of for l in range(params[n].shape[0]))]
        for n in done:
            layers = range(params[n].shape[0])
            grads[n], deltas[n], new_m[n], new_v[n] = _adamw_big(
                "adamw_" + n, params[n], given["m_" + n], given["v_" + n],
                [src_of[(n, l)] for l in layers], [land_of[(n, l)] for l in layers], me)
        if done:
            after = tuple(deltas[n] for n in done)

    return (loss, grad_x[None], *[grads[n] for n in WEIGHTS], *[deltas[n] for n in WEIGHTS],
            *[new_m[n] for n in WEIGHTS], *[new_v[n] for n in WEIGHTS])
```

```python
import functools
import math

import jax
import jax.numpy as jnp
from jax import lax
from jax.experimental import pallas as pl
from jax.experimental.pallas import tpu as pltpu

F32 = jnp.float32
BF16 = jnp.bfloat16
MESH = pl.DeviceIdType.MESH
ANY = pl.BlockSpec(memory_space=pl.ANY)

N_DEV = 8
D_MODEL = 1024
CHUNK = 64
EPS = 1e-6
ROPE_THETA = 10000.0
RET_HEADS = 4
RET_DK = 256
RET_DV = 512
RET_QK_W = RET_HEADS * RET_DK
RET_V_W = RET_HEADS * RET_DV
RET_IN = 2 * RET_QK_W + 2 * RET_V_W
MLA_HEADS = 8
MLA_NOPE = 128
MLA_ROPE = 64
MLA_QKD = MLA_NOPE + MLA_ROPE
MLA_VD = 128
MLA_Q_RANK = 384
MLA_KV_RANK = 256
MLA_IN = MLA_Q_RANK + MLA_KV_RANK + MLA_ROPE
MLA_IN_PAD = 768
MLA_HD_PAD = 256
D_FF = 4096
PLE_DIM = 256
ATT_SCALE = MLA_QKD ** -0.5
LOG2E = 1.4426950408889634
ATT_EXP2 = ATT_SCALE * LOG2E

ADAM_LR = 0.001
ADAM_B1 = 0.9
ADAM_B2 = 0.999
ADAM_EPS = 1e-08
ADAM_WD = 0.01
ADAM_STEP = 10

VMEM_LIMIT = 52 * 1024 * 1024
ROW_TILE = 1024
RET_ROWS = 256
ATT_BLOCK = 256
ATT_QROWS = 1024
ATT_KROWS = 1024
ATT_HEADS = 2

WEIGHTS = ['mix_norm', 'ret_w_in', 'ret_gn', 'ret_w_out', 'mla_w_in', 'mla_q_a_norm', 'mla_kv_a_norm',
           'mla_w_uq', 'mla_w_ukv', 'mla_q_norm', 'mla_k_norm', 'mla_w_out', 'mlp_norm', 'mlp_w1', 'mlp_w2',
           'ple_norm', 'ple_gate_w', 'ple_proj_w']
BIG = ['ret_w_in', 'ret_w_out', 'mla_w_in', 'mla_w_uq', 'mla_w_ukv', 'mla_w_out', 'mlp_w1', 'mlp_w2',
       'ple_gate_w', 'ple_proj_w']
SMALL = [w for w in WEIGHTS if w not in BIG]


def _cparams(sem=None):
    return pltpu.CompilerParams(dimension_semantics=sem, vmem_limit_bytes=VMEM_LIMIT)


def _dot(a, b, ca, cb):
    return lax.dot_general(a, b, (((ca,), (cb,)), ((), ())), preferred_element_type=F32)


def _bf(v):
    return v if v.dtype == BF16 else v.astype(BF16)


def _sigmoid(z):
    return 1.0 / (1.0 + jnp.exp(-z))


def _mm(name, grid, a, a_spec, b, b_spec, contract, outs, extras=(), epi=None, deps=(), split=None):
    nk = grid[2]
    n_ex, n_out, n_dep = len(extras), len(outs), len(deps)
    acc_shape = tuple(d for d in outs[0][1].block_shape if d is not None)
    if split is not None:
        acc_shape = (acc_shape[1], acc_shape[0] * split)

    def body(*refs):
        a_ref, b_ref = refs[:2]
        ex_refs = refs[2:2 + n_ex]
        out_refs = refs[2 + n_ex + n_dep:2 + n_ex + n_dep + n_out]

        def product():
            return _dot(_bf(a_ref[...]), _bf(b_ref[...]), contract[0], contract[1])

        def finish(acc):
            if split is not None:
                for j in range(acc_shape[1] // split):
                    out_refs[0][j] = acc[:, j * split:(j + 1) * split].astype(out_refs[0].dtype)
                return
            acc = acc[...]
            res = epi(acc, *[r[...] for r in ex_refs]) if epi is not None else (acc,)
            for o, r in zip(out_refs, res):
                o[...] = r.astype(o.dtype)

        if nk == 1:
            finish(product())
        else:
            acc_ref = refs[-1]
            k = pl.program_id(2)

            @pl.when(k == 0)
            def _():
                acc_ref[...] = jnp.zeros_like(acc_ref)

            acc_ref[...] += product()

            @pl.when(k == nk - 1)
            def _():
                finish(acc_ref)

    return pl.pallas_call(
        body, name=name, grid=grid,
        in_specs=[a_spec, b_spec] + [s for _, s in extras] + [ANY] * n_dep,
        out_specs=[s for _, s in outs],
        out_shape=[s for s, _ in outs],
        scratch_shapes=[pltpu.VMEM(acc_shape, F32)] if nk > 1 else [],
        compiler_params=_cparams(("parallel", "parallel", "arbitrary")),
    )(a, b, *[x for x, _ in extras], *deps)


def _mm_rows(name, tm, a, w, mode, outs, extras=(), epi=None, deps=()):
    n_sh, rows, cols = w.shape
    n_ex, n_out, n_dep = len(extras), len(outs), len(deps)
    by_cols = mode in ('nn_cols', 'nt_rows')
    width = cols if mode == 'nn_cols' else rows

    def body(*refs):
        a_ref, w_ref = refs[:2]
        ex_refs = refs[2:2 + n_ex]
        out_refs = refs[2 + n_ex + n_dep:2 + n_ex + n_dep + n_out]
        if by_cols:
            av = _bf(a_ref[...])
            for s in range(n_sh):
                cs = slice(s * width, (s + 1) * width)
                acc = _dot(av, w_ref[s], 1, 0 if mode == 'nn_cols' else 1)
                res = epi(acc, *[r[:, cs] for r in ex_refs]) if epi is not None else (acc,)
                for o, r in zip(out_refs, res):
                    o[:, cs] = r.astype(o.dtype)
        else:
            chunk = rows if mode == 'nn_rows' else cols
            acc = None
            for s in range(n_sh):
                part = _dot(_bf(a_ref[:, s * chunk:(s + 1) * chunk]), w_ref[s], 1, 0 if mode == 'nn_rows' else 1)
                acc = part if acc is None else acc + part
            res = epi(acc, *[r[...] for r in ex_refs]) if epi is not None else (acc,)
            for o, r in zip(out_refs, res):
                o[...] = r.astype(o.dtype)

    t, ka = a.shape
    return pl.pallas_call(
        body, name=name, grid=(t // tm, 1, 1),
        in_specs=[pl.BlockSpec((tm, ka), lambda i, j, k: (i, 0)),
                  pl.BlockSpec((n_sh, rows, cols), lambda i, j, k: (0, 0, 0))] + [s for _, s in extras] + [ANY] * n_dep,
        out_specs=[s for _, s in outs],
        out_shape=[s for s, _ in outs],
        compiler_params=_cparams(("parallel", "arbitrary", "arbitrary")),
    )(a, w, *[x for x, _ in extras], *deps)


def _sds(shape, dtype):
    return jax.ShapeDtypeStruct(shape, dtype)


def _row_tile(t, cap=ROW_TILE):
    return min(cap, t)


def _rms_fwd(name, x, g):
    t, d = x.shape
    tm = _row_tile(t)

    def body(x_ref, g_ref, o_ref):
        xv = x_ref[...]
        r = lax.rsqrt(jnp.mean(xv * xv, axis=-1, keepdims=True) + EPS)
        o_ref[...] = (xv * r * g_ref[...]).astype(o_ref.dtype)

    return pl.pallas_call(
        body, name=name, grid=(t // tm,),
        in_specs=[pl.BlockSpec((tm, d), lambda i: (i, 0)), pl.BlockSpec((1, d), lambda i: (0, 0))],
        out_specs=pl.BlockSpec((tm, d), lambda i: (i, 0)),
        out_shape=_sds((t, d), BF16),
        compiler_params=_cparams(("parallel",)),
    )(x, g)


def _rms_bwd_rows(dy, xv, g, n):
    r = lax.rsqrt(jnp.sum(xv * xv, axis=-1, keepdims=True) / n + EPS)
    xh = xv * r
    dxh = dy * g
    dx = r * (dxh - xh * (jnp.sum(dxh * xh, axis=-1, keepdims=True) / n))
    return dx, dy * xh


def _ple_gate_bwd(name, dh, gate, e):
    t, d = dh.shape
    tm = _row_tile(t)

    def body(dh_ref, g_ref, e_ref, de_ref, dz_ref):
        dh_v, gt = dh_ref[...], g_ref[...].astype(F32)
        de_ref[...] = (dh_v * gt).astype(BF16)
        dz_ref[...] = (dh_v * e_ref[...].astype(F32) * (gt * (1.0 - gt))).astype(BF16)

    row = pl.BlockSpec((tm, d), lambda i: (i, 0))
    return pl.pallas_call(
        body, name=name, grid=(t // tm,), in_specs=[row, row, row], out_specs=[row, row],
        out_shape=[_sds((t, d), BF16), _sds((t, d), BF16)],
        compiler_params=_cparams(("parallel",)),
    )(dh, gate, e)


def _rope_half(v, cos, sin):
    half = v.shape[-1] // 2
    v1, v2 = v[:, :half], v[:, half:]
    return jnp.concatenate([v1 * cos - v2 * sin, v2 * cos + v1 * sin], axis=-1)


def _ret_consts():
    lg = jnp.log(1.0 - 2.0 ** (-5.0 - jnp.arange(RET_HEADS, dtype=F32)))
    idx = jnp.arange(CHUNK, dtype=F32)
    intra = jnp.exp(lg[:, None, None] * jnp.abs(idx[:, None] - idx[None, :]))
    qdec = jnp.exp(lg[:, None] * (idx + 1.0))
    kdec = jnp.exp(lg[:, None] * (CHUNK - 1.0 - idx))
    cdec = jnp.exp(lg * CHUNK)
    qdec = jnp.broadcast_to(qdec[:, :, None], (RET_HEADS, CHUNK, RET_DK))
    kdec = jnp.broadcast_to(kdec[:, :, None], (RET_HEADS, CHUNK, RET_DK))
    cdec = jnp.broadcast_to(cdec[:, None, None], (RET_HEADS, 1, RET_DV))
    return intra, qdec, kdec, cdec


def _ret_specs(rb, rev_nb=None):
    blk = (lambda i: i) if rev_nb is None else (lambda i: rev_nb - 1 - i)
    full = lambda shape: pl.BlockSpec(shape, lambda i: (0,) * len(shape))
    return dict(
        proj=pl.BlockSpec((rb, RET_IN), lambda i: (blk(i), 0)),
        tab=pl.BlockSpec((rb, RET_DK // 2), lambda i: (blk(i), 0)),
        vw=pl.BlockSpec((rb, RET_V_W), lambda i: (blk(i), 0)),
        st=pl.BlockSpec((rb // CHUNK, RET_HEADS, RET_DK, RET_DV), lambda i: (blk(i), 0, 0, 0)),
        gn=full((RET_HEADS, 1, RET_DV)),
        intra=full((RET_HEADS, CHUNK, CHUNK)),
        dec=full((RET_HEADS, CHUNK, RET_DK)),
        cdec=full((RET_HEADS, 1, RET_DV)),
    )


def _ret_fwd(proj, cos, sin, gn):
    t = proj.shape[0]
    rb = min(RET_ROWS, t)
    cpb = rb // CHUNK
    intra, qdec, kdec, cdec = _ret_consts()
    sp = _ret_specs(rb)

    def body(proj_ref, cos_ref, sin_ref, gn_ref, intra_ref, qd_ref, kd_ref, cd_ref,
             gated_ref, outp_ref, st_ref, s_ref):
        @pl.when(pl.program_id(0) == 0)
        def _():
            s_ref[...] = jnp.zeros_like(s_ref)

        def chunk(c, carry):
            rows = pl.ds(pl.multiple_of(c * CHUNK, CHUNK), CHUNK)
            cs, sn = cos_ref[rows, :], sin_ref[rows, :]
            for h in range(RET_HEADS):
                q = proj_ref[rows, h * RET_DK:(h + 1) * RET_DK].astype(F32)
                k = proj_ref[rows, RET_QK_W + h * RET_DK:RET_QK_W + (h + 1) * RET_DK].astype(F32)
                v = proj_ref[rows, 2 * RET_QK_W + h * RET_DV:2 * RET_QK_W + (h + 1) * RET_DV]
                g = proj_ref[rows, 2 * RET_QK_W + RET_V_W + h * RET_DV:
                             2 * RET_QK_W + RET_V_W + (h + 1) * RET_DV].astype(F32)
                qr = _rope_half(q, cs, sn)
                kr = _rope_half(k, cs, sn) * (RET_DK ** -0.5)
                qb, kb, vb = qr.astype(BF16), kr.astype(BF16), v
                sc = _dot(qb, kb, 1, 1) * intra_ref[h]
                inner = _dot(sc.astype(BF16), vb, 1, 0)
                s_old = s_ref[h]
                sb = s_old.astype(BF16)
                st_ref[c, h] = sb
                cross = _dot((qr * qd_ref[h]).astype(BF16), sb, 1, 0)
                out = inner + cross
                s_ref[h] = s_old * cd_ref[h] + _dot((kr * kd_ref[h]).astype(BF16), vb, 0, 0)
                r = lax.rsqrt(jnp.mean(out * out, axis=-1, keepdims=True) + EPS)
                y = out * r * gn_ref[h]
                cols = slice(h * RET_DV, (h + 1) * RET_DV)
                gated_ref[rows, cols] = (g * _sigmoid(g) * y).astype(BF16)
                outp_ref[rows, cols] = out
            return carry

        lax.fori_loop(0, cpb, chunk, 0)

    return pl.pallas_call(
        body, name="ret_fwd", grid=(t // rb,),
        in_specs=[sp['proj'], sp['tab'], sp['tab'], sp['gn'], sp['intra'], sp['dec'], sp['dec'], sp['cdec']],
        out_specs=[sp['vw'], sp['vw'], sp['st']],
        out_shape=[_sds((t, RET_V_W), BF16), _sds((t, RET_V_W), F32),
                   _sds((t // CHUNK, RET_HEADS, RET_DK, RET_DV), BF16)],
        scratch_shapes=[pltpu.VMEM((RET_HEADS, RET_DK, RET_DV), F32)],
        compiler_params=_cparams(("arbitrary",)),
    )(proj, cos, sin, gn.reshape(RET_HEADS, 1, RET_DV), intra, qdec, kdec, cdec)


def _ret_bwd(proj, cos, sin, gn, outp, states, dgated, deps=()):
    t = proj.shape[0]
    rb = min(RET_ROWS, t)
    cpb = rb // CHUNK
    nb = t // rb
    intra, qdec, kdec, cdec = _ret_consts()
    sp = _ret_specs(rb, rev_nb=nb)

    def body(proj_ref, cos_ref, sin_ref, gn_ref, intra_ref, qd_ref, kd_ref, cd_ref, outp_ref, st_ref, dgt_ref, *rest):
        dproj_ref, dgn_ref, ds_ref = rest[len(deps):]
        @pl.when(pl.program_id(0) == 0)
        def _():
            ds_ref[...] = jnp.zeros_like(ds_ref)
            dgn_ref[...] = jnp.zeros_like(dgn_ref)

        def chunk(cc, carry):
            c = cpb - 1 - cc
            rows = pl.ds(pl.multiple_of(c * CHUNK, CHUNK), CHUNK)
            cs, sn = cos_ref[rows, :], sin_ref[rows, :]
            for h in range(RET_HEADS):
                q = proj_ref[rows, h * RET_DK:(h + 1) * RET_DK].astype(F32)
                k = proj_ref[rows, RET_QK_W + h * RET_DK:RET_QK_W + (h + 1) * RET_DK].astype(F32)
                v = proj_ref[rows, 2 * RET_QK_W + h * RET_DV:2 * RET_QK_W + (h + 1) * RET_DV]
                g = proj_ref[rows, 2 * RET_QK_W + RET_V_W + h * RET_DV:
                             2 * RET_QK_W + RET_V_W + (h + 1) * RET_DV].astype(F32)
                cols = slice(h * RET_DV, (h + 1) * RET_DV)
                qr = _rope_half(q, cs, sn)
                kr = _rope_half(k, cs, sn) * (RET_DK ** -0.5)
                qb, kb, vb = qr.astype(BF16), kr.astype(BF16), v
                qdb = (qr * qd_ref[h]).astype(BF16)
                kdb = (kr * kd_ref[h]).astype(BF16)
                out = outp_ref[rows, cols]
                dgt = dgt_ref[rows, cols]
                gnh = gn_ref[h]
                r = lax.rsqrt(jnp.mean(out * out, axis=-1, keepdims=True) + EPS)
                xh = out * r
                sg = _sigmoid(g)
                dgate = dgt * (xh * gnh) * (sg * (1.0 + g * (1.0 - sg)))
                dy = dgt * (g * sg)
                dgn_ref[h] += jnp.sum(dy * xh, axis=0, keepdims=True)
                dxh = dy * gnh
                dout = r * (dxh - xh * jnp.mean(dxh * xh, axis=-1, keepdims=True))
                doutb = dout.astype(BF16)
                itr = intra_ref[h]
                pb = (_dot(qb, kb, 1, 1) * itr).astype(BF16)
                dv = _dot(pb, doutb, 0, 0)
                dsc = (_dot(doutb, vb, 1, 1) * itr).astype(BF16)
                dq = _dot(dsc, kb, 1, 0)
                dk = _dot(dsc, qb, 0, 0)
                dq = dq + _dot(doutb, st_ref[c, h], 1, 1) * qd_ref[h]
                ds_new = ds_ref[h]
                dsb = ds_new.astype(BF16)
                dk = dk + _dot(vb, dsb, 1, 1) * kd_ref[h]
                dv = dv + _dot(kdb, dsb, 1, 0)
                ds_ref[h] = ds_new * cd_ref[h] + _dot(qdb, doutb, 0, 0)
                dproj_ref[rows, h * RET_DK:(h + 1) * RET_DK] = _rope_half(dq, cs, -sn).astype(BF16)
                dproj_ref[rows, RET_QK_W + h * RET_DK:RET_QK_W + (h + 1) * RET_DK] = (
                    _rope_half(dk * (RET_DK ** -0.5), cs, -sn).astype(BF16))
                dproj_ref[rows, 2 * RET_QK_W + h * RET_DV:2 * RET_QK_W + (h + 1) * RET_DV] = dv.astype(BF16)
                dproj_ref[rows, 2 * RET_QK_W + RET_V_W + h * RET_DV:
                          2 * RET_QK_W + RET_V_W + (h + 1) * RET_DV] = dgate.astype(BF16)
            return carry

        lax.fori_loop(0, cpb, chunk, 0)

    return pl.pallas_call(
        body, name="ret_bwd", grid=(nb,),
        in_specs=[sp['proj'], sp['tab'], sp['tab'], sp['gn'], sp['intra'], sp['dec'], sp['dec'], sp['cdec'],
                  sp['vw'], sp['st'], sp['vw']] + [ANY] * len(deps),
        out_specs=[sp['proj'], sp['gn']],
        out_shape=[_sds((t, RET_IN), BF16), _sds((RET_HEADS, 1, RET_DV), F32)],
        scratch_shapes=[pltpu.VMEM((RET_HEADS, RET_DK, RET_DV), F32)],
        compiler_params=_cparams(("arbitrary",)),
    )(proj, cos, sin, gn.reshape(RET_HEADS, 1, RET_DV), intra, qdec, kdec, cdec, outp, states, dgated, *deps)


def _mla_tables(t):
    half = MLA_ROPE // 2
    inv = 1.0 / (ROPE_THETA ** (jnp.arange(0, MLA_ROPE, 2, dtype=F32) / MLA_ROPE))
    ang = jnp.arange(t, dtype=F32)[:, None] * inv[None, :]
    cos, sin = jnp.cos(ang), jnp.sin(ang)
    z = jnp.zeros((t, half), F32)
    c = jnp.concatenate([cos, cos, z, z], axis=1)
    s1 = jnp.concatenate([-sin, z, z, z], axis=1)
    s2 = jnp.concatenate([z, sin, z, z], axis=1)
    return c, s1, s2


def _rope_tile(r, c, s1, s2):
    return r * c + pltpu.roll(r, 96, 1) * s1 + pltpu.roll(r, 32, 1) * s2


def _mla_mid(proj2, qa, kva):
    t = proj2.shape[0]
    tm = _row_tile(t)

    def body(p_ref, qa_ref, kva_ref, cq_ref, ckv_ref):
        cq = p_ref[:, :MLA_Q_RANK]
        ckv = p_ref[:, MLA_Q_RANK:MLA_Q_RANK + MLA_KV_RANK]
        rq = lax.rsqrt(jnp.mean(cq * cq, axis=-1, keepdims=True) + EPS)
        rkv = lax.rsqrt(jnp.mean(ckv * ckv, axis=-1, keepdims=True) + EPS)
        cq_ref[...] = (cq * rq * qa_ref[...]).astype(BF16)
        ckv_ref[...] = (ckv * rkv * kva_ref[...]).astype(BF16)

    return pl.pallas_call(
        body, name="mla_mid", grid=(t // tm,),
        in_specs=[pl.BlockSpec((tm, MLA_IN_PAD), lambda i: (i, 0)),
                  pl.BlockSpec((1, MLA_Q_RANK), lambda i: (0, 0)),
                  pl.BlockSpec((1, MLA_KV_RANK), lambda i: (0, 0))],
        out_specs=[pl.BlockSpec((tm, MLA_Q_RANK), lambda i: (i, 0)),
                   pl.BlockSpec((tm, MLA_KV_RANK), lambda i: (i, 0))],
        out_shape=[_sds((t, MLA_Q_RANK), BF16), _sds((t, MLA_KV_RANK), BF16)],
        compiler_params=_cparams(("parallel",)),
    )(proj2, qa, kva)


def _mla_mid_bwd(proj2, qa, kva, dcq, dckv, dkr):
    t = proj2.shape[0]
    tm = _row_tile(t)

    def body(p_ref, qa_ref, kva_ref, dcq_ref, dckv_ref, dkr_ref, dp_ref, dqa_ref, dkva_ref):
        @pl.when(pl.program_id(0) == 0)
        def _():
            dqa_ref[...] = jnp.zeros_like(dqa_ref)
            dkva_ref[...] = jnp.zeros_like(dkva_ref)

        dxq, dgq = _rms_bwd_rows(dcq_ref[...], p_ref[:, :MLA_Q_RANK], qa_ref[...], MLA_Q_RANK)
        dxk, dgk = _rms_bwd_rows(dckv_ref[...], p_ref[:, MLA_Q_RANK:MLA_Q_RANK + MLA_KV_RANK], kva_ref[...],
                                 MLA_KV_RANK)
        dp_ref[:, :MLA_Q_RANK] = dxq.astype(BF16)
        dp_ref[:, MLA_Q_RANK:MLA_Q_RANK + MLA_KV_RANK] = dxk.astype(BF16)
        dp_ref[:, MLA_Q_RANK + MLA_KV_RANK:] = dkr_ref[...].astype(BF16)
        dqa_ref[...] += jnp.sum(dgq, axis=0, keepdims=True)
        dkva_ref[...] += jnp.sum(dgk, axis=0, keepdims=True)

    return pl.pallas_call(
        body, name="mla_mid_bwd", grid=(t // tm,),
        in_specs=[pl.BlockSpec((tm, MLA_IN_PAD), lambda i: (i, 0)),
                  pl.BlockSpec((1, MLA_Q_RANK), lambda i: (0, 0)),
                  pl.BlockSpec((1, MLA_KV_RANK), lambda i: (0, 0)),
                  pl.BlockSpec((tm, MLA_Q_RANK), lambda i: (i, 0)),
                  pl.BlockSpec((tm, MLA_KV_RANK), lambda i: (i, 0)),
                  pl.BlockSpec((tm, 128), lambda i: (i, 0))],
        out_specs=[pl.BlockSpec((tm, MLA_IN_PAD), lambda i: (i, 0)),
                   pl.BlockSpec((1, MLA_Q_RANK), lambda i: (0, 0)),
                   pl.BlockSpec((1, MLA_KV_RANK), lambda i: (0, 0))],
        out_shape=[_sds((t, MLA_IN_PAD), BF16), _sds((1, MLA_Q_RANK), F32), _sds((1, MLA_KV_RANK), F32)],
        compiler_params=_cparams(("arbitrary",)),
    )(proj2, qa, kva, dcq, dckv, dkr)


def _mla_prep_specs(t, tm):
    head = lambda w: pl.BlockSpec((None, tm, w), lambda i, h: (h, i, 0))
    return dict(
        head256=head(MLA_HD_PAD), head128=head(MLA_VD),
        cols256=pl.BlockSpec((tm, MLA_HD_PAD), lambda i, h: (i, h)),
        cq=pl.BlockSpec((tm, MLA_Q_RANK), lambda i, h: (i, 0)),
        ckv=pl.BlockSpec((tm, MLA_KV_RANK), lambda i, h: (i, 0)),
        wuq=pl.BlockSpec((None, MLA_Q_RANK, MLA_HD_PAD), lambda i, h: (h, 0, 0)),
        wukv=pl.BlockSpec((None, MLA_KV_RANK, MLA_HD_PAD), lambda i, h: (h, 0, 0)),
        kr=pl.BlockSpec((tm, 128), lambda i, h: (i, (MLA_Q_RANK + MLA_KV_RANK) // 128)),
        gain=pl.BlockSpec((1, MLA_HD_PAD), lambda i, h: (0, 0)),
        tab=pl.BlockSpec((tm, 128), lambda i, h: (i, 0)),
    )


def _mla_prep(cq, ckv, wuq, wukv, proj2, gq, gk, tabs):
    t = cq.shape[0]
    tm = _row_tile(t)
    sp = _mla_prep_specs(t, tm)

    def body(cq_ref, ckv_ref, wuq_ref, wukv_ref, kr_ref, gq_ref, gk_ref, c_ref, s1_ref, s2_ref,
             qh_ref, kh_ref, vh_ref):
        c, s1, s2 = c_ref[...], s1_ref[...], s2_ref[...]

        def norm_rope(xv, gain):
            r = lax.rsqrt(jnp.sum(xv * xv, axis=-1, keepdims=True) / MLA_QKD + EPS)
            y = xv * r * gain
            return jnp.concatenate([y[:, :MLA_NOPE], _rope_tile(y[:, MLA_NOPE:], c, s1, s2)], axis=-1)

        kvv = _dot(ckv_ref[...], wukv_ref[...], 1, 0)
        qh_ref[...] = norm_rope(_dot(cq_ref[...], wuq_ref[...], 1, 0), gq_ref[...]).astype(BF16)
        kf = jnp.concatenate([kvv[:, :MLA_NOPE], kr_ref[...]], axis=-1)
        kh_ref[...] = norm_rope(kf, gk_ref[...]).astype(BF16)
        vh_ref[...] = jnp.concatenate([kvv[:, MLA_NOPE:], jnp.ones((tm, MLA_VD), F32)], axis=-1).astype(BF16)

    return pl.pallas_call(
        body, name="mla_prep", grid=(t // tm, MLA_HEADS),
        in_specs=[sp['cq'], sp['ckv'], sp['wuq'], sp['wukv'], sp['kr'], sp['gain'], sp['gain'],
                  sp['tab'], sp['tab'], sp['tab']],
        out_specs=[sp['head256'], sp['head256'], sp['head256']],
        out_shape=[_sds((MLA_HEADS, t, MLA_HD_PAD), BF16), _sds((MLA_HEADS, t, MLA_HD_PAD), BF16),
                   _sds((MLA_HEADS, t, 2 * MLA_VD), BF16)],
        compiler_params=_cparams(("parallel", "arbitrary")),
    )(cq, ckv, wuq, wukv, proj2, gq, gk, *tabs)


def _mla_prep_bwd(cq, ckv, wuq, wukv, proj2, gq, gk, tabs, dqt, dkh, dvh):
    t = cq.shape[0]
    tm = _row_tile(t)
    ab = dqt.shape[-1]
    sp = _mla_prep_specs(t, tm)

    def body(cq_ref, ckv_ref, wuq_ref, wukv_ref, kr_ref, gq_ref, gk_ref, c_ref, s1_ref, s2_ref,
             dqt_ref, dkh_ref, dvh_ref, dq_ref, dkv_ref, dkr_ref, dgq_ref, dgk_ref):
        dqh = jnp.concatenate([dqt_ref[b].T for b in range(tm // ab)], axis=0)
        i, h = pl.program_id(0), pl.program_id(1)

        @pl.when((i == 0) & (h == 0))
        def _():
            dgq_ref[...] = jnp.zeros_like(dgq_ref)
            dgk_ref[...] = jnp.zeros_like(dgk_ref)

        @pl.when(h == 0)
        def _():
            dkr_ref[...] = jnp.zeros_like(dkr_ref)

        c, s1, s2 = c_ref[...], s1_ref[...], s2_ref[...]

        def back(xv, gain, dout):
            dy = jnp.concatenate([dout[:, :MLA_NOPE], _rope_tile(dout[:, MLA_NOPE:], c, -s1, -s2)], axis=-1)
            return _rms_bwd_rows(dy, xv, gain, MLA_QKD)

        kvv = _dot(ckv_ref[...], wukv_ref[...], 1, 0)
        dxq, dgq = back(_dot(cq_ref[...], wuq_ref[...], 1, 0), gq_ref[...], dqh)
        kf = jnp.concatenate([kvv[:, :MLA_NOPE], kr_ref[...]], axis=-1)
        dxk, dgk = back(kf, gk_ref[...], dkh_ref[...])
        dq_ref[...] = dxq.astype(BF16)
        dkv_ref[...] = jnp.concatenate([dxk[:, :MLA_NOPE], dvh_ref[...]], axis=-1).astype(BF16)
        dkr_ref[...] += dxk[:, MLA_NOPE:]
        dgq_ref[...] += jnp.sum(dgq, axis=0, keepdims=True)
        dgk_ref[...] += jnp.sum(dgk, axis=0, keepdims=True)

    return pl.pallas_call(
        body, name="mla_prep_bwd", grid=(t // tm, MLA_HEADS),
        in_specs=[sp['cq'], sp['ckv'], sp['wuq'], sp['wukv'], sp['kr'], sp['gain'], sp['gain'],
                  sp['tab'], sp['tab'], sp['tab'],
                  pl.BlockSpec((None, tm // ab, MLA_HD_PAD, ab), lambda i, h: (h, i, 0, 0)),
                  sp['head256'], sp['head128']],
        out_specs=[sp['cols256'], sp['cols256'], sp['tab'], sp['gain'], sp['gain']],
        out_shape=[_sds((t, MLA_HEADS * MLA_HD_PAD), BF16), _sds((t, MLA_HEADS * MLA_HD_PAD), BF16),
                   _sds((t, 128), F32), _sds((1, MLA_HD_PAD), F32), _sds((1, MLA_HD_PAD), F32)],
        compiler_params=_cparams(("arbitrary", "arbitrary")),
    )(cq, ckv, wuq, wukv, proj2, gq, gk, *tabs, dqt, dkh, dvh)


def _chunk_visible(rows, cols, row_off, col_off):
    rq = lax.shift_right_logical(lax.broadcasted_iota(jnp.int32, (rows, cols), 0) + row_off, 6)
    ck = lax.shift_right_logical(lax.broadcasted_iota(jnp.int32, (rows, cols), 1) + col_off, 6)
    return ck <= rq


def _rows_to_lanes(col):
    return col.T[:8, :]


def _attn_fwd(qh, kh, vh):
    t = qh.shape[1]
    ab = min(ATT_BLOCK, t)
    tq = min(ATT_QROWS, t)
    r = tq // ab
    hg = ATT_HEADS

    def body(q_ref, k_ref, v_ref, o_ref, lse_ref, acc_ref):
        n_un = pl.program_id(1) * r
        acc_ref[...] = jnp.zeros_like(acc_ref)

        def step(b, ms, diag):
            rows = pl.ds(pl.multiple_of(b * ab, ab), ab)
            out = []
            for hh in range(hg):
                m = ms[hh]
                s = _dot(q_ref[hh], k_ref[hh, rows, :], 1, 1)
                if diag is not None:
                    s = jnp.where(_chunk_visible(tq, ab, 0, diag * ab), s, -1e30)
                m_new = jnp.maximum(m, jnp.max(s, axis=-1, keepdims=True))
                p = jnp.exp2((s - m_new) * ATT_EXP2).astype(BF16)
                acc_ref[hh] = jnp.exp2((m - m_new) * ATT_EXP2) * acc_ref[hh] + _dot(p, v_ref[hh, rows, :], 1, 0)
                out.append(m_new)
            return tuple(out)

        ms = tuple(jnp.full((tq, 1), -1e30, F32) for _ in range(hg))
        ms = lax.fori_loop(0, n_un, lambda b, st: step(b, st, None), ms)
        for d in range(r):
            ms = step(n_un + d, ms, d)
        for hh in range(hg):
            l = acc_ref[hh, :, MLA_VD:]
            o_ref[:, hh * MLA_VD:(hh + 1) * MLA_VD] = acc_ref[hh, :, :MLA_VD] / l
            lse_t = _rows_to_lanes(ms[hh] * ATT_EXP2 + jnp.log(l) * LOG2E)
            for d in range(r):
                lse_ref[hh, d] = lse_t[:, d * ab:(d + 1) * ab]

    return pl.pallas_call(
        body, name="mla_attn", grid=(MLA_HEADS // hg, t // tq),
        in_specs=[pl.BlockSpec((hg, tq, MLA_HD_PAD), lambda g, i: (g, i, 0)),
                  pl.BlockSpec((hg, t, MLA_HD_PAD), lambda g, i: (g, 0, 0)),
                  pl.BlockSpec((hg, t, 2 * MLA_VD), lambda g, i: (g, 0, 0))],
        out_specs=[pl.BlockSpec((tq, hg * MLA_VD), lambda g, i: (i, g)),
                   pl.BlockSpec((hg, r, 8, ab), lambda g, i: (g, i, 0, 0))],
        out_shape=[_sds((t, MLA_HEADS * MLA_VD), F32), _sds((MLA_HEADS, t // ab, 8, ab), F32)],
        scratch_shapes=[pltpu.VMEM((hg, tq, 2 * MLA_VD), F32)],
        compiler_params=_cparams(("parallel", "arbitrary")),
    )(qh, kh, vh)


def _attn_delta(do, o, ab):
    t = do.shape[0]
    tm = _row_tile(t)

    def body(do_ref, o_ref, d_ref):
        d = jnp.sum(do_ref[...] * o_ref[...], axis=-1, keepdims=True)
        d_t = _rows_to_lanes(jnp.broadcast_to(d, (tm, 128)))
        for b in range(tm // ab):
            d_ref[b] = d_t[:, b * ab:(b + 1) * ab]

    col = pl.BlockSpec((tm, MLA_VD), lambda i, h: (i, h))
    return pl.pallas_call(
        body, name="mla_delta", grid=(t // tm, MLA_HEADS), in_specs=[col, col],
        out_specs=pl.BlockSpec((None, tm // ab, 8, ab), lambda i, h: (h, i, 0, 0)),
        out_shape=_sds((MLA_HEADS, t // ab, 8, ab), F32),
        compiler_params=_cparams(("parallel", "parallel")),
    )(do, o)


def _attn_bwd(qh, kh, vh, dob, lse_t, dl_t):
    t = qh.shape[1]
    ab = min(ATT_BLOCK, t)
    kb = min(ATT_KROWS, t)
    r = kb // ab
    nq = t // ab
    hg = ATT_HEADS

    def body(q_ref, k_ref, v_ref, do_ref, lse_ref, dl_ref, dqt_ref, dk_ref, dv_ref):
        j = pl.program_id(1)

        @pl.when(j == 0)
        def _():
            dqt_ref[...] = jnp.zeros_like(dqt_ref)

        ks = [k_ref[hh] for hh in range(hg)]
        vs = [v_ref[hh, :, :MLA_VD] for hh in range(hg)]
        kts = [k.T for k in ks]

        dk_ref[...] = jnp.zeros_like(dk_ref)
        dv_ref[...] = jnp.zeros_like(dv_ref)

        def step(b, carry, diag):
            rows = pl.ds(pl.multiple_of(b * ab, ab), ab)
            hi = kb if diag is None else (diag + 1) * ab
            for hh in range(hg):
                q = q_ref[hh, rows, :]
                do = do_ref[rows, hh * MLA_VD:(hh + 1) * MLA_VD]
                s_t = _dot(ks[hh][:hi], q, 1, 1)
                if diag is not None:
                    key_chunk = lax.shift_right_logical(lax.broadcasted_iota(jnp.int32, (hi, ab), 0), 6)
                    query_chunk = lax.shift_right_logical(
                        lax.broadcasted_iota(jnp.int32, (hi, ab), 1) + diag * ab, 6)
                    s_t = jnp.where(key_chunk <= query_chunk, s_t, -1e30)
                p_t = jnp.exp2(s_t * ATT_EXP2 - lse_ref[hh, b][0:1, :])
                dp_t = _dot(vs[hh][:hi], do, 1, 1)
                ds_t = (p_t * (dp_t - dl_ref[hh, b][0:1, :]) * ATT_SCALE).astype(BF16)
                dqt_ref[hh, b] += _dot(kts[hh][:, :hi], ds_t, 1, 0)
                dk_ref[hh, :hi] += _dot(ds_t, q, 1, 0)
                dv_ref[hh, :hi] += _dot(p_t.astype(BF16), do, 1, 0)
            return carry

        for d in range(r):
            step(j * r + d, 0, d)
        lax.fori_loop((j + 1) * r, nq, lambda b, c: step(b, c, None), 0)

    whole = lambda w: pl.BlockSpec((hg, t, w), lambda g, j: (g, 0, 0))
    blk = lambda w: pl.BlockSpec((hg, kb, w), lambda g, j: (g, j, 0))
    stat = pl.BlockSpec((hg, nq, 8, ab), lambda g, j: (g, 0, 0, 0))
    return pl.pallas_call(
        body, name="mla_attn_bwd", grid=(MLA_HEADS // hg, t // kb),
        in_specs=[whole(MLA_HD_PAD), blk(MLA_HD_PAD), blk(2 * MLA_VD),
                  pl.BlockSpec((t, hg * MLA_VD), lambda g, j: (0, g)), stat, stat],
        out_specs=[pl.BlockSpec((hg, nq, MLA_HD_PAD, ab), lambda g, j: (g, 0, 0, 0)), blk(MLA_HD_PAD), blk(MLA_VD)],
        out_shape=[_sds((MLA_HEADS, nq, MLA_HD_PAD, ab), F32), _sds((MLA_HEADS, t, MLA_HD_PAD), F32),
                   _sds((MLA_HEADS, t, MLA_VD), F32)],
        compiler_params=_cparams(("parallel", "arbitrary")),
    )(qh, kh, vh, dob, lse_t, dl_t)


VEC = pl.BlockSpec((1, D_MODEL), lambda i, j, k: (0, 0))


def _rows(tm, width):
    return pl.BlockSpec((tm, width), lambda i, j, k: (i, 0))


def _residual_epi(next_gain):
    if next_gain is None:
        return [], lambda acc, hv: (acc + hv,)

    def epi(acc, hv, g):
        h_new = acc + hv
        r = lax.rsqrt(jnp.mean(h_new * h_new, axis=-1, keepdims=True) + EPS)
        return h_new, h_new * r * g

    return [(next_gain, VEC)], epi


def _residual_outs(t, row, next_gain):
    outs = [(_sds((t, D_MODEL), F32), row)]
    return outs + ([(_sds((t, D_MODEL), BF16), row)] if next_gain is not None else [])


def _mlp_fwd(l, h, hn, w1g, fetch_w2, next_gain):
    t = h.shape[0]
    tm = _row_tile(t, 512)

    def relu2(acc):
        r = jnp.maximum(acc, 0.0)
        return (r * r,)

    (u,) = _mm_rows(f"mlp_up{l}", tm, hn, w1g, 'nn_cols', [(_sds((t, D_FF), BF16), _rows(tm, D_FF))], epi=relu2)
    w2g = fetch_w2((u,))
    row = _rows(tm, D_MODEL)
    more, epi = _residual_epi(next_gain)
    h2, hn_next = _mm_rows(f"mlp_down{l}", tm, u, w2g, 'nn_rows', _residual_outs(t, row, next_gain),
                           extras=[(h, row)] + more, epi=epi)
    return h2, hn_next, (h, hn, u, w1g, w2g)


def _norm_bwd_outs(t, tm):
    return [(_sds((t, D_MODEL), F32), pl.BlockSpec((tm, D_MODEL), lambda i, j, k: (i, 0))),
            (_sds((t // tm, 1, D_MODEL), F32), pl.BlockSpec((None, 1, D_MODEL), lambda i, j, k: (i, 0, 0)))]


def _norm_bwd_epi(acc, xv, res, g):
    dx, dgr = _rms_bwd_rows(acc, xv, g, D_MODEL)
    return res + dx, jnp.sum(dgr, axis=0, keepdims=True)


def _mlp_bwd(l, dh, saved, norm_g):
    h, hn, u, w1g, w2g = saved
    t = h.shape[0]
    tm = _row_tile(t, 512)
    nsh, _, wsh = w1g.shape
    wide = _rows(tm, D_FF)
    (da,) = _mm_rows(f"mlp_du{l}", tm, dh, w2g, 'nt_rows', [(_sds((t, D_FF), BF16), wide)], extras=[(u, wide)],
                     epi=lambda acc, uv: (2.0 * jnp.sqrt(uv.astype(F32)) * acc,))
    tw = _row_tile(t, 512)
    (dw2,) = _mm(f"mlp_dw2{l}", (1, 1, t // tw),
                 u, pl.BlockSpec((tw, D_FF), lambda i, j, k: (k, 0)),
                 dh, pl.BlockSpec((tw, D_MODEL), lambda i, j, k: (k, 0)), (0, 0),
                 [(_sds((D_FF, D_MODEL), BF16), pl.BlockSpec((D_FF, D_MODEL), lambda i, j, k: (0, 0)))])
    dw2 = dw2.reshape(nsh, wsh, D_MODEL)
    (dw1,) = _mm(f"mlp_dw1{l}", (1, 1, t // tw),
                 hn, pl.BlockSpec((tw, D_MODEL), lambda i, j, k: (k, 0)),
                 da, pl.BlockSpec((tw, D_FF), lambda i, j, k: (k, 0)), (0, 0),
                 [(_sds((nsh, D_MODEL, wsh), BF16), pl.BlockSpec((nsh, D_MODEL, wsh), lambda i, j, k: (0, 0, 0)))],
                 split=wsh)
    row = _rows(tm, D_MODEL)
    dh_in, dg = _mm_rows(f"mlp_dhn{l}", tm, da, w1g, 'nt_cols', _norm_bwd_outs(t, tm),
                         extras=[(h, row), (dh, row), (norm_g, VEC)], epi=_norm_bwd_epi)
    return dh_in, jnp.sum(dg, axis=0), dw1, dw2


def _ple_fwd(l, h, hn, p, wg, wp, next_gain, target=None):
    t = h.shape[0]
    tm = _row_tile(t, 512)
    row = pl.BlockSpec((tm, D_MODEL), lambda i, j, k: (i, 0))
    full = lambda r: pl.BlockSpec((r, D_MODEL), lambda i, j, k: (0, 0))
    f32_row, bf_row = (_sds((t, D_MODEL), F32), row), (_sds((t, D_MODEL), BF16), row)
    common = [(h, row), (p, pl.BlockSpec((None, None, tm, PLE_DIM), lambda i, j, k: (l, 0, i, 0))),
              (wp, full(PLE_DIM))]
    if target is not None:
        def loss_epi(acc, hv, pv, wpv, tv):
            gt = _sigmoid(acc)
            ev = _dot(_bf(pv), wpv, 1, 0)
            err = hv + gt * ev - tv
            sq = jnp.sum(jnp.sum(err * err, axis=-1, keepdims=True), axis=0, keepdims=True)
            return err / D_MODEL, gt, ev, jnp.broadcast_to(sq, (8, 128))

        dy, gate, e, sq = _mm(f"ple_gate{l}", (t // tm, 1, 1), hn, row, wg, full(D_MODEL), (1, 0),
                              [f32_row, bf_row, bf_row, (_sds((t // tm, 8, 128), F32),
                                                         pl.BlockSpec((None, 8, 128), lambda i, j, k: (i, 0, 0)))],
                              extras=common + [(target, row)], epi=loss_epi)
        return dy, jnp.sum(sq, axis=0), (h, hn, gate, e)

    def gate_epi(acc, hv, pv, wpv, *gain):
        gt = _sigmoid(acc)
        ev = _dot(_bf(pv), wpv, 1, 0)
        h_new = hv + gt * ev
        if not gain:
            return h_new, gt, ev
        r = lax.rsqrt(jnp.mean(h_new * h_new, axis=-1, keepdims=True) + EPS)
        return h_new, gt, ev, h_new * r * gain[0]

    res = _mm(f"ple_gate{l}", (t // tm, 1, 1), hn, row, wg, full(D_MODEL), (1, 0),
              [f32_row, bf_row, bf_row] + ([bf_row] if next_gain is not None else []),
              extras=common + ([(next_gain, VEC)] if next_gain is not None else []), epi=gate_epi)
    h_out, gate, e = res[0], res[1], res[2]
    return h_out, (res[3] if next_gain is not None else None), (h, hn, gate, e)


def _ple_bwd(l, dh, saved, p, norm_g, wg, deps=()):
    h, hn, gate, e = saved
    t = h.shape[0]
    tm = _row_tile(t)
    tk = _row_tile(t, 512)
    de, dz = _ple_gate_bwd(f"ple_gate_bwd{l}", dh, gate, e)
    full = lambda r: pl.BlockSpec((r, D_MODEL), lambda i, j, k: (0, 0))
    rowk = pl.BlockSpec((tk, D_MODEL), lambda i, j, k: (k, 0))
    (dwp,) = _mm(f"ple_dwp{l}", (1, 1, t // tk),
                 p, pl.BlockSpec((None, None, tk, PLE_DIM), lambda i, j, k: (l, 0, k, 0)),
                 de, rowk, (0, 0), [(_sds((PLE_DIM, D_MODEL), BF16), full(PLE_DIM))], deps=deps)
    (dwg,) = _mm(f"ple_dwg{l}", (1, 1, t // tk), hn, rowk, dz, rowk, (0, 0),
                 [(_sds((D_MODEL, D_MODEL), BF16), full(D_MODEL))])
    row = pl.BlockSpec((tm, D_MODEL), lambda i, j, k: (i, 0))
    dh_in, dg = _mm(f"ple_dhn{l}", (t // tm, 1, 1), dz, row, wg, full(D_MODEL), (1, 1),
                    _norm_bwd_outs(t, tm), extras=[(h, row), (dh, row), (norm_g, VEC)], epi=_norm_bwd_epi)
    return dh_in, jnp.sum(dg, axis=0), dwg, dwp


def _ret_layer_fwd(x, norm_g, wri, fetch_wro, gn, cos, sin, next_gain, deps=()):
    t = x.shape[0]
    tm = _row_tile(t)
    nsh, _, wsh = wri.shape
    hn = _rms_fwd("mix_norm0", x, norm_g)
    tp = _row_tile(t, 512)
    (proj,) = _mm_rows("ret_in", tp, hn, wri, 'nn_cols', [(_sds((t, RET_IN), BF16), _rows(tp, RET_IN))], deps=deps)
    gated, outp, states = _ret_fwd(proj, cos, sin, gn)
    wro = fetch_wro((gated,))
    row = pl.BlockSpec((tm, D_MODEL), lambda i, j, k: (i, 0))
    kt = 512
    more, epi = _residual_epi(next_gain)
    h1, hn_next = _mm("ret_out", (t // tm, 1, RET_V_W // kt),
                      gated, pl.BlockSpec((tm, kt), lambda i, j, k: (i, k)),
                      wro, pl.BlockSpec((kt, D_MODEL), lambda i, j, k: (k, 0)), (1, 0),
                      _residual_outs(t, row, next_gain), extras=[(x, row)] + more, epi=epi)
    return h1, hn_next, (x, hn, proj, gated, outp, states, wro)


def _ret_layer_bwd(dh, saved, norm_g, wri, gn, cos, sin, emit_out, emit_in, deps=()):
    x, hn, proj, gated, outp, states, wro = saved
    t = x.shape[0]
    tm = _row_tile(t)
    tk = _row_tile(t, 512)
    nsh, _, wsh = wri.shape
    (dgated,) = _mm("ret_dgated", (t // tm, RET_V_W // D_MODEL, 1),
                    dh, pl.BlockSpec((tm, D_MODEL), lambda i, j, k: (i, 0)),
                    wro, pl.BlockSpec((D_MODEL, D_MODEL), lambda i, j, k: (j, 0)), (1, 1),
                    [(_sds((t, RET_V_W), F32), pl.BlockSpec((tm, D_MODEL), lambda i, j, k: (i, j)))], deps=deps)
    (dwro,) = _mm("ret_dwro", (1, 1, t // tk),
                  gated, pl.BlockSpec((tk, RET_V_W), lambda i, j, k: (k, 0)),
                  dh, pl.BlockSpec((tk, D_MODEL), lambda i, j, k: (k, 0)), (0, 0),
                  [(_sds((RET_V_W, D_MODEL), BF16), pl.BlockSpec((RET_V_W, D_MODEL), lambda i, j, k: (0, 0)))])
    dproj, dgn = _ret_bwd(proj, cos, sin, gn, outp, states, dgated, deps=emit_out(dwro))
    half = nsh // 2
    (dwri,) = _mm("ret_dwri", (2, 1, t // tk),
                  hn, pl.BlockSpec((tk, D_MODEL), lambda i, j, k: (k, 0)),
                  dproj, pl.BlockSpec((tk, half * wsh), lambda i, j, k: (k, i)), (0, 0),
                  [(_sds((nsh, D_MODEL, wsh), BF16), pl.BlockSpec((half, D_MODEL, wsh), lambda i, j, k: (i, 0, 0)))],
                  split=wsh)
    deps = emit_in(dwri)
    td = _row_tile(t, 256)
    row = _rows(td, D_MODEL)
    dx, dg = _mm_rows("ret_dhn", td, dproj, wri, 'nt_cols', _norm_bwd_outs(t, td),
                      extras=[(x, row), (dh, row), (norm_g, VEC)], epi=_norm_bwd_epi, deps=deps)
    return dx, jnp.sum(dg, axis=0), dgn.reshape(RET_HEADS, RET_DV)


def _mla_layer_fwd(h, hn, wmi, qa, kva, wuq, wukv, gq, gk, wmo, tabs, next_gain):
    t = h.shape[0]
    tm = _row_tile(t)
    row = pl.BlockSpec((tm, D_MODEL), lambda i, j, k: (i, 0))
    (proj2,) = _mm("mla_in", (t // tm, 1, 1), hn, row,
                   wmi, pl.BlockSpec((D_MODEL, MLA_IN_PAD), lambda i, j, k: (0, 0)), (1, 0),
                   [(_sds((t, MLA_IN_PAD), F32), pl.BlockSpec((tm, MLA_IN_PAD), lambda i, j, k: (i, 0)))])
    cq, ckv = _mla_mid(proj2, qa, kva)
    qh, kh, vh = _mla_prep(cq, ckv, wuq, wukv, proj2, gq, gk, tabs)
    o, lse = _attn_fwd(qh, kh, vh)
    more, epi = _residual_epi(next_gain)
    h_out, hn_next = _mm("mla_out", (t // tm, 1, 1), o, row,
                         wmo, pl.BlockSpec((D_MODEL, D_MODEL), lambda i, j, k: (0, 0)), (1, 0),
                         _residual_outs(t, row, next_gain), extras=[(h, row)] + more, epi=epi)
    return h_out, hn_next, (h, hn, proj2, cq, ckv, qh, kh, vh, o, lse)


def _mla_layer_bwd(dh, saved, norm_g, wmi, qa, kva, wuq, wukv, gq, gk, wmo, tabs, deps=()):
    h, hn, proj2, cq, ckv, qh, kh, vh, o, lse = saved
    t = h.shape[0]
    tm = _row_tile(t)
    tk = _row_tile(t, 512)
    row = pl.BlockSpec((tm, D_MODEL), lambda i, j, k: (i, 0))
    rowk = pl.BlockSpec((tk, D_MODEL), lambda i, j, k: (k, 0))
    sq = pl.BlockSpec((D_MODEL, D_MODEL), lambda i, j, k: (0, 0))
    do, dob = _mm("mla_do", (t // tm, 1, 1), dh, row, wmo, sq, (1, 1),
                  [(_sds((t, D_MODEL), F32), row), (_sds((t, D_MODEL), BF16), row)], epi=lambda acc: (acc, acc),
                  deps=deps)
    (dwmo,) = _mm("mla_dwo", (1, 1, t // tk), o, rowk, dh, rowk, (0, 0), [(_sds((D_MODEL, D_MODEL), BF16), sq)])
    delta = _attn_delta(do, o, lse.shape[-1])
    dqt, dkh, dvh = _attn_bwd(qh, kh, vh, dob, lse, delta)
    dq, dkv, dkr, dgq, dgk = _mla_prep_bwd(cq, ckv, wuq, wukv, proj2, gq, gk, tabs, dqt, dkh, dvh)

    wide = MLA_HEADS * MLA_HD_PAD
    widek = pl.BlockSpec((tk, wide), lambda i, j, k: (k, 0))
    (dwuq,) = _mm("mla_dwuq", (1, 1, t // tk),
                  cq, pl.BlockSpec((tk, MLA_Q_RANK), lambda i, j, k: (k, 0)), dq, widek, (0, 0),
                  [(_sds((MLA_HEADS, MLA_Q_RANK, MLA_HD_PAD), BF16),
                    pl.BlockSpec((MLA_HEADS, MLA_Q_RANK, MLA_HD_PAD), lambda i, j, k: (0, 0, 0)))], split=MLA_HD_PAD)
    (dwukv,) = _mm("mla_dwukv", (1, 1, t // tk),
                   ckv, pl.BlockSpec((tk, MLA_KV_RANK), lambda i, j, k: (k, 0)), dkv, widek, (0, 0),
                   [(_sds((MLA_HEADS, MLA_KV_RANK, MLA_HD_PAD), BF16),
                     pl.BlockSpec((MLA_HEADS, MLA_KV_RANK, MLA_HD_PAD), lambda i, j, k: (0, 0, 0)))],
                   split=MLA_HD_PAD)
    side_by_side = lambda wg: wg.transpose(1, 0, 2).reshape(wg.shape[1], wide)
    widei = pl.BlockSpec((tm, wide), lambda i, j, k: (i, 0))
    (dcq,) = _mm("mla_dcq", (t // tm, 1, 1), dq, widei,
                 side_by_side(wuq), pl.BlockSpec((MLA_Q_RANK, wide), lambda i, j, k: (0, 0)), (1, 1),
                 [(_sds((t, MLA_Q_RANK), F32), pl.BlockSpec((tm, MLA_Q_RANK), lambda i, j, k: (i, 0)))])
    (dckv,) = _mm("mla_dckv", (t // tm, 1, 1), dkv, widei,
                  side_by_side(wukv), pl.BlockSpec((MLA_KV_RANK, wide), lambda i, j, k: (0, 0)), (1, 1),
                  [(_sds((t, MLA_KV_RANK), F32), pl.BlockSpec((tm, MLA_KV_RANK), lambda i, j, k: (i, 0)))])
    dproj2, dqa, dkva = _mla_mid_bwd(proj2, qa, kva, dcq, dckv, dkr)
    win = pl.BlockSpec((D_MODEL, MLA_IN_PAD), lambda i, j, k: (0, 0))
    (dwmi,) = _mm("mla_dwin", (1, 1, t // tk), hn, rowk,
                  dproj2, pl.BlockSpec((tk, MLA_IN_PAD), lambda i, j, k: (k, 0)), (0, 0),
                  [(_sds((D_MODEL, MLA_IN_PAD), BF16), win)])
    dh_in, dg = _mm("mla_dhn", (t // tm, 1, 1),
                    dproj2, pl.BlockSpec((tm, MLA_IN_PAD), lambda i, j, k: (i, 0)), wmi, win, (1, 1),
                    _norm_bwd_outs(t, tm), extras=[(h, row), (dh, row), (norm_g, VEC)], epi=_norm_bwd_epi)
    return dh_in, dict(mix=jnp.sum(dg, axis=0), wmi=dwmi, qa=dqa, kva=dkva, wuq=dwuq, wukv=dwukv, gq=dgq, gk=dgk,
                       wmo=dwmo)


def _local_step(x, p, target, w, fetch, emit=lambda group: ()):
    t = x.shape[0]
    inv = 1.0 / (ROPE_THETA ** (jnp.arange(0, RET_DK, 2, dtype=F32) / RET_DK))
    ang = jnp.arange(t, dtype=F32)[:, None] * inv[None, :]
    cos_r, sin_r = jnp.cos(ang), jnp.sin(ang)
    tabs = _mla_tables(t)
    row = lambda a, i: a[i:i + 1]

    h1, hn1, s_ret = _ret_layer_fwd(x, row(w['mix_norm'], 0), w['ret_w_in'],
                                    lambda after: fetch('ret_out', after)['ret_w_out'], w['ret_gn'], cos_r, sin_r,
                                    row(w['mlp_norm'], 0), deps=w['deps'])
    h2, hn2, s_mlp0 = _mlp_fwd(0, h1, hn1, fetch('mlp_w1_0', (h1,))['mlp_w1'],
                               lambda after: fetch('mlp_w2_0', after)['mlp_w2'], row(w['ple_norm'], 0))
    w0 = fetch('ple_0', (h2,))
    h3, hn3, s_ple0 = _ple_fwd(0, h2, hn2, p, w0['ple_gate_w'], w0['ple_proj_w'], row(w['mix_norm'], 1))
    wm = fetch('mla', (h3,))
    mla_w = (wm['mla_w_in'], w['mla_q_a_norm'], w['mla_kv_a_norm'], wm['mla_w_uq'], wm['mla_w_ukv'],
             w['mla_q_norm'], w['mla_k_norm'], wm['mla_w_out'], tabs)
    h4, hn4, s_mla = _mla_layer_fwd(h3, hn3, *mla_w, row(w['mlp_norm'], 1))
    w1 = fetch('layer_1', (h4,))
    h5, hn5, s_mlp1 = _mlp_fwd(1, h4, hn4, w1['mlp_w1'], lambda after: w1['mlp_w2'], row(w['ple_norm'], 1))
    dy, sq_err, s_ple1 = _ple_fwd(1, h5, hn5, p, w1['ple_gate_w'], w1['ple_proj_w'], None, target)

    n = N_DEV
    colsh = lambda a: a.reshape(a.shape[0], n, a.shape[1] // n).transpose(1, 0, 2)
    rowsh = lambda a: a.reshape(n, a.shape[0] // n, a.shape[1])
    big = {}

    def emit_group(group):
        big.update(group)
        return emit(group)

    dh5, dg_ple1, dwg1, dwp1 = _ple_bwd(1, dy, s_ple1, p, row(w['ple_norm'], 1), w1['ple_gate_w'])
    dh4, dg_mlp1, dw1_1, dw2_1 = _mlp_bwd(1, dh5, s_mlp1, row(w['mlp_norm'], 1))
    deps = emit_group({('ple_gate_w', 1): rowsh(dwg1), ('ple_proj_w', 1): colsh(dwp1),
                       ('mlp_w2', 1): dw2_1, ('mlp_w1', 1): dw1_1})
    dh3, gm = _mla_layer_bwd(dh4, s_mla, row(w['mix_norm'], 1), *mla_w, deps=deps)
    deps = emit_group({('mla_w_out', 0): rowsh(gm['wmo']), ('mla_w_uq', 0): gm['wuq'][:, :, :MLA_QKD],
                       ('mla_w_ukv', 0): gm['wukv'], ('mla_w_in', 0): rowsh(gm['wmi'][:, :MLA_IN])})
    dh2, dg_ple0, dwg0, dwp0 = _ple_bwd(0, dh3, s_ple0, p, row(w['ple_norm'], 0), w0['ple_gate_w'], deps=deps)
    dh1, dg_mlp0, dw1_0, dw2_0 = _mlp_bwd(0, dh2, s_mlp0, row(w['mlp_norm'], 0))
    deps = emit_group({('ple_gate_w', 0): rowsh(dwg0), ('ple_proj_w', 0): colsh(dwp0),
                       ('mlp_w2', 0): dw2_0, ('mlp_w1', 0): dw1_0})
    dx, dg_mix0, dgn = _ret_layer_bwd(
        dh1, s_ret, row(w['mix_norm'], 0), w['ret_w_in'], w['ret_gn'], cos_r, sin_r,
        lambda dwro: emit_group({('ret_w_out', 0): rowsh(dwro)}),
        lambda dwri: emit_group({('ret_w_in', 0): dwri}), deps=deps)

    small = dict(
        mix_norm=[dg_mix0, gm['mix']], mlp_norm=[dg_mlp0, dg_mlp1], ple_norm=[dg_ple0, dg_ple1],
        ret_gn=dgn, mla_q_a_norm=gm['qa'], mla_kv_a_norm=gm['kva'], mla_q_norm=gm['gq'], mla_k_norm=gm['gk'],
    )
    return sq_err, dx, big, small


def _my_place():
    x, y, c = lax.axis_index("x"), lax.axis_index("y"), lax.axis_index("c")
    return x, y, c


def _flat(px, py, pc):
    return 4 * px + 2 * py + pc


def _peer(x, y, c, r):
    return (1 - x if r & 4 else x, 1 - y if r & 2 else y, 1 - c if r & 1 else c)


def _all_gather(arrays):
    n = len(arrays)

    def body(*refs):
        ins, outs = refs[:n], refs[n:2 * n]
        send_sems, recv_sems, local_sems = refs[2 * n:]
        x, y, c = _my_place()
        me, sibling = (x, y, c), (x, y, 1 - c)
        chips = [(1 - x, y), (x, 1 - y), (1 - x, 1 - y)]

        def copy(a, k, block, to, src=None):
            slot = outs[a].at[_flat(*block)]
            return pltpu.make_async_remote_copy(
                src_ref=slot if src is None else src, dst_ref=slot,
                send_sem=send_sems.at[a, k], recv_sem=recv_sems.at[a, k], device_id=to, device_id_type=MESH)

        mine = [pltpu.make_async_copy(ins[a], outs[a].at[_flat(*me)], local_sems.at[a]) for a in range(n)]
        for cp in mine:
            cp.start()
        first = []
        for a in range(n):
            first.append(copy(a, 0, me, sibling, src=ins[a]))
            first += [copy(a, 1 + j, me, (*chip, c), src=ins[a]) for j, chip in enumerate(chips)]
        for cp in first:
            cp.start()
        passed = []
        for a in range(n):
            for j, chip in enumerate(chips):
                copy(a, 1 + j, (*chip, c), me).wait_recv()
                passed.append(copy(a, 4 + j, (*chip, c), sibling))
                passed[-1].start()
        for a in range(n):
            copy(a, 0, sibling, me).wait_recv()
            for j, chip in enumerate(chips):
                copy(a, 4 + j, (*chip, 1 - c), me).wait_recv()
        for cp in first + passed:
            cp.wait_send()
        for cp in mine:
            cp.wait()

    return pl.pallas_call(
        body, name="all_gather_weights",
        in_specs=[ANY] * n, out_specs=[ANY] * n,
        out_shape=[_sds((N_DEV,) + a.shape, a.dtype) for a in arrays],
        scratch_shapes=[pltpu.SemaphoreType.DMA((n, 7)), pltpu.SemaphoreType.DMA((n, 7)),
                        pltpu.SemaphoreType.DMA((n,))],
    )(*arrays)


HBM = pl.BlockSpec(memory_space=pltpu.HBM)
SEMS = pl.BlockSpec(memory_space=pltpu.SEMAPHORE)
SIDE_EFFECT = pltpu.SideEffectType.DATAFLOW_SIDE_EFFECTING


def _rs_copies(x, y, c, srcs, lands, send_sems, recv_sems):
    copies = []
    for a in range(len(srcs)):
        for r in range(1, N_DEV):
            peer = _peer(x, y, c, r)
            k = a * (N_DEV - 1) + r - 1
            copies.append(pltpu.make_async_remote_copy(
                src_ref=srcs[a].at[_flat(*peer)], dst_ref=lands[a].at[r - 1],
                send_sem=send_sems.at[k], recv_sem=recv_sems.at[k], device_id=peer, device_id_type=MESH))
    return copies


def _rs_start(name, arrays):
    n = len(arrays)
    hbm = lambda a: pltpu.with_memory_space_constraint(a, pltpu.HBM)
    lands = [hbm(lax.empty((N_DEV - 1,) + a.shape[1:], a.dtype)) for a in arrays]

    def body(*refs):
        srcs, lnd = refs[:n], refs[n:2 * n]
        send_sems, recv_sems = refs[2 * n], refs[2 * n + 1]
        token = refs[-1]
        for cp in _rs_copies(*_my_place(), srcs, lnd, send_sems, recv_sems):
            cp.start()
        token[...] = jnp.zeros_like(token)

    outs = pl.pallas_call(
        body, name=name,
        in_specs=[HBM] * (2 * n),
        out_specs=[SEMS, SEMS] + [HBM] * (2 * n) + [pl.BlockSpec(memory_space=pltpu.VMEM)],
        out_shape=[pltpu.SemaphoreType.DMA((n * (N_DEV - 1),)), pltpu.SemaphoreType.DMA((n * (N_DEV - 1),))]
        + [pltpu.HBM(a.shape, a.dtype) for a in arrays] + [pltpu.HBM(l.shape, l.dtype) for l in lands]
        + [_sds((8, 128), F32)],
        input_output_aliases={i: 2 + i for i in range(2 * n)},
        compiler_params=pltpu.CompilerParams(has_side_effects=SIDE_EFFECT),
    )(*[hbm(a) for a in arrays], *lands)
    return outs[0], outs[1], outs[2:2 + n], outs[2 + n:2 + 2 * n], outs[-1]


def _rs_wait(name, send_sems, recv_sems, srcs, lands, after):
    n = len(srcs)

    def body(*refs):
        src_refs, lnd = refs[:n], refs[n:2 * n]
        send, recv = refs[2 * n], refs[2 * n + 1]
        for cp in _rs_copies(*_my_place(), src_refs, lnd, send, recv):
            cp.wait_send()
            cp.wait_recv()

    outs = pl.pallas_call(
        body, name=name,
        in_specs=[HBM] * (2 * n) + [SEMS, SEMS] + [ANY] * len(after),
        out_specs=[HBM] * (2 * n),
        out_shape=[pltpu.HBM(a.shape, a.dtype) for a in list(srcs) + list(lands)],
        input_output_aliases={i: i for i in range(2 * n)},
        compiler_params=pltpu.CompilerParams(has_side_effects=SIDE_EFFECT),
    )(*srcs, *lands, send_sems, recv_sems, *after)
    return outs[:n], outs[n:]


SMALL_PACK_ROWS = 16


def _all_reduce_small(rows, deps=()):
    n = len(rows)

    def body(*refs):
        ins = refs[:n]
        out_ref, mine, buf, send_sems, recv_sems = refs[n + len(deps):]
        x, y, c = _my_place()
        mine[...] = jnp.zeros_like(mine)
        for (r0, a), ref in zip(rows, ins):
            mine[r0:r0 + a.shape[0], 0:a.shape[1]] = ref[...]
        buf[_flat(x, y, c)] = mine[...]
        copies = []
        for r in range(1, N_DEV):
            peer = _peer(x, y, c, r)
            send = pltpu.make_async_remote_copy(
                src_ref=mine, dst_ref=buf.at[_flat(x, y, c)],
                send_sem=send_sems.at[r - 1], recv_sem=recv_sems.at[r - 1], device_id=peer, device_id_type=MESH)
            send.start()
            recv = pltpu.make_async_remote_copy(
                src_ref=mine, dst_ref=buf.at[_flat(*peer)],
                send_sem=send_sems.at[r - 1], recv_sem=recv_sems.at[r - 1], device_id=peer, device_id_type=MESH)
            copies.append((send, recv))
        for send, recv in copies:
            send.wait_send()
            recv.wait_recv()
        acc = buf[0]
        for s in range(1, N_DEV):
            acc = acc + buf[s]
        out_ref[...] = acc

    vm = pl.BlockSpec(memory_space=pltpu.VMEM)
    shape = (SMALL_PACK_ROWS, D_MODEL)
    return pl.pallas_call(
        body, name="all_reduce_small", in_specs=[vm] * n + [ANY] * len(deps), out_specs=vm,
        out_shape=_sds(shape, F32),
        scratch_shapes=[pltpu.VMEM(shape, F32), pltpu.VMEM((N_DEV,) + shape, F32),
                        pltpu.SemaphoreType.DMA((7,)), pltpu.SemaphoreType.DMA((7,))],
    )(*[a for _, a in rows], *deps)


def _adamw_math(w, g, m, v):
    m = ADAM_B1 * m + (1.0 - ADAM_B1) * g
    v = ADAM_B2 * v + (1.0 - ADAM_B2) * (g * g)
    m_hat = m / (1.0 - ADAM_B1 ** ADAM_STEP)
    v_hat = v / (1.0 - ADAM_B2 ** ADAM_STEP)
    delta = -ADAM_LR * (m_hat / (jnp.sqrt(v_hat) + ADAM_EPS) + ADAM_WD * w)
    return delta, m, v


def _adamw_big(name, w, m, v, srcs, lands, me):
    nl, rows, cols = w.shape
    tr = next(cand for cand in (256, 128, 64, 32, 16, 8) if rows % cand == 0)

    def body(me_ref, w_ref, m_ref, v_ref, *rest):
        src_refs, land_refs = rest[:nl], rest[nl:2 * nl]
        g_ref, d_ref, mo_ref, vo_ref = rest[2 * nl:]
        for layer in range(nl):
            @pl.when(pl.program_id(0) == layer)
            def _():
                g = src_refs[layer][...].astype(F32)
                for s in range(N_DEV - 1):
                    g = g + land_refs[layer][s].astype(F32)
                delta, mn, vn = _adamw_math(w_ref[...], g, m_ref[...], v_ref[...])
                g_ref[...] = g
                d_ref[...] = delta
                mo_ref[...] = mn
                vo_ref[...] = vn

    blk = pl.BlockSpec((None, tr, cols), lambda l, i, me_ref: (l, i, 0))
    at = lambda layer, l, i: jnp.where(l == layer, i, 0)
    own = [pl.BlockSpec((None, tr, cols), functools.partial(lambda layer, l, i, me_ref: (me_ref[0], at(layer, l, i), 0),
                                                            layer)) for layer in range(nl)]
    peers = [pl.BlockSpec((N_DEV - 1, tr, cols), functools.partial(lambda layer, l, i, me_ref: (0, at(layer, l, i), 0),
                                                                   layer)) for layer in range(nl)]
    return pl.pallas_call(
        body, name=name,
        grid_spec=pltpu.PrefetchScalarGridSpec(
            num_scalar_prefetch=1, grid=(nl, rows // tr),
            in_specs=[blk, blk, blk] + own + peers, out_specs=[blk] * 4),
        out_shape=[_sds((nl, rows, cols), F32)] * 4,
        compiler_params=_cparams(("arbitrary", "arbitrary")),
    )(me, w, m, v, *srcs, *lands)


def _adamw_small(ws, gs, ms, vs):
    n = len(ws)

    def body(*refs):
        w_refs, g_refs, m_refs, v_refs = (refs[i * n:(i + 1) * n] for i in range(4))
        d_out, m_out, v_out = (refs[(4 + i) * n:(5 + i) * n] for i in range(3))
        for i in range(n):
            delta, mn, vn = _adamw_math(w_refs[i][...], g_refs[i][...], m_refs[i][...], v_refs[i][...])
            d_out[i][...] = delta
            m_out[i][...] = mn
            v_out[i][...] = vn

    vm = pl.BlockSpec(memory_space=pltpu.VMEM)
    outs = pl.pallas_call(
        body, name="adamw_small", in_specs=[vm] * (4 * n), out_specs=[vm] * (3 * n),
        out_shape=[_sds(a.shape, F32) for a in ws] * 3,
    )(*ws, *gs, *ms, *vs)
    return outs[:n], outs[n:2 * n], outs[2 * n:]


SMALL_ROWS = 16


def _pad_to(a, rows, cols):
    return jnp.pad(a, ((0, rows - a.shape[0]), (0, cols - a.shape[1])))


def _place_own(blocks):
    me = _flat(*_my_place())
    return [lax.dynamic_update_slice(lax.empty((N_DEV,) + b.shape, b.dtype), b[None], (me,) + (0,) * b.ndim)
            for b in blocks]


def _ag_copies(x, y, c, blocks, bufs, send_sems, recv_sems):
    sends, recvs = [], []
    for a in range(len(blocks)):
        for r in range(1, N_DEV):
            peer = _peer(x, y, c, r)
            k = a * (N_DEV - 1) + r - 1
            make = lambda place: pltpu.make_async_remote_copy(
                src_ref=blocks[a], dst_ref=bufs[a].at[_flat(*place)],
                send_sem=send_sems.at[k], recv_sem=recv_sems.at[k], device_id=peer, device_id_type=MESH)
            sends.append(make((x, y, c)))
            recvs.append(make(peer))
    return sends, recvs


def _ag_start(groups, after):
    flat = [pair for g in groups for pair in g]
    n, ng = len(flat), len(groups)
    hbm = lambda a: pltpu.with_memory_space_constraint(a, pltpu.HBM)

    def body(*refs):
        blocks, bufs = refs[:n], refs[n:2 * n]
        sems = refs[2 * n + len(after):2 * n + len(after) + 2 * ng]
        x, y, c = _my_place()
        at = 0
        for gi, g in enumerate(groups):
            sends, _ = _ag_copies(x, y, c, blocks[at:at + len(g)], bufs[at:at + len(g)], sems[2 * gi], sems[2 * gi + 1])
            for cp in sends:
                cp.start()
            at += len(g)
        refs[-1][...] = jnp.zeros_like(refs[-1])

    sem_shapes = [pltpu.SemaphoreType.DMA((len(g) * (N_DEV - 1),)) for g in groups for _ in range(2)]
    outs = pl.pallas_call(
        body, name="gather_start",
        in_specs=[HBM] * (2 * n) + [ANY] * len(after),
        out_specs=[SEMS] * (2 * ng) + [HBM] * (2 * n) + [pl.BlockSpec(memory_space=pltpu.VMEM)],
        out_shape=sem_shapes + [pltpu.HBM(b.shape, b.dtype) for b, _ in flat]
        + [pltpu.HBM(u.shape, u.dtype) for _, u in flat] + [_sds((8, 128), F32)],
        input_output_aliases={i: 2 * ng + i for i in range(2 * n)},
        compiler_params=pltpu.CompilerParams(has_side_effects=SIDE_EFFECT),
    )(*[hbm(b) for b, _ in flat], *[hbm(u) for _, u in flat], *after)
    blocks_thru, bufs_thru = outs[2 * ng:2 * ng + n], outs[2 * ng + n:2 * ng + 2 * n]
    started, at = [], 0
    for gi, g in enumerate(groups):
        started.append((outs[2 * gi], outs[2 * gi + 1], blocks_thru[at:at + len(g)], bufs_thru[at:at + len(g)]))
        at += len(g)
    return started, outs[-1]


def _ag_wait(name, send_sems, recv_sems, blocks, bufs, after):
    n = len(blocks)

    def body(*refs):
        sends, recvs = _ag_copies(*_my_place(), refs[:n], refs[n:2 * n], refs[2 * n], refs[2 * n + 1])
        for s, r in zip(sends, recvs):
            s.wait_send()
            r.wait_recv()

    outs = pl.pallas_call(
        body, name=name,
        in_specs=[HBM] * (2 * n) + [SEMS, SEMS] + [ANY] * len(after),
        out_specs=[HBM] * (2 * n),
        out_shape=[pltpu.HBM(a.shape, a.dtype) for a in list(blocks) + list(bufs)],
        input_output_aliases={i: i for i in range(2 * n)},
        compiler_params=pltpu.CompilerParams(has_side_effects=SIDE_EFFECT),
    )(*blocks, *bufs, send_sems, recv_sems, *after)
    return outs[n:]


def _prepare_weights(p):
    n = N_DEV
    bf = lambda a: a.astype(BF16)
    gn_pack = jnp.concatenate([
        _pad_to(p['ret_gn'][0], RET_HEADS, 128), _pad_to(p['mla_q_a_norm'], 1, 128),
        _pad_to(p['mla_kv_a_norm'], 1, 128), jnp.zeros((2, 128), F32)], axis=0)
    ple = lambda l: [bf(p['ple_gate_w'][l]), bf(p['ple_proj_w'][l])]
    names = ('ret_out', 'mlp_w1_0', 'mlp_w2_0', 'ple_0', 'mla', 'layer_1')
    later = [[bf(p['ret_w_out'][0])], [bf(p['mlp_w1'][0])], [bf(p['mlp_w2'][0])], ple(0),
             [bf(p['mla_w_in'][0]), bf(p['mla_w_uq'][0]), bf(p['mla_w_ukv'][0]), bf(p['mla_w_out'][0])],
             [bf(p['mlp_w1'][1]), bf(p['mlp_w2'][1])] + ple(1)]
    bufs = _place_own([b for g in later for b in g])
    pack, wri = _all_gather([gn_pack, bf(p['ret_w_in'][0])])
    groups, at = [], 0
    for g in later:
        groups.append(list(zip(g, bufs[at:at + len(g)])))
        at += len(g)
    started, token = _ag_start(groups, (wri,))

    w = {k: p[k] for k in ('mix_norm', 'mlp_norm', 'ple_norm')}
    w['ret_gn'] = pack[:, :RET_HEADS, :RET_DV // n].transpose(1, 0, 2).reshape(RET_HEADS, RET_DV)
    w['mla_q_a_norm'] = pack[:, RET_HEADS, :MLA_Q_RANK // n].reshape(1, MLA_Q_RANK)
    w['mla_kv_a_norm'] = pack[:, RET_HEADS + 1, :MLA_KV_RANK // n].reshape(1, MLA_KV_RANK)
    w['ret_w_in'] = wri
    w['mla_q_norm'] = _pad_to(p['mla_q_norm'], 1, MLA_HD_PAD)
    w['mla_k_norm'] = _pad_to(p['mla_k_norm'], 1, MLA_HD_PAD)
    w['deps'] = (token,)

    def fetch(name, after):
        got = list(_ag_wait("gather_wait_" + name, *started[names.index(name)], after))
        if name == 'ret_out':
            return dict(ret_w_out=got[0].reshape(RET_V_W, D_MODEL))
        if name == 'mla':
            wmi, wuq, wukv, wmo = got
            return dict(mla_w_in=jnp.pad(wmi.reshape(D_MODEL, MLA_IN), ((0, 0), (0, MLA_IN_PAD - MLA_IN))),
                        mla_w_uq=jnp.pad(wuq, ((0, 0), (0, 0), (0, MLA_HD_PAD - MLA_QKD))),
                        mla_w_ukv=wukv, mla_w_out=wmo.reshape(D_MODEL, D_MODEL))
        out = {}
        if name in ('mlp_w1_0', 'layer_1'):
            out['mlp_w1'] = got.pop(0)
        if name in ('mlp_w2_0', 'layer_1'):
            out['mlp_w2'] = got.pop(0)
        if name in ('ple_0', 'layer_1'):
            out['ple_gate_w'] = got[0].reshape(D_MODEL, D_MODEL)
            out['ple_proj_w'] = got[1].transpose(1, 0, 2).reshape(PLE_DIM, D_MODEL)
        return out

    return w, fetch


def _small_grads(small, after):
    rows = [(0, small['mix_norm'][0]), (1, small['mix_norm'][1]), (2, small['mlp_norm'][0]),
            (3, small['mlp_norm'][1]), (4, small['ple_norm'][0]), (5, small['ple_norm'][1]),
            (6, small['ret_gn']), (10, small['mla_q_a_norm']), (11, small['mla_kv_a_norm']),
            (12, small['mla_q_norm']), (13, small['mla_k_norm'])]
    gs = _all_reduce_small(rows, after)
    me = _flat(*_my_place())
    n = N_DEV
    return dict(
        mix_norm=gs[0:2], mlp_norm=gs[2:4], ple_norm=gs[4:6],
        ret_gn=lax.dynamic_slice(gs, (6, me * (RET_DV // n)), (RET_HEADS, RET_DV // n)),
        mla_q_a_norm=lax.dynamic_slice(gs, (10, me * (MLA_Q_RANK // n)), (1, MLA_Q_RANK // n)),
        mla_kv_a_norm=lax.dynamic_slice(gs, (11, me * (MLA_KV_RANK // n)), (1, MLA_KV_RANK // n)),
        mla_q_norm=gs[12:13, :MLA_QKD], mla_k_norm=gs[13:14, :MLA_QKD])


def kernel(x, p, mix_norm, ret_w_in, ret_gn, ret_w_out, mla_w_in, mla_q_a_norm, mla_kv_a_norm, mla_w_uq, mla_w_ukv, mla_q_norm, mla_k_norm, mla_w_out, mlp_norm, mlp_w1, mlp_w2, ple_norm, ple_gate_w, ple_proj_w, loss_target, m_mix_norm, m_ret_w_in, m_ret_gn, m_ret_w_out, m_mla_w_in, m_mla_q_a_norm, m_mla_kv_a_norm, m_mla_w_uq, m_mla_w_ukv, m_mla_q_norm, m_mla_k_norm, m_mla_w_out, m_mlp_norm, m_mlp_w1, m_mlp_w2, m_ple_norm, m_ple_gate_w, m_ple_proj_w, v_mix_norm, v_ret_w_in, v_ret_gn, v_ret_w_out, v_mla_w_in, v_mla_q_a_norm, v_mla_kv_a_norm, v_mla_w_uq, v_mla_w_ukv, v_mla_q_norm, v_mla_k_norm, v_mla_w_out, v_mlp_norm, v_mlp_w1, v_mlp_w2, v_ple_norm, v_ple_gate_w, v_ple_proj_w):
    given = dict(locals())
    params = {n: given[n] for n in WEIGHTS}
    w, fetch = _prepare_weights(params)

    started = []

    def emit(group):
        keys = list(group)
        send, recv, srcs, lands, token = _rs_start(f"rs_start{len(started)}", [group[k] for k in keys])
        started.append((keys, send, recv, srcs, lands))
        return (token,)

    sq_err, grad_x, _, small = _local_step(x[0], p, loss_target[0], w, fetch, emit)
    loss = lax.psum(0.5 / D_MODEL * sq_err[0, 0], ("x", "y", "c"))

    grads, deltas, new_m, new_v = {}, {}, {}, {}

    def small_updates(after):
        sg = _small_grads(small, after)
        two_d = lambda a: a.reshape(-1, a.shape[-1])
        d_s, m_s, v_s = _adamw_small(
            [two_d(params[n]) for n in SMALL], [sg[n] for n in SMALL],
            [two_d(given["m_" + n]) for n in SMALL], [two_d(given["v_" + n]) for n in SMALL])
        for i, n in enumerate(SMALL):
            shape = params[n].shape
            grads[n], deltas[n], new_m[n], new_v[n] = (a.reshape(shape) for a in (sg[n], d_s[i], m_s[i], v_s[i]))
        return (d_s[0],)

    me = _flat(*_my_place()).astype(jnp.int32).reshape(1)
    after = (grad_x,)
    src_of, land_of = {}, {}
    for gi, (keys, send, recv, srcs, lands) in enumerate(started):
        if gi == len(started) - 1:
            after = small_updates(after)
        srcs, lands = _rs_wait(f"rs_wait{gi}", send, recv, srcs, lands, after)
        for k, s, l in zip(keys, srcs, lands):
            src_of[k], land_of[k] = s, l
        done = [n for n in BIG if n not in grads and all((n, l) in src_of for l in range(params[n].shape[0]))]
        for n in done:
            layers = range(params[n].shape[0])
            grads[n], deltas[n], new_m[n], new_v[n] = _adamw_big(
                "adamw_" + n, params[n], given["m_" + n], given["v_" + n],
                [src_of[(n, l)] for l in layers], [land_of[(n, l)] for l in layers], me)
        if done:
            after = tuple(deltas[n] for n in done)

    return (loss, grad_x[None], *[grads[n] for n in WEIGHTS], *[deltas[n] for n in WEIGHTS],
            *[new_m[n] for n in WEIGHTS], *[new_v[n] for n in WEIGHTS])
```

```python
import functools
import math

import jax
import jax.numpy as jnp
from jax import lax
from jax.experimental import pallas as pl
from jax.experimental.pallas import tpu as pltpu

F32 = jnp.float32
BF16 = jnp.bfloat16
MESH = pl.DeviceIdType.MESH
ANY = pl.BlockSpec(memory_space=pl.ANY)

N_DEV = 8
D_MODEL = 1024
CHUNK = 64
EPS = 1e-6
ROPE_THETA = 10000.0
RET_HEADS = 4
RET_DK = 256
RET_DV = 512
RET_QK_W = RET_HEADS * RET_DK
RET_V_W = RET_HEADS * RET_DV
RET_IN = 2 * RET_QK_W + 2 * RET_V_W
MLA_HEADS = 8
MLA_NOPE = 128
MLA_ROPE = 64
MLA_QKD = MLA_NOPE + MLA_ROPE
MLA_VD = 128
MLA_Q_RANK = 384
MLA_KV_RANK = 256
MLA_IN = MLA_Q_RANK + MLA_KV_RANK + MLA_ROPE
MLA_IN_PAD = 768
MLA_HD_PAD = 256
D_FF = 4096
PLE_DIM = 256
ATT_SCALE = MLA_QKD ** -0.5
LOG2E = 1.4426950408889634
ATT_EXP2 = ATT_SCALE * LOG2E

ADAM_LR = 0.001
ADAM_B1 = 0.9
ADAM_B2 = 0.999
ADAM_EPS = 1e-08
ADAM_WD = 0.01
ADAM_STEP = 10

VMEM_LIMIT = 52 * 1024 * 1024
ROW_TILE = 1024
RET_ROWS = 256
ATT_BLOCK = 256
ATT_QROWS = 1024
ATT_KROWS = 1024
ATT_HEADS = 2

WEIGHTS = ['mix_norm', 'ret_w_in', 'ret_gn', 'ret_w_out', 'mla_w_in', 'mla_q_a_norm', 'mla_kv_a_norm',
           'mla_w_uq', 'mla_w_ukv', 'mla_q_norm', 'mla_k_norm', 'mla_w_out', 'mlp_norm', 'mlp_w1', 'mlp_w2',
           'ple_norm', 'ple_gate_w', 'ple_proj_w']
BIG = ['ret_w_in', 'ret_w_out', 'mla_w_in', 'mla_w_uq', 'mla_w_ukv', 'mla_w_out', 'mlp_w1', 'mlp_w2',
       'ple_gate_w', 'ple_proj_w']
SMALL = [w for w in WEIGHTS if w not in BIG]


def _cparams(sem=None):
    return pltpu.CompilerParams(dimension_semantics=sem, vmem_limit_bytes=VMEM_LIMIT)


def _dot(a, b, ca, cb):
    return lax.dot_general(a, b, (((ca,), (cb,)), ((), ())), preferred_element_type=F32)


def _bf(v):
    return v if v.dtype == BF16 else v.astype(BF16)


def _sigmoid(z):
    return 1.0 / (1.0 + jnp.exp(-z))


def _mm(name, grid, a, a_spec, b, b_spec, contract, outs, extras=(), epi=None, deps=(), split=None):
    nk = grid[2]
    n_ex, n_out, n_dep = len(extras), len(outs), len(deps)
    acc_shape = tuple(d for d in outs[0][1].block_shape if d is not None)
    if split is not None:
        acc_shape = (acc_shape[1], acc_shape[0] * split)

    def body(*refs):
        a_ref, b_ref = refs[:2]
        ex_refs = refs[2:2 + n_ex]
        out_refs = refs[2 + n_ex + n_dep:2 + n_ex + n_dep + n_out]

        def product():
            return _dot(_bf(a_ref[...]), _bf(b_ref[...]), contract[0], contract[1])

        def finish(acc):
            if split is not None:
                for j in range(acc_shape[1] // split):
                    out_refs[0][j] = acc[:, j * split:(j + 1) * split].astype(out_refs[0].dtype)
                return
            acc = acc[...]
            res = epi(acc, *[r[...] for r in ex_refs]) if epi is not None else (acc,)
            for o, r in zip(out_refs, res):
                o[...] = r.astype(o.dtype)

        if nk == 1:
            finish(product())
        else:
            acc_ref = refs[-1]
            k = pl.program_id(2)

            @pl.when(k == 0)
            def _():
                acc_ref[...] = jnp.zeros_like(acc_ref)

            acc_ref[...] += product()

            @pl.when(k == nk - 1)
            def _():
                finish(acc_ref)

    return pl.pallas_call(
        body, name=name, grid=grid,
        in_specs=[a_spec, b_spec] + [s for _, s in extras] + [ANY] * n_dep,
        out_specs=[s for _, s in outs],
        out_shape=[s for s, _ in outs],
        scratch_shapes=[pltpu.VMEM(acc_shape, F32)] if nk > 1 else [],
        compiler_params=_cparams(("parallel", "parallel", "arbitrary")),
    )(a, b, *[x for x, _ in extras], *deps)


def _mm_rows(name, tm, a, w, mode, outs, extras=(), epi=None, deps=()):
    n_sh, rows, cols = w.shape
    n_ex, n_out, n_dep = len(extras), len(outs), len(deps)
    by_cols = mode in ('nn_cols', 'nt_rows')
    width = cols if mode == 'nn_cols' else rows

    def body(*refs):
        a_ref, w_ref = refs[:2]
        ex_refs = refs[2:2 + n_ex]
        out_refs = refs[2 + n_ex + n_dep:2 + n_ex + n_dep + n_out]
        if by_cols:
            av = _bf(a_ref[...])
            for s in range(n_sh):
                cs = slice(s * width, (s + 1) * width)
                acc = _dot(av, w_ref[s], 1, 0 if mode == 'nn_cols' else 1)
                res = epi(acc, *[r[:, cs] for r in ex_refs]) if epi is not None else (acc,)
                for o, r in zip(out_refs, res):
                    o[:, cs] = r.astype(o.dtype)
        else:
            chunk = rows if mode == 'nn_rows' else cols
            acc = None
            for s in range(n_sh):
                part = _dot(_bf(a_ref[:, s * chunk:(s + 1) * chunk]), w_ref[s], 1, 0 if mode == 'nn_rows' else 1)
                acc = part if acc is None else acc + part
            res = epi(acc, *[r[...] for r in ex_refs]) if epi is not None else (acc,)
            for o, r in zip(out_refs, res):
                o[...] = r.astype(o.dtype)

    t, ka = a.shape
    return pl.pallas_call(
        body, name=name, grid=(t // tm, 1, 1),
        in_specs=[pl.BlockSpec((tm, ka), lambda i, j, k: (i, 0)),
                  pl.BlockSpec((n_sh, rows, cols), lambda i, j, k: (0, 0, 0))] + [s for _, s in extras] + [ANY] * n_dep,
        out_specs=[s for _, s in outs],
        out_shape=[s for s, _ in outs],
        compiler_params=_cparams(("parallel", "arbitrary", "arbitrary")),
    )(a, w, *[x for x, _ in extras], *deps)


def _sds(shape, dtype):
    return jax.ShapeDtypeStruct(shape, dtype)


def _row_tile(t, cap=ROW_TILE):
    return min(cap, t)


def _rms_fwd(name, x, g, deps=()):
    t, d = x.shape
    tm = _row_tile(t)

    def body(x_ref, g_ref, *rest):
        o_ref = rest[-1]
        xv = x_ref[...]
        r = lax.rsqrt(jnp.mean(xv * xv, axis=-1, keepdims=True) + EPS)
        o_ref[...] = (xv * r * g_ref[...]).astype(o_ref.dtype)

    return pl.pallas_call(
        body, name=name, grid=(t // tm,),
        in_specs=[pl.BlockSpec((tm, d), lambda i: (i, 0)), pl.BlockSpec((1, d), lambda i: (0, 0))] + [ANY] * len(deps),
        out_specs=pl.BlockSpec((tm, d), lambda i: (i, 0)),
        out_shape=_sds((t, d), BF16),
        compiler_params=_cparams(("parallel",)),
    )(x, g, *deps)


def _rms_bwd_rows(dy, xv, g, n):
    r = lax.rsqrt(jnp.sum(xv * xv, axis=-1, keepdims=True) / n + EPS)
    xh = xv * r
    dxh = dy * g
    dx = r * (dxh - xh * (jnp.sum(dxh * xh, axis=-1, keepdims=True) / n))
    return dx, dy * xh


def _ple_gate_bwd(name, dh, gate, e):
    t, d = dh.shape
    tm = _row_tile(t)

    def body(dh_ref, g_ref, e_ref, de_ref, dz_ref):
        dh_v, gt = dh_ref[...], g_ref[...].astype(F32)
        de_ref[...] = (dh_v * gt).astype(BF16)
        dz_ref[...] = (dh_v * e_ref[...].astype(F32) * (gt * (1.0 - gt))).astype(BF16)

    row = pl.BlockSpec((tm, d), lambda i: (i, 0))
    return pl.pallas_call(
        body, name=name, grid=(t // tm,), in_specs=[row, row, row], out_specs=[row, row],
        out_shape=[_sds((t, d), BF16), _sds((t, d), BF16)],
        compiler_params=_cparams(("parallel",)),
    )(dh, gate, e)


def _rope_half(v, cos, sin):
    half = v.shape[-1] // 2
    v1, v2 = v[:, :half], v[:, half:]
    return jnp.concatenate([v1 * cos - v2 * sin, v2 * cos + v1 * sin], axis=-1)


def _ret_consts():
    lg = jnp.log(1.0 - 2.0 ** (-5.0 - jnp.arange(RET_HEADS, dtype=F32)))
    idx = jnp.arange(CHUNK, dtype=F32)
    intra = jnp.exp(lg[:, None, None] * jnp.abs(idx[:, None] - idx[None, :]))
    qdec = jnp.exp(lg[:, None] * (idx + 1.0))
    kdec = jnp.exp(lg[:, None] * (CHUNK - 1.0 - idx))
    cdec = jnp.exp(lg * CHUNK)
    qdec = jnp.broadcast_to(qdec[:, :, None], (RET_HEADS, CHUNK, RET_DK))
    kdec = jnp.broadcast_to(kdec[:, :, None], (RET_HEADS, CHUNK, RET_DK))
    cdec = jnp.broadcast_to(cdec[:, None, None], (RET_HEADS, 1, RET_DV))
    return intra, qdec, kdec, cdec


def _ret_specs(rb, rev_nb=None):
    blk = (lambda i: i) if rev_nb is None else (lambda i: rev_nb - 1 - i)
    full = lambda shape: pl.BlockSpec(shape, lambda i: (0,) * len(shape))
    return dict(
        proj=pl.BlockSpec((rb, RET_IN), lambda i: (blk(i), 0)),
        tab=pl.BlockSpec((rb, RET_DK // 2), lambda i: (blk(i), 0)),
        vw=pl.BlockSpec((rb, RET_V_W), lambda i: (blk(i), 0)),
        st=pl.BlockSpec((rb // CHUNK, RET_HEADS, RET_DK, RET_DV), lambda i: (blk(i), 0, 0, 0)),
        gn=full((RET_HEADS, 1, RET_DV)),
        intra=full((RET_HEADS, CHUNK, CHUNK)),
        dec=full((RET_HEADS, CHUNK, RET_DK)),
        cdec=full((RET_HEADS, 1, RET_DV)),
    )


def _ret_fwd(proj, cos, sin, gn):
    t = proj.shape[0]
    rb = min(RET_ROWS, t)
    cpb = rb // CHUNK
    intra, qdec, kdec, cdec = _ret_consts()
    sp = _ret_specs(rb)

    def body(proj_ref, cos_ref, sin_ref, gn_ref, intra_ref, qd_ref, kd_ref, cd_ref,
             gated_ref, outp_ref, st_ref, s_ref):
        @pl.when(pl.program_id(0) == 0)
        def _():
            s_ref[...] = jnp.zeros_like(s_ref)

        def chunk(c, carry):
            rows = pl.ds(pl.multiple_of(c * CHUNK, CHUNK), CHUNK)
            cs, sn = cos_ref[rows, :], sin_ref[rows, :]
            for h in range(RET_HEADS):
                q = proj_ref[rows, h * RET_DK:(h + 1) * RET_DK].astype(F32)
                k = proj_ref[rows, RET_QK_W + h * RET_DK:RET_QK_W + (h + 1) * RET_DK].astype(F32)
                v = proj_ref[rows, 2 * RET_QK_W + h * RET_DV:2 * RET_QK_W + (h + 1) * RET_DV]
                g = proj_ref[rows, 2 * RET_QK_W + RET_V_W + h * RET_DV:
                             2 * RET_QK_W + RET_V_W + (h + 1) * RET_DV].astype(F32)
                qr = _rope_half(q, cs, sn)
                kr = _rope_half(k, cs, sn) * (RET_DK ** -0.5)
                qb, kb, vb = qr.astype(BF16), kr.astype(BF16), v
                sc = _dot(qb, kb, 1, 1) * intra_ref[h]
                inner = _dot(sc.astype(BF16), vb, 1, 0)
                s_old = s_ref[h]
                sb = s_old.astype(BF16)
                st_ref[c, h] = sb
                cross = _dot((qr * qd_ref[h]).astype(BF16), sb, 1, 0)
                out = inner + cross
                s_ref[h] = s_old * cd_ref[h] + _dot((kr * kd_ref[h]).astype(BF16), vb, 0, 0)
                r = lax.rsqrt(jnp.mean(out * out, axis=-1, keepdims=True) + EPS)
                y = out * r * gn_ref[h]
                cols = slice(h * RET_DV, (h + 1) * RET_DV)
                gated_ref[rows, cols] = (g * _sigmoid(g) * y).astype(BF16)
                outp_ref[rows, cols] = out
            return carry

        lax.fori_loop(0, cpb, chunk, 0)

    return pl.pallas_call(
        body, name="ret_fwd", grid=(t // rb,),
        in_specs=[sp['proj'], sp['tab'], sp['tab'], sp['gn'], sp['intra'], sp['dec'], sp['dec'], sp['cdec']],
        out_specs=[sp['vw'], sp['vw'], sp['st']],
        out_shape=[_sds((t, RET_V_W), BF16), _sds((t, RET_V_W), F32),
                   _sds((t // CHUNK, RET_HEADS, RET_DK, RET_DV), BF16)],
        scratch_shapes=[pltpu.VMEM((RET_HEADS, RET_DK, RET_DV), F32)],
        compiler_params=_cparams(("arbitrary",)),
    )(proj, cos, sin, gn.reshape(RET_HEADS, 1, RET_DV), intra, qdec, kdec, cdec)


def _ret_bwd(proj, cos, sin, gn, outp, states, dgated, deps=()):
    t = proj.shape[0]
    rb = min(RET_ROWS, t)
    cpb = rb // CHUNK
    nb = t // rb
    intra, qdec, kdec, cdec = _ret_consts()
    sp = _ret_specs(rb, rev_nb=nb)

    def body(proj_ref, cos_ref, sin_ref, gn_ref, intra_ref, qd_ref, kd_ref, cd_ref, outp_ref, st_ref, dgt_ref, *rest):
        dproj_ref, dgn_ref, ds_ref = rest[len(deps):]
        @pl.when(pl.program_id(0) == 0)
        def _():
            ds_ref[...] = jnp.zeros_like(ds_ref)
            dgn_ref[...] = jnp.zeros_like(dgn_ref)

        def chunk(cc, carry):
            c = cpb - 1 - cc
            rows = pl.ds(pl.multiple_of(c * CHUNK, CHUNK), CHUNK)
            cs, sn = cos_ref[rows, :], sin_ref[rows, :]
            for h in range(RET_HEADS):
                q = proj_ref[rows, h * RET_DK:(h + 1) * RET_DK].astype(F32)
                k = proj_ref[rows, RET_QK_W + h * RET_DK:RET_QK_W + (h + 1) * RET_DK].astype(F32)
                v = proj_ref[rows, 2 * RET_QK_W + h * RET_DV:2 * RET_QK_W + (h + 1) * RET_DV]
                g = proj_ref[rows, 2 * RET_QK_W + RET_V_W + h * RET_DV:
                             2 * RET_QK_W + RET_V_W + (h + 1) * RET_DV].astype(F32)
                cols = slice(h * RET_DV, (h + 1) * RET_DV)
                qr = _rope_half(q, cs, sn)
                kr = _rope_half(k, cs, sn) * (RET_DK ** -0.5)
                qb, kb, vb = qr.astype(BF16), kr.astype(BF16), v
                qdb = (qr * qd_ref[h]).astype(BF16)
                kdb = (kr * kd_ref[h]).astype(BF16)
                out = outp_ref[rows, cols]
                dgt = dgt_ref[rows, cols]
                gnh = gn_ref[h]
                r = lax.rsqrt(jnp.mean(out * out, axis=-1, keepdims=True) + EPS)
                xh = out * r
                sg = _sigmoid(g)
                dgate = dgt * (xh * gnh) * (sg * (1.0 + g * (1.0 - sg)))
                dy = dgt * (g * sg)
                dgn_ref[h] += jnp.sum(dy * xh, axis=0, keepdims=True)
                dxh = dy * gnh
                dout = r * (dxh - xh * jnp.mean(dxh * xh, axis=-1, keepdims=True))
                doutb = dout.astype(BF16)
                itr = intra_ref[h]
                pb = (_dot(qb, kb, 1, 1) * itr).astype(BF16)
                dv = _dot(pb, doutb, 0, 0)
                dsc = (_dot(doutb, vb, 1, 1) * itr).astype(BF16)
                dq = _dot(dsc, kb, 1, 0)
                dk = _dot(dsc, qb, 0, 0)
                dq = dq + _dot(doutb, st_ref[c, h], 1, 1) * qd_ref[h]
                ds_new = ds_ref[h]
                dsb = ds_new.astype(BF16)
                dk = dk + _dot(vb, dsb, 1, 1) * kd_ref[h]
                dv = dv + _dot(kdb, dsb, 1, 0)
                ds_ref[h] = ds_new * cd_ref[h] + _dot(qdb, doutb, 0, 0)
                dproj_ref[rows, h * RET_DK:(h + 1) * RET_DK] = _rope_half(dq, cs, -sn).astype(BF16)
                dproj_ref[rows, RET_QK_W + h * RET_DK:RET_QK_W + (h + 1) * RET_DK] = (
                    _rope_half(dk * (RET_DK ** -0.5), cs, -sn).astype(BF16))
                dproj_ref[rows, 2 * RET_QK_W + h * RET_DV:2 * RET_QK_W + (h + 1) * RET_DV] = dv.astype(BF16)
                dproj_ref[rows, 2 * RET_QK_W + RET_V_W + h * RET_DV:
                          2 * RET_QK_W + RET_V_W + (h + 1) * RET_DV] = dgate.astype(BF16)
            return carry

        lax.fori_loop(0, cpb, chunk, 0)

    return pl.pallas_call(
        body, name="ret_bwd", grid=(nb,),
        in_specs=[sp['proj'], sp['tab'], sp['tab'], sp['gn'], sp['intra'], sp['dec'], sp['dec'], sp['cdec'],
                  sp['vw'], sp['st'], sp['vw']] + [ANY] * len(deps),
        out_specs=[sp['proj'], sp['gn']],
        out_shape=[_sds((t, RET_IN), BF16), _sds((RET_HEADS, 1, RET_DV), F32)],
        scratch_shapes=[pltpu.VMEM((RET_HEADS, RET_DK, RET_DV), F32)],
        compiler_params=_cparams(("arbitrary",)),
    )(proj, cos, sin, gn.reshape(RET_HEADS, 1, RET_DV), intra, qdec, kdec, cdec, outp, states, dgated, *deps)


def _mla_tables(t):
    half = MLA_ROPE // 2
    inv = 1.0 / (ROPE_THETA ** (jnp.arange(0, MLA_ROPE, 2, dtype=F32) / MLA_ROPE))
    ang = jnp.arange(t, dtype=F32)[:, None] * inv[None, :]
    cos, sin = jnp.cos(ang), jnp.sin(ang)
    z = jnp.zeros((t, half), F32)
    c = jnp.concatenate([cos, cos, z, z], axis=1)
    s1 = jnp.concatenate([-sin, z, z, z], axis=1)
    s2 = jnp.concatenate([z, sin, z, z], axis=1)
    return c, s1, s2


def _rope_tile(r, c, s1, s2):
    return r * c + pltpu.roll(r, 96, 1) * s1 + pltpu.roll(r, 32, 1) * s2


def _mla_mid(proj2, qa, kva):
    t = proj2.shape[0]
    tm = _row_tile(t)

    def body(p_ref, qa_ref, kva_ref, cq_ref, ckv_ref):
        cq = p_ref[:, :MLA_Q_RANK]
        ckv = p_ref[:, MLA_Q_RANK:MLA_Q_RANK + MLA_KV_RANK]
        rq = lax.rsqrt(jnp.mean(cq * cq, axis=-1, keepdims=True) + EPS)
        rkv = lax.rsqrt(jnp.mean(ckv * ckv, axis=-1, keepdims=True) + EPS)
        cq_ref[...] = (cq * rq * qa_ref[...]).astype(BF16)
        ckv_ref[...] = (ckv * rkv * kva_ref[...]).astype(BF16)

    return pl.pallas_call(
        body, name="mla_mid", grid=(t // tm,),
        in_specs=[pl.BlockSpec((tm, MLA_IN_PAD), lambda i: (i, 0)),
                  pl.BlockSpec((1, MLA_Q_RANK), lambda i: (0, 0)),
                  pl.BlockSpec((1, MLA_KV_RANK), lambda i: (0, 0))],
        out_specs=[pl.BlockSpec((tm, MLA_Q_RANK), lambda i: (i, 0)),
                   pl.BlockSpec((tm, MLA_KV_RANK), lambda i: (i, 0))],
        out_shape=[_sds((t, MLA_Q_RANK), BF16), _sds((t, MLA_KV_RANK), BF16)],
        compiler_params=_cparams(("parallel",)),
    )(proj2, qa, kva)


def _mla_mid_bwd(proj2, qa, kva, dcq, dckv, dkr):
    t = proj2.shape[0]
    tm = _row_tile(t)

    def body(p_ref, qa_ref, kva_ref, dcq_ref, dckv_ref, dkr_ref, dp_ref, dqa_ref, dkva_ref):
        @pl.when(pl.program_id(0) == 0)
        def _():
            dqa_ref[...] = jnp.zeros_like(dqa_ref)
            dkva_ref[...] = jnp.zeros_like(dkva_ref)

        dxq, dgq = _rms_bwd_rows(dcq_ref[...], p_ref[:, :MLA_Q_RANK], qa_ref[...], MLA_Q_RANK)
        dxk, dgk = _rms_bwd_rows(dckv_ref[...], p_ref[:, MLA_Q_RANK:MLA_Q_RANK + MLA_KV_RANK], kva_ref[...],
                                 MLA_KV_RANK)
        dp_ref[:, :MLA_Q_RANK] = dxq.astype(BF16)
        dp_ref[:, MLA_Q_RANK:MLA_Q_RANK + MLA_KV_RANK] = dxk.astype(BF16)
        dp_ref[:, MLA_Q_RANK + MLA_KV_RANK:] = dkr_ref[...].astype(BF16)
        dqa_ref[...] += jnp.sum(dgq, axis=0, keepdims=True)
        dkva_ref[...] += jnp.sum(dgk, axis=0, keepdims=True)

    return pl.pallas_call(
        body, name="mla_mid_bwd", grid=(t // tm,),
        in_specs=[pl.BlockSpec((tm, MLA_IN_PAD), lambda i: (i, 0)),
                  pl.BlockSpec((1, MLA_Q_RANK), lambda i: (0, 0)),
                  pl.BlockSpec((1, MLA_KV_RANK), lambda i: (0, 0)),
                  pl.BlockSpec((tm, MLA_Q_RANK), lambda i: (i, 0)),
                  pl.BlockSpec((tm, MLA_KV_RANK), lambda i: (i, 0)),
                  pl.BlockSpec((tm, 128), lambda i: (i, 0))],
        out_specs=[pl.BlockSpec((tm, MLA_IN_PAD), lambda i: (i, 0)),
                   pl.BlockSpec((1, MLA_Q_RANK), lambda i: (0, 0)),
                   pl.BlockSpec((1, MLA_KV_RANK), lambda i: (0, 0))],
        out_shape=[_sds((t, MLA_IN_PAD), BF16), _sds((1, MLA_Q_RANK), F32), _sds((1, MLA_KV_RANK), F32)],
        compiler_params=_cparams(("arbitrary",)),
    )(proj2, qa, kva, dcq, dckv, dkr)


def _mla_prep_specs(t, tm):
    head = lambda w: pl.BlockSpec((None, tm, w), lambda i, h: (h, i, 0))
    return dict(
        head256=head(MLA_HD_PAD), head128=head(MLA_VD),
        cols256=pl.BlockSpec((tm, MLA_HD_PAD), lambda i, h: (i, h)),
        cq=pl.BlockSpec((tm, MLA_Q_RANK), lambda i, h: (i, 0)),
        ckv=pl.BlockSpec((tm, MLA_KV_RANK), lambda i, h: (i, 0)),
        wuq=pl.BlockSpec((None, MLA_Q_RANK, MLA_HD_PAD), lambda i, h: (h, 0, 0)),
        wukv=pl.BlockSpec((None, MLA_KV_RANK, MLA_HD_PAD), lambda i, h: (h, 0, 0)),
        kr=pl.BlockSpec((tm, 128), lambda i, h: (i, (MLA_Q_RANK + MLA_KV_RANK) // 128)),
        gain=pl.BlockSpec((1, MLA_HD_PAD), lambda i, h: (0, 0)),
        tab=pl.BlockSpec((tm, 128), lambda i, h: (i, 0)),
    )


def _mla_prep(cq, ckv, wuq, wukv, proj2, gq, gk, tabs):
    t = cq.shape[0]
    tm = _row_tile(t)
    sp = _mla_prep_specs(t, tm)

    def body(cq_ref, ckv_ref, wuq_ref, wukv_ref, kr_ref, gq_ref, gk_ref, c_ref, s1_ref, s2_ref,
             qh_ref, kh_ref, vh_ref):
        c, s1, s2 = c_ref[...], s1_ref[...], s2_ref[...]

        def norm_rope(xv, gain):
            r = lax.rsqrt(jnp.sum(xv * xv, axis=-1, keepdims=True) / MLA_QKD + EPS)
            y = xv * r * gain
            return jnp.concatenate([y[:, :MLA_NOPE], _rope_tile(y[:, MLA_NOPE:], c, s1, s2)], axis=-1)

        kvv = _dot(ckv_ref[...], wukv_ref[...], 1, 0)
        qh_ref[...] = norm_rope(_dot(cq_ref[...], wuq_ref[...], 1, 0), gq_ref[...]).astype(BF16)
        kf = jnp.concatenate([kvv[:, :MLA_NOPE], kr_ref[...]], axis=-1)
        kh_ref[...] = norm_rope(kf, gk_ref[...]).astype(BF16)
        vh_ref[...] = jnp.concatenate([kvv[:, MLA_NOPE:], jnp.ones((tm, MLA_VD), F32)], axis=-1).astype(BF16)

    return pl.pallas_call(
        body, name="mla_prep", grid=(t // tm, MLA_HEADS),
        in_specs=[sp['cq'], sp['ckv'], sp['wuq'], sp['wukv'], sp['kr'], sp['gain'], sp['gain'],
                  sp['tab'], sp['tab'], sp['tab']],
        out_specs=[sp['head256'], sp['head256'], sp['head256']],
        out_shape=[_sds((MLA_HEADS, t, MLA_HD_PAD), BF16), _sds((MLA_HEADS, t, MLA_HD_PAD), BF16),
                   _sds((MLA_HEADS, t, 2 * MLA_VD), BF16)],
        compiler_params=_cparams(("parallel", "arbitrary")),
    )(cq, ckv, wuq, wukv, proj2, gq, gk, *tabs)


def _mla_prep_bwd(cq, ckv, wuq, wukv, proj2, gq, gk, tabs, dqt, dkh, dvh):
    t = cq.shape[0]
    tm = _row_tile(t)
    ab = dqt.shape[-1]
    sp = _mla_prep_specs(t, tm)

    def body(cq_ref, ckv_ref, wuq_ref, wukv_ref, kr_ref, gq_ref, gk_ref, c_ref, s1_ref, s2_ref,
             dqt_ref, dkh_ref, dvh_ref, dq_ref, dkv_ref, dkr_ref, dgq_ref, dgk_ref):
        dqh = jnp.concatenate([dqt_ref[b].T for b in range(tm // ab)], axis=0)
        i, h = pl.program_id(0), pl.program_id(1)

        @pl.when((i == 0) & (h == 0))
        def _():
            dgq_ref[...] = jnp.zeros_like(dgq_ref)
            dgk_ref[...] = jnp.zeros_like(dgk_ref)

        @pl.when(h == 0)
        def _():
            dkr_ref[...] = jnp.zeros_like(dkr_ref)

        c, s1, s2 = c_ref[...], s1_ref[...], s2_ref[...]

        def back(xv, gain, dout):
            dy = jnp.concatenate([dout[:, :MLA_NOPE], _rope_tile(dout[:, MLA_NOPE:], c, -s1, -s2)], axis=-1)
            return _rms_bwd_rows(dy, xv, gain, MLA_QKD)

        kvv = _dot(ckv_ref[...], wukv_ref[...], 1, 0)
        dxq, dgq = back(_dot(cq_ref[...], wuq_ref[...], 1, 0), gq_ref[...], dqh)
        kf = jnp.concatenate([kvv[:, :MLA_NOPE], kr_ref[...]], axis=-1)
        dxk, dgk = back(kf, gk_ref[...], dkh_ref[...])
        dq_ref[...] = dxq.astype(BF16)
        dkv_ref[...] = jnp.concatenate([dxk[:, :MLA_NOPE], dvh_ref[...]], axis=-1).astype(BF16)
        dkr_ref[...] += dxk[:, MLA_NOPE:]
        dgq_ref[...] += jnp.sum(dgq, axis=0, keepdims=True)
        dgk_ref[...] += jnp.sum(dgk, axis=0, keepdims=True)

    return pl.pallas_call(
        body, name="mla_prep_bwd", grid=(t // tm, MLA_HEADS),
        in_specs=[sp['cq'], sp['ckv'], sp['wuq'], sp['wukv'], sp['kr'], sp['gain'], sp['gain'],
                  sp['tab'], sp['tab'], sp['tab'],
                  pl.BlockSpec((None, tm // ab, MLA_HD_PAD, ab), lambda i, h: (h, i, 0, 0)),
                  sp['head256'], sp['head128']],
        out_specs=[sp['cols256'], sp['cols256'], sp['tab'], sp['gain'], sp['gain']],
        out_shape=[_sds((t, MLA_HEADS * MLA_HD_PAD), BF16), _sds((t, MLA_HEADS * MLA_HD_PAD), BF16),
                   _sds((t, 128), F32), _sds((1, MLA_HD_PAD), F32), _sds((1, MLA_HD_PAD), F32)],
        compiler_params=_cparams(("arbitrary", "arbitrary")),
    )(cq, ckv, wuq, wukv, proj2, gq, gk, *tabs, dqt, dkh, dvh)


def _chunk_visible(rows, cols, row_off, col_off):
    rq = lax.shift_right_logical(lax.broadcasted_iota(jnp.int32, (rows, cols), 0) + row_off, 6)
    ck = lax.shift_right_logical(lax.broadcasted_iota(jnp.int32, (rows, cols), 1) + col_off, 6)
    return ck <= rq


def _rows_to_lanes(col):
    return col.T[:8, :]


def _attn_fwd(qh, kh, vh):
    t = qh.shape[1]
    ab = min(ATT_BLOCK, t)
    tq = min(ATT_QROWS, t)
    r = tq // ab
    hg = ATT_HEADS

    def body(q_ref, k_ref, v_ref, o_ref, lse_ref, acc_ref):
        n_un = pl.program_id(1) * r
        acc_ref[...] = jnp.zeros_like(acc_ref)

        def step(b, ms, diag):
            rows = pl.ds(pl.multiple_of(b * ab, ab), ab)
            out = []
            for hh in range(hg):
                m = ms[hh]
                s = _dot(q_ref[hh], k_ref[hh, rows, :], 1, 1)
                if diag is not None:
                    s = jnp.where(_chunk_visible(tq, ab, 0, diag * ab), s, -1e30)
                m_new = jnp.maximum(m, jnp.max(s, axis=-1, keepdims=True))
                p = jnp.exp2((s - m_new) * ATT_EXP2).astype(BF16)
                acc_ref[hh] = jnp.exp2((m - m_new) * ATT_EXP2) * acc_ref[hh] + _dot(p, v_ref[hh, rows, :], 1, 0)
                out.append(m_new)
            return tuple(out)

        ms = tuple(jnp.full((tq, 1), -1e30, F32) for _ in range(hg))
        ms = lax.fori_loop(0, n_un, lambda b, st: step(b, st, None), ms)
        for d in range(r):
            ms = step(n_un + d, ms, d)
        for hh in range(hg):
            l = acc_ref[hh, :, MLA_VD:]
            o_ref[:, hh * MLA_VD:(hh + 1) * MLA_VD] = acc_ref[hh, :, :MLA_VD] / l
            lse_t = _rows_to_lanes(ms[hh] * ATT_EXP2 + jnp.log(l) * LOG2E)
            for d in range(r):
                lse_ref[hh, d] = lse_t[:, d * ab:(d + 1) * ab]

    return pl.pallas_call(
        body, name="mla_attn", grid=(MLA_HEADS // hg, t // tq),
        in_specs=[pl.BlockSpec((hg, tq, MLA_HD_PAD), lambda g, i: (g, i, 0)),
                  pl.BlockSpec((hg, t, MLA_HD_PAD), lambda g, i: (g, 0, 0)),
                  pl.BlockSpec((hg, t, 2 * MLA_VD), lambda g, i: (g, 0, 0))],
        out_specs=[pl.BlockSpec((tq, hg * MLA_VD), lambda g, i: (i, g)),
                   pl.BlockSpec((hg, r, 8, ab), lambda g, i: (g, i, 0, 0))],
        out_shape=[_sds((t, MLA_HEADS * MLA_VD), F32), _sds((MLA_HEADS, t // ab, 8, ab), F32)],
        scratch_shapes=[pltpu.VMEM((hg, tq, 2 * MLA_VD), F32)],
        compiler_params=_cparams(("parallel", "arbitrary")),
    )(qh, kh, vh)


def _attn_delta(do, o, ab):
    t = do.shape[0]
    tm = _row_tile(t)

    def body(do_ref, o_ref, d_ref):
        d = jnp.sum(do_ref[...] * o_ref[...], axis=-1, keepdims=True)
        d_t = _rows_to_lanes(jnp.broadcast_to(d, (tm, 128)))
        for b in range(tm // ab):
            d_ref[b] = d_t[:, b * ab:(b + 1) * ab]

    col = pl.BlockSpec((tm, MLA_VD), lambda i, h: (i, h))
    return pl.pallas_call(
        body, name="mla_delta", grid=(t // tm, MLA_HEADS), in_specs=[col, col],
        out_specs=pl.BlockSpec((None, tm // ab, 8, ab), lambda i, h: (h, i, 0, 0)),
        out_shape=_sds((MLA_HEADS, t // ab, 8, ab), F32),
        compiler_params=_cparams(("parallel", "parallel")),
    )(do, o)


def _attn_bwd(qh, kh, vh, dob, lse_t, dl_t):
    t = qh.shape[1]
    ab = min(ATT_BLOCK, t)
    kb = min(ATT_KROWS, t)
    r = kb // ab
    nq = t // ab
    hg = ATT_HEADS

    def body(q_ref, k_ref, v_ref, do_ref, lse_ref, dl_ref, dqt_ref, dk_ref, dv_ref):
        j = pl.program_id(1)

        @pl.when(j == 0)
        def _():
            dqt_ref[...] = jnp.zeros_like(dqt_ref)

        ks = [k_ref[hh] for hh in range(hg)]
        vs = [v_ref[hh, :, :MLA_VD] for hh in range(hg)]
        kts = [k.T for k in ks]

        dk_ref[...] = jnp.zeros_like(dk_ref)
        dv_ref[...] = jnp.zeros_like(dv_ref)

        def step(b, carry, diag):
            rows = pl.ds(pl.multiple_of(b * ab, ab), ab)
            hi = kb if diag is None else (diag + 1) * ab
            for hh in range(hg):
                q = q_ref[hh, rows, :]
                do = do_ref[rows, hh * MLA_VD:(hh + 1) * MLA_VD]
                s_t = _dot(ks[hh][:hi], q, 1, 1)
                if diag is not None:
                    key_chunk = lax.shift_right_logical(lax.broadcasted_iota(jnp.int32, (hi, ab), 0), 6)
                    query_chunk = lax.shift_right_logical(
                        lax.broadcasted_iota(jnp.int32, (hi, ab), 1) + diag * ab, 6)
                    s_t = jnp.where(key_chunk <= query_chunk, s_t, -1e30)
                p_t = jnp.exp2(s_t * ATT_EXP2 - lse_ref[hh, b][0:1, :])
                dp_t = _dot(vs[hh][:hi], do, 1, 1)
                ds_t = (p_t * (dp_t - dl_ref[hh, b][0:1, :]) * ATT_SCALE).astype(BF16)
                dqt_ref[hh, b] += _dot(kts[hh][:, :hi], ds_t, 1, 0)
                dk_ref[hh, :hi] += _dot(ds_t, q, 1, 0)
                dv_ref[hh, :hi] += _dot(p_t.astype(BF16), do, 1, 0)
            return carry

        for d in range(r):
            step(j * r + d, 0, d)
        lax.fori_loop((j + 1) * r, nq, lambda b, c: step(b, c, None), 0)

    whole = lambda w: pl.BlockSpec((hg, t, w), lambda g, j: (g, 0, 0))
    blk = lambda w: pl.BlockSpec((hg, kb, w), lambda g, j: (g, j, 0))
    stat = pl.BlockSpec((hg, nq, 8, ab), lambda g, j: (g, 0, 0, 0))
    return pl.pallas_call(
        body, name="mla_attn_bwd", grid=(MLA_HEADS // hg, t // kb),
        in_specs=[whole(MLA_HD_PAD), blk(MLA_HD_PAD), blk(2 * MLA_VD),
                  pl.BlockSpec((t, hg * MLA_VD), lambda g, j: (0, g)), stat, stat],
        out_specs=[pl.BlockSpec((hg, nq, MLA_HD_PAD, ab), lambda g, j: (g, 0, 0, 0)), blk(MLA_HD_PAD), blk(MLA_VD)],
        out_shape=[_sds((MLA_HEADS, nq, MLA_HD_PAD, ab), F32), _sds((MLA_HEADS, t, MLA_HD_PAD), F32),
                   _sds((MLA_HEADS, t, MLA_VD), F32)],
        compiler_params=_cparams(("parallel", "arbitrary")),
    )(qh, kh, vh, dob, lse_t, dl_t)


VEC = pl.BlockSpec((1, D_MODEL), lambda i, j, k: (0, 0))


def _rows(tm, width):
    return pl.BlockSpec((tm, width), lambda i, j, k: (i, 0))


def _residual_epi(next_gain):
    if next_gain is None:
        return [], lambda acc, hv: (acc + hv,)

    def epi(acc, hv, g):
        h_new = acc + hv
        r = lax.rsqrt(jnp.mean(h_new * h_new, axis=-1, keepdims=True) + EPS)
        return h_new, h_new * r * g

    return [(next_gain, VEC)], epi


def _residual_outs(t, row, next_gain):
    outs = [(_sds((t, D_MODEL), F32), row)]
    return outs + ([(_sds((t, D_MODEL), BF16), row)] if next_gain is not None else [])


def _mlp_fwd(l, h, hn, w1g, fetch_w2, next_gain):
    t = h.shape[0]
    tm = _row_tile(t, 512)

    def relu2(acc):
        r = jnp.maximum(acc, 0.0)
        return (r * r,)

    (u,) = _mm_rows(f"mlp_up{l}", tm, hn, w1g, 'nn_cols', [(_sds((t, D_FF), BF16), _rows(tm, D_FF))], epi=relu2)
    w2g = fetch_w2((u,))
    row = _rows(tm, D_MODEL)
    more, epi = _residual_epi(next_gain)
    h2, hn_next = _mm_rows(f"mlp_down{l}", tm, u, w2g, 'nn_rows', _residual_outs(t, row, next_gain),
                           extras=[(h, row)] + more, epi=epi)
    return h2, hn_next, (h, hn, u, w1g, w2g)


def _norm_bwd_outs(t, tm):
    return [(_sds((t, D_MODEL), F32), pl.BlockSpec((tm, D_MODEL), lambda i, j, k: (i, 0))),
            (_sds((t // tm, 1, D_MODEL), F32), pl.BlockSpec((None, 1, D_MODEL), lambda i, j, k: (i, 0, 0)))]


def _norm_bwd_epi(acc, xv, res, g):
    dx, dgr = _rms_bwd_rows(acc, xv, g, D_MODEL)
    return res + dx, jnp.sum(dgr, axis=0, keepdims=True)


def _mlp_bwd(l, dh, saved, norm_g):
    h, hn, u, w1g, w2g = saved
    t = h.shape[0]
    tm = _row_tile(t, 512)
    nsh, _, wsh = w1g.shape
    wide = _rows(tm, D_FF)
    (da,) = _mm_rows(f"mlp_du{l}", tm, dh, w2g, 'nt_rows', [(_sds((t, D_FF), BF16), wide)], extras=[(u, wide)],
                     epi=lambda acc, uv: (2.0 * jnp.sqrt(uv.astype(F32)) * acc,))
    tw = _row_tile(t, 512)
    (dw2,) = _mm(f"mlp_dw2{l}", (1, 1, t // tw),
                 u, pl.BlockSpec((tw, D_FF), lambda i, j, k: (k, 0)),
                 dh, pl.BlockSpec((tw, D_MODEL), lambda i, j, k: (k, 0)), (0, 0),
                 [(_sds((D_FF, D_MODEL), BF16), pl.BlockSpec((D_FF, D_MODEL), lambda i, j, k: (0, 0)))])
    dw2 = dw2.reshape(nsh, wsh, D_MODEL)
    (dw1,) = _mm(f"mlp_dw1{l}", (1, 1, t // tw),
                 hn, pl.BlockSpec((tw, D_MODEL), lambda i, j, k: (k, 0)),
                 da, pl.BlockSpec((tw, D_FF), lambda i, j, k: (k, 0)), (0, 0),
                 [(_sds((nsh, D_MODEL, wsh), BF16), pl.BlockSpec((nsh, D_MODEL, wsh), lambda i, j, k: (0, 0, 0)))],
                 split=wsh)
    row = _rows(tm, D_MODEL)
    dh_in, dg = _mm_rows(f"mlp_dhn{l}", tm, da, w1g, 'nt_cols', _norm_bwd_outs(t, tm),
                         extras=[(h, row), (dh, row), (norm_g, VEC)], epi=_norm_bwd_epi)
    return dh_in, jnp.sum(dg, axis=0), dw1, dw2


def _ple_fwd(l, h, hn, p, wg, wp, next_gain, target=None):
    t = h.shape[0]
    tm = _row_tile(t, 512)
    row = pl.BlockSpec((tm, D_MODEL), lambda i, j, k: (i, 0))
    full = lambda r: pl.BlockSpec((r, D_MODEL), lambda i, j, k: (0, 0))
    f32_row, bf_row = (_sds((t, D_MODEL), F32), row), (_sds((t, D_MODEL), BF16), row)
    common = [(h, row), (p, pl.BlockSpec((None, None, tm, PLE_DIM), lambda i, j, k: (l, 0, i, 0))),
              (wp, full(PLE_DIM))]
    if target is not None:
        def loss_epi(acc, hv, pv, wpv, tv):
            gt = _sigmoid(acc)
            ev = _dot(_bf(pv), wpv, 1, 0)
            err = hv + gt * ev - tv
            sq = jnp.sum(jnp.sum(err * err, axis=-1, keepdims=True), axis=0, keepdims=True)
            return err / D_MODEL, gt, ev, jnp.broadcast_to(sq, (8, 128))

        dy, gate, e, sq = _mm(f"ple_gate{l}", (t // tm, 1, 1), hn, row, wg, full(D_MODEL), (1, 0),
                              [f32_row, bf_row, bf_row, (_sds((t // tm, 8, 128), F32),
                                                         pl.BlockSpec((None, 8, 128), lambda i, j, k: (i, 0, 0)))],
                              extras=common + [(target, row)], epi=loss_epi)
        return dy, jnp.sum(sq, axis=0), (h, hn, gate, e)

    def gate_epi(acc, hv, pv, wpv, *gain):
        gt = _sigmoid(acc)
        ev = _dot(_bf(pv), wpv, 1, 0)
        h_new = hv + gt * ev
        if not gain:
            return h_new, gt, ev
        r = lax.rsqrt(jnp.mean(h_new * h_new, axis=-1, keepdims=True) + EPS)
        return h_new, gt, ev, h_new * r * gain[0]

    res = _mm(f"ple_gate{l}", (t // tm, 1, 1), hn, row, wg, full(D_MODEL), (1, 0),
              [f32_row, bf_row, bf_row] + ([bf_row] if next_gain is not None else []),
              extras=common + ([(next_gain, VEC)] if next_gain is not None else []), epi=gate_epi)
    h_out, gate, e = res[0], res[1], res[2]
    return h_out, (res[3] if next_gain is not None else None), (h, hn, gate, e)


def _ple_bwd(l, dh, saved, p, norm_g, wg, deps=()):
    h, hn, gate, e = saved
    t = h.shape[0]
    tm = _row_tile(t)
    tk = _row_tile(t, 512)
    de, dz = _ple_gate_bwd(f"ple_gate_bwd{l}", dh, gate, e)
    full = lambda r: pl.BlockSpec((r, D_MODEL), lambda i, j, k: (0, 0))
    rowk = pl.BlockSpec((tk, D_MODEL), lambda i, j, k: (k, 0))
    (dwp,) = _mm(f"ple_dwp{l}", (1, 1, t // tk),
                 p, pl.BlockSpec((None, None, tk, PLE_DIM), lambda i, j, k: (l, 0, k, 0)),
                 de, rowk, (0, 0), [(_sds((PLE_DIM, D_MODEL), BF16), full(PLE_DIM))], deps=deps)
    (dwg,) = _mm(f"ple_dwg{l}", (1, 1, t // tk), hn, rowk, dz, rowk, (0, 0),
                 [(_sds((D_MODEL, D_MODEL), BF16), full(D_MODEL))])
    row = pl.BlockSpec((tm, D_MODEL), lambda i, j, k: (i, 0))
    dh_in, dg = _mm(f"ple_dhn{l}", (t // tm, 1, 1), dz, row, wg, full(D_MODEL), (1, 1),
                    _norm_bwd_outs(t, tm), extras=[(h, row), (dh, row), (norm_g, VEC)], epi=_norm_bwd_epi)
    return dh_in, jnp.sum(dg, axis=0), dwg, dwp


def _ret_layer_fwd(x, norm_g, wri, fetch_wro, gn, cos, sin, next_gain, hn=None, deps=()):
    t = x.shape[0]
    tm = _row_tile(t)
    nsh, _, wsh = wri.shape
    if hn is None:
        hn = _rms_fwd("mix_norm0", x, norm_g)
    tp = _row_tile(t, 512)
    (proj,) = _mm_rows("ret_in", tp, hn, wri, 'nn_cols', [(_sds((t, RET_IN), BF16), _rows(tp, RET_IN))], deps=deps)
    gated, outp, states = _ret_fwd(proj, cos, sin, gn)
    wro = fetch_wro((gated,))
    row = pl.BlockSpec((tm, D_MODEL), lambda i, j, k: (i, 0))
    kt = 512
    more, epi = _residual_epi(next_gain)
    h1, hn_next = _mm("ret_out", (t // tm, 1, RET_V_W // kt),
                      gated, pl.BlockSpec((tm, kt), lambda i, j, k: (i, k)),
                      wro, pl.BlockSpec((kt, D_MODEL), lambda i, j, k: (k, 0)), (1, 0),
                      _residual_outs(t, row, next_gain), extras=[(x, row)] + more, epi=epi)
    return h1, hn_next, (x, hn, proj, gated, outp, states, wro)


def _ret_layer_bwd(dh, saved, norm_g, wri, gn, cos, sin, emit_out, emit_in, deps=()):
    x, hn, proj, gated, outp, states, wro = saved
    t = x.shape[0]
    tm = _row_tile(t)
    tk = _row_tile(t, 512)
    nsh, _, wsh = wri.shape
    (dgated,) = _mm("ret_dgated", (t // tm, RET_V_W // D_MODEL, 1),
                    dh, pl.BlockSpec((tm, D_MODEL), lambda i, j, k: (i, 0)),
                    wro, pl.BlockSpec((D_MODEL, D_MODEL), lambda i, j, k: (j, 0)), (1, 1),
                    [(_sds((t, RET_V_W), F32), pl.BlockSpec((tm, D_MODEL), lambda i, j, k: (i, j)))], deps=deps)
    (dwro,) = _mm("ret_dwro", (1, 1, t // tk),
                  gated, pl.BlockSpec((tk, RET_V_W), lambda i, j, k: (k, 0)),
                  dh, pl.BlockSpec((tk, D_MODEL), lambda i, j, k: (k, 0)), (0, 0),
                  [(_sds((RET_V_W, D_MODEL), BF16), pl.BlockSpec((RET_V_W, D_MODEL), lambda i, j, k: (0, 0)))])
    dproj, dgn = _ret_bwd(proj, cos, sin, gn, outp, states, dgated, deps=emit_out(dwro))
    half = nsh // 2
    (dwri,) = _mm("ret_dwri", (2, 1, t // tk),
                  hn, pl.BlockSpec((tk, D_MODEL), lambda i, j, k: (k, 0)),
                  dproj, pl.BlockSpec((tk, half * wsh), lambda i, j, k: (k, i)), (0, 0),
                  [(_sds((nsh, D_MODEL, wsh), BF16), pl.BlockSpec((half, D_MODEL, wsh), lambda i, j, k: (i, 0, 0)))],
                  split=wsh)
    deps = emit_in(dwri)
    td = _row_tile(t, 256)
    row = _rows(td, D_MODEL)
    dx, dg = _mm_rows("ret_dhn", td, dproj, wri, 'nt_cols', _norm_bwd_outs(t, td),
                      extras=[(x, row), (dh, row), (norm_g, VEC)], epi=_norm_bwd_epi, deps=deps)
    return dx, jnp.sum(dg, axis=0), dgn.reshape(RET_HEADS, RET_DV)


def _mla_layer_fwd(h, hn, wmi, qa, kva, wuq, wukv, gq, gk, wmo, tabs, next_gain):
    t = h.shape[0]
    tm = _row_tile(t)
    row = pl.BlockSpec((tm, D_MODEL), lambda i, j, k: (i, 0))
    (proj2,) = _mm("mla_in", (t // tm, 1, 1), hn, row,
                   wmi, pl.BlockSpec((D_MODEL, MLA_IN_PAD), lambda i, j, k: (0, 0)), (1, 0),
                   [(_sds((t, MLA_IN_PAD), F32), pl.BlockSpec((tm, MLA_IN_PAD), lambda i, j, k: (i, 0)))])
    cq, ckv = _mla_mid(proj2, qa, kva)
    qh, kh, vh = _mla_prep(cq, ckv, wuq, wukv, proj2, gq, gk, tabs)
    o, lse = _attn_fwd(qh, kh, vh)
    more, epi = _residual_epi(next_gain)
    h_out, hn_next = _mm("mla_out", (t // tm, 1, 1), o, row,
                         wmo, pl.BlockSpec((D_MODEL, D_MODEL), lambda i, j, k: (0, 0)), (1, 0),
                         _residual_outs(t, row, next_gain), extras=[(h, row)] + more, epi=epi)
    return h_out, hn_next, (h, hn, proj2, cq, ckv, qh, kh, vh, o, lse)


def _mla_layer_bwd(dh, saved, norm_g, wmi, qa, kva, wuq, wukv, gq, gk, wmo, tabs, deps=()):
    h, hn, proj2, cq, ckv, qh, kh, vh, o, lse = saved
    t = h.shape[0]
    tm = _row_tile(t)
    tk = _row_tile(t, 512)
    row = pl.BlockSpec((tm, D_MODEL), lambda i, j, k: (i, 0))
    rowk = pl.BlockSpec((tk, D_MODEL), lambda i, j, k: (k, 0))
    sq = pl.BlockSpec((D_MODEL, D_MODEL), lambda i, j, k: (0, 0))
    do, dob = _mm("mla_do", (t // tm, 1, 1), dh, row, wmo, sq, (1, 1),
                  [(_sds((t, D_MODEL), F32), row), (_sds((t, D_MODEL), BF16), row)], epi=lambda acc: (acc, acc),
                  deps=deps)
    (dwmo,) = _mm("mla_dwo", (1, 1, t // tk), o, rowk, dh, rowk, (0, 0), [(_sds((D_MODEL, D_MODEL), BF16), sq)])
    delta = _attn_delta(do, o, lse.shape[-1])
    dqt, dkh, dvh = _attn_bwd(qh, kh, vh, dob, lse, delta)
    dq, dkv, dkr, dgq, dgk = _mla_prep_bwd(cq, ckv, wuq, wukv, proj2, gq, gk, tabs, dqt, dkh, dvh)

    wide = MLA_HEADS * MLA_HD_PAD
    widek = pl.BlockSpec((tk, wide), lambda i, j, k: (k, 0))
    (dwuq,) = _mm("mla_dwuq", (1, 1, t // tk),
                  cq, pl.BlockSpec((tk, MLA_Q_RANK), lambda i, j, k: (k, 0)), dq, widek, (0, 0),
                  [(_sds((MLA_HEADS, MLA_Q_RANK, MLA_HD_PAD), BF16),
                    pl.BlockSpec((MLA_HEADS, MLA_Q_RANK, MLA_HD_PAD), lambda i, j, k: (0, 0, 0)))], split=MLA_HD_PAD)
    (dwukv,) = _mm("mla_dwukv", (1, 1, t // tk),
                   ckv, pl.BlockSpec((tk, MLA_KV_RANK), lambda i, j, k: (k, 0)), dkv, widek, (0, 0),
                   [(_sds((MLA_HEADS, MLA_KV_RANK, MLA_HD_PAD), BF16),
                     pl.BlockSpec((MLA_HEADS, MLA_KV_RANK, MLA_HD_PAD), lambda i, j, k: (0, 0, 0)))],
                   split=MLA_HD_PAD)
    side_by_side = lambda wg: wg.transpose(1, 0, 2).reshape(wg.shape[1], wide)
    widei = pl.BlockSpec((tm, wide), lambda i, j, k: (i, 0))
    (dcq,) = _mm("mla_dcq", (t // tm, 1, 1), dq, widei,
                 side_by_side(wuq), pl.BlockSpec((MLA_Q_RANK, wide), lambda i, j, k: (0, 0)), (1, 1),
                 [(_sds((t, MLA_Q_RANK), F32), pl.BlockSpec((tm, MLA_Q_RANK), lambda i, j, k: (i, 0)))])
    (dckv,) = _mm("mla_dckv", (t // tm, 1, 1), dkv, widei,
                  side_by_side(wukv), pl.BlockSpec((MLA_KV_RANK, wide), lambda i, j, k: (0, 0)), (1, 1),
                  [(_sds((t, MLA_KV_RANK), F32), pl.BlockSpec((tm, MLA_KV_RANK), lambda i, j, k: (i, 0)))])
    dproj2, dqa, dkva = _mla_mid_bwd(proj2, qa, kva, dcq, dckv, dkr)
    win = pl.BlockSpec((D_MODEL, MLA_IN_PAD), lambda i, j, k: (0, 0))
    (dwmi,) = _mm("mla_dwin", (1, 1, t // tk), hn, rowk,
                  dproj2, pl.BlockSpec((tk, MLA_IN_PAD), lambda i, j, k: (k, 0)), (0, 0),
                  [(_sds((D_MODEL, MLA_IN_PAD), BF16), win)])
    dh_in, dg = _mm("mla_dhn", (t // tm, 1, 1),
                    dproj2, pl.BlockSpec((tm, MLA_IN_PAD), lambda i, j, k: (i, 0)), wmi, win, (1, 1),
                    _norm_bwd_outs(t, tm), extras=[(h, row), (dh, row), (norm_g, VEC)], epi=_norm_bwd_epi)
    return dh_in, dict(mix=jnp.sum(dg, axis=0), wmi=dwmi, qa=dqa, kva=dkva, wuq=dwuq, wukv=dwukv, gq=dgq, gk=dgk,
                       wmo=dwmo)


def _local_step(x, p, target, w, fetch, emit=lambda group: ()):
    t = x.shape[0]
    inv = 1.0 / (ROPE_THETA ** (jnp.arange(0, RET_DK, 2, dtype=F32) / RET_DK))
    ang = jnp.arange(t, dtype=F32)[:, None] * inv[None, :]
    cos_r, sin_r = jnp.cos(ang), jnp.sin(ang)
    tabs = _mla_tables(t)
    row = lambda a, i: a[i:i + 1]

    h1, hn1, s_ret = _ret_layer_fwd(x, row(w['mix_norm'], 0), w['ret_w_in'],
                                    lambda after: fetch('ret_out', after)['ret_w_out'], w['ret_gn'], cos_r, sin_r,
                                    row(w['mlp_norm'], 0), hn=w.get('hn0'), deps=w['deps'])
    h2, hn2, s_mlp0 = _mlp_fwd(0, h1, hn1, fetch('mlp_w1_0', (h1,))['mlp_w1'],
                               lambda after: fetch('mlp_w2_0', after)['mlp_w2'], row(w['ple_norm'], 0))
    w0 = fetch('ple_0', (h2,))
    h3, hn3, s_ple0 = _ple_fwd(0, h2, hn2, p, w0['ple_gate_w'], w0['ple_proj_w'], row(w['mix_norm'], 1))
    wm = fetch('mla', (h3,))
    mla_w = (wm['mla_w_in'], w['mla_q_a_norm'], w['mla_kv_a_norm'], wm['mla_w_uq'], wm['mla_w_ukv'],
             w['mla_q_norm'], w['mla_k_norm'], wm['mla_w_out'], tabs)
    h4, hn4, s_mla = _mla_layer_fwd(h3, hn3, *mla_w, row(w['mlp_norm'], 1))
    w1 = fetch('layer_1', (h4,))
    h5, hn5, s_mlp1 = _mlp_fwd(1, h4, hn4, w1['mlp_w1'], lambda after: w1['mlp_w2'], row(w['ple_norm'], 1))
    dy, sq_err, s_ple1 = _ple_fwd(1, h5, hn5, p, w1['ple_gate_w'], w1['ple_proj_w'], None, target)

    n = N_DEV
    colsh = lambda a: a.reshape(a.shape[0], n, a.shape[1] // n).transpose(1, 0, 2)
    rowsh = lambda a: a.reshape(n, a.shape[0] // n, a.shape[1])
    big = {}

    def emit_group(group):
        big.update(group)
        return emit(group)

    dh5, dg_ple1, dwg1, dwp1 = _ple_bwd(1, dy, s_ple1, p, row(w['ple_norm'], 1), w1['ple_gate_w'])
    dh4, dg_mlp1, dw1_1, dw2_1 = _mlp_bwd(1, dh5, s_mlp1, row(w['mlp_norm'], 1))
    deps = emit_group({('ple_gate_w', 1): rowsh(dwg1), ('ple_proj_w', 1): colsh(dwp1),
                       ('mlp_w2', 1): dw2_1, ('mlp_w1', 1): dw1_1})
    dh3, gm = _mla_layer_bwd(dh4, s_mla, row(w['mix_norm'], 1), *mla_w, deps=deps)
    deps = emit_group({('mla_w_out', 0): rowsh(gm['wmo']), ('mla_w_uq', 0): gm['wuq'][:, :, :MLA_QKD],
                       ('mla_w_ukv', 0): gm['wukv'], ('mla_w_in', 0): rowsh(gm['wmi'][:, :MLA_IN])})
    dh2, dg_ple0, dwg0, dwp0 = _ple_bwd(0, dh3, s_ple0, p, row(w['ple_norm'], 0), w0['ple_gate_w'], deps=deps)
    dh1, dg_mlp0, dw1_0, dw2_0 = _mlp_bwd(0, dh2, s_mlp0, row(w['mlp_norm'], 0))
    deps = emit_group({('ple_gate_w', 0): rowsh(dwg0), ('ple_proj_w', 0): colsh(dwp0),
                       ('mlp_w2', 0): dw2_0, ('mlp_w1', 0): dw1_0})
    dx, dg_mix0, dgn = _ret_layer_bwd(
        dh1, s_ret, row(w['mix_norm'], 0), w['ret_w_in'], w['ret_gn'], cos_r, sin_r,
        lambda dwro: emit_group({('ret_w_out', 0): rowsh(dwro)}),
        lambda dwri: emit_group({('ret_w_in', 0): dwri}), deps=deps)

    small = dict(
        mix_norm=[dg_mix0, gm['mix']], mlp_norm=[dg_mlp0, dg_mlp1], ple_norm=[dg_ple0, dg_ple1],
        ret_gn=dgn, mla_q_a_norm=gm['qa'], mla_kv_a_norm=gm['kva'], mla_q_norm=gm['gq'], mla_k_norm=gm['gk'],
    )
    return sq_err, dx, big, small


def _my_place():
    x, y, c = lax.axis_index("x"), lax.axis_index("y"), lax.axis_index("c")
    return x, y, c


def _flat(px, py, pc):
    return 4 * px + 2 * py + pc


def _peer(x, y, c, r):
    return (1 - x if r & 4 else x, 1 - y if r & 2 else y, 1 - c if r & 1 else c)


def _all_gather(arrays):
    n = len(arrays)

    def body(*refs):
        ins, outs = refs[:n], refs[n:2 * n]
        send_sems, recv_sems, local_sems = refs[2 * n:]
        x, y, c = _my_place()
        me, sibling = (x, y, c), (x, y, 1 - c)
        chips = [(1 - x, y), (x, 1 - y), (1 - x, 1 - y)]

        def copy(a, k, block, to, src=None):
            slot = outs[a].at[_flat(*block)]
            return pltpu.make_async_remote_copy(
                src_ref=slot if src is None else src, dst_ref=slot,
                send_sem=send_sems.at[a, k], recv_sem=recv_sems.at[a, k], device_id=to, device_id_type=MESH)

        mine = [pltpu.make_async_copy(ins[a], outs[a].at[_flat(*me)], local_sems.at[a]) for a in range(n)]
        for cp in mine:
            cp.start()
        first = []
        for a in range(n):
            first.append(copy(a, 0, me, sibling, src=ins[a]))
            first += [copy(a, 1 + j, me, (*chip, c), src=ins[a]) for j, chip in enumerate(chips)]
        for cp in first:
            cp.start()
        passed = []
        for a in range(n):
            for j, chip in enumerate(chips):
                copy(a, 1 + j, (*chip, c), me).wait_recv()
                passed.append(copy(a, 4 + j, (*chip, c), sibling))
                passed[-1].start()
        for a in range(n):
            copy(a, 0, sibling, me).wait_recv()
            for j, chip in enumerate(chips):
                copy(a, 4 + j, (*chip, 1 - c), me).wait_recv()
        for cp in first + passed:
            cp.wait_send()
        for cp in mine:
            cp.wait()

    return pl.pallas_call(
        body, name="all_gather_weights",
        in_specs=[ANY] * n, out_specs=[ANY] * n,
        out_shape=[_sds((N_DEV,) + a.shape, a.dtype) for a in arrays],
        scratch_shapes=[pltpu.SemaphoreType.DMA((n, 7)), pltpu.SemaphoreType.DMA((n, 7)),
                        pltpu.SemaphoreType.DMA((n,))],
    )(*arrays)


HBM = pl.BlockSpec(memory_space=pltpu.HBM)
SEMS = pl.BlockSpec(memory_space=pltpu.SEMAPHORE)
SIDE_EFFECT = pltpu.SideEffectType.DATAFLOW_SIDE_EFFECTING


def _rs_copies(x, y, c, srcs, lands, send_sems, recv_sems):
    copies = []
    for a in range(len(srcs)):
        for r in range(1, N_DEV):
            peer = _peer(x, y, c, r)
            k = a * (N_DEV - 1) + r - 1
            copies.append(pltpu.make_async_remote_copy(
                src_ref=srcs[a].at[_flat(*peer)], dst_ref=lands[a].at[r - 1],
                send_sem=send_sems.at[k], recv_sem=recv_sems.at[k], device_id=peer, device_id_type=MESH))
    return copies


def _rs_start(name, arrays):
    n = len(arrays)
    hbm = lambda a: pltpu.with_memory_space_constraint(a, pltpu.HBM)
    lands = [hbm(lax.empty((N_DEV - 1,) + a.shape[1:], a.dtype)) for a in arrays]

    def body(*refs):
        srcs, lnd = refs[:n], refs[n:2 * n]
        send_sems, recv_sems = refs[2 * n], refs[2 * n + 1]
        token = refs[-1]
        for cp in _rs_copies(*_my_place(), srcs, lnd, send_sems, recv_sems):
            cp.start()
        token[...] = jnp.zeros_like(token)

    outs = pl.pallas_call(
        body, name=name,
        in_specs=[HBM] * (2 * n),
        out_specs=[SEMS, SEMS] + [HBM] * (2 * n) + [pl.BlockSpec(memory_space=pltpu.VMEM)],
        out_shape=[pltpu.SemaphoreType.DMA((n * (N_DEV - 1),)), pltpu.SemaphoreType.DMA((n * (N_DEV - 1),))]
        + [pltpu.HBM(a.shape, a.dtype) for a in arrays] + [pltpu.HBM(l.shape, l.dtype) for l in lands]
        + [_sds((8, 128), F32)],
        input_output_aliases={i: 2 + i for i in range(2 * n)},
        compiler_params=pltpu.CompilerParams(has_side_effects=SIDE_EFFECT),
    )(*[hbm(a) for a in arrays], *lands)
    return outs[0], outs[1], outs[2:2 + n], outs[2 + n:2 + 2 * n], outs[-1]


def _rs_wait(name, send_sems, recv_sems, srcs, lands, after):
    n = len(srcs)

    def body(*refs):
        src_refs, lnd = refs[:n], refs[n:2 * n]
        send, recv = refs[2 * n], refs[2 * n + 1]
        for cp in _rs_copies(*_my_place(), src_refs, lnd, send, recv):
            cp.wait_send()
            cp.wait_recv()

    outs = pl.pallas_call(
        body, name=name,
        in_specs=[HBM] * (2 * n) + [SEMS, SEMS] + [ANY] * len(after),
        out_specs=[HBM] * (2 * n),
        out_shape=[pltpu.HBM(a.shape, a.dtype) for a in list(srcs) + list(lands)],
        input_output_aliases={i: i for i in range(2 * n)},
        compiler_params=pltpu.CompilerParams(has_side_effects=SIDE_EFFECT),
    )(*srcs, *lands, send_sems, recv_sems, *after)
    return outs[:n], outs[n:]


SMALL_PACK_ROWS = 16


def _all_reduce_small(rows, deps=()):
    n = len(rows)

    def body(*refs):
        ins = refs[:n]
        out_ref, mine, buf, send_sems, recv_sems = refs[n + len(deps):]
        x, y, c = _my_place()
        mine[...] = jnp.zeros_like(mine)
        for (r0, a), ref in zip(rows, ins):
            mine[r0:r0 + a.shape[0], 0:a.shape[1]] = ref[...]
        buf[_flat(x, y, c)] = mine[...]
        copies = []
        for r in range(1, N_DEV):
            peer = _peer(x, y, c, r)
            send = pltpu.make_async_remote_copy(
                src_ref=mine, dst_ref=buf.at[_flat(x, y, c)],
                send_sem=send_sems.at[r - 1], recv_sem=recv_sems.at[r - 1], device_id=peer, device_id_type=MESH)
            send.start()
            recv = pltpu.make_async_remote_copy(
                src_ref=mine, dst_ref=buf.at[_flat(*peer)],
                send_sem=send_sems.at[r - 1], recv_sem=recv_sems.at[r - 1], device_id=peer, device_id_type=MESH)
            copies.append((send, recv))
        for send, recv in copies:
            send.wait_send()
            recv.wait_recv()
        acc = buf[0]
        for s in range(1, N_DEV):
            acc = acc + buf[s]
        out_ref[...] = acc

    vm = pl.BlockSpec(memory_space=pltpu.VMEM)
    shape = (SMALL_PACK_ROWS, D_MODEL)
    return pl.pallas_call(
        body, name="all_reduce_small", in_specs=[vm] * n + [ANY] * len(deps), out_specs=vm,
        out_shape=_sds(shape, F32),
        scratch_shapes=[pltpu.VMEM(shape, F32), pltpu.VMEM((N_DEV,) + shape, F32),
                        pltpu.SemaphoreType.DMA((7,)), pltpu.SemaphoreType.DMA((7,))],
    )(*[a for _, a in rows], *deps)


def _adamw_math(w, g, m, v):
    m = ADAM_B1 * m + (1.0 - ADAM_B1) * g
    v = ADAM_B2 * v + (1.0 - ADAM_B2) * (g * g)
    m_hat = m / (1.0 - ADAM_B1 ** ADAM_STEP)
    v_hat = v / (1.0 - ADAM_B2 ** ADAM_STEP)
    delta = -ADAM_LR * (m_hat / (jnp.sqrt(v_hat) + ADAM_EPS) + ADAM_WD * w)
    return delta, m, v


def _adamw_big(name, w, m, v, srcs, lands, me):
    nl, rows, cols = w.shape
    tr = next(cand for cand in (256, 128, 64, 32, 16, 8) if rows % cand == 0)

    def body(me_ref, w_ref, m_ref, v_ref, *rest):
        src_refs, land_refs = rest[:nl], rest[nl:2 * nl]
        g_ref, d_ref, mo_ref, vo_ref = rest[2 * nl:]
        for layer in range(nl):
            @pl.when(pl.program_id(0) == layer)
            def _():
                g = src_refs[layer][...].astype(F32)
                for s in range(N_DEV - 1):
                    g = g + land_refs[layer][s].astype(F32)
                delta, mn, vn = _adamw_math(w_ref[...], g, m_ref[...], v_ref[...])
                g_ref[...] = g
                d_ref[...] = delta
                mo_ref[...] = mn
                vo_ref[...] = vn

    blk = pl.BlockSpec((None, tr, cols), lambda l, i, me_ref: (l, i, 0))
    at = lambda layer, l, i: jnp.where(l == layer, i, 0)
    own = [pl.BlockSpec((None, tr, cols), functools.partial(lambda layer, l, i, me_ref: (me_ref[0], at(layer, l, i), 0),
                                                            layer)) for layer in range(nl)]
    peers = [pl.BlockSpec((N_DEV - 1, tr, cols), functools.partial(lambda layer, l, i, me_ref: (0, at(layer, l, i), 0),
                                                                   layer)) for layer in range(nl)]
    return pl.pallas_call(
        body, name=name,
        grid_spec=pltpu.PrefetchScalarGridSpec(
            num_scalar_prefetch=1, grid=(nl, rows // tr),
            in_specs=[blk, blk, blk] + own + peers, out_specs=[blk] * 4),
        out_shape=[_sds((nl, rows, cols), F32)] * 4,
        compiler_params=_cparams(("arbitrary", "arbitrary")),
    )(me, w, m, v, *srcs, *lands)


def _adamw_small(ws, gs, ms, vs):
    n = len(ws)

    def body(*refs):
        w_refs, g_refs, m_refs, v_refs = (refs[i * n:(i + 1) * n] for i in range(4))
        d_out, m_out, v_out = (refs[(4 + i) * n:(5 + i) * n] for i in range(3))
        for i in range(n):
            delta, mn, vn = _adamw_math(w_refs[i][...], g_refs[i][...], m_refs[i][...], v_refs[i][...])
            d_out[i][...] = delta
            m_out[i][...] = mn
            v_out[i][...] = vn

    vm = pl.BlockSpec(memory_space=pltpu.VMEM)
    outs = pl.pallas_call(
        body, name="adamw_small", in_specs=[vm] * (4 * n), out_specs=[vm] * (3 * n),
        out_shape=[_sds(a.shape, F32) for a in ws] * 3,
    )(*ws, *gs, *ms, *vs)
    return outs[:n], outs[n:2 * n], outs[2 * n:]


SMALL_ROWS = 16


def _pad_to(a, rows, cols):
    return jnp.pad(a, ((0, rows - a.shape[0]), (0, cols - a.shape[1])))


def _place_own(blocks):
    me = _flat(*_my_place())
    return [lax.dynamic_update_slice(lax.empty((N_DEV,) + b.shape, b.dtype), b[None], (me,) + (0,) * b.ndim)
            for b in blocks]


def _ag_copies(x, y, c, blocks, bufs, send_sems, recv_sems):
    sends, recvs = [], []
    for a in range(len(blocks)):
        for r in range(1, N_DEV):
            peer = _peer(x, y, c, r)
            k = a * (N_DEV - 1) + r - 1
            make = lambda place: pltpu.make_async_remote_copy(
                src_ref=blocks[a], dst_ref=bufs[a].at[_flat(*place)],
                send_sem=send_sems.at[k], recv_sem=recv_sems.at[k], device_id=peer, device_id_type=MESH)
            sends.append(make((x, y, c)))
            recvs.append(make(peer))
    return sends, recvs


def _ag_start(groups, after):
    flat = [pair for g in groups for pair in g]
    n, ng = len(flat), len(groups)
    hbm = lambda a: pltpu.with_memory_space_constraint(a, pltpu.HBM)

    def body(*refs):
        blocks, bufs = refs[:n], refs[n:2 * n]
        sems = refs[2 * n + len(after):2 * n + len(after) + 2 * ng]
        x, y, c = _my_place()
        at = 0
        for gi, g in enumerate(groups):
            sends, _ = _ag_copies(x, y, c, blocks[at:at + len(g)], bufs[at:at + len(g)], sems[2 * gi], sems[2 * gi + 1])
            for cp in sends:
                cp.start()
            at += len(g)
        refs[-1][...] = jnp.zeros_like(refs[-1])

    sem_shapes = [pltpu.SemaphoreType.DMA((len(g) * (N_DEV - 1),)) for g in groups for _ in range(2)]
    outs = pl.pallas_call(
        body, name="gather_start",
        in_specs=[HBM] * (2 * n) + [ANY] * len(after),
        out_specs=[SEMS] * (2 * ng) + [HBM] * (2 * n) + [pl.BlockSpec(memory_space=pltpu.VMEM)],
        out_shape=sem_shapes + [pltpu.HBM(b.shape, b.dtype) for b, _ in flat]
        + [pltpu.HBM(u.shape, u.dtype) for _, u in flat] + [_sds((8, 128), F32)],
        input_output_aliases={i: 2 * ng + i for i in range(2 * n)},
        compiler_params=pltpu.CompilerParams(has_side_effects=SIDE_EFFECT),
    )(*[hbm(b) for b, _ in flat], *[hbm(u) for _, u in flat], *after)
    blocks_thru, bufs_thru = outs[2 * ng:2 * ng + n], outs[2 * ng + n:2 * ng + 2 * n]
    started, at = [], 0
    for gi, g in enumerate(groups):
        started.append((outs[2 * gi], outs[2 * gi + 1], blocks_thru[at:at + len(g)], bufs_thru[at:at + len(g)]))
        at += len(g)
    return started, outs[-1]


def _ag_wait(name, send_sems, recv_sems, blocks, bufs, after):
    n = len(blocks)

    def body(*refs):
        sends, recvs = _ag_copies(*_my_place(), refs[:n], refs[n:2 * n], refs[2 * n], refs[2 * n + 1])
        for s, r in zip(sends, recvs):
            s.wait_send()
            r.wait_recv()

    outs = pl.pallas_call(
        body, name=name,
        in_specs=[HBM] * (2 * n) + [SEMS, SEMS] + [ANY] * len(after),
        out_specs=[HBM] * (2 * n),
        out_shape=[pltpu.HBM(a.shape, a.dtype) for a in list(blocks) + list(bufs)],
        input_output_aliases={i: i for i in range(2 * n)},
        compiler_params=pltpu.CompilerParams(has_side_effects=SIDE_EFFECT),
    )(*blocks, *bufs, send_sems, recv_sems, *after)
    return outs[n:]


def _split_call(name, body, thru, sems_in, new_sems, after):
    n, ns, nn = len(thru), len(sems_in), len(new_sems)
    hbm = lambda a: pltpu.with_memory_space_constraint(a, pltpu.HBM)

    def wrapped(*refs):
        body(refs[:n], refs[n:n + ns], refs[n + ns + len(after):n + ns + len(after) + nn])
        refs[-1][...] = jnp.zeros_like(refs[-1])

    outs = pl.pallas_call(
        wrapped, name=name,
        in_specs=[HBM] * n + [SEMS] * ns + [ANY] * len(after),
        out_specs=[SEMS] * nn + [HBM] * n + [pl.BlockSpec(memory_space=pltpu.VMEM)],
        out_shape=[pltpu.SemaphoreType.DMA((k,)) for k in new_sems] + [pltpu.HBM(a.shape, a.dtype) for a in thru]
        + [_sds((8, 128), F32)],
        input_output_aliases={i: nn + i for i in range(n)},
        compiler_params=pltpu.CompilerParams(has_side_effects=SIDE_EFFECT),
    )(*[hbm(a) for a in thru], *sems_in, *after)
    return list(outs[:nn]), list(outs[nn:nn + n]), outs[-1]


def _first_gather(blocks, bufs, overlap):
    n = len(blocks)

    def copies(refs, s1, r1, s2, r2):
        x, y, c = _my_place()
        me, sibling = (x, y, c), (x, y, 1 - c)
        chips = [(1 - x, y), (x, 1 - y), (1 - x, 1 - y)]
        blk, buf = refs[:n], refs[n:]
        out = dict(send1=[], recv1_sib=[], recv1_ici=[], send2=[], recv2=[])
        for a in range(n):
            place = lambda dev: buf[a].at[_flat(*dev)]
            for k, to in enumerate([sibling] + [(*chip, c) for chip in chips]):
                mk = lambda dst: pltpu.make_async_remote_copy(
                    src_ref=blk[a], dst_ref=dst, send_sem=s1.at[4 * a + k], recv_sem=r1.at[4 * a + k],
                    device_id=to, device_id_type=MESH)
                out['send1'].append(mk(place(me)))
                out['recv1_sib' if k == 0 else 'recv1_ici'].append(mk(place(to)))
            for j, chip in enumerate(chips):
                mk = lambda dev: pltpu.make_async_remote_copy(
                    src_ref=place(dev), dst_ref=place(dev), send_sem=s2.at[3 * a + j], recv_sem=r2.at[3 * a + j],
                    device_id=sibling, device_id_type=MESH)
                out['send2'].append(mk((*chip, c)))
                out['recv2'].append(mk((*chip, 1 - c)))
        return out

    def start(refs, sems_in, new):
        for cp in copies(refs, new[0], new[1], new[0], new[1])['send1']:
            cp.start()

    def forward(refs, sems_in, new):
        cps = copies(refs, sems_in[0], sems_in[1], new[0], new[1])
        for cp in cps['recv1_ici']:
            cp.wait_recv()
        for cp in cps['send2']:
            cp.start()

    def finish(refs, sems_in, new):
        cps = copies(refs, *sems_in)
        for cp in cps['recv1_sib'] + cps['recv2']:
            cp.wait_recv()
        for cp in cps['send1'] + cps['send2']:
            cp.wait_send()

    sems1, thru, token = _split_call("first_gather_start", start, list(blocks) + list(bufs), [], [4 * n, 4 * n], ())
    after = overlap(token)
    sems2, thru, token = _split_call("first_gather_forward", forward, thru, sems1, [3 * n, 3 * n], after)
    _, thru, _ = _split_call("first_gather_wait", finish, thru, sems1 + sems2, [], ())
    return thru[n:], token


def _prepare_weights(p, x):
    n = N_DEV
    bf = lambda a: a.astype(BF16)
    gn_pack = jnp.concatenate([
        _pad_to(p['ret_gn'][0], RET_HEADS, 128), _pad_to(p['mla_q_a_norm'], 1, 128),
        _pad_to(p['mla_kv_a_norm'], 1, 128), jnp.zeros((2, 128), F32)], axis=0)
    ple = lambda l: [bf(p['ple_gate_w'][l]), bf(p['ple_proj_w'][l])]
    names = ('ret_out', 'mlp_w1_0', 'mlp_w2_0', 'ple_0', 'mla', 'layer_1')
    later = [[bf(p['ret_w_out'][0])], [bf(p['mlp_w1'][0])], [bf(p['mlp_w2'][0])], ple(0),
             [bf(p['mla_w_in'][0]), bf(p['mla_w_uq'][0]), bf(p['mla_w_ukv'][0]), bf(p['mla_w_out'][0])],
             [bf(p['mlp_w1'][1]), bf(p['mlp_w2'][1])] + ple(1)]
    first = [gn_pack, bf(p['ret_w_in'][0])]
    behind = {}

    def overlap(token):
        behind['hn0'] = _rms_fwd("mix_norm0", x, p['mix_norm'][0:1], deps=(token,))
        behind['bufs'] = _place_own([b for g in later for b in g])
        return (behind['hn0'], *behind['bufs'])

    (pack, wri), token = _first_gather(first, _place_own(first), overlap)
    bufs = behind['bufs']
    groups, at = [], 0
    for g in later:
        groups.append(list(zip(g, bufs[at:at + len(g)])))
        at += len(g)
    started, token = _ag_start(groups, (token,))

    w = {k: p[k] for k in ('mix_norm', 'mlp_norm', 'ple_norm')}
    w['hn0'] = behind['hn0']
    w['ret_gn'] = pack[:, :RET_HEADS, :RET_DV // n].transpose(1, 0, 2).reshape(RET_HEADS, RET_DV)
    w['mla_q_a_norm'] = pack[:, RET_HEADS, :MLA_Q_RANK // n].reshape(1, MLA_Q_RANK)
    w['mla_kv_a_norm'] = pack[:, RET_HEADS + 1, :MLA_KV_RANK // n].reshape(1, MLA_KV_RANK)
    w['ret_w_in'] = wri
    w['mla_q_norm'] = _pad_to(p['mla_q_norm'], 1, MLA_HD_PAD)
    w['mla_k_norm'] = _pad_to(p['mla_k_norm'], 1, MLA_HD_PAD)
    w['deps'] = (token,)

    def fetch(name, after):
        got = list(_ag_wait("gather_wait_" + name, *started[names.index(name)], after))
        if name == 'ret_out':
            return dict(ret_w_out=got[0].reshape(RET_V_W, D_MODEL))
        if name == 'mla':
            wmi, wuq, wukv, wmo = got
            return dict(mla_w_in=jnp.pad(wmi.reshape(D_MODEL, MLA_IN), ((0, 0), (0, MLA_IN_PAD - MLA_IN))),
                        mla_w_uq=jnp.pad(wuq, ((0, 0), (0, 0), (0, MLA_HD_PAD - MLA_QKD))),
                        mla_w_ukv=wukv, mla_w_out=wmo.reshape(D_MODEL, D_MODEL))
        out = {}
        if name in ('mlp_w1_0', 'layer_1'):
            out['mlp_w1'] = got.pop(0)
        if name in ('mlp_w2_0', 'layer_1'):
            out['mlp_w2'] = got.pop(0)
        if name in ('ple_0', 'layer_1'):
            out['ple_gate_w'] = got[0].reshape(D_MODEL, D_MODEL)
            out['ple_proj_w'] = got[1].transpose(1, 0, 2).reshape(PLE_DIM, D_MODEL)
        return out

    return w, fetch


def _small_grads(small, after):
    rows = [(0, small['mix_norm'][0]), (1, small['mix_norm'][1]), (2, small['mlp_norm'][0]),
            (3, small['mlp_norm'][1]), (4, small['ple_norm'][0]), (5, small['ple_norm'][1]),
            (6, small['ret_gn']), (10, small['mla_q_a_norm']), (11, small['mla_kv_a_norm']),
            (12, small['mla_q_norm']), (13, small['mla_k_norm'])]
    gs = _all_reduce_small(rows, after)
    me = _flat(*_my_place())
    n = N_DEV
    return dict(
        mix_norm=gs[0:2], mlp_norm=gs[2:4], ple_norm=gs[4:6],
        ret_gn=lax.dynamic_slice(gs, (6, me * (RET_DV // n)), (RET_HEADS, RET_DV // n)),
        mla_q_a_norm=lax.dynamic_slice(gs, (10, me * (MLA_Q_RANK // n)), (1, MLA_Q_RANK // n)),
        mla_kv_a_norm=lax.dynamic_slice(gs, (11, me * (MLA_KV_RANK // n)), (1, MLA_KV_RANK // n)),
        mla_q_norm=gs[12:13, :MLA_QKD], mla_k_norm=gs[13:14, :MLA_QKD])


def kernel(x, p, mix_norm, ret_w_in, ret_gn, ret_w_out, mla_w_in, mla_q_a_norm, mla_kv_a_norm, mla_w_uq, mla_w_ukv, mla_q_norm, mla_k_norm, mla_w_out, mlp_norm, mlp_w1, mlp_w2, ple_norm, ple_gate_w, ple_proj_w, loss_target, m_mix_norm, m_ret_w_in, m_ret_gn, m_ret_w_out, m_mla_w_in, m_mla_q_a_norm, m_mla_kv_a_norm, m_mla_w_uq, m_mla_w_ukv, m_mla_q_norm, m_mla_k_norm, m_mla_w_out, m_mlp_norm, m_mlp_w1, m_mlp_w2, m_ple_norm, m_ple_gate_w, m_ple_proj_w, v_mix_norm, v_ret_w_in, v_ret_gn, v_ret_w_out, v_mla_w_in, v_mla_q_a_norm, v_mla_kv_a_norm, v_mla_w_uq, v_mla_w_ukv, v_mla_q_norm, v_mla_k_norm, v_mla_w_out, v_mlp_norm, v_mlp_w1, v_mlp_w2, v_ple_norm, v_ple_gate_w, v_ple_proj_w):
    given = dict(locals())
    params = {n: given[n] for n in WEIGHTS}
    w, fetch = _prepare_weights(params, x[0])

    started = []

    def emit(group):
        keys = list(group)
        send, recv, srcs, lands, token = _rs_start(f"rs_start{len(started)}", [group[k] for k in keys])
        started.append((keys, send, recv, srcs, lands))
        return (token,)

    sq_err, grad_x, _, small = _local_step(x[0], p, loss_target[0], w, fetch, emit)
    loss = lax.psum(0.5 / D_MODEL * sq_err[0, 0], ("x", "y", "c"))

    grads, deltas, new_m, new_v = {}, {}, {}, {}

    def small_updates(after):
        sg = _small_grads(small, after)
        two_d = lambda a: a.reshape(-1, a.shape[-1])
        d_s, m_s, v_s = _adamw_small(
            [two_d(params[n]) for n in SMALL], [sg[n] for n in SMALL],
            [two_d(given["m_" + n]) for n in SMALL], [two_d(given["v_" + n]) for n in SMALL])
        for i, n in enumerate(SMALL):
            shape = params[n].shape
            grads[n], deltas[n], new_m[n], new_v[n] = (a.reshape(shape) for a in (sg[n], d_s[i], m_s[i], v_s[i]))
        return (d_s[0],)

    me = _flat(*_my_place()).astype(jnp.int32).reshape(1)
    after = (grad_x,)
    src_of, land_of = {}, {}
    for gi, (keys, send, recv, srcs, lands) in enumerate(started):
        if gi == len(started) - 1:
            after = small_updates(after)
        srcs, lands = _rs_wait(f"rs_wait{gi}", send, recv, srcs, lands, after)
        for k, s, l in zip(keys, srcs, lands):
            src_of[k], land_of[k] = s, l
        done = [n for n in BIG if n not in grads and all((n, l) in src_of for l in range(params[n].shape[0]))]
        for n in done:
            layers = range(params[n].shape[0])
            grads[n], deltas[n], new_m[n], new_v[n] = _adamw_big(
                "adamw_" + n, params[n], given["m_" + n], given["v_" + n],
                [src_of[(n, l)] for l in layers], [land_of[(n, l)] for l in layers], me)
        if done:
            after = tuple(deltas[n] for n in done)

    return (loss, grad_x[None], *[grads[n] for n in WEIGHTS], *[deltas[n] for n in WEIGHTS],
            *[new_m[n] for n in WEIGHTS], *[new_v[n] for n in WEIGHTS])
```

```python
import functools
import math

import jax
import jax.numpy as jnp
from jax import lax
from jax.experimental import pallas as pl
from jax.experimental.pallas import tpu as pltpu

F32 = jnp.float32
BF16 = jnp.bfloat16
MESH = pl.DeviceIdType.MESH
ANY = pl.BlockSpec(memory_space=pl.ANY)

N_DEV = 8
D_MODEL = 1024
CHUNK = 64
EPS = 1e-6
ROPE_THETA = 10000.0
RET_HEADS = 4
RET_DK = 256
RET_DV = 512
RET_QK_W = RET_HEADS * RET_DK
RET_V_W = RET_HEADS * RET_DV
RET_IN = 2 * RET_QK_W + 2 * RET_V_W
MLA_HEADS = 8
MLA_NOPE = 128
MLA_ROPE = 64
MLA_QKD = MLA_NOPE + MLA_ROPE
MLA_VD = 128
MLA_Q_RANK = 384
MLA_KV_RANK = 256
MLA_IN = MLA_Q_RANK + MLA_KV_RANK + MLA_ROPE
MLA_IN_PAD = 768
MLA_HD_PAD = 256
D_FF = 4096
PLE_DIM = 256
ATT_SCALE = MLA_QKD ** -0.5
LOG2E = 1.4426950408889634
ATT_EXP2 = ATT_SCALE * LOG2E

ADAM_LR = 0.001
ADAM_B1 = 0.9
ADAM_B2 = 0.999
ADAM_EPS = 1e-08
ADAM_WD = 0.01
ADAM_STEP = 10

VMEM_LIMIT = 52 * 1024 * 1024
ROW_TILE = 1024
RET_ROWS = 256
ATT_BLOCK = 256
ATT_QROWS = 1024
ATT_KROWS = 1024
ATT_HEADS = 2

WEIGHTS = ['mix_norm', 'ret_w_in', 'ret_gn', 'ret_w_out', 'mla_w_in', 'mla_q_a_norm', 'mla_kv_a_norm',
           'mla_w_uq', 'mla_w_ukv', 'mla_q_norm', 'mla_k_norm', 'mla_w_out', 'mlp_norm', 'mlp_w1', 'mlp_w2',
           'ple_norm', 'ple_gate_w', 'ple_proj_w']
BIG = ['ret_w_in', 'ret_w_out', 'mla_w_in', 'mla_w_uq', 'mla_w_ukv', 'mla_w_out', 'mlp_w1', 'mlp_w2',
       'ple_gate_w', 'ple_proj_w']
SMALL = [w for w in WEIGHTS if w not in BIG]


def _cparams(sem=None):
    return pltpu.CompilerParams(dimension_semantics=sem, vmem_limit_bytes=VMEM_LIMIT)


def _dot(a, b, ca, cb):
    return lax.dot_general(a, b, (((ca,), (cb,)), ((), ())), preferred_element_type=F32)


def _bf(v):
    return v if v.dtype == BF16 else v.astype(BF16)


def _sigmoid(z):
    return 1.0 / (1.0 + jnp.exp(-z))


def _mm(name, grid, a, a_spec, b, b_spec, contract, outs, extras=(), epi=None, deps=(), split=None):
    nk = grid[2]
    n_ex, n_out, n_dep = len(extras), len(outs), len(deps)
    acc_shape = tuple(d for d in outs[0][1].block_shape if d is not None)
    if split is not None:
        acc_shape = (acc_shape[1], acc_shape[0] * split)

    def body(*refs):
        a_ref, b_ref = refs[:2]
        ex_refs = refs[2:2 + n_ex]
        out_refs = refs[2 + n_ex + n_dep:2 + n_ex + n_dep + n_out]

        def product():
            return _dot(_bf(a_ref[...]), _bf(b_ref[...]), contract[0], contract[1])

        def finish(acc):
            if split is not None:
                for j in range(acc_shape[1] // split):
                    out_refs[0][j] = acc[:, j * split:(j + 1) * split].astype(out_refs[0].dtype)
                return
            acc = acc[...]
            res = epi(acc, *[r[...] for r in ex_refs]) if epi is not None else (acc,)
            for o, r in zip(out_refs, res):
                o[...] = r.astype(o.dtype)

        if nk == 1:
            finish(product())
        else:
            acc_ref = refs[-1]
            k = pl.program_id(2)

            @pl.when(k == 0)
            def _():
                acc_ref[...] = jnp.zeros_like(acc_ref)

            acc_ref[...] += product()

            @pl.when(k == nk - 1)
            def _():
                finish(acc_ref)

    return pl.pallas_call(
        body, name=name, grid=grid,
        in_specs=[a_spec, b_spec] + [s for _, s in extras] + [ANY] * n_dep,
        out_specs=[s for _, s in outs],
        out_shape=[s for s, _ in outs],
        scratch_shapes=[pltpu.VMEM(acc_shape, F32)] if nk > 1 else [],
        compiler_params=_cparams(("parallel", "parallel", "arbitrary")),
    )(a, b, *[x for x, _ in extras], *deps)


def _mm_rows(name, tm, a, w, mode, outs, extras=(), epi=None, deps=()):
    n_sh, rows, cols = w.shape
    n_ex, n_out, n_dep = len(extras), len(outs), len(deps)
    by_cols = mode in ('nn_cols', 'nt_rows')
    width = cols if mode == 'nn_cols' else rows

    def body(*refs):
        a_ref, w_ref = refs[:2]
        ex_refs = refs[2:2 + n_ex]
        out_refs = refs[2 + n_ex + n_dep:2 + n_ex + n_dep + n_out]
        if by_cols:
            av = _bf(a_ref[...])
            for s in range(n_sh):
                cs = slice(s * width, (s + 1) * width)
                acc = _dot(av, w_ref[s], 1, 0 if mode == 'nn_cols' else 1)
                res = epi(acc, *[r[:, cs] for r in ex_refs]) if epi is not None else (acc,)
                for o, r in zip(out_refs, res):
                    o[:, cs] = r.astype(o.dtype)
        else:
            chunk = rows if mode == 'nn_rows' else cols
            acc = None
            for s in range(n_sh):
                part = _dot(_bf(a_ref[:, s * chunk:(s + 1) * chunk]), w_ref[s], 1, 0 if mode == 'nn_rows' else 1)
                acc = part if acc is None else acc + part
            res = epi(acc, *[r[...] for r in ex_refs]) if epi is not None else (acc,)
            for o, r in zip(out_refs, res):
                o[...] = r.astype(o.dtype)

    t, ka = a.shape
    return pl.pallas_call(
        body, name=name, grid=(t // tm, 1, 1),
        in_specs=[pl.BlockSpec((tm, ka), lambda i, j, k: (i, 0)),
                  pl.BlockSpec((n_sh, rows, cols), lambda i, j, k: (0, 0, 0))] + [s for _, s in extras] + [ANY] * n_dep,
        out_specs=[s for _, s in outs],
        out_shape=[s for s, _ in outs],
        compiler_params=_cparams(("parallel", "arbitrary", "arbitrary")),
    )(a, w, *[x for x, _ in extras], *deps)


def _sds(shape, dtype):
    return jax.ShapeDtypeStruct(shape, dtype)


def _row_tile(t, cap=ROW_TILE):
    return min(cap, t)


def _rms_fwd(name, x, g, deps=()):
    t, d = x.shape
    tm = _row_tile(t)

    def body(x_ref, g_ref, *rest):
        o_ref = rest[-1]
        xv = x_ref[...]
        r = lax.rsqrt(jnp.mean(xv * xv, axis=-1, keepdims=True) + EPS)
        o_ref[...] = (xv * r * g_ref[...]).astype(o_ref.dtype)

    return pl.pallas_call(
        body, name=name, grid=(t // tm,),
        in_specs=[pl.BlockSpec((tm, d), lambda i: (i, 0)), pl.BlockSpec((1, d), lambda i: (0, 0))] + [ANY] * len(deps),
        out_specs=pl.BlockSpec((tm, d), lambda i: (i, 0)),
        out_shape=_sds((t, d), BF16),
        compiler_params=_cparams(("parallel",)),
    )(x, g, *deps)


def _rms_bwd_rows(dy, xv, g, n):
    r = lax.rsqrt(jnp.sum(xv * xv, axis=-1, keepdims=True) / n + EPS)
    xh = xv * r
    dxh = dy * g
    dx = r * (dxh - xh * (jnp.sum(dxh * xh, axis=-1, keepdims=True) / n))
    return dx, dy * xh


def _ple_gate_bwd(name, dh, gate, e):
    t, d = dh.shape
    tm = _row_tile(t)

    def body(dh_ref, g_ref, e_ref, de_ref, dz_ref):
        dh_v, gt = dh_ref[...], g_ref[...].astype(F32)
        de_ref[...] = (dh_v * gt).astype(BF16)
        dz_ref[...] = (dh_v * e_ref[...].astype(F32) * (gt * (1.0 - gt))).astype(BF16)

    row = pl.BlockSpec((tm, d), lambda i: (i, 0))
    return pl.pallas_call(
        body, name=name, grid=(t // tm,), in_specs=[row, row, row], out_specs=[row, row],
        out_shape=[_sds((t, d), BF16), _sds((t, d), BF16)],
        compiler_params=_cparams(("parallel",)),
    )(dh, gate, e)


def _rope_half(v, cos, sin):
    half = v.shape[-1] // 2
    v1, v2 = v[:, :half], v[:, half:]
    return jnp.concatenate([v1 * cos - v2 * sin, v2 * cos + v1 * sin], axis=-1)


def _ret_consts():
    lg = jnp.log(1.0 - 2.0 ** (-5.0 - jnp.arange(RET_HEADS, dtype=F32)))
    idx = jnp.arange(CHUNK, dtype=F32)
    intra = jnp.exp(lg[:, None, None] * jnp.abs(idx[:, None] - idx[None, :]))
    qdec = jnp.exp(lg[:, None] * (idx + 1.0))
    kdec = jnp.exp(lg[:, None] * (CHUNK - 1.0 - idx))
    cdec = jnp.exp(lg * CHUNK)
    qdec = jnp.broadcast_to(qdec[:, :, None], (RET_HEADS, CHUNK, RET_DK))
    kdec = jnp.broadcast_to(kdec[:, :, None], (RET_HEADS, CHUNK, RET_DK))
    cdec = jnp.broadcast_to(cdec[:, None, None], (RET_HEADS, 1, RET_DV))
    return intra, qdec, kdec, cdec


def _ret_specs(rb, rev_nb=None):
    blk = (lambda i: i) if rev_nb is None else (lambda i: rev_nb - 1 - i)
    full = lambda shape: pl.BlockSpec(shape, lambda i: (0,) * len(shape))
    return dict(
        proj=pl.BlockSpec((rb, RET_IN), lambda i: (blk(i), 0)),
        tab=pl.BlockSpec((rb, RET_DK // 2), lambda i: (blk(i), 0)),
        vw=pl.BlockSpec((rb, RET_V_W), lambda i: (blk(i), 0)),
        st=pl.BlockSpec((rb // CHUNK, RET_HEADS, RET_DK, RET_DV), lambda i: (blk(i), 0, 0, 0)),
        gn=full((RET_HEADS, 1, RET_DV)),
        intra=full((RET_HEADS, CHUNK, CHUNK)),
        dec=full((RET_HEADS, CHUNK, RET_DK)),
        cdec=full((RET_HEADS, 1, RET_DV)),
    )


def _ret_fwd(proj, cos, sin, gn):
    t = proj.shape[0]
    rb = min(RET_ROWS, t)
    cpb = rb // CHUNK
    intra, qdec, kdec, cdec = _ret_consts()
    sp = _ret_specs(rb)

    def body(proj_ref, cos_ref, sin_ref, gn_ref, intra_ref, qd_ref, kd_ref, cd_ref,
             gated_ref, outp_ref, st_ref, s_ref):
        @pl.when(pl.program_id(0) == 0)
        def _():
            s_ref[...] = jnp.zeros_like(s_ref)

        def chunk(c, carry):
            rows = pl.ds(pl.multiple_of(c * CHUNK, CHUNK), CHUNK)
            cs, sn = cos_ref[rows, :], sin_ref[rows, :]
            for h in range(RET_HEADS):
                q = proj_ref[rows, h * RET_DK:(h + 1) * RET_DK].astype(F32)
                k = proj_ref[rows, RET_QK_W + h * RET_DK:RET_QK_W + (h + 1) * RET_DK].astype(F32)
                v = proj_ref[rows, 2 * RET_QK_W + h * RET_DV:2 * RET_QK_W + (h + 1) * RET_DV]
                g = proj_ref[rows, 2 * RET_QK_W + RET_V_W + h * RET_DV:
                             2 * RET_QK_W + RET_V_W + (h + 1) * RET_DV].astype(F32)
                qr = _rope_half(q, cs, sn)
                kr = _rope_half(k, cs, sn) * (RET_DK ** -0.5)
                qb, kb, vb = qr.astype(BF16), kr.astype(BF16), v
                sc = _dot(qb, kb, 1, 1) * intra_ref[h]
                inner = _dot(sc.astype(BF16), vb, 1, 0)
                s_old = s_ref[h]
                sb = s_old.astype(BF16)
                st_ref[c, h] = sb
                cross = _dot((qr * qd_ref[h]).astype(BF16), sb, 1, 0)
                out = inner + cross
                s_ref[h] = s_old * cd_ref[h] + _dot((kr * kd_ref[h]).astype(BF16), vb, 0, 0)
                r = lax.rsqrt(jnp.mean(out * out, axis=-1, keepdims=True) + EPS)
                y = out * r * gn_ref[h]
                cols = slice(h * RET_DV, (h + 1) * RET_DV)
                gated_ref[rows, cols] = (g * _sigmoid(g) * y).astype(BF16)
                outp_ref[rows, cols] = out
            return carry

        lax.fori_loop(0, cpb, chunk, 0)

    return pl.pallas_call(
        body, name="ret_fwd", grid=(t // rb,),
        in_specs=[sp['proj'], sp['tab'], sp['tab'], sp['gn'], sp['intra'], sp['dec'], sp['dec'], sp['cdec']],
        out_specs=[sp['vw'], sp['vw'], sp['st']],
        out_shape=[_sds((t, RET_V_W), BF16), _sds((t, RET_V_W), F32),
                   _sds((t // CHUNK, RET_HEADS, RET_DK, RET_DV), BF16)],
        scratch_shapes=[pltpu.VMEM((RET_HEADS, RET_DK, RET_DV), F32)],
        compiler_params=_cparams(("arbitrary",)),
    )(proj, cos, sin, gn.reshape(RET_HEADS, 1, RET_DV), intra, qdec, kdec, cdec)


def _ret_gate_bwd_epi(dgt, out, g, gn):
    g = g.astype(F32)
    r = lax.rsqrt(jnp.mean(out * out, axis=-1, keepdims=True) + EPS)
    xh = out * r
    sg = _sigmoid(g)
    dgate = dgt * (xh * gn) * (sg * (1.0 + g * (1.0 - sg)))
    dy = dgt * (g * sg)
    dxh = dy * gn
    dout = r * (dxh - xh * jnp.mean(dxh * xh, axis=-1, keepdims=True))
    return dout, dgate, jnp.sum(dy * xh, axis=0, keepdims=True)


def _ret_bwd(proj, cos, sin, states, dout, dgate, deps=()):
    t = proj.shape[0]
    rb = min(RET_ROWS, t)
    cpb = rb // CHUNK
    nb = t // rb
    intra, qdec, kdec, cdec = _ret_consts()
    sp = _ret_specs(rb, rev_nb=nb)

    def body(proj_ref, cos_ref, sin_ref, intra_ref, qd_ref, kd_ref, cd_ref, st_ref, dout_ref, dgate_ref, *rest):
        dproj_ref, ds_ref = rest[len(deps):]

        @pl.when(pl.program_id(0) == 0)
        def _():
            ds_ref[...] = jnp.zeros_like(ds_ref)

        def chunk(cc, carry):
            c = cpb - 1 - cc
            rows = pl.ds(pl.multiple_of(c * CHUNK, CHUNK), CHUNK)
            cs, sn = cos_ref[rows, :], sin_ref[rows, :]
            for h in range(RET_HEADS):
                q = proj_ref[rows, h * RET_DK:(h + 1) * RET_DK].astype(F32)
                k = proj_ref[rows, RET_QK_W + h * RET_DK:RET_QK_W + (h + 1) * RET_DK].astype(F32)
                v = proj_ref[rows, 2 * RET_QK_W + h * RET_DV:2 * RET_QK_W + (h + 1) * RET_DV]
                cols = slice(h * RET_DV, (h + 1) * RET_DV)
                qr = _rope_half(q, cs, sn)
                kr = _rope_half(k, cs, sn) * (RET_DK ** -0.5)
                qb, kb, vb = qr.astype(BF16), kr.astype(BF16), v
                qdb = (qr * qd_ref[h]).astype(BF16)
                kdb = (kr * kd_ref[h]).astype(BF16)
                doutb = dout_ref[rows, cols]
                itr = intra_ref[h]
                pb = (_dot(qb, kb, 1, 1) * itr).astype(BF16)
                dv = _dot(pb, doutb, 0, 0)
                dsc = (_dot(doutb, vb, 1, 1) * itr).astype(BF16)
                dq = _dot(dsc, kb, 1, 0)
                dk = _dot(dsc, qb, 0, 0)
                dq = dq + _dot(doutb, st_ref[c, h], 1, 1) * qd_ref[h]
                ds_new = ds_ref[h]
                dsb = ds_new.astype(BF16)
                dk = dk + _dot(vb, dsb, 1, 1) * kd_ref[h]
                dv = dv + _dot(kdb, dsb, 1, 0)
                ds_ref[h] = ds_new * cd_ref[h] + _dot(qdb, doutb, 0, 0)
                dproj_ref[rows, h * RET_DK:(h + 1) * RET_DK] = _rope_half(dq, cs, -sn).astype(BF16)
                dproj_ref[rows, RET_QK_W + h * RET_DK:RET_QK_W + (h + 1) * RET_DK] = (
                    _rope_half(dk * (RET_DK ** -0.5), cs, -sn).astype(BF16))
                dproj_ref[rows, 2 * RET_QK_W + h * RET_DV:2 * RET_QK_W + (h + 1) * RET_DV] = dv.astype(BF16)
                dproj_ref[rows, 2 * RET_QK_W + RET_V_W + h * RET_DV:
                          2 * RET_QK_W + RET_V_W + (h + 1) * RET_DV] = dgate_ref[rows, cols]
            return carry

        lax.fori_loop(0, cpb, chunk, 0)

    return pl.pallas_call(
        body, name="ret_bwd", grid=(nb,),
        in_specs=[sp['proj'], sp['tab'], sp['tab'], sp['intra'], sp['dec'], sp['dec'], sp['cdec'],
                  sp['st'], sp['vw'], sp['vw']] + [ANY] * len(deps),
        out_specs=sp['proj'],
        out_shape=_sds((t, RET_IN), BF16),
        scratch_shapes=[pltpu.VMEM((RET_HEADS, RET_DK, RET_DV), F32)],
        compiler_params=_cparams(("arbitrary",)),
    )(proj, cos, sin, intra, qdec, kdec, cdec, states, dout, dgate, *deps)


def _mla_tables(t):
    half = MLA_ROPE // 2
    inv = 1.0 / (ROPE_THETA ** (jnp.arange(0, MLA_ROPE, 2, dtype=F32) / MLA_ROPE))
    ang = jnp.arange(t, dtype=F32)[:, None] * inv[None, :]
    cos, sin = jnp.cos(ang), jnp.sin(ang)
    z = jnp.zeros((t, half), F32)
    c = jnp.concatenate([cos, cos, z, z], axis=1)
    s1 = jnp.concatenate([-sin, z, z, z], axis=1)
    s2 = jnp.concatenate([z, sin, z, z], axis=1)
    return c, s1, s2


def _rope_tile(r, c, s1, s2):
    return r * c + pltpu.roll(r, 96, 1) * s1 + pltpu.roll(r, 32, 1) * s2


def _mla_mid(proj2, qa, kva):
    t = proj2.shape[0]
    tm = _row_tile(t)

    def body(p_ref, qa_ref, kva_ref, cq_ref, ckv_ref):
        cq = p_ref[:, :MLA_Q_RANK]
        ckv = p_ref[:, MLA_Q_RANK:MLA_Q_RANK + MLA_KV_RANK]
        rq = lax.rsqrt(jnp.mean(cq * cq, axis=-1, keepdims=True) + EPS)
        rkv = lax.rsqrt(jnp.mean(ckv * ckv, axis=-1, keepdims=True) + EPS)
        cq_ref[...] = (cq * rq * qa_ref[...]).astype(BF16)
        ckv_ref[...] = (ckv * rkv * kva_ref[...]).astype(BF16)

    return pl.pallas_call(
        body, name="mla_mid", grid=(t // tm,),
        in_specs=[pl.BlockSpec((tm, MLA_IN_PAD), lambda i: (i, 0)),
                  pl.BlockSpec((1, MLA_Q_RANK), lambda i: (0, 0)),
                  pl.BlockSpec((1, MLA_KV_RANK), lambda i: (0, 0))],
        out_specs=[pl.BlockSpec((tm, MLA_Q_RANK), lambda i: (i, 0)),
                   pl.BlockSpec((tm, MLA_KV_RANK), lambda i: (i, 0))],
        out_shape=[_sds((t, MLA_Q_RANK), BF16), _sds((t, MLA_KV_RANK), BF16)],
        compiler_params=_cparams(("parallel",)),
    )(proj2, qa, kva)


def _mla_mid_bwd(proj2, qa, kva, dcq, dckv, dkr):
    t = proj2.shape[0]
    tm = _row_tile(t)

    def body(p_ref, qa_ref, kva_ref, dcq_ref, dckv_ref, dkr_ref, dp_ref, dqa_ref, dkva_ref):
        @pl.when(pl.program_id(0) == 0)
        def _():
            dqa_ref[...] = jnp.zeros_like(dqa_ref)
            dkva_ref[...] = jnp.zeros_like(dkva_ref)

        dxq, dgq = _rms_bwd_rows(dcq_ref[...], p_ref[:, :MLA_Q_RANK], qa_ref[...], MLA_Q_RANK)
        dxk, dgk = _rms_bwd_rows(dckv_ref[...], p_ref[:, MLA_Q_RANK:MLA_Q_RANK + MLA_KV_RANK], kva_ref[...],
                                 MLA_KV_RANK)
        dp_ref[:, :MLA_Q_RANK] = dxq.astype(BF16)
        dp_ref[:, MLA_Q_RANK:MLA_Q_RANK + MLA_KV_RANK] = dxk.astype(BF16)
        dp_ref[:, MLA_Q_RANK + MLA_KV_RANK:] = dkr_ref[...].astype(BF16)
        dqa_ref[...] += jnp.sum(dgq, axis=0, keepdims=True)
        dkva_ref[...] += jnp.sum(dgk, axis=0, keepdims=True)

    return pl.pallas_call(
        body, name="mla_mid_bwd", grid=(t // tm,),
        in_specs=[pl.BlockSpec((tm, MLA_IN_PAD), lambda i: (i, 0)),
                  pl.BlockSpec((1, MLA_Q_RANK), lambda i: (0, 0)),
                  pl.BlockSpec((1, MLA_KV_RANK), lambda i: (0, 0)),
                  pl.BlockSpec((tm, MLA_Q_RANK), lambda i: (i, 0)),
                  pl.BlockSpec((tm, MLA_KV_RANK), lambda i: (i, 0)),
                  pl.BlockSpec((tm, 128), lambda i: (i, 0))],
        out_specs=[pl.BlockSpec((tm, MLA_IN_PAD), lambda i: (i, 0)),
                   pl.BlockSpec((1, MLA_Q_RANK), lambda i: (0, 0)),
                   pl.BlockSpec((1, MLA_KV_RANK), lambda i: (0, 0))],
        out_shape=[_sds((t, MLA_IN_PAD), BF16), _sds((1, MLA_Q_RANK), F32), _sds((1, MLA_KV_RANK), F32)],
        compiler_params=_cparams(("arbitrary",)),
    )(proj2, qa, kva, dcq, dckv, dkr)


def _mla_prep_specs(t, tm):
    head = lambda w: pl.BlockSpec((None, tm, w), lambda i, h: (h, i, 0))
    return dict(
        head256=head(MLA_HD_PAD), head128=head(MLA_VD),
        cols256=pl.BlockSpec((tm, MLA_HD_PAD), lambda i, h: (i, h)),
        cq=pl.BlockSpec((tm, MLA_Q_RANK), lambda i, h: (i, 0)),
        ckv=pl.BlockSpec((tm, MLA_KV_RANK), lambda i, h: (i, 0)),
        wuq=pl.BlockSpec((None, MLA_Q_RANK, MLA_HD_PAD), lambda i, h: (h, 0, 0)),
        wukv=pl.BlockSpec((None, MLA_KV_RANK, MLA_HD_PAD), lambda i, h: (h, 0, 0)),
        kr=pl.BlockSpec((tm, 128), lambda i, h: (i, (MLA_Q_RANK + MLA_KV_RANK) // 128)),
        gain=pl.BlockSpec((1, MLA_HD_PAD), lambda i, h: (0, 0)),
        tab=pl.BlockSpec((tm, 128), lambda i, h: (i, 0)),
    )


def _mla_prep(cq, ckv, wuq, wukv, proj2, gq, gk, tabs):
    t = cq.shape[0]
    tm = _row_tile(t)
    sp = _mla_prep_specs(t, tm)

    def body(cq_ref, ckv_ref, wuq_ref, wukv_ref, kr_ref, gq_ref, gk_ref, c_ref, s1_ref, s2_ref,
             qh_ref, kh_ref, vh_ref):
        c, s1, s2 = c_ref[...], s1_ref[...], s2_ref[...]

        def norm_rope(xv, gain):
            r = lax.rsqrt(jnp.sum(xv * xv, axis=-1, keepdims=True) / MLA_QKD + EPS)
            y = xv * r * gain
            return jnp.concatenate([y[:, :MLA_NOPE], _rope_tile(y[:, MLA_NOPE:], c, s1, s2)], axis=-1)

        kvv = _dot(ckv_ref[...], wukv_ref[...], 1, 0)
        qh_ref[...] = norm_rope(_dot(cq_ref[...], wuq_ref[...], 1, 0), gq_ref[...]).astype(BF16)
        kf = jnp.concatenate([kvv[:, :MLA_NOPE], kr_ref[...]], axis=-1)
        kh_ref[...] = norm_rope(kf, gk_ref[...]).astype(BF16)
        vh_ref[...] = jnp.concatenate([kvv[:, MLA_NOPE:], jnp.ones((tm, MLA_VD), F32)], axis=-1).astype(BF16)

    return pl.pallas_call(
        body, name="mla_prep", grid=(t // tm, MLA_HEADS),
        in_specs=[sp['cq'], sp['ckv'], sp['wuq'], sp['wukv'], sp['kr'], sp['gain'], sp['gain'],
                  sp['tab'], sp['tab'], sp['tab']],
        out_specs=[sp['head256'], sp['head256'], sp['head256']],
        out_shape=[_sds((MLA_HEADS, t, MLA_HD_PAD), BF16), _sds((MLA_HEADS, t, MLA_HD_PAD), BF16),
                   _sds((MLA_HEADS, t, 2 * MLA_VD), BF16)],
        compiler_params=_cparams(("parallel", "arbitrary")),
    )(cq, ckv, wuq, wukv, proj2, gq, gk, *tabs)


def _mla_prep_bwd(cq, ckv, wuq, wukv, proj2, gq, gk, tabs, dqt, dkh, dvh):
    t = cq.shape[0]
    tm = _row_tile(t)
    ab = dqt.shape[-1]
    sp = _mla_prep_specs(t, tm)

    def body(cq_ref, ckv_ref, wuq_ref, wukv_ref, kr_ref, gq_ref, gk_ref, c_ref, s1_ref, s2_ref,
             dqt_ref, dkh_ref, dvh_ref, dq_ref, dkv_ref, dkr_ref, dgq_ref, dgk_ref):
        dqh = jnp.concatenate([dqt_ref[b].T for b in range(tm // ab)], axis=0)
        i, h = pl.program_id(0), pl.program_id(1)

        @pl.when((i == 0) & (h == 0))
        def _():
            dgq_ref[...] = jnp.zeros_like(dgq_ref)
            dgk_ref[...] = jnp.zeros_like(dgk_ref)

        @pl.when(h == 0)
        def _():
            dkr_ref[...] = jnp.zeros_like(dkr_ref)

        c, s1, s2 = c_ref[...], s1_ref[...], s2_ref[...]

        def back(xv, gain, dout):
            dy = jnp.concatenate([dout[:, :MLA_NOPE], _rope_tile(dout[:, MLA_NOPE:], c, -s1, -s2)], axis=-1)
            return _rms_bwd_rows(dy, xv, gain, MLA_QKD)

        kvv = _dot(ckv_ref[...], wukv_ref[...], 1, 0)
        dxq, dgq = back(_dot(cq_ref[...], wuq_ref[...], 1, 0), gq_ref[...], dqh)
        kf = jnp.concatenate([kvv[:, :MLA_NOPE], kr_ref[...]], axis=-1)
        dxk, dgk = back(kf, gk_ref[...], dkh_ref[...])
        dq_ref[...] = dxq.astype(BF16)
        dkv_ref[...] = jnp.concatenate([dxk[:, :MLA_NOPE], dvh_ref[...]], axis=-1).astype(BF16)
        dkr_ref[...] += dxk[:, MLA_NOPE:]
        dgq_ref[...] += jnp.sum(dgq, axis=0, keepdims=True)
        dgk_ref[...] += jnp.sum(dgk, axis=0, keepdims=True)

    return pl.pallas_call(
        body, name="mla_prep_bwd", grid=(t // tm, MLA_HEADS),
        in_specs=[sp['cq'], sp['ckv'], sp['wuq'], sp['wukv'], sp['kr'], sp['gain'], sp['gain'],
                  sp['tab'], sp['tab'], sp['tab'],
                  pl.BlockSpec((None, tm // ab, MLA_HD_PAD, ab), lambda i, h: (h, i, 0, 0)),
                  sp['head256'], sp['head128']],
        out_specs=[sp['cols256'], sp['cols256'], sp['tab'], sp['gain'], sp['gain']],
        out_shape=[_sds((t, MLA_HEADS * MLA_HD_PAD), BF16), _sds((t, MLA_HEADS * MLA_HD_PAD), BF16),
                   _sds((t, 128), F32), _sds((1, MLA_HD_PAD), F32), _sds((1, MLA_HD_PAD), F32)],
        compiler_params=_cparams(("arbitrary", "arbitrary")),
    )(cq, ckv, wuq, wukv, proj2, gq, gk, *tabs, dqt, dkh, dvh)


def _chunk_visible(rows, cols, row_off, col_off):
    rq = lax.shift_right_logical(lax.broadcasted_iota(jnp.int32, (rows, cols), 0) + row_off, 6)
    ck = lax.shift_right_logical(lax.broadcasted_iota(jnp.int32, (rows, cols), 1) + col_off, 6)
    return ck <= rq


def _rows_to_lanes(col):
    return col.T[:8, :]


def _attn_fwd(qh, kh, vh):
    t = qh.shape[1]
    ab = min(ATT_BLOCK, t)
    tq = min(ATT_QROWS, t)
    r = tq // ab
    hg = ATT_HEADS

    def body(q_ref, k_ref, v_ref, o_ref, lse_ref, acc_ref):
        n_un = pl.program_id(1) * r
        acc_ref[...] = jnp.zeros_like(acc_ref)

        def step(b, ms, diag):
            rows = pl.ds(pl.multiple_of(b * ab, ab), ab)
            out = []
            for hh in range(hg):
                m = ms[hh]
                s = _dot(q_ref[hh], k_ref[hh, rows, :], 1, 1)
                if diag is not None:
                    s = jnp.where(_chunk_visible(tq, ab, 0, diag * ab), s, -1e30)
                m_new = jnp.maximum(m, jnp.max(s, axis=-1, keepdims=True))
                p = jnp.exp2((s - m_new) * ATT_EXP2).astype(BF16)
                acc_ref[hh] = jnp.exp2((m - m_new) * ATT_EXP2) * acc_ref[hh] + _dot(p, v_ref[hh, rows, :], 1, 0)
                out.append(m_new)
            return tuple(out)

        ms = tuple(jnp.full((tq, 1), -1e30, F32) for _ in range(hg))
        ms = lax.fori_loop(0, n_un, lambda b, st: step(b, st, None), ms)
        for d in range(r):
            ms = step(n_un + d, ms, d)
        for hh in range(hg):
            l = acc_ref[hh, :, MLA_VD:]
            o_ref[:, hh * MLA_VD:(hh + 1) * MLA_VD] = acc_ref[hh, :, :MLA_VD] / l
            lse_t = _rows_to_lanes(ms[hh] * ATT_EXP2 + jnp.log(l) * LOG2E)
            for d in range(r):
                lse_ref[hh, d] = lse_t[:, d * ab:(d + 1) * ab]

    return pl.pallas_call(
        body, name="mla_attn", grid=(MLA_HEADS // hg, t // tq),
        in_specs=[pl.BlockSpec((hg, tq, MLA_HD_PAD), lambda g, i: (g, i, 0)),
                  pl.BlockSpec((hg, t, MLA_HD_PAD), lambda g, i: (g, 0, 0)),
                  pl.BlockSpec((hg, t, 2 * MLA_VD), lambda g, i: (g, 0, 0))],
        out_specs=[pl.BlockSpec((tq, hg * MLA_VD), lambda g, i: (i, g)),
                   pl.BlockSpec((hg, r, 8, ab), lambda g, i: (g, i, 0, 0))],
        out_shape=[_sds((t, MLA_HEADS * MLA_VD), F32), _sds((MLA_HEADS, t // ab, 8, ab), F32)],
        scratch_shapes=[pltpu.VMEM((hg, tq, 2 * MLA_VD), F32)],
        compiler_params=_cparams(("parallel", "arbitrary")),
    )(qh, kh, vh)


def _attn_delta(do, o, ab):
    t = do.shape[0]
    tm = _row_tile(t)

    def body(do_ref, o_ref, d_ref):
        d = jnp.sum(do_ref[...] * o_ref[...], axis=-1, keepdims=True)
        d_t = _rows_to_lanes(jnp.broadcast_to(d, (tm, 128)))
        for b in range(tm // ab):
            d_ref[b] = d_t[:, b * ab:(b + 1) * ab]

    col = pl.BlockSpec((tm, MLA_VD), lambda i, h: (i, h))
    return pl.pallas_call(
        body, name="mla_delta", grid=(t // tm, MLA_HEADS), in_specs=[col, col],
        out_specs=pl.BlockSpec((None, tm // ab, 8, ab), lambda i, h: (h, i, 0, 0)),
        out_shape=_sds((MLA_HEADS, t // ab, 8, ab), F32),
        compiler_params=_cparams(("parallel", "parallel")),
    )(do, o)


def _attn_bwd(qh, kh, vh, dob, lse_t, dl_t):
    t = qh.shape[1]
    ab = min(ATT_BLOCK, t)
    kb = min(ATT_KROWS, t)
    r = kb // ab
    nq = t // ab
    hg = ATT_HEADS

    def body(q_ref, k_ref, v_ref, do_ref, lse_ref, dl_ref, dqt_ref, dk_ref, dv_ref):
        j = pl.program_id(1)

        @pl.when(j == 0)
        def _():
            dqt_ref[...] = jnp.zeros_like(dqt_ref)

        ks = [k_ref[hh] for hh in range(hg)]
        vs = [v_ref[hh, :, :MLA_VD] for hh in range(hg)]
        kts = [k.T for k in ks]

        dk_ref[...] = jnp.zeros_like(dk_ref)
        dv_ref[...] = jnp.zeros_like(dv_ref)

        def step(b, carry, diag):
            rows = pl.ds(pl.multiple_of(b * ab, ab), ab)
            hi = kb if diag is None else (diag + 1) * ab
            for hh in range(hg):
                q = q_ref[hh, rows, :]
                do = do_ref[rows, hh * MLA_VD:(hh + 1) * MLA_VD]
                s_t = _dot(ks[hh][:hi], q, 1, 1)
                if diag is not None:
                    key_chunk = lax.shift_right_logical(lax.broadcasted_iota(jnp.int32, (hi, ab), 0), 6)
                    query_chunk = lax.shift_right_logical(
                        lax.broadcasted_iota(jnp.int32, (hi, ab), 1) + diag * ab, 6)
                    s_t = jnp.where(key_chunk <= query_chunk, s_t, -1e30)
                p_t = jnp.exp2(s_t * ATT_EXP2 - lse_ref[hh, b][0:1, :])
                dp_t = _dot(vs[hh][:hi], do, 1, 1)
                ds_t = (p_t * (dp_t - dl_ref[hh, b][0:1, :]) * ATT_SCALE).astype(BF16)
                dqt_ref[hh, b] += _dot(kts[hh][:, :hi], ds_t, 1, 0)
                dk_ref[hh, :hi] += _dot(ds_t, q, 1, 0)
                dv_ref[hh, :hi] += _dot(p_t.astype(BF16), do, 1, 0)
            return carry

        for d in range(r):
            step(j * r + d, 0, d)
        lax.fori_loop((j + 1) * r, nq, lambda b, c: step(b, c, None), 0)

    whole = lambda w: pl.BlockSpec((hg, t, w), lambda g, j: (g, 0, 0))
    blk = lambda w: pl.BlockSpec((hg, kb, w), lambda g, j: (g, j, 0))
    stat = pl.BlockSpec((hg, nq, 8, ab), lambda g, j: (g, 0, 0, 0))
    return pl.pallas_call(
        body, name="mla_attn_bwd", grid=(MLA_HEADS // hg, t // kb),
        in_specs=[whole(MLA_HD_PAD), blk(MLA_HD_PAD), blk(2 * MLA_VD),
                  pl.BlockSpec((t, hg * MLA_VD), lambda g, j: (0, g)), stat, stat],
        out_specs=[pl.BlockSpec((hg, nq, MLA_HD_PAD, ab), lambda g, j: (g, 0, 0, 0)), blk(MLA_HD_PAD), blk(MLA_VD)],
        out_shape=[_sds((MLA_HEADS, nq, MLA_HD_PAD, ab), F32), _sds((MLA_HEADS, t, MLA_HD_PAD), F32),
                   _sds((MLA_HEADS, t, MLA_VD), F32)],
        compiler_params=_cparams(("parallel", "arbitrary")),
    )(qh, kh, vh, dob, lse_t, dl_t)


VEC = pl.BlockSpec((1, D_MODEL), lambda i, j, k: (0, 0))


def _rows(tm, width):
    return pl.BlockSpec((tm, width), lambda i, j, k: (i, 0))


def _residual_epi(next_gain):
    if next_gain is None:
        return [], lambda acc, hv: (acc + hv,)

    def epi(acc, hv, g):
        h_new = acc + hv
        r = lax.rsqrt(jnp.mean(h_new * h_new, axis=-1, keepdims=True) + EPS)
        return h_new, h_new * r * g

    return [(next_gain, VEC)], epi


def _residual_outs(t, row, next_gain):
    outs = [(_sds((t, D_MODEL), F32), row)]
    return outs + ([(_sds((t, D_MODEL), BF16), row)] if next_gain is not None else [])


def _mlp_fwd(l, h, hn, w1g, fetch_w2, next_gain):
    t = h.shape[0]
    tm = _row_tile(t, 512)

    def relu2(acc):
        r = jnp.maximum(acc, 0.0)
        return (r * r,)

    (u,) = _mm_rows(f"mlp_up{l}", tm, hn, w1g, 'nn_cols', [(_sds((t, D_FF), BF16), _rows(tm, D_FF))], epi=relu2)
    w2g = fetch_w2((u,))
    row = _rows(tm, D_MODEL)
    more, epi = _residual_epi(next_gain)
    h2, hn_next = _mm_rows(f"mlp_down{l}", tm, u, w2g, 'nn_rows', _residual_outs(t, row, next_gain),
                           extras=[(h, row)] + more, epi=epi)
    return h2, hn_next, (h, hn, u, w1g, w2g)


def _norm_bwd_outs(t, tm):
    return [(_sds((t, D_MODEL), F32), pl.BlockSpec((tm, D_MODEL), lambda i, j, k: (i, 0))),
            (_sds((t // tm, 1, D_MODEL), F32), pl.BlockSpec((None, 1, D_MODEL), lambda i, j, k: (i, 0, 0)))]


def _norm_bwd_epi(acc, xv, res, g):
    dx, dgr = _rms_bwd_rows(acc, xv, g, D_MODEL)
    return res + dx, jnp.sum(dgr, axis=0, keepdims=True)


def _mlp_bwd(l, dh, saved, norm_g):
    h, hn, u, w1g, w2g = saved
    t = h.shape[0]
    tm = _row_tile(t, 512)
    nsh, _, wsh = w1g.shape
    wide = _rows(tm, D_FF)
    (da,) = _mm_rows(f"mlp_du{l}", tm, dh, w2g, 'nt_rows', [(_sds((t, D_FF), BF16), wide)], extras=[(u, wide)],
                     epi=lambda acc, uv: (2.0 * jnp.sqrt(uv.astype(F32)) * acc,))
    tw = _row_tile(t, 512)
    (dw2,) = _mm(f"mlp_dw2{l}", (1, 1, t // tw),
                 u, pl.BlockSpec((tw, D_FF), lambda i, j, k: (k, 0)),
                 dh, pl.BlockSpec((tw, D_MODEL), lambda i, j, k: (k, 0)), (0, 0),
                 [(_sds((D_FF, D_MODEL), BF16), pl.BlockSpec((D_FF, D_MODEL), lambda i, j, k: (0, 0)))])
    dw2 = dw2.reshape(nsh, wsh, D_MODEL)
    (dw1,) = _mm(f"mlp_dw1{l}", (1, 1, t // tw),
                 hn, pl.BlockSpec((tw, D_MODEL), lambda i, j, k: (k, 0)),
                 da, pl.BlockSpec((tw, D_FF), lambda i, j, k: (k, 0)), (0, 0),
                 [(_sds((nsh, D_MODEL, wsh), BF16), pl.BlockSpec((nsh, D_MODEL, wsh), lambda i, j, k: (0, 0, 0)))],
                 split=wsh)
    row = _rows(tm, D_MODEL)
    dh_in, dg = _mm_rows(f"mlp_dhn{l}", tm, da, w1g, 'nt_cols', _norm_bwd_outs(t, tm),
                         extras=[(h, row), (dh, row), (norm_g, VEC)], epi=_norm_bwd_epi)
    return dh_in, jnp.sum(dg, axis=0), dw1, dw2


def _ple_fwd(l, h, hn, p, wg, wp, next_gain, target=None):
    t = h.shape[0]
    tm = _row_tile(t, 512)
    row = pl.BlockSpec((tm, D_MODEL), lambda i, j, k: (i, 0))
    full = lambda r: pl.BlockSpec((r, D_MODEL), lambda i, j, k: (0, 0))
    f32_row, bf_row = (_sds((t, D_MODEL), F32), row), (_sds((t, D_MODEL), BF16), row)
    common = [(h, row), (p, pl.BlockSpec((None, None, tm, PLE_DIM), lambda i, j, k: (l, 0, i, 0))),
              (wp, full(PLE_DIM))]
    if target is not None:
        def loss_epi(acc, hv, pv, wpv, tv):
            gt = _sigmoid(acc)
            ev = _dot(_bf(pv), wpv, 1, 0)
            err = hv + gt * ev - tv
            sq = jnp.sum(jnp.sum(err * err, axis=-1, keepdims=True), axis=0, keepdims=True)
            return err / D_MODEL, gt, ev, jnp.broadcast_to(sq, (8, 128))

        dy, gate, e, sq = _mm(f"ple_gate{l}", (t // tm, 1, 1), hn, row, wg, full(D_MODEL), (1, 0),
                              [f32_row, bf_row, bf_row, (_sds((t // tm, 8, 128), F32),
                                                         pl.BlockSpec((None, 8, 128), lambda i, j, k: (i, 0, 0)))],
                              extras=common + [(target, row)], epi=loss_epi)
        return dy, jnp.sum(sq, axis=0), (h, hn, gate, e)

    def gate_epi(acc, hv, pv, wpv, *gain):
        gt = _sigmoid(acc)
        ev = _dot(_bf(pv), wpv, 1, 0)
        h_new = hv + gt * ev
        if not gain:
            return h_new, gt, ev
        r = lax.rsqrt(jnp.mean(h_new * h_new, axis=-1, keepdims=True) + EPS)
        return h_new, gt, ev, h_new * r * gain[0]

    res = _mm(f"ple_gate{l}", (t // tm, 1, 1), hn, row, wg, full(D_MODEL), (1, 0),
              [f32_row, bf_row, bf_row] + ([bf_row] if next_gain is not None else []),
              extras=common + ([(next_gain, VEC)] if next_gain is not None else []), epi=gate_epi)
    h_out, gate, e = res[0], res[1], res[2]
    return h_out, (res[3] if next_gain is not None else None), (h, hn, gate, e)


def _ple_bwd(l, dh, saved, p, norm_g, wg, deps=()):
    h, hn, gate, e = saved
    t = h.shape[0]
    tm = _row_tile(t)
    tk = _row_tile(t, 512)
    de, dz = _ple_gate_bwd(f"ple_gate_bwd{l}", dh, gate, e)
    full = lambda r: pl.BlockSpec((r, D_MODEL), lambda i, j, k: (0, 0))
    rowk = pl.BlockSpec((tk, D_MODEL), lambda i, j, k: (k, 0))
    (dwp,) = _mm(f"ple_dwp{l}", (1, 1, t // tk),
                 p, pl.BlockSpec((None, None, tk, PLE_DIM), lambda i, j, k: (l, 0, k, 0)),
                 de, rowk, (0, 0), [(_sds((PLE_DIM, D_MODEL), BF16), full(PLE_DIM))], deps=deps)
    (dwg,) = _mm(f"ple_dwg{l}", (1, 1, t // tk), hn, rowk, dz, rowk, (0, 0),
                 [(_sds((D_MODEL, D_MODEL), BF16), full(D_MODEL))])
    row = pl.BlockSpec((tm, D_MODEL), lambda i, j, k: (i, 0))
    dh_in, dg = _mm(f"ple_dhn{l}", (t // tm, 1, 1), dz, row, wg, full(D_MODEL), (1, 1),
                    _norm_bwd_outs(t, tm), extras=[(h, row), (dh, row), (norm_g, VEC)], epi=_norm_bwd_epi)
    return dh_in, jnp.sum(dg, axis=0), dwg, dwp


def _ret_layer_fwd(x, norm_g, wri, fetch_wro, gn, cos, sin, next_gain, hn=None, deps=()):
    t = x.shape[0]
    tm = _row_tile(t)
    nsh, _, wsh = wri.shape
    if hn is None:
        hn = _rms_fwd("mix_norm0", x, norm_g)
    tp = _row_tile(t, 512)
    (proj,) = _mm_rows("ret_in", tp, hn, wri, 'nn_cols', [(_sds((t, RET_IN), BF16), _rows(tp, RET_IN))], deps=deps)
    gated, outp, states = _ret_fwd(proj, cos, sin, gn)
    wro = fetch_wro((gated,))
    row = _rows(tp, D_MODEL)
    more, epi = _residual_epi(next_gain)
    h1, hn_next = _mm_rows("ret_out", tp, gated, wro.reshape(RET_HEADS, RET_DV, D_MODEL), 'nn_rows',
                           _residual_outs(t, row, next_gain), extras=[(x, row)] + more, epi=epi)
    return h1, hn_next, (x, hn, proj, gated, outp, states, wro)


def _ret_layer_bwd(dh, saved, norm_g, wri, gn, cos, sin, emit_out, emit_in, deps=()):
    x, hn, proj, gated, outp, states, wro = saved
    t = x.shape[0]
    tm = _row_tile(t)
    tk = _row_tile(t, 512)
    nsh, _, wsh = wri.shape
    tg = _row_tile(t, 512)
    vw = _rows(tg, RET_V_W)
    dout, dgate, dgn = _mm_rows(
        "ret_dgate", tg, dh, wro.reshape(RET_HEADS, RET_DV, D_MODEL), 'nt_rows',
        [(_sds((t, RET_V_W), BF16), vw), (_sds((t, RET_V_W), BF16), vw),
         (_sds((t // tg, 1, RET_V_W), F32), pl.BlockSpec((None, 1, RET_V_W), lambda i, j, k: (i, 0, 0)))],
        extras=[(outp, vw), (proj, pl.BlockSpec((tg, RET_V_W), lambda i, j, k: (i, (RET_IN - RET_V_W) // RET_V_W))),
                (gn.reshape(1, RET_V_W), pl.BlockSpec((1, RET_V_W), lambda i, j, k: (0, 0)))],
        epi=_ret_gate_bwd_epi, deps=deps)
    dgn = jnp.sum(dgn, axis=0)
    (dwro,) = _mm("ret_dwro", (1, 1, t // tk),
                  gated, pl.BlockSpec((tk, RET_V_W), lambda i, j, k: (k, 0)),
                  dh, pl.BlockSpec((tk, D_MODEL), lambda i, j, k: (k, 0)), (0, 0),
                  [(_sds((RET_V_W, D_MODEL), BF16), pl.BlockSpec((RET_V_W, D_MODEL), lambda i, j, k: (0, 0)))])
    dproj = _ret_bwd(proj, cos, sin, states, dout, dgate, deps=emit_out(dwro))
    half = nsh // 2
    (dwri,) = _mm("ret_dwri", (2, 1, t // tk),
                  hn, pl.BlockSpec((tk, D_MODEL), lambda i, j, k: (k, 0)),
                  dproj, pl.BlockSpec((tk, half * wsh), lambda i, j, k: (k, i)), (0, 0),
                  [(_sds((nsh, D_MODEL, wsh), BF16), pl.BlockSpec((half, D_MODEL, wsh), lambda i, j, k: (i, 0, 0)))],
                  split=wsh)
    deps = emit_in(dwri)
    td = _row_tile(t, 256)
    row = _rows(td, D_MODEL)
    dx, dg = _mm_rows("ret_dhn", td, dproj, wri, 'nt_cols', _norm_bwd_outs(t, td),
                      extras=[(x, row), (dh, row), (norm_g, VEC)], epi=_norm_bwd_epi, deps=deps)
    return dx, jnp.sum(dg, axis=0), dgn.reshape(RET_HEADS, RET_DV)


def _mla_layer_fwd(h, hn, wmi, qa, kva, wuq, wukv, gq, gk, wmo, tabs, next_gain):
    t = h.shape[0]
    tm = _row_tile(t)
    row = pl.BlockSpec((tm, D_MODEL), lambda i, j, k: (i, 0))
    (proj2,) = _mm("mla_in", (t // tm, 1, 1), hn, row,
                   wmi, pl.BlockSpec((D_MODEL, MLA_IN_PAD), lambda i, j, k: (0, 0)), (1, 0),
                   [(_sds((t, MLA_IN_PAD), F32), pl.BlockSpec((tm, MLA_IN_PAD), lambda i, j, k: (i, 0)))])
    cq, ckv = _mla_mid(proj2, qa, kva)
    qh, kh, vh = _mla_prep(cq, ckv, wuq, wukv, proj2, gq, gk, tabs)
    o, lse = _attn_fwd(qh, kh, vh)
    more, epi = _residual_epi(next_gain)
    h_out, hn_next = _mm("mla_out", (t // tm, 1, 1), o, row,
                         wmo, pl.BlockSpec((D_MODEL, D_MODEL), lambda i, j, k: (0, 0)), (1, 0),
                         _residual_outs(t, row, next_gain), extras=[(h, row)] + more, epi=epi)
    return h_out, hn_next, (h, hn, proj2, cq, ckv, qh, kh, vh, o, lse)


def _mla_layer_bwd(dh, saved, norm_g, wmi, qa, kva, wuq, wukv, gq, gk, wmo, tabs, deps=()):
    h, hn, proj2, cq, ckv, qh, kh, vh, o, lse = saved
    t = h.shape[0]
    tm = _row_tile(t)
    tk = _row_tile(t, 512)
    row = pl.BlockSpec((tm, D_MODEL), lambda i, j, k: (i, 0))
    rowk = pl.BlockSpec((tk, D_MODEL), lambda i, j, k: (k, 0))
    sq = pl.BlockSpec((D_MODEL, D_MODEL), lambda i, j, k: (0, 0))
    do, dob = _mm("mla_do", (t // tm, 1, 1), dh, row, wmo, sq, (1, 1),
                  [(_sds((t, D_MODEL), F32), row), (_sds((t, D_MODEL), BF16), row)], epi=lambda acc: (acc, acc),
                  deps=deps)
    (dwmo,) = _mm("mla_dwo", (1, 1, t // tk), o, rowk, dh, rowk, (0, 0), [(_sds((D_MODEL, D_MODEL), BF16), sq)])
    delta = _attn_delta(do, o, lse.shape[-1])
    dqt, dkh, dvh = _attn_bwd(qh, kh, vh, dob, lse, delta)
    dq, dkv, dkr, dgq, dgk = _mla_prep_bwd(cq, ckv, wuq, wukv, proj2, gq, gk, tabs, dqt, dkh, dvh)

    wide = MLA_HEADS * MLA_HD_PAD
    widek = pl.BlockSpec((tk, wide), lambda i, j, k: (k, 0))
    (dwuq,) = _mm("mla_dwuq", (1, 1, t // tk),
                  cq, pl.BlockSpec((tk, MLA_Q_RANK), lambda i, j, k: (k, 0)), dq, widek, (0, 0),
                  [(_sds((MLA_HEADS, MLA_Q_RANK, MLA_HD_PAD), BF16),
                    pl.BlockSpec((MLA_HEADS, MLA_Q_RANK, MLA_HD_PAD), lambda i, j, k: (0, 0, 0)))], split=MLA_HD_PAD)
    (dwukv,) = _mm("mla_dwukv", (1, 1, t // tk),
                   ckv, pl.BlockSpec((tk, MLA_KV_RANK), lambda i, j, k: (k, 0)), dkv, widek, (0, 0),
                   [(_sds((MLA_HEADS, MLA_KV_RANK, MLA_HD_PAD), BF16),
                     pl.BlockSpec((MLA_HEADS, MLA_KV_RANK, MLA_HD_PAD), lambda i, j, k: (0, 0, 0)))],
                   split=MLA_HD_PAD)
    side_by_side = lambda wg: wg.transpose(1, 0, 2).reshape(wg.shape[1], wide)
    widei = pl.BlockSpec((tm, wide), lambda i, j, k: (i, 0))
    (dcq,) = _mm("mla_dcq", (t // tm, 1, 1), dq, widei,
                 side_by_side(wuq), pl.BlockSpec((MLA_Q_RANK, wide), lambda i, j, k: (0, 0)), (1, 1),
                 [(_sds((t, MLA_Q_RANK), F32), pl.BlockSpec((tm, MLA_Q_RANK), lambda i, j, k: (i, 0)))])
    (dckv,) = _mm("mla_dckv", (t // tm, 1, 1), dkv, widei,
                  side_by_side(wukv), pl.BlockSpec((MLA_KV_RANK, wide), lambda i, j, k: (0, 0)), (1, 1),
                  [(_sds((t, MLA_KV_RANK), F32), pl.BlockSpec((tm, MLA_KV_RANK), lambda i, j, k: (i, 0)))])
    dproj2, dqa, dkva = _mla_mid_bwd(proj2, qa, kva, dcq, dckv, dkr)
    win = pl.BlockSpec((D_MODEL, MLA_IN_PAD), lambda i, j, k: (0, 0))
    (dwmi,) = _mm("mla_dwin", (1, 1, t // tk), hn, rowk,
                  dproj2, pl.BlockSpec((tk, MLA_IN_PAD), lambda i, j, k: (k, 0)), (0, 0),
                  [(_sds((D_MODEL, MLA_IN_PAD), BF16), win)])
    dh_in, dg = _mm("mla_dhn", (t // tm, 1, 1),
                    dproj2, pl.BlockSpec((tm, MLA_IN_PAD), lambda i, j, k: (i, 0)), wmi, win, (1, 1),
                    _norm_bwd_outs(t, tm), extras=[(h, row), (dh, row), (norm_g, VEC)], epi=_norm_bwd_epi)
    return dh_in, dict(mix=jnp.sum(dg, axis=0), wmi=dwmi, qa=dqa, kva=dkva, wuq=dwuq, wukv=dwukv, gq=dgq, gk=dgk,
                       wmo=dwmo)


def _local_step(x, p, target, w, fetch, emit=lambda group: ()):
    t = x.shape[0]
    inv = 1.0 / (ROPE_THETA ** (jnp.arange(0, RET_DK, 2, dtype=F32) / RET_DK))
    ang = jnp.arange(t, dtype=F32)[:, None] * inv[None, :]
    cos_r, sin_r = jnp.cos(ang), jnp.sin(ang)
    tabs = _mla_tables(t)
    row = lambda a, i: a[i:i + 1]

    h1, hn1, s_ret = _ret_layer_fwd(x, row(w['mix_norm'], 0), w['ret_w_in'],
                                    lambda after: fetch('ret_out', after)['ret_w_out'], w['ret_gn'], cos_r, sin_r,
                                    row(w['mlp_norm'], 0), hn=w.get('hn0'), deps=w['deps'])
    h2, hn2, s_mlp0 = _mlp_fwd(0, h1, hn1, fetch('mlp_w1_0', (h1,))['mlp_w1'],
                               lambda after: fetch('mlp_w2_0', after)['mlp_w2'], row(w['ple_norm'], 0))
    w0 = fetch('ple_0', (h2,))
    h3, hn3, s_ple0 = _ple_fwd(0, h2, hn2, p, w0['ple_gate_w'], w0['ple_proj_w'], row(w['mix_norm'], 1))
    wm = fetch('mla', (h3,))
    mla_w = (wm['mla_w_in'], w['mla_q_a_norm'], w['mla_kv_a_norm'], wm['mla_w_uq'], wm['mla_w_ukv'],
             w['mla_q_norm'], w['mla_k_norm'], wm['mla_w_out'], tabs)
    h4, hn4, s_mla = _mla_layer_fwd(h3, hn3, *mla_w, row(w['mlp_norm'], 1))
    w1 = fetch('layer_1', (h4,))
    h5, hn5, s_mlp1 = _mlp_fwd(1, h4, hn4, w1['mlp_w1'], lambda after: w1['mlp_w2'], row(w['ple_norm'], 1))
    dy, sq_err, s_ple1 = _ple_fwd(1, h5, hn5, p, w1['ple_gate_w'], w1['ple_proj_w'], None, target)

    n = N_DEV
    colsh = lambda a: a.reshape(a.shape[0], n, a.shape[1] // n).transpose(1, 0, 2)
    rowsh = lambda a: a.reshape(n, a.shape[0] // n, a.shape[1])
    big = {}

    def emit_group(group):
        big.update(group)
        return emit(group)

    dh5, dg_ple1, dwg1, dwp1 = _ple_bwd(1, dy, s_ple1, p, row(w['ple_norm'], 1), w1['ple_gate_w'])
    dh4, dg_mlp1, dw1_1, dw2_1 = _mlp_bwd(1, dh5, s_mlp1, row(w['mlp_norm'], 1))
    deps = emit_group({('ple_gate_w', 1): rowsh(dwg1), ('ple_proj_w', 1): colsh(dwp1),
                       ('mlp_w2', 1): dw2_1, ('mlp_w1', 1): dw1_1})
    dh3, gm = _mla_layer_bwd(dh4, s_mla, row(w['mix_norm'], 1), *mla_w, deps=deps)
    deps = emit_group({('mla_w_out', 0): rowsh(gm['wmo']), ('mla_w_uq', 0): gm['wuq'][:, :, :MLA_QKD],
                       ('mla_w_ukv', 0): gm['wukv'], ('mla_w_in', 0): rowsh(gm['wmi'][:, :MLA_IN])})
    dh2, dg_ple0, dwg0, dwp0 = _ple_bwd(0, dh3, s_ple0, p, row(w['ple_norm'], 0), w0['ple_gate_w'], deps=deps)
    dh1, dg_mlp0, dw1_0, dw2_0 = _mlp_bwd(0, dh2, s_mlp0, row(w['mlp_norm'], 0))
    deps = emit_group({('ple_gate_w', 0): rowsh(dwg0), ('ple_proj_w', 0): colsh(dwp0),
                       ('mlp_w2', 0): dw2_0, ('mlp_w1', 0): dw1_0})
    dx, dg_mix0, dgn = _ret_layer_bwd(
        dh1, s_ret, row(w['mix_norm'], 0), w['ret_w_in'], w['ret_gn'], cos_r, sin_r,
        lambda dwro: emit_group({('ret_w_out', 0): rowsh(dwro)}),
        lambda dwri: emit_group({('ret_w_in', 0): dwri}), deps=deps)

    small = dict(
        mix_norm=[dg_mix0, gm['mix']], mlp_norm=[dg_mlp0, dg_mlp1], ple_norm=[dg_ple0, dg_ple1],
        ret_gn=dgn, mla_q_a_norm=gm['qa'], mla_kv_a_norm=gm['kva'], mla_q_norm=gm['gq'], mla_k_norm=gm['gk'],
    )
    return sq_err, dx, big, small


def _my_place():
    x, y, c = lax.axis_index("x"), lax.axis_index("y"), lax.axis_index("c")
    return x, y, c


def _flat(px, py, pc):
    return 4 * px + 2 * py + pc


def _peer(x, y, c, r):
    return (1 - x if r & 4 else x, 1 - y if r & 2 else y, 1 - c if r & 1 else c)


def _all_gather(arrays):
    n = len(arrays)

    def body(*refs):
        ins, outs = refs[:n], refs[n:2 * n]
        send_sems, recv_sems, local_sems = refs[2 * n:]
        x, y, c = _my_place()
        me, sibling = (x, y, c), (x, y, 1 - c)
        chips = [(1 - x, y), (x, 1 - y), (1 - x, 1 - y)]

        def copy(a, k, block, to, src=None):
            slot = outs[a].at[_flat(*block)]
            return pltpu.make_async_remote_copy(
                src_ref=slot if src is None else src, dst_ref=slot,
                send_sem=send_sems.at[a, k], recv_sem=recv_sems.at[a, k], device_id=to, device_id_type=MESH)

        mine = [pltpu.make_async_copy(ins[a], outs[a].at[_flat(*me)], local_sems.at[a]) for a in range(n)]
        for cp in mine:
            cp.start()
        first = []
        for a in range(n):
            first.append(copy(a, 0, me, sibling, src=ins[a]))
            first += [copy(a, 1 + j, me, (*chip, c), src=ins[a]) for j, chip in enumerate(chips)]
        for cp in first:
            cp.start()
        passed = []
        for a in range(n):
            for j, chip in enumerate(chips):
                copy(a, 1 + j, (*chip, c), me).wait_recv()
                passed.append(copy(a, 4 + j, (*chip, c), sibling))
                passed[-1].start()
        for a in range(n):
            copy(a, 0, sibling, me).wait_recv()
            for j, chip in enumerate(chips):
                copy(a, 4 + j, (*chip, 1 - c), me).wait_recv()
        for cp in first + passed:
            cp.wait_send()
        for cp in mine:
            cp.wait()

    return pl.pallas_call(
        body, name="all_gather_weights",
        in_specs=[ANY] * n, out_specs=[ANY] * n,
        out_shape=[_sds((N_DEV,) + a.shape, a.dtype) for a in arrays],
        scratch_shapes=[pltpu.SemaphoreType.DMA((n, 7)), pltpu.SemaphoreType.DMA((n, 7)),
                        pltpu.SemaphoreType.DMA((n,))],
    )(*arrays)


HBM = pl.BlockSpec(memory_space=pltpu.HBM)
SEMS = pl.BlockSpec(memory_space=pltpu.SEMAPHORE)
SIDE_EFFECT = pltpu.SideEffectType.DATAFLOW_SIDE_EFFECTING


def _rs_copies(x, y, c, srcs, lands, send_sems, recv_sems):
    copies = []
    for a in range(len(srcs)):
        for r in range(1, N_DEV):
            peer = _peer(x, y, c, r)
            k = a * (N_DEV - 1) + r - 1
            copies.append(pltpu.make_async_remote_copy(
                src_ref=srcs[a].at[_flat(*peer)], dst_ref=lands[a].at[r - 1],
                send_sem=send_sems.at[k], recv_sem=recv_sems.at[k], device_id=peer, device_id_type=MESH))
    return copies


def _rs_start(name, arrays):
    n = len(arrays)
    hbm = lambda a: pltpu.with_memory_space_constraint(a, pltpu.HBM)
    lands = [hbm(lax.empty((N_DEV - 1,) + a.shape[1:], a.dtype)) for a in arrays]

    def body(*refs):
        srcs, lnd = refs[:n], refs[n:2 * n]
        send_sems, recv_sems = refs[2 * n], refs[2 * n + 1]
        token = refs[-1]
        for cp in _rs_copies(*_my_place(), srcs, lnd, send_sems, recv_sems):
            cp.start()
        token[...] = jnp.zeros_like(token)

    outs = pl.pallas_call(
        body, name=name,
        in_specs=[HBM] * (2 * n),
        out_specs=[SEMS, SEMS] + [HBM] * (2 * n) + [pl.BlockSpec(memory_space=pltpu.VMEM)],
        out_shape=[pltpu.SemaphoreType.DMA((n * (N_DEV - 1),)), pltpu.SemaphoreType.DMA((n * (N_DEV - 1),))]
        + [pltpu.HBM(a.shape, a.dtype) for a in arrays] + [pltpu.HBM(l.shape, l.dtype) for l in lands]
        + [_sds((8, 128), F32)],
        input_output_aliases={i: 2 + i for i in range(2 * n)},
        compiler_params=pltpu.CompilerParams(has_side_effects=SIDE_EFFECT),
    )(*[hbm(a) for a in arrays], *lands)
    return outs[0], outs[1], outs[2:2 + n], outs[2 + n:2 + 2 * n], outs[-1]


def _rs_wait(name, send_sems, recv_sems, srcs, lands, after):
    n = len(srcs)

    def body(*refs):
        src_refs, lnd = refs[:n], refs[n:2 * n]
        send, recv = refs[2 * n], refs[2 * n + 1]
        for cp in _rs_copies(*_my_place(), src_refs, lnd, send, recv):
            cp.wait_send()
            cp.wait_recv()

    outs = pl.pallas_call(
        body, name=name,
        in_specs=[HBM] * (2 * n) + [SEMS, SEMS] + [ANY] * len(after),
        out_specs=[HBM] * (2 * n),
        out_shape=[pltpu.HBM(a.shape, a.dtype) for a in list(srcs) + list(lands)],
        input_output_aliases={i: i for i in range(2 * n)},
        compiler_params=pltpu.CompilerParams(has_side_effects=SIDE_EFFECT),
    )(*srcs, *lands, send_sems, recv_sems, *after)
    return outs[:n], outs[n:]


SMALL_PACK_ROWS = 16


def _all_reduce_small(rows, deps=()):
    n = len(rows)

    def body(*refs):
        ins = refs[:n]
        out_ref, mine, buf, send_sems, recv_sems = refs[n + len(deps):]
        x, y, c = _my_place()
        mine[...] = jnp.zeros_like(mine)
        for (r0, a), ref in zip(rows, ins):
            mine[r0:r0 + a.shape[0], 0:a.shape[1]] = ref[...]
        buf[_flat(x, y, c)] = mine[...]
        copies = []
        for r in range(1, N_DEV):
            peer = _peer(x, y, c, r)
            send = pltpu.make_async_remote_copy(
                src_ref=mine, dst_ref=buf.at[_flat(x, y, c)],
                send_sem=send_sems.at[r - 1], recv_sem=recv_sems.at[r - 1], device_id=peer, device_id_type=MESH)
            send.start()
            recv = pltpu.make_async_remote_copy(
                src_ref=mine, dst_ref=buf.at[_flat(*peer)],
                send_sem=send_sems.at[r - 1], recv_sem=recv_sems.at[r - 1], device_id=peer, device_id_type=MESH)
            copies.append((send, recv))
        for send, recv in copies:
            send.wait_send()
            recv.wait_recv()
        acc = buf[0]
        for s in range(1, N_DEV):
            acc = acc + buf[s]
        out_ref[...] = acc

    vm = pl.BlockSpec(memory_space=pltpu.VMEM)
    shape = (SMALL_PACK_ROWS, D_MODEL)
    return pl.pallas_call(
        body, name="all_reduce_small", in_specs=[vm] * n + [ANY] * len(deps), out_specs=vm,
        out_shape=_sds(shape, F32),
        scratch_shapes=[pltpu.VMEM(shape, F32), pltpu.VMEM((N_DEV,) + shape, F32),
                        pltpu.SemaphoreType.DMA((7,)), pltpu.SemaphoreType.DMA((7,))],
    )(*[a for _, a in rows], *deps)


def _adamw_math(w, g, m, v):
    m = ADAM_B1 * m + (1.0 - ADAM_B1) * g
    v = ADAM_B2 * v + (1.0 - ADAM_B2) * (g * g)
    m_hat = m / (1.0 - ADAM_B1 ** ADAM_STEP)
    v_hat = v / (1.0 - ADAM_B2 ** ADAM_STEP)
    delta = -ADAM_LR * (m_hat / (jnp.sqrt(v_hat) + ADAM_EPS) + ADAM_WD * w)
    return delta, m, v


def _adamw_big(name, w, m, v, srcs, lands, me):
    nl, rows, cols = w.shape
    tr = next(cand for cand in (256, 128, 64, 32, 16, 8) if rows % cand == 0)

    def body(me_ref, w_ref, m_ref, v_ref, *rest):
        src_refs, land_refs = rest[:nl], rest[nl:2 * nl]
        g_ref, d_ref, mo_ref, vo_ref = rest[2 * nl:]
        for layer in range(nl):
            @pl.when(pl.program_id(0) == layer)
            def _():
                g = src_refs[layer][...].astype(F32)
                for s in range(N_DEV - 1):
                    g = g + land_refs[layer][s].astype(F32)
                delta, mn, vn = _adamw_math(w_ref[...], g, m_ref[...], v_ref[...])
                g_ref[...] = g
                d_ref[...] = delta
                mo_ref[...] = mn
                vo_ref[...] = vn

    blk = pl.BlockSpec((None, tr, cols), lambda l, i, me_ref: (l, i, 0))
    at = lambda layer, l, i: jnp.where(l == layer, i, 0)
    own = [pl.BlockSpec((None, tr, cols), functools.partial(lambda layer, l, i, me_ref: (me_ref[0], at(layer, l, i), 0),
                                                            layer)) for layer in range(nl)]
    peers = [pl.BlockSpec((N_DEV - 1, tr, cols), functools.partial(lambda layer, l, i, me_ref: (0, at(layer, l, i), 0),
                                                                   layer)) for layer in range(nl)]
    return pl.pallas_call(
        body, name=name,
        grid_spec=pltpu.PrefetchScalarGridSpec(
            num_scalar_prefetch=1, grid=(nl, rows // tr),
            in_specs=[blk, blk, blk] + own + peers, out_specs=[blk] * 4),
        out_shape=[_sds((nl, rows, cols), F32)] * 4,
        compiler_params=_cparams(("arbitrary", "arbitrary")),
    )(me, w, m, v, *srcs, *lands)


def _adamw_small(ws, gs, ms, vs):
    n = len(ws)

    def body(*refs):
        w_refs, g_refs, m_refs, v_refs = (refs[i * n:(i + 1) * n] for i in range(4))
        d_out, m_out, v_out = (refs[(4 + i) * n:(5 + i) * n] for i in range(3))
        for i in range(n):
            delta, mn, vn = _adamw_math(w_refs[i][...], g_refs[i][...], m_refs[i][...], v_refs[i][...])
            d_out[i][...] = delta
            m_out[i][...] = mn
            v_out[i][...] = vn

    vm = pl.BlockSpec(memory_space=pltpu.VMEM)
    outs = pl.pallas_call(
        body, name="adamw_small", in_specs=[vm] * (4 * n), out_specs=[vm] * (3 * n),
        out_shape=[_sds(a.shape, F32) for a in ws] * 3,
    )(*ws, *gs, *ms, *vs)
    return outs[:n], outs[n:2 * n], outs[2 * n:]


SMALL_ROWS = 16


def _pad_to(a, rows, cols):
    return jnp.pad(a, ((0, rows - a.shape[0]), (0, cols - a.shape[1])))


def _place_own(blocks):
    me = _flat(*_my_place())
    return [lax.dynamic_update_slice(lax.empty((N_DEV,) + b.shape, b.dtype), b[None], (me,) + (0,) * b.ndim)
            for b in blocks]


def _ag_copies(x, y, c, blocks, bufs, send_sems, recv_sems):
    sends, recvs = [], []
    for a in range(len(blocks)):
        for r in range(1, N_DEV):
            peer = _peer(x, y, c, r)
            k = a * (N_DEV - 1) + r - 1
            make = lambda place: pltpu.make_async_remote_copy(
                src_ref=blocks[a], dst_ref=bufs[a].at[_flat(*place)],
                send_sem=send_sems.at[k], recv_sem=recv_sems.at[k], device_id=peer, device_id_type=MESH)
            sends.append(make((x, y, c)))
            recvs.append(make(peer))
    return sends, recvs


def _ag_start(groups, after):
    flat = [pair for g in groups for pair in g]
    n, ng = len(flat), len(groups)
    hbm = lambda a: pltpu.with_memory_space_constraint(a, pltpu.HBM)

    def body(*refs):
        blocks, bufs = refs[:n], refs[n:2 * n]
        sems = refs[2 * n + len(after):2 * n + len(after) + 2 * ng]
        x, y, c = _my_place()
        at = 0
        for gi, g in enumerate(groups):
            sends, _ = _ag_copies(x, y, c, blocks[at:at + len(g)], bufs[at:at + len(g)], sems[2 * gi], sems[2 * gi + 1])
            for cp in sends:
                cp.start()
            at += len(g)
        refs[-1][...] = jnp.zeros_like(refs[-1])

    sem_shapes = [pltpu.SemaphoreType.DMA((len(g) * (N_DEV - 1),)) for g in groups for _ in range(2)]
    outs = pl.pallas_call(
        body, name="gather_start",
        in_specs=[HBM] * (2 * n) + [ANY] * len(after),
        out_specs=[SEMS] * (2 * ng) + [HBM] * (2 * n) + [pl.BlockSpec(memory_space=pltpu.VMEM)],
        out_shape=sem_shapes + [pltpu.HBM(b.shape, b.dtype) for b, _ in flat]
        + [pltpu.HBM(u.shape, u.dtype) for _, u in flat] + [_sds((8, 128), F32)],
        input_output_aliases={i: 2 * ng + i for i in range(2 * n)},
        compiler_params=pltpu.CompilerParams(has_side_effects=SIDE_EFFECT),
    )(*[hbm(b) for b, _ in flat], *[hbm(u) for _, u in flat], *after)
    blocks_thru, bufs_thru = outs[2 * ng:2 * ng + n], outs[2 * ng + n:2 * ng + 2 * n]
    started, at = [], 0
    for gi, g in enumerate(groups):
        started.append((outs[2 * gi], outs[2 * gi + 1], blocks_thru[at:at + len(g)], bufs_thru[at:at + len(g)]))
        at += len(g)
    return started, outs[-1]


def _ag_wait(name, send_sems, recv_sems, blocks, bufs, after):
    n = len(blocks)

    def body(*refs):
        sends, recvs = _ag_copies(*_my_place(), refs[:n], refs[n:2 * n], refs[2 * n], refs[2 * n + 1])
        for s, r in zip(sends, recvs):
            s.wait_send()
            r.wait_recv()

    outs = pl.pallas_call(
        body, name=name,
        in_specs=[HBM] * (2 * n) + [SEMS, SEMS] + [ANY] * len(after),
        out_specs=[HBM] * (2 * n),
        out_shape=[pltpu.HBM(a.shape, a.dtype) for a in list(blocks) + list(bufs)],
        input_output_aliases={i: i for i in range(2 * n)},
        compiler_params=pltpu.CompilerParams(has_side_effects=SIDE_EFFECT),
    )(*blocks, *bufs, send_sems, recv_sems, *after)
    return outs[n:]


def _split_call(name, body, thru, sems_in, new_sems, after):
    n, ns, nn = len(thru), len(sems_in), len(new_sems)
    hbm = lambda a: pltpu.with_memory_space_constraint(a, pltpu.HBM)

    def wrapped(*refs):
        body(refs[:n], refs[n:n + ns], refs[n + ns + len(after):n + ns + len(after) + nn])
        refs[-1][...] = jnp.zeros_like(refs[-1])

    outs = pl.pallas_call(
        wrapped, name=name,
        in_specs=[HBM] * n + [SEMS] * ns + [ANY] * len(after),
        out_specs=[SEMS] * nn + [HBM] * n + [pl.BlockSpec(memory_space=pltpu.VMEM)],
        out_shape=[pltpu.SemaphoreType.DMA((k,)) for k in new_sems] + [pltpu.HBM(a.shape, a.dtype) for a in thru]
        + [_sds((8, 128), F32)],
        input_output_aliases={i: nn + i for i in range(n)},
        compiler_params=pltpu.CompilerParams(has_side_effects=SIDE_EFFECT),
    )(*[hbm(a) for a in thru], *sems_in, *after)
    return list(outs[:nn]), list(outs[nn:nn + n]), outs[-1]


def _first_gather(blocks, bufs, overlap):
    n = len(blocks)

    def copies(refs, s1, r1, s2, r2):
        x, y, c = _my_place()
        me, sibling = (x, y, c), (x, y, 1 - c)
        chips = [(1 - x, y), (x, 1 - y), (1 - x, 1 - y)]
        blk, buf = refs[:n], refs[n:]
        out = dict(send1=[], recv1_sib=[], recv1_ici=[], send2=[], recv2=[])
        for a in range(n):
            place = lambda dev: buf[a].at[_flat(*dev)]
            for k, to in enumerate([sibling] + [(*chip, c) for chip in chips]):
                mk = lambda dst: pltpu.make_async_remote_copy(
                    src_ref=blk[a], dst_ref=dst, send_sem=s1.at[4 * a + k], recv_sem=r1.at[4 * a + k],
                    device_id=to, device_id_type=MESH)
                out['send1'].append(mk(place(me)))
                out['recv1_sib' if k == 0 else 'recv1_ici'].append(mk(place(to)))
            for j, chip in enumerate(chips):
                mk = lambda dev: pltpu.make_async_remote_copy(
                    src_ref=place(dev), dst_ref=place(dev), send_sem=s2.at[3 * a + j], recv_sem=r2.at[3 * a + j],
                    device_id=sibling, device_id_type=MESH)
                out['send2'].append(mk((*chip, c)))
                out['recv2'].append(mk((*chip, 1 - c)))
        return out

    def start(refs, sems_in, new):
        for cp in copies(refs, new[0], new[1], new[0], new[1])['send1']:
            cp.start()

    def forward(refs, sems_in, new):
        cps = copies(refs, sems_in[0], sems_in[1], new[0], new[1])
        for cp in cps['recv1_ici']:
            cp.wait_recv()
        for cp in cps['send2']:
            cp.start()

    def finish(refs, sems_in, new):
        cps = copies(refs, *sems_in)
        for cp in cps['recv1_sib'] + cps['recv2']:
            cp.wait_recv()
        for cp in cps['send1'] + cps['send2']:
            cp.wait_send()

    sems1, thru, token = _split_call("first_gather_start", start, list(blocks) + list(bufs), [], [4 * n, 4 * n], ())
    after = overlap(token)
    sems2, thru, token = _split_call("first_gather_forward", forward, thru, sems1, [3 * n, 3 * n], after)
    _, thru, _ = _split_call("first_gather_wait", finish, thru, sems1 + sems2, [], ())
    return thru[n:], token


def _prepare_weights(p, x):
    n = N_DEV
    bf = lambda a: a.astype(BF16)
    gn_pack = jnp.concatenate([
        _pad_to(p['ret_gn'][0], RET_HEADS, 128), _pad_to(p['mla_q_a_norm'], 1, 128),
        _pad_to(p['mla_kv_a_norm'], 1, 128), jnp.zeros((2, 128), F32)], axis=0)
    ple = lambda l: [bf(p['ple_gate_w'][l]), bf(p['ple_proj_w'][l])]
    names = ('ret_out', 'mlp_w1_0', 'mlp_w2_0', 'ple_0', 'mla', 'layer_1')
    later = [[bf(p['ret_w_out'][0])], [bf(p['mlp_w1'][0])], [bf(p['mlp_w2'][0])], ple(0),
             [bf(p['mla_w_in'][0]), bf(p['mla_w_uq'][0]), bf(p['mla_w_ukv'][0]), bf(p['mla_w_out'][0])],
             [bf(p['mlp_w1'][1]), bf(p['mlp_w2'][1])] + ple(1)]
    first = [gn_pack, bf(p['ret_w_in'][0])]
    behind = {}

    def overlap(token):
        behind['hn0'] = _rms_fwd("mix_norm0", x, p['mix_norm'][0:1], deps=(token,))
        behind['bufs'] = _place_own([b for g in later for b in g])
        return (behind['hn0'], *behind['bufs'])

    (pack, wri), token = _first_gather(first, _place_own(first), overlap)
    bufs = behind['bufs']
    groups, at = [], 0
    for g in later:
        groups.append(list(zip(g, bufs[at:at + len(g)])))
        at += len(g)
    started, token = _ag_start(groups, (token,))

    w = {k: p[k] for k in ('mix_norm', 'mlp_norm', 'ple_norm')}
    w['hn0'] = behind['hn0']
    w['ret_gn'] = pack[:, :RET_HEADS, :RET_DV // n].transpose(1, 0, 2).reshape(RET_HEADS, RET_DV)
    w['mla_q_a_norm'] = pack[:, RET_HEADS, :MLA_Q_RANK // n].reshape(1, MLA_Q_RANK)
    w['mla_kv_a_norm'] = pack[:, RET_HEADS + 1, :MLA_KV_RANK // n].reshape(1, MLA_KV_RANK)
    w['ret_w_in'] = wri
    w['mla_q_norm'] = _pad_to(p['mla_q_norm'], 1, MLA_HD_PAD)
    w['mla_k_norm'] = _pad_to(p['mla_k_norm'], 1, MLA_HD_PAD)
    w['deps'] = (token,)

    def fetch(name, after):
        got = list(_ag_wait("gather_wait_" + name, *started[names.index(name)], after))
        if name == 'ret_out':
            return dict(ret_w_out=got[0].reshape(RET_V_W, D_MODEL))
        if name == 'mla':
            wmi, wuq, wukv, wmo = got
            return dict(mla_w_in=jnp.pad(wmi.reshape(D_MODEL, MLA_IN), ((0, 0), (0, MLA_IN_PAD - MLA_IN))),
                        mla_w_uq=jnp.pad(wuq, ((0, 0), (0, 0), (0, MLA_HD_PAD - MLA_QKD))),
                        mla_w_ukv=wukv, mla_w_out=wmo.reshape(D_MODEL, D_MODEL))
        out = {}
        if name in ('mlp_w1_0', 'layer_1'):
            out['mlp_w1'] = got.pop(0)
        if name in ('mlp_w2_0', 'layer_1'):
            out['mlp_w2'] = got.pop(0)
        if name in ('ple_0', 'layer_1'):
            out['ple_gate_w'] = got[0].reshape(D_MODEL, D_MODEL)
            out['ple_proj_w'] = got[1].transpose(1, 0, 2).reshape(PLE_DIM, D_MODEL)
        return out

    return w, fetch


def _small_grads(small, after):
    rows = [(0, small['mix_norm'][0]), (1, small['mix_norm'][1]), (2, small['mlp_norm'][0]),
            (3, small['mlp_norm'][1]), (4, small['ple_norm'][0]), (5, small['ple_norm'][1]),
            (6, small['ret_gn']), (10, small['mla_q_a_norm']), (11, small['mla_kv_a_norm']),
            (12, small['mla_q_norm']), (13, small['mla_k_norm'])]
    gs = _all_reduce_small(rows, after)
    me = _flat(*_my_place())
    n = N_DEV
    return dict(
        mix_norm=gs[0:2], mlp_norm=gs[2:4], ple_norm=gs[4:6],
        ret_gn=lax.dynamic_slice(gs, (6, me * (RET_DV // n)), (RET_HEADS, RET_DV // n)),
        mla_q_a_norm=lax.dynamic_slice(gs, (10, me * (MLA_Q_RANK // n)), (1, MLA_Q_RANK // n)),
        mla_kv_a_norm=lax.dynamic_slice(gs, (11, me * (MLA_KV_RANK // n)), (1, MLA_KV_RANK // n)),
        mla_q_norm=gs[12:13, :MLA_QKD], mla_k_norm=gs[13:14, :MLA_QKD])


def kernel(x, p, mix_norm, ret_w_in, ret_gn, ret_w_out, mla_w_in, mla_q_a_norm, mla_kv_a_norm, mla_w_uq, mla_w_ukv, mla_q_norm, mla_k_norm, mla_w_out, mlp_norm, mlp_w1, mlp_w2, ple_norm, ple_gate_w, ple_proj_w, loss_target, m_mix_norm, m_ret_w_in, m_ret_gn, m_ret_w_out, m_mla_w_in, m_mla_q_a_norm, m_mla_kv_a_norm, m_mla_w_uq, m_mla_w_ukv, m_mla_q_norm, m_mla_k_norm, m_mla_w_out, m_mlp_norm, m_mlp_w1, m_mlp_w2, m_ple_norm, m_ple_gate_w, m_ple_proj_w, v_mix_norm, v_ret_w_in, v_ret_gn, v_ret_w_out, v_mla_w_in, v_mla_q_a_norm, v_mla_kv_a_norm, v_mla_w_uq, v_mla_w_ukv, v_mla_q_norm, v_mla_k_norm, v_mla_w_out, v_mlp_norm, v_mlp_w1, v_mlp_w2, v_ple_norm, v_ple_gate_w, v_ple_proj_w):
    given = dict(locals())
    params = {n: given[n] for n in WEIGHTS}
    w, fetch = _prepare_weights(params, x[0])

    started = []

    def emit(group):
        keys = list(group)
        send, recv, srcs, lands, token = _rs_start(f"rs_start{len(started)}", [group[k] for k in keys])
        started.append((keys, send, recv, srcs, lands))
        return (token,)

    sq_err, grad_x, _, small = _local_step(x[0], p, loss_target[0], w, fetch, emit)
    loss = lax.psum(0.5 / D_MODEL * sq_err[0, 0], ("x", "y", "c"))

    grads, deltas, new_m, new_v = {}, {}, {}, {}

    def small_updates(after):
        sg = _small_grads(small, after)
        two_d = lambda a: a.reshape(-1, a.shape[-1])
        d_s, m_s, v_s = _adamw_small(
            [two_d(params[n]) for n in SMALL], [sg[n] for n in SMALL],
            [two_d(given["m_" + n]) for n in SMALL], [two_d(given["v_" + n]) for n in SMALL])
        for i, n in enumerate(SMALL):
            shape = params[n].shape
            grads[n], deltas[n], new_m[n], new_v[n] = (a.reshape(shape) for a in (sg[n], d_s[i], m_s[i], v_s[i]))
        return (d_s[0],)

    me = _flat(*_my_place()).astype(jnp.int32).reshape(1)
    after = (grad_x,)
    src_of, land_of = {}, {}
    for gi, (keys, send, recv, srcs, lands) in enumerate(started):
        if gi == len(started) - 1:
            after = small_updates(after)
        srcs, lands = _rs_wait(f"rs_wait{gi}", send, recv, srcs, lands, after)
        for k, s, l in zip(keys, srcs, lands):
            src_of[k], land_of[k] = s, l
        done = [n for n in BIG if n not in grads and all((n, l) in src_of for l in range(params[n].shape[0]))]
        for n in done:
            layers = range(params[n].shape[0])
            grads[n], deltas[n], new_m[n], new_v[n] = _adamw_big(
                "adamw_" + n, params[n], given["m_" + n], given["v_" + n],
                [src_of[(n, l)] for l in layers], [land_of[(n, l)] for l in layers], me)
        if done:
            after = tuple(deltas[n] for n in done)

    return (loss, grad_x[None], *[grads[n] for n in WEIGHTS], *[deltas[n] for n in WEIGHTS],
            *[new_m[n] for n in WEIGHTS], *[new_v[n] for n in WEIGHTS])
```

```python
import functools

import jax
import jax.numpy as jnp
from jax import lax
from jax.experimental import pallas as pl
from jax.experimental.pallas import tpu as pltpu

F32 = jnp.float32
BF16 = jnp.bfloat16
MESH = pl.DeviceIdType.MESH
ANY = pl.BlockSpec(memory_space=pl.ANY)

N_DEV = 8
D_MODEL = 1024
CHUNK = 64
EPS = 1e-6
ROPE_THETA = 10000.0
RET_HEADS = 4
RET_DK = 256
RET_DV = 512
RET_QK_W = RET_HEADS * RET_DK
RET_V_W = RET_HEADS * RET_DV
RET_IN = 2 * RET_QK_W + 2 * RET_V_W
MLA_HEADS = 8
MLA_NOPE = 128
MLA_ROPE = 64
MLA_QKD = MLA_NOPE + MLA_ROPE
MLA_VD = 128
MLA_Q_RANK = 384
MLA_KV_RANK = 256
MLA_IN = MLA_Q_RANK + MLA_KV_RANK + MLA_ROPE
MLA_IN_PAD = 768
MLA_HD_PAD = 256
D_FF = 4096
PLE_DIM = 256
ATT_SCALE = MLA_QKD ** -0.5
LOG2E = 1.4426950408889634
ATT_EXP2 = ATT_SCALE * LOG2E

ADAM_LR = 0.001
ADAM_B1 = 0.9
ADAM_B2 = 0.999
ADAM_EPS = 1e-08
ADAM_WD = 0.01
ADAM_STEP = 10

VMEM_LIMIT = 52 * 1024 * 1024
ROW_TILE = 1024
RET_ROWS = 256
ATT_BLOCK = 256
ATT_QROWS = 1024
ATT_KROWS = 1024
ATT_HEADS = 2

WEIGHTS = ['mix_norm', 'ret_w_in', 'ret_gn', 'ret_w_out', 'mla_w_in', 'mla_q_a_norm', 'mla_kv_a_norm',
           'mla_w_uq', 'mla_w_ukv', 'mla_q_norm', 'mla_k_norm', 'mla_w_out', 'mlp_norm', 'mlp_w1', 'mlp_w2',
           'ple_norm', 'ple_gate_w', 'ple_proj_w']
BIG = ['ret_w_in', 'ret_w_out', 'mla_w_in', 'mla_w_uq', 'mla_w_ukv', 'mla_w_out', 'mlp_w1', 'mlp_w2',
       'ple_gate_w', 'ple_proj_w']
SMALL = [w for w in WEIGHTS if w not in BIG]


def _cparams(sem=None):
    return pltpu.CompilerParams(dimension_semantics=sem, vmem_limit_bytes=VMEM_LIMIT)


def _dot(a, b, ca, cb):
    return lax.dot_general(a, b, (((ca,), (cb,)), ((), ())), preferred_element_type=F32)


def _bf(v):
    return v if v.dtype == BF16 else v.astype(BF16)


def _sigmoid(z):
    return 1.0 / (1.0 + jnp.exp(-z))


def _mm(name, grid, a, a_spec, b, b_spec, contract, outs, extras=(), epi=None, deps=(), split=None):
    nk = grid[2]
    n_ex, n_out, n_dep = len(extras), len(outs), len(deps)
    acc_shape = tuple(d for d in outs[0][1].block_shape if d is not None)
    if split is not None:
        acc_shape = (acc_shape[1], acc_shape[0] * split)

    def body(*refs):
        a_ref, b_ref = refs[:2]
        ex_refs = refs[2:2 + n_ex]
        out_refs = refs[2 + n_ex + n_dep:2 + n_ex + n_dep + n_out]

        def product():
            return _dot(_bf(a_ref[...]), _bf(b_ref[...]), contract[0], contract[1])

        def finish(acc):
            if split is not None:
                for j in range(acc_shape[1] // split):
                    out_refs[0][j] = acc[:, j * split:(j + 1) * split].astype(out_refs[0].dtype)
                return
            acc = acc[...]
            res = epi(acc, *[r[...] for r in ex_refs]) if epi is not None else (acc,)
            for o, r in zip(out_refs, res):
                o[...] = r.astype(o.dtype)

        if nk == 1:
            finish(product())
        else:
            acc_ref = refs[-1]
            k = pl.program_id(2)

            @pl.when(k == 0)
            def _():
                acc_ref[...] = jnp.zeros_like(acc_ref)

            acc_ref[...] += product()

            @pl.when(k == nk - 1)
            def _():
                finish(acc_ref)

    return pl.pallas_call(
        body, name=name, grid=grid,
        in_specs=[a_spec, b_spec] + [s for _, s in extras] + [ANY] * n_dep,
        out_specs=[s for _, s in outs],
        out_shape=[s for s, _ in outs],
        scratch_shapes=[pltpu.VMEM(acc_shape, F32)] if nk > 1 else [],
        compiler_params=_cparams(("parallel", "parallel", "arbitrary")),
    )(a, b, *[x for x, _ in extras], *deps)


def _mm_rows(name, tm, a, w, mode, outs, extras=(), epi=None, deps=()):
    n_sh, rows, cols = w.shape
    n_ex, n_out, n_dep = len(extras), len(outs), len(deps)
    by_cols = mode in ('nn_cols', 'nt_rows')
    width = cols if mode == 'nn_cols' else rows

    def body(*refs):
        a_ref, w_ref = refs[:2]
        ex_refs = refs[2:2 + n_ex]
        out_refs = refs[2 + n_ex + n_dep:2 + n_ex + n_dep + n_out]
        if by_cols:
            av = _bf(a_ref[...])
            for s in range(n_sh):
                cs = slice(s * width, (s + 1) * width)
                acc = _dot(av, w_ref[s], 1, 0 if mode == 'nn_cols' else 1)
                res = epi(acc, *[r[:, cs] for r in ex_refs]) if epi is not None else (acc,)
                for o, r in zip(out_refs, res):
                    o[:, cs] = r.astype(o.dtype)
        else:
            chunk = rows if mode == 'nn_rows' else cols
            acc = None
            for s in range(n_sh):
                part = _dot(_bf(a_ref[:, s * chunk:(s + 1) * chunk]), w_ref[s], 1, 0 if mode == 'nn_rows' else 1)
                acc = part if acc is None else acc + part
            res = epi(acc, *[r[...] for r in ex_refs]) if epi is not None else (acc,)
            for o, r in zip(out_refs, res):
                o[...] = r.astype(o.dtype)

    t, ka = a.shape
    return pl.pallas_call(
        body, name=name, grid=(t // tm, 1, 1),
        in_specs=[pl.BlockSpec((tm, ka), lambda i, j, k: (i, 0)),
                  pl.BlockSpec((n_sh, rows, cols), lambda i, j, k: (0, 0, 0))] + [s for _, s in extras] + [ANY] * n_dep,
        out_specs=[s for _, s in outs],
        out_shape=[s for s, _ in outs],
        compiler_params=_cparams(("parallel", "arbitrary", "arbitrary")),
    )(a, w, *[x for x, _ in extras], *deps)


def _sds(shape, dtype):
    return jax.ShapeDtypeStruct(shape, dtype)


def _row_tile(t, cap=ROW_TILE):
    return min(cap, t)


def _rms_fwd(name, x, g, deps=()):
    t, d = x.shape
    tm = _row_tile(t)

    def body(x_ref, g_ref, *rest):
        o_ref = rest[-1]
        xv = x_ref[...]
        r = lax.rsqrt(jnp.mean(xv * xv, axis=-1, keepdims=True) + EPS)
        o_ref[...] = (xv * r * g_ref[...]).astype(o_ref.dtype)

    return pl.pallas_call(
        body, name=name, grid=(t // tm,),
        in_specs=[pl.BlockSpec((tm, d), lambda i: (i, 0)), pl.BlockSpec((1, d), lambda i: (0, 0))] + [ANY] * len(deps),
        out_specs=pl.BlockSpec((tm, d), lambda i: (i, 0)),
        out_shape=_sds((t, d), BF16),
        compiler_params=_cparams(("parallel",)),
    )(x, g, *deps)


def _rms_bwd_rows(dy, xv, g, n):
    r = lax.rsqrt(jnp.sum(xv * xv, axis=-1, keepdims=True) / n + EPS)
    xh = xv * r
    dxh = dy * g
    dx = r * (dxh - xh * (jnp.sum(dxh * xh, axis=-1, keepdims=True) / n))
    return dx, dy * xh


def _ple_gate_bwd(name, dh, gate, e):
    t, d = dh.shape
    tm = _row_tile(t)

    def body(dh_ref, g_ref, e_ref, de_ref, dz_ref):
        dh_v, gt = dh_ref[...], g_ref[...].astype(F32)
        de_ref[...] = (dh_v * gt).astype(BF16)
        dz_ref[...] = (dh_v * e_ref[...].astype(F32) * (gt * (1.0 - gt))).astype(BF16)

    row = pl.BlockSpec((tm, d), lambda i: (i, 0))
    return pl.pallas_call(
        body, name=name, grid=(t // tm,), in_specs=[row, row, row], out_specs=[row, row],
        out_shape=[_sds((t, d), BF16), _sds((t, d), BF16)],
        compiler_params=_cparams(("parallel",)),
    )(dh, gate, e)


def _rope_half(v, cos, sin):
    half = v.shape[-1] // 2
    v1, v2 = v[:, :half], v[:, half:]
    return jnp.concatenate([v1 * cos - v2 * sin, v2 * cos + v1 * sin], axis=-1)


def _ret_consts():
    lg = jnp.log(1.0 - 2.0 ** (-5.0 - jnp.arange(RET_HEADS, dtype=F32)))
    idx = jnp.arange(CHUNK, dtype=F32)
    intra = jnp.exp(lg[:, None, None] * jnp.abs(idx[:, None] - idx[None, :]))
    qdec = jnp.exp(lg[:, None] * (idx + 1.0))
    kdec = jnp.exp(lg[:, None] * (CHUNK - 1.0 - idx))
    cdec = jnp.exp(lg * CHUNK)
    qdec = jnp.broadcast_to(qdec[:, :, None], (RET_HEADS, CHUNK, RET_DK))
    kdec = jnp.broadcast_to(kdec[:, :, None], (RET_HEADS, CHUNK, RET_DK))
    cdec = jnp.broadcast_to(cdec[:, None, None], (RET_HEADS, 1, RET_DV))
    return intra, qdec, kdec, cdec


def _ret_specs(rb, rev_nb=None):
    blk = (lambda i: i) if rev_nb is None else (lambda i: rev_nb - 1 - i)
    full = lambda shape: pl.BlockSpec(shape, lambda i: (0,) * len(shape))
    return dict(
        proj=pl.BlockSpec((rb, RET_IN), lambda i: (blk(i), 0)),
        tab=pl.BlockSpec((rb, RET_DK // 2), lambda i: (blk(i), 0)),
        vw=pl.BlockSpec((rb, RET_V_W), lambda i: (blk(i), 0)),
        st=pl.BlockSpec((rb // CHUNK, RET_HEADS, RET_DK, RET_DV), lambda i: (blk(i), 0, 0, 0)),
        gn=full((RET_HEADS, 1, RET_DV)),
        intra=full((RET_HEADS, CHUNK, CHUNK)),
        dec=full((RET_HEADS, CHUNK, RET_DK)),
        cdec=full((RET_HEADS, 1, RET_DV)),
    )


def _ret_fwd(proj, cos, sin, gn):
    t = proj.shape[0]
    rb = min(RET_ROWS, t)
    cpb = rb // CHUNK
    intra, qdec, kdec, cdec = _ret_consts()
    sp = _ret_specs(rb)

    def body(proj_ref, cos_ref, sin_ref, gn_ref, intra_ref, qd_ref, kd_ref, cd_ref,
             gated_ref, outp_ref, st_ref, s_ref):
        @pl.when(pl.program_id(0) == 0)
        def _():
            s_ref[...] = jnp.zeros_like(s_ref)

        def chunk(c, carry):
            rows = pl.ds(pl.multiple_of(c * CHUNK, CHUNK), CHUNK)
            cs, sn = cos_ref[rows, :], sin_ref[rows, :]
            for h in range(RET_HEADS):
                q = proj_ref[rows, h * RET_DK:(h + 1) * RET_DK].astype(F32)
                k = proj_ref[rows, RET_QK_W + h * RET_DK:RET_QK_W + (h + 1) * RET_DK].astype(F32)
                v = proj_ref[rows, 2 * RET_QK_W + h * RET_DV:2 * RET_QK_W + (h + 1) * RET_DV]
                g = proj_ref[rows, 2 * RET_QK_W + RET_V_W + h * RET_DV:
                             2 * RET_QK_W + RET_V_W + (h + 1) * RET_DV].astype(F32)
                qr = _rope_half(q, cs, sn)
                kr = _rope_half(k, cs, sn) * (RET_DK ** -0.5)
                qb, kb, vb = qr.astype(BF16), kr.astype(BF16), v
                sc = _dot(qb, kb, 1, 1) * intra_ref[h]
                inner = _dot(sc.astype(BF16), vb, 1, 0)
                s_old = s_ref[h]
                sb = s_old.astype(BF16)
                st_ref[c, h] = sb
                cross = _dot((qr * qd_ref[h]).astype(BF16), sb, 1, 0)
                out = inner + cross
                s_ref[h] = s_old * cd_ref[h] + _dot((kr * kd_ref[h]).astype(BF16), vb, 0, 0)
                r = lax.rsqrt(jnp.mean(out * out, axis=-1, keepdims=True) + EPS)
                y = out * r * gn_ref[h]
                cols = slice(h * RET_DV, (h + 1) * RET_DV)
                gated_ref[rows, cols] = (g * _sigmoid(g) * y).astype(BF16)
                outp_ref[rows, cols] = out
            return carry

        lax.fori_loop(0, cpb, chunk, 0)

    return pl.pallas_call(
        body, name="ret_fwd", grid=(t // rb,),
        in_specs=[sp['proj'], sp['tab'], sp['tab'], sp['gn'], sp['intra'], sp['dec'], sp['dec'], sp['cdec']],
        out_specs=[sp['vw'], sp['vw'], sp['st']],
        out_shape=[_sds((t, RET_V_W), BF16), _sds((t, RET_V_W), F32),
                   _sds((t // CHUNK, RET_HEADS, RET_DK, RET_DV), BF16)],
        scratch_shapes=[pltpu.VMEM((RET_HEADS, RET_DK, RET_DV), F32)],
        compiler_params=_cparams(("arbitrary",)),
    )(proj, cos, sin, gn.reshape(RET_HEADS, 1, RET_DV), intra, qdec, kdec, cdec)


def _ret_gate_bwd_epi(dgt, out, g, gn):
    g = g.astype(F32)
    r = lax.rsqrt(jnp.mean(out * out, axis=-1, keepdims=True) + EPS)
    xh = out * r
    sg = _sigmoid(g)
    dgate = dgt * (xh * gn) * (sg * (1.0 + g * (1.0 - sg)))
    dy = dgt * (g * sg)
    dxh = dy * gn
    dout = r * (dxh - xh * jnp.mean(dxh * xh, axis=-1, keepdims=True))
    return dout, dgate, jnp.sum(dy * xh, axis=0, keepdims=True)


def _ret_bwd(proj, cos, sin, states, dout, dgate, deps=()):
    t = proj.shape[0]
    rb = min(RET_ROWS, t)
    cpb = rb // CHUNK
    nb = t // rb
    intra, qdec, kdec, cdec = _ret_consts()
    sp = _ret_specs(rb, rev_nb=nb)

    def body(proj_ref, cos_ref, sin_ref, intra_ref, qd_ref, kd_ref, cd_ref, st_ref, dout_ref, dgate_ref, *rest):
        dproj_ref, ds_ref = rest[len(deps):]

        @pl.when(pl.program_id(0) == 0)
        def _():
            ds_ref[...] = jnp.zeros_like(ds_ref)

        def chunk(cc, carry):
            c = cpb - 1 - cc
            rows = pl.ds(pl.multiple_of(c * CHUNK, CHUNK), CHUNK)
            cs, sn = cos_ref[rows, :], sin_ref[rows, :]
            for h in range(RET_HEADS):
                q = proj_ref[rows, h * RET_DK:(h + 1) * RET_DK].astype(F32)
                k = proj_ref[rows, RET_QK_W + h * RET_DK:RET_QK_W + (h + 1) * RET_DK].astype(F32)
                v = proj_ref[rows, 2 * RET_QK_W + h * RET_DV:2 * RET_QK_W + (h + 1) * RET_DV]
                cols = slice(h * RET_DV, (h + 1) * RET_DV)
                qr = _rope_half(q, cs, sn)
                kr = _rope_half(k, cs, sn) * (RET_DK ** -0.5)
                qb, kb, vb = qr.astype(BF16), kr.astype(BF16), v
                qdb = (qr * qd_ref[h]).astype(BF16)
                kdb = (kr * kd_ref[h]).astype(BF16)
                doutb = dout_ref[rows, cols]
                itr = intra_ref[h]
                pb = (_dot(qb, kb, 1, 1) * itr).astype(BF16)
                dv = _dot(pb, doutb, 0, 0)
                dsc = (_dot(doutb, vb, 1, 1) * itr).astype(BF16)
                dq = _dot(dsc, kb, 1, 0)
                dk = _dot(dsc, qb, 0, 0)
                dq = dq + _dot(doutb, st_ref[c, h], 1, 1) * qd_ref[h]
                ds_new = ds_ref[h]
                dsb = ds_new.astype(BF16)
                dk = dk + _dot(vb, dsb, 1, 1) * kd_ref[h]
                dv = dv + _dot(kdb, dsb, 1, 0)
                ds_ref[h] = ds_new * cd_ref[h] + _dot(qdb, doutb, 0, 0)
                dproj_ref[rows, h * RET_DK:(h + 1) * RET_DK] = _rope_half(dq, cs, -sn).astype(BF16)
                dproj_ref[rows, RET_QK_W + h * RET_DK:RET_QK_W + (h + 1) * RET_DK] = (
                    _rope_half(dk * (RET_DK ** -0.5), cs, -sn).astype(BF16))
                dproj_ref[rows, 2 * RET_QK_W + h * RET_DV:2 * RET_QK_W + (h + 1) * RET_DV] = dv.astype(BF16)
                dproj_ref[rows, 2 * RET_QK_W + RET_V_W + h * RET_DV:
                          2 * RET_QK_W + RET_V_W + (h + 1) * RET_DV] = dgate_ref[rows, cols]
            return carry

        lax.fori_loop(0, cpb, chunk, 0)

    return pl.pallas_call(
        body, name="ret_bwd", grid=(nb,),
        in_specs=[sp['proj'], sp['tab'], sp['tab'], sp['intra'], sp['dec'], sp['dec'], sp['cdec'],
                  sp['st'], sp['vw'], sp['vw']] + [ANY] * len(deps),
        out_specs=sp['proj'],
        out_shape=_sds((t, RET_IN), BF16),
        scratch_shapes=[pltpu.VMEM((RET_HEADS, RET_DK, RET_DV), F32)],
        compiler_params=_cparams(("arbitrary",)),
    )(proj, cos, sin, intra, qdec, kdec, cdec, states, dout, dgate, *deps)


def _spread_rope(a):
    half = MLA_ROPE // 2
    z = jnp.zeros(a.shape[:-1] + (half,), a.dtype)
    return jnp.concatenate([a[..., :-MLA_ROPE], a[..., -MLA_ROPE:-half], z, a[..., -half:], z], axis=-1)


def _gather_rope(a):
    half = MLA_ROPE // 2
    n = a.shape[-1] - 2 * MLA_ROPE
    return jnp.concatenate([a[..., :n + half], a[..., n + MLA_ROPE:n + MLA_ROPE + half]], axis=-1)


def _mla_tables(t):
    half = MLA_ROPE // 2
    inv = 1.0 / (ROPE_THETA ** (jnp.arange(0, MLA_ROPE, 2, dtype=F32) / MLA_ROPE))
    ang = jnp.arange(t, dtype=F32)[:, None] * inv[None, :]
    cos, sin = jnp.cos(ang), jnp.sin(ang)
    z = jnp.zeros((t, half), F32)
    return jnp.concatenate([cos, z, cos, z], axis=1), jnp.concatenate([-sin, z, sin, z], axis=1)


def _rope_tile(r, c, s):
    return r * c + pltpu.roll(r, 64, 1) * s


def _mla_mid(proj2, qa, kva):
    t = proj2.shape[0]
    tm = _row_tile(t)

    def body(p_ref, qa_ref, kva_ref, cq_ref, ckv_ref):
        cq = p_ref[:, :MLA_Q_RANK]
        ckv = p_ref[:, MLA_Q_RANK:MLA_Q_RANK + MLA_KV_RANK]
        rq = lax.rsqrt(jnp.mean(cq * cq, axis=-1, keepdims=True) + EPS)
        rkv = lax.rsqrt(jnp.mean(ckv * ckv, axis=-1, keepdims=True) + EPS)
        cq_ref[...] = (cq * rq * qa_ref[...]).astype(BF16)
        ckv_ref[...] = (ckv * rkv * kva_ref[...]).astype(BF16)

    return pl.pallas_call(
        body, name="mla_mid", grid=(t // tm,),
        in_specs=[pl.BlockSpec((tm, MLA_IN_PAD), lambda i: (i, 0)),
                  pl.BlockSpec((1, MLA_Q_RANK), lambda i: (0, 0)),
                  pl.BlockSpec((1, MLA_KV_RANK), lambda i: (0, 0))],
        out_specs=[pl.BlockSpec((tm, MLA_Q_RANK), lambda i: (i, 0)),
                   pl.BlockSpec((tm, MLA_KV_RANK), lambda i: (i, 0))],
        out_shape=[_sds((t, MLA_Q_RANK), BF16), _sds((t, MLA_KV_RANK), BF16)],
        compiler_params=_cparams(("parallel",)),
    )(proj2, qa, kva)


def _mla_mid_bwd(proj2, qa, kva, dcq, dckv, dkr):
    t = proj2.shape[0]
    tm = _row_tile(t)

    def body(p_ref, qa_ref, kva_ref, dcq_ref, dckv_ref, dkr_ref, dp_ref, dqa_ref, dkva_ref):
        @pl.when(pl.program_id(0) == 0)
        def _():
            dqa_ref[...] = jnp.zeros_like(dqa_ref)
            dkva_ref[...] = jnp.zeros_like(dkva_ref)

        dxq, dgq = _rms_bwd_rows(dcq_ref[...], p_ref[:, :MLA_Q_RANK], qa_ref[...], MLA_Q_RANK)
        dxk, dgk = _rms_bwd_rows(dckv_ref[...], p_ref[:, MLA_Q_RANK:MLA_Q_RANK + MLA_KV_RANK], kva_ref[...],
                                 MLA_KV_RANK)
        dp_ref[:, :MLA_Q_RANK] = dxq.astype(BF16)
        dp_ref[:, MLA_Q_RANK:MLA_Q_RANK + MLA_KV_RANK] = dxk.astype(BF16)
        dp_ref[:, MLA_Q_RANK + MLA_KV_RANK:] = dkr_ref[...].astype(BF16)
        dqa_ref[...] += jnp.sum(dgq, axis=0, keepdims=True)
        dkva_ref[...] += jnp.sum(dgk, axis=0, keepdims=True)

    return pl.pallas_call(
        body, name="mla_mid_bwd", grid=(t // tm,),
        in_specs=[pl.BlockSpec((tm, MLA_IN_PAD), lambda i: (i, 0)),
                  pl.BlockSpec((1, MLA_Q_RANK), lambda i: (0, 0)),
                  pl.BlockSpec((1, MLA_KV_RANK), lambda i: (0, 0)),
                  pl.BlockSpec((tm, MLA_Q_RANK), lambda i: (i, 0)),
                  pl.BlockSpec((tm, MLA_KV_RANK), lambda i: (i, 0)),
                  pl.BlockSpec((tm, 128), lambda i: (i, 0))],
        out_specs=[pl.BlockSpec((tm, MLA_IN_PAD), lambda i: (i, 0)),
                   pl.BlockSpec((1, MLA_Q_RANK), lambda i: (0, 0)),
                   pl.BlockSpec((1, MLA_KV_RANK), lambda i: (0, 0))],
        out_shape=[_sds((t, MLA_IN_PAD), BF16), _sds((1, MLA_Q_RANK), F32), _sds((1, MLA_KV_RANK), F32)],
        compiler_params=_cparams(("arbitrary",)),
    )(proj2, qa, kva, dcq, dckv, dkr)


def _mla_prep_specs(t, tm):
    head = lambda w: pl.BlockSpec((None, tm, w), lambda i, h: (h, i, 0))
    return dict(
        head256=head(MLA_HD_PAD), head128=head(MLA_VD),
        cols256=pl.BlockSpec((tm, MLA_HD_PAD), lambda i, h: (i, h)),
        cq=pl.BlockSpec((tm, MLA_Q_RANK), lambda i, h: (i, 0)),
        ckv=pl.BlockSpec((tm, MLA_KV_RANK), lambda i, h: (i, 0)),
        wuq=pl.BlockSpec((None, MLA_Q_RANK, MLA_HD_PAD), lambda i, h: (h, 0, 0)),
        wukv=pl.BlockSpec((None, MLA_KV_RANK, MLA_HD_PAD), lambda i, h: (h, 0, 0)),
        kr=pl.BlockSpec((tm, 128), lambda i, h: (i, (MLA_Q_RANK + MLA_KV_RANK) // 128)),
        gain=pl.BlockSpec((1, MLA_HD_PAD), lambda i, h: (0, 0)),
        tab=pl.BlockSpec((tm, 128), lambda i, h: (i, 0)),
    )


def _mla_prep(cq, ckv, wuq, wukv, proj2, gq, gk, tabs):
    t = cq.shape[0]
    tm = _row_tile(t)
    sp = _mla_prep_specs(t, tm)

    def body(cq_ref, ckv_ref, wuq_ref, wukv_ref, kr_ref, gq_ref, gk_ref, c_ref, s_ref,
             qh_ref, kh_ref, vh_ref):
        c, s = c_ref[...], s_ref[...]

        def norm_rope(xv, gain):
            r = lax.rsqrt(jnp.sum(xv * xv, axis=-1, keepdims=True) / MLA_QKD + EPS)
            y = xv * r * gain
            return jnp.concatenate([y[:, :MLA_NOPE], _rope_tile(y[:, MLA_NOPE:], c, s)], axis=-1)

        kvv = _dot(ckv_ref[...], wukv_ref[...], 1, 0)
        qh_ref[...] = norm_rope(_dot(cq_ref[...], wuq_ref[...], 1, 0), gq_ref[...]).astype(BF16)
        kf = jnp.concatenate([kvv[:, :MLA_NOPE], kr_ref[...]], axis=-1)
        kh_ref[...] = norm_rope(kf, gk_ref[...]).astype(BF16)
        vh_ref[...] = jnp.concatenate([kvv[:, MLA_NOPE:], jnp.ones((tm, MLA_VD), F32)], axis=-1).astype(BF16)

    return pl.pallas_call(
        body, name="mla_prep", grid=(t // tm, MLA_HEADS),
        in_specs=[sp['cq'], sp['ckv'], sp['wuq'], sp['wukv'], sp['kr'], sp['gain'], sp['gain'],
                  sp['tab'], sp['tab']],
        out_specs=[sp['head256'], sp['head256'], sp['head256']],
        out_shape=[_sds((MLA_HEADS, t, MLA_HD_PAD), BF16), _sds((MLA_HEADS, t, MLA_HD_PAD), BF16),
                   _sds((MLA_HEADS, t, 2 * MLA_VD), BF16)],
        compiler_params=_cparams(("parallel", "arbitrary")),
    )(cq, ckv, wuq, wukv, proj2, gq, gk, *tabs)


def _mla_prep_bwd(cq, ckv, wuq, wukv, proj2, gq, gk, tabs, dqt, dkh, dvh):
    t = cq.shape[0]
    tm = _row_tile(t)
    ab = dqt.shape[-1]
    sp = _mla_prep_specs(t, tm)

    def body(cq_ref, ckv_ref, wuq_ref, wukv_ref, kr_ref, gq_ref, gk_ref, c_ref, s_ref,
             dqt_ref, dkh_ref, dvh_ref, dq_ref, dkv_ref, dkr_ref, dgq_ref, dgk_ref):
        dqh = jnp.concatenate([dqt_ref[b].T for b in range(tm // ab)], axis=0)
        i, h = pl.program_id(0), pl.program_id(1)

        @pl.when((i == 0) & (h == 0))
        def _():
            dgq_ref[...] = jnp.zeros_like(dgq_ref)
            dgk_ref[...] = jnp.zeros_like(dgk_ref)

        @pl.when(h == 0)
        def _():
            dkr_ref[...] = jnp.zeros_like(dkr_ref)

        c, s = c_ref[...], s_ref[...]

        def back(xv, gain, dout):
            dy = jnp.concatenate([dout[:, :MLA_NOPE], _rope_tile(dout[:, MLA_NOPE:], c, -s)], axis=-1)
            return _rms_bwd_rows(dy, xv, gain, MLA_QKD)

        kvv = _dot(ckv_ref[...], wukv_ref[...], 1, 0)
        dxq, dgq = back(_dot(cq_ref[...], wuq_ref[...], 1, 0), gq_ref[...], dqh)
        kf = jnp.concatenate([kvv[:, :MLA_NOPE], kr_ref[...]], axis=-1)
        dxk, dgk = back(kf, gk_ref[...], dkh_ref[...])
        dq_ref[...] = dxq.astype(BF16)
        dkv_ref[...] = jnp.concatenate([dxk[:, :MLA_NOPE], dvh_ref[...]], axis=-1).astype(BF16)
        dkr_ref[...] += dxk[:, MLA_NOPE:]
        dgq_ref[...] += jnp.sum(dgq, axis=0, keepdims=True)
        dgk_ref[...] += jnp.sum(dgk, axis=0, keepdims=True)

    return pl.pallas_call(
        body, name="mla_prep_bwd", grid=(t // tm, MLA_HEADS),
        in_specs=[sp['cq'], sp['ckv'], sp['wuq'], sp['wukv'], sp['kr'], sp['gain'], sp['gain'],
                  sp['tab'], sp['tab'],
                  pl.BlockSpec((None, tm // ab, MLA_HD_PAD, ab), lambda i, h: (h, i, 0, 0)),
                  sp['head256'], sp['head128']],
        out_specs=[sp['cols256'], sp['cols256'], sp['tab'], sp['gain'], sp['gain']],
        out_shape=[_sds((t, MLA_HEADS * MLA_HD_PAD), BF16), _sds((t, MLA_HEADS * MLA_HD_PAD), BF16),
                   _sds((t, 128), F32), _sds((1, MLA_HD_PAD), F32), _sds((1, MLA_HD_PAD), F32)],
        compiler_params=_cparams(("arbitrary", "arbitrary")),
    )(cq, ckv, wuq, wukv, proj2, gq, gk, *tabs, dqt, dkh, dvh)


def _chunk_visible(rows, cols, row_off, col_off):
    rq = lax.shift_right_logical(lax.broadcasted_iota(jnp.int32, (rows, cols), 0) + row_off, 6)
    ck = lax.shift_right_logical(lax.broadcasted_iota(jnp.int32, (rows, cols), 1) + col_off, 6)
    return ck <= rq


def _rows_to_lanes(col):
    return col.T[:8, :]


def _attn_fwd(qh, kh, vh):
    t = qh.shape[1]
    ab = min(ATT_BLOCK, t)
    tq = min(ATT_QROWS, t)
    r = tq // ab
    hg = ATT_HEADS

    def body(q_ref, k_ref, v_ref, o_ref, lse_ref, acc_ref):
        n_un = pl.program_id(1) * r
        acc_ref[...] = jnp.zeros_like(acc_ref)

        def step(b, ms, diag):
            rows = pl.ds(pl.multiple_of(b * ab, ab), ab)
            out = []
            for hh in range(hg):
                m = ms[hh]
                s = _dot(q_ref[hh], k_ref[hh, rows, :], 1, 1)
                if diag is not None:
                    s = jnp.where(_chunk_visible(tq, ab, 0, diag * ab), s, -1e30)
                m_new = jnp.maximum(m, jnp.max(s, axis=-1, keepdims=True))
                p = jnp.exp2((s - m_new) * ATT_EXP2).astype(BF16)
                acc_ref[hh] = jnp.exp2((m - m_new) * ATT_EXP2) * acc_ref[hh] + _dot(p, v_ref[hh, rows, :], 1, 0)
                out.append(m_new)
            return tuple(out)

        ms = tuple(jnp.full((tq, 1), -1e30, F32) for _ in range(hg))
        ms = lax.fori_loop(0, n_un, lambda b, st: step(b, st, None), ms)
        for d in range(r):
            ms = step(n_un + d, ms, d)
        for hh in range(hg):
            l = acc_ref[hh, :, MLA_VD:]
            o_ref[:, hh * MLA_VD:(hh + 1) * MLA_VD] = acc_ref[hh, :, :MLA_VD] / l
            lse_t = _rows_to_lanes(ms[hh] * ATT_EXP2 + jnp.log(l) * LOG2E)
            for d in range(r):
                lse_ref[hh, d] = lse_t[:, d * ab:(d + 1) * ab]

    return pl.pallas_call(
        body, name="mla_attn", grid=(MLA_HEADS // hg, t // tq),
        in_specs=[pl.BlockSpec((hg, tq, MLA_HD_PAD), lambda g, i: (g, i, 0)),
                  pl.BlockSpec((hg, t, MLA_HD_PAD), lambda g, i: (g, 0, 0)),
                  pl.BlockSpec((hg, t, 2 * MLA_VD), lambda g, i: (g, 0, 0))],
        out_specs=[pl.BlockSpec((tq, hg * MLA_VD), lambda g, i: (i, g)),
                   pl.BlockSpec((hg, r, 8, ab), lambda g, i: (g, i, 0, 0))],
        out_shape=[_sds((t, MLA_HEADS * MLA_VD), F32), _sds((MLA_HEADS, t // ab, 8, ab), F32)],
        scratch_shapes=[pltpu.VMEM((hg, tq, 2 * MLA_VD), F32)],
        compiler_params=_cparams(("parallel", "arbitrary")),
    )(qh, kh, vh)


def _attn_delta(do, o, ab):
    t = do.shape[0]
    tm = _row_tile(t)

    def body(do_ref, o_ref, d_ref):
        d = jnp.sum(do_ref[...] * o_ref[...], axis=-1, keepdims=True)
        d_t = _rows_to_lanes(jnp.broadcast_to(d, (tm, 128)))
        for b in range(tm // ab):
            d_ref[b] = d_t[:, b * ab:(b + 1) * ab]

    col = pl.BlockSpec((tm, MLA_VD), lambda i, h: (i, h))
    return pl.pallas_call(
        body, name="mla_delta", grid=(t // tm, MLA_HEADS), in_specs=[col, col],
        out_specs=pl.BlockSpec((None, tm // ab, 8, ab), lambda i, h: (h, i, 0, 0)),
        out_shape=_sds((MLA_HEADS, t // ab, 8, ab), F32),
        compiler_params=_cparams(("parallel", "parallel")),
    )(do, o)


def _attn_bwd(qh, kh, vh, dob, lse_t, dl_t):
    t = qh.shape[1]
    ab = min(ATT_BLOCK, t)
    kb = min(ATT_KROWS, t)
    r = kb // ab
    nq = t // ab
    hg = ATT_HEADS

    def body(q_ref, k_ref, v_ref, do_ref, lse_ref, dl_ref, dqt_ref, dk_ref, dv_ref):
        j = pl.program_id(1)

        @pl.when(j == 0)
        def _():
            dqt_ref[...] = jnp.zeros_like(dqt_ref)

        ks = [k_ref[hh] for hh in range(hg)]
        vs = [v_ref[hh, :, :MLA_VD] for hh in range(hg)]
        kts = [k.T for k in ks]

        dk_ref[...] = jnp.zeros_like(dk_ref)
        dv_ref[...] = jnp.zeros_like(dv_ref)

        def step(b, carry, diag):
            rows = pl.ds(pl.multiple_of(b * ab, ab), ab)
            hi = kb if diag is None else (diag + 1) * ab
            for hh in range(hg):
                q = q_ref[hh, rows, :]
                do = do_ref[rows, hh * MLA_VD:(hh + 1) * MLA_VD]
                s_t = _dot(ks[hh][:hi], q, 1, 1)
                if diag is not None:
                    key_chunk = lax.shift_right_logical(lax.broadcasted_iota(jnp.int32, (hi, ab), 0), 6)
                    query_chunk = lax.shift_right_logical(
                        lax.broadcasted_iota(jnp.int32, (hi, ab), 1) + diag * ab, 6)
                    s_t = jnp.where(key_chunk <= query_chunk, s_t, -1e30)
                p_t = jnp.exp2(s_t * ATT_EXP2 - lse_ref[hh, b][0:1, :])
                dp_t = _dot(vs[hh][:hi], do, 1, 1)
                ds_t = (p_t * (dp_t - dl_ref[hh, b][0:1, :]) * ATT_SCALE).astype(BF16)
                dqt_ref[hh, b] += _dot(kts[hh][:, :hi], ds_t, 1, 0)
                dk_ref[hh, :hi] += _dot(ds_t, q, 1, 0)
                dv_ref[hh, :hi] += _dot(p_t.astype(BF16), do, 1, 0)
            return carry

        for d in range(r):
            step(j * r + d, 0, d)
        lax.fori_loop((j + 1) * r, nq, lambda b, c: step(b, c, None), 0)

    whole = lambda w: pl.BlockSpec((hg, t, w), lambda g, j: (g, 0, 0))
    blk = lambda w: pl.BlockSpec((hg, kb, w), lambda g, j: (g, j, 0))
    stat = pl.BlockSpec((hg, nq, 8, ab), lambda g, j: (g, 0, 0, 0))
    return pl.pallas_call(
        body, name="mla_attn_bwd", grid=(MLA_HEADS // hg, t // kb),
        in_specs=[whole(MLA_HD_PAD), blk(MLA_HD_PAD), blk(2 * MLA_VD),
                  pl.BlockSpec((t, hg * MLA_VD), lambda g, j: (0, g)), stat, stat],
        out_specs=[pl.BlockSpec((hg, nq, MLA_HD_PAD, ab), lambda g, j: (g, 0, 0, 0)), blk(MLA_HD_PAD), blk(MLA_VD)],
        out_shape=[_sds((MLA_HEADS, nq, MLA_HD_PAD, ab), F32), _sds((MLA_HEADS, t, MLA_HD_PAD), F32),
                   _sds((MLA_HEADS, t, MLA_VD), F32)],
        compiler_params=_cparams(("parallel", "arbitrary")),
    )(qh, kh, vh, dob, lse_t, dl_t)


VEC = pl.BlockSpec((1, D_MODEL), lambda i, j, k: (0, 0))


def _rows(tm, width):
    return pl.BlockSpec((tm, width), lambda i, j, k: (i, 0))


def _residual_epi(next_gain):
    if next_gain is None:
        return [], lambda acc, hv: (acc + hv,)

    def epi(acc, hv, g):
        h_new = acc + hv
        r = lax.rsqrt(jnp.mean(h_new * h_new, axis=-1, keepdims=True) + EPS)
        return h_new, h_new * r * g

    return [(next_gain, VEC)], epi


def _residual_outs(t, row, next_gain):
    outs = [(_sds((t, D_MODEL), F32), row)]
    return outs + ([(_sds((t, D_MODEL), BF16), row)] if next_gain is not None else [])


def _mlp_fwd(l, h, hn, w1g, fetch_w2, next_gain):
    t = h.shape[0]
    tm = _row_tile(t, 512)

    def relu2(acc):
        r = jnp.maximum(acc, 0.0)
        return (r * r,)

    (u,) = _mm_rows(f"mlp_up{l}", tm, hn, w1g, 'nn_cols', [(_sds((t, D_FF), BF16), _rows(tm, D_FF))], epi=relu2)
    w2g = fetch_w2((u,))
    row = _rows(tm, D_MODEL)
    more, epi = _residual_epi(next_gain)
    h2, hn_next = _mm_rows(f"mlp_down{l}", tm, u, w2g, 'nn_rows', _residual_outs(t, row, next_gain),
                           extras=[(h, row)] + more, epi=epi)
    return h2, hn_next, (h, hn, u, w1g, w2g)


def _norm_bwd_outs(t, tm):
    return [(_sds((t, D_MODEL), F32), pl.BlockSpec((tm, D_MODEL), lambda i, j, k: (i, 0))),
            (_sds((t // tm, 1, D_MODEL), F32), pl.BlockSpec((None, 1, D_MODEL), lambda i, j, k: (i, 0, 0)))]


def _norm_bwd_epi(acc, xv, res, g):
    dx, dgr = _rms_bwd_rows(acc, xv, g, D_MODEL)
    return res + dx, jnp.sum(dgr, axis=0, keepdims=True)


def _mlp_bwd(l, dh, saved, norm_g):
    h, hn, u, w1g, w2g = saved
    t = h.shape[0]
    tm = _row_tile(t, 512)
    nsh, _, wsh = w1g.shape
    wide = _rows(tm, D_FF)
    (da,) = _mm_rows(f"mlp_du{l}", tm, dh, w2g, 'nt_rows', [(_sds((t, D_FF), BF16), wide)], extras=[(u, wide)],
                     epi=lambda acc, uv: (2.0 * jnp.sqrt(uv.astype(F32)) * acc,))
    tw = _row_tile(t, 512)
    (dw2,) = _mm(f"mlp_dw2{l}", (1, 1, t // tw),
                 u, pl.BlockSpec((tw, D_FF), lambda i, j, k: (k, 0)),
                 dh, pl.BlockSpec((tw, D_MODEL), lambda i, j, k: (k, 0)), (0, 0),
                 [(_sds((D_FF, D_MODEL), BF16), pl.BlockSpec((D_FF, D_MODEL), lambda i, j, k: (0, 0)))])
    dw2 = dw2.reshape(nsh, wsh, D_MODEL)
    (dw1,) = _mm(f"mlp_dw1{l}", (1, 1, t // tw),
                 hn, pl.BlockSpec((tw, D_MODEL), lambda i, j, k: (k, 0)),
                 da, pl.BlockSpec((tw, D_FF), lambda i, j, k: (k, 0)), (0, 0),
                 [(_sds((nsh, D_MODEL, wsh), BF16), pl.BlockSpec((nsh, D_MODEL, wsh), lambda i, j, k: (0, 0, 0)))],
                 split=wsh)
    row = _rows(tm, D_MODEL)
    dh_in, dg = _mm_rows(f"mlp_dhn{l}", tm, da, w1g, 'nt_cols', _norm_bwd_outs(t, tm),
                         extras=[(h, row), (dh, row), (norm_g, VEC)], epi=_norm_bwd_epi)
    return dh_in, jnp.sum(dg, axis=0), dw1, dw2


def _ple_fwd(l, h, hn, p, wg, wp, next_gain, target=None):
    t = h.shape[0]
    tm = _row_tile(t, 512)
    row = pl.BlockSpec((tm, D_MODEL), lambda i, j, k: (i, 0))
    full = lambda r: pl.BlockSpec((r, D_MODEL), lambda i, j, k: (0, 0))
    f32_row, bf_row = (_sds((t, D_MODEL), F32), row), (_sds((t, D_MODEL), BF16), row)
    common = [(h, row), (p, pl.BlockSpec((None, None, tm, PLE_DIM), lambda i, j, k: (l, 0, i, 0))),
              (wp, full(PLE_DIM))]
    if target is not None:
        def loss_epi(acc, hv, pv, wpv, tv):
            gt = _sigmoid(acc)
            ev = _dot(_bf(pv), wpv, 1, 0)
            err = hv + gt * ev - tv
            sq = jnp.sum(jnp.sum(err * err, axis=-1, keepdims=True), axis=0, keepdims=True)
            return err / D_MODEL, gt, ev, jnp.broadcast_to(sq, (8, 128))

        dy, gate, e, sq = _mm(f"ple_gate{l}", (t // tm, 1, 1), hn, row, wg, full(D_MODEL), (1, 0),
                              [f32_row, bf_row, bf_row, (_sds((t // tm, 8, 128), F32),
                                                         pl.BlockSpec((None, 8, 128), lambda i, j, k: (i, 0, 0)))],
                              extras=common + [(target, row)], epi=loss_epi)
        return dy, jnp.sum(sq, axis=0), (h, hn, gate, e)

    def gate_epi(acc, hv, pv, wpv, *gain):
        gt = _sigmoid(acc)
        ev = _dot(_bf(pv), wpv, 1, 0)
        h_new = hv + gt * ev
        if not gain:
            return h_new, gt, ev
        r = lax.rsqrt(jnp.mean(h_new * h_new, axis=-1, keepdims=True) + EPS)
        return h_new, gt, ev, h_new * r * gain[0]

    res = _mm(f"ple_gate{l}", (t // tm, 1, 1), hn, row, wg, full(D_MODEL), (1, 0),
              [f32_row, bf_row, bf_row] + ([bf_row] if next_gain is not None else []),
              extras=common + ([(next_gain, VEC)] if next_gain is not None else []), epi=gate_epi)
    h_out, gate, e = res[0], res[1], res[2]
    return h_out, (res[3] if next_gain is not None else None), (h, hn, gate, e)


def _ple_bwd(l, dh, saved, p, norm_g, wg, deps=()):
    h, hn, gate, e = saved
    t = h.shape[0]
    tm = _row_tile(t)
    tk = _row_tile(t, 512)
    de, dz = _ple_gate_bwd(f"ple_gate_bwd{l}", dh, gate, e)
    full = lambda r: pl.BlockSpec((r, D_MODEL), lambda i, j, k: (0, 0))
    rowk = pl.BlockSpec((tk, D_MODEL), lambda i, j, k: (k, 0))
    (dwp,) = _mm(f"ple_dwp{l}", (1, 1, t // tk),
                 p, pl.BlockSpec((None, None, tk, PLE_DIM), lambda i, j, k: (l, 0, k, 0)),
                 de, rowk, (0, 0), [(_sds((PLE_DIM, D_MODEL), BF16), full(PLE_DIM))], deps=deps)
    (dwg,) = _mm(f"ple_dwg{l}", (1, 1, t // tk), hn, rowk, dz, rowk, (0, 0),
                 [(_sds((D_MODEL, D_MODEL), BF16), full(D_MODEL))])
    row = pl.BlockSpec((tm, D_MODEL), lambda i, j, k: (i, 0))
    dh_in, dg = _mm(f"ple_dhn{l}", (t // tm, 1, 1), dz, row, wg, full(D_MODEL), (1, 1),
                    _norm_bwd_outs(t, tm), extras=[(h, row), (dh, row), (norm_g, VEC)], epi=_norm_bwd_epi)
    return dh_in, jnp.sum(dg, axis=0), dwg, dwp


def _ret_layer_fwd(x, norm_g, wri, fetch_wro, gn, cos, sin, next_gain, hn=None, deps=()):
    t = x.shape[0]
    tm = _row_tile(t)
    nsh, _, wsh = wri.shape
    if hn is None:
        hn = _rms_fwd("mix_norm0", x, norm_g)
    tp = _row_tile(t, 512)
    (proj,) = _mm_rows("ret_in", tp, hn, wri, 'nn_cols', [(_sds((t, RET_IN), BF16), _rows(tp, RET_IN))], deps=deps)
    gated, outp, states = _ret_fwd(proj, cos, sin, gn)
    wro = fetch_wro((gated,))
    row = _rows(tp, D_MODEL)
    more, epi = _residual_epi(next_gain)
    h1, hn_next = _mm_rows("ret_out", tp, gated, wro.reshape(RET_HEADS, RET_DV, D_MODEL), 'nn_rows',
                           _residual_outs(t, row, next_gain), extras=[(x, row)] + more, epi=epi)
    return h1, hn_next, (x, hn, proj, gated, outp, states, wro)


def _ret_layer_bwd(dh, saved, norm_g, wri, gn, cos, sin, emit_out, emit_in, deps=()):
    x, hn, proj, gated, outp, states, wro = saved
    t = x.shape[0]
    tm = _row_tile(t)
    tk = _row_tile(t, 512)
    nsh, _, wsh = wri.shape
    tg = _row_tile(t, 512)
    vw = _rows(tg, RET_V_W)
    dout, dgate, dgn = _mm_rows(
        "ret_dgate", tg, dh, wro.reshape(RET_HEADS, RET_DV, D_MODEL), 'nt_rows',
        [(_sds((t, RET_V_W), BF16), vw), (_sds((t, RET_V_W), BF16), vw),
         (_sds((t // tg, 1, RET_V_W), F32), pl.BlockSpec((None, 1, RET_V_W), lambda i, j, k: (i, 0, 0)))],
        extras=[(outp, vw), (proj, pl.BlockSpec((tg, RET_V_W), lambda i, j, k: (i, (RET_IN - RET_V_W) // RET_V_W))),
                (gn.reshape(1, RET_V_W), pl.BlockSpec((1, RET_V_W), lambda i, j, k: (0, 0)))],
        epi=_ret_gate_bwd_epi, deps=deps)
    dgn = jnp.sum(dgn, axis=0)
    (dwro,) = _mm("ret_dwro", (1, 1, t // tk),
                  gated, pl.BlockSpec((tk, RET_V_W), lambda i, j, k: (k, 0)),
                  dh, pl.BlockSpec((tk, D_MODEL), lambda i, j, k: (k, 0)), (0, 0),
                  [(_sds((RET_V_W, D_MODEL), BF16), pl.BlockSpec((RET_V_W, D_MODEL), lambda i, j, k: (0, 0)))])
    dproj = _ret_bwd(proj, cos, sin, states, dout, dgate, deps=emit_out(dwro))
    half = nsh // 2
    (dwri,) = _mm("ret_dwri", (2, 1, t // tk),
                  hn, pl.BlockSpec((tk, D_MODEL), lambda i, j, k: (k, 0)),
                  dproj, pl.BlockSpec((tk, half * wsh), lambda i, j, k: (k, i)), (0, 0),
                  [(_sds((nsh, D_MODEL, wsh), BF16), pl.BlockSpec((half, D_MODEL, wsh), lambda i, j, k: (i, 0, 0)))],
                  split=wsh)
    deps = emit_in(dwri)
    td = _row_tile(t, 256)
    row = _rows(td, D_MODEL)
    dx, dg = _mm_rows("ret_dhn", td, dproj, wri, 'nt_cols', _norm_bwd_outs(t, td),
                      extras=[(x, row), (dh, row), (norm_g, VEC)], epi=_norm_bwd_epi, deps=deps)
    return dx, jnp.sum(dg, axis=0), dgn.reshape(RET_HEADS, RET_DV)


def _mla_layer_fwd(h, hn, wmi, qa, kva, wuq, wukv, gq, gk, wmo, tabs, next_gain):
    t = h.shape[0]
    tm = _row_tile(t)
    row = pl.BlockSpec((tm, D_MODEL), lambda i, j, k: (i, 0))
    (proj2,) = _mm("mla_in", (t // tm, 1, 1), hn, row,
                   wmi, pl.BlockSpec((D_MODEL, MLA_IN_PAD), lambda i, j, k: (0, 0)), (1, 0),
                   [(_sds((t, MLA_IN_PAD), F32), pl.BlockSpec((tm, MLA_IN_PAD), lambda i, j, k: (i, 0)))])
    cq, ckv = _mla_mid(proj2, qa, kva)
    qh, kh, vh = _mla_prep(cq, ckv, wuq, wukv, proj2, gq, gk, tabs)
    o, lse = _attn_fwd(qh, kh, vh)
    more, epi = _residual_epi(next_gain)
    h_out, hn_next = _mm("mla_out", (t // tm, 1, 1), o, row,
                         wmo, pl.BlockSpec((D_MODEL, D_MODEL), lambda i, j, k: (0, 0)), (1, 0),
                         _residual_outs(t, row, next_gain), extras=[(h, row)] + more, epi=epi)
    return h_out, hn_next, (h, hn, proj2, cq, ckv, qh, kh, vh, o, lse)


def _mla_layer_bwd(dh, saved, norm_g, wmi, qa, kva, wuq, wukv, gq, gk, wmo, tabs, deps=()):
    h, hn, proj2, cq, ckv, qh, kh, vh, o, lse = saved
    t = h.shape[0]
    tm = _row_tile(t)
    tk = _row_tile(t, 512)
    row = pl.BlockSpec((tm, D_MODEL), lambda i, j, k: (i, 0))
    rowk = pl.BlockSpec((tk, D_MODEL), lambda i, j, k: (k, 0))
    sq = pl.BlockSpec((D_MODEL, D_MODEL), lambda i, j, k: (0, 0))
    do, dob = _mm("mla_do", (t // tm, 1, 1), dh, row, wmo, sq, (1, 1),
                  [(_sds((t, D_MODEL), F32), row), (_sds((t, D_MODEL), BF16), row)], epi=lambda acc: (acc, acc),
                  deps=deps)
    (dwmo,) = _mm("mla_dwo", (1, 1, t // tk), o, rowk, dh, rowk, (0, 0), [(_sds((D_MODEL, D_MODEL), BF16), sq)])
    delta = _attn_delta(do, o, lse.shape[-1])
    dqt, dkh, dvh = _attn_bwd(qh, kh, vh, dob, lse, delta)
    dq, dkv, dkr, dgq, dgk = _mla_prep_bwd(cq, ckv, wuq, wukv, proj2, gq, gk, tabs, dqt, dkh, dvh)

    wide = MLA_HEADS * MLA_HD_PAD
    widek = pl.BlockSpec((tk, wide), lambda i, j, k: (k, 0))
    (dwuq,) = _mm("mla_dwuq", (1, 1, t // tk),
                  cq, pl.BlockSpec((tk, MLA_Q_RANK), lambda i, j, k: (k, 0)), dq, widek, (0, 0),
                  [(_sds((MLA_HEADS, MLA_Q_RANK, MLA_HD_PAD), BF16),
                    pl.BlockSpec((MLA_HEADS, MLA_Q_RANK, MLA_HD_PAD), lambda i, j, k: (0, 0, 0)))], split=MLA_HD_PAD)
    (dwukv,) = _mm("mla_dwukv", (1, 1, t // tk),
                   ckv, pl.BlockSpec((tk, MLA_KV_RANK), lambda i, j, k: (k, 0)), dkv, widek, (0, 0),
                   [(_sds((MLA_HEADS, MLA_KV_RANK, MLA_HD_PAD), BF16),
                     pl.BlockSpec((MLA_HEADS, MLA_KV_RANK, MLA_HD_PAD), lambda i, j, k: (0, 0, 0)))],
                   split=MLA_HD_PAD)
    side_by_side = lambda wg: wg.transpose(1, 0, 2).reshape(wg.shape[1], wide)
    widei = pl.BlockSpec((tm, wide), lambda i, j, k: (i, 0))
    (dcq,) = _mm("mla_dcq", (t // tm, 1, 1), dq, widei,
                 side_by_side(wuq), pl.BlockSpec((MLA_Q_RANK, wide), lambda i, j, k: (0, 0)), (1, 1),
                 [(_sds((t, MLA_Q_RANK), F32), pl.BlockSpec((tm, MLA_Q_RANK), lambda i, j, k: (i, 0)))])
    (dckv,) = _mm("mla_dckv", (t // tm, 1, 1), dkv, widei,
                  side_by_side(wukv), pl.BlockSpec((MLA_KV_RANK, wide), lambda i, j, k: (0, 0)), (1, 1),
                  [(_sds((t, MLA_KV_RANK), F32), pl.BlockSpec((tm, MLA_KV_RANK), lambda i, j, k: (i, 0)))])
    dproj2, dqa, dkva = _mla_mid_bwd(proj2, qa, kva, dcq, dckv, dkr)
    win = pl.BlockSpec((D_MODEL, MLA_IN_PAD), lambda i, j, k: (0, 0))
    (dwmi,) = _mm("mla_dwin", (1, 1, t // tk), hn, rowk,
                  dproj2, pl.BlockSpec((tk, MLA_IN_PAD), lambda i, j, k: (k, 0)), (0, 0),
                  [(_sds((D_MODEL, MLA_IN_PAD), BF16), win)])
    dh_in, dg = _mm("mla_dhn", (t // tm, 1, 1),
                    dproj2, pl.BlockSpec((tm, MLA_IN_PAD), lambda i, j, k: (i, 0)), wmi, win, (1, 1),
                    _norm_bwd_outs(t, tm), extras=[(h, row), (dh, row), (norm_g, VEC)], epi=_norm_bwd_epi)
    return dh_in, dict(mix=jnp.sum(dg, axis=0), wmi=dwmi, qa=dqa, kva=dkva, wuq=dwuq, wukv=dwukv, gq=dgq, gk=dgk,
                       wmo=dwmo)


def _local_step(x, p, target, w, fetch, emit=lambda group: ()):
    t = x.shape[0]
    inv = 1.0 / (ROPE_THETA ** (jnp.arange(0, RET_DK, 2, dtype=F32) / RET_DK))
    ang = jnp.arange(t, dtype=F32)[:, None] * inv[None, :]
    cos_r, sin_r = jnp.cos(ang), jnp.sin(ang)
    tabs = _mla_tables(t)
    row = lambda a, i: a[i:i + 1]

    h1, hn1, s_ret = _ret_layer_fwd(x, row(w['mix_norm'], 0), w['ret_w_in'],
                                    lambda after: fetch('ret_out', after)['ret_w_out'], w['ret_gn'], cos_r, sin_r,
                                    row(w['mlp_norm'], 0), hn=w.get('hn0'), deps=w['deps'])
    h2, hn2, s_mlp0 = _mlp_fwd(0, h1, hn1, fetch('mlp_w1_0', (h1,))['mlp_w1'],
                               lambda after: fetch('mlp_w2_0', after)['mlp_w2'], row(w['ple_norm'], 0))
    w0 = fetch('ple_0', (h2,))
    h3, hn3, s_ple0 = _ple_fwd(0, h2, hn2, p, w0['ple_gate_w'], w0['ple_proj_w'], row(w['mix_norm'], 1))
    wm = fetch('mla', (h3,))
    mla_w = (wm['mla_w_in'], w['mla_q_a_norm'], w['mla_kv_a_norm'], wm['mla_w_uq'], wm['mla_w_ukv'],
             w['mla_q_norm'], w['mla_k_norm'], wm['mla_w_out'], tabs)
    h4, hn4, s_mla = _mla_layer_fwd(h3, hn3, *mla_w, row(w['mlp_norm'], 1))
    w1 = fetch('layer_1', (h4,))
    h5, hn5, s_mlp1 = _mlp_fwd(1, h4, hn4, w1['mlp_w1'], lambda after: w1['mlp_w2'], row(w['ple_norm'], 1))
    dy, sq_err, s_ple1 = _ple_fwd(1, h5, hn5, p, w1['ple_gate_w'], w1['ple_proj_w'], None, target)

    n = N_DEV
    colsh = lambda a: a.reshape(a.shape[0], n, a.shape[1] // n).transpose(1, 0, 2)
    rowsh = lambda a: a.reshape(n, a.shape[0] // n, a.shape[1])
    big = {}

    def emit_group(group):
        big.update(group)
        return emit(group)

    dh5, dg_ple1, dwg1, dwp1 = _ple_bwd(1, dy, s_ple1, p, row(w['ple_norm'], 1), w1['ple_gate_w'])
    dh4, dg_mlp1, dw1_1, dw2_1 = _mlp_bwd(1, dh5, s_mlp1, row(w['mlp_norm'], 1))
    deps = emit_group({('ple_gate_w', 1): rowsh(dwg1), ('ple_proj_w', 1): colsh(dwp1),
                       ('mlp_w2', 1): dw2_1, ('mlp_w1', 1): dw1_1})
    dh3, gm = _mla_layer_bwd(dh4, s_mla, row(w['mix_norm'], 1), *mla_w, deps=deps)
    deps = emit_group({('mla_w_out', 0): rowsh(gm['wmo']), ('mla_w_uq', 0): _gather_rope(gm['wuq']),
                       ('mla_w_ukv', 0): gm['wukv'], ('mla_w_in', 0): rowsh(_gather_rope(gm['wmi']))})
    dh2, dg_ple0, dwg0, dwp0 = _ple_bwd(0, dh3, s_ple0, p, row(w['ple_norm'], 0), w0['ple_gate_w'], deps=deps)
    dh1, dg_mlp0, dw1_0, dw2_0 = _mlp_bwd(0, dh2, s_mlp0, row(w['mlp_norm'], 0))
    deps = emit_group({('ple_gate_w', 0): rowsh(dwg0), ('ple_proj_w', 0): colsh(dwp0),
                       ('mlp_w2', 0): dw2_0, ('mlp_w1', 0): dw1_0})
    dx, dg_mix0, dgn = _ret_layer_bwd(
        dh1, s_ret, row(w['mix_norm'], 0), w['ret_w_in'], w['ret_gn'], cos_r, sin_r,
        lambda dwro: emit_group({('ret_w_out', 0): rowsh(dwro)}),
        lambda dwri: emit_group({('ret_w_in', 0): dwri}), deps=deps)

    small = dict(
        mix_norm=[dg_mix0, gm['mix']], mlp_norm=[dg_mlp0, dg_mlp1], ple_norm=[dg_ple0, dg_ple1],
        ret_gn=dgn, mla_q_a_norm=gm['qa'], mla_kv_a_norm=gm['kva'], mla_q_norm=gm['gq'], mla_k_norm=gm['gk'],
    )
    return sq_err, dx, big, small


def _my_place():
    x, y, c = lax.axis_index("x"), lax.axis_index("y"), lax.axis_index("c")
    return x, y, c


def _flat(px, py, pc):
    return 4 * px + 2 * py + pc


def _peer(x, y, c, r):
    return (1 - x if r & 4 else x, 1 - y if r & 2 else y, 1 - c if r & 1 else c)


HBM = pl.BlockSpec(memory_space=pltpu.HBM)
SEMS = pl.BlockSpec(memory_space=pltpu.SEMAPHORE)
SIDE_EFFECT = pltpu.SideEffectType.DATAFLOW_SIDE_EFFECTING


def _rs_copies(x, y, c, srcs, lands, send_sems, recv_sems):
    copies = []
    for a in range(len(srcs)):
        for r in range(1, N_DEV):
            peer = _peer(x, y, c, r)
            k = a * (N_DEV - 1) + r - 1
            copies.append(pltpu.make_async_remote_copy(
                src_ref=srcs[a].at[_flat(*peer)], dst_ref=lands[a].at[r - 1],
                send_sem=send_sems.at[k], recv_sem=recv_sems.at[k], device_id=peer, device_id_type=MESH))
    return copies


def _rs_start(name, arrays):
    n = len(arrays)
    hbm = lambda a: pltpu.with_memory_space_constraint(a, pltpu.HBM)
    lands = [hbm(lax.empty((N_DEV - 1,) + a.shape[1:], a.dtype)) for a in arrays]

    def body(*refs):
        srcs, lnd = refs[:n], refs[n:2 * n]
        send_sems, recv_sems = refs[2 * n], refs[2 * n + 1]
        token = refs[-1]
        for cp in _rs_copies(*_my_place(), srcs, lnd, send_sems, recv_sems):
            cp.start()
        token[...] = jnp.zeros_like(token)

    outs = pl.pallas_call(
        body, name=name,
        in_specs=[HBM] * (2 * n),
        out_specs=[SEMS, SEMS] + [HBM] * (2 * n) + [pl.BlockSpec(memory_space=pltpu.VMEM)],
        out_shape=[pltpu.SemaphoreType.DMA((n * (N_DEV - 1),)), pltpu.SemaphoreType.DMA((n * (N_DEV - 1),))]
        + [pltpu.HBM(a.shape, a.dtype) for a in arrays] + [pltpu.HBM(l.shape, l.dtype) for l in lands]
        + [_sds((8, 128), F32)],
        input_output_aliases={i: 2 + i for i in range(2 * n)},
        compiler_params=pltpu.CompilerParams(has_side_effects=SIDE_EFFECT),
    )(*[hbm(a) for a in arrays], *lands)
    return outs[0], outs[1], outs[2:2 + n], outs[2 + n:2 + 2 * n], outs[-1]


def _rs_wait(name, send_sems, recv_sems, srcs, lands, after):
    n = len(srcs)

    def body(*refs):
        src_refs, lnd = refs[:n], refs[n:2 * n]
        send, recv = refs[2 * n], refs[2 * n + 1]
        for cp in _rs_copies(*_my_place(), src_refs, lnd, send, recv):
            cp.wait_send()
            cp.wait_recv()

    outs = pl.pallas_call(
        body, name=name,
        in_specs=[HBM] * (2 * n) + [SEMS, SEMS] + [ANY] * len(after),
        out_specs=[HBM] * (2 * n),
        out_shape=[pltpu.HBM(a.shape, a.dtype) for a in list(srcs) + list(lands)],
        input_output_aliases={i: i for i in range(2 * n)},
        compiler_params=pltpu.CompilerParams(has_side_effects=SIDE_EFFECT),
    )(*srcs, *lands, send_sems, recv_sems, *after)
    return outs[:n], outs[n:]


SMALL_PACK_ROWS = 16


def _all_reduce_small(rows, deps=()):
    n = len(rows)

    def body(*refs):
        ins = refs[:n]
        out_ref, mine, buf, send_sems, recv_sems = refs[n + len(deps):]
        x, y, c = _my_place()
        mine[...] = jnp.zeros_like(mine)
        for (r0, a), ref in zip(rows, ins):
            mine[r0:r0 + a.shape[0], 0:a.shape[1]] = ref[...]
        buf[_flat(x, y, c)] = mine[...]
        copies = []
        for r in range(1, N_DEV):
            peer = _peer(x, y, c, r)
            send = pltpu.make_async_remote_copy(
                src_ref=mine, dst_ref=buf.at[_flat(x, y, c)],
                send_sem=send_sems.at[r - 1], recv_sem=recv_sems.at[r - 1], device_id=peer, device_id_type=MESH)
            send.start()
            recv = pltpu.make_async_remote_copy(
                src_ref=mine, dst_ref=buf.at[_flat(*peer)],
                send_sem=send_sems.at[r - 1], recv_sem=recv_sems.at[r - 1], device_id=peer, device_id_type=MESH)
            copies.append((send, recv))
        for send, recv in copies:
            send.wait_send()
            recv.wait_recv()
        acc = buf[0]
        for s in range(1, N_DEV):
            acc = acc + buf[s]
        out_ref[...] = acc

    vm = pl.BlockSpec(memory_space=pltpu.VMEM)
    shape = (SMALL_PACK_ROWS, D_MODEL)
    return pl.pallas_call(
        body, name="all_reduce_small", in_specs=[vm] * n + [ANY] * len(deps), out_specs=vm,
        out_shape=_sds(shape, F32),
        scratch_shapes=[pltpu.VMEM(shape, F32), pltpu.VMEM((N_DEV,) + shape, F32),
                        pltpu.SemaphoreType.DMA((7,)), pltpu.SemaphoreType.DMA((7,))],
    )(*[a for _, a in rows], *deps)


def _adamw_math(w, g, m, v):
    m = ADAM_B1 * m + (1.0 - ADAM_B1) * g
    v = ADAM_B2 * v + (1.0 - ADAM_B2) * (g * g)
    m_hat = m / (1.0 - ADAM_B1 ** ADAM_STEP)
    v_hat = v / (1.0 - ADAM_B2 ** ADAM_STEP)
    delta = -ADAM_LR * (m_hat / (jnp.sqrt(v_hat) + ADAM_EPS) + ADAM_WD * w)
    return delta, m, v


def _adamw_big(name, w, m, v, srcs, lands, me):
    nl, rows, cols = w.shape
    tr = next(cand for cand in (256, 128, 64, 32, 16, 8) if rows % cand == 0)

    def body(me_ref, w_ref, m_ref, v_ref, *rest):
        src_refs, land_refs = rest[:nl], rest[nl:2 * nl]
        g_ref, d_ref, mo_ref, vo_ref = rest[2 * nl:]
        for layer in range(nl):
            @pl.when(pl.program_id(0) == layer)
            def _():
                g = src_refs[layer][...].astype(F32)
                for s in range(N_DEV - 1):
                    g = g + land_refs[layer][s].astype(F32)
                delta, mn, vn = _adamw_math(w_ref[...], g, m_ref[...], v_ref[...])
                g_ref[...] = g
                d_ref[...] = delta
                mo_ref[...] = mn
                vo_ref[...] = vn

    blk = pl.BlockSpec((None, tr, cols), lambda l, i, me_ref: (l, i, 0))
    at = lambda layer, l, i: jnp.where(l == layer, i, 0)
    own = [pl.BlockSpec((None, tr, cols), functools.partial(lambda layer, l, i, me_ref: (me_ref[0], at(layer, l, i), 0),
                                                            layer)) for layer in range(nl)]
    peers = [pl.BlockSpec((N_DEV - 1, tr, cols), functools.partial(lambda layer, l, i, me_ref: (0, at(layer, l, i), 0),
                                                                   layer)) for layer in range(nl)]
    return pl.pallas_call(
        body, name=name,
        grid_spec=pltpu.PrefetchScalarGridSpec(
            num_scalar_prefetch=1, grid=(nl, rows // tr),
            in_specs=[blk, blk, blk] + own + peers, out_specs=[blk] * 4),
        out_shape=[_sds((nl, rows, cols), F32)] * 4,
        compiler_params=_cparams(("arbitrary", "arbitrary")),
    )(me, w, m, v, *srcs, *lands)


def _adamw_small(ws, gs, ms, vs):
    n = len(ws)

    def body(*refs):
        w_refs, g_refs, m_refs, v_refs = (refs[i * n:(i + 1) * n] for i in range(4))
        d_out, m_out, v_out = (refs[(4 + i) * n:(5 + i) * n] for i in range(3))
        for i in range(n):
            delta, mn, vn = _adamw_math(w_refs[i][...], g_refs[i][...], m_refs[i][...], v_refs[i][...])
            d_out[i][...] = delta
            m_out[i][...] = mn
            v_out[i][...] = vn

    vm = pl.BlockSpec(memory_space=pltpu.VMEM)
    outs = pl.pallas_call(
        body, name="adamw_small", in_specs=[vm] * (4 * n), out_specs=[vm] * (3 * n),
        out_shape=[_sds(a.shape, F32) for a in ws] * 3,
    )(*ws, *gs, *ms, *vs)
    return outs[:n], outs[n:2 * n], outs[2 * n:]


def _pad_to(a, rows, cols):
    return jnp.pad(a, ((0, rows - a.shape[0]), (0, cols - a.shape[1])))


def _place_own(blocks):
    me = _flat(*_my_place())
    return [lax.dynamic_update_slice(lax.empty((N_DEV,) + b.shape, b.dtype), b[None], (me,) + (0,) * b.ndim)
            for b in blocks]


def _ag_copies(x, y, c, blocks, bufs, send_sems, recv_sems, arriving):
    copies = []
    for a in range(len(blocks)):
        for r in range(1, N_DEV):
            peer = _peer(x, y, c, r)
            k = a * (N_DEV - 1) + r - 1
            copies.append(pltpu.make_async_remote_copy(
                src_ref=blocks[a], dst_ref=bufs[a].at[_flat(*(peer if arriving else (x, y, c)))],
                send_sem=send_sems.at[k], recv_sem=recv_sems.at[k], device_id=peer, device_id_type=MESH))
    return copies


def _ag_start(groups, after):
    flat = [pair for g in groups for pair in g]
    n, ng = len(flat), len(groups)
    hbm = lambda a: pltpu.with_memory_space_constraint(a, pltpu.HBM)

    def body(*refs):
        blocks, bufs = refs[:n], refs[n:2 * n]
        sems = refs[2 * n + len(after):2 * n + len(after) + 2 * ng]
        x, y, c = _my_place()
        at = 0
        for gi, g in enumerate(groups):
            for cp in _ag_copies(x, y, c, blocks[at:at + len(g)], bufs[at:at + len(g)], sems[2 * gi],
                                 sems[2 * gi + 1], arriving=False):
                cp.start()
            at += len(g)
        refs[-1][...] = jnp.zeros_like(refs[-1])

    sem_shapes = [pltpu.SemaphoreType.DMA((len(g) * (N_DEV - 1),)) for g in groups for _ in range(2)]
    outs = pl.pallas_call(
        body, name="gather_start",
        in_specs=[HBM] * (2 * n) + [ANY] * len(after),
        out_specs=[SEMS] * (2 * ng) + [HBM] * (2 * n) + [pl.BlockSpec(memory_space=pltpu.VMEM)],
        out_shape=sem_shapes + [pltpu.HBM(b.shape, b.dtype) for b, _ in flat]
        + [pltpu.HBM(u.shape, u.dtype) for _, u in flat] + [_sds((8, 128), F32)],
        input_output_aliases={i: 2 * ng + i for i in range(2 * n)},
        compiler_params=pltpu.CompilerParams(has_side_effects=SIDE_EFFECT),
    )(*[hbm(b) for b, _ in flat], *[hbm(u) for _, u in flat], *after)
    blocks_thru, bufs_thru = outs[2 * ng:2 * ng + n], outs[2 * ng + n:2 * ng + 2 * n]
    started, at = [], 0
    for gi, g in enumerate(groups):
        started.append((outs[2 * gi], outs[2 * gi + 1], blocks_thru[at:at + len(g)], bufs_thru[at:at + len(g)]))
        at += len(g)
    return started, outs[-1]


def _ag_wait(name, send_sems, recv_sems, blocks, bufs, after):
    n = len(blocks)

    def body(*refs):
        for cp in _ag_copies(*_my_place(), refs[:n], refs[n:2 * n], refs[2 * n], refs[2 * n + 1], arriving=True):
            cp.wait_send()
            cp.wait_recv()

    outs = pl.pallas_call(
        body, name=name,
        in_specs=[HBM] * (2 * n) + [SEMS, SEMS] + [ANY] * len(after),
        out_specs=[HBM] * (2 * n),
        out_shape=[pltpu.HBM(a.shape, a.dtype) for a in list(blocks) + list(bufs)],
        input_output_aliases={i: i for i in range(2 * n)},
        compiler_params=pltpu.CompilerParams(has_side_effects=SIDE_EFFECT),
    )(*blocks, *bufs, send_sems, recv_sems, *after)
    return outs[n:]


def _split_call(name, body, thru, sems_in, new_sems, after):
    n, ns, nn = len(thru), len(sems_in), len(new_sems)
    hbm = lambda a: pltpu.with_memory_space_constraint(a, pltpu.HBM)

    def wrapped(*refs):
        body(refs[:n], refs[n:n + ns], refs[n + ns + len(after):n + ns + len(after) + nn])
        refs[-1][...] = jnp.zeros_like(refs[-1])

    outs = pl.pallas_call(
        wrapped, name=name,
        in_specs=[HBM] * n + [SEMS] * ns + [ANY] * len(after),
        out_specs=[SEMS] * nn + [HBM] * n + [pl.BlockSpec(memory_space=pltpu.VMEM)],
        out_shape=[pltpu.SemaphoreType.DMA((k,)) for k in new_sems] + [pltpu.HBM(a.shape, a.dtype) for a in thru]
        + [_sds((8, 128), F32)],
        input_output_aliases={i: nn + i for i in range(n)},
        compiler_params=pltpu.CompilerParams(has_side_effects=SIDE_EFFECT),
    )(*[hbm(a) for a in thru], *sems_in, *after)
    return list(outs[:nn]), list(outs[nn:nn + n]), outs[-1]


def _first_gather(blocks, bufs, overlap):
    n = len(blocks)

    def copies(refs, s1, r1, s2, r2):
        x, y, c = _my_place()
        me, sibling = (x, y, c), (x, y, 1 - c)
        chips = [(1 - x, y), (x, 1 - y), (1 - x, 1 - y)]
        blk, buf = refs[:n], refs[n:]
        out = dict(send1=[], recv1_sib=[], recv1_ici=[], send2=[], recv2=[])
        for a in range(n):
            place = lambda dev: buf[a].at[_flat(*dev)]
            for k, to in enumerate([sibling] + [(*chip, c) for chip in chips]):
                mk = lambda dst: pltpu.make_async_remote_copy(
                    src_ref=blk[a], dst_ref=dst, send_sem=s1.at[4 * a + k], recv_sem=r1.at[4 * a + k],
                    device_id=to, device_id_type=MESH)
                out['send1'].append(mk(place(me)))
                out['recv1_sib' if k == 0 else 'recv1_ici'].append(mk(place(to)))
            for j, chip in enumerate(chips):
                mk = lambda dev: pltpu.make_async_remote_copy(
                    src_ref=place(dev), dst_ref=place(dev), send_sem=s2.at[3 * a + j], recv_sem=r2.at[3 * a + j],
                    device_id=sibling, device_id_type=MESH)
                out['send2'].append(mk((*chip, c)))
                out['recv2'].append(mk((*chip, 1 - c)))
        return out

    def start(refs, sems_in, new):
        for cp in copies(refs, new[0], new[1], new[0], new[1])['send1']:
            cp.start()

    def forward(refs, sems_in, new):
        cps = copies(refs, sems_in[0], sems_in[1], new[0], new[1])
        for cp in cps['recv1_ici']:
            cp.wait_recv()
        for cp in cps['send2']:
            cp.start()

    def finish(refs, sems_in, new):
        cps = copies(refs, *sems_in)
        for cp in cps['recv1_sib'] + cps['recv2']:
            cp.wait_recv()
        for cp in cps['send1'] + cps['send2']:
            cp.wait_send()

    sems1, thru, token = _split_call("first_gather_start", start, list(blocks) + list(bufs), [], [4 * n, 4 * n], ())
    after = overlap(token)
    sems2, thru, token = _split_call("first_gather_forward", forward, thru, sems1, [3 * n, 3 * n], after)
    _, thru, _ = _split_call("first_gather_wait", finish, thru, sems1 + sems2, [], ())
    return thru[n:], token


def _prepare_weights(p, x):
    n = N_DEV
    bf = lambda a: a.astype(BF16)
    gn_pack = jnp.concatenate([
        _pad_to(p['ret_gn'][0], RET_HEADS, 128), _pad_to(p['mla_q_a_norm'], 1, 128),
        _pad_to(p['mla_kv_a_norm'], 1, 128), jnp.zeros((2, 128), F32)], axis=0)
    ple = lambda l: [bf(p['ple_gate_w'][l]), bf(p['ple_proj_w'][l])]
    names = ('ret_out', 'mlp_w1_0', 'mlp_w2_0', 'ple_0', 'mla', 'layer_1')
    later = [[bf(p['ret_w_out'][0])], [bf(p['mlp_w1'][0])], [bf(p['mlp_w2'][0])], ple(0),
             [bf(p['mla_w_in'][0]), bf(p['mla_w_uq'][0]), bf(p['mla_w_ukv'][0]), bf(p['mla_w_out'][0])],
             [bf(p['mlp_w1'][1]), bf(p['mlp_w2'][1])] + ple(1)]
    first = [gn_pack, bf(p['ret_w_in'][0])]
    behind = {}

    def overlap(token):
        behind['hn0'] = _rms_fwd("mix_norm0", x, p['mix_norm'][0:1], deps=(token,))
        behind['bufs'] = _place_own([b for g in later for b in g])
        return (behind['hn0'], *behind['bufs'])

    (pack, wri), token = _first_gather(first, _place_own(first), overlap)
    bufs = behind['bufs']
    groups, at = [], 0
    for g in later:
        groups.append(list(zip(g, bufs[at:at + len(g)])))
        at += len(g)
    started, token = _ag_start(groups, (token,))

    w = {k: p[k] for k in ('mix_norm', 'mlp_norm', 'ple_norm')}
    w['hn0'] = behind['hn0']
    w['ret_gn'] = pack[:, :RET_HEADS, :RET_DV // n].transpose(1, 0, 2).reshape(RET_HEADS, RET_DV)
    w['mla_q_a_norm'] = pack[:, RET_HEADS, :MLA_Q_RANK // n].reshape(1, MLA_Q_RANK)
    w['mla_kv_a_norm'] = pack[:, RET_HEADS + 1, :MLA_KV_RANK // n].reshape(1, MLA_KV_RANK)
    w['ret_w_in'] = wri
    w['mla_q_norm'] = _spread_rope(p['mla_q_norm'])
    w['mla_k_norm'] = _spread_rope(p['mla_k_norm'])
    w['deps'] = (token,)

    def fetch(name, after):
        got = list(_ag_wait("gather_wait_" + name, *started[names.index(name)], after))
        if name == 'ret_out':
            return dict(ret_w_out=got[0].reshape(RET_V_W, D_MODEL))
        if name == 'mla':
            wmi, wuq, wukv, wmo = got
            return dict(mla_w_in=_spread_rope(wmi.reshape(D_MODEL, MLA_IN)), mla_w_uq=_spread_rope(wuq),
                        mla_w_ukv=wukv, mla_w_out=wmo.reshape(D_MODEL, D_MODEL))
        out = {}
        if name in ('mlp_w1_0', 'layer_1'):
            out['mlp_w1'] = got.pop(0)
        if name in ('mlp_w2_0', 'layer_1'):
            out['mlp_w2'] = got.pop(0)
        if name in ('ple_0', 'layer_1'):
            out['ple_gate_w'] = got[0].reshape(D_MODEL, D_MODEL)
            out['ple_proj_w'] = got[1].transpose(1, 0, 2).reshape(PLE_DIM, D_MODEL)
        return out

    return w, fetch


def _small_grads(small, after):
    rows = [(0, small['mix_norm'][0]), (1, small['mix_norm'][1]), (2, small['mlp_norm'][0]),
            (3, small['mlp_norm'][1]), (4, small['ple_norm'][0]), (5, small['ple_norm'][1]),
            (6, small['ret_gn']), (10, small['mla_q_a_norm']), (11, small['mla_kv_a_norm']),
            (12, small['mla_q_norm']), (13, small['mla_k_norm'])]
    gs = _all_reduce_small(rows, after)
    me = _flat(*_my_place())
    n = N_DEV
    return dict(
        mix_norm=gs[0:2], mlp_norm=gs[2:4], ple_norm=gs[4:6],
        ret_gn=lax.dynamic_slice(gs, (6, me * (RET_DV // n)), (RET_HEADS, RET_DV // n)),
        mla_q_a_norm=lax.dynamic_slice(gs, (10, me * (MLA_Q_RANK // n)), (1, MLA_Q_RANK // n)),
        mla_kv_a_norm=lax.dynamic_slice(gs, (11, me * (MLA_KV_RANK // n)), (1, MLA_KV_RANK // n)),
        mla_q_norm=_gather_rope(gs[12:13, :MLA_HD_PAD]), mla_k_norm=_gather_rope(gs[13:14, :MLA_HD_PAD]))


def kernel(x, p, mix_norm, ret_w_in, ret_gn, ret_w_out, mla_w_in, mla_q_a_norm, mla_kv_a_norm, mla_w_uq, mla_w_ukv, mla_q_norm, mla_k_norm, mla_w_out, mlp_norm, mlp_w1, mlp_w2, ple_norm, ple_gate_w, ple_proj_w, loss_target, m_mix_norm, m_ret_w_in, m_ret_gn, m_ret_w_out, m_mla_w_in, m_mla_q_a_norm, m_mla_kv_a_norm, m_mla_w_uq, m_mla_w_ukv, m_mla_q_norm, m_mla_k_norm, m_mla_w_out, m_mlp_norm, m_mlp_w1, m_mlp_w2, m_ple_norm, m_ple_gate_w, m_ple_proj_w, v_mix_norm, v_ret_w_in, v_ret_gn, v_ret_w_out, v_mla_w_in, v_mla_q_a_norm, v_mla_kv_a_norm, v_mla_w_uq, v_mla_w_ukv, v_mla_q_norm, v_mla_k_norm, v_mla_w_out, v_mlp_norm, v_mlp_w1, v_mlp_w2, v_ple_norm, v_ple_gate_w, v_ple_proj_w):
    given = dict(locals())
    params = {n: given[n] for n in WEIGHTS}
    w, fetch = _prepare_weights(params, x[0])

    started = []

    def emit(group):
        keys = list(group)
        send, recv, srcs, lands, token = _rs_start(f"rs_start{len(started)}", [group[k] for k in keys])
        started.append((keys, send, recv, srcs, lands))
        return (token,)

    sq_err, grad_x, _, small = _local_step(x[0], p, loss_target[0], w, fetch, emit)
    loss = lax.psum(0.5 / D_MODEL * sq_err[0, 0], ("x", "y", "c"))

    grads, deltas, new_m, new_v = {}, {}, {}, {}

    def small_updates(after):
        sg = _small_grads(small, after)
        two_d = lambda a: a.reshape(-1, a.shape[-1])
        d_s, m_s, v_s = _adamw_small(
            [two_d(params[n]) for n in SMALL], [sg[n] for n in SMALL],
            [two_d(given["m_" + n]) for n in SMALL], [two_d(given["v_" + n]) for n in SMALL])
        for i, n in enumerate(SMALL):
            shape = params[n].shape
            grads[n], deltas[n], new_m[n], new_v[n] = (a.reshape(shape) for a in (sg[n], d_s[i], m_s[i], v_s[i]))
        return (d_s[0],)

    me = _flat(*_my_place()).astype(jnp.int32).reshape(1)
    after = (grad_x,)
    src_of, land_of = {}, {}
    for gi, (keys, send, recv, srcs, lands) in enumerate(started):
        if gi == len(started) - 1:
            after = small_updates(after)
        srcs, lands = _rs_wait(f"rs_wait{gi}", send, recv, srcs, lands, after)
        for k, s, l in zip(keys, srcs, lands):
            src_of[k], land_of[k] = s, l
        done = [n for n in BIG if n not in grads and all((n, l) in src_of for l in range(params[n].shape[0]))]
        for n in done:
            layers = range(params[n].shape[0])
            grads[n], deltas[n], new_m[n], new_v[n] = _adamw_big(
                "adamw_" + n, params[n], given["m_" + n], given["v_" + n],
                [src_of[(n, l)] for l in layers], [land_of[(n, l)] for l in layers], me)
        if done:
            after = tuple(deltas[n] for n in done)

    return (loss, grad_x[None], *[grads[n] for n in WEIGHTS], *[deltas[n] for n in WEIGHTS],
            *[new_m[n] for n in WEIGHTS], *[new_v[n] for n in WEIGHTS])
```

```python
import functools

import jax
import jax.numpy as jnp
from jax import lax
from jax.experimental import pallas as pl
from jax.experimental.pallas import tpu as pltpu

F32 = jnp.float32
BF16 = jnp.bfloat16
MESH = pl.DeviceIdType.MESH
ANY = pl.BlockSpec(memory_space=pl.ANY)

N_DEV = 8
D_MODEL = 1024
CHUNK = 64
EPS = 1e-6
ROPE_THETA = 10000.0
RET_HEADS = 4
RET_DK = 256
RET_DV = 512
RET_QK_W = RET_HEADS * RET_DK
RET_V_W = RET_HEADS * RET_DV
RET_IN = 2 * RET_QK_W + 2 * RET_V_W
MLA_HEADS = 8
MLA_NOPE = 128
MLA_ROPE = 64
MLA_QKD = MLA_NOPE + MLA_ROPE
MLA_VD = 128
MLA_Q_RANK = 384
MLA_KV_RANK = 256
MLA_IN = MLA_Q_RANK + MLA_KV_RANK + MLA_ROPE
MLA_IN_PAD = 768
MLA_HD_PAD = 256
D_FF = 4096
PLE_DIM = 256
ATT_SCALE = MLA_QKD ** -0.5
LOG2E = 1.4426950408889634
ATT_EXP2 = ATT_SCALE * LOG2E

ADAM_LR = 0.001
ADAM_B1 = 0.9
ADAM_B2 = 0.999
ADAM_EPS = 1e-08
ADAM_WD = 0.01
ADAM_STEP = 10

VMEM_LIMIT = 52 * 1024 * 1024
ROW_TILE = 1024
RET_ROWS = 256
ATT_BLOCK = 256
ATT_QROWS = 1024
ATT_KROWS = 1024
ATT_HEADS = 2

WEIGHTS = ['mix_norm', 'ret_w_in', 'ret_gn', 'ret_w_out', 'mla_w_in', 'mla_q_a_norm', 'mla_kv_a_norm',
           'mla_w_uq', 'mla_w_ukv', 'mla_q_norm', 'mla_k_norm', 'mla_w_out', 'mlp_norm', 'mlp_w1', 'mlp_w2',
           'ple_norm', 'ple_gate_w', 'ple_proj_w']
BIG = ['ret_w_in', 'ret_w_out', 'mla_w_in', 'mla_w_uq', 'mla_w_ukv', 'mla_w_out', 'mlp_w1', 'mlp_w2',
       'ple_gate_w', 'ple_proj_w']
SMALL = [w for w in WEIGHTS if w not in BIG]


def _cparams(sem=None):
    return pltpu.CompilerParams(dimension_semantics=sem, vmem_limit_bytes=VMEM_LIMIT)


def _dot(a, b, ca, cb):
    return lax.dot_general(a, b, (((ca,), (cb,)), ((), ())), preferred_element_type=F32)


def _bf(v):
    return v if v.dtype == BF16 else v.astype(BF16)


def _sigmoid(z):
    return 1.0 / (1.0 + jnp.exp(-z))


def _mm(name, grid, a, a_spec, b, b_spec, contract, outs, extras=(), epi=None, deps=(), split=None):
    nk = grid[2]
    n_ex, n_out, n_dep = len(extras), len(outs), len(deps)
    acc_shape = tuple(d for d in outs[0][1].block_shape if d is not None)
    if split is not None:
        acc_shape = (acc_shape[1], acc_shape[0] * split)

    def body(*refs):
        a_ref, b_ref = refs[:2]
        ex_refs = refs[2:2 + n_ex]
        out_refs = refs[2 + n_ex + n_dep:2 + n_ex + n_dep + n_out]

        def product():
            return _dot(_bf(a_ref[...]), _bf(b_ref[...]), contract[0], contract[1])

        def finish(acc):
            if split is not None:
                for j in range(acc_shape[1] // split):
                    out_refs[0][j] = acc[:, j * split:(j + 1) * split].astype(out_refs[0].dtype)
                return
            acc = acc[...]
            res = epi(acc, *[r[...] for r in ex_refs]) if epi is not None else (acc,)
            for o, r in zip(out_refs, res):
                o[...] = r.astype(o.dtype)

        if nk == 1:
            finish(product())
        else:
            acc_ref = refs[-1]
            k = pl.program_id(2)

            @pl.when(k == 0)
            def _():
                acc_ref[...] = jnp.zeros_like(acc_ref)

            acc_ref[...] += product()

            @pl.when(k == nk - 1)
            def _():
                finish(acc_ref)

    return pl.pallas_call(
        body, name=name, grid=grid,
        in_specs=[a_spec, b_spec] + [s for _, s in extras] + [ANY] * n_dep,
        out_specs=[s for _, s in outs],
        out_shape=[s for s, _ in outs],
        scratch_shapes=[pltpu.VMEM(acc_shape, F32)] if nk > 1 else [],
        compiler_params=_cparams(("parallel", "parallel", "arbitrary")),
    )(a, b, *[x for x, _ in extras], *deps)


def _mm_rows(name, tm, a, w, mode, outs, extras=(), epi=None, deps=()):
    n_sh, rows, cols = w.shape
    n_ex, n_out, n_dep = len(extras), len(outs), len(deps)
    by_cols = mode in ('nn_cols', 'nt_rows')
    width = cols if mode == 'nn_cols' else rows

    def body(*refs):
        a_ref, w_ref = refs[:2]
        ex_refs = refs[2:2 + n_ex]
        out_refs = refs[2 + n_ex + n_dep:2 + n_ex + n_dep + n_out]
        if by_cols:
            av = _bf(a_ref[...])
            for s in range(n_sh):
                cs = slice(s * width, (s + 1) * width)
                acc = _dot(av, w_ref[s], 1, 0 if mode == 'nn_cols' else 1)
                res = epi(acc, *[r[:, cs] for r in ex_refs]) if epi is not None else (acc,)
                for o, r in zip(out_refs, res):
                    o[:, cs] = r.astype(o.dtype)
        else:
            chunk = rows if mode == 'nn_rows' else cols
            acc = None
            for s in range(n_sh):
                part = _dot(_bf(a_ref[:, s * chunk:(s + 1) * chunk]), w_ref[s], 1, 0 if mode == 'nn_rows' else 1)
                acc = part if acc is None else acc + part
            res = epi(acc, *[r[...] for r in ex_refs]) if epi is not None else (acc,)
            for o, r in zip(out_refs, res):
                o[...] = r.astype(o.dtype)

    t, ka = a.shape
    return pl.pallas_call(
        body, name=name, grid=(t // tm, 1, 1),
        in_specs=[pl.BlockSpec((tm, ka), lambda i, j, k: (i, 0)),
                  pl.BlockSpec((n_sh, rows, cols), lambda i, j, k: (0, 0, 0))] + [s for _, s in extras] + [ANY] * n_dep,
        out_specs=[s for _, s in outs],
        out_shape=[s for s, _ in outs],
        compiler_params=_cparams(("parallel", "arbitrary", "arbitrary")),
    )(a, w, *[x for x, _ in extras], *deps)


def _sds(shape, dtype):
    return jax.ShapeDtypeStruct(shape, dtype)


def _row_tile(t, cap=ROW_TILE):
    return min(cap, t)


def _rms_fwd(name, x, g, deps=()):
    t, d = x.shape
    tm = _row_tile(t)

    def body(x_ref, g_ref, *rest):
        o_ref = rest[-1]
        xv = x_ref[...]
        r = lax.rsqrt(jnp.mean(xv * xv, axis=-1, keepdims=True) + EPS)
        o_ref[...] = (xv * r * g_ref[...]).astype(o_ref.dtype)

    return pl.pallas_call(
        body, name=name, grid=(t // tm,),
        in_specs=[pl.BlockSpec((tm, d), lambda i: (i, 0)), pl.BlockSpec((1, d), lambda i: (0, 0))] + [ANY] * len(deps),
        out_specs=pl.BlockSpec((tm, d), lambda i: (i, 0)),
        out_shape=_sds((t, d), BF16),
        compiler_params=_cparams(("parallel",)),
    )(x, g, *deps)


def _rms_bwd_rows(dy, xv, g, n):
    r = lax.rsqrt(jnp.sum(xv * xv, axis=-1, keepdims=True) / n + EPS)
    xh = xv * r
    dxh = dy * g
    dx = r * (dxh - xh * (jnp.sum(dxh * xh, axis=-1, keepdims=True) / n))
    return dx, dy * xh


def _ple_gate_bwd(name, dh, gate, e):
    t, d = dh.shape
    tm = _row_tile(t)

    def body(dh_ref, g_ref, e_ref, de_ref, dz_ref):
        dh_v, gt = dh_ref[...], g_ref[...].astype(F32)
        de_ref[...] = (dh_v * gt).astype(BF16)
        dz_ref[...] = (dh_v * e_ref[...].astype(F32) * (gt * (1.0 - gt))).astype(BF16)

    row = pl.BlockSpec((tm, d), lambda i: (i, 0))
    return pl.pallas_call(
        body, name=name, grid=(t // tm,), in_specs=[row, row, row], out_specs=[row, row],
        out_shape=[_sds((t, d), BF16), _sds((t, d), BF16)],
        compiler_params=_cparams(("parallel",)),
    )(dh, gate, e)


def _rope_half(v, cos, sin):
    half = v.shape[-1] // 2
    v1, v2 = v[:, :half], v[:, half:]
    return jnp.concatenate([v1 * cos - v2 * sin, v2 * cos + v1 * sin], axis=-1)


def _ret_consts():
    lg = jnp.log(1.0 - 2.0 ** (-5.0 - jnp.arange(RET_HEADS, dtype=F32)))
    idx = jnp.arange(CHUNK, dtype=F32)
    intra = jnp.exp(lg[:, None, None] * jnp.abs(idx[:, None] - idx[None, :]))
    qdec = jnp.exp(lg[:, None] * (idx + 1.0))
    kdec = jnp.exp(lg[:, None] * (CHUNK - 1.0 - idx))
    cdec = jnp.exp(lg * CHUNK)
    qdec = jnp.broadcast_to(qdec[:, :, None], (RET_HEADS, CHUNK, RET_DK))
    kdec = jnp.broadcast_to(kdec[:, :, None], (RET_HEADS, CHUNK, RET_DK))
    cdec = jnp.broadcast_to(cdec[:, None, None], (RET_HEADS, 1, RET_DV))
    return intra, qdec, kdec, cdec


def _ret_specs(rb, rev_nb=None):
    blk = (lambda i: i) if rev_nb is None else (lambda i: rev_nb - 1 - i)
    full = lambda shape: pl.BlockSpec(shape, lambda i: (0,) * len(shape))
    return dict(
        proj=pl.BlockSpec((rb, RET_IN), lambda i: (blk(i), 0)),
        tab=pl.BlockSpec((rb, RET_DK // 2), lambda i: (blk(i), 0)),
        vw=pl.BlockSpec((rb, RET_V_W), lambda i: (blk(i), 0)),
        st=pl.BlockSpec((rb // CHUNK, RET_HEADS, RET_DK, RET_DV), lambda i: (blk(i), 0, 0, 0)),
        gn=full((RET_HEADS, 1, RET_DV)),
        intra=full((RET_HEADS, CHUNK, CHUNK)),
        dec=full((RET_HEADS, CHUNK, RET_DK)),
        cdec=full((RET_HEADS, 1, RET_DV)),
    )


def _ret_fwd(proj, cos, sin, gn):
    t = proj.shape[0]
    rb = min(RET_ROWS, t)
    cpb = rb // CHUNK
    intra, qdec, kdec, cdec = _ret_consts()
    sp = _ret_specs(rb)

    def body(proj_ref, cos_ref, sin_ref, gn_ref, intra_ref, qd_ref, kd_ref, cd_ref,
             gated_ref, outp_ref, st_ref, s_ref):
        @pl.when(pl.program_id(0) == 0)
        def _():
            s_ref[...] = jnp.zeros_like(s_ref)

        def chunk(c, carry):
            rows = pl.ds(pl.multiple_of(c * CHUNK, CHUNK), CHUNK)
            cs, sn = cos_ref[rows, :], sin_ref[rows, :]
            for h in range(RET_HEADS):
                q = proj_ref[rows, h * RET_DK:(h + 1) * RET_DK].astype(F32)
                k = proj_ref[rows, RET_QK_W + h * RET_DK:RET_QK_W + (h + 1) * RET_DK].astype(F32)
                v = proj_ref[rows, 2 * RET_QK_W + h * RET_DV:2 * RET_QK_W + (h + 1) * RET_DV]
                g = proj_ref[rows, 2 * RET_QK_W + RET_V_W + h * RET_DV:
                             2 * RET_QK_W + RET_V_W + (h + 1) * RET_DV].astype(F32)
                qr = _rope_half(q, cs, sn)
                kr = _rope_half(k, cs, sn) * (RET_DK ** -0.5)
                qb, kb, vb = qr.astype(BF16), kr.astype(BF16), v
                sc = _dot(qb, kb, 1, 1) * intra_ref[h]
                inner = _dot(sc.astype(BF16), vb, 1, 0)
                s_old = s_ref[h]
                sb = s_old.astype(BF16)
                st_ref[c, h] = sb
                cross = _dot((qr * qd_ref[h]).astype(BF16), sb, 1, 0)
                out = inner + cross
                s_ref[h] = s_old * cd_ref[h] + _dot((kr * kd_ref[h]).astype(BF16), vb, 0, 0)
                r = lax.rsqrt(jnp.mean(out * out, axis=-1, keepdims=True) + EPS)
                y = out * r * gn_ref[h]
                cols = slice(h * RET_DV, (h + 1) * RET_DV)
                gated_ref[rows, cols] = (g * _sigmoid(g) * y).astype(BF16)
                outp_ref[rows, cols] = out
            return carry

        lax.fori_loop(0, cpb, chunk, 0)

    return pl.pallas_call(
        body, name="ret_fwd", grid=(t // rb,),
        in_specs=[sp['proj'], sp['tab'], sp['tab'], sp['gn'], sp['intra'], sp['dec'], sp['dec'], sp['cdec']],
        out_specs=[sp['vw'], sp['vw'], sp['st']],
        out_shape=[_sds((t, RET_V_W), BF16), _sds((t, RET_V_W), F32),
                   _sds((t // CHUNK, RET_HEADS, RET_DK, RET_DV), BF16)],
        scratch_shapes=[pltpu.VMEM((RET_HEADS, RET_DK, RET_DV), F32)],
        compiler_params=_cparams(("arbitrary",)),
    )(proj, cos, sin, gn.reshape(RET_HEADS, 1, RET_DV), intra, qdec, kdec, cdec)


def _ret_gate_bwd_epi(dgt, out, g, gn):
    g = g.astype(F32)
    r = lax.rsqrt(jnp.mean(out * out, axis=-1, keepdims=True) + EPS)
    xh = out * r
    sg = _sigmoid(g)
    dgate = dgt * (xh * gn) * (sg * (1.0 + g * (1.0 - sg)))
    dy = dgt * (g * sg)
    dxh = dy * gn
    dout = r * (dxh - xh * jnp.mean(dxh * xh, axis=-1, keepdims=True))
    return dout, dgate, jnp.sum(dy * xh, axis=0, keepdims=True)


def _ret_bwd(proj, cos, sin, states, dout, dgate, deps=()):
    t = proj.shape[0]
    rb = min(RET_ROWS, t)
    cpb = rb // CHUNK
    nb = t // rb
    intra, qdec, kdec, cdec = _ret_consts()
    sp = _ret_specs(rb, rev_nb=nb)

    def body(proj_ref, cos_ref, sin_ref, intra_ref, qd_ref, kd_ref, cd_ref, st_ref, dout_ref, dgate_ref, *rest):
        dproj_ref, ds_ref = rest[len(deps):]

        @pl.when(pl.program_id(0) == 0)
        def _():
            ds_ref[...] = jnp.zeros_like(ds_ref)

        def chunk(cc, carry):
            c = cpb - 1 - cc
            rows = pl.ds(pl.multiple_of(c * CHUNK, CHUNK), CHUNK)
            cs, sn = cos_ref[rows, :], sin_ref[rows, :]
            for h in range(RET_HEADS):
                q = proj_ref[rows, h * RET_DK:(h + 1) * RET_DK].astype(F32)
                k = proj_ref[rows, RET_QK_W + h * RET_DK:RET_QK_W + (h + 1) * RET_DK].astype(F32)
                v = proj_ref[rows, 2 * RET_QK_W + h * RET_DV:2 * RET_QK_W + (h + 1) * RET_DV]
                cols = slice(h * RET_DV, (h + 1) * RET_DV)
                qr = _rope_half(q, cs, sn)
                kr = _rope_half(k, cs, sn) * (RET_DK ** -0.5)
                qb, kb, vb = qr.astype(BF16), kr.astype(BF16), v
                qdb = (qr * qd_ref[h]).astype(BF16)
                kdb = (kr * kd_ref[h]).astype(BF16)
                doutb = dout_ref[rows, cols]
                itr = intra_ref[h]
                pb = (_dot(qb, kb, 1, 1) * itr).astype(BF16)
                dv = _dot(pb, doutb, 0, 0)
                dsc = (_dot(doutb, vb, 1, 1) * itr).astype(BF16)
                dq = _dot(dsc, kb, 1, 0)
                dk = _dot(dsc, qb, 0, 0)
                dq = dq + _dot(doutb, st_ref[c, h], 1, 1) * qd_ref[h]
                ds_new = ds_ref[h]
                dsb = ds_new.astype(BF16)
                dk = dk + _dot(vb, dsb, 1, 1) * kd_ref[h]
                dv = dv + _dot(kdb, dsb, 1, 0)
                ds_ref[h] = ds_new * cd_ref[h] + _dot(qdb, doutb, 0, 0)
                dproj_ref[rows, h * RET_DK:(h + 1) * RET_DK] = _rope_half(dq, cs, -sn).astype(BF16)
                dproj_ref[rows, RET_QK_W + h * RET_DK:RET_QK_W + (h + 1) * RET_DK] = (
                    _rope_half(dk * (RET_DK ** -0.5), cs, -sn).astype(BF16))
                dproj_ref[rows, 2 * RET_QK_W + h * RET_DV:2 * RET_QK_W + (h + 1) * RET_DV] = dv.astype(BF16)
                dproj_ref[rows, 2 * RET_QK_W + RET_V_W + h * RET_DV:
                          2 * RET_QK_W + RET_V_W + (h + 1) * RET_DV] = dgate_ref[rows, cols]
            return carry

        lax.fori_loop(0, cpb, chunk, 0)

    return pl.pallas_call(
        body, name="ret_bwd", grid=(nb,),
        in_specs=[sp['proj'], sp['tab'], sp['tab'], sp['intra'], sp['dec'], sp['dec'], sp['cdec'],
                  sp['st'], sp['vw'], sp['vw']] + [ANY] * len(deps),
        out_specs=sp['proj'],
        out_shape=_sds((t, RET_IN), BF16),
        scratch_shapes=[pltpu.VMEM((RET_HEADS, RET_DK, RET_DV), F32)],
        compiler_params=_cparams(("arbitrary",)),
    )(proj, cos, sin, intra, qdec, kdec, cdec, states, dout, dgate, *deps)


def _spread_rope(a):
    return jnp.pad(a, [(0, 0)] * (a.ndim - 1) + [(0, MLA_ROPE)])


def _gather_rope(a):
    return a[..., :a.shape[-1] - MLA_ROPE]


def _mla_tables(t, cos_r, sin_r):
    half = MLA_ROPE // 2
    step = (RET_DK // 2) // half
    cos, sin = cos_r[:, ::step], sin_r[:, ::step]
    z = jnp.zeros((t, half), F32)
    c = jnp.concatenate([cos, cos, z, z], axis=1)
    s1 = jnp.concatenate([-sin, z, z, z], axis=1)
    s2 = jnp.concatenate([z, sin, z, z], axis=1)
    return c, s1, s2


def _rope_tile(r, c, s1, s2):
    return r * c + pltpu.roll(r, 96, 1) * s1 + pltpu.roll(r, 32, 1) * s2


def _mla_mid(proj2, qa, kva):
    t = proj2.shape[0]
    tm = _row_tile(t)

    def body(p_ref, qa_ref, kva_ref, cq_ref, ckv_ref):
        cq = p_ref[:, :MLA_Q_RANK]
        ckv = p_ref[:, MLA_Q_RANK:MLA_Q_RANK + MLA_KV_RANK]
        rq = lax.rsqrt(jnp.mean(cq * cq, axis=-1, keepdims=True) + EPS)
        rkv = lax.rsqrt(jnp.mean(ckv * ckv, axis=-1, keepdims=True) + EPS)
        cq_ref[...] = (cq * rq * qa_ref[...]).astype(BF16)
        ckv_ref[...] = (ckv * rkv * kva_ref[...]).astype(BF16)

    return pl.pallas_call(
        body, name="mla_mid", grid=(t // tm,),
        in_specs=[pl.BlockSpec((tm, MLA_IN_PAD), lambda i: (i, 0)),
                  pl.BlockSpec((1, MLA_Q_RANK), lambda i: (0, 0)),
                  pl.BlockSpec((1, MLA_KV_RANK), lambda i: (0, 0))],
        out_specs=[pl.BlockSpec((tm, MLA_Q_RANK), lambda i: (i, 0)),
                   pl.BlockSpec((tm, MLA_KV_RANK), lambda i: (i, 0))],
        out_shape=[_sds((t, MLA_Q_RANK), BF16), _sds((t, MLA_KV_RANK), BF16)],
        compiler_params=_cparams(("parallel",)),
    )(proj2, qa, kva)


def _mla_mid_bwd(proj2, qa, kva, dcq, dckv, dkr):
    t = proj2.shape[0]
    tm = _row_tile(t)

    def body(p_ref, qa_ref, kva_ref, dcq_ref, dckv_ref, dkr_ref, dp_ref, dqa_ref, dkva_ref):
        @pl.when(pl.program_id(0) == 0)
        def _():
            dqa_ref[...] = jnp.zeros_like(dqa_ref)
            dkva_ref[...] = jnp.zeros_like(dkva_ref)

        dxq, dgq = _rms_bwd_rows(dcq_ref[...], p_ref[:, :MLA_Q_RANK], qa_ref[...], MLA_Q_RANK)
        dxk, dgk = _rms_bwd_rows(dckv_ref[...], p_ref[:, MLA_Q_RANK:MLA_Q_RANK + MLA_KV_RANK], kva_ref[...],
                                 MLA_KV_RANK)
        dp_ref[:, :MLA_Q_RANK] = dxq.astype(BF16)
        dp_ref[:, MLA_Q_RANK:MLA_Q_RANK + MLA_KV_RANK] = dxk.astype(BF16)
        dp_ref[:, MLA_Q_RANK + MLA_KV_RANK:] = dkr_ref[...].astype(BF16)
        dqa_ref[...] += jnp.sum(dgq, axis=0, keepdims=True)
        dkva_ref[...] += jnp.sum(dgk, axis=0, keepdims=True)

    return pl.pallas_call(
        body, name="mla_mid_bwd", grid=(t // tm,),
        in_specs=[pl.BlockSpec((tm, MLA_IN_PAD), lambda i: (i, 0)),
                  pl.BlockSpec((1, MLA_Q_RANK), lambda i: (0, 0)),
                  pl.BlockSpec((1, MLA_KV_RANK), lambda i: (0, 0)),
                  pl.BlockSpec((tm, MLA_Q_RANK), lambda i: (i, 0)),
                  pl.BlockSpec((tm, MLA_KV_RANK), lambda i: (i, 0)),
                  pl.BlockSpec((tm, 128), lambda i: (i, 0))],
        out_specs=[pl.BlockSpec((tm, MLA_IN_PAD), lambda i: (i, 0)),
                   pl.BlockSpec((1, MLA_Q_RANK), lambda i: (0, 0)),
                   pl.BlockSpec((1, MLA_KV_RANK), lambda i: (0, 0))],
        out_shape=[_sds((t, MLA_IN_PAD), BF16), _sds((1, MLA_Q_RANK), F32), _sds((1, MLA_KV_RANK), F32)],
        compiler_params=_cparams(("arbitrary",)),
    )(proj2, qa, kva, dcq, dckv, dkr)


def _mla_prep_specs(t, tm):
    head = lambda w: pl.BlockSpec((None, tm, w), lambda i, h: (h, i, 0))
    return dict(
        head256=head(MLA_HD_PAD), head128=head(MLA_VD),
        cols256=pl.BlockSpec((tm, MLA_HD_PAD), lambda i, h: (i, h)),
        cq=pl.BlockSpec((tm, MLA_Q_RANK), lambda i, h: (i, 0)),
        ckv=pl.BlockSpec((tm, MLA_KV_RANK), lambda i, h: (i, 0)),
        wuq=pl.BlockSpec((None, MLA_Q_RANK, MLA_HD_PAD), lambda i, h: (h, 0, 0)),
        wukv=pl.BlockSpec((None, MLA_KV_RANK, MLA_HD_PAD), lambda i, h: (h, 0, 0)),
        kr=pl.BlockSpec((tm, 128), lambda i, h: (i, (MLA_Q_RANK + MLA_KV_RANK) // 128)),
        gain=pl.BlockSpec((1, MLA_HD_PAD), lambda i, h: (0, 0)),
        tab=pl.BlockSpec((tm, 128), lambda i, h: (i, 0)),
    )


def _mla_prep(cq, ckv, wuq, wukv, proj2, gq, gk, tabs):
    t = cq.shape[0]
    tm = _row_tile(t)
    sp = _mla_prep_specs(t, tm)

    def body(cq_ref, ckv_ref, wuq_ref, wukv_ref, kr_ref, gq_ref, gk_ref, c_ref, s1_ref, s2_ref,
             qh_ref, kh_ref, vh_ref):
        c, s1, s2 = c_ref[...], s1_ref[...], s2_ref[...]

        def norm_rope(xv, gain):
            r = lax.rsqrt(jnp.sum(xv * xv, axis=-1, keepdims=True) / MLA_QKD + EPS)
            y = xv * r * gain
            return jnp.concatenate([y[:, :MLA_NOPE], _rope_tile(y[:, MLA_NOPE:], c, s1, s2)], axis=-1)

        kvv = _dot(ckv_ref[...], wukv_ref[...], 1, 0)
        qh_ref[...] = norm_rope(_dot(cq_ref[...], wuq_ref[...], 1, 0), gq_ref[...]).astype(BF16)
        kf = jnp.concatenate([kvv[:, :MLA_NOPE], kr_ref[...]], axis=-1)
        kh_ref[...] = norm_rope(kf, gk_ref[...]).astype(BF16)
        vh_ref[...] = jnp.concatenate([kvv[:, MLA_NOPE:], jnp.ones((tm, MLA_VD), F32)], axis=-1).astype(BF16)

    return pl.pallas_call(
        body, name="mla_prep", grid=(t // tm, MLA_HEADS),
        in_specs=[sp['cq'], sp['ckv'], sp['wuq'], sp['wukv'], sp['kr'], sp['gain'], sp['gain'],
                  sp['tab'], sp['tab'], sp['tab']],
        out_specs=[sp['head256'], sp['head256'], sp['head256']],
        out_shape=[_sds((MLA_HEADS, t, MLA_HD_PAD), BF16), _sds((MLA_HEADS, t, MLA_HD_PAD), BF16),
                   _sds((MLA_HEADS, t, 2 * MLA_VD), BF16)],
        compiler_params=_cparams(("parallel", "arbitrary")),
    )(cq, ckv, wuq, wukv, proj2, gq, gk, *tabs)


def _mla_prep_bwd(cq, ckv, wuq, wukv, proj2, gq, gk, tabs, dqt, dkh, dvh):
    t = cq.shape[0]
    tm = _row_tile(t)
    ab = dqt.shape[-1]
    sp = _mla_prep_specs(t, tm)

    def body(cq_ref, ckv_ref, wuq_ref, wukv_ref, kr_ref, gq_ref, gk_ref, c_ref, s1_ref, s2_ref,
             dqt_ref, dkh_ref, dvh_ref, dq_ref, dkv_ref, dkr_ref, dgq_ref, dgk_ref):
        dqh = jnp.concatenate([dqt_ref[b].T for b in range(tm // ab)], axis=0)
        i, h = pl.program_id(0), pl.program_id(1)

        @pl.when((i == 0) & (h == 0))
        def _():
            dgq_ref[...] = jnp.zeros_like(dgq_ref)
            dgk_ref[...] = jnp.zeros_like(dgk_ref)

        @pl.when(h == 0)
        def _():
            dkr_ref[...] = jnp.zeros_like(dkr_ref)

        c, s1, s2 = c_ref[...], s1_ref[...], s2_ref[...]

        def back(xv, gain, dout):
            dy = jnp.concatenate([dout[:, :MLA_NOPE], _rope_tile(dout[:, MLA_NOPE:], c, -s1, -s2)], axis=-1)
            return _rms_bwd_rows(dy, xv, gain, MLA_QKD)

        kvv = _dot(ckv_ref[...], wukv_ref[...], 1, 0)
        dxq, dgq = back(_dot(cq_ref[...], wuq_ref[...], 1, 0), gq_ref[...], dqh)
        kf = jnp.concatenate([kvv[:, :MLA_NOPE], kr_ref[...]], axis=-1)
        dxk, dgk = back(kf, gk_ref[...], dkh_ref[...])
        dq_ref[...] = dxq.astype(BF16)
        dkv_ref[...] = jnp.concatenate([dxk[:, :MLA_NOPE], dvh_ref[...]], axis=-1).astype(BF16)
        dkr_ref[...] += dxk[:, MLA_NOPE:]
        dgq_ref[...] += jnp.sum(dgq, axis=0, keepdims=True)
        dgk_ref[...] += jnp.sum(dgk, axis=0, keepdims=True)

    return pl.pallas_call(
        body, name="mla_prep_bwd", grid=(t // tm, MLA_HEADS),
        in_specs=[sp['cq'], sp['ckv'], sp['wuq'], sp['wukv'], sp['kr'], sp['gain'], sp['gain'],
                  sp['tab'], sp['tab'], sp['tab'],
                  pl.BlockSpec((None, tm // ab, MLA_HD_PAD, ab), lambda i, h: (h, i, 0, 0)),
                  sp['head256'], sp['head128']],
        out_specs=[sp['cols256'], sp['cols256'], sp['tab'], sp['gain'], sp['gain']],
        out_shape=[_sds((t, MLA_HEADS * MLA_HD_PAD), BF16), _sds((t, MLA_HEADS * MLA_HD_PAD), BF16),
                   _sds((t, 128), F32), _sds((1, MLA_HD_PAD), F32), _sds((1, MLA_HD_PAD), F32)],
        compiler_params=_cparams(("arbitrary", "arbitrary")),
    )(cq, ckv, wuq, wukv, proj2, gq, gk, *tabs, dqt, dkh, dvh)


def _chunk_visible(rows, cols, row_off, col_off):
    rq = lax.shift_right_logical(lax.broadcasted_iota(jnp.int32, (rows, cols), 0) + row_off, 6)
    ck = lax.shift_right_logical(lax.broadcasted_iota(jnp.int32, (rows, cols), 1) + col_off, 6)
    return ck <= rq


def _rows_to_lanes(col):
    return col.T[:8, :]


def _attn_fwd(qh, kh, vh):
    t = qh.shape[1]
    ab = min(ATT_BLOCK, t)
    tq = min(ATT_QROWS, t)
    r = tq // ab
    hg = ATT_HEADS

    def body(q_ref, k_ref, v_ref, o_ref, lse_ref, acc_ref):
        n_un = pl.program_id(1) * r
        acc_ref[...] = jnp.zeros_like(acc_ref)

        def step(b, ms, diag):
            rows = pl.ds(pl.multiple_of(b * ab, ab), ab)
            out = []
            for hh in range(hg):
                m = ms[hh]
                s = _dot(q_ref[hh], k_ref[hh, rows, :], 1, 1)
                if diag is not None:
                    s = jnp.where(_chunk_visible(tq, ab, 0, diag * ab), s, -1e30)
                m_new = jnp.maximum(m, jnp.max(s, axis=-1, keepdims=True))
                p = jnp.exp2((s - m_new) * ATT_EXP2).astype(BF16)
                acc_ref[hh] = jnp.exp2((m - m_new) * ATT_EXP2) * acc_ref[hh] + _dot(p, v_ref[hh, rows, :], 1, 0)
                out.append(m_new)
            return tuple(out)

        ms = tuple(jnp.full((tq, 1), -1e30, F32) for _ in range(hg))
        ms = lax.fori_loop(0, n_un, lambda b, st: step(b, st, None), ms)
        for d in range(r):
            ms = step(n_un + d, ms, d)
        for hh in range(hg):
            l = acc_ref[hh, :, MLA_VD:]
            o_ref[:, hh * MLA_VD:(hh + 1) * MLA_VD] = acc_ref[hh, :, :MLA_VD] / l
            lse_t = _rows_to_lanes(ms[hh] * ATT_EXP2 + jnp.log(l) * LOG2E)
            for d in range(r):
                lse_ref[hh, d] = lse_t[:, d * ab:(d + 1) * ab]

    return pl.pallas_call(
        body, name="mla_attn", grid=(MLA_HEADS // hg, t // tq),
        in_specs=[pl.BlockSpec((hg, tq, MLA_HD_PAD), lambda g, i: (g, i, 0)),
                  pl.BlockSpec((hg, t, MLA_HD_PAD), lambda g, i: (g, 0, 0)),
                  pl.BlockSpec((hg, t, 2 * MLA_VD), lambda g, i: (g, 0, 0))],
        out_specs=[pl.BlockSpec((tq, hg * MLA_VD), lambda g, i: (i, g)),
                   pl.BlockSpec((hg, r, 8, ab), lambda g, i: (g, i, 0, 0))],
        out_shape=[_sds((t, MLA_HEADS * MLA_VD), F32), _sds((MLA_HEADS, t // ab, 8, ab), F32)],
        scratch_shapes=[pltpu.VMEM((hg, tq, 2 * MLA_VD), F32)],
        compiler_params=_cparams(("parallel", "arbitrary")),
    )(qh, kh, vh)


def _attn_delta(do, o, ab):
    t = do.shape[0]
    tm = _row_tile(t)

    def body(do_ref, o_ref, d_ref):
        d = jnp.sum(do_ref[...] * o_ref[...], axis=-1, keepdims=True)
        d_t = _rows_to_lanes(jnp.broadcast_to(d, (tm, 128)))
        for b in range(tm // ab):
            d_ref[b] = d_t[:, b * ab:(b + 1) * ab]

    col = pl.BlockSpec((tm, MLA_VD), lambda i, h: (i, h))
    return pl.pallas_call(
        body, name="mla_delta", grid=(t // tm, MLA_HEADS), in_specs=[col, col],
        out_specs=pl.BlockSpec((None, tm // ab, 8, ab), lambda i, h: (h, i, 0, 0)),
        out_shape=_sds((MLA_HEADS, t // ab, 8, ab), F32),
        compiler_params=_cparams(("parallel", "parallel")),
    )(do, o)


def _attn_bwd(qh, kh, vh, dob, lse_t, dl_t):
    t = qh.shape[1]
    ab = min(ATT_BLOCK, t)
    kb = min(ATT_KROWS, t)
    r = kb // ab
    nq = t // ab
    hg = ATT_HEADS

    def body(q_ref, k_ref, v_ref, do_ref, lse_ref, dl_ref, dqt_ref, dk_ref, dv_ref):
        j = pl.program_id(1)

        @pl.when(j == 0)
        def _():
            dqt_ref[...] = jnp.zeros_like(dqt_ref)

        ks = [k_ref[hh] for hh in range(hg)]
        vs = [v_ref[hh, :, :MLA_VD] for hh in range(hg)]
        kts = [k.T for k in ks]

        dk_ref[...] = jnp.zeros_like(dk_ref)
        dv_ref[...] = jnp.zeros_like(dv_ref)

        def step(b, carry, diag):
            rows = pl.ds(pl.multiple_of(b * ab, ab), ab)
            hi = kb if diag is None else (diag + 1) * ab
            for hh in range(hg):
                q = q_ref[hh, rows, :]
                do = do_ref[rows, hh * MLA_VD:(hh + 1) * MLA_VD]
                s_t = _dot(ks[hh][:hi], q, 1, 1)
                if diag is not None:
                    key_chunk = lax.shift_right_logical(lax.broadcasted_iota(jnp.int32, (hi, ab), 0), 6)
                    query_chunk = lax.shift_right_logical(
                        lax.broadcasted_iota(jnp.int32, (hi, ab), 1) + diag * ab, 6)
                    s_t = jnp.where(key_chunk <= query_chunk, s_t, -1e30)
                p_t = jnp.exp2(s_t * ATT_EXP2 - lse_ref[hh, b][0:1, :])
                dp_t = _dot(vs[hh][:hi], do, 1, 1)
                ds_t = (p_t * (dp_t - dl_ref[hh, b][0:1, :]) * ATT_SCALE).astype(BF16)
                dqt_ref[hh, b] += _dot(kts[hh][:, :hi], ds_t, 1, 0)
                dk_ref[hh, :hi] += _dot(ds_t, q, 1, 0)
                dv_ref[hh, :hi] += _dot(p_t.astype(BF16), do, 1, 0)
            return carry

        for d in range(r):
            step(j * r + d, 0, d)
        lax.fori_loop((j + 1) * r, nq, lambda b, c: step(b, c, None), 0)

    whole = lambda w: pl.BlockSpec((hg, t, w), lambda g, j: (g, 0, 0))
    blk = lambda w: pl.BlockSpec((hg, kb, w), lambda g, j: (g, j, 0))
    stat = pl.BlockSpec((hg, nq, 8, ab), lambda g, j: (g, 0, 0, 0))
    return pl.pallas_call(
        body, name="mla_attn_bwd", grid=(MLA_HEADS // hg, t // kb),
        in_specs=[whole(MLA_HD_PAD), blk(MLA_HD_PAD), blk(2 * MLA_VD),
                  pl.BlockSpec((t, hg * MLA_VD), lambda g, j: (0, g)), stat, stat],
        out_specs=[pl.BlockSpec((hg, nq, MLA_HD_PAD, ab), lambda g, j: (g, 0, 0, 0)), blk(MLA_HD_PAD), blk(MLA_VD)],
        out_shape=[_sds((MLA_HEADS, nq, MLA_HD_PAD, ab), F32), _sds((MLA_HEADS, t, MLA_HD_PAD), F32),
                   _sds((MLA_HEADS, t, MLA_VD), F32)],
        compiler_params=_cparams(("parallel", "arbitrary")),
    )(qh, kh, vh, dob, lse_t, dl_t)


VEC = pl.BlockSpec((1, D_MODEL), lambda i, j, k: (0, 0))


def _rows(tm, width):
    return pl.BlockSpec((tm, width), lambda i, j, k: (i, 0))


def _residual_epi(next_gain):
    if next_gain is None:
        return [], lambda acc, hv: (acc + hv,)

    def epi(acc, hv, g):
        h_new = acc + hv
        r = lax.rsqrt(jnp.mean(h_new * h_new, axis=-1, keepdims=True) + EPS)
        return h_new, h_new * r * g

    return [(next_gain, VEC)], epi


def _residual_outs(t, row, next_gain):
    outs = [(_sds((t, D_MODEL), F32), row)]
    return outs + ([(_sds((t, D_MODEL), BF16), row)] if next_gain is not None else [])


def _mlp_fwd(l, h, hn, w1g, fetch_w2, next_gain):
    t = h.shape[0]
    tm = _row_tile(t, 512)

    def relu2(acc):
        r = jnp.maximum(acc, 0.0)
        return (r * r,)

    (u,) = _mm_rows(f"mlp_up{l}", tm, hn, w1g, 'nn_cols', [(_sds((t, D_FF), BF16), _rows(tm, D_FF))], epi=relu2)
    w2g = fetch_w2((u,))
    row = _rows(tm, D_MODEL)
    more, epi = _residual_epi(next_gain)
    h2, hn_next = _mm_rows(f"mlp_down{l}", tm, u, w2g, 'nn_rows', _residual_outs(t, row, next_gain),
                           extras=[(h, row)] + more, epi=epi)
    return h2, hn_next, (h, hn, u, w1g, w2g)


def _norm_bwd_outs(t, tm):
    return [(_sds((t, D_MODEL), F32), pl.BlockSpec((tm, D_MODEL), lambda i, j, k: (i, 0))),
            (_sds((t // tm, 1, D_MODEL), F32), pl.BlockSpec((None, 1, D_MODEL), lambda i, j, k: (i, 0, 0)))]


def _norm_bwd_epi(acc, xv, res, g):
    dx, dgr = _rms_bwd_rows(acc, xv, g, D_MODEL)
    return res + dx, jnp.sum(dgr, axis=0, keepdims=True)


def _mlp_bwd(l, dh, saved, norm_g):
    h, hn, u, w1g, w2g = saved
    t = h.shape[0]
    tm = _row_tile(t, 512)
    nsh, _, wsh = w1g.shape
    wide = _rows(tm, D_FF)
    (da,) = _mm_rows(f"mlp_du{l}", tm, dh, w2g, 'nt_rows', [(_sds((t, D_FF), BF16), wide)], extras=[(u, wide)],
                     epi=lambda acc, uv: (2.0 * jnp.sqrt(uv.astype(F32)) * acc,))
    tw = _row_tile(t, 512)
    (dw2,) = _mm(f"mlp_dw2{l}", (1, 1, t // tw),
                 u, pl.BlockSpec((tw, D_FF), lambda i, j, k: (k, 0)),
                 dh, pl.BlockSpec((tw, D_MODEL), lambda i, j, k: (k, 0)), (0, 0),
                 [(_sds((D_FF, D_MODEL), BF16), pl.BlockSpec((D_FF, D_MODEL), lambda i, j, k: (0, 0)))])
    dw2 = dw2.reshape(nsh, wsh, D_MODEL)
    (dw1,) = _mm(f"mlp_dw1{l}", (1, 1, t // tw),
                 hn, pl.BlockSpec((tw, D_MODEL), lambda i, j, k: (k, 0)),
                 da, pl.BlockSpec((tw, D_FF), lambda i, j, k: (k, 0)), (0, 0),
                 [(_sds((nsh, D_MODEL, wsh), BF16), pl.BlockSpec((nsh, D_MODEL, wsh), lambda i, j, k: (0, 0, 0)))],
                 split=wsh)
    row = _rows(tm, D_MODEL)
    dh_in, dg = _mm_rows(f"mlp_dhn{l}", tm, da, w1g, 'nt_cols', _norm_bwd_outs(t, tm),
                         extras=[(h, row), (dh, row), (norm_g, VEC)], epi=_norm_bwd_epi)
    return dh_in, jnp.sum(dg, axis=0), dw1, dw2


def _ple_fwd(l, h, hn, p, wg, wp, next_gain, target=None):
    t = h.shape[0]
    tm = _row_tile(t, 512)
    row = pl.BlockSpec((tm, D_MODEL), lambda i, j, k: (i, 0))
    full = lambda r: pl.BlockSpec((r, D_MODEL), lambda i, j, k: (0, 0))
    f32_row, bf_row = (_sds((t, D_MODEL), F32), row), (_sds((t, D_MODEL), BF16), row)
    common = [(h, row), (p, pl.BlockSpec((None, None, tm, PLE_DIM), lambda i, j, k: (l, 0, i, 0))),
              (wp, full(PLE_DIM))]
    if target is not None:
        def loss_epi(acc, hv, pv, wpv, tv):
            gt = _sigmoid(acc)
            ev = _dot(_bf(pv), wpv, 1, 0)
            err = hv + gt * ev - tv
            sq = jnp.sum(jnp.sum(err * err, axis=-1, keepdims=True), axis=0, keepdims=True)
            return err / D_MODEL, gt, ev, jnp.broadcast_to(sq, (8, 128))

        dy, gate, e, sq = _mm(f"ple_gate{l}", (t // tm, 1, 1), hn, row, wg, full(D_MODEL), (1, 0),
                              [f32_row, bf_row, bf_row, (_sds((t // tm, 8, 128), F32),
                                                         pl.BlockSpec((None, 8, 128), lambda i, j, k: (i, 0, 0)))],
                              extras=common + [(target, row)], epi=loss_epi)
        return dy, jnp.sum(sq, axis=0), (h, hn, gate, e)

    def gate_epi(acc, hv, pv, wpv, *gain):
        gt = _sigmoid(acc)
        ev = _dot(_bf(pv), wpv, 1, 0)
        h_new = hv + gt * ev
        if not gain:
            return h_new, gt, ev
        r = lax.rsqrt(jnp.mean(h_new * h_new, axis=-1, keepdims=True) + EPS)
        return h_new, gt, ev, h_new * r * gain[0]

    res = _mm(f"ple_gate{l}", (t // tm, 1, 1), hn, row, wg, full(D_MODEL), (1, 0),
              [f32_row, bf_row, bf_row] + ([bf_row] if next_gain is not None else []),
              extras=common + ([(next_gain, VEC)] if next_gain is not None else []), epi=gate_epi)
    h_out, gate, e = res[0], res[1], res[2]
    return h_out, (res[3] if next_gain is not None else None), (h, hn, gate, e)


def _ple_bwd(l, dh, saved, p, norm_g, wg, deps=()):
    h, hn, gate, e = saved
    t = h.shape[0]
    tm = _row_tile(t)
    tk = _row_tile(t, 512)
    de, dz = _ple_gate_bwd(f"ple_gate_bwd{l}", dh, gate, e)
    full = lambda r: pl.BlockSpec((r, D_MODEL), lambda i, j, k: (0, 0))
    rowk = pl.BlockSpec((tk, D_MODEL), lambda i, j, k: (k, 0))
    (dwp,) = _mm(f"ple_dwp{l}", (1, 1, t // tk),
                 p, pl.BlockSpec((None, None, tk, PLE_DIM), lambda i, j, k: (l, 0, k, 0)),
                 de, rowk, (0, 0), [(_sds((PLE_DIM, D_MODEL), BF16), full(PLE_DIM))], deps=deps)
    (dwg,) = _mm(f"ple_dwg{l}", (1, 1, t // tk), hn, rowk, dz, rowk, (0, 0),
                 [(_sds((D_MODEL, D_MODEL), BF16), full(D_MODEL))])
    row = pl.BlockSpec((tm, D_MODEL), lambda i, j, k: (i, 0))
    dh_in, dg = _mm(f"ple_dhn{l}", (t // tm, 1, 1), dz, row, wg, full(D_MODEL), (1, 1),
                    _norm_bwd_outs(t, tm), extras=[(h, row), (dh, row), (norm_g, VEC)], epi=_norm_bwd_epi)
    return dh_in, jnp.sum(dg, axis=0), dwg, dwp


def _ret_layer_fwd(x, norm_g, wri, fetch_wro, gn, cos, sin, next_gain, hn=None, deps=()):
    t = x.shape[0]
    tm = _row_tile(t)
    nsh, _, wsh = wri.shape
    if hn is None:
        hn = _rms_fwd("mix_norm0", x, norm_g)
    tp = _row_tile(t, 512)
    (proj,) = _mm_rows("ret_in", tp, hn, wri, 'nn_cols', [(_sds((t, RET_IN), BF16), _rows(tp, RET_IN))], deps=deps)
    gated, outp, states = _ret_fwd(proj, cos, sin, gn)
    wro = fetch_wro((gated,))
    row = _rows(tp, D_MODEL)
    more, epi = _residual_epi(next_gain)
    h1, hn_next = _mm_rows("ret_out", tp, gated, wro.reshape(RET_HEADS, RET_DV, D_MODEL), 'nn_rows',
                           _residual_outs(t, row, next_gain), extras=[(x, row)] + more, epi=epi)
    return h1, hn_next, (x, hn, proj, gated, outp, states, wro)


def _ret_layer_bwd(dh, saved, norm_g, wri, gn, cos, sin, emit_out, emit_in, deps=()):
    x, hn, proj, gated, outp, states, wro = saved
    t = x.shape[0]
    tm = _row_tile(t)
    tk = _row_tile(t, 512)
    nsh, _, wsh = wri.shape
    tg = _row_tile(t, 512)
    vw = _rows(tg, RET_V_W)
    dout, dgate, dgn = _mm_rows(
        "ret_dgate", tg, dh, wro.reshape(RET_HEADS, RET_DV, D_MODEL), 'nt_rows',
        [(_sds((t, RET_V_W), BF16), vw), (_sds((t, RET_V_W), BF16), vw),
         (_sds((t // tg, 1, RET_V_W), F32), pl.BlockSpec((None, 1, RET_V_W), lambda i, j, k: (i, 0, 0)))],
        extras=[(outp, vw), (proj, pl.BlockSpec((tg, RET_V_W), lambda i, j, k: (i, (RET_IN - RET_V_W) // RET_V_W))),
                (gn.reshape(1, RET_V_W), pl.BlockSpec((1, RET_V_W), lambda i, j, k: (0, 0)))],
        epi=_ret_gate_bwd_epi, deps=deps)
    dgn = jnp.sum(dgn, axis=0)
    (dwro,) = _mm("ret_dwro", (1, 1, t // tk),
                  gated, pl.BlockSpec((tk, RET_V_W), lambda i, j, k: (k, 0)),
                  dh, pl.BlockSpec((tk, D_MODEL), lambda i, j, k: (k, 0)), (0, 0),
                  [(_sds((RET_V_W, D_MODEL), BF16), pl.BlockSpec((RET_V_W, D_MODEL), lambda i, j, k: (0, 0)))])
    dproj = _ret_bwd(proj, cos, sin, states, dout, dgate, deps=emit_out(dwro))
    half = nsh // 2
    (dwri,) = _mm("ret_dwri", (2, 1, t // tk),
                  hn, pl.BlockSpec((tk, D_MODEL), lambda i, j, k: (k, 0)),
                  dproj, pl.BlockSpec((tk, half * wsh), lambda i, j, k: (k, i)), (0, 0),
                  [(_sds((nsh, D_MODEL, wsh), BF16), pl.BlockSpec((half, D_MODEL, wsh), lambda i, j, k: (i, 0, 0)))],
                  split=wsh)
    deps = emit_in(dwri)
    td = _row_tile(t, 256)
    row = _rows(td, D_MODEL)
    dx, dg = _mm_rows("ret_dhn", td, dproj, wri, 'nt_cols', _norm_bwd_outs(t, td),
                      extras=[(x, row), (dh, row), (norm_g, VEC)], epi=_norm_bwd_epi, deps=deps)
    return dx, jnp.sum(dg, axis=0), dgn.reshape(RET_HEADS, RET_DV)


def _mla_layer_fwd(h, hn, wmi, qa, kva, wuq, wukv, gq, gk, wmo, tabs, next_gain):
    t = h.shape[0]
    tm = _row_tile(t)
    row = pl.BlockSpec((tm, D_MODEL), lambda i, j, k: (i, 0))
    (proj2,) = _mm("mla_in", (t // tm, 1, 1), hn, row,
                   wmi, pl.BlockSpec((D_MODEL, MLA_IN_PAD), lambda i, j, k: (0, 0)), (1, 0),
                   [(_sds((t, MLA_IN_PAD), F32), pl.BlockSpec((tm, MLA_IN_PAD), lambda i, j, k: (i, 0)))])
    cq, ckv = _mla_mid(proj2, qa, kva)
    qh, kh, vh = _mla_prep(cq, ckv, wuq, wukv, proj2, gq, gk, tabs)
    o, lse = _attn_fwd(qh, kh, vh)
    more, epi = _residual_epi(next_gain)
    h_out, hn_next = _mm("mla_out", (t // tm, 1, 1), o, row,
                         wmo, pl.BlockSpec((D_MODEL, D_MODEL), lambda i, j, k: (0, 0)), (1, 0),
                         _residual_outs(t, row, next_gain), extras=[(h, row)] + more, epi=epi)
    return h_out, hn_next, (h, hn, proj2, cq, ckv, qh, kh, vh, o, lse)


def _mla_layer_bwd(dh, saved, norm_g, wmi, qa, kva, wuq, wukv, gq, gk, wmo, tabs, deps=()):
    h, hn, proj2, cq, ckv, qh, kh, vh, o, lse = saved
    t = h.shape[0]
    tm = _row_tile(t)
    tk = _row_tile(t, 512)
    row = pl.BlockSpec((tm, D_MODEL), lambda i, j, k: (i, 0))
    rowk = pl.BlockSpec((tk, D_MODEL), lambda i, j, k: (k, 0))
    sq = pl.BlockSpec((D_MODEL, D_MODEL), lambda i, j, k: (0, 0))
    do, dob = _mm("mla_do", (t // tm, 1, 1), dh, row, wmo, sq, (1, 1),
                  [(_sds((t, D_MODEL), F32), row), (_sds((t, D_MODEL), BF16), row)], epi=lambda acc: (acc, acc),
                  deps=deps)
    (dwmo,) = _mm("mla_dwo", (1, 1, t // tk), o, rowk, dh, rowk, (0, 0), [(_sds((D_MODEL, D_MODEL), BF16), sq)])
    delta = _attn_delta(do, o, lse.shape[-1])
    dqt, dkh, dvh = _attn_bwd(qh, kh, vh, dob, lse, delta)
    dq, dkv, dkr, dgq, dgk = _mla_prep_bwd(cq, ckv, wuq, wukv, proj2, gq, gk, tabs, dqt, dkh, dvh)

    wide = MLA_HEADS * MLA_HD_PAD
    widek = pl.BlockSpec((tk, wide), lambda i, j, k: (k, 0))
    (dwuq,) = _mm("mla_dwuq", (1, 1, t // tk),
                  cq, pl.BlockSpec((tk, MLA_Q_RANK), lambda i, j, k: (k, 0)), dq, widek, (0, 0),
                  [(_sds((MLA_HEADS, MLA_Q_RANK, MLA_HD_PAD), BF16),
                    pl.BlockSpec((MLA_HEADS, MLA_Q_RANK, MLA_HD_PAD), lambda i, j, k: (0, 0, 0)))], split=MLA_HD_PAD)
    (dwukv,) = _mm("mla_dwukv", (1, 1, t // tk),
                   ckv, pl.BlockSpec((tk, MLA_KV_RANK), lambda i, j, k: (k, 0)), dkv, widek, (0, 0),
                   [(_sds((MLA_HEADS, MLA_KV_RANK, MLA_HD_PAD), BF16),
                     pl.BlockSpec((MLA_HEADS, MLA_KV_RANK, MLA_HD_PAD), lambda i, j, k: (0, 0, 0)))],
                   split=MLA_HD_PAD)
    side_by_side = lambda wg: wg.transpose(1, 0, 2).reshape(wg.shape[1], wide)
    widei = pl.BlockSpec((tm, wide), lambda i, j, k: (i, 0))
    (dcq,) = _mm("mla_dcq", (t // tm, 1, 1), dq, widei,
                 side_by_side(wuq), pl.BlockSpec((MLA_Q_RANK, wide), lambda i, j, k: (0, 0)), (1, 1),
                 [(_sds((t, MLA_Q_RANK), F32), pl.BlockSpec((tm, MLA_Q_RANK), lambda i, j, k: (i, 0)))])
    (dckv,) = _mm("mla_dckv", (t // tm, 1, 1), dkv, widei,
                  side_by_side(wukv), pl.BlockSpec((MLA_KV_RANK, wide), lambda i, j, k: (0, 0)), (1, 1),
                  [(_sds((t, MLA_KV_RANK), F32), pl.BlockSpec((tm, MLA_KV_RANK), lambda i, j, k: (i, 0)))])
    dproj2, dqa, dkva = _mla_mid_bwd(proj2, qa, kva, dcq, dckv, dkr)
    win = pl.BlockSpec((D_MODEL, MLA_IN_PAD), lambda i, j, k: (0, 0))
    (dwmi,) = _mm("mla_dwin", (1, 1, t // tk), hn, rowk,
                  dproj2, pl.BlockSpec((tk, MLA_IN_PAD), lambda i, j, k: (k, 0)), (0, 0),
                  [(_sds((D_MODEL, MLA_IN_PAD), BF16), win)])
    dh_in, dg = _mm("mla_dhn", (t // tm, 1, 1),
                    dproj2, pl.BlockSpec((tm, MLA_IN_PAD), lambda i, j, k: (i, 0)), wmi, win, (1, 1),
                    _norm_bwd_outs(t, tm), extras=[(h, row), (dh, row), (norm_g, VEC)], epi=_norm_bwd_epi)
    return dh_in, dict(mix=jnp.sum(dg, axis=0), wmi=dwmi, qa=dqa, kva=dkva, wuq=dwuq, wukv=dwukv, gq=dgq, gk=dgk,
                       wmo=dwmo)


def _local_step(x, p, target, w, fetch, emit=lambda group: ()):
    t = x.shape[0]
    inv = 1.0 / (ROPE_THETA ** (jnp.arange(0, RET_DK, 2, dtype=F32) / RET_DK))
    ang = jnp.arange(t, dtype=F32)[:, None] * inv[None, :]
    cos_r, sin_r = jnp.cos(ang), jnp.sin(ang)
    tabs = _mla_tables(t, cos_r, sin_r)
    row = lambda a, i: a[i:i + 1]

    h1, hn1, s_ret = _ret_layer_fwd(x, row(w['mix_norm'], 0), w['ret_w_in'],
                                    lambda after: fetch('ret_out', after)['ret_w_out'], w['ret_gn'], cos_r, sin_r,
                                    row(w['mlp_norm'], 0), hn=w.get('hn0'), deps=w['deps'])
    h2, hn2, s_mlp0 = _mlp_fwd(0, h1, hn1, fetch('mlp_w1_0', (h1,))['mlp_w1'],
                               lambda after: fetch('mlp_w2_0', after)['mlp_w2'], row(w['ple_norm'], 0))
    w0 = fetch('ple_0', (h2,))
    h3, hn3, s_ple0 = _ple_fwd(0, h2, hn2, p, w0['ple_gate_w'], w0['ple_proj_w'], row(w['mix_norm'], 1))
    wm = fetch('mla', (h3,))
    mla_w = (wm['mla_w_in'], w['mla_q_a_norm'], w['mla_kv_a_norm'], wm['mla_w_uq'], wm['mla_w_ukv'],
             w['mla_q_norm'], w['mla_k_norm'], wm['mla_w_out'], tabs)
    h4, hn4, s_mla = _mla_layer_fwd(h3, hn3, *mla_w, row(w['mlp_norm'], 1))
    w1 = fetch('layer_1', (h4,))
    h5, hn5, s_mlp1 = _mlp_fwd(1, h4, hn4, w1['mlp_w1'], lambda after: w1['mlp_w2'], row(w['ple_norm'], 1))
    dy, sq_err, s_ple1 = _ple_fwd(1, h5, hn5, p, w1['ple_gate_w'], w1['ple_proj_w'], None, target)

    n = N_DEV
    colsh = lambda a: a.reshape(a.shape[0], n, a.shape[1] // n).transpose(1, 0, 2)
    rowsh = lambda a: a.reshape(n, a.shape[0] // n, a.shape[1])
    big = {}

    def emit_group(group):
        big.update(group)
        return emit(group)

    dh5, dg_ple1, dwg1, dwp1 = _ple_bwd(1, dy, s_ple1, p, row(w['ple_norm'], 1), w1['ple_gate_w'])
    dh4, dg_mlp1, dw1_1, dw2_1 = _mlp_bwd(1, dh5, s_mlp1, row(w['mlp_norm'], 1))
    deps = emit_group({('ple_gate_w', 1): rowsh(dwg1), ('ple_proj_w', 1): colsh(dwp1),
                       ('mlp_w2', 1): dw2_1, ('mlp_w1', 1): dw1_1})
    dh3, gm = _mla_layer_bwd(dh4, s_mla, row(w['mix_norm'], 1), *mla_w, deps=deps)
    deps = emit_group({('mla_w_out', 0): rowsh(gm['wmo']), ('mla_w_uq', 0): _gather_rope(gm['wuq']),
                       ('mla_w_ukv', 0): gm['wukv'], ('mla_w_in', 0): rowsh(_gather_rope(gm['wmi']))})
    dh2, dg_ple0, dwg0, dwp0 = _ple_bwd(0, dh3, s_ple0, p, row(w['ple_norm'], 0), w0['ple_gate_w'], deps=deps)
    dh1, dg_mlp0, dw1_0, dw2_0 = _mlp_bwd(0, dh2, s_mlp0, row(w['mlp_norm'], 0))
    deps = emit_group({('ple_gate_w', 0): rowsh(dwg0), ('ple_proj_w', 0): colsh(dwp0),
                       ('mlp_w2', 0): dw2_0, ('mlp_w1', 0): dw1_0})
    dx, dg_mix0, dgn = _ret_layer_bwd(
        dh1, s_ret, row(w['mix_norm'], 0), w['ret_w_in'], w['ret_gn'], cos_r, sin_r,
        lambda dwro: emit_group({('ret_w_out', 0): rowsh(dwro)}),
        lambda dwri: emit_group({('ret_w_in', 0): dwri}), deps=deps)

    small = dict(
        mix_norm=[dg_mix0, gm['mix']], mlp_norm=[dg_mlp0, dg_mlp1], ple_norm=[dg_ple0, dg_ple1],
        ret_gn=dgn, mla_q_a_norm=gm['qa'], mla_kv_a_norm=gm['kva'], mla_q_norm=gm['gq'], mla_k_norm=gm['gk'],
    )
    return sq_err, dx, big, small


def _my_place():
    x, y, c = lax.axis_index("x"), lax.axis_index("y"), lax.axis_index("c")
    return x, y, c


def _flat(px, py, pc):
    return 4 * px + 2 * py + pc


def _peer(x, y, c, r):
    return (1 - x if r & 4 else x, 1 - y if r & 2 else y, 1 - c if r & 1 else c)


HBM = pl.BlockSpec(memory_space=pltpu.HBM)
SEMS = pl.BlockSpec(memory_space=pltpu.SEMAPHORE)
SIDE_EFFECT = pltpu.SideEffectType.DATAFLOW_SIDE_EFFECTING


def _rs_copies(x, y, c, srcs, lands, send_sems, recv_sems):
    copies = []
    for a in range(len(srcs)):
        for r in range(1, N_DEV):
            peer = _peer(x, y, c, r)
            k = a * (N_DEV - 1) + r - 1
            copies.append(pltpu.make_async_remote_copy(
                src_ref=srcs[a].at[_flat(*peer)], dst_ref=lands[a].at[r - 1],
                send_sem=send_sems.at[k], recv_sem=recv_sems.at[k], device_id=peer, device_id_type=MESH))
    return copies


def _rs_start(name, arrays):
    n = len(arrays)
    hbm = lambda a: pltpu.with_memory_space_constraint(a, pltpu.HBM)
    lands = [hbm(lax.empty((N_DEV - 1,) + a.shape[1:], a.dtype)) for a in arrays]

    def body(*refs):
        srcs, lnd = refs[:n], refs[n:2 * n]
        send_sems, recv_sems = refs[2 * n], refs[2 * n + 1]
        token = refs[-1]
        for cp in _rs_copies(*_my_place(), srcs, lnd, send_sems, recv_sems):
            cp.start()
        token[...] = jnp.zeros_like(token)

    outs = pl.pallas_call(
        body, name=name,
        in_specs=[HBM] * (2 * n),
        out_specs=[SEMS, SEMS] + [HBM] * (2 * n) + [pl.BlockSpec(memory_space=pltpu.VMEM)],
        out_shape=[pltpu.SemaphoreType.DMA((n * (N_DEV - 1),)), pltpu.SemaphoreType.DMA((n * (N_DEV - 1),))]
        + [pltpu.HBM(a.shape, a.dtype) for a in arrays] + [pltpu.HBM(l.shape, l.dtype) for l in lands]
        + [_sds((8, 128), F32)],
        input_output_aliases={i: 2 + i for i in range(2 * n)},
        compiler_params=pltpu.CompilerParams(has_side_effects=SIDE_EFFECT),
    )(*[hbm(a) for a in arrays], *lands)
    return outs[0], outs[1], outs[2:2 + n], outs[2 + n:2 + 2 * n], outs[-1]


def _rs_wait(name, send_sems, recv_sems, srcs, lands, after):
    n = len(srcs)

    def body(*refs):
        src_refs, lnd = refs[:n], refs[n:2 * n]
        send, recv = refs[2 * n], refs[2 * n + 1]
        for cp in _rs_copies(*_my_place(), src_refs, lnd, send, recv):
            cp.wait_send()
            cp.wait_recv()

    outs = pl.pallas_call(
        body, name=name,
        in_specs=[HBM] * (2 * n) + [SEMS, SEMS] + [ANY] * len(after),
        out_specs=[HBM] * (2 * n),
        out_shape=[pltpu.HBM(a.shape, a.dtype) for a in list(srcs) + list(lands)],
        input_output_aliases={i: i for i in range(2 * n)},
        compiler_params=pltpu.CompilerParams(has_side_effects=SIDE_EFFECT),
    )(*srcs, *lands, send_sems, recv_sems, *after)
    return outs[:n], outs[n:]


SMALL_PACK_ROWS = 16


def _all_reduce_small(rows, deps=()):
    n = len(rows)

    def body(*refs):
        ins = refs[:n]
        out_ref, mine, buf, send_sems, recv_sems = refs[n + len(deps):]
        x, y, c = _my_place()
        mine[...] = jnp.zeros_like(mine)
        for (r0, a), ref in zip(rows, ins):
            mine[r0:r0 + a.shape[0], 0:a.shape[1]] = ref[...]
        buf[_flat(x, y, c)] = mine[...]
        copies = []
        for r in range(1, N_DEV):
            peer = _peer(x, y, c, r)
            send = pltpu.make_async_remote_copy(
                src_ref=mine, dst_ref=buf.at[_flat(x, y, c)],
                send_sem=send_sems.at[r - 1], recv_sem=recv_sems.at[r - 1], device_id=peer, device_id_type=MESH)
            send.start()
            recv = pltpu.make_async_remote_copy(
                src_ref=mine, dst_ref=buf.at[_flat(*peer)],
                send_sem=send_sems.at[r - 1], recv_sem=recv_sems.at[r - 1], device_id=peer, device_id_type=MESH)
            copies.append((send, recv))
        for send, recv in copies:
            send.wait_send()
            recv.wait_recv()
        acc = buf[0]
        for s in range(1, N_DEV):
            acc = acc + buf[s]
        out_ref[...] = acc

    vm = pl.BlockSpec(memory_space=pltpu.VMEM)
    shape = (SMALL_PACK_ROWS, D_MODEL)
    return pl.pallas_call(
        body, name="all_reduce_small", in_specs=[vm] * n + [ANY] * len(deps), out_specs=vm,
        out_shape=_sds(shape, F32),
        scratch_shapes=[pltpu.VMEM(shape, F32), pltpu.VMEM((N_DEV,) + shape, F32),
                        pltpu.SemaphoreType.DMA((7,)), pltpu.SemaphoreType.DMA((7,))],
    )(*[a for _, a in rows], *deps)


def _adamw_math(w, g, m, v):
    m = ADAM_B1 * m + (1.0 - ADAM_B1) * g
    v = ADAM_B2 * v + (1.0 - ADAM_B2) * (g * g)
    m_hat = m / (1.0 - ADAM_B1 ** ADAM_STEP)
    v_hat = v / (1.0 - ADAM_B2 ** ADAM_STEP)
    delta = -ADAM_LR * (m_hat / (jnp.sqrt(v_hat) + ADAM_EPS) + ADAM_WD * w)
    return delta, m, v


def _adamw_big(name, w, m, v, srcs, lands, me):
    nl, rows, cols = w.shape
    tr = next(cand for cand in (256, 128, 64, 32, 16, 8) if rows % cand == 0)

    def body(me_ref, w_ref, m_ref, v_ref, *rest):
        src_refs, land_refs = rest[:nl], rest[nl:2 * nl]
        g_ref, d_ref, mo_ref, vo_ref = rest[2 * nl:]
        for layer in range(nl):
            @pl.when(pl.program_id(0) == layer)
            def _():
                g = src_refs[layer][...].astype(F32)
                for s in range(N_DEV - 1):
                    g = g + land_refs[layer][s].astype(F32)
                delta, mn, vn = _adamw_math(w_ref[...], g, m_ref[...], v_ref[...])
                g_ref[...] = g
                d_ref[...] = delta
                mo_ref[...] = mn
                vo_ref[...] = vn

    blk = pl.BlockSpec((None, tr, cols), lambda l, i, me_ref: (l, i, 0))
    at = lambda layer, l, i: jnp.where(l == layer, i, 0)
    own = [pl.BlockSpec((None, tr, cols), functools.partial(lambda layer, l, i, me_ref: (me_ref[0], at(layer, l, i), 0),
                                                            layer)) for layer in range(nl)]
    peers = [pl.BlockSpec((N_DEV - 1, tr, cols), functools.partial(lambda layer, l, i, me_ref: (0, at(layer, l, i), 0),
                                                                   layer)) for layer in range(nl)]
    return pl.pallas_call(
        body, name=name,
        grid_spec=pltpu.PrefetchScalarGridSpec(
            num_scalar_prefetch=1, grid=(nl, rows // tr),
            in_specs=[blk, blk, blk] + own + peers, out_specs=[blk] * 4),
        out_shape=[_sds((nl, rows, cols), F32)] * 4,
        compiler_params=_cparams(("arbitrary", "arbitrary")),
    )(me, w, m, v, *srcs, *lands)


def _adamw_small(ws, gs, ms, vs):
    n = len(ws)

    def body(*refs):
        w_refs, g_refs, m_refs, v_refs = (refs[i * n:(i + 1) * n] for i in range(4))
        d_out, m_out, v_out = (refs[(4 + i) * n:(5 + i) * n] for i in range(3))
        for i in range(n):
            delta, mn, vn = _adamw_math(w_refs[i][...], g_refs[i][...], m_refs[i][...], v_refs[i][...])
            d_out[i][...] = delta
            m_out[i][...] = mn
            v_out[i][...] = vn

    vm = pl.BlockSpec(memory_space=pltpu.VMEM)
    outs = pl.pallas_call(
        body, name="adamw_small", in_specs=[vm] * (4 * n), out_specs=[vm] * (3 * n),
        out_shape=[_sds(a.shape, F32) for a in ws] * 3,
    )(*ws, *gs, *ms, *vs)
    return outs[:n], outs[n:2 * n], outs[2 * n:]


def _pad_to(a, rows, cols):
    return jnp.pad(a, ((0, rows - a.shape[0]), (0, cols - a.shape[1])))


def _place_own(blocks):
    me = _flat(*_my_place())
    return [lax.dynamic_update_slice(lax.empty((N_DEV,) + b.shape, b.dtype), b[None], (me,) + (0,) * b.ndim)
            for b in blocks]


def _ag_copies(x, y, c, blocks, bufs, send_sems, recv_sems, arriving):
    copies = []
    for a in range(len(blocks)):
        for r in range(1, N_DEV):
            peer = _peer(x, y, c, r)
            k = a * (N_DEV - 1) + r - 1
            copies.append(pltpu.make_async_remote_copy(
                src_ref=blocks[a], dst_ref=bufs[a].at[_flat(*(peer if arriving else (x, y, c)))],
                send_sem=send_sems.at[k], recv_sem=recv_sems.at[k], device_id=peer, device_id_type=MESH))
    return copies


def _ag_start(groups, after):
    flat = [pair for g in groups for pair in g]
    n, ng = len(flat), len(groups)
    hbm = lambda a: pltpu.with_memory_space_constraint(a, pltpu.HBM)

    def body(*refs):
        blocks, bufs = refs[:n], refs[n:2 * n]
        sems = refs[2 * n + len(after):2 * n + len(after) + 2 * ng]
        x, y, c = _my_place()
        at = 0
        for gi, g in enumerate(groups):
            for cp in _ag_copies(x, y, c, blocks[at:at + len(g)], bufs[at:at + len(g)], sems[2 * gi],
                                 sems[2 * gi + 1], arriving=False):
                cp.start()
            at += len(g)
        refs[-1][...] = jnp.zeros_like(refs[-1])

    sem_shapes = [pltpu.SemaphoreType.DMA((len(g) * (N_DEV - 1),)) for g in groups for _ in range(2)]
    outs = pl.pallas_call(
        body, name="gather_start",
        in_specs=[HBM] * (2 * n) + [ANY] * len(after),
        out_specs=[SEMS] * (2 * ng) + [HBM] * (2 * n) + [pl.BlockSpec(memory_space=pltpu.VMEM)],
        out_shape=sem_shapes + [pltpu.HBM(b.shape, b.dtype) for b, _ in flat]
        + [pltpu.HBM(u.shape, u.dtype) for _, u in flat] + [_sds((8, 128), F32)],
        input_output_aliases={i: 2 * ng + i for i in range(2 * n)},
        compiler_params=pltpu.CompilerParams(has_side_effects=SIDE_EFFECT),
    )(*[hbm(b) for b, _ in flat], *[hbm(u) for _, u in flat], *after)
    blocks_thru, bufs_thru = outs[2 * ng:2 * ng + n], outs[2 * ng + n:2 * ng + 2 * n]
    started, at = [], 0
    for gi, g in enumerate(groups):
        started.append((outs[2 * gi], outs[2 * gi + 1], blocks_thru[at:at + len(g)], bufs_thru[at:at + len(g)]))
        at += len(g)
    return started, outs[-1]


def _ag_wait(name, send_sems, recv_sems, blocks, bufs, after):
    n = len(blocks)

    def body(*refs):
        for cp in _ag_copies(*_my_place(), refs[:n], refs[n:2 * n], refs[2 * n], refs[2 * n + 1], arriving=True):
            cp.wait_send()
            cp.wait_recv()

    outs = pl.pallas_call(
        body, name=name,
        in_specs=[HBM] * (2 * n) + [SEMS, SEMS] + [ANY] * len(after),
        out_specs=[HBM] * (2 * n),
        out_shape=[pltpu.HBM(a.shape, a.dtype) for a in list(blocks) + list(bufs)],
        input_output_aliases={i: i for i in range(2 * n)},
        compiler_params=pltpu.CompilerParams(has_side_effects=SIDE_EFFECT),
    )(*blocks, *bufs, send_sems, recv_sems, *after)
    return outs[n:]


def _split_call(name, body, thru, sems_in, new_sems, after):
    n, ns, nn = len(thru), len(sems_in), len(new_sems)
    hbm = lambda a: pltpu.with_memory_space_constraint(a, pltpu.HBM)

    def wrapped(*refs):
        body(refs[:n], refs[n:n + ns], refs[n + ns + len(after):n + ns + len(after) + nn])
        refs[-1][...] = jnp.zeros_like(refs[-1])

    outs = pl.pallas_call(
        wrapped, name=name,
        in_specs=[HBM] * n + [SEMS] * ns + [ANY] * len(after),
        out_specs=[SEMS] * nn + [HBM] * n + [pl.BlockSpec(memory_space=pltpu.VMEM)],
        out_shape=[pltpu.SemaphoreType.DMA((k,)) for k in new_sems] + [pltpu.HBM(a.shape, a.dtype) for a in thru]
        + [_sds((8, 128), F32)],
        input_output_aliases={i: nn + i for i in range(n)},
        compiler_params=pltpu.CompilerParams(has_side_effects=SIDE_EFFECT),
    )(*[hbm(a) for a in thru], *sems_in, *after)
    return list(outs[:nn]), list(outs[nn:nn + n]), outs[-1]


def _first_gather(blocks, bufs, overlap):
    n = len(blocks)

    def copies(refs, s1, r1, s2, r2):
        x, y, c = _my_place()
        me, sibling = (x, y, c), (x, y, 1 - c)
        chips = [(1 - x, y), (x, 1 - y), (1 - x, 1 - y)]
        blk, buf = refs[:n], refs[n:]
        out = dict(send1=[], recv1_sib=[], recv1_ici=[], send2=[], recv2=[])
        for a in range(n):
            place = lambda dev: buf[a].at[_flat(*dev)]
            for k, to in enumerate([sibling] + [(*chip, c) for chip in chips]):
                mk = lambda dst: pltpu.make_async_remote_copy(
                    src_ref=blk[a], dst_ref=dst, send_sem=s1.at[4 * a + k], recv_sem=r1.at[4 * a + k],
                    device_id=to, device_id_type=MESH)
                out['send1'].append(mk(place(me)))
                out['recv1_sib' if k == 0 else 'recv1_ici'].append(mk(place(to)))
            for j, chip in enumerate(chips):
                mk = lambda dev: pltpu.make_async_remote_copy(
                    src_ref=place(dev), dst_ref=place(dev), send_sem=s2.at[3 * a + j], recv_sem=r2.at[3 * a + j],
                    device_id=sibling, device_id_type=MESH)
                out['send2'].append(mk((*chip, c)))
                out['recv2'].append(mk((*chip, 1 - c)))
        return out

    def start(refs, sems_in, new):
        for cp in copies(refs, new[0], new[1], new[0], new[1])['send1']:
            cp.start()

    def forward(refs, sems_in, new):
        cps = copies(refs, sems_in[0], sems_in[1], new[0], new[1])
        for cp in cps['recv1_ici']:
            cp.wait_recv()
        for cp in cps['send2']:
            cp.start()

    def finish(refs, sems_in, new):
        cps = copies(refs, *sems_in)
        for cp in cps['recv1_sib'] + cps['recv2']:
            cp.wait_recv()
        for cp in cps['send1'] + cps['send2']:
            cp.wait_send()

    sems1, thru, token = _split_call("first_gather_start", start, list(blocks) + list(bufs), [], [4 * n, 4 * n], ())
    after = overlap(token)
    sems2, thru, token = _split_call("first_gather_forward", forward, thru, sems1, [3 * n, 3 * n], after)
    _, thru, _ = _split_call("first_gather_wait", finish, thru, sems1 + sems2, [], ())
    return thru[n:], token


def _prepare_weights(p, x):
    n = N_DEV
    bf = lambda a: a.astype(BF16)
    gn_pack = jnp.concatenate([
        _pad_to(p['ret_gn'][0], RET_HEADS, 128), _pad_to(p['mla_q_a_norm'], 1, 128),
        _pad_to(p['mla_kv_a_norm'], 1, 128), jnp.zeros((2, 128), F32)], axis=0)
    ple = lambda l: [bf(p['ple_gate_w'][l]), bf(p['ple_proj_w'][l])]
    names = ('ret_out', 'mlp_w1_0', 'mlp_w2_0', 'ple_0', 'mla', 'layer_1')
    later = [[bf(p['ret_w_out'][0])], [bf(p['mlp_w1'][0])], [bf(p['mlp_w2'][0])], ple(0),
             [bf(p['mla_w_in'][0]), bf(p['mla_w_uq'][0]), bf(p['mla_w_ukv'][0]), bf(p['mla_w_out'][0])],
             [bf(p['mlp_w1'][1]), bf(p['mlp_w2'][1])] + ple(1)]
    first = [gn_pack, bf(p['ret_w_in'][0])]
    behind = {}

    def overlap(token):
        behind['hn0'] = _rms_fwd("mix_norm0", x, p['mix_norm'][0:1], deps=(token,))
        behind['bufs'] = _place_own([b for g in later for b in g])
        return (behind['hn0'], *behind['bufs'])

    (pack, wri), token = _first_gather(first, _place_own(first), overlap)
    bufs = behind['bufs']
    groups, at = [], 0
    for g in later:
        groups.append(list(zip(g, bufs[at:at + len(g)])))
        at += len(g)
    started, token = _ag_start(groups, (token,))

    w = {k: p[k] for k in ('mix_norm', 'mlp_norm', 'ple_norm')}
    w['hn0'] = behind['hn0']
    w['ret_gn'] = pack[:, :RET_HEADS, :RET_DV // n].transpose(1, 0, 2).reshape(RET_HEADS, RET_DV)
    w['mla_q_a_norm'] = pack[:, RET_HEADS, :MLA_Q_RANK // n].reshape(1, MLA_Q_RANK)
    w['mla_kv_a_norm'] = pack[:, RET_HEADS + 1, :MLA_KV_RANK // n].reshape(1, MLA_KV_RANK)
    w['ret_w_in'] = wri
    w['mla_q_norm'] = _spread_rope(p['mla_q_norm'])
    w['mla_k_norm'] = _spread_rope(p['mla_k_norm'])
    w['deps'] = (token,)

    def fetch(name, after):
        got = list(_ag_wait("gather_wait_" + name, *started[names.index(name)], after))
        if name == 'ret_out':
            return dict(ret_w_out=got[0].reshape(RET_V_W, D_MODEL))
        if name == 'mla':
            wmi, wuq, wukv, wmo = got
            return dict(mla_w_in=_spread_rope(wmi.reshape(D_MODEL, MLA_IN)), mla_w_uq=_spread_rope(wuq),
                        mla_w_ukv=wukv, mla_w_out=wmo.reshape(D_MODEL, D_MODEL))
        out = {}
        if name in ('mlp_w1_0', 'layer_1'):
            out['mlp_w1'] = got.pop(0)
        if name in ('mlp_w2_0', 'layer_1'):
            out['mlp_w2'] = got.pop(0)
        if name in ('ple_0', 'layer_1'):
            out['ple_gate_w'] = got[0].reshape(D_MODEL, D_MODEL)
            out['ple_proj_w'] = got[1].transpose(1, 0, 2).reshape(PLE_DIM, D_MODEL)
        return out

    return w, fetch


def _small_grads(small, after):
    rows = [(0, small['mix_norm'][0]), (1, small['mix_norm'][1]), (2, small['mlp_norm'][0]),
            (3, small['mlp_norm'][1]), (4, small['ple_norm'][0]), (5, small['ple_norm'][1]),
            (6, small['ret_gn']), (10, small['mla_q_a_norm']), (11, small['mla_kv_a_norm']),
            (12, small['mla_q_norm']), (13, small['mla_k_norm']), (14, small['sq_err'])]
    gs = _all_reduce_small(rows, after)
    me = _flat(*_my_place())
    n = N_DEV
    return dict(
        sq_err=gs[14, 0],
        mix_norm=gs[0:2], mlp_norm=gs[2:4], ple_norm=gs[4:6],
        ret_gn=lax.dynamic_slice(gs, (6, me * (RET_DV // n)), (RET_HEADS, RET_DV // n)),
        mla_q_a_norm=lax.dynamic_slice(gs, (10, me * (MLA_Q_RANK // n)), (1, MLA_Q_RANK // n)),
        mla_kv_a_norm=lax.dynamic_slice(gs, (11, me * (MLA_KV_RANK // n)), (1, MLA_KV_RANK // n)),
        mla_q_norm=_gather_rope(gs[12:13, :MLA_HD_PAD]), mla_k_norm=_gather_rope(gs[13:14, :MLA_HD_PAD]))


def kernel(x, p, mix_norm, ret_w_in, ret_gn, ret_w_out, mla_w_in, mla_q_a_norm, mla_kv_a_norm, mla_w_uq, mla_w_ukv, mla_q_norm, mla_k_norm, mla_w_out, mlp_norm, mlp_w1, mlp_w2, ple_norm, ple_gate_w, ple_proj_w, loss_target, m_mix_norm, m_ret_w_in, m_ret_gn, m_ret_w_out, m_mla_w_in, m_mla_q_a_norm, m_mla_kv_a_norm, m_mla_w_uq, m_mla_w_ukv, m_mla_q_norm, m_mla_k_norm, m_mla_w_out, m_mlp_norm, m_mlp_w1, m_mlp_w2, m_ple_norm, m_ple_gate_w, m_ple_proj_w, v_mix_norm, v_ret_w_in, v_ret_gn, v_ret_w_out, v_mla_w_in, v_mla_q_a_norm, v_mla_kv_a_norm, v_mla_w_uq, v_mla_w_ukv, v_mla_q_norm, v_mla_k_norm, v_mla_w_out, v_mlp_norm, v_mlp_w1, v_mlp_w2, v_ple_norm, v_ple_gate_w, v_ple_proj_w):
    given = dict(locals())
    params = {n: given[n] for n in WEIGHTS}
    w, fetch = _prepare_weights(params, x[0])

    started = []

    def emit(group):
        keys = list(group)
        send, recv, srcs, lands, token = _rs_start(f"rs_start{len(started)}", [group[k] for k in keys])
        started.append((keys, send, recv, srcs, lands))
        return (token,)

    sq_err, grad_x, _, small = _local_step(x[0], p, loss_target[0], w, fetch, emit)
    small['sq_err'] = sq_err[0:1]

    grads, deltas, new_m, new_v = {}, {}, {}, {}
    total = {}

    def small_updates(after):
        sg = _small_grads(small, after)
        total['loss'] = 0.5 / D_MODEL * sg['sq_err']
        two_d = lambda a: a.reshape(-1, a.shape[-1])
        d_s, m_s, v_s = _adamw_small(
            [two_d(params[n]) for n in SMALL], [sg[n] for n in SMALL],
            [two_d(given["m_" + n]) for n in SMALL], [two_d(given["v_" + n]) for n in SMALL])
        for i, n in enumerate(SMALL):
            shape = params[n].shape
            grads[n], deltas[n], new_m[n], new_v[n] = (a.reshape(shape) for a in (sg[n], d_s[i], m_s[i], v_s[i]))
        return (d_s[0],)

    me = _flat(*_my_place()).astype(jnp.int32).reshape(1)
    after = (grad_x,)
    src_of, land_of = {}, {}
    for gi, (keys, send, recv, srcs, lands) in enumerate(started):
        if gi == len(started) - 1:
            after = small_updates(after)
        srcs, lands = _rs_wait(f"rs_wait{gi}", send, recv, srcs, lands, after)
        for k, s, l in zip(keys, srcs, lands):
            src_of[k], land_of[k] = s, l
        done = [n for n in BIG if n not in grads and all((n, l) in src_of for l in range(params[n].shape[0]))]
        for n in done:
            layers = range(params[n].shape[0])
            grads[n], deltas[n], new_m[n], new_v[n] = _adamw_big(
                "adamw_" + n, params[n], given["m_" + n], given["v_" + n],
                [src_of[(n, l)] for l in layers], [land_of[(n, l)] for l in layers], me)
        if done:
            after = tuple(deltas[n] for n in done)

    return (total['loss'], grad_x[None], *[grads[n] for n in WEIGHTS], *[deltas[n] for n in WEIGHTS],
            *[new_m[n] for n in WEIGHTS], *[new_v[n] for n in WEIGHTS])
```

```python
import functools

import jax
import jax.numpy as jnp
from jax import lax
from jax.experimental import pallas as pl
from jax.experimental.pallas import tpu as pltpu

F32 = jnp.float32
BF16 = jnp.bfloat16
MESH = pl.DeviceIdType.MESH
ANY = pl.BlockSpec(memory_space=pl.ANY)

N_DEV = 8
D_MODEL = 1024
CHUNK = 64
EPS = 1e-6
ROPE_THETA = 10000.0
RET_HEADS = 4
RET_DK = 256
RET_DV = 512
RET_QK_W = RET_HEADS * RET_DK
RET_V_W = RET_HEADS * RET_DV
RET_IN = 2 * RET_QK_W + 2 * RET_V_W
MLA_HEADS = 8
MLA_NOPE = 128
MLA_ROPE = 64
MLA_QKD = MLA_NOPE + MLA_ROPE
MLA_VD = 128
MLA_Q_RANK = 384
MLA_KV_RANK = 256
MLA_IN = MLA_Q_RANK + MLA_KV_RANK + MLA_ROPE
MLA_IN_PAD = 768
MLA_HD_PAD = 256
D_FF = 4096
PLE_DIM = 256
ATT_SCALE = MLA_QKD ** -0.5
LOG2E = 1.4426950408889634
ATT_EXP2 = ATT_SCALE * LOG2E

ADAM_LR = 0.001
ADAM_B1 = 0.9
ADAM_B2 = 0.999
ADAM_EPS = 1e-08
ADAM_WD = 0.01
ADAM_STEP = 10

VMEM_LIMIT = 52 * 1024 * 1024
ROW_TILE = 1024
RET_ROWS = 256
ATT_BLOCK = 256
ATT_QROWS = 1024
ATT_KROWS = 1024
ATT_HEADS = 2

WEIGHTS = ['mix_norm', 'ret_w_in', 'ret_gn', 'ret_w_out', 'mla_w_in', 'mla_q_a_norm', 'mla_kv_a_norm',
           'mla_w_uq', 'mla_w_ukv', 'mla_q_norm', 'mla_k_norm', 'mla_w_out', 'mlp_norm', 'mlp_w1', 'mlp_w2',
           'ple_norm', 'ple_gate_w', 'ple_proj_w']
BIG = ['ret_w_in', 'ret_w_out', 'mla_w_in', 'mla_w_uq', 'mla_w_ukv', 'mla_w_out', 'mlp_w1', 'mlp_w2',
       'ple_gate_w', 'ple_proj_w']
SMALL = [w for w in WEIGHTS if w not in BIG]


def _cparams(sem=None):
    return pltpu.CompilerParams(dimension_semantics=sem, vmem_limit_bytes=VMEM_LIMIT)


def _dot(a, b, ca, cb):
    return lax.dot_general(a, b, (((ca,), (cb,)), ((), ())), preferred_element_type=F32)


def _bf(v):
    return v if v.dtype == BF16 else v.astype(BF16)


def _sigmoid(z):
    return 1.0 / (1.0 + jnp.exp(-z))


def _mm(name, grid, a, a_spec, b, b_spec, contract, outs, extras=(), epi=None, deps=(), split=None):
    nk = grid[2]
    n_ex, n_out, n_dep = len(extras), len(outs), len(deps)
    acc_shape = tuple(d for d in outs[0][1].block_shape if d is not None)
    if split is not None:
        acc_shape = (acc_shape[1], acc_shape[0] * split)

    def body(*refs):
        a_ref, b_ref = refs[:2]
        ex_refs = refs[2:2 + n_ex]
        out_refs = refs[2 + n_ex + n_dep:2 + n_ex + n_dep + n_out]

        def product():
            return _dot(_bf(a_ref[...]), _bf(b_ref[...]), contract[0], contract[1])

        def finish(acc):
            if split is not None:
                for j in range(acc_shape[1] // split):
                    out_refs[0][j] = acc[:, j * split:(j + 1) * split].astype(out_refs[0].dtype)
                return
            acc = acc[...]
            res = epi(acc, *[r[...] for r in ex_refs]) if epi is not None else (acc,)
            for o, r in zip(out_refs, res):
                o[...] = r.astype(o.dtype)

        if nk == 1:
            finish(product())
        else:
            acc_ref = refs[-1]
            k = pl.program_id(2)

            @pl.when(k == 0)
            def _():
                acc_ref[...] = jnp.zeros_like(acc_ref)

            acc_ref[...] += product()

            @pl.when(k == nk - 1)
            def _():
                finish(acc_ref)

    return pl.pallas_call(
        body, name=name, grid=grid,
        in_specs=[a_spec, b_spec] + [s for _, s in extras] + [ANY] * n_dep,
        out_specs=[s for _, s in outs],
        out_shape=[s for s, _ in outs],
        scratch_shapes=[pltpu.VMEM(acc_shape, F32)] if nk > 1 else [],
        compiler_params=_cparams(("parallel", "parallel", "arbitrary")),
    )(a, b, *[x for x, _ in extras], *deps)


def _mm_rows(name, tm, a, w, mode, outs, extras=(), epi=None, deps=()):
    n_sh, rows, cols = w.shape
    n_ex, n_out, n_dep = len(extras), len(outs), len(deps)
    by_cols = mode in ('nn_cols', 'nt_rows')
    width = cols if mode == 'nn_cols' else rows

    def body(*refs):
        a_ref, w_ref = refs[:2]
        ex_refs = refs[2:2 + n_ex]
        out_refs = refs[2 + n_ex + n_dep:2 + n_ex + n_dep + n_out]
        if by_cols:
            av = _bf(a_ref[...])
            for s in range(n_sh):
                cs = slice(s * width, (s + 1) * width)
                acc = _dot(av, w_ref[s], 1, 0 if mode == 'nn_cols' else 1)
                res = epi(acc, *[r[:, cs] for r in ex_refs]) if epi is not None else (acc,)
                for o, r in zip(out_refs, res):
                    o[:, cs] = r.astype(o.dtype)
        else:
            chunk = rows if mode == 'nn_rows' else cols
            acc = None
            for s in range(n_sh):
                part = _dot(_bf(a_ref[:, s * chunk:(s + 1) * chunk]), w_ref[s], 1, 0 if mode == 'nn_rows' else 1)
                acc = part if acc is None else acc + part
            res = epi(acc, *[r[...] for r in ex_refs]) if epi is not None else (acc,)
            for o, r in zip(out_refs, res):
                o[...] = r.astype(o.dtype)

    t, ka = a.shape
    return pl.pallas_call(
        body, name=name, grid=(t // tm, 1, 1),
        in_specs=[pl.BlockSpec((tm, ka), lambda i, j, k: (i, 0)),
                  pl.BlockSpec((n_sh, rows, cols), lambda i, j, k: (0, 0, 0))] + [s for _, s in extras] + [ANY] * n_dep,
        out_specs=[s for _, s in outs],
        out_shape=[s for s, _ in outs],
        compiler_params=_cparams(("parallel", "arbitrary", "arbitrary")),
    )(a, w, *[x for x, _ in extras], *deps)


def _sds(shape, dtype):
    return jax.ShapeDtypeStruct(shape, dtype)


def _row_tile(t, cap=ROW_TILE):
    return min(cap, t)


def _rms_fwd(name, x, g, deps=()):
    t, d = x.shape
    tm = _row_tile(t)

    def body(x_ref, g_ref, *rest):
        o_ref = rest[-1]
        xv = x_ref[...]
        r = lax.rsqrt(jnp.mean(xv * xv, axis=-1, keepdims=True) + EPS)
        o_ref[...] = (xv * r * g_ref[...]).astype(o_ref.dtype)

    return pl.pallas_call(
        body, name=name, grid=(t // tm,),
        in_specs=[pl.BlockSpec((tm, d), lambda i: (i, 0)), pl.BlockSpec((1, d), lambda i: (0, 0))] + [ANY] * len(deps),
        out_specs=pl.BlockSpec((tm, d), lambda i: (i, 0)),
        out_shape=_sds((t, d), BF16),
        compiler_params=_cparams(("parallel",)),
    )(x, g, *deps)


def _rms_bwd_rows(dy, xv, g, n):
    r = lax.rsqrt(jnp.sum(xv * xv, axis=-1, keepdims=True) / n + EPS)
    xh = xv * r
    dxh = dy * g
    dx = r * (dxh - xh * (jnp.sum(dxh * xh, axis=-1, keepdims=True) / n))
    return dx, dy * xh


def _ple_gate_bwd(name, dh, gate, e):
    t, d = dh.shape
    tm = _row_tile(t)

    def body(dh_ref, g_ref, e_ref, de_ref, dz_ref):
        dh_v, gt = dh_ref[...], g_ref[...].astype(F32)
        de_ref[...] = (dh_v * gt).astype(BF16)
        dz_ref[...] = (dh_v * e_ref[...].astype(F32) * (gt * (1.0 - gt))).astype(BF16)

    row = pl.BlockSpec((tm, d), lambda i: (i, 0))
    return pl.pallas_call(
        body, name=name, grid=(t // tm,), in_specs=[row, row, row], out_specs=[row, row],
        out_shape=[_sds((t, d), BF16), _sds((t, d), BF16)],
        compiler_params=_cparams(("parallel",)),
    )(dh, gate, e)


def _rope_half(v, cos, sin):
    half = v.shape[-1] // 2
    v1, v2 = v[:, :half], v[:, half:]
    return jnp.concatenate([v1 * cos - v2 * sin, v2 * cos + v1 * sin], axis=-1)


def _ret_consts():
    lg = jnp.log(1.0 - 2.0 ** (-5.0 - jnp.arange(RET_HEADS, dtype=F32)))
    idx = jnp.arange(CHUNK, dtype=F32)
    intra = jnp.exp(lg[:, None, None] * jnp.abs(idx[:, None] - idx[None, :]))
    qdec = jnp.exp(lg[:, None] * (idx + 1.0))
    kdec = jnp.exp(lg[:, None] * (CHUNK - 1.0 - idx))
    cdec = jnp.exp(lg * CHUNK)
    qdec = jnp.broadcast_to(qdec[:, :, None], (RET_HEADS, CHUNK, RET_DK))
    kdec = jnp.broadcast_to(kdec[:, :, None], (RET_HEADS, CHUNK, RET_DK))
    cdec = jnp.broadcast_to(cdec[:, None, None], (RET_HEADS, 1, RET_DV))
    return intra, qdec, kdec, cdec


def _ret_specs(rb, rev_nb=None):
    blk = (lambda i: i) if rev_nb is None else (lambda i: rev_nb - 1 - i)
    full = lambda shape: pl.BlockSpec(shape, lambda i: (0,) * len(shape))
    return dict(
        proj=pl.BlockSpec((rb, RET_IN), lambda i: (blk(i), 0)),
        tab=pl.BlockSpec((rb, RET_DK // 2), lambda i: (blk(i), 0)),
        vw=pl.BlockSpec((rb, RET_V_W), lambda i: (blk(i), 0)),
        st=pl.BlockSpec((rb // CHUNK, RET_HEADS, RET_DK, RET_DV), lambda i: (blk(i), 0, 0, 0)),
        gn=full((RET_HEADS, 1, RET_DV)),
        intra=full((RET_HEADS, CHUNK, CHUNK)),
        dec=full((RET_HEADS, CHUNK, RET_DK)),
        cdec=full((RET_HEADS, 1, RET_DV)),
    )


def _ret_fwd(proj, cos, sin, gn):
    t = proj.shape[0]
    rb = min(RET_ROWS, t)
    cpb = rb // CHUNK
    intra, qdec, kdec, cdec = _ret_consts()
    sp = _ret_specs(rb)

    def body(proj_ref, cos_ref, sin_ref, gn_ref, intra_ref, qd_ref, kd_ref, cd_ref,
             gated_ref, outp_ref, st_ref, s_ref):
        @pl.when(pl.program_id(0) == 0)
        def _():
            s_ref[...] = jnp.zeros_like(s_ref)

        def chunk(c, carry):
            rows = pl.ds(pl.multiple_of(c * CHUNK, CHUNK), CHUNK)
            cs, sn = cos_ref[rows, :], sin_ref[rows, :]
            for h in range(RET_HEADS):
                q = proj_ref[rows, h * RET_DK:(h + 1) * RET_DK].astype(F32)
                k = proj_ref[rows, RET_QK_W + h * RET_DK:RET_QK_W + (h + 1) * RET_DK].astype(F32)
                v = proj_ref[rows, 2 * RET_QK_W + h * RET_DV:2 * RET_QK_W + (h + 1) * RET_DV]
                g = proj_ref[rows, 2 * RET_QK_W + RET_V_W + h * RET_DV:
                             2 * RET_QK_W + RET_V_W + (h + 1) * RET_DV].astype(F32)
                qr = _rope_half(q, cs, sn)
                kr = _rope_half(k, cs, sn) * (RET_DK ** -0.5)
                qb, kb, vb = qr.astype(BF16), kr.astype(BF16), v
                sc = _dot(qb, kb, 1, 1) * intra_ref[h]
                inner = _dot(sc.astype(BF16), vb, 1, 0)
                s_old = s_ref[h]
                sb = s_old.astype(BF16)
                st_ref[c, h] = sb
                cross = _dot((qr * qd_ref[h]).astype(BF16), sb, 1, 0)
                out = inner + cross
                s_ref[h] = s_old * cd_ref[h] + _dot((kr * kd_ref[h]).astype(BF16), vb, 0, 0)
                r = lax.rsqrt(jnp.mean(out * out, axis=-1, keepdims=True) + EPS)
                y = out * r * gn_ref[h]
                cols = slice(h * RET_DV, (h + 1) * RET_DV)
                gated_ref[rows, cols] = (g * _sigmoid(g) * y).astype(BF16)
                outp_ref[rows, cols] = out
            return carry

        lax.fori_loop(0, cpb, chunk, 0)

    return pl.pallas_call(
        body, name="ret_fwd", grid=(t // rb,),
        in_specs=[sp['proj'], sp['tab'], sp['tab'], sp['gn'], sp['intra'], sp['dec'], sp['dec'], sp['cdec']],
        out_specs=[sp['vw'], sp['vw'], sp['st']],
        out_shape=[_sds((t, RET_V_W), BF16), _sds((t, RET_V_W), F32),
                   _sds((t // CHUNK, RET_HEADS, RET_DK, RET_DV), BF16)],
        scratch_shapes=[pltpu.VMEM((RET_HEADS, RET_DK, RET_DV), F32)],
        compiler_params=_cparams(("arbitrary",)),
    )(proj, cos, sin, gn.reshape(RET_HEADS, 1, RET_DV), intra, qdec, kdec, cdec)


def _ret_gate_bwd_epi(dgt, out, g, gn):
    g = g.astype(F32)
    r = lax.rsqrt(jnp.mean(out * out, axis=-1, keepdims=True) + EPS)
    xh = out * r
    sg = _sigmoid(g)
    dgate = dgt * (xh * gn) * (sg * (1.0 + g * (1.0 - sg)))
    dy = dgt * (g * sg)
    dxh = dy * gn
    dout = r * (dxh - xh * jnp.mean(dxh * xh, axis=-1, keepdims=True))
    return dout, dgate, jnp.sum(dy * xh, axis=0, keepdims=True)


def _ret_bwd(proj, cos, sin, states, dout, dgate, deps=()):
    t = proj.shape[0]
    rb = min(RET_ROWS, t)
    cpb = rb // CHUNK
    nb = t // rb
    intra, qdec, kdec, cdec = _ret_consts()
    sp = _ret_specs(rb, rev_nb=nb)

    def body(proj_ref, cos_ref, sin_ref, intra_ref, qd_ref, kd_ref, cd_ref, st_ref, dout_ref, dgate_ref, *rest):
        dproj_ref, ds_ref = rest[len(deps):]

        @pl.when(pl.program_id(0) == 0)
        def _():
            ds_ref[...] = jnp.zeros_like(ds_ref)

        def chunk(cc, carry):
            c = cpb - 1 - cc
            rows = pl.ds(pl.multiple_of(c * CHUNK, CHUNK), CHUNK)
            cs, sn = cos_ref[rows, :], sin_ref[rows, :]
            for h in range(RET_HEADS):
                q = proj_ref[rows, h * RET_DK:(h + 1) * RET_DK].astype(F32)
                k = proj_ref[rows, RET_QK_W + h * RET_DK:RET_QK_W + (h + 1) * RET_DK].astype(F32)
                v = proj_ref[rows, 2 * RET_QK_W + h * RET_DV:2 * RET_QK_W + (h + 1) * RET_DV]
                cols = slice(h * RET_DV, (h + 1) * RET_DV)
                qr = _rope_half(q, cs, sn)
                kr = _rope_half(k, cs, sn) * (RET_DK ** -0.5)
                qb, kb, vb = qr.astype(BF16), kr.astype(BF16), v
                qdb = (qr * qd_ref[h]).astype(BF16)
                kdb = (kr * kd_ref[h]).astype(BF16)
                doutb = dout_ref[rows, cols]
                itr = intra_ref[h]
                pb = (_dot(qb, kb, 1, 1) * itr).astype(BF16)
                dv = _dot(pb, doutb, 0, 0)
                dsc = (_dot(doutb, vb, 1, 1) * itr).astype(BF16)
                dq = _dot(dsc, kb, 1, 0)
                dk = _dot(dsc, qb, 0, 0)
                dq = dq + _dot(doutb, st_ref[c, h], 1, 1) * qd_ref[h]
                ds_new = ds_ref[h]
                dsb = ds_new.astype(BF16)
                dk = dk + _dot(vb, dsb, 1, 1) * kd_ref[h]
                dv = dv + _dot(kdb, dsb, 1, 0)
                ds_ref[h] = ds_new * cd_ref[h] + _dot(qdb, doutb, 0, 0)
                dproj_ref[rows, h * RET_DK:(h + 1) * RET_DK] = _rope_half(dq, cs, -sn).astype(BF16)
                dproj_ref[rows, RET_QK_W + h * RET_DK:RET_QK_W + (h + 1) * RET_DK] = (
                    _rope_half(dk * (RET_DK ** -0.5), cs, -sn).astype(BF16))
                dproj_ref[rows, 2 * RET_QK_W + h * RET_DV:2 * RET_QK_W + (h + 1) * RET_DV] = dv.astype(BF16)
                dproj_ref[rows, 2 * RET_QK_W + RET_V_W + h * RET_DV:
                          2 * RET_QK_W + RET_V_W + (h + 1) * RET_DV] = dgate_ref[rows, cols]
            return carry

        lax.fori_loop(0, cpb, chunk, 0)

    return pl.pallas_call(
        body, name="ret_bwd", grid=(nb,),
        in_specs=[sp['proj'], sp['tab'], sp['tab'], sp['intra'], sp['dec'], sp['dec'], sp['cdec'],
                  sp['st'], sp['vw'], sp['vw']] + [ANY] * len(deps),
        out_specs=sp['proj'],
        out_shape=_sds((t, RET_IN), BF16),
        scratch_shapes=[pltpu.VMEM((RET_HEADS, RET_DK, RET_DV), F32)],
        compiler_params=_cparams(("arbitrary",)),
    )(proj, cos, sin, intra, qdec, kdec, cdec, states, dout, dgate, *deps)


def _spread_rope(a):
    return jnp.pad(a, [(0, 0)] * (a.ndim - 1) + [(0, MLA_ROPE)])


def _gather_rope(a):
    return a[..., :a.shape[-1] - MLA_ROPE]


def _mla_tables(t):
    half = MLA_ROPE // 2
    inv = 1.0 / (ROPE_THETA ** (jnp.arange(0, MLA_ROPE, 2, dtype=F32) / MLA_ROPE))
    ang = jnp.arange(t, dtype=F32)[:, None] * inv[None, :]
    cos, sin = jnp.cos(ang), jnp.sin(ang)
    z = jnp.zeros((t, half), F32)
    c = jnp.concatenate([cos, cos, z, z], axis=1)
    s1 = jnp.concatenate([-sin, z, z, z], axis=1)
    s2 = jnp.concatenate([z, sin, z, z], axis=1)
    return c, s1, s2


def _rope_tile(r, c, s1, s2):
    return r * c + pltpu.roll(r, 96, 1) * s1 + pltpu.roll(r, 32, 1) * s2


def _mla_mid(proj2, qa, kva):
    t = proj2.shape[0]
    tm = _row_tile(t)

    def body(p_ref, qa_ref, kva_ref, cq_ref, ckv_ref):
        cq = p_ref[:, :MLA_Q_RANK]
        ckv = p_ref[:, MLA_Q_RANK:MLA_Q_RANK + MLA_KV_RANK]
        rq = lax.rsqrt(jnp.mean(cq * cq, axis=-1, keepdims=True) + EPS)
        rkv = lax.rsqrt(jnp.mean(ckv * ckv, axis=-1, keepdims=True) + EPS)
        cq_ref[...] = (cq * rq * qa_ref[...]).astype(BF16)
        ckv_ref[...] = (ckv * rkv * kva_ref[...]).astype(BF16)

    return pl.pallas_call(
        body, name="mla_mid", grid=(t // tm,),
        in_specs=[pl.BlockSpec((tm, MLA_IN_PAD), lambda i: (i, 0)),
                  pl.BlockSpec((1, MLA_Q_RANK), lambda i: (0, 0)),
                  pl.BlockSpec((1, MLA_KV_RANK), lambda i: (0, 0))],
        out_specs=[pl.BlockSpec((tm, MLA_Q_RANK), lambda i: (i, 0)),
                   pl.BlockSpec((tm, MLA_KV_RANK), lambda i: (i, 0))],
        out_shape=[_sds((t, MLA_Q_RANK), BF16), _sds((t, MLA_KV_RANK), BF16)],
        compiler_params=_cparams(("parallel",)),
    )(proj2, qa, kva)


def _mla_mid_bwd(proj2, qa, kva, dcq, dckv, dkr):
    t = proj2.shape[0]
    tm = _row_tile(t)

    def body(p_ref, qa_ref, kva_ref, dcq_ref, dckv_ref, dkr_ref, dp_ref, dqa_ref, dkva_ref):
        @pl.when(pl.program_id(0) == 0)
        def _():
            dqa_ref[...] = jnp.zeros_like(dqa_ref)
            dkva_ref[...] = jnp.zeros_like(dkva_ref)

        dxq, dgq = _rms_bwd_rows(dcq_ref[...], p_ref[:, :MLA_Q_RANK], qa_ref[...], MLA_Q_RANK)
        dxk, dgk = _rms_bwd_rows(dckv_ref[...], p_ref[:, MLA_Q_RANK:MLA_Q_RANK + MLA_KV_RANK], kva_ref[...],
                                 MLA_KV_RANK)
        dp_ref[:, :MLA_Q_RANK] = dxq.astype(BF16)
        dp_ref[:, MLA_Q_RANK:MLA_Q_RANK + MLA_KV_RANK] = dxk.astype(BF16)
        dp_ref[:, MLA_Q_RANK + MLA_KV_RANK:] = dkr_ref[...].astype(BF16)
        dqa_ref[...] += jnp.sum(dgq, axis=0, keepdims=True)
        dkva_ref[...] += jnp.sum(dgk, axis=0, keepdims=True)

    return pl.pallas_call(
        body, name="mla_mid_bwd", grid=(t // tm,),
        in_specs=[pl.BlockSpec((tm, MLA_IN_PAD), lambda i: (i, 0)),
                  pl.BlockSpec((1, MLA_Q_RANK), lambda i: (0, 0)),
                  pl.BlockSpec((1, MLA_KV_RANK), lambda i: (0, 0)),
                  pl.BlockSpec((tm, MLA_Q_RANK), lambda i: (i, 0)),
                  pl.BlockSpec((tm, MLA_KV_RANK), lambda i: (i, 0)),
                  pl.BlockSpec((tm, 128), lambda i: (i, 0))],
        out_specs=[pl.BlockSpec((tm, MLA_IN_PAD), lambda i: (i, 0)),
                   pl.BlockSpec((1, MLA_Q_RANK), lambda i: (0, 0)),
                   pl.BlockSpec((1, MLA_KV_RANK), lambda i: (0, 0))],
        out_shape=[_sds((t, MLA_IN_PAD), BF16), _sds((1, MLA_Q_RANK), F32), _sds((1, MLA_KV_RANK), F32)],
        compiler_params=_cparams(("arbitrary",)),
    )(proj2, qa, kva, dcq, dckv, dkr)


def _mla_prep_specs(t, tm):
    head = lambda w: pl.BlockSpec((None, tm, w), lambda i, h: (h, i, 0))
    return dict(
        head256=head(MLA_HD_PAD), head128=head(MLA_VD),
        cols256=pl.BlockSpec((tm, MLA_HD_PAD), lambda i, h: (i, h)),
        cq=pl.BlockSpec((tm, MLA_Q_RANK), lambda i, h: (i, 0)),
        ckv=pl.BlockSpec((tm, MLA_KV_RANK), lambda i, h: (i, 0)),
        wuq=pl.BlockSpec((None, MLA_Q_RANK, MLA_HD_PAD), lambda i, h: (h, 0, 0)),
        wukv=pl.BlockSpec((None, MLA_KV_RANK, MLA_HD_PAD), lambda i, h: (h, 0, 0)),
        kr=pl.BlockSpec((tm, 128), lambda i, h: (i, (MLA_Q_RANK + MLA_KV_RANK) // 128)),
        gain=pl.BlockSpec((1, MLA_HD_PAD), lambda i, h: (0, 0)),
        tab=pl.BlockSpec((tm, 128), lambda i, h: (i, 0)),
    )


def _mla_prep(cq, ckv, wuq, wukv, proj2, gq, gk, tabs):
    t = cq.shape[0]
    tm = _row_tile(t)
    sp = _mla_prep_specs(t, tm)

    def body(cq_ref, ckv_ref, wuq_ref, wukv_ref, kr_ref, gq_ref, gk_ref, c_ref, s1_ref, s2_ref,
             qh_ref, kh_ref, vh_ref):
        c, s1, s2 = c_ref[...], s1_ref[...], s2_ref[...]

        def norm_rope(xv, gain):
            r = lax.rsqrt(jnp.sum(xv * xv, axis=-1, keepdims=True) / MLA_QKD + EPS)
            y = xv * r * gain
            return jnp.concatenate([y[:, :MLA_NOPE], _rope_tile(y[:, MLA_NOPE:], c, s1, s2)], axis=-1)

        kvv = _dot(ckv_ref[...], wukv_ref[...], 1, 0)
        qh_ref[...] = norm_rope(_dot(cq_ref[...], wuq_ref[...], 1, 0), gq_ref[...]).astype(BF16)
        kf = jnp.concatenate([kvv[:, :MLA_NOPE], kr_ref[...]], axis=-1)
        kh_ref[...] = norm_rope(kf, gk_ref[...]).astype(BF16)
        vh_ref[...] = jnp.concatenate([kvv[:, MLA_NOPE:], jnp.ones((tm, MLA_VD), F32)], axis=-1).astype(BF16)

    return pl.pallas_call(
        body, name="mla_prep", grid=(t // tm, MLA_HEADS),
        in_specs=[sp['cq'], sp['ckv'], sp['wuq'], sp['wukv'], sp['kr'], sp['gain'], sp['gain'],
                  sp['tab'], sp['tab'], sp['tab']],
        out_specs=[sp['head256'], sp['head256'], sp['head256']],
        out_shape=[_sds((MLA_HEADS, t, MLA_HD_PAD), BF16), _sds((MLA_HEADS, t, MLA_HD_PAD), BF16),
                   _sds((MLA_HEADS, t, 2 * MLA_VD), BF16)],
        compiler_params=_cparams(("parallel", "arbitrary")),
    )(cq, ckv, wuq, wukv, proj2, gq, gk, *tabs)


def _mla_prep_bwd(cq, ckv, wuq, wukv, proj2, gq, gk, tabs, dqt, dkh, dvh):
    t = cq.shape[0]
    tm = _row_tile(t)
    ab = dqt.shape[-1]
    sp = _mla_prep_specs(t, tm)

    def body(cq_ref, ckv_ref, wuq_ref, wukv_ref, kr_ref, gq_ref, gk_ref, c_ref, s1_ref, s2_ref,
             dqt_ref, dkh_ref, dvh_ref, dq_ref, dkv_ref, dkr_ref, dgq_ref, dgk_ref):
        dqh = jnp.concatenate([dqt_ref[b].T for b in range(tm // ab)], axis=0)
        i, h = pl.program_id(0), pl.program_id(1)

        @pl.when((i == 0) & (h == 0))
        def _():
            dgq_ref[...] = jnp.zeros_like(dgq_ref)
            dgk_ref[...] = jnp.zeros_like(dgk_ref)

        @pl.when(h == 0)
        def _():
            dkr_ref[...] = jnp.zeros_like(dkr_ref)

        c, s1, s2 = c_ref[...], s1_ref[...], s2_ref[...]

        def back(xv, gain, dout):
            dy = jnp.concatenate([dout[:, :MLA_NOPE], _rope_tile(dout[:, MLA_NOPE:], c, -s1, -s2)], axis=-1)
            return _rms_bwd_rows(dy, xv, gain, MLA_QKD)

        kvv = _dot(ckv_ref[...], wukv_ref[...], 1, 0)
        dxq, dgq = back(_dot(cq_ref[...], wuq_ref[...], 1, 0), gq_ref[...], dqh)
        kf = jnp.concatenate([kvv[:, :MLA_NOPE], kr_ref[...]], axis=-1)
        dxk, dgk = back(kf, gk_ref[...], dkh_ref[...])
        dq_ref[...] = dxq.astype(BF16)
        dkv_ref[...] = jnp.concatenate([dxk[:, :MLA_NOPE], dvh_ref[...]], axis=-1).astype(BF16)
        dkr_ref[...] += dxk[:, MLA_NOPE:]
        dgq_ref[...] += jnp.sum(dgq, axis=0, keepdims=True)
        dgk_ref[...] += jnp.sum(dgk, axis=0, keepdims=True)

    return pl.pallas_call(
        body, name="mla_prep_bwd", grid=(t // tm, MLA_HEADS),
        in_specs=[sp['cq'], sp['ckv'], sp['wuq'], sp['wukv'], sp['kr'], sp['gain'], sp['gain'],
                  sp['tab'], sp['tab'], sp['tab'],
                  pl.BlockSpec((None, tm // ab, MLA_HD_PAD, ab), lambda i, h: (h, i, 0, 0)),
                  sp['head256'], sp['head128']],
        out_specs=[sp['cols256'], sp['cols256'], sp['tab'], sp['gain'], sp['gain']],
        out_shape=[_sds((t, MLA_HEADS * MLA_HD_PAD), BF16), _sds((t, MLA_HEADS * MLA_HD_PAD), BF16),
                   _sds((t, 128), F32), _sds((1, MLA_HD_PAD), F32), _sds((1, MLA_HD_PAD), F32)],
        compiler_params=_cparams(("arbitrary", "arbitrary")),
    )(cq, ckv, wuq, wukv, proj2, gq, gk, *tabs, dqt, dkh, dvh)


def _chunk_visible(rows, cols, row_off, col_off):
    rq = lax.shift_right_logical(lax.broadcasted_iota(jnp.int32, (rows, cols), 0) + row_off, 6)
    ck = lax.shift_right_logical(lax.broadcasted_iota(jnp.int32, (rows, cols), 1) + col_off, 6)
    return ck <= rq


def _rows_to_lanes(col):
    return col.T[:8, :]


def _attn_fwd(qh, kh, vh):
    t = qh.shape[1]
    ab = min(ATT_BLOCK, t)
    tq = min(ATT_QROWS, t)
    r = tq // ab
    hg = ATT_HEADS

    def body(q_ref, k_ref, v_ref, o_ref, lse_ref, acc_ref):
        n_un = pl.program_id(1) * r
        acc_ref[...] = jnp.zeros_like(acc_ref)

        def step(b, ms, diag):
            rows = pl.ds(pl.multiple_of(b * ab, ab), ab)
            out = []
            for hh in range(hg):
                m = ms[hh]
                s = _dot(q_ref[hh], k_ref[hh, rows, :], 1, 1)
                if diag is not None:
                    s = jnp.where(_chunk_visible(tq, ab, 0, diag * ab), s, -1e30)
                m_new = jnp.maximum(m, jnp.max(s, axis=-1, keepdims=True))
                p = jnp.exp2((s - m_new) * ATT_EXP2).astype(BF16)
                acc_ref[hh] = jnp.exp2((m - m_new) * ATT_EXP2) * acc_ref[hh] + _dot(p, v_ref[hh, rows, :], 1, 0)
                out.append(m_new)
            return tuple(out)

        ms = tuple(jnp.full((tq, 1), -1e30, F32) for _ in range(hg))
        ms = lax.fori_loop(0, n_un, lambda b, st: step(b, st, None), ms)
        for d in range(r):
            ms = step(n_un + d, ms, d)
        for hh in range(hg):
            l = acc_ref[hh, :, MLA_VD:]
            o_ref[:, hh * MLA_VD:(hh + 1) * MLA_VD] = acc_ref[hh, :, :MLA_VD] / l
            lse_t = _rows_to_lanes(ms[hh] * ATT_EXP2 + jnp.log(l) * LOG2E)
            for d in range(r):
                lse_ref[hh, d] = lse_t[:, d * ab:(d + 1) * ab]

    return pl.pallas_call(
        body, name="mla_attn", grid=(MLA_HEADS // hg, t // tq),
        in_specs=[pl.BlockSpec((hg, tq, MLA_HD_PAD), lambda g, i: (g, i, 0)),
                  pl.BlockSpec((hg, t, MLA_HD_PAD), lambda g, i: (g, 0, 0)),
                  pl.BlockSpec((hg, t, 2 * MLA_VD), lambda g, i: (g, 0, 0))],
        out_specs=[pl.BlockSpec((tq, hg * MLA_VD), lambda g, i: (i, g)),
                   pl.BlockSpec((hg, r, 8, ab), lambda g, i: (g, i, 0, 0))],
        out_shape=[_sds((t, MLA_HEADS * MLA_VD), F32), _sds((MLA_HEADS, t // ab, 8, ab), F32)],
        scratch_shapes=[pltpu.VMEM((hg, tq, 2 * MLA_VD), F32)],
        compiler_params=_cparams(("parallel", "arbitrary")),
    )(qh, kh, vh)


def _attn_delta(do, o, ab):
    t = do.shape[0]
    tm = _row_tile(t)

    def body(do_ref, o_ref, d_ref):
        d = jnp.sum(do_ref[...] * o_ref[...], axis=-1, keepdims=True)
        d_t = _rows_to_lanes(jnp.broadcast_to(d, (tm, 128)))
        for b in range(tm // ab):
            d_ref[b] = d_t[:, b * ab:(b + 1) * ab]

    col = pl.BlockSpec((tm, MLA_VD), lambda i, h: (i, h))
    return pl.pallas_call(
        body, name="mla_delta", grid=(t // tm, MLA_HEADS), in_specs=[col, col],
        out_specs=pl.BlockSpec((None, tm // ab, 8, ab), lambda i, h: (h, i, 0, 0)),
        out_shape=_sds((MLA_HEADS, t // ab, 8, ab), F32),
        compiler_params=_cparams(("parallel", "parallel")),
    )(do, o)


def _attn_bwd(qh, kh, vh, dob, lse_t, dl_t):
    t = qh.shape[1]
    ab = min(ATT_BLOCK, t)
    kb = min(ATT_KROWS, t)
    r = kb // ab
    nq = t // ab
    hg = ATT_HEADS

    def body(q_ref, k_ref, v_ref, do_ref, lse_ref, dl_ref, dqt_ref, dk_ref, dv_ref):
        j = pl.program_id(1)

        @pl.when(j == 0)
        def _():
            dqt_ref[...] = jnp.zeros_like(dqt_ref)

        ks = [k_ref[hh] for hh in range(hg)]
        vs = [v_ref[hh, :, :MLA_VD] for hh in range(hg)]
        kts = [k.T for k in ks]

        dk_ref[...] = jnp.zeros_like(dk_ref)
        dv_ref[...] = jnp.zeros_like(dv_ref)

        def step(b, carry, diag):
            rows = pl.ds(pl.multiple_of(b * ab, ab), ab)
            hi = kb if diag is None else (diag + 1) * ab
            for hh in range(hg):
                q = q_ref[hh, rows, :]
                do = do_ref[rows, hh * MLA_VD:(hh + 1) * MLA_VD]
                s_t = _dot(ks[hh][:hi], q, 1, 1)
                if diag is not None:
                    key_chunk = lax.shift_right_logical(lax.broadcasted_iota(jnp.int32, (hi, ab), 0), 6)
                    query_chunk = lax.shift_right_logical(
                        lax.broadcasted_iota(jnp.int32, (hi, ab), 1) + diag * ab, 6)
                    s_t = jnp.where(key_chunk <= query_chunk, s_t, -1e30)
                p_t = jnp.exp2(s_t * ATT_EXP2 - lse_ref[hh, b][0:1, :])
                dp_t = _dot(vs[hh][:hi], do, 1, 1)
                ds_t = (p_t * (dp_t - dl_ref[hh, b][0:1, :]) * ATT_SCALE).astype(BF16)
                dqt_ref[hh, b] += _dot(kts[hh][:, :hi], ds_t, 1, 0)
                dk_ref[hh, :hi] += _dot(ds_t, q, 1, 0)
                dv_ref[hh, :hi] += _dot(p_t.astype(BF16), do, 1, 0)
            return carry

        for d in range(r):
            step(j * r + d, 0, d)
        lax.fori_loop((j + 1) * r, nq, lambda b, c: step(b, c, None), 0)

    whole = lambda w: pl.BlockSpec((hg, t, w), lambda g, j: (g, 0, 0))
    blk = lambda w: pl.BlockSpec((hg, kb, w), lambda g, j: (g, j, 0))
    stat = pl.BlockSpec((hg, nq, 8, ab), lambda g, j: (g, 0, 0, 0))
    return pl.pallas_call(
        body, name="mla_attn_bwd", grid=(MLA_HEADS // hg, t // kb),
        in_specs=[whole(MLA_HD_PAD), blk(MLA_HD_PAD), blk(2 * MLA_VD),
                  pl.BlockSpec((t, hg * MLA_VD), lambda g, j: (0, g)), stat, stat],
        out_specs=[pl.BlockSpec((hg, nq, MLA_HD_PAD, ab), lambda g, j: (g, 0, 0, 0)), blk(MLA_HD_PAD), blk(MLA_VD)],
        out_shape=[_sds((MLA_HEADS, nq, MLA_HD_PAD, ab), F32), _sds((MLA_HEADS, t, MLA_HD_PAD), F32),
                   _sds((MLA_HEADS, t, MLA_VD), F32)],
        compiler_params=_cparams(("parallel", "arbitrary")),
    )(qh, kh, vh, dob, lse_t, dl_t)


VEC = pl.BlockSpec((1, D_MODEL), lambda i, j, k: (0, 0))


def _rows(tm, width):
    return pl.BlockSpec((tm, width), lambda i, j, k: (i, 0))


def _residual_epi(next_gain):
    if next_gain is None:
        return [], lambda acc, hv: (acc + hv,)

    def epi(acc, hv, g):
        h_new = acc + hv
        r = lax.rsqrt(jnp.mean(h_new * h_new, axis=-1, keepdims=True) + EPS)
        return h_new, h_new * r * g

    return [(next_gain, VEC)], epi


def _residual_outs(t, row, next_gain):
    outs = [(_sds((t, D_MODEL), F32), row)]
    return outs + ([(_sds((t, D_MODEL), BF16), row)] if next_gain is not None else [])


def _mlp_fwd(l, h, hn, w1g, fetch_w2, next_gain):
    t = h.shape[0]
    tm = _row_tile(t, 512)

    def relu2(acc):
        r = jnp.maximum(acc, 0.0)
        return (r * r,)

    (u,) = _mm_rows(f"mlp_up{l}", tm, hn, w1g, 'nn_cols', [(_sds((t, D_FF), BF16), _rows(tm, D_FF))], epi=relu2)
    w2g = fetch_w2((u,))
    row = _rows(tm, D_MODEL)
    more, epi = _residual_epi(next_gain)
    h2, hn_next = _mm_rows(f"mlp_down{l}", tm, u, w2g, 'nn_rows', _residual_outs(t, row, next_gain),
                           extras=[(h, row)] + more, epi=epi)
    return h2, hn_next, (h, hn, u, w1g, w2g)


def _norm_bwd_outs(t, tm):
    return [(_sds((t, D_MODEL), F32), pl.BlockSpec((tm, D_MODEL), lambda i, j, k: (i, 0))),
            (_sds((t // tm, 1, D_MODEL), F32), pl.BlockSpec((None, 1, D_MODEL), lambda i, j, k: (i, 0, 0)))]


def _norm_bwd_epi(acc, xv, res, g):
    dx, dgr = _rms_bwd_rows(acc, xv, g, D_MODEL)
    return res + dx, jnp.sum(dgr, axis=0, keepdims=True)


def _mlp_bwd(l, dh, saved, norm_g):
    h, hn, u, w1g, w2g = saved
    t = h.shape[0]
    tm = _row_tile(t, 512)
    nsh, _, wsh = w1g.shape
    wide = _rows(tm, D_FF)
    (da,) = _mm_rows(f"mlp_du{l}", tm, dh, w2g, 'nt_rows', [(_sds((t, D_FF), BF16), wide)], extras=[(u, wide)],
                     epi=lambda acc, uv: (2.0 * jnp.sqrt(uv.astype(F32)) * acc,))
    tw = _row_tile(t, 512)
    (dw2,) = _mm(f"mlp_dw2{l}", (1, 1, t // tw),
                 u, pl.BlockSpec((tw, D_FF), lambda i, j, k: (k, 0)),
                 dh, pl.BlockSpec((tw, D_MODEL), lambda i, j, k: (k, 0)), (0, 0),
                 [(_sds((D_FF, D_MODEL), BF16), pl.BlockSpec((D_FF, D_MODEL), lambda i, j, k: (0, 0)))])
    dw2 = dw2.reshape(nsh, wsh, D_MODEL)
    (dw1,) = _mm(f"mlp_dw1{l}", (1, 1, t // tw),
                 hn, pl.BlockSpec((tw, D_MODEL), lambda i, j, k: (k, 0)),
                 da, pl.BlockSpec((tw, D_FF), lambda i, j, k: (k, 0)), (0, 0),
                 [(_sds((nsh, D_MODEL, wsh), BF16), pl.BlockSpec((nsh, D_MODEL, wsh), lambda i, j, k: (0, 0, 0)))],
                 split=wsh)
    row = _rows(tm, D_MODEL)
    dh_in, dg = _mm_rows(f"mlp_dhn{l}", tm, da, w1g, 'nt_cols', _norm_bwd_outs(t, tm),
                         extras=[(h, row), (dh, row), (norm_g, VEC)], epi=_norm_bwd_epi)
    return dh_in, jnp.sum(dg, axis=0), dw1, dw2


def _ple_fwd(l, h, hn, p, wg, wp, next_gain, target=None):
    t = h.shape[0]
    tm = _row_tile(t, 512)
    row = pl.BlockSpec((tm, D_MODEL), lambda i, j, k: (i, 0))
    full = lambda r: pl.BlockSpec((r, D_MODEL), lambda i, j, k: (0, 0))
    f32_row, bf_row = (_sds((t, D_MODEL), F32), row), (_sds((t, D_MODEL), BF16), row)
    common = [(h, row), (p, pl.BlockSpec((None, None, tm, PLE_DIM), lambda i, j, k: (l, 0, i, 0))),
              (wp, full(PLE_DIM))]
    if target is not None:
        def loss_epi(acc, hv, pv, wpv, tv):
            gt = _sigmoid(acc)
            ev = _dot(_bf(pv), wpv, 1, 0)
            err = hv + gt * ev - tv
            sq = jnp.sum(jnp.sum(err * err, axis=-1, keepdims=True), axis=0, keepdims=True)
            return err / D_MODEL, gt, ev, jnp.broadcast_to(sq, (8, 128))

        dy, gate, e, sq = _mm(f"ple_gate{l}", (t // tm, 1, 1), hn, row, wg, full(D_MODEL), (1, 0),
                              [f32_row, bf_row, bf_row, (_sds((t // tm, 8, 128), F32),
                                                         pl.BlockSpec((None, 8, 128), lambda i, j, k: (i, 0, 0)))],
                              extras=common + [(target, row)], epi=loss_epi)
        return dy, jnp.sum(sq, axis=0), (h, hn, gate, e)

    def gate_epi(acc, hv, pv, wpv, *gain):
        gt = _sigmoid(acc)
        ev = _dot(_bf(pv), wpv, 1, 0)
        h_new = hv + gt * ev
        if not gain:
            return h_new, gt, ev
        r = lax.rsqrt(jnp.mean(h_new * h_new, axis=-1, keepdims=True) + EPS)
        return h_new, gt, ev, h_new * r * gain[0]

    res = _mm(f"ple_gate{l}", (t // tm, 1, 1), hn, row, wg, full(D_MODEL), (1, 0),
              [f32_row, bf_row, bf_row] + ([bf_row] if next_gain is not None else []),
              extras=common + ([(next_gain, VEC)] if next_gain is not None else []), epi=gate_epi)
    h_out, gate, e = res[0], res[1], res[2]
    return h_out, (res[3] if next_gain is not None else None), (h, hn, gate, e)


def _ple_bwd(l, dh, saved, p, norm_g, wg, deps=()):
    h, hn, gate, e = saved
    t = h.shape[0]
    tm = _row_tile(t)
    tk = _row_tile(t, 512)
    de, dz = _ple_gate_bwd(f"ple_gate_bwd{l}", dh, gate, e)
    full = lambda r: pl.BlockSpec((r, D_MODEL), lambda i, j, k: (0, 0))
    rowk = pl.BlockSpec((tk, D_MODEL), lambda i, j, k: (k, 0))
    (dwp,) = _mm(f"ple_dwp{l}", (1, 1, t // tk),
                 p, pl.BlockSpec((None, None, tk, PLE_DIM), lambda i, j, k: (l, 0, k, 0)),
                 de, rowk, (0, 0), [(_sds((PLE_DIM, D_MODEL), BF16), full(PLE_DIM))], deps=deps)
    (dwg,) = _mm(f"ple_dwg{l}", (1, 1, t // tk), hn, rowk, dz, rowk, (0, 0),
                 [(_sds((D_MODEL, D_MODEL), BF16), full(D_MODEL))])
    row = pl.BlockSpec((tm, D_MODEL), lambda i, j, k: (i, 0))
    dh_in, dg = _mm(f"ple_dhn{l}", (t // tm, 1, 1), dz, row, wg, full(D_MODEL), (1, 1),
                    _norm_bwd_outs(t, tm), extras=[(h, row), (dh, row), (norm_g, VEC)], epi=_norm_bwd_epi)
    return dh_in, jnp.sum(dg, axis=0), dwg, dwp


def _ret_layer_fwd(x, norm_g, wri, fetch_wro, gn, cos, sin, next_gain, hn=None, deps=()):
    t = x.shape[0]
    tm = _row_tile(t)
    nsh, _, wsh = wri.shape
    if hn is None:
        hn = _rms_fwd("mix_norm0", x, norm_g)
    tp = _row_tile(t, 512)
    (proj,) = _mm_rows("ret_in", tp, hn, wri, 'nn_cols', [(_sds((t, RET_IN), BF16), _rows(tp, RET_IN))], deps=deps)
    gated, outp, states = _ret_fwd(proj, cos, sin, gn)
    wro = fetch_wro((gated,))
    row = _rows(tp, D_MODEL)
    more, epi = _residual_epi(next_gain)
    h1, hn_next = _mm_rows("ret_out", tp, gated, wro.reshape(RET_HEADS, RET_DV, D_MODEL), 'nn_rows',
                           _residual_outs(t, row, next_gain), extras=[(x, row)] + more, epi=epi)
    return h1, hn_next, (x, hn, proj, gated, outp, states, wro)


def _ret_layer_bwd(dh, saved, norm_g, wri, gn, cos, sin, emit_out, emit_in, deps=()):
    x, hn, proj, gated, outp, states, wro = saved
    t = x.shape[0]
    tm = _row_tile(t)
    tk = _row_tile(t, 512)
    nsh, _, wsh = wri.shape
    tg = _row_tile(t, 512)
    vw = _rows(tg, RET_V_W)
    dout, dgate, dgn = _mm_rows(
        "ret_dgate", tg, dh, wro.reshape(RET_HEADS, RET_DV, D_MODEL), 'nt_rows',
        [(_sds((t, RET_V_W), BF16), vw), (_sds((t, RET_V_W), BF16), vw),
         (_sds((t // tg, 1, RET_V_W), F32), pl.BlockSpec((None, 1, RET_V_W), lambda i, j, k: (i, 0, 0)))],
        extras=[(outp, vw), (proj, pl.BlockSpec((tg, RET_V_W), lambda i, j, k: (i, (RET_IN - RET_V_W) // RET_V_W))),
                (gn.reshape(1, RET_V_W), pl.BlockSpec((1, RET_V_W), lambda i, j, k: (0, 0)))],
        epi=_ret_gate_bwd_epi, deps=deps)
    dgn = jnp.sum(dgn, axis=0)
    (dwro,) = _mm("ret_dwro", (1, 1, t // tk),
                  gated, pl.BlockSpec((tk, RET_V_W), lambda i, j, k: (k, 0)),
                  dh, pl.BlockSpec((tk, D_MODEL), lambda i, j, k: (k, 0)), (0, 0),
                  [(_sds((RET_V_W, D_MODEL), BF16), pl.BlockSpec((RET_V_W, D_MODEL), lambda i, j, k: (0, 0)))])
    dproj = _ret_bwd(proj, cos, sin, states, dout, dgate, deps=emit_out(dwro))
    half = nsh // 2
    (dwri,) = _mm("ret_dwri", (2, 1, t // tk),
                  hn, pl.BlockSpec((tk, D_MODEL), lambda i, j, k: (k, 0)),
                  dproj, pl.BlockSpec((tk, half * wsh), lambda i, j, k: (k, i)), (0, 0),
                  [(_sds((nsh, D_MODEL, wsh), BF16), pl.BlockSpec((half, D_MODEL, wsh), lambda i, j, k: (i, 0, 0)))],
                  split=wsh)
    deps = emit_in(dwri)
    td = _row_tile(t, 256)
    row = _rows(td, D_MODEL)
    dx, dg = _mm_rows("ret_dhn", td, dproj, wri, 'nt_cols', _norm_bwd_outs(t, td),
                      extras=[(x, row), (dh, row), (norm_g, VEC)], epi=_norm_bwd_epi, deps=deps)
    return dx, jnp.sum(dg, axis=0), dgn.reshape(RET_HEADS, RET_DV)


def _mla_layer_fwd(h, hn, wmi, qa, kva, wuq, wukv, gq, gk, wmo, tabs, next_gain):
    t = h.shape[0]
    tm = _row_tile(t)
    row = pl.BlockSpec((tm, D_MODEL), lambda i, j, k: (i, 0))
    (proj2,) = _mm("mla_in", (t // tm, 1, 1), hn, row,
                   wmi, pl.BlockSpec((D_MODEL, MLA_IN_PAD), lambda i, j, k: (0, 0)), (1, 0),
                   [(_sds((t, MLA_IN_PAD), F32), pl.BlockSpec((tm, MLA_IN_PAD), lambda i, j, k: (i, 0)))])
    cq, ckv = _mla_mid(proj2, qa, kva)
    qh, kh, vh = _mla_prep(cq, ckv, wuq, wukv, proj2, gq, gk, tabs)
    o, lse = _attn_fwd(qh, kh, vh)
    more, epi = _residual_epi(next_gain)
    h_out, hn_next = _mm("mla_out", (t // tm, 1, 1), o, row,
                         wmo, pl.BlockSpec((D_MODEL, D_MODEL), lambda i, j, k: (0, 0)), (1, 0),
                         _residual_outs(t, row, next_gain), extras=[(h, row)] + more, epi=epi)
    return h_out, hn_next, (h, hn, proj2, cq, ckv, qh, kh, vh, o, lse)


def _mla_layer_bwd(dh, saved, norm_g, wmi, qa, kva, wuq, wukv, gq, gk, wmo, tabs, deps=()):
    h, hn, proj2, cq, ckv, qh, kh, vh, o, lse = saved
    t = h.shape[0]
    tm = _row_tile(t)
    tk = _row_tile(t, 512)
    row = pl.BlockSpec((tm, D_MODEL), lambda i, j, k: (i, 0))
    rowk = pl.BlockSpec((tk, D_MODEL), lambda i, j, k: (k, 0))
    sq = pl.BlockSpec((D_MODEL, D_MODEL), lambda i, j, k: (0, 0))
    do, dob = _mm("mla_do", (t // tm, 1, 1), dh, row, wmo, sq, (1, 1),
                  [(_sds((t, D_MODEL), F32), row), (_sds((t, D_MODEL), BF16), row)], epi=lambda acc: (acc, acc),
                  deps=deps)
    (dwmo,) = _mm("mla_dwo", (1, 1, t // tk), o, rowk, dh, rowk, (0, 0), [(_sds((D_MODEL, D_MODEL), BF16), sq)])
    delta = _attn_delta(do, o, lse.shape[-1])
    dqt, dkh, dvh = _attn_bwd(qh, kh, vh, dob, lse, delta)
    dq, dkv, dkr, dgq, dgk = _mla_prep_bwd(cq, ckv, wuq, wukv, proj2, gq, gk, tabs, dqt, dkh, dvh)

    wide = MLA_HEADS * MLA_HD_PAD
    widek = pl.BlockSpec((tk, wide), lambda i, j, k: (k, 0))
    (dwuq,) = _mm("mla_dwuq", (1, 1, t // tk),
                  cq, pl.BlockSpec((tk, MLA_Q_RANK), lambda i, j, k: (k, 0)), dq, widek, (0, 0),
                  [(_sds((MLA_HEADS, MLA_Q_RANK, MLA_HD_PAD), BF16),
                    pl.BlockSpec((MLA_HEADS, MLA_Q_RANK, MLA_HD_PAD), lambda i, j, k: (0, 0, 0)))], split=MLA_HD_PAD)
    (dwukv,) = _mm("mla_dwukv", (1, 1, t // tk),
                   ckv, pl.BlockSpec((tk, MLA_KV_RANK), lambda i, j, k: (k, 0)), dkv, widek, (0, 0),
                   [(_sds((MLA_HEADS, MLA_KV_RANK, MLA_HD_PAD), BF16),
                     pl.BlockSpec((MLA_HEADS, MLA_KV_RANK, MLA_HD_PAD), lambda i, j, k: (0, 0, 0)))],
                   split=MLA_HD_PAD)
    side_by_side = lambda wg: wg.transpose(1, 0, 2).reshape(wg.shape[1], wide)
    widei = pl.BlockSpec((tm, wide), lambda i, j, k: (i, 0))
    (dcq,) = _mm("mla_dcq", (t // tm, 1, 1), dq, widei,
                 side_by_side(wuq), pl.BlockSpec((MLA_Q_RANK, wide), lambda i, j, k: (0, 0)), (1, 1),
                 [(_sds((t, MLA_Q_RANK), F32), pl.BlockSpec((tm, MLA_Q_RANK), lambda i, j, k: (i, 0)))])
    (dckv,) = _mm("mla_dckv", (t // tm, 1, 1), dkv, widei,
                  side_by_side(wukv), pl.BlockSpec((MLA_KV_RANK, wide), lambda i, j, k: (0, 0)), (1, 1),
                  [(_sds((t, MLA_KV_RANK), F32), pl.BlockSpec((tm, MLA_KV_RANK), lambda i, j, k: (i, 0)))])
    dproj2, dqa, dkva = _mla_mid_bwd(proj2, qa, kva, dcq, dckv, dkr)
    win = pl.BlockSpec((D_MODEL, MLA_IN_PAD), lambda i, j, k: (0, 0))
    (dwmi,) = _mm("mla_dwin", (1, 1, t // tk), hn, rowk,
                  dproj2, pl.BlockSpec((tk, MLA_IN_PAD), lambda i, j, k: (k, 0)), (0, 0),
                  [(_sds((D_MODEL, MLA_IN_PAD), BF16), win)])
    dh_in, dg = _mm("mla_dhn", (t // tm, 1, 1),
                    dproj2, pl.BlockSpec((tm, MLA_IN_PAD), lambda i, j, k: (i, 0)), wmi, win, (1, 1),
                    _norm_bwd_outs(t, tm), extras=[(h, row), (dh, row), (norm_g, VEC)], epi=_norm_bwd_epi)
    return dh_in, dict(mix=jnp.sum(dg, axis=0), wmi=dwmi, qa=dqa, kva=dkva, wuq=dwuq, wukv=dwukv, gq=dgq, gk=dgk,
                       wmo=dwmo)


def _local_step(x, p, target, w, fetch, emit=lambda group: ()):
    t = x.shape[0]
    inv = 1.0 / (ROPE_THETA ** (jnp.arange(0, RET_DK, 2, dtype=F32) / RET_DK))
    ang = jnp.arange(t, dtype=F32)[:, None] * inv[None, :]
    cos_r, sin_r = jnp.cos(ang), jnp.sin(ang)
    tabs = _mla_tables(t)
    row = lambda a, i: a[i:i + 1]

    h1, hn1, s_ret = _ret_layer_fwd(x, row(w['mix_norm'], 0), w['ret_w_in'],
                                    lambda after: fetch('ret_out', after)['ret_w_out'], w['ret_gn'], cos_r, sin_r,
                                    row(w['mlp_norm'], 0), hn=w.get('hn0'), deps=w['deps'])
    h2, hn2, s_mlp0 = _mlp_fwd(0, h1, hn1, fetch('mlp_w1_0', (h1,))['mlp_w1'],
                               lambda after: fetch('mlp_w2_0', after)['mlp_w2'], row(w['ple_norm'], 0))
    w0 = fetch('ple_0', (h2,))
    h3, hn3, s_ple0 = _ple_fwd(0, h2, hn2, p, w0['ple_gate_w'], w0['ple_proj_w'], row(w['mix_norm'], 1))
    wm = fetch('mla', (h3,))
    mla_w = (wm['mla_w_in'], w['mla_q_a_norm'], w['mla_kv_a_norm'], wm['mla_w_uq'], wm['mla_w_ukv'],
             w['mla_q_norm'], w['mla_k_norm'], wm['mla_w_out'], tabs)
    h4, hn4, s_mla = _mla_layer_fwd(h3, hn3, *mla_w, row(w['mlp_norm'], 1))
    w1 = fetch('layer_1', (h4,))
    h5, hn5, s_mlp1 = _mlp_fwd(1, h4, hn4, w1['mlp_w1'], lambda after: w1['mlp_w2'], row(w['ple_norm'], 1))
    dy, sq_err, s_ple1 = _ple_fwd(1, h5, hn5, p, w1['ple_gate_w'], w1['ple_proj_w'], None, target)

    n = N_DEV
    colsh = lambda a: a.reshape(a.shape[0], n, a.shape[1] // n).transpose(1, 0, 2)
    rowsh = lambda a: a.reshape(n, a.shape[0] // n, a.shape[1])
    big = {}

    def emit_group(group):
        big.update(group)
        return emit(group)

    dh5, dg_ple1, dwg1, dwp1 = _ple_bwd(1, dy, s_ple1, p, row(w['ple_norm'], 1), w1['ple_gate_w'])
    dh4, dg_mlp1, dw1_1, dw2_1 = _mlp_bwd(1, dh5, s_mlp1, row(w['mlp_norm'], 1))
    deps = emit_group({('ple_gate_w', 1): rowsh(dwg1), ('ple_proj_w', 1): colsh(dwp1),
                       ('mlp_w2', 1): dw2_1, ('mlp_w1', 1): dw1_1})
    dh3, gm = _mla_layer_bwd(dh4, s_mla, row(w['mix_norm'], 1), *mla_w, deps=deps)
    deps = emit_group({('mla_w_out', 0): rowsh(gm['wmo']), ('mla_w_uq', 0): _gather_rope(gm['wuq']),
                       ('mla_w_ukv', 0): gm['wukv'], ('mla_w_in', 0): rowsh(_gather_rope(gm['wmi']))})
    dh2, dg_ple0, dwg0, dwp0 = _ple_bwd(0, dh3, s_ple0, p, row(w['ple_norm'], 0), w0['ple_gate_w'], deps=deps)
    dh1, dg_mlp0, dw1_0, dw2_0 = _mlp_bwd(0, dh2, s_mlp0, row(w['mlp_norm'], 0))
    deps = emit_group({('ple_gate_w', 0): rowsh(dwg0), ('ple_proj_w', 0): colsh(dwp0),
                       ('mlp_w2', 0): dw2_0, ('mlp_w1', 0): dw1_0})
    dx, dg_mix0, dgn = _ret_layer_bwd(
        dh1, s_ret, row(w['mix_norm'], 0), w['ret_w_in'], w['ret_gn'], cos_r, sin_r,
        lambda dwro: emit_group({('ret_w_out', 0): rowsh(dwro)}),
        lambda dwri: emit_group({('ret_w_in', 0): dwri}), deps=deps)

    small = dict(
        mix_norm=[dg_mix0, gm['mix']], mlp_norm=[dg_mlp0, dg_mlp1], ple_norm=[dg_ple0, dg_ple1],
        ret_gn=dgn, mla_q_a_norm=gm['qa'], mla_kv_a_norm=gm['kva'], mla_q_norm=gm['gq'], mla_k_norm=gm['gk'],
    )
    return sq_err, dx, big, small


def _my_place():
    x, y, c = lax.axis_index("x"), lax.axis_index("y"), lax.axis_index("c")
    return x, y, c


def _flat(px, py, pc):
    return 4 * px + 2 * py + pc


def _peer(x, y, c, r):
    return (1 - x if r & 4 else x, 1 - y if r & 2 else y, 1 - c if r & 1 else c)


HBM = pl.BlockSpec(memory_space=pltpu.HBM)
SEMS = pl.BlockSpec(memory_space=pltpu.SEMAPHORE)
SIDE_EFFECT = pltpu.SideEffectType.DATAFLOW_SIDE_EFFECTING


def _rs_copies(x, y, c, srcs, lands, send_sems, recv_sems):
    copies = []
    for a in range(len(srcs)):
        for r in range(1, N_DEV):
            peer = _peer(x, y, c, r)
            k = a * (N_DEV - 1) + r - 1
            copies.append(pltpu.make_async_remote_copy(
                src_ref=srcs[a].at[_flat(*peer)], dst_ref=lands[a].at[r - 1],
                send_sem=send_sems.at[k], recv_sem=recv_sems.at[k], device_id=peer, device_id_type=MESH))
    return copies


def _rs_start(name, arrays):
    n = len(arrays)
    hbm = lambda a: pltpu.with_memory_space_constraint(a, pltpu.HBM)
    lands = [hbm(lax.empty((N_DEV - 1,) + a.shape[1:], a.dtype)) for a in arrays]

    def body(*refs):
        srcs, lnd = refs[:n], refs[n:2 * n]
        send_sems, recv_sems = refs[2 * n], refs[2 * n + 1]
        token = refs[-1]
        for cp in _rs_copies(*_my_place(), srcs, lnd, send_sems, recv_sems):
            cp.start()
        token[...] = jnp.zeros_like(token)

    outs = pl.pallas_call(
        body, name=name,
        in_specs=[HBM] * (2 * n),
        out_specs=[SEMS, SEMS] + [HBM] * (2 * n) + [pl.BlockSpec(memory_space=pltpu.VMEM)],
        out_shape=[pltpu.SemaphoreType.DMA((n * (N_DEV - 1),)), pltpu.SemaphoreType.DMA((n * (N_DEV - 1),))]
        + [pltpu.HBM(a.shape, a.dtype) for a in arrays] + [pltpu.HBM(l.shape, l.dtype) for l in lands]
        + [_sds((8, 128), F32)],
        input_output_aliases={i: 2 + i for i in range(2 * n)},
        compiler_params=pltpu.CompilerParams(has_side_effects=SIDE_EFFECT),
    )(*[hbm(a) for a in arrays], *lands)
    return outs[0], outs[1], outs[2:2 + n], outs[2 + n:2 + 2 * n], outs[-1]


def _rs_wait(name, send_sems, recv_sems, srcs, lands, after):
    n = len(srcs)

    def body(*refs):
        src_refs, lnd = refs[:n], refs[n:2 * n]
        send, recv = refs[2 * n], refs[2 * n + 1]
        for cp in _rs_copies(*_my_place(), src_refs, lnd, send, recv):
            cp.wait_send()
            cp.wait_recv()

    outs = pl.pallas_call(
        body, name=name,
        in_specs=[HBM] * (2 * n) + [SEMS, SEMS] + [ANY] * len(after),
        out_specs=[HBM] * (2 * n),
        out_shape=[pltpu.HBM(a.shape, a.dtype) for a in list(srcs) + list(lands)],
        input_output_aliases={i: i for i in range(2 * n)},
        compiler_params=pltpu.CompilerParams(has_side_effects=SIDE_EFFECT),
    )(*srcs, *lands, send_sems, recv_sems, *after)
    return outs[:n], outs[n:]


SMALL_PACK_ROWS = 16


def _all_reduce_small(rows, deps=()):
    n = len(rows)

    def body(*refs):
        ins = refs[:n]
        out_ref, mine, buf, send_sems, recv_sems = refs[n + len(deps):]
        x, y, c = _my_place()
        mine[...] = jnp.zeros_like(mine)
        for (r0, a), ref in zip(rows, ins):
            mine[r0:r0 + a.shape[0], 0:a.shape[1]] = ref[...]
        buf[_flat(x, y, c)] = mine[...]
        copies = []
        for r in range(1, N_DEV):
            peer = _peer(x, y, c, r)
            send = pltpu.make_async_remote_copy(
                src_ref=mine, dst_ref=buf.at[_flat(x, y, c)],
                send_sem=send_sems.at[r - 1], recv_sem=recv_sems.at[r - 1], device_id=peer, device_id_type=MESH)
            send.start()
            recv = pltpu.make_async_remote_copy(
                src_ref=mine, dst_ref=buf.at[_flat(*peer)],
                send_sem=send_sems.at[r - 1], recv_sem=recv_sems.at[r - 1], device_id=peer, device_id_type=MESH)
            copies.append((send, recv))
        for send, recv in copies:
            send.wait_send()
            recv.wait_recv()
        acc = buf[0]
        for s in range(1, N_DEV):
            acc = acc + buf[s]
        out_ref[...] = acc

    vm = pl.BlockSpec(memory_space=pltpu.VMEM)
    shape = (SMALL_PACK_ROWS, D_MODEL)
    return pl.pallas_call(
        body, name="all_reduce_small", in_specs=[vm] * n + [ANY] * len(deps), out_specs=vm,
        out_shape=_sds(shape, F32),
        scratch_shapes=[pltpu.VMEM(shape, F32), pltpu.VMEM((N_DEV,) + shape, F32),
                        pltpu.SemaphoreType.DMA((7,)), pltpu.SemaphoreType.DMA((7,))],
    )(*[a for _, a in rows], *deps)


def _adamw_math(w, g, m, v):
    m = ADAM_B1 * m + (1.0 - ADAM_B1) * g
    v = ADAM_B2 * v + (1.0 - ADAM_B2) * (g * g)
    m_hat = m / (1.0 - ADAM_B1 ** ADAM_STEP)
    v_hat = v / (1.0 - ADAM_B2 ** ADAM_STEP)
    delta = -ADAM_LR * (m_hat / (jnp.sqrt(v_hat) + ADAM_EPS) + ADAM_WD * w)
    return delta, m, v


def _adamw_big(name, w, m, v, srcs, lands, me):
    nl, rows, cols = w.shape
    tr = next(cand for cand in (256, 128, 64, 32, 16, 8) if rows % cand == 0)

    def body(me_ref, w_ref, m_ref, v_ref, *rest):
        src_refs, land_refs = rest[:nl], rest[nl:2 * nl]
        g_ref, d_ref, mo_ref, vo_ref = rest[2 * nl:]
        for layer in range(nl):
            @pl.when(pl.program_id(0) == layer)
            def _():
                g = src_refs[layer][...].astype(F32)
                for s in range(N_DEV - 1):
                    g = g + land_refs[layer][s].astype(F32)
                delta, mn, vn = _adamw_math(w_ref[...], g, m_ref[...], v_ref[...])
                g_ref[...] = g
                d_ref[...] = delta
                mo_ref[...] = mn
                vo_ref[...] = vn

    blk = pl.BlockSpec((None, tr, cols), lambda l, i, me_ref: (l, i, 0))
    at = lambda layer, l, i: jnp.where(l == layer, i, 0)
    own = [pl.BlockSpec((None, tr, cols), functools.partial(lambda layer, l, i, me_ref: (me_ref[0], at(layer, l, i), 0),
                                                            layer)) for layer in range(nl)]
    peers = [pl.BlockSpec((N_DEV - 1, tr, cols), functools.partial(lambda layer, l, i, me_ref: (0, at(layer, l, i), 0),
                                                                   layer)) for layer in range(nl)]
    return pl.pallas_call(
        body, name=name,
        grid_spec=pltpu.PrefetchScalarGridSpec(
            num_scalar_prefetch=1, grid=(nl, rows // tr),
            in_specs=[blk, blk, blk] + own + peers, out_specs=[blk] * 4),
        out_shape=[_sds((nl, rows, cols), F32)] * 4,
        compiler_params=_cparams(("arbitrary", "arbitrary")),
    )(me, w, m, v, *srcs, *lands)


def _adamw_small(ws, gs, ms, vs):
    n = len(ws)

    def body(*refs):
        w_refs, g_refs, m_refs, v_refs = (refs[i * n:(i + 1) * n] for i in range(4))
        d_out, m_out, v_out = (refs[(4 + i) * n:(5 + i) * n] for i in range(3))
        for i in range(n):
            delta, mn, vn = _adamw_math(w_refs[i][...], g_refs[i][...], m_refs[i][...], v_refs[i][...])
            d_out[i][...] = delta
            m_out[i][...] = mn
            v_out[i][...] = vn

    vm = pl.BlockSpec(memory_space=pltpu.VMEM)
    outs = pl.pallas_call(
        body, name="adamw_small", in_specs=[vm] * (4 * n), out_specs=[vm] * (3 * n),
        out_shape=[_sds(a.shape, F32) for a in ws] * 3,
    )(*ws, *gs, *ms, *vs)
    return outs[:n], outs[n:2 * n], outs[2 * n:]


def _pad_to(a, rows, cols):
    return jnp.pad(a, ((0, rows - a.shape[0]), (0, cols - a.shape[1])))


def _place_own(blocks):
    me = _flat(*_my_place())
    return [lax.dynamic_update_slice(lax.empty((N_DEV,) + b.shape, b.dtype), b[None], (me,) + (0,) * b.ndim)
            for b in blocks]


def _ag_copies(x, y, c, blocks, bufs, send_sems, recv_sems, arriving):
    copies = []
    for a in range(len(blocks)):
        for r in range(1, N_DEV):
            peer = _peer(x, y, c, r)
            k = a * (N_DEV - 1) + r - 1
            copies.append(pltpu.make_async_remote_copy(
                src_ref=blocks[a], dst_ref=bufs[a].at[_flat(*(peer if arriving else (x, y, c)))],
                send_sem=send_sems.at[k], recv_sem=recv_sems.at[k], device_id=peer, device_id_type=MESH))
    return copies


def _ag_start(groups, after):
    flat = [pair for g in groups for pair in g]
    n, ng = len(flat), len(groups)
    hbm = lambda a: pltpu.with_memory_space_constraint(a, pltpu.HBM)

    def body(*refs):
        blocks, bufs = refs[:n], refs[n:2 * n]
        sems = refs[2 * n + len(after):2 * n + len(after) + 2 * ng]
        x, y, c = _my_place()
        at = 0
        for gi, g in enumerate(groups):
            for cp in _ag_copies(x, y, c, blocks[at:at + len(g)], bufs[at:at + len(g)], sems[2 * gi],
                                 sems[2 * gi + 1], arriving=False):
                cp.start()
            at += len(g)
        refs[-1][...] = jnp.zeros_like(refs[-1])

    sem_shapes = [pltpu.SemaphoreType.DMA((len(g) * (N_DEV - 1),)) for g in groups for _ in range(2)]
    outs = pl.pallas_call(
        body, name="gather_start",
        in_specs=[HBM] * (2 * n) + [ANY] * len(after),
        out_specs=[SEMS] * (2 * ng) + [HBM] * (2 * n) + [pl.BlockSpec(memory_space=pltpu.VMEM)],
        out_shape=sem_shapes + [pltpu.HBM(b.shape, b.dtype) for b, _ in flat]
        + [pltpu.HBM(u.shape, u.dtype) for _, u in flat] + [_sds((8, 128), F32)],
        input_output_aliases={i: 2 * ng + i for i in range(2 * n)},
        compiler_params=pltpu.CompilerParams(has_side_effects=SIDE_EFFECT),
    )(*[hbm(b) for b, _ in flat], *[hbm(u) for _, u in flat], *after)
    blocks_thru, bufs_thru = outs[2 * ng:2 * ng + n], outs[2 * ng + n:2 * ng + 2 * n]
    started, at = [], 0
    for gi, g in enumerate(groups):
        started.append((outs[2 * gi], outs[2 * gi + 1], blocks_thru[at:at + len(g)], bufs_thru[at:at + len(g)]))
        at += len(g)
    return started, outs[-1]


def _ag_wait(name, send_sems, recv_sems, blocks, bufs, after):
    n = len(blocks)

    def body(*refs):
        for cp in _ag_copies(*_my_place(), refs[:n], refs[n:2 * n], refs[2 * n], refs[2 * n + 1], arriving=True):
            cp.wait_send()
            cp.wait_recv()

    outs = pl.pallas_call(
        body, name=name,
        in_specs=[HBM] * (2 * n) + [SEMS, SEMS] + [ANY] * len(after),
        out_specs=[HBM] * (2 * n),
        out_shape=[pltpu.HBM(a.shape, a.dtype) for a in list(blocks) + list(bufs)],
        input_output_aliases={i: i for i in range(2 * n)},
        compiler_params=pltpu.CompilerParams(has_side_effects=SIDE_EFFECT),
    )(*blocks, *bufs, send_sems, recv_sems, *after)
    return outs[n:]


def _split_call(name, body, thru, sems_in, new_sems, after):
    n, ns, nn = len(thru), len(sems_in), len(new_sems)
    hbm = lambda a: pltpu.with_memory_space_constraint(a, pltpu.HBM)

    def wrapped(*refs):
        body(refs[:n], refs[n:n + ns], refs[n + ns + len(after):n + ns + len(after) + nn])
        refs[-1][...] = jnp.zeros_like(refs[-1])

    outs = pl.pallas_call(
        wrapped, name=name,
        in_specs=[HBM] * n + [SEMS] * ns + [ANY] * len(after),
        out_specs=[SEMS] * nn + [HBM] * n + [pl.BlockSpec(memory_space=pltpu.VMEM)],
        out_shape=[pltpu.SemaphoreType.DMA((k,)) for k in new_sems] + [pltpu.HBM(a.shape, a.dtype) for a in thru]
        + [_sds((8, 128), F32)],
        input_output_aliases={i: nn + i for i in range(n)},
        compiler_params=pltpu.CompilerParams(has_side_effects=SIDE_EFFECT),
    )(*[hbm(a) for a in thru], *sems_in, *after)
    return list(outs[:nn]), list(outs[nn:nn + n]), outs[-1]


def _first_gather(blocks, bufs, overlap):
    n = len(blocks)

    def copies(refs, s1, r1, s2, r2):
        x, y, c = _my_place()
        me, sibling = (x, y, c), (x, y, 1 - c)
        chips = [(1 - x, y), (x, 1 - y), (1 - x, 1 - y)]
        blk, buf = refs[:n], refs[n:]
        out = dict(send1=[], recv1_sib=[], recv1_ici=[], send2=[], recv2=[])
        for a in range(n):
            place = lambda dev: buf[a].at[_flat(*dev)]
            for k, to in enumerate([sibling] + [(*chip, c) for chip in chips]):
                mk = lambda dst: pltpu.make_async_remote_copy(
                    src_ref=blk[a], dst_ref=dst, send_sem=s1.at[4 * a + k], recv_sem=r1.at[4 * a + k],
                    device_id=to, device_id_type=MESH)
                out['send1'].append(mk(place(me)))
                out['recv1_sib' if k == 0 else 'recv1_ici'].append(mk(place(to)))
            for j, chip in enumerate(chips):
                mk = lambda dev: pltpu.make_async_remote_copy(
                    src_ref=place(dev), dst_ref=place(dev), send_sem=s2.at[3 * a + j], recv_sem=r2.at[3 * a + j],
                    device_id=sibling, device_id_type=MESH)
                out['send2'].append(mk((*chip, c)))
                out['recv2'].append(mk((*chip, 1 - c)))
        return out

    def start(refs, sems_in, new):
        for cp in copies(refs, new[0], new[1], new[0], new[1])['send1']:
            cp.start()

    def forward(refs, sems_in, new):
        cps = copies(refs, sems_in[0], sems_in[1], new[0], new[1])
        for cp in cps['recv1_ici']:
            cp.wait_recv()
        for cp in cps['send2']:
            cp.start()

    def finish(refs, sems_in, new):
        cps = copies(refs, *sems_in)
        for cp in cps['recv1_sib'] + cps['recv2']:
            cp.wait_recv()
        for cp in cps['send1'] + cps['send2']:
            cp.wait_send()

    sems1, thru, token = _split_call("first_gather_start", start, list(blocks) + list(bufs), [], [4 * n, 4 * n], ())
    after = overlap(token)
    sems2, thru, token = _split_call("first_gather_forward", forward, thru, sems1, [3 * n, 3 * n], after)
    _, thru, _ = _split_call("first_gather_wait", finish, thru, sems1 + sems2, [], ())
    return thru[n:], token


def _prepare_weights(p, x):
    n = N_DEV
    bf = lambda a: a.astype(BF16)
    gn_pack = jnp.concatenate([
        _pad_to(p['ret_gn'][0], RET_HEADS, 128), _pad_to(p['mla_q_a_norm'], 1, 128),
        _pad_to(p['mla_kv_a_norm'], 1, 128), jnp.zeros((2, 128), F32)], axis=0)
    ple = lambda l: [bf(p['ple_gate_w'][l]), bf(p['ple_proj_w'][l])]
    names = ('ret_out', 'mlp_w1_0', 'mlp_w2_0', 'ple_0', 'mla', 'layer_1')
    later = [[bf(p['ret_w_out'][0])], [bf(p['mlp_w1'][0])], [bf(p['mlp_w2'][0])], ple(0),
             [bf(p['mla_w_in'][0]), bf(p['mla_w_uq'][0]), bf(p['mla_w_ukv'][0]), bf(p['mla_w_out'][0])],
             [bf(p['mlp_w1'][1]), bf(p['mlp_w2'][1])] + ple(1)]
    first = [gn_pack, bf(p['ret_w_in'][0])]
    behind = {}

    def overlap(token):
        behind['hn0'] = _rms_fwd("mix_norm0", x, p['mix_norm'][0:1], deps=(token,))
        behind['bufs'] = _place_own([b for g in later for b in g])
        return (behind['hn0'], *behind['bufs'])

    (pack, wri), token = _first_gather(first, _place_own(first), overlap)
    bufs = behind['bufs']
    groups, at = [], 0
    for g in later:
        groups.append(list(zip(g, bufs[at:at + len(g)])))
        at += len(g)
    started, token = _ag_start(groups, (token,))

    w = {k: p[k] for k in ('mix_norm', 'mlp_norm', 'ple_norm')}
    w['hn0'] = behind['hn0']
    w['ret_gn'] = pack[:, :RET_HEADS, :RET_DV // n].transpose(1, 0, 2).reshape(RET_HEADS, RET_DV)
    w['mla_q_a_norm'] = pack[:, RET_HEADS, :MLA_Q_RANK // n].reshape(1, MLA_Q_RANK)
    w['mla_kv_a_norm'] = pack[:, RET_HEADS + 1, :MLA_KV_RANK // n].reshape(1, MLA_KV_RANK)
    w['ret_w_in'] = wri
    w['mla_q_norm'] = _spread_rope(p['mla_q_norm'])
    w['mla_k_norm'] = _spread_rope(p['mla_k_norm'])
    w['deps'] = (token,)

    def fetch(name, after):
        got = list(_ag_wait("gather_wait_" + name, *started[names.index(name)], after))
        if name == 'ret_out':
            return dict(ret_w_out=got[0].reshape(RET_V_W, D_MODEL))
        if name == 'mla':
            wmi, wuq, wukv, wmo = got
            return dict(mla_w_in=_spread_rope(wmi.reshape(D_MODEL, MLA_IN)), mla_w_uq=_spread_rope(wuq),
                        mla_w_ukv=wukv, mla_w_out=wmo.reshape(D_MODEL, D_MODEL))
        out = {}
        if name in ('mlp_w1_0', 'layer_1'):
            out['mlp_w1'] = got.pop(0)
        if name in ('mlp_w2_0', 'layer_1'):
            out['mlp_w2'] = got.pop(0)
        if name in ('ple_0', 'layer_1'):
            out['ple_gate_w'] = got[0].reshape(D_MODEL, D_MODEL)
            out['ple_proj_w'] = got[1].transpose(1, 0, 2).reshape(PLE_DIM, D_MODEL)
        return out

    return w, fetch


def _small_grads(small, after):
    rows = [(0, small['mix_norm'][0]), (1, small['mix_norm'][1]), (2, small['mlp_norm'][0]),
            (3, small['mlp_norm'][1]), (4, small['ple_norm'][0]), (5, small['ple_norm'][1]),
            (6, small['ret_gn']), (10, small['mla_q_a_norm']), (11, small['mla_kv_a_norm']),
            (12, small['mla_q_norm']), (13, small['mla_k_norm']), (14, small['sq_err'])]
    gs = _all_reduce_small(rows, after)
    me = _flat(*_my_place())
    n = N_DEV
    return dict(
        sq_err=gs[14, 0],
        mix_norm=gs[0:2], mlp_norm=gs[2:4], ple_norm=gs[4:6],
        ret_gn=lax.dynamic_slice(gs, (6, me * (RET_DV // n)), (RET_HEADS, RET_DV // n)),
        mla_q_a_norm=lax.dynamic_slice(gs, (10, me * (MLA_Q_RANK // n)), (1, MLA_Q_RANK // n)),
        mla_kv_a_norm=lax.dynamic_slice(gs, (11, me * (MLA_KV_RANK // n)), (1, MLA_KV_RANK // n)),
        mla_q_norm=_gather_rope(gs[12:13, :MLA_HD_PAD]), mla_k_norm=_gather_rope(gs[13:14, :MLA_HD_PAD]))


def kernel(x, p, mix_norm, ret_w_in, ret_gn, ret_w_out, mla_w_in, mla_q_a_norm, mla_kv_a_norm, mla_w_uq, mla_w_ukv, mla_q_norm, mla_k_norm, mla_w_out, mlp_norm, mlp_w1, mlp_w2, ple_norm, ple_gate_w, ple_proj_w, loss_target, m_mix_norm, m_ret_w_in, m_ret_gn, m_ret_w_out, m_mla_w_in, m_mla_q_a_norm, m_mla_kv_a_norm, m_mla_w_uq, m_mla_w_ukv, m_mla_q_norm, m_mla_k_norm, m_mla_w_out, m_mlp_norm, m_mlp_w1, m_mlp_w2, m_ple_norm, m_ple_gate_w, m_ple_proj_w, v_mix_norm, v_ret_w_in, v_ret_gn, v_ret_w_out, v_mla_w_in, v_mla_q_a_norm, v_mla_kv_a_norm, v_mla_w_uq, v_mla_w_ukv, v_mla_q_norm, v_mla_k_norm, v_mla_w_out, v_mlp_norm, v_mlp_w1, v_mlp_w2, v_ple_norm, v_ple_gate_w, v_ple_proj_w):
    given = dict(locals())
    params = {n: given[n] for n in WEIGHTS}
    w, fetch = _prepare_weights(params, x[0])

    started = []

    def emit(group):
        keys = list(group)
        send, recv, srcs, lands, token = _rs_start(f"rs_start{len(started)}", [group[k] for k in keys])
        started.append((keys, send, recv, srcs, lands))
        return (token,)

    sq_err, grad_x, _, small = _local_step(x[0], p, loss_target[0], w, fetch, emit)
    small['sq_err'] = sq_err[0:1]

    grads, deltas, new_m, new_v = {}, {}, {}, {}
    total = {}

    def small_updates(after):
        sg = _small_grads(small, after)
        total['loss'] = 0.5 / D_MODEL * sg['sq_err']
        two_d = lambda a: a.reshape(-1, a.shape[-1])
        d_s, m_s, v_s = _adamw_small(
            [two_d(params[n]) for n in SMALL], [sg[n] for n in SMALL],
            [two_d(given["m_" + n]) for n in SMALL], [two_d(given["v_" + n]) for n in SMALL])
        for i, n in enumerate(SMALL):
            shape = params[n].shape
            grads[n], deltas[n], new_m[n], new_v[n] = (a.reshape(shape) for a in (sg[n], d_s[i], m_s[i], v_s[i]))
        return (d_s[0],)

    me = _flat(*_my_place()).astype(jnp.int32).reshape(1)
    after = (grad_x,)
    src_of, land_of = {}, {}
    for gi, (keys, send, recv, srcs, lands) in enumerate(started):
        if gi == len(started) - 1:
            after = small_updates(after)
        srcs, lands = _rs_wait(f"rs_wait{gi}", send, recv, srcs, lands, after)
        for k, s, l in zip(keys, srcs, lands):
            src_of[k], land_of[k] = s, l
        done = [n for n in BIG if n not in grads and all((n, l) in src_of for l in range(params[n].shape[0]))]
        for n in done:
            layers = range(params[n].shape[0])
            grads[n], deltas[n], new_m[n], new_v[n] = _adamw_big(
                "adamw_" + n, params[n], given["m_" + n], given["v_" + n],
                [src_of[(n, l)] for l in layers], [land_of[(n, l)] for l in layers], me)
        if done:
            after = tuple(deltas[n] for n in done)

    return (total['loss'], grad_x[None], *[grads[n] for n in WEIGHTS], *[deltas[n] for n in WEIGHTS],
            *[new_m[n] for n in WEIGHTS], *[new_v[n] for n in WEIGHTS])
```

```python
import functools

import jax
import jax.numpy as jnp
from jax import lax
from jax.experimental import pallas as pl
from jax.experimental.pallas import tpu as pltpu

F32 = jnp.float32
BF16 = jnp.bfloat16
MESH = pl.DeviceIdType.MESH
ANY = pl.BlockSpec(memory_space=pl.ANY)

N_DEV = 8
D_MODEL = 1024
CHUNK = 64
RET_BLOCK = 2 * CHUNK
EPS = 1e-6
ROPE_THETA = 10000.0
RET_HEADS = 4
RET_DK = 256
RET_DV = 512
RET_QK_W = RET_HEADS * RET_DK
RET_V_W = RET_HEADS * RET_DV
RET_IN = 2 * RET_QK_W + 2 * RET_V_W
MLA_HEADS = 8
MLA_NOPE = 128
MLA_ROPE = 64
MLA_QKD = MLA_NOPE + MLA_ROPE
MLA_VD = 128
MLA_Q_RANK = 384
MLA_KV_RANK = 256
MLA_IN = MLA_Q_RANK + MLA_KV_RANK + MLA_ROPE
MLA_IN_PAD = 768
MLA_HD_PAD = 256
D_FF = 4096
PLE_DIM = 256
ATT_SCALE = MLA_QKD ** -0.5
LOG2E = 1.4426950408889634
ATT_EXP2 = ATT_SCALE * LOG2E

ADAM_LR = 0.001
ADAM_B1 = 0.9
ADAM_B2 = 0.999
ADAM_EPS = 1e-08
ADAM_WD = 0.01
ADAM_STEP = 10

VMEM_LIMIT = 52 * 1024 * 1024
ROW_TILE = 1024
RET_ROWS = 256
ATT_BLOCK = 256
ATT_QROWS = 1024
ATT_KROWS = 1024
ATT_HEADS = 2
PREP_ROWS = 1024

WEIGHTS = ['mix_norm', 'ret_w_in', 'ret_gn', 'ret_w_out', 'mla_w_in', 'mla_q_a_norm', 'mla_kv_a_norm',
           'mla_w_uq', 'mla_w_ukv', 'mla_q_norm', 'mla_k_norm', 'mla_w_out', 'mlp_norm', 'mlp_w1', 'mlp_w2',
           'ple_norm', 'ple_gate_w', 'ple_proj_w']
BIG = ['ret_w_in', 'ret_w_out', 'mla_w_in', 'mla_w_uq', 'mla_w_ukv', 'mla_w_out', 'mlp_w1', 'mlp_w2',
       'ple_gate_w', 'ple_proj_w']
SMALL = [w for w in WEIGHTS if w not in BIG]


def _cparams(sem=None):
    return pltpu.CompilerParams(dimension_semantics=sem, vmem_limit_bytes=VMEM_LIMIT)


def _dot(a, b, ca, cb):
    return lax.dot_general(a, b, (((ca,), (cb,)), ((), ())), preferred_element_type=F32)


def _bf(v):
    return v if v.dtype == BF16 else v.astype(BF16)


def _sigmoid(z):
    return 1.0 / (1.0 + jnp.exp(-z))


def _mm(name, grid, a, a_spec, b, b_spec, contract, outs, extras=(), epi=None, deps=(), split=None):
    nk = grid[2]
    n_ex, n_out, n_dep = len(extras), len(outs), len(deps)
    acc_shape = tuple(d for d in outs[0][1].block_shape if d is not None)
    if split is not None:
        acc_shape = (acc_shape[1], acc_shape[0] * split)

    def body(*refs):
        a_ref, b_ref = refs[:2]
        ex_refs = refs[2:2 + n_ex]
        out_refs = refs[2 + n_ex + n_dep:2 + n_ex + n_dep + n_out]

        def product():
            return _dot(_bf(a_ref[...]), _bf(b_ref[...]), contract[0], contract[1])

        def finish(acc):
            if split is not None:
                for j in range(acc_shape[1] // split):
                    out_refs[0][j] = acc[:, j * split:(j + 1) * split].astype(out_refs[0].dtype)
                return
            acc = acc[...]
            res = epi(acc, *[r[...] for r in ex_refs]) if epi is not None else (acc,)
            for o, r in zip(out_refs, res):
                o[...] = r.astype(o.dtype)

        if nk == 1:
            finish(product())
        else:
            acc_ref = refs[-1]
            k = pl.program_id(2)

            @pl.when(k == 0)
            def _():
                acc_ref[...] = jnp.zeros_like(acc_ref)

            acc_ref[...] += product()

            @pl.when(k == nk - 1)
            def _():
                finish(acc_ref)

    return pl.pallas_call(
        body, name=name, grid=grid,
        in_specs=[a_spec, b_spec] + [s for _, s in extras] + [ANY] * n_dep,
        out_specs=[s for _, s in outs],
        out_shape=[s for s, _ in outs],
        scratch_shapes=[pltpu.VMEM(acc_shape, F32)] if nk > 1 else [],
        compiler_params=_cparams(("parallel", "parallel", "arbitrary")),
    )(a, b, *[x for x, _ in extras], *deps)


def _mm_rows(name, tm, a, w, mode, outs, extras=(), epi=None, deps=()):
    n_sh, rows, cols = w.shape
    n_ex, n_out, n_dep = len(extras), len(outs), len(deps)
    by_cols = mode in ('nn_cols', 'nt_rows')
    width = cols if mode == 'nn_cols' else rows

    def body(*refs):
        a_ref, w_ref = refs[:2]
        ex_refs = refs[2:2 + n_ex]
        out_refs = refs[2 + n_ex + n_dep:2 + n_ex + n_dep + n_out]
        if by_cols:
            av = _bf(a_ref[...])
            for s in range(n_sh):
                cs = slice(s * width, (s + 1) * width)
                acc = _dot(av, w_ref[s], 1, 0 if mode == 'nn_cols' else 1)
                res = epi(acc, *[r[:, cs] for r in ex_refs]) if epi is not None else (acc,)
                for o, r in zip(out_refs, res):
                    o[:, cs] = r.astype(o.dtype)
        else:
            chunk = rows if mode == 'nn_rows' else cols
            acc = None
            for s in range(n_sh):
                part = _dot(_bf(a_ref[:, s * chunk:(s + 1) * chunk]), w_ref[s], 1, 0 if mode == 'nn_rows' else 1)
                acc = part if acc is None else acc + part
            res = epi(acc, *[r[...] for r in ex_refs]) if epi is not None else (acc,)
            for o, r in zip(out_refs, res):
                o[...] = r.astype(o.dtype)

    t, ka = a.shape
    return pl.pallas_call(
        body, name=name, grid=(t // tm, 1, 1),
        in_specs=[pl.BlockSpec((tm, ka), lambda i, j, k: (i, 0)),
                  pl.BlockSpec((n_sh, rows, cols), lambda i, j, k: (0, 0, 0))] + [s for _, s in extras] + [ANY] * n_dep,
        out_specs=[s for _, s in outs],
        out_shape=[s for s, _ in outs],
        compiler_params=_cparams(("parallel", "arbitrary", "arbitrary")),
    )(a, w, *[x for x, _ in extras], *deps)


def _sds(shape, dtype):
    return jax.ShapeDtypeStruct(shape, dtype)


def _row_tile(t, cap=ROW_TILE):
    return min(cap, t)


def _rms_fwd(name, x, g, deps=()):
    t, d = x.shape
    tm = _row_tile(t)

    def body(x_ref, g_ref, *rest):
        o_ref = rest[-1]
        xv = x_ref[...]
        r = lax.rsqrt(jnp.mean(xv * xv, axis=-1, keepdims=True) + EPS)
        o_ref[...] = (xv * r * g_ref[...]).astype(o_ref.dtype)

    return pl.pallas_call(
        body, name=name, grid=(t // tm,),
        in_specs=[pl.BlockSpec((tm, d), lambda i: (i, 0)), pl.BlockSpec((1, d), lambda i: (0, 0))] + [ANY] * len(deps),
        out_specs=pl.BlockSpec((tm, d), lambda i: (i, 0)),
        out_shape=_sds((t, d), BF16),
        compiler_params=_cparams(("parallel",)),
    )(x, g, *deps)


def _rms_bwd_rows(dy, xv, g, n):
    r = lax.rsqrt(jnp.sum(xv * xv, axis=-1, keepdims=True) / n + EPS)
    xh = xv * r
    dxh = dy * g
    dx = r * (dxh - xh * (jnp.sum(dxh * xh, axis=-1, keepdims=True) / n))
    return dx, dy * xh


def _ple_gate_bwd(name, dh, gate, e):
    t, d = dh.shape
    tm = _row_tile(t)

    def body(dh_ref, g_ref, e_ref, de_ref, dz_ref):
        dh_v, gt = dh_ref[...], g_ref[...].astype(F32)
        de_ref[...] = (dh_v * gt).astype(BF16)
        dz_ref[...] = (dh_v * e_ref[...].astype(F32) * (gt * (1.0 - gt))).astype(BF16)

    row = pl.BlockSpec((tm, d), lambda i: (i, 0))
    return pl.pallas_call(
        body, name=name, grid=(t // tm,), in_specs=[row, row, row], out_specs=[row, row],
        out_shape=[_sds((t, d), BF16), _sds((t, d), BF16)],
        compiler_params=_cparams(("parallel",)),
    )(dh, gate, e)


def _rope_half(v, cos, sin):
    half = v.shape[-1] // 2
    v1, v2 = v[:, :half], v[:, half:]
    return jnp.concatenate([v1 * cos - v2 * sin, v2 * cos + v1 * sin], axis=-1)


def _ret_consts():
    lg = jnp.log(1.0 - 2.0 ** (-5.0 - jnp.arange(RET_HEADS, dtype=F32)))
    idx = jnp.arange(RET_BLOCK, dtype=F32)
    chunk = jnp.floor(idx / CHUNK)
    dist = idx[:, None] - idx[None, :]
    same = chunk[:, None] == chunk[None, :]
    seen = jnp.where(same, jnp.abs(dist), jnp.where(chunk[None, :] < chunk[:, None], dist, jnp.inf))
    intra = jnp.exp(lg[:, None, None] * seen)
    qdec = jnp.exp(lg[:, None] * (idx + 1.0))
    kdec = jnp.exp(lg[:, None] * (RET_BLOCK - 1.0 - idx))
    cdec = jnp.exp(lg * RET_BLOCK)
    qdec = jnp.broadcast_to(qdec[:, :, None], (RET_HEADS, RET_BLOCK, RET_DK))
    kdec = jnp.broadcast_to(kdec[:, :, None], (RET_HEADS, RET_BLOCK, RET_DK))
    cdec = jnp.broadcast_to(cdec[:, None, None], (RET_HEADS, 1, RET_DV))
    return intra, qdec, kdec, cdec


def _ret_specs(rb, rev_nb=None):
    blk = (lambda i: i) if rev_nb is None else (lambda i: rev_nb - 1 - i)
    full = lambda shape: pl.BlockSpec(shape, lambda i: (0,) * len(shape))
    return dict(
        proj=pl.BlockSpec((rb, RET_IN), lambda i: (blk(i), 0)),
        tab=pl.BlockSpec((rb, RET_DK // 2), lambda i: (blk(i), 0)),
        vw=pl.BlockSpec((rb, RET_V_W), lambda i: (blk(i), 0)),
        st=pl.BlockSpec((rb // RET_BLOCK, RET_HEADS, RET_DK, RET_DV), lambda i: (blk(i), 0, 0, 0)),
        gn=full((RET_HEADS, 1, RET_DV)),
        intra=full((RET_HEADS, RET_BLOCK, RET_BLOCK)),
        dec=full((RET_HEADS, RET_BLOCK, RET_DK)),
        cdec=full((RET_HEADS, 1, RET_DV)),
    )


def _ret_fwd(proj, cos, sin, gn):
    t = proj.shape[0]
    rb = min(RET_ROWS, t)
    cpb = rb // RET_BLOCK
    intra, qdec, kdec, cdec = _ret_consts()
    sp = _ret_specs(rb)

    def body(proj_ref, cos_ref, sin_ref, gn_ref, intra_ref, qd_ref, kd_ref, cd_ref,
             gated_ref, outp_ref, st_ref, s_ref):
        @pl.when(pl.program_id(0) == 0)
        def _():
            s_ref[...] = jnp.zeros_like(s_ref)

        def chunk(c, carry):
            rows = pl.ds(pl.multiple_of(c * RET_BLOCK, RET_BLOCK), RET_BLOCK)
            cs, sn = cos_ref[rows, :], sin_ref[rows, :]
            for h in range(RET_HEADS):
                q = proj_ref[rows, h * RET_DK:(h + 1) * RET_DK].astype(F32)
                k = proj_ref[rows, RET_QK_W + h * RET_DK:RET_QK_W + (h + 1) * RET_DK].astype(F32)
                v = proj_ref[rows, 2 * RET_QK_W + h * RET_DV:2 * RET_QK_W + (h + 1) * RET_DV]
                g = proj_ref[rows, 2 * RET_QK_W + RET_V_W + h * RET_DV:
                             2 * RET_QK_W + RET_V_W + (h + 1) * RET_DV].astype(F32)
                qr = _rope_half(q, cs, sn)
                kr = _rope_half(k, cs, sn) * (RET_DK ** -0.5)
                qb, kb, vb = qr.astype(BF16), kr.astype(BF16), v
                sc = _dot(qb, kb, 1, 1) * intra_ref[h]
                inner = _dot(sc.astype(BF16), vb, 1, 0)
                s_old = s_ref[h]
                sb = s_old.astype(BF16)
                st_ref[c, h] = sb
                cross = _dot((qr * qd_ref[h]).astype(BF16), sb, 1, 0)
                out = inner + cross
                s_ref[h] = s_old * cd_ref[h] + _dot((kr * kd_ref[h]).astype(BF16), vb, 0, 0)
                r = lax.rsqrt(jnp.mean(out * out, axis=-1, keepdims=True) + EPS)
                y = out * r * gn_ref[h]
                cols = slice(h * RET_DV, (h + 1) * RET_DV)
                gated_ref[rows, cols] = (g * _sigmoid(g) * y).astype(BF16)
                outp_ref[rows, cols] = out
            return carry

        lax.fori_loop(0, cpb, chunk, 0)

    return pl.pallas_call(
        body, name="ret_fwd", grid=(t // rb,),
        in_specs=[sp['proj'], sp['tab'], sp['tab'], sp['gn'], sp['intra'], sp['dec'], sp['dec'], sp['cdec']],
        out_specs=[sp['vw'], sp['vw'], sp['st']],
        out_shape=[_sds((t, RET_V_W), BF16), _sds((t, RET_V_W), F32),
                   _sds((t // RET_BLOCK, RET_HEADS, RET_DK, RET_DV), BF16)],
        scratch_shapes=[pltpu.VMEM((RET_HEADS, RET_DK, RET_DV), F32)],
        compiler_params=_cparams(("arbitrary",)),
    )(proj, cos, sin, gn.reshape(RET_HEADS, 1, RET_DV), intra, qdec, kdec, cdec)


def _ret_gate_bwd_epi(dgt, out, g, gn):
    g = g.astype(F32)
    r = lax.rsqrt(jnp.mean(out * out, axis=-1, keepdims=True) + EPS)
    xh = out * r
    sg = _sigmoid(g)
    dgate = dgt * (xh * gn) * (sg * (1.0 + g * (1.0 - sg)))
    dy = dgt * (g * sg)
    dxh = dy * gn
    dout = r * (dxh - xh * jnp.mean(dxh * xh, axis=-1, keepdims=True))
    return dout, dgate, jnp.sum(dy * xh, axis=0, keepdims=True)


def _ret_bwd(proj, cos, sin, states, dout, dgate, deps=()):
    t = proj.shape[0]
    rb = min(RET_ROWS, t)
    cpb = rb // RET_BLOCK
    nb = t // rb
    intra, qdec, kdec, cdec = _ret_consts()
    sp = _ret_specs(rb, rev_nb=nb)

    def body(proj_ref, cos_ref, sin_ref, intra_ref, qd_ref, kd_ref, cd_ref, st_ref, dout_ref, dgate_ref, *rest):
        dproj_ref, ds_ref = rest[len(deps):]

        @pl.when(pl.program_id(0) == 0)
        def _():
            ds_ref[...] = jnp.zeros_like(ds_ref)

        def chunk(cc, carry):
            c = cpb - 1 - cc
            rows = pl.ds(pl.multiple_of(c * RET_BLOCK, RET_BLOCK), RET_BLOCK)
            cs, sn = cos_ref[rows, :], sin_ref[rows, :]
            for h in range(RET_HEADS):
                q = proj_ref[rows, h * RET_DK:(h + 1) * RET_DK].astype(F32)
                k = proj_ref[rows, RET_QK_W + h * RET_DK:RET_QK_W + (h + 1) * RET_DK].astype(F32)
                v = proj_ref[rows, 2 * RET_QK_W + h * RET_DV:2 * RET_QK_W + (h + 1) * RET_DV]
                cols = slice(h * RET_DV, (h + 1) * RET_DV)
                qr = _rope_half(q, cs, sn)
                kr = _rope_half(k, cs, sn) * (RET_DK ** -0.5)
                qb, kb, vb = qr.astype(BF16), kr.astype(BF16), v
                qdb = (qr * qd_ref[h]).astype(BF16)
                kdb = (kr * kd_ref[h]).astype(BF16)
                doutb = dout_ref[rows, cols]
                itr = intra_ref[h]
                pb = (_dot(qb, kb, 1, 1) * itr).astype(BF16)
                dv = _dot(pb, doutb, 0, 0)
                dsc = (_dot(doutb, vb, 1, 1) * itr).astype(BF16)
                dq = _dot(dsc, kb, 1, 0)
                dk = _dot(dsc, qb, 0, 0)
                dq = dq + _dot(doutb, st_ref[c, h], 1, 1) * qd_ref[h]
                ds_new = ds_ref[h]
                dsb = ds_new.astype(BF16)
                dk = dk + _dot(vb, dsb, 1, 1) * kd_ref[h]
                dv = dv + _dot(kdb, dsb, 1, 0)
                ds_ref[h] = ds_new * cd_ref[h] + _dot(qdb, doutb, 0, 0)
                dproj_ref[rows, h * RET_DK:(h + 1) * RET_DK] = _rope_half(dq, cs, -sn).astype(BF16)
                dproj_ref[rows, RET_QK_W + h * RET_DK:RET_QK_W + (h + 1) * RET_DK] = (
                    _rope_half(dk * (RET_DK ** -0.5), cs, -sn).astype(BF16))
                dproj_ref[rows, 2 * RET_QK_W + h * RET_DV:2 * RET_QK_W + (h + 1) * RET_DV] = dv.astype(BF16)
                dproj_ref[rows, 2 * RET_QK_W + RET_V_W + h * RET_DV:
                          2 * RET_QK_W + RET_V_W + (h + 1) * RET_DV] = dgate_ref[rows, cols]
            return carry

        lax.fori_loop(0, cpb, chunk, 0)

    return pl.pallas_call(
        body, name="ret_bwd", grid=(nb,),
        in_specs=[sp['proj'], sp['tab'], sp['tab'], sp['intra'], sp['dec'], sp['dec'], sp['cdec'],
                  sp['st'], sp['vw'], sp['vw']] + [ANY] * len(deps),
        out_specs=sp['proj'],
        out_shape=_sds((t, RET_IN), BF16),
        scratch_shapes=[pltpu.VMEM((RET_HEADS, RET_DK, RET_DV), F32)],
        compiler_params=_cparams(("arbitrary",)),
    )(proj, cos, sin, intra, qdec, kdec, cdec, states, dout, dgate, *deps)


def _spread_rope(a):
    return jnp.pad(a, [(0, 0)] * (a.ndim - 1) + [(0, MLA_ROPE)])


def _gather_rope(a):
    return a[..., :a.shape[-1] - MLA_ROPE]


def _mla_tables(t):
    half = MLA_ROPE // 2
    inv = 1.0 / (ROPE_THETA ** (jnp.arange(0, MLA_ROPE, 2, dtype=F32) / MLA_ROPE))
    ang = jnp.arange(t, dtype=F32)[:, None] * inv[None, :]
    cos, sin = jnp.cos(ang), jnp.sin(ang)
    z = jnp.zeros((t, half), F32)
    c = jnp.concatenate([cos, cos, z, z], axis=1)
    s1 = jnp.concatenate([-sin, z, z, z], axis=1)
    s2 = jnp.concatenate([z, sin, z, z], axis=1)
    return c, s1, s2


def _rope_tile(r, c, s1, s2):
    return r * c + pltpu.roll(r, 96, 1) * s1 + pltpu.roll(r, 32, 1) * s2


def _mla_mid(proj2, qa, kva):
    t = proj2.shape[0]
    tm = _row_tile(t)

    def body(p_ref, qa_ref, kva_ref, cq_ref, ckv_ref):
        cq = p_ref[:, :MLA_Q_RANK]
        ckv = p_ref[:, MLA_Q_RANK:MLA_Q_RANK + MLA_KV_RANK]
        rq = lax.rsqrt(jnp.mean(cq * cq, axis=-1, keepdims=True) + EPS)
        rkv = lax.rsqrt(jnp.mean(ckv * ckv, axis=-1, keepdims=True) + EPS)
        cq_ref[...] = (cq * rq * qa_ref[...]).astype(BF16)
        ckv_ref[...] = (ckv * rkv * kva_ref[...]).astype(BF16)

    return pl.pallas_call(
        body, name="mla_mid", grid=(t // tm,),
        in_specs=[pl.BlockSpec((tm, MLA_IN_PAD), lambda i: (i, 0)),
                  pl.BlockSpec((1, MLA_Q_RANK), lambda i: (0, 0)),
                  pl.BlockSpec((1, MLA_KV_RANK), lambda i: (0, 0))],
        out_specs=[pl.BlockSpec((tm, MLA_Q_RANK), lambda i: (i, 0)),
                   pl.BlockSpec((tm, MLA_KV_RANK), lambda i: (i, 0))],
        out_shape=[_sds((t, MLA_Q_RANK), BF16), _sds((t, MLA_KV_RANK), BF16)],
        compiler_params=_cparams(("parallel",)),
    )(proj2, qa, kva)


def _mla_mid_bwd(proj2, qa, kva, dcq, dckv, dkr):
    t = proj2.shape[0]
    tm = _row_tile(t)

    def body(p_ref, qa_ref, kva_ref, dcq_ref, dckv_ref, dkr_ref, dp_ref, dqa_ref, dkva_ref):
        @pl.when(pl.program_id(0) == 0)
        def _():
            dqa_ref[...] = jnp.zeros_like(dqa_ref)
            dkva_ref[...] = jnp.zeros_like(dkva_ref)

        dxq, dgq = _rms_bwd_rows(dcq_ref[...], p_ref[:, :MLA_Q_RANK], qa_ref[...], MLA_Q_RANK)
        dxk, dgk = _rms_bwd_rows(dckv_ref[...], p_ref[:, MLA_Q_RANK:MLA_Q_RANK + MLA_KV_RANK], kva_ref[...],
                                 MLA_KV_RANK)
        dp_ref[:, :MLA_Q_RANK] = dxq.astype(BF16)
        dp_ref[:, MLA_Q_RANK:MLA_Q_RANK + MLA_KV_RANK] = dxk.astype(BF16)
        dp_ref[:, MLA_Q_RANK + MLA_KV_RANK:] = dkr_ref[...].astype(BF16)
        dqa_ref[...] += jnp.sum(dgq, axis=0, keepdims=True)
        dkva_ref[...] += jnp.sum(dgk, axis=0, keepdims=True)

    return pl.pallas_call(
        body, name="mla_mid_bwd", grid=(t // tm,),
        in_specs=[pl.BlockSpec((tm, MLA_IN_PAD), lambda i: (i, 0)),
                  pl.BlockSpec((1, MLA_Q_RANK), lambda i: (0, 0)),
                  pl.BlockSpec((1, MLA_KV_RANK), lambda i: (0, 0)),
                  pl.BlockSpec((tm, MLA_Q_RANK), lambda i: (i, 0)),
                  pl.BlockSpec((tm, MLA_KV_RANK), lambda i: (i, 0)),
                  pl.BlockSpec((tm, 128), lambda i: (i, 0))],
        out_specs=[pl.BlockSpec((tm, MLA_IN_PAD), lambda i: (i, 0)),
                   pl.BlockSpec((1, MLA_Q_RANK), lambda i: (0, 0)),
                   pl.BlockSpec((1, MLA_KV_RANK), lambda i: (0, 0))],
        out_shape=[_sds((t, MLA_IN_PAD), BF16), _sds((1, MLA_Q_RANK), F32), _sds((1, MLA_KV_RANK), F32)],
        compiler_params=_cparams(("arbitrary",)),
    )(proj2, qa, kva, dcq, dckv, dkr)


def _mla_prep_specs(t, tm):
    head = lambda w: pl.BlockSpec((None, tm, w), lambda i, h: (h, i, 0))
    return dict(
        head256=head(MLA_HD_PAD), head128=head(MLA_VD),
        cols256=pl.BlockSpec((tm, MLA_HD_PAD), lambda i, h: (i, h)),
        cq=pl.BlockSpec((tm, MLA_Q_RANK), lambda i, h: (i, 0)),
        ckv=pl.BlockSpec((tm, MLA_KV_RANK), lambda i, h: (i, 0)),
        wuq=pl.BlockSpec((None, MLA_Q_RANK, MLA_HD_PAD), lambda i, h: (h, 0, 0)),
        wukv=pl.BlockSpec((None, MLA_KV_RANK, MLA_HD_PAD), lambda i, h: (h, 0, 0)),
        kr=pl.BlockSpec((tm, 128), lambda i, h: (i, (MLA_Q_RANK + MLA_KV_RANK) // 128)),
        gain=pl.BlockSpec((1, MLA_HD_PAD), lambda i, h: (0, 0)),
        tab=pl.BlockSpec((tm, 128), lambda i, h: (i, 0)),
    )


def _mla_prep(cq, ckv, wuq, wukv, proj2, gq, gk, tabs):
    t = cq.shape[0]
    tm = _row_tile(t, PREP_ROWS)
    sp = _mla_prep_specs(t, tm)

    def body(cq_ref, ckv_ref, wuq_ref, wukv_ref, kr_ref, gq_ref, gk_ref, c_ref, s1_ref, s2_ref,
             qh_ref, kh_ref, vh_ref):
        c, s1, s2 = c_ref[...], s1_ref[...], s2_ref[...]

        def norm_rope(xv, gain):
            r = lax.rsqrt(jnp.sum(xv * xv, axis=-1, keepdims=True) / MLA_QKD + EPS)
            y = xv * r * gain
            return jnp.concatenate([y[:, :MLA_NOPE], _rope_tile(y[:, MLA_NOPE:], c, s1, s2)], axis=-1)

        kvv = _dot(ckv_ref[...], wukv_ref[...], 1, 0)
        qh_ref[...] = norm_rope(_dot(cq_ref[...], wuq_ref[...], 1, 0), gq_ref[...]).astype(BF16)
        kf = jnp.concatenate([kvv[:, :MLA_NOPE], kr_ref[...]], axis=-1)
        kh_ref[...] = norm_rope(kf, gk_ref[...]).astype(BF16)
        vh_ref[...] = jnp.concatenate([kvv[:, MLA_NOPE:], jnp.ones((tm, MLA_VD), F32)], axis=-1).astype(BF16)

    return pl.pallas_call(
        body, name="mla_prep", grid=(t // tm, MLA_HEADS),
        in_specs=[sp['cq'], sp['ckv'], sp['wuq'], sp['wukv'], sp['kr'], sp['gain'], sp['gain'],
                  sp['tab'], sp['tab'], sp['tab']],
        out_specs=[sp['head256'], sp['head256'], sp['head256']],
        out_shape=[_sds((MLA_HEADS, t, MLA_HD_PAD), BF16), _sds((MLA_HEADS, t, MLA_HD_PAD), BF16),
                   _sds((MLA_HEADS, t, 2 * MLA_VD), BF16)],
        compiler_params=_cparams(("parallel", "arbitrary")),
    )(cq, ckv, wuq, wukv, proj2, gq, gk, *tabs)


def _mla_prep_bwd(cq, ckv, wuq, wukv, proj2, gq, gk, tabs, dqt, dkh, dvh):
    t = cq.shape[0]
    tm = _row_tile(t, PREP_ROWS)
    ab = dqt.shape[-1]
    sp = _mla_prep_specs(t, tm)

    def body(cq_ref, ckv_ref, wuq_ref, wukv_ref, kr_ref, gq_ref, gk_ref, c_ref, s1_ref, s2_ref,
             dqt_ref, dkh_ref, dvh_ref, dq_ref, dkv_ref, dkr_ref, dgq_ref, dgk_ref):
        dqh = jnp.concatenate([dqt_ref[b].T for b in range(tm // ab)], axis=0)
        i, h = pl.program_id(0), pl.program_id(1)

        @pl.when((i == 0) & (h == 0))
        def _():
            dgq_ref[...] = jnp.zeros_like(dgq_ref)
            dgk_ref[...] = jnp.zeros_like(dgk_ref)

        @pl.when(h == 0)
        def _():
            dkr_ref[...] = jnp.zeros_like(dkr_ref)

        c, s1, s2 = c_ref[...], s1_ref[...], s2_ref[...]

        def back(xv, gain, dout):
            dy = jnp.concatenate([dout[:, :MLA_NOPE], _rope_tile(dout[:, MLA_NOPE:], c, -s1, -s2)], axis=-1)
            return _rms_bwd_rows(dy, xv, gain, MLA_QKD)

        kvv = _dot(ckv_ref[...], wukv_ref[...], 1, 0)
        dxq, dgq = back(_dot(cq_ref[...], wuq_ref[...], 1, 0), gq_ref[...], dqh)
        kf = jnp.concatenate([kvv[:, :MLA_NOPE], kr_ref[...]], axis=-1)
        dxk, dgk = back(kf, gk_ref[...], dkh_ref[...])
        dq_ref[...] = dxq.astype(BF16)
        dkv_ref[...] = jnp.concatenate([dxk[:, :MLA_NOPE], dvh_ref[...]], axis=-1).astype(BF16)
        dkr_ref[...] += dxk[:, MLA_NOPE:]
        dgq_ref[...] += jnp.sum(dgq, axis=0, keepdims=True)
        dgk_ref[...] += jnp.sum(dgk, axis=0, keepdims=True)

    return pl.pallas_call(
        body, name="mla_prep_bwd", grid=(t // tm, MLA_HEADS),
        in_specs=[sp['cq'], sp['ckv'], sp['wuq'], sp['wukv'], sp['kr'], sp['gain'], sp['gain'],
                  sp['tab'], sp['tab'], sp['tab'],
                  pl.BlockSpec((None, tm // ab, MLA_HD_PAD, ab), lambda i, h: (h, i, 0, 0)),
                  sp['head256'], sp['head128']],
        out_specs=[sp['cols256'], sp['cols256'], sp['tab'], sp['gain'], sp['gain']],
        out_shape=[_sds((t, MLA_HEADS * MLA_HD_PAD), BF16), _sds((t, MLA_HEADS * MLA_HD_PAD), BF16),
                   _sds((t, 128), F32), _sds((1, MLA_HD_PAD), F32), _sds((1, MLA_HD_PAD), F32)],
        compiler_params=_cparams(("arbitrary", "arbitrary")),
    )(cq, ckv, wuq, wukv, proj2, gq, gk, *tabs, dqt, dkh, dvh)


def _chunk_visible(rows, cols, row_off, col_off):
    rq = lax.shift_right_logical(lax.broadcasted_iota(jnp.int32, (rows, cols), 0) + row_off, 6)
    ck = lax.shift_right_logical(lax.broadcasted_iota(jnp.int32, (rows, cols), 1) + col_off, 6)
    return ck <= rq


def _rows_to_lanes(col):
    return col.T[:8, :]


def _attn_fwd(qh, kh, vh):
    t = qh.shape[1]
    ab = min(ATT_BLOCK, t)
    tq = min(ATT_QROWS, t)
    r = tq // ab
    hg = ATT_HEADS

    def body(q_ref, k_ref, v_ref, o_ref, lse_ref, acc_ref):
        n_un = pl.program_id(1) * r
        acc_ref[...] = jnp.zeros_like(acc_ref)

        def step(b, ms, diag):
            rows = pl.ds(pl.multiple_of(b * ab, ab), ab)
            out = []
            for hh in range(hg):
                m = ms[hh]
                s = _dot(q_ref[hh], k_ref[hh, rows, :], 1, 1)
                if diag is not None:
                    s = jnp.where(_chunk_visible(tq, ab, 0, diag * ab), s, -1e30)
                m_new = jnp.maximum(m, jnp.max(s, axis=-1, keepdims=True))
                p = jnp.exp2((s - m_new) * ATT_EXP2).astype(BF16)
                acc_ref[hh] = jnp.exp2((m - m_new) * ATT_EXP2) * acc_ref[hh] + _dot(p, v_ref[hh, rows, :], 1, 0)
                out.append(m_new)
            return tuple(out)

        ms = tuple(jnp.full((tq, 1), -1e30, F32) for _ in range(hg))
        ms = lax.fori_loop(0, n_un, lambda b, st: step(b, st, None), ms)
        for d in range(r):
            ms = step(n_un + d, ms, d)
        for hh in range(hg):
            l = acc_ref[hh, :, MLA_VD:]
            o_ref[:, hh * MLA_VD:(hh + 1) * MLA_VD] = acc_ref[hh, :, :MLA_VD] / l
            lse_t = _rows_to_lanes(ms[hh] * ATT_EXP2 + jnp.log(l) * LOG2E)
            for d in range(r):
                lse_ref[hh, d] = lse_t[:, d * ab:(d + 1) * ab]

    return pl.pallas_call(
        body, name="mla_attn", grid=(MLA_HEADS // hg, t // tq),
        in_specs=[pl.BlockSpec((hg, tq, MLA_HD_PAD), lambda g, i: (g, i, 0)),
                  pl.BlockSpec((hg, t, MLA_HD_PAD), lambda g, i: (g, 0, 0)),
                  pl.BlockSpec((hg, t, 2 * MLA_VD), lambda g, i: (g, 0, 0))],
        out_specs=[pl.BlockSpec((tq, hg * MLA_VD), lambda g, i: (i, g)),
                   pl.BlockSpec((hg, r, 8, ab), lambda g, i: (g, i, 0, 0))],
        out_shape=[_sds((t, MLA_HEADS * MLA_VD), F32), _sds((MLA_HEADS, t // ab, 8, ab), F32)],
        scratch_shapes=[pltpu.VMEM((hg, tq, 2 * MLA_VD), F32)],
        compiler_params=_cparams(("parallel", "arbitrary")),
    )(qh, kh, vh)


def _attn_delta(do, o, ab):
    t = do.shape[0]
    tm = _row_tile(t)

    def body(do_ref, o_ref, d_ref):
        d = jnp.sum(do_ref[...] * o_ref[...], axis=-1, keepdims=True)
        d_t = _rows_to_lanes(jnp.broadcast_to(d, (tm, 128)))
        for b in range(tm // ab):
            d_ref[b] = d_t[:, b * ab:(b + 1) * ab]

    col = pl.BlockSpec((tm, MLA_VD), lambda i, h: (i, h))
    return pl.pallas_call(
        body, name="mla_delta", grid=(t // tm, MLA_HEADS), in_specs=[col, col],
        out_specs=pl.BlockSpec((None, tm // ab, 8, ab), lambda i, h: (h, i, 0, 0)),
        out_shape=_sds((MLA_HEADS, t // ab, 8, ab), F32),
        compiler_params=_cparams(("parallel", "parallel")),
    )(do, o)


def _attn_bwd(qh, kh, vh, dob, lse_t, dl_t):
    t = qh.shape[1]
    ab = min(ATT_BLOCK, t)
    kb = min(ATT_KROWS, t)
    r = kb // ab
    nq = t // ab
    hg = ATT_HEADS

    def body(q_ref, k_ref, v_ref, do_ref, lse_ref, dl_ref, dqt_ref, dk_ref, dv_ref):
        j = pl.program_id(1)

        @pl.when(j == 0)
        def _():
            dqt_ref[...] = jnp.zeros_like(dqt_ref)

        ks = [k_ref[hh] for hh in range(hg)]
        vs = [v_ref[hh, :, :MLA_VD] for hh in range(hg)]
        kts = [k.T for k in ks]

        dk_ref[...] = jnp.zeros_like(dk_ref)
        dv_ref[...] = jnp.zeros_like(dv_ref)

        def step(b, carry, diag):
            rows = pl.ds(pl.multiple_of(b * ab, ab), ab)
            hi = kb if diag is None else (diag + 1) * ab
            for hh in range(hg):
                q = q_ref[hh, rows, :]
                do = do_ref[rows, hh * MLA_VD:(hh + 1) * MLA_VD]
                s_t = _dot(ks[hh][:hi], q, 1, 1)
                if diag is not None:
                    key_chunk = lax.shift_right_logical(lax.broadcasted_iota(jnp.int32, (hi, ab), 0), 6)
                    query_chunk = lax.shift_right_logical(
                        lax.broadcasted_iota(jnp.int32, (hi, ab), 1) + diag * ab, 6)
                    s_t = jnp.where(key_chunk <= query_chunk, s_t, -1e30)
                p_t = jnp.exp2(s_t * ATT_EXP2 - lse_ref[hh, b][0:1, :])
                dp_t = _dot(vs[hh][:hi], do, 1, 1)
                ds_t = (p_t * (dp_t - dl_ref[hh, b][0:1, :]) * ATT_SCALE).astype(BF16)
                dqt_ref[hh, b] += _dot(kts[hh][:, :hi], ds_t, 1, 0)
                dk_ref[hh, :hi] += _dot(ds_t, q, 1, 0)
                dv_ref[hh, :hi] += _dot(p_t.astype(BF16), do, 1, 0)
            return carry

        for d in range(r):
            step(j * r + d, 0, d)
        lax.fori_loop((j + 1) * r, nq, lambda b, c: step(b, c, None), 0)

    whole = lambda w: pl.BlockSpec((hg, t, w), lambda g, j: (g, 0, 0))
    blk = lambda w: pl.BlockSpec((hg, kb, w), lambda g, j: (g, j, 0))
    stat = pl.BlockSpec((hg, nq, 8, ab), lambda g, j: (g, 0, 0, 0))
    return pl.pallas_call(
        body, name="mla_attn_bwd", grid=(MLA_HEADS // hg, t // kb),
        in_specs=[whole(MLA_HD_PAD), blk(MLA_HD_PAD), blk(2 * MLA_VD),
                  pl.BlockSpec((t, hg * MLA_VD), lambda g, j: (0, g)), stat, stat],
        out_specs=[pl.BlockSpec((hg, nq, MLA_HD_PAD, ab), lambda g, j: (g, 0, 0, 0)), blk(MLA_HD_PAD), blk(MLA_VD)],
        out_shape=[_sds((MLA_HEADS, nq, MLA_HD_PAD, ab), F32), _sds((MLA_HEADS, t, MLA_HD_PAD), F32),
                   _sds((MLA_HEADS, t, MLA_VD), F32)],
        compiler_params=_cparams(("parallel", "arbitrary")),
    )(qh, kh, vh, dob, lse_t, dl_t)


VEC = pl.BlockSpec((1, D_MODEL), lambda i, j, k: (0, 0))


def _rows(tm, width):
    return pl.BlockSpec((tm, width), lambda i, j, k: (i, 0))


def _residual_epi(next_gain):
    if next_gain is None:
        return [], lambda acc, hv: (acc + hv,)

    def epi(acc, hv, g):
        h_new = acc + hv
        r = lax.rsqrt(jnp.mean(h_new * h_new, axis=-1, keepdims=True) + EPS)
        return h_new, h_new * r * g

    return [(next_gain, VEC)], epi


def _residual_outs(t, row, next_gain):
    outs = [(_sds((t, D_MODEL), F32), row)]
    return outs + ([(_sds((t, D_MODEL), BF16), row)] if next_gain is not None else [])


def _mlp_fwd(l, h, hn, w1g, fetch_w2, next_gain):
    t = h.shape[0]
    tm = _row_tile(t, 512)

    def relu2(acc):
        r = jnp.maximum(acc, 0.0)
        return (r * r,)

    (u,) = _mm_rows(f"mlp_up{l}", tm, hn, w1g, 'nn_cols', [(_sds((t, D_FF), BF16), _rows(tm, D_FF))], epi=relu2)
    w2g = fetch_w2((u,))
    row = _rows(tm, D_MODEL)
    more, epi = _residual_epi(next_gain)
    h2, hn_next = _mm_rows(f"mlp_down{l}", tm, u, w2g, 'nn_rows', _residual_outs(t, row, next_gain),
                           extras=[(h, row)] + more, epi=epi)
    return h2, hn_next, (h, hn, u, w1g, w2g)


def _norm_bwd_outs(t, tm):
    return [(_sds((t, D_MODEL), F32), pl.BlockSpec((tm, D_MODEL), lambda i, j, k: (i, 0))),
            (_sds((t // tm, 1, D_MODEL), F32), pl.BlockSpec((None, 1, D_MODEL), lambda i, j, k: (i, 0, 0)))]


def _norm_bwd_epi(acc, xv, res, g):
    dx, dgr = _rms_bwd_rows(acc, xv, g, D_MODEL)
    return res + dx, jnp.sum(dgr, axis=0, keepdims=True)


def _mlp_bwd(l, dh, saved, norm_g):
    h, hn, u, w1g, w2g = saved
    t = h.shape[0]
    tm = _row_tile(t, 512)
    nsh, _, wsh = w1g.shape
    wide = _rows(tm, D_FF)
    (da,) = _mm_rows(f"mlp_du{l}", tm, dh, w2g, 'nt_rows', [(_sds((t, D_FF), BF16), wide)], extras=[(u, wide)],
                     epi=lambda acc, uv: (2.0 * jnp.sqrt(uv.astype(F32)) * acc,))
    tw = _row_tile(t, 512)
    (dw2,) = _mm(f"mlp_dw2{l}", (1, 1, t // tw),
                 u, pl.BlockSpec((tw, D_FF), lambda i, j, k: (k, 0)),
                 dh, pl.BlockSpec((tw, D_MODEL), lambda i, j, k: (k, 0)), (0, 0),
                 [(_sds((D_FF, D_MODEL), BF16), pl.BlockSpec((D_FF, D_MODEL), lambda i, j, k: (0, 0)))])
    dw2 = dw2.reshape(nsh, wsh, D_MODEL)
    (dw1,) = _mm(f"mlp_dw1{l}", (1, 1, t // tw),
                 hn, pl.BlockSpec((tw, D_MODEL), lambda i, j, k: (k, 0)),
                 da, pl.BlockSpec((tw, D_FF), lambda i, j, k: (k, 0)), (0, 0),
                 [(_sds((nsh, D_MODEL, wsh), BF16), pl.BlockSpec((nsh, D_MODEL, wsh), lambda i, j, k: (0, 0, 0)))],
                 split=wsh)
    row = _rows(tm, D_MODEL)
    dh_in, dg = _mm_rows(f"mlp_dhn{l}", tm, da, w1g, 'nt_cols', _norm_bwd_outs(t, tm),
                         extras=[(h, row), (dh, row), (norm_g, VEC)], epi=_norm_bwd_epi)
    return dh_in, jnp.sum(dg, axis=0), dw1, dw2


def _ple_fwd(l, h, hn, p, wg, wp, next_gain, target=None):
    t = h.shape[0]
    tm = _row_tile(t, 512)
    row = pl.BlockSpec((tm, D_MODEL), lambda i, j, k: (i, 0))
    full = lambda r: pl.BlockSpec((r, D_MODEL), lambda i, j, k: (0, 0))
    f32_row, bf_row = (_sds((t, D_MODEL), F32), row), (_sds((t, D_MODEL), BF16), row)
    common = [(h, row), (p, pl.BlockSpec((None, None, tm, PLE_DIM), lambda i, j, k: (l, 0, i, 0))),
              (wp, full(PLE_DIM))]
    if target is not None:
        def loss_epi(acc, hv, pv, wpv, tv):
            gt = _sigmoid(acc)
            ev = _dot(_bf(pv), wpv, 1, 0)
            err = hv + gt * ev - tv
            sq = jnp.sum(jnp.sum(err * err, axis=-1, keepdims=True), axis=0, keepdims=True)
            return err / D_MODEL, gt, ev, jnp.broadcast_to(sq, (8, 128))

        dy, gate, e, sq = _mm(f"ple_gate{l}", (t // tm, 1, 1), hn, row, wg, full(D_MODEL), (1, 0),
                              [f32_row, bf_row, bf_row, (_sds((t // tm, 8, 128), F32),
                                                         pl.BlockSpec((None, 8, 128), lambda i, j, k: (i, 0, 0)))],
                              extras=common + [(target, row)], epi=loss_epi)
        return dy, jnp.sum(sq, axis=0), (h, hn, gate, e)

    def gate_epi(acc, hv, pv, wpv, *gain):
        gt = _sigmoid(acc)
        ev = _dot(_bf(pv), wpv, 1, 0)
        h_new = hv + gt * ev
        if not gain:
            return h_new, gt, ev
        r = lax.rsqrt(jnp.mean(h_new * h_new, axis=-1, keepdims=True) + EPS)
        return h_new, gt, ev, h_new * r * gain[0]

    res = _mm(f"ple_gate{l}", (t // tm, 1, 1), hn, row, wg, full(D_MODEL), (1, 0),
              [f32_row, bf_row, bf_row] + ([bf_row] if next_gain is not None else []),
              extras=common + ([(next_gain, VEC)] if next_gain is not None else []), epi=gate_epi)
    h_out, gate, e = res[0], res[1], res[2]
    return h_out, (res[3] if next_gain is not None else None), (h, hn, gate, e)


def _ple_bwd(l, dh, saved, p, norm_g, wg, deps=()):
    h, hn, gate, e = saved
    t = h.shape[0]
    tm = _row_tile(t)
    tk = _row_tile(t, 512)
    de, dz = _ple_gate_bwd(f"ple_gate_bwd{l}", dh, gate, e)
    full = lambda r: pl.BlockSpec((r, D_MODEL), lambda i, j, k: (0, 0))
    rowk = pl.BlockSpec((tk, D_MODEL), lambda i, j, k: (k, 0))
    (dwp,) = _mm(f"ple_dwp{l}", (1, 1, t // tk),
                 p, pl.BlockSpec((None, None, tk, PLE_DIM), lambda i, j, k: (l, 0, k, 0)),
                 de, rowk, (0, 0), [(_sds((PLE_DIM, D_MODEL), BF16), full(PLE_DIM))], deps=deps)
    (dwg,) = _mm(f"ple_dwg{l}", (1, 1, t // tk), hn, rowk, dz, rowk, (0, 0),
                 [(_sds((D_MODEL, D_MODEL), BF16), full(D_MODEL))])
    row = pl.BlockSpec((tm, D_MODEL), lambda i, j, k: (i, 0))
    dh_in, dg = _mm(f"ple_dhn{l}", (t // tm, 1, 1), dz, row, wg, full(D_MODEL), (1, 1),
                    _norm_bwd_outs(t, tm), extras=[(h, row), (dh, row), (norm_g, VEC)], epi=_norm_bwd_epi)
    return dh_in, jnp.sum(dg, axis=0), dwg, dwp


def _ret_layer_fwd(x, norm_g, wri, fetch_wro, gn, cos, sin, next_gain, hn=None, deps=()):
    t = x.shape[0]
    tm = _row_tile(t)
    nsh, _, wsh = wri.shape
    if hn is None:
        hn = _rms_fwd("mix_norm0", x, norm_g)
    tp = _row_tile(t, 512)
    (proj,) = _mm_rows("ret_in", tp, hn, wri, 'nn_cols', [(_sds((t, RET_IN), BF16), _rows(tp, RET_IN))], deps=deps)
    gated, outp, states = _ret_fwd(proj, cos, sin, gn)
    wro = fetch_wro((gated,))
    row = _rows(tp, D_MODEL)
    more, epi = _residual_epi(next_gain)
    h1, hn_next = _mm_rows("ret_out", tp, gated, wro.reshape(RET_HEADS, RET_DV, D_MODEL), 'nn_rows',
                           _residual_outs(t, row, next_gain), extras=[(x, row)] + more, epi=epi)
    return h1, hn_next, (x, hn, proj, gated, outp, states, wro)


def _ret_layer_bwd(dh, saved, norm_g, wri, gn, cos, sin, emit_out, emit_in, deps=()):
    x, hn, proj, gated, outp, states, wro = saved
    t = x.shape[0]
    tm = _row_tile(t)
    tk = _row_tile(t, 512)
    nsh, _, wsh = wri.shape
    tg = _row_tile(t, 512)
    vw = _rows(tg, RET_V_W)
    dout, dgate, dgn = _mm_rows(
        "ret_dgate", tg, dh, wro.reshape(RET_HEADS, RET_DV, D_MODEL), 'nt_rows',
        [(_sds((t, RET_V_W), BF16), vw), (_sds((t, RET_V_W), BF16), vw),
         (_sds((t // tg, 1, RET_V_W), F32), pl.BlockSpec((None, 1, RET_V_W), lambda i, j, k: (i, 0, 0)))],
        extras=[(outp, vw), (proj, pl.BlockSpec((tg, RET_V_W), lambda i, j, k: (i, (RET_IN - RET_V_W) // RET_V_W))),
                (gn.reshape(1, RET_V_W), pl.BlockSpec((1, RET_V_W), lambda i, j, k: (0, 0)))],
        epi=_ret_gate_bwd_epi, deps=deps)
    dgn = jnp.sum(dgn, axis=0)
    (dwro,) = _mm("ret_dwro", (1, 1, t // tk),
                  gated, pl.BlockSpec((tk, RET_V_W), lambda i, j, k: (k, 0)),
                  dh, pl.BlockSpec((tk, D_MODEL), lambda i, j, k: (k, 0)), (0, 0),
                  [(_sds((RET_V_W, D_MODEL), BF16), pl.BlockSpec((RET_V_W, D_MODEL), lambda i, j, k: (0, 0)))])
    dproj = _ret_bwd(proj, cos, sin, states, dout, dgate, deps=emit_out(dwro))
    half = nsh // 2
    (dwri,) = _mm("ret_dwri", (2, 1, t // tk),
                  hn, pl.BlockSpec((tk, D_MODEL), lambda i, j, k: (k, 0)),
                  dproj, pl.BlockSpec((tk, half * wsh), lambda i, j, k: (k, i)), (0, 0),
                  [(_sds((nsh, D_MODEL, wsh), BF16), pl.BlockSpec((half, D_MODEL, wsh), lambda i, j, k: (i, 0, 0)))],
                  split=wsh)
    deps = emit_in(dwri)
    td = _row_tile(t, 256)
    row = _rows(td, D_MODEL)
    dx, dg = _mm_rows("ret_dhn", td, dproj, wri, 'nt_cols', _norm_bwd_outs(t, td),
                      extras=[(x, row), (dh, row), (norm_g, VEC)], epi=_norm_bwd_epi, deps=deps)
    return dx, jnp.sum(dg, axis=0), dgn.reshape(RET_HEADS, RET_DV)


def _mla_layer_fwd(h, hn, wmi, qa, kva, wuq, wukv, gq, gk, wmo, tabs, next_gain):
    t = h.shape[0]
    tm = _row_tile(t)
    row = pl.BlockSpec((tm, D_MODEL), lambda i, j, k: (i, 0))
    (proj2,) = _mm("mla_in", (t // tm, 1, 1), hn, row,
                   wmi, pl.BlockSpec((D_MODEL, MLA_IN_PAD), lambda i, j, k: (0, 0)), (1, 0),
                   [(_sds((t, MLA_IN_PAD), F32), pl.BlockSpec((tm, MLA_IN_PAD), lambda i, j, k: (i, 0)))])
    cq, ckv = _mla_mid(proj2, qa, kva)
    qh, kh, vh = _mla_prep(cq, ckv, wuq, wukv, proj2, gq, gk, tabs)
    o, lse = _attn_fwd(qh, kh, vh)
    more, epi = _residual_epi(next_gain)
    h_out, hn_next = _mm("mla_out", (t // tm, 1, 1), o, row,
                         wmo, pl.BlockSpec((D_MODEL, D_MODEL), lambda i, j, k: (0, 0)), (1, 0),
                         _residual_outs(t, row, next_gain), extras=[(h, row)] + more, epi=epi)
    return h_out, hn_next, (h, hn, proj2, cq, ckv, qh, kh, vh, o, lse)


def _mla_layer_bwd(dh, saved, norm_g, wmi, qa, kva, wuq, wukv, gq, gk, wmo, tabs, deps=()):
    h, hn, proj2, cq, ckv, qh, kh, vh, o, lse = saved
    t = h.shape[0]
    tm = _row_tile(t)
    tk = _row_tile(t, 512)
    row = pl.BlockSpec((tm, D_MODEL), lambda i, j, k: (i, 0))
    rowk = pl.BlockSpec((tk, D_MODEL), lambda i, j, k: (k, 0))
    sq = pl.BlockSpec((D_MODEL, D_MODEL), lambda i, j, k: (0, 0))
    do, dob = _mm("mla_do", (t // tm, 1, 1), dh, row, wmo, sq, (1, 1),
                  [(_sds((t, D_MODEL), F32), row), (_sds((t, D_MODEL), BF16), row)], epi=lambda acc: (acc, acc),
                  deps=deps)
    (dwmo,) = _mm("mla_dwo", (1, 1, t // tk), o, rowk, dh, rowk, (0, 0), [(_sds((D_MODEL, D_MODEL), BF16), sq)])
    delta = _attn_delta(do, o, lse.shape[-1])
    dqt, dkh, dvh = _attn_bwd(qh, kh, vh, dob, lse, delta)
    dq, dkv, dkr, dgq, dgk = _mla_prep_bwd(cq, ckv, wuq, wukv, proj2, gq, gk, tabs, dqt, dkh, dvh)

    wide = MLA_HEADS * MLA_HD_PAD
    widek = pl.BlockSpec((tk, wide), lambda i, j, k: (k, 0))
    (dwuq,) = _mm("mla_dwuq", (1, 1, t // tk),
                  cq, pl.BlockSpec((tk, MLA_Q_RANK), lambda i, j, k: (k, 0)), dq, widek, (0, 0),
                  [(_sds((MLA_HEADS, MLA_Q_RANK, MLA_HD_PAD), BF16),
                    pl.BlockSpec((MLA_HEADS, MLA_Q_RANK, MLA_HD_PAD), lambda i, j, k: (0, 0, 0)))], split=MLA_HD_PAD)
    (dwukv,) = _mm("mla_dwukv", (1, 1, t // tk),
                   ckv, pl.BlockSpec((tk, MLA_KV_RANK), lambda i, j, k: (k, 0)), dkv, widek, (0, 0),
                   [(_sds((MLA_HEADS, MLA_KV_RANK, MLA_HD_PAD), BF16),
                     pl.BlockSpec((MLA_HEADS, MLA_KV_RANK, MLA_HD_PAD), lambda i, j, k: (0, 0, 0)))],
                   split=MLA_HD_PAD)
    side_by_side = lambda wg: wg.transpose(1, 0, 2).reshape(wg.shape[1], wide)
    widei = pl.BlockSpec((tm, wide), lambda i, j, k: (i, 0))
    (dcq,) = _mm("mla_dcq", (t // tm, 1, 1), dq, widei,
                 side_by_side(wuq), pl.BlockSpec((MLA_Q_RANK, wide), lambda i, j, k: (0, 0)), (1, 1),
                 [(_sds((t, MLA_Q_RANK), F32), pl.BlockSpec((tm, MLA_Q_RANK), lambda i, j, k: (i, 0)))])
    (dckv,) = _mm("mla_dckv", (t // tm, 1, 1), dkv, widei,
                  side_by_side(wukv), pl.BlockSpec((MLA_KV_RANK, wide), lambda i, j, k: (0, 0)), (1, 1),
                  [(_sds((t, MLA_KV_RANK), F32), pl.BlockSpec((tm, MLA_KV_RANK), lambda i, j, k: (i, 0)))])
    dproj2, dqa, dkva = _mla_mid_bwd(proj2, qa, kva, dcq, dckv, dkr)
    win = pl.BlockSpec((D_MODEL, MLA_IN_PAD), lambda i, j, k: (0, 0))
    (dwmi,) = _mm("mla_dwin", (1, 1, t // tk), hn, rowk,
                  dproj2, pl.BlockSpec((tk, MLA_IN_PAD), lambda i, j, k: (k, 0)), (0, 0),
                  [(_sds((D_MODEL, MLA_IN_PAD), BF16), win)])
    dh_in, dg = _mm("mla_dhn", (t // tm, 1, 1),
                    dproj2, pl.BlockSpec((tm, MLA_IN_PAD), lambda i, j, k: (i, 0)), wmi, win, (1, 1),
                    _norm_bwd_outs(t, tm), extras=[(h, row), (dh, row), (norm_g, VEC)], epi=_norm_bwd_epi)
    return dh_in, dict(mix=jnp.sum(dg, axis=0), wmi=dwmi, qa=dqa, kva=dkva, wuq=dwuq, wukv=dwukv, gq=dgq, gk=dgk,
                       wmo=dwmo)


def _local_step(x, p, target, w, fetch, emit=lambda group: ()):
    t = x.shape[0]
    inv = 1.0 / (ROPE_THETA ** (jnp.arange(0, RET_DK, 2, dtype=F32) / RET_DK))
    ang = jnp.arange(t, dtype=F32)[:, None] * inv[None, :]
    cos_r, sin_r = jnp.cos(ang), jnp.sin(ang)
    tabs = _mla_tables(t)
    row = lambda a, i: a[i:i + 1]

    h1, hn1, s_ret = _ret_layer_fwd(x, row(w['mix_norm'], 0), w['ret_w_in'],
                                    lambda after: fetch('ret_out', after)['ret_w_out'], w['ret_gn'], cos_r, sin_r,
                                    row(w['mlp_norm'], 0), hn=w.get('hn0'), deps=w['deps'])
    h2, hn2, s_mlp0 = _mlp_fwd(0, h1, hn1, fetch('mlp_w1_0', (h1,))['mlp_w1'],
                               lambda after: fetch('mlp_w2_0', after)['mlp_w2'], row(w['ple_norm'], 0))
    w0 = fetch('ple_0', (h2,))
    h3, hn3, s_ple0 = _ple_fwd(0, h2, hn2, p, w0['ple_gate_w'], w0['ple_proj_w'], row(w['mix_norm'], 1))
    wm = fetch('mla', (h3,))
    mla_w = (wm['mla_w_in'], w['mla_q_a_norm'], w['mla_kv_a_norm'], wm['mla_w_uq'], wm['mla_w_ukv'],
             w['mla_q_norm'], w['mla_k_norm'], wm['mla_w_out'], tabs)
    h4, hn4, s_mla = _mla_layer_fwd(h3, hn3, *mla_w, row(w['mlp_norm'], 1))
    w1 = fetch('layer_1', (h4,))
    h5, hn5, s_mlp1 = _mlp_fwd(1, h4, hn4, w1['mlp_w1'], lambda after: w1['mlp_w2'], row(w['ple_norm'], 1))
    dy, sq_err, s_ple1 = _ple_fwd(1, h5, hn5, p, w1['ple_gate_w'], w1['ple_proj_w'], None, target)

    n = N_DEV
    colsh = lambda a: a.reshape(a.shape[0], n, a.shape[1] // n).transpose(1, 0, 2)
    rowsh = lambda a: a.reshape(n, a.shape[0] // n, a.shape[1])
    big = {}

    def emit_group(group):
        big.update(group)
        return emit(group)

    dh5, dg_ple1, dwg1, dwp1 = _ple_bwd(1, dy, s_ple1, p, row(w['ple_norm'], 1), w1['ple_gate_w'])
    dh4, dg_mlp1, dw1_1, dw2_1 = _mlp_bwd(1, dh5, s_mlp1, row(w['mlp_norm'], 1))
    deps = emit_group({('ple_gate_w', 1): rowsh(dwg1), ('ple_proj_w', 1): colsh(dwp1),
                       ('mlp_w2', 1): dw2_1, ('mlp_w1', 1): dw1_1})
    dh3, gm = _mla_layer_bwd(dh4, s_mla, row(w['mix_norm'], 1), *mla_w, deps=deps)
    deps = emit_group({('mla_w_out', 0): rowsh(gm['wmo']), ('mla_w_uq', 0): _gather_rope(gm['wuq']),
                       ('mla_w_ukv', 0): gm['wukv'], ('mla_w_in', 0): rowsh(_gather_rope(gm['wmi']))})
    dh2, dg_ple0, dwg0, dwp0 = _ple_bwd(0, dh3, s_ple0, p, row(w['ple_norm'], 0), w0['ple_gate_w'], deps=deps)
    dh1, dg_mlp0, dw1_0, dw2_0 = _mlp_bwd(0, dh2, s_mlp0, row(w['mlp_norm'], 0))
    deps = emit_group({('ple_gate_w', 0): rowsh(dwg0), ('ple_proj_w', 0): colsh(dwp0),
                       ('mlp_w2', 0): dw2_0, ('mlp_w1', 0): dw1_0})
    dx, dg_mix0, dgn = _ret_layer_bwd(
        dh1, s_ret, row(w['mix_norm'], 0), w['ret_w_in'], w['ret_gn'], cos_r, sin_r,
        lambda dwro: emit_group({('ret_w_out', 0): rowsh(dwro)}),
        lambda dwri: emit_group({('ret_w_in', 0): dwri}), deps=deps)

    small = dict(
        mix_norm=[dg_mix0, gm['mix']], mlp_norm=[dg_mlp0, dg_mlp1], ple_norm=[dg_ple0, dg_ple1],
        ret_gn=dgn, mla_q_a_norm=gm['qa'], mla_kv_a_norm=gm['kva'], mla_q_norm=gm['gq'], mla_k_norm=gm['gk'],
    )
    return sq_err, dx, big, small


def _my_place():
    x, y, c = lax.axis_index("x"), lax.axis_index("y"), lax.axis_index("c")
    return x, y, c


def _flat(px, py, pc):
    return 4 * px + 2 * py + pc


def _peer(x, y, c, r):
    return (1 - x if r & 4 else x, 1 - y if r & 2 else y, 1 - c if r & 1 else c)


HBM = pl.BlockSpec(memory_space=pltpu.HBM)
SEMS = pl.BlockSpec(memory_space=pltpu.SEMAPHORE)
SIDE_EFFECT = pltpu.SideEffectType.DATAFLOW_SIDE_EFFECTING


def _rs_copies(x, y, c, srcs, lands, send_sems, recv_sems):
    copies = []
    for a in range(len(srcs)):
        for r in range(1, N_DEV):
            peer = _peer(x, y, c, r)
            k = a * (N_DEV - 1) + r - 1
            copies.append(pltpu.make_async_remote_copy(
                src_ref=srcs[a].at[_flat(*peer)], dst_ref=lands[a].at[r - 1],
                send_sem=send_sems.at[k], recv_sem=recv_sems.at[k], device_id=peer, device_id_type=MESH))
    return copies


def _rs_start(name, arrays):
    n = len(arrays)
    hbm = lambda a: pltpu.with_memory_space_constraint(a, pltpu.HBM)
    lands = [hbm(lax.empty((N_DEV - 1,) + a.shape[1:], a.dtype)) for a in arrays]

    def body(*refs):
        srcs, lnd = refs[:n], refs[n:2 * n]
        send_sems, recv_sems = refs[2 * n], refs[2 * n + 1]
        token = refs[-1]
        for cp in _rs_copies(*_my_place(), srcs, lnd, send_sems, recv_sems):
            cp.start()
        token[...] = jnp.zeros_like(token)

    outs = pl.pallas_call(
        body, name=name,
        in_specs=[HBM] * (2 * n),
        out_specs=[SEMS, SEMS] + [HBM] * (2 * n) + [pl.BlockSpec(memory_space=pltpu.VMEM)],
        out_shape=[pltpu.SemaphoreType.DMA((n * (N_DEV - 1),)), pltpu.SemaphoreType.DMA((n * (N_DEV - 1),))]
        + [pltpu.HBM(a.shape, a.dtype) for a in arrays] + [pltpu.HBM(l.shape, l.dtype) for l in lands]
        + [_sds((8, 128), F32)],
        input_output_aliases={i: 2 + i for i in range(2 * n)},
        compiler_params=pltpu.CompilerParams(has_side_effects=SIDE_EFFECT),
    )(*[hbm(a) for a in arrays], *lands)
    return outs[0], outs[1], outs[2:2 + n], outs[2 + n:2 + 2 * n], outs[-1]


def _rs_wait(name, send_sems, recv_sems, srcs, lands, after):
    n = len(srcs)

    def body(*refs):
        src_refs, lnd = refs[:n], refs[n:2 * n]
        send, recv = refs[2 * n], refs[2 * n + 1]
        for cp in _rs_copies(*_my_place(), src_refs, lnd, send, recv):
            cp.wait_send()
            cp.wait_recv()

    outs = pl.pallas_call(
        body, name=name,
        in_specs=[HBM] * (2 * n) + [SEMS, SEMS] + [ANY] * len(after),
        out_specs=[HBM] * (2 * n),
        out_shape=[pltpu.HBM(a.shape, a.dtype) for a in list(srcs) + list(lands)],
        input_output_aliases={i: i for i in range(2 * n)},
        compiler_params=pltpu.CompilerParams(has_side_effects=SIDE_EFFECT),
    )(*srcs, *lands, send_sems, recv_sems, *after)
    return outs[:n], outs[n:]


SMALL_PACK_ROWS = 16


def _all_reduce_small(rows, deps=()):
    n = len(rows)

    def body(*refs):
        ins = refs[:n]
        out_ref, mine, buf, send_sems, recv_sems = refs[n + len(deps):]
        x, y, c = _my_place()
        mine[...] = jnp.zeros_like(mine)
        for (r0, a), ref in zip(rows, ins):
            mine[r0:r0 + a.shape[0], 0:a.shape[1]] = ref[...]
        buf[_flat(x, y, c)] = mine[...]
        copies = []
        for r in range(1, N_DEV):
            peer = _peer(x, y, c, r)
            send = pltpu.make_async_remote_copy(
                src_ref=mine, dst_ref=buf.at[_flat(x, y, c)],
                send_sem=send_sems.at[r - 1], recv_sem=recv_sems.at[r - 1], device_id=peer, device_id_type=MESH)
            send.start()
            recv = pltpu.make_async_remote_copy(
                src_ref=mine, dst_ref=buf.at[_flat(*peer)],
                send_sem=send_sems.at[r - 1], recv_sem=recv_sems.at[r - 1], device_id=peer, device_id_type=MESH)
            copies.append((send, recv))
        for send, recv in copies:
            send.wait_send()
            recv.wait_recv()
        acc = buf[0]
        for s in range(1, N_DEV):
            acc = acc + buf[s]
        out_ref[...] = acc

    vm = pl.BlockSpec(memory_space=pltpu.VMEM)
    shape = (SMALL_PACK_ROWS, D_MODEL)
    return pl.pallas_call(
        body, name="all_reduce_small", in_specs=[vm] * n + [ANY] * len(deps), out_specs=vm,
        out_shape=_sds(shape, F32),
        scratch_shapes=[pltpu.VMEM(shape, F32), pltpu.VMEM((N_DEV,) + shape, F32),
                        pltpu.SemaphoreType.DMA((7,)), pltpu.SemaphoreType.DMA((7,))],
    )(*[a for _, a in rows], *deps)


def _adamw_math(w, g, m, v):
    m = ADAM_B1 * m + (1.0 - ADAM_B1) * g
    v = ADAM_B2 * v + (1.0 - ADAM_B2) * (g * g)
    m_hat = m / (1.0 - ADAM_B1 ** ADAM_STEP)
    v_hat = v / (1.0 - ADAM_B2 ** ADAM_STEP)
    delta = -ADAM_LR * (m_hat / (jnp.sqrt(v_hat) + ADAM_EPS) + ADAM_WD * w)
    return delta, m, v


def _adamw_big(name, w, m, v, srcs, lands, me):
    nl, rows, cols = w.shape
    tr = next(cand for cand in (256, 128, 64, 32, 16, 8) if rows % cand == 0)

    def body(me_ref, w_ref, m_ref, v_ref, *rest):
        src_refs, land_refs = rest[:nl], rest[nl:2 * nl]
        g_ref, d_ref, mo_ref, vo_ref = rest[2 * nl:]
        for layer in range(nl):
            @pl.when(pl.program_id(0) == layer)
            def _():
                g = src_refs[layer][...].astype(F32)
                for s in range(N_DEV - 1):
                    g = g + land_refs[layer][s].astype(F32)
                delta, mn, vn = _adamw_math(w_ref[...], g, m_ref[...], v_ref[...])
                g_ref[...] = g
                d_ref[...] = delta
                mo_ref[...] = mn
                vo_ref[...] = vn

    blk = pl.BlockSpec((None, tr, cols), lambda l, i, me_ref: (l, i, 0))
    at = lambda layer, l, i: jnp.where(l == layer, i, 0)
    own = [pl.BlockSpec((None, tr, cols), functools.partial(lambda layer, l, i, me_ref: (me_ref[0], at(layer, l, i), 0),
                                                            layer)) for layer in range(nl)]
    peers = [pl.BlockSpec((N_DEV - 1, tr, cols), functools.partial(lambda layer, l, i, me_ref: (0, at(layer, l, i), 0),
                                                                   layer)) for layer in range(nl)]
    return pl.pallas_call(
        body, name=name,
        grid_spec=pltpu.PrefetchScalarGridSpec(
            num_scalar_prefetch=1, grid=(nl, rows // tr),
            in_specs=[blk, blk, blk] + own + peers, out_specs=[blk] * 4),
        out_shape=[_sds((nl, rows, cols), F32)] * 4,
        compiler_params=_cparams(("arbitrary", "arbitrary")),
    )(me, w, m, v, *srcs, *lands)


def _adamw_small(ws, gs, ms, vs):
    n = len(ws)

    def body(*refs):
        w_refs, g_refs, m_refs, v_refs = (refs[i * n:(i + 1) * n] for i in range(4))
        d_out, m_out, v_out = (refs[(4 + i) * n:(5 + i) * n] for i in range(3))
        for i in range(n):
            delta, mn, vn = _adamw_math(w_refs[i][...], g_refs[i][...], m_refs[i][...], v_refs[i][...])
            d_out[i][...] = delta
            m_out[i][...] = mn
            v_out[i][...] = vn

    vm = pl.BlockSpec(memory_space=pltpu.VMEM)
    outs = pl.pallas_call(
        body, name="adamw_small", in_specs=[vm] * (4 * n), out_specs=[vm] * (3 * n),
        out_shape=[_sds(a.shape, F32) for a in ws] * 3,
    )(*ws, *gs, *ms, *vs)
    return outs[:n], outs[n:2 * n], outs[2 * n:]


def _pad_to(a, rows, cols):
    return jnp.pad(a, ((0, rows - a.shape[0]), (0, cols - a.shape[1])))


def _place_own(blocks):
    me = _flat(*_my_place())
    return [lax.dynamic_update_slice(lax.empty((N_DEV,) + b.shape, b.dtype), b[None], (me,) + (0,) * b.ndim)
            for b in blocks]


def _ag_copies(x, y, c, blocks, bufs, send_sems, recv_sems, arriving):
    copies = []
    for a in range(len(blocks)):
        for r in range(1, N_DEV):
            peer = _peer(x, y, c, r)
            k = a * (N_DEV - 1) + r - 1
            copies.append(pltpu.make_async_remote_copy(
                src_ref=blocks[a], dst_ref=bufs[a].at[_flat(*(peer if arriving else (x, y, c)))],
                send_sem=send_sems.at[k], recv_sem=recv_sems.at[k], device_id=peer, device_id_type=MESH))
    return copies


def _ag_start(groups, after):
    flat = [pair for g in groups for pair in g]
    n, ng = len(flat), len(groups)
    hbm = lambda a: pltpu.with_memory_space_constraint(a, pltpu.HBM)

    def body(*refs):
        blocks, bufs = refs[:n], refs[n:2 * n]
        sems = refs[2 * n + len(after):2 * n + len(after) + 2 * ng]
        x, y, c = _my_place()
        at = 0
        for gi, g in enumerate(groups):
            for cp in _ag_copies(x, y, c, blocks[at:at + len(g)], bufs[at:at + len(g)], sems[2 * gi],
                                 sems[2 * gi + 1], arriving=False):
                cp.start()
            at += len(g)
        refs[-1][...] = jnp.zeros_like(refs[-1])

    sem_shapes = [pltpu.SemaphoreType.DMA((len(g) * (N_DEV - 1),)) for g in groups for _ in range(2)]
    outs = pl.pallas_call(
        body, name="gather_start",
        in_specs=[HBM] * (2 * n) + [ANY] * len(after),
        out_specs=[SEMS] * (2 * ng) + [HBM] * (2 * n) + [pl.BlockSpec(memory_space=pltpu.VMEM)],
        out_shape=sem_shapes + [pltpu.HBM(b.shape, b.dtype) for b, _ in flat]
        + [pltpu.HBM(u.shape, u.dtype) for _, u in flat] + [_sds((8, 128), F32)],
        input_output_aliases={i: 2 * ng + i for i in range(2 * n)},
        compiler_params=pltpu.CompilerParams(has_side_effects=SIDE_EFFECT),
    )(*[hbm(b) for b, _ in flat], *[hbm(u) for _, u in flat], *after)
    blocks_thru, bufs_thru = outs[2 * ng:2 * ng + n], outs[2 * ng + n:2 * ng + 2 * n]
    started, at = [], 0
    for gi, g in enumerate(groups):
        started.append((outs[2 * gi], outs[2 * gi + 1], blocks_thru[at:at + len(g)], bufs_thru[at:at + len(g)]))
        at += len(g)
    return started, outs[-1]


def _ag_wait(name, send_sems, recv_sems, blocks, bufs, after):
    n = len(blocks)

    def body(*refs):
        for cp in _ag_copies(*_my_place(), refs[:n], refs[n:2 * n], refs[2 * n], refs[2 * n + 1], arriving=True):
            cp.wait_send()
            cp.wait_recv()

    outs = pl.pallas_call(
        body, name=name,
        in_specs=[HBM] * (2 * n) + [SEMS, SEMS] + [ANY] * len(after),
        out_specs=[HBM] * (2 * n),
        out_shape=[pltpu.HBM(a.shape, a.dtype) for a in list(blocks) + list(bufs)],
        input_output_aliases={i: i for i in range(2 * n)},
        compiler_params=pltpu.CompilerParams(has_side_effects=SIDE_EFFECT),
    )(*blocks, *bufs, send_sems, recv_sems, *after)
    return outs[n:]


def _split_call(name, body, thru, sems_in, new_sems, after):
    n, ns, nn = len(thru), len(sems_in), len(new_sems)
    hbm = lambda a: pltpu.with_memory_space_constraint(a, pltpu.HBM)

    def wrapped(*refs):
        body(refs[:n], refs[n:n + ns], refs[n + ns + len(after):n + ns + len(after) + nn])
        refs[-1][...] = jnp.zeros_like(refs[-1])

    outs = pl.pallas_call(
        wrapped, name=name,
        in_specs=[HBM] * n + [SEMS] * ns + [ANY] * len(after),
        out_specs=[SEMS] * nn + [HBM] * n + [pl.BlockSpec(memory_space=pltpu.VMEM)],
        out_shape=[pltpu.SemaphoreType.DMA((k,)) for k in new_sems] + [pltpu.HBM(a.shape, a.dtype) for a in thru]
        + [_sds((8, 128), F32)],
        input_output_aliases={i: nn + i for i in range(n)},
        compiler_params=pltpu.CompilerParams(has_side_effects=SIDE_EFFECT),
    )(*[hbm(a) for a in thru], *sems_in, *after)
    return list(outs[:nn]), list(outs[nn:nn + n]), outs[-1]


def _first_gather(blocks, bufs, overlap):
    n = len(blocks)

    def copies(refs, s1, r1, s2, r2):
        x, y, c = _my_place()
        me, sibling = (x, y, c), (x, y, 1 - c)
        chips = [(1 - x, y), (x, 1 - y), (1 - x, 1 - y)]
        blk, buf = refs[:n], refs[n:]
        out = dict(send1=[], recv1_sib=[], recv1_ici=[], send2=[], recv2=[])
        for a in range(n):
            place = lambda dev: buf[a].at[_flat(*dev)]
            for k, to in enumerate([sibling] + [(*chip, c) for chip in chips]):
                mk = lambda dst: pltpu.make_async_remote_copy(
                    src_ref=blk[a], dst_ref=dst, send_sem=s1.at[4 * a + k], recv_sem=r1.at[4 * a + k],
                    device_id=to, device_id_type=MESH)
                out['send1'].append(mk(place(me)))
                out['recv1_sib' if k == 0 else 'recv1_ici'].append(mk(place(to)))
            for j, chip in enumerate(chips):
                mk = lambda dev: pltpu.make_async_remote_copy(
                    src_ref=place(dev), dst_ref=place(dev), send_sem=s2.at[3 * a + j], recv_sem=r2.at[3 * a + j],
                    device_id=sibling, device_id_type=MESH)
                out['send2'].append(mk((*chip, c)))
                out['recv2'].append(mk((*chip, 1 - c)))
        return out

    def start(refs, sems_in, new):
        for cp in copies(refs, new[0], new[1], new[0], new[1])['send1']:
            cp.start()

    def forward(refs, sems_in, new):
        cps = copies(refs, sems_in[0], sems_in[1], new[0], new[1])
        for cp in cps['recv1_ici']:
            cp.wait_recv()
        for cp in cps['send2']:
            cp.start()

    def finish(refs, sems_in, new):
        cps = copies(refs, *sems_in)
        for cp in cps['recv1_sib'] + cps['recv2']:
            cp.wait_recv()
        for cp in cps['send1'] + cps['send2']:
            cp.wait_send()

    sems1, thru, token = _split_call("first_gather_start", start, list(blocks) + list(bufs), [], [4 * n, 4 * n], ())
    after = overlap(token)
    sems2, thru, token = _split_call("first_gather_forward", forward, thru, sems1, [3 * n, 3 * n], after)
    _, thru, _ = _split_call("first_gather_wait", finish, thru, sems1 + sems2, [], ())
    return thru[n:], token


def _prepare_weights(p, x):
    n = N_DEV
    bf = lambda a: a.astype(BF16)
    gn_pack = jnp.concatenate([
        _pad_to(p['ret_gn'][0], RET_HEADS, 128), _pad_to(p['mla_q_a_norm'], 1, 128),
        _pad_to(p['mla_kv_a_norm'], 1, 128), jnp.zeros((2, 128), F32)], axis=0)
    ple = lambda l: [bf(p['ple_gate_w'][l]), bf(p['ple_proj_w'][l])]
    names = ('ret_out', 'mlp_w1_0', 'mlp_w2_0', 'ple_0', 'mla', 'layer_1')
    later = [[bf(p['ret_w_out'][0])], [bf(p['mlp_w1'][0])], [bf(p['mlp_w2'][0])], ple(0),
             [bf(p['mla_w_in'][0]), bf(p['mla_w_uq'][0]), bf(p['mla_w_ukv'][0]), bf(p['mla_w_out'][0])],
             [bf(p['mlp_w1'][1]), bf(p['mlp_w2'][1])] + ple(1)]
    first = [gn_pack, bf(p['ret_w_in'][0])]
    behind = {}

    def overlap(token):
        behind['hn0'] = _rms_fwd("mix_norm0", x, p['mix_norm'][0:1], deps=(token,))
        behind['bufs'] = _place_own([b for g in later for b in g])
        return (behind['hn0'], *behind['bufs'])

    (pack, wri), token = _first_gather(first, _place_own(first), overlap)
    bufs = behind['bufs']
    groups, at = [], 0
    for g in later:
        groups.append(list(zip(g, bufs[at:at + len(g)])))
        at += len(g)
    started, token = _ag_start(groups, (token,))

    w = {k: p[k] for k in ('mix_norm', 'mlp_norm', 'ple_norm')}
    w['hn0'] = behind['hn0']
    w['ret_gn'] = pack[:, :RET_HEADS, :RET_DV // n].transpose(1, 0, 2).reshape(RET_HEADS, RET_DV)
    w['mla_q_a_norm'] = pack[:, RET_HEADS, :MLA_Q_RANK // n].reshape(1, MLA_Q_RANK)
    w['mla_kv_a_norm'] = pack[:, RET_HEADS + 1, :MLA_KV_RANK // n].reshape(1, MLA_KV_RANK)
    w['ret_w_in'] = wri
    w['mla_q_norm'] = _spread_rope(p['mla_q_norm'])
    w['mla_k_norm'] = _spread_rope(p['mla_k_norm'])
    w['deps'] = (token,)

    def fetch(name, after):
        got = list(_ag_wait("gather_wait_" + name, *started[names.index(name)], after))
        if name == 'ret_out':
            return dict(ret_w_out=got[0].reshape(RET_V_W, D_MODEL))
        if name == 'mla':
            wmi, wuq, wukv, wmo = got
            return dict(mla_w_in=_spread_rope(wmi.reshape(D_MODEL, MLA_IN)), mla_w_uq=_spread_rope(wuq),
                        mla_w_ukv=wukv, mla_w_out=wmo.reshape(D_MODEL, D_MODEL))
        out = {}
        if name in ('mlp_w1_0', 'layer_1'):
            out['mlp_w1'] = got.pop(0)
        if name in ('mlp_w2_0', 'layer_1'):
            out['mlp_w2'] = got.pop(0)
        if name in ('ple_0', 'layer_1'):
            out['ple_gate_w'] = got[0].reshape(D_MODEL, D_MODEL)
            out['ple_proj_w'] = got[1].transpose(1, 0, 2).reshape(PLE_DIM, D_MODEL)
        return out

    return w, fetch


def _small_grads(small, after):
    rows = [(0, small['mix_norm'][0]), (1, small['mix_norm'][1]), (2, small['mlp_norm'][0]),
            (3, small['mlp_norm'][1]), (4, small['ple_norm'][0]), (5, small['ple_norm'][1]),
            (6, small['ret_gn']), (10, small['mla_q_a_norm']), (11, small['mla_kv_a_norm']),
            (12, small['mla_q_norm']), (13, small['mla_k_norm']), (14, small['sq_err'])]
    gs = _all_reduce_small(rows, after)
    me = _flat(*_my_place())
    n = N_DEV
    return dict(
        sq_err=gs[14, 0],
        mix_norm=gs[0:2], mlp_norm=gs[2:4], ple_norm=gs[4:6],
        ret_gn=lax.dynamic_slice(gs, (6, me * (RET_DV // n)), (RET_HEADS, RET_DV // n)),
        mla_q_a_norm=lax.dynamic_slice(gs, (10, me * (MLA_Q_RANK // n)), (1, MLA_Q_RANK // n)),
        mla_kv_a_norm=lax.dynamic_slice(gs, (11, me * (MLA_KV_RANK // n)), (1, MLA_KV_RANK // n)),
        mla_q_norm=_gather_rope(gs[12:13, :MLA_HD_PAD]), mla_k_norm=_gather_rope(gs[13:14, :MLA_HD_PAD]))


def kernel(x, p, mix_norm, ret_w_in, ret_gn, ret_w_out, mla_w_in, mla_q_a_norm, mla_kv_a_norm, mla_w_uq, mla_w_ukv, mla_q_norm, mla_k_norm, mla_w_out, mlp_norm, mlp_w1, mlp_w2, ple_norm, ple_gate_w, ple_proj_w, loss_target, m_mix_norm, m_ret_w_in, m_ret_gn, m_ret_w_out, m_mla_w_in, m_mla_q_a_norm, m_mla_kv_a_norm, m_mla_w_uq, m_mla_w_ukv, m_mla_q_norm, m_mla_k_norm, m_mla_w_out, m_mlp_norm, m_mlp_w1, m_mlp_w2, m_ple_norm, m_ple_gate_w, m_ple_proj_w, v_mix_norm, v_ret_w_in, v_ret_gn, v_ret_w_out, v_mla_w_in, v_mla_q_a_norm, v_mla_kv_a_norm, v_mla_w_uq, v_mla_w_ukv, v_mla_q_norm, v_mla_k_norm, v_mla_w_out, v_mlp_norm, v_mlp_w1, v_mlp_w2, v_ple_norm, v_ple_gate_w, v_ple_proj_w):
    given = dict(locals())
    params = {n: given[n] for n in WEIGHTS}
    w, fetch = _prepare_weights(params, x[0])

    started = []

    def emit(group):
        keys = list(group)
        send, recv, srcs, lands, token = _rs_start(f"rs_start{len(started)}", [group[k] for k in keys])
        started.append((keys, send, recv, srcs, lands))
        return (token,)

    sq_err, grad_x, _, small = _local_step(x[0], p, loss_target[0], w, fetch, emit)
    small['sq_err'] = sq_err[0:1]

    grads, deltas, new_m, new_v = {}, {}, {}, {}
    total = {}

    def small_updates(after):
        sg = _small_grads(small, after)
        total['loss'] = 0.5 / D_MODEL * sg['sq_err']
        two_d = lambda a: a.reshape(-1, a.shape[-1])
        d_s, m_s, v_s = _adamw_small(
            [two_d(params[n]) for n in SMALL], [sg[n] for n in SMALL],
            [two_d(given["m_" + n]) for n in SMALL], [two_d(given["v_" + n]) for n in SMALL])
        for i, n in enumerate(SMALL):
            shape = params[n].shape
            grads[n], deltas[n], new_m[n], new_v[n] = (a.reshape(shape) for a in (sg[n], d_s[i], m_s[i], v_s[i]))
        return (d_s[0],)

    me = _flat(*_my_place()).astype(jnp.int32).reshape(1)
    after = (grad_x,)
    src_of, land_of = {}, {}
    for gi, (keys, send, recv, srcs, lands) in enumerate(started):
        if gi == len(started) - 1:
            after = small_updates(after)
        srcs, lands = _rs_wait(f"rs_wait{gi}", send, recv, srcs, lands, after)
        for k, s, l in zip(keys, srcs, lands):
            src_of[k], land_of[k] = s, l
        done = [n for n in BIG if n not in grads and all((n, l) in src_of for l in range(params[n].shape[0]))]
        for n in done:
            layers = range(params[n].shape[0])
            grads[n], deltas[n], new_m[n], new_v[n] = _adamw_big(
                "adamw_" + n, params[n], given["m_" + n], given["v_" + n],
                [src_of[(n, l)] for l in layers], [land_of[(n, l)] for l in layers], me)
        if done:
            after = tuple(deltas[n] for n in done)

    return (total['loss'], grad_x[None], *[grads[n] for n in WEIGHTS], *[deltas[n] for n in WEIGHTS],
            *[new_m[n] for n in WEIGHTS], *[new_v[n] for n in WEIGHTS])
```

```python
import functools

import jax
import jax.numpy as jnp
from jax import lax
from jax.experimental import pallas as pl
from jax.experimental.pallas import tpu as pltpu

F32 = jnp.float32
BF16 = jnp.bfloat16
MESH = pl.DeviceIdType.MESH
ANY = pl.BlockSpec(memory_space=pl.ANY)

N_DEV = 8
D_MODEL = 1024
CHUNK = 64
RET_BLOCK = 4 * CHUNK
EPS = 1e-6
ROPE_THETA = 10000.0
RET_HEADS = 4
RET_DK = 256
RET_DV = 512
RET_QK_W = RET_HEADS * RET_DK
RET_V_W = RET_HEADS * RET_DV
RET_IN = 2 * RET_QK_W + 2 * RET_V_W
MLA_HEADS = 8
MLA_NOPE = 128
MLA_ROPE = 64
MLA_QKD = MLA_NOPE + MLA_ROPE
MLA_VD = 128
MLA_Q_RANK = 384
MLA_KV_RANK = 256
MLA_IN = MLA_Q_RANK + MLA_KV_RANK + MLA_ROPE
MLA_IN_PAD = 768
MLA_HD_PAD = 256
D_FF = 4096
PLE_DIM = 256
ATT_SCALE = MLA_QKD ** -0.5
LOG2E = 1.4426950408889634
ATT_EXP2 = ATT_SCALE * LOG2E

ADAM_LR = 0.001
ADAM_B1 = 0.9
ADAM_B2 = 0.999
ADAM_EPS = 1e-08
ADAM_WD = 0.01
ADAM_STEP = 10

VMEM_LIMIT = 52 * 1024 * 1024
ROW_TILE = 1024
RET_ROWS = 512
ATT_BLOCK = 256
ATT_QROWS = 1024
ATT_KROWS = 1024
ATT_HEADS = 2
PREP_ROWS = 1024

WEIGHTS = ['mix_norm', 'ret_w_in', 'ret_gn', 'ret_w_out', 'mla_w_in', 'mla_q_a_norm', 'mla_kv_a_norm',
           'mla_w_uq', 'mla_w_ukv', 'mla_q_norm', 'mla_k_norm', 'mla_w_out', 'mlp_norm', 'mlp_w1', 'mlp_w2',
           'ple_norm', 'ple_gate_w', 'ple_proj_w']
BIG = ['ret_w_in', 'ret_w_out', 'mla_w_in', 'mla_w_uq', 'mla_w_ukv', 'mla_w_out', 'mlp_w1', 'mlp_w2',
       'ple_gate_w', 'ple_proj_w']
SMALL = [w for w in WEIGHTS if w not in BIG]


def _cparams(sem=None):
    return pltpu.CompilerParams(dimension_semantics=sem, vmem_limit_bytes=VMEM_LIMIT)


def _dot(a, b, ca, cb):
    return lax.dot_general(a, b, (((ca,), (cb,)), ((), ())), preferred_element_type=F32)


def _bf(v):
    return v if v.dtype == BF16 else v.astype(BF16)


def _sigmoid(z):
    return 1.0 / (1.0 + jnp.exp(-z))


def _mm(name, grid, a, a_spec, b, b_spec, contract, outs, extras=(), epi=None, deps=(), split=None):
    nk = grid[2]
    n_ex, n_out, n_dep = len(extras), len(outs), len(deps)
    acc_shape = tuple(d for d in outs[0][1].block_shape if d is not None)
    if split is not None:
        acc_shape = (acc_shape[1], acc_shape[0] * split)

    def body(*refs):
        a_ref, b_ref = refs[:2]
        ex_refs = refs[2:2 + n_ex]
        out_refs = refs[2 + n_ex + n_dep:2 + n_ex + n_dep + n_out]

        def product():
            return _dot(_bf(a_ref[...]), _bf(b_ref[...]), contract[0], contract[1])

        def finish(acc):
            if split is not None:
                for j in range(acc_shape[1] // split):
                    out_refs[0][j] = acc[:, j * split:(j + 1) * split].astype(out_refs[0].dtype)
                return
            acc = acc[...]
            res = epi(acc, *[r[...] for r in ex_refs]) if epi is not None else (acc,)
            for o, r in zip(out_refs, res):
                o[...] = r.astype(o.dtype)

        if nk == 1:
            finish(product())
        else:
            acc_ref = refs[-1]
            k = pl.program_id(2)

            @pl.when(k == 0)
            def _():
                acc_ref[...] = jnp.zeros_like(acc_ref)

            acc_ref[...] += product()

            @pl.when(k == nk - 1)
            def _():
                finish(acc_ref)

    return pl.pallas_call(
        body, name=name, grid=grid,
        in_specs=[a_spec, b_spec] + [s for _, s in extras] + [ANY] * n_dep,
        out_specs=[s for _, s in outs],
        out_shape=[s for s, _ in outs],
        scratch_shapes=[pltpu.VMEM(acc_shape, F32)] if nk > 1 else [],
        compiler_params=_cparams(("parallel", "parallel", "arbitrary")),
    )(a, b, *[x for x, _ in extras], *deps)


def _mm_rows(name, tm, a, w, mode, outs, extras=(), epi=None, deps=()):
    n_sh, rows, cols = w.shape
    n_ex, n_out, n_dep = len(extras), len(outs), len(deps)
    by_cols = mode in ('nn_cols', 'nt_rows')
    width = cols if mode == 'nn_cols' else rows

    def body(*refs):
        a_ref, w_ref = refs[:2]
        ex_refs = refs[2:2 + n_ex]
        out_refs = refs[2 + n_ex + n_dep:2 + n_ex + n_dep + n_out]
        if by_cols:
            av = _bf(a_ref[...])
            for s in range(n_sh):
                cs = slice(s * width, (s + 1) * width)
                acc = _dot(av, w_ref[s], 1, 0 if mode == 'nn_cols' else 1)
                res = epi(acc, *[r[:, cs] for r in ex_refs]) if epi is not None else (acc,)
                for o, r in zip(out_refs, res):
                    o[:, cs] = r.astype(o.dtype)
        else:
            chunk = rows if mode == 'nn_rows' else cols
            acc = None
            for s in range(n_sh):
                part = _dot(_bf(a_ref[:, s * chunk:(s + 1) * chunk]), w_ref[s], 1, 0 if mode == 'nn_rows' else 1)
                acc = part if acc is None else acc + part
            res = epi(acc, *[r[...] for r in ex_refs]) if epi is not None else (acc,)
            for o, r in zip(out_refs, res):
                o[...] = r.astype(o.dtype)

    t, ka = a.shape
    return pl.pallas_call(
        body, name=name, grid=(t // tm, 1, 1),
        in_specs=[pl.BlockSpec((tm, ka), lambda i, j, k: (i, 0)),
                  pl.BlockSpec((n_sh, rows, cols), lambda i, j, k: (0, 0, 0))] + [s for _, s in extras] + [ANY] * n_dep,
        out_specs=[s for _, s in outs],
        out_shape=[s for s, _ in outs],
        compiler_params=_cparams(("parallel", "arbitrary", "arbitrary")),
    )(a, w, *[x for x, _ in extras], *deps)


def _sds(shape, dtype):
    return jax.ShapeDtypeStruct(shape, dtype)


def _row_tile(t, cap=ROW_TILE):
    return min(cap, t)


def _rms_fwd(name, x, g, deps=()):
    t, d = x.shape
    tm = _row_tile(t)

    def body(x_ref, g_ref, *rest):
        o_ref = rest[-1]
        xv = x_ref[...]
        r = lax.rsqrt(jnp.mean(xv * xv, axis=-1, keepdims=True) + EPS)
        o_ref[...] = (xv * r * g_ref[...]).astype(o_ref.dtype)

    return pl.pallas_call(
        body, name=name, grid=(t // tm,),
        in_specs=[pl.BlockSpec((tm, d), lambda i: (i, 0)), pl.BlockSpec((1, d), lambda i: (0, 0))] + [ANY] * len(deps),
        out_specs=pl.BlockSpec((tm, d), lambda i: (i, 0)),
        out_shape=_sds((t, d), BF16),
        compiler_params=_cparams(("parallel",)),
    )(x, g, *deps)


def _rms_bwd_rows(dy, xv, g, n):
    r = lax.rsqrt(jnp.sum(xv * xv, axis=-1, keepdims=True) / n + EPS)
    xh = xv * r
    dxh = dy * g
    dx = r * (dxh - xh * (jnp.sum(dxh * xh, axis=-1, keepdims=True) / n))
    return dx, dy * xh


def _ple_gate_bwd(name, dh, gate, e):
    t, d = dh.shape
    tm = _row_tile(t)

    def body(dh_ref, g_ref, e_ref, de_ref, dz_ref):
        dh_v, gt = dh_ref[...], g_ref[...].astype(F32)
        de_ref[...] = (dh_v * gt).astype(BF16)
        dz_ref[...] = (dh_v * e_ref[...].astype(F32) * (gt * (1.0 - gt))).astype(BF16)

    row = pl.BlockSpec((tm, d), lambda i: (i, 0))
    return pl.pallas_call(
        body, name=name, grid=(t // tm,), in_specs=[row, row, row], out_specs=[row, row],
        out_shape=[_sds((t, d), BF16), _sds((t, d), BF16)],
        compiler_params=_cparams(("parallel",)),
    )(dh, gate, e)


def _rope_half(v, cos, sin):
    half = v.shape[-1] // 2
    v1, v2 = v[:, :half], v[:, half:]
    return jnp.concatenate([v1 * cos - v2 * sin, v2 * cos + v1 * sin], axis=-1)


def _ret_consts():
    lg = jnp.log(1.0 - 2.0 ** (-5.0 - jnp.arange(RET_HEADS, dtype=F32)))
    idx = jnp.arange(RET_BLOCK, dtype=F32)
    chunk = jnp.floor(idx / CHUNK)
    dist = idx[:, None] - idx[None, :]
    same = chunk[:, None] == chunk[None, :]
    seen = jnp.where(same, jnp.abs(dist), jnp.where(chunk[None, :] < chunk[:, None], dist, jnp.inf))
    intra = jnp.exp(lg[:, None, None] * seen)
    qdec = jnp.exp(lg[:, None] * (idx + 1.0))
    kdec = jnp.exp(lg[:, None] * (RET_BLOCK - 1.0 - idx))
    cdec = jnp.exp(lg * RET_BLOCK)
    qdec = jnp.broadcast_to(qdec[:, :, None], (RET_HEADS, RET_BLOCK, RET_DK))
    kdec = jnp.broadcast_to(kdec[:, :, None], (RET_HEADS, RET_BLOCK, RET_DK))
    cdec = jnp.broadcast_to(cdec[:, None, None], (RET_HEADS, 1, RET_DV))
    return intra, qdec, kdec, cdec


def _ret_specs(rb, rev_nb=None):
    blk = (lambda i: i) if rev_nb is None else (lambda i: rev_nb - 1 - i)
    full = lambda shape: pl.BlockSpec(shape, lambda i: (0,) * len(shape))
    return dict(
        proj=pl.BlockSpec((rb, RET_IN), lambda i: (blk(i), 0)),
        tab=pl.BlockSpec((rb, RET_DK // 2), lambda i: (blk(i), 0)),
        vw=pl.BlockSpec((rb, RET_V_W), lambda i: (blk(i), 0)),
        st=pl.BlockSpec((rb // RET_BLOCK, RET_HEADS, RET_DK, RET_DV), lambda i: (blk(i), 0, 0, 0)),
        gn=full((RET_HEADS, 1, RET_DV)),
        intra=full((RET_HEADS, RET_BLOCK, RET_BLOCK)),
        dec=full((RET_HEADS, RET_BLOCK, RET_DK)),
        cdec=full((RET_HEADS, 1, RET_DV)),
    )


def _ret_fwd(proj, cos, sin, gn):
    t = proj.shape[0]
    rb = min(RET_ROWS, t)
    cpb = rb // RET_BLOCK
    intra, qdec, kdec, cdec = _ret_consts()
    sp = _ret_specs(rb)

    def body(proj_ref, cos_ref, sin_ref, gn_ref, intra_ref, qd_ref, kd_ref, cd_ref,
             gated_ref, outp_ref, st_ref, s_ref):
        @pl.when(pl.program_id(0) == 0)
        def _():
            s_ref[...] = jnp.zeros_like(s_ref)

        def chunk(c, carry):
            rows = pl.ds(pl.multiple_of(c * RET_BLOCK, RET_BLOCK), RET_BLOCK)
            cs, sn = cos_ref[rows, :], sin_ref[rows, :]
            for h in range(RET_HEADS):
                q = proj_ref[rows, h * RET_DK:(h + 1) * RET_DK].astype(F32)
                k = proj_ref[rows, RET_QK_W + h * RET_DK:RET_QK_W + (h + 1) * RET_DK].astype(F32)
                v = proj_ref[rows, 2 * RET_QK_W + h * RET_DV:2 * RET_QK_W + (h + 1) * RET_DV]
                g = proj_ref[rows, 2 * RET_QK_W + RET_V_W + h * RET_DV:
                             2 * RET_QK_W + RET_V_W + (h + 1) * RET_DV].astype(F32)
                qr = _rope_half(q, cs, sn)
                kr = _rope_half(k, cs, sn) * (RET_DK ** -0.5)
                qb, kb, vb = qr.astype(BF16), kr.astype(BF16), v
                sc = _dot(qb, kb, 1, 1) * intra_ref[h]
                inner = _dot(sc.astype(BF16), vb, 1, 0)
                s_old = s_ref[h]
                sb = s_old.astype(BF16)
                st_ref[c, h] = sb
                cross = _dot((qr * qd_ref[h]).astype(BF16), sb, 1, 0)
                out = inner + cross
                s_ref[h] = s_old * cd_ref[h] + _dot((kr * kd_ref[h]).astype(BF16), vb, 0, 0)
                r = lax.rsqrt(jnp.mean(out * out, axis=-1, keepdims=True) + EPS)
                y = out * r * gn_ref[h]
                cols = slice(h * RET_DV, (h + 1) * RET_DV)
                gated_ref[rows, cols] = (g * _sigmoid(g) * y).astype(BF16)
                outp_ref[rows, cols] = out
            return carry

        lax.fori_loop(0, cpb, chunk, 0)

    return pl.pallas_call(
        body, name="ret_fwd", grid=(t // rb,),
        in_specs=[sp['proj'], sp['tab'], sp['tab'], sp['gn'], sp['intra'], sp['dec'], sp['dec'], sp['cdec']],
        out_specs=[sp['vw'], sp['vw'], sp['st']],
        out_shape=[_sds((t, RET_V_W), BF16), _sds((t, RET_V_W), F32),
                   _sds((t // RET_BLOCK, RET_HEADS, RET_DK, RET_DV), BF16)],
        scratch_shapes=[pltpu.VMEM((RET_HEADS, RET_DK, RET_DV), F32)],
        compiler_params=_cparams(("arbitrary",)),
    )(proj, cos, sin, gn.reshape(RET_HEADS, 1, RET_DV), intra, qdec, kdec, cdec)


def _ret_gate_bwd_epi(dgt, out, g, gn):
    g = g.astype(F32)
    r = lax.rsqrt(jnp.mean(out * out, axis=-1, keepdims=True) + EPS)
    xh = out * r
    sg = _sigmoid(g)
    dgate = dgt * (xh * gn) * (sg * (1.0 + g * (1.0 - sg)))
    dy = dgt * (g * sg)
    dxh = dy * gn
    dout = r * (dxh - xh * jnp.mean(dxh * xh, axis=-1, keepdims=True))
    return dout, dgate, jnp.sum(dy * xh, axis=0, keepdims=True)


def _ret_bwd(proj, cos, sin, states, dout, dgate, deps=()):
    t = proj.shape[0]
    rb = min(RET_ROWS, t)
    cpb = rb // RET_BLOCK
    nb = t // rb
    intra, qdec, kdec, cdec = _ret_consts()
    sp = _ret_specs(rb, rev_nb=nb)

    def body(proj_ref, cos_ref, sin_ref, intra_ref, qd_ref, kd_ref, cd_ref, st_ref, dout_ref, dgate_ref, *rest):
        dproj_ref, ds_ref = rest[len(deps):]

        @pl.when(pl.program_id(0) == 0)
        def _():
            ds_ref[...] = jnp.zeros_like(ds_ref)

        def chunk(cc, carry):
            c = cpb - 1 - cc
            rows = pl.ds(pl.multiple_of(c * RET_BLOCK, RET_BLOCK), RET_BLOCK)
            cs, sn = cos_ref[rows, :], sin_ref[rows, :]
            for h in range(RET_HEADS):
                q = proj_ref[rows, h * RET_DK:(h + 1) * RET_DK].astype(F32)
                k = proj_ref[rows, RET_QK_W + h * RET_DK:RET_QK_W + (h + 1) * RET_DK].astype(F32)
                v = proj_ref[rows, 2 * RET_QK_W + h * RET_DV:2 * RET_QK_W + (h + 1) * RET_DV]
                cols = slice(h * RET_DV, (h + 1) * RET_DV)
                qr = _rope_half(q, cs, sn)
                kr = _rope_half(k, cs, sn) * (RET_DK ** -0.5)
                qb, kb, vb = qr.astype(BF16), kr.astype(BF16), v
                qdb = (qr * qd_ref[h]).astype(BF16)
                kdb = (kr * kd_ref[h]).astype(BF16)
                doutb = dout_ref[rows, cols]
                itr = intra_ref[h]
                pb = (_dot(qb, kb, 1, 1) * itr).astype(BF16)
                dv = _dot(pb, doutb, 0, 0)
                dsc = (_dot(doutb, vb, 1, 1) * itr).astype(BF16)
                dq = _dot(dsc, kb, 1, 0)
                dk = _dot(dsc, qb, 0, 0)
                dq = dq + _dot(doutb, st_ref[c, h], 1, 1) * qd_ref[h]
                ds_new = ds_ref[h]
                dsb = ds_new.astype(BF16)
                dk = dk + _dot(vb, dsb, 1, 1) * kd_ref[h]
                dv = dv + _dot(kdb, dsb, 1, 0)
                ds_ref[h] = ds_new * cd_ref[h] + _dot(qdb, doutb, 0, 0)
                dproj_ref[rows, h * RET_DK:(h + 1) * RET_DK] = _rope_half(dq, cs, -sn).astype(BF16)
                dproj_ref[rows, RET_QK_W + h * RET_DK:RET_QK_W + (h + 1) * RET_DK] = (
                    _rope_half(dk * (RET_DK ** -0.5), cs, -sn).astype(BF16))
                dproj_ref[rows, 2 * RET_QK_W + h * RET_DV:2 * RET_QK_W + (h + 1) * RET_DV] = dv.astype(BF16)
                dproj_ref[rows, 2 * RET_QK_W + RET_V_W + h * RET_DV:
                          2 * RET_QK_W + RET_V_W + (h + 1) * RET_DV] = dgate_ref[rows, cols]
            return carry

        lax.fori_loop(0, cpb, chunk, 0)

    return pl.pallas_call(
        body, name="ret_bwd", grid=(nb,),
        in_specs=[sp['proj'], sp['tab'], sp['tab'], sp['intra'], sp['dec'], sp['dec'], sp['cdec'],
                  sp['st'], sp['vw'], sp['vw']] + [ANY] * len(deps),
        out_specs=sp['proj'],
        out_shape=_sds((t, RET_IN), BF16),
        scratch_shapes=[pltpu.VMEM((RET_HEADS, RET_DK, RET_DV), F32)],
        compiler_params=_cparams(("arbitrary",)),
    )(proj, cos, sin, intra, qdec, kdec, cdec, states, dout, dgate, *deps)


def _spread_rope(a):
    return jnp.pad(a, [(0, 0)] * (a.ndim - 1) + [(0, MLA_ROPE)])


def _gather_rope(a):
    return a[..., :a.shape[-1] - MLA_ROPE]


def _mla_tables(t):
    half = MLA_ROPE // 2
    inv = 1.0 / (ROPE_THETA ** (jnp.arange(0, MLA_ROPE, 2, dtype=F32) / MLA_ROPE))
    ang = jnp.arange(t, dtype=F32)[:, None] * inv[None, :]
    cos, sin = jnp.cos(ang), jnp.sin(ang)
    z = jnp.zeros((t, half), F32)
    c = jnp.concatenate([cos, cos, z, z], axis=1)
    s1 = jnp.concatenate([-sin, z, z, z], axis=1)
    s2 = jnp.concatenate([z, sin, z, z], axis=1)
    return c, s1, s2


def _rope_tile(r, c, s1, s2):
    return r * c + pltpu.roll(r, 96, 1) * s1 + pltpu.roll(r, 32, 1) * s2


def _mla_mid(proj2, qa, kva):
    t = proj2.shape[0]
    tm = _row_tile(t)

    def body(p_ref, qa_ref, kva_ref, cq_ref, ckv_ref):
        cq = p_ref[:, :MLA_Q_RANK]
        ckv = p_ref[:, MLA_Q_RANK:MLA_Q_RANK + MLA_KV_RANK]
        rq = lax.rsqrt(jnp.mean(cq * cq, axis=-1, keepdims=True) + EPS)
        rkv = lax.rsqrt(jnp.mean(ckv * ckv, axis=-1, keepdims=True) + EPS)
        cq_ref[...] = (cq * rq * qa_ref[...]).astype(BF16)
        ckv_ref[...] = (ckv * rkv * kva_ref[...]).astype(BF16)

    return pl.pallas_call(
        body, name="mla_mid", grid=(t // tm,),
        in_specs=[pl.BlockSpec((tm, MLA_IN_PAD), lambda i: (i, 0)),
                  pl.BlockSpec((1, MLA_Q_RANK), lambda i: (0, 0)),
                  pl.BlockSpec((1, MLA_KV_RANK), lambda i: (0, 0))],
        out_specs=[pl.BlockSpec((tm, MLA_Q_RANK), lambda i: (i, 0)),
                   pl.BlockSpec((tm, MLA_KV_RANK), lambda i: (i, 0))],
        out_shape=[_sds((t, MLA_Q_RANK), BF16), _sds((t, MLA_KV_RANK), BF16)],
        compiler_params=_cparams(("parallel",)),
    )(proj2, qa, kva)


def _mla_mid_bwd(proj2, qa, kva, dcq, dckv, dkr):
    t = proj2.shape[0]
    tm = _row_tile(t)

    def body(p_ref, qa_ref, kva_ref, dcq_ref, dckv_ref, dkr_ref, dp_ref, dqa_ref, dkva_ref):
        @pl.when(pl.program_id(0) == 0)
        def _():
            dqa_ref[...] = jnp.zeros_like(dqa_ref)
            dkva_ref[...] = jnp.zeros_like(dkva_ref)

        dxq, dgq = _rms_bwd_rows(dcq_ref[...], p_ref[:, :MLA_Q_RANK], qa_ref[...], MLA_Q_RANK)
        dxk, dgk = _rms_bwd_rows(dckv_ref[...], p_ref[:, MLA_Q_RANK:MLA_Q_RANK + MLA_KV_RANK], kva_ref[...],
                                 MLA_KV_RANK)
        dp_ref[:, :MLA_Q_RANK] = dxq.astype(BF16)
        dp_ref[:, MLA_Q_RANK:MLA_Q_RANK + MLA_KV_RANK] = dxk.astype(BF16)
        dp_ref[:, MLA_Q_RANK + MLA_KV_RANK:] = dkr_ref[...].astype(BF16)
        dqa_ref[...] += jnp.sum(dgq, axis=0, keepdims=True)
        dkva_ref[...] += jnp.sum(dgk, axis=0, keepdims=True)

    return pl.pallas_call(
        body, name="mla_mid_bwd", grid=(t // tm,),
        in_specs=[pl.BlockSpec((tm, MLA_IN_PAD), lambda i: (i, 0)),
                  pl.BlockSpec((1, MLA_Q_RANK), lambda i: (0, 0)),
                  pl.BlockSpec((1, MLA_KV_RANK), lambda i: (0, 0)),
                  pl.BlockSpec((tm, MLA_Q_RANK), lambda i: (i, 0)),
                  pl.BlockSpec((tm, MLA_KV_RANK), lambda i: (i, 0)),
                  pl.BlockSpec((tm, 128), lambda i: (i, 0))],
        out_specs=[pl.BlockSpec((tm, MLA_IN_PAD), lambda i: (i, 0)),
                   pl.BlockSpec((1, MLA_Q_RANK), lambda i: (0, 0)),
                   pl.BlockSpec((1, MLA_KV_RANK), lambda i: (0, 0))],
        out_shape=[_sds((t, MLA_IN_PAD), BF16), _sds((1, MLA_Q_RANK), F32), _sds((1, MLA_KV_RANK), F32)],
        compiler_params=_cparams(("arbitrary",)),
    )(proj2, qa, kva, dcq, dckv, dkr)


def _mla_prep_specs(t, tm):
    head = lambda w: pl.BlockSpec((None, tm, w), lambda i, h: (h, i, 0))
    return dict(
        head256=head(MLA_HD_PAD), head128=head(MLA_VD),
        cols256=pl.BlockSpec((tm, MLA_HD_PAD), lambda i, h: (i, h)),
        cq=pl.BlockSpec((tm, MLA_Q_RANK), lambda i, h: (i, 0)),
        ckv=pl.BlockSpec((tm, MLA_KV_RANK), lambda i, h: (i, 0)),
        wuq=pl.BlockSpec((None, MLA_Q_RANK, MLA_HD_PAD), lambda i, h: (h, 0, 0)),
        wukv=pl.BlockSpec((None, MLA_KV_RANK, MLA_HD_PAD), lambda i, h: (h, 0, 0)),
        kr=pl.BlockSpec((tm, 128), lambda i, h: (i, (MLA_Q_RANK + MLA_KV_RANK) // 128)),
        gain=pl.BlockSpec((1, MLA_HD_PAD), lambda i, h: (0, 0)),
        tab=pl.BlockSpec((tm, 128), lambda i, h: (i, 0)),
    )


def _mla_prep(cq, ckv, wuq, wukv, proj2, gq, gk, tabs):
    t = cq.shape[0]
    tm = _row_tile(t, PREP_ROWS)
    sp = _mla_prep_specs(t, tm)

    def body(cq_ref, ckv_ref, wuq_ref, wukv_ref, kr_ref, gq_ref, gk_ref, c_ref, s1_ref, s2_ref,
             qh_ref, kh_ref, vh_ref):
        c, s1, s2 = c_ref[...], s1_ref[...], s2_ref[...]

        def norm_rope(xv, gain):
            r = lax.rsqrt(jnp.sum(xv * xv, axis=-1, keepdims=True) / MLA_QKD + EPS)
            y = xv * r * gain
            return jnp.concatenate([y[:, :MLA_NOPE], _rope_tile(y[:, MLA_NOPE:], c, s1, s2)], axis=-1)

        kvv = _dot(ckv_ref[...], wukv_ref[...], 1, 0)
        qh_ref[...] = norm_rope(_dot(cq_ref[...], wuq_ref[...], 1, 0), gq_ref[...]).astype(BF16)
        kf = jnp.concatenate([kvv[:, :MLA_NOPE], kr_ref[...]], axis=-1)
        kh_ref[...] = norm_rope(kf, gk_ref[...]).astype(BF16)
        vh_ref[...] = jnp.concatenate([kvv[:, MLA_NOPE:], jnp.ones((tm, MLA_VD), F32)], axis=-1).astype(BF16)

    return pl.pallas_call(
        body, name="mla_prep", grid=(t // tm, MLA_HEADS),
        in_specs=[sp['cq'], sp['ckv'], sp['wuq'], sp['wukv'], sp['kr'], sp['gain'], sp['gain'],
                  sp['tab'], sp['tab'], sp['tab']],
        out_specs=[sp['head256'], sp['head256'], sp['head256']],
        out_shape=[_sds((MLA_HEADS, t, MLA_HD_PAD), BF16), _sds((MLA_HEADS, t, MLA_HD_PAD), BF16),
                   _sds((MLA_HEADS, t, 2 * MLA_VD), BF16)],
        compiler_params=_cparams(("parallel", "arbitrary")),
    )(cq, ckv, wuq, wukv, proj2, gq, gk, *tabs)


def _mla_prep_bwd(cq, ckv, wuq, wukv, proj2, gq, gk, tabs, dqt, dkh, dvh):
    t = cq.shape[0]
    tm = _row_tile(t, PREP_ROWS)
    ab = dqt.shape[-1]
    sp = _mla_prep_specs(t, tm)

    def body(cq_ref, ckv_ref, wuq_ref, wukv_ref, kr_ref, gq_ref, gk_ref, c_ref, s1_ref, s2_ref,
             dqt_ref, dkh_ref, dvh_ref, dq_ref, dkv_ref, dkr_ref, dgq_ref, dgk_ref):
        dqh = jnp.concatenate([dqt_ref[b].T for b in range(tm // ab)], axis=0)
        i, h = pl.program_id(0), pl.program_id(1)

        @pl.when((i == 0) & (h == 0))
        def _():
            dgq_ref[...] = jnp.zeros_like(dgq_ref)
            dgk_ref[...] = jnp.zeros_like(dgk_ref)

        @pl.when(h == 0)
        def _():
            dkr_ref[...] = jnp.zeros_like(dkr_ref)

        c, s1, s2 = c_ref[...], s1_ref[...], s2_ref[...]

        def back(xv, gain, dout):
            dy = jnp.concatenate([dout[:, :MLA_NOPE], _rope_tile(dout[:, MLA_NOPE:], c, -s1, -s2)], axis=-1)
            return _rms_bwd_rows(dy, xv, gain, MLA_QKD)

        kvv = _dot(ckv_ref[...], wukv_ref[...], 1, 0)
        dxq, dgq = back(_dot(cq_ref[...], wuq_ref[...], 1, 0), gq_ref[...], dqh)
        kf = jnp.concatenate([kvv[:, :MLA_NOPE], kr_ref[...]], axis=-1)
        dxk, dgk = back(kf, gk_ref[...], dkh_ref[...])
        dq_ref[...] = dxq.astype(BF16)
        dkv_ref[...] = jnp.concatenate([dxk[:, :MLA_NOPE], dvh_ref[...]], axis=-1).astype(BF16)
        dkr_ref[...] += dxk[:, MLA_NOPE:]
        dgq_ref[...] += jnp.sum(dgq, axis=0, keepdims=True)
        dgk_ref[...] += jnp.sum(dgk, axis=0, keepdims=True)

    return pl.pallas_call(
        body, name="mla_prep_bwd", grid=(t // tm, MLA_HEADS),
        in_specs=[sp['cq'], sp['ckv'], sp['wuq'], sp['wukv'], sp['kr'], sp['gain'], sp['gain'],
                  sp['tab'], sp['tab'], sp['tab'],
                  pl.BlockSpec((None, tm // ab, MLA_HD_PAD, ab), lambda i, h: (h, i, 0, 0)),
                  sp['head256'], sp['head128']],
        out_specs=[sp['cols256'], sp['cols256'], sp['tab'], sp['gain'], sp['gain']],
        out_shape=[_sds((t, MLA_HEADS * MLA_HD_PAD), BF16), _sds((t, MLA_HEADS * MLA_HD_PAD), BF16),
                   _sds((t, 128), F32), _sds((1, MLA_HD_PAD), F32), _sds((1, MLA_HD_PAD), F32)],
        compiler_params=_cparams(("arbitrary", "arbitrary")),
    )(cq, ckv, wuq, wukv, proj2, gq, gk, *tabs, dqt, dkh, dvh)


def _chunk_visible(rows, cols, row_off, col_off):
    rq = lax.shift_right_logical(lax.broadcasted_iota(jnp.int32, (rows, cols), 0) + row_off, 6)
    ck = lax.shift_right_logical(lax.broadcasted_iota(jnp.int32, (rows, cols), 1) + col_off, 6)
    return ck <= rq


def _rows_to_lanes(col):
    return col.T[:8, :]


def _attn_fwd(qh, kh, vh):
    t = qh.shape[1]
    ab = min(ATT_BLOCK, t)
    tq = min(ATT_QROWS, t)
    r = tq // ab
    hg = ATT_HEADS

    def body(q_ref, k_ref, v_ref, o_ref, lse_ref, acc_ref):
        n_un = pl.program_id(1) * r
        acc_ref[...] = jnp.zeros_like(acc_ref)

        def step(b, ms, diag):
            rows = pl.ds(pl.multiple_of(b * ab, ab), ab)
            out = []
            for hh in range(hg):
                m = ms[hh]
                s = _dot(q_ref[hh], k_ref[hh, rows, :], 1, 1)
                if diag is not None:
                    s = jnp.where(_chunk_visible(tq, ab, 0, diag * ab), s, -1e30)
                m_new = jnp.maximum(m, jnp.max(s, axis=-1, keepdims=True))
                p = jnp.exp2((s - m_new) * ATT_EXP2).astype(BF16)
                acc_ref[hh] = jnp.exp2((m - m_new) * ATT_EXP2) * acc_ref[hh] + _dot(p, v_ref[hh, rows, :], 1, 0)
                out.append(m_new)
            return tuple(out)

        ms = tuple(jnp.full((tq, 1), -1e30, F32) for _ in range(hg))
        ms = lax.fori_loop(0, n_un, lambda b, st: step(b, st, None), ms)
        for d in range(r):
            ms = step(n_un + d, ms, d)
        for hh in range(hg):
            l = acc_ref[hh, :, MLA_VD:]
            o_ref[:, hh * MLA_VD:(hh + 1) * MLA_VD] = acc_ref[hh, :, :MLA_VD] / l
            lse_t = _rows_to_lanes(ms[hh] * ATT_EXP2 + jnp.log(l) * LOG2E)
            for d in range(r):
                lse_ref[hh, d] = lse_t[:, d * ab:(d + 1) * ab]

    return pl.pallas_call(
        body, name="mla_attn", grid=(MLA_HEADS // hg, t // tq),
        in_specs=[pl.BlockSpec((hg, tq, MLA_HD_PAD), lambda g, i: (g, i, 0)),
                  pl.BlockSpec((hg, t, MLA_HD_PAD), lambda g, i: (g, 0, 0)),
                  pl.BlockSpec((hg, t, 2 * MLA_VD), lambda g, i: (g, 0, 0))],
        out_specs=[pl.BlockSpec((tq, hg * MLA_VD), lambda g, i: (i, g)),
                   pl.BlockSpec((hg, r, 8, ab), lambda g, i: (g, i, 0, 0))],
        out_shape=[_sds((t, MLA_HEADS * MLA_VD), F32), _sds((MLA_HEADS, t // ab, 8, ab), F32)],
        scratch_shapes=[pltpu.VMEM((hg, tq, 2 * MLA_VD), F32)],
        compiler_params=_cparams(("parallel", "arbitrary")),
    )(qh, kh, vh)


def _attn_delta(do, o, ab):
    t = do.shape[0]
    tm = _row_tile(t)

    def body(do_ref, o_ref, d_ref):
        d = jnp.sum(do_ref[...] * o_ref[...], axis=-1, keepdims=True)
        d_t = _rows_to_lanes(jnp.broadcast_to(d, (tm, 128)))
        for b in range(tm // ab):
            d_ref[b] = d_t[:, b * ab:(b + 1) * ab]

    col = pl.BlockSpec((tm, MLA_VD), lambda i, h: (i, h))
    return pl.pallas_call(
        body, name="mla_delta", grid=(t // tm, MLA_HEADS), in_specs=[col, col],
        out_specs=pl.BlockSpec((None, tm // ab, 8, ab), lambda i, h: (h, i, 0, 0)),
        out_shape=_sds((MLA_HEADS, t // ab, 8, ab), F32),
        compiler_params=_cparams(("parallel", "parallel")),
    )(do, o)


def _attn_bwd(qh, kh, vh, dob, lse_t, dl_t):
    t = qh.shape[1]
    ab = min(ATT_BLOCK, t)
    kb = min(ATT_KROWS, t)
    r = kb // ab
    nq = t // ab
    hg = ATT_HEADS

    def body(q_ref, k_ref, v_ref, do_ref, lse_ref, dl_ref, dqt_ref, dk_ref, dv_ref):
        j = pl.program_id(1)

        @pl.when(j == 0)
        def _():
            dqt_ref[...] = jnp.zeros_like(dqt_ref)

        ks = [k_ref[hh] for hh in range(hg)]
        vs = [v_ref[hh, :, :MLA_VD] for hh in range(hg)]
        kts = [k.T for k in ks]

        dk_ref[...] = jnp.zeros_like(dk_ref)
        dv_ref[...] = jnp.zeros_like(dv_ref)

        def step(b, carry, diag):
            rows = pl.ds(pl.multiple_of(b * ab, ab), ab)
            hi = kb if diag is None else (diag + 1) * ab
            for hh in range(hg):
                q = q_ref[hh, rows, :]
                do = do_ref[rows, hh * MLA_VD:(hh + 1) * MLA_VD]
                s_t = _dot(ks[hh][:hi], q, 1, 1)
                if diag is not None:
                    key_chunk = lax.shift_right_logical(lax.broadcasted_iota(jnp.int32, (hi, ab), 0), 6)
                    query_chunk = lax.shift_right_logical(
                        lax.broadcasted_iota(jnp.int32, (hi, ab), 1) + diag * ab, 6)
                    s_t = jnp.where(key_chunk <= query_chunk, s_t, -1e30)
                p_t = jnp.exp2(s_t * ATT_EXP2 - lse_ref[hh, b][0:1, :])
                dp_t = _dot(vs[hh][:hi], do, 1, 1)
                ds_t = (p_t * (dp_t - dl_ref[hh, b][0:1, :]) * ATT_SCALE).astype(BF16)
                dqt_ref[hh, b] += _dot(kts[hh][:, :hi], ds_t, 1, 0)
                dk_ref[hh, :hi] += _dot(ds_t, q, 1, 0)
                dv_ref[hh, :hi] += _dot(p_t.astype(BF16), do, 1, 0)
            return carry

        for d in range(r):
            step(j * r + d, 0, d)
        lax.fori_loop((j + 1) * r, nq, lambda b, c: step(b, c, None), 0)

    whole = lambda w: pl.BlockSpec((hg, t, w), lambda g, j: (g, 0, 0))
    blk = lambda w: pl.BlockSpec((hg, kb, w), lambda g, j: (g, j, 0))
    stat = pl.BlockSpec((hg, nq, 8, ab), lambda g, j: (g, 0, 0, 0))
    return pl.pallas_call(
        body, name="mla_attn_bwd", grid=(MLA_HEADS // hg, t // kb),
        in_specs=[whole(MLA_HD_PAD), blk(MLA_HD_PAD), blk(2 * MLA_VD),
                  pl.BlockSpec((t, hg * MLA_VD), lambda g, j: (0, g)), stat, stat],
        out_specs=[pl.BlockSpec((hg, nq, MLA_HD_PAD, ab), lambda g, j: (g, 0, 0, 0)), blk(MLA_HD_PAD), blk(MLA_VD)],
        out_shape=[_sds((MLA_HEADS, nq, MLA_HD_PAD, ab), F32), _sds((MLA_HEADS, t, MLA_HD_PAD), F32),
                   _sds((MLA_HEADS, t, MLA_VD), F32)],
        compiler_params=_cparams(("parallel", "arbitrary")),
    )(qh, kh, vh, dob, lse_t, dl_t)


VEC = pl.BlockSpec((1, D_MODEL), lambda i, j, k: (0, 0))


def _rows(tm, width):
    return pl.BlockSpec((tm, width), lambda i, j, k: (i, 0))


def _residual_epi(next_gain):
    if next_gain is None:
        return [], lambda acc, hv: (acc + hv,)

    def epi(acc, hv, g):
        h_new = acc + hv
        r = lax.rsqrt(jnp.mean(h_new * h_new, axis=-1, keepdims=True) + EPS)
        return h_new, h_new * r * g

    return [(next_gain, VEC)], epi


def _residual_outs(t, row, next_gain):
    outs = [(_sds((t, D_MODEL), F32), row)]
    return outs + ([(_sds((t, D_MODEL), BF16), row)] if next_gain is not None else [])


def _mlp_fwd(l, h, hn, w1g, fetch_w2, next_gain):
    t = h.shape[0]
    tm = _row_tile(t, 512)

    def relu2(acc):
        r = jnp.maximum(acc, 0.0)
        return (r * r,)

    (u,) = _mm_rows(f"mlp_up{l}", tm, hn, w1g, 'nn_cols', [(_sds((t, D_FF), BF16), _rows(tm, D_FF))], epi=relu2)
    w2g = fetch_w2((u,))
    row = _rows(tm, D_MODEL)
    more, epi = _residual_epi(next_gain)
    h2, hn_next = _mm_rows(f"mlp_down{l}", tm, u, w2g, 'nn_rows', _residual_outs(t, row, next_gain),
                           extras=[(h, row)] + more, epi=epi)
    return h2, hn_next, (h, hn, u, w1g, w2g)


def _norm_bwd_outs(t, tm):
    return [(_sds((t, D_MODEL), F32), pl.BlockSpec((tm, D_MODEL), lambda i, j, k: (i, 0))),
            (_sds((t // tm, 1, D_MODEL), F32), pl.BlockSpec((None, 1, D_MODEL), lambda i, j, k: (i, 0, 0)))]


def _norm_bwd_epi(acc, xv, res, g):
    dx, dgr = _rms_bwd_rows(acc, xv, g, D_MODEL)
    return res + dx, jnp.sum(dgr, axis=0, keepdims=True)


def _mlp_bwd(l, dh, saved, norm_g):
    h, hn, u, w1g, w2g = saved
    t = h.shape[0]
    tm = _row_tile(t, 512)
    nsh, _, wsh = w1g.shape
    wide = _rows(tm, D_FF)
    (da,) = _mm_rows(f"mlp_du{l}", tm, dh, w2g, 'nt_rows', [(_sds((t, D_FF), BF16), wide)], extras=[(u, wide)],
                     epi=lambda acc, uv: (2.0 * jnp.sqrt(uv.astype(F32)) * acc,))
    tw = _row_tile(t, 512)
    (dw2,) = _mm(f"mlp_dw2{l}", (1, 1, t // tw),
                 u, pl.BlockSpec((tw, D_FF), lambda i, j, k: (k, 0)),
                 dh, pl.BlockSpec((tw, D_MODEL), lambda i, j, k: (k, 0)), (0, 0),
                 [(_sds((D_FF, D_MODEL), BF16), pl.BlockSpec((D_FF, D_MODEL), lambda i, j, k: (0, 0)))])
    dw2 = dw2.reshape(nsh, wsh, D_MODEL)
    (dw1,) = _mm(f"mlp_dw1{l}", (1, 1, t // tw),
                 hn, pl.BlockSpec((tw, D_MODEL), lambda i, j, k: (k, 0)),
                 da, pl.BlockSpec((tw, D_FF), lambda i, j, k: (k, 0)), (0, 0),
                 [(_sds((nsh, D_MODEL, wsh), BF16), pl.BlockSpec((nsh, D_MODEL, wsh), lambda i, j, k: (0, 0, 0)))],
                 split=wsh)
    row = _rows(tm, D_MODEL)
    dh_in, dg = _mm_rows(f"mlp_dhn{l}", tm, da, w1g, 'nt_cols', _norm_bwd_outs(t, tm),
                         extras=[(h, row), (dh, row), (norm_g, VEC)], epi=_norm_bwd_epi)
    return dh_in, jnp.sum(dg, axis=0), dw1, dw2


def _ple_fwd(l, h, hn, p, wg, wp, next_gain, target=None):
    t = h.shape[0]
    tm = _row_tile(t, 512)
    row = pl.BlockSpec((tm, D_MODEL), lambda i, j, k: (i, 0))
    full = lambda r: pl.BlockSpec((r, D_MODEL), lambda i, j, k: (0, 0))
    f32_row, bf_row = (_sds((t, D_MODEL), F32), row), (_sds((t, D_MODEL), BF16), row)
    common = [(h, row), (p, pl.BlockSpec((None, None, tm, PLE_DIM), lambda i, j, k: (l, 0, i, 0))),
              (wp, full(PLE_DIM))]
    if target is not None:
        def loss_epi(acc, hv, pv, wpv, tv):
            gt = _sigmoid(acc)
            ev = _dot(_bf(pv), wpv, 1, 0)
            err = hv + gt * ev - tv
            sq = jnp.sum(jnp.sum(err * err, axis=-1, keepdims=True), axis=0, keepdims=True)
            return err / D_MODEL, gt, ev, jnp.broadcast_to(sq, (8, 128))

        dy, gate, e, sq = _mm(f"ple_gate{l}", (t // tm, 1, 1), hn, row, wg, full(D_MODEL), (1, 0),
                              [f32_row, bf_row, bf_row, (_sds((t // tm, 8, 128), F32),
                                                         pl.BlockSpec((None, 8, 128), lambda i, j, k: (i, 0, 0)))],
                              extras=common + [(target, row)], epi=loss_epi)
        return dy, jnp.sum(sq, axis=0), (h, hn, gate, e)

    def gate_epi(acc, hv, pv, wpv, *gain):
        gt = _sigmoid(acc)
        ev = _dot(_bf(pv), wpv, 1, 0)
        h_new = hv + gt * ev
        if not gain:
            return h_new, gt, ev
        r = lax.rsqrt(jnp.mean(h_new * h_new, axis=-1, keepdims=True) + EPS)
        return h_new, gt, ev, h_new * r * gain[0]

    res = _mm(f"ple_gate{l}", (t // tm, 1, 1), hn, row, wg, full(D_MODEL), (1, 0),
              [f32_row, bf_row, bf_row] + ([bf_row] if next_gain is not None else []),
              extras=common + ([(next_gain, VEC)] if next_gain is not None else []), epi=gate_epi)
    h_out, gate, e = res[0], res[1], res[2]
    return h_out, (res[3] if next_gain is not None else None), (h, hn, gate, e)


def _ple_bwd(l, dh, saved, p, norm_g, wg, deps=()):
    h, hn, gate, e = saved
    t = h.shape[0]
    tm = _row_tile(t)
    tk = _row_tile(t, 512)
    de, dz = _ple_gate_bwd(f"ple_gate_bwd{l}", dh, gate, e)
    full = lambda r: pl.BlockSpec((r, D_MODEL), lambda i, j, k: (0, 0))
    rowk = pl.BlockSpec((tk, D_MODEL), lambda i, j, k: (k, 0))
    (dwp,) = _mm(f"ple_dwp{l}", (1, 1, t // tk),
                 p, pl.BlockSpec((None, None, tk, PLE_DIM), lambda i, j, k: (l, 0, k, 0)),
                 de, rowk, (0, 0), [(_sds((PLE_DIM, D_MODEL), BF16), full(PLE_DIM))], deps=deps)
    (dwg,) = _mm(f"ple_dwg{l}", (1, 1, t // tk), hn, rowk, dz, rowk, (0, 0),
                 [(_sds((D_MODEL, D_MODEL), BF16), full(D_MODEL))])
    row = pl.BlockSpec((tm, D_MODEL), lambda i, j, k: (i, 0))
    dh_in, dg = _mm(f"ple_dhn{l}", (t // tm, 1, 1), dz, row, wg, full(D_MODEL), (1, 1),
                    _norm_bwd_outs(t, tm), extras=[(h, row), (dh, row), (norm_g, VEC)], epi=_norm_bwd_epi)
    return dh_in, jnp.sum(dg, axis=0), dwg, dwp


def _ret_layer_fwd(x, norm_g, wri, fetch_wro, gn, cos, sin, next_gain, hn=None, deps=()):
    t = x.shape[0]
    tm = _row_tile(t)
    nsh, _, wsh = wri.shape
    if hn is None:
        hn = _rms_fwd("mix_norm0", x, norm_g)
    tp = _row_tile(t, 512)
    (proj,) = _mm_rows("ret_in", tp, hn, wri, 'nn_cols', [(_sds((t, RET_IN), BF16), _rows(tp, RET_IN))], deps=deps)
    gated, outp, states = _ret_fwd(proj, cos, sin, gn)
    wro = fetch_wro((gated,))
    row = _rows(tp, D_MODEL)
    more, epi = _residual_epi(next_gain)
    h1, hn_next = _mm_rows("ret_out", tp, gated, wro.reshape(RET_HEADS, RET_DV, D_MODEL), 'nn_rows',
                           _residual_outs(t, row, next_gain), extras=[(x, row)] + more, epi=epi)
    return h1, hn_next, (x, hn, proj, gated, outp, states, wro)


def _ret_layer_bwd(dh, saved, norm_g, wri, gn, cos, sin, emit_out, emit_in, deps=()):
    x, hn, proj, gated, outp, states, wro = saved
    t = x.shape[0]
    tm = _row_tile(t)
    tk = _row_tile(t, 512)
    nsh, _, wsh = wri.shape
    tg = _row_tile(t, 512)
    vw = _rows(tg, RET_V_W)
    dout, dgate, dgn = _mm_rows(
        "ret_dgate", tg, dh, wro.reshape(RET_HEADS, RET_DV, D_MODEL), 'nt_rows',
        [(_sds((t, RET_V_W), BF16), vw), (_sds((t, RET_V_W), BF16), vw),
         (_sds((t // tg, 1, RET_V_W), F32), pl.BlockSpec((None, 1, RET_V_W), lambda i, j, k: (i, 0, 0)))],
        extras=[(outp, vw), (proj, pl.BlockSpec((tg, RET_V_W), lambda i, j, k: (i, (RET_IN - RET_V_W) // RET_V_W))),
                (gn.reshape(1, RET_V_W), pl.BlockSpec((1, RET_V_W), lambda i, j, k: (0, 0)))],
        epi=_ret_gate_bwd_epi, deps=deps)
    dgn = jnp.sum(dgn, axis=0)
    (dwro,) = _mm("ret_dwro", (1, 1, t // tk),
                  gated, pl.BlockSpec((tk, RET_V_W), lambda i, j, k: (k, 0)),
                  dh, pl.BlockSpec((tk, D_MODEL), lambda i, j, k: (k, 0)), (0, 0),
                  [(_sds((RET_V_W, D_MODEL), BF16), pl.BlockSpec((RET_V_W, D_MODEL), lambda i, j, k: (0, 0)))])
    dproj = _ret_bwd(proj, cos, sin, states, dout, dgate, deps=emit_out(dwro))
    half = nsh // 2
    (dwri,) = _mm("ret_dwri", (2, 1, t // tk),
                  hn, pl.BlockSpec((tk, D_MODEL), lambda i, j, k: (k, 0)),
                  dproj, pl.BlockSpec((tk, half * wsh), lambda i, j, k: (k, i)), (0, 0),
                  [(_sds((nsh, D_MODEL, wsh), BF16), pl.BlockSpec((half, D_MODEL, wsh), lambda i, j, k: (i, 0, 0)))],
                  split=wsh)
    deps = emit_in(dwri)
    td = _row_tile(t, 256)
    row = _rows(td, D_MODEL)
    dx, dg = _mm_rows("ret_dhn", td, dproj, wri, 'nt_cols', _norm_bwd_outs(t, td),
                      extras=[(x, row), (dh, row), (norm_g, VEC)], epi=_norm_bwd_epi, deps=deps)
    return dx, jnp.sum(dg, axis=0), dgn.reshape(RET_HEADS, RET_DV)


def _mla_layer_fwd(h, hn, wmi, qa, kva, wuq, wukv, gq, gk, wmo, tabs, next_gain):
    t = h.shape[0]
    tm = _row_tile(t)
    row = pl.BlockSpec((tm, D_MODEL), lambda i, j, k: (i, 0))
    (proj2,) = _mm("mla_in", (t // tm, 1, 1), hn, row,
                   wmi, pl.BlockSpec((D_MODEL, MLA_IN_PAD), lambda i, j, k: (0, 0)), (1, 0),
                   [(_sds((t, MLA_IN_PAD), F32), pl.BlockSpec((tm, MLA_IN_PAD), lambda i, j, k: (i, 0)))])
    cq, ckv = _mla_mid(proj2, qa, kva)
    qh, kh, vh = _mla_prep(cq, ckv, wuq, wukv, proj2, gq, gk, tabs)
    o, lse = _attn_fwd(qh, kh, vh)
    more, epi = _residual_epi(next_gain)
    h_out, hn_next = _mm("mla_out", (t // tm, 1, 1), o, row,
                         wmo, pl.BlockSpec((D_MODEL, D_MODEL), lambda i, j, k: (0, 0)), (1, 0),
                         _residual_outs(t, row, next_gain), extras=[(h, row)] + more, epi=epi)
    return h_out, hn_next, (h, hn, proj2, cq, ckv, qh, kh, vh, o, lse)


def _mla_layer_bwd(dh, saved, norm_g, wmi, qa, kva, wuq, wukv, gq, gk, wmo, tabs, deps=()):
    h, hn, proj2, cq, ckv, qh, kh, vh, o, lse = saved
    t = h.shape[0]
    tm = _row_tile(t)
    tk = _row_tile(t, 512)
    row = pl.BlockSpec((tm, D_MODEL), lambda i, j, k: (i, 0))
    rowk = pl.BlockSpec((tk, D_MODEL), lambda i, j, k: (k, 0))
    sq = pl.BlockSpec((D_MODEL, D_MODEL), lambda i, j, k: (0, 0))
    do, dob = _mm("mla_do", (t // tm, 1, 1), dh, row, wmo, sq, (1, 1),
                  [(_sds((t, D_MODEL), F32), row), (_sds((t, D_MODEL), BF16), row)], epi=lambda acc: (acc, acc),
                  deps=deps)
    (dwmo,) = _mm("mla_dwo", (1, 1, t // tk), o, rowk, dh, rowk, (0, 0), [(_sds((D_MODEL, D_MODEL), BF16), sq)])
    delta = _attn_delta(do, o, lse.shape[-1])
    dqt, dkh, dvh = _attn_bwd(qh, kh, vh, dob, lse, delta)
    dq, dkv, dkr, dgq, dgk = _mla_prep_bwd(cq, ckv, wuq, wukv, proj2, gq, gk, tabs, dqt, dkh, dvh)

    wide = MLA_HEADS * MLA_HD_PAD
    widek = pl.BlockSpec((tk, wide), lambda i, j, k: (k, 0))
    (dwuq,) = _mm("mla_dwuq", (1, 1, t // tk),
                  cq, pl.BlockSpec((tk, MLA_Q_RANK), lambda i, j, k: (k, 0)), dq, widek, (0, 0),
                  [(_sds((MLA_HEADS, MLA_Q_RANK, MLA_HD_PAD), BF16),
                    pl.BlockSpec((MLA_HEADS, MLA_Q_RANK, MLA_HD_PAD), lambda i, j, k: (0, 0, 0)))], split=MLA_HD_PAD)
    (dwukv,) = _mm("mla_dwukv", (1, 1, t // tk),
                   ckv, pl.BlockSpec((tk, MLA_KV_RANK), lambda i, j, k: (k, 0)), dkv, widek, (0, 0),
                   [(_sds((MLA_HEADS, MLA_KV_RANK, MLA_HD_PAD), BF16),
                     pl.BlockSpec((MLA_HEADS, MLA_KV_RANK, MLA_HD_PAD), lambda i, j, k: (0, 0, 0)))],
                   split=MLA_HD_PAD)
    side_by_side = lambda wg: wg.transpose(1, 0, 2).reshape(wg.shape[1], wide)
    widei = pl.BlockSpec((tm, wide), lambda i, j, k: (i, 0))
    (dcq,) = _mm("mla_dcq", (t // tm, 1, 1), dq, widei,
                 side_by_side(wuq), pl.BlockSpec((MLA_Q_RANK, wide), lambda i, j, k: (0, 0)), (1, 1),
                 [(_sds((t, MLA_Q_RANK), F32), pl.BlockSpec((tm, MLA_Q_RANK), lambda i, j, k: (i, 0)))])
    (dckv,) = _mm("mla_dckv", (t // tm, 1, 1), dkv, widei,
                  side_by_side(wukv), pl.BlockSpec((MLA_KV_RANK, wide), lambda i, j, k: (0, 0)), (1, 1),
                  [(_sds((t, MLA_KV_RANK), F32), pl.BlockSpec((tm, MLA_KV_RANK), lambda i, j, k: (i, 0)))])
    dproj2, dqa, dkva = _mla_mid_bwd(proj2, qa, kva, dcq, dckv, dkr)
    win = pl.BlockSpec((D_MODEL, MLA_IN_PAD), lambda i, j, k: (0, 0))
    (dwmi,) = _mm("mla_dwin", (1, 1, t // tk), hn, rowk,
                  dproj2, pl.BlockSpec((tk, MLA_IN_PAD), lambda i, j, k: (k, 0)), (0, 0),
                  [(_sds((D_MODEL, MLA_IN_PAD), BF16), win)])
    dh_in, dg = _mm("mla_dhn", (t // tm, 1, 1),
                    dproj2, pl.BlockSpec((tm, MLA_IN_PAD), lambda i, j, k: (i, 0)), wmi, win, (1, 1),
                    _norm_bwd_outs(t, tm), extras=[(h, row), (dh, row), (norm_g, VEC)], epi=_norm_bwd_epi)
    return dh_in, dict(mix=jnp.sum(dg, axis=0), wmi=dwmi, qa=dqa, kva=dkva, wuq=dwuq, wukv=dwukv, gq=dgq, gk=dgk,
                       wmo=dwmo)


def _local_step(x, p, target, w, fetch, emit=lambda group: ()):
    t = x.shape[0]
    inv = 1.0 / (ROPE_THETA ** (jnp.arange(0, RET_DK, 2, dtype=F32) / RET_DK))
    ang = jnp.arange(t, dtype=F32)[:, None] * inv[None, :]
    cos_r, sin_r = jnp.cos(ang), jnp.sin(ang)
    tabs = _mla_tables(t)
    row = lambda a, i: a[i:i + 1]

    h1, hn1, s_ret = _ret_layer_fwd(x, row(w['mix_norm'], 0), w['ret_w_in'],
                                    lambda after: fetch('ret_out', after)['ret_w_out'], w['ret_gn'], cos_r, sin_r,
                                    row(w['mlp_norm'], 0), hn=w.get('hn0'), deps=w['deps'])
    h2, hn2, s_mlp0 = _mlp_fwd(0, h1, hn1, fetch('mlp_w1_0', (h1,))['mlp_w1'],
                               lambda after: fetch('mlp_w2_0', after)['mlp_w2'], row(w['ple_norm'], 0))
    w0 = fetch('ple_0', (h2,))
    h3, hn3, s_ple0 = _ple_fwd(0, h2, hn2, p, w0['ple_gate_w'], w0['ple_proj_w'], row(w['mix_norm'], 1))
    wm = fetch('mla', (h3,))
    mla_w = (wm['mla_w_in'], w['mla_q_a_norm'], w['mla_kv_a_norm'], wm['mla_w_uq'], wm['mla_w_ukv'],
             w['mla_q_norm'], w['mla_k_norm'], wm['mla_w_out'], tabs)
    h4, hn4, s_mla = _mla_layer_fwd(h3, hn3, *mla_w, row(w['mlp_norm'], 1))
    w1 = fetch('layer_1', (h4,))
    h5, hn5, s_mlp1 = _mlp_fwd(1, h4, hn4, w1['mlp_w1'], lambda after: w1['mlp_w2'], row(w['ple_norm'], 1))
    dy, sq_err, s_ple1 = _ple_fwd(1, h5, hn5, p, w1['ple_gate_w'], w1['ple_proj_w'], None, target)

    n = N_DEV
    colsh = lambda a: a.reshape(a.shape[0], n, a.shape[1] // n).transpose(1, 0, 2)
    rowsh = lambda a: a.reshape(n, a.shape[0] // n, a.shape[1])
    big = {}

    def emit_group(group):
        big.update(group)
        return emit(group)

    dh5, dg_ple1, dwg1, dwp1 = _ple_bwd(1, dy, s_ple1, p, row(w['ple_norm'], 1), w1['ple_gate_w'])
    dh4, dg_mlp1, dw1_1, dw2_1 = _mlp_bwd(1, dh5, s_mlp1, row(w['mlp_norm'], 1))
    deps = emit_group({('ple_gate_w', 1): rowsh(dwg1), ('ple_proj_w', 1): colsh(dwp1),
                       ('mlp_w2', 1): dw2_1, ('mlp_w1', 1): dw1_1})
    dh3, gm = _mla_layer_bwd(dh4, s_mla, row(w['mix_norm'], 1), *mla_w, deps=deps)
    deps = emit_group({('mla_w_out', 0): rowsh(gm['wmo']), ('mla_w_uq', 0): _gather_rope(gm['wuq']),
                       ('mla_w_ukv', 0): gm['wukv'], ('mla_w_in', 0): rowsh(_gather_rope(gm['wmi']))})
    dh2, dg_ple0, dwg0, dwp0 = _ple_bwd(0, dh3, s_ple0, p, row(w['ple_norm'], 0), w0['ple_gate_w'], deps=deps)
    dh1, dg_mlp0, dw1_0, dw2_0 = _mlp_bwd(0, dh2, s_mlp0, row(w['mlp_norm'], 0))
    deps = emit_group({('ple_gate_w', 0): rowsh(dwg0), ('ple_proj_w', 0): colsh(dwp0),
                       ('mlp_w2', 0): dw2_0, ('mlp_w1', 0): dw1_0})
    dx, dg_mix0, dgn = _ret_layer_bwd(
        dh1, s_ret, row(w['mix_norm'], 0), w['ret_w_in'], w['ret_gn'], cos_r, sin_r,
        lambda dwro: emit_group({('ret_w_out', 0): rowsh(dwro)}),
        lambda dwri: emit_group({('ret_w_in', 0): dwri}), deps=deps)

    small = dict(
        mix_norm=[dg_mix0, gm['mix']], mlp_norm=[dg_mlp0, dg_mlp1], ple_norm=[dg_ple0, dg_ple1],
        ret_gn=dgn, mla_q_a_norm=gm['qa'], mla_kv_a_norm=gm['kva'], mla_q_norm=gm['gq'], mla_k_norm=gm['gk'],
    )
    return sq_err, dx, big, small


def _my_place():
    x, y, c = lax.axis_index("x"), lax.axis_index("y"), lax.axis_index("c")
    return x, y, c


def _flat(px, py, pc):
    return 4 * px + 2 * py + pc


def _peer(x, y, c, r):
    return (1 - x if r & 4 else x, 1 - y if r & 2 else y, 1 - c if r & 1 else c)


HBM = pl.BlockSpec(memory_space=pltpu.HBM)
SEMS = pl.BlockSpec(memory_space=pltpu.SEMAPHORE)
SIDE_EFFECT = pltpu.SideEffectType.DATAFLOW_SIDE_EFFECTING


def _rs_copies(x, y, c, srcs, lands, send_sems, recv_sems):
    copies = []
    for a in range(len(srcs)):
        for r in range(1, N_DEV):
            peer = _peer(x, y, c, r)
            k = a * (N_DEV - 1) + r - 1
            copies.append(pltpu.make_async_remote_copy(
                src_ref=srcs[a].at[_flat(*peer)], dst_ref=lands[a].at[r - 1],
                send_sem=send_sems.at[k], recv_sem=recv_sems.at[k], device_id=peer, device_id_type=MESH))
    return copies


def _rs_start(name, arrays):
    n = len(arrays)
    hbm = lambda a: pltpu.with_memory_space_constraint(a, pltpu.HBM)
    lands = [hbm(lax.empty((N_DEV - 1,) + a.shape[1:], a.dtype)) for a in arrays]

    def body(*refs):
        srcs, lnd = refs[:n], refs[n:2 * n]
        send_sems, recv_sems = refs[2 * n], refs[2 * n + 1]
        token = refs[-1]
        for cp in _rs_copies(*_my_place(), srcs, lnd, send_sems, recv_sems):
            cp.start()
        token[...] = jnp.zeros_like(token)

    outs = pl.pallas_call(
        body, name=name,
        in_specs=[HBM] * (2 * n),
        out_specs=[SEMS, SEMS] + [HBM] * (2 * n) + [pl.BlockSpec(memory_space=pltpu.VMEM)],
        out_shape=[pltpu.SemaphoreType.DMA((n * (N_DEV - 1),)), pltpu.SemaphoreType.DMA((n * (N_DEV - 1),))]
        + [pltpu.HBM(a.shape, a.dtype) for a in arrays] + [pltpu.HBM(l.shape, l.dtype) for l in lands]
        + [_sds((8, 128), F32)],
        input_output_aliases={i: 2 + i for i in range(2 * n)},
        compiler_params=pltpu.CompilerParams(has_side_effects=SIDE_EFFECT),
    )(*[hbm(a) for a in arrays], *lands)
    return outs[0], outs[1], outs[2:2 + n], outs[2 + n:2 + 2 * n], outs[-1]


def _rs_wait(name, send_sems, recv_sems, srcs, lands, after):
    n = len(srcs)

    def body(*refs):
        src_refs, lnd = refs[:n], refs[n:2 * n]
        send, recv = refs[2 * n], refs[2 * n + 1]
        for cp in _rs_copies(*_my_place(), src_refs, lnd, send, recv):
            cp.wait_send()
            cp.wait_recv()

    outs = pl.pallas_call(
        body, name=name,
        in_specs=[HBM] * (2 * n) + [SEMS, SEMS] + [ANY] * len(after),
        out_specs=[HBM] * (2 * n),
        out_shape=[pltpu.HBM(a.shape, a.dtype) for a in list(srcs) + list(lands)],
        input_output_aliases={i: i for i in range(2 * n)},
        compiler_params=pltpu.CompilerParams(has_side_effects=SIDE_EFFECT),
    )(*srcs, *lands, send_sems, recv_sems, *after)
    return outs[:n], outs[n:]


SMALL_PACK_ROWS = 16


def _all_reduce_small(rows, deps=()):
    n = len(rows)

    def body(*refs):
        ins = refs[:n]
        out_ref, mine, buf, send_sems, recv_sems = refs[n + len(deps):]
        x, y, c = _my_place()
        mine[...] = jnp.zeros_like(mine)
        for (r0, a), ref in zip(rows, ins):
            mine[r0:r0 + a.shape[0], 0:a.shape[1]] = ref[...]
        buf[_flat(x, y, c)] = mine[...]
        copies = []
        for r in range(1, N_DEV):
            peer = _peer(x, y, c, r)
            send = pltpu.make_async_remote_copy(
                src_ref=mine, dst_ref=buf.at[_flat(x, y, c)],
                send_sem=send_sems.at[r - 1], recv_sem=recv_sems.at[r - 1], device_id=peer, device_id_type=MESH)
            send.start()
            recv = pltpu.make_async_remote_copy(
                src_ref=mine, dst_ref=buf.at[_flat(*peer)],
                send_sem=send_sems.at[r - 1], recv_sem=recv_sems.at[r - 1], device_id=peer, device_id_type=MESH)
            copies.append((send, recv))
        for send, recv in copies:
            send.wait_send()
            recv.wait_recv()
        acc = buf[0]
        for s in range(1, N_DEV):
            acc = acc + buf[s]
        out_ref[...] = acc

    vm = pl.BlockSpec(memory_space=pltpu.VMEM)
    shape = (SMALL_PACK_ROWS, D_MODEL)
    return pl.pallas_call(
        body, name="all_reduce_small", in_specs=[vm] * n + [ANY] * len(deps), out_specs=vm,
        out_shape=_sds(shape, F32),
        scratch_shapes=[pltpu.VMEM(shape, F32), pltpu.VMEM((N_DEV,) + shape, F32),
                        pltpu.SemaphoreType.DMA((7,)), pltpu.SemaphoreType.DMA((7,))],
    )(*[a for _, a in rows], *deps)


def _adamw_math(w, g, m, v):
    m = ADAM_B1 * m + (1.0 - ADAM_B1) * g
    v = ADAM_B2 * v + (1.0 - ADAM_B2) * (g * g)
    m_hat = m / (1.0 - ADAM_B1 ** ADAM_STEP)
    v_hat = v / (1.0 - ADAM_B2 ** ADAM_STEP)
    delta = -ADAM_LR * (m_hat / (jnp.sqrt(v_hat) + ADAM_EPS) + ADAM_WD * w)
    return delta, m, v


def _adamw_big(name, w, m, v, srcs, lands, me):
    nl, rows, cols = w.shape
    tr = next(cand for cand in (256, 128, 64, 32, 16, 8) if rows % cand == 0)

    def body(me_ref, w_ref, m_ref, v_ref, *rest):
        src_refs, land_refs = rest[:nl], rest[nl:2 * nl]
        g_ref, d_ref, mo_ref, vo_ref = rest[2 * nl:]
        for layer in range(nl):
            @pl.when(pl.program_id(0) == layer)
            def _():
                g = src_refs[layer][...].astype(F32)
                for s in range(N_DEV - 1):
                    g = g + land_refs[layer][s].astype(F32)
                delta, mn, vn = _adamw_math(w_ref[...], g, m_ref[...], v_ref[...])
                g_ref[...] = g
                d_ref[...] = delta
                mo_ref[...] = mn
                vo_ref[...] = vn

    blk = pl.BlockSpec((None, tr, cols), lambda l, i, me_ref: (l, i, 0))
    at = lambda layer, l, i: jnp.where(l == layer, i, 0)
    own = [pl.BlockSpec((None, tr, cols), functools.partial(lambda layer, l, i, me_ref: (me_ref[0], at(layer, l, i), 0),
                                                            layer)) for layer in range(nl)]
    peers = [pl.BlockSpec((N_DEV - 1, tr, cols), functools.partial(lambda layer, l, i, me_ref: (0, at(layer, l, i), 0),
                                                                   layer)) for layer in range(nl)]
    return pl.pallas_call(
        body, name=name,
        grid_spec=pltpu.PrefetchScalarGridSpec(
            num_scalar_prefetch=1, grid=(nl, rows // tr),
            in_specs=[blk, blk, blk] + own + peers, out_specs=[blk] * 4),
        out_shape=[_sds((nl, rows, cols), F32)] * 4,
        compiler_params=_cparams(("arbitrary", "arbitrary")),
    )(me, w, m, v, *srcs, *lands)


def _adamw_small(ws, gs, ms, vs):
    n = len(ws)

    def body(*refs):
        w_refs, g_refs, m_refs, v_refs = (refs[i * n:(i + 1) * n] for i in range(4))
        d_out, m_out, v_out = (refs[(4 + i) * n:(5 + i) * n] for i in range(3))
        for i in range(n):
            delta, mn, vn = _adamw_math(w_refs[i][...], g_refs[i][...], m_refs[i][...], v_refs[i][...])
            d_out[i][...] = delta
            m_out[i][...] = mn
            v_out[i][...] = vn

    vm = pl.BlockSpec(memory_space=pltpu.VMEM)
    outs = pl.pallas_call(
        body, name="adamw_small", in_specs=[vm] * (4 * n), out_specs=[vm] * (3 * n),
        out_shape=[_sds(a.shape, F32) for a in ws] * 3,
    )(*ws, *gs, *ms, *vs)
    return outs[:n], outs[n:2 * n], outs[2 * n:]


def _pad_to(a, rows, cols):
    return jnp.pad(a, ((0, rows - a.shape[0]), (0, cols - a.shape[1])))


def _place_own(blocks):
    me = _flat(*_my_place())
    return [lax.dynamic_update_slice(lax.empty((N_DEV,) + b.shape, b.dtype), b[None], (me,) + (0,) * b.ndim)
            for b in blocks]


def _ag_copies(x, y, c, blocks, bufs, send_sems, recv_sems, arriving):
    copies = []
    for a in range(len(blocks)):
        for r in range(1, N_DEV):
            peer = _peer(x, y, c, r)
            k = a * (N_DEV - 1) + r - 1
            copies.append(pltpu.make_async_remote_copy(
                src_ref=blocks[a], dst_ref=bufs[a].at[_flat(*(peer if arriving else (x, y, c)))],
                send_sem=send_sems.at[k], recv_sem=recv_sems.at[k], device_id=peer, device_id_type=MESH))
    return copies


def _ag_start(groups, after):
    flat = [pair for g in groups for pair in g]
    n, ng = len(flat), len(groups)
    hbm = lambda a: pltpu.with_memory_space_constraint(a, pltpu.HBM)

    def body(*refs):
        blocks, bufs = refs[:n], refs[n:2 * n]
        sems = refs[2 * n + len(after):2 * n + len(after) + 2 * ng]
        x, y, c = _my_place()
        at = 0
        for gi, g in enumerate(groups):
            for cp in _ag_copies(x, y, c, blocks[at:at + len(g)], bufs[at:at + len(g)], sems[2 * gi],
                                 sems[2 * gi + 1], arriving=False):
                cp.start()
            at += len(g)
        refs[-1][...] = jnp.zeros_like(refs[-1])

    sem_shapes = [pltpu.SemaphoreType.DMA((len(g) * (N_DEV - 1),)) for g in groups for _ in range(2)]
    outs = pl.pallas_call(
        body, name="gather_start",
        in_specs=[HBM] * (2 * n) + [ANY] * len(after),
        out_specs=[SEMS] * (2 * ng) + [HBM] * (2 * n) + [pl.BlockSpec(memory_space=pltpu.VMEM)],
        out_shape=sem_shapes + [pltpu.HBM(b.shape, b.dtype) for b, _ in flat]
        + [pltpu.HBM(u.shape, u.dtype) for _, u in flat] + [_sds((8, 128), F32)],
        input_output_aliases={i: 2 * ng + i for i in range(2 * n)},
        compiler_params=pltpu.CompilerParams(has_side_effects=SIDE_EFFECT),
    )(*[hbm(b) for b, _ in flat], *[hbm(u) for _, u in flat], *after)
    blocks_thru, bufs_thru = outs[2 * ng:2 * ng + n], outs[2 * ng + n:2 * ng + 2 * n]
    started, at = [], 0
    for gi, g in enumerate(groups):
        started.append((outs[2 * gi], outs[2 * gi + 1], blocks_thru[at:at + len(g)], bufs_thru[at:at + len(g)]))
        at += len(g)
    return started, outs[-1]


def _ag_wait(name, send_sems, recv_sems, blocks, bufs, after):
    n = len(blocks)

    def body(*refs):
        for cp in _ag_copies(*_my_place(), refs[:n], refs[n:2 * n], refs[2 * n], refs[2 * n + 1], arriving=True):
            cp.wait_send()
            cp.wait_recv()

    outs = pl.pallas_call(
        body, name=name,
        in_specs=[HBM] * (2 * n) + [SEMS, SEMS] + [ANY] * len(after),
        out_specs=[HBM] * (2 * n),
        out_shape=[pltpu.HBM(a.shape, a.dtype) for a in list(blocks) + list(bufs)],
        input_output_aliases={i: i for i in range(2 * n)},
        compiler_params=pltpu.CompilerParams(has_side_effects=SIDE_EFFECT),
    )(*blocks, *bufs, send_sems, recv_sems, *after)
    return outs[n:]


def _split_call(name, body, thru, sems_in, new_sems, after):
    n, ns, nn = len(thru), len(sems_in), len(new_sems)
    hbm = lambda a: pltpu.with_memory_space_constraint(a, pltpu.HBM)

    def wrapped(*refs):
        body(refs[:n], refs[n:n + ns], refs[n + ns + len(after):n + ns + len(after) + nn])
        refs[-1][...] = jnp.zeros_like(refs[-1])

    outs = pl.pallas_call(
        wrapped, name=name,
        in_specs=[HBM] * n + [SEMS] * ns + [ANY] * len(after),
        out_specs=[SEMS] * nn + [HBM] * n + [pl.BlockSpec(memory_space=pltpu.VMEM)],
        out_shape=[pltpu.SemaphoreType.DMA((k,)) for k in new_sems] + [pltpu.HBM(a.shape, a.dtype) for a in thru]
        + [_sds((8, 128), F32)],
        input_output_aliases={i: nn + i for i in range(n)},
        compiler_params=pltpu.CompilerParams(has_side_effects=SIDE_EFFECT),
    )(*[hbm(a) for a in thru], *sems_in, *after)
    return list(outs[:nn]), list(outs[nn:nn + n]), outs[-1]


def _first_gather(blocks, bufs, overlap):
    n = len(blocks)

    def copies(refs, s1, r1, s2, r2):
        x, y, c = _my_place()
        me, sibling = (x, y, c), (x, y, 1 - c)
        chips = [(1 - x, y), (x, 1 - y), (1 - x, 1 - y)]
        blk, buf = refs[:n], refs[n:]
        out = dict(send1=[], recv1_sib=[], recv1_ici=[], send2=[], recv2=[])
        for a in range(n):
            place = lambda dev: buf[a].at[_flat(*dev)]
            for k, to in enumerate([sibling] + [(*chip, c) for chip in chips]):
                mk = lambda dst: pltpu.make_async_remote_copy(
                    src_ref=blk[a], dst_ref=dst, send_sem=s1.at[4 * a + k], recv_sem=r1.at[4 * a + k],
                    device_id=to, device_id_type=MESH)
                out['send1'].append(mk(place(me)))
                out['recv1_sib' if k == 0 else 'recv1_ici'].append(mk(place(to)))
            for j, chip in enumerate(chips):
                mk = lambda dev: pltpu.make_async_remote_copy(
                    src_ref=place(dev), dst_ref=place(dev), send_sem=s2.at[3 * a + j], recv_sem=r2.at[3 * a + j],
                    device_id=sibling, device_id_type=MESH)
                out['send2'].append(mk((*chip, c)))
                out['recv2'].append(mk((*chip, 1 - c)))
        return out

    def start(refs, sems_in, new):
        for cp in copies(refs, new[0], new[1], new[0], new[1])['send1']:
            cp.start()

    def forward(refs, sems_in, new):
        cps = copies(refs, sems_in[0], sems_in[1], new[0], new[1])
        for cp in cps['recv1_ici']:
            cp.wait_recv()
        for cp in cps['send2']:
            cp.start()

    def finish(refs, sems_in, new):
        cps = copies(refs, *sems_in)
        for cp in cps['recv1_sib'] + cps['recv2']:
            cp.wait_recv()
        for cp in cps['send1'] + cps['send2']:
            cp.wait_send()

    sems1, thru, token = _split_call("first_gather_start", start, list(blocks) + list(bufs), [], [4 * n, 4 * n], ())
    after = overlap(token)
    sems2, thru, token = _split_call("first_gather_forward", forward, thru, sems1, [3 * n, 3 * n], after)
    _, thru, _ = _split_call("first_gather_wait", finish, thru, sems1 + sems2, [], ())
    return thru[n:], token


def _prepare_weights(p, x):
    n = N_DEV
    bf = lambda a: a.astype(BF16)
    gn_pack = jnp.concatenate([
        _pad_to(p['ret_gn'][0], RET_HEADS, 128), _pad_to(p['mla_q_a_norm'], 1, 128),
        _pad_to(p['mla_kv_a_norm'], 1, 128), jnp.zeros((2, 128), F32)], axis=0)
    ple = lambda l: [bf(p['ple_gate_w'][l]), bf(p['ple_proj_w'][l])]
    names = ('ret_out', 'mlp_w1_0', 'mlp_w2_0', 'ple_0', 'mla', 'layer_1')
    later = [[bf(p['ret_w_out'][0])], [bf(p['mlp_w1'][0])], [bf(p['mlp_w2'][0])], ple(0),
             [bf(p['mla_w_in'][0]), bf(p['mla_w_uq'][0]), bf(p['mla_w_ukv'][0]), bf(p['mla_w_out'][0])],
             [bf(p['mlp_w1'][1]), bf(p['mlp_w2'][1])] + ple(1)]
    first = [gn_pack, bf(p['ret_w_in'][0])]
    behind = {}

    def overlap(token):
        behind['hn0'] = _rms_fwd("mix_norm0", x, p['mix_norm'][0:1], deps=(token,))
        behind['bufs'] = _place_own([b for g in later for b in g])
        return (behind['hn0'], *behind['bufs'])

    (pack, wri), token = _first_gather(first, _place_own(first), overlap)
    bufs = behind['bufs']
    groups, at = [], 0
    for g in later:
        groups.append(list(zip(g, bufs[at:at + len(g)])))
        at += len(g)
    started, token = _ag_start(groups, (token,))

    w = {k: p[k] for k in ('mix_norm', 'mlp_norm', 'ple_norm')}
    w['hn0'] = behind['hn0']
    w['ret_gn'] = pack[:, :RET_HEADS, :RET_DV // n].transpose(1, 0, 2).reshape(RET_HEADS, RET_DV)
    w['mla_q_a_norm'] = pack[:, RET_HEADS, :MLA_Q_RANK // n].reshape(1, MLA_Q_RANK)
    w['mla_kv_a_norm'] = pack[:, RET_HEADS + 1, :MLA_KV_RANK // n].reshape(1, MLA_KV_RANK)
    w['ret_w_in'] = wri
    w['mla_q_norm'] = _spread_rope(p['mla_q_norm'])
    w['mla_k_norm'] = _spread_rope(p['mla_k_norm'])
    w['deps'] = (token,)

    def fetch(name, after):
        got = list(_ag_wait("gather_wait_" + name, *started[names.index(name)], after))
        if name == 'ret_out':
            return dict(ret_w_out=got[0].reshape(RET_V_W, D_MODEL))
        if name == 'mla':
            wmi, wuq, wukv, wmo = got
            return dict(mla_w_in=_spread_rope(wmi.reshape(D_MODEL, MLA_IN)), mla_w_uq=_spread_rope(wuq),
                        mla_w_ukv=wukv, mla_w_out=wmo.reshape(D_MODEL, D_MODEL))
        out = {}
        if name in ('mlp_w1_0', 'layer_1'):
            out['mlp_w1'] = got.pop(0)
        if name in ('mlp_w2_0', 'layer_1'):
            out['mlp_w2'] = got.pop(0)
        if name in ('ple_0', 'layer_1'):
            out['ple_gate_w'] = got[0].reshape(D_MODEL, D_MODEL)
            out['ple_proj_w'] = got[1].transpose(1, 0, 2).reshape(PLE_DIM, D_MODEL)
        return out

    return w, fetch


def _small_grads(small, after):
    rows = [(0, small['mix_norm'][0]), (1, small['mix_norm'][1]), (2, small['mlp_norm'][0]),
            (3, small['mlp_norm'][1]), (4, small['ple_norm'][0]), (5, small['ple_norm'][1]),
            (6, small['ret_gn']), (10, small['mla_q_a_norm']), (11, small['mla_kv_a_norm']),
            (12, small['mla_q_norm']), (13, small['mla_k_norm']), (14, small['sq_err'])]
    gs = _all_reduce_small(rows, after)
    me = _flat(*_my_place())
    n = N_DEV
    return dict(
        sq_err=gs[14, 0],
        mix_norm=gs[0:2], mlp_norm=gs[2:4], ple_norm=gs[4:6],
        ret_gn=lax.dynamic_slice(gs, (6, me * (RET_DV // n)), (RET_HEADS, RET_DV // n)),
        mla_q_a_norm=lax.dynamic_slice(gs, (10, me * (MLA_Q_RANK // n)), (1, MLA_Q_RANK // n)),
        mla_kv_a_norm=lax.dynamic_slice(gs, (11, me * (MLA_KV_RANK // n)), (1, MLA_KV_RANK // n)),
        mla_q_norm=_gather_rope(gs[12:13, :MLA_HD_PAD]), mla_k_norm=_gather_rope(gs[13:14, :MLA_HD_PAD]))


def kernel(x, p, mix_norm, ret_w_in, ret_gn, ret_w_out, mla_w_in, mla_q_a_norm, mla_kv_a_norm, mla_w_uq, mla_w_ukv, mla_q_norm, mla_k_norm, mla_w_out, mlp_norm, mlp_w1, mlp_w2, ple_norm, ple_gate_w, ple_proj_w, loss_target, m_mix_norm, m_ret_w_in, m_ret_gn, m_ret_w_out, m_mla_w_in, m_mla_q_a_norm, m_mla_kv_a_norm, m_mla_w_uq, m_mla_w_ukv, m_mla_q_norm, m_mla_k_norm, m_mla_w_out, m_mlp_norm, m_mlp_w1, m_mlp_w2, m_ple_norm, m_ple_gate_w, m_ple_proj_w, v_mix_norm, v_ret_w_in, v_ret_gn, v_ret_w_out, v_mla_w_in, v_mla_q_a_norm, v_mla_kv_a_norm, v_mla_w_uq, v_mla_w_ukv, v_mla_q_norm, v_mla_k_norm, v_mla_w_out, v_mlp_norm, v_mlp_w1, v_mlp_w2, v_ple_norm, v_ple_gate_w, v_ple_proj_w):
    given = dict(locals())
    params = {n: given[n] for n in WEIGHTS}
    w, fetch = _prepare_weights(params, x[0])

    started = []

    def emit(group):
        keys = list(group)
        send, recv, srcs, lands, token = _rs_start(f"rs_start{len(started)}", [group[k] for k in keys])
        started.append((keys, send, recv, srcs, lands))
        return (token,)

    sq_err, grad_x, _, small = _local_step(x[0], p, loss_target[0], w, fetch, emit)
    small['sq_err'] = sq_err[0:1]

    grads, deltas, new_m, new_v = {}, {}, {}, {}
    total = {}

    def small_updates(after):
        sg = _small_grads(small, after)
        total['loss'] = 0.5 / D_MODEL * sg['sq_err']
        two_d = lambda a: a.reshape(-1, a.shape[-1])
        d_s, m_s, v_s = _adamw_small(
            [two_d(params[n]) for n in SMALL], [sg[n] for n in SMALL],
            [two_d(given["m_" + n]) for n in SMALL], [two_d(given["v_" + n]) for n in SMALL])
        for i, n in enumerate(SMALL):
            shape = params[n].shape
            grads[n], deltas[n], new_m[n], new_v[n] = (a.reshape(shape) for a in (sg[n], d_s[i], m_s[i], v_s[i]))
        return (d_s[0],)

    me = _flat(*_my_place()).astype(jnp.int32).reshape(1)
    after = (grad_x,)
    src_of, land_of = {}, {}
    for gi, (keys, send, recv, srcs, lands) in enumerate(started):
        if gi == len(started) - 1:
            after = small_updates(after)
        srcs, lands = _rs_wait(f"rs_wait{gi}", send, recv, srcs, lands, after)
        for k, s, l in zip(keys, srcs, lands):
            src_of[k], land_of[k] = s, l
        done = [n for n in BIG if n not in grads and all((n, l) in src_of for l in range(params[n].shape[0]))]
        for n in done:
            layers = range(params[n].shape[0])
            grads[n], deltas[n], new_m[n], new_v[n] = _adamw_big(
                "adamw_" + n, params[n], given["m_" + n], given["v_" + n],
                [src_of[(n, l)] for l in layers], [land_of[(n, l)] for l in layers], me)
        if done:
            after = tuple(deltas[n] for n in done)

    return (total['loss'], grad_x[None], *[grads[n] for n in WEIGHTS], *[deltas[n] for n in WEIGHTS],
            *[new_m[n] for n in WEIGHTS], *[new_v[n] for n in WEIGHTS])
```

```python
import functools

import jax
import jax.numpy as jnp
from jax import lax
from jax.experimental import pallas as pl
from jax.experimental.pallas import tpu as pltpu

F32 = jnp.float32
BF16 = jnp.bfloat16
MESH = pl.DeviceIdType.MESH
ANY = pl.BlockSpec(memory_space=pl.ANY)

N_DEV = 8
D_MODEL = 1024
CHUNK = 64
RET_BLOCK = 4 * CHUNK
EPS = 1e-6
ROPE_THETA = 10000.0
RET_HEADS = 4
RET_DK = 256
RET_DV = 512
RET_QK_W = RET_HEADS * RET_DK
RET_V_W = RET_HEADS * RET_DV
RET_IN = 2 * RET_QK_W + 2 * RET_V_W
MLA_HEADS = 8
MLA_NOPE = 128
MLA_ROPE = 64
MLA_QKD = MLA_NOPE + MLA_ROPE
MLA_VD = 128
MLA_Q_RANK = 384
MLA_KV_RANK = 256
MLA_IN = MLA_Q_RANK + MLA_KV_RANK + MLA_ROPE
MLA_IN_PAD = 768
MLA_HD_PAD = 256
D_FF = 4096
PLE_DIM = 256
ATT_SCALE = MLA_QKD ** -0.5
LOG2E = 1.4426950408889634
ATT_EXP2 = ATT_SCALE * LOG2E

ADAM_LR = 0.001
ADAM_B1 = 0.9
ADAM_B2 = 0.999
ADAM_EPS = 1e-08
ADAM_WD = 0.01
ADAM_STEP = 10

VMEM_LIMIT = 52 * 1024 * 1024
ROW_TILE = 1024
RET_ROWS = 512
ATT_BLOCK = 256
ATT_QROWS = 1024
ATT_KROWS = 1024
ATT_HEADS = 2
PREP_ROWS = 1024

WEIGHTS = ['mix_norm', 'ret_w_in', 'ret_gn', 'ret_w_out', 'mla_w_in', 'mla_q_a_norm', 'mla_kv_a_norm',
           'mla_w_uq', 'mla_w_ukv', 'mla_q_norm', 'mla_k_norm', 'mla_w_out', 'mlp_norm', 'mlp_w1', 'mlp_w2',
           'ple_norm', 'ple_gate_w', 'ple_proj_w']
BIG = ['ret_w_in', 'ret_w_out', 'mla_w_in', 'mla_w_uq', 'mla_w_ukv', 'mla_w_out', 'mlp_w1', 'mlp_w2',
       'ple_gate_w', 'ple_proj_w']
SMALL = [w for w in WEIGHTS if w not in BIG]


def _cparams(sem=None):
    return pltpu.CompilerParams(dimension_semantics=sem, vmem_limit_bytes=VMEM_LIMIT)


def _dot(a, b, ca, cb):
    return lax.dot_general(a, b, (((ca,), (cb,)), ((), ())), preferred_element_type=F32)


def _bf(v):
    return v if v.dtype == BF16 else v.astype(BF16)


def _sigmoid(z):
    return 1.0 / (1.0 + jnp.exp(-z))


def _mm(name, grid, a, a_spec, b, b_spec, contract, outs, extras=(), epi=None, deps=(), split=None):
    nk = grid[2]
    n_ex, n_out, n_dep = len(extras), len(outs), len(deps)
    acc_shape = tuple(d for d in outs[0][1].block_shape if d is not None)
    if split is not None:
        acc_shape = (acc_shape[1], acc_shape[0] * split)

    def body(*refs):
        a_ref, b_ref = refs[:2]
        ex_refs = refs[2:2 + n_ex]
        out_refs = refs[2 + n_ex + n_dep:2 + n_ex + n_dep + n_out]

        def product():
            return _dot(_bf(a_ref[...]), _bf(b_ref[...]), contract[0], contract[1])

        def finish(acc):
            if split is not None:
                for j in range(acc_shape[1] // split):
                    out_refs[0][j] = acc[:, j * split:(j + 1) * split].astype(out_refs[0].dtype)
                return
            acc = acc[...]
            res = epi(acc, *[r[...] for r in ex_refs]) if epi is not None else (acc,)
            for o, r in zip(out_refs, res):
                o[...] = r.astype(o.dtype)

        if nk == 1:
            finish(product())
        else:
            acc_ref = refs[-1]
            k = pl.program_id(2)

            @pl.when(k == 0)
            def _():
                acc_ref[...] = jnp.zeros_like(acc_ref)

            acc_ref[...] += product()

            @pl.when(k == nk - 1)
            def _():
                finish(acc_ref)

    return pl.pallas_call(
        body, name=name, grid=grid,
        in_specs=[a_spec, b_spec] + [s for _, s in extras] + [ANY] * n_dep,
        out_specs=[s for _, s in outs],
        out_shape=[s for s, _ in outs],
        scratch_shapes=[pltpu.VMEM(acc_shape, F32)] if nk > 1 else [],
        compiler_params=_cparams(("parallel", "parallel", "arbitrary")),
    )(a, b, *[x for x, _ in extras], *deps)


def _mm_rows(name, tm, a, w, mode, outs, extras=(), epi=None, deps=()):
    n_sh, rows, cols = w.shape
    n_ex, n_out, n_dep = len(extras), len(outs), len(deps)
    by_cols = mode in ('nn_cols', 'nt_rows')
    width = cols if mode == 'nn_cols' else rows

    def body(*refs):
        a_ref, w_ref = refs[:2]
        ex_refs = refs[2:2 + n_ex]
        out_refs = refs[2 + n_ex + n_dep:2 + n_ex + n_dep + n_out]
        if by_cols:
            av = _bf(a_ref[...])
            for s in range(n_sh):
                cs = slice(s * width, (s + 1) * width)
                acc = _dot(av, w_ref[s], 1, 0 if mode == 'nn_cols' else 1)
                res = epi(acc, *[r[:, cs] for r in ex_refs]) if epi is not None else (acc,)
                for o, r in zip(out_refs, res):
                    o[:, cs] = r.astype(o.dtype)
        else:
            chunk = rows if mode == 'nn_rows' else cols
            acc = None
            for s in range(n_sh):
                part = _dot(_bf(a_ref[:, s * chunk:(s + 1) * chunk]), w_ref[s], 1, 0 if mode == 'nn_rows' else 1)
                acc = part if acc is None else acc + part
            res = epi(acc, *[r[...] for r in ex_refs]) if epi is not None else (acc,)
            for o, r in zip(out_refs, res):
                o[...] = r.astype(o.dtype)

    t, ka = a.shape
    return pl.pallas_call(
        body, name=name, grid=(t // tm, 1, 1),
        in_specs=[pl.BlockSpec((tm, ka), lambda i, j, k: (i, 0)),
                  pl.BlockSpec((n_sh, rows, cols), lambda i, j, k: (0, 0, 0))] + [s for _, s in extras] + [ANY] * n_dep,
        out_specs=[s for _, s in outs],
        out_shape=[s for s, _ in outs],
        compiler_params=_cparams(("parallel", "arbitrary", "arbitrary")),
    )(a, w, *[x for x, _ in extras], *deps)


def _sds(shape, dtype):
    return jax.ShapeDtypeStruct(shape, dtype)


def _row_tile(t, cap=ROW_TILE):
    return min(cap, t)


def _rms_fwd(name, x, g, deps=()):
    t, d = x.shape
    tm = _row_tile(t)

    def body(x_ref, g_ref, *rest):
        o_ref = rest[-1]
        xv = x_ref[...]
        r = lax.rsqrt(jnp.mean(xv * xv, axis=-1, keepdims=True) + EPS)
        o_ref[...] = (xv * r * g_ref[...]).astype(o_ref.dtype)

    return pl.pallas_call(
        body, name=name, grid=(t // tm,),
        in_specs=[pl.BlockSpec((tm, d), lambda i: (i, 0)), pl.BlockSpec((1, d), lambda i: (0, 0))] + [ANY] * len(deps),
        out_specs=pl.BlockSpec((tm, d), lambda i: (i, 0)),
        out_shape=_sds((t, d), BF16),
        compiler_params=_cparams(("parallel",)),
    )(x, g, *deps)


def _rms_bwd_rows(dy, xv, g, n):
    r = lax.rsqrt(jnp.sum(xv * xv, axis=-1, keepdims=True) / n + EPS)
    xh = xv * r
    dxh = dy * g
    dx = r * (dxh - xh * (jnp.sum(dxh * xh, axis=-1, keepdims=True) / n))
    return dx, dy * xh


def _ple_gate_bwd(name, dh, gate, e):
    t, d = dh.shape
    tm = _row_tile(t)

    def body(dh_ref, g_ref, e_ref, de_ref, dz_ref):
        dh_v, gt = dh_ref[...], g_ref[...].astype(F32)
        de_ref[...] = (dh_v * gt).astype(BF16)
        dz_ref[...] = (dh_v * e_ref[...].astype(F32) * (gt * (1.0 - gt))).astype(BF16)

    row = pl.BlockSpec((tm, d), lambda i: (i, 0))
    return pl.pallas_call(
        body, name=name, grid=(t // tm,), in_specs=[row, row, row], out_specs=[row, row],
        out_shape=[_sds((t, d), BF16), _sds((t, d), BF16)],
        compiler_params=_cparams(("parallel",)),
    )(dh, gate, e)


def _rope_half(v, cos, sin):
    half = v.shape[-1] // 2
    v1, v2 = v[:, :half], v[:, half:]
    return jnp.concatenate([v1 * cos - v2 * sin, v2 * cos + v1 * sin], axis=-1)


def _ret_consts():
    lg = jnp.log(1.0 - 2.0 ** (-5.0 - jnp.arange(RET_HEADS, dtype=F32)))
    idx = jnp.arange(RET_BLOCK, dtype=F32)
    chunk = jnp.floor(idx / CHUNK)
    dist = idx[:, None] - idx[None, :]
    same = chunk[:, None] == chunk[None, :]
    seen = jnp.where(same, jnp.abs(dist), jnp.where(chunk[None, :] < chunk[:, None], dist, jnp.inf))
    intra = jnp.exp(lg[:, None, None] * seen)
    qdec = jnp.exp(lg[:, None] * (idx + 1.0))
    kdec = jnp.exp(lg[:, None] * (RET_BLOCK - 1.0 - idx))
    cdec = jnp.exp(lg * RET_BLOCK)
    qdec = jnp.broadcast_to(qdec[:, :, None], (RET_HEADS, RET_BLOCK, RET_DK))
    kdec = jnp.broadcast_to(kdec[:, :, None], (RET_HEADS, RET_BLOCK, RET_DK))
    cdec = jnp.broadcast_to(cdec[:, None, None], (RET_HEADS, 1, RET_DV))
    return intra, qdec, kdec, cdec


def _ret_specs(rb, rev_nb=None):
    blk = (lambda i: i) if rev_nb is None else (lambda i: rev_nb - 1 - i)
    full = lambda shape: pl.BlockSpec(shape, lambda i: (0,) * len(shape))
    return dict(
        proj=pl.BlockSpec((rb, RET_IN), lambda i: (blk(i), 0)),
        tab=pl.BlockSpec((rb, RET_DK // 2), lambda i: (blk(i), 0)),
        vw=pl.BlockSpec((rb, RET_V_W), lambda i: (blk(i), 0)),
        st=pl.BlockSpec((rb // RET_BLOCK, RET_HEADS, RET_DK, RET_DV), lambda i: (blk(i), 0, 0, 0)),
        gn=full((RET_HEADS, 1, RET_DV)),
        intra=full((RET_HEADS, RET_BLOCK, RET_BLOCK)),
        dec=full((RET_HEADS, RET_BLOCK, RET_DK)),
        cdec=full((RET_HEADS, 1, RET_DV)),
    )


def _ret_fwd(proj, cos, sin, gn):
    t = proj.shape[0]
    rb = min(RET_ROWS, t)
    cpb = rb // RET_BLOCK
    intra, qdec, kdec, cdec = _ret_consts()
    sp = _ret_specs(rb)

    def body(proj_ref, cos_ref, sin_ref, gn_ref, intra_ref, qd_ref, kd_ref, cd_ref,
             gated_ref, outp_ref, st_ref, s_ref):
        @pl.when(pl.program_id(0) == 0)
        def _():
            s_ref[...] = jnp.zeros_like(s_ref)

        def chunk(c, carry):
            rows = pl.ds(pl.multiple_of(c * RET_BLOCK, RET_BLOCK), RET_BLOCK)
            cs, sn = cos_ref[rows, :], sin_ref[rows, :]
            for h in range(RET_HEADS):
                q = proj_ref[rows, h * RET_DK:(h + 1) * RET_DK].astype(F32)
                k = proj_ref[rows, RET_QK_W + h * RET_DK:RET_QK_W + (h + 1) * RET_DK].astype(F32)
                v = proj_ref[rows, 2 * RET_QK_W + h * RET_DV:2 * RET_QK_W + (h + 1) * RET_DV]
                g = proj_ref[rows, 2 * RET_QK_W + RET_V_W + h * RET_DV:
                             2 * RET_QK_W + RET_V_W + (h + 1) * RET_DV].astype(F32)
                qr = _rope_half(q, cs, sn)
                kr = _rope_half(k, cs, sn) * (RET_DK ** -0.5)
                qb, kb, vb = qr.astype(BF16), kr.astype(BF16), v
                sc = _dot(qb, kb, 1, 1) * intra_ref[h]
                inner = _dot(sc.astype(BF16), vb, 1, 0)
                s_old = s_ref[h]
                sb = s_old.astype(BF16)
                st_ref[c, h] = sb
                cross = _dot((qr * qd_ref[h]).astype(BF16), sb, 1, 0)
                out = inner + cross
                s_ref[h] = s_old * cd_ref[h] + _dot((kr * kd_ref[h]).astype(BF16), vb, 0, 0)
                r = lax.rsqrt(jnp.mean(out * out, axis=-1, keepdims=True) + EPS)
                y = out * r * gn_ref[h]
                cols = slice(h * RET_DV, (h + 1) * RET_DV)
                gated_ref[rows, cols] = (g * _sigmoid(g) * y).astype(BF16)
                outp_ref[rows, cols] = out
            return carry

        lax.fori_loop(0, cpb, chunk, 0)

    return pl.pallas_call(
        body, name="ret_fwd", grid=(t // rb,),
        in_specs=[sp['proj'], sp['tab'], sp['tab'], sp['gn'], sp['intra'], sp['dec'], sp['dec'], sp['cdec']],
        out_specs=[sp['vw'], sp['vw'], sp['st']],
        out_shape=[_sds((t, RET_V_W), BF16), _sds((t, RET_V_W), F32),
                   _sds((t // RET_BLOCK, RET_HEADS, RET_DK, RET_DV), BF16)],
        scratch_shapes=[pltpu.VMEM((RET_HEADS, RET_DK, RET_DV), F32)],
        compiler_params=_cparams(("arbitrary",)),
    )(proj, cos, sin, gn.reshape(RET_HEADS, 1, RET_DV), intra, qdec, kdec, cdec)


def _ret_gate_bwd_epi(dgt, out, g, gn):
    g = g.astype(F32)
    r = lax.rsqrt(jnp.mean(out * out, axis=-1, keepdims=True) + EPS)
    xh = out * r
    sg = _sigmoid(g)
    dgate = dgt * (xh * gn) * (sg * (1.0 + g * (1.0 - sg)))
    dy = dgt * (g * sg)
    dxh = dy * gn
    dout = r * (dxh - xh * jnp.mean(dxh * xh, axis=-1, keepdims=True))
    return dout, dgate, jnp.sum(dy * xh, axis=0, keepdims=True)


def _ret_bwd(proj, cos, sin, states, dout, dgate, deps=()):
    t = proj.shape[0]
    rb = min(RET_ROWS, t)
    cpb = rb // RET_BLOCK
    nb = t // rb
    intra, qdec, kdec, cdec = _ret_consts()
    sp = _ret_specs(rb, rev_nb=nb)

    def body(proj_ref, cos_ref, sin_ref, intra_ref, qd_ref, kd_ref, cd_ref, st_ref, dout_ref, dgate_ref, *rest):
        dproj_ref, ds_ref = rest[len(deps):]

        @pl.when(pl.program_id(0) == 0)
        def _():
            ds_ref[...] = jnp.zeros_like(ds_ref)

        def chunk(cc, carry):
            c = cpb - 1 - cc
            rows = pl.ds(pl.multiple_of(c * RET_BLOCK, RET_BLOCK), RET_BLOCK)
            cs, sn = cos_ref[rows, :], sin_ref[rows, :]
            for h in range(RET_HEADS):
                q = proj_ref[rows, h * RET_DK:(h + 1) * RET_DK].astype(F32)
                k = proj_ref[rows, RET_QK_W + h * RET_DK:RET_QK_W + (h + 1) * RET_DK].astype(F32)
                v = proj_ref[rows, 2 * RET_QK_W + h * RET_DV:2 * RET_QK_W + (h + 1) * RET_DV]
                cols = slice(h * RET_DV, (h + 1) * RET_DV)
                qr = _rope_half(q, cs, sn)
                kr = _rope_half(k, cs, sn) * (RET_DK ** -0.5)
                qb, kb, vb = qr.astype(BF16), kr.astype(BF16), v
                qdb = (qr * qd_ref[h]).astype(BF16)
                kdb = (kr * kd_ref[h]).astype(BF16)
                doutb = dout_ref[rows, cols]
                itr = intra_ref[h]
                pb = (_dot(qb, kb, 1, 1) * itr).astype(BF16)
                dv = _dot(pb, doutb, 0, 0)
                dsc = (_dot(doutb, vb, 1, 1) * itr).astype(BF16)
                dq = _dot(dsc, kb, 1, 0)
                dk = _dot(dsc, qb, 0, 0)
                dq = dq + _dot(doutb, st_ref[c, h], 1, 1) * qd_ref[h]
                ds_new = ds_ref[h]
                dsb = ds_new.astype(BF16)
                dk = dk + _dot(vb, dsb, 1, 1) * kd_ref[h]
                dv = dv + _dot(kdb, dsb, 1, 0)
                ds_ref[h] = ds_new * cd_ref[h] + _dot(qdb, doutb, 0, 0)
                dproj_ref[rows, h * RET_DK:(h + 1) * RET_DK] = _rope_half(dq, cs, -sn).astype(BF16)
                dproj_ref[rows, RET_QK_W + h * RET_DK:RET_QK_W + (h + 1) * RET_DK] = (
                    _rope_half(dk * (RET_DK ** -0.5), cs, -sn).astype(BF16))
                dproj_ref[rows, 2 * RET_QK_W + h * RET_DV:2 * RET_QK_W + (h + 1) * RET_DV] = dv.astype(BF16)
                dproj_ref[rows, 2 * RET_QK_W + RET_V_W + h * RET_DV:
                          2 * RET_QK_W + RET_V_W + (h + 1) * RET_DV] = dgate_ref[rows, cols]
            return carry

        lax.fori_loop(0, cpb, chunk, 0)

    return pl.pallas_call(
        body, name="ret_bwd", grid=(nb,),
        in_specs=[sp['proj'], sp['tab'], sp['tab'], sp['intra'], sp['dec'], sp['dec'], sp['cdec'],
                  sp['st'], sp['vw'], sp['vw']] + [ANY] * len(deps),
        out_specs=sp['proj'],
        out_shape=_sds((t, RET_IN), BF16),
        scratch_shapes=[pltpu.VMEM((RET_HEADS, RET_DK, RET_DV), F32)],
        compiler_params=_cparams(("arbitrary",)),
    )(proj, cos, sin, intra, qdec, kdec, cdec, states, dout, dgate, *deps)


def _spread_rope(a):
    return jnp.pad(a, [(0, 0)] * (a.ndim - 1) + [(0, MLA_ROPE)])


def _gather_rope(a):
    return a[..., :a.shape[-1] - MLA_ROPE]


def _mla_tables(t):
    half = MLA_ROPE // 2
    inv = 1.0 / (ROPE_THETA ** (jnp.arange(0, MLA_ROPE, 2, dtype=F32) / MLA_ROPE))
    ang = jnp.arange(t, dtype=F32)[:, None] * inv[None, :]
    cos, sin = jnp.cos(ang), jnp.sin(ang)
    z = jnp.zeros((t, half), F32)
    c = jnp.concatenate([cos, cos, z, z], axis=1)
    s1 = jnp.concatenate([-sin, z, z, z], axis=1)
    s2 = jnp.concatenate([z, sin, z, z], axis=1)
    return c, s1, s2


def _rope_tile(r, c, s1, s2):
    return r * c + pltpu.roll(r, 96, 1) * s1 + pltpu.roll(r, 32, 1) * s2


def _mla_mid(proj2, qa, kva):
    t = proj2.shape[0]
    tm = _row_tile(t)

    def body(p_ref, qa_ref, kva_ref, cq_ref, ckv_ref):
        cq = p_ref[:, :MLA_Q_RANK]
        ckv = p_ref[:, MLA_Q_RANK:MLA_Q_RANK + MLA_KV_RANK]
        rq = lax.rsqrt(jnp.mean(cq * cq, axis=-1, keepdims=True) + EPS)
        rkv = lax.rsqrt(jnp.mean(ckv * ckv, axis=-1, keepdims=True) + EPS)
        cq_ref[...] = (cq * rq * qa_ref[...]).astype(BF16)
        ckv_ref[...] = (ckv * rkv * kva_ref[...]).astype(BF16)

    return pl.pallas_call(
        body, name="mla_mid", grid=(t // tm,),
        in_specs=[pl.BlockSpec((tm, MLA_IN_PAD), lambda i: (i, 0)),
                  pl.BlockSpec((1, MLA_Q_RANK), lambda i: (0, 0)),
                  pl.BlockSpec((1, MLA_KV_RANK), lambda i: (0, 0))],
        out_specs=[pl.BlockSpec((tm, MLA_Q_RANK), lambda i: (i, 0)),
                   pl.BlockSpec((tm, MLA_KV_RANK), lambda i: (i, 0))],
        out_shape=[_sds((t, MLA_Q_RANK), BF16), _sds((t, MLA_KV_RANK), BF16)],
        compiler_params=_cparams(("parallel",)),
    )(proj2, qa, kva)


def _mla_mid_bwd(proj2, qa, kva, dcq, dckv, dkr):
    t = proj2.shape[0]
    tm = _row_tile(t)

    def body(p_ref, qa_ref, kva_ref, dcq_ref, dckv_ref, dkr_ref, dp_ref, dqa_ref, dkva_ref):
        @pl.when(pl.program_id(0) == 0)
        def _():
            dqa_ref[...] = jnp.zeros_like(dqa_ref)
            dkva_ref[...] = jnp.zeros_like(dkva_ref)

        dxq, dgq = _rms_bwd_rows(dcq_ref[...], p_ref[:, :MLA_Q_RANK], qa_ref[...], MLA_Q_RANK)
        dxk, dgk = _rms_bwd_rows(dckv_ref[...], p_ref[:, MLA_Q_RANK:MLA_Q_RANK + MLA_KV_RANK], kva_ref[...],
                                 MLA_KV_RANK)
        dp_ref[:, :MLA_Q_RANK] = dxq.astype(BF16)
        dp_ref[:, MLA_Q_RANK:MLA_Q_RANK + MLA_KV_RANK] = dxk.astype(BF16)
        dp_ref[:, MLA_Q_RANK + MLA_KV_RANK:] = dkr_ref[...].astype(BF16)
        dqa_ref[...] += jnp.sum(dgq, axis=0, keepdims=True)
        dkva_ref[...] += jnp.sum(dgk, axis=0, keepdims=True)

    return pl.pallas_call(
        body, name="mla_mid_bwd", grid=(t // tm,),
        in_specs=[pl.BlockSpec((tm, MLA_IN_PAD), lambda i: (i, 0)),
                  pl.BlockSpec((1, MLA_Q_RANK), lambda i: (0, 0)),
                  pl.BlockSpec((1, MLA_KV_RANK), lambda i: (0, 0)),
                  pl.BlockSpec((tm, MLA_Q_RANK), lambda i: (i, 0)),
                  pl.BlockSpec((tm, MLA_KV_RANK), lambda i: (i, 0)),
                  pl.BlockSpec((tm, 128), lambda i: (i, 0))],
        out_specs=[pl.BlockSpec((tm, MLA_IN_PAD), lambda i: (i, 0)),
                   pl.BlockSpec((1, MLA_Q_RANK), lambda i: (0, 0)),
                   pl.BlockSpec((1, MLA_KV_RANK), lambda i: (0, 0))],
        out_shape=[_sds((t, MLA_IN_PAD), BF16), _sds((1, MLA_Q_RANK), F32), _sds((1, MLA_KV_RANK), F32)],
        compiler_params=_cparams(("arbitrary",)),
    )(proj2, qa, kva, dcq, dckv, dkr)


def _mla_prep_specs(t, tm):
    head = lambda w: pl.BlockSpec((None, tm, w), lambda i, h: (h, i, 0))
    return dict(
        head256=head(MLA_HD_PAD), head128=head(MLA_VD),
        cols256=pl.BlockSpec((tm, MLA_HD_PAD), lambda i, h: (i, h)),
        cq=pl.BlockSpec((tm, MLA_Q_RANK), lambda i, h: (i, 0)),
        ckv=pl.BlockSpec((tm, MLA_KV_RANK), lambda i, h: (i, 0)),
        wuq=pl.BlockSpec((None, MLA_Q_RANK, MLA_HD_PAD), lambda i, h: (h, 0, 0)),
        wukv=pl.BlockSpec((None, MLA_KV_RANK, MLA_HD_PAD), lambda i, h: (h, 0, 0)),
        kr=pl.BlockSpec((tm, 128), lambda i, h: (i, (MLA_Q_RANK + MLA_KV_RANK) // 128)),
        gain=pl.BlockSpec((1, MLA_HD_PAD), lambda i, h: (0, 0)),
        tab=pl.BlockSpec((tm, 128), lambda i, h: (i, 0)),
    )


def _mla_prep(cq, ckv, wuq, wukv, proj2, gq, gk, tabs):
    t = cq.shape[0]
    tm = _row_tile(t, PREP_ROWS)
    sp = _mla_prep_specs(t, tm)

    def body(cq_ref, ckv_ref, wuq_ref, wukv_ref, kr_ref, gq_ref, gk_ref, c_ref, s1_ref, s2_ref,
             qh_ref, kh_ref, vh_ref):
        c, s1, s2 = c_ref[...], s1_ref[...], s2_ref[...]

        def norm_rope(xv, gain):
            r = lax.rsqrt(jnp.sum(xv * xv, axis=-1, keepdims=True) / MLA_QKD + EPS)
            y = xv * r * gain
            return jnp.concatenate([y[:, :MLA_NOPE], _rope_tile(y[:, MLA_NOPE:], c, s1, s2)], axis=-1)

        kvv = _dot(ckv_ref[...], wukv_ref[...], 1, 0)
        qh_ref[...] = norm_rope(_dot(cq_ref[...], wuq_ref[...], 1, 0), gq_ref[...]).astype(BF16)
        kf = jnp.concatenate([kvv[:, :MLA_NOPE], kr_ref[...]], axis=-1)
        kh_ref[...] = norm_rope(kf, gk_ref[...]).astype(BF16)
        vh_ref[...] = jnp.concatenate([kvv[:, MLA_NOPE:], jnp.ones((tm, MLA_VD), F32)], axis=-1).astype(BF16)

    return pl.pallas_call(
        body, name="mla_prep", grid=(t // tm, MLA_HEADS),
        in_specs=[sp['cq'], sp['ckv'], sp['wuq'], sp['wukv'], sp['kr'], sp['gain'], sp['gain'],
                  sp['tab'], sp['tab'], sp['tab']],
        out_specs=[sp['head256'], sp['head256'], sp['head256']],
        out_shape=[_sds((MLA_HEADS, t, MLA_HD_PAD), BF16), _sds((MLA_HEADS, t, MLA_HD_PAD), BF16),
                   _sds((MLA_HEADS, t, 2 * MLA_VD), BF16)],
        compiler_params=_cparams(("parallel", "arbitrary")),
    )(cq, ckv, wuq, wukv, proj2, gq, gk, *tabs)


def _mla_prep_bwd(cq, ckv, wuq, wukv, proj2, gq, gk, tabs, dqt, dkh, dvh):
    t = cq.shape[0]
    tm = _row_tile(t, PREP_ROWS)
    ab = dqt.shape[-1]
    sp = _mla_prep_specs(t, tm)

    def body(cq_ref, ckv_ref, wuq_ref, wukv_ref, kr_ref, gq_ref, gk_ref, c_ref, s1_ref, s2_ref,
             dqt_ref, dkh_ref, dvh_ref, dq_ref, dkv_ref, dkr_ref, dgq_ref, dgk_ref):
        dqh = jnp.concatenate([dqt_ref[b].T for b in range(tm // ab)], axis=0)
        i, h = pl.program_id(0), pl.program_id(1)

        @pl.when((i == 0) & (h == 0))
        def _():
            dgq_ref[...] = jnp.zeros_like(dgq_ref)
            dgk_ref[...] = jnp.zeros_like(dgk_ref)

        @pl.when(h == 0)
        def _():
            dkr_ref[...] = jnp.zeros_like(dkr_ref)

        c, s1, s2 = c_ref[...], s1_ref[...], s2_ref[...]

        def back(xv, gain, dout):
            dy = jnp.concatenate([dout[:, :MLA_NOPE], _rope_tile(dout[:, MLA_NOPE:], c, -s1, -s2)], axis=-1)
            return _rms_bwd_rows(dy, xv, gain, MLA_QKD)

        kvv = _dot(ckv_ref[...], wukv_ref[...], 1, 0)
        dxq, dgq = back(_dot(cq_ref[...], wuq_ref[...], 1, 0), gq_ref[...], dqh)
        kf = jnp.concatenate([kvv[:, :MLA_NOPE], kr_ref[...]], axis=-1)
        dxk, dgk = back(kf, gk_ref[...], dkh_ref[...])
        dq_ref[...] = dxq.astype(BF16)
        dkv_ref[...] = jnp.concatenate([dxk[:, :MLA_NOPE], dvh_ref[...]], axis=-1).astype(BF16)
        dkr_ref[...] += dxk[:, MLA_NOPE:]
        dgq_ref[...] += jnp.sum(dgq, axis=0, keepdims=True)
        dgk_ref[...] += jnp.sum(dgk, axis=0, keepdims=True)

    return pl.pallas_call(
        body, name="mla_prep_bwd", grid=(t // tm, MLA_HEADS),
        in_specs=[sp['cq'], sp['ckv'], sp['wuq'], sp['wukv'], sp['kr'], sp['gain'], sp['gain'],
                  sp['tab'], sp['tab'], sp['tab'],
                  pl.BlockSpec((None, tm // ab, MLA_HD_PAD, ab), lambda i, h: (h, i, 0, 0)),
                  sp['head256'], sp['head128']],
        out_specs=[sp['cols256'], sp['cols256'], sp['tab'], sp['gain'], sp['gain']],
        out_shape=[_sds((t, MLA_HEADS * MLA_HD_PAD), BF16), _sds((t, MLA_HEADS * MLA_HD_PAD), BF16),
                   _sds((t, 128), F32), _sds((1, MLA_HD_PAD), F32), _sds((1, MLA_HD_PAD), F32)],
        compiler_params=_cparams(("arbitrary", "arbitrary")),
    )(cq, ckv, wuq, wukv, proj2, gq, gk, *tabs, dqt, dkh, dvh)


def _chunk_visible(rows, cols, row_off, col_off):
    rq = lax.shift_right_logical(lax.broadcasted_iota(jnp.int32, (rows, cols), 0) + row_off, 6)
    ck = lax.shift_right_logical(lax.broadcasted_iota(jnp.int32, (rows, cols), 1) + col_off, 6)
    return ck <= rq


def _rows_to_lanes(col):
    return col.T[:8, :]


def _attn_fwd(qh, kh, vh):
    t = qh.shape[1]
    ab = min(ATT_BLOCK, t)
    tq = min(ATT_QROWS, t)
    r = tq // ab
    hg = ATT_HEADS

    def body(q_ref, k_ref, v_ref, o_ref, lse_ref, acc_ref):
        n_un = pl.program_id(1) * r
        acc_ref[...] = jnp.zeros_like(acc_ref)

        def step(b, ms, diag):
            rows = pl.ds(pl.multiple_of(b * ab, ab), ab)
            out = []
            for hh in range(hg):
                m = ms[hh]
                s = _dot(q_ref[hh], k_ref[hh, rows, :], 1, 1)
                if diag is not None:
                    s = jnp.where(_chunk_visible(tq, ab, 0, diag * ab), s, -1e30)
                m_new = jnp.maximum(m, jnp.max(s, axis=-1, keepdims=True))
                p = jnp.exp2((s - m_new) * ATT_EXP2).astype(BF16)
                acc_ref[hh] = jnp.exp2((m - m_new) * ATT_EXP2) * acc_ref[hh] + _dot(p, v_ref[hh, rows, :], 1, 0)
                out.append(m_new)
            return tuple(out)

        ms = tuple(jnp.full((tq, 1), -1e30, F32) for _ in range(hg))
        ms = lax.fori_loop(0, n_un, lambda b, st: step(b, st, None), ms)
        for d in range(r):
            ms = step(n_un + d, ms, d)
        for hh in range(hg):
            l = acc_ref[hh, :, MLA_VD:]
            o_ref[:, hh * MLA_VD:(hh + 1) * MLA_VD] = acc_ref[hh, :, :MLA_VD] / l
            lse_t = _rows_to_lanes(ms[hh] * ATT_EXP2 + jnp.log(l) * LOG2E)
            for d in range(r):
                lse_ref[hh, d] = lse_t[:, d * ab:(d + 1) * ab]

    return pl.pallas_call(
        body, name="mla_attn", grid=(MLA_HEADS // hg, t // tq),
        in_specs=[pl.BlockSpec((hg, tq, MLA_HD_PAD), lambda g, i: (g, i, 0)),
                  pl.BlockSpec((hg, t, MLA_HD_PAD), lambda g, i: (g, 0, 0)),
                  pl.BlockSpec((hg, t, 2 * MLA_VD), lambda g, i: (g, 0, 0))],
        out_specs=[pl.BlockSpec((tq, hg * MLA_VD), lambda g, i: (i, g)),
                   pl.BlockSpec((hg, r, 8, ab), lambda g, i: (g, i, 0, 0))],
        out_shape=[_sds((t, MLA_HEADS * MLA_VD), F32), _sds((MLA_HEADS, t // ab, 8, ab), F32)],
        scratch_shapes=[pltpu.VMEM((hg, tq, 2 * MLA_VD), F32)],
        compiler_params=_cparams(("parallel", "arbitrary")),
    )(qh, kh, vh)


def _attn_bwd(qh, kh, vh, dob, o, lse_t):
    t = qh.shape[1]
    ab = min(ATT_BLOCK, t)
    kb = min(ATT_KROWS, t)
    r = kb // ab
    nq = t // ab
    hg = ATT_HEADS

    def body(q_ref, k_ref, v_ref, do_ref, o_ref, lse_ref, dqt_ref, dk_ref, dv_ref, dl_ref):
        j = pl.program_id(1)

        @pl.when(j == 0)
        def _():
            dqt_ref[...] = jnp.zeros_like(dqt_ref)
            ones = jnp.ones((8, MLA_VD), F32)

            def delta(b, carry):
                rows = pl.ds(pl.multiple_of(b * ab, ab), ab)
                for hh in range(hg):
                    cols = slice(hh * MLA_VD, (hh + 1) * MLA_VD)
                    prod = do_ref[rows, cols].astype(F32) * o_ref[rows, cols]
                    dl_ref[hh, b] = lax.dot_general(ones, prod, (((1,), (1,)), ((), ())),
                                                    precision=lax.Precision.HIGHEST, preferred_element_type=F32)
                return carry

            lax.fori_loop(0, nq, delta, 0)

        ks = [k_ref[hh] for hh in range(hg)]
        vs = [v_ref[hh, :, :MLA_VD] for hh in range(hg)]
        kts = [k.T for k in ks]

        dk_ref[...] = jnp.zeros_like(dk_ref)
        dv_ref[...] = jnp.zeros_like(dv_ref)

        def step(b, carry, diag):
            rows = pl.ds(pl.multiple_of(b * ab, ab), ab)
            hi = kb if diag is None else (diag + 1) * ab
            for hh in range(hg):
                q = q_ref[hh, rows, :]
                do = do_ref[rows, hh * MLA_VD:(hh + 1) * MLA_VD]
                s_t = _dot(ks[hh][:hi], q, 1, 1)
                if diag is not None:
                    key_chunk = lax.shift_right_logical(lax.broadcasted_iota(jnp.int32, (hi, ab), 0), 6)
                    query_chunk = lax.shift_right_logical(
                        lax.broadcasted_iota(jnp.int32, (hi, ab), 1) + diag * ab, 6)
                    s_t = jnp.where(key_chunk <= query_chunk, s_t, -1e30)
                p_t = jnp.exp2(s_t * ATT_EXP2 - lse_ref[hh, b][0:1, :])
                dp_t = _dot(vs[hh][:hi], do, 1, 1)
                ds_t = (p_t * (dp_t - dl_ref[hh, b][0:1, :]) * ATT_SCALE).astype(BF16)
                dqt_ref[hh, b] += _dot(kts[hh][:, :hi], ds_t, 1, 0)
                dk_ref[hh, :hi] += _dot(ds_t, q, 1, 0)
                dv_ref[hh, :hi] += _dot(p_t.astype(BF16), do, 1, 0)
            return carry

        for d in range(r):
            step(j * r + d, 0, d)
        lax.fori_loop((j + 1) * r, nq, lambda b, c: step(b, c, None), 0)

    whole = lambda w: pl.BlockSpec((hg, t, w), lambda g, j: (g, 0, 0))
    blk = lambda w: pl.BlockSpec((hg, kb, w), lambda g, j: (g, j, 0))
    stat = pl.BlockSpec((hg, nq, 8, ab), lambda g, j: (g, 0, 0, 0))
    cols = pl.BlockSpec((t, hg * MLA_VD), lambda g, j: (0, g))
    return pl.pallas_call(
        body, name="mla_attn_bwd", grid=(MLA_HEADS // hg, t // kb),
        in_specs=[whole(MLA_HD_PAD), blk(MLA_HD_PAD), blk(2 * MLA_VD),
                  cols, cols, stat],
        out_specs=[pl.BlockSpec((hg, nq, MLA_HD_PAD, ab), lambda g, j: (g, 0, 0, 0)), blk(MLA_HD_PAD), blk(MLA_VD)],
        out_shape=[_sds((MLA_HEADS, nq, MLA_HD_PAD, ab), F32), _sds((MLA_HEADS, t, MLA_HD_PAD), F32),
                   _sds((MLA_HEADS, t, MLA_VD), F32)],
        scratch_shapes=[pltpu.VMEM((hg, nq, 8, ab), F32)],
        compiler_params=_cparams(("parallel", "arbitrary")),
    )(qh, kh, vh, dob, o, lse_t)


VEC = pl.BlockSpec((1, D_MODEL), lambda i, j, k: (0, 0))


def _rows(tm, width):
    return pl.BlockSpec((tm, width), lambda i, j, k: (i, 0))


def _residual_epi(next_gain):
    if next_gain is None:
        return [], lambda acc, hv: (acc + hv,)

    def epi(acc, hv, g):
        h_new = acc + hv
        r = lax.rsqrt(jnp.mean(h_new * h_new, axis=-1, keepdims=True) + EPS)
        return h_new, h_new * r * g

    return [(next_gain, VEC)], epi


def _residual_outs(t, row, next_gain):
    outs = [(_sds((t, D_MODEL), F32), row)]
    return outs + ([(_sds((t, D_MODEL), BF16), row)] if next_gain is not None else [])


def _mlp_fwd(l, h, hn, w1g, fetch_w2, next_gain):
    t = h.shape[0]
    tm = _row_tile(t, 512)

    def relu2(acc):
        r = jnp.maximum(acc, 0.0)
        return (r * r,)

    (u,) = _mm_rows(f"mlp_up{l}", tm, hn, w1g, 'nn_cols', [(_sds((t, D_FF), BF16), _rows(tm, D_FF))], epi=relu2)
    w2g = fetch_w2((u,))
    row = _rows(tm, D_MODEL)
    more, epi = _residual_epi(next_gain)
    h2, hn_next = _mm_rows(f"mlp_down{l}", tm, u, w2g, 'nn_rows', _residual_outs(t, row, next_gain),
                           extras=[(h, row)] + more, epi=epi)
    return h2, hn_next, (h, hn, u, w1g, w2g)


def _norm_bwd_outs(t, tm):
    return [(_sds((t, D_MODEL), F32), pl.BlockSpec((tm, D_MODEL), lambda i, j, k: (i, 0))),
            (_sds((t // tm, 1, D_MODEL), F32), pl.BlockSpec((None, 1, D_MODEL), lambda i, j, k: (i, 0, 0)))]


def _norm_bwd_epi(acc, xv, res, g):
    dx, dgr = _rms_bwd_rows(acc, xv, g, D_MODEL)
    return res + dx, jnp.sum(dgr, axis=0, keepdims=True)


def _mlp_bwd(l, dh, saved, norm_g):
    h, hn, u, w1g, w2g = saved
    t = h.shape[0]
    tm = _row_tile(t, 512)
    nsh, _, wsh = w1g.shape
    wide = _rows(tm, D_FF)
    (da,) = _mm_rows(f"mlp_du{l}", tm, dh, w2g, 'nt_rows', [(_sds((t, D_FF), BF16), wide)], extras=[(u, wide)],
                     epi=lambda acc, uv: (2.0 * jnp.sqrt(uv.astype(F32)) * acc,))
    tw = _row_tile(t, 512)
    (dw2,) = _mm(f"mlp_dw2{l}", (1, 1, t // tw),
                 u, pl.BlockSpec((tw, D_FF), lambda i, j, k: (k, 0)),
                 dh, pl.BlockSpec((tw, D_MODEL), lambda i, j, k: (k, 0)), (0, 0),
                 [(_sds((D_FF, D_MODEL), BF16), pl.BlockSpec((D_FF, D_MODEL), lambda i, j, k: (0, 0)))])
    dw2 = dw2.reshape(nsh, wsh, D_MODEL)
    (dw1,) = _mm(f"mlp_dw1{l}", (1, 1, t // tw),
                 hn, pl.BlockSpec((tw, D_MODEL), lambda i, j, k: (k, 0)),
                 da, pl.BlockSpec((tw, D_FF), lambda i, j, k: (k, 0)), (0, 0),
                 [(_sds((nsh, D_MODEL, wsh), BF16), pl.BlockSpec((nsh, D_MODEL, wsh), lambda i, j, k: (0, 0, 0)))],
                 split=wsh)
    row = _rows(tm, D_MODEL)
    dh_in, dg = _mm_rows(f"mlp_dhn{l}", tm, da, w1g, 'nt_cols', _norm_bwd_outs(t, tm),
                         extras=[(h, row), (dh, row), (norm_g, VEC)], epi=_norm_bwd_epi)
    return dh_in, jnp.sum(dg, axis=0), dw1, dw2


def _ple_fwd(l, h, hn, p, wg, wp, next_gain, target=None):
    t = h.shape[0]
    tm = _row_tile(t, 512)
    row = pl.BlockSpec((tm, D_MODEL), lambda i, j, k: (i, 0))
    full = lambda r: pl.BlockSpec((r, D_MODEL), lambda i, j, k: (0, 0))
    f32_row, bf_row = (_sds((t, D_MODEL), F32), row), (_sds((t, D_MODEL), BF16), row)
    common = [(h, row), (p, pl.BlockSpec((None, None, tm, PLE_DIM), lambda i, j, k: (l, 0, i, 0))),
              (wp, full(PLE_DIM))]
    if target is not None:
        def loss_epi(acc, hv, pv, wpv, tv):
            gt = _sigmoid(acc)
            ev = _dot(_bf(pv), wpv, 1, 0)
            err = hv + gt * ev - tv
            sq = jnp.sum(jnp.sum(err * err, axis=-1, keepdims=True), axis=0, keepdims=True)
            return err / D_MODEL, gt, ev, jnp.broadcast_to(sq, (8, 128))

        dy, gate, e, sq = _mm(f"ple_gate{l}", (t // tm, 1, 1), hn, row, wg, full(D_MODEL), (1, 0),
                              [f32_row, bf_row, bf_row, (_sds((t // tm, 8, 128), F32),
                                                         pl.BlockSpec((None, 8, 128), lambda i, j, k: (i, 0, 0)))],
                              extras=common + [(target, row)], epi=loss_epi)
        return dy, jnp.sum(sq, axis=0), (h, hn, gate, e)

    def gate_epi(acc, hv, pv, wpv, *gain):
        gt = _sigmoid(acc)
        ev = _dot(_bf(pv), wpv, 1, 0)
        h_new = hv + gt * ev
        if not gain:
            return h_new, gt, ev
        r = lax.rsqrt(jnp.mean(h_new * h_new, axis=-1, keepdims=True) + EPS)
        return h_new, gt, ev, h_new * r * gain[0]

    res = _mm(f"ple_gate{l}", (t // tm, 1, 1), hn, row, wg, full(D_MODEL), (1, 0),
              [f32_row, bf_row, bf_row] + ([bf_row] if next_gain is not None else []),
              extras=common + ([(next_gain, VEC)] if next_gain is not None else []), epi=gate_epi)
    h_out, gate, e = res[0], res[1], res[2]
    return h_out, (res[3] if next_gain is not None else None), (h, hn, gate, e)


def _ple_bwd(l, dh, saved, p, norm_g, wg, deps=()):
    h, hn, gate, e = saved
    t = h.shape[0]
    tm = _row_tile(t)
    tk = _row_tile(t, 512)
    de, dz = _ple_gate_bwd(f"ple_gate_bwd{l}", dh, gate, e)
    full = lambda r: pl.BlockSpec((r, D_MODEL), lambda i, j, k: (0, 0))
    rowk = pl.BlockSpec((tk, D_MODEL), lambda i, j, k: (k, 0))
    (dwp,) = _mm(f"ple_dwp{l}", (1, 1, t // tk),
                 p, pl.BlockSpec((None, None, tk, PLE_DIM), lambda i, j, k: (l, 0, k, 0)),
                 de, rowk, (0, 0), [(_sds((PLE_DIM, D_MODEL), BF16), full(PLE_DIM))], deps=deps)
    (dwg,) = _mm(f"ple_dwg{l}", (1, 1, t // tk), hn, rowk, dz, rowk, (0, 0),
                 [(_sds((D_MODEL, D_MODEL), BF16), full(D_MODEL))])
    row = pl.BlockSpec((tm, D_MODEL), lambda i, j, k: (i, 0))
    dh_in, dg = _mm(f"ple_dhn{l}", (t // tm, 1, 1), dz, row, wg, full(D_MODEL), (1, 1),
                    _norm_bwd_outs(t, tm), extras=[(h, row), (dh, row), (norm_g, VEC)], epi=_norm_bwd_epi)
    return dh_in, jnp.sum(dg, axis=0), dwg, dwp


def _ret_layer_fwd(x, norm_g, wri, fetch_wro, gn, cos, sin, next_gain, hn=None, deps=()):
    t = x.shape[0]
    tm = _row_tile(t)
    nsh, _, wsh = wri.shape
    if hn is None:
        hn = _rms_fwd("mix_norm0", x, norm_g)
    tp = _row_tile(t, 512)
    (proj,) = _mm_rows("ret_in", tp, hn, wri, 'nn_cols', [(_sds((t, RET_IN), BF16), _rows(tp, RET_IN))], deps=deps)
    gated, outp, states = _ret_fwd(proj, cos, sin, gn)
    wro = fetch_wro((gated,))
    row = _rows(tp, D_MODEL)
    more, epi = _residual_epi(next_gain)
    h1, hn_next = _mm_rows("ret_out", tp, gated, wro.reshape(RET_HEADS, RET_DV, D_MODEL), 'nn_rows',
                           _residual_outs(t, row, next_gain), extras=[(x, row)] + more, epi=epi)
    return h1, hn_next, (x, hn, proj, gated, outp, states, wro)


def _ret_layer_bwd(dh, saved, norm_g, wri, gn, cos, sin, emit_out, emit_in, deps=()):
    x, hn, proj, gated, outp, states, wro = saved
    t = x.shape[0]
    tm = _row_tile(t)
    tk = _row_tile(t, 512)
    nsh, _, wsh = wri.shape
    tg = _row_tile(t, 512)
    vw = _rows(tg, RET_V_W)
    dout, dgate, dgn = _mm_rows(
        "ret_dgate", tg, dh, wro.reshape(RET_HEADS, RET_DV, D_MODEL), 'nt_rows',
        [(_sds((t, RET_V_W), BF16), vw), (_sds((t, RET_V_W), BF16), vw),
         (_sds((t // tg, 1, RET_V_W), F32), pl.BlockSpec((None, 1, RET_V_W), lambda i, j, k: (i, 0, 0)))],
        extras=[(outp, vw), (proj, pl.BlockSpec((tg, RET_V_W), lambda i, j, k: (i, (RET_IN - RET_V_W) // RET_V_W))),
                (gn.reshape(1, RET_V_W), pl.BlockSpec((1, RET_V_W), lambda i, j, k: (0, 0)))],
        epi=_ret_gate_bwd_epi, deps=deps)
    dgn = jnp.sum(dgn, axis=0)
    (dwro,) = _mm("ret_dwro", (1, 1, t // tk),
                  gated, pl.BlockSpec((tk, RET_V_W), lambda i, j, k: (k, 0)),
                  dh, pl.BlockSpec((tk, D_MODEL), lambda i, j, k: (k, 0)), (0, 0),
                  [(_sds((RET_V_W, D_MODEL), BF16), pl.BlockSpec((RET_V_W, D_MODEL), lambda i, j, k: (0, 0)))])
    dproj = _ret_bwd(proj, cos, sin, states, dout, dgate, deps=emit_out(dwro))
    half = nsh // 2
    (dwri,) = _mm("ret_dwri", (2, 1, t // tk),
                  hn, pl.BlockSpec((tk, D_MODEL), lambda i, j, k: (k, 0)),
                  dproj, pl.BlockSpec((tk, half * wsh), lambda i, j, k: (k, i)), (0, 0),
                  [(_sds((nsh, D_MODEL, wsh), BF16), pl.BlockSpec((half, D_MODEL, wsh), lambda i, j, k: (i, 0, 0)))],
                  split=wsh)
    deps = emit_in(dwri)
    td = _row_tile(t, 256)
    row = _rows(td, D_MODEL)
    dx, dg = _mm_rows("ret_dhn", td, dproj, wri, 'nt_cols', _norm_bwd_outs(t, td),
                      extras=[(x, row), (dh, row), (norm_g, VEC)], epi=_norm_bwd_epi, deps=deps)
    return dx, jnp.sum(dg, axis=0), dgn.reshape(RET_HEADS, RET_DV)


def _mla_layer_fwd(h, hn, wmi, qa, kva, wuq, wukv, gq, gk, wmo, tabs, next_gain):
    t = h.shape[0]
    tm = _row_tile(t)
    row = pl.BlockSpec((tm, D_MODEL), lambda i, j, k: (i, 0))
    (proj2,) = _mm("mla_in", (t // tm, 1, 1), hn, row,
                   wmi, pl.BlockSpec((D_MODEL, MLA_IN_PAD), lambda i, j, k: (0, 0)), (1, 0),
                   [(_sds((t, MLA_IN_PAD), F32), pl.BlockSpec((tm, MLA_IN_PAD), lambda i, j, k: (i, 0)))])
    cq, ckv = _mla_mid(proj2, qa, kva)
    qh, kh, vh = _mla_prep(cq, ckv, wuq, wukv, proj2, gq, gk, tabs)
    o, lse = _attn_fwd(qh, kh, vh)
    more, epi = _residual_epi(next_gain)
    h_out, hn_next = _mm("mla_out", (t // tm, 1, 1), o, row,
                         wmo, pl.BlockSpec((D_MODEL, D_MODEL), lambda i, j, k: (0, 0)), (1, 0),
                         _residual_outs(t, row, next_gain), extras=[(h, row)] + more, epi=epi)
    return h_out, hn_next, (h, hn, proj2, cq, ckv, qh, kh, vh, o, lse)


def _mla_layer_bwd(dh, saved, norm_g, wmi, qa, kva, wuq, wukv, gq, gk, wmo, tabs, deps=()):
    h, hn, proj2, cq, ckv, qh, kh, vh, o, lse = saved
    t = h.shape[0]
    tm = _row_tile(t)
    tk = _row_tile(t, 512)
    row = pl.BlockSpec((tm, D_MODEL), lambda i, j, k: (i, 0))
    rowk = pl.BlockSpec((tk, D_MODEL), lambda i, j, k: (k, 0))
    sq = pl.BlockSpec((D_MODEL, D_MODEL), lambda i, j, k: (0, 0))
    (dob,) = _mm("mla_do", (t // tm, 1, 1), dh, row, wmo, sq, (1, 1), [(_sds((t, D_MODEL), BF16), row)], deps=deps)
    (dwmo,) = _mm("mla_dwo", (1, 1, t // tk), o, rowk, dh, rowk, (0, 0), [(_sds((D_MODEL, D_MODEL), BF16), sq)])
    dqt, dkh, dvh = _attn_bwd(qh, kh, vh, dob, o, lse)
    dq, dkv, dkr, dgq, dgk = _mla_prep_bwd(cq, ckv, wuq, wukv, proj2, gq, gk, tabs, dqt, dkh, dvh)

    wide = MLA_HEADS * MLA_HD_PAD
    widek = pl.BlockSpec((tk, wide), lambda i, j, k: (k, 0))
    (dwuq,) = _mm("mla_dwuq", (1, 1, t // tk),
                  cq, pl.BlockSpec((tk, MLA_Q_RANK), lambda i, j, k: (k, 0)), dq, widek, (0, 0),
                  [(_sds((MLA_HEADS, MLA_Q_RANK, MLA_HD_PAD), BF16),
                    pl.BlockSpec((MLA_HEADS, MLA_Q_RANK, MLA_HD_PAD), lambda i, j, k: (0, 0, 0)))], split=MLA_HD_PAD)
    (dwukv,) = _mm("mla_dwukv", (1, 1, t // tk),
                   ckv, pl.BlockSpec((tk, MLA_KV_RANK), lambda i, j, k: (k, 0)), dkv, widek, (0, 0),
                   [(_sds((MLA_HEADS, MLA_KV_RANK, MLA_HD_PAD), BF16),
                     pl.BlockSpec((MLA_HEADS, MLA_KV_RANK, MLA_HD_PAD), lambda i, j, k: (0, 0, 0)))],
                   split=MLA_HD_PAD)
    side_by_side = lambda wg: wg.transpose(1, 0, 2).reshape(wg.shape[1], wide)
    widei = pl.BlockSpec((tm, wide), lambda i, j, k: (i, 0))
    (dcq,) = _mm("mla_dcq", (t // tm, 1, 1), dq, widei,
                 side_by_side(wuq), pl.BlockSpec((MLA_Q_RANK, wide), lambda i, j, k: (0, 0)), (1, 1),
                 [(_sds((t, MLA_Q_RANK), F32), pl.BlockSpec((tm, MLA_Q_RANK), lambda i, j, k: (i, 0)))])
    (dckv,) = _mm("mla_dckv", (t // tm, 1, 1), dkv, widei,
                  side_by_side(wukv), pl.BlockSpec((MLA_KV_RANK, wide), lambda i, j, k: (0, 0)), (1, 1),
                  [(_sds((t, MLA_KV_RANK), F32), pl.BlockSpec((tm, MLA_KV_RANK), lambda i, j, k: (i, 0)))])
    dproj2, dqa, dkva = _mla_mid_bwd(proj2, qa, kva, dcq, dckv, dkr)
    win = pl.BlockSpec((D_MODEL, MLA_IN_PAD), lambda i, j, k: (0, 0))
    (dwmi,) = _mm("mla_dwin", (1, 1, t // tk), hn, rowk,
                  dproj2, pl.BlockSpec((tk, MLA_IN_PAD), lambda i, j, k: (k, 0)), (0, 0),
                  [(_sds((D_MODEL, MLA_IN_PAD), BF16), win)])
    dh_in, dg = _mm("mla_dhn", (t // tm, 1, 1),
                    dproj2, pl.BlockSpec((tm, MLA_IN_PAD), lambda i, j, k: (i, 0)), wmi, win, (1, 1),
                    _norm_bwd_outs(t, tm), extras=[(h, row), (dh, row), (norm_g, VEC)], epi=_norm_bwd_epi)
    return dh_in, dict(mix=jnp.sum(dg, axis=0), wmi=dwmi, qa=dqa, kva=dkva, wuq=dwuq, wukv=dwukv, gq=dgq, gk=dgk,
                       wmo=dwmo)


def _local_step(x, p, target, w, fetch, emit=lambda group: ()):
    t = x.shape[0]
    inv = 1.0 / (ROPE_THETA ** (jnp.arange(0, RET_DK, 2, dtype=F32) / RET_DK))
    ang = jnp.arange(t, dtype=F32)[:, None] * inv[None, :]
    cos_r, sin_r = jnp.cos(ang), jnp.sin(ang)
    tabs = _mla_tables(t)
    row = lambda a, i: a[i:i + 1]

    h1, hn1, s_ret = _ret_layer_fwd(x, row(w['mix_norm'], 0), w['ret_w_in'],
                                    lambda after: fetch('ret_out', after)['ret_w_out'], w['ret_gn'], cos_r, sin_r,
                                    row(w['mlp_norm'], 0), hn=w.get('hn0'), deps=w['deps'])
    h2, hn2, s_mlp0 = _mlp_fwd(0, h1, hn1, fetch('mlp_w1_0', (h1,))['mlp_w1'],
                               lambda after: fetch('mlp_w2_0', after)['mlp_w2'], row(w['ple_norm'], 0))
    w0 = fetch('ple_0', (h2,))
    h3, hn3, s_ple0 = _ple_fwd(0, h2, hn2, p, w0['ple_gate_w'], w0['ple_proj_w'], row(w['mix_norm'], 1))
    wm = fetch('mla', (h3,))
    mla_w = (wm['mla_w_in'], w['mla_q_a_norm'], w['mla_kv_a_norm'], wm['mla_w_uq'], wm['mla_w_ukv'],
             w['mla_q_norm'], w['mla_k_norm'], wm['mla_w_out'], tabs)
    h4, hn4, s_mla = _mla_layer_fwd(h3, hn3, *mla_w, row(w['mlp_norm'], 1))
    w1 = fetch('layer_1', (h4,))
    h5, hn5, s_mlp1 = _mlp_fwd(1, h4, hn4, w1['mlp_w1'], lambda after: w1['mlp_w2'], row(w['ple_norm'], 1))
    dy, sq_err, s_ple1 = _ple_fwd(1, h5, hn5, p, w1['ple_gate_w'], w1['ple_proj_w'], None, target)

    n = N_DEV
    colsh = lambda a: a.reshape(a.shape[0], n, a.shape[1] // n).transpose(1, 0, 2)
    rowsh = lambda a: a.reshape(n, a.shape[0] // n, a.shape[1])
    big = {}

    def emit_group(group):
        big.update(group)
        return emit(group)

    dh5, dg_ple1, dwg1, dwp1 = _ple_bwd(1, dy, s_ple1, p, row(w['ple_norm'], 1), w1['ple_gate_w'])
    dh4, dg_mlp1, dw1_1, dw2_1 = _mlp_bwd(1, dh5, s_mlp1, row(w['mlp_norm'], 1))
    deps = emit_group({('ple_gate_w', 1): rowsh(dwg1), ('ple_proj_w', 1): colsh(dwp1),
                       ('mlp_w2', 1): dw2_1, ('mlp_w1', 1): dw1_1})
    dh3, gm = _mla_layer_bwd(dh4, s_mla, row(w['mix_norm'], 1), *mla_w, deps=deps)
    deps = emit_group({('mla_w_out', 0): rowsh(gm['wmo']), ('mla_w_uq', 0): _gather_rope(gm['wuq']),
                       ('mla_w_ukv', 0): gm['wukv'], ('mla_w_in', 0): rowsh(_gather_rope(gm['wmi']))})
    dh2, dg_ple0, dwg0, dwp0 = _ple_bwd(0, dh3, s_ple0, p, row(w['ple_norm'], 0), w0['ple_gate_w'], deps=deps)
    dh1, dg_mlp0, dw1_0, dw2_0 = _mlp_bwd(0, dh2, s_mlp0, row(w['mlp_norm'], 0))
    deps = emit_group({('ple_gate_w', 0): rowsh(dwg0), ('ple_proj_w', 0): colsh(dwp0),
                       ('mlp_w2', 0): dw2_0, ('mlp_w1', 0): dw1_0})
    dx, dg_mix0, dgn = _ret_layer_bwd(
        dh1, s_ret, row(w['mix_norm'], 0), w['ret_w_in'], w['ret_gn'], cos_r, sin_r,
        lambda dwro: emit_group({('ret_w_out', 0): rowsh(dwro)}),
        lambda dwri: emit_group({('ret_w_in', 0): dwri}), deps=deps)

    small = dict(
        mix_norm=[dg_mix0, gm['mix']], mlp_norm=[dg_mlp0, dg_mlp1], ple_norm=[dg_ple0, dg_ple1],
        ret_gn=dgn, mla_q_a_norm=gm['qa'], mla_kv_a_norm=gm['kva'], mla_q_norm=gm['gq'], mla_k_norm=gm['gk'],
    )
    return sq_err, dx, big, small


def _my_place():
    x, y, c = lax.axis_index("x"), lax.axis_index("y"), lax.axis_index("c")
    return x, y, c


def _flat(px, py, pc):
    return 4 * px + 2 * py + pc


def _peer(x, y, c, r):
    return (1 - x if r & 4 else x, 1 - y if r & 2 else y, 1 - c if r & 1 else c)


HBM = pl.BlockSpec(memory_space=pltpu.HBM)
SEMS = pl.BlockSpec(memory_space=pltpu.SEMAPHORE)
SIDE_EFFECT = pltpu.SideEffectType.DATAFLOW_SIDE_EFFECTING


def _rs_copies(x, y, c, srcs, lands, send_sems, recv_sems):
    copies = []
    for a in range(len(srcs)):
        for r in range(1, N_DEV):
            peer = _peer(x, y, c, r)
            k = a * (N_DEV - 1) + r - 1
            copies.append(pltpu.make_async_remote_copy(
                src_ref=srcs[a].at[_flat(*peer)], dst_ref=lands[a].at[r - 1],
                send_sem=send_sems.at[k], recv_sem=recv_sems.at[k], device_id=peer, device_id_type=MESH))
    return copies


def _rs_start(name, arrays):
    n = len(arrays)
    hbm = lambda a: pltpu.with_memory_space_constraint(a, pltpu.HBM)
    lands = [hbm(lax.empty((N_DEV - 1,) + a.shape[1:], a.dtype)) for a in arrays]

    def body(*refs):
        srcs, lnd = refs[:n], refs[n:2 * n]
        send_sems, recv_sems = refs[2 * n], refs[2 * n + 1]
        token = refs[-1]
        for cp in _rs_copies(*_my_place(), srcs, lnd, send_sems, recv_sems):
            cp.start()
        token[...] = jnp.zeros_like(token)

    outs = pl.pallas_call(
        body, name=name,
        in_specs=[HBM] * (2 * n),
        out_specs=[SEMS, SEMS] + [HBM] * (2 * n) + [pl.BlockSpec(memory_space=pltpu.VMEM)],
        out_shape=[pltpu.SemaphoreType.DMA((n * (N_DEV - 1),)), pltpu.SemaphoreType.DMA((n * (N_DEV - 1),))]
        + [pltpu.HBM(a.shape, a.dtype) for a in arrays] + [pltpu.HBM(l.shape, l.dtype) for l in lands]
        + [_sds((8, 128), F32)],
        input_output_aliases={i: 2 + i for i in range(2 * n)},
        compiler_params=pltpu.CompilerParams(has_side_effects=SIDE_EFFECT),
    )(*[hbm(a) for a in arrays], *lands)
    return outs[0], outs[1], outs[2:2 + n], outs[2 + n:2 + 2 * n], outs[-1]


def _rs_wait(name, send_sems, recv_sems, srcs, lands, after):
    n = len(srcs)

    def body(*refs):
        src_refs, lnd = refs[:n], refs[n:2 * n]
        send, recv = refs[2 * n], refs[2 * n + 1]
        for cp in _rs_copies(*_my_place(), src_refs, lnd, send, recv):
            cp.wait_send()
            cp.wait_recv()

    outs = pl.pallas_call(
        body, name=name,
        in_specs=[HBM] * (2 * n) + [SEMS, SEMS] + [ANY] * len(after),
        out_specs=[HBM] * (2 * n),
        out_shape=[pltpu.HBM(a.shape, a.dtype) for a in list(srcs) + list(lands)],
        input_output_aliases={i: i for i in range(2 * n)},
        compiler_params=pltpu.CompilerParams(has_side_effects=SIDE_EFFECT),
    )(*srcs, *lands, send_sems, recv_sems, *after)
    return outs[:n], outs[n:]


SMALL_PACK_ROWS = 16


def _all_reduce_small(rows, deps=()):
    n = len(rows)

    def body(*refs):
        ins = refs[:n]
        out_ref, mine, buf, send_sems, recv_sems = refs[n + len(deps):]
        x, y, c = _my_place()
        mine[...] = jnp.zeros_like(mine)
        for (r0, a), ref in zip(rows, ins):
            mine[r0:r0 + a.shape[0], 0:a.shape[1]] = ref[...]
        buf[_flat(x, y, c)] = mine[...]
        copies = []
        for r in range(1, N_DEV):
            peer = _peer(x, y, c, r)
            send = pltpu.make_async_remote_copy(
                src_ref=mine, dst_ref=buf.at[_flat(x, y, c)],
                send_sem=send_sems.at[r - 1], recv_sem=recv_sems.at[r - 1], device_id=peer, device_id_type=MESH)
            send.start()
            recv = pltpu.make_async_remote_copy(
                src_ref=mine, dst_ref=buf.at[_flat(*peer)],
                send_sem=send_sems.at[r - 1], recv_sem=recv_sems.at[r - 1], device_id=peer, device_id_type=MESH)
            copies.append((send, recv))
        for send, recv in copies:
            send.wait_send()
            recv.wait_recv()
        acc = buf[0]
        for s in range(1, N_DEV):
            acc = acc + buf[s]
        out_ref[...] = acc

    vm = pl.BlockSpec(memory_space=pltpu.VMEM)
    shape = (SMALL_PACK_ROWS, D_MODEL)
    return pl.pallas_call(
        body, name="all_reduce_small", in_specs=[vm] * n + [ANY] * len(deps), out_specs=vm,
        out_shape=_sds(shape, F32),
        scratch_shapes=[pltpu.VMEM(shape, F32), pltpu.VMEM((N_DEV,) + shape, F32),
                        pltpu.SemaphoreType.DMA((7,)), pltpu.SemaphoreType.DMA((7,))],
    )(*[a for _, a in rows], *deps)


def _adamw_math(w, g, m, v):
    m = ADAM_B1 * m + (1.0 - ADAM_B1) * g
    v = ADAM_B2 * v + (1.0 - ADAM_B2) * (g * g)
    m_hat = m / (1.0 - ADAM_B1 ** ADAM_STEP)
    v_hat = v / (1.0 - ADAM_B2 ** ADAM_STEP)
    delta = -ADAM_LR * (m_hat / (jnp.sqrt(v_hat) + ADAM_EPS) + ADAM_WD * w)
    return delta, m, v


def _adamw_big(name, w, m, v, srcs, lands, me):
    nl, rows, cols = w.shape
    tr = next(cand for cand in (256, 128, 64, 32, 16, 8) if rows % cand == 0)

    def body(me_ref, w_ref, m_ref, v_ref, *rest):
        src_refs, land_refs = rest[:nl], rest[nl:2 * nl]
        g_ref, d_ref, mo_ref, vo_ref = rest[2 * nl:]
        for layer in range(nl):
            @pl.when(pl.program_id(0) == layer)
            def _():
                g = src_refs[layer][...].astype(F32)
                for s in range(N_DEV - 1):
                    g = g + land_refs[layer][s].astype(F32)
                delta, mn, vn = _adamw_math(w_ref[...], g, m_ref[...], v_ref[...])
                g_ref[...] = g
                d_ref[...] = delta
                mo_ref[...] = mn
                vo_ref[...] = vn

    blk = pl.BlockSpec((None, tr, cols), lambda l, i, me_ref: (l, i, 0))
    at = lambda layer, l, i: jnp.where(l == layer, i, 0)
    own = [pl.BlockSpec((None, tr, cols), functools.partial(lambda layer, l, i, me_ref: (me_ref[0], at(layer, l, i), 0),
                                                            layer)) for layer in range(nl)]
    peers = [pl.BlockSpec((N_DEV - 1, tr, cols), functools.partial(lambda layer, l, i, me_ref: (0, at(layer, l, i), 0),
                                                                   layer)) for layer in range(nl)]
    return pl.pallas_call(
        body, name=name,
        grid_spec=pltpu.PrefetchScalarGridSpec(
            num_scalar_prefetch=1, grid=(nl, rows // tr),
            in_specs=[blk, blk, blk] + own + peers, out_specs=[blk] * 4),
        out_shape=[_sds((nl, rows, cols), F32)] * 4,
        compiler_params=_cparams(("arbitrary", "arbitrary")),
    )(me, w, m, v, *srcs, *lands)


def _adamw_small(ws, gs, ms, vs):
    n = len(ws)

    def body(*refs):
        w_refs, g_refs, m_refs, v_refs = (refs[i * n:(i + 1) * n] for i in range(4))
        d_out, m_out, v_out = (refs[(4 + i) * n:(5 + i) * n] for i in range(3))
        for i in range(n):
            delta, mn, vn = _adamw_math(w_refs[i][...], g_refs[i][...], m_refs[i][...], v_refs[i][...])
            d_out[i][...] = delta
            m_out[i][...] = mn
            v_out[i][...] = vn

    vm = pl.BlockSpec(memory_space=pltpu.VMEM)
    outs = pl.pallas_call(
        body, name="adamw_small", in_specs=[vm] * (4 * n), out_specs=[vm] * (3 * n),
        out_shape=[_sds(a.shape, F32) for a in ws] * 3,
    )(*ws, *gs, *ms, *vs)
    return outs[:n], outs[n:2 * n], outs[2 * n:]


def _pad_to(a, rows, cols):
    return jnp.pad(a, ((0, rows - a.shape[0]), (0, cols - a.shape[1])))


def _place_own(blocks):
    me = _flat(*_my_place())
    return [lax.dynamic_update_slice(lax.empty((N_DEV,) + b.shape, b.dtype), b[None], (me,) + (0,) * b.ndim)
            for b in blocks]


def _ag_copies(x, y, c, blocks, bufs, send_sems, recv_sems, arriving):
    copies = []
    for a in range(len(blocks)):
        for r in range(1, N_DEV):
            peer = _peer(x, y, c, r)
            k = a * (N_DEV - 1) + r - 1
            copies.append(pltpu.make_async_remote_copy(
                src_ref=blocks[a], dst_ref=bufs[a].at[_flat(*(peer if arriving else (x, y, c)))],
                send_sem=send_sems.at[k], recv_sem=recv_sems.at[k], device_id=peer, device_id_type=MESH))
    return copies


def _ag_start(groups, after):
    flat = [pair for g in groups for pair in g]
    n, ng = len(flat), len(groups)
    hbm = lambda a: pltpu.with_memory_space_constraint(a, pltpu.HBM)

    def body(*refs):
        blocks, bufs = refs[:n], refs[n:2 * n]
        sems = refs[2 * n + len(after):2 * n + len(after) + 2 * ng]
        x, y, c = _my_place()
        at = 0
        for gi, g in enumerate(groups):
            for cp in _ag_copies(x, y, c, blocks[at:at + len(g)], bufs[at:at + len(g)], sems[2 * gi],
                                 sems[2 * gi + 1], arriving=False):
                cp.start()
            at += len(g)
        refs[-1][...] = jnp.zeros_like(refs[-1])

    sem_shapes = [pltpu.SemaphoreType.DMA((len(g) * (N_DEV - 1),)) for g in groups for _ in range(2)]
    outs = pl.pallas_call(
        body, name="gather_start",
        in_specs=[HBM] * (2 * n) + [ANY] * len(after),
        out_specs=[SEMS] * (2 * ng) + [HBM] * (2 * n) + [pl.BlockSpec(memory_space=pltpu.VMEM)],
        out_shape=sem_shapes + [pltpu.HBM(b.shape, b.dtype) for b, _ in flat]
        + [pltpu.HBM(u.shape, u.dtype) for _, u in flat] + [_sds((8, 128), F32)],
        input_output_aliases={i: 2 * ng + i for i in range(2 * n)},
        compiler_params=pltpu.CompilerParams(has_side_effects=SIDE_EFFECT),
    )(*[hbm(b) for b, _ in flat], *[hbm(u) for _, u in flat], *after)
    blocks_thru, bufs_thru = outs[2 * ng:2 * ng + n], outs[2 * ng + n:2 * ng + 2 * n]
    started, at = [], 0
    for gi, g in enumerate(groups):
        started.append((outs[2 * gi], outs[2 * gi + 1], blocks_thru[at:at + len(g)], bufs_thru[at:at + len(g)]))
        at += len(g)
    return started, outs[-1]


def _ag_wait(name, send_sems, recv_sems, blocks, bufs, after):
    n = len(blocks)

    def body(*refs):
        for cp in _ag_copies(*_my_place(), refs[:n], refs[n:2 * n], refs[2 * n], refs[2 * n + 1], arriving=True):
            cp.wait_send()
            cp.wait_recv()

    outs = pl.pallas_call(
        body, name=name,
        in_specs=[HBM] * (2 * n) + [SEMS, SEMS] + [ANY] * len(after),
        out_specs=[HBM] * (2 * n),
        out_shape=[pltpu.HBM(a.shape, a.dtype) for a in list(blocks) + list(bufs)],
        input_output_aliases={i: i for i in range(2 * n)},
        compiler_params=pltpu.CompilerParams(has_side_effects=SIDE_EFFECT),
    )(*blocks, *bufs, send_sems, recv_sems, *after)
    return outs[n:]


def _split_call(name, body, thru, sems_in, new_sems, after):
    n, ns, nn = len(thru), len(sems_in), len(new_sems)
    hbm = lambda a: pltpu.with_memory_space_constraint(a, pltpu.HBM)

    def wrapped(*refs):
        body(refs[:n], refs[n:n + ns], refs[n + ns + len(after):n + ns + len(after) + nn])
        refs[-1][...] = jnp.zeros_like(refs[-1])

    outs = pl.pallas_call(
        wrapped, name=name,
        in_specs=[HBM] * n + [SEMS] * ns + [ANY] * len(after),
        out_specs=[SEMS] * nn + [HBM] * n + [pl.BlockSpec(memory_space=pltpu.VMEM)],
        out_shape=[pltpu.SemaphoreType.DMA((k,)) for k in new_sems] + [pltpu.HBM(a.shape, a.dtype) for a in thru]
        + [_sds((8, 128), F32)],
        input_output_aliases={i: nn + i for i in range(n)},
        compiler_params=pltpu.CompilerParams(has_side_effects=SIDE_EFFECT),
    )(*[hbm(a) for a in thru], *sems_in, *after)
    return list(outs[:nn]), list(outs[nn:nn + n]), outs[-1]


def _first_gather(blocks, bufs, overlap):
    n = len(blocks)

    def copies(refs, s1, r1, s2, r2):
        x, y, c = _my_place()
        me, sibling = (x, y, c), (x, y, 1 - c)
        chips = [(1 - x, y), (x, 1 - y), (1 - x, 1 - y)]
        blk, buf = refs[:n], refs[n:]
        out = dict(send1=[], recv1_sib=[], recv1_ici=[], send2=[], recv2=[])
        for a in range(n):
            place = lambda dev: buf[a].at[_flat(*dev)]
            for k, to in enumerate([sibling] + [(*chip, c) for chip in chips]):
                mk = lambda dst: pltpu.make_async_remote_copy(
                    src_ref=blk[a], dst_ref=dst, send_sem=s1.at[4 * a + k], recv_sem=r1.at[4 * a + k],
                    device_id=to, device_id_type=MESH)
                out['send1'].append(mk(place(me)))
                out['recv1_sib' if k == 0 else 'recv1_ici'].append(mk(place(to)))
            for j, chip in enumerate(chips):
                mk = lambda dev: pltpu.make_async_remote_copy(
                    src_ref=place(dev), dst_ref=place(dev), send_sem=s2.at[3 * a + j], recv_sem=r2.at[3 * a + j],
                    device_id=sibling, device_id_type=MESH)
                out['send2'].append(mk((*chip, c)))
                out['recv2'].append(mk((*chip, 1 - c)))
        return out

    def start(refs, sems_in, new):
        for cp in copies(refs, new[0], new[1], new[0], new[1])['send1']:
            cp.start()

    def forward(refs, sems_in, new):
        cps = copies(refs, sems_in[0], sems_in[1], new[0], new[1])
        for cp in cps['recv1_ici']:
            cp.wait_recv()
        for cp in cps['send2']:
            cp.start()

    def finish(refs, sems_in, new):
        cps = copies(refs, *sems_in)
        for cp in cps['recv1_sib'] + cps['recv2']:
            cp.wait_recv()
        for cp in cps['send1'] + cps['send2']:
            cp.wait_send()

    sems1, thru, token = _split_call("first_gather_start", start, list(blocks) + list(bufs), [], [4 * n, 4 * n], ())
    after = overlap(token)
    sems2, thru, token = _split_call("first_gather_forward", forward, thru, sems1, [3 * n, 3 * n], after)
    _, thru, _ = _split_call("first_gather_wait", finish, thru, sems1 + sems2, [], ())
    return thru[n:], token


def _prepare_weights(p, x):
    n = N_DEV
    bf = lambda a: a.astype(BF16)
    gn_pack = jnp.concatenate([
        _pad_to(p['ret_gn'][0], RET_HEADS, 128), _pad_to(p['mla_q_a_norm'], 1, 128),
        _pad_to(p['mla_kv_a_norm'], 1, 128), jnp.zeros((2, 128), F32)], axis=0)
    ple = lambda l: [bf(p['ple_gate_w'][l]), bf(p['ple_proj_w'][l])]
    names = ('ret_out', 'mlp_w1_0', 'mlp_w2_0', 'ple_0', 'mla', 'layer_1')
    later = [[bf(p['ret_w_out'][0])], [bf(p['mlp_w1'][0])], [bf(p['mlp_w2'][0])], ple(0),
             [bf(p['mla_w_in'][0]), bf(p['mla_w_uq'][0]), bf(p['mla_w_ukv'][0]), bf(p['mla_w_out'][0])],
             [bf(p['mlp_w1'][1]), bf(p['mlp_w2'][1])] + ple(1)]
    first = [gn_pack, bf(p['ret_w_in'][0])]
    behind = {}

    def overlap(token):
        behind['hn0'] = _rms_fwd("mix_norm0", x, p['mix_norm'][0:1], deps=(token,))
        behind['bufs'] = _place_own([b for g in later for b in g])
        return (behind['hn0'], *behind['bufs'])

    (pack, wri), token = _first_gather(first, _place_own(first), overlap)
    bufs = behind['bufs']
    groups, at = [], 0
    for g in later:
        groups.append(list(zip(g, bufs[at:at + len(g)])))
        at += len(g)
    started, token = _ag_start(groups, (token,))

    w = {k: p[k] for k in ('mix_norm', 'mlp_norm', 'ple_norm')}
    w['hn0'] = behind['hn0']
    w['ret_gn'] = pack[:, :RET_HEADS, :RET_DV // n].transpose(1, 0, 2).reshape(RET_HEADS, RET_DV)
    w['mla_q_a_norm'] = pack[:, RET_HEADS, :MLA_Q_RANK // n].reshape(1, MLA_Q_RANK)
    w['mla_kv_a_norm'] = pack[:, RET_HEADS + 1, :MLA_KV_RANK // n].reshape(1, MLA_KV_RANK)
    w['ret_w_in'] = wri
    w['mla_q_norm'] = _spread_rope(p['mla_q_norm'])
    w['mla_k_norm'] = _spread_rope(p['mla_k_norm'])
    w['deps'] = (token,)

    def fetch(name, after):
        got = list(_ag_wait("gather_wait_" + name, *started[names.index(name)], after))
        if name == 'ret_out':
            return dict(ret_w_out=got[0].reshape(RET_V_W, D_MODEL))
        if name == 'mla':
            wmi, wuq, wukv, wmo = got
            return dict(mla_w_in=_spread_rope(wmi.reshape(D_MODEL, MLA_IN)), mla_w_uq=_spread_rope(wuq),
                        mla_w_ukv=wukv, mla_w_out=wmo.reshape(D_MODEL, D_MODEL))
        out = {}
        if name in ('mlp_w1_0', 'layer_1'):
            out['mlp_w1'] = got.pop(0)
        if name in ('mlp_w2_0', 'layer_1'):
            out['mlp_w2'] = got.pop(0)
        if name in ('ple_0', 'layer_1'):
            out['ple_gate_w'] = got[0].reshape(D_MODEL, D_MODEL)
            out['ple_proj_w'] = got[1].transpose(1, 0, 2).reshape(PLE_DIM, D_MODEL)
        return out

    return w, fetch


def _small_grads(small, after):
    rows = [(0, small['mix_norm'][0]), (1, small['mix_norm'][1]), (2, small['mlp_norm'][0]),
            (3, small['mlp_norm'][1]), (4, small['ple_norm'][0]), (5, small['ple_norm'][1]),
            (6, small['ret_gn']), (10, small['mla_q_a_norm']), (11, small['mla_kv_a_norm']),
            (12, small['mla_q_norm']), (13, small['mla_k_norm']), (14, small['sq_err'])]
    gs = _all_reduce_small(rows, after)
    me = _flat(*_my_place())
    n = N_DEV
    return dict(
        sq_err=gs[14, 0],
        mix_norm=gs[0:2], mlp_norm=gs[2:4], ple_norm=gs[4:6],
        ret_gn=lax.dynamic_slice(gs, (6, me * (RET_DV // n)), (RET_HEADS, RET_DV // n)),
        mla_q_a_norm=lax.dynamic_slice(gs, (10, me * (MLA_Q_RANK // n)), (1, MLA_Q_RANK // n)),
        mla_kv_a_norm=lax.dynamic_slice(gs, (11, me * (MLA_KV_RANK // n)), (1, MLA_KV_RANK // n)),
        mla_q_norm=_gather_rope(gs[12:13, :MLA_HD_PAD]), mla_k_norm=_gather_rope(gs[13:14, :MLA_HD_PAD]))


def kernel(x, p, mix_norm, ret_w_in, ret_gn, ret_w_out, mla_w_in, mla_q_a_norm, mla_kv_a_norm, mla_w_uq, mla_w_ukv, mla_q_norm, mla_k_norm, mla_w_out, mlp_norm, mlp_w1, mlp_w2, ple_norm, ple_gate_w, ple_proj_w, loss_target, m_mix_norm, m_ret_w_in, m_ret_gn, m_ret_w_out, m_mla_w_in, m_mla_q_a_norm, m_mla_kv_a_norm, m_mla_w_uq, m_mla_w_ukv, m_mla_q_norm, m_mla_k_norm, m_mla_w_out, m_mlp_norm, m_mlp_w1, m_mlp_w2, m_ple_norm, m_ple_gate_w, m_ple_proj_w, v_mix_norm, v_ret_w_in, v_ret_gn, v_ret_w_out, v_mla_w_in, v_mla_q_a_norm, v_mla_kv_a_norm, v_mla_w_uq, v_mla_w_ukv, v_mla_q_norm, v_mla_k_norm, v_mla_w_out, v_mlp_norm, v_mlp_w1, v_mlp_w2, v_ple_norm, v_ple_gate_w, v_ple_proj_w):
    given = dict(locals())
    params = {n: given[n] for n in WEIGHTS}
    w, fetch = _prepare_weights(params, x[0])

    started = []

    def emit(group):
        keys = list(group)
        send, recv, srcs, lands, token = _rs_start(f"rs_start{len(started)}", [group[k] for k in keys])
        started.append((keys, send, recv, srcs, lands))
        return (token,)

    sq_err, grad_x, _, small = _local_step(x[0], p, loss_target[0], w, fetch, emit)
    small['sq_err'] = sq_err[0:1]

    grads, deltas, new_m, new_v = {}, {}, {}, {}
    total = {}

    def small_updates(after):
        sg = _small_grads(small, after)
        total['loss'] = 0.5 / D_MODEL * sg['sq_err']
        two_d = lambda a: a.reshape(-1, a.shape[-1])
        d_s, m_s, v_s = _adamw_small(
            [two_d(params[n]) for n in SMALL], [sg[n] for n in SMALL],
            [two_d(given["m_" + n]) for n in SMALL], [two_d(given["v_" + n]) for n in SMALL])
        for i, n in enumerate(SMALL):
            shape = params[n].shape
            grads[n], deltas[n], new_m[n], new_v[n] = (a.reshape(shape) for a in (sg[n], d_s[i], m_s[i], v_s[i]))
        return (d_s[0],)

    me = _flat(*_my_place()).astype(jnp.int32).reshape(1)
    after = (grad_x,)
    src_of, land_of = {}, {}
    for gi, (keys, send, recv, srcs, lands) in enumerate(started):
        if gi == len(started) - 1:
            after = small_updates(after)
        srcs, lands = _rs_wait(f"rs_wait{gi}", send, recv, srcs, lands, after)
        for k, s, l in zip(keys, srcs, lands):
            src_of[k], land_of[k] = s, l
        done = [n for n in BIG if n not in grads and all((n, l) in src_of for l in range(params[n].shape[0]))]
        for n in done:
            layers = range(params[n].shape[0])
            grads[n], deltas[n], new_m[n], new_v[n] = _adamw_big(
                "adamw_" + n, params[n], given["m_" + n], given["v_" + n],
                [src_of[(n, l)] for l in layers], [land_of[(n, l)] for l in layers], me)
        if done:
            after = tuple(deltas[n] for n in done)

    return (total['loss'], grad_x[None], *[grads[n] for n in WEIGHTS], *[deltas[n] for n in WEIGHTS],
            *[new_m[n] for n in WEIGHTS], *[new_v[n] for n in WEIGHTS])
```

```python
import functools

import jax
import jax.numpy as jnp
from jax import lax
from jax.experimental import pallas as pl
from jax.experimental.pallas import tpu as pltpu

F32 = jnp.float32
BF16 = jnp.bfloat16
MESH = pl.DeviceIdType.MESH
ANY = pl.BlockSpec(memory_space=pl.ANY)

N_DEV = 8
D_MODEL = 1024
CHUNK = 64
RET_BLOCK = 4 * CHUNK
EPS = 1e-6
ROPE_THETA = 10000.0
RET_HEADS = 4
RET_DK = 256
RET_DV = 512
RET_QK_W = RET_HEADS * RET_DK
RET_V_W = RET_HEADS * RET_DV
RET_IN = 2 * RET_QK_W + 2 * RET_V_W
MLA_HEADS = 8
MLA_NOPE = 128
MLA_ROPE = 64
MLA_QKD = MLA_NOPE + MLA_ROPE
MLA_VD = 128
MLA_Q_RANK = 384
MLA_KV_RANK = 256
MLA_IN = MLA_Q_RANK + MLA_KV_RANK + MLA_ROPE
MLA_IN_PAD = 768
MLA_HD_PAD = 256
D_FF = 4096
PLE_DIM = 256
ATT_SCALE = MLA_QKD ** -0.5
LOG2E = 1.4426950408889634
ATT_EXP2 = ATT_SCALE * LOG2E

ADAM_LR = 0.001
ADAM_B1 = 0.9
ADAM_B2 = 0.999
ADAM_EPS = 1e-08
ADAM_WD = 0.01
ADAM_STEP = 10

VMEM_LIMIT = 52 * 1024 * 1024
ROW_TILE = 1024
RET_ROWS = 512
ATT_BLOCK = 256
ATT_QROWS = 1024
ATT_KROWS = 1024
ATT_HEADS = 2
PREP_ROWS = 1024

WEIGHTS = ['mix_norm', 'ret_w_in', 'ret_gn', 'ret_w_out', 'mla_w_in', 'mla_q_a_norm', 'mla_kv_a_norm',
           'mla_w_uq', 'mla_w_ukv', 'mla_q_norm', 'mla_k_norm', 'mla_w_out', 'mlp_norm', 'mlp_w1', 'mlp_w2',
           'ple_norm', 'ple_gate_w', 'ple_proj_w']
BIG = ['ret_w_in', 'ret_w_out', 'mla_w_in', 'mla_w_uq', 'mla_w_ukv', 'mla_w_out', 'mlp_w1', 'mlp_w2',
       'ple_gate_w', 'ple_proj_w']
SMALL = [w for w in WEIGHTS if w not in BIG]


def _cparams(sem=None):
    return pltpu.CompilerParams(dimension_semantics=sem, vmem_limit_bytes=VMEM_LIMIT)


def _dot(a, b, ca, cb):
    return lax.dot_general(a, b, (((ca,), (cb,)), ((), ())), preferred_element_type=F32)


def _bf(v):
    return v if v.dtype == BF16 else v.astype(BF16)


def _sigmoid(z):
    return 1.0 / (1.0 + jnp.exp(-z))


def _mm(name, grid, a, a_spec, b, b_spec, contract, outs, extras=(), epi=None, deps=(), split=None):
    nk = grid[2]
    n_ex, n_out, n_dep = len(extras), len(outs), len(deps)
    acc_shape = tuple(d for d in outs[0][1].block_shape if d is not None)
    if split is not None:
        acc_shape = (acc_shape[1], acc_shape[0] * split)

    def body(*refs):
        a_ref, b_ref = refs[:2]
        ex_refs = refs[2:2 + n_ex]
        out_refs = refs[2 + n_ex + n_dep:2 + n_ex + n_dep + n_out]

        def product():
            return _dot(_bf(a_ref[...]), _bf(b_ref[...]), contract[0], contract[1])

        def finish(acc):
            if split is not None:
                for j in range(acc_shape[1] // split):
                    out_refs[0][j] = acc[:, j * split:(j + 1) * split].astype(out_refs[0].dtype)
                return
            acc = acc[...]
            res = epi(acc, *[r[...] for r in ex_refs]) if epi is not None else (acc,)
            for o, r in zip(out_refs, res):
                o[...] = r.astype(o.dtype)

        if nk == 1:
            finish(product())
        else:
            acc_ref = refs[-1]
            k = pl.program_id(2)

            @pl.when(k == 0)
            def _():
                acc_ref[...] = jnp.zeros_like(acc_ref)

            acc_ref[...] += product()

            @pl.when(k == nk - 1)
            def _():
                finish(acc_ref)

    return pl.pallas_call(
        body, name=name, grid=grid,
        in_specs=[a_spec, b_spec] + [s for _, s in extras] + [ANY] * n_dep,
        out_specs=[s for _, s in outs],
        out_shape=[s for s, _ in outs],
        scratch_shapes=[pltpu.VMEM(acc_shape, F32)] if nk > 1 else [],
        compiler_params=_cparams(("parallel", "parallel", "arbitrary")),
    )(a, b, *[x for x, _ in extras], *deps)


def _mm_rows(name, tm, a, w, mode, outs, extras=(), epi=None, deps=()):
    n_sh, rows, cols = w.shape
    n_ex, n_out, n_dep = len(extras), len(outs), len(deps)
    by_cols = mode in ('nn_cols', 'nt_rows')
    width = cols if mode == 'nn_cols' else rows

    def body(*refs):
        a_ref, w_ref = refs[:2]
        ex_refs = refs[2:2 + n_ex]
        out_refs = refs[2 + n_ex + n_dep:2 + n_ex + n_dep + n_out]
        if by_cols:
            av = _bf(a_ref[...])
            for s in range(n_sh):
                cs = slice(s * width, (s + 1) * width)
                acc = _dot(av, w_ref[s], 1, 0 if mode == 'nn_cols' else 1)
                res = epi(acc, *[r[:, cs] for r in ex_refs]) if epi is not None else (acc,)
                for o, r in zip(out_refs, res):
                    o[:, cs] = r.astype(o.dtype)
        else:
            chunk = rows if mode == 'nn_rows' else cols
            acc = None
            for s in range(n_sh):
                part = _dot(_bf(a_ref[:, s * chunk:(s + 1) * chunk]), w_ref[s], 1, 0 if mode == 'nn_rows' else 1)
                acc = part if acc is None else acc + part
            res = epi(acc, *[r[...] for r in ex_refs]) if epi is not None else (acc,)
            for o, r in zip(out_refs, res):
                o[...] = r.astype(o.dtype)

    t, ka = a.shape
    return pl.pallas_call(
        body, name=name, grid=(t // tm, 1, 1),
        in_specs=[pl.BlockSpec((tm, ka), lambda i, j, k: (i, 0)),
                  pl.BlockSpec((n_sh, rows, cols), lambda i, j, k: (0, 0, 0))] + [s for _, s in extras] + [ANY] * n_dep,
        out_specs=[s for _, s in outs],
        out_shape=[s for s, _ in outs],
        compiler_params=_cparams(("parallel", "arbitrary", "arbitrary")),
    )(a, w, *[x for x, _ in extras], *deps)


def _sds(shape, dtype):
    return jax.ShapeDtypeStruct(shape, dtype)


def _row_tile(t, cap=ROW_TILE):
    return min(cap, t)


def _rms_fwd(name, x, g, deps=()):
    t, d = x.shape
    tm = _row_tile(t)

    def body(x_ref, g_ref, *rest):
        o_ref = rest[-1]
        xv = x_ref[...]
        r = lax.rsqrt(jnp.mean(xv * xv, axis=-1, keepdims=True) + EPS)
        o_ref[...] = (xv * r * g_ref[...]).astype(o_ref.dtype)

    return pl.pallas_call(
        body, name=name, grid=(t // tm,),
        in_specs=[pl.BlockSpec((tm, d), lambda i: (i, 0)), pl.BlockSpec((1, d), lambda i: (0, 0))] + [ANY] * len(deps),
        out_specs=pl.BlockSpec((tm, d), lambda i: (i, 0)),
        out_shape=_sds((t, d), BF16),
        compiler_params=_cparams(("parallel",)),
    )(x, g, *deps)


def _rms_bwd_rows(dy, xv, g, n):
    r = lax.rsqrt(jnp.sum(xv * xv, axis=-1, keepdims=True) / n + EPS)
    xh = xv * r
    dxh = dy * g
    dx = r * (dxh - xh * (jnp.sum(dxh * xh, axis=-1, keepdims=True) / n))
    return dx, dy * xh


def _ple_gate_bwd(name, dh, gate, e):
    t, d = dh.shape
    tm = _row_tile(t)

    def body(dh_ref, g_ref, e_ref, de_ref, dz_ref):
        dh_v, gt = dh_ref[...], g_ref[...].astype(F32)
        de_ref[...] = (dh_v * gt).astype(BF16)
        dz_ref[...] = (dh_v * e_ref[...].astype(F32) * (gt * (1.0 - gt))).astype(BF16)

    row = pl.BlockSpec((tm, d), lambda i: (i, 0))
    return pl.pallas_call(
        body, name=name, grid=(t // tm,), in_specs=[row, row, row], out_specs=[row, row],
        out_shape=[_sds((t, d), BF16), _sds((t, d), BF16)],
        compiler_params=_cparams(("parallel",)),
    )(dh, gate, e)


def _rope_half(v, cos, sin):
    half = v.shape[-1] // 2
    v1, v2 = v[:, :half], v[:, half:]
    return jnp.concatenate([v1 * cos - v2 * sin, v2 * cos + v1 * sin], axis=-1)


def _ret_consts():
    lg = jnp.log(1.0 - 2.0 ** (-5.0 - jnp.arange(RET_HEADS, dtype=F32)))
    idx = jnp.arange(RET_BLOCK, dtype=F32)
    chunk = jnp.floor(idx / CHUNK)
    dist = idx[:, None] - idx[None, :]
    same = chunk[:, None] == chunk[None, :]
    seen = jnp.where(same, jnp.abs(dist), jnp.where(chunk[None, :] < chunk[:, None], dist, jnp.inf))
    intra = jnp.exp(lg[:, None, None] * seen)
    qdec = jnp.exp(lg[:, None] * (idx + 1.0))
    kdec = jnp.exp(lg[:, None] * (RET_BLOCK - 1.0 - idx))
    cdec = jnp.exp(lg * RET_BLOCK)
    qdec = jnp.broadcast_to(qdec[:, :, None], (RET_HEADS, RET_BLOCK, RET_DK))
    kdec = jnp.broadcast_to(kdec[:, :, None], (RET_HEADS, RET_BLOCK, RET_DK))
    cdec = jnp.broadcast_to(cdec[:, None, None], (RET_HEADS, 1, RET_DV))
    return intra, qdec, kdec, cdec


def _ret_specs(rb, rev_nb=None):
    blk = (lambda i: i) if rev_nb is None else (lambda i: rev_nb - 1 - i)
    full = lambda shape: pl.BlockSpec(shape, lambda i: (0,) * len(shape))
    return dict(
        proj=pl.BlockSpec((rb, RET_IN), lambda i: (blk(i), 0)),
        tab=pl.BlockSpec((rb, RET_DK // 2), lambda i: (blk(i), 0)),
        vw=pl.BlockSpec((rb, RET_V_W), lambda i: (blk(i), 0)),
        st=pl.BlockSpec((rb // RET_BLOCK, RET_HEADS, RET_DK, RET_DV), lambda i: (blk(i), 0, 0, 0)),
        gn=full((RET_HEADS, 1, RET_DV)),
        intra=full((RET_HEADS, RET_BLOCK, RET_BLOCK)),
        dec=full((RET_HEADS, RET_BLOCK, RET_DK)),
        cdec=full((RET_HEADS, 1, RET_DV)),
    )


def _ret_fwd(proj, cos, sin, gn):
    t = proj.shape[0]
    rb = min(RET_ROWS, t)
    cpb = rb // RET_BLOCK
    intra, qdec, kdec, cdec = _ret_consts()
    sp = _ret_specs(rb)

    def body(proj_ref, cos_ref, sin_ref, gn_ref, intra_ref, qd_ref, kd_ref, cd_ref,
             gated_ref, outp_ref, st_ref, s_ref):
        @pl.when(pl.program_id(0) == 0)
        def _():
            s_ref[...] = jnp.zeros_like(s_ref)

        def chunk(c, carry):
            rows = pl.ds(pl.multiple_of(c * RET_BLOCK, RET_BLOCK), RET_BLOCK)
            cs, sn = cos_ref[rows, :], sin_ref[rows, :]
            for h in range(RET_HEADS):
                q = proj_ref[rows, h * RET_DK:(h + 1) * RET_DK].astype(F32)
                k = proj_ref[rows, RET_QK_W + h * RET_DK:RET_QK_W + (h + 1) * RET_DK].astype(F32)
                v = proj_ref[rows, 2 * RET_QK_W + h * RET_DV:2 * RET_QK_W + (h + 1) * RET_DV]
                g = proj_ref[rows, 2 * RET_QK_W + RET_V_W + h * RET_DV:
                             2 * RET_QK_W + RET_V_W + (h + 1) * RET_DV].astype(F32)
                qr = _rope_half(q, cs, sn)
                kr = _rope_half(k, cs, sn) * (RET_DK ** -0.5)
                qb, kb, vb = qr.astype(BF16), kr.astype(BF16), v
                sc = _dot(qb, kb, 1, 1) * intra_ref[h]
                inner = _dot(sc.astype(BF16), vb, 1, 0)
                s_old = s_ref[h]
                sb = s_old.astype(BF16)
                st_ref[c, h] = sb
                cross = _dot((qr * qd_ref[h]).astype(BF16), sb, 1, 0)
                out = inner + cross
                s_ref[h] = s_old * cd_ref[h] + _dot((kr * kd_ref[h]).astype(BF16), vb, 0, 0)
                r = lax.rsqrt(jnp.mean(out * out, axis=-1, keepdims=True) + EPS)
                y = out * r * gn_ref[h]
                cols = slice(h * RET_DV, (h + 1) * RET_DV)
                gated_ref[rows, cols] = (g * _sigmoid(g) * y).astype(BF16)
                outp_ref[rows, cols] = out
            return carry

        lax.fori_loop(0, cpb, chunk, 0)

    return pl.pallas_call(
        body, name="ret_fwd", grid=(t // rb,),
        in_specs=[sp['proj'], sp['tab'], sp['tab'], sp['gn'], sp['intra'], sp['dec'], sp['dec'], sp['cdec']],
        out_specs=[sp['vw'], sp['vw'], sp['st']],
        out_shape=[_sds((t, RET_V_W), BF16), _sds((t, RET_V_W), F32),
                   _sds((t // RET_BLOCK, RET_HEADS, RET_DK, RET_DV), BF16)],
        scratch_shapes=[pltpu.VMEM((RET_HEADS, RET_DK, RET_DV), F32)],
        compiler_params=_cparams(("arbitrary",)),
    )(proj, cos, sin, gn.reshape(RET_HEADS, 1, RET_DV), intra, qdec, kdec, cdec)


def _ret_gate_bwd_epi(dgt, out, g, gn):
    g = g.astype(F32)
    r = lax.rsqrt(jnp.mean(out * out, axis=-1, keepdims=True) + EPS)
    xh = out * r
    sg = _sigmoid(g)
    dgate = dgt * (xh * gn) * (sg * (1.0 + g * (1.0 - sg)))
    dy = dgt * (g * sg)
    dxh = dy * gn
    dout = r * (dxh - xh * jnp.mean(dxh * xh, axis=-1, keepdims=True))
    return dout, dgate, jnp.sum(dy * xh, axis=0, keepdims=True)


def _ret_bwd(proj, cos, sin, states, dout, dgate, deps=()):
    t = proj.shape[0]
    rb = min(RET_ROWS, t)
    cpb = rb // RET_BLOCK
    nb = t // rb
    intra, qdec, kdec, cdec = _ret_consts()
    sp = _ret_specs(rb, rev_nb=nb)

    def body(proj_ref, cos_ref, sin_ref, intra_ref, qd_ref, kd_ref, cd_ref, st_ref, dout_ref, dgate_ref, *rest):
        dproj_ref, ds_ref = rest[len(deps):]

        @pl.when(pl.program_id(0) == 0)
        def _():
            ds_ref[...] = jnp.zeros_like(ds_ref)

        def chunk(cc, carry):
            c = cpb - 1 - cc
            rows = pl.ds(pl.multiple_of(c * RET_BLOCK, RET_BLOCK), RET_BLOCK)
            cs, sn = cos_ref[rows, :], sin_ref[rows, :]
            for h in range(RET_HEADS):
                q = proj_ref[rows, h * RET_DK:(h + 1) * RET_DK].astype(F32)
                k = proj_ref[rows, RET_QK_W + h * RET_DK:RET_QK_W + (h + 1) * RET_DK].astype(F32)
                v = proj_ref[rows, 2 * RET_QK_W + h * RET_DV:2 * RET_QK_W + (h + 1) * RET_DV]
                cols = slice(h * RET_DV, (h + 1) * RET_DV)
                qr = _rope_half(q, cs, sn)
                kr = _rope_half(k, cs, sn) * (RET_DK ** -0.5)
                qb, kb, vb = qr.astype(BF16), kr.astype(BF16), v
                qdb = (qr * qd_ref[h]).astype(BF16)
                kdb = (kr * kd_ref[h]).astype(BF16)
                doutb = dout_ref[rows, cols]
                itr = intra_ref[h]
                pb = (_dot(qb, kb, 1, 1) * itr).astype(BF16)
                dv = _dot(pb, doutb, 0, 0)
                dsc = (_dot(doutb, vb, 1, 1) * itr).astype(BF16)
                dq = _dot(dsc, kb, 1, 0)
                dk = _dot(dsc, qb, 0, 0)
                dq = dq + _dot(doutb, st_ref[c, h], 1, 1) * qd_ref[h]
                ds_new = ds_ref[h]
                dsb = ds_new.astype(BF16)
                dk = dk + _dot(vb, dsb, 1, 1) * kd_ref[h]
                dv = dv + _dot(kdb, dsb, 1, 0)
                ds_ref[h] = ds_new * cd_ref[h] + _dot(qdb, doutb, 0, 0)
                dproj_ref[rows, h * RET_DK:(h + 1) * RET_DK] = _rope_half(dq, cs, -sn).astype(BF16)
                dproj_ref[rows, RET_QK_W + h * RET_DK:RET_QK_W + (h + 1) * RET_DK] = (
                    _rope_half(dk * (RET_DK ** -0.5), cs, -sn).astype(BF16))
                dproj_ref[rows, 2 * RET_QK_W + h * RET_DV:2 * RET_QK_W + (h + 1) * RET_DV] = dv.astype(BF16)
                dproj_ref[rows, 2 * RET_QK_W + RET_V_W + h * RET_DV:
                          2 * RET_QK_W + RET_V_W + (h + 1) * RET_DV] = dgate_ref[rows, cols]
            return carry

        lax.fori_loop(0, cpb, chunk, 0)

    return pl.pallas_call(
        body, name="ret_bwd", grid=(nb,),
        in_specs=[sp['proj'], sp['tab'], sp['tab'], sp['intra'], sp['dec'], sp['dec'], sp['cdec'],
                  sp['st'], sp['vw'], sp['vw']] + [ANY] * len(deps),
        out_specs=sp['proj'],
        out_shape=_sds((t, RET_IN), BF16),
        scratch_shapes=[pltpu.VMEM((RET_HEADS, RET_DK, RET_DV), F32)],
        compiler_params=_cparams(("arbitrary",)),
    )(proj, cos, sin, intra, qdec, kdec, cdec, states, dout, dgate, *deps)


def _spread_rope(a):
    return jnp.pad(a, [(0, 0)] * (a.ndim - 1) + [(0, MLA_ROPE)])


def _gather_rope(a):
    return a[..., :a.shape[-1] - MLA_ROPE]


def _mla_tables(t):
    half = MLA_ROPE // 2
    inv = 1.0 / (ROPE_THETA ** (jnp.arange(0, MLA_ROPE, 2, dtype=F32) / MLA_ROPE))
    ang = jnp.arange(t, dtype=F32)[:, None] * inv[None, :]
    cos, sin = jnp.cos(ang), jnp.sin(ang)
    z = jnp.zeros((t, half), F32)
    c = jnp.concatenate([cos, cos, z, z], axis=1)
    s1 = jnp.concatenate([-sin, z, z, z], axis=1)
    s2 = jnp.concatenate([z, sin, z, z], axis=1)
    return c, s1, s2


def _rope_tile(r, c, s1, s2):
    return r * c + pltpu.roll(r, 96, 1) * s1 + pltpu.roll(r, 32, 1) * s2


def _mla_mid(proj2, qa, kva):
    t = proj2.shape[0]
    tm = _row_tile(t)

    def body(p_ref, qa_ref, kva_ref, cq_ref, ckv_ref):
        cq = p_ref[:, :MLA_Q_RANK]
        ckv = p_ref[:, MLA_Q_RANK:MLA_Q_RANK + MLA_KV_RANK]
        rq = lax.rsqrt(jnp.mean(cq * cq, axis=-1, keepdims=True) + EPS)
        rkv = lax.rsqrt(jnp.mean(ckv * ckv, axis=-1, keepdims=True) + EPS)
        cq_ref[...] = (cq * rq * qa_ref[...]).astype(BF16)
        ckv_ref[...] = (ckv * rkv * kva_ref[...]).astype(BF16)

    return pl.pallas_call(
        body, name="mla_mid", grid=(t // tm,),
        in_specs=[pl.BlockSpec((tm, MLA_IN_PAD), lambda i: (i, 0)),
                  pl.BlockSpec((1, MLA_Q_RANK), lambda i: (0, 0)),
                  pl.BlockSpec((1, MLA_KV_RANK), lambda i: (0, 0))],
        out_specs=[pl.BlockSpec((tm, MLA_Q_RANK), lambda i: (i, 0)),
                   pl.BlockSpec((tm, MLA_KV_RANK), lambda i: (i, 0))],
        out_shape=[_sds((t, MLA_Q_RANK), BF16), _sds((t, MLA_KV_RANK), BF16)],
        compiler_params=_cparams(("parallel",)),
    )(proj2, qa, kva)


def _mla_mid_bwd(proj2, qa, kva, dcq, dckv, dkr):
    t = proj2.shape[0]
    tm = _row_tile(t)

    def body(p_ref, qa_ref, kva_ref, dcq_ref, dckv_ref, dkr_ref, dp_ref, dqa_ref, dkva_ref):
        @pl.when(pl.program_id(0) == 0)
        def _():
            dqa_ref[...] = jnp.zeros_like(dqa_ref)
            dkva_ref[...] = jnp.zeros_like(dkva_ref)

        dxq, dgq = _rms_bwd_rows(dcq_ref[...], p_ref[:, :MLA_Q_RANK], qa_ref[...], MLA_Q_RANK)
        dxk, dgk = _rms_bwd_rows(dckv_ref[...], p_ref[:, MLA_Q_RANK:MLA_Q_RANK + MLA_KV_RANK], kva_ref[...],
                                 MLA_KV_RANK)
        dp_ref[:, :MLA_Q_RANK] = dxq.astype(BF16)
        dp_ref[:, MLA_Q_RANK:MLA_Q_RANK + MLA_KV_RANK] = dxk.astype(BF16)
        dp_ref[:, MLA_Q_RANK + MLA_KV_RANK:] = dkr_ref[...].astype(BF16)
        dqa_ref[...] += jnp.sum(dgq, axis=0, keepdims=True)
        dkva_ref[...] += jnp.sum(dgk, axis=0, keepdims=True)

    return pl.pallas_call(
        body, name="mla_mid_bwd", grid=(t // tm,),
        in_specs=[pl.BlockSpec((tm, MLA_IN_PAD), lambda i: (i, 0)),
                  pl.BlockSpec((1, MLA_Q_RANK), lambda i: (0, 0)),
                  pl.BlockSpec((1, MLA_KV_RANK), lambda i: (0, 0)),
                  pl.BlockSpec((tm, MLA_Q_RANK), lambda i: (i, 0)),
                  pl.BlockSpec((tm, MLA_KV_RANK), lambda i: (i, 0)),
                  pl.BlockSpec((tm, 128), lambda i: (i, 0))],
        out_specs=[pl.BlockSpec((tm, MLA_IN_PAD), lambda i: (i, 0)),
                   pl.BlockSpec((1, MLA_Q_RANK), lambda i: (0, 0)),
                   pl.BlockSpec((1, MLA_KV_RANK), lambda i: (0, 0))],
        out_shape=[_sds((t, MLA_IN_PAD), BF16), _sds((1, MLA_Q_RANK), F32), _sds((1, MLA_KV_RANK), F32)],
        compiler_params=_cparams(("arbitrary",)),
    )(proj2, qa, kva, dcq, dckv, dkr)


def _mla_prep_specs(t, tm):
    head = lambda w: pl.BlockSpec((None, tm, w), lambda i, h: (h, i, 0))
    return dict(
        head256=head(MLA_HD_PAD), head128=head(MLA_VD),
        cols256=pl.BlockSpec((tm, MLA_HD_PAD), lambda i, h: (i, h)),
        cq=pl.BlockSpec((tm, MLA_Q_RANK), lambda i, h: (i, 0)),
        ckv=pl.BlockSpec((tm, MLA_KV_RANK), lambda i, h: (i, 0)),
        wuq=pl.BlockSpec((None, MLA_Q_RANK, MLA_HD_PAD), lambda i, h: (h, 0, 0)),
        wukv=pl.BlockSpec((None, MLA_KV_RANK, MLA_HD_PAD), lambda i, h: (h, 0, 0)),
        kr=pl.BlockSpec((tm, 128), lambda i, h: (i, (MLA_Q_RANK + MLA_KV_RANK) // 128)),
        gain=pl.BlockSpec((1, MLA_HD_PAD), lambda i, h: (0, 0)),
        tab=pl.BlockSpec((tm, 128), lambda i, h: (i, 0)),
    )


def _mla_prep(cq, ckv, wuq, wukv, proj2, gq, gk, tabs):
    t = cq.shape[0]
    tm = _row_tile(t, PREP_ROWS)
    sp = _mla_prep_specs(t, tm)

    def body(cq_ref, ckv_ref, wuq_ref, wukv_ref, kr_ref, gq_ref, gk_ref, c_ref, s1_ref, s2_ref,
             qh_ref, kh_ref, vh_ref):
        c, s1, s2 = c_ref[...], s1_ref[...], s2_ref[...]

        def norm_rope(xv, gain):
            r = lax.rsqrt(jnp.sum(xv * xv, axis=-1, keepdims=True) / MLA_QKD + EPS)
            y = xv * r * gain
            return jnp.concatenate([y[:, :MLA_NOPE], _rope_tile(y[:, MLA_NOPE:], c, s1, s2)], axis=-1)

        kvv = _dot(ckv_ref[...], wukv_ref[...], 1, 0)
        qh_ref[...] = norm_rope(_dot(cq_ref[...], wuq_ref[...], 1, 0), gq_ref[...]).astype(BF16)
        kf = jnp.concatenate([kvv[:, :MLA_NOPE], kr_ref[...]], axis=-1)
        kh_ref[...] = norm_rope(kf, gk_ref[...]).astype(BF16)
        vh_ref[...] = jnp.concatenate([kvv[:, MLA_NOPE:], jnp.ones((tm, MLA_VD), F32)], axis=-1).astype(BF16)

    return pl.pallas_call(
        body, name="mla_prep", grid=(t // tm, MLA_HEADS),
        in_specs=[sp['cq'], sp['ckv'], sp['wuq'], sp['wukv'], sp['kr'], sp['gain'], sp['gain'],
                  sp['tab'], sp['tab'], sp['tab']],
        out_specs=[sp['head256'], sp['head256'], sp['head256']],
        out_shape=[_sds((MLA_HEADS, t, MLA_HD_PAD), BF16), _sds((MLA_HEADS, t, MLA_HD_PAD), BF16),
                   _sds((MLA_HEADS, t, 2 * MLA_VD), BF16)],
        compiler_params=_cparams(("parallel", "arbitrary")),
    )(cq, ckv, wuq, wukv, proj2, gq, gk, *tabs)


def _mla_prep_bwd(cq, ckv, wuq, wukv, proj2, gq, gk, tabs, dqt, dkh, dvh):
    t = cq.shape[0]
    tm = _row_tile(t, PREP_ROWS)
    ab = dqt.shape[-1]
    sp = _mla_prep_specs(t, tm)

    def body(cq_ref, ckv_ref, wuq_ref, wukv_ref, kr_ref, gq_ref, gk_ref, c_ref, s1_ref, s2_ref,
             dqt_ref, dkh_ref, dvh_ref, dq_ref, dkv_ref, dkr_ref, dgq_ref, dgk_ref):
        dqh = jnp.concatenate([dqt_ref[b].T for b in range(tm // ab)], axis=0)
        i, h = pl.program_id(0), pl.program_id(1)

        @pl.when((i == 0) & (h == 0))
        def _():
            dgq_ref[...] = jnp.zeros_like(dgq_ref)
            dgk_ref[...] = jnp.zeros_like(dgk_ref)

        @pl.when(h == 0)
        def _():
            dkr_ref[...] = jnp.zeros_like(dkr_ref)

        c, s1, s2 = c_ref[...], s1_ref[...], s2_ref[...]

        def back(xv, gain, dout):
            dy = jnp.concatenate([dout[:, :MLA_NOPE], _rope_tile(dout[:, MLA_NOPE:], c, -s1, -s2)], axis=-1)
            return _rms_bwd_rows(dy, xv, gain, MLA_QKD)

        kvv = _dot(ckv_ref[...], wukv_ref[...], 1, 0)
        dxq, dgq = back(_dot(cq_ref[...], wuq_ref[...], 1, 0), gq_ref[...], dqh)
        kf = jnp.concatenate([kvv[:, :MLA_NOPE], kr_ref[...]], axis=-1)
        dxk, dgk = back(kf, gk_ref[...], dkh_ref[...])
        dq_ref[...] = dxq.astype(BF16)
        dkv_ref[...] = jnp.concatenate([dxk[:, :MLA_NOPE], dvh_ref[...]], axis=-1).astype(BF16)
        dkr_ref[...] += dxk[:, MLA_NOPE:]
        dgq_ref[...] += jnp.sum(dgq, axis=0, keepdims=True)
        dgk_ref[...] += jnp.sum(dgk, axis=0, keepdims=True)

    return pl.pallas_call(
        body, name="mla_prep_bwd", grid=(t // tm, MLA_HEADS),
        in_specs=[sp['cq'], sp['ckv'], sp['wuq'], sp['wukv'], sp['kr'], sp['gain'], sp['gain'],
                  sp['tab'], sp['tab'], sp['tab'],
                  pl.BlockSpec((None, tm // ab, MLA_HD_PAD, ab), lambda i, h: (h, i, 0, 0)),
                  sp['head256'], sp['head128']],
        out_specs=[sp['cols256'], sp['cols256'], sp['tab'], sp['gain'], sp['gain']],
        out_shape=[_sds((t, MLA_HEADS * MLA_HD_PAD), BF16), _sds((t, MLA_HEADS * MLA_HD_PAD), BF16),
                   _sds((t, 128), F32), _sds((1, MLA_HD_PAD), F32), _sds((1, MLA_HD_PAD), F32)],
        compiler_params=_cparams(("arbitrary", "arbitrary")),
    )(cq, ckv, wuq, wukv, proj2, gq, gk, *tabs, dqt, dkh, dvh)


def _chunk_visible(rows, cols, row_off, col_off):
    rq = lax.shift_right_logical(lax.broadcasted_iota(jnp.int32, (rows, cols), 0) + row_off, 6)
    ck = lax.shift_right_logical(lax.broadcasted_iota(jnp.int32, (rows, cols), 1) + col_off, 6)
    return ck <= rq


def _rows_to_lanes(col):
    return col.T[:8, :]


def _attn_fwd(qh, kh, vh):
    t = qh.shape[1]
    ab = min(ATT_BLOCK, t)
    tq = min(ATT_QROWS, t)
    r = tq // ab
    hg = ATT_HEADS

    def body(q_ref, k_ref, v_ref, o_ref, lse_ref, acc_ref):
        n_un = pl.program_id(1) * r
        acc_ref[...] = jnp.zeros_like(acc_ref)

        def step(b, ms, diag):
            rows = pl.ds(pl.multiple_of(b * ab, ab), ab)
            out = []
            for hh in range(hg):
                m = ms[hh]
                s = _dot(q_ref[hh], k_ref[hh, rows, :], 1, 1)
                if diag is not None:
                    s = jnp.where(_chunk_visible(tq, ab, 0, diag * ab), s, -1e30)
                m_new = jnp.maximum(m, jnp.max(s, axis=-1, keepdims=True))
                p = jnp.exp2((s - m_new) * ATT_EXP2).astype(BF16)
                acc_ref[hh] = jnp.exp2((m - m_new) * ATT_EXP2) * acc_ref[hh] + _dot(p, v_ref[hh, rows, :], 1, 0)
                out.append(m_new)
            return tuple(out)

        ms = tuple(jnp.full((tq, 1), -1e30, F32) for _ in range(hg))
        ms = lax.fori_loop(0, n_un, lambda b, st: step(b, st, None), ms)
        for d in range(r):
            ms = step(n_un + d, ms, d)
        for hh in range(hg):
            l = acc_ref[hh, :, MLA_VD:]
            o_ref[:, hh * MLA_VD:(hh + 1) * MLA_VD] = acc_ref[hh, :, :MLA_VD] / l
            lse_t = _rows_to_lanes(ms[hh] * ATT_EXP2 + jnp.log(l) * LOG2E)
            for d in range(r):
                lse_ref[hh, d] = lse_t[:, d * ab:(d + 1) * ab]

    return pl.pallas_call(
        body, name="mla_attn", grid=(MLA_HEADS // hg, t // tq),
        in_specs=[pl.BlockSpec((hg, tq, MLA_HD_PAD), lambda g, i: (g, i, 0)),
                  pl.BlockSpec((hg, t, MLA_HD_PAD), lambda g, i: (g, 0, 0)),
                  pl.BlockSpec((hg, t, 2 * MLA_VD), lambda g, i: (g, 0, 0))],
        out_specs=[pl.BlockSpec((tq, hg * MLA_VD), lambda g, i: (i, g)),
                   pl.BlockSpec((hg, r, 8, ab), lambda g, i: (g, i, 0, 0))],
        out_shape=[_sds((t, MLA_HEADS * MLA_VD), F32), _sds((MLA_HEADS, t // ab, 8, ab), F32)],
        scratch_shapes=[pltpu.VMEM((hg, tq, 2 * MLA_VD), F32)],
        compiler_params=_cparams(("parallel", "arbitrary")),
    )(qh, kh, vh)


def _attn_bwd(qh, kh, vh, dob, o, lse_t):
    t = qh.shape[1]
    ab = min(ATT_BLOCK, t)
    kb = min(ATT_KROWS, t)
    r = kb // ab
    nq = t // ab
    hg = ATT_HEADS

    def body(q_ref, k_ref, v_ref, do_ref, o_ref, lse_ref, dqt_ref, dk_ref, dv_ref, dl_ref):
        j = pl.program_id(1)

        @pl.when(j == 0)
        def _():
            dqt_ref[...] = jnp.zeros_like(dqt_ref)
            ones = jnp.ones((8, MLA_VD), F32)

            def delta(b, carry):
                rows = pl.ds(pl.multiple_of(b * ab, ab), ab)
                for hh in range(hg):
                    cols = slice(hh * MLA_VD, (hh + 1) * MLA_VD)
                    prod = do_ref[rows, cols].astype(F32) * o_ref[rows, cols]
                    dl_ref[hh, b] = lax.dot_general(ones, prod, (((1,), (1,)), ((), ())),
                                                    precision=lax.Precision.HIGHEST, preferred_element_type=F32)
                return carry

            lax.fori_loop(0, nq, delta, 0)

        ks = [k_ref[hh] for hh in range(hg)]
        vs = [v_ref[hh, :, :MLA_VD] for hh in range(hg)]
        kts = [k.T for k in ks]

        dk_ref[...] = jnp.zeros_like(dk_ref)
        dv_ref[...] = jnp.zeros_like(dv_ref)

        def step(b, carry, diag):
            rows = pl.ds(pl.multiple_of(b * ab, ab), ab)
            hi = kb if diag is None else (diag + 1) * ab
            for hh in range(hg):
                q = q_ref[hh, rows, :]
                do = do_ref[rows, hh * MLA_VD:(hh + 1) * MLA_VD]
                s_t = _dot(ks[hh][:hi], q, 1, 1)
                if diag is not None:
                    key_chunk = lax.shift_right_logical(lax.broadcasted_iota(jnp.int32, (hi, ab), 0), 6)
                    query_chunk = lax.shift_right_logical(
                        lax.broadcasted_iota(jnp.int32, (hi, ab), 1) + diag * ab, 6)
                    s_t = jnp.where(key_chunk <= query_chunk, s_t, -1e30)
                p_t = jnp.exp2(s_t * ATT_EXP2 - lse_ref[hh, b][0:1, :])
                dp_t = _dot(vs[hh][:hi], do, 1, 1)
                ds_t = (p_t * (dp_t - dl_ref[hh, b][0:1, :]) * ATT_SCALE).astype(BF16)
                dqt_ref[hh, b] += _dot(kts[hh][:, :hi], ds_t, 1, 0)
                dk_ref[hh, :hi] += _dot(ds_t, q, 1, 0)
                dv_ref[hh, :hi] += _dot(p_t.astype(BF16), do, 1, 0)
            return carry

        for d in range(r):
            step(j * r + d, 0, d)
        lax.fori_loop((j + 1) * r, nq, lambda b, c: step(b, c, None), 0)

    whole = lambda w: pl.BlockSpec((hg, t, w), lambda g, j: (g, 0, 0))
    blk = lambda w: pl.BlockSpec((hg, kb, w), lambda g, j: (g, j, 0))
    stat = pl.BlockSpec((hg, nq, 8, ab), lambda g, j: (g, 0, 0, 0))
    cols = pl.BlockSpec((t, hg * MLA_VD), lambda g, j: (0, g))
    return pl.pallas_call(
        body, name="mla_attn_bwd", grid=(MLA_HEADS // hg, t // kb),
        in_specs=[whole(MLA_HD_PAD), blk(MLA_HD_PAD), blk(2 * MLA_VD),
                  cols, cols, stat],
        out_specs=[pl.BlockSpec((hg, nq, MLA_HD_PAD, ab), lambda g, j: (g, 0, 0, 0)), blk(MLA_HD_PAD), blk(MLA_VD)],
        out_shape=[_sds((MLA_HEADS, nq, MLA_HD_PAD, ab), F32), _sds((MLA_HEADS, t, MLA_HD_PAD), F32),
                   _sds((MLA_HEADS, t, MLA_VD), F32)],
        scratch_shapes=[pltpu.VMEM((hg, nq, 8, ab), F32)],
        compiler_params=_cparams(("parallel", "arbitrary")),
    )(qh, kh, vh, dob, o, lse_t)


VEC = pl.BlockSpec((1, D_MODEL), lambda i, j, k: (0, 0))


def _rows(tm, width):
    return pl.BlockSpec((tm, width), lambda i, j, k: (i, 0))


def _residual_epi(next_gain):
    if next_gain is None:
        return [], lambda acc, hv: (acc + hv,)

    def epi(acc, hv, g):
        h_new = acc + hv
        r = lax.rsqrt(jnp.mean(h_new * h_new, axis=-1, keepdims=True) + EPS)
        return h_new, h_new * r * g

    return [(next_gain, VEC)], epi


def _residual_outs(t, row, next_gain):
    outs = [(_sds((t, D_MODEL), F32), row)]
    return outs + ([(_sds((t, D_MODEL), BF16), row)] if next_gain is not None else [])


def _mlp_fwd(l, h, hn, w1g, fetch_w2, next_gain):
    t = h.shape[0]
    tm = _row_tile(t, 512)

    def relu2(acc):
        r = jnp.maximum(acc, 0.0)
        return (r * r,)

    (u,) = _mm_rows(f"mlp_up{l}", tm, hn, w1g, 'nn_cols', [(_sds((t, D_FF), BF16), _rows(tm, D_FF))], epi=relu2)
    w2g = fetch_w2((u,))
    row = _rows(tm, D_MODEL)
    more, epi = _residual_epi(next_gain)
    h2, hn_next = _mm_rows(f"mlp_down{l}", tm, u, w2g, 'nn_rows', _residual_outs(t, row, next_gain),
                           extras=[(h, row)] + more, epi=epi)
    return h2, hn_next, (h, hn, u, w1g, w2g)


def _norm_bwd_outs(t, tm):
    return [(_sds((t, D_MODEL), F32), pl.BlockSpec((tm, D_MODEL), lambda i, j, k: (i, 0))),
            (_sds((t // tm, 1, D_MODEL), F32), pl.BlockSpec((None, 1, D_MODEL), lambda i, j, k: (i, 0, 0)))]


def _norm_bwd_epi(acc, xv, res, g):
    dx, dgr = _rms_bwd_rows(acc, xv, g, D_MODEL)
    return res + dx, jnp.sum(dgr, axis=0, keepdims=True)


def _mlp_bwd(l, dh, saved, norm_g, emit_w2=None, emit_w1=None):
    h, hn, u, w1g, w2g = saved
    t = h.shape[0]
    tm = _row_tile(t, 512)
    nsh, _, wsh = w1g.shape
    wide = _rows(tm, D_FF)
    (da,) = _mm_rows(f"mlp_du{l}", tm, dh, w2g, 'nt_rows', [(_sds((t, D_FF), BF16), wide)], extras=[(u, wide)],
                     epi=lambda acc, uv: (2.0 * jnp.sqrt(uv.astype(F32)) * acc,))
    tw = _row_tile(t, 512)
    (dw2,) = _mm(f"mlp_dw2{l}", (1, 1, t // tw),
                 u, pl.BlockSpec((tw, D_FF), lambda i, j, k: (k, 0)),
                 dh, pl.BlockSpec((tw, D_MODEL), lambda i, j, k: (k, 0)), (0, 0),
                 [(_sds((D_FF, D_MODEL), BF16), pl.BlockSpec((D_FF, D_MODEL), lambda i, j, k: (0, 0)))])
    dw2 = dw2.reshape(nsh, wsh, D_MODEL)
    (dw1,) = _mm(f"mlp_dw1{l}", (1, 1, t // tw),
                 hn, pl.BlockSpec((tw, D_MODEL), lambda i, j, k: (k, 0)),
                 da, pl.BlockSpec((tw, D_FF), lambda i, j, k: (k, 0)), (0, 0),
                 [(_sds((nsh, D_MODEL, wsh), BF16), pl.BlockSpec((nsh, D_MODEL, wsh), lambda i, j, k: (0, 0, 0)))],
                 split=wsh, deps=emit_w2(dw2) if emit_w2 else ())
    row = _rows(tm, D_MODEL)
    dh_in, dg = _mm_rows(f"mlp_dhn{l}", tm, da, w1g, 'nt_cols', _norm_bwd_outs(t, tm),
                         extras=[(h, row), (dh, row), (norm_g, VEC)], epi=_norm_bwd_epi,
                         deps=emit_w1(dw1) if emit_w1 else ())
    return dh_in, jnp.sum(dg, axis=0), dw1, dw2


def _ple_fwd(l, h, hn, p, wg, wp, next_gain, target=None):
    t = h.shape[0]
    tm = _row_tile(t, 512)
    row = pl.BlockSpec((tm, D_MODEL), lambda i, j, k: (i, 0))
    full = lambda r: pl.BlockSpec((r, D_MODEL), lambda i, j, k: (0, 0))
    f32_row, bf_row = (_sds((t, D_MODEL), F32), row), (_sds((t, D_MODEL), BF16), row)
    common = [(h, row), (p, pl.BlockSpec((None, None, tm, PLE_DIM), lambda i, j, k: (l, 0, i, 0))),
              (wp, full(PLE_DIM))]
    if target is not None:
        def loss_epi(acc, hv, pv, wpv, tv):
            gt = _sigmoid(acc)
            ev = _dot(_bf(pv), wpv, 1, 0)
            err = hv + gt * ev - tv
            sq = jnp.sum(jnp.sum(err * err, axis=-1, keepdims=True), axis=0, keepdims=True)
            return err / D_MODEL, gt, ev, jnp.broadcast_to(sq, (8, 128))

        dy, gate, e, sq = _mm(f"ple_gate{l}", (t // tm, 1, 1), hn, row, wg, full(D_MODEL), (1, 0),
                              [f32_row, bf_row, bf_row, (_sds((t // tm, 8, 128), F32),
                                                         pl.BlockSpec((None, 8, 128), lambda i, j, k: (i, 0, 0)))],
                              extras=common + [(target, row)], epi=loss_epi)
        return dy, jnp.sum(sq, axis=0), (h, hn, gate, e)

    def gate_epi(acc, hv, pv, wpv, *gain):
        gt = _sigmoid(acc)
        ev = _dot(_bf(pv), wpv, 1, 0)
        h_new = hv + gt * ev
        if not gain:
            return h_new, gt, ev
        r = lax.rsqrt(jnp.mean(h_new * h_new, axis=-1, keepdims=True) + EPS)
        return h_new, gt, ev, h_new * r * gain[0]

    res = _mm(f"ple_gate{l}", (t // tm, 1, 1), hn, row, wg, full(D_MODEL), (1, 0),
              [f32_row, bf_row, bf_row] + ([bf_row] if next_gain is not None else []),
              extras=common + ([(next_gain, VEC)] if next_gain is not None else []), epi=gate_epi)
    h_out, gate, e = res[0], res[1], res[2]
    return h_out, (res[3] if next_gain is not None else None), (h, hn, gate, e)


def _ple_bwd(l, dh, saved, p, norm_g, wg, deps=(), emit=None):
    h, hn, gate, e = saved
    t = h.shape[0]
    tm = _row_tile(t)
    tk = _row_tile(t, 512)
    de, dz = _ple_gate_bwd(f"ple_gate_bwd{l}", dh, gate, e)
    full = lambda r: pl.BlockSpec((r, D_MODEL), lambda i, j, k: (0, 0))
    rowk = pl.BlockSpec((tk, D_MODEL), lambda i, j, k: (k, 0))
    (dwp,) = _mm(f"ple_dwp{l}", (1, 1, t // tk),
                 p, pl.BlockSpec((None, None, tk, PLE_DIM), lambda i, j, k: (l, 0, k, 0)),
                 de, rowk, (0, 0), [(_sds((PLE_DIM, D_MODEL), BF16), full(PLE_DIM))], deps=deps)
    (dwg,) = _mm(f"ple_dwg{l}", (1, 1, t // tk), hn, rowk, dz, rowk, (0, 0),
                 [(_sds((D_MODEL, D_MODEL), BF16), full(D_MODEL))])
    row = pl.BlockSpec((tm, D_MODEL), lambda i, j, k: (i, 0))
    dh_in, dg = _mm(f"ple_dhn{l}", (t // tm, 1, 1), dz, row, wg, full(D_MODEL), (1, 1),
                    _norm_bwd_outs(t, tm), extras=[(h, row), (dh, row), (norm_g, VEC)], epi=_norm_bwd_epi,
                    deps=emit(dwg, dwp) if emit else ())
    return dh_in, jnp.sum(dg, axis=0), dwg, dwp


def _ret_layer_fwd(x, norm_g, wri, fetch_wro, gn, cos, sin, next_gain, hn=None, deps=()):
    t = x.shape[0]
    tm = _row_tile(t)
    nsh, _, wsh = wri.shape
    if hn is None:
        hn = _rms_fwd("mix_norm0", x, norm_g)
    tp = _row_tile(t, 512)
    (proj,) = _mm_rows("ret_in", tp, hn, wri, 'nn_cols', [(_sds((t, RET_IN), BF16), _rows(tp, RET_IN))], deps=deps)
    gated, outp, states = _ret_fwd(proj, cos, sin, gn)
    wro = fetch_wro((gated,))
    row = _rows(tp, D_MODEL)
    more, epi = _residual_epi(next_gain)
    h1, hn_next = _mm_rows("ret_out", tp, gated, wro.reshape(RET_HEADS, RET_DV, D_MODEL), 'nn_rows',
                           _residual_outs(t, row, next_gain), extras=[(x, row)] + more, epi=epi)
    return h1, hn_next, (x, hn, proj, gated, outp, states, wro)


def _ret_layer_bwd(dh, saved, norm_g, wri, gn, cos, sin, emit_out, emit_in, deps=()):
    x, hn, proj, gated, outp, states, wro = saved
    t = x.shape[0]
    tm = _row_tile(t)
    tk = _row_tile(t, 512)
    nsh, _, wsh = wri.shape
    tg = _row_tile(t, 512)
    vw = _rows(tg, RET_V_W)
    dout, dgate, dgn = _mm_rows(
        "ret_dgate", tg, dh, wro.reshape(RET_HEADS, RET_DV, D_MODEL), 'nt_rows',
        [(_sds((t, RET_V_W), BF16), vw), (_sds((t, RET_V_W), BF16), vw),
         (_sds((t // tg, 1, RET_V_W), F32), pl.BlockSpec((None, 1, RET_V_W), lambda i, j, k: (i, 0, 0)))],
        extras=[(outp, vw), (proj, pl.BlockSpec((tg, RET_V_W), lambda i, j, k: (i, (RET_IN - RET_V_W) // RET_V_W))),
                (gn.reshape(1, RET_V_W), pl.BlockSpec((1, RET_V_W), lambda i, j, k: (0, 0)))],
        epi=_ret_gate_bwd_epi, deps=deps)
    dgn = jnp.sum(dgn, axis=0)
    (dwro,) = _mm("ret_dwro", (1, 1, t // tk),
                  gated, pl.BlockSpec((tk, RET_V_W), lambda i, j, k: (k, 0)),
                  dh, pl.BlockSpec((tk, D_MODEL), lambda i, j, k: (k, 0)), (0, 0),
                  [(_sds((RET_V_W, D_MODEL), BF16), pl.BlockSpec((RET_V_W, D_MODEL), lambda i, j, k: (0, 0)))])
    dproj = _ret_bwd(proj, cos, sin, states, dout, dgate, deps=emit_out(dwro))
    half = nsh // 2
    (dwri,) = _mm("ret_dwri", (2, 1, t // tk),
                  hn, pl.BlockSpec((tk, D_MODEL), lambda i, j, k: (k, 0)),
                  dproj, pl.BlockSpec((tk, half * wsh), lambda i, j, k: (k, i)), (0, 0),
                  [(_sds((nsh, D_MODEL, wsh), BF16), pl.BlockSpec((half, D_MODEL, wsh), lambda i, j, k: (i, 0, 0)))],
                  split=wsh)
    deps = emit_in(dwri)
    td = _row_tile(t, 256)
    row = _rows(td, D_MODEL)
    dx, dg = _mm_rows("ret_dhn", td, dproj, wri, 'nt_cols', _norm_bwd_outs(t, td),
                      extras=[(x, row), (dh, row), (norm_g, VEC)], epi=_norm_bwd_epi, deps=deps)
    return dx, jnp.sum(dg, axis=0), dgn.reshape(RET_HEADS, RET_DV)


def _mla_layer_fwd(h, hn, wmi, qa, kva, wuq, wukv, gq, gk, wmo, tabs, next_gain):
    t = h.shape[0]
    tm = _row_tile(t)
    row = pl.BlockSpec((tm, D_MODEL), lambda i, j, k: (i, 0))
    (proj2,) = _mm("mla_in", (t // tm, 1, 1), hn, row,
                   wmi, pl.BlockSpec((D_MODEL, MLA_IN_PAD), lambda i, j, k: (0, 0)), (1, 0),
                   [(_sds((t, MLA_IN_PAD), F32), pl.BlockSpec((tm, MLA_IN_PAD), lambda i, j, k: (i, 0)))])
    cq, ckv = _mla_mid(proj2, qa, kva)
    qh, kh, vh = _mla_prep(cq, ckv, wuq, wukv, proj2, gq, gk, tabs)
    o, lse = _attn_fwd(qh, kh, vh)
    more, epi = _residual_epi(next_gain)
    h_out, hn_next = _mm("mla_out", (t // tm, 1, 1), o, row,
                         wmo, pl.BlockSpec((D_MODEL, D_MODEL), lambda i, j, k: (0, 0)), (1, 0),
                         _residual_outs(t, row, next_gain), extras=[(h, row)] + more, epi=epi)
    return h_out, hn_next, (h, hn, proj2, cq, ckv, qh, kh, vh, o, lse)


def _mla_layer_bwd(dh, saved, norm_g, wmi, qa, kva, wuq, wukv, gq, gk, wmo, tabs, deps=()):
    h, hn, proj2, cq, ckv, qh, kh, vh, o, lse = saved
    t = h.shape[0]
    tm = _row_tile(t)
    tk = _row_tile(t, 512)
    row = pl.BlockSpec((tm, D_MODEL), lambda i, j, k: (i, 0))
    rowk = pl.BlockSpec((tk, D_MODEL), lambda i, j, k: (k, 0))
    sq = pl.BlockSpec((D_MODEL, D_MODEL), lambda i, j, k: (0, 0))
    (dob,) = _mm("mla_do", (t // tm, 1, 1), dh, row, wmo, sq, (1, 1), [(_sds((t, D_MODEL), BF16), row)], deps=deps)
    (dwmo,) = _mm("mla_dwo", (1, 1, t // tk), o, rowk, dh, rowk, (0, 0), [(_sds((D_MODEL, D_MODEL), BF16), sq)])
    dqt, dkh, dvh = _attn_bwd(qh, kh, vh, dob, o, lse)
    dq, dkv, dkr, dgq, dgk = _mla_prep_bwd(cq, ckv, wuq, wukv, proj2, gq, gk, tabs, dqt, dkh, dvh)

    wide = MLA_HEADS * MLA_HD_PAD
    widek = pl.BlockSpec((tk, wide), lambda i, j, k: (k, 0))
    (dwuq,) = _mm("mla_dwuq", (1, 1, t // tk),
                  cq, pl.BlockSpec((tk, MLA_Q_RANK), lambda i, j, k: (k, 0)), dq, widek, (0, 0),
                  [(_sds((MLA_HEADS, MLA_Q_RANK, MLA_HD_PAD), BF16),
                    pl.BlockSpec((MLA_HEADS, MLA_Q_RANK, MLA_HD_PAD), lambda i, j, k: (0, 0, 0)))], split=MLA_HD_PAD)
    (dwukv,) = _mm("mla_dwukv", (1, 1, t // tk),
                   ckv, pl.BlockSpec((tk, MLA_KV_RANK), lambda i, j, k: (k, 0)), dkv, widek, (0, 0),
                   [(_sds((MLA_HEADS, MLA_KV_RANK, MLA_HD_PAD), BF16),
                     pl.BlockSpec((MLA_HEADS, MLA_KV_RANK, MLA_HD_PAD), lambda i, j, k: (0, 0, 0)))],
                   split=MLA_HD_PAD)
    side_by_side = lambda wg: wg.transpose(1, 0, 2).reshape(wg.shape[1], wide)
    widei = pl.BlockSpec((tm, wide), lambda i, j, k: (i, 0))
    (dcq,) = _mm("mla_dcq", (t // tm, 1, 1), dq, widei,
                 side_by_side(wuq), pl.BlockSpec((MLA_Q_RANK, wide), lambda i, j, k: (0, 0)), (1, 1),
                 [(_sds((t, MLA_Q_RANK), F32), pl.BlockSpec((tm, MLA_Q_RANK), lambda i, j, k: (i, 0)))])
    (dckv,) = _mm("mla_dckv", (t // tm, 1, 1), dkv, widei,
                  side_by_side(wukv), pl.BlockSpec((MLA_KV_RANK, wide), lambda i, j, k: (0, 0)), (1, 1),
                  [(_sds((t, MLA_KV_RANK), F32), pl.BlockSpec((tm, MLA_KV_RANK), lambda i, j, k: (i, 0)))])
    dproj2, dqa, dkva = _mla_mid_bwd(proj2, qa, kva, dcq, dckv, dkr)
    win = pl.BlockSpec((D_MODEL, MLA_IN_PAD), lambda i, j, k: (0, 0))
    (dwmi,) = _mm("mla_dwin", (1, 1, t // tk), hn, rowk,
                  dproj2, pl.BlockSpec((tk, MLA_IN_PAD), lambda i, j, k: (k, 0)), (0, 0),
                  [(_sds((D_MODEL, MLA_IN_PAD), BF16), win)])
    dh_in, dg = _mm("mla_dhn", (t // tm, 1, 1),
                    dproj2, pl.BlockSpec((tm, MLA_IN_PAD), lambda i, j, k: (i, 0)), wmi, win, (1, 1),
                    _norm_bwd_outs(t, tm), extras=[(h, row), (dh, row), (norm_g, VEC)], epi=_norm_bwd_epi)
    return dh_in, dict(mix=jnp.sum(dg, axis=0), wmi=dwmi, qa=dqa, kva=dkva, wuq=dwuq, wukv=dwukv, gq=dgq, gk=dgk,
                       wmo=dwmo)


def _local_step(x, p, target, w, fetch, emit=lambda group: ()):
    t = x.shape[0]
    inv = 1.0 / (ROPE_THETA ** (jnp.arange(0, RET_DK, 2, dtype=F32) / RET_DK))
    ang = jnp.arange(t, dtype=F32)[:, None] * inv[None, :]
    cos_r, sin_r = jnp.cos(ang), jnp.sin(ang)
    tabs = _mla_tables(t)
    row = lambda a, i: a[i:i + 1]

    h1, hn1, s_ret = _ret_layer_fwd(x, row(w['mix_norm'], 0), w['ret_w_in'],
                                    lambda after: fetch('ret_out', after)['ret_w_out'], w['ret_gn'], cos_r, sin_r,
                                    row(w['mlp_norm'], 0), hn=w.get('hn0'), deps=w['deps'])
    h2, hn2, s_mlp0 = _mlp_fwd(0, h1, hn1, fetch('mlp_w1_0', (h1,))['mlp_w1'],
                               lambda after: fetch('mlp_w2_0', after)['mlp_w2'], row(w['ple_norm'], 0))
    w0 = fetch('ple_0', (h2,))
    h3, hn3, s_ple0 = _ple_fwd(0, h2, hn2, p, w0['ple_gate_w'], w0['ple_proj_w'], row(w['mix_norm'], 1))
    wm = fetch('mla', (h3,))
    mla_w = (wm['mla_w_in'], w['mla_q_a_norm'], w['mla_kv_a_norm'], wm['mla_w_uq'], wm['mla_w_ukv'],
             w['mla_q_norm'], w['mla_k_norm'], wm['mla_w_out'], tabs)
    h4, hn4, s_mla = _mla_layer_fwd(h3, hn3, *mla_w, row(w['mlp_norm'], 1))
    w1 = fetch('layer_1', (h4,))
    h5, hn5, s_mlp1 = _mlp_fwd(1, h4, hn4, w1['mlp_w1'], lambda after: w1['mlp_w2'], row(w['ple_norm'], 1))
    dy, sq_err, s_ple1 = _ple_fwd(1, h5, hn5, p, w1['ple_gate_w'], w1['ple_proj_w'], None, target)

    n = N_DEV
    colsh = lambda a: a.reshape(a.shape[0], n, a.shape[1] // n).transpose(1, 0, 2)
    rowsh = lambda a: a.reshape(n, a.shape[0] // n, a.shape[1])
    big = {}

    def emit_group(group):
        big.update(group)
        return emit(group)

    dh5, dg_ple1, dwg1, dwp1 = _ple_bwd(1, dy, s_ple1, p, row(w['ple_norm'], 1), w1['ple_gate_w'])
    dh4, dg_mlp1, dw1_1, dw2_1 = _mlp_bwd(1, dh5, s_mlp1, row(w['mlp_norm'], 1))
    deps = emit_group({('ple_gate_w', 1): rowsh(dwg1), ('ple_proj_w', 1): colsh(dwp1),
                       ('mlp_w2', 1): dw2_1, ('mlp_w1', 1): dw1_1})
    dh3, gm = _mla_layer_bwd(dh4, s_mla, row(w['mix_norm'], 1), *mla_w, deps=deps)
    deps = emit_group({('mla_w_out', 0): rowsh(gm['wmo']), ('mla_w_uq', 0): _gather_rope(gm['wuq']),
                       ('mla_w_ukv', 0): gm['wukv'], ('mla_w_in', 0): rowsh(_gather_rope(gm['wmi']))})
    dh2, dg_ple0, _, _ = _ple_bwd(
        0, dh3, s_ple0, p, row(w['ple_norm'], 0), w0['ple_gate_w'], deps=deps,
        emit=lambda dwg, dwp: emit_group({('ple_gate_w', 0): rowsh(dwg), ('ple_proj_w', 0): colsh(dwp)}))
    dh1, dg_mlp0, _, _ = _mlp_bwd(0, dh2, s_mlp0, row(w['mlp_norm'], 0),
                                  emit_w2=lambda dw2: emit_group({('mlp_w2', 0): dw2}),
                                  emit_w1=lambda dw1: emit_group({('mlp_w1', 0): dw1}))
    dx, dg_mix0, dgn = _ret_layer_bwd(
        dh1, s_ret, row(w['mix_norm'], 0), w['ret_w_in'], w['ret_gn'], cos_r, sin_r,
        lambda dwro: emit_group({('ret_w_out', 0): rowsh(dwro)}),
        lambda dwri: emit_group({('ret_w_in', 0): dwri}))

    small = dict(
        mix_norm=[dg_mix0, gm['mix']], mlp_norm=[dg_mlp0, dg_mlp1], ple_norm=[dg_ple0, dg_ple1],
        ret_gn=dgn, mla_q_a_norm=gm['qa'], mla_kv_a_norm=gm['kva'], mla_q_norm=gm['gq'], mla_k_norm=gm['gk'],
    )
    return sq_err, dx, big, small


def _my_place():
    x, y, c = lax.axis_index("x"), lax.axis_index("y"), lax.axis_index("c")
    return x, y, c


def _flat(px, py, pc):
    return 4 * px + 2 * py + pc


def _peer(x, y, c, r):
    return (1 - x if r & 4 else x, 1 - y if r & 2 else y, 1 - c if r & 1 else c)


HBM = pl.BlockSpec(memory_space=pltpu.HBM)
SEMS = pl.BlockSpec(memory_space=pltpu.SEMAPHORE)
SIDE_EFFECT = pltpu.SideEffectType.DATAFLOW_SIDE_EFFECTING


def _rs_copies(x, y, c, srcs, lands, send_sems, recv_sems):
    copies = []
    for a in range(len(srcs)):
        for r in range(1, N_DEV):
            peer = _peer(x, y, c, r)
            k = a * (N_DEV - 1) + r - 1
            copies.append(pltpu.make_async_remote_copy(
                src_ref=srcs[a].at[_flat(*peer)], dst_ref=lands[a].at[r - 1],
                send_sem=send_sems.at[k], recv_sem=recv_sems.at[k], device_id=peer, device_id_type=MESH))
    return copies


def _rs_start(name, arrays):
    n = len(arrays)
    hbm = lambda a: pltpu.with_memory_space_constraint(a, pltpu.HBM)
    lands = [hbm(lax.empty((N_DEV - 1,) + a.shape[1:], a.dtype)) for a in arrays]

    def body(*refs):
        srcs, lnd = refs[:n], refs[n:2 * n]
        send_sems, recv_sems = refs[2 * n], refs[2 * n + 1]
        token = refs[-1]
        for cp in _rs_copies(*_my_place(), srcs, lnd, send_sems, recv_sems):
            cp.start()
        token[...] = jnp.zeros_like(token)

    outs = pl.pallas_call(
        body, name=name,
        in_specs=[HBM] * (2 * n),
        out_specs=[SEMS, SEMS] + [HBM] * (2 * n) + [pl.BlockSpec(memory_space=pltpu.VMEM)],
        out_shape=[pltpu.SemaphoreType.DMA((n * (N_DEV - 1),)), pltpu.SemaphoreType.DMA((n * (N_DEV - 1),))]
        + [pltpu.HBM(a.shape, a.dtype) for a in arrays] + [pltpu.HBM(l.shape, l.dtype) for l in lands]
        + [_sds((8, 128), F32)],
        input_output_aliases={i: 2 + i for i in range(2 * n)},
        compiler_params=pltpu.CompilerParams(has_side_effects=SIDE_EFFECT),
    )(*[hbm(a) for a in arrays], *lands)
    return outs[0], outs[1], outs[2:2 + n], outs[2 + n:2 + 2 * n], outs[-1]


def _rs_wait(name, send_sems, recv_sems, srcs, lands, after):
    n = len(srcs)

    def body(*refs):
        src_refs, lnd = refs[:n], refs[n:2 * n]
        send, recv = refs[2 * n], refs[2 * n + 1]
        for cp in _rs_copies(*_my_place(), src_refs, lnd, send, recv):
            cp.wait_send()
            cp.wait_recv()

    outs = pl.pallas_call(
        body, name=name,
        in_specs=[HBM] * (2 * n) + [SEMS, SEMS] + [ANY] * len(after),
        out_specs=[HBM] * (2 * n),
        out_shape=[pltpu.HBM(a.shape, a.dtype) for a in list(srcs) + list(lands)],
        input_output_aliases={i: i for i in range(2 * n)},
        compiler_params=pltpu.CompilerParams(has_side_effects=SIDE_EFFECT),
    )(*srcs, *lands, send_sems, recv_sems, *after)
    return outs[:n], outs[n:]


SMALL_PACK_ROWS = 16


def _all_reduce_small(rows, deps=()):
    n = len(rows)

    def body(*refs):
        ins = refs[:n]
        out_ref, mine, buf, send_sems, recv_sems = refs[n + len(deps):]
        x, y, c = _my_place()
        mine[...] = jnp.zeros_like(mine)
        for (r0, a), ref in zip(rows, ins):
            mine[r0:r0 + a.shape[0], 0:a.shape[1]] = ref[...]
        buf[_flat(x, y, c)] = mine[...]
        copies = []
        for r in range(1, N_DEV):
            peer = _peer(x, y, c, r)
            send = pltpu.make_async_remote_copy(
                src_ref=mine, dst_ref=buf.at[_flat(x, y, c)],
                send_sem=send_sems.at[r - 1], recv_sem=recv_sems.at[r - 1], device_id=peer, device_id_type=MESH)
            send.start()
            recv = pltpu.make_async_remote_copy(
                src_ref=mine, dst_ref=buf.at[_flat(*peer)],
                send_sem=send_sems.at[r - 1], recv_sem=recv_sems.at[r - 1], device_id=peer, device_id_type=MESH)
            copies.append((send, recv))
        for send, recv in copies:
            send.wait_send()
            recv.wait_recv()
        acc = buf[0]
        for s in range(1, N_DEV):
            acc = acc + buf[s]
        out_ref[...] = acc

    vm = pl.BlockSpec(memory_space=pltpu.VMEM)
    shape = (SMALL_PACK_ROWS, D_MODEL)
    return pl.pallas_call(
        body, name="all_reduce_small", in_specs=[vm] * n + [ANY] * len(deps), out_specs=vm,
        out_shape=_sds(shape, F32),
        scratch_shapes=[pltpu.VMEM(shape, F32), pltpu.VMEM((N_DEV,) + shape, F32),
                        pltpu.SemaphoreType.DMA((7,)), pltpu.SemaphoreType.DMA((7,))],
    )(*[a for _, a in rows], *deps)


def _adamw_math(w, g, m, v):
    m = ADAM_B1 * m + (1.0 - ADAM_B1) * g
    v = ADAM_B2 * v + (1.0 - ADAM_B2) * (g * g)
    m_hat = m / (1.0 - ADAM_B1 ** ADAM_STEP)
    v_hat = v / (1.0 - ADAM_B2 ** ADAM_STEP)
    delta = -ADAM_LR * (m_hat / (jnp.sqrt(v_hat) + ADAM_EPS) + ADAM_WD * w)
    return delta, m, v


def _adamw_big(name, w, m, v, srcs, lands, me):
    nl, rows, cols = w.shape
    tr = next(cand for cand in (256, 128, 64, 32, 16, 8) if rows % cand == 0)

    def body(me_ref, w_ref, m_ref, v_ref, *rest):
        src_refs, land_refs = rest[:nl], rest[nl:2 * nl]
        g_ref, d_ref, mo_ref, vo_ref = rest[2 * nl:]
        for layer in range(nl):
            @pl.when(pl.program_id(0) == layer)
            def _():
                g = src_refs[layer][...].astype(F32)
                for s in range(N_DEV - 1):
                    g = g + land_refs[layer][s].astype(F32)
                delta, mn, vn = _adamw_math(w_ref[...], g, m_ref[...], v_ref[...])
                g_ref[...] = g
                d_ref[...] = delta
                mo_ref[...] = mn
                vo_ref[...] = vn

    blk = pl.BlockSpec((None, tr, cols), lambda l, i, me_ref: (l, i, 0))
    at = lambda layer, l, i: jnp.where(l == layer, i, 0)
    own = [pl.BlockSpec((None, tr, cols), functools.partial(lambda layer, l, i, me_ref: (me_ref[0], at(layer, l, i), 0),
                                                            layer)) for layer in range(nl)]
    peers = [pl.BlockSpec((N_DEV - 1, tr, cols), functools.partial(lambda layer, l, i, me_ref: (0, at(layer, l, i), 0),
                                                                   layer)) for layer in range(nl)]
    return pl.pallas_call(
        body, name=name,
        grid_spec=pltpu.PrefetchScalarGridSpec(
            num_scalar_prefetch=1, grid=(nl, rows // tr),
            in_specs=[blk, blk, blk] + own + peers, out_specs=[blk] * 4),
        out_shape=[_sds((nl, rows, cols), F32)] * 4,
        compiler_params=_cparams(("arbitrary", "arbitrary")),
    )(me, w, m, v, *srcs, *lands)


def _adamw_small(ws, gs, ms, vs):
    n = len(ws)

    def body(*refs):
        w_refs, g_refs, m_refs, v_refs = (refs[i * n:(i + 1) * n] for i in range(4))
        d_out, m_out, v_out = (refs[(4 + i) * n:(5 + i) * n] for i in range(3))
        for i in range(n):
            delta, mn, vn = _adamw_math(w_refs[i][...], g_refs[i][...], m_refs[i][...], v_refs[i][...])
            d_out[i][...] = delta
            m_out[i][...] = mn
            v_out[i][...] = vn

    vm = pl.BlockSpec(memory_space=pltpu.VMEM)
    outs = pl.pallas_call(
        body, name="adamw_small", in_specs=[vm] * (4 * n), out_specs=[vm] * (3 * n),
        out_shape=[_sds(a.shape, F32) for a in ws] * 3,
    )(*ws, *gs, *ms, *vs)
    return outs[:n], outs[n:2 * n], outs[2 * n:]


def _pad_to(a, rows, cols):
    return jnp.pad(a, ((0, rows - a.shape[0]), (0, cols - a.shape[1])))


def _place_own(blocks):
    me = _flat(*_my_place())
    return [lax.dynamic_update_slice(lax.empty((N_DEV,) + b.shape, b.dtype), b[None], (me,) + (0,) * b.ndim)
            for b in blocks]


def _ag_copies(x, y, c, blocks, bufs, send_sems, recv_sems, arriving):
    copies = []
    for a in range(len(blocks)):
        for r in range(1, N_DEV):
            peer = _peer(x, y, c, r)
            k = a * (N_DEV - 1) + r - 1
            copies.append(pltpu.make_async_remote_copy(
                src_ref=blocks[a], dst_ref=bufs[a].at[_flat(*(peer if arriving else (x, y, c)))],
                send_sem=send_sems.at[k], recv_sem=recv_sems.at[k], device_id=peer, device_id_type=MESH))
    return copies


def _ag_start(groups, after):
    flat = [pair for g in groups for pair in g]
    n, ng = len(flat), len(groups)
    hbm = lambda a: pltpu.with_memory_space_constraint(a, pltpu.HBM)

    def body(*refs):
        blocks, bufs = refs[:n], refs[n:2 * n]
        sems = refs[2 * n + len(after):2 * n + len(after) + 2 * ng]
        x, y, c = _my_place()
        at = 0
        for gi, g in enumerate(groups):
            for cp in _ag_copies(x, y, c, blocks[at:at + len(g)], bufs[at:at + len(g)], sems[2 * gi],
                                 sems[2 * gi + 1], arriving=False):
                cp.start()
            at += len(g)
        refs[-1][...] = jnp.zeros_like(refs[-1])

    sem_shapes = [pltpu.SemaphoreType.DMA((len(g) * (N_DEV - 1),)) for g in groups for _ in range(2)]
    outs = pl.pallas_call(
        body, name="gather_start",
        in_specs=[HBM] * (2 * n) + [ANY] * len(after),
        out_specs=[SEMS] * (2 * ng) + [HBM] * (2 * n) + [pl.BlockSpec(memory_space=pltpu.VMEM)],
        out_shape=sem_shapes + [pltpu.HBM(b.shape, b.dtype) for b, _ in flat]
        + [pltpu.HBM(u.shape, u.dtype) for _, u in flat] + [_sds((8, 128), F32)],
        input_output_aliases={i: 2 * ng + i for i in range(2 * n)},
        compiler_params=pltpu.CompilerParams(has_side_effects=SIDE_EFFECT),
    )(*[hbm(b) for b, _ in flat], *[hbm(u) for _, u in flat], *after)
    blocks_thru, bufs_thru = outs[2 * ng:2 * ng + n], outs[2 * ng + n:2 * ng + 2 * n]
    started, at = [], 0
    for gi, g in enumerate(groups):
        started.append((outs[2 * gi], outs[2 * gi + 1], blocks_thru[at:at + len(g)], bufs_thru[at:at + len(g)]))
        at += len(g)
    return started, outs[-1]


def _ag_wait(name, send_sems, recv_sems, blocks, bufs, after):
    n = len(blocks)

    def body(*refs):
        for cp in _ag_copies(*_my_place(), refs[:n], refs[n:2 * n], refs[2 * n], refs[2 * n + 1], arriving=True):
            cp.wait_send()
            cp.wait_recv()

    outs = pl.pallas_call(
        body, name=name,
        in_specs=[HBM] * (2 * n) + [SEMS, SEMS] + [ANY] * len(after),
        out_specs=[HBM] * (2 * n),
        out_shape=[pltpu.HBM(a.shape, a.dtype) for a in list(blocks) + list(bufs)],
        input_output_aliases={i: i for i in range(2 * n)},
        compiler_params=pltpu.CompilerParams(has_side_effects=SIDE_EFFECT),
    )(*blocks, *bufs, send_sems, recv_sems, *after)
    return outs[n:]


def _split_call(name, body, thru, sems_in, new_sems, after):
    n, ns, nn = len(thru), len(sems_in), len(new_sems)
    hbm = lambda a: pltpu.with_memory_space_constraint(a, pltpu.HBM)

    def wrapped(*refs):
        body(refs[:n], refs[n:n + ns], refs[n + ns + len(after):n + ns + len(after) + nn])
        refs[-1][...] = jnp.zeros_like(refs[-1])

    outs = pl.pallas_call(
        wrapped, name=name,
        in_specs=[HBM] * n + [SEMS] * ns + [ANY] * len(after),
        out_specs=[SEMS] * nn + [HBM] * n + [pl.BlockSpec(memory_space=pltpu.VMEM)],
        out_shape=[pltpu.SemaphoreType.DMA((k,)) for k in new_sems] + [pltpu.HBM(a.shape, a.dtype) for a in thru]
        + [_sds((8, 128), F32)],
        input_output_aliases={i: nn + i for i in range(n)},
        compiler_params=pltpu.CompilerParams(has_side_effects=SIDE_EFFECT),
    )(*[hbm(a) for a in thru], *sems_in, *after)
    return list(outs[:nn]), list(outs[nn:nn + n]), outs[-1]


def _first_gather(blocks, bufs, overlap):
    n = len(blocks)

    def copies(refs, s1, r1, s2, r2):
        x, y, c = _my_place()
        me, sibling = (x, y, c), (x, y, 1 - c)
        chips = [(1 - x, y), (x, 1 - y), (1 - x, 1 - y)]
        blk, buf = refs[:n], refs[n:]
        out = dict(send1=[], recv1_sib=[], recv1_ici=[], send2=[], recv2=[])
        for a in range(n):
            place = lambda dev: buf[a].at[_flat(*dev)]
            for k, to in enumerate([sibling] + [(*chip, c) for chip in chips]):
                mk = lambda dst: pltpu.make_async_remote_copy(
                    src_ref=blk[a], dst_ref=dst, send_sem=s1.at[4 * a + k], recv_sem=r1.at[4 * a + k],
                    device_id=to, device_id_type=MESH)
                out['send1'].append(mk(place(me)))
                out['recv1_sib' if k == 0 else 'recv1_ici'].append(mk(place(to)))
            for j, chip in enumerate(chips):
                mk = lambda dev: pltpu.make_async_remote_copy(
                    src_ref=place(dev), dst_ref=place(dev), send_sem=s2.at[3 * a + j], recv_sem=r2.at[3 * a + j],
                    device_id=sibling, device_id_type=MESH)
                out['send2'].append(mk((*chip, c)))
                out['recv2'].append(mk((*chip, 1 - c)))
        return out

    def start(refs, sems_in, new):
        for cp in copies(refs, new[0], new[1], new[0], new[1])['send1']:
            cp.start()

    def forward(refs, sems_in, new):
        cps = copies(refs, sems_in[0], sems_in[1], new[0], new[1])
        for cp in cps['recv1_ici']:
            cp.wait_recv()
        for cp in cps['send2']:
            cp.start()

    def finish(refs, sems_in, new):
        cps = copies(refs, *sems_in)
        for cp in cps['recv1_sib'] + cps['recv2']:
            cp.wait_recv()
        for cp in cps['send1'] + cps['send2']:
            cp.wait_send()

    sems1, thru, token = _split_call("first_gather_start", start, list(blocks) + list(bufs), [], [4 * n, 4 * n], ())
    after = overlap(token)
    sems2, thru, token = _split_call("first_gather_forward", forward, thru, sems1, [3 * n, 3 * n], after)
    _, thru, _ = _split_call("first_gather_wait", finish, thru, sems1 + sems2, [], ())
    return thru[n:], token


def _prepare_weights(p, x):
    n = N_DEV
    bf = lambda a: a.astype(BF16)
    gn_pack = jnp.concatenate([
        _pad_to(p['ret_gn'][0], RET_HEADS, 128), _pad_to(p['mla_q_a_norm'], 1, 128),
        _pad_to(p['mla_kv_a_norm'], 1, 128), jnp.zeros((2, 128), F32)], axis=0)
    ple = lambda l: [bf(p['ple_gate_w'][l]), bf(p['ple_proj_w'][l])]
    names = ('ret_out', 'mlp_w1_0', 'mlp_w2_0', 'ple_0', 'mla', 'layer_1')
    later = [[bf(p['ret_w_out'][0])], [bf(p['mlp_w1'][0])], [bf(p['mlp_w2'][0])], ple(0),
             [bf(p['mla_w_in'][0]), bf(p['mla_w_uq'][0]), bf(p['mla_w_ukv'][0]), bf(p['mla_w_out'][0])],
             [bf(p['mlp_w1'][1]), bf(p['mlp_w2'][1])] + ple(1)]
    first = [gn_pack, bf(p['ret_w_in'][0])]
    behind = {}

    def overlap(token):
        behind['hn0'] = _rms_fwd("mix_norm0", x, p['mix_norm'][0:1], deps=(token,))
        behind['bufs'] = _place_own([b for g in later for b in g])
        return (behind['hn0'], *behind['bufs'])

    (pack, wri), token = _first_gather(first, _place_own(first), overlap)
    bufs = behind['bufs']
    groups, at = [], 0
    for g in later:
        groups.append(list(zip(g, bufs[at:at + len(g)])))
        at += len(g)
    started, token = _ag_start(groups, (token,))

    w = {k: p[k] for k in ('mix_norm', 'mlp_norm', 'ple_norm')}
    w['hn0'] = behind['hn0']
    w['ret_gn'] = pack[:, :RET_HEADS, :RET_DV // n].transpose(1, 0, 2).reshape(RET_HEADS, RET_DV)
    w['mla_q_a_norm'] = pack[:, RET_HEADS, :MLA_Q_RANK // n].reshape(1, MLA_Q_RANK)
    w['mla_kv_a_norm'] = pack[:, RET_HEADS + 1, :MLA_KV_RANK // n].reshape(1, MLA_KV_RANK)
    w['ret_w_in'] = wri
    w['mla_q_norm'] = _spread_rope(p['mla_q_norm'])
    w['mla_k_norm'] = _spread_rope(p['mla_k_norm'])
    w['deps'] = (token,)

    def fetch(name, after):
        got = list(_ag_wait("gather_wait_" + name, *started[names.index(name)], after))
        if name == 'ret_out':
            return dict(ret_w_out=got[0].reshape(RET_V_W, D_MODEL))
        if name == 'mla':
            wmi, wuq, wukv, wmo = got
            return dict(mla_w_in=_spread_rope(wmi.reshape(D_MODEL, MLA_IN)), mla_w_uq=_spread_rope(wuq),
                        mla_w_ukv=wukv, mla_w_out=wmo.reshape(D_MODEL, D_MODEL))
        out = {}
        if name in ('mlp_w1_0', 'layer_1'):
            out['mlp_w1'] = got.pop(0)
        if name in ('mlp_w2_0', 'layer_1'):
            out['mlp_w2'] = got.pop(0)
        if name in ('ple_0', 'layer_1'):
            out['ple_gate_w'] = got[0].reshape(D_MODEL, D_MODEL)
            out['ple_proj_w'] = got[1].transpose(1, 0, 2).reshape(PLE_DIM, D_MODEL)
        return out

    return w, fetch


def _small_grads(small, after):
    rows = [(0, small['mix_norm'][0]), (1, small['mix_norm'][1]), (2, small['mlp_norm'][0]),
            (3, small['mlp_norm'][1]), (4, small['ple_norm'][0]), (5, small['ple_norm'][1]),
            (6, small['ret_gn']), (10, small['mla_q_a_norm']), (11, small['mla_kv_a_norm']),
            (12, small['mla_q_norm']), (13, small['mla_k_norm']), (14, small['sq_err'])]
    gs = _all_reduce_small(rows, after)
    me = _flat(*_my_place())
    n = N_DEV
    return dict(
        sq_err=gs[14, 0],
        mix_norm=gs[0:2], mlp_norm=gs[2:4], ple_norm=gs[4:6],
        ret_gn=lax.dynamic_slice(gs, (6, me * (RET_DV // n)), (RET_HEADS, RET_DV // n)),
        mla_q_a_norm=lax.dynamic_slice(gs, (10, me * (MLA_Q_RANK // n)), (1, MLA_Q_RANK // n)),
        mla_kv_a_norm=lax.dynamic_slice(gs, (11, me * (MLA_KV_RANK // n)), (1, MLA_KV_RANK // n)),
        mla_q_norm=_gather_rope(gs[12:13, :MLA_HD_PAD]), mla_k_norm=_gather_rope(gs[13:14, :MLA_HD_PAD]))


def kernel(x, p, mix_norm, ret_w_in, ret_gn, ret_w_out, mla_w_in, mla_q_a_norm, mla_kv_a_norm, mla_w_uq, mla_w_ukv, mla_q_norm, mla_k_norm, mla_w_out, mlp_norm, mlp_w1, mlp_w2, ple_norm, ple_gate_w, ple_proj_w, loss_target, m_mix_norm, m_ret_w_in, m_ret_gn, m_ret_w_out, m_mla_w_in, m_mla_q_a_norm, m_mla_kv_a_norm, m_mla_w_uq, m_mla_w_ukv, m_mla_q_norm, m_mla_k_norm, m_mla_w_out, m_mlp_norm, m_mlp_w1, m_mlp_w2, m_ple_norm, m_ple_gate_w, m_ple_proj_w, v_mix_norm, v_ret_w_in, v_ret_gn, v_ret_w_out, v_mla_w_in, v_mla_q_a_norm, v_mla_kv_a_norm, v_mla_w_uq, v_mla_w_ukv, v_mla_q_norm, v_mla_k_norm, v_mla_w_out, v_mlp_norm, v_mlp_w1, v_mlp_w2, v_ple_norm, v_ple_gate_w, v_ple_proj_w):
    given = dict(locals())
    params = {n: given[n] for n in WEIGHTS}
    w, fetch = _prepare_weights(params, x[0])

    started = []

    def emit(group):
        keys = list(group)
        send, recv, srcs, lands, token = _rs_start(f"rs_start{len(started)}", [group[k] for k in keys])
        started.append((keys, send, recv, srcs, lands))
        return (token,)

    sq_err, grad_x, _, small = _local_step(x[0], p, loss_target[0], w, fetch, emit)
    small['sq_err'] = sq_err[0:1]

    grads, deltas, new_m, new_v = {}, {}, {}, {}
    total = {}

    def small_updates(after):
        sg = _small_grads(small, after)
        total['loss'] = 0.5 / D_MODEL * sg['sq_err']
        two_d = lambda a: a.reshape(-1, a.shape[-1])
        d_s, m_s, v_s = _adamw_small(
            [two_d(params[n]) for n in SMALL], [sg[n] for n in SMALL],
            [two_d(given["m_" + n]) for n in SMALL], [two_d(given["v_" + n]) for n in SMALL])
        for i, n in enumerate(SMALL):
            shape = params[n].shape
            grads[n], deltas[n], new_m[n], new_v[n] = (a.reshape(shape) for a in (sg[n], d_s[i], m_s[i], v_s[i]))
        return (d_s[0],)

    me = _flat(*_my_place()).astype(jnp.int32).reshape(1)
    after = (grad_x,)
    src_of, land_of = {}, {}
    for gi, (keys, send, recv, srcs, lands) in enumerate(started):
        if gi == len(started) - 1:
            after = small_updates(after)
        srcs, lands = _rs_wait(f"rs_wait{gi}", send, recv, srcs, lands, after)
        for k, s, l in zip(keys, srcs, lands):
            src_of[k], land_of[k] = s, l
        done = [n for n in BIG if n not in grads and all((n, l) in src_of for l in range(params[n].shape[0]))]
        for n in done:
            layers = range(params[n].shape[0])
            grads[n], deltas[n], new_m[n], new_v[n] = _adamw_big(
                "adamw_" + n, params[n], given["m_" + n], given["v_" + n],
                [src_of[(n, l)] for l in layers], [land_of[(n, l)] for l in layers], me)
        if done:
            after = tuple(deltas[n] for n in done)

    return (total['loss'], grad_x[None], *[grads[n] for n in WEIGHTS], *[deltas[n] for n in WEIGHTS],
            *[new_m[n] for n in WEIGHTS], *[new_v[n] for n in WEIGHTS])
```

```python
import functools

import jax
import jax.numpy as jnp
from jax import lax
from jax.experimental import pallas as pl
from jax.experimental.pallas import tpu as pltpu

F32 = jnp.float32
BF16 = jnp.bfloat16
MESH = pl.DeviceIdType.MESH
ANY = pl.BlockSpec(memory_space=pl.ANY)

N_DEV = 8
D_MODEL = 1024
CHUNK = 64
RET_BLOCK = 4 * CHUNK
EPS = 1e-6
ROPE_THETA = 10000.0
RET_HEADS = 4
RET_DK = 256
RET_DV = 512
RET_QK_W = RET_HEADS * RET_DK
RET_V_W = RET_HEADS * RET_DV
RET_IN = 2 * RET_QK_W + 2 * RET_V_W
MLA_HEADS = 8
MLA_NOPE = 128
MLA_ROPE = 64
MLA_QKD = MLA_NOPE + MLA_ROPE
MLA_VD = 128
MLA_Q_RANK = 384
MLA_KV_RANK = 256
MLA_IN = MLA_Q_RANK + MLA_KV_RANK + MLA_ROPE
MLA_IN_PAD = 768
MLA_HD_PAD = 256
D_FF = 4096
PLE_DIM = 256
ATT_SCALE = MLA_QKD ** -0.5
LOG2E = 1.4426950408889634
ATT_EXP2 = ATT_SCALE * LOG2E

ADAM_LR = 0.001
ADAM_B1 = 0.9
ADAM_B2 = 0.999
ADAM_EPS = 1e-08
ADAM_WD = 0.01
ADAM_STEP = 10

VMEM_LIMIT = 52 * 1024 * 1024
ROW_TILE = 1024
RET_ROWS = 512
ATT_BLOCK = 256
ATT_QROWS = 1024
ATT_KROWS = 1024
ATT_HEADS = 2
PREP_ROWS = 1024

WEIGHTS = ['mix_norm', 'ret_w_in', 'ret_gn', 'ret_w_out', 'mla_w_in', 'mla_q_a_norm', 'mla_kv_a_norm',
           'mla_w_uq', 'mla_w_ukv', 'mla_q_norm', 'mla_k_norm', 'mla_w_out', 'mlp_norm', 'mlp_w1', 'mlp_w2',
           'ple_norm', 'ple_gate_w', 'ple_proj_w']
BIG = ['ret_w_in', 'ret_w_out', 'mla_w_in', 'mla_w_uq', 'mla_w_ukv', 'mla_w_out', 'mlp_w1', 'mlp_w2',
       'ple_gate_w', 'ple_proj_w']
SMALL = [w for w in WEIGHTS if w not in BIG]


def _cparams(sem=None):
    return pltpu.CompilerParams(dimension_semantics=sem, vmem_limit_bytes=VMEM_LIMIT)


def _dot(a, b, ca, cb):
    return lax.dot_general(a, b, (((ca,), (cb,)), ((), ())), preferred_element_type=F32)


def _bf(v):
    return v if v.dtype == BF16 else v.astype(BF16)


def _sigmoid(z):
    return 1.0 / (1.0 + jnp.exp(-z))


def _mm(name, grid, a, a_spec, b, b_spec, contract, outs, extras=(), epi=None, deps=(), split=None):
    nk = grid[2]
    n_ex, n_out, n_dep = len(extras), len(outs), len(deps)
    acc_shape = tuple(d for d in outs[0][1].block_shape if d is not None)
    if split is not None:
        acc_shape = (acc_shape[1], acc_shape[0] * split)

    def body(*refs):
        a_ref, b_ref = refs[:2]
        ex_refs = refs[2:2 + n_ex]
        out_refs = refs[2 + n_ex + n_dep:2 + n_ex + n_dep + n_out]

        def product():
            return _dot(_bf(a_ref[...]), _bf(b_ref[...]), contract[0], contract[1])

        def finish(acc):
            if split is not None:
                for j in range(acc_shape[1] // split):
                    out_refs[0][j] = acc[:, j * split:(j + 1) * split].astype(out_refs[0].dtype)
                return
            acc = acc[...]
            res = epi(acc, *[r[...] for r in ex_refs]) if epi is not None else (acc,)
            for o, r in zip(out_refs, res):
                o[...] = r.astype(o.dtype)

        if nk == 1:
            finish(product())
        else:
            acc_ref = refs[-1]
            k = pl.program_id(2)

            @pl.when(k == 0)
            def _():
                acc_ref[...] = jnp.zeros_like(acc_ref)

            acc_ref[...] += product()

            @pl.when(k == nk - 1)
            def _():
                finish(acc_ref)

    return pl.pallas_call(
        body, name=name, grid=grid,
        in_specs=[a_spec, b_spec] + [s for _, s in extras] + [ANY] * n_dep,
        out_specs=[s for _, s in outs],
        out_shape=[s for s, _ in outs],
        scratch_shapes=[pltpu.VMEM(acc_shape, F32)] if nk > 1 else [],
        compiler_params=_cparams(("parallel", "parallel", "arbitrary")),
    )(a, b, *[x for x, _ in extras], *deps)


def _mm_rows(name, tm, a, w, mode, outs, extras=(), epi=None, deps=()):
    n_sh, rows, cols = w.shape
    n_ex, n_out, n_dep = len(extras), len(outs), len(deps)
    by_cols = mode in ('nn_cols', 'nt_rows')
    width = cols if mode == 'nn_cols' else rows

    def body(*refs):
        a_ref, w_ref = refs[:2]
        ex_refs = refs[2:2 + n_ex]
        out_refs = refs[2 + n_ex + n_dep:2 + n_ex + n_dep + n_out]
        if by_cols:
            av = _bf(a_ref[...])
            for s in range(n_sh):
                cs = slice(s * width, (s + 1) * width)
                acc = _dot(av, w_ref[s], 1, 0 if mode == 'nn_cols' else 1)
                res = epi(acc, *[r[:, cs] for r in ex_refs]) if epi is not None else (acc,)
                for o, r in zip(out_refs, res):
                    o[:, cs] = r.astype(o.dtype)
        else:
            chunk = rows if mode == 'nn_rows' else cols
            acc = None
            for s in range(n_sh):
                part = _dot(_bf(a_ref[:, s * chunk:(s + 1) * chunk]), w_ref[s], 1, 0 if mode == 'nn_rows' else 1)
                acc = part if acc is None else acc + part
            res = epi(acc, *[r[...] for r in ex_refs]) if epi is not None else (acc,)
            for o, r in zip(out_refs, res):
                o[...] = r.astype(o.dtype)

    t, ka = a.shape
    return pl.pallas_call(
        body, name=name, grid=(t // tm, 1, 1),
        in_specs=[pl.BlockSpec((tm, ka), lambda i, j, k: (i, 0)),
                  pl.BlockSpec((n_sh, rows, cols), lambda i, j, k: (0, 0, 0))] + [s for _, s in extras] + [ANY] * n_dep,
        out_specs=[s for _, s in outs],
        out_shape=[s for s, _ in outs],
        compiler_params=_cparams(("parallel", "arbitrary", "arbitrary")),
    )(a, w, *[x for x, _ in extras], *deps)


def _sds(shape, dtype):
    return jax.ShapeDtypeStruct(shape, dtype)


def _row_tile(t, cap=ROW_TILE):
    return min(cap, t)


def _rms_fwd(name, x, g, deps=()):
    t, d = x.shape
    tm = _row_tile(t)

    def body(x_ref, g_ref, *rest):
        o_ref = rest[-1]
        xv = x_ref[...]
        r = lax.rsqrt(jnp.mean(xv * xv, axis=-1, keepdims=True) + EPS)
        o_ref[...] = (xv * r * g_ref[...]).astype(o_ref.dtype)

    return pl.pallas_call(
        body, name=name, grid=(t // tm,),
        in_specs=[pl.BlockSpec((tm, d), lambda i: (i, 0)), pl.BlockSpec((1, d), lambda i: (0, 0))] + [ANY] * len(deps),
        out_specs=pl.BlockSpec((tm, d), lambda i: (i, 0)),
        out_shape=_sds((t, d), BF16),
        compiler_params=_cparams(("parallel",)),
    )(x, g, *deps)


def _rms_bwd_rows(dy, xv, g, n):
    r = lax.rsqrt(jnp.sum(xv * xv, axis=-1, keepdims=True) / n + EPS)
    xh = xv * r
    dxh = dy * g
    dx = r * (dxh - xh * (jnp.sum(dxh * xh, axis=-1, keepdims=True) / n))
    return dx, dy * xh


def _ple_gate_bwd(name, dh, gate, e):
    t, d = dh.shape
    tm = _row_tile(t)

    def body(dh_ref, g_ref, e_ref, de_ref, dz_ref):
        dh_v, gt = dh_ref[...], g_ref[...].astype(F32)
        de_ref[...] = (dh_v * gt).astype(BF16)
        dz_ref[...] = (dh_v * e_ref[...].astype(F32) * (gt * (1.0 - gt))).astype(BF16)

    row = pl.BlockSpec((tm, d), lambda i: (i, 0))
    return pl.pallas_call(
        body, name=name, grid=(t // tm,), in_specs=[row, row, row], out_specs=[row, row],
        out_shape=[_sds((t, d), BF16), _sds((t, d), BF16)],
        compiler_params=_cparams(("parallel",)),
    )(dh, gate, e)


def _rope_half(v, cos, sin):
    half = v.shape[-1] // 2
    v1, v2 = v[:, :half], v[:, half:]
    return jnp.concatenate([v1 * cos - v2 * sin, v2 * cos + v1 * sin], axis=-1)


def _ret_consts():
    lg = jnp.log(1.0 - 2.0 ** (-5.0 - jnp.arange(RET_HEADS, dtype=F32)))
    idx = jnp.arange(RET_BLOCK, dtype=F32)
    chunk = jnp.floor(idx / CHUNK)
    dist = idx[:, None] - idx[None, :]
    same = chunk[:, None] == chunk[None, :]
    seen = jnp.where(same, jnp.abs(dist), jnp.where(chunk[None, :] < chunk[:, None], dist, jnp.inf))
    intra = jnp.exp(lg[:, None, None] * seen)
    qdec = jnp.exp(lg[:, None] * (idx + 1.0))
    kdec = jnp.exp(lg[:, None] * (RET_BLOCK - 1.0 - idx))
    cdec = jnp.exp(lg * RET_BLOCK)
    qdec = jnp.broadcast_to(qdec[:, :, None], (RET_HEADS, RET_BLOCK, RET_DK))
    kdec = jnp.broadcast_to(kdec[:, :, None], (RET_HEADS, RET_BLOCK, RET_DK))
    cdec = jnp.broadcast_to(cdec[:, None, None], (RET_HEADS, 1, RET_DV))
    return intra, qdec, kdec, cdec


def _ret_specs(rb, rev_nb=None):
    blk = (lambda i: i) if rev_nb is None else (lambda i: rev_nb - 1 - i)
    full = lambda shape: pl.BlockSpec(shape, lambda i: (0,) * len(shape))
    return dict(
        proj=pl.BlockSpec((rb, RET_IN), lambda i: (blk(i), 0)),
        tab=pl.BlockSpec((rb, RET_DK // 2), lambda i: (blk(i), 0)),
        vw=pl.BlockSpec((rb, RET_V_W), lambda i: (blk(i), 0)),
        st=pl.BlockSpec((rb // RET_BLOCK, RET_HEADS, RET_DK, RET_DV), lambda i: (blk(i), 0, 0, 0)),
        gn=full((RET_HEADS, 1, RET_DV)),
        intra=full((RET_HEADS, RET_BLOCK, RET_BLOCK)),
        dec=full((RET_HEADS, RET_BLOCK, RET_DK)),
        cdec=full((RET_HEADS, 1, RET_DV)),
    )


def _ret_fwd(proj, cos, sin, gn):
    t = proj.shape[0]
    rb = min(RET_ROWS, t)
    cpb = rb // RET_BLOCK
    intra, qdec, kdec, cdec = _ret_consts()
    sp = _ret_specs(rb)

    def body(proj_ref, cos_ref, sin_ref, gn_ref, intra_ref, qd_ref, kd_ref, cd_ref,
             gated_ref, outp_ref, st_ref, s_ref):
        @pl.when(pl.program_id(0) == 0)
        def _():
            s_ref[...] = jnp.zeros_like(s_ref)

        def chunk(c, carry):
            rows = pl.ds(pl.multiple_of(c * RET_BLOCK, RET_BLOCK), RET_BLOCK)
            cs, sn = cos_ref[rows, :], sin_ref[rows, :]
            for h in range(RET_HEADS):
                q = proj_ref[rows, h * RET_DK:(h + 1) * RET_DK].astype(F32)
                k = proj_ref[rows, RET_QK_W + h * RET_DK:RET_QK_W + (h + 1) * RET_DK].astype(F32)
                v = proj_ref[rows, 2 * RET_QK_W + h * RET_DV:2 * RET_QK_W + (h + 1) * RET_DV]
                g = proj_ref[rows, 2 * RET_QK_W + RET_V_W + h * RET_DV:
                             2 * RET_QK_W + RET_V_W + (h + 1) * RET_DV].astype(F32)
                qr = _rope_half(q, cs, sn)
                kr = _rope_half(k, cs, sn) * (RET_DK ** -0.5)
                qb, kb, vb = qr.astype(BF16), kr.astype(BF16), v
                sc = _dot(qb, kb, 1, 1) * intra_ref[h]
                inner = _dot(sc.astype(BF16), vb, 1, 0)
                s_old = s_ref[h]
                sb = s_old.astype(BF16)
                st_ref[c, h] = sb
                cross = _dot((qr * qd_ref[h]).astype(BF16), sb, 1, 0)
                out = inner + cross
                s_ref[h] = s_old * cd_ref[h] + _dot((kr * kd_ref[h]).astype(BF16), vb, 0, 0)
                r = lax.rsqrt(jnp.mean(out * out, axis=-1, keepdims=True) + EPS)
                y = out * r * gn_ref[h]
                cols = slice(h * RET_DV, (h + 1) * RET_DV)
                gated_ref[rows, cols] = (g * _sigmoid(g) * y).astype(BF16)
                outp_ref[rows, cols] = out
            return carry

        lax.fori_loop(0, cpb, chunk, 0)

    return pl.pallas_call(
        body, name="ret_fwd", grid=(t // rb,),
        in_specs=[sp['proj'], sp['tab'], sp['tab'], sp['gn'], sp['intra'], sp['dec'], sp['dec'], sp['cdec']],
        out_specs=[sp['vw'], sp['vw'], sp['st']],
        out_shape=[_sds((t, RET_V_W), BF16), _sds((t, RET_V_W), F32),
                   _sds((t // RET_BLOCK, RET_HEADS, RET_DK, RET_DV), BF16)],
        scratch_shapes=[pltpu.VMEM((RET_HEADS, RET_DK, RET_DV), F32)],
        compiler_params=_cparams(("arbitrary",)),
    )(proj, cos, sin, gn.reshape(RET_HEADS, 1, RET_DV), intra, qdec, kdec, cdec)


def _ret_gate_bwd_epi(dgt, out, g, gn):
    g = g.astype(F32)
    r = lax.rsqrt(jnp.mean(out * out, axis=-1, keepdims=True) + EPS)
    xh = out * r
    sg = _sigmoid(g)
    dgate = dgt * (xh * gn) * (sg * (1.0 + g * (1.0 - sg)))
    dy = dgt * (g * sg)
    dxh = dy * gn
    dout = r * (dxh - xh * jnp.mean(dxh * xh, axis=-1, keepdims=True))
    return dout, dgate, jnp.sum(dy * xh, axis=0, keepdims=True)


def _ret_bwd(proj, cos, sin, states, dout, dgate, deps=()):
    t = proj.shape[0]
    rb = min(RET_ROWS, t)
    cpb = rb // RET_BLOCK
    nb = t // rb
    intra, qdec, kdec, cdec = _ret_consts()
    sp = _ret_specs(rb, rev_nb=nb)

    def body(proj_ref, cos_ref, sin_ref, intra_ref, qd_ref, kd_ref, cd_ref, st_ref, dout_ref, dgate_ref, *rest):
        dproj_ref, ds_ref = rest[len(deps):]

        @pl.when(pl.program_id(0) == 0)
        def _():
            ds_ref[...] = jnp.zeros_like(ds_ref)

        def chunk(cc, carry):
            c = cpb - 1 - cc
            rows = pl.ds(pl.multiple_of(c * RET_BLOCK, RET_BLOCK), RET_BLOCK)
            cs, sn = cos_ref[rows, :], sin_ref[rows, :]
            for h in range(RET_HEADS):
                q = proj_ref[rows, h * RET_DK:(h + 1) * RET_DK].astype(F32)
                k = proj_ref[rows, RET_QK_W + h * RET_DK:RET_QK_W + (h + 1) * RET_DK].astype(F32)
                v = proj_ref[rows, 2 * RET_QK_W + h * RET_DV:2 * RET_QK_W + (h + 1) * RET_DV]
                cols = slice(h * RET_DV, (h + 1) * RET_DV)
                qr = _rope_half(q, cs, sn)
                kr = _rope_half(k, cs, sn) * (RET_DK ** -0.5)
                qb, kb, vb = qr.astype(BF16), kr.astype(BF16), v
                qdb = (qr * qd_ref[h]).astype(BF16)
                kdb = (kr * kd_ref[h]).astype(BF16)
                doutb = dout_ref[rows, cols]
                itr = intra_ref[h]
                pb = (_dot(qb, kb, 1, 1) * itr).astype(BF16)
                dv = _dot(pb, doutb, 0, 0)
                dsc = (_dot(doutb, vb, 1, 1) * itr).astype(BF16)
                dq = _dot(dsc, kb, 1, 0)
                dk = _dot(dsc, qb, 0, 0)
                dq = dq + _dot(doutb, st_ref[c, h], 1, 1) * qd_ref[h]
                ds_new = ds_ref[h]
                dsb = ds_new.astype(BF16)
                dk = dk + _dot(vb, dsb, 1, 1) * kd_ref[h]
                dv = dv + _dot(kdb, dsb, 1, 0)
                ds_ref[h] = ds_new * cd_ref[h] + _dot(qdb, doutb, 0, 0)
                dproj_ref[rows, h * RET_DK:(h + 1) * RET_DK] = _rope_half(dq, cs, -sn).astype(BF16)
                dproj_ref[rows, RET_QK_W + h * RET_DK:RET_QK_W + (h + 1) * RET_DK] = (
                    _rope_half(dk * (RET_DK ** -0.5), cs, -sn).astype(BF16))
                dproj_ref[rows, 2 * RET_QK_W + h * RET_DV:2 * RET_QK_W + (h + 1) * RET_DV] = dv.astype(BF16)
                dproj_ref[rows, 2 * RET_QK_W + RET_V_W + h * RET_DV:
                          2 * RET_QK_W + RET_V_W + (h + 1) * RET_DV] = dgate_ref[rows, cols]
            return carry

        lax.fori_loop(0, cpb, chunk, 0)

    return pl.pallas_call(
        body, name="ret_bwd", grid=(nb,),
        in_specs=[sp['proj'], sp['tab'], sp['tab'], sp['intra'], sp['dec'], sp['dec'], sp['cdec'],
                  sp['st'], sp['vw'], sp['vw']] + [ANY] * len(deps),
        out_specs=sp['proj'],
        out_shape=_sds((t, RET_IN), BF16),
        scratch_shapes=[pltpu.VMEM((RET_HEADS, RET_DK, RET_DV), F32)],
        compiler_params=_cparams(("arbitrary",)),
    )(proj, cos, sin, intra, qdec, kdec, cdec, states, dout, dgate, *deps)


def _spread_rope(a):
    return jnp.pad(a, [(0, 0)] * (a.ndim - 1) + [(0, MLA_ROPE)])


def _gather_rope(a):
    return a[..., :a.shape[-1] - MLA_ROPE]


def _rope_tables(t, zero):
    pos = jnp.arange(t, dtype=F32)[:, None] + zero
    inv = 1.0 / (ROPE_THETA ** (jnp.arange(0, RET_DK, 2, dtype=F32) / RET_DK))
    ang = pos * inv[None, :]
    return jnp.cos(ang), jnp.sin(ang), _mla_tables(t, pos)


def _mla_tables(t, pos):
    half = MLA_ROPE // 2
    inv = 1.0 / (ROPE_THETA ** (jnp.arange(0, MLA_ROPE, 2, dtype=F32) / MLA_ROPE))
    ang = pos * inv[None, :]
    cos, sin = jnp.cos(ang), jnp.sin(ang)
    z = jnp.zeros((t, half), F32)
    c = jnp.concatenate([cos, cos, z, z], axis=1)
    s1 = jnp.concatenate([-sin, z, z, z], axis=1)
    s2 = jnp.concatenate([z, sin, z, z], axis=1)
    return c, s1, s2


def _rope_tile(r, c, s1, s2):
    return r * c + pltpu.roll(r, 96, 1) * s1 + pltpu.roll(r, 32, 1) * s2


def _mla_mid(proj2, qa, kva):
    t = proj2.shape[0]
    tm = _row_tile(t)

    def body(p_ref, qa_ref, kva_ref, cq_ref, ckv_ref):
        cq = p_ref[:, :MLA_Q_RANK]
        ckv = p_ref[:, MLA_Q_RANK:MLA_Q_RANK + MLA_KV_RANK]
        rq = lax.rsqrt(jnp.mean(cq * cq, axis=-1, keepdims=True) + EPS)
        rkv = lax.rsqrt(jnp.mean(ckv * ckv, axis=-1, keepdims=True) + EPS)
        cq_ref[...] = (cq * rq * qa_ref[...]).astype(BF16)
        ckv_ref[...] = (ckv * rkv * kva_ref[...]).astype(BF16)

    return pl.pallas_call(
        body, name="mla_mid", grid=(t // tm,),
        in_specs=[pl.BlockSpec((tm, MLA_IN_PAD), lambda i: (i, 0)),
                  pl.BlockSpec((1, MLA_Q_RANK), lambda i: (0, 0)),
                  pl.BlockSpec((1, MLA_KV_RANK), lambda i: (0, 0))],
        out_specs=[pl.BlockSpec((tm, MLA_Q_RANK), lambda i: (i, 0)),
                   pl.BlockSpec((tm, MLA_KV_RANK), lambda i: (i, 0))],
        out_shape=[_sds((t, MLA_Q_RANK), BF16), _sds((t, MLA_KV_RANK), BF16)],
        compiler_params=_cparams(("parallel",)),
    )(proj2, qa, kva)


def _mla_mid_bwd(proj2, qa, kva, dcq, dckv, dkr):
    t = proj2.shape[0]
    tm = _row_tile(t)

    def body(p_ref, qa_ref, kva_ref, dcq_ref, dckv_ref, dkr_ref, dp_ref, dqa_ref, dkva_ref):
        @pl.when(pl.program_id(0) == 0)
        def _():
            dqa_ref[...] = jnp.zeros_like(dqa_ref)
            dkva_ref[...] = jnp.zeros_like(dkva_ref)

        dxq, dgq = _rms_bwd_rows(dcq_ref[...], p_ref[:, :MLA_Q_RANK], qa_ref[...], MLA_Q_RANK)
        dxk, dgk = _rms_bwd_rows(dckv_ref[...], p_ref[:, MLA_Q_RANK:MLA_Q_RANK + MLA_KV_RANK], kva_ref[...],
                                 MLA_KV_RANK)
        dp_ref[:, :MLA_Q_RANK] = dxq.astype(BF16)
        dp_ref[:, MLA_Q_RANK:MLA_Q_RANK + MLA_KV_RANK] = dxk.astype(BF16)
        dp_ref[:, MLA_Q_RANK + MLA_KV_RANK:] = dkr_ref[...].astype(BF16)
        dqa_ref[...] += jnp.sum(dgq, axis=0, keepdims=True)
        dkva_ref[...] += jnp.sum(dgk, axis=0, keepdims=True)

    return pl.pallas_call(
        body, name="mla_mid_bwd", grid=(t // tm,),
        in_specs=[pl.BlockSpec((tm, MLA_IN_PAD), lambda i: (i, 0)),
                  pl.BlockSpec((1, MLA_Q_RANK), lambda i: (0, 0)),
                  pl.BlockSpec((1, MLA_KV_RANK), lambda i: (0, 0)),
                  pl.BlockSpec((tm, MLA_Q_RANK), lambda i: (i, 0)),
                  pl.BlockSpec((tm, MLA_KV_RANK), lambda i: (i, 0)),
                  pl.BlockSpec((tm, 128), lambda i: (i, 0))],
        out_specs=[pl.BlockSpec((tm, MLA_IN_PAD), lambda i: (i, 0)),
                   pl.BlockSpec((1, MLA_Q_RANK), lambda i: (0, 0)),
                   pl.BlockSpec((1, MLA_KV_RANK), lambda i: (0, 0))],
        out_shape=[_sds((t, MLA_IN_PAD), BF16), _sds((1, MLA_Q_RANK), F32), _sds((1, MLA_KV_RANK), F32)],
        compiler_params=_cparams(("arbitrary",)),
    )(proj2, qa, kva, dcq, dckv, dkr)


def _mla_prep_specs(t, tm):
    head = lambda w: pl.BlockSpec((None, tm, w), lambda i, h: (h, i, 0))
    return dict(
        head256=head(MLA_HD_PAD), head128=head(MLA_VD),
        cols256=pl.BlockSpec((tm, MLA_HD_PAD), lambda i, h: (i, h)),
        cq=pl.BlockSpec((tm, MLA_Q_RANK), lambda i, h: (i, 0)),
        ckv=pl.BlockSpec((tm, MLA_KV_RANK), lambda i, h: (i, 0)),
        wuq=pl.BlockSpec((None, MLA_Q_RANK, MLA_HD_PAD), lambda i, h: (h, 0, 0)),
        wukv=pl.BlockSpec((None, MLA_KV_RANK, MLA_HD_PAD), lambda i, h: (h, 0, 0)),
        kr=pl.BlockSpec((tm, 128), lambda i, h: (i, (MLA_Q_RANK + MLA_KV_RANK) // 128)),
        gain=pl.BlockSpec((1, MLA_HD_PAD), lambda i, h: (0, 0)),
        tab=pl.BlockSpec((tm, 128), lambda i, h: (i, 0)),
    )


def _mla_prep(cq, ckv, wuq, wukv, proj2, gq, gk, tabs):
    t = cq.shape[0]
    tm = _row_tile(t, PREP_ROWS)
    sp = _mla_prep_specs(t, tm)

    def body(cq_ref, ckv_ref, wuq_ref, wukv_ref, kr_ref, gq_ref, gk_ref, c_ref, s1_ref, s2_ref,
             qh_ref, kh_ref, vh_ref):
        c, s1, s2 = c_ref[...], s1_ref[...], s2_ref[...]

        def norm_rope(xv, gain):
            r = lax.rsqrt(jnp.sum(xv * xv, axis=-1, keepdims=True) / MLA_QKD + EPS)
            y = xv * r * gain
            return jnp.concatenate([y[:, :MLA_NOPE], _rope_tile(y[:, MLA_NOPE:], c, s1, s2)], axis=-1)

        kvv = _dot(ckv_ref[...], wukv_ref[...], 1, 0)
        qh_ref[...] = norm_rope(_dot(cq_ref[...], wuq_ref[...], 1, 0), gq_ref[...]).astype(BF16)
        kf = jnp.concatenate([kvv[:, :MLA_NOPE], kr_ref[...]], axis=-1)
        kh_ref[...] = norm_rope(kf, gk_ref[...]).astype(BF16)
        vh_ref[...] = jnp.concatenate([kvv[:, MLA_NOPE:], jnp.ones((tm, MLA_VD), F32)], axis=-1).astype(BF16)

    return pl.pallas_call(
        body, name="mla_prep", grid=(t // tm, MLA_HEADS),
        in_specs=[sp['cq'], sp['ckv'], sp['wuq'], sp['wukv'], sp['kr'], sp['gain'], sp['gain'],
                  sp['tab'], sp['tab'], sp['tab']],
        out_specs=[sp['head256'], sp['head256'], sp['head256']],
        out_shape=[_sds((MLA_HEADS, t, MLA_HD_PAD), BF16), _sds((MLA_HEADS, t, MLA_HD_PAD), BF16),
                   _sds((MLA_HEADS, t, 2 * MLA_VD), BF16)],
        compiler_params=_cparams(("parallel", "arbitrary")),
    )(cq, ckv, wuq, wukv, proj2, gq, gk, *tabs)


def _mla_prep_bwd(cq, ckv, wuq, wukv, proj2, gq, gk, tabs, dqt, dkh, dvh):
    t = cq.shape[0]
    tm = _row_tile(t, PREP_ROWS)
    ab = dqt.shape[-1]
    sp = _mla_prep_specs(t, tm)

    def body(cq_ref, ckv_ref, wuq_ref, wukv_ref, kr_ref, gq_ref, gk_ref, c_ref, s1_ref, s2_ref,
             dqt_ref, dkh_ref, dvh_ref, dq_ref, dkv_ref, dkr_ref, dgq_ref, dgk_ref):
        dqh = jnp.concatenate([dqt_ref[b].T for b in range(tm // ab)], axis=0)
        i, h = pl.program_id(0), pl.program_id(1)

        @pl.when((i == 0) & (h == 0))
        def _():
            dgq_ref[...] = jnp.zeros_like(dgq_ref)
            dgk_ref[...] = jnp.zeros_like(dgk_ref)

        @pl.when(h == 0)
        def _():
            dkr_ref[...] = jnp.zeros_like(dkr_ref)

        c, s1, s2 = c_ref[...], s1_ref[...], s2_ref[...]

        def back(xv, gain, dout):
            dy = jnp.concatenate([dout[:, :MLA_NOPE], _rope_tile(dout[:, MLA_NOPE:], c, -s1, -s2)], axis=-1)
            return _rms_bwd_rows(dy, xv, gain, MLA_QKD)

        kvv = _dot(ckv_ref[...], wukv_ref[...], 1, 0)
        dxq, dgq = back(_dot(cq_ref[...], wuq_ref[...], 1, 0), gq_ref[...], dqh)
        kf = jnp.concatenate([kvv[:, :MLA_NOPE], kr_ref[...]], axis=-1)
        dxk, dgk = back(kf, gk_ref[...], dkh_ref[...])
        dq_ref[...] = dxq.astype(BF16)
        dkv_ref[...] = jnp.concatenate([dxk[:, :MLA_NOPE], dvh_ref[...]], axis=-1).astype(BF16)
        dkr_ref[...] += dxk[:, MLA_NOPE:]
        dgq_ref[...] += jnp.sum(dgq, axis=0, keepdims=True)
        dgk_ref[...] += jnp.sum(dgk, axis=0, keepdims=True)

    return pl.pallas_call(
        body, name="mla_prep_bwd", grid=(t // tm, MLA_HEADS),
        in_specs=[sp['cq'], sp['ckv'], sp['wuq'], sp['wukv'], sp['kr'], sp['gain'], sp['gain'],
                  sp['tab'], sp['tab'], sp['tab'],
                  pl.BlockSpec((None, tm // ab, MLA_HD_PAD, ab), lambda i, h: (h, i, 0, 0)),
                  sp['head256'], sp['head128']],
        out_specs=[sp['cols256'], sp['cols256'], sp['tab'], sp['gain'], sp['gain']],
        out_shape=[_sds((t, MLA_HEADS * MLA_HD_PAD), BF16), _sds((t, MLA_HEADS * MLA_HD_PAD), BF16),
                   _sds((t, 128), F32), _sds((1, MLA_HD_PAD), F32), _sds((1, MLA_HD_PAD), F32)],
        compiler_params=_cparams(("arbitrary", "arbitrary")),
    )(cq, ckv, wuq, wukv, proj2, gq, gk, *tabs, dqt, dkh, dvh)


def _chunk_visible(rows, cols, row_off, col_off):
    rq = lax.shift_right_logical(lax.broadcasted_iota(jnp.int32, (rows, cols), 0) + row_off, 6)
    ck = lax.shift_right_logical(lax.broadcasted_iota(jnp.int32, (rows, cols), 1) + col_off, 6)
    return ck <= rq


def _rows_to_lanes(col):
    return col.T[:8, :]


def _attn_fwd(qh, kh, vh):
    t = qh.shape[1]
    ab = min(ATT_BLOCK, t)
    tq = min(ATT_QROWS, t)
    r = tq // ab
    hg = ATT_HEADS

    def body(q_ref, k_ref, v_ref, o_ref, lse_ref, acc_ref):
        n_un = pl.program_id(1) * r
        acc_ref[...] = jnp.zeros_like(acc_ref)

        def step(b, ms, diag):
            rows = pl.ds(pl.multiple_of(b * ab, ab), ab)
            out = []
            for hh in range(hg):
                m = ms[hh]
                s = _dot(q_ref[hh], k_ref[hh, rows, :], 1, 1)
                if diag is not None:
                    s = jnp.where(_chunk_visible(tq, ab, 0, diag * ab), s, -1e30)
                m_new = jnp.maximum(m, jnp.max(s, axis=-1, keepdims=True))
                p = jnp.exp2((s - m_new) * ATT_EXP2).astype(BF16)
                acc_ref[hh] = jnp.exp2((m - m_new) * ATT_EXP2) * acc_ref[hh] + _dot(p, v_ref[hh, rows, :], 1, 0)
                out.append(m_new)
            return tuple(out)

        ms = tuple(jnp.full((tq, 1), -1e30, F32) for _ in range(hg))
        ms = lax.fori_loop(0, n_un, lambda b, st: step(b, st, None), ms)
        for d in range(r):
            ms = step(n_un + d, ms, d)
        for hh in range(hg):
            l = acc_ref[hh, :, MLA_VD:]
            o_ref[:, hh * MLA_VD:(hh + 1) * MLA_VD] = acc_ref[hh, :, :MLA_VD] / l
            lse_t = _rows_to_lanes(ms[hh] * ATT_EXP2 + jnp.log(l) * LOG2E)
            for d in range(r):
                lse_ref[hh, d] = lse_t[:, d * ab:(d + 1) * ab]

    return pl.pallas_call(
        body, name="mla_attn", grid=(MLA_HEADS // hg, t // tq),
        in_specs=[pl.BlockSpec((hg, tq, MLA_HD_PAD), lambda g, i: (g, i, 0)),
                  pl.BlockSpec((hg, t, MLA_HD_PAD), lambda g, i: (g, 0, 0)),
                  pl.BlockSpec((hg, t, 2 * MLA_VD), lambda g, i: (g, 0, 0))],
        out_specs=[pl.BlockSpec((tq, hg * MLA_VD), lambda g, i: (i, g)),
                   pl.BlockSpec((hg, r, 8, ab), lambda g, i: (g, i, 0, 0))],
        out_shape=[_sds((t, MLA_HEADS * MLA_VD), F32), _sds((MLA_HEADS, t // ab, 8, ab), F32)],
        scratch_shapes=[pltpu.VMEM((hg, tq, 2 * MLA_VD), F32)],
        compiler_params=_cparams(("parallel", "arbitrary")),
    )(qh, kh, vh)


def _attn_bwd(qh, kh, vh, dob, o, lse_t):
    t = qh.shape[1]
    ab = min(ATT_BLOCK, t)
    kb = min(ATT_KROWS, t)
    r = kb // ab
    nq = t // ab
    hg = ATT_HEADS

    def body(q_ref, k_ref, v_ref, do_ref, o_ref, lse_ref, dqt_ref, dk_ref, dv_ref, dl_ref):
        j = pl.program_id(1)

        @pl.when(j == 0)
        def _():
            dqt_ref[...] = jnp.zeros_like(dqt_ref)
            ones = jnp.ones((8, MLA_VD), F32)

            def delta(b, carry):
                rows = pl.ds(pl.multiple_of(b * ab, ab), ab)
                for hh in range(hg):
                    cols = slice(hh * MLA_VD, (hh + 1) * MLA_VD)
                    prod = do_ref[rows, cols].astype(F32) * o_ref[rows, cols]
                    dl_ref[hh, b] = lax.dot_general(ones, prod, (((1,), (1,)), ((), ())),
                                                    precision=lax.Precision.HIGHEST, preferred_element_type=F32)
                return carry

            lax.fori_loop(0, nq, delta, 0)

        ks = [k_ref[hh] for hh in range(hg)]
        vs = [v_ref[hh, :, :MLA_VD] for hh in range(hg)]
        kts = [k.T for k in ks]

        dk_ref[...] = jnp.zeros_like(dk_ref)
        dv_ref[...] = jnp.zeros_like(dv_ref)

        def step(b, carry, diag):
            rows = pl.ds(pl.multiple_of(b * ab, ab), ab)
            hi = kb if diag is None else (diag + 1) * ab
            for hh in range(hg):
                q = q_ref[hh, rows, :]
                do = do_ref[rows, hh * MLA_VD:(hh + 1) * MLA_VD]
                s_t = _dot(ks[hh][:hi], q, 1, 1)
                if diag is not None:
                    key_chunk = lax.shift_right_logical(lax.broadcasted_iota(jnp.int32, (hi, ab), 0), 6)
                    query_chunk = lax.shift_right_logical(
                        lax.broadcasted_iota(jnp.int32, (hi, ab), 1) + diag * ab, 6)
                    s_t = jnp.where(key_chunk <= query_chunk, s_t, -1e30)
                p_t = jnp.exp2(s_t * ATT_EXP2 - lse_ref[hh, b][0:1, :])
                dp_t = _dot(vs[hh][:hi], do, 1, 1)
                ds_t = (p_t * (dp_t - dl_ref[hh, b][0:1, :]) * ATT_SCALE).astype(BF16)
                dqt_ref[hh, b] += _dot(kts[hh][:, :hi], ds_t, 1, 0)
                dk_ref[hh, :hi] += _dot(ds_t, q, 1, 0)
                dv_ref[hh, :hi] += _dot(p_t.astype(BF16), do, 1, 0)
            return carry

        for d in range(r):
            step(j * r + d, 0, d)
        lax.fori_loop((j + 1) * r, nq, lambda b, c: step(b, c, None), 0)

    whole = lambda w: pl.BlockSpec((hg, t, w), lambda g, j: (g, 0, 0))
    blk = lambda w: pl.BlockSpec((hg, kb, w), lambda g, j: (g, j, 0))
    stat = pl.BlockSpec((hg, nq, 8, ab), lambda g, j: (g, 0, 0, 0))
    cols = pl.BlockSpec((t, hg * MLA_VD), lambda g, j: (0, g))
    return pl.pallas_call(
        body, name="mla_attn_bwd", grid=(MLA_HEADS // hg, t // kb),
        in_specs=[whole(MLA_HD_PAD), blk(MLA_HD_PAD), blk(2 * MLA_VD),
                  cols, cols, stat],
        out_specs=[pl.BlockSpec((hg, nq, MLA_HD_PAD, ab), lambda g, j: (g, 0, 0, 0)), blk(MLA_HD_PAD), blk(MLA_VD)],
        out_shape=[_sds((MLA_HEADS, nq, MLA_HD_PAD, ab), F32), _sds((MLA_HEADS, t, MLA_HD_PAD), F32),
                   _sds((MLA_HEADS, t, MLA_VD), F32)],
        scratch_shapes=[pltpu.VMEM((hg, nq, 8, ab), F32)],
        compiler_params=_cparams(("parallel", "arbitrary")),
    )(qh, kh, vh, dob, o, lse_t)


VEC = pl.BlockSpec((1, D_MODEL), lambda i, j, k: (0, 0))


def _rows(tm, width):
    return pl.BlockSpec((tm, width), lambda i, j, k: (i, 0))


def _residual_epi(next_gain):
    if next_gain is None:
        return [], lambda acc, hv: (acc + hv,)

    def epi(acc, hv, g):
        h_new = acc + hv
        r = lax.rsqrt(jnp.mean(h_new * h_new, axis=-1, keepdims=True) + EPS)
        return h_new, h_new * r * g

    return [(next_gain, VEC)], epi


def _residual_outs(t, row, next_gain):
    outs = [(_sds((t, D_MODEL), F32), row)]
    return outs + ([(_sds((t, D_MODEL), BF16), row)] if next_gain is not None else [])


def _mlp_fwd(l, h, hn, w1g, fetch_w2, next_gain):
    t = h.shape[0]
    tm = _row_tile(t, 512)

    def relu2(acc):
        r = jnp.maximum(acc, 0.0)
        return (r * r,)

    (u,) = _mm_rows(f"mlp_up{l}", tm, hn, w1g, 'nn_cols', [(_sds((t, D_FF), BF16), _rows(tm, D_FF))], epi=relu2)
    w2g = fetch_w2((u,))
    row = _rows(tm, D_MODEL)
    more, epi = _residual_epi(next_gain)
    h2, hn_next = _mm_rows(f"mlp_down{l}", tm, u, w2g, 'nn_rows', _residual_outs(t, row, next_gain),
                           extras=[(h, row)] + more, epi=epi)
    return h2, hn_next, (h, hn, u, w1g, w2g)


def _norm_bwd_outs(t, tm):
    return [(_sds((t, D_MODEL), F32), pl.BlockSpec((tm, D_MODEL), lambda i, j, k: (i, 0))),
            (_sds((t // tm, 1, D_MODEL), F32), pl.BlockSpec((None, 1, D_MODEL), lambda i, j, k: (i, 0, 0)))]


def _norm_bwd_epi(acc, xv, res, g):
    dx, dgr = _rms_bwd_rows(acc, xv, g, D_MODEL)
    return res + dx, jnp.sum(dgr, axis=0, keepdims=True)


def _mlp_bwd(l, dh, saved, norm_g, emit_w2=None, emit_w1=None):
    h, hn, u, w1g, w2g = saved
    t = h.shape[0]
    tm = _row_tile(t, 512)
    nsh, _, wsh = w1g.shape
    wide = _rows(tm, D_FF)
    (da,) = _mm_rows(f"mlp_du{l}", tm, dh, w2g, 'nt_rows', [(_sds((t, D_FF), BF16), wide)], extras=[(u, wide)],
                     epi=lambda acc, uv: (2.0 * jnp.sqrt(uv.astype(F32)) * acc,))
    tw = _row_tile(t, 512)
    (dw2,) = _mm(f"mlp_dw2{l}", (1, 1, t // tw),
                 u, pl.BlockSpec((tw, D_FF), lambda i, j, k: (k, 0)),
                 dh, pl.BlockSpec((tw, D_MODEL), lambda i, j, k: (k, 0)), (0, 0),
                 [(_sds((D_FF, D_MODEL), BF16), pl.BlockSpec((D_FF, D_MODEL), lambda i, j, k: (0, 0)))])
    dw2 = dw2.reshape(nsh, wsh, D_MODEL)
    (dw1,) = _mm(f"mlp_dw1{l}", (1, 1, t // tw),
                 hn, pl.BlockSpec((tw, D_MODEL), lambda i, j, k: (k, 0)),
                 da, pl.BlockSpec((tw, D_FF), lambda i, j, k: (k, 0)), (0, 0),
                 [(_sds((nsh, D_MODEL, wsh), BF16), pl.BlockSpec((nsh, D_MODEL, wsh), lambda i, j, k: (0, 0, 0)))],
                 split=wsh, deps=emit_w2(dw2) if emit_w2 else ())
    row = _rows(tm, D_MODEL)
    dh_in, dg = _mm_rows(f"mlp_dhn{l}", tm, da, w1g, 'nt_cols', _norm_bwd_outs(t, tm),
                         extras=[(h, row), (dh, row), (norm_g, VEC)], epi=_norm_bwd_epi,
                         deps=emit_w1(dw1) if emit_w1 else ())
    return dh_in, jnp.sum(dg, axis=0), dw1, dw2


def _ple_fwd(l, h, hn, p, wg, wp, next_gain, target=None):
    t = h.shape[0]
    tm = _row_tile(t, 512)
    row = pl.BlockSpec((tm, D_MODEL), lambda i, j, k: (i, 0))
    full = lambda r: pl.BlockSpec((r, D_MODEL), lambda i, j, k: (0, 0))
    f32_row, bf_row = (_sds((t, D_MODEL), F32), row), (_sds((t, D_MODEL), BF16), row)
    common = [(h, row), (p, pl.BlockSpec((None, None, tm, PLE_DIM), lambda i, j, k: (l, 0, i, 0))),
              (wp, full(PLE_DIM))]
    if target is not None:
        def loss_epi(acc, hv, pv, wpv, tv):
            gt = _sigmoid(acc)
            ev = _dot(_bf(pv), wpv, 1, 0)
            err = hv + gt * ev - tv
            sq = jnp.sum(jnp.sum(err * err, axis=-1, keepdims=True), axis=0, keepdims=True)
            return err / D_MODEL, gt, ev, jnp.broadcast_to(sq, (8, 128))

        dy, gate, e, sq = _mm(f"ple_gate{l}", (t // tm, 1, 1), hn, row, wg, full(D_MODEL), (1, 0),
                              [f32_row, bf_row, bf_row, (_sds((t // tm, 8, 128), F32),
                                                         pl.BlockSpec((None, 8, 128), lambda i, j, k: (i, 0, 0)))],
                              extras=common + [(target, row)], epi=loss_epi)
        return dy, jnp.sum(sq, axis=0), (h, hn, gate, e)

    def gate_epi(acc, hv, pv, wpv, *gain):
        gt = _sigmoid(acc)
        ev = _dot(_bf(pv), wpv, 1, 0)
        h_new = hv + gt * ev
        if not gain:
            return h_new, gt, ev
        r = lax.rsqrt(jnp.mean(h_new * h_new, axis=-1, keepdims=True) + EPS)
        return h_new, gt, ev, h_new * r * gain[0]

    res = _mm(f"ple_gate{l}", (t // tm, 1, 1), hn, row, wg, full(D_MODEL), (1, 0),
              [f32_row, bf_row, bf_row] + ([bf_row] if next_gain is not None else []),
              extras=common + ([(next_gain, VEC)] if next_gain is not None else []), epi=gate_epi)
    h_out, gate, e = res[0], res[1], res[2]
    return h_out, (res[3] if next_gain is not None else None), (h, hn, gate, e)


def _ple_bwd(l, dh, saved, p, norm_g, wg, deps=(), emit=None):
    h, hn, gate, e = saved
    t = h.shape[0]
    tm = _row_tile(t)
    tk = _row_tile(t, 512)
    de, dz = _ple_gate_bwd(f"ple_gate_bwd{l}", dh, gate, e)
    full = lambda r: pl.BlockSpec((r, D_MODEL), lambda i, j, k: (0, 0))
    rowk = pl.BlockSpec((tk, D_MODEL), lambda i, j, k: (k, 0))
    (dwp,) = _mm(f"ple_dwp{l}", (1, 1, t // tk),
                 p, pl.BlockSpec((None, None, tk, PLE_DIM), lambda i, j, k: (l, 0, k, 0)),
                 de, rowk, (0, 0), [(_sds((PLE_DIM, D_MODEL), BF16), full(PLE_DIM))], deps=deps)
    (dwg,) = _mm(f"ple_dwg{l}", (1, 1, t // tk), hn, rowk, dz, rowk, (0, 0),
                 [(_sds((D_MODEL, D_MODEL), BF16), full(D_MODEL))])
    row = pl.BlockSpec((tm, D_MODEL), lambda i, j, k: (i, 0))
    dh_in, dg = _mm(f"ple_dhn{l}", (t // tm, 1, 1), dz, row, wg, full(D_MODEL), (1, 1),
                    _norm_bwd_outs(t, tm), extras=[(h, row), (dh, row), (norm_g, VEC)], epi=_norm_bwd_epi,
                    deps=emit(dwg, dwp) if emit else ())
    return dh_in, jnp.sum(dg, axis=0), dwg, dwp


def _ret_layer_fwd(x, norm_g, wri, fetch_wro, gn, cos, sin, next_gain, hn=None, deps=()):
    t = x.shape[0]
    tm = _row_tile(t)
    nsh, _, wsh = wri.shape
    if hn is None:
        hn = _rms_fwd("mix_norm0", x, norm_g)
    tp = _row_tile(t, 512)
    (proj,) = _mm_rows("ret_in", tp, hn, wri, 'nn_cols', [(_sds((t, RET_IN), BF16), _rows(tp, RET_IN))], deps=deps)
    gated, outp, states = _ret_fwd(proj, cos, sin, gn)
    wro = fetch_wro((gated,))
    row = _rows(tp, D_MODEL)
    more, epi = _residual_epi(next_gain)
    h1, hn_next = _mm_rows("ret_out", tp, gated, wro.reshape(RET_HEADS, RET_DV, D_MODEL), 'nn_rows',
                           _residual_outs(t, row, next_gain), extras=[(x, row)] + more, epi=epi)
    return h1, hn_next, (x, hn, proj, gated, outp, states, wro)


def _ret_layer_bwd(dh, saved, norm_g, wri, gn, cos, sin, emit_out, emit_in, deps=()):
    x, hn, proj, gated, outp, states, wro = saved
    t = x.shape[0]
    tm = _row_tile(t)
    tk = _row_tile(t, 512)
    nsh, _, wsh = wri.shape
    tg = _row_tile(t, 512)
    vw = _rows(tg, RET_V_W)
    dout, dgate, dgn = _mm_rows(
        "ret_dgate", tg, dh, wro.reshape(RET_HEADS, RET_DV, D_MODEL), 'nt_rows',
        [(_sds((t, RET_V_W), BF16), vw), (_sds((t, RET_V_W), BF16), vw),
         (_sds((t // tg, 1, RET_V_W), F32), pl.BlockSpec((None, 1, RET_V_W), lambda i, j, k: (i, 0, 0)))],
        extras=[(outp, vw), (proj, pl.BlockSpec((tg, RET_V_W), lambda i, j, k: (i, (RET_IN - RET_V_W) // RET_V_W))),
                (gn.reshape(1, RET_V_W), pl.BlockSpec((1, RET_V_W), lambda i, j, k: (0, 0)))],
        epi=_ret_gate_bwd_epi, deps=deps)
    dgn = jnp.sum(dgn, axis=0)
    (dwro,) = _mm("ret_dwro", (1, 1, t // tk),
                  gated, pl.BlockSpec((tk, RET_V_W), lambda i, j, k: (k, 0)),
                  dh, pl.BlockSpec((tk, D_MODEL), lambda i, j, k: (k, 0)), (0, 0),
                  [(_sds((RET_V_W, D_MODEL), BF16), pl.BlockSpec((RET_V_W, D_MODEL), lambda i, j, k: (0, 0)))])
    dproj = _ret_bwd(proj, cos, sin, states, dout, dgate, deps=emit_out(dwro))
    half = nsh // 2
    (dwri,) = _mm("ret_dwri", (2, 1, t // tk),
                  hn, pl.BlockSpec((tk, D_MODEL), lambda i, j, k: (k, 0)),
                  dproj, pl.BlockSpec((tk, half * wsh), lambda i, j, k: (k, i)), (0, 0),
                  [(_sds((nsh, D_MODEL, wsh), BF16), pl.BlockSpec((half, D_MODEL, wsh), lambda i, j, k: (i, 0, 0)))],
                  split=wsh)
    deps = emit_in(dwri)
    td = _row_tile(t, 256)
    row = _rows(td, D_MODEL)
    dx, dg = _mm_rows("ret_dhn", td, dproj, wri, 'nt_cols', _norm_bwd_outs(t, td),
                      extras=[(x, row), (dh, row), (norm_g, VEC)], epi=_norm_bwd_epi, deps=deps)
    return dx, jnp.sum(dg, axis=0), dgn.reshape(RET_HEADS, RET_DV)


def _mla_layer_fwd(h, hn, wmi, qa, kva, wuq, wukv, gq, gk, wmo, tabs, next_gain):
    t = h.shape[0]
    tm = _row_tile(t)
    row = pl.BlockSpec((tm, D_MODEL), lambda i, j, k: (i, 0))
    (proj2,) = _mm("mla_in", (t // tm, 1, 1), hn, row,
                   wmi, pl.BlockSpec((D_MODEL, MLA_IN_PAD), lambda i, j, k: (0, 0)), (1, 0),
                   [(_sds((t, MLA_IN_PAD), F32), pl.BlockSpec((tm, MLA_IN_PAD), lambda i, j, k: (i, 0)))])
    cq, ckv = _mla_mid(proj2, qa, kva)
    qh, kh, vh = _mla_prep(cq, ckv, wuq, wukv, proj2, gq, gk, tabs)
    o, lse = _attn_fwd(qh, kh, vh)
    more, epi = _residual_epi(next_gain)
    h_out, hn_next = _mm("mla_out", (t // tm, 1, 1), o, row,
                         wmo, pl.BlockSpec((D_MODEL, D_MODEL), lambda i, j, k: (0, 0)), (1, 0),
                         _residual_outs(t, row, next_gain), extras=[(h, row)] + more, epi=epi)
    return h_out, hn_next, (h, hn, proj2, cq, ckv, qh, kh, vh, o, lse)


def _mla_layer_bwd(dh, saved, norm_g, wmi, qa, kva, wuq, wukv, gq, gk, wmo, tabs, deps=()):
    h, hn, proj2, cq, ckv, qh, kh, vh, o, lse = saved
    t = h.shape[0]
    tm = _row_tile(t)
    tk = _row_tile(t, 512)
    row = pl.BlockSpec((tm, D_MODEL), lambda i, j, k: (i, 0))
    rowk = pl.BlockSpec((tk, D_MODEL), lambda i, j, k: (k, 0))
    sq = pl.BlockSpec((D_MODEL, D_MODEL), lambda i, j, k: (0, 0))
    (dob,) = _mm("mla_do", (t // tm, 1, 1), dh, row, wmo, sq, (1, 1), [(_sds((t, D_MODEL), BF16), row)], deps=deps)
    (dwmo,) = _mm("mla_dwo", (1, 1, t // tk), o, rowk, dh, rowk, (0, 0), [(_sds((D_MODEL, D_MODEL), BF16), sq)])
    dqt, dkh, dvh = _attn_bwd(qh, kh, vh, dob, o, lse)
    dq, dkv, dkr, dgq, dgk = _mla_prep_bwd(cq, ckv, wuq, wukv, proj2, gq, gk, tabs, dqt, dkh, dvh)

    wide = MLA_HEADS * MLA_HD_PAD
    widek = pl.BlockSpec((tk, wide), lambda i, j, k: (k, 0))
    (dwuq,) = _mm("mla_dwuq", (1, 1, t // tk),
                  cq, pl.BlockSpec((tk, MLA_Q_RANK), lambda i, j, k: (k, 0)), dq, widek, (0, 0),
                  [(_sds((MLA_HEADS, MLA_Q_RANK, MLA_HD_PAD), BF16),
                    pl.BlockSpec((MLA_HEADS, MLA_Q_RANK, MLA_HD_PAD), lambda i, j, k: (0, 0, 0)))], split=MLA_HD_PAD)
    (dwukv,) = _mm("mla_dwukv", (1, 1, t // tk),
                   ckv, pl.BlockSpec((tk, MLA_KV_RANK), lambda i, j, k: (k, 0)), dkv, widek, (0, 0),
                   [(_sds((MLA_HEADS, MLA_KV_RANK, MLA_HD_PAD), BF16),
                     pl.BlockSpec((MLA_HEADS, MLA_KV_RANK, MLA_HD_PAD), lambda i, j, k: (0, 0, 0)))],
                   split=MLA_HD_PAD)
    side_by_side = lambda wg: wg.transpose(1, 0, 2).reshape(wg.shape[1], wide)
    widei = pl.BlockSpec((tm, wide), lambda i, j, k: (i, 0))
    (dcq,) = _mm("mla_dcq", (t // tm, 1, 1), dq, widei,
                 side_by_side(wuq), pl.BlockSpec((MLA_Q_RANK, wide), lambda i, j, k: (0, 0)), (1, 1),
                 [(_sds((t, MLA_Q_RANK), F32), pl.BlockSpec((tm, MLA_Q_RANK), lambda i, j, k: (i, 0)))])
    (dckv,) = _mm("mla_dckv", (t // tm, 1, 1), dkv, widei,
                  side_by_side(wukv), pl.BlockSpec((MLA_KV_RANK, wide), lambda i, j, k: (0, 0)), (1, 1),
                  [(_sds((t, MLA_KV_RANK), F32), pl.BlockSpec((tm, MLA_KV_RANK), lambda i, j, k: (i, 0)))])
    dproj2, dqa, dkva = _mla_mid_bwd(proj2, qa, kva, dcq, dckv, dkr)
    win = pl.BlockSpec((D_MODEL, MLA_IN_PAD), lambda i, j, k: (0, 0))
    (dwmi,) = _mm("mla_dwin", (1, 1, t // tk), hn, rowk,
                  dproj2, pl.BlockSpec((tk, MLA_IN_PAD), lambda i, j, k: (k, 0)), (0, 0),
                  [(_sds((D_MODEL, MLA_IN_PAD), BF16), win)])
    dh_in, dg = _mm("mla_dhn", (t // tm, 1, 1),
                    dproj2, pl.BlockSpec((tm, MLA_IN_PAD), lambda i, j, k: (i, 0)), wmi, win, (1, 1),
                    _norm_bwd_outs(t, tm), extras=[(h, row), (dh, row), (norm_g, VEC)], epi=_norm_bwd_epi)
    return dh_in, dict(mix=jnp.sum(dg, axis=0), wmi=dwmi, qa=dqa, kva=dkva, wuq=dwuq, wukv=dwukv, gq=dgq, gk=dgk,
                       wmo=dwmo)


def _local_step(x, p, target, w, fetch, emit=lambda group: ()):
    t = x.shape[0]
    cos_r, sin_r, tabs = w['tables'] if 'tables' in w else _rope_tables(t, 0.0)
    row = lambda a, i: a[i:i + 1]

    h1, hn1, s_ret = _ret_layer_fwd(x, row(w['mix_norm'], 0), w['ret_w_in'],
                                    lambda after: fetch('ret_out', after)['ret_w_out'], w['ret_gn'], cos_r, sin_r,
                                    row(w['mlp_norm'], 0), hn=w.get('hn0'), deps=w['deps'])
    h2, hn2, s_mlp0 = _mlp_fwd(0, h1, hn1, fetch('mlp_w1_0', (h1,))['mlp_w1'],
                               lambda after: fetch('mlp_w2_0', after)['mlp_w2'], row(w['ple_norm'], 0))
    w0 = fetch('ple_0', (h2,))
    h3, hn3, s_ple0 = _ple_fwd(0, h2, hn2, p, w0['ple_gate_w'], w0['ple_proj_w'], row(w['mix_norm'], 1))
    wm = fetch('mla', (h3,))
    mla_w = (wm['mla_w_in'], w['mla_q_a_norm'], w['mla_kv_a_norm'], wm['mla_w_uq'], wm['mla_w_ukv'],
             w['mla_q_norm'], w['mla_k_norm'], wm['mla_w_out'], tabs)
    h4, hn4, s_mla = _mla_layer_fwd(h3, hn3, *mla_w, row(w['mlp_norm'], 1))
    w1 = fetch('layer_1', (h4,))
    h5, hn5, s_mlp1 = _mlp_fwd(1, h4, hn4, w1['mlp_w1'], lambda after: w1['mlp_w2'], row(w['ple_norm'], 1))
    dy, sq_err, s_ple1 = _ple_fwd(1, h5, hn5, p, w1['ple_gate_w'], w1['ple_proj_w'], None, target)

    n = N_DEV
    colsh = lambda a: a.reshape(a.shape[0], n, a.shape[1] // n).transpose(1, 0, 2)
    rowsh = lambda a: a.reshape(n, a.shape[0] // n, a.shape[1])
    big = {}

    def emit_group(group):
        big.update(group)
        return emit(group)

    dh5, dg_ple1, dwg1, dwp1 = _ple_bwd(1, dy, s_ple1, p, row(w['ple_norm'], 1), w1['ple_gate_w'])
    dh4, dg_mlp1, dw1_1, dw2_1 = _mlp_bwd(1, dh5, s_mlp1, row(w['mlp_norm'], 1))
    deps = emit_group({('ple_gate_w', 1): rowsh(dwg1), ('ple_proj_w', 1): colsh(dwp1),
                       ('mlp_w2', 1): dw2_1, ('mlp_w1', 1): dw1_1})
    dh3, gm = _mla_layer_bwd(dh4, s_mla, row(w['mix_norm'], 1), *mla_w, deps=deps)
    deps = emit_group({('mla_w_out', 0): rowsh(gm['wmo']), ('mla_w_uq', 0): _gather_rope(gm['wuq']),
                       ('mla_w_ukv', 0): gm['wukv'], ('mla_w_in', 0): rowsh(_gather_rope(gm['wmi']))})
    dh2, dg_ple0, _, _ = _ple_bwd(
        0, dh3, s_ple0, p, row(w['ple_norm'], 0), w0['ple_gate_w'], deps=deps,
        emit=lambda dwg, dwp: emit_group({('ple_gate_w', 0): rowsh(dwg), ('ple_proj_w', 0): colsh(dwp)}))
    dh1, dg_mlp0, _, _ = _mlp_bwd(0, dh2, s_mlp0, row(w['mlp_norm'], 0),
                                  emit_w2=lambda dw2: emit_group({('mlp_w2', 0): dw2}),
                                  emit_w1=lambda dw1: emit_group({('mlp_w1', 0): dw1}))
    dx, dg_mix0, dgn = _ret_layer_bwd(
        dh1, s_ret, row(w['mix_norm'], 0), w['ret_w_in'], w['ret_gn'], cos_r, sin_r,
        lambda dwro: emit_group({('ret_w_out', 0): rowsh(dwro)}),
        lambda dwri: emit_group({('ret_w_in', 0): dwri}))

    small = dict(
        mix_norm=[dg_mix0, gm['mix']], mlp_norm=[dg_mlp0, dg_mlp1], ple_norm=[dg_ple0, dg_ple1],
        ret_gn=dgn, mla_q_a_norm=gm['qa'], mla_kv_a_norm=gm['kva'], mla_q_norm=gm['gq'], mla_k_norm=gm['gk'],
    )
    return sq_err, dx, big, small


def _my_place():
    x, y, c = lax.axis_index("x"), lax.axis_index("y"), lax.axis_index("c")
    return x, y, c


def _flat(px, py, pc):
    return 4 * px + 2 * py + pc


def _peer(x, y, c, r):
    return (1 - x if r & 4 else x, 1 - y if r & 2 else y, 1 - c if r & 1 else c)


HBM = pl.BlockSpec(memory_space=pltpu.HBM)
SEMS = pl.BlockSpec(memory_space=pltpu.SEMAPHORE)
SIDE_EFFECT = pltpu.SideEffectType.DATAFLOW_SIDE_EFFECTING


def _rs_copies(x, y, c, srcs, lands, send_sems, recv_sems):
    copies = []
    for a in range(len(srcs)):
        for r in range(1, N_DEV):
            peer = _peer(x, y, c, r)
            k = a * (N_DEV - 1) + r - 1
            copies.append(pltpu.make_async_remote_copy(
                src_ref=srcs[a].at[_flat(*peer)], dst_ref=lands[a].at[r - 1],
                send_sem=send_sems.at[k], recv_sem=recv_sems.at[k], device_id=peer, device_id_type=MESH))
    return copies


def _rs_start(name, arrays):
    n = len(arrays)
    hbm = lambda a: pltpu.with_memory_space_constraint(a, pltpu.HBM)
    lands = [hbm(lax.empty((N_DEV - 1,) + a.shape[1:], a.dtype)) for a in arrays]

    def body(*refs):
        srcs, lnd = refs[:n], refs[n:2 * n]
        send_sems, recv_sems = refs[2 * n], refs[2 * n + 1]
        token = refs[-1]
        for cp in _rs_copies(*_my_place(), srcs, lnd, send_sems, recv_sems):
            cp.start()
        token[...] = jnp.zeros_like(token)

    outs = pl.pallas_call(
        body, name=name,
        in_specs=[HBM] * (2 * n),
        out_specs=[SEMS, SEMS] + [HBM] * (2 * n) + [pl.BlockSpec(memory_space=pltpu.VMEM)],
        out_shape=[pltpu.SemaphoreType.DMA((n * (N_DEV - 1),)), pltpu.SemaphoreType.DMA((n * (N_DEV - 1),))]
        + [pltpu.HBM(a.shape, a.dtype) for a in arrays] + [pltpu.HBM(l.shape, l.dtype) for l in lands]
        + [_sds((8, 128), F32)],
        input_output_aliases={i: 2 + i for i in range(2 * n)},
        compiler_params=pltpu.CompilerParams(has_side_effects=SIDE_EFFECT),
    )(*[hbm(a) for a in arrays], *lands)
    return outs[0], outs[1], outs[2:2 + n], outs[2 + n:2 + 2 * n], outs[-1]


def _rs_wait(name, send_sems, recv_sems, srcs, lands, after):
    n = len(srcs)

    def body(*refs):
        src_refs, lnd = refs[:n], refs[n:2 * n]
        send, recv = refs[2 * n], refs[2 * n + 1]
        for cp in _rs_copies(*_my_place(), src_refs, lnd, send, recv):
            cp.wait_send()
            cp.wait_recv()

    outs = pl.pallas_call(
        body, name=name,
        in_specs=[HBM] * (2 * n) + [SEMS, SEMS] + [ANY] * len(after),
        out_specs=[HBM] * (2 * n),
        out_shape=[pltpu.HBM(a.shape, a.dtype) for a in list(srcs) + list(lands)],
        input_output_aliases={i: i for i in range(2 * n)},
        compiler_params=pltpu.CompilerParams(has_side_effects=SIDE_EFFECT),
    )(*srcs, *lands, send_sems, recv_sems, *after)
    return outs[:n], outs[n:]


SMALL_PACK_ROWS = 16


def _all_reduce_small(rows, deps=()):
    n = len(rows)

    def body(*refs):
        ins = refs[:n]
        out_ref, mine, buf, send_sems, recv_sems = refs[n + len(deps):]
        x, y, c = _my_place()
        mine[...] = jnp.zeros_like(mine)
        for (r0, a), ref in zip(rows, ins):
            mine[r0:r0 + a.shape[0], 0:a.shape[1]] = ref[...]
        buf[_flat(x, y, c)] = mine[...]
        copies = []
        for r in range(1, N_DEV):
            peer = _peer(x, y, c, r)
            send = pltpu.make_async_remote_copy(
                src_ref=mine, dst_ref=buf.at[_flat(x, y, c)],
                send_sem=send_sems.at[r - 1], recv_sem=recv_sems.at[r - 1], device_id=peer, device_id_type=MESH)
            send.start()
            recv = pltpu.make_async_remote_copy(
                src_ref=mine, dst_ref=buf.at[_flat(*peer)],
                send_sem=send_sems.at[r - 1], recv_sem=recv_sems.at[r - 1], device_id=peer, device_id_type=MESH)
            copies.append((send, recv))
        for send, recv in copies:
            send.wait_send()
            recv.wait_recv()
        acc = buf[0]
        for s in range(1, N_DEV):
            acc = acc + buf[s]
        out_ref[...] = acc

    vm = pl.BlockSpec(memory_space=pltpu.VMEM)
    shape = (SMALL_PACK_ROWS, D_MODEL)
    return pl.pallas_call(
        body, name="all_reduce_small", in_specs=[vm] * n + [ANY] * len(deps), out_specs=vm,
        out_shape=_sds(shape, F32),
        scratch_shapes=[pltpu.VMEM(shape, F32), pltpu.VMEM((N_DEV,) + shape, F32),
                        pltpu.SemaphoreType.DMA((7,)), pltpu.SemaphoreType.DMA((7,))],
    )(*[a for _, a in rows], *deps)


def _adamw_math(w, g, m, v):
    m = ADAM_B1 * m + (1.0 - ADAM_B1) * g
    v = ADAM_B2 * v + (1.0 - ADAM_B2) * (g * g)
    m_hat = m / (1.0 - ADAM_B1 ** ADAM_STEP)
    v_hat = v / (1.0 - ADAM_B2 ** ADAM_STEP)
    delta = -ADAM_LR * (m_hat / (jnp.sqrt(v_hat) + ADAM_EPS) + ADAM_WD * w)
    return delta, m, v


def _adamw_big(name, w, m, v, srcs, lands, me):
    nl, rows, cols = w.shape
    tr = next(cand for cand in (256, 128, 64, 32, 16, 8) if rows % cand == 0)

    def body(me_ref, w_ref, m_ref, v_ref, *rest):
        src_refs, land_refs = rest[:nl], rest[nl:2 * nl]
        g_ref, d_ref, mo_ref, vo_ref = rest[2 * nl:]
        for layer in range(nl):
            @pl.when(pl.program_id(0) == layer)
            def _():
                g = src_refs[layer][...].astype(F32)
                for s in range(N_DEV - 1):
                    g = g + land_refs[layer][s].astype(F32)
                delta, mn, vn = _adamw_math(w_ref[...], g, m_ref[...], v_ref[...])
                g_ref[...] = g
                d_ref[...] = delta
                mo_ref[...] = mn
                vo_ref[...] = vn

    blk = pl.BlockSpec((None, tr, cols), lambda l, i, me_ref: (l, i, 0))
    at = lambda layer, l, i: jnp.where(l == layer, i, 0)
    own = [pl.BlockSpec((None, tr, cols), functools.partial(lambda layer, l, i, me_ref: (me_ref[0], at(layer, l, i), 0),
                                                            layer)) for layer in range(nl)]
    peers = [pl.BlockSpec((N_DEV - 1, tr, cols), functools.partial(lambda layer, l, i, me_ref: (0, at(layer, l, i), 0),
                                                                   layer)) for layer in range(nl)]
    return pl.pallas_call(
        body, name=name,
        grid_spec=pltpu.PrefetchScalarGridSpec(
            num_scalar_prefetch=1, grid=(nl, rows // tr),
            in_specs=[blk, blk, blk] + own + peers, out_specs=[blk] * 4),
        out_shape=[_sds((nl, rows, cols), F32)] * 4,
        compiler_params=_cparams(("arbitrary", "arbitrary")),
    )(me, w, m, v, *srcs, *lands)


def _adamw_small(ws, gs, ms, vs):
    n = len(ws)

    def body(*refs):
        w_refs, g_refs, m_refs, v_refs = (refs[i * n:(i + 1) * n] for i in range(4))
        d_out, m_out, v_out = (refs[(4 + i) * n:(5 + i) * n] for i in range(3))
        for i in range(n):
            delta, mn, vn = _adamw_math(w_refs[i][...], g_refs[i][...], m_refs[i][...], v_refs[i][...])
            d_out[i][...] = delta
            m_out[i][...] = mn
            v_out[i][...] = vn

    vm = pl.BlockSpec(memory_space=pltpu.VMEM)
    outs = pl.pallas_call(
        body, name="adamw_small", in_specs=[vm] * (4 * n), out_specs=[vm] * (3 * n),
        out_shape=[_sds(a.shape, F32) for a in ws] * 3,
    )(*ws, *gs, *ms, *vs)
    return outs[:n], outs[n:2 * n], outs[2 * n:]


def _pad_to(a, rows, cols):
    return jnp.pad(a, ((0, rows - a.shape[0]), (0, cols - a.shape[1])))


def _place_own(blocks):
    me = _flat(*_my_place())
    return [lax.dynamic_update_slice(lax.empty((N_DEV,) + b.shape, b.dtype), b[None], (me,) + (0,) * b.ndim)
            for b in blocks]


def _ag_copies(x, y, c, blocks, bufs, send_sems, recv_sems, arriving):
    copies = []
    for a in range(len(blocks)):
        for r in range(1, N_DEV):
            peer = _peer(x, y, c, r)
            k = a * (N_DEV - 1) + r - 1
            copies.append(pltpu.make_async_remote_copy(
                src_ref=blocks[a], dst_ref=bufs[a].at[_flat(*(peer if arriving else (x, y, c)))],
                send_sem=send_sems.at[k], recv_sem=recv_sems.at[k], device_id=peer, device_id_type=MESH))
    return copies


def _ag_start(groups, after):
    flat = [pair for g in groups for pair in g]
    n, ng = len(flat), len(groups)
    hbm = lambda a: pltpu.with_memory_space_constraint(a, pltpu.HBM)

    def body(*refs):
        blocks, bufs = refs[:n], refs[n:2 * n]
        sems = refs[2 * n + len(after):2 * n + len(after) + 2 * ng]
        x, y, c = _my_place()
        at = 0
        for gi, g in enumerate(groups):
            for cp in _ag_copies(x, y, c, blocks[at:at + len(g)], bufs[at:at + len(g)], sems[2 * gi],
                                 sems[2 * gi + 1], arriving=False):
                cp.start()
            at += len(g)
        refs[-1][...] = jnp.zeros_like(refs[-1])

    sem_shapes = [pltpu.SemaphoreType.DMA((len(g) * (N_DEV - 1),)) for g in groups for _ in range(2)]
    outs = pl.pallas_call(
        body, name="gather_start",
        in_specs=[HBM] * (2 * n) + [ANY] * len(after),
        out_specs=[SEMS] * (2 * ng) + [HBM] * (2 * n) + [pl.BlockSpec(memory_space=pltpu.VMEM)],
        out_shape=sem_shapes + [pltpu.HBM(b.shape, b.dtype) for b, _ in flat]
        + [pltpu.HBM(u.shape, u.dtype) for _, u in flat] + [_sds((8, 128), F32)],
        input_output_aliases={i: 2 * ng + i for i in range(2 * n)},
        compiler_params=pltpu.CompilerParams(has_side_effects=SIDE_EFFECT),
    )(*[hbm(b) for b, _ in flat], *[hbm(u) for _, u in flat], *after)
    blocks_thru, bufs_thru = outs[2 * ng:2 * ng + n], outs[2 * ng + n:2 * ng + 2 * n]
    started, at = [], 0
    for gi, g in enumerate(groups):
        started.append((outs[2 * gi], outs[2 * gi + 1], blocks_thru[at:at + len(g)], bufs_thru[at:at + len(g)]))
        at += len(g)
    return started, outs[-1]


def _ag_wait(name, send_sems, recv_sems, blocks, bufs, after):
    n = len(blocks)

    def body(*refs):
        for cp in _ag_copies(*_my_place(), refs[:n], refs[n:2 * n], refs[2 * n], refs[2 * n + 1], arriving=True):
            cp.wait_send()
            cp.wait_recv()

    outs = pl.pallas_call(
        body, name=name,
        in_specs=[HBM] * (2 * n) + [SEMS, SEMS] + [ANY] * len(after),
        out_specs=[HBM] * (2 * n),
        out_shape=[pltpu.HBM(a.shape, a.dtype) for a in list(blocks) + list(bufs)],
        input_output_aliases={i: i for i in range(2 * n)},
        compiler_params=pltpu.CompilerParams(has_side_effects=SIDE_EFFECT),
    )(*blocks, *bufs, send_sems, recv_sems, *after)
    return outs[n:]


def _split_call(name, body, thru, sems_in, new_sems, after):
    n, ns, nn = len(thru), len(sems_in), len(new_sems)
    hbm = lambda a: pltpu.with_memory_space_constraint(a, pltpu.HBM)

    def wrapped(*refs):
        body(refs[:n], refs[n:n + ns], refs[n + ns + len(after):n + ns + len(after) + nn])
        refs[-1][...] = jnp.zeros_like(refs[-1])

    outs = pl.pallas_call(
        wrapped, name=name,
        in_specs=[HBM] * n + [SEMS] * ns + [ANY] * len(after),
        out_specs=[SEMS] * nn + [HBM] * n + [pl.BlockSpec(memory_space=pltpu.VMEM)],
        out_shape=[pltpu.SemaphoreType.DMA((k,)) for k in new_sems] + [pltpu.HBM(a.shape, a.dtype) for a in thru]
        + [_sds((8, 128), F32)],
        input_output_aliases={i: nn + i for i in range(n)},
        compiler_params=pltpu.CompilerParams(has_side_effects=SIDE_EFFECT),
    )(*[hbm(a) for a in thru], *sems_in, *after)
    return list(outs[:nn]), list(outs[nn:nn + n]), outs[-1]


def _first_gather(blocks, bufs, overlap):
    n = len(blocks)

    def copies(refs, s1, r1, s2, r2):
        x, y, c = _my_place()
        me, sibling = (x, y, c), (x, y, 1 - c)
        chips = [(1 - x, y), (x, 1 - y), (1 - x, 1 - y)]
        blk, buf = refs[:n], refs[n:]
        out = dict(send1=[], recv1_sib=[], recv1_ici=[], send2=[], recv2=[])
        for a in range(n):
            place = lambda dev: buf[a].at[_flat(*dev)]
            for k, to in enumerate([sibling] + [(*chip, c) for chip in chips]):
                mk = lambda dst: pltpu.make_async_remote_copy(
                    src_ref=blk[a], dst_ref=dst, send_sem=s1.at[4 * a + k], recv_sem=r1.at[4 * a + k],
                    device_id=to, device_id_type=MESH)
                out['send1'].append(mk(place(me)))
                out['recv1_sib' if k == 0 else 'recv1_ici'].append(mk(place(to)))
            for j, chip in enumerate(chips):
                mk = lambda dev: pltpu.make_async_remote_copy(
                    src_ref=place(dev), dst_ref=place(dev), send_sem=s2.at[3 * a + j], recv_sem=r2.at[3 * a + j],
                    device_id=sibling, device_id_type=MESH)
                out['send2'].append(mk((*chip, c)))
                out['recv2'].append(mk((*chip, 1 - c)))
        return out

    def start(refs, sems_in, new):
        for cp in copies(refs, new[0], new[1], new[0], new[1])['send1']:
            cp.start()

    def forward(refs, sems_in, new):
        cps = copies(refs, sems_in[0], sems_in[1], new[0], new[1])
        for cp in cps['recv1_ici']:
            cp.wait_recv()
        for cp in cps['send2']:
            cp.start()

    def finish(refs, sems_in, new):
        cps = copies(refs, *sems_in)
        for cp in cps['recv1_sib'] + cps['recv2']:
            cp.wait_recv()
        for cp in cps['send1'] + cps['send2']:
            cp.wait_send()

    sems1, thru, token = _split_call("first_gather_start", start, list(blocks) + list(bufs), [], [4 * n, 4 * n], ())
    after = overlap(token)
    sems2, thru, token = _split_call("first_gather_forward", forward, thru, sems1, [3 * n, 3 * n], after)
    _, thru, _ = _split_call("first_gather_wait", finish, thru, sems1 + sems2, [], ())
    return thru[n:], token


def _prepare_weights(p, x):
    n = N_DEV
    bf = lambda a: a.astype(BF16)
    gn_pack = jnp.concatenate([
        _pad_to(p['ret_gn'][0], RET_HEADS, 128), _pad_to(p['mla_q_a_norm'], 1, 128),
        _pad_to(p['mla_kv_a_norm'], 1, 128), jnp.zeros((2, 128), F32)], axis=0)
    ple = lambda l: [bf(p['ple_gate_w'][l]), bf(p['ple_proj_w'][l])]
    names = ('ret_out', 'mlp_w1_0', 'mlp_w2_0', 'ple_0', 'mla', 'layer_1')
    later = [[bf(p['ret_w_out'][0])], [bf(p['mlp_w1'][0])], [bf(p['mlp_w2'][0])], ple(0),
             [bf(p['mla_w_in'][0]), bf(p['mla_w_uq'][0]), bf(p['mla_w_ukv'][0]), bf(p['mla_w_out'][0])],
             [bf(p['mlp_w1'][1]), bf(p['mlp_w2'][1])] + ple(1)]
    first = [gn_pack, bf(p['ret_w_in'][0])]
    behind = {}

    def overlap(token):
        behind['hn0'] = _rms_fwd("mix_norm0", x, p['mix_norm'][0:1], deps=(token,))
        behind['bufs'] = _place_own([b for g in later for b in g])
        behind['tables'] = _rope_tables(x.shape[0], token[0, 0])
        cos_r, sin_r, tabs = behind['tables']
        return (behind['hn0'], cos_r, sin_r, *tabs, *behind['bufs'])

    (pack, wri), token = _first_gather(first, _place_own(first), overlap)
    bufs = behind['bufs']
    groups, at = [], 0
    for g in later:
        groups.append(list(zip(g, bufs[at:at + len(g)])))
        at += len(g)
    started, token = _ag_start(groups, (token,))

    w = {k: p[k] for k in ('mix_norm', 'mlp_norm', 'ple_norm')}
    w['hn0'] = behind['hn0']
    w['tables'] = behind['tables']
    w['ret_gn'] = pack[:, :RET_HEADS, :RET_DV // n].transpose(1, 0, 2).reshape(RET_HEADS, RET_DV)
    w['mla_q_a_norm'] = pack[:, RET_HEADS, :MLA_Q_RANK // n].reshape(1, MLA_Q_RANK)
    w['mla_kv_a_norm'] = pack[:, RET_HEADS + 1, :MLA_KV_RANK // n].reshape(1, MLA_KV_RANK)
    w['ret_w_in'] = wri
    w['mla_q_norm'] = _spread_rope(p['mla_q_norm'])
    w['mla_k_norm'] = _spread_rope(p['mla_k_norm'])
    w['deps'] = (token,)

    def fetch(name, after):
        got = list(_ag_wait("gather_wait_" + name, *started[names.index(name)], after))
        if name == 'ret_out':
            return dict(ret_w_out=got[0].reshape(RET_V_W, D_MODEL))
        if name == 'mla':
            wmi, wuq, wukv, wmo = got
            return dict(mla_w_in=_spread_rope(wmi.reshape(D_MODEL, MLA_IN)), mla_w_uq=_spread_rope(wuq),
                        mla_w_ukv=wukv, mla_w_out=wmo.reshape(D_MODEL, D_MODEL))
        out = {}
        if name in ('mlp_w1_0', 'layer_1'):
            out['mlp_w1'] = got.pop(0)
        if name in ('mlp_w2_0', 'layer_1'):
            out['mlp_w2'] = got.pop(0)
        if name in ('ple_0', 'layer_1'):
            out['ple_gate_w'] = got[0].reshape(D_MODEL, D_MODEL)
            out['ple_proj_w'] = got[1].transpose(1, 0, 2).reshape(PLE_DIM, D_MODEL)
        return out

    return w, fetch


def _small_grads(small, after):
    rows = [(0, small['mix_norm'][0]), (1, small['mix_norm'][1]), (2, small['mlp_norm'][0]),
            (3, small['mlp_norm'][1]), (4, small['ple_norm'][0]), (5, small['ple_norm'][1]),
            (6, small['ret_gn']), (10, small['mla_q_a_norm']), (11, small['mla_kv_a_norm']),
            (12, small['mla_q_norm']), (13, small['mla_k_norm']), (14, small['sq_err'])]
    gs = _all_reduce_small(rows, after)
    me = _flat(*_my_place())
    n = N_DEV
    return dict(
        sq_err=gs[14, 0],
        mix_norm=gs[0:2], mlp_norm=gs[2:4], ple_norm=gs[4:6],
        ret_gn=lax.dynamic_slice(gs, (6, me * (RET_DV // n)), (RET_HEADS, RET_DV // n)),
        mla_q_a_norm=lax.dynamic_slice(gs, (10, me * (MLA_Q_RANK // n)), (1, MLA_Q_RANK // n)),
        mla_kv_a_norm=lax.dynamic_slice(gs, (11, me * (MLA_KV_RANK // n)), (1, MLA_KV_RANK // n)),
        mla_q_norm=_gather_rope(gs[12:13, :MLA_HD_PAD]), mla_k_norm=_gather_rope(gs[13:14, :MLA_HD_PAD]))


def kernel(x, p, mix_norm, ret_w_in, ret_gn, ret_w_out, mla_w_in, mla_q_a_norm, mla_kv_a_norm, mla_w_uq, mla_w_ukv, mla_q_norm, mla_k_norm, mla_w_out, mlp_norm, mlp_w1, mlp_w2, ple_norm, ple_gate_w, ple_proj_w, loss_target, m_mix_norm, m_ret_w_in, m_ret_gn, m_ret_w_out, m_mla_w_in, m_mla_q_a_norm, m_mla_kv_a_norm, m_mla_w_uq, m_mla_w_ukv, m_mla_q_norm, m_mla_k_norm, m_mla_w_out, m_mlp_norm, m_mlp_w1, m_mlp_w2, m_ple_norm, m_ple_gate_w, m_ple_proj_w, v_mix_norm, v_ret_w_in, v_ret_gn, v_ret_w_out, v_mla_w_in, v_mla_q_a_norm, v_mla_kv_a_norm, v_mla_w_uq, v_mla_w_ukv, v_mla_q_norm, v_mla_k_norm, v_mla_w_out, v_mlp_norm, v_mlp_w1, v_mlp_w2, v_ple_norm, v_ple_gate_w, v_ple_proj_w):
    given = dict(locals())
    params = {n: given[n] for n in WEIGHTS}
    w, fetch = _prepare_weights(params, x[0])

    started = []

    def emit(group):
        keys = list(group)
        send, recv, srcs, lands, token = _rs_start(f"rs_start{len(started)}", [group[k] for k in keys])
        started.append((keys, send, recv, srcs, lands))
        return (token,)

    sq_err, grad_x, _, small = _local_step(x[0], p, loss_target[0], w, fetch, emit)
    small['sq_err'] = sq_err[0:1]

    grads, deltas, new_m, new_v = {}, {}, {}, {}
    total = {}

    def small_updates(after):
        sg = _small_grads(small, after)
        total['loss'] = 0.5 / D_MODEL * sg['sq_err']
        two_d = lambda a: a.reshape(-1, a.shape[-1])
        d_s, m_s, v_s = _adamw_small(
            [two_d(params[n]) for n in SMALL], [sg[n] for n in SMALL],
            [two_d(given["m_" + n]) for n in SMALL], [two_d(given["v_" + n]) for n in SMALL])
        for i, n in enumerate(SMALL):
            shape = params[n].shape
            grads[n], deltas[n], new_m[n], new_v[n] = (a.reshape(shape) for a in (sg[n], d_s[i], m_s[i], v_s[i]))
        return (d_s[0],)

    me = _flat(*_my_place()).astype(jnp.int32).reshape(1)
    after = (grad_x,)
    src_of, land_of = {}, {}
    for gi, (keys, send, recv, srcs, lands) in enumerate(started):
        if gi == len(started) - 1:
            after = small_updates(after)
        srcs, lands = _rs_wait(f"rs_wait{gi}", send, recv, srcs, lands, after)
        for k, s, l in zip(keys, srcs, lands):
            src_of[k], land_of[k] = s, l
        done = [n for n in BIG if n not in grads and all((n, l) in src_of for l in range(params[n].shape[0]))]
        for n in done:
            layers = range(params[n].shape[0])
            grads[n], deltas[n], new_m[n], new_v[n] = _adamw_big(
                "adamw_" + n, params[n], given["m_" + n], given["v_" + n],
                [src_of[(n, l)] for l in layers], [land_of[(n, l)] for l in layers], me)
        if done:
            after = tuple(deltas[n] for n in done)

    return (total['loss'], grad_x[None], *[grads[n] for n in WEIGHTS], *[deltas[n] for n in WEIGHTS],
            *[new_m[n] for n in WEIGHTS], *[new_v[n] for n in WEIGHTS])
```

```python
import functools

import jax
import jax.numpy as jnp
from jax import lax
from jax.experimental import pallas as pl
from jax.experimental.pallas import tpu as pltpu

F32 = jnp.float32
BF16 = jnp.bfloat16
MESH = pl.DeviceIdType.MESH
ANY = pl.BlockSpec(memory_space=pl.ANY)

N_DEV = 8
D_MODEL = 1024
CHUNK = 64
RET_BLOCK = 4 * CHUNK
EPS = 1e-6
ROPE_THETA = 10000.0
RET_HEADS = 4
RET_DK = 256
RET_DV = 512
RET_QK_W = RET_HEADS * RET_DK
RET_V_W = RET_HEADS * RET_DV
RET_IN = 2 * RET_QK_W + 2 * RET_V_W
MLA_HEADS = 8
MLA_NOPE = 128
MLA_ROPE = 64
MLA_QKD = MLA_NOPE + MLA_ROPE
MLA_VD = 128
MLA_Q_RANK = 384
MLA_KV_RANK = 256
MLA_IN = MLA_Q_RANK + MLA_KV_RANK + MLA_ROPE
MLA_IN_PAD = 768
MLA_HD_PAD = 256
D_FF = 4096
PLE_DIM = 256
ATT_SCALE = MLA_QKD ** -0.5
LOG2E = 1.4426950408889634
ATT_EXP2 = ATT_SCALE * LOG2E

ADAM_LR = 0.001
ADAM_B1 = 0.9
ADAM_B2 = 0.999
ADAM_EPS = 1e-08
ADAM_WD = 0.01
ADAM_STEP = 10

VMEM_LIMIT = 52 * 1024 * 1024
ROW_TILE = 1024
RET_ROWS = 512
ATT_BLOCK = 256
ATT_QROWS = 1024
ATT_KROWS = 1024
ATT_HEADS = 2
PREP_ROWS = 1024

WEIGHTS = ['mix_norm', 'ret_w_in', 'ret_gn', 'ret_w_out', 'mla_w_in', 'mla_q_a_norm', 'mla_kv_a_norm',
           'mla_w_uq', 'mla_w_ukv', 'mla_q_norm', 'mla_k_norm', 'mla_w_out', 'mlp_norm', 'mlp_w1', 'mlp_w2',
           'ple_norm', 'ple_gate_w', 'ple_proj_w']
BIG = ['ret_w_in', 'ret_w_out', 'mla_w_in', 'mla_w_uq', 'mla_w_ukv', 'mla_w_out', 'mlp_w1', 'mlp_w2',
       'ple_gate_w', 'ple_proj_w']
SMALL = [w for w in WEIGHTS if w not in BIG]


def _cparams(sem=None):
    return pltpu.CompilerParams(dimension_semantics=sem, vmem_limit_bytes=VMEM_LIMIT)


def _dot(a, b, ca, cb):
    return lax.dot_general(a, b, (((ca,), (cb,)), ((), ())), preferred_element_type=F32)


def _bf(v):
    return v if v.dtype == BF16 else v.astype(BF16)


def _sigmoid(z):
    return 1.0 / (1.0 + jnp.exp(-z))


def _mm(name, grid, a, a_spec, b, b_spec, contract, outs, extras=(), epi=None, deps=(), split=None):
    nk = grid[2]
    n_ex, n_out, n_dep = len(extras), len(outs), len(deps)
    acc_shape = tuple(d for d in outs[0][1].block_shape if d is not None)
    if split is not None:
        acc_shape = (acc_shape[1], acc_shape[0] * split)

    def body(*refs):
        a_ref, b_ref = refs[:2]
        ex_refs = refs[2:2 + n_ex]
        out_refs = refs[2 + n_ex + n_dep:2 + n_ex + n_dep + n_out]

        def product():
            return _dot(_bf(a_ref[...]), _bf(b_ref[...]), contract[0], contract[1])

        def finish(acc):
            if split is not None:
                for j in range(acc_shape[1] // split):
                    out_refs[0][j] = acc[:, j * split:(j + 1) * split].astype(out_refs[0].dtype)
                return
            acc = acc[...]
            res = epi(acc, *[r[...] for r in ex_refs]) if epi is not None else (acc,)
            for o, r in zip(out_refs, res):
                o[...] = r.astype(o.dtype)

        if nk == 1:
            finish(product())
        else:
            acc_ref = refs[-1]
            k = pl.program_id(2)

            @pl.when(k == 0)
            def _():
                acc_ref[...] = jnp.zeros_like(acc_ref)

            acc_ref[...] += product()

            @pl.when(k == nk - 1)
            def _():
                finish(acc_ref)

    return pl.pallas_call(
        body, name=name, grid=grid,
        in_specs=[a_spec, b_spec] + [s for _, s in extras] + [ANY] * n_dep,
        out_specs=[s for _, s in outs],
        out_shape=[s for s, _ in outs],
        scratch_shapes=[pltpu.VMEM(acc_shape, F32)] if nk > 1 else [],
        compiler_params=_cparams(("parallel", "parallel", "arbitrary")),
    )(a, b, *[x for x, _ in extras], *deps)


def _mm_rows(name, tm, a, w, mode, outs, extras=(), epi=None, deps=()):
    n_sh, rows, cols = w.shape
    n_ex, n_out, n_dep = len(extras), len(outs), len(deps)
    by_cols = mode in ('nn_cols', 'nt_rows')
    width = cols if mode == 'nn_cols' else rows

    def body(*refs):
        a_ref, w_ref = refs[:2]
        ex_refs = refs[2:2 + n_ex]
        out_refs = refs[2 + n_ex + n_dep:2 + n_ex + n_dep + n_out]
        if by_cols:
            av = _bf(a_ref[...])
            for s in range(n_sh):
                cs = slice(s * width, (s + 1) * width)
                acc = _dot(av, w_ref[s], 1, 0 if mode == 'nn_cols' else 1)
                res = epi(acc, *[r[:, cs] for r in ex_refs]) if epi is not None else (acc,)
                for o, r in zip(out_refs, res):
                    o[:, cs] = r.astype(o.dtype)
        else:
            chunk = rows if mode == 'nn_rows' else cols
            acc = None
            for s in range(n_sh):
                part = _dot(_bf(a_ref[:, s * chunk:(s + 1) * chunk]), w_ref[s], 1, 0 if mode == 'nn_rows' else 1)
                acc = part if acc is None else acc + part
            res = epi(acc, *[r[...] for r in ex_refs]) if epi is not None else (acc,)
            for o, r in zip(out_refs, res):
                o[...] = r.astype(o.dtype)

    t, ka = a.shape
    return pl.pallas_call(
        body, name=name, grid=(t // tm, 1, 1),
        in_specs=[pl.BlockSpec((tm, ka), lambda i, j, k: (i, 0)),
                  pl.BlockSpec((n_sh, rows, cols), lambda i, j, k: (0, 0, 0))] + [s for _, s in extras] + [ANY] * n_dep,
        out_specs=[s for _, s in outs],
        out_shape=[s for s, _ in outs],
        compiler_params=_cparams(("parallel", "arbitrary", "arbitrary")),
    )(a, w, *[x for x, _ in extras], *deps)


def _sds(shape, dtype):
    return jax.ShapeDtypeStruct(shape, dtype)


def _row_tile(t, cap=ROW_TILE):
    return min(cap, t)


def _rms_fwd(name, x, g, deps=()):
    t, d = x.shape
    tm = _row_tile(t)

    def body(x_ref, g_ref, *rest):
        o_ref = rest[-1]
        xv = x_ref[...]
        r = lax.rsqrt(jnp.mean(xv * xv, axis=-1, keepdims=True) + EPS)
        o_ref[...] = (xv * r * g_ref[...]).astype(o_ref.dtype)

    return pl.pallas_call(
        body, name=name, grid=(t // tm,),
        in_specs=[pl.BlockSpec((tm, d), lambda i: (i, 0)), pl.BlockSpec((1, d), lambda i: (0, 0))] + [ANY] * len(deps),
        out_specs=pl.BlockSpec((tm, d), lambda i: (i, 0)),
        out_shape=_sds((t, d), BF16),
        compiler_params=_cparams(("parallel",)),
    )(x, g, *deps)


def _rms_bwd_rows(dy, xv, g, n):
    r = lax.rsqrt(jnp.sum(xv * xv, axis=-1, keepdims=True) / n + EPS)
    xh = xv * r
    dxh = dy * g
    dx = r * (dxh - xh * (jnp.sum(dxh * xh, axis=-1, keepdims=True) / n))
    return dx, dy * xh


def _ple_gate_bwd(name, dh, gate, e):
    t, d = dh.shape
    tm = _row_tile(t)

    def body(dh_ref, g_ref, e_ref, de_ref, dz_ref):
        dh_v, gt = dh_ref[...], g_ref[...].astype(F32)
        de_ref[...] = (dh_v * gt).astype(BF16)
        dz_ref[...] = (dh_v * e_ref[...].astype(F32) * (gt * (1.0 - gt))).astype(BF16)

    row = pl.BlockSpec((tm, d), lambda i: (i, 0))
    return pl.pallas_call(
        body, name=name, grid=(t // tm,), in_specs=[row, row, row], out_specs=[row, row],
        out_shape=[_sds((t, d), BF16), _sds((t, d), BF16)],
        compiler_params=_cparams(("parallel",)),
    )(dh, gate, e)


def _rope_half(v, cos, sin):
    half = v.shape[-1] // 2
    v1, v2 = v[:, :half], v[:, half:]
    return jnp.concatenate([v1 * cos - v2 * sin, v2 * cos + v1 * sin], axis=-1)


def _ret_consts():
    lg = jnp.log(1.0 - 2.0 ** (-5.0 - jnp.arange(RET_HEADS, dtype=F32)))
    idx = jnp.arange(RET_BLOCK, dtype=F32)
    chunk = jnp.floor(idx / CHUNK)
    dist = idx[:, None] - idx[None, :]
    same = chunk[:, None] == chunk[None, :]
    seen = jnp.where(same, jnp.abs(dist), jnp.where(chunk[None, :] < chunk[:, None], dist, jnp.inf))
    intra = jnp.exp(lg[:, None, None] * seen)
    qdec = jnp.exp(lg[:, None] * (idx + 1.0))
    kdec = jnp.exp(lg[:, None] * (RET_BLOCK - 1.0 - idx))
    cdec = jnp.exp(lg * RET_BLOCK)
    qdec = jnp.broadcast_to(qdec[:, :, None], (RET_HEADS, RET_BLOCK, RET_DK))
    kdec = jnp.broadcast_to(kdec[:, :, None], (RET_HEADS, RET_BLOCK, RET_DK))
    cdec = jnp.broadcast_to(cdec[:, None, None], (RET_HEADS, 1, RET_DV))
    return intra, qdec, kdec, cdec


def _ret_specs(rb, rev_nb=None):
    blk = (lambda i: i) if rev_nb is None else (lambda i: rev_nb - 1 - i)
    full = lambda shape: pl.BlockSpec(shape, lambda i: (0,) * len(shape))
    return dict(
        proj=pl.BlockSpec((rb, RET_IN), lambda i: (blk(i), 0)),
        tab=pl.BlockSpec((rb, RET_DK // 2), lambda i: (blk(i), 0)),
        vw=pl.BlockSpec((rb, RET_V_W), lambda i: (blk(i), 0)),
        st=pl.BlockSpec((rb // RET_BLOCK, RET_HEADS, RET_DK, RET_DV), lambda i: (blk(i), 0, 0, 0)),
        gn=full((RET_HEADS, 1, RET_DV)),
        intra=full((RET_HEADS, RET_BLOCK, RET_BLOCK)),
        dec=full((RET_HEADS, RET_BLOCK, RET_DK)),
        cdec=full((RET_HEADS, 1, RET_DV)),
    )


def _ret_fwd(proj, cos, sin, gn):
    t = proj.shape[0]
    rb = min(RET_ROWS, t)
    cpb = rb // RET_BLOCK
    intra, qdec, kdec, cdec = _ret_consts()
    sp = _ret_specs(rb)

    def body(proj_ref, cos_ref, sin_ref, gn_ref, intra_ref, qd_ref, kd_ref, cd_ref,
             gated_ref, outp_ref, st_ref, s_ref):
        @pl.when(pl.program_id(0) == 0)
        def _():
            s_ref[...] = jnp.zeros_like(s_ref)

        def chunk(c, carry):
            rows = pl.ds(pl.multiple_of(c * RET_BLOCK, RET_BLOCK), RET_BLOCK)
            cs, sn = cos_ref[rows, :], sin_ref[rows, :]
            for h in range(RET_HEADS):
                q = proj_ref[rows, h * RET_DK:(h + 1) * RET_DK].astype(F32)
                k = proj_ref[rows, RET_QK_W + h * RET_DK:RET_QK_W + (h + 1) * RET_DK].astype(F32)
                v = proj_ref[rows, 2 * RET_QK_W + h * RET_DV:2 * RET_QK_W + (h + 1) * RET_DV]
                g = proj_ref[rows, 2 * RET_QK_W + RET_V_W + h * RET_DV:
                             2 * RET_QK_W + RET_V_W + (h + 1) * RET_DV].astype(F32)
                qr = _rope_half(q, cs, sn)
                kr = _rope_half(k, cs, sn) * (RET_DK ** -0.5)
                qb, kb, vb = qr.astype(BF16), kr.astype(BF16), v
                sc = _dot(qb, kb, 1, 1) * intra_ref[h]
                inner = _dot(sc.astype(BF16), vb, 1, 0)
                s_old = s_ref[h]
                sb = s_old.astype(BF16)
                st_ref[c, h] = sb
                cross = _dot((qr * qd_ref[h]).astype(BF16), sb, 1, 0)
                out = inner + cross
                s_ref[h] = s_old * cd_ref[h] + _dot((kr * kd_ref[h]).astype(BF16), vb, 0, 0)
                r = lax.rsqrt(jnp.mean(out * out, axis=-1, keepdims=True) + EPS)
                y = out * r * gn_ref[h]
                cols = slice(h * RET_DV, (h + 1) * RET_DV)
                gated_ref[rows, cols] = (g * _sigmoid(g) * y).astype(BF16)
                outp_ref[rows, cols] = out
            return carry

        lax.fori_loop(0, cpb, chunk, 0)

    return pl.pallas_call(
        body, name="ret_fwd", grid=(t // rb,),
        in_specs=[sp['proj'], sp['tab'], sp['tab'], sp['gn'], sp['intra'], sp['dec'], sp['dec'], sp['cdec']],
        out_specs=[sp['vw'], sp['vw'], sp['st']],
        out_shape=[_sds((t, RET_V_W), BF16), _sds((t, RET_V_W), F32),
                   _sds((t // RET_BLOCK, RET_HEADS, RET_DK, RET_DV), BF16)],
        scratch_shapes=[pltpu.VMEM((RET_HEADS, RET_DK, RET_DV), F32)],
        compiler_params=_cparams(("arbitrary",)),
    )(proj, cos, sin, gn.reshape(RET_HEADS, 1, RET_DV), intra, qdec, kdec, cdec)


def _ret_gate_bwd_epi(dgt, out, g, gn):
    g = g.astype(F32)
    r = lax.rsqrt(jnp.mean(out * out, axis=-1, keepdims=True) + EPS)
    xh = out * r
    sg = _sigmoid(g)
    dgate = dgt * (xh * gn) * (sg * (1.0 + g * (1.0 - sg)))
    dy = dgt * (g * sg)
    dxh = dy * gn
    dout = r * (dxh - xh * jnp.mean(dxh * xh, axis=-1, keepdims=True))
    return dout, dgate, jnp.sum(dy * xh, axis=0, keepdims=True)


def _ret_bwd(proj, cos, sin, states, dout, dgate, deps=()):
    t = proj.shape[0]
    rb = min(RET_ROWS, t)
    cpb = rb // RET_BLOCK
    nb = t // rb
    intra, qdec, kdec, cdec = _ret_consts()
    sp = _ret_specs(rb, rev_nb=nb)

    def body(proj_ref, cos_ref, sin_ref, intra_ref, qd_ref, kd_ref, cd_ref, st_ref, dout_ref, dgate_ref, *rest):
        dproj_ref, ds_ref = rest[len(deps):]

        @pl.when(pl.program_id(0) == 0)
        def _():
            ds_ref[...] = jnp.zeros_like(ds_ref)

        def chunk(cc, carry):
            c = cpb - 1 - cc
            rows = pl.ds(pl.multiple_of(c * RET_BLOCK, RET_BLOCK), RET_BLOCK)
            cs, sn = cos_ref[rows, :], sin_ref[rows, :]
            for h in range(RET_HEADS):
                q = proj_ref[rows, h * RET_DK:(h + 1) * RET_DK].astype(F32)
                k = proj_ref[rows, RET_QK_W + h * RET_DK:RET_QK_W + (h + 1) * RET_DK].astype(F32)
                v = proj_ref[rows, 2 * RET_QK_W + h * RET_DV:2 * RET_QK_W + (h + 1) * RET_DV]
                cols = slice(h * RET_DV, (h + 1) * RET_DV)
                qr = _rope_half(q, cs, sn)
                kr = _rope_half(k, cs, sn) * (RET_DK ** -0.5)
                qb, kb, vb = qr.astype(BF16), kr.astype(BF16), v
                qdb = (qr * qd_ref[h]).astype(BF16)
                kdb = (kr * kd_ref[h]).astype(BF16)
                doutb = dout_ref[rows, cols]
                itr = intra_ref[h]
                pb = (_dot(qb, kb, 1, 1) * itr).astype(BF16)
                dv = _dot(pb, doutb, 0, 0)
                dsc = (_dot(doutb, vb, 1, 1) * itr).astype(BF16)
                dq = _dot(dsc, kb, 1, 0)
                dk = _dot(dsc, qb, 0, 0)
                dq = dq + _dot(doutb, st_ref[c, h], 1, 1) * qd_ref[h]
                ds_new = ds_ref[h]
                dsb = ds_new.astype(BF16)
                dk = dk + _dot(vb, dsb, 1, 1) * kd_ref[h]
                dv = dv + _dot(kdb, dsb, 1, 0)
                ds_ref[h] = ds_new * cd_ref[h] + _dot(qdb, doutb, 0, 0)
                dproj_ref[rows, h * RET_DK:(h + 1) * RET_DK] = _rope_half(dq, cs, -sn).astype(BF16)
                dproj_ref[rows, RET_QK_W + h * RET_DK:RET_QK_W + (h + 1) * RET_DK] = (
                    _rope_half(dk * (RET_DK ** -0.5), cs, -sn).astype(BF16))
                dproj_ref[rows, 2 * RET_QK_W + h * RET_DV:2 * RET_QK_W + (h + 1) * RET_DV] = dv.astype(BF16)
                dproj_ref[rows, 2 * RET_QK_W + RET_V_W + h * RET_DV:
                          2 * RET_QK_W + RET_V_W + (h + 1) * RET_DV] = dgate_ref[rows, cols]
            return carry

        lax.fori_loop(0, cpb, chunk, 0)

    return pl.pallas_call(
        body, name="ret_bwd", grid=(nb,),
        in_specs=[sp['proj'], sp['tab'], sp['tab'], sp['intra'], sp['dec'], sp['dec'], sp['cdec'],
                  sp['st'], sp['vw'], sp['vw']] + [ANY] * len(deps),
        out_specs=sp['proj'],
        out_shape=_sds((t, RET_IN), BF16),
        scratch_shapes=[pltpu.VMEM((RET_HEADS, RET_DK, RET_DV), F32)],
        compiler_params=_cparams(("arbitrary",)),
    )(proj, cos, sin, intra, qdec, kdec, cdec, states, dout, dgate, *deps)


def _spread_rope(a):
    return jnp.pad(a, [(0, 0)] * (a.ndim - 1) + [(0, MLA_ROPE)])


def _gather_rope(a):
    return a[..., :a.shape[-1] - MLA_ROPE]


def _rope_tables(t, zero):
    pos = jnp.arange(t, dtype=F32)[:, None] + zero
    inv = 1.0 / (ROPE_THETA ** (jnp.arange(0, RET_DK, 2, dtype=F32) / RET_DK))
    ang = pos * inv[None, :]
    return jnp.cos(ang), jnp.sin(ang), _mla_tables(t, pos)


def _mla_tables(t, pos):
    half = MLA_ROPE // 2
    inv = 1.0 / (ROPE_THETA ** (jnp.arange(0, MLA_ROPE, 2, dtype=F32) / MLA_ROPE))
    ang = pos * inv[None, :]
    cos, sin = jnp.cos(ang), jnp.sin(ang)
    z = jnp.zeros((t, half), F32)
    c = jnp.concatenate([cos, cos, z, z], axis=1)
    s1 = jnp.concatenate([-sin, z, z, z], axis=1)
    s2 = jnp.concatenate([z, sin, z, z], axis=1)
    return c, s1, s2


def _rope_tile(r, c, s1, s2):
    return r * c + pltpu.roll(r, 96, 1) * s1 + pltpu.roll(r, 32, 1) * s2


def _mla_mid(proj2, qa, kva):
    t = proj2.shape[0]
    tm = _row_tile(t)

    def body(p_ref, qa_ref, kva_ref, cq_ref, ckv_ref):
        cq = p_ref[:, :MLA_Q_RANK]
        ckv = p_ref[:, MLA_Q_RANK:MLA_Q_RANK + MLA_KV_RANK]
        rq = lax.rsqrt(jnp.mean(cq * cq, axis=-1, keepdims=True) + EPS)
        rkv = lax.rsqrt(jnp.mean(ckv * ckv, axis=-1, keepdims=True) + EPS)
        cq_ref[...] = (cq * rq * qa_ref[...]).astype(BF16)
        ckv_ref[...] = (ckv * rkv * kva_ref[...]).astype(BF16)

    return pl.pallas_call(
        body, name="mla_mid", grid=(t // tm,),
        in_specs=[pl.BlockSpec((tm, MLA_IN_PAD), lambda i: (i, 0)),
                  pl.BlockSpec((1, MLA_Q_RANK), lambda i: (0, 0)),
                  pl.BlockSpec((1, MLA_KV_RANK), lambda i: (0, 0))],
        out_specs=[pl.BlockSpec((tm, MLA_Q_RANK), lambda i: (i, 0)),
                   pl.BlockSpec((tm, MLA_KV_RANK), lambda i: (i, 0))],
        out_shape=[_sds((t, MLA_Q_RANK), BF16), _sds((t, MLA_KV_RANK), BF16)],
        compiler_params=_cparams(("parallel",)),
    )(proj2, qa, kva)


def _mla_mid_bwd(proj2, qa, kva, dcq, dckv, dkr):
    t = proj2.shape[0]
    tm = _row_tile(t)

    def body(p_ref, qa_ref, kva_ref, dcq_ref, dckv_ref, dkr_ref, dp_ref, dqa_ref, dkva_ref):
        @pl.when(pl.program_id(0) == 0)
        def _():
            dqa_ref[...] = jnp.zeros_like(dqa_ref)
            dkva_ref[...] = jnp.zeros_like(dkva_ref)

        dxq, dgq = _rms_bwd_rows(dcq_ref[...], p_ref[:, :MLA_Q_RANK], qa_ref[...], MLA_Q_RANK)
        dxk, dgk = _rms_bwd_rows(dckv_ref[...], p_ref[:, MLA_Q_RANK:MLA_Q_RANK + MLA_KV_RANK], kva_ref[...],
                                 MLA_KV_RANK)
        dp_ref[:, :MLA_Q_RANK] = dxq.astype(BF16)
        dp_ref[:, MLA_Q_RANK:MLA_Q_RANK + MLA_KV_RANK] = dxk.astype(BF16)
        dp_ref[:, MLA_Q_RANK + MLA_KV_RANK:] = dkr_ref[...].astype(BF16)
        dqa_ref[...] += jnp.sum(dgq, axis=0, keepdims=True)
        dkva_ref[...] += jnp.sum(dgk, axis=0, keepdims=True)

    return pl.pallas_call(
        body, name="mla_mid_bwd", grid=(t // tm,),
        in_specs=[pl.BlockSpec((tm, MLA_IN_PAD), lambda i: (i, 0)),
                  pl.BlockSpec((1, MLA_Q_RANK), lambda i: (0, 0)),
                  pl.BlockSpec((1, MLA_KV_RANK), lambda i: (0, 0)),
                  pl.BlockSpec((tm, MLA_Q_RANK), lambda i: (i, 0)),
                  pl.BlockSpec((tm, MLA_KV_RANK), lambda i: (i, 0)),
                  pl.BlockSpec((tm, 128), lambda i: (i, 0))],
        out_specs=[pl.BlockSpec((tm, MLA_IN_PAD), lambda i: (i, 0)),
                   pl.BlockSpec((1, MLA_Q_RANK), lambda i: (0, 0)),
                   pl.BlockSpec((1, MLA_KV_RANK), lambda i: (0, 0))],
        out_shape=[_sds((t, MLA_IN_PAD), BF16), _sds((1, MLA_Q_RANK), F32), _sds((1, MLA_KV_RANK), F32)],
        compiler_params=_cparams(("arbitrary",)),
    )(proj2, qa, kva, dcq, dckv, dkr)


def _mla_prep_specs(t, tm):
    head = lambda w: pl.BlockSpec((None, tm, w), lambda i, h: (h, i, 0))
    return dict(
        head256=head(MLA_HD_PAD), head128=head(MLA_VD),
        cols256=pl.BlockSpec((tm, MLA_HD_PAD), lambda i, h: (i, h)),
        cq=pl.BlockSpec((tm, MLA_Q_RANK), lambda i, h: (i, 0)),
        ckv=pl.BlockSpec((tm, MLA_KV_RANK), lambda i, h: (i, 0)),
        wuq=pl.BlockSpec((None, MLA_Q_RANK, MLA_HD_PAD), lambda i, h: (h, 0, 0)),
        wukv=pl.BlockSpec((None, MLA_KV_RANK, MLA_HD_PAD), lambda i, h: (h, 0, 0)),
        kr=pl.BlockSpec((tm, 128), lambda i, h: (i, (MLA_Q_RANK + MLA_KV_RANK) // 128)),
        gain=pl.BlockSpec((1, MLA_HD_PAD), lambda i, h: (0, 0)),
        tab=pl.BlockSpec((tm, 128), lambda i, h: (i, 0)),
    )


def _mla_prep(cq, ckv, wuq, wukv, proj2, gq, gk, tabs):
    t = cq.shape[0]
    tm = _row_tile(t, PREP_ROWS)
    sp = _mla_prep_specs(t, tm)

    def body(cq_ref, ckv_ref, wuq_ref, wukv_ref, kr_ref, gq_ref, gk_ref, c_ref, s1_ref, s2_ref,
             qh_ref, kh_ref, vh_ref):
        c, s1, s2 = c_ref[...], s1_ref[...], s2_ref[...]

        def norm_rope(xv, gain):
            r = lax.rsqrt(jnp.sum(xv * xv, axis=-1, keepdims=True) / MLA_QKD + EPS)
            y = xv * r * gain
            return jnp.concatenate([y[:, :MLA_NOPE], _rope_tile(y[:, MLA_NOPE:], c, s1, s2)], axis=-1)

        kvv = _dot(ckv_ref[...], wukv_ref[...], 1, 0)
        qh_ref[...] = norm_rope(_dot(cq_ref[...], wuq_ref[...], 1, 0), gq_ref[...]).astype(BF16)
        kf = jnp.concatenate([kvv[:, :MLA_NOPE], kr_ref[...]], axis=-1)
        kh_ref[...] = norm_rope(kf, gk_ref[...]).astype(BF16)
        vh_ref[...] = jnp.concatenate([kvv[:, MLA_NOPE:], jnp.ones((tm, MLA_VD), F32)], axis=-1).astype(BF16)

    return pl.pallas_call(
        body, name="mla_prep", grid=(t // tm, MLA_HEADS),
        in_specs=[sp['cq'], sp['ckv'], sp['wuq'], sp['wukv'], sp['kr'], sp['gain'], sp['gain'],
                  sp['tab'], sp['tab'], sp['tab']],
        out_specs=[sp['head256'], sp['head256'], sp['head256']],
        out_shape=[_sds((MLA_HEADS, t, MLA_HD_PAD), BF16), _sds((MLA_HEADS, t, MLA_HD_PAD), BF16),
                   _sds((MLA_HEADS, t, 2 * MLA_VD), BF16)],
        compiler_params=_cparams(("parallel", "arbitrary")),
    )(cq, ckv, wuq, wukv, proj2, gq, gk, *tabs)


def _mla_prep_bwd(cq, ckv, wuq, wukv, proj2, gq, gk, tabs, dqt, dkh, dvh):
    t = cq.shape[0]
    tm = _row_tile(t, PREP_ROWS)
    ab = dqt.shape[-1]
    sp = _mla_prep_specs(t, tm)

    def body(cq_ref, ckv_ref, wuq_ref, wukv_ref, kr_ref, gq_ref, gk_ref, c_ref, s1_ref, s2_ref,
             dqt_ref, dkh_ref, dvh_ref, dq_ref, dkv_ref, dkr_ref, dgq_ref, dgk_ref):
        dqh = jnp.concatenate([dqt_ref[b].T for b in range(tm // ab)], axis=0)
        i, h = pl.program_id(0), pl.program_id(1)

        @pl.when((i == 0) & (h == 0))
        def _():
            dgq_ref[...] = jnp.zeros_like(dgq_ref)
            dgk_ref[...] = jnp.zeros_like(dgk_ref)

        @pl.when(h == 0)
        def _():
            dkr_ref[...] = jnp.zeros_like(dkr_ref)

        c, s1, s2 = c_ref[...], s1_ref[...], s2_ref[...]

        def back(xv, gain, dout):
            dy = jnp.concatenate([dout[:, :MLA_NOPE], _rope_tile(dout[:, MLA_NOPE:], c, -s1, -s2)], axis=-1)
            return _rms_bwd_rows(dy, xv, gain, MLA_QKD)

        kvv = _dot(ckv_ref[...], wukv_ref[...], 1, 0)
        dxq, dgq = back(_dot(cq_ref[...], wuq_ref[...], 1, 0), gq_ref[...], dqh)
        kf = jnp.concatenate([kvv[:, :MLA_NOPE], kr_ref[...]], axis=-1)
        dxk, dgk = back(kf, gk_ref[...], dkh_ref[...])
        dq_ref[...] = dxq.astype(BF16)
        dkv_ref[...] = jnp.concatenate([dxk[:, :MLA_NOPE], dvh_ref[...]], axis=-1).astype(BF16)
        dkr_ref[...] += dxk[:, MLA_NOPE:]
        dgq_ref[...] += jnp.sum(dgq, axis=0, keepdims=True)
        dgk_ref[...] += jnp.sum(dgk, axis=0, keepdims=True)

    return pl.pallas_call(
        body, name="mla_prep_bwd", grid=(t // tm, MLA_HEADS),
        in_specs=[sp['cq'], sp['ckv'], sp['wuq'], sp['wukv'], sp['kr'], sp['gain'], sp['gain'],
                  sp['tab'], sp['tab'], sp['tab'],
                  pl.BlockSpec((None, tm // ab, MLA_HD_PAD, ab), lambda i, h: (h, i, 0, 0)),
                  sp['head256'], sp['head128']],
        out_specs=[sp['cols256'], sp['cols256'], sp['tab'], sp['gain'], sp['gain']],
        out_shape=[_sds((t, MLA_HEADS * MLA_HD_PAD), BF16), _sds((t, MLA_HEADS * MLA_HD_PAD), BF16),
                   _sds((t, 128), F32), _sds((1, MLA_HD_PAD), F32), _sds((1, MLA_HD_PAD), F32)],
        compiler_params=_cparams(("arbitrary", "arbitrary")),
    )(cq, ckv, wuq, wukv, proj2, gq, gk, *tabs, dqt, dkh, dvh)


def _chunk_visible(rows, cols, row_off, col_off):
    rq = lax.shift_right_logical(lax.broadcasted_iota(jnp.int32, (rows, cols), 0) + row_off, 6)
    ck = lax.shift_right_logical(lax.broadcasted_iota(jnp.int32, (rows, cols), 1) + col_off, 6)
    return ck <= rq


def _rows_to_lanes(col):
    return col.T[:8, :]


def _attn_fwd(qh, kh, vh):
    t = qh.shape[1]
    ab = min(ATT_BLOCK, t)
    tq = min(ATT_QROWS, t)
    r = tq // ab
    hg = ATT_HEADS

    def body(q_ref, k_ref, v_ref, o_ref, lse_ref, acc_ref):
        n_un = pl.program_id(1) * r
        acc_ref[...] = jnp.zeros_like(acc_ref)

        def step(b, ms, diag):
            rows = pl.ds(pl.multiple_of(b * ab, ab), ab)
            out = []
            for hh in range(hg):
                m = ms[hh]
                s = _dot(q_ref[hh], k_ref[hh, rows, :], 1, 1)
                if diag is not None:
                    s = jnp.where(_chunk_visible(tq, ab, 0, diag * ab), s, -1e30)
                m_new = jnp.maximum(m, jnp.max(s, axis=-1, keepdims=True))
                p = jnp.exp2((s - m_new) * ATT_EXP2).astype(BF16)
                acc_ref[hh] = jnp.exp2((m - m_new) * ATT_EXP2) * acc_ref[hh] + _dot(p, v_ref[hh, rows, :], 1, 0)
                out.append(m_new)
            return tuple(out)

        ms = tuple(jnp.full((tq, 1), -1e30, F32) for _ in range(hg))
        ms = lax.fori_loop(0, n_un, lambda b, st: step(b, st, None), ms)
        for d in range(r):
            ms = step(n_un + d, ms, d)
        for hh in range(hg):
            l = acc_ref[hh, :, MLA_VD:]
            o_ref[:, hh * MLA_VD:(hh + 1) * MLA_VD] = acc_ref[hh, :, :MLA_VD] / l
            lse_t = _rows_to_lanes(ms[hh] * ATT_EXP2 + jnp.log(l) * LOG2E)
            for d in range(r):
                lse_ref[hh, d] = lse_t[:, d * ab:(d + 1) * ab]

    return pl.pallas_call(
        body, name="mla_attn", grid=(MLA_HEADS // hg, t // tq),
        in_specs=[pl.BlockSpec((hg, tq, MLA_HD_PAD), lambda g, i: (g, i, 0)),
                  pl.BlockSpec((hg, t, MLA_HD_PAD), lambda g, i: (g, 0, 0)),
                  pl.BlockSpec((hg, t, 2 * MLA_VD), lambda g, i: (g, 0, 0))],
        out_specs=[pl.BlockSpec((tq, hg * MLA_VD), lambda g, i: (i, g)),
                   pl.BlockSpec((hg, r, 8, ab), lambda g, i: (g, i, 0, 0))],
        out_shape=[_sds((t, MLA_HEADS * MLA_VD), F32), _sds((MLA_HEADS, t // ab, 8, ab), F32)],
        scratch_shapes=[pltpu.VMEM((hg, tq, 2 * MLA_VD), F32)],
        compiler_params=_cparams(("parallel", "arbitrary")),
    )(qh, kh, vh)


def _attn_bwd(qh, kh, vh, dob, o, lse_t):
    t = qh.shape[1]
    ab = min(ATT_BLOCK, t)
    kb = min(ATT_KROWS, t)
    r = kb // ab
    nq = t // ab
    hg = ATT_HEADS

    def body(q_ref, k_ref, v_ref, do_ref, o_ref, lse_ref, dqt_ref, dk_ref, dv_ref, dl_ref):
        j = pl.program_id(1)

        @pl.when(j == 0)
        def _():
            dqt_ref[...] = jnp.zeros_like(dqt_ref)
            ones = jnp.ones((8, MLA_VD), F32)

            def delta(b, carry):
                rows = pl.ds(pl.multiple_of(b * ab, ab), ab)
                for hh in range(hg):
                    cols = slice(hh * MLA_VD, (hh + 1) * MLA_VD)
                    prod = do_ref[rows, cols].astype(F32) * o_ref[rows, cols]
                    dl_ref[hh, b] = lax.dot_general(ones, prod, (((1,), (1,)), ((), ())),
                                                    precision=lax.Precision.HIGHEST, preferred_element_type=F32)
                return carry

            lax.fori_loop(0, nq, delta, 0)

        ks = [k_ref[hh] for hh in range(hg)]
        vs = [v_ref[hh, :, :MLA_VD] for hh in range(hg)]
        kts = [k.T for k in ks]

        dk_ref[...] = jnp.zeros_like(dk_ref)
        dv_ref[...] = jnp.zeros_like(dv_ref)

        def step(b, carry, diag):
            rows = pl.ds(pl.multiple_of(b * ab, ab), ab)
            hi = kb if diag is None else (diag + 1) * ab
            for hh in range(hg):
                q = q_ref[hh, rows, :]
                do = do_ref[rows, hh * MLA_VD:(hh + 1) * MLA_VD]
                s_t = _dot(ks[hh][:hi], q, 1, 1)
                if diag is not None:
                    key_chunk = lax.shift_right_logical(lax.broadcasted_iota(jnp.int32, (hi, ab), 0), 6)
                    query_chunk = lax.shift_right_logical(
                        lax.broadcasted_iota(jnp.int32, (hi, ab), 1) + diag * ab, 6)
                    s_t = jnp.where(key_chunk <= query_chunk, s_t, -1e30)
                p_t = jnp.exp2(s_t * ATT_EXP2 - lse_ref[hh, b][0:1, :])
                dp_t = _dot(vs[hh][:hi], do, 1, 1)
                ds_t = (p_t * (dp_t - dl_ref[hh, b][0:1, :]) * ATT_SCALE).astype(BF16)
                dqt_ref[hh, b] += _dot(kts[hh][:, :hi], ds_t, 1, 0)
                dk_ref[hh, :hi] += _dot(ds_t, q, 1, 0)
                dv_ref[hh, :hi] += _dot(p_t.astype(BF16), do, 1, 0)
            return carry

        for d in range(r):
            step(j * r + d, 0, d)
        lax.fori_loop((j + 1) * r, nq, lambda b, c: step(b, c, None), 0)

    whole = lambda w: pl.BlockSpec((hg, t, w), lambda g, j: (g, 0, 0))
    blk = lambda w: pl.BlockSpec((hg, kb, w), lambda g, j: (g, j, 0))
    stat = pl.BlockSpec((hg, nq, 8, ab), lambda g, j: (g, 0, 0, 0))
    cols = pl.BlockSpec((t, hg * MLA_VD), lambda g, j: (0, g))
    return pl.pallas_call(
        body, name="mla_attn_bwd", grid=(MLA_HEADS // hg, t // kb),
        in_specs=[whole(MLA_HD_PAD), blk(MLA_HD_PAD), blk(2 * MLA_VD),
                  cols, cols, stat],
        out_specs=[pl.BlockSpec((hg, nq, MLA_HD_PAD, ab), lambda g, j: (g, 0, 0, 0)), blk(MLA_HD_PAD), blk(MLA_VD)],
        out_shape=[_sds((MLA_HEADS, nq, MLA_HD_PAD, ab), F32), _sds((MLA_HEADS, t, MLA_HD_PAD), F32),
                   _sds((MLA_HEADS, t, MLA_VD), F32)],
        scratch_shapes=[pltpu.VMEM((hg, nq, 8, ab), F32)],
        compiler_params=_cparams(("parallel", "arbitrary")),
    )(qh, kh, vh, dob, o, lse_t)


VEC = pl.BlockSpec((1, D_MODEL), lambda i, j, k: (0, 0))


def _rows(tm, width):
    return pl.BlockSpec((tm, width), lambda i, j, k: (i, 0))


def _residual_epi(next_gain):
    if next_gain is None:
        return [], lambda acc, hv: (acc + hv,)

    def epi(acc, hv, g):
        h_new = acc + hv
        r = lax.rsqrt(jnp.mean(h_new * h_new, axis=-1, keepdims=True) + EPS)
        return h_new, h_new * r * g

    return [(next_gain, VEC)], epi


def _residual_outs(t, row, next_gain):
    outs = [(_sds((t, D_MODEL), F32), row)]
    return outs + ([(_sds((t, D_MODEL), BF16), row)] if next_gain is not None else [])


def _mlp_fwd(l, h, hn, w1g, fetch_w2, next_gain):
    t = h.shape[0]
    tm = _row_tile(t, 512)

    def relu2(acc):
        r = jnp.maximum(acc, 0.0)
        return (r * r,)

    (u,) = _mm_rows(f"mlp_up{l}", tm, hn, w1g, 'nn_cols', [(_sds((t, D_FF), BF16), _rows(tm, D_FF))], epi=relu2)
    w2g = fetch_w2((u,))
    row = _rows(tm, D_MODEL)
    more, epi = _residual_epi(next_gain)
    h2, hn_next = _mm_rows(f"mlp_down{l}", tm, u, w2g, 'nn_rows', _residual_outs(t, row, next_gain),
                           extras=[(h, row)] + more, epi=epi)
    return h2, hn_next, (h, hn, u, w1g, w2g)


def _norm_bwd_outs(t, tm):
    return [(_sds((t, D_MODEL), F32), pl.BlockSpec((tm, D_MODEL), lambda i, j, k: (i, 0))),
            (_sds((t // tm, 1, D_MODEL), F32), pl.BlockSpec((None, 1, D_MODEL), lambda i, j, k: (i, 0, 0)))]


def _norm_bwd_epi(acc, xv, res, g):
    dx, dgr = _rms_bwd_rows(acc, xv, g, D_MODEL)
    return res + dx, jnp.sum(dgr, axis=0, keepdims=True)


def _mlp_bwd(l, dh, saved, norm_g, emit_w2=None, emit_w1=None):
    h, hn, u, w1g, w2g = saved
    t = h.shape[0]
    tm = _row_tile(t, 512)
    nsh, _, wsh = w1g.shape
    wide = _rows(tm, D_FF)
    (da,) = _mm_rows(f"mlp_du{l}", tm, dh, w2g, 'nt_rows', [(_sds((t, D_FF), BF16), wide)], extras=[(u, wide)],
                     epi=lambda acc, uv: (2.0 * jnp.sqrt(uv.astype(F32)) * acc,))
    tw = _row_tile(t, 512)
    (dw2,) = _mm(f"mlp_dw2{l}", (1, 1, t // tw),
                 u, pl.BlockSpec((tw, D_FF), lambda i, j, k: (k, 0)),
                 dh, pl.BlockSpec((tw, D_MODEL), lambda i, j, k: (k, 0)), (0, 0),
                 [(_sds((D_FF, D_MODEL), BF16), pl.BlockSpec((D_FF, D_MODEL), lambda i, j, k: (0, 0)))])
    dw2 = dw2.reshape(nsh, wsh, D_MODEL)
    (dw1,) = _mm(f"mlp_dw1{l}", (1, 1, t // tw),
                 hn, pl.BlockSpec((tw, D_MODEL), lambda i, j, k: (k, 0)),
                 da, pl.BlockSpec((tw, D_FF), lambda i, j, k: (k, 0)), (0, 0),
                 [(_sds((nsh, D_MODEL, wsh), BF16), pl.BlockSpec((nsh, D_MODEL, wsh), lambda i, j, k: (0, 0, 0)))],
                 split=wsh, deps=emit_w2(dw2) if emit_w2 else ())
    row = _rows(tm, D_MODEL)
    dh_in, dg = _mm_rows(f"mlp_dhn{l}", tm, da, w1g, 'nt_cols', _norm_bwd_outs(t, tm),
                         extras=[(h, row), (dh, row), (norm_g, VEC)], epi=_norm_bwd_epi,
                         deps=emit_w1(dw1) if emit_w1 else ())
    return dh_in, jnp.sum(dg, axis=0), dw1, dw2


def _ple_fwd(l, h, hn, p, wg, wp, next_gain, target=None):
    t = h.shape[0]
    tm = _row_tile(t, 512)
    row = pl.BlockSpec((tm, D_MODEL), lambda i, j, k: (i, 0))
    full = lambda r: pl.BlockSpec((r, D_MODEL), lambda i, j, k: (0, 0))
    f32_row, bf_row = (_sds((t, D_MODEL), F32), row), (_sds((t, D_MODEL), BF16), row)
    common = [(h, row), (p, pl.BlockSpec((None, None, tm, PLE_DIM), lambda i, j, k: (l, 0, i, 0))),
              (wp, full(PLE_DIM))]
    if target is not None:
        def loss_epi(acc, hv, pv, wpv, tv):
            gt = _sigmoid(acc)
            ev = _dot(_bf(pv), wpv, 1, 0)
            err = hv + gt * ev - tv
            sq = jnp.sum(jnp.sum(err * err, axis=-1, keepdims=True), axis=0, keepdims=True)
            return err / D_MODEL, gt, ev, jnp.broadcast_to(sq, (8, 128))

        dy, gate, e, sq = _mm(f"ple_gate{l}", (t // tm, 1, 1), hn, row, wg, full(D_MODEL), (1, 0),
                              [f32_row, bf_row, bf_row, (_sds((t // tm, 8, 128), F32),
                                                         pl.BlockSpec((None, 8, 128), lambda i, j, k: (i, 0, 0)))],
                              extras=common + [(target, row)], epi=loss_epi)
        return dy, jnp.sum(sq, axis=0), (h, hn, gate, e)

    def gate_epi(acc, hv, pv, wpv, *gain):
        gt = _sigmoid(acc)
        ev = _dot(_bf(pv), wpv, 1, 0)
        h_new = hv + gt * ev
        if not gain:
            return h_new, gt, ev
        r = lax.rsqrt(jnp.mean(h_new * h_new, axis=-1, keepdims=True) + EPS)
        return h_new, gt, ev, h_new * r * gain[0]

    res = _mm(f"ple_gate{l}", (t // tm, 1, 1), hn, row, wg, full(D_MODEL), (1, 0),
              [f32_row, bf_row, bf_row] + ([bf_row] if next_gain is not None else []),
              extras=common + ([(next_gain, VEC)] if next_gain is not None else []), epi=gate_epi)
    h_out, gate, e = res[0], res[1], res[2]
    return h_out, (res[3] if next_gain is not None else None), (h, hn, gate, e)


def _ple_bwd(l, dh, saved, p, norm_g, wg, deps=(), emit=None):
    h, hn, gate, e = saved
    t = h.shape[0]
    tm = _row_tile(t)
    tk = _row_tile(t, 512)
    de, dz = _ple_gate_bwd(f"ple_gate_bwd{l}", dh, gate, e)
    full = lambda r: pl.BlockSpec((r, D_MODEL), lambda i, j, k: (0, 0))
    rowk = pl.BlockSpec((tk, D_MODEL), lambda i, j, k: (k, 0))
    (dwp,) = _mm(f"ple_dwp{l}", (1, 1, t // tk),
                 p, pl.BlockSpec((None, None, tk, PLE_DIM), lambda i, j, k: (l, 0, k, 0)),
                 de, rowk, (0, 0), [(_sds((PLE_DIM, D_MODEL), BF16), full(PLE_DIM))], deps=deps)
    (dwg,) = _mm(f"ple_dwg{l}", (1, 1, t // tk), hn, rowk, dz, rowk, (0, 0),
                 [(_sds((D_MODEL, D_MODEL), BF16), full(D_MODEL))])
    row = pl.BlockSpec((tm, D_MODEL), lambda i, j, k: (i, 0))
    dh_in, dg = _mm(f"ple_dhn{l}", (t // tm, 1, 1), dz, row, wg, full(D_MODEL), (1, 1),
                    _norm_bwd_outs(t, tm), extras=[(h, row), (dh, row), (norm_g, VEC)], epi=_norm_bwd_epi,
                    deps=emit(dwg, dwp) if emit else ())
    return dh_in, jnp.sum(dg, axis=0), dwg, dwp


def _ret_layer_fwd(x, norm_g, wri, fetch_wro, gn, cos, sin, next_gain, hn=None, deps=()):
    t = x.shape[0]
    tm = _row_tile(t)
    nsh, _, wsh = wri.shape
    if hn is None:
        hn = _rms_fwd("mix_norm0", x, norm_g)
    tp = _row_tile(t, 512)
    (proj,) = _mm_rows("ret_in", tp, hn, wri, 'nn_cols', [(_sds((t, RET_IN), BF16), _rows(tp, RET_IN))], deps=deps)
    gated, outp, states = _ret_fwd(proj, cos, sin, gn)
    wro = fetch_wro((gated,))
    row = _rows(tp, D_MODEL)
    more, epi = _residual_epi(next_gain)
    h1, hn_next = _mm_rows("ret_out", tp, gated, wro.reshape(RET_HEADS, RET_DV, D_MODEL), 'nn_rows',
                           _residual_outs(t, row, next_gain), extras=[(x, row)] + more, epi=epi)
    return h1, hn_next, (x, hn, proj, gated, outp, states, wro)


def _ret_layer_bwd(dh, saved, norm_g, wri, gn, cos, sin, emit_out, emit_in, deps=()):
    x, hn, proj, gated, outp, states, wro = saved
    t = x.shape[0]
    tm = _row_tile(t)
    tk = _row_tile(t, 512)
    nsh, _, wsh = wri.shape
    tg = _row_tile(t, 512)
    vw = _rows(tg, RET_V_W)
    dout, dgate, dgn = _mm_rows(
        "ret_dgate", tg, dh, wro.reshape(RET_HEADS, RET_DV, D_MODEL), 'nt_rows',
        [(_sds((t, RET_V_W), BF16), vw), (_sds((t, RET_V_W), BF16), vw),
         (_sds((t // tg, 1, RET_V_W), F32), pl.BlockSpec((None, 1, RET_V_W), lambda i, j, k: (i, 0, 0)))],
        extras=[(outp, vw), (proj, pl.BlockSpec((tg, RET_V_W), lambda i, j, k: (i, (RET_IN - RET_V_W) // RET_V_W))),
                (gn.reshape(1, RET_V_W), pl.BlockSpec((1, RET_V_W), lambda i, j, k: (0, 0)))],
        epi=_ret_gate_bwd_epi, deps=deps)
    dgn = jnp.sum(dgn, axis=0)
    (dwro,) = _mm("ret_dwro", (1, 1, t // tk),
                  gated, pl.BlockSpec((tk, RET_V_W), lambda i, j, k: (k, 0)),
                  dh, pl.BlockSpec((tk, D_MODEL), lambda i, j, k: (k, 0)), (0, 0),
                  [(_sds((RET_V_W, D_MODEL), BF16), pl.BlockSpec((RET_V_W, D_MODEL), lambda i, j, k: (0, 0)))])
    dproj = _ret_bwd(proj, cos, sin, states, dout, dgate, deps=emit_out(dwro))
    half = nsh // 2
    (dwri,) = _mm("ret_dwri", (2, 1, t // tk),
                  hn, pl.BlockSpec((tk, D_MODEL), lambda i, j, k: (k, 0)),
                  dproj, pl.BlockSpec((tk, half * wsh), lambda i, j, k: (k, i)), (0, 0),
                  [(_sds((nsh, D_MODEL, wsh), BF16), pl.BlockSpec((half, D_MODEL, wsh), lambda i, j, k: (i, 0, 0)))],
                  split=wsh)
    deps = emit_in(dwri)
    td = _row_tile(t, 256)
    row = _rows(td, D_MODEL)
    dx, dg = _mm_rows("ret_dhn", td, dproj, wri, 'nt_cols', _norm_bwd_outs(t, td),
                      extras=[(x, row), (dh, row), (norm_g, VEC)], epi=_norm_bwd_epi, deps=deps)
    return dx, jnp.sum(dg, axis=0), dgn.reshape(RET_HEADS, RET_DV)


def _mla_layer_fwd(h, hn, fetch, qa, kva, gq, gk, tabs, next_gain):
    t = h.shape[0]
    tm = _row_tile(t)
    row = pl.BlockSpec((tm, D_MODEL), lambda i, j, k: (i, 0))
    wmi = fetch('mla_in', (h,))['mla_w_in']
    (proj2,) = _mm("mla_in", (t // tm, 1, 1), hn, row,
                   wmi, pl.BlockSpec((D_MODEL, MLA_IN_PAD), lambda i, j, k: (0, 0)), (1, 0),
                   [(_sds((t, MLA_IN_PAD), F32), pl.BlockSpec((tm, MLA_IN_PAD), lambda i, j, k: (i, 0)))])
    cq, ckv = _mla_mid(proj2, qa, kva)
    up = fetch('mla_up', (cq,))
    wuq, wukv = up['mla_w_uq'], up['mla_w_ukv']
    qh, kh, vh = _mla_prep(cq, ckv, wuq, wukv, proj2, gq, gk, tabs)
    o, lse = _attn_fwd(qh, kh, vh)
    wmo = fetch('mla_out', (o,))['mla_w_out']
    more, epi = _residual_epi(next_gain)
    h_out, hn_next = _mm("mla_out", (t // tm, 1, 1), o, row,
                         wmo, pl.BlockSpec((D_MODEL, D_MODEL), lambda i, j, k: (0, 0)), (1, 0),
                         _residual_outs(t, row, next_gain), extras=[(h, row)] + more, epi=epi)
    return h_out, hn_next, (h, hn, proj2, cq, ckv, qh, kh, vh, o, lse), (wmi, wuq, wukv, wmo)


def _mla_layer_bwd(dh, saved, norm_g, wmi, qa, kva, wuq, wukv, gq, gk, wmo, tabs, deps=()):
    h, hn, proj2, cq, ckv, qh, kh, vh, o, lse = saved
    t = h.shape[0]
    tm = _row_tile(t)
    tk = _row_tile(t, 512)
    row = pl.BlockSpec((tm, D_MODEL), lambda i, j, k: (i, 0))
    rowk = pl.BlockSpec((tk, D_MODEL), lambda i, j, k: (k, 0))
    sq = pl.BlockSpec((D_MODEL, D_MODEL), lambda i, j, k: (0, 0))
    (dob,) = _mm("mla_do", (t // tm, 1, 1), dh, row, wmo, sq, (1, 1), [(_sds((t, D_MODEL), BF16), row)], deps=deps)
    (dwmo,) = _mm("mla_dwo", (1, 1, t // tk), o, rowk, dh, rowk, (0, 0), [(_sds((D_MODEL, D_MODEL), BF16), sq)])
    dqt, dkh, dvh = _attn_bwd(qh, kh, vh, dob, o, lse)
    dq, dkv, dkr, dgq, dgk = _mla_prep_bwd(cq, ckv, wuq, wukv, proj2, gq, gk, tabs, dqt, dkh, dvh)

    wide = MLA_HEADS * MLA_HD_PAD
    widek = pl.BlockSpec((tk, wide), lambda i, j, k: (k, 0))
    (dwuq,) = _mm("mla_dwuq", (1, 1, t // tk),
                  cq, pl.BlockSpec((tk, MLA_Q_RANK), lambda i, j, k: (k, 0)), dq, widek, (0, 0),
                  [(_sds((MLA_HEADS, MLA_Q_RANK, MLA_HD_PAD), BF16),
                    pl.BlockSpec((MLA_HEADS, MLA_Q_RANK, MLA_HD_PAD), lambda i, j, k: (0, 0, 0)))], split=MLA_HD_PAD)
    (dwukv,) = _mm("mla_dwukv", (1, 1, t // tk),
                   ckv, pl.BlockSpec((tk, MLA_KV_RANK), lambda i, j, k: (k, 0)), dkv, widek, (0, 0),
                   [(_sds((MLA_HEADS, MLA_KV_RANK, MLA_HD_PAD), BF16),
                     pl.BlockSpec((MLA_HEADS, MLA_KV_RANK, MLA_HD_PAD), lambda i, j, k: (0, 0, 0)))],
                   split=MLA_HD_PAD)
    side_by_side = lambda wg: wg.transpose(1, 0, 2).reshape(wg.shape[1], wide)
    widei = pl.BlockSpec((tm, wide), lambda i, j, k: (i, 0))
    (dcq,) = _mm("mla_dcq", (t // tm, 1, 1), dq, widei,
                 side_by_side(wuq), pl.BlockSpec((MLA_Q_RANK, wide), lambda i, j, k: (0, 0)), (1, 1),
                 [(_sds((t, MLA_Q_RANK), F32), pl.BlockSpec((tm, MLA_Q_RANK), lambda i, j, k: (i, 0)))])
    (dckv,) = _mm("mla_dckv", (t // tm, 1, 1), dkv, widei,
                  side_by_side(wukv), pl.BlockSpec((MLA_KV_RANK, wide), lambda i, j, k: (0, 0)), (1, 1),
                  [(_sds((t, MLA_KV_RANK), F32), pl.BlockSpec((tm, MLA_KV_RANK), lambda i, j, k: (i, 0)))])
    dproj2, dqa, dkva = _mla_mid_bwd(proj2, qa, kva, dcq, dckv, dkr)
    win = pl.BlockSpec((D_MODEL, MLA_IN_PAD), lambda i, j, k: (0, 0))
    (dwmi,) = _mm("mla_dwin", (1, 1, t // tk), hn, rowk,
                  dproj2, pl.BlockSpec((tk, MLA_IN_PAD), lambda i, j, k: (k, 0)), (0, 0),
                  [(_sds((D_MODEL, MLA_IN_PAD), BF16), win)])
    dh_in, dg = _mm("mla_dhn", (t // tm, 1, 1),
                    dproj2, pl.BlockSpec((tm, MLA_IN_PAD), lambda i, j, k: (i, 0)), wmi, win, (1, 1),
                    _norm_bwd_outs(t, tm), extras=[(h, row), (dh, row), (norm_g, VEC)], epi=_norm_bwd_epi)
    return dh_in, dict(mix=jnp.sum(dg, axis=0), wmi=dwmi, qa=dqa, kva=dkva, wuq=dwuq, wukv=dwukv, gq=dgq, gk=dgk,
                       wmo=dwmo)


def _local_step(x, p, target, w, fetch, emit=lambda group: ()):
    t = x.shape[0]
    cos_r, sin_r, tabs = w['tables'] if 'tables' in w else _rope_tables(t, 0.0)
    row = lambda a, i: a[i:i + 1]

    h1, hn1, s_ret = _ret_layer_fwd(x, row(w['mix_norm'], 0), w['ret_w_in'],
                                    lambda after: fetch('ret_out', after)['ret_w_out'], w['ret_gn'], cos_r, sin_r,
                                    row(w['mlp_norm'], 0), hn=w.get('hn0'), deps=w['deps'])
    h2, hn2, s_mlp0 = _mlp_fwd(0, h1, hn1, fetch('mlp_w1_0', (h1,))['mlp_w1'],
                               lambda after: fetch('mlp_w2_0', after)['mlp_w2'], row(w['ple_norm'], 0))
    w0 = fetch('ple_0', (h2,))
    h3, hn3, s_ple0 = _ple_fwd(0, h2, hn2, p, w0['ple_gate_w'], w0['ple_proj_w'], row(w['mix_norm'], 1))
    h4, hn4, s_mla, (wmi, wuq, wukv, wmo) = _mla_layer_fwd(
        h3, hn3, fetch, w['mla_q_a_norm'], w['mla_kv_a_norm'], w['mla_q_norm'], w['mla_k_norm'], tabs,
        row(w['mlp_norm'], 1))
    mla_w = (wmi, w['mla_q_a_norm'], w['mla_kv_a_norm'], wuq, wukv, w['mla_q_norm'], w['mla_k_norm'], wmo, tabs)
    w1 = fetch('layer_1', (h4,))
    h5, hn5, s_mlp1 = _mlp_fwd(1, h4, hn4, w1['mlp_w1'], lambda after: w1['mlp_w2'], row(w['ple_norm'], 1))
    dy, sq_err, s_ple1 = _ple_fwd(1, h5, hn5, p, w1['ple_gate_w'], w1['ple_proj_w'], None, target)

    n = N_DEV
    colsh = lambda a: a.reshape(a.shape[0], n, a.shape[1] // n).transpose(1, 0, 2)
    rowsh = lambda a: a.reshape(n, a.shape[0] // n, a.shape[1])
    big = {}

    def emit_group(group):
        big.update(group)
        return emit(group)

    dh5, dg_ple1, dwg1, dwp1 = _ple_bwd(1, dy, s_ple1, p, row(w['ple_norm'], 1), w1['ple_gate_w'])
    dh4, dg_mlp1, dw1_1, dw2_1 = _mlp_bwd(1, dh5, s_mlp1, row(w['mlp_norm'], 1))
    deps = emit_group({('ple_gate_w', 1): rowsh(dwg1), ('ple_proj_w', 1): colsh(dwp1),
                       ('mlp_w2', 1): dw2_1, ('mlp_w1', 1): dw1_1})
    dh3, gm = _mla_layer_bwd(dh4, s_mla, row(w['mix_norm'], 1), *mla_w, deps=deps)
    deps = emit_group({('mla_w_out', 0): rowsh(gm['wmo']), ('mla_w_uq', 0): _gather_rope(gm['wuq']),
                       ('mla_w_ukv', 0): gm['wukv'], ('mla_w_in', 0): rowsh(_gather_rope(gm['wmi']))})
    dh2, dg_ple0, _, _ = _ple_bwd(
        0, dh3, s_ple0, p, row(w['ple_norm'], 0), w0['ple_gate_w'], deps=deps,
        emit=lambda dwg, dwp: emit_group({('ple_gate_w', 0): rowsh(dwg), ('ple_proj_w', 0): colsh(dwp)}))
    dh1, dg_mlp0, _, _ = _mlp_bwd(0, dh2, s_mlp0, row(w['mlp_norm'], 0),
                                  emit_w2=lambda dw2: emit_group({('mlp_w2', 0): dw2}),
                                  emit_w1=lambda dw1: emit_group({('mlp_w1', 0): dw1}))
    dx, dg_mix0, dgn = _ret_layer_bwd(
        dh1, s_ret, row(w['mix_norm'], 0), w['ret_w_in'], w['ret_gn'], cos_r, sin_r,
        lambda dwro: emit_group({('ret_w_out', 0): rowsh(dwro)}),
        lambda dwri: emit_group({('ret_w_in', 0): dwri}))

    small = dict(
        mix_norm=[dg_mix0, gm['mix']], mlp_norm=[dg_mlp0, dg_mlp1], ple_norm=[dg_ple0, dg_ple1],
        ret_gn=dgn, mla_q_a_norm=gm['qa'], mla_kv_a_norm=gm['kva'], mla_q_norm=gm['gq'], mla_k_norm=gm['gk'],
    )
    return sq_err, dx, big, small


def _my_place():
    x, y, c = lax.axis_index("x"), lax.axis_index("y"), lax.axis_index("c")
    return x, y, c


def _flat(px, py, pc):
    return 4 * px + 2 * py + pc


def _peer(x, y, c, r):
    return (1 - x if r & 4 else x, 1 - y if r & 2 else y, 1 - c if r & 1 else c)


HBM = pl.BlockSpec(memory_space=pltpu.HBM)
SEMS = pl.BlockSpec(memory_space=pltpu.SEMAPHORE)
SIDE_EFFECT = pltpu.SideEffectType.DATAFLOW_SIDE_EFFECTING


def _rs_copies(x, y, c, srcs, lands, send_sems, recv_sems):
    copies = []
    for a in range(len(srcs)):
        for r in range(1, N_DEV):
            peer = _peer(x, y, c, r)
            k = a * (N_DEV - 1) + r - 1
            copies.append(pltpu.make_async_remote_copy(
                src_ref=srcs[a].at[_flat(*peer)], dst_ref=lands[a].at[r - 1],
                send_sem=send_sems.at[k], recv_sem=recv_sems.at[k], device_id=peer, device_id_type=MESH))
    return copies


def _rs_start(name, arrays):
    n = len(arrays)
    hbm = lambda a: pltpu.with_memory_space_constraint(a, pltpu.HBM)
    lands = [hbm(lax.empty((N_DEV - 1,) + a.shape[1:], a.dtype)) for a in arrays]

    def body(*refs):
        srcs, lnd = refs[:n], refs[n:2 * n]
        send_sems, recv_sems = refs[2 * n], refs[2 * n + 1]
        token = refs[-1]
        for cp in _rs_copies(*_my_place(), srcs, lnd, send_sems, recv_sems):
            cp.start()
        token[...] = jnp.zeros_like(token)

    outs = pl.pallas_call(
        body, name=name,
        in_specs=[HBM] * (2 * n),
        out_specs=[SEMS, SEMS] + [HBM] * (2 * n) + [pl.BlockSpec(memory_space=pltpu.VMEM)],
        out_shape=[pltpu.SemaphoreType.DMA((n * (N_DEV - 1),)), pltpu.SemaphoreType.DMA((n * (N_DEV - 1),))]
        + [pltpu.HBM(a.shape, a.dtype) for a in arrays] + [pltpu.HBM(l.shape, l.dtype) for l in lands]
        + [_sds((8, 128), F32)],
        input_output_aliases={i: 2 + i for i in range(2 * n)},
        compiler_params=pltpu.CompilerParams(has_side_effects=SIDE_EFFECT),
    )(*[hbm(a) for a in arrays], *lands)
    return outs[0], outs[1], outs[2:2 + n], outs[2 + n:2 + 2 * n], outs[-1]


def _rs_wait(name, send_sems, recv_sems, srcs, lands, after):
    n = len(srcs)

    def body(*refs):
        src_refs, lnd = refs[:n], refs[n:2 * n]
        send, recv = refs[2 * n], refs[2 * n + 1]
        for cp in _rs_copies(*_my_place(), src_refs, lnd, send, recv):
            cp.wait_send()
            cp.wait_recv()

    outs = pl.pallas_call(
        body, name=name,
        in_specs=[HBM] * (2 * n) + [SEMS, SEMS] + [ANY] * len(after),
        out_specs=[HBM] * (2 * n),
        out_shape=[pltpu.HBM(a.shape, a.dtype) for a in list(srcs) + list(lands)],
        input_output_aliases={i: i for i in range(2 * n)},
        compiler_params=pltpu.CompilerParams(has_side_effects=SIDE_EFFECT),
    )(*srcs, *lands, send_sems, recv_sems, *after)
    return outs[:n], outs[n:]


SMALL_PACK_ROWS = 16


def _all_reduce_small(rows, deps=()):
    n = len(rows)

    def body(*refs):
        ins = refs[:n]
        out_ref, mine, buf, send_sems, recv_sems = refs[n + len(deps):]
        x, y, c = _my_place()
        mine[...] = jnp.zeros_like(mine)
        for (r0, a), ref in zip(rows, ins):
            mine[r0:r0 + a.shape[0], 0:a.shape[1]] = ref[...]
        buf[_flat(x, y, c)] = mine[...]
        copies = []
        for r in range(1, N_DEV):
            peer = _peer(x, y, c, r)
            send = pltpu.make_async_remote_copy(
                src_ref=mine, dst_ref=buf.at[_flat(x, y, c)],
                send_sem=send_sems.at[r - 1], recv_sem=recv_sems.at[r - 1], device_id=peer, device_id_type=MESH)
            send.start()
            recv = pltpu.make_async_remote_copy(
                src_ref=mine, dst_ref=buf.at[_flat(*peer)],
                send_sem=send_sems.at[r - 1], recv_sem=recv_sems.at[r - 1], device_id=peer, device_id_type=MESH)
            copies.append((send, recv))
        for send, recv in copies:
            send.wait_send()
            recv.wait_recv()
        acc = buf[0]
        for s in range(1, N_DEV):
            acc = acc + buf[s]
        out_ref[...] = acc

    vm = pl.BlockSpec(memory_space=pltpu.VMEM)
    shape = (SMALL_PACK_ROWS, D_MODEL)
    return pl.pallas_call(
        body, name="all_reduce_small", in_specs=[vm] * n + [ANY] * len(deps), out_specs=vm,
        out_shape=_sds(shape, F32),
        scratch_shapes=[pltpu.VMEM(shape, F32), pltpu.VMEM((N_DEV,) + shape, F32),
                        pltpu.SemaphoreType.DMA((7,)), pltpu.SemaphoreType.DMA((7,))],
    )(*[a for _, a in rows], *deps)


def _adamw_math(w, g, m, v):
    m = ADAM_B1 * m + (1.0 - ADAM_B1) * g
    v = ADAM_B2 * v + (1.0 - ADAM_B2) * (g * g)
    m_hat = m / (1.0 - ADAM_B1 ** ADAM_STEP)
    v_hat = v / (1.0 - ADAM_B2 ** ADAM_STEP)
    delta = -ADAM_LR * (m_hat / (jnp.sqrt(v_hat) + ADAM_EPS) + ADAM_WD * w)
    return delta, m, v


def _adamw_big(name, w, m, v, srcs, lands, me):
    nl, rows, cols = w.shape
    tr = next(cand for cand in (256, 128, 64, 32, 16, 8) if rows % cand == 0)

    def body(me_ref, w_ref, m_ref, v_ref, *rest):
        src_refs, land_refs = rest[:nl], rest[nl:2 * nl]
        g_ref, d_ref, mo_ref, vo_ref = rest[2 * nl:]
        for layer in range(nl):
            @pl.when(pl.program_id(0) == layer)
            def _():
                g = src_refs[layer][...].astype(F32)
                for s in range(N_DEV - 1):
                    g = g + land_refs[layer][s].astype(F32)
                delta, mn, vn = _adamw_math(w_ref[...], g, m_ref[...], v_ref[...])
                g_ref[...] = g
                d_ref[...] = delta
                mo_ref[...] = mn
                vo_ref[...] = vn

    blk = pl.BlockSpec((None, tr, cols), lambda l, i, me_ref: (l, i, 0))
    at = lambda layer, l, i: jnp.where(l == layer, i, 0)
    own = [pl.BlockSpec((None, tr, cols), functools.partial(lambda layer, l, i, me_ref: (me_ref[0], at(layer, l, i), 0),
                                                            layer)) for layer in range(nl)]
    peers = [pl.BlockSpec((N_DEV - 1, tr, cols), functools.partial(lambda layer, l, i, me_ref: (0, at(layer, l, i), 0),
                                                                   layer)) for layer in range(nl)]
    return pl.pallas_call(
        body, name=name,
        grid_spec=pltpu.PrefetchScalarGridSpec(
            num_scalar_prefetch=1, grid=(nl, rows // tr),
            in_specs=[blk, blk, blk] + own + peers, out_specs=[blk] * 4),
        out_shape=[_sds((nl, rows, cols), F32)] * 4,
        compiler_params=_cparams(("arbitrary", "arbitrary")),
    )(me, w, m, v, *srcs, *lands)


def _adamw_small(ws, gs, ms, vs):
    n = len(ws)

    def body(*refs):
        w_refs, g_refs, m_refs, v_refs = (refs[i * n:(i + 1) * n] for i in range(4))
        d_out, m_out, v_out = (refs[(4 + i) * n:(5 + i) * n] for i in range(3))
        for i in range(n):
            delta, mn, vn = _adamw_math(w_refs[i][...], g_refs[i][...], m_refs[i][...], v_refs[i][...])
            d_out[i][...] = delta
            m_out[i][...] = mn
            v_out[i][...] = vn

    vm = pl.BlockSpec(memory_space=pltpu.VMEM)
    outs = pl.pallas_call(
        body, name="adamw_small", in_specs=[vm] * (4 * n), out_specs=[vm] * (3 * n),
        out_shape=[_sds(a.shape, F32) for a in ws] * 3,
    )(*ws, *gs, *ms, *vs)
    return outs[:n], outs[n:2 * n], outs[2 * n:]


def _pad_to(a, rows, cols):
    return jnp.pad(a, ((0, rows - a.shape[0]), (0, cols - a.shape[1])))


def _place_own(blocks):
    me = _flat(*_my_place())
    return [lax.dynamic_update_slice(lax.empty((N_DEV,) + b.shape, b.dtype), b[None], (me,) + (0,) * b.ndim)
            for b in blocks]


def _ag_copies(x, y, c, blocks, bufs, send_sems, recv_sems, arriving):
    copies = []
    for a in range(len(blocks)):
        for r in range(1, N_DEV):
            peer = _peer(x, y, c, r)
            k = a * (N_DEV - 1) + r - 1
            copies.append(pltpu.make_async_remote_copy(
                src_ref=blocks[a], dst_ref=bufs[a].at[_flat(*(peer if arriving else (x, y, c)))],
                send_sem=send_sems.at[k], recv_sem=recv_sems.at[k], device_id=peer, device_id_type=MESH))
    return copies


def _ag_start(groups, after):
    flat = [pair for g in groups for pair in g]
    n, ng = len(flat), len(groups)
    hbm = lambda a: pltpu.with_memory_space_constraint(a, pltpu.HBM)

    def body(*refs):
        blocks, bufs = refs[:n], refs[n:2 * n]
        sems = refs[2 * n + len(after):2 * n + len(after) + 2 * ng]
        x, y, c = _my_place()
        at = 0
        for gi, g in enumerate(groups):
            for cp in _ag_copies(x, y, c, blocks[at:at + len(g)], bufs[at:at + len(g)], sems[2 * gi],
                                 sems[2 * gi + 1], arriving=False):
                cp.start()
            at += len(g)
        refs[-1][...] = jnp.zeros_like(refs[-1])

    sem_shapes = [pltpu.SemaphoreType.DMA((len(g) * (N_DEV - 1),)) for g in groups for _ in range(2)]
    outs = pl.pallas_call(
        body, name="gather_start",
        in_specs=[HBM] * (2 * n) + [ANY] * len(after),
        out_specs=[SEMS] * (2 * ng) + [HBM] * (2 * n) + [pl.BlockSpec(memory_space=pltpu.VMEM)],
        out_shape=sem_shapes + [pltpu.HBM(b.shape, b.dtype) for b, _ in flat]
        + [pltpu.HBM(u.shape, u.dtype) for _, u in flat] + [_sds((8, 128), F32)],
        input_output_aliases={i: 2 * ng + i for i in range(2 * n)},
        compiler_params=pltpu.CompilerParams(has_side_effects=SIDE_EFFECT),
    )(*[hbm(b) for b, _ in flat], *[hbm(u) for _, u in flat], *after)
    blocks_thru, bufs_thru = outs[2 * ng:2 * ng + n], outs[2 * ng + n:2 * ng + 2 * n]
    started, at = [], 0
    for gi, g in enumerate(groups):
        started.append((outs[2 * gi], outs[2 * gi + 1], blocks_thru[at:at + len(g)], bufs_thru[at:at + len(g)]))
        at += len(g)
    return started, outs[-1]


def _ag_wait(name, send_sems, recv_sems, blocks, bufs, after):
    n = len(blocks)

    def body(*refs):
        for cp in _ag_copies(*_my_place(), refs[:n], refs[n:2 * n], refs[2 * n], refs[2 * n + 1], arriving=True):
            cp.wait_send()
            cp.wait_recv()

    outs = pl.pallas_call(
        body, name=name,
        in_specs=[HBM] * (2 * n) + [SEMS, SEMS] + [ANY] * len(after),
        out_specs=[HBM] * (2 * n),
        out_shape=[pltpu.HBM(a.shape, a.dtype) for a in list(blocks) + list(bufs)],
        input_output_aliases={i: i for i in range(2 * n)},
        compiler_params=pltpu.CompilerParams(has_side_effects=SIDE_EFFECT),
    )(*blocks, *bufs, send_sems, recv_sems, *after)
    return outs[n:]


def _split_call(name, body, thru, sems_in, new_sems, after):
    n, ns, nn = len(thru), len(sems_in), len(new_sems)
    hbm = lambda a: pltpu.with_memory_space_constraint(a, pltpu.HBM)

    def wrapped(*refs):
        body(refs[:n], refs[n:n + ns], refs[n + ns + len(after):n + ns + len(after) + nn])
        refs[-1][...] = jnp.zeros_like(refs[-1])

    outs = pl.pallas_call(
        wrapped, name=name,
        in_specs=[HBM] * n + [SEMS] * ns + [ANY] * len(after),
        out_specs=[SEMS] * nn + [HBM] * n + [pl.BlockSpec(memory_space=pltpu.VMEM)],
        out_shape=[pltpu.SemaphoreType.DMA((k,)) for k in new_sems] + [pltpu.HBM(a.shape, a.dtype) for a in thru]
        + [_sds((8, 128), F32)],
        input_output_aliases={i: nn + i for i in range(n)},
        compiler_params=pltpu.CompilerParams(has_side_effects=SIDE_EFFECT),
    )(*[hbm(a) for a in thru], *sems_in, *after)
    return list(outs[:nn]), list(outs[nn:nn + n]), outs[-1]


def _first_gather(blocks, bufs, overlap):
    n = len(blocks)

    def copies(refs, s1, r1, s2, r2):
        x, y, c = _my_place()
        me, sibling = (x, y, c), (x, y, 1 - c)
        chips = [(1 - x, y), (x, 1 - y), (1 - x, 1 - y)]
        blk, buf = refs[:n], refs[n:]
        out = dict(send1=[], recv1_sib=[], recv1_ici=[], send2=[], recv2=[])
        for a in range(n):
            place = lambda dev: buf[a].at[_flat(*dev)]
            for k, to in enumerate([sibling] + [(*chip, c) for chip in chips]):
                mk = lambda dst: pltpu.make_async_remote_copy(
                    src_ref=blk[a], dst_ref=dst, send_sem=s1.at[4 * a + k], recv_sem=r1.at[4 * a + k],
                    device_id=to, device_id_type=MESH)
                out['send1'].append(mk(place(me)))
                out['recv1_sib' if k == 0 else 'recv1_ici'].append(mk(place(to)))
            for j, chip in enumerate(chips):
                mk = lambda dev: pltpu.make_async_remote_copy(
                    src_ref=place(dev), dst_ref=place(dev), send_sem=s2.at[3 * a + j], recv_sem=r2.at[3 * a + j],
                    device_id=sibling, device_id_type=MESH)
                out['send2'].append(mk((*chip, c)))
                out['recv2'].append(mk((*chip, 1 - c)))
        return out

    def start(refs, sems_in, new):
        for cp in copies(refs, new[0], new[1], new[0], new[1])['send1']:
            cp.start()

    def forward(refs, sems_in, new):
        cps = copies(refs, sems_in[0], sems_in[1], new[0], new[1])
        for cp in cps['recv1_ici']:
            cp.wait_recv()
        for cp in cps['send2']:
            cp.start()

    def finish(refs, sems_in, new):
        cps = copies(refs, *sems_in)
        for cp in cps['recv1_sib'] + cps['recv2']:
            cp.wait_recv()
        for cp in cps['send1'] + cps['send2']:
            cp.wait_send()

    sems1, thru, token = _split_call("first_gather_start", start, list(blocks) + list(bufs), [], [4 * n, 4 * n], ())
    after = overlap(token)
    sems2, thru, token = _split_call("first_gather_forward", forward, thru, sems1, [3 * n, 3 * n], after)
    _, thru, _ = _split_call("first_gather_wait", finish, thru, sems1 + sems2, [], ())
    return thru[n:], token


def _prepare_weights(p, x):
    n = N_DEV
    bf = lambda a: a.astype(BF16)
    gn_pack = jnp.concatenate([
        _pad_to(p['ret_gn'][0], RET_HEADS, 128), _pad_to(p['mla_q_a_norm'], 1, 128),
        _pad_to(p['mla_kv_a_norm'], 1, 128), jnp.zeros((2, 128), F32)], axis=0)
    ple = lambda l: [bf(p['ple_gate_w'][l]), bf(p['ple_proj_w'][l])]
    names = ('ret_out', 'mlp_w1_0', 'mlp_w2_0', 'ple_0', 'mla_in', 'mla_up', 'mla_out', 'layer_1')
    later = [[bf(p['ret_w_out'][0])], [bf(p['mlp_w1'][0])], [bf(p['mlp_w2'][0])], ple(0),
             [bf(p['mla_w_in'][0])], [bf(p['mla_w_uq'][0]), bf(p['mla_w_ukv'][0])], [bf(p['mla_w_out'][0])],
             [bf(p['mlp_w1'][1]), bf(p['mlp_w2'][1])] + ple(1)]
    first = [gn_pack, bf(p['ret_w_in'][0])]
    behind = {}

    def overlap(token):
        behind['hn0'] = _rms_fwd("mix_norm0", x, p['mix_norm'][0:1], deps=(token,))
        behind['bufs'] = _place_own([b for g in later for b in g])
        behind['tables'] = _rope_tables(x.shape[0], token[0, 0])
        cos_r, sin_r, tabs = behind['tables']
        return (behind['hn0'], cos_r, sin_r, *tabs, *behind['bufs'])

    (pack, wri), token = _first_gather(first, _place_own(first), overlap)
    bufs = behind['bufs']
    groups, at = [], 0
    for g in later:
        groups.append(list(zip(g, bufs[at:at + len(g)])))
        at += len(g)
    started, token = _ag_start(groups, (token,))

    w = {k: p[k] for k in ('mix_norm', 'mlp_norm', 'ple_norm')}
    w['hn0'] = behind['hn0']
    w['tables'] = behind['tables']
    w['ret_gn'] = pack[:, :RET_HEADS, :RET_DV // n].transpose(1, 0, 2).reshape(RET_HEADS, RET_DV)
    w['mla_q_a_norm'] = pack[:, RET_HEADS, :MLA_Q_RANK // n].reshape(1, MLA_Q_RANK)
    w['mla_kv_a_norm'] = pack[:, RET_HEADS + 1, :MLA_KV_RANK // n].reshape(1, MLA_KV_RANK)
    w['ret_w_in'] = wri
    w['mla_q_norm'] = _spread_rope(p['mla_q_norm'])
    w['mla_k_norm'] = _spread_rope(p['mla_k_norm'])
    w['deps'] = (token,)

    def fetch(name, after):
        got = list(_ag_wait("gather_wait_" + name, *started[names.index(name)], after))
        if name == 'ret_out':
            return dict(ret_w_out=got[0].reshape(RET_V_W, D_MODEL))
        if name == 'mla_in':
            return dict(mla_w_in=_spread_rope(got[0].reshape(D_MODEL, MLA_IN)))
        if name == 'mla_up':
            return dict(mla_w_uq=_spread_rope(got[0]), mla_w_ukv=got[1])
        if name == 'mla_out':
            return dict(mla_w_out=got[0].reshape(D_MODEL, D_MODEL))
        out = {}
        if name in ('mlp_w1_0', 'layer_1'):
            out['mlp_w1'] = got.pop(0)
        if name in ('mlp_w2_0', 'layer_1'):
            out['mlp_w2'] = got.pop(0)
        if name in ('ple_0', 'layer_1'):
            out['ple_gate_w'] = got[0].reshape(D_MODEL, D_MODEL)
            out['ple_proj_w'] = got[1].transpose(1, 0, 2).reshape(PLE_DIM, D_MODEL)
        return out

    return w, fetch


def _small_grads(small, after):
    rows = [(0, small['mix_norm'][0]), (1, small['mix_norm'][1]), (2, small['mlp_norm'][0]),
            (3, small['mlp_norm'][1]), (4, small['ple_norm'][0]), (5, small['ple_norm'][1]),
            (6, small['ret_gn']), (10, small['mla_q_a_norm']), (11, small['mla_kv_a_norm']),
            (12, small['mla_q_norm']), (13, small['mla_k_norm']), (14, small['sq_err'])]
    gs = _all_reduce_small(rows, after)
    me = _flat(*_my_place())
    n = N_DEV
    return dict(
        sq_err=gs[14, 0],
        mix_norm=gs[0:2], mlp_norm=gs[2:4], ple_norm=gs[4:6],
        ret_gn=lax.dynamic_slice(gs, (6, me * (RET_DV // n)), (RET_HEADS, RET_DV // n)),
        mla_q_a_norm=lax.dynamic_slice(gs, (10, me * (MLA_Q_RANK // n)), (1, MLA_Q_RANK // n)),
        mla_kv_a_norm=lax.dynamic_slice(gs, (11, me * (MLA_KV_RANK // n)), (1, MLA_KV_RANK // n)),
        mla_q_norm=_gather_rope(gs[12:13, :MLA_HD_PAD]), mla_k_norm=_gather_rope(gs[13:14, :MLA_HD_PAD]))


def kernel(x, p, mix_norm, ret_w_in, ret_gn, ret_w_out, mla_w_in, mla_q_a_norm, mla_kv_a_norm, mla_w_uq, mla_w_ukv, mla_q_norm, mla_k_norm, mla_w_out, mlp_norm, mlp_w1, mlp_w2, ple_norm, ple_gate_w, ple_proj_w, loss_target, m_mix_norm, m_ret_w_in, m_ret_gn, m_ret_w_out, m_mla_w_in, m_mla_q_a_norm, m_mla_kv_a_norm, m_mla_w_uq, m_mla_w_ukv, m_mla_q_norm, m_mla_k_norm, m_mla_w_out, m_mlp_norm, m_mlp_w1, m_mlp_w2, m_ple_norm, m_ple_gate_w, m_ple_proj_w, v_mix_norm, v_ret_w_in, v_ret_gn, v_ret_w_out, v_mla_w_in, v_mla_q_a_norm, v_mla_kv_a_norm, v_mla_w_uq, v_mla_w_ukv, v_mla_q_norm, v_mla_k_norm, v_mla_w_out, v_mlp_norm, v_mlp_w1, v_mlp_w2, v_ple_norm, v_ple_gate_w, v_ple_proj_w):
    given = dict(locals())
    params = {n: given[n] for n in WEIGHTS}
    w, fetch = _prepare_weights(params, x[0])

    started = []

    def emit(group):
        keys = list(group)
        send, recv, srcs, lands, token = _rs_start(f"rs_start{len(started)}", [group[k] for k in keys])
        started.append((keys, send, recv, srcs, lands))
        return (token,)

    sq_err, grad_x, _, small = _local_step(x[0], p, loss_target[0], w, fetch, emit)
    small['sq_err'] = sq_err[0:1]

    grads, deltas, new_m, new_v = {}, {}, {}, {}
    total = {}

    def small_updates(after):
        sg = _small_grads(small, after)
        total['loss'] = 0.5 / D_MODEL * sg['sq_err']
        two_d = lambda a: a.reshape(-1, a.shape[-1])
        d_s, m_s, v_s = _adamw_small(
            [two_d(params[n]) for n in SMALL], [sg[n] for n in SMALL],
            [two_d(given["m_" + n]) for n in SMALL], [two_d(given["v_" + n]) for n in SMALL])
        for i, n in enumerate(SMALL):
            shape = params[n].shape
            grads[n], deltas[n], new_m[n], new_v[n] = (a.reshape(shape) for a in (sg[n], d_s[i], m_s[i], v_s[i]))
        return (d_s[0],)

    me = _flat(*_my_place()).astype(jnp.int32).reshape(1)
    after = (grad_x,)
    src_of, land_of = {}, {}
    for gi, (keys, send, recv, srcs, lands) in enumerate(started):
        if gi == len(started) - 1:
            after = small_updates(after)
        srcs, lands = _rs_wait(f"rs_wait{gi}", send, recv, srcs, lands, after)
        for k, s, l in zip(keys, srcs, lands):
            src_of[k], land_of[k] = s, l
        done = [n for n in BIG if n not in grads and all((n, l) in src_of for l in range(params[n].shape[0]))]
        for n in done:
            layers = range(params[n].shape[0])
            grads[n], deltas[n], new_m[n], new_v[n] = _adamw_big(
                "adamw_" + n, params[n], given["m_" + n], given["v_" + n],
                [src_of[(n, l)] for l in layers], [land_of[(n, l)] for l in layers], me)
        if done:
            after = tuple(deltas[n] for n in done)

    return (total['loss'], grad_x[None], *[grads[n] for n in WEIGHTS], *[deltas[n] for n in WEIGHTS],
            *[new_m[n] for n in WEIGHTS], *[new_v[n] for n in WEIGHTS])
```

```python
import functools

import jax
import jax.numpy as jnp
from jax import lax
from jax.experimental import pallas as pl
from jax.experimental.pallas import tpu as pltpu

F32 = jnp.float32
BF16 = jnp.bfloat16
MESH = pl.DeviceIdType.MESH
ANY = pl.BlockSpec(memory_space=pl.ANY)

N_DEV = 8
D_MODEL = 1024
CHUNK = 64
RET_BLOCK = 4 * CHUNK
EPS = 1e-6
ROPE_THETA = 10000.0
RET_HEADS = 4
RET_DK = 256
RET_DV = 512
RET_QK_W = RET_HEADS * RET_DK
RET_V_W = RET_HEADS * RET_DV
RET_IN = 2 * RET_QK_W + 2 * RET_V_W
MLA_HEADS = 8
MLA_NOPE = 128
MLA_ROPE = 64
MLA_QKD = MLA_NOPE + MLA_ROPE
MLA_VD = 128
MLA_Q_RANK = 384
MLA_KV_RANK = 256
MLA_IN = MLA_Q_RANK + MLA_KV_RANK + MLA_ROPE
MLA_IN_PAD = 768
MLA_HD_PAD = 256
D_FF = 4096
PLE_DIM = 256
ATT_SCALE = MLA_QKD ** -0.5
LOG2E = 1.4426950408889634
ATT_EXP2 = ATT_SCALE * LOG2E

ADAM_LR = 0.001
ADAM_B1 = 0.9
ADAM_B2 = 0.999
ADAM_EPS = 1e-08
ADAM_WD = 0.01
ADAM_STEP = 10

VMEM_LIMIT = 52 * 1024 * 1024
ROW_TILE = 1024
RET_ROWS = 512
ATT_BLOCK = 256
ATT_QROWS = 1024
ATT_KROWS = 1024
ATT_HEADS = 2
PREP_ROWS = 1024

WEIGHTS = ['mix_norm', 'ret_w_in', 'ret_gn', 'ret_w_out', 'mla_w_in', 'mla_q_a_norm', 'mla_kv_a_norm',
           'mla_w_uq', 'mla_w_ukv', 'mla_q_norm', 'mla_k_norm', 'mla_w_out', 'mlp_norm', 'mlp_w1', 'mlp_w2',
           'ple_norm', 'ple_gate_w', 'ple_proj_w']
BIG = ['ret_w_in', 'ret_w_out', 'mla_w_in', 'mla_w_uq', 'mla_w_ukv', 'mla_w_out', 'mlp_w1', 'mlp_w2',
       'ple_gate_w', 'ple_proj_w']
SMALL = [w for w in WEIGHTS if w not in BIG]


def _cparams(sem=None):
    return pltpu.CompilerParams(dimension_semantics=sem, vmem_limit_bytes=VMEM_LIMIT)


def _dot(a, b, ca, cb):
    return lax.dot_general(a, b, (((ca,), (cb,)), ((), ())), preferred_element_type=F32)


def _bf(v):
    return v if v.dtype == BF16 else v.astype(BF16)


def _sigmoid(z):
    return 1.0 / (1.0 + jnp.exp(-z))


def _mm(name, grid, a, a_spec, b, b_spec, contract, outs, extras=(), epi=None, deps=(), split=None):
    nk = grid[2]
    n_ex, n_out, n_dep = len(extras), len(outs), len(deps)
    acc_shape = tuple(d for d in outs[0][1].block_shape if d is not None)
    if split is not None:
        acc_shape = (acc_shape[1], acc_shape[0] * split)

    def body(*refs):
        a_ref, b_ref = refs[:2]
        ex_refs = refs[2:2 + n_ex]
        out_refs = refs[2 + n_ex + n_dep:2 + n_ex + n_dep + n_out]

        def product():
            return _dot(_bf(a_ref[...]), _bf(b_ref[...]), contract[0], contract[1])

        def finish(acc):
            if split is not None:
                for j in range(acc_shape[1] // split):
                    out_refs[0][j] = acc[:, j * split:(j + 1) * split].astype(out_refs[0].dtype)
                return
            acc = acc[...]
            res = epi(acc, *[r[...] for r in ex_refs]) if epi is not None else (acc,)
            for o, r in zip(out_refs, res):
                o[...] = r.astype(o.dtype)

        if nk == 1:
            finish(product())
        else:
            acc_ref = refs[-1]
            k = pl.program_id(2)

            @pl.when(k == 0)
            def _():
                acc_ref[...] = jnp.zeros_like(acc_ref)

            acc_ref[...] += product()

            @pl.when(k == nk - 1)
            def _():
                finish(acc_ref)

    return pl.pallas_call(
        body, name=name, grid=grid,
        in_specs=[a_spec, b_spec] + [s for _, s in extras] + [ANY] * n_dep,
        out_specs=[s for _, s in outs],
        out_shape=[s for s, _ in outs],
        scratch_shapes=[pltpu.VMEM(acc_shape, F32)] if nk > 1 else [],
        compiler_params=_cparams(("parallel", "parallel", "arbitrary")),
    )(a, b, *[x for x, _ in extras], *deps)


def _mm_rows(name, tm, a, w, mode, outs, extras=(), epi=None, deps=()):
    n_sh, rows, cols = w.shape
    n_ex, n_out, n_dep = len(extras), len(outs), len(deps)
    by_cols = mode in ('nn_cols', 'nt_rows')
    width = cols if mode == 'nn_cols' else rows

    def body(*refs):
        a_ref, w_ref = refs[:2]
        ex_refs = refs[2:2 + n_ex]
        out_refs = refs[2 + n_ex + n_dep:2 + n_ex + n_dep + n_out]
        if by_cols:
            av = _bf(a_ref[...])
            for s in range(n_sh):
                cs = slice(s * width, (s + 1) * width)
                acc = _dot(av, w_ref[s], 1, 0 if mode == 'nn_cols' else 1)
                res = epi(acc, *[r[:, cs] for r in ex_refs]) if epi is not None else (acc,)
                for o, r in zip(out_refs, res):
                    o[:, cs] = r.astype(o.dtype)
        else:
            chunk = rows if mode == 'nn_rows' else cols
            acc = None
            for s in range(n_sh):
                part = _dot(_bf(a_ref[:, s * chunk:(s + 1) * chunk]), w_ref[s], 1, 0 if mode == 'nn_rows' else 1)
                acc = part if acc is None else acc + part
            res = epi(acc, *[r[...] for r in ex_refs]) if epi is not None else (acc,)
            for o, r in zip(out_refs, res):
                o[...] = r.astype(o.dtype)

    t, ka = a.shape
    return pl.pallas_call(
        body, name=name, grid=(t // tm, 1, 1),
        in_specs=[pl.BlockSpec((tm, ka), lambda i, j, k: (i, 0)),
                  pl.BlockSpec((n_sh, rows, cols), lambda i, j, k: (0, 0, 0))] + [s for _, s in extras] + [ANY] * n_dep,
        out_specs=[s for _, s in outs],
        out_shape=[s for s, _ in outs],
        compiler_params=_cparams(("parallel", "arbitrary", "arbitrary")),
    )(a, w, *[x for x, _ in extras], *deps)


def _sds(shape, dtype):
    return jax.ShapeDtypeStruct(shape, dtype)


def _row_tile(t, cap=ROW_TILE):
    return min(cap, t)


def _rms_fwd(name, x, g, deps=()):
    t, d = x.shape
    tm = _row_tile(t)

    def body(x_ref, g_ref, *rest):
        o_ref = rest[-1]
        xv = x_ref[...]
        r = lax.rsqrt(jnp.mean(xv * xv, axis=-1, keepdims=True) + EPS)
        o_ref[...] = (xv * r * g_ref[...]).astype(o_ref.dtype)

    return pl.pallas_call(
        body, name=name, grid=(t // tm,),
        in_specs=[pl.BlockSpec((tm, d), lambda i: (i, 0)), pl.BlockSpec((1, d), lambda i: (0, 0))] + [ANY] * len(deps),
        out_specs=pl.BlockSpec((tm, d), lambda i: (i, 0)),
        out_shape=_sds((t, d), BF16),
        compiler_params=_cparams(("parallel",)),
    )(x, g, *deps)


def _rms_bwd_rows(dy, xv, g, n):
    r = lax.rsqrt(jnp.sum(xv * xv, axis=-1, keepdims=True) / n + EPS)
    xh = xv * r
    dxh = dy * g
    dx = r * (dxh - xh * (jnp.sum(dxh * xh, axis=-1, keepdims=True) / n))
    return dx, dy * xh


def _ple_gate_bwd(name, dh, gate, e):
    t, d = dh.shape
    tm = _row_tile(t)

    def body(dh_ref, g_ref, e_ref, de_ref, dz_ref):
        dh_v, gt = dh_ref[...], g_ref[...].astype(F32)
        de_ref[...] = (dh_v * gt).astype(BF16)
        dz_ref[...] = (dh_v * e_ref[...].astype(F32) * (gt * (1.0 - gt))).astype(BF16)

    row = pl.BlockSpec((tm, d), lambda i: (i, 0))
    return pl.pallas_call(
        body, name=name, grid=(t // tm,), in_specs=[row, row, row], out_specs=[row, row],
        out_shape=[_sds((t, d), BF16), _sds((t, d), BF16)],
        compiler_params=_cparams(("parallel",)),
    )(dh, gate, e)


def _rope_half(v, cos, sin):
    half = v.shape[-1] // 2
    v1, v2 = v[:, :half], v[:, half:]
    return jnp.concatenate([v1 * cos - v2 * sin, v2 * cos + v1 * sin], axis=-1)


def _ret_consts():
    lg = jnp.log(1.0 - 2.0 ** (-5.0 - jnp.arange(RET_HEADS, dtype=F32)))
    idx = jnp.arange(RET_BLOCK, dtype=F32)
    chunk = jnp.floor(idx / CHUNK)
    dist = idx[:, None] - idx[None, :]
    same = chunk[:, None] == chunk[None, :]
    seen = jnp.where(same, jnp.abs(dist), jnp.where(chunk[None, :] < chunk[:, None], dist, jnp.inf))
    intra = jnp.exp(lg[:, None, None] * seen)
    qdec = jnp.exp(lg[:, None] * (idx + 1.0))
    kdec = jnp.exp(lg[:, None] * (RET_BLOCK - 1.0 - idx))
    cdec = jnp.exp(lg * RET_BLOCK)
    qdec = jnp.broadcast_to(qdec[:, :, None], (RET_HEADS, RET_BLOCK, RET_DK))
    kdec = jnp.broadcast_to(kdec[:, :, None], (RET_HEADS, RET_BLOCK, RET_DK))
    cdec = jnp.broadcast_to(cdec[:, None, None], (RET_HEADS, 1, RET_DV))
    return intra, qdec, kdec, cdec


def _ret_specs(rb, rev_nb=None):
    blk = (lambda i: i) if rev_nb is None else (lambda i: rev_nb - 1 - i)
    full = lambda shape: pl.BlockSpec(shape, lambda i: (0,) * len(shape))
    return dict(
        proj=pl.BlockSpec((rb, RET_IN), lambda i: (blk(i), 0)),
        tab=pl.BlockSpec((rb, RET_DK // 2), lambda i: (blk(i), 0)),
        vw=pl.BlockSpec((rb, RET_V_W), lambda i: (blk(i), 0)),
        st=pl.BlockSpec((rb // RET_BLOCK, RET_HEADS, RET_DK, RET_DV), lambda i: (blk(i), 0, 0, 0)),
        gn=full((RET_HEADS, 1, RET_DV)),
        intra=full((RET_HEADS, RET_BLOCK, RET_BLOCK)),
        dec=full((RET_HEADS, RET_BLOCK, RET_DK)),
        cdec=full((RET_HEADS, 1, RET_DV)),
    )


def _ret_fwd(proj, cos, sin, gn):
    t = proj.shape[0]
    rb = min(RET_ROWS, t)
    cpb = rb // RET_BLOCK
    intra, qdec, kdec, cdec = _ret_consts()
    sp = _ret_specs(rb)

    def body(proj_ref, cos_ref, sin_ref, gn_ref, intra_ref, qd_ref, kd_ref, cd_ref,
             gated_ref, outp_ref, st_ref, s_ref):
        @pl.when(pl.program_id(0) == 0)
        def _():
            s_ref[...] = jnp.zeros_like(s_ref)

        def chunk(c, carry):
            rows = pl.ds(pl.multiple_of(c * RET_BLOCK, RET_BLOCK), RET_BLOCK)
            cs, sn = cos_ref[rows, :], sin_ref[rows, :]
            for h in range(RET_HEADS):
                q = proj_ref[rows, h * RET_DK:(h + 1) * RET_DK].astype(F32)
                k = proj_ref[rows, RET_QK_W + h * RET_DK:RET_QK_W + (h + 1) * RET_DK].astype(F32)
                v = proj_ref[rows, 2 * RET_QK_W + h * RET_DV:2 * RET_QK_W + (h + 1) * RET_DV]
                g = proj_ref[rows, 2 * RET_QK_W + RET_V_W + h * RET_DV:
                             2 * RET_QK_W + RET_V_W + (h + 1) * RET_DV].astype(F32)
                qr = _rope_half(q, cs, sn)
                kr = _rope_half(k, cs, sn) * (RET_DK ** -0.5)
                qb, kb, vb = qr.astype(BF16), kr.astype(BF16), v
                sc = _dot(qb, kb, 1, 1) * intra_ref[h]
                inner = _dot(sc.astype(BF16), vb, 1, 0)
                s_old = s_ref[h]
                sb = s_old.astype(BF16)
                st_ref[c, h] = sb
                cross = _dot((qr * qd_ref[h]).astype(BF16), sb, 1, 0)
                out = inner + cross
                s_ref[h] = s_old * cd_ref[h] + _dot((kr * kd_ref[h]).astype(BF16), vb, 0, 0)
                r = lax.rsqrt(jnp.mean(out * out, axis=-1, keepdims=True) + EPS)
                y = out * r * gn_ref[h]
                cols = slice(h * RET_DV, (h + 1) * RET_DV)
                gated_ref[rows, cols] = (g * _sigmoid(g) * y).astype(BF16)
                outp_ref[rows, cols] = out
            return carry

        lax.fori_loop(0, cpb, chunk, 0)

    return pl.pallas_call(
        body, name="ret_fwd", grid=(t // rb,),
        in_specs=[sp['proj'], sp['tab'], sp['tab'], sp['gn'], sp['intra'], sp['dec'], sp['dec'], sp['cdec']],
        out_specs=[sp['vw'], sp['vw'], sp['st']],
        out_shape=[_sds((t, RET_V_W), BF16), _sds((t, RET_V_W), F32),
                   _sds((t // RET_BLOCK, RET_HEADS, RET_DK, RET_DV), BF16)],
        scratch_shapes=[pltpu.VMEM((RET_HEADS, RET_DK, RET_DV), F32)],
        compiler_params=_cparams(("arbitrary",)),
    )(proj, cos, sin, gn.reshape(RET_HEADS, 1, RET_DV), intra, qdec, kdec, cdec)


def _ret_gate_bwd_epi(dgt, out, g, gn):
    g = g.astype(F32)
    r = lax.rsqrt(jnp.mean(out * out, axis=-1, keepdims=True) + EPS)
    xh = out * r
    sg = _sigmoid(g)
    dgate = dgt * (xh * gn) * (sg * (1.0 + g * (1.0 - sg)))
    dy = dgt * (g * sg)
    dxh = dy * gn
    dout = r * (dxh - xh * jnp.mean(dxh * xh, axis=-1, keepdims=True))
    return dout, dgate, jnp.sum(dy * xh, axis=0, keepdims=True)


def _ret_bwd(proj, cos, sin, states, dout, dgate, deps=()):
    t = proj.shape[0]
    rb = min(RET_ROWS, t)
    cpb = rb // RET_BLOCK
    nb = t // rb
    intra, qdec, kdec, cdec = _ret_consts()
    sp = _ret_specs(rb, rev_nb=nb)

    def body(proj_ref, cos_ref, sin_ref, intra_ref, qd_ref, kd_ref, cd_ref, st_ref, dout_ref, dgate_ref, *rest):
        dproj_ref, ds_ref = rest[len(deps):]

        @pl.when(pl.program_id(0) == 0)
        def _():
            ds_ref[...] = jnp.zeros_like(ds_ref)

        def chunk(cc, carry):
            c = cpb - 1 - cc
            rows = pl.ds(pl.multiple_of(c * RET_BLOCK, RET_BLOCK), RET_BLOCK)
            cs, sn = cos_ref[rows, :], sin_ref[rows, :]
            for h in range(RET_HEADS):
                q = proj_ref[rows, h * RET_DK:(h + 1) * RET_DK].astype(F32)
                k = proj_ref[rows, RET_QK_W + h * RET_DK:RET_QK_W + (h + 1) * RET_DK].astype(F32)
                v = proj_ref[rows, 2 * RET_QK_W + h * RET_DV:2 * RET_QK_W + (h + 1) * RET_DV]
                cols = slice(h * RET_DV, (h + 1) * RET_DV)
                qr = _rope_half(q, cs, sn)
                kr = _rope_half(k, cs, sn) * (RET_DK ** -0.5)
                qb, kb, vb = qr.astype(BF16), kr.astype(BF16), v
                qdb = (qr * qd_ref[h]).astype(BF16)
                kdb = (kr * kd_ref[h]).astype(BF16)
                doutb = dout_ref[rows, cols]
                itr = intra_ref[h]
                pb = (_dot(qb, kb, 1, 1) * itr).astype(BF16)
                dv = _dot(pb, doutb, 0, 0)
                dsc = (_dot(doutb, vb, 1, 1) * itr).astype(BF16)
                dq = _dot(dsc, kb, 1, 0)
                dk = _dot(dsc, qb, 0, 0)
                dq = dq + _dot(doutb, st_ref[c, h], 1, 1) * qd_ref[h]
                ds_new = ds_ref[h]
                dsb = ds_new.astype(BF16)
                dk = dk + _dot(vb, dsb, 1, 1) * kd_ref[h]
                dv = dv + _dot(kdb, dsb, 1, 0)
                ds_ref[h] = ds_new * cd_ref[h] + _dot(qdb, doutb, 0, 0)
                dproj_ref[rows, h * RET_DK:(h + 1) * RET_DK] = _rope_half(dq, cs, -sn).astype(BF16)
                dproj_ref[rows, RET_QK_W + h * RET_DK:RET_QK_W + (h + 1) * RET_DK] = (
                    _rope_half(dk * (RET_DK ** -0.5), cs, -sn).astype(BF16))
                dproj_ref[rows, 2 * RET_QK_W + h * RET_DV:2 * RET_QK_W + (h + 1) * RET_DV] = dv.astype(BF16)
                dproj_ref[rows, 2 * RET_QK_W + RET_V_W + h * RET_DV:
                          2 * RET_QK_W + RET_V_W + (h + 1) * RET_DV] = dgate_ref[rows, cols]
            return carry

        lax.fori_loop(0, cpb, chunk, 0)

    return pl.pallas_call(
        body, name="ret_bwd", grid=(nb,),
        in_specs=[sp['proj'], sp['tab'], sp['tab'], sp['intra'], sp['dec'], sp['dec'], sp['cdec'],
                  sp['st'], sp['vw'], sp['vw']] + [ANY] * len(deps),
        out_specs=sp['proj'],
        out_shape=_sds((t, RET_IN), BF16),
        scratch_shapes=[pltpu.VMEM((RET_HEADS, RET_DK, RET_DV), F32)],
        compiler_params=_cparams(("arbitrary",)),
    )(proj, cos, sin, intra, qdec, kdec, cdec, states, dout, dgate, *deps)


def _spread_rope(a):
    return jnp.pad(a, [(0, 0)] * (a.ndim - 1) + [(0, MLA_ROPE)])


def _gather_rope(a):
    return a[..., :a.shape[-1] - MLA_ROPE]


def _rope_tables(t, zero):
    pos = jnp.arange(t, dtype=F32)[:, None] + zero
    inv = 1.0 / (ROPE_THETA ** (jnp.arange(0, RET_DK, 2, dtype=F32) / RET_DK))
    ang = pos * inv[None, :]
    return jnp.cos(ang), jnp.sin(ang), _mla_tables(t, pos)


def _mla_tables(t, pos):
    half = MLA_ROPE // 2
    inv = 1.0 / (ROPE_THETA ** (jnp.arange(0, MLA_ROPE, 2, dtype=F32) / MLA_ROPE))
    ang = pos * inv[None, :]
    cos, sin = jnp.cos(ang), jnp.sin(ang)
    z = jnp.zeros((t, half), F32)
    c = jnp.concatenate([cos, cos, z, z], axis=1)
    s1 = jnp.concatenate([-sin, z, z, z], axis=1)
    s2 = jnp.concatenate([z, sin, z, z], axis=1)
    return c, s1, s2


def _rope_tile(r, c, s1, s2):
    return r * c + pltpu.roll(r, 96, 1) * s1 + pltpu.roll(r, 32, 1) * s2


def _mla_mid(proj2, qa, kva):
    t = proj2.shape[0]
    tm = _row_tile(t)

    def body(p_ref, qa_ref, kva_ref, cq_ref, ckv_ref):
        cq = p_ref[:, :MLA_Q_RANK]
        ckv = p_ref[:, MLA_Q_RANK:MLA_Q_RANK + MLA_KV_RANK]
        rq = lax.rsqrt(jnp.mean(cq * cq, axis=-1, keepdims=True) + EPS)
        rkv = lax.rsqrt(jnp.mean(ckv * ckv, axis=-1, keepdims=True) + EPS)
        cq_ref[...] = (cq * rq * qa_ref[...]).astype(BF16)
        ckv_ref[...] = (ckv * rkv * kva_ref[...]).astype(BF16)

    return pl.pallas_call(
        body, name="mla_mid", grid=(t // tm,),
        in_specs=[pl.BlockSpec((tm, MLA_IN_PAD), lambda i: (i, 0)),
                  pl.BlockSpec((1, MLA_Q_RANK), lambda i: (0, 0)),
                  pl.BlockSpec((1, MLA_KV_RANK), lambda i: (0, 0))],
        out_specs=[pl.BlockSpec((tm, MLA_Q_RANK), lambda i: (i, 0)),
                   pl.BlockSpec((tm, MLA_KV_RANK), lambda i: (i, 0))],
        out_shape=[_sds((t, MLA_Q_RANK), BF16), _sds((t, MLA_KV_RANK), BF16)],
        compiler_params=_cparams(("parallel",)),
    )(proj2, qa, kva)


def _mla_mid_bwd(proj2, qa, kva, dcq, dckv, dkr):
    t = proj2.shape[0]
    tm = _row_tile(t)

    def body(p_ref, qa_ref, kva_ref, dcq_ref, dckv_ref, dkr_ref, dp_ref, dqa_ref, dkva_ref):
        @pl.when(pl.program_id(0) == 0)
        def _():
            dqa_ref[...] = jnp.zeros_like(dqa_ref)
            dkva_ref[...] = jnp.zeros_like(dkva_ref)

        dxq, dgq = _rms_bwd_rows(dcq_ref[...], p_ref[:, :MLA_Q_RANK], qa_ref[...], MLA_Q_RANK)
        dxk, dgk = _rms_bwd_rows(dckv_ref[...], p_ref[:, MLA_Q_RANK:MLA_Q_RANK + MLA_KV_RANK], kva_ref[...],
                                 MLA_KV_RANK)
        dp_ref[:, :MLA_Q_RANK] = dxq.astype(BF16)
        dp_ref[:, MLA_Q_RANK:MLA_Q_RANK + MLA_KV_RANK] = dxk.astype(BF16)
        dp_ref[:, MLA_Q_RANK + MLA_KV_RANK:] = dkr_ref[...].astype(BF16)
        dqa_ref[...] += jnp.sum(dgq, axis=0, keepdims=True)
        dkva_ref[...] += jnp.sum(dgk, axis=0, keepdims=True)

    return pl.pallas_call(
        body, name="mla_mid_bwd", grid=(t // tm,),
        in_specs=[pl.BlockSpec((tm, MLA_IN_PAD), lambda i: (i, 0)),
                  pl.BlockSpec((1, MLA_Q_RANK), lambda i: (0, 0)),
                  pl.BlockSpec((1, MLA_KV_RANK), lambda i: (0, 0)),
                  pl.BlockSpec((tm, MLA_Q_RANK), lambda i: (i, 0)),
                  pl.BlockSpec((tm, MLA_KV_RANK), lambda i: (i, 0)),
                  pl.BlockSpec((tm, 128), lambda i: (i, 0))],
        out_specs=[pl.BlockSpec((tm, MLA_IN_PAD), lambda i: (i, 0)),
                   pl.BlockSpec((1, MLA_Q_RANK), lambda i: (0, 0)),
                   pl.BlockSpec((1, MLA_KV_RANK), lambda i: (0, 0))],
        out_shape=[_sds((t, MLA_IN_PAD), BF16), _sds((1, MLA_Q_RANK), F32), _sds((1, MLA_KV_RANK), F32)],
        compiler_params=_cparams(("arbitrary",)),
    )(proj2, qa, kva, dcq, dckv, dkr)


def _mla_prep_specs(t, tm):
    head = lambda w: pl.BlockSpec((None, tm, w), lambda i, h: (h, i, 0))
    return dict(
        head256=head(MLA_HD_PAD), head128=head(MLA_VD),
        cols256=pl.BlockSpec((tm, MLA_HD_PAD), lambda i, h: (i, h)),
        cq=pl.BlockSpec((tm, MLA_Q_RANK), lambda i, h: (i, 0)),
        ckv=pl.BlockSpec((tm, MLA_KV_RANK), lambda i, h: (i, 0)),
        wuq=pl.BlockSpec((None, MLA_Q_RANK, MLA_HD_PAD), lambda i, h: (h, 0, 0)),
        wukv=pl.BlockSpec((None, MLA_KV_RANK, MLA_HD_PAD), lambda i, h: (h, 0, 0)),
        kr=pl.BlockSpec((tm, 128), lambda i, h: (i, (MLA_Q_RANK + MLA_KV_RANK) // 128)),
        gain=pl.BlockSpec((1, MLA_HD_PAD), lambda i, h: (0, 0)),
        tab=pl.BlockSpec((tm, 128), lambda i, h: (i, 0)),
    )


def _mla_prep(cq, ckv, wuq, wukv, proj2, gq, gk, tabs):
    t = cq.shape[0]
    tm = _row_tile(t, PREP_ROWS)
    sp = _mla_prep_specs(t, tm)

    def body(cq_ref, ckv_ref, wuq_ref, wukv_ref, kr_ref, gq_ref, gk_ref, c_ref, s1_ref, s2_ref,
             qh_ref, kh_ref, vh_ref):
        c, s1, s2 = c_ref[...], s1_ref[...], s2_ref[...]

        def norm_rope(xv, gain):
            r = lax.rsqrt(jnp.sum(xv * xv, axis=-1, keepdims=True) / MLA_QKD + EPS)
            y = xv * r * gain
            return jnp.concatenate([y[:, :MLA_NOPE], _rope_tile(y[:, MLA_NOPE:], c, s1, s2)], axis=-1)

        kvv = _dot(ckv_ref[...], wukv_ref[...], 1, 0)
        qh_ref[...] = norm_rope(_dot(cq_ref[...], wuq_ref[...], 1, 0), gq_ref[...]).astype(BF16)
        kf = jnp.concatenate([kvv[:, :MLA_NOPE], kr_ref[...]], axis=-1)
        kh_ref[...] = norm_rope(kf, gk_ref[...]).astype(BF16)
        vh_ref[...] = jnp.concatenate([kvv[:, MLA_NOPE:], jnp.ones((tm, MLA_VD), F32)], axis=-1).astype(BF16)

    return pl.pallas_call(
        body, name="mla_prep", grid=(t // tm, MLA_HEADS),
        in_specs=[sp['cq'], sp['ckv'], sp['wuq'], sp['wukv'], sp['kr'], sp['gain'], sp['gain'],
                  sp['tab'], sp['tab'], sp['tab']],
        out_specs=[sp['head256'], sp['head256'], sp['head256']],
        out_shape=[_sds((MLA_HEADS, t, MLA_HD_PAD), BF16), _sds((MLA_HEADS, t, MLA_HD_PAD), BF16),
                   _sds((MLA_HEADS, t, 2 * MLA_VD), BF16)],
        compiler_params=_cparams(("parallel", "arbitrary")),
    )(cq, ckv, wuq, wukv, proj2, gq, gk, *tabs)


def _mla_prep_bwd(cq, ckv, wuq, wukv, proj2, gq, gk, tabs, dqt, dkh, dvh):
    t = cq.shape[0]
    tm = _row_tile(t, PREP_ROWS)
    ab = dqt.shape[-1]
    sp = _mla_prep_specs(t, tm)

    def body(cq_ref, ckv_ref, wuq_ref, wukv_ref, kr_ref, gq_ref, gk_ref, c_ref, s1_ref, s2_ref,
             dqt_ref, dkh_ref, dvh_ref, dq_ref, dkv_ref, dkr_ref, dgq_ref, dgk_ref):
        dqh = jnp.concatenate([dqt_ref[b].T for b in range(tm // ab)], axis=0)
        i, h = pl.program_id(0), pl.program_id(1)

        @pl.when((i == 0) & (h == 0))
        def _():
            dgq_ref[...] = jnp.zeros_like(dgq_ref)
            dgk_ref[...] = jnp.zeros_like(dgk_ref)

        @pl.when(h == 0)
        def _():
            dkr_ref[...] = jnp.zeros_like(dkr_ref)

        c, s1, s2 = c_ref[...], s1_ref[...], s2_ref[...]

        def back(xv, gain, dout):
            dy = jnp.concatenate([dout[:, :MLA_NOPE], _rope_tile(dout[:, MLA_NOPE:], c, -s1, -s2)], axis=-1)
            return _rms_bwd_rows(dy, xv, gain, MLA_QKD)

        kvv = _dot(ckv_ref[...], wukv_ref[...], 1, 0)
        dxq, dgq = back(_dot(cq_ref[...], wuq_ref[...], 1, 0), gq_ref[...], dqh)
        kf = jnp.concatenate([kvv[:, :MLA_NOPE], kr_ref[...]], axis=-1)
        dxk, dgk = back(kf, gk_ref[...], dkh_ref[...])
        dq_ref[...] = dxq.astype(BF16)
        dkv_ref[...] = jnp.concatenate([dxk[:, :MLA_NOPE], dvh_ref[...]], axis=-1).astype(BF16)
        dkr_ref[...] += dxk[:, MLA_NOPE:]
        dgq_ref[...] += jnp.sum(dgq, axis=0, keepdims=True)
        dgk_ref[...] += jnp.sum(dgk, axis=0, keepdims=True)

    return pl.pallas_call(
        body, name="mla_prep_bwd", grid=(t // tm, MLA_HEADS),
        in_specs=[sp['cq'], sp['ckv'], sp['wuq'], sp['wukv'], sp['kr'], sp['gain'], sp['gain'],
                  sp['tab'], sp['tab'], sp['tab'],
                  pl.BlockSpec((None, tm // ab, MLA_HD_PAD, ab), lambda i, h: (h, i, 0, 0)),
                  sp['head256'], sp['head128']],
        out_specs=[sp['cols256'], sp['cols256'], sp['tab'], sp['gain'], sp['gain']],
        out_shape=[_sds((t, MLA_HEADS * MLA_HD_PAD), BF16), _sds((t, MLA_HEADS * MLA_HD_PAD), BF16),
                   _sds((t, 128), F32), _sds((1, MLA_HD_PAD), F32), _sds((1, MLA_HD_PAD), F32)],
        compiler_params=_cparams(("arbitrary", "arbitrary")),
    )(cq, ckv, wuq, wukv, proj2, gq, gk, *tabs, dqt, dkh, dvh)


def _chunk_visible(rows, cols, row_off, col_off):
    rq = lax.shift_right_logical(lax.broadcasted_iota(jnp.int32, (rows, cols), 0) + row_off, 6)
    ck = lax.shift_right_logical(lax.broadcasted_iota(jnp.int32, (rows, cols), 1) + col_off, 6)
    return ck <= rq


def _rows_to_lanes(col):
    return col.T[:8, :]


def _attn_fwd(qh, kh, vh):
    t = qh.shape[1]
    ab = min(ATT_BLOCK, t)
    tq = min(ATT_QROWS, t)
    r = tq // ab
    hg = ATT_HEADS

    def body(q_ref, k_ref, v_ref, o_ref, lse_ref, acc_ref):
        n_un = pl.program_id(1) * r
        acc_ref[...] = jnp.zeros_like(acc_ref)

        def step(b, ms, diag):
            rows = pl.ds(pl.multiple_of(b * ab, ab), ab)
            out = []
            for hh in range(hg):
                m = ms[hh]
                s = _dot(q_ref[hh], k_ref[hh, rows, :], 1, 1)
                if diag is not None:
                    s = jnp.where(_chunk_visible(tq, ab, 0, diag * ab), s, -1e30)
                m_new = jnp.maximum(m, jnp.max(s, axis=-1, keepdims=True))
                p = jnp.exp2((s - m_new) * ATT_EXP2).astype(BF16)
                acc_ref[hh] = jnp.exp2((m - m_new) * ATT_EXP2) * acc_ref[hh] + _dot(p, v_ref[hh, rows, :], 1, 0)
                out.append(m_new)
            return tuple(out)

        ms = tuple(jnp.full((tq, 1), -1e30, F32) for _ in range(hg))
        ms = lax.fori_loop(0, n_un, lambda b, st: step(b, st, None), ms)
        for d in range(r):
            ms = step(n_un + d, ms, d)
        for hh in range(hg):
            l = acc_ref[hh, :, MLA_VD:]
            o_ref[:, hh * MLA_VD:(hh + 1) * MLA_VD] = acc_ref[hh, :, :MLA_VD] / l
            lse_t = _rows_to_lanes(ms[hh] * ATT_EXP2 + jnp.log(l) * LOG2E)
            for d in range(r):
                lse_ref[hh, d] = lse_t[:, d * ab:(d + 1) * ab]

    return pl.pallas_call(
        body, name="mla_attn", grid=(MLA_HEADS // hg, t // tq),
        in_specs=[pl.BlockSpec((hg, tq, MLA_HD_PAD), lambda g, i: (g, i, 0)),
                  pl.BlockSpec((hg, t, MLA_HD_PAD), lambda g, i: (g, 0, 0)),
                  pl.BlockSpec((hg, t, 2 * MLA_VD), lambda g, i: (g, 0, 0))],
        out_specs=[pl.BlockSpec((tq, hg * MLA_VD), lambda g, i: (i, g)),
                   pl.BlockSpec((hg, r, 8, ab), lambda g, i: (g, i, 0, 0))],
        out_shape=[_sds((t, MLA_HEADS * MLA_VD), F32), _sds((MLA_HEADS, t // ab, 8, ab), F32)],
        scratch_shapes=[pltpu.VMEM((hg, tq, 2 * MLA_VD), F32)],
        compiler_params=_cparams(("parallel", "arbitrary")),
    )(qh, kh, vh)


def _attn_bwd(qh, kh, vh, dob, o, lse_t):
    t = qh.shape[1]
    ab = min(ATT_BLOCK, t)
    kb = min(ATT_KROWS, t)
    r = kb // ab
    nq = t // ab
    hg = ATT_HEADS

    def body(q_ref, k_ref, v_ref, do_ref, o_ref, lse_ref, dqt_ref, dk_ref, dv_ref, dl_ref):
        j = pl.program_id(1)

        @pl.when(j == 0)
        def _():
            dqt_ref[...] = jnp.zeros_like(dqt_ref)
            ones = jnp.ones((8, MLA_VD), F32)

            def delta(b, carry):
                rows = pl.ds(pl.multiple_of(b * ab, ab), ab)
                for hh in range(hg):
                    cols = slice(hh * MLA_VD, (hh + 1) * MLA_VD)
                    prod = do_ref[rows, cols].astype(F32) * o_ref[rows, cols]
                    dl_ref[hh, b] = lax.dot_general(ones, prod, (((1,), (1,)), ((), ())),
                                                    precision=lax.Precision.HIGHEST, preferred_element_type=F32)
                return carry

            lax.fori_loop(0, nq, delta, 0)

        ks = [k_ref[hh] for hh in range(hg)]
        vs = [v_ref[hh, :, :MLA_VD] for hh in range(hg)]
        kts = [k.T for k in ks]

        dk_ref[...] = jnp.zeros_like(dk_ref)
        dv_ref[...] = jnp.zeros_like(dv_ref)

        def step(b, carry, diag):
            rows = pl.ds(pl.multiple_of(b * ab, ab), ab)
            hi = kb if diag is None else (diag + 1) * ab
            for hh in range(hg):
                q = q_ref[hh, rows, :]
                do = do_ref[rows, hh * MLA_VD:(hh + 1) * MLA_VD]
                s_t = _dot(ks[hh][:hi], q, 1, 1)
                if diag is not None:
                    key_chunk = lax.shift_right_logical(lax.broadcasted_iota(jnp.int32, (hi, ab), 0), 6)
                    query_chunk = lax.shift_right_logical(
                        lax.broadcasted_iota(jnp.int32, (hi, ab), 1) + diag * ab, 6)
                    s_t = jnp.where(key_chunk <= query_chunk, s_t, -1e30)
                p_t = jnp.exp2(s_t * ATT_EXP2 - lse_ref[hh, b][0:1, :])
                dp_t = _dot(vs[hh][:hi], do, 1, 1)
                ds_t = (p_t * (dp_t - dl_ref[hh, b][0:1, :]) * ATT_SCALE).astype(BF16)
                dqt_ref[hh, b] += _dot(kts[hh][:, :hi], ds_t, 1, 0)
                dk_ref[hh, :hi] += _dot(ds_t, q, 1, 0)
                dv_ref[hh, :hi] += _dot(p_t.astype(BF16), do, 1, 0)
            return carry

        for d in range(r):
            step(j * r + d, 0, d)
        lax.fori_loop((j + 1) * r, nq, lambda b, c: step(b, c, None), 0)

    whole = lambda w: pl.BlockSpec((hg, t, w), lambda g, j: (g, 0, 0))
    blk = lambda w: pl.BlockSpec((hg, kb, w), lambda g, j: (g, j, 0))
    stat = pl.BlockSpec((hg, nq, 8, ab), lambda g, j: (g, 0, 0, 0))
    cols = pl.BlockSpec((t, hg * MLA_VD), lambda g, j: (0, g))
    return pl.pallas_call(
        body, name="mla_attn_bwd", grid=(MLA_HEADS // hg, t // kb),
        in_specs=[whole(MLA_HD_PAD), blk(MLA_HD_PAD), blk(2 * MLA_VD),
                  cols, cols, stat],
        out_specs=[pl.BlockSpec((hg, nq, MLA_HD_PAD, ab), lambda g, j: (g, 0, 0, 0)), blk(MLA_HD_PAD), blk(MLA_VD)],
        out_shape=[_sds((MLA_HEADS, nq, MLA_HD_PAD, ab), F32), _sds((MLA_HEADS, t, MLA_HD_PAD), F32),
                   _sds((MLA_HEADS, t, MLA_VD), F32)],
        scratch_shapes=[pltpu.VMEM((hg, nq, 8, ab), F32)],
        compiler_params=_cparams(("parallel", "arbitrary")),
    )(qh, kh, vh, dob, o, lse_t)


VEC = pl.BlockSpec((1, D_MODEL), lambda i, j, k: (0, 0))


def _rows(tm, width):
    return pl.BlockSpec((tm, width), lambda i, j, k: (i, 0))


def _residual_epi(next_gain):
    if next_gain is None:
        return [], lambda acc, hv: (acc + hv,)

    def epi(acc, hv, g):
        h_new = acc + hv
        r = lax.rsqrt(jnp.mean(h_new * h_new, axis=-1, keepdims=True) + EPS)
        return h_new, h_new * r * g

    return [(next_gain, VEC)], epi


def _residual_outs(t, row, next_gain):
    outs = [(_sds((t, D_MODEL), F32), row)]
    return outs + ([(_sds((t, D_MODEL), BF16), row)] if next_gain is not None else [])


def _mlp_fwd(l, h, hn, w1g, fetch_w2, next_gain):
    t = h.shape[0]
    tm = _row_tile(t, 512)

    def relu2(acc):
        r = jnp.maximum(acc, 0.0)
        return (r * r,)

    (u,) = _mm_rows(f"mlp_up{l}", tm, hn, w1g, 'nn_cols', [(_sds((t, D_FF), BF16), _rows(tm, D_FF))], epi=relu2)
    w2g = fetch_w2((u,))
    row = _rows(tm, D_MODEL)
    more, epi = _residual_epi(next_gain)
    h2, hn_next = _mm_rows(f"mlp_down{l}", tm, u, w2g, 'nn_rows', _residual_outs(t, row, next_gain),
                           extras=[(h, row)] + more, epi=epi)
    return h2, hn_next, (h, hn, u, w1g, w2g)


def _norm_bwd_outs(t, tm):
    return [(_sds((t, D_MODEL), F32), pl.BlockSpec((tm, D_MODEL), lambda i, j, k: (i, 0))),
            (_sds((t // tm, 1, D_MODEL), F32), pl.BlockSpec((None, 1, D_MODEL), lambda i, j, k: (i, 0, 0)))]


def _norm_bwd_epi(acc, xv, res, g):
    dx, dgr = _rms_bwd_rows(acc, xv, g, D_MODEL)
    return res + dx, jnp.sum(dgr, axis=0, keepdims=True)


def _mlp_bwd(l, dh, saved, norm_g, emit_w2=None, emit_w1=None):
    h, hn, u, w1g, w2g = saved
    t = h.shape[0]
    tm = _row_tile(t, 512)
    nsh, _, wsh = w1g.shape
    wide = _rows(tm, D_FF)
    (da,) = _mm_rows(f"mlp_du{l}", tm, dh, w2g, 'nt_rows', [(_sds((t, D_FF), BF16), wide)], extras=[(u, wide)],
                     epi=lambda acc, uv: (2.0 * jnp.sqrt(uv.astype(F32)) * acc,))
    tw = _row_tile(t, 512)
    (dw2,) = _mm(f"mlp_dw2{l}", (1, 1, t // tw),
                 u, pl.BlockSpec((tw, D_FF), lambda i, j, k: (k, 0)),
                 dh, pl.BlockSpec((tw, D_MODEL), lambda i, j, k: (k, 0)), (0, 0),
                 [(_sds((D_FF, D_MODEL), BF16), pl.BlockSpec((D_FF, D_MODEL), lambda i, j, k: (0, 0)))])
    dw2 = dw2.reshape(nsh, wsh, D_MODEL)
    (dw1,) = _mm(f"mlp_dw1{l}", (1, 1, t // tw),
                 hn, pl.BlockSpec((tw, D_MODEL), lambda i, j, k: (k, 0)),
                 da, pl.BlockSpec((tw, D_FF), lambda i, j, k: (k, 0)), (0, 0),
                 [(_sds((nsh, D_MODEL, wsh), BF16), pl.BlockSpec((nsh, D_MODEL, wsh), lambda i, j, k: (0, 0, 0)))],
                 split=wsh, deps=emit_w2(dw2) if emit_w2 else ())
    row = _rows(tm, D_MODEL)
    dh_in, dg = _mm_rows(f"mlp_dhn{l}", tm, da, w1g, 'nt_cols', _norm_bwd_outs(t, tm),
                         extras=[(h, row), (dh, row), (norm_g, VEC)], epi=_norm_bwd_epi,
                         deps=emit_w1(dw1) if emit_w1 else ())
    return dh_in, jnp.sum(dg, axis=0), dw1, dw2


def _ple_fwd(l, h, hn, p, wg, wp, next_gain, target=None):
    t = h.shape[0]
    tm = _row_tile(t, 512)
    row = pl.BlockSpec((tm, D_MODEL), lambda i, j, k: (i, 0))
    full = lambda r: pl.BlockSpec((r, D_MODEL), lambda i, j, k: (0, 0))
    f32_row, bf_row = (_sds((t, D_MODEL), F32), row), (_sds((t, D_MODEL), BF16), row)
    common = [(h, row), (p, pl.BlockSpec((None, None, tm, PLE_DIM), lambda i, j, k: (l, 0, i, 0))),
              (wp, full(PLE_DIM))]
    if target is not None:
        def loss_epi(acc, hv, pv, wpv, tv):
            gt = _sigmoid(acc)
            ev = _dot(_bf(pv), wpv, 1, 0)
            err = hv + gt * ev - tv
            sq = jnp.sum(jnp.sum(err * err, axis=-1, keepdims=True), axis=0, keepdims=True)
            return err / D_MODEL, gt, ev, jnp.broadcast_to(sq, (8, 128))

        dy, gate, e, sq = _mm(f"ple_gate{l}", (t // tm, 1, 1), hn, row, wg, full(D_MODEL), (1, 0),
                              [f32_row, bf_row, bf_row, (_sds((t // tm, 8, 128), F32),
                                                         pl.BlockSpec((None, 8, 128), lambda i, j, k: (i, 0, 0)))],
                              extras=common + [(target, row)], epi=loss_epi)
        return dy, jnp.sum(sq, axis=0), (h, hn, gate, e)

    def gate_epi(acc, hv, pv, wpv, *gain):
        gt = _sigmoid(acc)
        ev = _dot(_bf(pv), wpv, 1, 0)
        h_new = hv + gt * ev
        if not gain:
            return h_new, gt, ev
        r = lax.rsqrt(jnp.mean(h_new * h_new, axis=-1, keepdims=True) + EPS)
        return h_new, gt, ev, h_new * r * gain[0]

    res = _mm(f"ple_gate{l}", (t // tm, 1, 1), hn, row, wg, full(D_MODEL), (1, 0),
              [f32_row, bf_row, bf_row] + ([bf_row] if next_gain is not None else []),
              extras=common + ([(next_gain, VEC)] if next_gain is not None else []), epi=gate_epi)
    h_out, gate, e = res[0], res[1], res[2]
    return h_out, (res[3] if next_gain is not None else None), (h, hn, gate, e)


def _ple_bwd(l, dh, saved, p, norm_g, wg, deps=(), emit=None):
    h, hn, gate, e = saved
    t = h.shape[0]
    tm = _row_tile(t)
    tk = _row_tile(t, 512)
    de, dz = _ple_gate_bwd(f"ple_gate_bwd{l}", dh, gate, e)
    full = lambda r: pl.BlockSpec((r, D_MODEL), lambda i, j, k: (0, 0))
    rowk = pl.BlockSpec((tk, D_MODEL), lambda i, j, k: (k, 0))
    (dwp,) = _mm(f"ple_dwp{l}", (1, 1, t // tk),
                 p, pl.BlockSpec((None, None, tk, PLE_DIM), lambda i, j, k: (l, 0, k, 0)),
                 de, rowk, (0, 0), [(_sds((PLE_DIM, D_MODEL), BF16), full(PLE_DIM))], deps=deps)
    (dwg,) = _mm(f"ple_dwg{l}", (1, 1, t // tk), hn, rowk, dz, rowk, (0, 0),
                 [(_sds((D_MODEL, D_MODEL), BF16), full(D_MODEL))])
    row = pl.BlockSpec((tm, D_MODEL), lambda i, j, k: (i, 0))
    dh_in, dg = _mm(f"ple_dhn{l}", (t // tm, 1, 1), dz, row, wg, full(D_MODEL), (1, 1),
                    _norm_bwd_outs(t, tm), extras=[(h, row), (dh, row), (norm_g, VEC)], epi=_norm_bwd_epi,
                    deps=emit(dwg, dwp) if emit else ())
    return dh_in, jnp.sum(dg, axis=0), dwg, dwp


def _ret_layer_fwd(x, norm_g, wri, fetch_wro, gn, cos, sin, next_gain, hn=None, deps=()):
    t = x.shape[0]
    tm = _row_tile(t)
    nsh, _, wsh = wri.shape
    if hn is None:
        hn = _rms_fwd("mix_norm0", x, norm_g)
    tp = _row_tile(t, 512)
    (proj,) = _mm_rows("ret_in", tp, hn, wri, 'nn_cols', [(_sds((t, RET_IN), BF16), _rows(tp, RET_IN))], deps=deps)
    gated, outp, states = _ret_fwd(proj, cos, sin, gn)
    wro = fetch_wro((gated,))
    row = _rows(tp, D_MODEL)
    more, epi = _residual_epi(next_gain)
    h1, hn_next = _mm_rows("ret_out", tp, gated, wro.reshape(RET_HEADS, RET_DV, D_MODEL), 'nn_rows',
                           _residual_outs(t, row, next_gain), extras=[(x, row)] + more, epi=epi)
    return h1, hn_next, (x, hn, proj, gated, outp, states, wro)


def _ret_layer_bwd(dh, saved, norm_g, wri, gn, cos, sin, emit_out, emit_in, deps=()):
    x, hn, proj, gated, outp, states, wro = saved
    t = x.shape[0]
    tm = _row_tile(t)
    tk = _row_tile(t, 512)
    nsh, _, wsh = wri.shape
    tg = _row_tile(t, 512)
    vw = _rows(tg, RET_V_W)
    dout, dgate, dgn = _mm_rows(
        "ret_dgate", tg, dh, wro.reshape(RET_HEADS, RET_DV, D_MODEL), 'nt_rows',
        [(_sds((t, RET_V_W), BF16), vw), (_sds((t, RET_V_W), BF16), vw),
         (_sds((t // tg, 1, RET_V_W), F32), pl.BlockSpec((None, 1, RET_V_W), lambda i, j, k: (i, 0, 0)))],
        extras=[(outp, vw), (proj, pl.BlockSpec((tg, RET_V_W), lambda i, j, k: (i, (RET_IN - RET_V_W) // RET_V_W))),
                (gn.reshape(1, RET_V_W), pl.BlockSpec((1, RET_V_W), lambda i, j, k: (0, 0)))],
        epi=_ret_gate_bwd_epi, deps=deps)
    dgn = jnp.sum(dgn, axis=0)
    (dwro,) = _mm("ret_dwro", (1, 1, t // tk),
                  gated, pl.BlockSpec((tk, RET_V_W), lambda i, j, k: (k, 0)),
                  dh, pl.BlockSpec((tk, D_MODEL), lambda i, j, k: (k, 0)), (0, 0),
                  [(_sds((RET_V_W, D_MODEL), BF16), pl.BlockSpec((RET_V_W, D_MODEL), lambda i, j, k: (0, 0)))])
    dproj = _ret_bwd(proj, cos, sin, states, dout, dgate, deps=emit_out(dwro))
    half = nsh // 2
    (dwri,) = _mm("ret_dwri", (2, 1, t // tk),
                  hn, pl.BlockSpec((tk, D_MODEL), lambda i, j, k: (k, 0)),
                  dproj, pl.BlockSpec((tk, half * wsh), lambda i, j, k: (k, i)), (0, 0),
                  [(_sds((nsh, D_MODEL, wsh), BF16), pl.BlockSpec((half, D_MODEL, wsh), lambda i, j, k: (i, 0, 0)))],
                  split=wsh)
    deps = emit_in(dwri)
    td = _row_tile(t, 256)
    row = _rows(td, D_MODEL)
    dx, dg = _mm_rows("ret_dhn", td, dproj, wri, 'nt_cols', _norm_bwd_outs(t, td),
                      extras=[(x, row), (dh, row), (norm_g, VEC)], epi=_norm_bwd_epi, deps=deps)
    return dx, jnp.sum(dg, axis=0), dgn.reshape(RET_HEADS, RET_DV)


def _mla_layer_fwd(h, hn, fetch, qa, kva, gq, gk, tabs, next_gain):
    t = h.shape[0]
    tm = _row_tile(t)
    row = pl.BlockSpec((tm, D_MODEL), lambda i, j, k: (i, 0))
    wmi = fetch('mla_in', (h,))['mla_w_in']
    (proj2,) = _mm("mla_in", (t // tm, 1, 1), hn, row,
                   wmi, pl.BlockSpec((D_MODEL, MLA_IN_PAD), lambda i, j, k: (0, 0)), (1, 0),
                   [(_sds((t, MLA_IN_PAD), F32), pl.BlockSpec((tm, MLA_IN_PAD), lambda i, j, k: (i, 0)))])
    cq, ckv = _mla_mid(proj2, qa, kva)
    up = fetch('mla_up', (cq,))
    wuq, wukv = up['mla_w_uq'], up['mla_w_ukv']
    qh, kh, vh = _mla_prep(cq, ckv, wuq, wukv, proj2, gq, gk, tabs)
    o, lse = _attn_fwd(qh, kh, vh)
    wmo = fetch('mla_out', (o,))['mla_w_out']
    more, epi = _residual_epi(next_gain)
    h_out, hn_next = _mm("mla_out", (t // tm, 1, 1), o, row,
                         wmo, pl.BlockSpec((D_MODEL, D_MODEL), lambda i, j, k: (0, 0)), (1, 0),
                         _residual_outs(t, row, next_gain), extras=[(h, row)] + more, epi=epi)
    return h_out, hn_next, (h, hn, proj2, cq, ckv, qh, kh, vh, o, lse), (wmi, wuq, wukv, wmo)


def _mla_layer_bwd(dh, saved, norm_g, wmi, qa, kva, wuq, wukv, gq, gk, wmo, tabs, deps=()):
    h, hn, proj2, cq, ckv, qh, kh, vh, o, lse = saved
    t = h.shape[0]
    tm = _row_tile(t)
    tk = _row_tile(t, 512)
    row = pl.BlockSpec((tm, D_MODEL), lambda i, j, k: (i, 0))
    rowk = pl.BlockSpec((tk, D_MODEL), lambda i, j, k: (k, 0))
    sq = pl.BlockSpec((D_MODEL, D_MODEL), lambda i, j, k: (0, 0))
    (dob,) = _mm("mla_do", (t // tm, 1, 1), dh, row, wmo, sq, (1, 1), [(_sds((t, D_MODEL), BF16), row)], deps=deps)
    (dwmo,) = _mm("mla_dwo", (1, 1, t // tk), o, rowk, dh, rowk, (0, 0), [(_sds((D_MODEL, D_MODEL), BF16), sq)])
    dqt, dkh, dvh = _attn_bwd(qh, kh, vh, dob, o, lse)
    dq, dkv, dkr, dgq, dgk = _mla_prep_bwd(cq, ckv, wuq, wukv, proj2, gq, gk, tabs, dqt, dkh, dvh)

    wide = MLA_HEADS * MLA_HD_PAD
    widek = pl.BlockSpec((tk, wide), lambda i, j, k: (k, 0))
    (dwuq,) = _mm("mla_dwuq", (1, 1, t // tk),
                  cq, pl.BlockSpec((tk, MLA_Q_RANK), lambda i, j, k: (k, 0)), dq, widek, (0, 0),
                  [(_sds((MLA_HEADS, MLA_Q_RANK, MLA_HD_PAD), BF16),
                    pl.BlockSpec((MLA_HEADS, MLA_Q_RANK, MLA_HD_PAD), lambda i, j, k: (0, 0, 0)))], split=MLA_HD_PAD)
    (dwukv,) = _mm("mla_dwukv", (1, 1, t // tk),
                   ckv, pl.BlockSpec((tk, MLA_KV_RANK), lambda i, j, k: (k, 0)), dkv, widek, (0, 0),
                   [(_sds((MLA_HEADS, MLA_KV_RANK, MLA_HD_PAD), BF16),
                     pl.BlockSpec((MLA_HEADS, MLA_KV_RANK, MLA_HD_PAD), lambda i, j, k: (0, 0, 0)))],
                   split=MLA_HD_PAD)
    side_by_side = lambda wg: wg.transpose(1, 0, 2).reshape(wg.shape[1], wide)
    widei = pl.BlockSpec((tm, wide), lambda i, j, k: (i, 0))
    (dcq,) = _mm("mla_dcq", (t // tm, 1, 1), dq, widei,
                 side_by_side(wuq), pl.BlockSpec((MLA_Q_RANK, wide), lambda i, j, k: (0, 0)), (1, 1),
                 [(_sds((t, MLA_Q_RANK), F32), pl.BlockSpec((tm, MLA_Q_RANK), lambda i, j, k: (i, 0)))])
    (dckv,) = _mm("mla_dckv", (t // tm, 1, 1), dkv, widei,
                  side_by_side(wukv), pl.BlockSpec((MLA_KV_RANK, wide), lambda i, j, k: (0, 0)), (1, 1),
                  [(_sds((t, MLA_KV_RANK), F32), pl.BlockSpec((tm, MLA_KV_RANK), lambda i, j, k: (i, 0)))])
    dproj2, dqa, dkva = _mla_mid_bwd(proj2, qa, kva, dcq, dckv, dkr)
    win = pl.BlockSpec((D_MODEL, MLA_IN_PAD), lambda i, j, k: (0, 0))
    (dwmi,) = _mm("mla_dwin", (1, 1, t // tk), hn, rowk,
                  dproj2, pl.BlockSpec((tk, MLA_IN_PAD), lambda i, j, k: (k, 0)), (0, 0),
                  [(_sds((D_MODEL, MLA_IN_PAD), BF16), win)])
    dh_in, dg = _mm("mla_dhn", (t // tm, 1, 1),
                    dproj2, pl.BlockSpec((tm, MLA_IN_PAD), lambda i, j, k: (i, 0)), wmi, win, (1, 1),
                    _norm_bwd_outs(t, tm), extras=[(h, row), (dh, row), (norm_g, VEC)], epi=_norm_bwd_epi)
    return dh_in, dict(mix=jnp.sum(dg, axis=0), wmi=dwmi, qa=dqa, kva=dkva, wuq=dwuq, wukv=dwukv, gq=dgq, gk=dgk,
                       wmo=dwmo)


def _local_step(x, p, target, w, fetch, emit=lambda group: ()):
    t = x.shape[0]
    cos_r, sin_r, tabs = w['tables'] if 'tables' in w else _rope_tables(t, 0.0)
    row = lambda a, i: a[i:i + 1]

    h1, hn1, s_ret = _ret_layer_fwd(x, row(w['mix_norm'], 0), w['ret_w_in'],
                                    lambda after: fetch('ret_out', after)['ret_w_out'], w['ret_gn'], cos_r, sin_r,
                                    row(w['mlp_norm'], 0), hn=w.get('hn0'), deps=w['deps'])
    h2, hn2, s_mlp0 = _mlp_fwd(0, h1, hn1, fetch('mlp_w1_0', (h1,))['mlp_w1'],
                               lambda after: fetch('mlp_w2_0', after)['mlp_w2'], row(w['ple_norm'], 0))
    w0 = fetch('ple_0', (h2,))
    h3, hn3, s_ple0 = _ple_fwd(0, h2, hn2, p, w0['ple_gate_w'], w0['ple_proj_w'], row(w['mix_norm'], 1))
    h4, hn4, s_mla, (wmi, wuq, wukv, wmo) = _mla_layer_fwd(
        h3, hn3, fetch, w['mla_q_a_norm'], w['mla_kv_a_norm'], w['mla_q_norm'], w['mla_k_norm'], tabs,
        row(w['mlp_norm'], 1))
    mla_w = (wmi, w['mla_q_a_norm'], w['mla_kv_a_norm'], wuq, wukv, w['mla_q_norm'], w['mla_k_norm'], wmo, tabs)
    w1 = fetch('layer_1', (h4,))
    h5, hn5, s_mlp1 = _mlp_fwd(1, h4, hn4, w1['mlp_w1'], lambda after: w1['mlp_w2'], row(w['ple_norm'], 1))
    dy, sq_err, s_ple1 = _ple_fwd(1, h5, hn5, p, w1['ple_gate_w'], w1['ple_proj_w'], None, target)

    n = N_DEV
    colsh = lambda a: a.reshape(a.shape[0], n, a.shape[1] // n).transpose(1, 0, 2)
    rowsh = lambda a: a.reshape(n, a.shape[0] // n, a.shape[1])
    big = {}

    def emit_group(group):
        big.update(group)
        return emit(group)

    dh5, dg_ple1, dwg1, dwp1 = _ple_bwd(1, dy, s_ple1, p, row(w['ple_norm'], 1), w1['ple_gate_w'])
    dh4, dg_mlp1, dw1_1, dw2_1 = _mlp_bwd(1, dh5, s_mlp1, row(w['mlp_norm'], 1))
    deps = emit_group({('ple_gate_w', 1): rowsh(dwg1), ('ple_proj_w', 1): colsh(dwp1),
                       ('mlp_w2', 1): dw2_1, ('mlp_w1', 1): dw1_1})
    dh3, gm = _mla_layer_bwd(dh4, s_mla, row(w['mix_norm'], 1), *mla_w, deps=deps)
    deps = emit_group({('mla_w_out', 0): rowsh(gm['wmo']), ('mla_w_uq', 0): _gather_rope(gm['wuq']),
                       ('mla_w_ukv', 0): gm['wukv'], ('mla_w_in', 0): rowsh(_gather_rope(gm['wmi']))})
    dh2, dg_ple0, _, _ = _ple_bwd(
        0, dh3, s_ple0, p, row(w['ple_norm'], 0), w0['ple_gate_w'], deps=deps,
        emit=lambda dwg, dwp: emit_group({('ple_gate_w', 0): rowsh(dwg), ('ple_proj_w', 0): colsh(dwp)}))
    dh1, dg_mlp0, _, _ = _mlp_bwd(0, dh2, s_mlp0, row(w['mlp_norm'], 0),
                                  emit_w2=lambda dw2: emit_group({('mlp_w2', 0): dw2}),
                                  emit_w1=lambda dw1: emit_group({('mlp_w1', 0): dw1}))
    dx, dg_mix0, dgn = _ret_layer_bwd(
        dh1, s_ret, row(w['mix_norm'], 0), w['ret_w_in'], w['ret_gn'], cos_r, sin_r,
        lambda dwro: emit_group({('ret_w_out', 0): rowsh(dwro)}),
        lambda dwri: emit_group({('ret_w_in', 0): dwri}))

    small = dict(
        mix_norm=[dg_mix0, gm['mix']], mlp_norm=[dg_mlp0, dg_mlp1], ple_norm=[dg_ple0, dg_ple1],
        ret_gn=dgn, mla_q_a_norm=gm['qa'], mla_kv_a_norm=gm['kva'], mla_q_norm=gm['gq'], mla_k_norm=gm['gk'],
    )
    return sq_err, dx, big, small


def _my_place():
    x, y, c = lax.axis_index("x"), lax.axis_index("y"), lax.axis_index("c")
    return x, y, c


def _flat(px, py, pc):
    return 4 * px + 2 * py + pc


def _peer(x, y, c, r):
    return (1 - x if r & 4 else x, 1 - y if r & 2 else y, 1 - c if r & 1 else c)


HBM = pl.BlockSpec(memory_space=pltpu.HBM)
SEMS = pl.BlockSpec(memory_space=pltpu.SEMAPHORE)
SIDE_EFFECT = pltpu.SideEffectType.DATAFLOW_SIDE_EFFECTING


def _rs_copies(x, y, c, srcs, lands, send_sems, recv_sems):
    copies = []
    for a in range(len(srcs)):
        for r in range(1, N_DEV):
            peer = _peer(x, y, c, r)
            k = a * (N_DEV - 1) + r - 1
            copies.append(pltpu.make_async_remote_copy(
                src_ref=srcs[a].at[_flat(*peer)], dst_ref=lands[a].at[r - 1],
                send_sem=send_sems.at[k], recv_sem=recv_sems.at[k], device_id=peer, device_id_type=MESH))
    return copies


def _rs_start(name, arrays):
    n = len(arrays)
    hbm = lambda a: pltpu.with_memory_space_constraint(a, pltpu.HBM)
    lands = [hbm(lax.empty((N_DEV - 1,) + a.shape[1:], a.dtype)) for a in arrays]

    def body(*refs):
        srcs, lnd = refs[:n], refs[n:2 * n]
        send_sems, recv_sems = refs[2 * n], refs[2 * n + 1]
        token = refs[-1]
        for cp in _rs_copies(*_my_place(), srcs, lnd, send_sems, recv_sems):
            cp.start()
        token[...] = jnp.zeros_like(token)

    outs = pl.pallas_call(
        body, name=name,
        in_specs=[HBM] * (2 * n),
        out_specs=[SEMS, SEMS] + [HBM] * (2 * n) + [pl.BlockSpec(memory_space=pltpu.VMEM)],
        out_shape=[pltpu.SemaphoreType.DMA((n * (N_DEV - 1),)), pltpu.SemaphoreType.DMA((n * (N_DEV - 1),))]
        + [pltpu.HBM(a.shape, a.dtype) for a in arrays] + [pltpu.HBM(l.shape, l.dtype) for l in lands]
        + [_sds((8, 128), F32)],
        input_output_aliases={i: 2 + i for i in range(2 * n)},
        compiler_params=pltpu.CompilerParams(has_side_effects=SIDE_EFFECT),
    )(*[hbm(a) for a in arrays], *lands)
    return outs[0], outs[1], outs[2:2 + n], outs[2 + n:2 + 2 * n], outs[-1]


def _rs_wait(name, send_sems, recv_sems, srcs, lands, after):
    n = len(srcs)

    def body(*refs):
        src_refs, lnd = refs[:n], refs[n:2 * n]
        send, recv = refs[2 * n], refs[2 * n + 1]
        for cp in _rs_copies(*_my_place(), src_refs, lnd, send, recv):
            cp.wait_send()
            cp.wait_recv()

    outs = pl.pallas_call(
        body, name=name,
        in_specs=[HBM] * (2 * n) + [SEMS, SEMS] + [ANY] * len(after),
        out_specs=[HBM] * (2 * n),
        out_shape=[pltpu.HBM(a.shape, a.dtype) for a in list(srcs) + list(lands)],
        input_output_aliases={i: i for i in range(2 * n)},
        compiler_params=pltpu.CompilerParams(has_side_effects=SIDE_EFFECT),
    )(*srcs, *lands, send_sems, recv_sems, *after)
    return outs[:n], outs[n:]


SMALL_PACK_ROWS = 16


def _all_reduce_small(rows, deps=()):
    n = len(rows)

    def body(*refs):
        ins = refs[:n]
        out_ref, mine, buf, send_sems, recv_sems = refs[n + len(deps):]
        x, y, c = _my_place()
        mine[...] = jnp.zeros_like(mine)
        for (r0, a), ref in zip(rows, ins):
            mine[r0:r0 + a.shape[0], 0:a.shape[1]] = ref[...]
        buf[_flat(x, y, c)] = mine[...]
        copies = []
        for r in range(1, N_DEV):
            peer = _peer(x, y, c, r)
            send = pltpu.make_async_remote_copy(
                src_ref=mine, dst_ref=buf.at[_flat(x, y, c)],
                send_sem=send_sems.at[r - 1], recv_sem=recv_sems.at[r - 1], device_id=peer, device_id_type=MESH)
            send.start()
            recv = pltpu.make_async_remote_copy(
                src_ref=mine, dst_ref=buf.at[_flat(*peer)],
                send_sem=send_sems.at[r - 1], recv_sem=recv_sems.at[r - 1], device_id=peer, device_id_type=MESH)
            copies.append((send, recv))
        for send, recv in copies:
            send.wait_send()
            recv.wait_recv()
        acc = buf[0]
        for s in range(1, N_DEV):
            acc = acc + buf[s]
        out_ref[...] = acc

    vm = pl.BlockSpec(memory_space=pltpu.VMEM)
    shape = (SMALL_PACK_ROWS, D_MODEL)
    return pl.pallas_call(
        body, name="all_reduce_small", in_specs=[vm] * n + [ANY] * len(deps), out_specs=vm,
        out_shape=_sds(shape, F32),
        scratch_shapes=[pltpu.VMEM(shape, F32), pltpu.VMEM((N_DEV,) + shape, F32),
                        pltpu.SemaphoreType.DMA((7,)), pltpu.SemaphoreType.DMA((7,))],
    )(*[a for _, a in rows], *deps)


def _adamw_math(w, g, m, v):
    m = ADAM_B1 * m + (1.0 - ADAM_B1) * g
    v = ADAM_B2 * v + (1.0 - ADAM_B2) * (g * g)
    m_hat = m / (1.0 - ADAM_B1 ** ADAM_STEP)
    v_hat = v / (1.0 - ADAM_B2 ** ADAM_STEP)
    delta = -ADAM_LR * (m_hat / (jnp.sqrt(v_hat) + ADAM_EPS) + ADAM_WD * w)
    return delta, m, v


def _adamw_big(name, w, m, v, srcs, lands, me):
    nl, rows, cols = w.shape
    tr = next(cand for cand in (256, 128, 64, 32, 16, 8) if rows % cand == 0)

    def body(me_ref, w_ref, m_ref, v_ref, *rest):
        src_refs, land_refs = rest[:nl], rest[nl:2 * nl]
        g_ref, d_ref, mo_ref, vo_ref = rest[2 * nl:]
        for layer in range(nl):
            @pl.when(pl.program_id(0) == layer)
            def _():
                g = src_refs[layer][...].astype(F32)
                for s in range(N_DEV - 1):
                    g = g + land_refs[layer][s].astype(F32)
                delta, mn, vn = _adamw_math(w_ref[...], g, m_ref[...], v_ref[...])
                g_ref[...] = g
                d_ref[...] = delta
                mo_ref[...] = mn
                vo_ref[...] = vn

    blk = pl.BlockSpec((None, tr, cols), lambda l, i, me_ref: (l, i, 0))
    at = lambda layer, l, i: jnp.where(l == layer, i, 0)
    own = [pl.BlockSpec((None, tr, cols), functools.partial(lambda layer, l, i, me_ref: (me_ref[0], at(layer, l, i), 0),
                                                            layer)) for layer in range(nl)]
    peers = [pl.BlockSpec((N_DEV - 1, tr, cols), functools.partial(lambda layer, l, i, me_ref: (0, at(layer, l, i), 0),
                                                                   layer)) for layer in range(nl)]
    return pl.pallas_call(
        body, name=name,
        grid_spec=pltpu.PrefetchScalarGridSpec(
            num_scalar_prefetch=1, grid=(nl, rows // tr),
            in_specs=[blk, blk, blk] + own + peers, out_specs=[blk] * 4),
        out_shape=[_sds((nl, rows, cols), F32)] * 4,
        compiler_params=_cparams(("arbitrary", "arbitrary")),
    )(me, w, m, v, *srcs, *lands)


def _adamw_small(ws, gs, ms, vs):
    n = len(ws)

    def body(*refs):
        w_refs, g_refs, m_refs, v_refs = (refs[i * n:(i + 1) * n] for i in range(4))
        d_out, m_out, v_out = (refs[(4 + i) * n:(5 + i) * n] for i in range(3))
        for i in range(n):
            delta, mn, vn = _adamw_math(w_refs[i][...], g_refs[i][...], m_refs[i][...], v_refs[i][...])
            d_out[i][...] = delta
            m_out[i][...] = mn
            v_out[i][...] = vn

    vm = pl.BlockSpec(memory_space=pltpu.VMEM)
    outs = pl.pallas_call(
        body, name="adamw_small", in_specs=[vm] * (4 * n), out_specs=[vm] * (3 * n),
        out_shape=[_sds(a.shape, F32) for a in ws] * 3,
    )(*ws, *gs, *ms, *vs)
    return outs[:n], outs[n:2 * n], outs[2 * n:]


def _pad_to(a, rows, cols):
    return jnp.pad(a, ((0, rows - a.shape[0]), (0, cols - a.shape[1])))


def _place_own(blocks):
    me = _flat(*_my_place())
    return [lax.dynamic_update_slice(lax.empty((N_DEV,) + b.shape, b.dtype), b[None], (me,) + (0,) * b.ndim)
            for b in blocks]


def _ag_copies(x, y, c, blocks, bufs, send_sems, recv_sems, arriving):
    copies = []
    for a in range(len(blocks)):
        for r in range(1, N_DEV):
            peer = _peer(x, y, c, r)
            k = a * (N_DEV - 1) + r - 1
            copies.append(pltpu.make_async_remote_copy(
                src_ref=blocks[a], dst_ref=bufs[a].at[_flat(*(peer if arriving else (x, y, c)))],
                send_sem=send_sems.at[k], recv_sem=recv_sems.at[k], device_id=peer, device_id_type=MESH))
    return copies


def _ag_start(groups, after):
    flat = [pair for g in groups for pair in g]
    n, ng = len(flat), len(groups)
    hbm = lambda a: pltpu.with_memory_space_constraint(a, pltpu.HBM)

    def body(*refs):
        blocks, bufs = refs[:n], refs[n:2 * n]
        sems = refs[2 * n + len(after):2 * n + len(after) + 2 * ng]
        x, y, c = _my_place()
        at = 0
        for gi, g in enumerate(groups):
            for cp in _ag_copies(x, y, c, blocks[at:at + len(g)], bufs[at:at + len(g)], sems[2 * gi],
                                 sems[2 * gi + 1], arriving=False):
                cp.start()
            at += len(g)
        refs[-1][...] = jnp.zeros_like(refs[-1])

    sem_shapes = [pltpu.SemaphoreType.DMA((len(g) * (N_DEV - 1),)) for g in groups for _ in range(2)]
    outs = pl.pallas_call(
        body, name="gather_start",
        in_specs=[HBM] * (2 * n) + [ANY] * len(after),
        out_specs=[SEMS] * (2 * ng) + [HBM] * (2 * n) + [pl.BlockSpec(memory_space=pltpu.VMEM)],
        out_shape=sem_shapes + [pltpu.HBM(b.shape, b.dtype) for b, _ in flat]
        + [pltpu.HBM(u.shape, u.dtype) for _, u in flat] + [_sds((8, 128), F32)],
        input_output_aliases={i: 2 * ng + i for i in range(2 * n)},
        compiler_params=pltpu.CompilerParams(has_side_effects=SIDE_EFFECT),
    )(*[hbm(b) for b, _ in flat], *[hbm(u) for _, u in flat], *after)
    blocks_thru, bufs_thru = outs[2 * ng:2 * ng + n], outs[2 * ng + n:2 * ng + 2 * n]
    started, at = [], 0
    for gi, g in enumerate(groups):
        started.append((outs[2 * gi], outs[2 * gi + 1], blocks_thru[at:at + len(g)], bufs_thru[at:at + len(g)]))
        at += len(g)
    return started, outs[-1]


def _ag_wait(name, send_sems, recv_sems, blocks, bufs, after):
    n = len(blocks)

    def body(*refs):
        for cp in _ag_copies(*_my_place(), refs[:n], refs[n:2 * n], refs[2 * n], refs[2 * n + 1], arriving=True):
            cp.wait_send()
            cp.wait_recv()

    outs = pl.pallas_call(
        body, name=name,
        in_specs=[HBM] * (2 * n) + [SEMS, SEMS] + [ANY] * len(after),
        out_specs=[HBM] * (2 * n),
        out_shape=[pltpu.HBM(a.shape, a.dtype) for a in list(blocks) + list(bufs)],
        input_output_aliases={i: i for i in range(2 * n)},
        compiler_params=pltpu.CompilerParams(has_side_effects=SIDE_EFFECT),
    )(*blocks, *bufs, send_sems, recv_sems, *after)
    return outs[n:]


def _split_call(name, body, thru, sems_in, new_sems, after):
    n, ns, nn = len(thru), len(sems_in), len(new_sems)
    hbm = lambda a: pltpu.with_memory_space_constraint(a, pltpu.HBM)

    def wrapped(*refs):
        body(refs[:n], refs[n:n + ns], refs[n + ns + len(after):n + ns + len(after) + nn])
        refs[-1][...] = jnp.zeros_like(refs[-1])

    outs = pl.pallas_call(
        wrapped, name=name,
        in_specs=[HBM] * n + [SEMS] * ns + [ANY] * len(after),
        out_specs=[SEMS] * nn + [HBM] * n + [pl.BlockSpec(memory_space=pltpu.VMEM)],
        out_shape=[pltpu.SemaphoreType.DMA((k,)) for k in new_sems] + [pltpu.HBM(a.shape, a.dtype) for a in thru]
        + [_sds((8, 128), F32)],
        input_output_aliases={i: nn + i for i in range(n)},
        compiler_params=pltpu.CompilerParams(has_side_effects=SIDE_EFFECT),
    )(*[hbm(a) for a in thru], *sems_in, *after)
    return list(outs[:nn]), list(outs[nn:nn + n]), outs[-1]


def _first_gather(blocks, bufs, overlap):
    n = len(blocks)

    def copies(refs, s1, r1, s2, r2):
        x, y, c = _my_place()
        me, sibling = (x, y, c), (x, y, 1 - c)
        chips = [(1 - x, y), (x, 1 - y), (1 - x, 1 - y)]
        blk, buf = refs[:n], refs[n:]
        out = dict(send1=[], recv1_sib=[], recv1_ici=[], send2=[], recv2=[])
        for a in range(n):
            place = lambda dev: buf[a].at[_flat(*dev)]
            for k, to in enumerate([sibling] + [(*chip, c) for chip in chips]):
                mk = lambda dst: pltpu.make_async_remote_copy(
                    src_ref=blk[a], dst_ref=dst, send_sem=s1.at[4 * a + k], recv_sem=r1.at[4 * a + k],
                    device_id=to, device_id_type=MESH)
                out['send1'].append(mk(place(me)))
                out['recv1_sib' if k == 0 else 'recv1_ici'].append(mk(place(to)))
            for j, chip in enumerate(chips):
                mk = lambda dev: pltpu.make_async_remote_copy(
                    src_ref=place(dev), dst_ref=place(dev), send_sem=s2.at[3 * a + j], recv_sem=r2.at[3 * a + j],
                    device_id=sibling, device_id_type=MESH)
                out['send2'].append(mk((*chip, c)))
                out['recv2'].append(mk((*chip, 1 - c)))
        return out

    def start(refs, sems_in, new):
        for cp in copies(refs, new[0], new[1], new[0], new[1])['send1']:
            cp.start()

    def forward(refs, sems_in, new):
        cps = copies(refs, sems_in[0], sems_in[1], new[0], new[1])
        for cp in cps['recv1_ici']:
            cp.wait_recv()
        for cp in cps['send2']:
            cp.start()

    def finish(refs, sems_in, new):
        cps = copies(refs, *sems_in)
        for cp in cps['recv1_sib'] + cps['recv2']:
            cp.wait_recv()
        for cp in cps['send1'] + cps['send2']:
            cp.wait_send()

    sems1, thru, token = _split_call("first_gather_start", start, list(blocks) + list(bufs), [], [4 * n, 4 * n], ())
    after = overlap(token)
    sems2, thru, token = _split_call("first_gather_forward", forward, thru, sems1, [3 * n, 3 * n], after)
    _, thru, _ = _split_call("first_gather_wait", finish, thru, sems1 + sems2, [], ())
    return thru[n:], token


def _prepare_weights(p, x):
    n = N_DEV
    bf = lambda a: a.astype(BF16)
    gn_pack = jnp.concatenate([
        _pad_to(p['ret_gn'][0], RET_HEADS, 128), _pad_to(p['mla_q_a_norm'], 1, 128),
        _pad_to(p['mla_kv_a_norm'], 1, 128), jnp.zeros((2, 128), F32)], axis=0)
    ple = lambda l: [bf(p['ple_gate_w'][l]), bf(p['ple_proj_w'][l])]
    names = ('mlp_w1_0', 'mlp_w2_0', 'ple_0', 'mla_in', 'mla_up', 'mla_out', 'layer_1')
    later = [[bf(p['mlp_w1'][0])], [bf(p['mlp_w2'][0])], ple(0),
             [bf(p['mla_w_in'][0])], [bf(p['mla_w_uq'][0]), bf(p['mla_w_ukv'][0])], [bf(p['mla_w_out'][0])],
             [bf(p['mlp_w1'][1]), bf(p['mlp_w2'][1])] + ple(1)]
    first = [gn_pack, bf(p['ret_w_in'][0]), bf(p['ret_w_out'][0])]
    behind = {}

    def overlap(token):
        behind['hn0'] = _rms_fwd("mix_norm0", x, p['mix_norm'][0:1], deps=(token,))
        behind['bufs'] = _place_own([b for g in later for b in g])
        behind['tables'] = _rope_tables(x.shape[0], token[0, 0])
        cos_r, sin_r, tabs = behind['tables']
        return (behind['hn0'], cos_r, sin_r, *tabs, *behind['bufs'])

    (pack, wri, wro), token = _first_gather(first, _place_own(first), overlap)
    bufs = behind['bufs']
    groups, at = [], 0
    for g in later:
        groups.append(list(zip(g, bufs[at:at + len(g)])))
        at += len(g)
    started, token = _ag_start(groups, (token,))

    w = {k: p[k] for k in ('mix_norm', 'mlp_norm', 'ple_norm')}
    w['hn0'] = behind['hn0']
    w['tables'] = behind['tables']
    w['ret_gn'] = pack[:, :RET_HEADS, :RET_DV // n].transpose(1, 0, 2).reshape(RET_HEADS, RET_DV)
    w['mla_q_a_norm'] = pack[:, RET_HEADS, :MLA_Q_RANK // n].reshape(1, MLA_Q_RANK)
    w['mla_kv_a_norm'] = pack[:, RET_HEADS + 1, :MLA_KV_RANK // n].reshape(1, MLA_KV_RANK)
    w['ret_w_in'] = wri
    w['mla_q_norm'] = _spread_rope(p['mla_q_norm'])
    w['mla_k_norm'] = _spread_rope(p['mla_k_norm'])
    w['deps'] = (token,)

    def fetch(name, after):
        if name == 'ret_out':
            return dict(ret_w_out=wro.reshape(RET_V_W, D_MODEL))
        got = list(_ag_wait("gather_wait_" + name, *started[names.index(name)], after))
        if name == 'mla_in':
            return dict(mla_w_in=_spread_rope(got[0].reshape(D_MODEL, MLA_IN)))
        if name == 'mla_up':
            return dict(mla_w_uq=_spread_rope(got[0]), mla_w_ukv=got[1])
        if name == 'mla_out':
            return dict(mla_w_out=got[0].reshape(D_MODEL, D_MODEL))
        out = {}
        if name in ('mlp_w1_0', 'layer_1'):
            out['mlp_w1'] = got.pop(0)
        if name in ('mlp_w2_0', 'layer_1'):
            out['mlp_w2'] = got.pop(0)
        if name in ('ple_0', 'layer_1'):
            out['ple_gate_w'] = got[0].reshape(D_MODEL, D_MODEL)
            out['ple_proj_w'] = got[1].transpose(1, 0, 2).reshape(PLE_DIM, D_MODEL)
        return out

    return w, fetch


def _small_grads(small, after):
    rows = [(0, small['mix_norm'][0]), (1, small['mix_norm'][1]), (2, small['mlp_norm'][0]),
            (3, small['mlp_norm'][1]), (4, small['ple_norm'][0]), (5, small['ple_norm'][1]),
            (6, small['ret_gn']), (10, small['mla_q_a_norm']), (11, small['mla_kv_a_norm']),
            (12, small['mla_q_norm']), (13, small['mla_k_norm']), (14, small['sq_err'])]
    gs = _all_reduce_small(rows, after)
    me = _flat(*_my_place())
    n = N_DEV
    return dict(
        sq_err=gs[14, 0],
        mix_norm=gs[0:2], mlp_norm=gs[2:4], ple_norm=gs[4:6],
        ret_gn=lax.dynamic_slice(gs, (6, me * (RET_DV // n)), (RET_HEADS, RET_DV // n)),
        mla_q_a_norm=lax.dynamic_slice(gs, (10, me * (MLA_Q_RANK // n)), (1, MLA_Q_RANK // n)),
        mla_kv_a_norm=lax.dynamic_slice(gs, (11, me * (MLA_KV_RANK // n)), (1, MLA_KV_RANK // n)),
        mla_q_norm=_gather_rope(gs[12:13, :MLA_HD_PAD]), mla_k_norm=_gather_rope(gs[13:14, :MLA_HD_PAD]))


def kernel(x, p, mix_norm, ret_w_in, ret_gn, ret_w_out, mla_w_in, mla_q_a_norm, mla_kv_a_norm, mla_w_uq, mla_w_ukv, mla_q_norm, mla_k_norm, mla_w_out, mlp_norm, mlp_w1, mlp_w2, ple_norm, ple_gate_w, ple_proj_w, loss_target, m_mix_norm, m_ret_w_in, m_ret_gn, m_ret_w_out, m_mla_w_in, m_mla_q_a_norm, m_mla_kv_a_norm, m_mla_w_uq, m_mla_w_ukv, m_mla_q_norm, m_mla_k_norm, m_mla_w_out, m_mlp_norm, m_mlp_w1, m_mlp_w2, m_ple_norm, m_ple_gate_w, m_ple_proj_w, v_mix_norm, v_ret_w_in, v_ret_gn, v_ret_w_out, v_mla_w_in, v_mla_q_a_norm, v_mla_kv_a_norm, v_mla_w_uq, v_mla_w_ukv, v_mla_q_norm, v_mla_k_norm, v_mla_w_out, v_mlp_norm, v_mlp_w1, v_mlp_w2, v_ple_norm, v_ple_gate_w, v_ple_proj_w):
    given = dict(locals())
    params = {n: given[n] for n in WEIGHTS}
    w, fetch = _prepare_weights(params, x[0])

    started = []

    def emit(group):
        keys = list(group)
        send, recv, srcs, lands, token = _rs_start(f"rs_start{len(started)}", [group[k] for k in keys])
        started.append((keys, send, recv, srcs, lands))
        return (token,)

    sq_err, grad_x, _, small = _local_step(x[0], p, loss_target[0], w, fetch, emit)
    small['sq_err'] = sq_err[0:1]

    grads, deltas, new_m, new_v = {}, {}, {}, {}
    total = {}

    def small_updates(after):
        sg = _small_grads(small, after)
        total['loss'] = 0.5 / D_MODEL * sg['sq_err']
        two_d = lambda a: a.reshape(-1, a.shape[-1])
        d_s, m_s, v_s = _adamw_small(
            [two_d(params[n]) for n in SMALL], [sg[n] for n in SMALL],
            [two_d(given["m_" + n]) for n in SMALL], [two_d(given["v_" + n]) for n in SMALL])
        for i, n in enumerate(SMALL):
            shape = params[n].shape
            grads[n], deltas[n], new_m[n], new_v[n] = (a.reshape(shape) for a in (sg[n], d_s[i], m_s[i], v_s[i]))
        return (d_s[0],)

    me = _flat(*_my_place()).astype(jnp.int32).reshape(1)
    after = (grad_x,)
    src_of, land_of = {}, {}
    for gi, (keys, send, recv, srcs, lands) in enumerate(started):
        if gi == len(started) - 1:
            after = small_updates(after)
        srcs, lands = _rs_wait(f"rs_wait{gi}", send, recv, srcs, lands, after)
        for k, s, l in zip(keys, srcs, lands):
            src_of[k], land_of[k] = s, l
        done = [n for n in BIG if n not in grads and all((n, l) in src_of for l in range(params[n].shape[0]))]
        for n in done:
            layers = range(params[n].shape[0])
            grads[n], deltas[n], new_m[n], new_v[n] = _adamw_big(
                "adamw_" + n, params[n], given["m_" + n], given["v_" + n],
                [src_of[(n, l)] for l in layers], [land_of[(n, l)] for l in layers], me)
        if done:
            after = tuple(deltas[n] for n in done)

    return (total['loss'], grad_x[None], *[grads[n] for n in WEIGHTS], *[deltas[n] for n in WEIGHTS],
            *[new_m[n] for n in WEIGHTS], *[new_v[n] for n in WEIGHTS])
```

```python
import functools

import jax
import jax.numpy as jnp
from jax import lax
from jax.experimental import pallas as pl
from jax.experimental.pallas import tpu as pltpu

F32 = jnp.float32
BF16 = jnp.bfloat16
MESH = pl.DeviceIdType.MESH
ANY = pl.BlockSpec(memory_space=pl.ANY)

N_DEV = 8
D_MODEL = 1024
CHUNK = 64
RET_BLOCK = 4 * CHUNK
EPS = 1e-6
ROPE_THETA = 10000.0
RET_HEADS = 4
RET_DK = 256
RET_DV = 512
RET_QK_W = RET_HEADS * RET_DK
RET_V_W = RET_HEADS * RET_DV
RET_IN = 2 * RET_QK_W + 2 * RET_V_W
MLA_HEADS = 8
MLA_NOPE = 128
MLA_ROPE = 64
MLA_QKD = MLA_NOPE + MLA_ROPE
MLA_VD = 128
MLA_Q_RANK = 384
MLA_KV_RANK = 256
MLA_IN = MLA_Q_RANK + MLA_KV_RANK + MLA_ROPE
MLA_IN_PAD = 768
MLA_HD_PAD = 256
D_FF = 4096
PLE_DIM = 256
ATT_SCALE = MLA_QKD ** -0.5
LOG2E = 1.4426950408889634
ATT_EXP2 = ATT_SCALE * LOG2E

ADAM_LR = 0.001
ADAM_B1 = 0.9
ADAM_B2 = 0.999
ADAM_EPS = 1e-08
ADAM_WD = 0.01
ADAM_STEP = 10

VMEM_LIMIT = 52 * 1024 * 1024
ROW_TILE = 1024
RET_ROWS = 512
ATT_BLOCK = 256
ATT_QROWS = 1024
ATT_KROWS = 1024
ATT_HEADS = 2
PREP_ROWS = 1024

WEIGHTS = ['mix_norm', 'ret_w_in', 'ret_gn', 'ret_w_out', 'mla_w_in', 'mla_q_a_norm', 'mla_kv_a_norm',
           'mla_w_uq', 'mla_w_ukv', 'mla_q_norm', 'mla_k_norm', 'mla_w_out', 'mlp_norm', 'mlp_w1', 'mlp_w2',
           'ple_norm', 'ple_gate_w', 'ple_proj_w']
BIG = ['ret_w_in', 'ret_w_out', 'mla_w_in', 'mla_w_uq', 'mla_w_ukv', 'mla_w_out', 'mlp_w1', 'mlp_w2',
       'ple_gate_w', 'ple_proj_w']
SMALL = [w for w in WEIGHTS if w not in BIG]


def _cparams(sem=None):
    return pltpu.CompilerParams(dimension_semantics=sem, vmem_limit_bytes=VMEM_LIMIT)


def _dot(a, b, ca, cb):
    return lax.dot_general(a, b, (((ca,), (cb,)), ((), ())), preferred_element_type=F32)


def _bf(v):
    return v if v.dtype == BF16 else v.astype(BF16)


def _sigmoid(z):
    return 1.0 / (1.0 + jnp.exp(-z))


def _mm(name, grid, a, a_spec, b, b_spec, contract, outs, extras=(), epi=None, deps=(), split=None):
    nk = grid[2]
    n_ex, n_out, n_dep = len(extras), len(outs), len(deps)
    acc_shape = tuple(d for d in outs[0][1].block_shape if d is not None)
    if split is not None:
        acc_shape = (acc_shape[1], acc_shape[0] * split)

    def body(*refs):
        a_ref, b_ref = refs[:2]
        ex_refs = refs[2:2 + n_ex]
        out_refs = refs[2 + n_ex + n_dep:2 + n_ex + n_dep + n_out]

        def product():
            return _dot(_bf(a_ref[...]), _bf(b_ref[...]), contract[0], contract[1])

        def finish(acc):
            if split is not None:
                for j in range(acc_shape[1] // split):
                    out_refs[0][j] = acc[:, j * split:(j + 1) * split].astype(out_refs[0].dtype)
                return
            acc = acc[...]
            res = epi(acc, *[r[...] for r in ex_refs]) if epi is not None else (acc,)
            for o, r in zip(out_refs, res):
                o[...] = r.astype(o.dtype)

        if nk == 1:
            finish(product())
        else:
            acc_ref = refs[-1]
            k = pl.program_id(2)

            @pl.when(k == 0)
            def _():
                acc_ref[...] = jnp.zeros_like(acc_ref)

            acc_ref[...] += product()

            @pl.when(k == nk - 1)
            def _():
                finish(acc_ref)

    return pl.pallas_call(
        body, name=name, grid=grid,
        in_specs=[a_spec, b_spec] + [s for _, s in extras] + [ANY] * n_dep,
        out_specs=[s for _, s in outs],
        out_shape=[s for s, _ in outs],
        scratch_shapes=[pltpu.VMEM(acc_shape, F32)] if nk > 1 else [],
        compiler_params=_cparams(("parallel", "parallel", "arbitrary")),
    )(a, b, *[x for x, _ in extras], *deps)


def _mm_rows(name, tm, a, w, mode, outs, extras=(), epi=None, deps=()):
    n_sh, rows, cols = w.shape
    n_ex, n_out, n_dep = len(extras), len(outs), len(deps)
    by_cols = mode in ('nn_cols', 'nt_rows')
    width = cols if mode == 'nn_cols' else rows

    def body(*refs):
        a_ref, w_ref = refs[:2]
        ex_refs = refs[2:2 + n_ex]
        out_refs = refs[2 + n_ex + n_dep:2 + n_ex + n_dep + n_out]
        if by_cols:
            av = _bf(a_ref[...])
            for s in range(n_sh):
                cs = slice(s * width, (s + 1) * width)
                acc = _dot(av, w_ref[s], 1, 0 if mode == 'nn_cols' else 1)
                res = epi(acc, *[r[:, cs] for r in ex_refs]) if epi is not None else (acc,)
                for o, r in zip(out_refs, res):
                    o[:, cs] = r.astype(o.dtype)
        else:
            chunk = rows if mode == 'nn_rows' else cols
            acc = None
            for s in range(n_sh):
                part = _dot(_bf(a_ref[:, s * chunk:(s + 1) * chunk]), w_ref[s], 1, 0 if mode == 'nn_rows' else 1)
                acc = part if acc is None else acc + part
            res = epi(acc, *[r[...] for r in ex_refs]) if epi is not None else (acc,)
            for o, r in zip(out_refs, res):
                o[...] = r.astype(o.dtype)

    t, ka = a.shape
    return pl.pallas_call(
        body, name=name, grid=(t // tm, 1, 1),
        in_specs=[pl.BlockSpec((tm, ka), lambda i, j, k: (i, 0)),
                  pl.BlockSpec((n_sh, rows, cols), lambda i, j, k: (0, 0, 0))] + [s for _, s in extras] + [ANY] * n_dep,
        out_specs=[s for _, s in outs],
        out_shape=[s for s, _ in outs],
        compiler_params=_cparams(("parallel", "arbitrary", "arbitrary")),
    )(a, w, *[x for x, _ in extras], *deps)


def _sds(shape, dtype):
    return jax.ShapeDtypeStruct(shape, dtype)


def _row_tile(t, cap=ROW_TILE):
    return min(cap, t)


def _rms_fwd(name, x, g, deps=()):
    t, d = x.shape
    tm = _row_tile(t)

    def body(x_ref, g_ref, *rest):
        o_ref = rest[-1]
        xv = x_ref[...]
        r = lax.rsqrt(jnp.mean(xv * xv, axis=-1, keepdims=True) + EPS)
        o_ref[...] = (xv * r * g_ref[...]).astype(o_ref.dtype)

    return pl.pallas_call(
        body, name=name, grid=(t // tm,),
        in_specs=[pl.BlockSpec((tm, d), lambda i: (i, 0)), pl.BlockSpec((1, d), lambda i: (0, 0))] + [ANY] * len(deps),
        out_specs=pl.BlockSpec((tm, d), lambda i: (i, 0)),
        out_shape=_sds((t, d), BF16),
        compiler_params=_cparams(("parallel",)),
    )(x, g, *deps)


def _rms_bwd_rows(dy, xv, g, n):
    r = lax.rsqrt(jnp.sum(xv * xv, axis=-1, keepdims=True) / n + EPS)
    xh = xv * r
    dxh = dy * g
    dx = r * (dxh - xh * (jnp.sum(dxh * xh, axis=-1, keepdims=True) / n))
    return dx, dy * xh


def _ple_gate_bwd(name, dh, gate, e):
    t, d = dh.shape
    tm = _row_tile(t)

    def body(dh_ref, g_ref, e_ref, de_ref, dz_ref):
        dh_v, gt = dh_ref[...], g_ref[...].astype(F32)
        de_ref[...] = (dh_v * gt).astype(BF16)
        dz_ref[...] = (dh_v * e_ref[...].astype(F32) * (gt * (1.0 - gt))).astype(BF16)

    row = pl.BlockSpec((tm, d), lambda i: (i, 0))
    return pl.pallas_call(
        body, name=name, grid=(t // tm,), in_specs=[row, row, row], out_specs=[row, row],
        out_shape=[_sds((t, d), BF16), _sds((t, d), BF16)],
        compiler_params=_cparams(("parallel",)),
    )(dh, gate, e)


def _rope_half(v, cos, sin):
    half = v.shape[-1] // 2
    v1, v2 = v[:, :half], v[:, half:]
    return jnp.concatenate([v1 * cos - v2 * sin, v2 * cos + v1 * sin], axis=-1)


def _ret_consts():
    lg = jnp.log(1.0 - 2.0 ** (-5.0 - jnp.arange(RET_HEADS, dtype=F32)))
    idx = jnp.arange(RET_BLOCK, dtype=F32)
    chunk = jnp.floor(idx / CHUNK)
    dist = idx[:, None] - idx[None, :]
    same = chunk[:, None] == chunk[None, :]
    seen = jnp.where(same, jnp.abs(dist), jnp.where(chunk[None, :] < chunk[:, None], dist, jnp.inf))
    intra = jnp.exp(lg[:, None, None] * seen)
    qdec = jnp.exp(lg[:, None] * (idx + 1.0))
    kdec = jnp.exp(lg[:, None] * (RET_BLOCK - 1.0 - idx))
    cdec = jnp.exp(lg * RET_BLOCK)
    qdec = jnp.broadcast_to(qdec[:, :, None], (RET_HEADS, RET_BLOCK, RET_DK))
    kdec = jnp.broadcast_to(kdec[:, :, None], (RET_HEADS, RET_BLOCK, RET_DK))
    cdec = jnp.broadcast_to(cdec[:, None, None], (RET_HEADS, 1, RET_DV))
    return intra, qdec, kdec, cdec


def _ret_specs(rb, rev_nb=None):
    blk = (lambda i: i) if rev_nb is None else (lambda i: rev_nb - 1 - i)
    full = lambda shape: pl.BlockSpec(shape, lambda i: (0,) * len(shape))
    return dict(
        proj=pl.BlockSpec((rb, RET_IN), lambda i: (blk(i), 0)),
        tab=pl.BlockSpec((rb, RET_DK // 2), lambda i: (blk(i), 0)),
        vw=pl.BlockSpec((rb, RET_V_W), lambda i: (blk(i), 0)),
        st=pl.BlockSpec((rb // RET_BLOCK, RET_HEADS, RET_DK, RET_DV), lambda i: (blk(i), 0, 0, 0)),
        gn=full((RET_HEADS, 1, RET_DV)),
        intra=full((RET_HEADS, RET_BLOCK, RET_BLOCK)),
        dec=full((RET_HEADS, RET_BLOCK, RET_DK)),
        cdec=full((RET_HEADS, 1, RET_DV)),
    )


def _ret_fwd(proj, cos, sin, gn):
    t = proj.shape[0]
    rb = min(RET_ROWS, t)
    cpb = rb // RET_BLOCK
    intra, qdec, kdec, cdec = _ret_consts()
    sp = _ret_specs(rb)

    def body(proj_ref, cos_ref, sin_ref, gn_ref, intra_ref, qd_ref, kd_ref, cd_ref,
             gated_ref, outp_ref, st_ref, s_ref):
        @pl.when(pl.program_id(0) == 0)
        def _():
            s_ref[...] = jnp.zeros_like(s_ref)

        def chunk(c, carry):
            rows = pl.ds(pl.multiple_of(c * RET_BLOCK, RET_BLOCK), RET_BLOCK)
            cs, sn = cos_ref[rows, :], sin_ref[rows, :]
            for h in range(RET_HEADS):
                q = proj_ref[rows, h * RET_DK:(h + 1) * RET_DK].astype(F32)
                k = proj_ref[rows, RET_QK_W + h * RET_DK:RET_QK_W + (h + 1) * RET_DK].astype(F32)
                v = proj_ref[rows, 2 * RET_QK_W + h * RET_DV:2 * RET_QK_W + (h + 1) * RET_DV]
                g = proj_ref[rows, 2 * RET_QK_W + RET_V_W + h * RET_DV:
                             2 * RET_QK_W + RET_V_W + (h + 1) * RET_DV].astype(F32)
                qr = _rope_half(q, cs, sn)
                kr = _rope_half(k, cs, sn) * (RET_DK ** -0.5)
                qb, kb, vb = qr.astype(BF16), kr.astype(BF16), v
                sc = _dot(qb, kb, 1, 1) * intra_ref[h]
                inner = _dot(sc.astype(BF16), vb, 1, 0)
                s_old = s_ref[h]
                sb = s_old.astype(BF16)
                st_ref[c, h] = sb
                cross = _dot((qr * qd_ref[h]).astype(BF16), sb, 1, 0)
                out = inner + cross
                s_ref[h] = s_old * cd_ref[h] + _dot((kr * kd_ref[h]).astype(BF16), vb, 0, 0)
                r = lax.rsqrt(jnp.mean(out * out, axis=-1, keepdims=True) + EPS)
                y = out * r * gn_ref[h]
                cols = slice(h * RET_DV, (h + 1) * RET_DV)
                gated_ref[rows, cols] = (g * _sigmoid(g) * y).astype(BF16)
                outp_ref[rows, cols] = out
            return carry

        lax.fori_loop(0, cpb, chunk, 0)

    return pl.pallas_call(
        body, name="ret_fwd", grid=(t // rb,),
        in_specs=[sp['proj'], sp['tab'], sp['tab'], sp['gn'], sp['intra'], sp['dec'], sp['dec'], sp['cdec']],
        out_specs=[sp['vw'], sp['vw'], sp['st']],
        out_shape=[_sds((t, RET_V_W), BF16), _sds((t, RET_V_W), F32),
                   _sds((t // RET_BLOCK, RET_HEADS, RET_DK, RET_DV), BF16)],
        scratch_shapes=[pltpu.VMEM((RET_HEADS, RET_DK, RET_DV), F32)],
        compiler_params=_cparams(("arbitrary",)),
    )(proj, cos, sin, gn.reshape(RET_HEADS, 1, RET_DV), intra, qdec, kdec, cdec)


def _ret_gate_bwd_epi(dgt, out, g, gn):
    g = g.astype(F32)
    r = lax.rsqrt(jnp.mean(out * out, axis=-1, keepdims=True) + EPS)
    xh = out * r
    sg = _sigmoid(g)
    dgate = dgt * (xh * gn) * (sg * (1.0 + g * (1.0 - sg)))
    dy = dgt * (g * sg)
    dxh = dy * gn
    dout = r * (dxh - xh * jnp.mean(dxh * xh, axis=-1, keepdims=True))
    return dout, dgate, jnp.sum(dy * xh, axis=0, keepdims=True)


def _ret_bwd(proj, cos, sin, states, dout, dgate, deps=()):
    t = proj.shape[0]
    rb = min(RET_ROWS, t)
    cpb = rb // RET_BLOCK
    nb = t // rb
    intra, qdec, kdec, cdec = _ret_consts()
    sp = _ret_specs(rb, rev_nb=nb)

    def body(proj_ref, cos_ref, sin_ref, intra_ref, qd_ref, kd_ref, cd_ref, st_ref, dout_ref, dgate_ref, *rest):
        dproj_ref, ds_ref = rest[len(deps):]

        @pl.when(pl.program_id(0) == 0)
        def _():
            ds_ref[...] = jnp.zeros_like(ds_ref)

        def chunk(cc, carry):
            c = cpb - 1 - cc
            rows = pl.ds(pl.multiple_of(c * RET_BLOCK, RET_BLOCK), RET_BLOCK)
            cs, sn = cos_ref[rows, :], sin_ref[rows, :]
            for h in range(RET_HEADS):
                q = proj_ref[rows, h * RET_DK:(h + 1) * RET_DK].astype(F32)
                k = proj_ref[rows, RET_QK_W + h * RET_DK:RET_QK_W + (h + 1) * RET_DK].astype(F32)
                v = proj_ref[rows, 2 * RET_QK_W + h * RET_DV:2 * RET_QK_W + (h + 1) * RET_DV]
                cols = slice(h * RET_DV, (h + 1) * RET_DV)
                qr = _rope_half(q, cs, sn)
                kr = _rope_half(k, cs, sn) * (RET_DK ** -0.5)
                qb, kb, vb = qr.astype(BF16), kr.astype(BF16), v
                qdb = (qr * qd_ref[h]).astype(BF16)
                kdb = (kr * kd_ref[h]).astype(BF16)
                doutb = dout_ref[rows, cols]
                itr = intra_ref[h]
                pb = (_dot(qb, kb, 1, 1) * itr).astype(BF16)
                dv = _dot(pb, doutb, 0, 0)
                dsc = (_dot(doutb, vb, 1, 1) * itr).astype(BF16)
                dq = _dot(dsc, kb, 1, 0)
                dk = _dot(dsc, qb, 0, 0)
                dq = dq + _dot(doutb, st_ref[c, h], 1, 1) * qd_ref[h]
                ds_new = ds_ref[h]
                dsb = ds_new.astype(BF16)
                dk = dk + _dot(vb, dsb, 1, 1) * kd_ref[h]
                dv = dv + _dot(kdb, dsb, 1, 0)
                ds_ref[h] = ds_new * cd_ref[h] + _dot(qdb, doutb, 0, 0)
                dproj_ref[rows, h * RET_DK:(h + 1) * RET_DK] = _rope_half(dq, cs, -sn).astype(BF16)
                dproj_ref[rows, RET_QK_W + h * RET_DK:RET_QK_W + (h + 1) * RET_DK] = (
                    _rope_half(dk * (RET_DK ** -0.5), cs, -sn).astype(BF16))
                dproj_ref[rows, 2 * RET_QK_W + h * RET_DV:2 * RET_QK_W + (h + 1) * RET_DV] = dv.astype(BF16)
                dproj_ref[rows, 2 * RET_QK_W + RET_V_W + h * RET_DV:
                          2 * RET_QK_W + RET_V_W + (h + 1) * RET_DV] = dgate_ref[rows, cols]
            return carry

        lax.fori_loop(0, cpb, chunk, 0)

    return pl.pallas_call(
        body, name="ret_bwd", grid=(nb,),
        in_specs=[sp['proj'], sp['tab'], sp['tab'], sp['intra'], sp['dec'], sp['dec'], sp['cdec'],
                  sp['st'], sp['vw'], sp['vw']] + [ANY] * len(deps),
        out_specs=sp['proj'],
        out_shape=_sds((t, RET_IN), BF16),
        scratch_shapes=[pltpu.VMEM((RET_HEADS, RET_DK, RET_DV), F32)],
        compiler_params=_cparams(("arbitrary",)),
    )(proj, cos, sin, intra, qdec, kdec, cdec, states, dout, dgate, *deps)


def _spread_rope(a):
    return jnp.pad(a, [(0, 0)] * (a.ndim - 1) + [(0, MLA_ROPE)])


def _gather_rope(a):
    return a[..., :a.shape[-1] - MLA_ROPE]


def _rope_tables(t, zero):
    pos = jnp.arange(t, dtype=F32)[:, None] + zero
    inv = 1.0 / (ROPE_THETA ** (jnp.arange(0, RET_DK, 2, dtype=F32) / RET_DK))
    ang = pos * inv[None, :]
    return jnp.cos(ang), jnp.sin(ang), _mla_tables(t, pos)


def _mla_tables(t, pos):
    half = MLA_ROPE // 2
    inv = 1.0 / (ROPE_THETA ** (jnp.arange(0, MLA_ROPE, 2, dtype=F32) / MLA_ROPE))
    ang = pos * inv[None, :]
    cos, sin = jnp.cos(ang), jnp.sin(ang)
    z = jnp.zeros((t, half), F32)
    c = jnp.concatenate([cos, cos, z, z], axis=1)
    s1 = jnp.concatenate([-sin, z, z, z], axis=1)
    s2 = jnp.concatenate([z, sin, z, z], axis=1)
    return c, s1, s2


def _rope_tile(r, c, s1, s2):
    return r * c + pltpu.roll(r, 96, 1) * s1 + pltpu.roll(r, 32, 1) * s2


def _mla_mid(proj2, qa, kva):
    t = proj2.shape[0]
    tm = _row_tile(t)

    def body(p_ref, qa_ref, kva_ref, cq_ref, ckv_ref):
        cq = p_ref[:, :MLA_Q_RANK]
        ckv = p_ref[:, MLA_Q_RANK:MLA_Q_RANK + MLA_KV_RANK]
        rq = lax.rsqrt(jnp.mean(cq * cq, axis=-1, keepdims=True) + EPS)
        rkv = lax.rsqrt(jnp.mean(ckv * ckv, axis=-1, keepdims=True) + EPS)
        cq_ref[...] = (cq * rq * qa_ref[...]).astype(BF16)
        ckv_ref[...] = (ckv * rkv * kva_ref[...]).astype(BF16)

    return pl.pallas_call(
        body, name="mla_mid", grid=(t // tm,),
        in_specs=[pl.BlockSpec((tm, MLA_IN_PAD), lambda i: (i, 0)),
                  pl.BlockSpec((1, MLA_Q_RANK), lambda i: (0, 0)),
                  pl.BlockSpec((1, MLA_KV_RANK), lambda i: (0, 0))],
        out_specs=[pl.BlockSpec((tm, MLA_Q_RANK), lambda i: (i, 0)),
                   pl.BlockSpec((tm, MLA_KV_RANK), lambda i: (i, 0))],
        out_shape=[_sds((t, MLA_Q_RANK), BF16), _sds((t, MLA_KV_RANK), BF16)],
        compiler_params=_cparams(("parallel",)),
    )(proj2, qa, kva)


def _mla_mid_bwd(proj2, qa, kva, dcq, dckv, dkr):
    t = proj2.shape[0]
    tm = _row_tile(t)

    def body(p_ref, qa_ref, kva_ref, dcq_ref, dckv_ref, dkr_ref, dp_ref, dqa_ref, dkva_ref):
        @pl.when(pl.program_id(0) == 0)
        def _():
            dqa_ref[...] = jnp.zeros_like(dqa_ref)
            dkva_ref[...] = jnp.zeros_like(dkva_ref)

        dxq, dgq = _rms_bwd_rows(dcq_ref[...], p_ref[:, :MLA_Q_RANK], qa_ref[...], MLA_Q_RANK)
        dxk, dgk = _rms_bwd_rows(dckv_ref[...], p_ref[:, MLA_Q_RANK:MLA_Q_RANK + MLA_KV_RANK], kva_ref[...],
                                 MLA_KV_RANK)
        dp_ref[:, :MLA_Q_RANK] = dxq.astype(BF16)
        dp_ref[:, MLA_Q_RANK:MLA_Q_RANK + MLA_KV_RANK] = dxk.astype(BF16)
        dp_ref[:, MLA_Q_RANK + MLA_KV_RANK:] = dkr_ref[...].astype(BF16)
        dqa_ref[...] += jnp.sum(dgq, axis=0, keepdims=True)
        dkva_ref[...] += jnp.sum(dgk, axis=0, keepdims=True)

    return pl.pallas_call(
        body, name="mla_mid_bwd", grid=(t // tm,),
        in_specs=[pl.BlockSpec((tm, MLA_IN_PAD), lambda i: (i, 0)),
                  pl.BlockSpec((1, MLA_Q_RANK), lambda i: (0, 0)),
                  pl.BlockSpec((1, MLA_KV_RANK), lambda i: (0, 0)),
                  pl.BlockSpec((tm, MLA_Q_RANK), lambda i: (i, 0)),
                  pl.BlockSpec((tm, MLA_KV_RANK), lambda i: (i, 0)),
                  pl.BlockSpec((tm, 128), lambda i: (i, 0))],
        out_specs=[pl.BlockSpec((tm, MLA_IN_PAD), lambda i: (i, 0)),
                   pl.BlockSpec((1, MLA_Q_RANK), lambda i: (0, 0)),
                   pl.BlockSpec((1, MLA_KV_RANK), lambda i: (0, 0))],
        out_shape=[_sds((t, MLA_IN_PAD), BF16), _sds((1, MLA_Q_RANK), F32), _sds((1, MLA_KV_RANK), F32)],
        compiler_params=_cparams(("arbitrary",)),
    )(proj2, qa, kva, dcq, dckv, dkr)


def _mla_prep_specs(t, tm):
    head = lambda w: pl.BlockSpec((None, tm, w), lambda i, h: (h, i, 0))
    return dict(
        head256=head(MLA_HD_PAD), head128=head(MLA_VD),
        cols256=pl.BlockSpec((tm, MLA_HD_PAD), lambda i, h: (i, h)),
        cq=pl.BlockSpec((tm, MLA_Q_RANK), lambda i, h: (i, 0)),
        ckv=pl.BlockSpec((tm, MLA_KV_RANK), lambda i, h: (i, 0)),
        wuq=pl.BlockSpec((None, MLA_Q_RANK, MLA_HD_PAD), lambda i, h: (h, 0, 0)),
        wukv=pl.BlockSpec((None, MLA_KV_RANK, MLA_HD_PAD), lambda i, h: (h, 0, 0)),
        kr=pl.BlockSpec((tm, 128), lambda i, h: (i, (MLA_Q_RANK + MLA_KV_RANK) // 128)),
        gain=pl.BlockSpec((1, MLA_HD_PAD), lambda i, h: (0, 0)),
        tab=pl.BlockSpec((tm, 128), lambda i, h: (i, 0)),
    )


def _mla_prep(cq, ckv, wuq, wukv, proj2, gq, gk, tabs):
    t = cq.shape[0]
    tm = _row_tile(t, PREP_ROWS)
    sp = _mla_prep_specs(t, tm)

    def body(cq_ref, ckv_ref, wuq_ref, wukv_ref, kr_ref, gq_ref, gk_ref, c_ref, s1_ref, s2_ref,
             qh_ref, kh_ref, vh_ref):
        c, s1, s2 = c_ref[...], s1_ref[...], s2_ref[...]

        def norm_rope(xv, gain):
            r = lax.rsqrt(jnp.sum(xv * xv, axis=-1, keepdims=True) / MLA_QKD + EPS)
            y = xv * r * gain
            return jnp.concatenate([y[:, :MLA_NOPE], _rope_tile(y[:, MLA_NOPE:], c, s1, s2)], axis=-1)

        kvv = _dot(ckv_ref[...], wukv_ref[...], 1, 0)
        qh_ref[...] = norm_rope(_dot(cq_ref[...], wuq_ref[...], 1, 0), gq_ref[...]).astype(BF16)
        kf = jnp.concatenate([kvv[:, :MLA_NOPE], kr_ref[...]], axis=-1)
        kh_ref[...] = norm_rope(kf, gk_ref[...]).astype(BF16)
        vh_ref[...] = jnp.concatenate([kvv[:, MLA_NOPE:], jnp.ones((tm, MLA_VD), F32)], axis=-1).astype(BF16)

    return pl.pallas_call(
        body, name="mla_prep", grid=(t // tm, MLA_HEADS),
        in_specs=[sp['cq'], sp['ckv'], sp['wuq'], sp['wukv'], sp['kr'], sp['gain'], sp['gain'],
                  sp['tab'], sp['tab'], sp['tab']],
        out_specs=[sp['head256'], sp['head256'], sp['head256']],
        out_shape=[_sds((MLA_HEADS, t, MLA_HD_PAD), BF16), _sds((MLA_HEADS, t, MLA_HD_PAD), BF16),
                   _sds((MLA_HEADS, t, 2 * MLA_VD), BF16)],
        compiler_params=_cparams(("parallel", "arbitrary")),
    )(cq, ckv, wuq, wukv, proj2, gq, gk, *tabs)


def _mla_prep_bwd(cq, ckv, wuq, wukv, proj2, gq, gk, tabs, dqt, dkh, dvh):
    t = cq.shape[0]
    tm = _row_tile(t, PREP_ROWS)
    ab = dqt.shape[-1]
    sp = _mla_prep_specs(t, tm)

    def body(cq_ref, ckv_ref, wuq_ref, wukv_ref, kr_ref, gq_ref, gk_ref, c_ref, s1_ref, s2_ref,
             dqt_ref, dkh_ref, dvh_ref, dq_ref, dkv_ref, dkr_ref, dgq_ref, dgk_ref):
        dqh = jnp.concatenate([dqt_ref[b].T for b in range(tm // ab)], axis=0)
        i, h = pl.program_id(0), pl.program_id(1)

        @pl.when((i == 0) & (h == 0))
        def _():
            dgq_ref[...] = jnp.zeros_like(dgq_ref)
            dgk_ref[...] = jnp.zeros_like(dgk_ref)

        @pl.when(h == 0)
        def _():
            dkr_ref[...] = jnp.zeros_like(dkr_ref)

        c, s1, s2 = c_ref[...], s1_ref[...], s2_ref[...]

        def back(xv, gain, dout):
            dy = jnp.concatenate([dout[:, :MLA_NOPE], _rope_tile(dout[:, MLA_NOPE:], c, -s1, -s2)], axis=-1)
            return _rms_bwd_rows(dy, xv, gain, MLA_QKD)

        kvv = _dot(ckv_ref[...], wukv_ref[...], 1, 0)
        dxq, dgq = back(_dot(cq_ref[...], wuq_ref[...], 1, 0), gq_ref[...], dqh)
        kf = jnp.concatenate([kvv[:, :MLA_NOPE], kr_ref[...]], axis=-1)
        dxk, dgk = back(kf, gk_ref[...], dkh_ref[...])
        dq_ref[...] = dxq.astype(BF16)
        dkv_ref[...] = jnp.concatenate([dxk[:, :MLA_NOPE], dvh_ref[...]], axis=-1).astype(BF16)
        dkr_ref[...] += dxk[:, MLA_NOPE:]
        dgq_ref[...] += jnp.sum(dgq, axis=0, keepdims=True)
        dgk_ref[...] += jnp.sum(dgk, axis=0, keepdims=True)

    return pl.pallas_call(
        body, name="mla_prep_bwd", grid=(t // tm, MLA_HEADS),
        in_specs=[sp['cq'], sp['ckv'], sp['wuq'], sp['wukv'], sp['kr'], sp['gain'], sp['gain'],
                  sp['tab'], sp['tab'], sp['tab'],
                  pl.BlockSpec((None, tm // ab, MLA_HD_PAD, ab), lambda i, h: (h, i, 0, 0)),
                  sp['head256'], sp['head128']],
        out_specs=[sp['cols256'], sp['cols256'], sp['tab'], sp['gain'], sp['gain']],
        out_shape=[_sds((t, MLA_HEADS * MLA_HD_PAD), BF16), _sds((t, MLA_HEADS * MLA_HD_PAD), BF16),
                   _sds((t, 128), F32), _sds((1, MLA_HD_PAD), F32), _sds((1, MLA_HD_PAD), F32)],
        compiler_params=_cparams(("arbitrary", "arbitrary")),
    )(cq, ckv, wuq, wukv, proj2, gq, gk, *tabs, dqt, dkh, dvh)


def _chunk_visible(rows, cols, row_off, col_off):
    rq = lax.shift_right_logical(lax.broadcasted_iota(jnp.int32, (rows, cols), 0) + row_off, 6)
    ck = lax.shift_right_logical(lax.broadcasted_iota(jnp.int32, (rows, cols), 1) + col_off, 6)
    return ck <= rq


def _rows_to_lanes(col):
    return col.T[:8, :]


def _attn_fwd(qh, kh, vh):
    t = qh.shape[1]
    ab = min(ATT_BLOCK, t)
    tq = min(ATT_QROWS, t)
    r = tq // ab
    hg = ATT_HEADS

    def body(q_ref, k_ref, v_ref, o_ref, lse_ref, acc_ref):
        n_un = pl.program_id(1) * r
        acc_ref[...] = jnp.zeros_like(acc_ref)

        def step(b, ms, diag):
            rows = pl.ds(pl.multiple_of(b * ab, ab), ab)
            out = []
            for hh in range(hg):
                m = ms[hh]
                s = _dot(q_ref[hh], k_ref[hh, rows, :], 1, 1)
                if diag is not None:
                    s = jnp.where(_chunk_visible(tq, ab, 0, diag * ab), s, -1e30)
                m_new = jnp.maximum(m, jnp.max(s, axis=-1, keepdims=True))
                p = jnp.exp2((s - m_new) * ATT_EXP2).astype(BF16)
                acc_ref[hh] = jnp.exp2((m - m_new) * ATT_EXP2) * acc_ref[hh] + _dot(p, v_ref[hh, rows, :], 1, 0)
                out.append(m_new)
            return tuple(out)

        ms = tuple(jnp.full((tq, 1), -1e30, F32) for _ in range(hg))
        ms = lax.fori_loop(0, n_un, lambda b, st: step(b, st, None), ms)
        for d in range(r):
            ms = step(n_un + d, ms, d)
        for hh in range(hg):
            l = acc_ref[hh, :, MLA_VD:]
            o_ref[:, hh * MLA_VD:(hh + 1) * MLA_VD] = acc_ref[hh, :, :MLA_VD] / l
            lse_t = _rows_to_lanes(ms[hh] * ATT_EXP2 + jnp.log(l) * LOG2E)
            for d in range(r):
                lse_ref[hh, d] = lse_t[:, d * ab:(d + 1) * ab]

    return pl.pallas_call(
        body, name="mla_attn", grid=(MLA_HEADS // hg, t // tq),
        in_specs=[pl.BlockSpec((hg, tq, MLA_HD_PAD), lambda g, i: (g, i, 0)),
                  pl.BlockSpec((hg, t, MLA_HD_PAD), lambda g, i: (g, 0, 0)),
                  pl.BlockSpec((hg, t, 2 * MLA_VD), lambda g, i: (g, 0, 0))],
        out_specs=[pl.BlockSpec((tq, hg * MLA_VD), lambda g, i: (i, g)),
                   pl.BlockSpec((hg, r, 8, ab), lambda g, i: (g, i, 0, 0))],
        out_shape=[_sds((t, MLA_HEADS * MLA_VD), F32), _sds((MLA_HEADS, t // ab, 8, ab), F32)],
        scratch_shapes=[pltpu.VMEM((hg, tq, 2 * MLA_VD), F32)],
        compiler_params=_cparams(("parallel", "arbitrary")),
    )(qh, kh, vh)


def _attn_bwd(qh, kh, vh, dob, o, lse_t):
    t = qh.shape[1]
    ab = min(ATT_BLOCK, t)
    kb = min(ATT_KROWS, t)
    r = kb // ab
    nq = t // ab
    hg = ATT_HEADS

    def body(q_ref, k_ref, v_ref, do_ref, o_ref, lse_ref, dqt_ref, dk_ref, dv_ref, dl_ref):
        j = pl.program_id(1)

        @pl.when(j == 0)
        def _():
            dqt_ref[...] = jnp.zeros_like(dqt_ref)
            ones = jnp.ones((8, MLA_VD), F32)

            def delta(b, carry):
                rows = pl.ds(pl.multiple_of(b * ab, ab), ab)
                for hh in range(hg):
                    cols = slice(hh * MLA_VD, (hh + 1) * MLA_VD)
                    prod = do_ref[rows, cols].astype(F32) * o_ref[rows, cols]
                    dl_ref[hh, b] = lax.dot_general(ones, prod, (((1,), (1,)), ((), ())),
                                                    precision=lax.Precision.HIGHEST, preferred_element_type=F32)
                return carry

            lax.fori_loop(0, nq, delta, 0)

        ks = [k_ref[hh] for hh in range(hg)]
        vs = [v_ref[hh, :, :MLA_VD] for hh in range(hg)]
        kts = [k.T for k in ks]

        dk_ref[...] = jnp.zeros_like(dk_ref)
        dv_ref[...] = jnp.zeros_like(dv_ref)

        def step(b, carry, diag):
            rows = pl.ds(pl.multiple_of(b * ab, ab), ab)
            hi = kb if diag is None else (diag + 1) * ab
            for hh in range(hg):
                q = q_ref[hh, rows, :]
                do = do_ref[rows, hh * MLA_VD:(hh + 1) * MLA_VD]
                s_t = _dot(ks[hh][:hi], q, 1, 1)
                if diag is not None:
                    key_chunk = lax.shift_right_logical(lax.broadcasted_iota(jnp.int32, (hi, ab), 0), 6)
                    query_chunk = lax.shift_right_logical(
                        lax.broadcasted_iota(jnp.int32, (hi, ab), 1) + diag * ab, 6)
                    s_t = jnp.where(key_chunk <= query_chunk, s_t, -1e30)
                p_t = jnp.exp2(s_t * ATT_EXP2 - lse_ref[hh, b][0:1, :])
                dp_t = _dot(vs[hh][:hi], do, 1, 1)
                ds_t = (p_t * (dp_t - dl_ref[hh, b][0:1, :]) * ATT_SCALE).astype(BF16)
                dqt_ref[hh, b] += _dot(kts[hh][:, :hi], ds_t, 1, 0)
                dk_ref[hh, :hi] += _dot(ds_t, q, 1, 0)
                dv_ref[hh, :hi] += _dot(p_t.astype(BF16), do, 1, 0)
            return carry

        for d in range(r):
            step(j * r + d, 0, d)
        lax.fori_loop((j + 1) * r, nq, lambda b, c: step(b, c, None), 0)

    whole = lambda w: pl.BlockSpec((hg, t, w), lambda g, j: (g, 0, 0))
    blk = lambda w: pl.BlockSpec((hg, kb, w), lambda g, j: (g, j, 0))
    stat = pl.BlockSpec((hg, nq, 8, ab), lambda g, j: (g, 0, 0, 0))
    cols = pl.BlockSpec((t, hg * MLA_VD), lambda g, j: (0, g))
    return pl.pallas_call(
        body, name="mla_attn_bwd", grid=(MLA_HEADS // hg, t // kb),
        in_specs=[whole(MLA_HD_PAD), blk(MLA_HD_PAD), blk(2 * MLA_VD),
                  cols, cols, stat],
        out_specs=[pl.BlockSpec((hg, nq, MLA_HD_PAD, ab), lambda g, j: (g, 0, 0, 0)), blk(MLA_HD_PAD), blk(MLA_VD)],
        out_shape=[_sds((MLA_HEADS, nq, MLA_HD_PAD, ab), F32), _sds((MLA_HEADS, t, MLA_HD_PAD), F32),
                   _sds((MLA_HEADS, t, MLA_VD), F32)],
        scratch_shapes=[pltpu.VMEM((hg, nq, 8, ab), F32)],
        compiler_params=_cparams(("parallel", "arbitrary")),
    )(qh, kh, vh, dob, o, lse_t)


VEC = pl.BlockSpec((1, D_MODEL), lambda i, j, k: (0, 0))


def _rows(tm, width):
    return pl.BlockSpec((tm, width), lambda i, j, k: (i, 0))


def _residual_epi(next_gain):
    if next_gain is None:
        return [], lambda acc, hv: (acc + hv,)

    def epi(acc, hv, g):
        h_new = acc + hv
        r = lax.rsqrt(jnp.mean(h_new * h_new, axis=-1, keepdims=True) + EPS)
        return h_new, h_new * r * g

    return [(next_gain, VEC)], epi


def _residual_outs(t, row, next_gain):
    outs = [(_sds((t, D_MODEL), F32), row)]
    return outs + ([(_sds((t, D_MODEL), BF16), row)] if next_gain is not None else [])


def _mlp_fwd(l, h, hn, w1g, fetch_w2, next_gain):
    t = h.shape[0]
    tm = _row_tile(t, 512)

    def relu2(acc):
        r = jnp.maximum(acc, 0.0)
        return (r * r,)

    (u,) = _mm_rows(f"mlp_up{l}", tm, hn, w1g, 'nn_cols', [(_sds((t, D_FF), BF16), _rows(tm, D_FF))], epi=relu2)
    w2g = fetch_w2((u,))
    row = _rows(tm, D_MODEL)
    more, epi = _residual_epi(next_gain)
    h2, hn_next = _mm_rows(f"mlp_down{l}", tm, u, w2g, 'nn_rows', _residual_outs(t, row, next_gain),
                           extras=[(h, row)] + more, epi=epi)
    return h2, hn_next, (h, hn, u, w1g, w2g)


def _norm_bwd_outs(t, tm):
    return [(_sds((t, D_MODEL), F32), pl.BlockSpec((tm, D_MODEL), lambda i, j, k: (i, 0))),
            (_sds((t // tm, 1, D_MODEL), F32), pl.BlockSpec((None, 1, D_MODEL), lambda i, j, k: (i, 0, 0)))]


def _norm_bwd_epi(acc, xv, res, g):
    dx, dgr = _rms_bwd_rows(acc, xv, g, D_MODEL)
    return res + dx, jnp.sum(dgr, axis=0, keepdims=True)


def _mlp_bwd(l, dh, saved, norm_g, emit_w2=None, emit_w1=None):
    h, hn, u, w1g, w2g = saved
    t = h.shape[0]
    tm = _row_tile(t, 512)
    nsh, _, wsh = w1g.shape
    wide = _rows(tm, D_FF)
    (da,) = _mm_rows(f"mlp_du{l}", tm, dh, w2g, 'nt_rows', [(_sds((t, D_FF), BF16), wide)], extras=[(u, wide)],
                     epi=lambda acc, uv: (2.0 * jnp.sqrt(uv.astype(F32)) * acc,))
    tw = _row_tile(t, 512)
    (dw2,) = _mm(f"mlp_dw2{l}", (1, 1, t // tw),
                 u, pl.BlockSpec((tw, D_FF), lambda i, j, k: (k, 0)),
                 dh, pl.BlockSpec((tw, D_MODEL), lambda i, j, k: (k, 0)), (0, 0),
                 [(_sds((D_FF, D_MODEL), BF16), pl.BlockSpec((D_FF, D_MODEL), lambda i, j, k: (0, 0)))])
    dw2 = dw2.reshape(nsh, wsh, D_MODEL)
    (dw1,) = _mm(f"mlp_dw1{l}", (1, 1, t // tw),
                 hn, pl.BlockSpec((tw, D_MODEL), lambda i, j, k: (k, 0)),
                 da, pl.BlockSpec((tw, D_FF), lambda i, j, k: (k, 0)), (0, 0),
                 [(_sds((nsh, D_MODEL, wsh), BF16), pl.BlockSpec((nsh, D_MODEL, wsh), lambda i, j, k: (0, 0, 0)))],
                 split=wsh, deps=emit_w2(dw2) if emit_w2 else ())
    row = _rows(tm, D_MODEL)
    dh_in, dg = _mm_rows(f"mlp_dhn{l}", tm, da, w1g, 'nt_cols', _norm_bwd_outs(t, tm),
                         extras=[(h, row), (dh, row), (norm_g, VEC)], epi=_norm_bwd_epi,
                         deps=emit_w1(dw1) if emit_w1 else ())
    return dh_in, jnp.sum(dg, axis=0), dw1, dw2


def _ple_fwd(l, h, hn, p, wg, wp, next_gain, target=None):
    t = h.shape[0]
    tm = _row_tile(t, 512)
    row = pl.BlockSpec((tm, D_MODEL), lambda i, j, k: (i, 0))
    full = lambda r: pl.BlockSpec((r, D_MODEL), lambda i, j, k: (0, 0))
    f32_row, bf_row = (_sds((t, D_MODEL), F32), row), (_sds((t, D_MODEL), BF16), row)
    common = [(h, row), (p, pl.BlockSpec((None, None, tm, PLE_DIM), lambda i, j, k: (l, 0, i, 0))),
              (wp, full(PLE_DIM))]
    if target is not None:
        def loss_epi(acc, hv, pv, wpv, tv):
            gt = _sigmoid(acc)
            ev = _dot(_bf(pv), wpv, 1, 0)
            err = hv + gt * ev - tv
            sq = jnp.sum(jnp.sum(err * err, axis=-1, keepdims=True), axis=0, keepdims=True)
            return err / D_MODEL, gt, ev, jnp.broadcast_to(sq, (8, 128))

        dy, gate, e, sq = _mm(f"ple_gate{l}", (t // tm, 1, 1), hn, row, wg, full(D_MODEL), (1, 0),
                              [f32_row, bf_row, bf_row, (_sds((t // tm, 8, 128), F32),
                                                         pl.BlockSpec((None, 8, 128), lambda i, j, k: (i, 0, 0)))],
                              extras=common + [(target, row)], epi=loss_epi)
        return dy, jnp.sum(sq, axis=0), (h, hn, gate, e)

    def gate_epi(acc, hv, pv, wpv, *gain):
        gt = _sigmoid(acc)
        ev = _dot(_bf(pv), wpv, 1, 0)
        h_new = hv + gt * ev
        if not gain:
            return h_new, gt, ev
        r = lax.rsqrt(jnp.mean(h_new * h_new, axis=-1, keepdims=True) + EPS)
        return h_new, gt, ev, h_new * r * gain[0]

    res = _mm(f"ple_gate{l}", (t // tm, 1, 1), hn, row, wg, full(D_MODEL), (1, 0),
              [f32_row, bf_row, bf_row] + ([bf_row] if next_gain is not None else []),
              extras=common + ([(next_gain, VEC)] if next_gain is not None else []), epi=gate_epi)
    h_out, gate, e = res[0], res[1], res[2]
    return h_out, (res[3] if next_gain is not None else None), (h, hn, gate, e)


def _ple_bwd(l, dh, saved, p, norm_g, wg, deps=(), emit=None):
    h, hn, gate, e = saved
    t = h.shape[0]
    tm = _row_tile(t)
    tk = _row_tile(t, 512)
    de, dz = _ple_gate_bwd(f"ple_gate_bwd{l}", dh, gate, e)
    full = lambda r: pl.BlockSpec((r, D_MODEL), lambda i, j, k: (0, 0))
    rowk = pl.BlockSpec((tk, D_MODEL), lambda i, j, k: (k, 0))
    (dwp,) = _mm(f"ple_dwp{l}", (1, 1, t // tk),
                 p, pl.BlockSpec((None, None, tk, PLE_DIM), lambda i, j, k: (l, 0, k, 0)),
                 de, rowk, (0, 0), [(_sds((PLE_DIM, D_MODEL), BF16), full(PLE_DIM))], deps=deps)
    (dwg,) = _mm(f"ple_dwg{l}", (1, 1, t // tk), hn, rowk, dz, rowk, (0, 0),
                 [(_sds((D_MODEL, D_MODEL), BF16), full(D_MODEL))])
    row = pl.BlockSpec((tm, D_MODEL), lambda i, j, k: (i, 0))
    dh_in, dg = _mm(f"ple_dhn{l}", (t // tm, 1, 1), dz, row, wg, full(D_MODEL), (1, 1),
                    _norm_bwd_outs(t, tm), extras=[(h, row), (dh, row), (norm_g, VEC)], epi=_norm_bwd_epi,
                    deps=emit(dwg, dwp) if emit else ())
    return dh_in, jnp.sum(dg, axis=0), dwg, dwp


def _ret_layer_fwd(x, norm_g, wri, fetch_wro, gn, cos, sin, next_gain, hn=None, deps=()):
    t = x.shape[0]
    tm = _row_tile(t)
    nsh, _, wsh = wri.shape
    if hn is None:
        hn = _rms_fwd("mix_norm0", x, norm_g)
    tp = _row_tile(t, 512)
    (proj,) = _mm_rows("ret_in", tp, hn, wri, 'nn_cols', [(_sds((t, RET_IN), BF16), _rows(tp, RET_IN))], deps=deps)
    gated, outp, states = _ret_fwd(proj, cos, sin, gn)
    wro = fetch_wro((gated,))
    row = _rows(tp, D_MODEL)
    more, epi = _residual_epi(next_gain)
    h1, hn_next = _mm_rows("ret_out", tp, gated, wro.reshape(RET_HEADS, RET_DV, D_MODEL), 'nn_rows',
                           _residual_outs(t, row, next_gain), extras=[(x, row)] + more, epi=epi)
    return h1, hn_next, (x, hn, proj, gated, outp, states, wro)


def _ret_layer_bwd(dh, saved, norm_g, wri, gn, cos, sin, emit_out, emit_in, deps=()):
    x, hn, proj, gated, outp, states, wro = saved
    t = x.shape[0]
    tm = _row_tile(t)
    tk = _row_tile(t, 512)
    nsh, _, wsh = wri.shape
    tg = _row_tile(t, 512)
    vw = _rows(tg, RET_V_W)
    dout, dgate, dgn = _mm_rows(
        "ret_dgate", tg, dh, wro.reshape(RET_HEADS, RET_DV, D_MODEL), 'nt_rows',
        [(_sds((t, RET_V_W), BF16), vw), (_sds((t, RET_V_W), BF16), vw),
         (_sds((t // tg, 1, RET_V_W), F32), pl.BlockSpec((None, 1, RET_V_W), lambda i, j, k: (i, 0, 0)))],
        extras=[(outp, vw), (proj, pl.BlockSpec((tg, RET_V_W), lambda i, j, k: (i, (RET_IN - RET_V_W) // RET_V_W))),
                (gn.reshape(1, RET_V_W), pl.BlockSpec((1, RET_V_W), lambda i, j, k: (0, 0)))],
        epi=_ret_gate_bwd_epi, deps=deps)
    dgn = jnp.sum(dgn, axis=0)
    (dwro,) = _mm("ret_dwro", (1, 1, t // tk),
                  gated, pl.BlockSpec((tk, RET_V_W), lambda i, j, k: (k, 0)),
                  dh, pl.BlockSpec((tk, D_MODEL), lambda i, j, k: (k, 0)), (0, 0),
                  [(_sds((RET_V_W, D_MODEL), BF16), pl.BlockSpec((RET_V_W, D_MODEL), lambda i, j, k: (0, 0)))])
    dproj = _ret_bwd(proj, cos, sin, states, dout, dgate, deps=emit_out(dwro))
    half = nsh // 2
    (dwri,) = _mm("ret_dwri", (2, 1, t // tk),
                  hn, pl.BlockSpec((tk, D_MODEL), lambda i, j, k: (k, 0)),
                  dproj, pl.BlockSpec((tk, half * wsh), lambda i, j, k: (k, i)), (0, 0),
                  [(_sds((nsh, D_MODEL, wsh), BF16), pl.BlockSpec((half, D_MODEL, wsh), lambda i, j, k: (i, 0, 0)))],
                  split=wsh)
    deps = emit_in(dwri)
    td = _row_tile(t, 256)
    row = _rows(td, D_MODEL)
    dx, dg = _mm_rows("ret_dhn", td, dproj, wri, 'nt_cols', _norm_bwd_outs(t, td),
                      extras=[(x, row), (dh, row), (norm_g, VEC)], epi=_norm_bwd_epi, deps=deps)
    return dx, jnp.sum(dg, axis=0), dgn.reshape(RET_HEADS, RET_DV)


def _mla_layer_fwd(h, hn, fetch, qa, kva, gq, gk, tabs, next_gain):
    t = h.shape[0]
    tm = _row_tile(t)
    row = pl.BlockSpec((tm, D_MODEL), lambda i, j, k: (i, 0))
    wmi = fetch('mla_in', (h,))['mla_w_in']
    (proj2,) = _mm("mla_in", (t // tm, 1, 1), hn, row,
                   wmi, pl.BlockSpec((D_MODEL, MLA_IN_PAD), lambda i, j, k: (0, 0)), (1, 0),
                   [(_sds((t, MLA_IN_PAD), F32), pl.BlockSpec((tm, MLA_IN_PAD), lambda i, j, k: (i, 0)))])
    cq, ckv = _mla_mid(proj2, qa, kva)
    up = fetch('mla_up', (cq,))
    wuq, wukv = up['mla_w_uq'], up['mla_w_ukv']
    qh, kh, vh = _mla_prep(cq, ckv, wuq, wukv, proj2, gq, gk, tabs)
    o, lse = _attn_fwd(qh, kh, vh)
    wmo = fetch('mla_out', (o,))['mla_w_out']
    more, epi = _residual_epi(next_gain)
    h_out, hn_next = _mm("mla_out", (t // tm, 1, 1), o, row,
                         wmo, pl.BlockSpec((D_MODEL, D_MODEL), lambda i, j, k: (0, 0)), (1, 0),
                         _residual_outs(t, row, next_gain), extras=[(h, row)] + more, epi=epi)
    return h_out, hn_next, (h, hn, proj2, cq, ckv, qh, kh, vh, o, lse), (wmi, wuq, wukv, wmo)


def _mla_layer_bwd(dh, saved, norm_g, wmi, qa, kva, wuq, wukv, gq, gk, wmo, tabs, deps=()):
    h, hn, proj2, cq, ckv, qh, kh, vh, o, lse = saved
    t = h.shape[0]
    tm = _row_tile(t)
    tk = _row_tile(t, 512)
    row = pl.BlockSpec((tm, D_MODEL), lambda i, j, k: (i, 0))
    rowk = pl.BlockSpec((tk, D_MODEL), lambda i, j, k: (k, 0))
    sq = pl.BlockSpec((D_MODEL, D_MODEL), lambda i, j, k: (0, 0))
    (dob,) = _mm("mla_do", (t // tm, 1, 1), dh, row, wmo, sq, (1, 1), [(_sds((t, D_MODEL), BF16), row)], deps=deps)
    (dwmo,) = _mm("mla_dwo", (1, 1, t // tk), o, rowk, dh, rowk, (0, 0), [(_sds((D_MODEL, D_MODEL), BF16), sq)])
    dqt, dkh, dvh = _attn_bwd(qh, kh, vh, dob, o, lse)
    dq, dkv, dkr, dgq, dgk = _mla_prep_bwd(cq, ckv, wuq, wukv, proj2, gq, gk, tabs, dqt, dkh, dvh)

    wide = MLA_HEADS * MLA_HD_PAD
    widek = pl.BlockSpec((tk, wide), lambda i, j, k: (k, 0))
    (dwuq,) = _mm("mla_dwuq", (1, 1, t // tk),
                  cq, pl.BlockSpec((tk, MLA_Q_RANK), lambda i, j, k: (k, 0)), dq, widek, (0, 0),
                  [(_sds((MLA_HEADS, MLA_Q_RANK, MLA_HD_PAD), BF16),
                    pl.BlockSpec((MLA_HEADS, MLA_Q_RANK, MLA_HD_PAD), lambda i, j, k: (0, 0, 0)))], split=MLA_HD_PAD)
    (dwukv,) = _mm("mla_dwukv", (1, 1, t // tk),
                   ckv, pl.BlockSpec((tk, MLA_KV_RANK), lambda i, j, k: (k, 0)), dkv, widek, (0, 0),
                   [(_sds((MLA_HEADS, MLA_KV_RANK, MLA_HD_PAD), BF16),
                     pl.BlockSpec((MLA_HEADS, MLA_KV_RANK, MLA_HD_PAD), lambda i, j, k: (0, 0, 0)))],
                   split=MLA_HD_PAD)
    side_by_side = lambda wg: wg.transpose(1, 0, 2).reshape(wg.shape[1], wide)
    widei = pl.BlockSpec((tm, wide), lambda i, j, k: (i, 0))
    (dcq,) = _mm("mla_dcq", (t // tm, 1, 1), dq, widei,
                 side_by_side(wuq), pl.BlockSpec((MLA_Q_RANK, wide), lambda i, j, k: (0, 0)), (1, 1),
                 [(_sds((t, MLA_Q_RANK), F32), pl.BlockSpec((tm, MLA_Q_RANK), lambda i, j, k: (i, 0)))])
    (dckv,) = _mm("mla_dckv", (t // tm, 1, 1), dkv, widei,
                  side_by_side(wukv), pl.BlockSpec((MLA_KV_RANK, wide), lambda i, j, k: (0, 0)), (1, 1),
                  [(_sds((t, MLA_KV_RANK), F32), pl.BlockSpec((tm, MLA_KV_RANK), lambda i, j, k: (i, 0)))])
    dproj2, dqa, dkva = _mla_mid_bwd(proj2, qa, kva, dcq, dckv, dkr)
    win = pl.BlockSpec((D_MODEL, MLA_IN_PAD), lambda i, j, k: (0, 0))
    (dwmi,) = _mm("mla_dwin", (1, 1, t // tk), hn, rowk,
                  dproj2, pl.BlockSpec((tk, MLA_IN_PAD), lambda i, j, k: (k, 0)), (0, 0),
                  [(_sds((D_MODEL, MLA_IN_PAD), BF16), win)])
    dh_in, dg = _mm("mla_dhn", (t // tm, 1, 1),
                    dproj2, pl.BlockSpec((tm, MLA_IN_PAD), lambda i, j, k: (i, 0)), wmi, win, (1, 1),
                    _norm_bwd_outs(t, tm), extras=[(h, row), (dh, row), (norm_g, VEC)], epi=_norm_bwd_epi)
    return dh_in, dict(mix=jnp.sum(dg, axis=0), wmi=dwmi, qa=dqa, kva=dkva, wuq=dwuq, wukv=dwukv, gq=dgq, gk=dgk,
                       wmo=dwmo)


def _local_step(x, p, target, w, fetch, emit=lambda group: ()):
    t = x.shape[0]
    cos_r, sin_r, tabs = w['tables'] if 'tables' in w else _rope_tables(t, 0.0)
    row = lambda a, i: a[i:i + 1]

    h1, hn1, s_ret = _ret_layer_fwd(x, row(w['mix_norm'], 0), w['ret_w_in'],
                                    lambda after: fetch('ret_out', after)['ret_w_out'], w['ret_gn'], cos_r, sin_r,
                                    row(w['mlp_norm'], 0), hn=w.get('hn0'), deps=w['deps'])
    h2, hn2, s_mlp0 = _mlp_fwd(0, h1, hn1, fetch('mlp_w1_0', (h1,))['mlp_w1'],
                               lambda after: fetch('mlp_w2_0', after)['mlp_w2'], row(w['ple_norm'], 0))
    w0 = fetch('ple_0', (h2,))
    h3, hn3, s_ple0 = _ple_fwd(0, h2, hn2, p, w0['ple_gate_w'], w0['ple_proj_w'], row(w['mix_norm'], 1))
    h4, hn4, s_mla, (wmi, wuq, wukv, wmo) = _mla_layer_fwd(
        h3, hn3, fetch, w['mla_q_a_norm'], w['mla_kv_a_norm'], w['mla_q_norm'], w['mla_k_norm'], tabs,
        row(w['mlp_norm'], 1))
    mla_w = (wmi, w['mla_q_a_norm'], w['mla_kv_a_norm'], wuq, wukv, w['mla_q_norm'], w['mla_k_norm'], wmo, tabs)
    w1 = fetch('layer_1', (h4,))
    h5, hn5, s_mlp1 = _mlp_fwd(1, h4, hn4, w1['mlp_w1'], lambda after: w1['mlp_w2'], row(w['ple_norm'], 1))
    dy, sq_err, s_ple1 = _ple_fwd(1, h5, hn5, p, w1['ple_gate_w'], w1['ple_proj_w'], None, target)

    n = N_DEV
    colsh = lambda a: a.reshape(a.shape[0], n, a.shape[1] // n).transpose(1, 0, 2)
    rowsh = lambda a: a.reshape(n, a.shape[0] // n, a.shape[1])
    big = {}

    def emit_group(group):
        big.update(group)
        return emit(group)

    dh5, dg_ple1, dwg1, dwp1 = _ple_bwd(1, dy, s_ple1, p, row(w['ple_norm'], 1), w1['ple_gate_w'])
    dh4, dg_mlp1, dw1_1, dw2_1 = _mlp_bwd(1, dh5, s_mlp1, row(w['mlp_norm'], 1))
    deps = emit_group({('ple_gate_w', 1): rowsh(dwg1), ('ple_proj_w', 1): colsh(dwp1),
                       ('mlp_w2', 1): dw2_1, ('mlp_w1', 1): dw1_1})
    dh3, gm = _mla_layer_bwd(dh4, s_mla, row(w['mix_norm'], 1), *mla_w, deps=deps)
    deps = emit_group({('mla_w_out', 0): rowsh(gm['wmo']), ('mla_w_uq', 0): _gather_rope(gm['wuq']),
                       ('mla_w_ukv', 0): gm['wukv'], ('mla_w_in', 0): rowsh(_gather_rope(gm['wmi']))})
    dh2, dg_ple0, _, _ = _ple_bwd(
        0, dh3, s_ple0, p, row(w['ple_norm'], 0), w0['ple_gate_w'], deps=deps,
        emit=lambda dwg, dwp: emit_group({('ple_gate_w', 0): rowsh(dwg), ('ple_proj_w', 0): colsh(dwp)}))
    dh1, dg_mlp0, _, _ = _mlp_bwd(0, dh2, s_mlp0, row(w['mlp_norm'], 0),
                                  emit_w2=lambda dw2: emit_group({('mlp_w2', 0): dw2}),
                                  emit_w1=lambda dw1: emit_group({('mlp_w1', 0): dw1}))
    dx, dg_mix0, dgn = _ret_layer_bwd(
        dh1, s_ret, row(w['mix_norm'], 0), w['ret_w_in'], w['ret_gn'], cos_r, sin_r,
        lambda dwro: emit_group({('ret_w_out', 0): rowsh(dwro)}),
        lambda dwri: emit_group({('ret_w_in', 0): dwri}))

    small = dict(
        mix_norm=[dg_mix0, gm['mix']], mlp_norm=[dg_mlp0, dg_mlp1], ple_norm=[dg_ple0, dg_ple1],
        ret_gn=dgn, mla_q_a_norm=gm['qa'], mla_kv_a_norm=gm['kva'], mla_q_norm=gm['gq'], mla_k_norm=gm['gk'],
    )
    return sq_err, dx, big, small


def _my_place():
    x, y, c = lax.axis_index("x"), lax.axis_index("y"), lax.axis_index("c")
    return x, y, c


def _flat(px, py, pc):
    return 4 * px + 2 * py + pc


def _peer(x, y, c, r):
    return (1 - x if r & 4 else x, 1 - y if r & 2 else y, 1 - c if r & 1 else c)


HBM = pl.BlockSpec(memory_space=pltpu.HBM)
SEMS = pl.BlockSpec(memory_space=pltpu.SEMAPHORE)
SIDE_EFFECT = pltpu.SideEffectType.DATAFLOW_SIDE_EFFECTING


def _rs_copies(x, y, c, srcs, lands, send_sems, recv_sems):
    copies = []
    for a in range(len(srcs)):
        for r in range(1, N_DEV):
            peer = _peer(x, y, c, r)
            k = a * (N_DEV - 1) + r - 1
            copies.append(pltpu.make_async_remote_copy(
                src_ref=srcs[a].at[_flat(*peer)], dst_ref=lands[a].at[r - 1],
                send_sem=send_sems.at[k], recv_sem=recv_sems.at[k], device_id=peer, device_id_type=MESH))
    return copies


def _rs_start(name, arrays):
    n = len(arrays)
    hbm = lambda a: pltpu.with_memory_space_constraint(a, pltpu.HBM)
    lands = [hbm(lax.empty((N_DEV - 1,) + a.shape[1:], a.dtype)) for a in arrays]

    def body(*refs):
        srcs, lnd = refs[:n], refs[n:2 * n]
        send_sems, recv_sems = refs[2 * n], refs[2 * n + 1]
        token = refs[-1]
        for cp in _rs_copies(*_my_place(), srcs, lnd, send_sems, recv_sems):
            cp.start()
        token[...] = jnp.zeros_like(token)

    outs = pl.pallas_call(
        body, name=name,
        in_specs=[HBM] * (2 * n),
        out_specs=[SEMS, SEMS] + [HBM] * (2 * n) + [pl.BlockSpec(memory_space=pltpu.VMEM)],
        out_shape=[pltpu.SemaphoreType.DMA((n * (N_DEV - 1),)), pltpu.SemaphoreType.DMA((n * (N_DEV - 1),))]
        + [pltpu.HBM(a.shape, a.dtype) for a in arrays] + [pltpu.HBM(l.shape, l.dtype) for l in lands]
        + [_sds((8, 128), F32)],
        input_output_aliases={i: 2 + i for i in range(2 * n)},
        compiler_params=pltpu.CompilerParams(has_side_effects=SIDE_EFFECT),
    )(*[hbm(a) for a in arrays], *lands)
    return outs[0], outs[1], outs[2:2 + n], outs[2 + n:2 + 2 * n], outs[-1]


def _rs_wait(name, send_sems, recv_sems, srcs, lands, after):
    n = len(srcs)

    def body(*refs):
        src_refs, lnd = refs[:n], refs[n:2 * n]
        send, recv = refs[2 * n], refs[2 * n + 1]
        for cp in _rs_copies(*_my_place(), src_refs, lnd, send, recv):
            cp.wait_send()
            cp.wait_recv()

    outs = pl.pallas_call(
        body, name=name,
        in_specs=[HBM] * (2 * n) + [SEMS, SEMS] + [ANY] * len(after),
        out_specs=[HBM] * (2 * n),
        out_shape=[pltpu.HBM(a.shape, a.dtype) for a in list(srcs) + list(lands)],
        input_output_aliases={i: i for i in range(2 * n)},
        compiler_params=pltpu.CompilerParams(has_side_effects=SIDE_EFFECT),
    )(*srcs, *lands, send_sems, recv_sems, *after)
    return outs[:n], outs[n:]


SMALL_PACK_ROWS = 16


def _all_reduce_small(rows, deps=()):
    n = len(rows)

    def body(*refs):
        ins = refs[:n]
        out_ref, mine, buf, send_sems, recv_sems = refs[n + len(deps):]
        x, y, c = _my_place()
        mine[...] = jnp.zeros_like(mine)
        for (r0, a), ref in zip(rows, ins):
            mine[r0:r0 + a.shape[0], 0:a.shape[1]] = ref[...]
        buf[_flat(x, y, c)] = mine[...]
        copies = []
        for r in range(1, N_DEV):
            peer = _peer(x, y, c, r)
            send = pltpu.make_async_remote_copy(
                src_ref=mine, dst_ref=buf.at[_flat(x, y, c)],
                send_sem=send_sems.at[r - 1], recv_sem=recv_sems.at[r - 1], device_id=peer, device_id_type=MESH)
            send.start()
            recv = pltpu.make_async_remote_copy(
                src_ref=mine, dst_ref=buf.at[_flat(*peer)],
                send_sem=send_sems.at[r - 1], recv_sem=recv_sems.at[r - 1], device_id=peer, device_id_type=MESH)
            copies.append((send, recv))
        for send, recv in copies:
            send.wait_send()
            recv.wait_recv()
        acc = buf[0]
        for s in range(1, N_DEV):
            acc = acc + buf[s]
        out_ref[...] = acc

    vm = pl.BlockSpec(memory_space=pltpu.VMEM)
    shape = (SMALL_PACK_ROWS, D_MODEL)
    return pl.pallas_call(
        body, name="all_reduce_small", in_specs=[vm] * n + [ANY] * len(deps), out_specs=vm,
        out_shape=_sds(shape, F32),
        scratch_shapes=[pltpu.VMEM(shape, F32), pltpu.VMEM((N_DEV,) + shape, F32),
                        pltpu.SemaphoreType.DMA((7,)), pltpu.SemaphoreType.DMA((7,))],
    )(*[a for _, a in rows], *deps)


def _adamw_math(w, g, m, v):
    m = ADAM_B1 * m + (1.0 - ADAM_B1) * g
    v = ADAM_B2 * v + (1.0 - ADAM_B2) * (g * g)
    m_hat = m / (1.0 - ADAM_B1 ** ADAM_STEP)
    v_hat = v / (1.0 - ADAM_B2 ** ADAM_STEP)
    delta = -ADAM_LR * (m_hat / (jnp.sqrt(v_hat) + ADAM_EPS) + ADAM_WD * w)
    return delta, m, v


def _adamw_big(name, w, m, v, srcs, lands, me):
    nl, rows, cols = w.shape
    tr = next(cand for cand in (256, 128, 64, 32, 16, 8) if rows % cand == 0)

    def body(me_ref, w_ref, m_ref, v_ref, *rest):
        src_refs, land_refs = rest[:nl], rest[nl:2 * nl]
        g_ref, d_ref, mo_ref, vo_ref = rest[2 * nl:]
        for layer in range(nl):
            @pl.when(pl.program_id(0) == layer)
            def _():
                g = src_refs[layer][...].astype(F32)
                for s in range(N_DEV - 1):
                    g = g + land_refs[layer][s].astype(F32)
                delta, mn, vn = _adamw_math(w_ref[...], g, m_ref[...], v_ref[...])
                g_ref[...] = g
                d_ref[...] = delta
                mo_ref[...] = mn
                vo_ref[...] = vn

    blk = pl.BlockSpec((None, tr, cols), lambda l, i, me_ref: (l, i, 0))
    at = lambda layer, l, i: jnp.where(l == layer, i, 0)
    own = [pl.BlockSpec((None, tr, cols), functools.partial(lambda layer, l, i, me_ref: (me_ref[0], at(layer, l, i), 0),
                                                            layer)) for layer in range(nl)]
    peers = [pl.BlockSpec((N_DEV - 1, tr, cols), functools.partial(lambda layer, l, i, me_ref: (0, at(layer, l, i), 0),
                                                                   layer)) for layer in range(nl)]
    return pl.pallas_call(
        body, name=name,
        grid_spec=pltpu.PrefetchScalarGridSpec(
            num_scalar_prefetch=1, grid=(nl, rows // tr),
            in_specs=[blk, blk, blk] + own + peers, out_specs=[blk] * 4),
        out_shape=[_sds((nl, rows, cols), F32)] * 4,
        compiler_params=_cparams(("arbitrary", "arbitrary")),
    )(me, w, m, v, *srcs, *lands)


def _adamw_small(ws, gs, ms, vs):
    n = len(ws)

    def body(*refs):
        w_refs, g_refs, m_refs, v_refs = (refs[i * n:(i + 1) * n] for i in range(4))
        d_out, m_out, v_out = (refs[(4 + i) * n:(5 + i) * n] for i in range(3))
        for i in range(n):
            delta, mn, vn = _adamw_math(w_refs[i][...], g_refs[i][...], m_refs[i][...], v_refs[i][...])
            d_out[i][...] = delta
            m_out[i][...] = mn
            v_out[i][...] = vn

    vm = pl.BlockSpec(memory_space=pltpu.VMEM)
    outs = pl.pallas_call(
        body, name="adamw_small", in_specs=[vm] * (4 * n), out_specs=[vm] * (3 * n),
        out_shape=[_sds(a.shape, F32) for a in ws] * 3,
    )(*ws, *gs, *ms, *vs)
    return outs[:n], outs[n:2 * n], outs[2 * n:]


def _pad_to(a, rows, cols):
    return jnp.pad(a, ((0, rows - a.shape[0]), (0, cols - a.shape[1])))


def _place_own(blocks):
    me = _flat(*_my_place())
    return [lax.dynamic_update_slice(lax.empty((N_DEV,) + b.shape, b.dtype), b[None], (me,) + (0,) * b.ndim)
            for b in blocks]


def _ag_copies(x, y, c, blocks, bufs, send_sems, recv_sems, arriving):
    copies = []
    for a in range(len(blocks)):
        for r in range(1, N_DEV):
            peer = _peer(x, y, c, r)
            k = a * (N_DEV - 1) + r - 1
            copies.append(pltpu.make_async_remote_copy(
                src_ref=blocks[a], dst_ref=bufs[a].at[_flat(*(peer if arriving else (x, y, c)))],
                send_sem=send_sems.at[k], recv_sem=recv_sems.at[k], device_id=peer, device_id_type=MESH))
    return copies


def _ag_start(groups, after):
    flat = [pair for g in groups for pair in g]
    n, ng = len(flat), len(groups)
    hbm = lambda a: pltpu.with_memory_space_constraint(a, pltpu.HBM)

    def body(*refs):
        blocks, bufs = refs[:n], refs[n:2 * n]
        sems = refs[2 * n + len(after):2 * n + len(after) + 2 * ng]
        x, y, c = _my_place()
        at = 0
        for gi, g in enumerate(groups):
            for cp in _ag_copies(x, y, c, blocks[at:at + len(g)], bufs[at:at + len(g)], sems[2 * gi],
                                 sems[2 * gi + 1], arriving=False):
                cp.start()
            at += len(g)
        refs[-1][...] = jnp.zeros_like(refs[-1])

    sem_shapes = [pltpu.SemaphoreType.DMA((len(g) * (N_DEV - 1),)) for g in groups for _ in range(2)]
    outs = pl.pallas_call(
        body, name="gather_start",
        in_specs=[HBM] * (2 * n) + [ANY] * len(after),
        out_specs=[SEMS] * (2 * ng) + [HBM] * (2 * n) + [pl.BlockSpec(memory_space=pltpu.VMEM)],
        out_shape=sem_shapes + [pltpu.HBM(b.shape, b.dtype) for b, _ in flat]
        + [pltpu.HBM(u.shape, u.dtype) for _, u in flat] + [_sds((8, 128), F32)],
        input_output_aliases={i: 2 * ng + i for i in range(2 * n)},
        compiler_params=pltpu.CompilerParams(has_side_effects=SIDE_EFFECT),
    )(*[hbm(b) for b, _ in flat], *[hbm(u) for _, u in flat], *after)
    blocks_thru, bufs_thru = outs[2 * ng:2 * ng + n], outs[2 * ng + n:2 * ng + 2 * n]
    started, at = [], 0
    for gi, g in enumerate(groups):
        started.append((outs[2 * gi], outs[2 * gi + 1], blocks_thru[at:at + len(g)], bufs_thru[at:at + len(g)]))
        at += len(g)
    return started, outs[-1]


def _ag_wait(name, send_sems, recv_sems, blocks, bufs, after):
    n = len(blocks)

    def body(*refs):
        for cp in _ag_copies(*_my_place(), refs[:n], refs[n:2 * n], refs[2 * n], refs[2 * n + 1], arriving=True):
            cp.wait_send()
            cp.wait_recv()

    outs = pl.pallas_call(
        body, name=name,
        in_specs=[HBM] * (2 * n) + [SEMS, SEMS] + [ANY] * len(after),
        out_specs=[HBM] * (2 * n),
        out_shape=[pltpu.HBM(a.shape, a.dtype) for a in list(blocks) + list(bufs)],
        input_output_aliases={i: i for i in range(2 * n)},
        compiler_params=pltpu.CompilerParams(has_side_effects=SIDE_EFFECT),
    )(*blocks, *bufs, send_sems, recv_sems, *after)
    return outs[n:]


def _split_call(name, body, thru, sems_in, new_sems, after):
    n, ns, nn = len(thru), len(sems_in), len(new_sems)
    hbm = lambda a: pltpu.with_memory_space_constraint(a, pltpu.HBM)

    def wrapped(*refs):
        body(refs[:n], refs[n:n + ns], refs[n + ns + len(after):n + ns + len(after) + nn])
        refs[-1][...] = jnp.zeros_like(refs[-1])

    outs = pl.pallas_call(
        wrapped, name=name,
        in_specs=[HBM] * n + [SEMS] * ns + [ANY] * len(after),
        out_specs=[SEMS] * nn + [HBM] * n + [pl.BlockSpec(memory_space=pltpu.VMEM)],
        out_shape=[pltpu.SemaphoreType.DMA((k,)) for k in new_sems] + [pltpu.HBM(a.shape, a.dtype) for a in thru]
        + [_sds((8, 128), F32)],
        input_output_aliases={i: nn + i for i in range(n)},
        compiler_params=pltpu.CompilerParams(has_side_effects=SIDE_EFFECT),
    )(*[hbm(a) for a in thru], *sems_in, *after)
    return list(outs[:nn]), list(outs[nn:nn + n]), outs[-1]


def _two_level_gather(name, blocks, bufs, after=()):
    n = len(blocks)

    def copies(refs, s1, r1, s2, r2):
        x, y, c = _my_place()
        me, sibling = (x, y, c), (x, y, 1 - c)
        chips = [(1 - x, y), (x, 1 - y), (1 - x, 1 - y)]
        blk, buf = refs[:n], refs[n:]
        out = dict(send1=[], recv1_sib=[], recv1_ici=[], send2=[], recv2=[])
        for a in range(n):
            place = lambda dev: buf[a].at[_flat(*dev)]
            for k, to in enumerate([sibling] + [(*chip, c) for chip in chips]):
                mk = lambda dst: pltpu.make_async_remote_copy(
                    src_ref=blk[a], dst_ref=dst, send_sem=s1.at[4 * a + k], recv_sem=r1.at[4 * a + k],
                    device_id=to, device_id_type=MESH)
                out['send1'].append(mk(place(me)))
                out['recv1_sib' if k == 0 else 'recv1_ici'].append(mk(place(to)))
            for j, chip in enumerate(chips):
                mk = lambda dev: pltpu.make_async_remote_copy(
                    src_ref=place(dev), dst_ref=place(dev), send_sem=s2.at[3 * a + j], recv_sem=r2.at[3 * a + j],
                    device_id=sibling, device_id_type=MESH)
                out['send2'].append(mk((*chip, c)))
                out['recv2'].append(mk((*chip, 1 - c)))
        return out

    def start(refs, sems_in, new):
        for cp in copies(refs, new[0], new[1], new[0], new[1])['send1']:
            cp.start()

    def forward(refs, sems_in, new):
        cps = copies(refs, sems_in[0], sems_in[1], new[0], new[1])
        for cp in cps['recv1_ici']:
            cp.wait_recv()
        for cp in cps['send2']:
            cp.start()

    def finish(refs, sems_in, new):
        cps = copies(refs, *sems_in)
        for cp in cps['recv1_sib'] + cps['recv2']:
            cp.wait_recv()
        for cp in cps['send1'] + cps['send2']:
            cp.wait_send()

    sems1, thru, token = _split_call(name + "_start", start, list(blocks) + list(bufs), [], [4 * n, 4 * n], after)

    def complete(after):
        sems2, thru2, token2 = _split_call(name + "_forward", forward, thru, sems1, [3 * n, 3 * n], after)
        _, thru3, _ = _split_call(name + "_wait", finish, thru2, sems1 + sems2, [], ())
        return thru3[n:], token2

    return token, complete


def _prepare_weights(p, x):
    n = N_DEV
    bf = lambda a: a.astype(BF16)
    gn_pack = jnp.concatenate([
        _pad_to(p['ret_gn'][0], RET_HEADS, 128), _pad_to(p['mla_q_a_norm'], 1, 128),
        _pad_to(p['mla_kv_a_norm'], 1, 128), jnp.zeros((2, 128), F32)], axis=0)
    ple = lambda l: [bf(p['ple_gate_w'][l]), bf(p['ple_proj_w'][l])]
    names = ('mlp_w1_0', 'mlp_w2_0', 'ple_0', 'mla_in', 'mla_up', 'mla_out', 'layer_1')
    later = [[bf(p['mlp_w1'][0])], [bf(p['mlp_w2'][0])], ple(0),
             [bf(p['mla_w_in'][0])], [bf(p['mla_w_uq'][0]), bf(p['mla_w_ukv'][0])], [bf(p['mla_w_out'][0])],
             [bf(p['mlp_w1'][1]), bf(p['mlp_w2'][1])] + ple(1)]
    first = [gn_pack, bf(p['ret_w_in'][0])]
    second = [bf(p['ret_w_out'][0])]
    token, complete_first = _two_level_gather("first_gather", first, _place_own(first))
    token2, complete_second = _two_level_gather("second_gather", second, _place_own(second), (token,))
    hn0 = _rms_fwd("mix_norm0", x, p['mix_norm'][0:1], deps=(token2,))
    bufs = _place_own([b for g in later for b in g])
    tables = _rope_tables(x.shape[0], token2[0, 0])
    (pack, wri), token = complete_first((hn0, tables[0], tables[1], *tables[2], *bufs))
    groups, at = [], 0
    for g in later:
        groups.append(list(zip(g, bufs[at:at + len(g)])))
        at += len(g)
    started, token = _ag_start(groups, (token,))

    w = {k: p[k] for k in ('mix_norm', 'mlp_norm', 'ple_norm')}
    w['hn0'] = hn0
    w['tables'] = tables
    w['ret_gn'] = pack[:, :RET_HEADS, :RET_DV // n].transpose(1, 0, 2).reshape(RET_HEADS, RET_DV)
    w['mla_q_a_norm'] = pack[:, RET_HEADS, :MLA_Q_RANK // n].reshape(1, MLA_Q_RANK)
    w['mla_kv_a_norm'] = pack[:, RET_HEADS + 1, :MLA_KV_RANK // n].reshape(1, MLA_KV_RANK)
    w['ret_w_in'] = wri
    w['mla_q_norm'] = _spread_rope(p['mla_q_norm'])
    w['mla_k_norm'] = _spread_rope(p['mla_k_norm'])
    w['deps'] = (token,)

    def fetch(name, after):
        if name == 'ret_out':
            return dict(ret_w_out=complete_second(after)[0][0].reshape(RET_V_W, D_MODEL))
        got = list(_ag_wait("gather_wait_" + name, *started[names.index(name)], after))
        if name == 'mla_in':
            return dict(mla_w_in=_spread_rope(got[0].reshape(D_MODEL, MLA_IN)))
        if name == 'mla_up':
            return dict(mla_w_uq=_spread_rope(got[0]), mla_w_ukv=got[1])
        if name == 'mla_out':
            return dict(mla_w_out=got[0].reshape(D_MODEL, D_MODEL))
        out = {}
        if name in ('mlp_w1_0', 'layer_1'):
            out['mlp_w1'] = got.pop(0)
        if name in ('mlp_w2_0', 'layer_1'):
            out['mlp_w2'] = got.pop(0)
        if name in ('ple_0', 'layer_1'):
            out['ple_gate_w'] = got[0].reshape(D_MODEL, D_MODEL)
            out['ple_proj_w'] = got[1].transpose(1, 0, 2).reshape(PLE_DIM, D_MODEL)
        return out

    return w, fetch


def _small_grads(small, after):
    rows = [(0, small['mix_norm'][0]), (1, small['mix_norm'][1]), (2, small['mlp_norm'][0]),
            (3, small['mlp_norm'][1]), (4, small['ple_norm'][0]), (5, small['ple_norm'][1]),
            (6, small['ret_gn']), (10, small['mla_q_a_norm']), (11, small['mla_kv_a_norm']),
            (12, small['mla_q_norm']), (13, small['mla_k_norm']), (14, small['sq_err'])]
    gs = _all_reduce_small(rows, after)
    me = _flat(*_my_place())
    n = N_DEV
    return dict(
        sq_err=gs[14, 0],
        mix_norm=gs[0:2], mlp_norm=gs[2:4], ple_norm=gs[4:6],
        ret_gn=lax.dynamic_slice(gs, (6, me * (RET_DV // n)), (RET_HEADS, RET_DV // n)),
        mla_q_a_norm=lax.dynamic_slice(gs, (10, me * (MLA_Q_RANK // n)), (1, MLA_Q_RANK // n)),
        mla_kv_a_norm=lax.dynamic_slice(gs, (11, me * (MLA_KV_RANK // n)), (1, MLA_KV_RANK // n)),
        mla_q_norm=_gather_rope(gs[12:13, :MLA_HD_PAD]), mla_k_norm=_gather_rope(gs[13:14, :MLA_HD_PAD]))


def kernel(x, p, mix_norm, ret_w_in, ret_gn, ret_w_out, mla_w_in, mla_q_a_norm, mla_kv_a_norm, mla_w_uq, mla_w_ukv, mla_q_norm, mla_k_norm, mla_w_out, mlp_norm, mlp_w1, mlp_w2, ple_norm, ple_gate_w, ple_proj_w, loss_target, m_mix_norm, m_ret_w_in, m_ret_gn, m_ret_w_out, m_mla_w_in, m_mla_q_a_norm, m_mla_kv_a_norm, m_mla_w_uq, m_mla_w_ukv, m_mla_q_norm, m_mla_k_norm, m_mla_w_out, m_mlp_norm, m_mlp_w1, m_mlp_w2, m_ple_norm, m_ple_gate_w, m_ple_proj_w, v_mix_norm, v_ret_w_in, v_ret_gn, v_ret_w_out, v_mla_w_in, v_mla_q_a_norm, v_mla_kv_a_norm, v_mla_w_uq, v_mla_w_ukv, v_mla_q_norm, v_mla_k_norm, v_mla_w_out, v_mlp_norm, v_mlp_w1, v_mlp_w2, v_ple_norm, v_ple_gate_w, v_ple_proj_w):
    given = dict(locals())
    params = {n: given[n] for n in WEIGHTS}
    w, fetch = _prepare_weights(params, x[0])

    started = []

    def emit(group):
        keys = list(group)
        send, recv, srcs, lands, token = _rs_start(f"rs_start{len(started)}", [group[k] for k in keys])
        started.append((keys, send, recv, srcs, lands))
        return (token,)

    sq_err, grad_x, _, small = _local_step(x[0], p, loss_target[0], w, fetch, emit)
    small['sq_err'] = sq_err[0:1]

    grads, deltas, new_m, new_v = {}, {}, {}, {}
    total = {}

    def small_updates(after):
        sg = _small_grads(small, after)
        total['loss'] = 0.5 / D_MODEL * sg['sq_err']
        two_d = lambda a: a.reshape(-1, a.shape[-1])
        d_s, m_s, v_s = _adamw_small(
            [two_d(params[n]) for n in SMALL], [sg[n] for n in SMALL],
            [two_d(given["m_" + n]) for n in SMALL], [two_d(given["v_" + n]) for n in SMALL])
        for i, n in enumerate(SMALL):
            shape = params[n].shape
            grads[n], deltas[n], new_m[n], new_v[n] = (a.reshape(shape) for a in (sg[n], d_s[i], m_s[i], v_s[i]))
        return (d_s[0],)

    me = _flat(*_my_place()).astype(jnp.int32).reshape(1)
    after = (grad_x,)
    src_of, land_of = {}, {}
    for gi, (keys, send, recv, srcs, lands) in enumerate(started):
        if gi == len(started) - 1:
            after = small_updates(after)
        srcs, lands = _rs_wait(f"rs_wait{gi}", send, recv, srcs, lands, after)
        for k, s, l in zip(keys, srcs, lands):
            src_of[k], land_of[k] = s, l
        done = [n for n in BIG if n not in grads and all((n, l) in src_of for l in range(params[n].shape[0]))]
        for n in done:
            layers = range(params[n].shape[0])
            grads[n], deltas[n], new_m[n], new_v[n] = _adamw_big(
                "adamw_" + n, params[n], given["m_" + n], given["v_" + n],
                [src_of[(n, l)] for l in layers], [land_of[(n, l)] for l in layers], me)
        if done:
            after = tuple(deltas[n] for n in done)

    return (total['loss'], grad_x[None], *[grads[n] for n in WEIGHTS], *[deltas[n] for n in WEIGHTS],
            *[new_m[n] for n in WEIGHTS], *[new_v[n] for n in WEIGHTS])
```

```python
import functools

import jax
import jax.numpy as jnp
from jax import lax
from jax.experimental import pallas as pl
from jax.experimental.pallas import tpu as pltpu

F32 = jnp.float32
BF16 = jnp.bfloat16
MESH = pl.DeviceIdType.MESH
ANY = pl.BlockSpec(memory_space=pl.ANY)

N_DEV = 8
D_MODEL = 1024
CHUNK = 64
RET_BLOCK = 4 * CHUNK
EPS = 1e-6
ROPE_THETA = 10000.0
RET_HEADS = 4
RET_DK = 256
RET_DV = 512
RET_QK_W = RET_HEADS * RET_DK
RET_V_W = RET_HEADS * RET_DV
RET_IN = 2 * RET_QK_W + 2 * RET_V_W
MLA_HEADS = 8
MLA_NOPE = 128
MLA_ROPE = 64
MLA_QKD = MLA_NOPE + MLA_ROPE
MLA_VD = 128
MLA_Q_RANK = 384
MLA_KV_RANK = 256
MLA_IN = MLA_Q_RANK + MLA_KV_RANK + MLA_ROPE
MLA_IN_PAD = 768
MLA_HD_PAD = 256
D_FF = 4096
PLE_DIM = 256
ATT_SCALE = MLA_QKD ** -0.5
LOG2E = 1.4426950408889634
ATT_EXP2 = ATT_SCALE * LOG2E

ADAM_LR = 0.001
ADAM_B1 = 0.9
ADAM_B2 = 0.999
ADAM_EPS = 1e-08
ADAM_WD = 0.01
ADAM_STEP = 10

VMEM_LIMIT = 52 * 1024 * 1024
ROW_TILE = 1024
RET_ROWS = 512
ATT_BLOCK = 256
ATT_QROWS = 1024
ATT_KROWS = 1024
ATT_HEADS = 2
PREP_ROWS = 1024

WEIGHTS = ['mix_norm', 'ret_w_in', 'ret_gn', 'ret_w_out', 'mla_w_in', 'mla_q_a_norm', 'mla_kv_a_norm',
           'mla_w_uq', 'mla_w_ukv', 'mla_q_norm', 'mla_k_norm', 'mla_w_out', 'mlp_norm', 'mlp_w1', 'mlp_w2',
           'ple_norm', 'ple_gate_w', 'ple_proj_w']
BIG = ['ret_w_in', 'ret_w_out', 'mla_w_in', 'mla_w_uq', 'mla_w_ukv', 'mla_w_out', 'mlp_w1', 'mlp_w2',
       'ple_gate_w', 'ple_proj_w']
SMALL = [w for w in WEIGHTS if w not in BIG]


def _cparams(sem=None):
    return pltpu.CompilerParams(dimension_semantics=sem, vmem_limit_bytes=VMEM_LIMIT)


def _dot(a, b, ca, cb):
    return lax.dot_general(a, b, (((ca,), (cb,)), ((), ())), preferred_element_type=F32)


def _bf(v):
    return v if v.dtype == BF16 else v.astype(BF16)


def _sigmoid(z):
    return 1.0 / (1.0 + jnp.exp(-z))


def _mm(name, grid, a, a_spec, b, b_spec, contract, outs, extras=(), epi=None, deps=(), split=None):
    nk = grid[2]
    n_ex, n_out, n_dep = len(extras), len(outs), len(deps)
    acc_shape = tuple(d for d in outs[0][1].block_shape if d is not None)
    if split is not None:
        acc_shape = (acc_shape[1], acc_shape[0] * split)

    def body(*refs):
        a_ref, b_ref = refs[:2]
        ex_refs = refs[2:2 + n_ex]
        out_refs = refs[2 + n_ex + n_dep:2 + n_ex + n_dep + n_out]

        def product():
            return _dot(_bf(a_ref[...]), _bf(b_ref[...]), contract[0], contract[1])

        def finish(acc):
            if split is not None:
                for j in range(acc_shape[1] // split):
                    out_refs[0][j] = acc[:, j * split:(j + 1) * split].astype(out_refs[0].dtype)
                return
            acc = acc[...]
            res = epi(acc, *[r[...] for r in ex_refs]) if epi is not None else (acc,)
            for o, r in zip(out_refs, res):
                o[...] = r.astype(o.dtype)

        if nk == 1:
            finish(product())
        else:
            acc_ref = refs[-1]
            k = pl.program_id(2)

            @pl.when(k == 0)
            def _():
                acc_ref[...] = jnp.zeros_like(acc_ref)

            acc_ref[...] += product()

            @pl.when(k == nk - 1)
            def _():
                finish(acc_ref)

    return pl.pallas_call(
        body, name=name, grid=grid,
        in_specs=[a_spec, b_spec] + [s for _, s in extras] + [ANY] * n_dep,
        out_specs=[s for _, s in outs],
        out_shape=[s for s, _ in outs],
        scratch_shapes=[pltpu.VMEM(acc_shape, F32)] if nk > 1 else [],
        compiler_params=_cparams(("parallel", "parallel", "arbitrary")),
    )(a, b, *[x for x, _ in extras], *deps)


def _mm_rows(name, tm, a, w, mode, outs, extras=(), epi=None, deps=()):
    n_sh, rows, cols = w.shape
    n_ex, n_out, n_dep = len(extras), len(outs), len(deps)
    by_cols = mode in ('nn_cols', 'nt_rows')
    width = cols if mode == 'nn_cols' else rows

    def body(*refs):
        a_ref, w_ref = refs[:2]
        ex_refs = refs[2:2 + n_ex]
        out_refs = refs[2 + n_ex + n_dep:2 + n_ex + n_dep + n_out]
        if by_cols:
            av = _bf(a_ref[...])
            for s in range(n_sh):
                cs = slice(s * width, (s + 1) * width)
                acc = _dot(av, w_ref[s], 1, 0 if mode == 'nn_cols' else 1)
                res = epi(acc, *[r[:, cs] for r in ex_refs]) if epi is not None else (acc,)
                for o, r in zip(out_refs, res):
                    o[:, cs] = r.astype(o.dtype)
        else:
            chunk = rows if mode == 'nn_rows' else cols
            acc = None
            for s in range(n_sh):
                part = _dot(_bf(a_ref[:, s * chunk:(s + 1) * chunk]), w_ref[s], 1, 0 if mode == 'nn_rows' else 1)
                acc = part if acc is None else acc + part
            res = epi(acc, *[r[...] for r in ex_refs]) if epi is not None else (acc,)
            for o, r in zip(out_refs, res):
                o[...] = r.astype(o.dtype)

    t, ka = a.shape
    return pl.pallas_call(
        body, name=name, grid=(t // tm, 1, 1),
        in_specs=[pl.BlockSpec((tm, ka), lambda i, j, k: (i, 0)),
                  pl.BlockSpec((n_sh, rows, cols), lambda i, j, k: (0, 0, 0))] + [s for _, s in extras] + [ANY] * n_dep,
        out_specs=[s for _, s in outs],
        out_shape=[s for s, _ in outs],
        compiler_params=_cparams(("parallel", "arbitrary", "arbitrary")),
    )(a, w, *[x for x, _ in extras], *deps)


def _sds(shape, dtype):
    return jax.ShapeDtypeStruct(shape, dtype)


def _row_tile(t, cap=ROW_TILE):
    return min(cap, t)


def _rms_fwd(name, x, g, deps=()):
    t, d = x.shape
    tm = _row_tile(t)

    def body(x_ref, g_ref, *rest):
        o_ref = rest[-1]
        xv = x_ref[...]
        r = lax.rsqrt(jnp.mean(xv * xv, axis=-1, keepdims=True) + EPS)
        o_ref[...] = (xv * r * g_ref[...]).astype(o_ref.dtype)

    return pl.pallas_call(
        body, name=name, grid=(t // tm,),
        in_specs=[pl.BlockSpec((tm, d), lambda i: (i, 0)), pl.BlockSpec((1, d), lambda i: (0, 0))] + [ANY] * len(deps),
        out_specs=pl.BlockSpec((tm, d), lambda i: (i, 0)),
        out_shape=_sds((t, d), BF16),
        compiler_params=_cparams(("parallel",)),
    )(x, g, *deps)


def _rms_bwd_rows(dy, xv, g, n):
    r = lax.rsqrt(jnp.sum(xv * xv, axis=-1, keepdims=True) / n + EPS)
    xh = xv * r
    dxh = dy * g
    dx = r * (dxh - xh * (jnp.sum(dxh * xh, axis=-1, keepdims=True) / n))
    return dx, dy * xh


def _ple_gate_bwd(name, dh, gate, e):
    t, d = dh.shape
    tm = _row_tile(t)

    def body(dh_ref, g_ref, e_ref, de_ref, dz_ref):
        dh_v, gt = dh_ref[...], g_ref[...].astype(F32)
        de_ref[...] = (dh_v * gt).astype(BF16)
        dz_ref[...] = (dh_v * e_ref[...].astype(F32) * (gt * (1.0 - gt))).astype(BF16)

    row = pl.BlockSpec((tm, d), lambda i: (i, 0))
    return pl.pallas_call(
        body, name=name, grid=(t // tm,), in_specs=[row, row, row], out_specs=[row, row],
        out_shape=[_sds((t, d), BF16), _sds((t, d), BF16)],
        compiler_params=_cparams(("parallel",)),
    )(dh, gate, e)


def _rope_half(v, cos, sin):
    half = v.shape[-1] // 2
    v1, v2 = v[:, :half], v[:, half:]
    return jnp.concatenate([v1 * cos - v2 * sin, v2 * cos + v1 * sin], axis=-1)


def _ret_consts():
    lg = jnp.log(1.0 - 2.0 ** (-5.0 - jnp.arange(RET_HEADS, dtype=F32)))
    idx = jnp.arange(RET_BLOCK, dtype=F32)
    chunk = jnp.floor(idx / CHUNK)
    dist = idx[:, None] - idx[None, :]
    same = chunk[:, None] == chunk[None, :]
    seen = jnp.where(same, jnp.abs(dist), jnp.where(chunk[None, :] < chunk[:, None], dist, jnp.inf))
    intra = jnp.exp(lg[:, None, None] * seen)
    qdec = jnp.exp(lg[:, None] * (idx + 1.0))
    kdec = jnp.exp(lg[:, None] * (RET_BLOCK - 1.0 - idx))
    cdec = jnp.exp(lg * RET_BLOCK)
    qdec = jnp.broadcast_to(qdec[:, :, None], (RET_HEADS, RET_BLOCK, RET_DK))
    kdec = jnp.broadcast_to(kdec[:, :, None], (RET_HEADS, RET_BLOCK, RET_DK))
    cdec = jnp.broadcast_to(cdec[:, None, None], (RET_HEADS, 1, RET_DV))
    return intra, qdec, kdec, cdec


def _ret_specs(rb, rev_nb=None):
    blk = (lambda i: i) if rev_nb is None else (lambda i: rev_nb - 1 - i)
    full = lambda shape: pl.BlockSpec(shape, lambda i: (0,) * len(shape))
    return dict(
        proj=pl.BlockSpec((rb, RET_IN), lambda i: (blk(i), 0)),
        tab=pl.BlockSpec((rb, RET_DK // 2), lambda i: (blk(i), 0)),
        vw=pl.BlockSpec((rb, RET_V_W), lambda i: (blk(i), 0)),
        st=pl.BlockSpec((rb // RET_BLOCK, RET_HEADS, RET_DK, RET_DV), lambda i: (blk(i), 0, 0, 0)),
        gn=full((RET_HEADS, 1, RET_DV)),
        intra=full((RET_HEADS, RET_BLOCK, RET_BLOCK)),
        dec=full((RET_HEADS, RET_BLOCK, RET_DK)),
        cdec=full((RET_HEADS, 1, RET_DV)),
    )


def _ret_fwd(proj, cos, sin, gn):
    t = proj.shape[0]
    rb = min(RET_ROWS, t)
    cpb = rb // RET_BLOCK
    intra, qdec, kdec, cdec = _ret_consts()
    sp = _ret_specs(rb)

    def body(proj_ref, cos_ref, sin_ref, gn_ref, intra_ref, qd_ref, kd_ref, cd_ref,
             gated_ref, outp_ref, st_ref, s_ref):
        @pl.when(pl.program_id(0) == 0)
        def _():
            s_ref[...] = jnp.zeros_like(s_ref)

        def chunk(c, carry):
            rows = pl.ds(pl.multiple_of(c * RET_BLOCK, RET_BLOCK), RET_BLOCK)
            cs, sn = cos_ref[rows, :], sin_ref[rows, :]
            for h in range(RET_HEADS):
                q = proj_ref[rows, h * RET_DK:(h + 1) * RET_DK].astype(F32)
                k = proj_ref[rows, RET_QK_W + h * RET_DK:RET_QK_W + (h + 1) * RET_DK].astype(F32)
                v = proj_ref[rows, 2 * RET_QK_W + h * RET_DV:2 * RET_QK_W + (h + 1) * RET_DV]
                g = proj_ref[rows, 2 * RET_QK_W + RET_V_W + h * RET_DV:
                             2 * RET_QK_W + RET_V_W + (h + 1) * RET_DV].astype(F32)
                qr = _rope_half(q, cs, sn)
                kr = _rope_half(k, cs, sn) * (RET_DK ** -0.5)
                qb, kb, vb = qr.astype(BF16), kr.astype(BF16), v
                sc = _dot(qb, kb, 1, 1) * intra_ref[h]
                inner = _dot(sc.astype(BF16), vb, 1, 0)
                s_old = s_ref[h]
                sb = s_old.astype(BF16)
                st_ref[c, h] = sb
                cross = _dot((qr * qd_ref[h]).astype(BF16), sb, 1, 0)
                out = inner + cross
                s_ref[h] = s_old * cd_ref[h] + _dot((kr * kd_ref[h]).astype(BF16), vb, 0, 0)
                r = lax.rsqrt(jnp.mean(out * out, axis=-1, keepdims=True) + EPS)
                y = out * r * gn_ref[h]
                cols = slice(h * RET_DV, (h + 1) * RET_DV)
                gated_ref[rows, cols] = (g * _sigmoid(g) * y).astype(BF16)
                outp_ref[rows, cols] = out
            return carry

        lax.fori_loop(0, cpb, chunk, 0)

    return pl.pallas_call(
        body, name="ret_fwd", grid=(t // rb,),
        in_specs=[sp['proj'], sp['tab'], sp['tab'], sp['gn'], sp['intra'], sp['dec'], sp['dec'], sp['cdec']],
        out_specs=[sp['vw'], sp['vw'], sp['st']],
        out_shape=[_sds((t, RET_V_W), BF16), _sds((t, RET_V_W), F32),
                   _sds((t // RET_BLOCK, RET_HEADS, RET_DK, RET_DV), BF16)],
        scratch_shapes=[pltpu.VMEM((RET_HEADS, RET_DK, RET_DV), F32)],
        compiler_params=_cparams(("arbitrary",)),
    )(proj, cos, sin, gn.reshape(RET_HEADS, 1, RET_DV), intra, qdec, kdec, cdec)


def _ret_gate_bwd_epi(dgt, out, g, gn):
    g = g.astype(F32)
    r = lax.rsqrt(jnp.mean(out * out, axis=-1, keepdims=True) + EPS)
    xh = out * r
    sg = _sigmoid(g)
    dgate = dgt * (xh * gn) * (sg * (1.0 + g * (1.0 - sg)))
    dy = dgt * (g * sg)
    dxh = dy * gn
    dout = r * (dxh - xh * jnp.mean(dxh * xh, axis=-1, keepdims=True))
    return dout, dgate, jnp.sum(dy * xh, axis=0, keepdims=True)


def _ret_bwd(proj, cos, sin, states, dout, dgate, deps=()):
    t = proj.shape[0]
    rb = min(RET_ROWS, t)
    cpb = rb // RET_BLOCK
    nb = t // rb
    intra, qdec, kdec, cdec = _ret_consts()
    sp = _ret_specs(rb, rev_nb=nb)

    def body(proj_ref, cos_ref, sin_ref, intra_ref, qd_ref, kd_ref, cd_ref, st_ref, dout_ref, dgate_ref, *rest):
        dproj_ref, ds_ref = rest[len(deps):]

        @pl.when(pl.program_id(0) == 0)
        def _():
            ds_ref[...] = jnp.zeros_like(ds_ref)

        def chunk(cc, carry):
            c = cpb - 1 - cc
            rows = pl.ds(pl.multiple_of(c * RET_BLOCK, RET_BLOCK), RET_BLOCK)
            cs, sn = cos_ref[rows, :], sin_ref[rows, :]
            for h in range(RET_HEADS):
                q = proj_ref[rows, h * RET_DK:(h + 1) * RET_DK].astype(F32)
                k = proj_ref[rows, RET_QK_W + h * RET_DK:RET_QK_W + (h + 1) * RET_DK].astype(F32)
                v = proj_ref[rows, 2 * RET_QK_W + h * RET_DV:2 * RET_QK_W + (h + 1) * RET_DV]
                cols = slice(h * RET_DV, (h + 1) * RET_DV)
                qr = _rope_half(q, cs, sn)
                kr = _rope_half(k, cs, sn) * (RET_DK ** -0.5)
                qb, kb, vb = qr.astype(BF16), kr.astype(BF16), v
                qdb = (qr * qd_ref[h]).astype(BF16)
                kdb = (kr * kd_ref[h]).astype(BF16)
                doutb = dout_ref[rows, cols]
                itr = intra_ref[h]
                pb = (_dot(qb, kb, 1, 1) * itr).astype(BF16)
                dv = _dot(pb, doutb, 0, 0)
                dsc = (_dot(doutb, vb, 1, 1) * itr).astype(BF16)
                dq = _dot(dsc, kb, 1, 0)
                dk = _dot(dsc, qb, 0, 0)
                dq = dq + _dot(doutb, st_ref[c, h], 1, 1) * qd_ref[h]
                ds_new = ds_ref[h]
                dsb = ds_new.astype(BF16)
                dk = dk + _dot(vb, dsb, 1, 1) * kd_ref[h]
                dv = dv + _dot(kdb, dsb, 1, 0)
                ds_ref[h] = ds_new * cd_ref[h] + _dot(qdb, doutb, 0, 0)
                dproj_ref[rows, h * RET_DK:(h + 1) * RET_DK] = _rope_half(dq, cs, -sn).astype(BF16)
                dproj_ref[rows, RET_QK_W + h * RET_DK:RET_QK_W + (h + 1) * RET_DK] = (
                    _rope_half(dk * (RET_DK ** -0.5), cs, -sn).astype(BF16))
                dproj_ref[rows, 2 * RET_QK_W + h * RET_DV:2 * RET_QK_W + (h + 1) * RET_DV] = dv.astype(BF16)
                dproj_ref[rows, 2 * RET_QK_W + RET_V_W + h * RET_DV:
                          2 * RET_QK_W + RET_V_W + (h + 1) * RET_DV] = dgate_ref[rows, cols]
            return carry

        lax.fori_loop(0, cpb, chunk, 0)

    return pl.pallas_call(
        body, name="ret_bwd", grid=(nb,),
        in_specs=[sp['proj'], sp['tab'], sp['tab'], sp['intra'], sp['dec'], sp['dec'], sp['cdec'],
                  sp['st'], sp['vw'], sp['vw']] + [ANY] * len(deps),
        out_specs=sp['proj'],
        out_shape=_sds((t, RET_IN), BF16),
        scratch_shapes=[pltpu.VMEM((RET_HEADS, RET_DK, RET_DV), F32)],
        compiler_params=_cparams(("arbitrary",)),
    )(proj, cos, sin, intra, qdec, kdec, cdec, states, dout, dgate, *deps)


def _spread_rope(a):
    return jnp.pad(a, [(0, 0)] * (a.ndim - 1) + [(0, MLA_ROPE)])


def _gather_rope(a):
    return a[..., :a.shape[-1] - MLA_ROPE]


def _rope_tables(t, zero):
    pos = jnp.arange(t, dtype=F32)[:, None] + zero
    inv = 1.0 / (ROPE_THETA ** (jnp.arange(0, RET_DK, 2, dtype=F32) / RET_DK))
    ang = pos * inv[None, :]
    return jnp.cos(ang), jnp.sin(ang), _mla_tables(t, pos)


def _mla_tables(t, pos):
    half = MLA_ROPE // 2
    inv = 1.0 / (ROPE_THETA ** (jnp.arange(0, MLA_ROPE, 2, dtype=F32) / MLA_ROPE))
    ang = pos * inv[None, :]
    cos, sin = jnp.cos(ang), jnp.sin(ang)
    z = jnp.zeros((t, half), F32)
    c = jnp.concatenate([cos, cos, z, z], axis=1)
    s1 = jnp.concatenate([-sin, z, z, z], axis=1)
    s2 = jnp.concatenate([z, sin, z, z], axis=1)
    return c, s1, s2


def _rope_tile(r, c, s1, s2):
    return r * c + pltpu.roll(r, 96, 1) * s1 + pltpu.roll(r, 32, 1) * s2


def _mla_mid(proj2, qa, kva):
    t = proj2.shape[0]
    tm = _row_tile(t)

    def body(p_ref, qa_ref, kva_ref, cq_ref, ckv_ref):
        cq = p_ref[:, :MLA_Q_RANK]
        ckv = p_ref[:, MLA_Q_RANK:MLA_Q_RANK + MLA_KV_RANK]
        rq = lax.rsqrt(jnp.mean(cq * cq, axis=-1, keepdims=True) + EPS)
        rkv = lax.rsqrt(jnp.mean(ckv * ckv, axis=-1, keepdims=True) + EPS)
        cq_ref[...] = (cq * rq * qa_ref[...]).astype(BF16)
        ckv_ref[...] = (ckv * rkv * kva_ref[...]).astype(BF16)

    return pl.pallas_call(
        body, name="mla_mid", grid=(t // tm,),
        in_specs=[pl.BlockSpec((tm, MLA_IN_PAD), lambda i: (i, 0)),
                  pl.BlockSpec((1, MLA_Q_RANK), lambda i: (0, 0)),
                  pl.BlockSpec((1, MLA_KV_RANK), lambda i: (0, 0))],
        out_specs=[pl.BlockSpec((tm, MLA_Q_RANK), lambda i: (i, 0)),
                   pl.BlockSpec((tm, MLA_KV_RANK), lambda i: (i, 0))],
        out_shape=[_sds((t, MLA_Q_RANK), BF16), _sds((t, MLA_KV_RANK), BF16)],
        compiler_params=_cparams(("parallel",)),
    )(proj2, qa, kva)


def _mla_mid_bwd(proj2, qa, kva, dcq, dckv, dkr):
    t = proj2.shape[0]
    tm = _row_tile(t)

    def body(p_ref, qa_ref, kva_ref, dcq_ref, dckv_ref, dkr_ref, dp_ref, dqa_ref, dkva_ref):
        @pl.when(pl.program_id(0) == 0)
        def _():
            dqa_ref[...] = jnp.zeros_like(dqa_ref)
            dkva_ref[...] = jnp.zeros_like(dkva_ref)

        dxq, dgq = _rms_bwd_rows(dcq_ref[...], p_ref[:, :MLA_Q_RANK], qa_ref[...], MLA_Q_RANK)
        dxk, dgk = _rms_bwd_rows(dckv_ref[...], p_ref[:, MLA_Q_RANK:MLA_Q_RANK + MLA_KV_RANK], kva_ref[...],
                                 MLA_KV_RANK)
        dp_ref[:, :MLA_Q_RANK] = dxq.astype(BF16)
        dp_ref[:, MLA_Q_RANK:MLA_Q_RANK + MLA_KV_RANK] = dxk.astype(BF16)
        dp_ref[:, MLA_Q_RANK + MLA_KV_RANK:] = dkr_ref[...].astype(BF16)
        dqa_ref[...] += jnp.sum(dgq, axis=0, keepdims=True)
        dkva_ref[...] += jnp.sum(dgk, axis=0, keepdims=True)

    return pl.pallas_call(
        body, name="mla_mid_bwd", grid=(t // tm,),
        in_specs=[pl.BlockSpec((tm, MLA_IN_PAD), lambda i: (i, 0)),
                  pl.BlockSpec((1, MLA_Q_RANK), lambda i: (0, 0)),
                  pl.BlockSpec((1, MLA_KV_RANK), lambda i: (0, 0)),
                  pl.BlockSpec((tm, MLA_Q_RANK), lambda i: (i, 0)),
                  pl.BlockSpec((tm, MLA_KV_RANK), lambda i: (i, 0)),
                  pl.BlockSpec((tm, 128), lambda i: (i, 0))],
        out_specs=[pl.BlockSpec((tm, MLA_IN_PAD), lambda i: (i, 0)),
                   pl.BlockSpec((1, MLA_Q_RANK), lambda i: (0, 0)),
                   pl.BlockSpec((1, MLA_KV_RANK), lambda i: (0, 0))],
        out_shape=[_sds((t, MLA_IN_PAD), BF16), _sds((1, MLA_Q_RANK), F32), _sds((1, MLA_KV_RANK), F32)],
        compiler_params=_cparams(("arbitrary",)),
    )(proj2, qa, kva, dcq, dckv, dkr)


def _mla_prep_specs(t, tm):
    head = lambda w: pl.BlockSpec((None, tm, w), lambda i, h: (h, i, 0))
    return dict(
        head256=head(MLA_HD_PAD), head128=head(MLA_VD),
        cols256=pl.BlockSpec((tm, MLA_HD_PAD), lambda i, h: (i, h)),
        cq=pl.BlockSpec((tm, MLA_Q_RANK), lambda i, h: (i, 0)),
        ckv=pl.BlockSpec((tm, MLA_KV_RANK), lambda i, h: (i, 0)),
        wuq=pl.BlockSpec((None, MLA_Q_RANK, MLA_HD_PAD), lambda i, h: (h, 0, 0)),
        wukv=pl.BlockSpec((None, MLA_KV_RANK, MLA_HD_PAD), lambda i, h: (h, 0, 0)),
        kr=pl.BlockSpec((tm, 128), lambda i, h: (i, (MLA_Q_RANK + MLA_KV_RANK) // 128)),
        gain=pl.BlockSpec((1, MLA_HD_PAD), lambda i, h: (0, 0)),
        tab=pl.BlockSpec((tm, 128), lambda i, h: (i, 0)),
    )


def _mla_prep(cq, ckv, wuq, wukv, proj2, gq, gk, tabs):
    t = cq.shape[0]
    tm = _row_tile(t, PREP_ROWS)
    sp = _mla_prep_specs(t, tm)

    def body(cq_ref, ckv_ref, wuq_ref, wukv_ref, kr_ref, gq_ref, gk_ref, c_ref, s1_ref, s2_ref,
             qh_ref, kh_ref, vh_ref):
        c, s1, s2 = c_ref[...], s1_ref[...], s2_ref[...]

        def norm_rope(xv, gain):
            r = lax.rsqrt(jnp.sum(xv * xv, axis=-1, keepdims=True) / MLA_QKD + EPS)
            y = xv * r * gain
            return jnp.concatenate([y[:, :MLA_NOPE], _rope_tile(y[:, MLA_NOPE:], c, s1, s2)], axis=-1)

        kvv = _dot(ckv_ref[...], wukv_ref[...], 1, 0)
        qh_ref[...] = norm_rope(_dot(cq_ref[...], wuq_ref[...], 1, 0), gq_ref[...]).astype(BF16)
        kf = jnp.concatenate([kvv[:, :MLA_NOPE], kr_ref[...]], axis=-1)
        kh_ref[...] = norm_rope(kf, gk_ref[...]).astype(BF16)
        vh_ref[...] = jnp.concatenate([kvv[:, MLA_NOPE:], jnp.ones((tm, MLA_VD), F32)], axis=-1).astype(BF16)

    return pl.pallas_call(
        body, name="mla_prep", grid=(t // tm, MLA_HEADS),
        in_specs=[sp['cq'], sp['ckv'], sp['wuq'], sp['wukv'], sp['kr'], sp['gain'], sp['gain'],
                  sp['tab'], sp['tab'], sp['tab']],
        out_specs=[sp['head256'], sp['head256'], sp['head256']],
        out_shape=[_sds((MLA_HEADS, t, MLA_HD_PAD), BF16), _sds((MLA_HEADS, t, MLA_HD_PAD), BF16),
                   _sds((MLA_HEADS, t, 2 * MLA_VD), BF16)],
        compiler_params=_cparams(("parallel", "arbitrary")),
    )(cq, ckv, wuq, wukv, proj2, gq, gk, *tabs)


def _mla_prep_bwd(cq, ckv, wuq, wukv, proj2, gq, gk, tabs, dqt, dkh, dvh):
    t = cq.shape[0]
    tm = _row_tile(t, PREP_ROWS)
    ab = dqt.shape[-1]
    sp = _mla_prep_specs(t, tm)

    def body(cq_ref, ckv_ref, wuq_ref, wukv_ref, kr_ref, gq_ref, gk_ref, c_ref, s1_ref, s2_ref,
             dqt_ref, dkh_ref, dvh_ref, dq_ref, dkv_ref, dkr_ref, dgq_ref, dgk_ref):
        dqh = jnp.concatenate([dqt_ref[b].T for b in range(tm // ab)], axis=0)
        i, h = pl.program_id(0), pl.program_id(1)

        @pl.when((i == 0) & (h == 0))
        def _():
            dgq_ref[...] = jnp.zeros_like(dgq_ref)
            dgk_ref[...] = jnp.zeros_like(dgk_ref)

        @pl.when(h == 0)
        def _():
            dkr_ref[...] = jnp.zeros_like(dkr_ref)

        c, s1, s2 = c_ref[...], s1_ref[...], s2_ref[...]

        def back(xv, gain, dout):
            dy = jnp.concatenate([dout[:, :MLA_NOPE], _rope_tile(dout[:, MLA_NOPE:], c, -s1, -s2)], axis=-1)
            return _rms_bwd_rows(dy, xv, gain, MLA_QKD)

        kvv = _dot(ckv_ref[...], wukv_ref[...], 1, 0)
        dxq, dgq = back(_dot(cq_ref[...], wuq_ref[...], 1, 0), gq_ref[...], dqh)
        kf = jnp.concatenate([kvv[:, :MLA_NOPE], kr_ref[...]], axis=-1)
        dxk, dgk = back(kf, gk_ref[...], dkh_ref[...])
        dq_ref[...] = dxq.astype(BF16)
        dkv_ref[...] = jnp.concatenate([dxk[:, :MLA_NOPE], dvh_ref[...]], axis=-1).astype(BF16)
        dkr_ref[...] += dxk[:, MLA_NOPE:]
        dgq_ref[...] += jnp.sum(dgq, axis=0, keepdims=True)
        dgk_ref[...] += jnp.sum(dgk, axis=0, keepdims=True)

    return pl.pallas_call(
        body, name="mla_prep_bwd", grid=(t // tm, MLA_HEADS),
        in_specs=[sp['cq'], sp['ckv'], sp['wuq'], sp['wukv'], sp['kr'], sp['gain'], sp['gain'],
                  sp['tab'], sp['tab'], sp['tab'],
                  pl.BlockSpec((None, tm // ab, MLA_HD_PAD, ab), lambda i, h: (h, i, 0, 0)),
                  sp['head256'], sp['head128']],
        out_specs=[sp['cols256'], sp['cols256'], sp['tab'], sp['gain'], sp['gain']],
        out_shape=[_sds((t, MLA_HEADS * MLA_HD_PAD), BF16), _sds((t, MLA_HEADS * MLA_HD_PAD), BF16),
                   _sds((t, 128), F32), _sds((1, MLA_HD_PAD), F32), _sds((1, MLA_HD_PAD), F32)],
        compiler_params=_cparams(("arbitrary", "arbitrary")),
    )(cq, ckv, wuq, wukv, proj2, gq, gk, *tabs, dqt, dkh, dvh)


def _chunk_visible(rows, cols, row_off, col_off):
    rq = lax.shift_right_logical(lax.broadcasted_iota(jnp.int32, (rows, cols), 0) + row_off, 6)
    ck = lax.shift_right_logical(lax.broadcasted_iota(jnp.int32, (rows, cols), 1) + col_off, 6)
    return ck <= rq


def _rows_to_lanes(col):
    return col.T[:8, :]


def _attn_fwd(qh, kh, vh):
    t = qh.shape[1]
    ab = min(ATT_BLOCK, t)
    tq = min(ATT_QROWS, t)
    r = tq // ab
    hg = ATT_HEADS

    def body(q_ref, k_ref, v_ref, o_ref, lse_ref, acc_ref):
        n_un = pl.program_id(1) * r
        acc_ref[...] = jnp.zeros_like(acc_ref)

        def step(b, ms, diag):
            rows = pl.ds(pl.multiple_of(b * ab, ab), ab)
            out = []
            for hh in range(hg):
                m = ms[hh]
                s = _dot(q_ref[hh], k_ref[hh, rows, :], 1, 1)
                if diag is not None:
                    s = jnp.where(_chunk_visible(tq, ab, 0, diag * ab), s, -1e30)
                m_new = jnp.maximum(m, jnp.max(s, axis=-1, keepdims=True))
                p = jnp.exp2((s - m_new) * ATT_EXP2).astype(BF16)
                acc_ref[hh] = jnp.exp2((m - m_new) * ATT_EXP2) * acc_ref[hh] + _dot(p, v_ref[hh, rows, :], 1, 0)
                out.append(m_new)
            return tuple(out)

        ms = tuple(jnp.full((tq, 1), -1e30, F32) for _ in range(hg))
        ms = lax.fori_loop(0, n_un, lambda b, st: step(b, st, None), ms)
        for d in range(r):
            ms = step(n_un + d, ms, d)
        for hh in range(hg):
            l = acc_ref[hh, :, MLA_VD:]
            o_ref[:, hh * MLA_VD:(hh + 1) * MLA_VD] = acc_ref[hh, :, :MLA_VD] / l
            lse_t = _rows_to_lanes(ms[hh] * ATT_EXP2 + jnp.log(l) * LOG2E)
            for d in range(r):
                lse_ref[hh, d] = lse_t[:, d * ab:(d + 1) * ab]

    return pl.pallas_call(
        body, name="mla_attn", grid=(MLA_HEADS // hg, t // tq),
        in_specs=[pl.BlockSpec((hg, tq, MLA_HD_PAD), lambda g, i: (g, i, 0)),
                  pl.BlockSpec((hg, t, MLA_HD_PAD), lambda g, i: (g, 0, 0)),
                  pl.BlockSpec((hg, t, 2 * MLA_VD), lambda g, i: (g, 0, 0))],
        out_specs=[pl.BlockSpec((tq, hg * MLA_VD), lambda g, i: (i, g)),
                   pl.BlockSpec((hg, r, 8, ab), lambda g, i: (g, i, 0, 0))],
        out_shape=[_sds((t, MLA_HEADS * MLA_VD), F32), _sds((MLA_HEADS, t // ab, 8, ab), F32)],
        scratch_shapes=[pltpu.VMEM((hg, tq, 2 * MLA_VD), F32)],
        compiler_params=_cparams(("parallel", "arbitrary")),
    )(qh, kh, vh)


def _attn_bwd(qh, kh, vh, dob, o, lse_t):
    t = qh.shape[1]
    ab = min(ATT_BLOCK, t)
    kb = min(ATT_KROWS, t)
    r = kb // ab
    nq = t // ab
    hg = ATT_HEADS

    def body(q_ref, k_ref, v_ref, do_ref, o_ref, lse_ref, dqt_ref, dk_ref, dv_ref, dl_ref):
        j = pl.program_id(1)

        @pl.when(j == 0)
        def _():
            dqt_ref[...] = jnp.zeros_like(dqt_ref)
            ones = jnp.ones((8, MLA_VD), F32)

            def delta(b, carry):
                rows = pl.ds(pl.multiple_of(b * ab, ab), ab)
                for hh in range(hg):
                    cols = slice(hh * MLA_VD, (hh + 1) * MLA_VD)
                    prod = do_ref[rows, cols].astype(F32) * o_ref[rows, cols]
                    dl_ref[hh, b] = lax.dot_general(ones, prod, (((1,), (1,)), ((), ())),
                                                    precision=lax.Precision.HIGHEST, preferred_element_type=F32)
                return carry

            lax.fori_loop(0, nq, delta, 0)

        ks = [k_ref[hh] for hh in range(hg)]
        vs = [v_ref[hh, :, :MLA_VD] for hh in range(hg)]
        kts = [k.T for k in ks]

        dk_ref[...] = jnp.zeros_like(dk_ref)
        dv_ref[...] = jnp.zeros_like(dv_ref)

        def step(b, carry, diag):
            rows = pl.ds(pl.multiple_of(b * ab, ab), ab)
            hi = kb if diag is None else (diag + 1) * ab
            for hh in range(hg):
                q = q_ref[hh, rows, :]
                do = do_ref[rows, hh * MLA_VD:(hh + 1) * MLA_VD]
                s_t = _dot(ks[hh][:hi], q, 1, 1)
                if diag is not None:
                    key_chunk = lax.shift_right_logical(lax.broadcasted_iota(jnp.int32, (hi, ab), 0), 6)
                    query_chunk = lax.shift_right_logical(
                        lax.broadcasted_iota(jnp.int32, (hi, ab), 1) + diag * ab, 6)
                    s_t = jnp.where(key_chunk <= query_chunk, s_t, -1e30)
                p_t = jnp.exp2(s_t * ATT_EXP2 - lse_ref[hh, b][0:1, :])
                dp_t = _dot(vs[hh][:hi], do, 1, 1)
                ds_t = (p_t * (dp_t - dl_ref[hh, b][0:1, :]) * ATT_SCALE).astype(BF16)
                dqt_ref[hh, b] += _dot(kts[hh][:, :hi], ds_t, 1, 0)
                dk_ref[hh, :hi] += _dot(ds_t, q, 1, 0)
                dv_ref[hh, :hi] += _dot(p_t.astype(BF16), do, 1, 0)
            return carry

        for d in range(r):
            step(j * r + d, 0, d)
        lax.fori_loop((j + 1) * r, nq, lambda b, c: step(b, c, None), 0)

    whole = lambda w: pl.BlockSpec((hg, t, w), lambda g, j: (g, 0, 0))
    blk = lambda w: pl.BlockSpec((hg, kb, w), lambda g, j: (g, j, 0))
    stat = pl.BlockSpec((hg, nq, 8, ab), lambda g, j: (g, 0, 0, 0))
    cols = pl.BlockSpec((t, hg * MLA_VD), lambda g, j: (0, g))
    return pl.pallas_call(
        body, name="mla_attn_bwd", grid=(MLA_HEADS // hg, t // kb),
        in_specs=[whole(MLA_HD_PAD), blk(MLA_HD_PAD), blk(2 * MLA_VD),
                  cols, cols, stat],
        out_specs=[pl.BlockSpec((hg, nq, MLA_HD_PAD, ab), lambda g, j: (g, 0, 0, 0)), blk(MLA_HD_PAD), blk(MLA_VD)],
        out_shape=[_sds((MLA_HEADS, nq, MLA_HD_PAD, ab), F32), _sds((MLA_HEADS, t, MLA_HD_PAD), F32),
                   _sds((MLA_HEADS, t, MLA_VD), F32)],
        scratch_shapes=[pltpu.VMEM((hg, nq, 8, ab), F32)],
        compiler_params=_cparams(("parallel", "arbitrary")),
    )(qh, kh, vh, dob, o, lse_t)


VEC = pl.BlockSpec((1, D_MODEL), lambda i, j, k: (0, 0))


def _rows(tm, width):
    return pl.BlockSpec((tm, width), lambda i, j, k: (i, 0))


def _residual_epi(next_gain):
    if next_gain is None:
        return [], lambda acc, hv: (acc + hv,)

    def epi(acc, hv, g):
        h_new = acc + hv
        r = lax.rsqrt(jnp.mean(h_new * h_new, axis=-1, keepdims=True) + EPS)
        return h_new, h_new * r * g

    return [(next_gain, VEC)], epi


def _residual_outs(t, row, next_gain):
    outs = [(_sds((t, D_MODEL), F32), row)]
    return outs + ([(_sds((t, D_MODEL), BF16), row)] if next_gain is not None else [])


def _mlp_fwd(l, h, hn, w1g, fetch_w2, next_gain):
    t = h.shape[0]
    tm = _row_tile(t, 512)

    def relu2(acc):
        r = jnp.maximum(acc, 0.0)
        return (r * r,)

    (u,) = _mm_rows(f"mlp_up{l}", tm, hn, w1g, 'nn_cols', [(_sds((t, D_FF), BF16), _rows(tm, D_FF))], epi=relu2)
    w2g = fetch_w2((u,))
    row = _rows(tm, D_MODEL)
    more, epi = _residual_epi(next_gain)
    h2, hn_next = _mm_rows(f"mlp_down{l}", tm, u, w2g, 'nn_rows', _residual_outs(t, row, next_gain),
                           extras=[(h, row)] + more, epi=epi)
    return h2, hn_next, (h, hn, u, w1g, w2g)


def _norm_bwd_outs(t, tm):
    return [(_sds((t, D_MODEL), F32), pl.BlockSpec((tm, D_MODEL), lambda i, j, k: (i, 0))),
            (_sds((t // tm, 1, D_MODEL), F32), pl.BlockSpec((None, 1, D_MODEL), lambda i, j, k: (i, 0, 0)))]


def _norm_bwd_epi(acc, xv, res, g):
    dx, dgr = _rms_bwd_rows(acc, xv, g, D_MODEL)
    return res + dx, jnp.sum(dgr, axis=0, keepdims=True)


def _mlp_bwd(l, dh, saved, norm_g, emit_w2=None, emit_w1=None):
    h, hn, u, w1g, w2g = saved
    t = h.shape[0]
    tm = _row_tile(t, 512)
    nsh, _, wsh = w1g.shape
    wide = _rows(tm, D_FF)
    (da,) = _mm_rows(f"mlp_du{l}", tm, dh, w2g, 'nt_rows', [(_sds((t, D_FF), BF16), wide)], extras=[(u, wide)],
                     epi=lambda acc, uv: (2.0 * jnp.sqrt(uv.astype(F32)) * acc,))
    tw = _row_tile(t, 512)
    (dw2,) = _mm(f"mlp_dw2{l}", (1, 1, t // tw),
                 u, pl.BlockSpec((tw, D_FF), lambda i, j, k: (k, 0)),
                 dh, pl.BlockSpec((tw, D_MODEL), lambda i, j, k: (k, 0)), (0, 0),
                 [(_sds((D_FF, D_MODEL), BF16), pl.BlockSpec((D_FF, D_MODEL), lambda i, j, k: (0, 0)))])
    dw2 = dw2.reshape(nsh, wsh, D_MODEL)
    (dw1,) = _mm(f"mlp_dw1{l}", (1, 1, t // tw),
                 hn, pl.BlockSpec((tw, D_MODEL), lambda i, j, k: (k, 0)),
                 da, pl.BlockSpec((tw, D_FF), lambda i, j, k: (k, 0)), (0, 0),
                 [(_sds((nsh, D_MODEL, wsh), BF16), pl.BlockSpec((nsh, D_MODEL, wsh), lambda i, j, k: (0, 0, 0)))],
                 split=wsh, deps=emit_w2(dw2) if emit_w2 else ())
    row = _rows(tm, D_MODEL)
    dh_in, dg = _mm_rows(f"mlp_dhn{l}", tm, da, w1g, 'nt_cols', _norm_bwd_outs(t, tm),
                         extras=[(h, row), (dh, row), (norm_g, VEC)], epi=_norm_bwd_epi,
                         deps=emit_w1(dw1) if emit_w1 else ())
    return dh_in, jnp.sum(dg, axis=0), dw1, dw2


def _ple_fwd(l, h, hn, p, wg, wp, next_gain, target=None):
    t = h.shape[0]
    tm = _row_tile(t, 512)
    row = pl.BlockSpec((tm, D_MODEL), lambda i, j, k: (i, 0))
    full = lambda r: pl.BlockSpec((r, D_MODEL), lambda i, j, k: (0, 0))
    f32_row, bf_row = (_sds((t, D_MODEL), F32), row), (_sds((t, D_MODEL), BF16), row)
    common = [(h, row), (p, pl.BlockSpec((None, None, tm, PLE_DIM), lambda i, j, k: (l, 0, i, 0))),
              (wp, full(PLE_DIM))]
    if target is not None:
        def loss_epi(acc, hv, pv, wpv, tv):
            gt = _sigmoid(acc)
            ev = _dot(_bf(pv), wpv, 1, 0)
            err = hv + gt * ev - tv
            sq = jnp.sum(jnp.sum(err * err, axis=-1, keepdims=True), axis=0, keepdims=True)
            return err / D_MODEL, gt, ev, jnp.broadcast_to(sq, (8, 128))

        dy, gate, e, sq = _mm(f"ple_gate{l}", (t // tm, 1, 1), hn, row, wg, full(D_MODEL), (1, 0),
                              [f32_row, bf_row, bf_row, (_sds((t // tm, 8, 128), F32),
                                                         pl.BlockSpec((None, 8, 128), lambda i, j, k: (i, 0, 0)))],
                              extras=common + [(target, row)], epi=loss_epi)
        return dy, jnp.sum(sq, axis=0), (h, hn, gate, e)

    def gate_epi(acc, hv, pv, wpv, *gain):
        gt = _sigmoid(acc)
        ev = _dot(_bf(pv), wpv, 1, 0)
        h_new = hv + gt * ev
        if not gain:
            return h_new, gt, ev
        r = lax.rsqrt(jnp.mean(h_new * h_new, axis=-1, keepdims=True) + EPS)
        return h_new, gt, ev, h_new * r * gain[0]

    res = _mm(f"ple_gate{l}", (t // tm, 1, 1), hn, row, wg, full(D_MODEL), (1, 0),
              [f32_row, bf_row, bf_row] + ([bf_row] if next_gain is not None else []),
              extras=common + ([(next_gain, VEC)] if next_gain is not None else []), epi=gate_epi)
    h_out, gate, e = res[0], res[1], res[2]
    return h_out, (res[3] if next_gain is not None else None), (h, hn, gate, e)


def _ple_bwd(l, dh, saved, p, norm_g, wg, deps=(), emit=None):
    h, hn, gate, e = saved
    t = h.shape[0]
    tm = _row_tile(t)
    tk = _row_tile(t, 512)
    de, dz = _ple_gate_bwd(f"ple_gate_bwd{l}", dh, gate, e)
    full = lambda r: pl.BlockSpec((r, D_MODEL), lambda i, j, k: (0, 0))
    rowk = pl.BlockSpec((tk, D_MODEL), lambda i, j, k: (k, 0))
    (dwp,) = _mm(f"ple_dwp{l}", (1, 1, t // tk),
                 p, pl.BlockSpec((None, None, tk, PLE_DIM), lambda i, j, k: (l, 0, k, 0)),
                 de, rowk, (0, 0), [(_sds((PLE_DIM, D_MODEL), BF16), full(PLE_DIM))], deps=deps)
    (dwg,) = _mm(f"ple_dwg{l}", (1, 1, t // tk), hn, rowk, dz, rowk, (0, 0),
                 [(_sds((D_MODEL, D_MODEL), BF16), full(D_MODEL))])
    row = pl.BlockSpec((tm, D_MODEL), lambda i, j, k: (i, 0))
    dh_in, dg = _mm(f"ple_dhn{l}", (t // tm, 1, 1), dz, row, wg, full(D_MODEL), (1, 1),
                    _norm_bwd_outs(t, tm), extras=[(h, row), (dh, row), (norm_g, VEC)], epi=_norm_bwd_epi,
                    deps=emit(dwg, dwp) if emit else ())
    return dh_in, jnp.sum(dg, axis=0), dwg, dwp


def _ret_layer_fwd(x, norm_g, wri, fetch_wro, gn, cos, sin, next_gain, hn=None, deps=()):
    t = x.shape[0]
    tm = _row_tile(t)
    nsh, _, wsh = wri.shape
    if hn is None:
        hn = _rms_fwd("mix_norm0", x, norm_g)
    tp = _row_tile(t, 512)
    (proj,) = _mm_rows("ret_in", tp, hn, wri, 'nn_cols', [(_sds((t, RET_IN), BF16), _rows(tp, RET_IN))], deps=deps)
    gated, outp, states = _ret_fwd(proj, cos, sin, gn)
    wro = fetch_wro((gated,))
    row = _rows(tp, D_MODEL)
    more, epi = _residual_epi(next_gain)
    h1, hn_next = _mm_rows("ret_out", tp, gated, wro.reshape(RET_HEADS, RET_DV, D_MODEL), 'nn_rows',
                           _residual_outs(t, row, next_gain), extras=[(x, row)] + more, epi=epi)
    return h1, hn_next, (x, hn, proj, gated, outp, states, wro)


def _ret_layer_bwd(dh, saved, norm_g, wri, gn, cos, sin, emit_out, emit_in, deps=()):
    x, hn, proj, gated, outp, states, wro = saved
    t = x.shape[0]
    tm = _row_tile(t)
    tk = _row_tile(t, 512)
    nsh, _, wsh = wri.shape
    tg = _row_tile(t, 512)
    vw = _rows(tg, RET_V_W)
    dout, dgate, dgn = _mm_rows(
        "ret_dgate", tg, dh, wro.reshape(RET_HEADS, RET_DV, D_MODEL), 'nt_rows',
        [(_sds((t, RET_V_W), BF16), vw), (_sds((t, RET_V_W), BF16), vw),
         (_sds((t // tg, 1, RET_V_W), F32), pl.BlockSpec((None, 1, RET_V_W), lambda i, j, k: (i, 0, 0)))],
        extras=[(outp, vw), (proj, pl.BlockSpec((tg, RET_V_W), lambda i, j, k: (i, (RET_IN - RET_V_W) // RET_V_W))),
                (gn.reshape(1, RET_V_W), pl.BlockSpec((1, RET_V_W), lambda i, j, k: (0, 0)))],
        epi=_ret_gate_bwd_epi, deps=deps)
    dgn = jnp.sum(dgn, axis=0)
    (dwro,) = _mm("ret_dwro", (1, 1, t // tk),
                  gated, pl.BlockSpec((tk, RET_V_W), lambda i, j, k: (k, 0)),
                  dh, pl.BlockSpec((tk, D_MODEL), lambda i, j, k: (k, 0)), (0, 0),
                  [(_sds((RET_V_W, D_MODEL), BF16), pl.BlockSpec((RET_V_W, D_MODEL), lambda i, j, k: (0, 0)))])
    dproj = _ret_bwd(proj, cos, sin, states, dout, dgate, deps=emit_out(dwro))
    half = nsh // 2
    (dwri,) = _mm("ret_dwri", (2, 1, t // tk),
                  hn, pl.BlockSpec((tk, D_MODEL), lambda i, j, k: (k, 0)),
                  dproj, pl.BlockSpec((tk, half * wsh), lambda i, j, k: (k, i)), (0, 0),
                  [(_sds((nsh, D_MODEL, wsh), BF16), pl.BlockSpec((half, D_MODEL, wsh), lambda i, j, k: (i, 0, 0)))],
                  split=wsh)
    deps = emit_in(dwri)
    td = _row_tile(t, 256)
    row = _rows(td, D_MODEL)
    dx, dg = _mm_rows("ret_dhn", td, dproj, wri, 'nt_cols', _norm_bwd_outs(t, td),
                      extras=[(x, row), (dh, row), (norm_g, VEC)], epi=_norm_bwd_epi, deps=deps)
    return dx, jnp.sum(dg, axis=0), dgn.reshape(RET_HEADS, RET_DV)


def _mla_layer_fwd(h, hn, fetch, qa, kva, gq, gk, tabs, next_gain):
    t = h.shape[0]
    tm = _row_tile(t)
    row = pl.BlockSpec((tm, D_MODEL), lambda i, j, k: (i, 0))
    wmi = fetch('mla_in', (h,))['mla_w_in']
    (proj2,) = _mm("mla_in", (t // tm, 1, 1), hn, row,
                   wmi, pl.BlockSpec((D_MODEL, MLA_IN_PAD), lambda i, j, k: (0, 0)), (1, 0),
                   [(_sds((t, MLA_IN_PAD), F32), pl.BlockSpec((tm, MLA_IN_PAD), lambda i, j, k: (i, 0)))])
    cq, ckv = _mla_mid(proj2, qa, kva)
    up = fetch('mla_up', (cq,))
    wuq, wukv = up['mla_w_uq'], up['mla_w_ukv']
    qh, kh, vh = _mla_prep(cq, ckv, wuq, wukv, proj2, gq, gk, tabs)
    o, lse = _attn_fwd(qh, kh, vh)
    wmo = fetch('mla_out', (o,))['mla_w_out']
    more, epi = _residual_epi(next_gain)
    h_out, hn_next = _mm("mla_out", (t // tm, 1, 1), o, row,
                         wmo, pl.BlockSpec((D_MODEL, D_MODEL), lambda i, j, k: (0, 0)), (1, 0),
                         _residual_outs(t, row, next_gain), extras=[(h, row)] + more, epi=epi)
    return h_out, hn_next, (h, hn, proj2, cq, ckv, qh, kh, vh, o, lse), (wmi, wuq, wukv, wmo)


def _mla_layer_bwd(dh, saved, norm_g, wmi, qa, kva, wuq, wukv, gq, gk, wmo, tabs, deps=()):
    h, hn, proj2, cq, ckv, qh, kh, vh, o, lse = saved
    t = h.shape[0]
    tm = _row_tile(t)
    tk = _row_tile(t, 512)
    row = pl.BlockSpec((tm, D_MODEL), lambda i, j, k: (i, 0))
    rowk = pl.BlockSpec((tk, D_MODEL), lambda i, j, k: (k, 0))
    sq = pl.BlockSpec((D_MODEL, D_MODEL), lambda i, j, k: (0, 0))
    (dob,) = _mm("mla_do", (t // tm, 1, 1), dh, row, wmo, sq, (1, 1), [(_sds((t, D_MODEL), BF16), row)], deps=deps)
    (dwmo,) = _mm("mla_dwo", (1, 1, t // tk), o, rowk, dh, rowk, (0, 0), [(_sds((D_MODEL, D_MODEL), BF16), sq)])
    dqt, dkh, dvh = _attn_bwd(qh, kh, vh, dob, o, lse)
    dq, dkv, dkr, dgq, dgk = _mla_prep_bwd(cq, ckv, wuq, wukv, proj2, gq, gk, tabs, dqt, dkh, dvh)

    wide = MLA_HEADS * MLA_HD_PAD
    widek = pl.BlockSpec((tk, wide), lambda i, j, k: (k, 0))
    (dwuq,) = _mm("mla_dwuq", (1, 1, t // tk),
                  cq, pl.BlockSpec((tk, MLA_Q_RANK), lambda i, j, k: (k, 0)), dq, widek, (0, 0),
                  [(_sds((MLA_HEADS, MLA_Q_RANK, MLA_HD_PAD), BF16),
                    pl.BlockSpec((MLA_HEADS, MLA_Q_RANK, MLA_HD_PAD), lambda i, j, k: (0, 0, 0)))], split=MLA_HD_PAD)
    (dwukv,) = _mm("mla_dwukv", (1, 1, t // tk),
                   ckv, pl.BlockSpec((tk, MLA_KV_RANK), lambda i, j, k: (k, 0)), dkv, widek, (0, 0),
                   [(_sds((MLA_HEADS, MLA_KV_RANK, MLA_HD_PAD), BF16),
                     pl.BlockSpec((MLA_HEADS, MLA_KV_RANK, MLA_HD_PAD), lambda i, j, k: (0, 0, 0)))],
                   split=MLA_HD_PAD)
    side_by_side = lambda wg: wg.transpose(1, 0, 2).reshape(wg.shape[1], wide)
    widei = pl.BlockSpec((tm, wide), lambda i, j, k: (i, 0))
    (dcq,) = _mm("mla_dcq", (t // tm, 1, 1), dq, widei,
                 side_by_side(wuq), pl.BlockSpec((MLA_Q_RANK, wide), lambda i, j, k: (0, 0)), (1, 1),
                 [(_sds((t, MLA_Q_RANK), F32), pl.BlockSpec((tm, MLA_Q_RANK), lambda i, j, k: (i, 0)))])
    (dckv,) = _mm("mla_dckv", (t // tm, 1, 1), dkv, widei,
                  side_by_side(wukv), pl.BlockSpec((MLA_KV_RANK, wide), lambda i, j, k: (0, 0)), (1, 1),
                  [(_sds((t, MLA_KV_RANK), F32), pl.BlockSpec((tm, MLA_KV_RANK), lambda i, j, k: (i, 0)))])
    dproj2, dqa, dkva = _mla_mid_bwd(proj2, qa, kva, dcq, dckv, dkr)
    win = pl.BlockSpec((D_MODEL, MLA_IN_PAD), lambda i, j, k: (0, 0))
    (dwmi,) = _mm("mla_dwin", (1, 1, t // tk), hn, rowk,
                  dproj2, pl.BlockSpec((tk, MLA_IN_PAD), lambda i, j, k: (k, 0)), (0, 0),
                  [(_sds((D_MODEL, MLA_IN_PAD), BF16), win)])
    dh_in, dg = _mm("mla_dhn", (t // tm, 1, 1),
                    dproj2, pl.BlockSpec((tm, MLA_IN_PAD), lambda i, j, k: (i, 0)), wmi, win, (1, 1),
                    _norm_bwd_outs(t, tm), extras=[(h, row), (dh, row), (norm_g, VEC)], epi=_norm_bwd_epi)
    return dh_in, dict(mix=jnp.sum(dg, axis=0), wmi=dwmi, qa=dqa, kva=dkva, wuq=dwuq, wukv=dwukv, gq=dgq, gk=dgk,
                       wmo=dwmo)


def _local_step(x, p, target, w, fetch, emit=lambda group: ()):
    t = x.shape[0]
    cos_r, sin_r, tabs = w['tables'] if 'tables' in w else _rope_tables(t, 0.0)
    row = lambda a, i: a[i:i + 1]

    h1, hn1, s_ret = _ret_layer_fwd(x, row(w['mix_norm'], 0), w['ret_w_in'],
                                    lambda after: fetch('ret_out', after)['ret_w_out'], w['ret_gn'], cos_r, sin_r,
                                    row(w['mlp_norm'], 0), hn=w.get('hn0'), deps=w['deps'])
    h2, hn2, s_mlp0 = _mlp_fwd(0, h1, hn1, fetch('mlp_w1_0', (h1,))['mlp_w1'],
                               lambda after: fetch('mlp_w2_0', after)['mlp_w2'], row(w['ple_norm'], 0))
    w0 = fetch('ple_0', (h2,))
    h3, hn3, s_ple0 = _ple_fwd(0, h2, hn2, p, w0['ple_gate_w'], w0['ple_proj_w'], row(w['mix_norm'], 1))
    h4, hn4, s_mla, (wmi, wuq, wukv, wmo) = _mla_layer_fwd(
        h3, hn3, fetch, w['mla_q_a_norm'], w['mla_kv_a_norm'], w['mla_q_norm'], w['mla_k_norm'], tabs,
        row(w['mlp_norm'], 1))
    mla_w = (wmi, w['mla_q_a_norm'], w['mla_kv_a_norm'], wuq, wukv, w['mla_q_norm'], w['mla_k_norm'], wmo, tabs)
    w1 = fetch('layer_1', (h4,))
    h5, hn5, s_mlp1 = _mlp_fwd(1, h4, hn4, w1['mlp_w1'], lambda after: w1['mlp_w2'], row(w['ple_norm'], 1))
    dy, sq_err, s_ple1 = _ple_fwd(1, h5, hn5, p, w1['ple_gate_w'], w1['ple_proj_w'], None, target)

    n = N_DEV
    colsh = lambda a: a.reshape(a.shape[0], n, a.shape[1] // n).transpose(1, 0, 2)
    rowsh = lambda a: a.reshape(n, a.shape[0] // n, a.shape[1])
    big = {}

    def emit_group(group):
        big.update(group)
        return emit(group)

    dh5, dg_ple1, dwg1, dwp1 = _ple_bwd(1, dy, s_ple1, p, row(w['ple_norm'], 1), w1['ple_gate_w'])
    dh4, dg_mlp1, dw1_1, dw2_1 = _mlp_bwd(1, dh5, s_mlp1, row(w['mlp_norm'], 1))
    deps = emit_group({('ple_gate_w', 1): rowsh(dwg1), ('ple_proj_w', 1): colsh(dwp1),
                       ('mlp_w2', 1): dw2_1, ('mlp_w1', 1): dw1_1})
    dh3, gm = _mla_layer_bwd(dh4, s_mla, row(w['mix_norm'], 1), *mla_w, deps=deps)
    deps = emit_group({('mla_w_out', 0): rowsh(gm['wmo']), ('mla_w_uq', 0): _gather_rope(gm['wuq']),
                       ('mla_w_ukv', 0): gm['wukv'], ('mla_w_in', 0): rowsh(_gather_rope(gm['wmi']))})
    dh2, dg_ple0, _, _ = _ple_bwd(
        0, dh3, s_ple0, p, row(w['ple_norm'], 0), w0['ple_gate_w'], deps=deps,
        emit=lambda dwg, dwp: emit_group({('ple_gate_w', 0): rowsh(dwg), ('ple_proj_w', 0): colsh(dwp)}))
    dh1, dg_mlp0, _, _ = _mlp_bwd(0, dh2, s_mlp0, row(w['mlp_norm'], 0),
                                  emit_w2=lambda dw2: emit_group({('mlp_w2', 0): dw2}),
                                  emit_w1=lambda dw1: emit_group({('mlp_w1', 0): dw1}))
    dx, dg_mix0, dgn = _ret_layer_bwd(
        dh1, s_ret, row(w['mix_norm'], 0), w['ret_w_in'], w['ret_gn'], cos_r, sin_r,
        lambda dwro: emit_group({('ret_w_out', 0): rowsh(dwro)}),
        lambda dwri: emit_group({('ret_w_in', 0): dwri}))

    small = dict(
        mix_norm=[dg_mix0, gm['mix']], mlp_norm=[dg_mlp0, dg_mlp1], ple_norm=[dg_ple0, dg_ple1],
        ret_gn=dgn, mla_q_a_norm=gm['qa'], mla_kv_a_norm=gm['kva'], mla_q_norm=gm['gq'], mla_k_norm=gm['gk'],
    )
    return sq_err, dx, big, small


def _my_place():
    x, y, c = lax.axis_index("x"), lax.axis_index("y"), lax.axis_index("c")
    return x, y, c


def _flat(px, py, pc):
    return 4 * px + 2 * py + pc


def _peer(x, y, c, r):
    return (1 - x if r & 4 else x, 1 - y if r & 2 else y, 1 - c if r & 1 else c)


HBM = pl.BlockSpec(memory_space=pltpu.HBM)
SEMS = pl.BlockSpec(memory_space=pltpu.SEMAPHORE)
SIDE_EFFECT = pltpu.SideEffectType.DATAFLOW_SIDE_EFFECTING


def _rs_copies(x, y, c, srcs, lands, send_sems, recv_sems):
    copies = []
    for a in range(len(srcs)):
        for r in range(1, N_DEV):
            peer = _peer(x, y, c, r)
            k = a * (N_DEV - 1) + r - 1
            copies.append(pltpu.make_async_remote_copy(
                src_ref=srcs[a].at[_flat(*peer)], dst_ref=lands[a].at[r - 1],
                send_sem=send_sems.at[k], recv_sem=recv_sems.at[k], device_id=peer, device_id_type=MESH))
    return copies


def _rs_start(name, arrays):
    n = len(arrays)
    hbm = lambda a: pltpu.with_memory_space_constraint(a, pltpu.HBM)
    lands = [hbm(lax.empty((N_DEV - 1,) + a.shape[1:], a.dtype)) for a in arrays]

    def body(*refs):
        srcs, lnd = refs[:n], refs[n:2 * n]
        send_sems, recv_sems = refs[2 * n], refs[2 * n + 1]
        token = refs[-1]
        for cp in _rs_copies(*_my_place(), srcs, lnd, send_sems, recv_sems):
            cp.start()
        token[...] = jnp.zeros_like(token)

    outs = pl.pallas_call(
        body, name=name,
        in_specs=[HBM] * (2 * n),
        out_specs=[SEMS, SEMS] + [HBM] * (2 * n) + [pl.BlockSpec(memory_space=pltpu.VMEM)],
        out_shape=[pltpu.SemaphoreType.DMA((n * (N_DEV - 1),)), pltpu.SemaphoreType.DMA((n * (N_DEV - 1),))]
        + [pltpu.HBM(a.shape, a.dtype) for a in arrays] + [pltpu.HBM(l.shape, l.dtype) for l in lands]
        + [_sds((8, 128), F32)],
        input_output_aliases={i: 2 + i for i in range(2 * n)},
        compiler_params=pltpu.CompilerParams(has_side_effects=SIDE_EFFECT),
    )(*[hbm(a) for a in arrays], *lands)
    return outs[0], outs[1], outs[2:2 + n], outs[2 + n:2 + 2 * n], outs[-1]


def _rs_wait(name, send_sems, recv_sems, srcs, lands, after):
    n = len(srcs)

    def body(*refs):
        src_refs, lnd = refs[:n], refs[n:2 * n]
        send, recv = refs[2 * n], refs[2 * n + 1]
        for cp in _rs_copies(*_my_place(), src_refs, lnd, send, recv):
            cp.wait_send()
            cp.wait_recv()

    outs = pl.pallas_call(
        body, name=name,
        in_specs=[HBM] * (2 * n) + [SEMS, SEMS] + [ANY] * len(after),
        out_specs=[HBM] * (2 * n),
        out_shape=[pltpu.HBM(a.shape, a.dtype) for a in list(srcs) + list(lands)],
        input_output_aliases={i: i for i in range(2 * n)},
        compiler_params=pltpu.CompilerParams(has_side_effects=SIDE_EFFECT),
    )(*srcs, *lands, send_sems, recv_sems, *after)
    return outs[:n], outs[n:]


SMALL_PACK_ROWS = 16


def _all_reduce_small(rows, deps=()):
    n = len(rows)

    def body(*refs):
        ins = refs[:n]
        out_ref, mine, buf, send_sems, recv_sems = refs[n + len(deps):]
        x, y, c = _my_place()
        mine[...] = jnp.zeros_like(mine)
        for (r0, a), ref in zip(rows, ins):
            mine[r0:r0 + a.shape[0], 0:a.shape[1]] = ref[...]
        buf[_flat(x, y, c)] = mine[...]
        copies = []
        for r in range(1, N_DEV):
            peer = _peer(x, y, c, r)
            send = pltpu.make_async_remote_copy(
                src_ref=mine, dst_ref=buf.at[_flat(x, y, c)],
                send_sem=send_sems.at[r - 1], recv_sem=recv_sems.at[r - 1], device_id=peer, device_id_type=MESH)
            send.start()
            recv = pltpu.make_async_remote_copy(
                src_ref=mine, dst_ref=buf.at[_flat(*peer)],
                send_sem=send_sems.at[r - 1], recv_sem=recv_sems.at[r - 1], device_id=peer, device_id_type=MESH)
            copies.append((send, recv))
        for send, recv in copies:
            send.wait_send()
            recv.wait_recv()
        acc = buf[0]
        for s in range(1, N_DEV):
            acc = acc + buf[s]
        out_ref[...] = acc

    vm = pl.BlockSpec(memory_space=pltpu.VMEM)
    shape = (SMALL_PACK_ROWS, D_MODEL)
    return pl.pallas_call(
        body, name="all_reduce_small", in_specs=[vm] * n + [ANY] * len(deps), out_specs=vm,
        out_shape=_sds(shape, F32),
        scratch_shapes=[pltpu.VMEM(shape, F32), pltpu.VMEM((N_DEV,) + shape, F32),
                        pltpu.SemaphoreType.DMA((7,)), pltpu.SemaphoreType.DMA((7,))],
    )(*[a for _, a in rows], *deps)


def _adamw_math(w, g, m, v):
    m = ADAM_B1 * m + (1.0 - ADAM_B1) * g
    v = ADAM_B2 * v + (1.0 - ADAM_B2) * (g * g)
    m_hat = m / (1.0 - ADAM_B1 ** ADAM_STEP)
    v_hat = v / (1.0 - ADAM_B2 ** ADAM_STEP)
    delta = -ADAM_LR * (m_hat / (jnp.sqrt(v_hat) + ADAM_EPS) + ADAM_WD * w)
    return delta, m, v


def _adamw_big(name, w, m, v, srcs, lands, me):
    nl, rows, cols = w.shape
    tr = next(cand for cand in (256, 128, 64, 32, 16, 8) if rows % cand == 0)

    def body(me_ref, w_ref, m_ref, v_ref, *rest):
        src_refs, land_refs = rest[:nl], rest[nl:2 * nl]
        g_ref, d_ref, mo_ref, vo_ref = rest[2 * nl:]
        for layer in range(nl):
            @pl.when(pl.program_id(0) == layer)
            def _():
                g = src_refs[layer][...].astype(F32)
                for s in range(N_DEV - 1):
                    g = g + land_refs[layer][s].astype(F32)
                delta, mn, vn = _adamw_math(w_ref[...], g, m_ref[...], v_ref[...])
                g_ref[...] = g
                d_ref[...] = delta
                mo_ref[...] = mn
                vo_ref[...] = vn

    blk = pl.BlockSpec((None, tr, cols), lambda l, i, me_ref: (l, i, 0))
    at = lambda layer, l, i: jnp.where(l == layer, i, 0)
    own = [pl.BlockSpec((None, tr, cols), functools.partial(lambda layer, l, i, me_ref: (me_ref[0], at(layer, l, i), 0),
                                                            layer)) for layer in range(nl)]
    peers = [pl.BlockSpec((N_DEV - 1, tr, cols), functools.partial(lambda layer, l, i, me_ref: (0, at(layer, l, i), 0),
                                                                   layer)) for layer in range(nl)]
    return pl.pallas_call(
        body, name=name,
        grid_spec=pltpu.PrefetchScalarGridSpec(
            num_scalar_prefetch=1, grid=(nl, rows // tr),
            in_specs=[blk, blk, blk] + own + peers, out_specs=[blk] * 4),
        out_shape=[_sds((nl, rows, cols), F32)] * 4,
        compiler_params=_cparams(("arbitrary", "arbitrary")),
    )(me, w, m, v, *srcs, *lands)


def _adamw_small(ws, gs, ms, vs):
    n = len(ws)

    def body(*refs):
        w_refs, g_refs, m_refs, v_refs = (refs[i * n:(i + 1) * n] for i in range(4))
        d_out, m_out, v_out = (refs[(4 + i) * n:(5 + i) * n] for i in range(3))
        for i in range(n):
            delta, mn, vn = _adamw_math(w_refs[i][...], g_refs[i][...], m_refs[i][...], v_refs[i][...])
            d_out[i][...] = delta
            m_out[i][...] = mn
            v_out[i][...] = vn

    vm = pl.BlockSpec(memory_space=pltpu.VMEM)
    outs = pl.pallas_call(
        body, name="adamw_small", in_specs=[vm] * (4 * n), out_specs=[vm] * (3 * n),
        out_shape=[_sds(a.shape, F32) for a in ws] * 3,
    )(*ws, *gs, *ms, *vs)
    return outs[:n], outs[n:2 * n], outs[2 * n:]


def _pad_to(a, rows, cols):
    return jnp.pad(a, ((0, rows - a.shape[0]), (0, cols - a.shape[1])))


def _place_own(blocks):
    me = _flat(*_my_place())
    return [lax.dynamic_update_slice(lax.empty((N_DEV,) + b.shape, b.dtype), b[None], (me,) + (0,) * b.ndim)
            for b in blocks]


def _ag_copies(x, y, c, blocks, bufs, send_sems, recv_sems, arriving):
    copies = []
    for a in range(len(blocks)):
        for r in range(1, N_DEV):
            peer = _peer(x, y, c, r)
            k = a * (N_DEV - 1) + r - 1
            copies.append(pltpu.make_async_remote_copy(
                src_ref=blocks[a], dst_ref=bufs[a].at[_flat(*(peer if arriving else (x, y, c)))],
                send_sem=send_sems.at[k], recv_sem=recv_sems.at[k], device_id=peer, device_id_type=MESH))
    return copies


def _ag_start(groups, after):
    flat = [pair for g in groups for pair in g]
    n, ng = len(flat), len(groups)
    hbm = lambda a: pltpu.with_memory_space_constraint(a, pltpu.HBM)

    def body(*refs):
        blocks, bufs = refs[:n], refs[n:2 * n]
        sems = refs[2 * n + len(after):2 * n + len(after) + 2 * ng]
        x, y, c = _my_place()
        at = 0
        for gi, g in enumerate(groups):
            for cp in _ag_copies(x, y, c, blocks[at:at + len(g)], bufs[at:at + len(g)], sems[2 * gi],
                                 sems[2 * gi + 1], arriving=False):
                cp.start()
            at += len(g)
        refs[-1][...] = jnp.zeros_like(refs[-1])

    sem_shapes = [pltpu.SemaphoreType.DMA((len(g) * (N_DEV - 1),)) for g in groups for _ in range(2)]
    outs = pl.pallas_call(
        body, name="gather_start",
        in_specs=[HBM] * (2 * n) + [ANY] * len(after),
        out_specs=[SEMS] * (2 * ng) + [HBM] * (2 * n) + [pl.BlockSpec(memory_space=pltpu.VMEM)],
        out_shape=sem_shapes + [pltpu.HBM(b.shape, b.dtype) for b, _ in flat]
        + [pltpu.HBM(u.shape, u.dtype) for _, u in flat] + [_sds((8, 128), F32)],
        input_output_aliases={i: 2 * ng + i for i in range(2 * n)},
        compiler_params=pltpu.CompilerParams(has_side_effects=SIDE_EFFECT),
    )(*[hbm(b) for b, _ in flat], *[hbm(u) for _, u in flat], *after)
    blocks_thru, bufs_thru = outs[2 * ng:2 * ng + n], outs[2 * ng + n:2 * ng + 2 * n]
    started, at = [], 0
    for gi, g in enumerate(groups):
        started.append((outs[2 * gi], outs[2 * gi + 1], blocks_thru[at:at + len(g)], bufs_thru[at:at + len(g)]))
        at += len(g)
    return started, outs[-1]


def _ag_wait(name, send_sems, recv_sems, blocks, bufs, after):
    n = len(blocks)

    def body(*refs):
        for cp in _ag_copies(*_my_place(), refs[:n], refs[n:2 * n], refs[2 * n], refs[2 * n + 1], arriving=True):
            cp.wait_send()
            cp.wait_recv()

    outs = pl.pallas_call(
        body, name=name,
        in_specs=[HBM] * (2 * n) + [SEMS, SEMS] + [ANY] * len(after),
        out_specs=[HBM] * (2 * n),
        out_shape=[pltpu.HBM(a.shape, a.dtype) for a in list(blocks) + list(bufs)],
        input_output_aliases={i: i for i in range(2 * n)},
        compiler_params=pltpu.CompilerParams(has_side_effects=SIDE_EFFECT),
    )(*blocks, *bufs, send_sems, recv_sems, *after)
    return outs[n:]


def _split_call(name, body, thru, sems_in, new_sems, after):
    n, ns, nn = len(thru), len(sems_in), len(new_sems)
    hbm = lambda a: pltpu.with_memory_space_constraint(a, pltpu.HBM)

    def wrapped(*refs):
        body(refs[:n], refs[n:n + ns], refs[n + ns + len(after):n + ns + len(after) + nn])
        refs[-1][...] = jnp.zeros_like(refs[-1])

    outs = pl.pallas_call(
        wrapped, name=name,
        in_specs=[HBM] * n + [SEMS] * ns + [ANY] * len(after),
        out_specs=[SEMS] * nn + [HBM] * n + [pl.BlockSpec(memory_space=pltpu.VMEM)],
        out_shape=[pltpu.SemaphoreType.DMA((k,)) for k in new_sems] + [pltpu.HBM(a.shape, a.dtype) for a in thru]
        + [_sds((8, 128), F32)],
        input_output_aliases={i: nn + i for i in range(n)},
        compiler_params=pltpu.CompilerParams(has_side_effects=SIDE_EFFECT),
    )(*[hbm(a) for a in thru], *sems_in, *after)
    return list(outs[:nn]), list(outs[nn:nn + n]), outs[-1]


def _two_level_gather(name, blocks, bufs, after=()):
    n = len(blocks)

    def copies(refs, s1, r1, s2, r2):
        x, y, c = _my_place()
        me, sibling = (x, y, c), (x, y, 1 - c)
        chips = [(1 - x, y), (x, 1 - y), (1 - x, 1 - y)]
        blk, buf = refs[:n], refs[n:]
        out = dict(send1=[], recv1_sib=[], recv1_ici=[], send2=[], recv2=[])
        for a in range(n):
            place = lambda dev: buf[a].at[_flat(*dev)]
            for k, to in enumerate([sibling] + [(*chip, c) for chip in chips]):
                mk = lambda dst: pltpu.make_async_remote_copy(
                    src_ref=blk[a], dst_ref=dst, send_sem=s1.at[4 * a + k], recv_sem=r1.at[4 * a + k],
                    device_id=to, device_id_type=MESH)
                out['send1'].append(mk(place(me)))
                out['recv1_sib' if k == 0 else 'recv1_ici'].append(mk(place(to)))
            for j, chip in enumerate(chips):
                mk = lambda dev: pltpu.make_async_remote_copy(
                    src_ref=place(dev), dst_ref=place(dev), send_sem=s2.at[3 * a + j], recv_sem=r2.at[3 * a + j],
                    device_id=sibling, device_id_type=MESH)
                out['send2'].append(mk((*chip, c)))
                out['recv2'].append(mk((*chip, 1 - c)))
        return out

    def start(refs, sems_in, new):
        for cp in copies(refs, new[0], new[1], new[0], new[1])['send1']:
            cp.start()

    def forward(refs, sems_in, new):
        cps = copies(refs, sems_in[0], sems_in[1], new[0], new[1])
        for cp in cps['recv1_ici']:
            cp.wait_recv()
        for cp in cps['send2']:
            cp.start()

    def finish(refs, sems_in, new):
        cps = copies(refs, *sems_in)
        for cp in cps['recv1_sib'] + cps['recv2']:
            cp.wait_recv()
        for cp in cps['send1'] + cps['send2']:
            cp.wait_send()

    sems1, thru, token = _split_call(name + "_start", start, list(blocks) + list(bufs), [], [4 * n, 4 * n], after)

    def complete(after):
        sems2, thru2, token2 = _split_call(name + "_forward", forward, thru, sems1, [3 * n, 3 * n], after)
        _, thru3, _ = _split_call(name + "_wait", finish, thru2, sems1 + sems2, [], ())
        return thru3[n:], token2

    return token, complete


def _prepare_weights(p, x):
    n = N_DEV
    bf = lambda a: a.astype(BF16)
    gn_pack = jnp.concatenate([
        _pad_to(p['ret_gn'][0], RET_HEADS, 128), _pad_to(p['mla_q_a_norm'], 1, 128),
        _pad_to(p['mla_kv_a_norm'], 1, 128), jnp.zeros((2, 128), F32)], axis=0)
    ple = lambda l: [bf(p['ple_gate_w'][l]), bf(p['ple_proj_w'][l])]
    names = ('mlp_w1_0', 'mlp_w2_0', 'ple_0', 'mla_in', 'mla_up', 'mla_out', 'layer_1')
    later = [[bf(p['mlp_w1'][0])], [bf(p['mlp_w2'][0])], ple(0),
             [bf(p['mla_w_in'][0])], [bf(p['mla_w_uq'][0]), bf(p['mla_w_ukv'][0])], [bf(p['mla_w_out'][0])],
             [bf(p['mlp_w1'][1]), bf(p['mlp_w2'][1])] + ple(1)]
    first = [gn_pack, bf(p['ret_w_in'][0])]
    second = [bf(p['ret_w_out'][0])]
    token, complete_first = _two_level_gather("first_gather", first, _place_own(first))
    hn0 = _rms_fwd("mix_norm0", x, p['mix_norm'][0:1], deps=(token,))
    bufs = _place_own([b for g in later for b in g])
    bufs2 = _place_own(second)
    tables = _rope_tables(x.shape[0], token[0, 0])
    (pack, wri), token = complete_first((hn0, tables[0], tables[1], *tables[2], *bufs, *bufs2))
    token, complete_second = _two_level_gather("second_gather", second, bufs2, (token,))
    groups, at = [], 0
    for g in later:
        groups.append(list(zip(g, bufs[at:at + len(g)])))
        at += len(g)
    started, token = _ag_start(groups, (token,))

    w = {k: p[k] for k in ('mix_norm', 'mlp_norm', 'ple_norm')}
    w['hn0'] = hn0
    w['tables'] = tables
    w['ret_gn'] = pack[:, :RET_HEADS, :RET_DV // n].transpose(1, 0, 2).reshape(RET_HEADS, RET_DV)
    w['mla_q_a_norm'] = pack[:, RET_HEADS, :MLA_Q_RANK // n].reshape(1, MLA_Q_RANK)
    w['mla_kv_a_norm'] = pack[:, RET_HEADS + 1, :MLA_KV_RANK // n].reshape(1, MLA_KV_RANK)
    w['ret_w_in'] = wri
    w['mla_q_norm'] = _spread_rope(p['mla_q_norm'])
    w['mla_k_norm'] = _spread_rope(p['mla_k_norm'])
    w['deps'] = (token,)

    def fetch(name, after):
        if name == 'ret_out':
            return dict(ret_w_out=complete_second(after)[0][0].reshape(RET_V_W, D_MODEL))
        got = list(_ag_wait("gather_wait_" + name, *started[names.index(name)], after))
        if name == 'mla_in':
            return dict(mla_w_in=_spread_rope(got[0].reshape(D_MODEL, MLA_IN)))
        if name == 'mla_up':
            return dict(mla_w_uq=_spread_rope(got[0]), mla_w_ukv=got[1])
        if name == 'mla_out':
            return dict(mla_w_out=got[0].reshape(D_MODEL, D_MODEL))
        out = {}
        if name in ('mlp_w1_0', 'layer_1'):
            out['mlp_w1'] = got.pop(0)
        if name in ('mlp_w2_0', 'layer_1'):
            out['mlp_w2'] = got.pop(0)
        if name in ('ple_0', 'layer_1'):
            out['ple_gate_w'] = got[0].reshape(D_MODEL, D_MODEL)
            out['ple_proj_w'] = got[1].transpose(1, 0, 2).reshape(PLE_DIM, D_MODEL)
        return out

    return w, fetch


def _small_grads(small, after):
    rows = [(0, small['mix_norm'][0]), (1, small['mix_norm'][1]), (2, small['mlp_norm'][0]),
            (3, small['mlp_norm'][1]), (4, small['ple_norm'][0]), (5, small['ple_norm'][1]),
            (6, small['ret_gn']), (10, small['mla_q_a_norm']), (11, small['mla_kv_a_norm']),
            (12, small['mla_q_norm']), (13, small['mla_k_norm']), (14, small['sq_err'])]
    gs = _all_reduce_small(rows, after)
    me = _flat(*_my_place())
    n = N_DEV
    return dict(
        sq_err=gs[14, 0],
        mix_norm=gs[0:2], mlp_norm=gs[2:4], ple_norm=gs[4:6],
        ret_gn=lax.dynamic_slice(gs, (6, me * (RET_DV // n)), (RET_HEADS, RET_DV // n)),
        mla_q_a_norm=lax.dynamic_slice(gs, (10, me * (MLA_Q_RANK // n)), (1, MLA_Q_RANK // n)),
        mla_kv_a_norm=lax.dynamic_slice(gs, (11, me * (MLA_KV_RANK // n)), (1, MLA_KV_RANK // n)),
        mla_q_norm=_gather_rope(gs[12:13, :MLA_HD_PAD]), mla_k_norm=_gather_rope(gs[13:14, :MLA_HD_PAD]))


def kernel(x, p, mix_norm, ret_w_in, ret_gn, ret_w_out, mla_w_in, mla_q_a_norm, mla_kv_a_norm, mla_w_uq, mla_w_ukv, mla_q_norm, mla_k_norm, mla_w_out, mlp_norm, mlp_w1, mlp_w2, ple_norm, ple_gate_w, ple_proj_w, loss_target, m_mix_norm, m_ret_w_in, m_ret_gn, m_ret_w_out, m_mla_w_in, m_mla_q_a_norm, m_mla_kv_a_norm, m_mla_w_uq, m_mla_w_ukv, m_mla_q_norm, m_mla_k_norm, m_mla_w_out, m_mlp_norm, m_mlp_w1, m_mlp_w2, m_ple_norm, m_ple_gate_w, m_ple_proj_w, v_mix_norm, v_ret_w_in, v_ret_gn, v_ret_w_out, v_mla_w_in, v_mla_q_a_norm, v_mla_kv_a_norm, v_mla_w_uq, v_mla_w_ukv, v_mla_q_norm, v_mla_k_norm, v_mla_w_out, v_mlp_norm, v_mlp_w1, v_mlp_w2, v_ple_norm, v_ple_gate_w, v_ple_proj_w):
    given = dict(locals())
    params = {n: given[n] for n in WEIGHTS}
    w, fetch = _prepare_weights(params, x[0])

    started = []

    def emit(group):
        keys = list(group)
        send, recv, srcs, lands, token = _rs_start(f"rs_start{len(started)}", [group[k] for k in keys])
        started.append((keys, send, recv, srcs, lands))
        return (token,)

    sq_err, grad_x, _, small = _local_step(x[0], p, loss_target[0], w, fetch, emit)
    small['sq_err'] = sq_err[0:1]

    grads, deltas, new_m, new_v = {}, {}, {}, {}
    total = {}

    def small_updates(after):
        sg = _small_grads(small, after)
        total['loss'] = 0.5 / D_MODEL * sg['sq_err']
        two_d = lambda a: a.reshape(-1, a.shape[-1])
        d_s, m_s, v_s = _adamw_small(
            [two_d(params[n]) for n in SMALL], [sg[n] for n in SMALL],
            [two_d(given["m_" + n]) for n in SMALL], [two_d(given["v_" + n]) for n in SMALL])
        for i, n in enumerate(SMALL):
            shape = params[n].shape
            grads[n], deltas[n], new_m[n], new_v[n] = (a.reshape(shape) for a in (sg[n], d_s[i], m_s[i], v_s[i]))
        return (d_s[0],)

    me = _flat(*_my_place()).astype(jnp.int32).reshape(1)
    after = (grad_x,)
    src_of, land_of = {}, {}
    for gi, (keys, send, recv, srcs, lands) in enumerate(started):
        if gi == len(started) - 1:
            after = small_updates(after)
        srcs, lands = _rs_wait(f"rs_wait{gi}", send, recv, srcs, lands, after)
        for k, s, l in zip(keys, srcs, lands):
            src_of[k], land_of[k] = s, l
        done = [n for n in BIG if n not in grads and all((n, l) in src_of for l in range(params[n].shape[0]))]
        for n in done:
            layers = range(params[n].shape[0])
            grads[n], deltas[n], new_m[n], new_v[n] = _adamw_big(
                "adamw_" + n, params[n], given["m_" + n], given["v_" + n],
                [src_of[(n, l)] for l in layers], [land_of[(n, l)] for l in layers], me)
        if done:
            after = tuple(deltas[n] for n in done)

    return (total['loss'], grad_x[None], *[grads[n] for n in WEIGHTS], *[deltas[n] for n in WEIGHTS],
            *[new_m[n] for n in WEIGHTS], *[new_v[n] for n in WEIGHTS])
```

```python
import functools

import jax
import jax.numpy as jnp
from jax import lax
from jax.experimental import pallas as pl
from jax.experimental.pallas import tpu as pltpu

F32 = jnp.float32
BF16 = jnp.bfloat16
MESH = pl.DeviceIdType.MESH
ANY = pl.BlockSpec(memory_space=pl.ANY)

N_DEV = 8
D_MODEL = 1024
CHUNK = 64
RET_BLOCK = 4 * CHUNK
EPS = 1e-6
ROPE_THETA = 10000.0
RET_HEADS = 4
RET_DK = 256
RET_DV = 512
RET_QK_W = RET_HEADS * RET_DK
RET_V_W = RET_HEADS * RET_DV
RET_IN = 2 * RET_QK_W + 2 * RET_V_W
MLA_HEADS = 8
MLA_NOPE = 128
MLA_ROPE = 64
MLA_QKD = MLA_NOPE + MLA_ROPE
MLA_VD = 128
MLA_Q_RANK = 384
MLA_KV_RANK = 256
MLA_IN = MLA_Q_RANK + MLA_KV_RANK + MLA_ROPE
MLA_IN_PAD = 768
MLA_HD_PAD = 256
D_FF = 4096
PLE_DIM = 256
ATT_SCALE = MLA_QKD ** -0.5
LOG2E = 1.4426950408889634
ATT_EXP2 = ATT_SCALE * LOG2E

ADAM_LR = 0.001
ADAM_B1 = 0.9
ADAM_B2 = 0.999
ADAM_EPS = 1e-08
ADAM_WD = 0.01
ADAM_STEP = 10

VMEM_LIMIT = 52 * 1024 * 1024
ROW_TILE = 1024
RET_ROWS = 512
ATT_BLOCK = 256
ATT_QROWS = 1024
ATT_KROWS = 1024
ATT_HEADS = 2
PREP_ROWS = 2048

WEIGHTS = ['mix_norm', 'ret_w_in', 'ret_gn', 'ret_w_out', 'mla_w_in', 'mla_q_a_norm', 'mla_kv_a_norm',
           'mla_w_uq', 'mla_w_ukv', 'mla_q_norm', 'mla_k_norm', 'mla_w_out', 'mlp_norm', 'mlp_w1', 'mlp_w2',
           'ple_norm', 'ple_gate_w', 'ple_proj_w']
BIG = ['ret_w_in', 'ret_w_out', 'mla_w_in', 'mla_w_uq', 'mla_w_ukv', 'mla_w_out', 'mlp_w1', 'mlp_w2',
       'ple_gate_w', 'ple_proj_w']
SMALL = [w for w in WEIGHTS if w not in BIG]


def _cparams(sem=None):
    return pltpu.CompilerParams(dimension_semantics=sem, vmem_limit_bytes=VMEM_LIMIT)


def _dot(a, b, ca, cb):
    return lax.dot_general(a, b, (((ca,), (cb,)), ((), ())), preferred_element_type=F32)


def _bf(v):
    return v if v.dtype == BF16 else v.astype(BF16)


def _sigmoid(z):
    return 1.0 / (1.0 + jnp.exp(-z))


def _mm(name, grid, a, a_spec, b, b_spec, contract, outs, extras=(), epi=None, deps=(), split=None):
    nk = grid[2]
    n_ex, n_out, n_dep = len(extras), len(outs), len(deps)
    acc_shape = tuple(d for d in outs[0][1].block_shape if d is not None)
    if split is not None:
        acc_shape = (acc_shape[1], acc_shape[0] * split)

    def body(*refs):
        a_ref, b_ref = refs[:2]
        ex_refs = refs[2:2 + n_ex]
        out_refs = refs[2 + n_ex + n_dep:2 + n_ex + n_dep + n_out]

        def product():
            return _dot(_bf(a_ref[...]), _bf(b_ref[...]), contract[0], contract[1])

        def finish(acc):
            if split is not None:
                for j in range(acc_shape[1] // split):
                    out_refs[0][j] = acc[:, j * split:(j + 1) * split].astype(out_refs[0].dtype)
                return
            acc = acc[...]
            res = epi(acc, *[r[...] for r in ex_refs]) if epi is not None else (acc,)
            for o, r in zip(out_refs, res):
                o[...] = r.astype(o.dtype)

        if nk == 1:
            finish(product())
        else:
            acc_ref = refs[-1]
            k = pl.program_id(2)

            @pl.when(k == 0)
            def _():
                acc_ref[...] = jnp.zeros_like(acc_ref)

            acc_ref[...] += product()

            @pl.when(k == nk - 1)
            def _():
                finish(acc_ref)

    return pl.pallas_call(
        body, name=name, grid=grid,
        in_specs=[a_spec, b_spec] + [s for _, s in extras] + [ANY] * n_dep,
        out_specs=[s for _, s in outs],
        out_shape=[s for s, _ in outs],
        scratch_shapes=[pltpu.VMEM(acc_shape, F32)] if nk > 1 else [],
        compiler_params=_cparams(("parallel", "parallel", "arbitrary")),
    )(a, b, *[x for x, _ in extras], *deps)


def _mm_rows(name, tm, a, w, mode, outs, extras=(), epi=None, deps=()):
    n_sh, rows, cols = w.shape
    n_ex, n_out, n_dep = len(extras), len(outs), len(deps)
    by_cols = mode in ('nn_cols', 'nt_rows')
    width = cols if mode == 'nn_cols' else rows

    def body(*refs):
        a_ref, w_ref = refs[:2]
        ex_refs = refs[2:2 + n_ex]
        out_refs = refs[2 + n_ex + n_dep:2 + n_ex + n_dep + n_out]
        if by_cols:
            av = _bf(a_ref[...])
            for s in range(n_sh):
                cs = slice(s * width, (s + 1) * width)
                acc = _dot(av, w_ref[s], 1, 0 if mode == 'nn_cols' else 1)
                res = epi(acc, *[r[:, cs] for r in ex_refs]) if epi is not None else (acc,)
                for o, r in zip(out_refs, res):
                    o[:, cs] = r.astype(o.dtype)
        else:
            chunk = rows if mode == 'nn_rows' else cols
            acc = None
            for s in range(n_sh):
                part = _dot(_bf(a_ref[:, s * chunk:(s + 1) * chunk]), w_ref[s], 1, 0 if mode == 'nn_rows' else 1)
                acc = part if acc is None else acc + part
            res = epi(acc, *[r[...] for r in ex_refs]) if epi is not None else (acc,)
            for o, r in zip(out_refs, res):
                o[...] = r.astype(o.dtype)

    t, ka = a.shape
    return pl.pallas_call(
        body, name=name, grid=(t // tm, 1, 1),
        in_specs=[pl.BlockSpec((tm, ka), lambda i, j, k: (i, 0)),
                  pl.BlockSpec((n_sh, rows, cols), lambda i, j, k: (0, 0, 0))] + [s for _, s in extras] + [ANY] * n_dep,
        out_specs=[s for _, s in outs],
        out_shape=[s for s, _ in outs],
        compiler_params=_cparams(("parallel", "arbitrary", "arbitrary")),
    )(a, w, *[x for x, _ in extras], *deps)


def _sds(shape, dtype):
    return jax.ShapeDtypeStruct(shape, dtype)


def _row_tile(t, cap=ROW_TILE):
    return min(cap, t)


def _rms_fwd(name, x, g, deps=()):
    t, d = x.shape
    tm = _row_tile(t)

    def body(x_ref, g_ref, *rest):
        o_ref = rest[-1]
        xv = x_ref[...]
        r = lax.rsqrt(jnp.mean(xv * xv, axis=-1, keepdims=True) + EPS)
        o_ref[...] = (xv * r * g_ref[...]).astype(o_ref.dtype)

    return pl.pallas_call(
        body, name=name, grid=(t // tm,),
        in_specs=[pl.BlockSpec((tm, d), lambda i: (i, 0)), pl.BlockSpec((1, d), lambda i: (0, 0))] + [ANY] * len(deps),
        out_specs=pl.BlockSpec((tm, d), lambda i: (i, 0)),
        out_shape=_sds((t, d), BF16),
        compiler_params=_cparams(("parallel",)),
    )(x, g, *deps)


def _rms_bwd_rows(dy, xv, g, n):
    r = lax.rsqrt(jnp.sum(xv * xv, axis=-1, keepdims=True) / n + EPS)
    xh = xv * r
    dxh = dy * g
    dx = r * (dxh - xh * (jnp.sum(dxh * xh, axis=-1, keepdims=True) / n))
    return dx, dy * xh


def _ple_gate_bwd(name, dh, gate, e):
    t, d = dh.shape
    tm = _row_tile(t)

    def body(dh_ref, g_ref, e_ref, de_ref, dz_ref):
        dh_v, gt = dh_ref[...], g_ref[...].astype(F32)
        de_ref[...] = (dh_v * gt).astype(BF16)
        dz_ref[...] = (dh_v * e_ref[...].astype(F32) * (gt * (1.0 - gt))).astype(BF16)

    row = pl.BlockSpec((tm, d), lambda i: (i, 0))
    return pl.pallas_call(
        body, name=name, grid=(t // tm,), in_specs=[row, row, row], out_specs=[row, row],
        out_shape=[_sds((t, d), BF16), _sds((t, d), BF16)],
        compiler_params=_cparams(("parallel",)),
    )(dh, gate, e)


def _rope_half(v, cos, sin):
    half = v.shape[-1] // 2
    v1, v2 = v[:, :half], v[:, half:]
    return jnp.concatenate([v1 * cos - v2 * sin, v2 * cos + v1 * sin], axis=-1)


def _ret_consts():
    lg = jnp.log(1.0 - 2.0 ** (-5.0 - jnp.arange(RET_HEADS, dtype=F32)))
    idx = jnp.arange(RET_BLOCK, dtype=F32)
    chunk = jnp.floor(idx / CHUNK)
    dist = idx[:, None] - idx[None, :]
    same = chunk[:, None] == chunk[None, :]
    seen = jnp.where(same, jnp.abs(dist), jnp.where(chunk[None, :] < chunk[:, None], dist, jnp.inf))
    intra = jnp.exp(lg[:, None, None] * seen)
    qdec = jnp.exp(lg[:, None] * (idx + 1.0))
    kdec = jnp.exp(lg[:, None] * (RET_BLOCK - 1.0 - idx))
    cdec = jnp.exp(lg * RET_BLOCK)
    qdec = jnp.broadcast_to(qdec[:, :, None], (RET_HEADS, RET_BLOCK, RET_DK))
    kdec = jnp.broadcast_to(kdec[:, :, None], (RET_HEADS, RET_BLOCK, RET_DK))
    cdec = jnp.broadcast_to(cdec[:, None, None], (RET_HEADS, 1, RET_DV))
    return intra, qdec, kdec, cdec


def _ret_specs(rb, rev_nb=None):
    blk = (lambda i: i) if rev_nb is None else (lambda i: rev_nb - 1 - i)
    full = lambda shape: pl.BlockSpec(shape, lambda i: (0,) * len(shape))
    return dict(
        proj=pl.BlockSpec((rb, RET_IN), lambda i: (blk(i), 0)),
        tab=pl.BlockSpec((rb, RET_DK // 2), lambda i: (blk(i), 0)),
        vw=pl.BlockSpec((rb, RET_V_W), lambda i: (blk(i), 0)),
        st=pl.BlockSpec((rb // RET_BLOCK, RET_HEADS, RET_DK, RET_DV), lambda i: (blk(i), 0, 0, 0)),
        gn=full((RET_HEADS, 1, RET_DV)),
        intra=full((RET_HEADS, RET_BLOCK, RET_BLOCK)),
        dec=full((RET_HEADS, RET_BLOCK, RET_DK)),
        cdec=full((RET_HEADS, 1, RET_DV)),
    )


def _ret_fwd(proj, cos, sin, gn):
    t = proj.shape[0]
    rb = min(RET_ROWS, t)
    cpb = rb // RET_BLOCK
    intra, qdec, kdec, cdec = _ret_consts()
    sp = _ret_specs(rb)

    def body(proj_ref, cos_ref, sin_ref, gn_ref, intra_ref, qd_ref, kd_ref, cd_ref,
             gated_ref, outp_ref, st_ref, s_ref):
        @pl.when(pl.program_id(0) == 0)
        def _():
            s_ref[...] = jnp.zeros_like(s_ref)

        def chunk(c, carry):
            rows = pl.ds(pl.multiple_of(c * RET_BLOCK, RET_BLOCK), RET_BLOCK)
            cs, sn = cos_ref[rows, :], sin_ref[rows, :]
            for h in range(RET_HEADS):
                q = proj_ref[rows, h * RET_DK:(h + 1) * RET_DK].astype(F32)
                k = proj_ref[rows, RET_QK_W + h * RET_DK:RET_QK_W + (h + 1) * RET_DK].astype(F32)
                v = proj_ref[rows, 2 * RET_QK_W + h * RET_DV:2 * RET_QK_W + (h + 1) * RET_DV]
                g = proj_ref[rows, 2 * RET_QK_W + RET_V_W + h * RET_DV:
                             2 * RET_QK_W + RET_V_W + (h + 1) * RET_DV].astype(F32)
                qr = _rope_half(q, cs, sn)
                kr = _rope_half(k, cs, sn) * (RET_DK ** -0.5)
                qb, kb, vb = qr.astype(BF16), kr.astype(BF16), v
                sc = _dot(qb, kb, 1, 1) * intra_ref[h]
                inner = _dot(sc.astype(BF16), vb, 1, 0)
                s_old = s_ref[h]
                sb = s_old.astype(BF16)
                st_ref[c, h] = sb
                cross = _dot((qr * qd_ref[h]).astype(BF16), sb, 1, 0)
                out = inner + cross
                s_ref[h] = s_old * cd_ref[h] + _dot((kr * kd_ref[h]).astype(BF16), vb, 0, 0)
                r = lax.rsqrt(jnp.mean(out * out, axis=-1, keepdims=True) + EPS)
                y = out * r * gn_ref[h]
                cols = slice(h * RET_DV, (h + 1) * RET_DV)
                gated_ref[rows, cols] = (g * _sigmoid(g) * y).astype(BF16)
                outp_ref[rows, cols] = out
            return carry

        lax.fori_loop(0, cpb, chunk, 0)

    return pl.pallas_call(
        body, name="ret_fwd", grid=(t // rb,),
        in_specs=[sp['proj'], sp['tab'], sp['tab'], sp['gn'], sp['intra'], sp['dec'], sp['dec'], sp['cdec']],
        out_specs=[sp['vw'], sp['vw'], sp['st']],
        out_shape=[_sds((t, RET_V_W), BF16), _sds((t, RET_V_W), F32),
                   _sds((t // RET_BLOCK, RET_HEADS, RET_DK, RET_DV), BF16)],
        scratch_shapes=[pltpu.VMEM((RET_HEADS, RET_DK, RET_DV), F32)],
        compiler_params=_cparams(("arbitrary",)),
    )(proj, cos, sin, gn.reshape(RET_HEADS, 1, RET_DV), intra, qdec, kdec, cdec)


def _ret_gate_bwd_epi(dgt, out, g, gn):
    g = g.astype(F32)
    r = lax.rsqrt(jnp.mean(out * out, axis=-1, keepdims=True) + EPS)
    xh = out * r
    sg = _sigmoid(g)
    dgate = dgt * (xh * gn) * (sg * (1.0 + g * (1.0 - sg)))
    dy = dgt * (g * sg)
    dxh = dy * gn
    dout = r * (dxh - xh * jnp.mean(dxh * xh, axis=-1, keepdims=True))
    return dout, dgate, jnp.sum(dy * xh, axis=0, keepdims=True)


def _ret_bwd(proj, cos, sin, states, dout, dgate, deps=()):
    t = proj.shape[0]
    rb = min(RET_ROWS, t)
    cpb = rb // RET_BLOCK
    nb = t // rb
    intra, qdec, kdec, cdec = _ret_consts()
    sp = _ret_specs(rb, rev_nb=nb)

    def body(proj_ref, cos_ref, sin_ref, intra_ref, qd_ref, kd_ref, cd_ref, st_ref, dout_ref, dgate_ref, *rest):
        dproj_ref, ds_ref = rest[len(deps):]

        @pl.when(pl.program_id(0) == 0)
        def _():
            ds_ref[...] = jnp.zeros_like(ds_ref)

        def chunk(cc, carry):
            c = cpb - 1 - cc
            rows = pl.ds(pl.multiple_of(c * RET_BLOCK, RET_BLOCK), RET_BLOCK)
            cs, sn = cos_ref[rows, :], sin_ref[rows, :]
            for h in range(RET_HEADS):
                q = proj_ref[rows, h * RET_DK:(h + 1) * RET_DK].astype(F32)
                k = proj_ref[rows, RET_QK_W + h * RET_DK:RET_QK_W + (h + 1) * RET_DK].astype(F32)
                v = proj_ref[rows, 2 * RET_QK_W + h * RET_DV:2 * RET_QK_W + (h + 1) * RET_DV]
                cols = slice(h * RET_DV, (h + 1) * RET_DV)
                qr = _rope_half(q, cs, sn)
                kr = _rope_half(k, cs, sn) * (RET_DK ** -0.5)
                qb, kb, vb = qr.astype(BF16), kr.astype(BF16), v
                qdb = (qr * qd_ref[h]).astype(BF16)
                kdb = (kr * kd_ref[h]).astype(BF16)
                doutb = dout_ref[rows, cols]
                itr = intra_ref[h]
                pb = (_dot(qb, kb, 1, 1) * itr).astype(BF16)
                dv = _dot(pb, doutb, 0, 0)
                dsc = (_dot(doutb, vb, 1, 1) * itr).astype(BF16)
                dq = _dot(dsc, kb, 1, 0)
                dk = _dot(dsc, qb, 0, 0)
                dq = dq + _dot(doutb, st_ref[c, h], 1, 1) * qd_ref[h]
                ds_new = ds_ref[h]
                dsb = ds_new.astype(BF16)
                dk = dk + _dot(vb, dsb, 1, 1) * kd_ref[h]
                dv = dv + _dot(kdb, dsb, 1, 0)
                ds_ref[h] = ds_new * cd_ref[h] + _dot(qdb, doutb, 0, 0)
                dproj_ref[rows, h * RET_DK:(h + 1) * RET_DK] = _rope_half(dq, cs, -sn).astype(BF16)
                dproj_ref[rows, RET_QK_W + h * RET_DK:RET_QK_W + (h + 1) * RET_DK] = (
                    _rope_half(dk * (RET_DK ** -0.5), cs, -sn).astype(BF16))
                dproj_ref[rows, 2 * RET_QK_W + h * RET_DV:2 * RET_QK_W + (h + 1) * RET_DV] = dv.astype(BF16)
                dproj_ref[rows, 2 * RET_QK_W + RET_V_W + h * RET_DV:
                          2 * RET_QK_W + RET_V_W + (h + 1) * RET_DV] = dgate_ref[rows, cols]
            return carry

        lax.fori_loop(0, cpb, chunk, 0)

    return pl.pallas_call(
        body, name="ret_bwd", grid=(nb,),
        in_specs=[sp['proj'], sp['tab'], sp['tab'], sp['intra'], sp['dec'], sp['dec'], sp['cdec'],
                  sp['st'], sp['vw'], sp['vw']] + [ANY] * len(deps),
        out_specs=sp['proj'],
        out_shape=_sds((t, RET_IN), BF16),
        scratch_shapes=[pltpu.VMEM((RET_HEADS, RET_DK, RET_DV), F32)],
        compiler_params=_cparams(("arbitrary",)),
    )(proj, cos, sin, intra, qdec, kdec, cdec, states, dout, dgate, *deps)


def _spread_rope(a):
    return jnp.pad(a, [(0, 0)] * (a.ndim - 1) + [(0, MLA_ROPE)])


def _gather_rope(a):
    return a[..., :a.shape[-1] - MLA_ROPE]


def _rope_tables(t, zero):
    pos = jnp.arange(t, dtype=F32)[:, None] + zero
    inv = 1.0 / (ROPE_THETA ** (jnp.arange(0, RET_DK, 2, dtype=F32) / RET_DK))
    ang = pos * inv[None, :]
    return jnp.cos(ang), jnp.sin(ang), _mla_tables(t, pos)


def _mla_tables(t, pos):
    half = MLA_ROPE // 2
    inv = 1.0 / (ROPE_THETA ** (jnp.arange(0, MLA_ROPE, 2, dtype=F32) / MLA_ROPE))
    ang = pos * inv[None, :]
    cos, sin = jnp.cos(ang), jnp.sin(ang)
    z = jnp.zeros((t, half), F32)
    c = jnp.concatenate([cos, cos, z, z], axis=1)
    s1 = jnp.concatenate([-sin, z, z, z], axis=1)
    s2 = jnp.concatenate([z, sin, z, z], axis=1)
    return c, s1, s2


def _rope_tile(r, c, s1, s2):
    return r * c + pltpu.roll(r, 96, 1) * s1 + pltpu.roll(r, 32, 1) * s2


def _mla_mid(proj2, qa, kva):
    t = proj2.shape[0]
    tm = _row_tile(t)

    def body(p_ref, qa_ref, kva_ref, cq_ref, ckv_ref):
        cq = p_ref[:, :MLA_Q_RANK]
        ckv = p_ref[:, MLA_Q_RANK:MLA_Q_RANK + MLA_KV_RANK]
        rq = lax.rsqrt(jnp.mean(cq * cq, axis=-1, keepdims=True) + EPS)
        rkv = lax.rsqrt(jnp.mean(ckv * ckv, axis=-1, keepdims=True) + EPS)
        cq_ref[...] = (cq * rq * qa_ref[...]).astype(BF16)
        ckv_ref[...] = (ckv * rkv * kva_ref[...]).astype(BF16)

    return pl.pallas_call(
        body, name="mla_mid", grid=(t // tm,),
        in_specs=[pl.BlockSpec((tm, MLA_IN_PAD), lambda i: (i, 0)),
                  pl.BlockSpec((1, MLA_Q_RANK), lambda i: (0, 0)),
                  pl.BlockSpec((1, MLA_KV_RANK), lambda i: (0, 0))],
        out_specs=[pl.BlockSpec((tm, MLA_Q_RANK), lambda i: (i, 0)),
                   pl.BlockSpec((tm, MLA_KV_RANK), lambda i: (i, 0))],
        out_shape=[_sds((t, MLA_Q_RANK), BF16), _sds((t, MLA_KV_RANK), BF16)],
        compiler_params=_cparams(("parallel",)),
    )(proj2, qa, kva)


def _mla_mid_bwd(proj2, qa, kva, dcq, dckv, dkr):
    t = proj2.shape[0]
    tm = _row_tile(t)

    def body(p_ref, qa_ref, kva_ref, dcq_ref, dckv_ref, dkr_ref, dp_ref, dqa_ref, dkva_ref):
        @pl.when(pl.program_id(0) == 0)
        def _():
            dqa_ref[...] = jnp.zeros_like(dqa_ref)
            dkva_ref[...] = jnp.zeros_like(dkva_ref)

        dxq, dgq = _rms_bwd_rows(dcq_ref[...], p_ref[:, :MLA_Q_RANK], qa_ref[...], MLA_Q_RANK)
        dxk, dgk = _rms_bwd_rows(dckv_ref[...], p_ref[:, MLA_Q_RANK:MLA_Q_RANK + MLA_KV_RANK], kva_ref[...],
                                 MLA_KV_RANK)
        dp_ref[:, :MLA_Q_RANK] = dxq.astype(BF16)
        dp_ref[:, MLA_Q_RANK:MLA_Q_RANK + MLA_KV_RANK] = dxk.astype(BF16)
        dp_ref[:, MLA_Q_RANK + MLA_KV_RANK:] = dkr_ref[...].astype(BF16)
        dqa_ref[...] += jnp.sum(dgq, axis=0, keepdims=True)
        dkva_ref[...] += jnp.sum(dgk, axis=0, keepdims=True)

    return pl.pallas_call(
        body, name="mla_mid_bwd", grid=(t // tm,),
        in_specs=[pl.BlockSpec((tm, MLA_IN_PAD), lambda i: (i, 0)),
                  pl.BlockSpec((1, MLA_Q_RANK), lambda i: (0, 0)),
                  pl.BlockSpec((1, MLA_KV_RANK), lambda i: (0, 0)),
                  pl.BlockSpec((tm, MLA_Q_RANK), lambda i: (i, 0)),
                  pl.BlockSpec((tm, MLA_KV_RANK), lambda i: (i, 0)),
                  pl.BlockSpec((tm, 128), lambda i: (i, 0))],
        out_specs=[pl.BlockSpec((tm, MLA_IN_PAD), lambda i: (i, 0)),
                   pl.BlockSpec((1, MLA_Q_RANK), lambda i: (0, 0)),
                   pl.BlockSpec((1, MLA_KV_RANK), lambda i: (0, 0))],
        out_shape=[_sds((t, MLA_IN_PAD), BF16), _sds((1, MLA_Q_RANK), F32), _sds((1, MLA_KV_RANK), F32)],
        compiler_params=_cparams(("arbitrary",)),
    )(proj2, qa, kva, dcq, dckv, dkr)


def _mla_prep_specs(t, tm):
    head = lambda w: pl.BlockSpec((None, tm, w), lambda i, h: (h, i, 0))
    return dict(
        head256=head(MLA_HD_PAD), head128=head(MLA_VD),
        cols256=pl.BlockSpec((tm, MLA_HD_PAD), lambda i, h: (i, h)),
        cq=pl.BlockSpec((tm, MLA_Q_RANK), lambda i, h: (i, 0)),
        ckv=pl.BlockSpec((tm, MLA_KV_RANK), lambda i, h: (i, 0)),
        wuq=pl.BlockSpec((None, MLA_Q_RANK, MLA_HD_PAD), lambda i, h: (h, 0, 0)),
        wukv=pl.BlockSpec((None, MLA_KV_RANK, MLA_HD_PAD), lambda i, h: (h, 0, 0)),
        kr=pl.BlockSpec((tm, 128), lambda i, h: (i, (MLA_Q_RANK + MLA_KV_RANK) // 128)),
        gain=pl.BlockSpec((1, MLA_HD_PAD), lambda i, h: (0, 0)),
        tab=pl.BlockSpec((tm, 128), lambda i, h: (i, 0)),
    )


def _mla_prep(cq, ckv, wuq, wukv, proj2, gq, gk, tabs):
    t = cq.shape[0]
    tm = _row_tile(t, PREP_ROWS)
    sp = _mla_prep_specs(t, tm)

    def body(cq_ref, ckv_ref, wuq_ref, wukv_ref, kr_ref, gq_ref, gk_ref, c_ref, s1_ref, s2_ref,
             qh_ref, kh_ref, vh_ref):
        c, s1, s2 = c_ref[...], s1_ref[...], s2_ref[...]

        def norm_rope(xv, gain):
            r = lax.rsqrt(jnp.sum(xv * xv, axis=-1, keepdims=True) / MLA_QKD + EPS)
            y = xv * r * gain
            return jnp.concatenate([y[:, :MLA_NOPE], _rope_tile(y[:, MLA_NOPE:], c, s1, s2)], axis=-1)

        kvv = _dot(ckv_ref[...], wukv_ref[...], 1, 0)
        qh_ref[...] = norm_rope(_dot(cq_ref[...], wuq_ref[...], 1, 0), gq_ref[...]).astype(BF16)
        kf = jnp.concatenate([kvv[:, :MLA_NOPE], kr_ref[...]], axis=-1)
        kh_ref[...] = norm_rope(kf, gk_ref[...]).astype(BF16)
        vh_ref[...] = jnp.concatenate([kvv[:, MLA_NOPE:], jnp.ones((tm, MLA_VD), F32)], axis=-1).astype(BF16)

    return pl.pallas_call(
        body, name="mla_prep", grid=(t // tm, MLA_HEADS),
        in_specs=[sp['cq'], sp['ckv'], sp['wuq'], sp['wukv'], sp['kr'], sp['gain'], sp['gain'],
                  sp['tab'], sp['tab'], sp['tab']],
        out_specs=[sp['head256'], sp['head256'], sp['head256']],
        out_shape=[_sds((MLA_HEADS, t, MLA_HD_PAD), BF16), _sds((MLA_HEADS, t, MLA_HD_PAD), BF16),
                   _sds((MLA_HEADS, t, 2 * MLA_VD), BF16)],
        compiler_params=_cparams(("parallel", "arbitrary")),
    )(cq, ckv, wuq, wukv, proj2, gq, gk, *tabs)


def _mla_prep_bwd(cq, ckv, wuq, wukv, proj2, gq, gk, tabs, dqt, dkh, dvh):
    t = cq.shape[0]
    tm = _row_tile(t, PREP_ROWS)
    ab = dqt.shape[-1]
    sp = _mla_prep_specs(t, tm)

    def body(cq_ref, ckv_ref, wuq_ref, wukv_ref, kr_ref, gq_ref, gk_ref, c_ref, s1_ref, s2_ref,
             dqt_ref, dkh_ref, dvh_ref, dq_ref, dkv_ref, dkr_ref, dgq_ref, dgk_ref):
        dqh = jnp.concatenate([dqt_ref[b].T for b in range(tm // ab)], axis=0)
        i, h = pl.program_id(0), pl.program_id(1)

        @pl.when((i == 0) & (h == 0))
        def _():
            dgq_ref[...] = jnp.zeros_like(dgq_ref)
            dgk_ref[...] = jnp.zeros_like(dgk_ref)

        @pl.when(h == 0)
        def _():
            dkr_ref[...] = jnp.zeros_like(dkr_ref)

        c, s1, s2 = c_ref[...], s1_ref[...], s2_ref[...]

        def back(xv, gain, dout):
            dy = jnp.concatenate([dout[:, :MLA_NOPE], _rope_tile(dout[:, MLA_NOPE:], c, -s1, -s2)], axis=-1)
            return _rms_bwd_rows(dy, xv, gain, MLA_QKD)

        kvv = _dot(ckv_ref[...], wukv_ref[...], 1, 0)
        dxq, dgq = back(_dot(cq_ref[...], wuq_ref[...], 1, 0), gq_ref[...], dqh)
        kf = jnp.concatenate([kvv[:, :MLA_NOPE], kr_ref[...]], axis=-1)
        dxk, dgk = back(kf, gk_ref[...], dkh_ref[...])
        dq_ref[...] = dxq.astype(BF16)
        dkv_ref[...] = jnp.concatenate([dxk[:, :MLA_NOPE], dvh_ref[...]], axis=-1).astype(BF16)
        dkr_ref[...] += dxk[:, MLA_NOPE:]
        dgq_ref[...] += jnp.sum(dgq, axis=0, keepdims=True)
        dgk_ref[...] += jnp.sum(dgk, axis=0, keepdims=True)

    return pl.pallas_call(
        body, name="mla_prep_bwd", grid=(t // tm, MLA_HEADS),
        in_specs=[sp['cq'], sp['ckv'], sp['wuq'], sp['wukv'], sp['kr'], sp['gain'], sp['gain'],
                  sp['tab'], sp['tab'], sp['tab'],
                  pl.BlockSpec((None, tm // ab, MLA_HD_PAD, ab), lambda i, h: (h, i, 0, 0)),
                  sp['head256'], sp['head128']],
        out_specs=[sp['cols256'], sp['cols256'], sp['tab'], sp['gain'], sp['gain']],
        out_shape=[_sds((t, MLA_HEADS * MLA_HD_PAD), BF16), _sds((t, MLA_HEADS * MLA_HD_PAD), BF16),
                   _sds((t, 128), F32), _sds((1, MLA_HD_PAD), F32), _sds((1, MLA_HD_PAD), F32)],
        compiler_params=_cparams(("arbitrary", "arbitrary")),
    )(cq, ckv, wuq, wukv, proj2, gq, gk, *tabs, dqt, dkh, dvh)


def _chunk_visible(rows, cols, row_off, col_off):
    rq = lax.shift_right_logical(lax.broadcasted_iota(jnp.int32, (rows, cols), 0) + row_off, 6)
    ck = lax.shift_right_logical(lax.broadcasted_iota(jnp.int32, (rows, cols), 1) + col_off, 6)
    return ck <= rq


def _rows_to_lanes(col):
    return col.T[:8, :]


def _attn_fwd(qh, kh, vh):
    t = qh.shape[1]
    ab = min(ATT_BLOCK, t)
    tq = min(ATT_QROWS, t)
    r = tq // ab
    hg = ATT_HEADS

    def body(q_ref, k_ref, v_ref, o_ref, lse_ref, acc_ref):
        n_un = pl.program_id(1) * r
        acc_ref[...] = jnp.zeros_like(acc_ref)

        def step(b, ms, diag):
            rows = pl.ds(pl.multiple_of(b * ab, ab), ab)
            out = []
            for hh in range(hg):
                m = ms[hh]
                s = _dot(q_ref[hh], k_ref[hh, rows, :], 1, 1)
                if diag is not None:
                    s = jnp.where(_chunk_visible(tq, ab, 0, diag * ab), s, -1e30)
                m_new = jnp.maximum(m, jnp.max(s, axis=-1, keepdims=True))
                p = jnp.exp2((s - m_new) * ATT_EXP2).astype(BF16)
                acc_ref[hh] = jnp.exp2((m - m_new) * ATT_EXP2) * acc_ref[hh] + _dot(p, v_ref[hh, rows, :], 1, 0)
                out.append(m_new)
            return tuple(out)

        ms = tuple(jnp.full((tq, 1), -1e30, F32) for _ in range(hg))
        ms = lax.fori_loop(0, n_un, lambda b, st: step(b, st, None), ms)
        for d in range(r):
            ms = step(n_un + d, ms, d)
        for hh in range(hg):
            l = acc_ref[hh, :, MLA_VD:]
            o_ref[:, hh * MLA_VD:(hh + 1) * MLA_VD] = acc_ref[hh, :, :MLA_VD] / l
            lse_t = _rows_to_lanes(ms[hh] * ATT_EXP2 + jnp.log(l) * LOG2E)
            for d in range(r):
                lse_ref[hh, d] = lse_t[:, d * ab:(d + 1) * ab]

    return pl.pallas_call(
        body, name="mla_attn", grid=(MLA_HEADS // hg, t // tq),
        in_specs=[pl.BlockSpec((hg, tq, MLA_HD_PAD), lambda g, i: (g, i, 0)),
                  pl.BlockSpec((hg, t, MLA_HD_PAD), lambda g, i: (g, 0, 0)),
                  pl.BlockSpec((hg, t, 2 * MLA_VD), lambda g, i: (g, 0, 0))],
        out_specs=[pl.BlockSpec((tq, hg * MLA_VD), lambda g, i: (i, g)),
                   pl.BlockSpec((hg, r, 8, ab), lambda g, i: (g, i, 0, 0))],
        out_shape=[_sds((t, MLA_HEADS * MLA_VD), F32), _sds((MLA_HEADS, t // ab, 8, ab), F32)],
        scratch_shapes=[pltpu.VMEM((hg, tq, 2 * MLA_VD), F32)],
        compiler_params=_cparams(("parallel", "arbitrary")),
    )(qh, kh, vh)


def _attn_bwd(qh, kh, vh, dob, o, lse_t):
    t = qh.shape[1]
    ab = min(ATT_BLOCK, t)
    kb = min(ATT_KROWS, t)
    r = kb // ab
    nq = t // ab
    hg = ATT_HEADS

    def body(q_ref, k_ref, v_ref, do_ref, o_ref, lse_ref, dqt_ref, dk_ref, dv_ref, dl_ref):
        j = pl.program_id(1)

        @pl.when(j == 0)
        def _():
            dqt_ref[...] = jnp.zeros_like(dqt_ref)
            ones = jnp.ones((8, MLA_VD), F32)

            def delta(b, carry):
                rows = pl.ds(pl.multiple_of(b * ab, ab), ab)
                for hh in range(hg):
                    cols = slice(hh * MLA_VD, (hh + 1) * MLA_VD)
                    prod = do_ref[rows, cols].astype(F32) * o_ref[rows, cols]
                    dl_ref[hh, b] = lax.dot_general(ones, prod, (((1,), (1,)), ((), ())),
                                                    precision=lax.Precision.HIGHEST, preferred_element_type=F32)
                return carry

            lax.fori_loop(0, nq, delta, 0)

        ks = [k_ref[hh] for hh in range(hg)]
        vs = [v_ref[hh, :, :MLA_VD] for hh in range(hg)]
        kts = [k.T for k in ks]

        dk_ref[...] = jnp.zeros_like(dk_ref)
        dv_ref[...] = jnp.zeros_like(dv_ref)

        def step(b, carry, diag):
            rows = pl.ds(pl.multiple_of(b * ab, ab), ab)
            hi = kb if diag is None else (diag + 1) * ab
            for hh in range(hg):
                q = q_ref[hh, rows, :]
                do = do_ref[rows, hh * MLA_VD:(hh + 1) * MLA_VD]
                s_t = _dot(ks[hh][:hi], q, 1, 1)
                if diag is not None:
                    key_chunk = lax.shift_right_logical(lax.broadcasted_iota(jnp.int32, (hi, ab), 0), 6)
                    query_chunk = lax.shift_right_logical(
                        lax.broadcasted_iota(jnp.int32, (hi, ab), 1) + diag * ab, 6)
                    s_t = jnp.where(key_chunk <= query_chunk, s_t, -1e30)
                p_t = jnp.exp2(s_t * ATT_EXP2 - lse_ref[hh, b][0:1, :])
                dp_t = _dot(vs[hh][:hi], do, 1, 1)
                ds_t = (p_t * (dp_t - dl_ref[hh, b][0:1, :]) * ATT_SCALE).astype(BF16)
                dqt_ref[hh, b] += _dot(kts[hh][:, :hi], ds_t, 1, 0)
                dk_ref[hh, :hi] += _dot(ds_t, q, 1, 0)
                dv_ref[hh, :hi] += _dot(p_t.astype(BF16), do, 1, 0)
            return carry

        for d in range(r):
            step(j * r + d, 0, d)
        lax.fori_loop((j + 1) * r, nq, lambda b, c: step(b, c, None), 0)

    whole = lambda w: pl.BlockSpec((hg, t, w), lambda g, j: (g, 0, 0))
    blk = lambda w: pl.BlockSpec((hg, kb, w), lambda g, j: (g, j, 0))
    stat = pl.BlockSpec((hg, nq, 8, ab), lambda g, j: (g, 0, 0, 0))
    cols = pl.BlockSpec((t, hg * MLA_VD), lambda g, j: (0, g))
    return pl.pallas_call(
        body, name="mla_attn_bwd", grid=(MLA_HEADS // hg, t // kb),
        in_specs=[whole(MLA_HD_PAD), blk(MLA_HD_PAD), blk(2 * MLA_VD),
                  cols, cols, stat],
        out_specs=[pl.BlockSpec((hg, nq, MLA_HD_PAD, ab), lambda g, j: (g, 0, 0, 0)), blk(MLA_HD_PAD), blk(MLA_VD)],
        out_shape=[_sds((MLA_HEADS, nq, MLA_HD_PAD, ab), F32), _sds((MLA_HEADS, t, MLA_HD_PAD), F32),
                   _sds((MLA_HEADS, t, MLA_VD), F32)],
        scratch_shapes=[pltpu.VMEM((hg, nq, 8, ab), F32)],
        compiler_params=_cparams(("parallel", "arbitrary")),
    )(qh, kh, vh, dob, o, lse_t)


VEC = pl.BlockSpec((1, D_MODEL), lambda i, j, k: (0, 0))


def _rows(tm, width):
    return pl.BlockSpec((tm, width), lambda i, j, k: (i, 0))


def _residual_epi(next_gain):
    if next_gain is None:
        return [], lambda acc, hv: (acc + hv,)

    def epi(acc, hv, g):
        h_new = acc + hv
        r = lax.rsqrt(jnp.mean(h_new * h_new, axis=-1, keepdims=True) + EPS)
        return h_new, h_new * r * g

    return [(next_gain, VEC)], epi


def _residual_outs(t, row, next_gain):
    outs = [(_sds((t, D_MODEL), F32), row)]
    return outs + ([(_sds((t, D_MODEL), BF16), row)] if next_gain is not None else [])


def _mlp_fwd(l, h, hn, w1g, fetch_w2, next_gain):
    t = h.shape[0]
    tm = _row_tile(t, 512)

    def relu2(acc):
        r = jnp.maximum(acc, 0.0)
        return (r * r,)

    (u,) = _mm_rows(f"mlp_up{l}", tm, hn, w1g, 'nn_cols', [(_sds((t, D_FF), BF16), _rows(tm, D_FF))], epi=relu2)
    w2g = fetch_w2((u,))
    row = _rows(tm, D_MODEL)
    more, epi = _residual_epi(next_gain)
    h2, hn_next = _mm_rows(f"mlp_down{l}", tm, u, w2g, 'nn_rows', _residual_outs(t, row, next_gain),
                           extras=[(h, row)] + more, epi=epi)
    return h2, hn_next, (h, hn, u, w1g, w2g)


def _norm_bwd_outs(t, tm):
    return [(_sds((t, D_MODEL), F32), pl.BlockSpec((tm, D_MODEL), lambda i, j, k: (i, 0))),
            (_sds((t // tm, 1, D_MODEL), F32), pl.BlockSpec((None, 1, D_MODEL), lambda i, j, k: (i, 0, 0)))]


def _norm_bwd_epi(acc, xv, res, g):
    dx, dgr = _rms_bwd_rows(acc, xv, g, D_MODEL)
    return res + dx, jnp.sum(dgr, axis=0, keepdims=True)


def _mlp_bwd(l, dh, saved, norm_g, emit_w2=None, emit_w1=None):
    h, hn, u, w1g, w2g = saved
    t = h.shape[0]
    tm = _row_tile(t, 512)
    nsh, _, wsh = w1g.shape
    wide = _rows(tm, D_FF)
    (da,) = _mm_rows(f"mlp_du{l}", tm, dh, w2g, 'nt_rows', [(_sds((t, D_FF), BF16), wide)], extras=[(u, wide)],
                     epi=lambda acc, uv: (2.0 * jnp.sqrt(uv.astype(F32)) * acc,))
    tw = _row_tile(t, 512)
    (dw2,) = _mm(f"mlp_dw2{l}", (1, 1, t // tw),
                 u, pl.BlockSpec((tw, D_FF), lambda i, j, k: (k, 0)),
                 dh, pl.BlockSpec((tw, D_MODEL), lambda i, j, k: (k, 0)), (0, 0),
                 [(_sds((D_FF, D_MODEL), BF16), pl.BlockSpec((D_FF, D_MODEL), lambda i, j, k: (0, 0)))])
    dw2 = dw2.reshape(nsh, wsh, D_MODEL)
    (dw1,) = _mm(f"mlp_dw1{l}", (1, 1, t // tw),
                 hn, pl.BlockSpec((tw, D_MODEL), lambda i, j, k: (k, 0)),
                 da, pl.BlockSpec((tw, D_FF), lambda i, j, k: (k, 0)), (0, 0),
                 [(_sds((nsh, D_MODEL, wsh), BF16), pl.BlockSpec((nsh, D_MODEL, wsh), lambda i, j, k: (0, 0, 0)))],
                 split=wsh, deps=emit_w2(dw2) if emit_w2 else ())
    row = _rows(tm, D_MODEL)
    dh_in, dg = _mm_rows(f"mlp_dhn{l}", tm, da, w1g, 'nt_cols', _norm_bwd_outs(t, tm),
                         extras=[(h, row), (dh, row), (norm_g, VEC)], epi=_norm_bwd_epi,
                         deps=emit_w1(dw1) if emit_w1 else ())
    return dh_in, jnp.sum(dg, axis=0), dw1, dw2


def _ple_fwd(l, h, hn, p, wg, wp, next_gain, target=None):
    t = h.shape[0]
    tm = _row_tile(t, 512)
    row = pl.BlockSpec((tm, D_MODEL), lambda i, j, k: (i, 0))
    full = lambda r: pl.BlockSpec((r, D_MODEL), lambda i, j, k: (0, 0))
    f32_row, bf_row = (_sds((t, D_MODEL), F32), row), (_sds((t, D_MODEL), BF16), row)
    common = [(h, row), (p, pl.BlockSpec((None, None, tm, PLE_DIM), lambda i, j, k: (l, 0, i, 0))),
              (wp, full(PLE_DIM))]
    if target is not None:
        def loss_epi(acc, hv, pv, wpv, tv):
            gt = _sigmoid(acc)
            ev = _dot(_bf(pv), wpv, 1, 0)
            err = hv + gt * ev - tv
            sq = jnp.sum(jnp.sum(err * err, axis=-1, keepdims=True), axis=0, keepdims=True)
            return err / D_MODEL, gt, ev, jnp.broadcast_to(sq, (8, 128))

        dy, gate, e, sq = _mm(f"ple_gate{l}", (t // tm, 1, 1), hn, row, wg, full(D_MODEL), (1, 0),
                              [f32_row, bf_row, bf_row, (_sds((t // tm, 8, 128), F32),
                                                         pl.BlockSpec((None, 8, 128), lambda i, j, k: (i, 0, 0)))],
                              extras=common + [(target, row)], epi=loss_epi)
        return dy, jnp.sum(sq, axis=0), (h, hn, gate, e)

    def gate_epi(acc, hv, pv, wpv, *gain):
        gt = _sigmoid(acc)
        ev = _dot(_bf(pv), wpv, 1, 0)
        h_new = hv + gt * ev
        if not gain:
            return h_new, gt, ev
        r = lax.rsqrt(jnp.mean(h_new * h_new, axis=-1, keepdims=True) + EPS)
        return h_new, gt, ev, h_new * r * gain[0]

    res = _mm(f"ple_gate{l}", (t // tm, 1, 1), hn, row, wg, full(D_MODEL), (1, 0),
              [f32_row, bf_row, bf_row] + ([bf_row] if next_gain is not None else []),
              extras=common + ([(next_gain, VEC)] if next_gain is not None else []), epi=gate_epi)
    h_out, gate, e = res[0], res[1], res[2]
    return h_out, (res[3] if next_gain is not None else None), (h, hn, gate, e)


def _ple_bwd(l, dh, saved, p, norm_g, wg, deps=(), emit=None):
    h, hn, gate, e = saved
    t = h.shape[0]
    tm = _row_tile(t)
    tk = _row_tile(t, 512)
    de, dz = _ple_gate_bwd(f"ple_gate_bwd{l}", dh, gate, e)
    full = lambda r: pl.BlockSpec((r, D_MODEL), lambda i, j, k: (0, 0))
    rowk = pl.BlockSpec((tk, D_MODEL), lambda i, j, k: (k, 0))
    (dwp,) = _mm(f"ple_dwp{l}", (1, 1, t // tk),
                 p, pl.BlockSpec((None, None, tk, PLE_DIM), lambda i, j, k: (l, 0, k, 0)),
                 de, rowk, (0, 0), [(_sds((PLE_DIM, D_MODEL), BF16), full(PLE_DIM))], deps=deps)
    (dwg,) = _mm(f"ple_dwg{l}", (1, 1, t // tk), hn, rowk, dz, rowk, (0, 0),
                 [(_sds((D_MODEL, D_MODEL), BF16), full(D_MODEL))])
    row = pl.BlockSpec((tm, D_MODEL), lambda i, j, k: (i, 0))
    dh_in, dg = _mm(f"ple_dhn{l}", (t // tm, 1, 1), dz, row, wg, full(D_MODEL), (1, 1),
                    _norm_bwd_outs(t, tm), extras=[(h, row), (dh, row), (norm_g, VEC)], epi=_norm_bwd_epi,
                    deps=emit(dwg, dwp) if emit else ())
    return dh_in, jnp.sum(dg, axis=0), dwg, dwp


def _ret_layer_fwd(x, norm_g, wri, fetch_wro, gn, cos, sin, next_gain, hn=None, deps=()):
    t = x.shape[0]
    tm = _row_tile(t)
    nsh, _, wsh = wri.shape
    if hn is None:
        hn = _rms_fwd("mix_norm0", x, norm_g)
    tp = _row_tile(t, 512)
    (proj,) = _mm_rows("ret_in", tp, hn, wri, 'nn_cols', [(_sds((t, RET_IN), BF16), _rows(tp, RET_IN))], deps=deps)
    gated, outp, states = _ret_fwd(proj, cos, sin, gn)
    wro = fetch_wro((gated,))
    row = _rows(tp, D_MODEL)
    more, epi = _residual_epi(next_gain)
    h1, hn_next = _mm_rows("ret_out", tp, gated, wro.reshape(RET_HEADS, RET_DV, D_MODEL), 'nn_rows',
                           _residual_outs(t, row, next_gain), extras=[(x, row)] + more, epi=epi)
    return h1, hn_next, (x, hn, proj, gated, outp, states, wro)


def _ret_layer_bwd(dh, saved, norm_g, wri, gn, cos, sin, emit_out, emit_in, deps=()):
    x, hn, proj, gated, outp, states, wro = saved
    t = x.shape[0]
    tm = _row_tile(t)
    tk = _row_tile(t, 512)
    nsh, _, wsh = wri.shape
    tg = _row_tile(t, 512)
    vw = _rows(tg, RET_V_W)
    dout, dgate, dgn = _mm_rows(
        "ret_dgate", tg, dh, wro.reshape(RET_HEADS, RET_DV, D_MODEL), 'nt_rows',
        [(_sds((t, RET_V_W), BF16), vw), (_sds((t, RET_V_W), BF16), vw),
         (_sds((t // tg, 1, RET_V_W), F32), pl.BlockSpec((None, 1, RET_V_W), lambda i, j, k: (i, 0, 0)))],
        extras=[(outp, vw), (proj, pl.BlockSpec((tg, RET_V_W), lambda i, j, k: (i, (RET_IN - RET_V_W) // RET_V_W))),
                (gn.reshape(1, RET_V_W), pl.BlockSpec((1, RET_V_W), lambda i, j, k: (0, 0)))],
        epi=_ret_gate_bwd_epi, deps=deps)
    dgn = jnp.sum(dgn, axis=0)
    (dwro,) = _mm("ret_dwro", (1, 1, t // tk),
                  gated, pl.BlockSpec((tk, RET_V_W), lambda i, j, k: (k, 0)),
                  dh, pl.BlockSpec((tk, D_MODEL), lambda i, j, k: (k, 0)), (0, 0),
                  [(_sds((RET_V_W, D_MODEL), BF16), pl.BlockSpec((RET_V_W, D_MODEL), lambda i, j, k: (0, 0)))])
    dproj = _ret_bwd(proj, cos, sin, states, dout, dgate, deps=emit_out(dwro))
    half = nsh // 2
    (dwri,) = _mm("ret_dwri", (2, 1, t // tk),
                  hn, pl.BlockSpec((tk, D_MODEL), lambda i, j, k: (k, 0)),
                  dproj, pl.BlockSpec((tk, half * wsh), lambda i, j, k: (k, i)), (0, 0),
                  [(_sds((nsh, D_MODEL, wsh), BF16), pl.BlockSpec((half, D_MODEL, wsh), lambda i, j, k: (i, 0, 0)))],
                  split=wsh)
    deps = emit_in(dwri)
    td = _row_tile(t, 256)
    row = _rows(td, D_MODEL)
    dx, dg = _mm_rows("ret_dhn", td, dproj, wri, 'nt_cols', _norm_bwd_outs(t, td),
                      extras=[(x, row), (dh, row), (norm_g, VEC)], epi=_norm_bwd_epi, deps=deps)
    return dx, jnp.sum(dg, axis=0), dgn.reshape(RET_HEADS, RET_DV)


def _mla_layer_fwd(h, hn, fetch, qa, kva, gq, gk, tabs, next_gain):
    t = h.shape[0]
    tm = _row_tile(t)
    row = pl.BlockSpec((tm, D_MODEL), lambda i, j, k: (i, 0))
    wmi = fetch('mla_in', (h,))['mla_w_in']
    (proj2,) = _mm("mla_in", (t // tm, 1, 1), hn, row,
                   wmi, pl.BlockSpec((D_MODEL, MLA_IN_PAD), lambda i, j, k: (0, 0)), (1, 0),
                   [(_sds((t, MLA_IN_PAD), F32), pl.BlockSpec((tm, MLA_IN_PAD), lambda i, j, k: (i, 0)))])
    cq, ckv = _mla_mid(proj2, qa, kva)
    up = fetch('mla_up', (cq,))
    wuq, wukv = up['mla_w_uq'], up['mla_w_ukv']
    qh, kh, vh = _mla_prep(cq, ckv, wuq, wukv, proj2, gq, gk, tabs)
    o, lse = _attn_fwd(qh, kh, vh)
    wmo = fetch('mla_out', (o,))['mla_w_out']
    more, epi = _residual_epi(next_gain)
    h_out, hn_next = _mm("mla_out", (t // tm, 1, 1), o, row,
                         wmo, pl.BlockSpec((D_MODEL, D_MODEL), lambda i, j, k: (0, 0)), (1, 0),
                         _residual_outs(t, row, next_gain), extras=[(h, row)] + more, epi=epi)
    return h_out, hn_next, (h, hn, proj2, cq, ckv, qh, kh, vh, o, lse), (wmi, wuq, wukv, wmo)


def _mla_layer_bwd(dh, saved, norm_g, wmi, qa, kva, wuq, wukv, gq, gk, wmo, tabs, deps=()):
    h, hn, proj2, cq, ckv, qh, kh, vh, o, lse = saved
    t = h.shape[0]
    tm = _row_tile(t)
    tk = _row_tile(t, 512)
    row = pl.BlockSpec((tm, D_MODEL), lambda i, j, k: (i, 0))
    rowk = pl.BlockSpec((tk, D_MODEL), lambda i, j, k: (k, 0))
    sq = pl.BlockSpec((D_MODEL, D_MODEL), lambda i, j, k: (0, 0))
    (dob,) = _mm("mla_do", (t // tm, 1, 1), dh, row, wmo, sq, (1, 1), [(_sds((t, D_MODEL), BF16), row)], deps=deps)
    (dwmo,) = _mm("mla_dwo", (1, 1, t // tk), o, rowk, dh, rowk, (0, 0), [(_sds((D_MODEL, D_MODEL), BF16), sq)])
    dqt, dkh, dvh = _attn_bwd(qh, kh, vh, dob, o, lse)
    dq, dkv, dkr, dgq, dgk = _mla_prep_bwd(cq, ckv, wuq, wukv, proj2, gq, gk, tabs, dqt, dkh, dvh)

    wide = MLA_HEADS * MLA_HD_PAD
    widek = pl.BlockSpec((tk, wide), lambda i, j, k: (k, 0))
    (dwuq,) = _mm("mla_dwuq", (1, 1, t // tk),
                  cq, pl.BlockSpec((tk, MLA_Q_RANK), lambda i, j, k: (k, 0)), dq, widek, (0, 0),
                  [(_sds((MLA_HEADS, MLA_Q_RANK, MLA_HD_PAD), BF16),
                    pl.BlockSpec((MLA_HEADS, MLA_Q_RANK, MLA_HD_PAD), lambda i, j, k: (0, 0, 0)))], split=MLA_HD_PAD)
    (dwukv,) = _mm("mla_dwukv", (1, 1, t // tk),
                   ckv, pl.BlockSpec((tk, MLA_KV_RANK), lambda i, j, k: (k, 0)), dkv, widek, (0, 0),
                   [(_sds((MLA_HEADS, MLA_KV_RANK, MLA_HD_PAD), BF16),
                     pl.BlockSpec((MLA_HEADS, MLA_KV_RANK, MLA_HD_PAD), lambda i, j, k: (0, 0, 0)))],
                   split=MLA_HD_PAD)
    side_by_side = lambda wg: wg.transpose(1, 0, 2).reshape(wg.shape[1], wide)
    widei = pl.BlockSpec((tm, wide), lambda i, j, k: (i, 0))
    (dcq,) = _mm("mla_dcq", (t // tm, 1, 1), dq, widei,
                 side_by_side(wuq), pl.BlockSpec((MLA_Q_RANK, wide), lambda i, j, k: (0, 0)), (1, 1),
                 [(_sds((t, MLA_Q_RANK), F32), pl.BlockSpec((tm, MLA_Q_RANK), lambda i, j, k: (i, 0)))])
    (dckv,) = _mm("mla_dckv", (t // tm, 1, 1), dkv, widei,
                  side_by_side(wukv), pl.BlockSpec((MLA_KV_RANK, wide), lambda i, j, k: (0, 0)), (1, 1),
                  [(_sds((t, MLA_KV_RANK), F32), pl.BlockSpec((tm, MLA_KV_RANK), lambda i, j, k: (i, 0)))])
    dproj2, dqa, dkva = _mla_mid_bwd(proj2, qa, kva, dcq, dckv, dkr)
    win = pl.BlockSpec((D_MODEL, MLA_IN_PAD), lambda i, j, k: (0, 0))
    (dwmi,) = _mm("mla_dwin", (1, 1, t // tk), hn, rowk,
                  dproj2, pl.BlockSpec((tk, MLA_IN_PAD), lambda i, j, k: (k, 0)), (0, 0),
                  [(_sds((D_MODEL, MLA_IN_PAD), BF16), win)])
    dh_in, dg = _mm("mla_dhn", (t // tm, 1, 1),
                    dproj2, pl.BlockSpec((tm, MLA_IN_PAD), lambda i, j, k: (i, 0)), wmi, win, (1, 1),
                    _norm_bwd_outs(t, tm), extras=[(h, row), (dh, row), (norm_g, VEC)], epi=_norm_bwd_epi)
    return dh_in, dict(mix=jnp.sum(dg, axis=0), wmi=dwmi, qa=dqa, kva=dkva, wuq=dwuq, wukv=dwukv, gq=dgq, gk=dgk,
                       wmo=dwmo)


def _local_step(x, p, target, w, fetch, emit=lambda group: ()):
    t = x.shape[0]
    cos_r, sin_r, tabs = w['tables'] if 'tables' in w else _rope_tables(t, 0.0)
    row = lambda a, i: a[i:i + 1]

    h1, hn1, s_ret = _ret_layer_fwd(x, row(w['mix_norm'], 0), w['ret_w_in'],
                                    lambda after: fetch('ret_out', after)['ret_w_out'], w['ret_gn'], cos_r, sin_r,
                                    row(w['mlp_norm'], 0), hn=w.get('hn0'), deps=w['deps'])
    h2, hn2, s_mlp0 = _mlp_fwd(0, h1, hn1, fetch('mlp_w1_0', (h1,))['mlp_w1'],
                               lambda after: fetch('mlp_w2_0', after)['mlp_w2'], row(w['ple_norm'], 0))
    w0 = fetch('ple_0', (h2,))
    h3, hn3, s_ple0 = _ple_fwd(0, h2, hn2, p, w0['ple_gate_w'], w0['ple_proj_w'], row(w['mix_norm'], 1))
    h4, hn4, s_mla, (wmi, wuq, wukv, wmo) = _mla_layer_fwd(
        h3, hn3, fetch, w['mla_q_a_norm'], w['mla_kv_a_norm'], w['mla_q_norm'], w['mla_k_norm'], tabs,
        row(w['mlp_norm'], 1))
    mla_w = (wmi, w['mla_q_a_norm'], w['mla_kv_a_norm'], wuq, wukv, w['mla_q_norm'], w['mla_k_norm'], wmo, tabs)
    w1 = fetch('layer_1', (h4,))
    h5, hn5, s_mlp1 = _mlp_fwd(1, h4, hn4, w1['mlp_w1'], lambda after: w1['mlp_w2'], row(w['ple_norm'], 1))
    dy, sq_err, s_ple1 = _ple_fwd(1, h5, hn5, p, w1['ple_gate_w'], w1['ple_proj_w'], None, target)

    n = N_DEV
    colsh = lambda a: a.reshape(a.shape[0], n, a.shape[1] // n).transpose(1, 0, 2)
    rowsh = lambda a: a.reshape(n, a.shape[0] // n, a.shape[1])
    big = {}

    def emit_group(group):
        big.update(group)
        return emit(group)

    dh5, dg_ple1, dwg1, dwp1 = _ple_bwd(1, dy, s_ple1, p, row(w['ple_norm'], 1), w1['ple_gate_w'])
    dh4, dg_mlp1, dw1_1, dw2_1 = _mlp_bwd(1, dh5, s_mlp1, row(w['mlp_norm'], 1))
    deps = emit_group({('ple_gate_w', 1): rowsh(dwg1), ('ple_proj_w', 1): colsh(dwp1),
                       ('mlp_w2', 1): dw2_1, ('mlp_w1', 1): dw1_1})
    dh3, gm = _mla_layer_bwd(dh4, s_mla, row(w['mix_norm'], 1), *mla_w, deps=deps)
    deps = emit_group({('mla_w_out', 0): rowsh(gm['wmo']), ('mla_w_uq', 0): _gather_rope(gm['wuq']),
                       ('mla_w_ukv', 0): gm['wukv'], ('mla_w_in', 0): rowsh(_gather_rope(gm['wmi']))})
    dh2, dg_ple0, _, _ = _ple_bwd(
        0, dh3, s_ple0, p, row(w['ple_norm'], 0), w0['ple_gate_w'], deps=deps,
        emit=lambda dwg, dwp: emit_group({('ple_gate_w', 0): rowsh(dwg), ('ple_proj_w', 0): colsh(dwp)}))
    dh1, dg_mlp0, _, _ = _mlp_bwd(0, dh2, s_mlp0, row(w['mlp_norm'], 0),
                                  emit_w2=lambda dw2: emit_group({('mlp_w2', 0): dw2}),
                                  emit_w1=lambda dw1: emit_group({('mlp_w1', 0): dw1}))
    dx, dg_mix0, dgn = _ret_layer_bwd(
        dh1, s_ret, row(w['mix_norm'], 0), w['ret_w_in'], w['ret_gn'], cos_r, sin_r,
        lambda dwro: emit_group({('ret_w_out', 0): rowsh(dwro)}),
        lambda dwri: emit_group({('ret_w_in', 0): dwri}))

    small = dict(
        mix_norm=[dg_mix0, gm['mix']], mlp_norm=[dg_mlp0, dg_mlp1], ple_norm=[dg_ple0, dg_ple1],
        ret_gn=dgn, mla_q_a_norm=gm['qa'], mla_kv_a_norm=gm['kva'], mla_q_norm=gm['gq'], mla_k_norm=gm['gk'],
    )
    return sq_err, dx, big, small


def _my_place():
    x, y, c = lax.axis_index("x"), lax.axis_index("y"), lax.axis_index("c")
    return x, y, c


def _flat(px, py, pc):
    return 4 * px + 2 * py + pc


def _peer(x, y, c, r):
    return (1 - x if r & 4 else x, 1 - y if r & 2 else y, 1 - c if r & 1 else c)


HBM = pl.BlockSpec(memory_space=pltpu.HBM)
SEMS = pl.BlockSpec(memory_space=pltpu.SEMAPHORE)
SIDE_EFFECT = pltpu.SideEffectType.DATAFLOW_SIDE_EFFECTING


def _rs_copies(x, y, c, srcs, lands, send_sems, recv_sems):
    copies = []
    for a in range(len(srcs)):
        for r in range(1, N_DEV):
            peer = _peer(x, y, c, r)
            k = a * (N_DEV - 1) + r - 1
            copies.append(pltpu.make_async_remote_copy(
                src_ref=srcs[a].at[_flat(*peer)], dst_ref=lands[a].at[r - 1],
                send_sem=send_sems.at[k], recv_sem=recv_sems.at[k], device_id=peer, device_id_type=MESH))
    return copies


def _rs_start(name, arrays):
    n = len(arrays)
    hbm = lambda a: pltpu.with_memory_space_constraint(a, pltpu.HBM)
    lands = [hbm(lax.empty((N_DEV - 1,) + a.shape[1:], a.dtype)) for a in arrays]

    def body(*refs):
        srcs, lnd = refs[:n], refs[n:2 * n]
        send_sems, recv_sems = refs[2 * n], refs[2 * n + 1]
        token = refs[-1]
        for cp in _rs_copies(*_my_place(), srcs, lnd, send_sems, recv_sems):
            cp.start()
        token[...] = jnp.zeros_like(token)

    outs = pl.pallas_call(
        body, name=name,
        in_specs=[HBM] * (2 * n),
        out_specs=[SEMS, SEMS] + [HBM] * (2 * n) + [pl.BlockSpec(memory_space=pltpu.VMEM)],
        out_shape=[pltpu.SemaphoreType.DMA((n * (N_DEV - 1),)), pltpu.SemaphoreType.DMA((n * (N_DEV - 1),))]
        + [pltpu.HBM(a.shape, a.dtype) for a in arrays] + [pltpu.HBM(l.shape, l.dtype) for l in lands]
        + [_sds((8, 128), F32)],
        input_output_aliases={i: 2 + i for i in range(2 * n)},
        compiler_params=pltpu.CompilerParams(has_side_effects=SIDE_EFFECT),
    )(*[hbm(a) for a in arrays], *lands)
    return outs[0], outs[1], outs[2:2 + n], outs[2 + n:2 + 2 * n], outs[-1]


def _rs_wait(name, send_sems, recv_sems, srcs, lands, after):
    n = len(srcs)

    def body(*refs):
        src_refs, lnd = refs[:n], refs[n:2 * n]
        send, recv = refs[2 * n], refs[2 * n + 1]
        for cp in _rs_copies(*_my_place(), src_refs, lnd, send, recv):
            cp.wait_send()
            cp.wait_recv()

    outs = pl.pallas_call(
        body, name=name,
        in_specs=[HBM] * (2 * n) + [SEMS, SEMS] + [ANY] * len(after),
        out_specs=[HBM] * (2 * n),
        out_shape=[pltpu.HBM(a.shape, a.dtype) for a in list(srcs) + list(lands)],
        input_output_aliases={i: i for i in range(2 * n)},
        compiler_params=pltpu.CompilerParams(has_side_effects=SIDE_EFFECT),
    )(*srcs, *lands, send_sems, recv_sems, *after)
    return outs[:n], outs[n:]


SMALL_PACK_ROWS = 16


def _all_reduce_small(rows, deps=()):
    n = len(rows)

    def body(*refs):
        ins = refs[:n]
        out_ref, mine, buf, send_sems, recv_sems = refs[n + len(deps):]
        x, y, c = _my_place()
        mine[...] = jnp.zeros_like(mine)
        for (r0, a), ref in zip(rows, ins):
            mine[r0:r0 + a.shape[0], 0:a.shape[1]] = ref[...]
        buf[_flat(x, y, c)] = mine[...]
        copies = []
        for r in range(1, N_DEV):
            peer = _peer(x, y, c, r)
            send = pltpu.make_async_remote_copy(
                src_ref=mine, dst_ref=buf.at[_flat(x, y, c)],
                send_sem=send_sems.at[r - 1], recv_sem=recv_sems.at[r - 1], device_id=peer, device_id_type=MESH)
            send.start()
            recv = pltpu.make_async_remote_copy(
                src_ref=mine, dst_ref=buf.at[_flat(*peer)],
                send_sem=send_sems.at[r - 1], recv_sem=recv_sems.at[r - 1], device_id=peer, device_id_type=MESH)
            copies.append((send, recv))
        for send, recv in copies:
            send.wait_send()
            recv.wait_recv()
        acc = buf[0]
        for s in range(1, N_DEV):
            acc = acc + buf[s]
        out_ref[...] = acc

    vm = pl.BlockSpec(memory_space=pltpu.VMEM)
    shape = (SMALL_PACK_ROWS, D_MODEL)
    return pl.pallas_call(
        body, name="all_reduce_small", in_specs=[vm] * n + [ANY] * len(deps), out_specs=vm,
        out_shape=_sds(shape, F32),
        scratch_shapes=[pltpu.VMEM(shape, F32), pltpu.VMEM((N_DEV,) + shape, F32),
                        pltpu.SemaphoreType.DMA((7,)), pltpu.SemaphoreType.DMA((7,))],
    )(*[a for _, a in rows], *deps)


def _adamw_math(w, g, m, v):
    m = ADAM_B1 * m + (1.0 - ADAM_B1) * g
    v = ADAM_B2 * v + (1.0 - ADAM_B2) * (g * g)
    m_hat = m / (1.0 - ADAM_B1 ** ADAM_STEP)
    v_hat = v / (1.0 - ADAM_B2 ** ADAM_STEP)
    delta = -ADAM_LR * (m_hat / (jnp.sqrt(v_hat) + ADAM_EPS) + ADAM_WD * w)
    return delta, m, v


def _adamw_big(name, w, m, v, srcs, lands, me):
    nl, rows, cols = w.shape
    tr = next(cand for cand in (256, 128, 64, 32, 16, 8) if rows % cand == 0)

    def body(me_ref, w_ref, m_ref, v_ref, *rest):
        src_refs, land_refs = rest[:nl], rest[nl:2 * nl]
        g_ref, d_ref, mo_ref, vo_ref = rest[2 * nl:]
        for layer in range(nl):
            @pl.when(pl.program_id(0) == layer)
            def _():
                g = src_refs[layer][...].astype(F32)
                for s in range(N_DEV - 1):
                    g = g + land_refs[layer][s].astype(F32)
                delta, mn, vn = _adamw_math(w_ref[...], g, m_ref[...], v_ref[...])
                g_ref[...] = g
                d_ref[...] = delta
                mo_ref[...] = mn
                vo_ref[...] = vn

    blk = pl.BlockSpec((None, tr, cols), lambda l, i, me_ref: (l, i, 0))
    at = lambda layer, l, i: jnp.where(l == layer, i, 0)
    own = [pl.BlockSpec((None, tr, cols), functools.partial(lambda layer, l, i, me_ref: (me_ref[0], at(layer, l, i), 0),
                                                            layer)) for layer in range(nl)]
    peers = [pl.BlockSpec((N_DEV - 1, tr, cols), functools.partial(lambda layer, l, i, me_ref: (0, at(layer, l, i), 0),
                                                                   layer)) for layer in range(nl)]
    return pl.pallas_call(
        body, name=name,
        grid_spec=pltpu.PrefetchScalarGridSpec(
            num_scalar_prefetch=1, grid=(nl, rows // tr),
            in_specs=[blk, blk, blk] + own + peers, out_specs=[blk] * 4),
        out_shape=[_sds((nl, rows, cols), F32)] * 4,
        compiler_params=_cparams(("arbitrary", "arbitrary")),
    )(me, w, m, v, *srcs, *lands)


def _adamw_small(ws, gs, ms, vs):
    n = len(ws)

    def body(*refs):
        w_refs, g_refs, m_refs, v_refs = (refs[i * n:(i + 1) * n] for i in range(4))
        d_out, m_out, v_out = (refs[(4 + i) * n:(5 + i) * n] for i in range(3))
        for i in range(n):
            delta, mn, vn = _adamw_math(w_refs[i][...], g_refs[i][...], m_refs[i][...], v_refs[i][...])
            d_out[i][...] = delta
            m_out[i][...] = mn
            v_out[i][...] = vn

    vm = pl.BlockSpec(memory_space=pltpu.VMEM)
    outs = pl.pallas_call(
        body, name="adamw_small", in_specs=[vm] * (4 * n), out_specs=[vm] * (3 * n),
        out_shape=[_sds(a.shape, F32) for a in ws] * 3,
    )(*ws, *gs, *ms, *vs)
    return outs[:n], outs[n:2 * n], outs[2 * n:]


def _pad_to(a, rows, cols):
    return jnp.pad(a, ((0, rows - a.shape[0]), (0, cols - a.shape[1])))


def _place_own(blocks):
    me = _flat(*_my_place())
    return [lax.dynamic_update_slice(lax.empty((N_DEV,) + b.shape, b.dtype), b[None], (me,) + (0,) * b.ndim)
            for b in blocks]


def _ag_copies(x, y, c, blocks, bufs, send_sems, recv_sems, arriving):
    copies = []
    for a in range(len(blocks)):
        for r in range(1, N_DEV):
            peer = _peer(x, y, c, r)
            k = a * (N_DEV - 1) + r - 1
            copies.append(pltpu.make_async_remote_copy(
                src_ref=blocks[a], dst_ref=bufs[a].at[_flat(*(peer if arriving else (x, y, c)))],
                send_sem=send_sems.at[k], recv_sem=recv_sems.at[k], device_id=peer, device_id_type=MESH))
    return copies


def _ag_start(groups, after):
    flat = [pair for g in groups for pair in g]
    n, ng = len(flat), len(groups)
    hbm = lambda a: pltpu.with_memory_space_constraint(a, pltpu.HBM)

    def body(*refs):
        blocks, bufs = refs[:n], refs[n:2 * n]
        sems = refs[2 * n + len(after):2 * n + len(after) + 2 * ng]
        x, y, c = _my_place()
        at = 0
        for gi, g in enumerate(groups):
            for cp in _ag_copies(x, y, c, blocks[at:at + len(g)], bufs[at:at + len(g)], sems[2 * gi],
                                 sems[2 * gi + 1], arriving=False):
                cp.start()
            at += len(g)
        refs[-1][...] = jnp.zeros_like(refs[-1])

    sem_shapes = [pltpu.SemaphoreType.DMA((len(g) * (N_DEV - 1),)) for g in groups for _ in range(2)]
    outs = pl.pallas_call(
        body, name="gather_start",
        in_specs=[HBM] * (2 * n) + [ANY] * len(after),
        out_specs=[SEMS] * (2 * ng) + [HBM] * (2 * n) + [pl.BlockSpec(memory_space=pltpu.VMEM)],
        out_shape=sem_shapes + [pltpu.HBM(b.shape, b.dtype) for b, _ in flat]
        + [pltpu.HBM(u.shape, u.dtype) for _, u in flat] + [_sds((8, 128), F32)],
        input_output_aliases={i: 2 * ng + i for i in range(2 * n)},
        compiler_params=pltpu.CompilerParams(has_side_effects=SIDE_EFFECT),
    )(*[hbm(b) for b, _ in flat], *[hbm(u) for _, u in flat], *after)
    blocks_thru, bufs_thru = outs[2 * ng:2 * ng + n], outs[2 * ng + n:2 * ng + 2 * n]
    started, at = [], 0
    for gi, g in enumerate(groups):
        started.append((outs[2 * gi], outs[2 * gi + 1], blocks_thru[at:at + len(g)], bufs_thru[at:at + len(g)]))
        at += len(g)
    return started, outs[-1]


def _ag_wait(name, send_sems, recv_sems, blocks, bufs, after):
    n = len(blocks)

    def body(*refs):
        for cp in _ag_copies(*_my_place(), refs[:n], refs[n:2 * n], refs[2 * n], refs[2 * n + 1], arriving=True):
            cp.wait_send()
            cp.wait_recv()

    outs = pl.pallas_call(
        body, name=name,
        in_specs=[HBM] * (2 * n) + [SEMS, SEMS] + [ANY] * len(after),
        out_specs=[HBM] * (2 * n),
        out_shape=[pltpu.HBM(a.shape, a.dtype) for a in list(blocks) + list(bufs)],
        input_output_aliases={i: i for i in range(2 * n)},
        compiler_params=pltpu.CompilerParams(has_side_effects=SIDE_EFFECT),
    )(*blocks, *bufs, send_sems, recv_sems, *after)
    return outs[n:]


def _split_call(name, body, thru, sems_in, new_sems, after):
    n, ns, nn = len(thru), len(sems_in), len(new_sems)
    hbm = lambda a: pltpu.with_memory_space_constraint(a, pltpu.HBM)

    def wrapped(*refs):
        body(refs[:n], refs[n:n + ns], refs[n + ns + len(after):n + ns + len(after) + nn])
        refs[-1][...] = jnp.zeros_like(refs[-1])

    outs = pl.pallas_call(
        wrapped, name=name,
        in_specs=[HBM] * n + [SEMS] * ns + [ANY] * len(after),
        out_specs=[SEMS] * nn + [HBM] * n + [pl.BlockSpec(memory_space=pltpu.VMEM)],
        out_shape=[pltpu.SemaphoreType.DMA((k,)) for k in new_sems] + [pltpu.HBM(a.shape, a.dtype) for a in thru]
        + [_sds((8, 128), F32)],
        input_output_aliases={i: nn + i for i in range(n)},
        compiler_params=pltpu.CompilerParams(has_side_effects=SIDE_EFFECT),
    )(*[hbm(a) for a in thru], *sems_in, *after)
    return list(outs[:nn]), list(outs[nn:nn + n]), outs[-1]


def _two_level_gather(name, blocks, bufs, after=()):
    n = len(blocks)

    def copies(refs, s1, r1, s2, r2):
        x, y, c = _my_place()
        me, sibling = (x, y, c), (x, y, 1 - c)
        chips = [(1 - x, y), (x, 1 - y), (1 - x, 1 - y)]
        blk, buf = refs[:n], refs[n:]
        out = dict(send1=[], recv1_sib=[], recv1_ici=[], send2=[], recv2=[])
        for a in range(n):
            place = lambda dev: buf[a].at[_flat(*dev)]
            for k, to in enumerate([sibling] + [(*chip, c) for chip in chips]):
                mk = lambda dst: pltpu.make_async_remote_copy(
                    src_ref=blk[a], dst_ref=dst, send_sem=s1.at[4 * a + k], recv_sem=r1.at[4 * a + k],
                    device_id=to, device_id_type=MESH)
                out['send1'].append(mk(place(me)))
                out['recv1_sib' if k == 0 else 'recv1_ici'].append(mk(place(to)))
            for j, chip in enumerate(chips):
                mk = lambda dev: pltpu.make_async_remote_copy(
                    src_ref=place(dev), dst_ref=place(dev), send_sem=s2.at[3 * a + j], recv_sem=r2.at[3 * a + j],
                    device_id=sibling, device_id_type=MESH)
                out['send2'].append(mk((*chip, c)))
                out['recv2'].append(mk((*chip, 1 - c)))
        return out

    def start(refs, sems_in, new):
        for cp in copies(refs, new[0], new[1], new[0], new[1])['send1']:
            cp.start()

    def forward(refs, sems_in, new):
        cps = copies(refs, sems_in[0], sems_in[1], new[0], new[1])
        for cp in cps['recv1_ici']:
            cp.wait_recv()
        for cp in cps['send2']:
            cp.start()

    def finish(refs, sems_in, new):
        cps = copies(refs, *sems_in)
        for cp in cps['recv1_sib'] + cps['recv2']:
            cp.wait_recv()
        for cp in cps['send1'] + cps['send2']:
            cp.wait_send()

    sems1, thru, token = _split_call(name + "_start", start, list(blocks) + list(bufs), [], [4 * n, 4 * n], after)

    def complete(after):
        sems2, thru2, token2 = _split_call(name + "_forward", forward, thru, sems1, [3 * n, 3 * n], after)
        _, thru3, _ = _split_call(name + "_wait", finish, thru2, sems1 + sems2, [], ())
        return thru3[n:], token2

    return token, complete


def _prepare_weights(p, x):
    n = N_DEV
    bf = lambda a: a.astype(BF16)
    gn_pack = jnp.concatenate([
        _pad_to(p['ret_gn'][0], RET_HEADS, 128), _pad_to(p['mla_q_a_norm'], 1, 128),
        _pad_to(p['mla_kv_a_norm'], 1, 128), jnp.zeros((2, 128), F32)], axis=0)
    ple = lambda l: [bf(p['ple_gate_w'][l]), bf(p['ple_proj_w'][l])]
    names = ('mlp_w1_0', 'mlp_w2_0', 'ple_0', 'mla_in', 'mla_up', 'mla_out', 'layer_1')
    later = [[bf(p['mlp_w1'][0])], [bf(p['mlp_w2'][0])], ple(0),
             [bf(p['mla_w_in'][0])], [bf(p['mla_w_uq'][0]), bf(p['mla_w_ukv'][0])], [bf(p['mla_w_out'][0])],
             [bf(p['mlp_w1'][1]), bf(p['mlp_w2'][1])] + ple(1)]
    first = [gn_pack, bf(p['ret_w_in'][0])]
    second = [bf(p['ret_w_out'][0])]
    token, complete_first = _two_level_gather("first_gather", first, _place_own(first))
    token2, complete_second = _two_level_gather("second_gather", second, _place_own(second), (token,))
    hn0 = _rms_fwd("mix_norm0", x, p['mix_norm'][0:1], deps=(token2,))
    bufs = _place_own([b for g in later for b in g])
    tables = _rope_tables(x.shape[0], token2[0, 0])
    (pack, wri), token = complete_first((hn0, tables[0], tables[1], *tables[2], *bufs))
    groups, at = [], 0
    for g in later:
        groups.append(list(zip(g, bufs[at:at + len(g)])))
        at += len(g)
    started, token = _ag_start(groups, (token,))

    w = {k: p[k] for k in ('mix_norm', 'mlp_norm', 'ple_norm')}
    w['hn0'] = hn0
    w['tables'] = tables
    w['ret_gn'] = pack[:, :RET_HEADS, :RET_DV // n].transpose(1, 0, 2).reshape(RET_HEADS, RET_DV)
    w['mla_q_a_norm'] = pack[:, RET_HEADS, :MLA_Q_RANK // n].reshape(1, MLA_Q_RANK)
    w['mla_kv_a_norm'] = pack[:, RET_HEADS + 1, :MLA_KV_RANK // n].reshape(1, MLA_KV_RANK)
    w['ret_w_in'] = wri
    w['mla_q_norm'] = _spread_rope(p['mla_q_norm'])
    w['mla_k_norm'] = _spread_rope(p['mla_k_norm'])
    w['deps'] = (token,)

    def fetch(name, after):
        if name == 'ret_out':
            return dict(ret_w_out=complete_second(after)[0][0].reshape(RET_V_W, D_MODEL))
        got = list(_ag_wait("gather_wait_" + name, *started[names.index(name)], after))
        if name == 'mla_in':
            return dict(mla_w_in=_spread_rope(got[0].reshape(D_MODEL, MLA_IN)))
        if name == 'mla_up':
            return dict(mla_w_uq=_spread_rope(got[0]), mla_w_ukv=got[1])
        if name == 'mla_out':
            return dict(mla_w_out=got[0].reshape(D_MODEL, D_MODEL))
        out = {}
        if name in ('mlp_w1_0', 'layer_1'):
            out['mlp_w1'] = got.pop(0)
        if name in ('mlp_w2_0', 'layer_1'):
            out['mlp_w2'] = got.pop(0)
        if name in ('ple_0', 'layer_1'):
            out['ple_gate_w'] = got[0].reshape(D_MODEL, D_MODEL)
            out['ple_proj_w'] = got[1].transpose(1, 0, 2).reshape(PLE_DIM, D_MODEL)
        return out

    return w, fetch


def _small_grads(small, after):
    rows = [(0, small['mix_norm'][0]), (1, small['mix_norm'][1]), (2, small['mlp_norm'][0]),
            (3, small['mlp_norm'][1]), (4, small['ple_norm'][0]), (5, small['ple_norm'][1]),
            (6, small['ret_gn']), (10, small['mla_q_a_norm']), (11, small['mla_kv_a_norm']),
            (12, small['mla_q_norm']), (13, small['mla_k_norm']), (14, small['sq_err'])]
    gs = _all_reduce_small(rows, after)
    me = _flat(*_my_place())
    n = N_DEV
    return dict(
        sq_err=gs[14, 0],
        mix_norm=gs[0:2], mlp_norm=gs[2:4], ple_norm=gs[4:6],
        ret_gn=lax.dynamic_slice(gs, (6, me * (RET_DV // n)), (RET_HEADS, RET_DV // n)),
        mla_q_a_norm=lax.dynamic_slice(gs, (10, me * (MLA_Q_RANK // n)), (1, MLA_Q_RANK // n)),
        mla_kv_a_norm=lax.dynamic_slice(gs, (11, me * (MLA_KV_RANK // n)), (1, MLA_KV_RANK // n)),
        mla_q_norm=_gather_rope(gs[12:13, :MLA_HD_PAD]), mla_k_norm=_gather_rope(gs[13:14, :MLA_HD_PAD]))


def kernel(x, p, mix_norm, ret_w_in, ret_gn, ret_w_out, mla_w_in, mla_q_a_norm, mla_kv_a_norm, mla_w_uq, mla_w_ukv, mla_q_norm, mla_k_norm, mla_w_out, mlp_norm, mlp_w1, mlp_w2, ple_norm, ple_gate_w, ple_proj_w, loss_target, m_mix_norm, m_ret_w_in, m_ret_gn, m_ret_w_out, m_mla_w_in, m_mla_q_a_norm, m_mla_kv_a_norm, m_mla_w_uq, m_mla_w_ukv, m_mla_q_norm, m_mla_k_norm, m_mla_w_out, m_mlp_norm, m_mlp_w1, m_mlp_w2, m_ple_norm, m_ple_gate_w, m_ple_proj_w, v_mix_norm, v_ret_w_in, v_ret_gn, v_ret_w_out, v_mla_w_in, v_mla_q_a_norm, v_mla_kv_a_norm, v_mla_w_uq, v_mla_w_ukv, v_mla_q_norm, v_mla_k_norm, v_mla_w_out, v_mlp_norm, v_mlp_w1, v_mlp_w2, v_ple_norm, v_ple_gate_w, v_ple_proj_w):
    given = dict(locals())
    params = {n: given[n] for n in WEIGHTS}
    w, fetch = _prepare_weights(params, x[0])

    started = []

    def emit(group):
        keys = list(group)
        send, recv, srcs, lands, token = _rs_start(f"rs_start{len(started)}", [group[k] for k in keys])
        started.append((keys, send, recv, srcs, lands))
        return (token,)

    sq_err, grad_x, _, small = _local_step(x[0], p, loss_target[0], w, fetch, emit)
    small['sq_err'] = sq_err[0:1]

    grads, deltas, new_m, new_v = {}, {}, {}, {}
    total = {}

    def small_updates(after):
        sg = _small_grads(small, after)
        total['loss'] = 0.5 / D_MODEL * sg['sq_err']
        two_d = lambda a: a.reshape(-1, a.shape[-1])
        d_s, m_s, v_s = _adamw_small(
            [two_d(params[n]) for n in SMALL], [sg[n] for n in SMALL],
            [two_d(given["m_" + n]) for n in SMALL], [two_d(given["v_" + n]) for n in SMALL])
        for i, n in enumerate(SMALL):
            shape = params[n].shape
            grads[n], deltas[n], new_m[n], new_v[n] = (a.reshape(shape) for a in (sg[n], d_s[i], m_s[i], v_s[i]))
        return (d_s[0],)

    me = _flat(*_my_place()).astype(jnp.int32).reshape(1)
    after = (grad_x,)
    src_of, land_of = {}, {}
    for gi, (keys, send, recv, srcs, lands) in enumerate(started):
        if gi == len(started) - 1:
            after = small_updates(after)
        srcs, lands = _rs_wait(f"rs_wait{gi}", send, recv, srcs, lands, after)
        for k, s, l in zip(keys, srcs, lands):
            src_of[k], land_of[k] = s, l
        done = [n for n in BIG if n not in grads and all((n, l) in src_of for l in range(params[n].shape[0]))]
        for n in done:
            layers = range(params[n].shape[0])
            grads[n], deltas[n], new_m[n], new_v[n] = _adamw_big(
                "adamw_" + n, params[n], given["m_" + n], given["v_" + n],
                [src_of[(n, l)] for l in layers], [land_of[(n, l)] for l in layers], me)
        if done:
            after = tuple(deltas[n] for n in done)

    return (total['loss'], grad_x[None], *[grads[n] for n in WEIGHTS], *[deltas[n] for n in WEIGHTS],
            *[new_m[n] for n in WEIGHTS], *[new_v[n] for n in WEIGHTS])
```

```python
import functools

import jax
import jax.numpy as jnp
from jax import lax
from jax.experimental import pallas as pl
from jax.experimental.pallas import tpu as pltpu

F32 = jnp.float32
BF16 = jnp.bfloat16
MESH = pl.DeviceIdType.MESH
ANY = pl.BlockSpec(memory_space=pl.ANY)

N_DEV = 8
D_MODEL = 1024
CHUNK = 64
RET_BLOCK = 4 * CHUNK
EPS = 1e-6
ROPE_THETA = 10000.0
RET_HEADS = 4
RET_DK = 256
RET_DV = 512
RET_QK_W = RET_HEADS * RET_DK
RET_V_W = RET_HEADS * RET_DV
RET_IN = 2 * RET_QK_W + 2 * RET_V_W
MLA_HEADS = 8
MLA_NOPE = 128
MLA_ROPE = 64
MLA_QKD = MLA_NOPE + MLA_ROPE
MLA_VD = 128
MLA_Q_RANK = 384
MLA_KV_RANK = 256
MLA_IN = MLA_Q_RANK + MLA_KV_RANK + MLA_ROPE
MLA_IN_PAD = 768
MLA_HD_PAD = 256
D_FF = 4096
PLE_DIM = 256
ATT_SCALE = MLA_QKD ** -0.5
LOG2E = 1.4426950408889634
ATT_EXP2 = ATT_SCALE * LOG2E

ADAM_LR = 0.001
ADAM_B1 = 0.9
ADAM_B2 = 0.999
ADAM_EPS = 1e-08
ADAM_WD = 0.01
ADAM_STEP = 10

VMEM_LIMIT = 52 * 1024 * 1024
ROW_TILE = 1024
RET_ROWS = 512
ATT_BLOCK = 256
ATT_QROWS = 1024
ATT_KROWS = 1024
ATT_HEADS = 2
PREP_ROWS = 2048

WEIGHTS = ['mix_norm', 'ret_w_in', 'ret_gn', 'ret_w_out', 'mla_w_in', 'mla_q_a_norm', 'mla_kv_a_norm',
           'mla_w_uq', 'mla_w_ukv', 'mla_q_norm', 'mla_k_norm', 'mla_w_out', 'mlp_norm', 'mlp_w1', 'mlp_w2',
           'ple_norm', 'ple_gate_w', 'ple_proj_w']
BIG = ['ret_w_in', 'ret_w_out', 'mla_w_in', 'mla_w_uq', 'mla_w_ukv', 'mla_w_out', 'mlp_w1', 'mlp_w2',
       'ple_gate_w', 'ple_proj_w']
SMALL = [w for w in WEIGHTS if w not in BIG]


def _cparams(sem=None):
    return pltpu.CompilerParams(dimension_semantics=sem, vmem_limit_bytes=VMEM_LIMIT)


def _dot(a, b, ca, cb):
    return lax.dot_general(a, b, (((ca,), (cb,)), ((), ())), preferred_element_type=F32)


def _bf(v):
    return v if v.dtype == BF16 else v.astype(BF16)


def _sigmoid(z):
    return 1.0 / (1.0 + jnp.exp(-z))


def _mm(name, grid, a, a_spec, b, b_spec, contract, outs, extras=(), epi=None, deps=(), split=None):
    nk = grid[2]
    n_ex, n_out, n_dep = len(extras), len(outs), len(deps)
    acc_shape = tuple(d for d in outs[0][1].block_shape if d is not None)
    if split is not None:
        acc_shape = (acc_shape[1], acc_shape[0] * split)

    def body(*refs):
        a_ref, b_ref = refs[:2]
        ex_refs = refs[2:2 + n_ex]
        out_refs = refs[2 + n_ex + n_dep:2 + n_ex + n_dep + n_out]

        def product():
            return _dot(_bf(a_ref[...]), _bf(b_ref[...]), contract[0], contract[1])

        def finish(acc):
            if split is not None:
                for j in range(acc_shape[1] // split):
                    out_refs[0][j] = acc[:, j * split:(j + 1) * split].astype(out_refs[0].dtype)
                return
            acc = acc[...]
            res = epi(acc, *[r[...] for r in ex_refs]) if epi is not None else (acc,)
            for o, r in zip(out_refs, res):
                o[...] = r.astype(o.dtype)

        if nk == 1:
            finish(product())
        else:
            acc_ref = refs[-1]
            k = pl.program_id(2)

            @pl.when(k == 0)
            def _():
                acc_ref[...] = jnp.zeros_like(acc_ref)

            acc_ref[...] += product()

            @pl.when(k == nk - 1)
            def _():
                finish(acc_ref)

    return pl.pallas_call(
        body, name=name, grid=grid,
        in_specs=[a_spec, b_spec] + [s for _, s in extras] + [ANY] * n_dep,
        out_specs=[s for _, s in outs],
        out_shape=[s for s, _ in outs],
        scratch_shapes=[pltpu.VMEM(acc_shape, F32)] if nk > 1 else [],
        compiler_params=_cparams(("parallel", "parallel", "arbitrary")),
    )(a, b, *[x for x, _ in extras], *deps)


def _mm_rows(name, tm, a, w, mode, outs, extras=(), epi=None, deps=()):
    n_sh, rows, cols = w.shape
    n_ex, n_out, n_dep = len(extras), len(outs), len(deps)
    by_cols = mode in ('nn_cols', 'nt_rows')
    width = cols if mode == 'nn_cols' else rows

    def body(*refs):
        a_ref, w_ref = refs[:2]
        ex_refs = refs[2:2 + n_ex]
        out_refs = refs[2 + n_ex + n_dep:2 + n_ex + n_dep + n_out]
        if by_cols:
            av = _bf(a_ref[...])
            for s in range(n_sh):
                cs = slice(s * width, (s + 1) * width)
                acc = _dot(av, w_ref[s], 1, 0 if mode == 'nn_cols' else 1)
                res = epi(acc, *[r[:, cs] for r in ex_refs]) if epi is not None else (acc,)
                for o, r in zip(out_refs, res):
                    o[:, cs] = r.astype(o.dtype)
        else:
            chunk = rows if mode == 'nn_rows' else cols
            acc = None
            for s in range(n_sh):
                part = _dot(_bf(a_ref[:, s * chunk:(s + 1) * chunk]), w_ref[s], 1, 0 if mode == 'nn_rows' else 1)
                acc = part if acc is None else acc + part
            res = epi(acc, *[r[...] for r in ex_refs]) if epi is not None else (acc,)
            for o, r in zip(out_refs, res):
                o[...] = r.astype(o.dtype)

    t, ka = a.shape
    return pl.pallas_call(
        body, name=name, grid=(t // tm, 1, 1),
        in_specs=[pl.BlockSpec((tm, ka), lambda i, j, k: (i, 0)),
                  pl.BlockSpec((n_sh, rows, cols), lambda i, j, k: (0, 0, 0))] + [s for _, s in extras] + [ANY] * n_dep,
        out_specs=[s for _, s in outs],
        out_shape=[s for s, _ in outs],
        compiler_params=_cparams(("parallel", "arbitrary", "arbitrary")),
    )(a, w, *[x for x, _ in extras], *deps)


def _sds(shape, dtype):
    return jax.ShapeDtypeStruct(shape, dtype)


def _row_tile(t, cap=ROW_TILE):
    return min(cap, t)


def _rms_fwd(name, x, g, deps=()):
    t, d = x.shape
    tm = _row_tile(t)

    def body(x_ref, g_ref, *rest):
        o_ref = rest[-1]
        xv = x_ref[...]
        r = lax.rsqrt(jnp.mean(xv * xv, axis=-1, keepdims=True) + EPS)
        o_ref[...] = (xv * r * g_ref[...]).astype(o_ref.dtype)

    return pl.pallas_call(
        body, name=name, grid=(t // tm,),
        in_specs=[pl.BlockSpec((tm, d), lambda i: (i, 0)), pl.BlockSpec((1, d), lambda i: (0, 0))] + [ANY] * len(deps),
        out_specs=pl.BlockSpec((tm, d), lambda i: (i, 0)),
        out_shape=_sds((t, d), BF16),
        compiler_params=_cparams(("parallel",)),
    )(x, g, *deps)


def _rms_bwd_rows(dy, xv, g, n):
    r = lax.rsqrt(jnp.sum(xv * xv, axis=-1, keepdims=True) / n + EPS)
    xh = xv * r
    dxh = dy * g
    dx = r * (dxh - xh * (jnp.sum(dxh * xh, axis=-1, keepdims=True) / n))
    return dx, dy * xh


def _ple_gate_bwd(name, dh, gate, e):
    t, d = dh.shape
    tm = _row_tile(t)

    def body(dh_ref, g_ref, e_ref, de_ref, dz_ref):
        dh_v, gt = dh_ref[...], g_ref[...].astype(F32)
        de_ref[...] = (dh_v * gt).astype(BF16)
        dz_ref[...] = (dh_v * e_ref[...].astype(F32) * (gt * (1.0 - gt))).astype(BF16)

    row = pl.BlockSpec((tm, d), lambda i: (i, 0))
    return pl.pallas_call(
        body, name=name, grid=(t // tm,), in_specs=[row, row, row], out_specs=[row, row],
        out_shape=[_sds((t, d), BF16), _sds((t, d), BF16)],
        compiler_params=_cparams(("parallel",)),
    )(dh, gate, e)


def _rope_half(v, cos, sin):
    half = v.shape[-1] // 2
    v1, v2 = v[:, :half], v[:, half:]
    return jnp.concatenate([v1 * cos - v2 * sin, v2 * cos + v1 * sin], axis=-1)


def _ret_consts():
    lg = jnp.log(1.0 - 2.0 ** (-5.0 - jnp.arange(RET_HEADS, dtype=F32)))
    idx = jnp.arange(RET_BLOCK, dtype=F32)
    chunk = jnp.floor(idx / CHUNK)
    dist = idx[:, None] - idx[None, :]
    same = chunk[:, None] == chunk[None, :]
    seen = jnp.where(same, jnp.abs(dist), jnp.where(chunk[None, :] < chunk[:, None], dist, jnp.inf))
    intra = jnp.exp(lg[:, None, None] * seen)
    qdec = jnp.exp(lg[:, None] * (idx + 1.0))
    kdec = jnp.exp(lg[:, None] * (RET_BLOCK - 1.0 - idx))
    cdec = jnp.exp(lg * RET_BLOCK)
    qdec = jnp.broadcast_to(qdec[:, :, None], (RET_HEADS, RET_BLOCK, RET_DK))
    kdec = jnp.broadcast_to(kdec[:, :, None], (RET_HEADS, RET_BLOCK, RET_DK))
    cdec = jnp.broadcast_to(cdec[:, None, None], (RET_HEADS, 1, RET_DV))
    return intra, qdec, kdec, cdec


def _ret_specs(rb, rev_nb=None):
    blk = (lambda i: i) if rev_nb is None else (lambda i: rev_nb - 1 - i)
    full = lambda shape: pl.BlockSpec(shape, lambda i: (0,) * len(shape))
    return dict(
        proj=pl.BlockSpec((rb, RET_IN), lambda i: (blk(i), 0)),
        tab=pl.BlockSpec((rb, RET_DK // 2), lambda i: (blk(i), 0)),
        vw=pl.BlockSpec((rb, RET_V_W), lambda i: (blk(i), 0)),
        st=pl.BlockSpec((rb // RET_BLOCK, RET_HEADS, RET_DK, RET_DV), lambda i: (blk(i), 0, 0, 0)),
        gn=full((RET_HEADS, 1, RET_DV)),
        intra=full((RET_HEADS, RET_BLOCK, RET_BLOCK)),
        dec=full((RET_HEADS, RET_BLOCK, RET_DK)),
        cdec=full((RET_HEADS, 1, RET_DV)),
    )


def _ret_fwd(proj, cos, sin, gn):
    t = proj.shape[0]
    rb = min(RET_ROWS, t)
    cpb = rb // RET_BLOCK
    intra, qdec, kdec, cdec = _ret_consts()
    sp = _ret_specs(rb)

    def body(proj_ref, cos_ref, sin_ref, gn_ref, intra_ref, qd_ref, kd_ref, cd_ref,
             gated_ref, outp_ref, st_ref, s_ref):
        @pl.when(pl.program_id(0) == 0)
        def _():
            s_ref[...] = jnp.zeros_like(s_ref)

        def chunk(c, carry):
            rows = pl.ds(pl.multiple_of(c * RET_BLOCK, RET_BLOCK), RET_BLOCK)
            cs, sn = cos_ref[rows, :], sin_ref[rows, :]
            for h in range(RET_HEADS):
                q = proj_ref[rows, h * RET_DK:(h + 1) * RET_DK].astype(F32)
                k = proj_ref[rows, RET_QK_W + h * RET_DK:RET_QK_W + (h + 1) * RET_DK].astype(F32)
                v = proj_ref[rows, 2 * RET_QK_W + h * RET_DV:2 * RET_QK_W + (h + 1) * RET_DV]
                g = proj_ref[rows, 2 * RET_QK_W + RET_V_W + h * RET_DV:
                             2 * RET_QK_W + RET_V_W + (h + 1) * RET_DV].astype(F32)
                qr = _rope_half(q, cs, sn)
                kr = _rope_half(k, cs, sn) * (RET_DK ** -0.5)
                qb, kb, vb = qr.astype(BF16), kr.astype(BF16), v
                sc = _dot(qb, kb, 1, 1) * intra_ref[h]
                inner = _dot(sc.astype(BF16), vb, 1, 0)
                s_old = s_ref[h]
                sb = s_old.astype(BF16)
                st_ref[c, h] = sb
                cross = _dot((qr * qd_ref[h]).astype(BF16), sb, 1, 0)
                out = inner + cross
                s_ref[h] = s_old * cd_ref[h] + _dot((kr * kd_ref[h]).astype(BF16), vb, 0, 0)
                r = lax.rsqrt(jnp.mean(out * out, axis=-1, keepdims=True) + EPS)
                y = out * r * gn_ref[h]
                cols = slice(h * RET_DV, (h + 1) * RET_DV)
                gated_ref[rows, cols] = (g * _sigmoid(g) * y).astype(BF16)
                outp_ref[rows, cols] = out.astype(BF16)
            return carry

        lax.fori_loop(0, cpb, chunk, 0)

    return pl.pallas_call(
        body, name="ret_fwd", grid=(t // rb,),
        in_specs=[sp['proj'], sp['tab'], sp['tab'], sp['gn'], sp['intra'], sp['dec'], sp['dec'], sp['cdec']],
        out_specs=[sp['vw'], sp['vw'], sp['st']],
        out_shape=[_sds((t, RET_V_W), BF16), _sds((t, RET_V_W), BF16),
                   _sds((t // RET_BLOCK, RET_HEADS, RET_DK, RET_DV), BF16)],
        scratch_shapes=[pltpu.VMEM((RET_HEADS, RET_DK, RET_DV), F32)],
        compiler_params=_cparams(("arbitrary",)),
    )(proj, cos, sin, gn.reshape(RET_HEADS, 1, RET_DV), intra, qdec, kdec, cdec)


def _ret_gate_bwd_epi(dgt, out, g, gn):
    g = g.astype(F32)
    out = out.astype(F32)
    r = lax.rsqrt(jnp.mean(out * out, axis=-1, keepdims=True) + EPS)
    xh = out * r
    sg = _sigmoid(g)
    dgate = dgt * (xh * gn) * (sg * (1.0 + g * (1.0 - sg)))
    dy = dgt * (g * sg)
    dxh = dy * gn
    dout = r * (dxh - xh * jnp.mean(dxh * xh, axis=-1, keepdims=True))
    return dout, dgate, jnp.sum(dy * xh, axis=0, keepdims=True)


def _ret_bwd(proj, cos, sin, states, dout, dgate, deps=()):
    t = proj.shape[0]
    rb = min(RET_ROWS, t)
    cpb = rb // RET_BLOCK
    nb = t // rb
    intra, qdec, kdec, cdec = _ret_consts()
    sp = _ret_specs(rb, rev_nb=nb)

    def body(proj_ref, cos_ref, sin_ref, intra_ref, qd_ref, kd_ref, cd_ref, st_ref, dout_ref, dgate_ref, *rest):
        dproj_ref, ds_ref = rest[len(deps):]

        @pl.when(pl.program_id(0) == 0)
        def _():
            ds_ref[...] = jnp.zeros_like(ds_ref)

        def chunk(cc, carry):
            c = cpb - 1 - cc
            rows = pl.ds(pl.multiple_of(c * RET_BLOCK, RET_BLOCK), RET_BLOCK)
            cs, sn = cos_ref[rows, :], sin_ref[rows, :]
            for h in range(RET_HEADS):
                q = proj_ref[rows, h * RET_DK:(h + 1) * RET_DK].astype(F32)
                k = proj_ref[rows, RET_QK_W + h * RET_DK:RET_QK_W + (h + 1) * RET_DK].astype(F32)
                v = proj_ref[rows, 2 * RET_QK_W + h * RET_DV:2 * RET_QK_W + (h + 1) * RET_DV]
                cols = slice(h * RET_DV, (h + 1) * RET_DV)
                qr = _rope_half(q, cs, sn)
                kr = _rope_half(k, cs, sn) * (RET_DK ** -0.5)
                qb, kb, vb = qr.astype(BF16), kr.astype(BF16), v
                qdb = (qr * qd_ref[h]).astype(BF16)
                kdb = (kr * kd_ref[h]).astype(BF16)
                doutb = dout_ref[rows, cols]
                itr = intra_ref[h]
                pb = (_dot(qb, kb, 1, 1) * itr).astype(BF16)
                dv = _dot(pb, doutb, 0, 0)
                dsc = (_dot(doutb, vb, 1, 1) * itr).astype(BF16)
                dq = _dot(dsc, kb, 1, 0)
                dk = _dot(dsc, qb, 0, 0)
                dq = dq + _dot(doutb, st_ref[c, h], 1, 1) * qd_ref[h]
                ds_new = ds_ref[h]
                dsb = ds_new.astype(BF16)
                dk = dk + _dot(vb, dsb, 1, 1) * kd_ref[h]
                dv = dv + _dot(kdb, dsb, 1, 0)
                ds_ref[h] = ds_new * cd_ref[h] + _dot(qdb, doutb, 0, 0)
                dproj_ref[rows, h * RET_DK:(h + 1) * RET_DK] = _rope_half(dq, cs, -sn).astype(BF16)
                dproj_ref[rows, RET_QK_W + h * RET_DK:RET_QK_W + (h + 1) * RET_DK] = (
                    _rope_half(dk * (RET_DK ** -0.5), cs, -sn).astype(BF16))
                dproj_ref[rows, 2 * RET_QK_W + h * RET_DV:2 * RET_QK_W + (h + 1) * RET_DV] = dv.astype(BF16)
                dproj_ref[rows, 2 * RET_QK_W + RET_V_W + h * RET_DV:
                          2 * RET_QK_W + RET_V_W + (h + 1) * RET_DV] = dgate_ref[rows, cols]
            return carry

        lax.fori_loop(0, cpb, chunk, 0)

    return pl.pallas_call(
        body, name="ret_bwd", grid=(nb,),
        in_specs=[sp['proj'], sp['tab'], sp['tab'], sp['intra'], sp['dec'], sp['dec'], sp['cdec'],
                  sp['st'], sp['vw'], sp['vw']] + [ANY] * len(deps),
        out_specs=sp['proj'],
        out_shape=_sds((t, RET_IN), BF16),
        scratch_shapes=[pltpu.VMEM((RET_HEADS, RET_DK, RET_DV), F32)],
        compiler_params=_cparams(("arbitrary",)),
    )(proj, cos, sin, intra, qdec, kdec, cdec, states, dout, dgate, *deps)


def _spread_rope(a):
    return jnp.pad(a, [(0, 0)] * (a.ndim - 1) + [(0, MLA_ROPE)])


def _gather_rope(a):
    return a[..., :a.shape[-1] - MLA_ROPE]


def _rope_tables(t, zero):
    pos = jnp.arange(t, dtype=F32)[:, None] + zero
    inv = 1.0 / (ROPE_THETA ** (jnp.arange(0, RET_DK, 2, dtype=F32) / RET_DK))
    ang = pos * inv[None, :]
    return jnp.cos(ang), jnp.sin(ang), _mla_tables(t, pos)


def _mla_tables(t, pos):
    half = MLA_ROPE // 2
    inv = 1.0 / (ROPE_THETA ** (jnp.arange(0, MLA_ROPE, 2, dtype=F32) / MLA_ROPE))
    ang = pos * inv[None, :]
    cos, sin = jnp.cos(ang), jnp.sin(ang)
    z = jnp.zeros((t, half), F32)
    c = jnp.concatenate([cos, cos, z, z], axis=1)
    s1 = jnp.concatenate([-sin, z, z, z], axis=1)
    s2 = jnp.concatenate([z, sin, z, z], axis=1)
    return c, s1, s2


def _rope_tile(r, c, s1, s2):
    return r * c + pltpu.roll(r, 96, 1) * s1 + pltpu.roll(r, 32, 1) * s2


def _mla_mid(proj2, qa, kva):
    t = proj2.shape[0]
    tm = _row_tile(t)

    def body(p_ref, qa_ref, kva_ref, cq_ref, ckv_ref):
        cq = p_ref[:, :MLA_Q_RANK]
        ckv = p_ref[:, MLA_Q_RANK:MLA_Q_RANK + MLA_KV_RANK]
        rq = lax.rsqrt(jnp.mean(cq * cq, axis=-1, keepdims=True) + EPS)
        rkv = lax.rsqrt(jnp.mean(ckv * ckv, axis=-1, keepdims=True) + EPS)
        cq_ref[...] = (cq * rq * qa_ref[...]).astype(BF16)
        ckv_ref[...] = (ckv * rkv * kva_ref[...]).astype(BF16)

    return pl.pallas_call(
        body, name="mla_mid", grid=(t // tm,),
        in_specs=[pl.BlockSpec((tm, MLA_IN_PAD), lambda i: (i, 0)),
                  pl.BlockSpec((1, MLA_Q_RANK), lambda i: (0, 0)),
                  pl.BlockSpec((1, MLA_KV_RANK), lambda i: (0, 0))],
        out_specs=[pl.BlockSpec((tm, MLA_Q_RANK), lambda i: (i, 0)),
                   pl.BlockSpec((tm, MLA_KV_RANK), lambda i: (i, 0))],
        out_shape=[_sds((t, MLA_Q_RANK), BF16), _sds((t, MLA_KV_RANK), BF16)],
        compiler_params=_cparams(("parallel",)),
    )(proj2, qa, kva)


def _mla_mid_bwd(proj2, qa, kva, dcq, dckv, dkr):
    t = proj2.shape[0]
    tm = _row_tile(t)

    def body(p_ref, qa_ref, kva_ref, dcq_ref, dckv_ref, dkr_ref, dp_ref, dqa_ref, dkva_ref):
        @pl.when(pl.program_id(0) == 0)
        def _():
            dqa_ref[...] = jnp.zeros_like(dqa_ref)
            dkva_ref[...] = jnp.zeros_like(dkva_ref)

        dxq, dgq = _rms_bwd_rows(dcq_ref[...], p_ref[:, :MLA_Q_RANK], qa_ref[...], MLA_Q_RANK)
        dxk, dgk = _rms_bwd_rows(dckv_ref[...], p_ref[:, MLA_Q_RANK:MLA_Q_RANK + MLA_KV_RANK], kva_ref[...],
                                 MLA_KV_RANK)
        dp_ref[:, :MLA_Q_RANK] = dxq.astype(BF16)
        dp_ref[:, MLA_Q_RANK:MLA_Q_RANK + MLA_KV_RANK] = dxk.astype(BF16)
        dp_ref[:, MLA_Q_RANK + MLA_KV_RANK:] = dkr_ref[...].astype(BF16)
        dqa_ref[...] += jnp.sum(dgq, axis=0, keepdims=True)
        dkva_ref[...] += jnp.sum(dgk, axis=0, keepdims=True)

    return pl.pallas_call(
        body, name="mla_mid_bwd", grid=(t // tm,),
        in_specs=[pl.BlockSpec((tm, MLA_IN_PAD), lambda i: (i, 0)),
                  pl.BlockSpec((1, MLA_Q_RANK), lambda i: (0, 0)),
                  pl.BlockSpec((1, MLA_KV_RANK), lambda i: (0, 0)),
                  pl.BlockSpec((tm, MLA_Q_RANK), lambda i: (i, 0)),
                  pl.BlockSpec((tm, MLA_KV_RANK), lambda i: (i, 0)),
                  pl.BlockSpec((tm, 128), lambda i: (i, 0))],
        out_specs=[pl.BlockSpec((tm, MLA_IN_PAD), lambda i: (i, 0)),
                   pl.BlockSpec((1, MLA_Q_RANK), lambda i: (0, 0)),
                   pl.BlockSpec((1, MLA_KV_RANK), lambda i: (0, 0))],
        out_shape=[_sds((t, MLA_IN_PAD), BF16), _sds((1, MLA_Q_RANK), F32), _sds((1, MLA_KV_RANK), F32)],
        compiler_params=_cparams(("arbitrary",)),
    )(proj2, qa, kva, dcq, dckv, dkr)


def _mla_prep_specs(t, tm):
    head = lambda w: pl.BlockSpec((None, tm, w), lambda i, h: (h, i, 0))
    return dict(
        head256=head(MLA_HD_PAD), head128=head(MLA_VD),
        cols256=pl.BlockSpec((tm, MLA_HD_PAD), lambda i, h: (i, h)),
        cq=pl.BlockSpec((tm, MLA_Q_RANK), lambda i, h: (i, 0)),
        ckv=pl.BlockSpec((tm, MLA_KV_RANK), lambda i, h: (i, 0)),
        wuq=pl.BlockSpec((None, MLA_Q_RANK, MLA_HD_PAD), lambda i, h: (h, 0, 0)),
        wukv=pl.BlockSpec((None, MLA_KV_RANK, MLA_HD_PAD), lambda i, h: (h, 0, 0)),
        kr=pl.BlockSpec((tm, 128), lambda i, h: (i, (MLA_Q_RANK + MLA_KV_RANK) // 128)),
        gain=pl.BlockSpec((1, MLA_HD_PAD), lambda i, h: (0, 0)),
        tab=pl.BlockSpec((tm, 128), lambda i, h: (i, 0)),
    )


def _mla_prep(cq, ckv, wuq, wukv, proj2, gq, gk, tabs):
    t = cq.shape[0]
    tm = _row_tile(t, PREP_ROWS)
    sp = _mla_prep_specs(t, tm)

    def body(cq_ref, ckv_ref, wuq_ref, wukv_ref, kr_ref, gq_ref, gk_ref, c_ref, s1_ref, s2_ref,
             qh_ref, kh_ref, vh_ref):
        c, s1, s2 = c_ref[...], s1_ref[...], s2_ref[...]

        def norm_rope(xv, gain):
            r = lax.rsqrt(jnp.sum(xv * xv, axis=-1, keepdims=True) / MLA_QKD + EPS)
            y = xv * r * gain
            return jnp.concatenate([y[:, :MLA_NOPE], _rope_tile(y[:, MLA_NOPE:], c, s1, s2)], axis=-1)

        kvv = _dot(ckv_ref[...], wukv_ref[...], 1, 0)
        qh_ref[...] = norm_rope(_dot(cq_ref[...], wuq_ref[...], 1, 0), gq_ref[...]).astype(BF16)
        kf = jnp.concatenate([kvv[:, :MLA_NOPE], kr_ref[...]], axis=-1)
        kh_ref[...] = norm_rope(kf, gk_ref[...]).astype(BF16)
        vh_ref[...] = jnp.concatenate([kvv[:, MLA_NOPE:], jnp.ones((tm, MLA_VD), F32)], axis=-1).astype(BF16)

    return pl.pallas_call(
        body, name="mla_prep", grid=(t // tm, MLA_HEADS),
        in_specs=[sp['cq'], sp['ckv'], sp['wuq'], sp['wukv'], sp['kr'], sp['gain'], sp['gain'],
                  sp['tab'], sp['tab'], sp['tab']],
        out_specs=[sp['head256'], sp['head256'], sp['head256']],
        out_shape=[_sds((MLA_HEADS, t, MLA_HD_PAD), BF16), _sds((MLA_HEADS, t, MLA_HD_PAD), BF16),
                   _sds((MLA_HEADS, t, 2 * MLA_VD), BF16)],
        compiler_params=_cparams(("parallel", "arbitrary")),
    )(cq, ckv, wuq, wukv, proj2, gq, gk, *tabs)


def _mla_prep_bwd(cq, ckv, wuq, wukv, proj2, gq, gk, tabs, dqt, dkh, dvh):
    t = cq.shape[0]
    tm = _row_tile(t, PREP_ROWS)
    ab = dqt.shape[-1]
    sp = _mla_prep_specs(t, tm)

    def body(cq_ref, ckv_ref, wuq_ref, wukv_ref, kr_ref, gq_ref, gk_ref, c_ref, s1_ref, s2_ref,
             dqt_ref, dkh_ref, dvh_ref, dq_ref, dkv_ref, dkr_ref, dgq_ref, dgk_ref):
        dqh = jnp.concatenate([dqt_ref[b].T for b in range(tm // ab)], axis=0)
        i, h = pl.program_id(0), pl.program_id(1)

        @pl.when((i == 0) & (h == 0))
        def _():
            dgq_ref[...] = jnp.zeros_like(dgq_ref)
            dgk_ref[...] = jnp.zeros_like(dgk_ref)

        @pl.when(h == 0)
        def _():
            dkr_ref[...] = jnp.zeros_like(dkr_ref)

        c, s1, s2 = c_ref[...], s1_ref[...], s2_ref[...]

        def back(xv, gain, dout):
            dy = jnp.concatenate([dout[:, :MLA_NOPE], _rope_tile(dout[:, MLA_NOPE:], c, -s1, -s2)], axis=-1)
            return _rms_bwd_rows(dy, xv, gain, MLA_QKD)

        kvv = _dot(ckv_ref[...], wukv_ref[...], 1, 0)
        dxq, dgq = back(_dot(cq_ref[...], wuq_ref[...], 1, 0), gq_ref[...], dqh)
        kf = jnp.concatenate([kvv[:, :MLA_NOPE], kr_ref[...]], axis=-1)
        dxk, dgk = back(kf, gk_ref[...], dkh_ref[...])
        dq_ref[...] = dxq.astype(BF16)
        dkv_ref[...] = jnp.concatenate([dxk[:, :MLA_NOPE], dvh_ref[...]], axis=-1).astype(BF16)
        dkr_ref[...] += dxk[:, MLA_NOPE:]
        dgq_ref[...] += jnp.sum(dgq, axis=0, keepdims=True)
        dgk_ref[...] += jnp.sum(dgk, axis=0, keepdims=True)

    return pl.pallas_call(
        body, name="mla_prep_bwd", grid=(t // tm, MLA_HEADS),
        in_specs=[sp['cq'], sp['ckv'], sp['wuq'], sp['wukv'], sp['kr'], sp['gain'], sp['gain'],
                  sp['tab'], sp['tab'], sp['tab'],
                  pl.BlockSpec((None, tm // ab, MLA_HD_PAD, ab), lambda i, h: (h, i, 0, 0)),
                  sp['head256'], sp['head128']],
        out_specs=[sp['cols256'], sp['cols256'], sp['tab'], sp['gain'], sp['gain']],
        out_shape=[_sds((t, MLA_HEADS * MLA_HD_PAD), BF16), _sds((t, MLA_HEADS * MLA_HD_PAD), BF16),
                   _sds((t, 128), F32), _sds((1, MLA_HD_PAD), F32), _sds((1, MLA_HD_PAD), F32)],
        compiler_params=_cparams(("arbitrary", "arbitrary")),
    )(cq, ckv, wuq, wukv, proj2, gq, gk, *tabs, dqt, dkh, dvh)


def _chunk_visible(rows, cols, row_off, col_off):
    rq = lax.shift_right_logical(lax.broadcasted_iota(jnp.int32, (rows, cols), 0) + row_off, 6)
    ck = lax.shift_right_logical(lax.broadcasted_iota(jnp.int32, (rows, cols), 1) + col_off, 6)
    return ck <= rq


def _rows_to_lanes(col):
    return col.T[:8, :]


def _attn_fwd(qh, kh, vh):
    t = qh.shape[1]
    ab = min(ATT_BLOCK, t)
    tq = min(ATT_QROWS, t)
    r = tq // ab
    hg = ATT_HEADS

    def body(q_ref, k_ref, v_ref, o_ref, lse_ref, acc_ref):
        n_un = pl.program_id(1) * r
        acc_ref[...] = jnp.zeros_like(acc_ref)

        def step(b, ms, diag):
            rows = pl.ds(pl.multiple_of(b * ab, ab), ab)
            out = []
            for hh in range(hg):
                m = ms[hh]
                s = _dot(q_ref[hh], k_ref[hh, rows, :], 1, 1)
                if diag is not None:
                    s = jnp.where(_chunk_visible(tq, ab, 0, diag * ab), s, -1e30)
                m_new = jnp.maximum(m, jnp.max(s, axis=-1, keepdims=True))
                p = jnp.exp2((s - m_new) * ATT_EXP2).astype(BF16)
                acc_ref[hh] = jnp.exp2((m - m_new) * ATT_EXP2) * acc_ref[hh] + _dot(p, v_ref[hh, rows, :], 1, 0)
                out.append(m_new)
            return tuple(out)

        ms = tuple(jnp.full((tq, 1), -1e30, F32) for _ in range(hg))
        ms = lax.fori_loop(0, n_un, lambda b, st: step(b, st, None), ms)
        for d in range(r):
            ms = step(n_un + d, ms, d)
        for hh in range(hg):
            l = acc_ref[hh, :, MLA_VD:]
            o_ref[:, hh * MLA_VD:(hh + 1) * MLA_VD] = acc_ref[hh, :, :MLA_VD] / l
            lse_t = _rows_to_lanes(ms[hh] * ATT_EXP2 + jnp.log(l) * LOG2E)
            for d in range(r):
                lse_ref[hh, d] = lse_t[:, d * ab:(d + 1) * ab]

    return pl.pallas_call(
        body, name="mla_attn", grid=(MLA_HEADS // hg, t // tq),
        in_specs=[pl.BlockSpec((hg, tq, MLA_HD_PAD), lambda g, i: (g, i, 0)),
                  pl.BlockSpec((hg, t, MLA_HD_PAD), lambda g, i: (g, 0, 0)),
                  pl.BlockSpec((hg, t, 2 * MLA_VD), lambda g, i: (g, 0, 0))],
        out_specs=[pl.BlockSpec((tq, hg * MLA_VD), lambda g, i: (i, g)),
                   pl.BlockSpec((hg, r, 8, ab), lambda g, i: (g, i, 0, 0))],
        out_shape=[_sds((t, MLA_HEADS * MLA_VD), F32), _sds((MLA_HEADS, t // ab, 8, ab), F32)],
        scratch_shapes=[pltpu.VMEM((hg, tq, 2 * MLA_VD), F32)],
        compiler_params=_cparams(("parallel", "arbitrary")),
    )(qh, kh, vh)


def _attn_bwd(qh, kh, vh, dob, o, lse_t):
    t = qh.shape[1]
    ab = min(ATT_BLOCK, t)
    kb = min(ATT_KROWS, t)
    r = kb // ab
    nq = t // ab
    hg = ATT_HEADS

    def body(q_ref, k_ref, v_ref, do_ref, o_ref, lse_ref, dqt_ref, dk_ref, dv_ref, dl_ref):
        j = pl.program_id(1)

        @pl.when(j == 0)
        def _():
            dqt_ref[...] = jnp.zeros_like(dqt_ref)
            ones = jnp.ones((8, MLA_VD), F32)

            def delta(b, carry):
                rows = pl.ds(pl.multiple_of(b * ab, ab), ab)
                for hh in range(hg):
                    cols = slice(hh * MLA_VD, (hh + 1) * MLA_VD)
                    prod = do_ref[rows, cols].astype(F32) * o_ref[rows, cols]
                    dl_ref[hh, b] = lax.dot_general(ones, prod, (((1,), (1,)), ((), ())),
                                                    precision=lax.Precision.HIGHEST, preferred_element_type=F32)
                return carry

            lax.fori_loop(0, nq, delta, 0)

        ks = [k_ref[hh] for hh in range(hg)]
        vs = [v_ref[hh, :, :MLA_VD] for hh in range(hg)]
        kts = [k.T for k in ks]

        dk_ref[...] = jnp.zeros_like(dk_ref)
        dv_ref[...] = jnp.zeros_like(dv_ref)

        def step(b, carry, diag):
            rows = pl.ds(pl.multiple_of(b * ab, ab), ab)
            hi = kb if diag is None else (diag + 1) * ab
            for hh in range(hg):
                q = q_ref[hh, rows, :]
                do = do_ref[rows, hh * MLA_VD:(hh + 1) * MLA_VD]
                s_t = _dot(ks[hh][:hi], q, 1, 1)
                if diag is not None:
                    key_chunk = lax.shift_right_logical(lax.broadcasted_iota(jnp.int32, (hi, ab), 0), 6)
                    query_chunk = lax.shift_right_logical(
                        lax.broadcasted_iota(jnp.int32, (hi, ab), 1) + diag * ab, 6)
                    s_t = jnp.where(key_chunk <= query_chunk, s_t, -1e30)
                p_t = jnp.exp2(s_t * ATT_EXP2 - lse_ref[hh, b][0:1, :])
                dp_t = _dot(vs[hh][:hi], do, 1, 1)
                ds_t = (p_t * (dp_t - dl_ref[hh, b][0:1, :]) * ATT_SCALE).astype(BF16)
                dqt_ref[hh, b] += _dot(kts[hh][:, :hi], ds_t, 1, 0)
                dk_ref[hh, :hi] += _dot(ds_t, q, 1, 0)
                dv_ref[hh, :hi] += _dot(p_t.astype(BF16), do, 1, 0)
            return carry

        for d in range(r):
            step(j * r + d, 0, d)
        lax.fori_loop((j + 1) * r, nq, lambda b, c: step(b, c, None), 0)

    whole = lambda w: pl.BlockSpec((hg, t, w), lambda g, j: (g, 0, 0))
    blk = lambda w: pl.BlockSpec((hg, kb, w), lambda g, j: (g, j, 0))
    stat = pl.BlockSpec((hg, nq, 8, ab), lambda g, j: (g, 0, 0, 0))
    cols = pl.BlockSpec((t, hg * MLA_VD), lambda g, j: (0, g))
    return pl.pallas_call(
        body, name="mla_attn_bwd", grid=(MLA_HEADS // hg, t // kb),
        in_specs=[whole(MLA_HD_PAD), blk(MLA_HD_PAD), blk(2 * MLA_VD),
                  cols, cols, stat],
        out_specs=[pl.BlockSpec((hg, nq, MLA_HD_PAD, ab), lambda g, j: (g, 0, 0, 0)), blk(MLA_HD_PAD), blk(MLA_VD)],
        out_shape=[_sds((MLA_HEADS, nq, MLA_HD_PAD, ab), F32), _sds((MLA_HEADS, t, MLA_HD_PAD), F32),
                   _sds((MLA_HEADS, t, MLA_VD), F32)],
        scratch_shapes=[pltpu.VMEM((hg, nq, 8, ab), F32)],
        compiler_params=_cparams(("parallel", "arbitrary")),
    )(qh, kh, vh, dob, o, lse_t)


VEC = pl.BlockSpec((1, D_MODEL), lambda i, j, k: (0, 0))


def _rows(tm, width):
    return pl.BlockSpec((tm, width), lambda i, j, k: (i, 0))


def _residual_epi(next_gain):
    if next_gain is None:
        return [], lambda acc, hv: (acc + hv,)

    def epi(acc, hv, g):
        h_new = acc + hv
        r = lax.rsqrt(jnp.mean(h_new * h_new, axis=-1, keepdims=True) + EPS)
        return h_new, h_new * r * g

    return [(next_gain, VEC)], epi


def _residual_outs(t, row, next_gain):
    outs = [(_sds((t, D_MODEL), F32), row)]
    return outs + ([(_sds((t, D_MODEL), BF16), row)] if next_gain is not None else [])


def _mlp_fwd(l, h, hn, w1g, fetch_w2, next_gain):
    t = h.shape[0]
    tm = _row_tile(t, 512)

    def relu2(acc):
        r = jnp.maximum(acc, 0.0)
        return (r * r,)

    (u,) = _mm_rows(f"mlp_up{l}", tm, hn, w1g, 'nn_cols', [(_sds((t, D_FF), BF16), _rows(tm, D_FF))], epi=relu2)
    w2g = fetch_w2((u,))
    row = _rows(tm, D_MODEL)
    more, epi = _residual_epi(next_gain)
    h2, hn_next = _mm_rows(f"mlp_down{l}", tm, u, w2g, 'nn_rows', _residual_outs(t, row, next_gain),
                           extras=[(h, row)] + more, epi=epi)
    return h2, hn_next, (h, hn, u, w1g, w2g)


def _norm_bwd_outs(t, tm):
    return [(_sds((t, D_MODEL), F32), pl.BlockSpec((tm, D_MODEL), lambda i, j, k: (i, 0))),
            (_sds((t // tm, 1, D_MODEL), F32), pl.BlockSpec((None, 1, D_MODEL), lambda i, j, k: (i, 0, 0)))]


def _norm_bwd_epi(acc, xv, res, g):
    dx, dgr = _rms_bwd_rows(acc, xv, g, D_MODEL)
    return res + dx, jnp.sum(dgr, axis=0, keepdims=True)


def _mlp_bwd(l, dh, saved, norm_g, emit_w2=None, emit_w1=None):
    h, hn, u, w1g, w2g = saved
    t = h.shape[0]
    tm = _row_tile(t, 512)
    nsh, _, wsh = w1g.shape
    wide = _rows(tm, D_FF)
    (da,) = _mm_rows(f"mlp_du{l}", tm, dh, w2g, 'nt_rows', [(_sds((t, D_FF), BF16), wide)], extras=[(u, wide)],
                     epi=lambda acc, uv: (2.0 * jnp.sqrt(uv.astype(F32)) * acc,))
    tw = _row_tile(t, 512)
    (dw2,) = _mm(f"mlp_dw2{l}", (1, 1, t // tw),
                 u, pl.BlockSpec((tw, D_FF), lambda i, j, k: (k, 0)),
                 dh, pl.BlockSpec((tw, D_MODEL), lambda i, j, k: (k, 0)), (0, 0),
                 [(_sds((D_FF, D_MODEL), BF16), pl.BlockSpec((D_FF, D_MODEL), lambda i, j, k: (0, 0)))])
    dw2 = dw2.reshape(nsh, wsh, D_MODEL)
    (dw1,) = _mm(f"mlp_dw1{l}", (1, 1, t // tw),
                 hn, pl.BlockSpec((tw, D_MODEL), lambda i, j, k: (k, 0)),
                 da, pl.BlockSpec((tw, D_FF), lambda i, j, k: (k, 0)), (0, 0),
                 [(_sds((nsh, D_MODEL, wsh), BF16), pl.BlockSpec((nsh, D_MODEL, wsh), lambda i, j, k: (0, 0, 0)))],
                 split=wsh, deps=emit_w2(dw2) if emit_w2 else ())
    row = _rows(tm, D_MODEL)
    dh_in, dg = _mm_rows(f"mlp_dhn{l}", tm, da, w1g, 'nt_cols', _norm_bwd_outs(t, tm),
                         extras=[(h, row), (dh, row), (norm_g, VEC)], epi=_norm_bwd_epi,
                         deps=emit_w1(dw1) if emit_w1 else ())
    return dh_in, jnp.sum(dg, axis=0), dw1, dw2


def _ple_fwd(l, h, hn, p, wg, wp, next_gain, target=None):
    t = h.shape[0]
    tm = _row_tile(t, 512)
    row = pl.BlockSpec((tm, D_MODEL), lambda i, j, k: (i, 0))
    full = lambda r: pl.BlockSpec((r, D_MODEL), lambda i, j, k: (0, 0))
    f32_row, bf_row = (_sds((t, D_MODEL), F32), row), (_sds((t, D_MODEL), BF16), row)
    common = [(h, row), (p, pl.BlockSpec((None, None, tm, PLE_DIM), lambda i, j, k: (l, 0, i, 0))),
              (wp, full(PLE_DIM))]
    if target is not None:
        def loss_epi(acc, hv, pv, wpv, tv):
            gt = _sigmoid(acc)
            ev = _dot(_bf(pv), wpv, 1, 0)
            err = hv + gt * ev - tv
            sq = jnp.sum(jnp.sum(err * err, axis=-1, keepdims=True), axis=0, keepdims=True)
            return err / D_MODEL, gt, ev, jnp.broadcast_to(sq, (8, 128))

        dy, gate, e, sq = _mm(f"ple_gate{l}", (t // tm, 1, 1), hn, row, wg, full(D_MODEL), (1, 0),
                              [f32_row, bf_row, bf_row, (_sds((t // tm, 8, 128), F32),
                                                         pl.BlockSpec((None, 8, 128), lambda i, j, k: (i, 0, 0)))],
                              extras=common + [(target, row)], epi=loss_epi)
        return dy, jnp.sum(sq, axis=0), (h, hn, gate, e)

    def gate_epi(acc, hv, pv, wpv, *gain):
        gt = _sigmoid(acc)
        ev = _dot(_bf(pv), wpv, 1, 0)
        h_new = hv + gt * ev
        if not gain:
            return h_new, gt, ev
        r = lax.rsqrt(jnp.mean(h_new * h_new, axis=-1, keepdims=True) + EPS)
        return h_new, gt, ev, h_new * r * gain[0]

    res = _mm(f"ple_gate{l}", (t // tm, 1, 1), hn, row, wg, full(D_MODEL), (1, 0),
              [f32_row, bf_row, bf_row] + ([bf_row] if next_gain is not None else []),
              extras=common + ([(next_gain, VEC)] if next_gain is not None else []), epi=gate_epi)
    h_out, gate, e = res[0], res[1], res[2]
    return h_out, (res[3] if next_gain is not None else None), (h, hn, gate, e)


def _ple_bwd(l, dh, saved, p, norm_g, wg, deps=(), emit=None):
    h, hn, gate, e = saved
    t = h.shape[0]
    tm = _row_tile(t)
    tk = _row_tile(t, 512)
    de, dz = _ple_gate_bwd(f"ple_gate_bwd{l}", dh, gate, e)
    full = lambda r: pl.BlockSpec((r, D_MODEL), lambda i, j, k: (0, 0))
    rowk = pl.BlockSpec((tk, D_MODEL), lambda i, j, k: (k, 0))
    (dwp,) = _mm(f"ple_dwp{l}", (1, 1, t // tk),
                 p, pl.BlockSpec((None, None, tk, PLE_DIM), lambda i, j, k: (l, 0, k, 0)),
                 de, rowk, (0, 0), [(_sds((PLE_DIM, D_MODEL), BF16), full(PLE_DIM))], deps=deps)
    (dwg,) = _mm(f"ple_dwg{l}", (1, 1, t // tk), hn, rowk, dz, rowk, (0, 0),
                 [(_sds((D_MODEL, D_MODEL), BF16), full(D_MODEL))])
    row = pl.BlockSpec((tm, D_MODEL), lambda i, j, k: (i, 0))
    dh_in, dg = _mm(f"ple_dhn{l}", (t // tm, 1, 1), dz, row, wg, full(D_MODEL), (1, 1),
                    _norm_bwd_outs(t, tm), extras=[(h, row), (dh, row), (norm_g, VEC)], epi=_norm_bwd_epi,
                    deps=emit(dwg, dwp) if emit else ())
    return dh_in, jnp.sum(dg, axis=0), dwg, dwp


def _ret_layer_fwd(x, norm_g, wri, fetch_wro, gn, cos, sin, next_gain, hn=None, deps=()):
    t = x.shape[0]
    tm = _row_tile(t)
    nsh, _, wsh = wri.shape
    if hn is None:
        hn = _rms_fwd("mix_norm0", x, norm_g)
    tp = _row_tile(t, 512)
    (proj,) = _mm_rows("ret_in", tp, hn, wri, 'nn_cols', [(_sds((t, RET_IN), BF16), _rows(tp, RET_IN))], deps=deps)
    gated, outp, states = _ret_fwd(proj, cos, sin, gn)
    wro = fetch_wro((gated,))
    row = _rows(tp, D_MODEL)
    more, epi = _residual_epi(next_gain)
    h1, hn_next = _mm_rows("ret_out", tp, gated, wro.reshape(RET_HEADS, RET_DV, D_MODEL), 'nn_rows',
                           _residual_outs(t, row, next_gain), extras=[(x, row)] + more, epi=epi)
    return h1, hn_next, (x, hn, proj, gated, outp, states, wro)


def _ret_layer_bwd(dh, saved, norm_g, wri, gn, cos, sin, emit_out, emit_in, deps=()):
    x, hn, proj, gated, outp, states, wro = saved
    t = x.shape[0]
    tm = _row_tile(t)
    tk = _row_tile(t, 512)
    nsh, _, wsh = wri.shape
    tg = _row_tile(t, 512)
    vw = _rows(tg, RET_V_W)
    dout, dgate, dgn = _mm_rows(
        "ret_dgate", tg, dh, wro.reshape(RET_HEADS, RET_DV, D_MODEL), 'nt_rows',
        [(_sds((t, RET_V_W), BF16), vw), (_sds((t, RET_V_W), BF16), vw),
         (_sds((t // tg, 1, RET_V_W), F32), pl.BlockSpec((None, 1, RET_V_W), lambda i, j, k: (i, 0, 0)))],
        extras=[(outp, vw), (proj, pl.BlockSpec((tg, RET_V_W), lambda i, j, k: (i, (RET_IN - RET_V_W) // RET_V_W))),
                (gn.reshape(1, RET_V_W), pl.BlockSpec((1, RET_V_W), lambda i, j, k: (0, 0)))],
        epi=_ret_gate_bwd_epi, deps=deps)
    dgn = jnp.sum(dgn, axis=0)
    (dwro,) = _mm("ret_dwro", (1, 1, t // tk),
                  gated, pl.BlockSpec((tk, RET_V_W), lambda i, j, k: (k, 0)),
                  dh, pl.BlockSpec((tk, D_MODEL), lambda i, j, k: (k, 0)), (0, 0),
                  [(_sds((RET_V_W, D_MODEL), BF16), pl.BlockSpec((RET_V_W, D_MODEL), lambda i, j, k: (0, 0)))])
    dproj = _ret_bwd(proj, cos, sin, states, dout, dgate, deps=emit_out(dwro))
    half = nsh // 2
    (dwri,) = _mm("ret_dwri", (2, 1, t // tk),
                  hn, pl.BlockSpec((tk, D_MODEL), lambda i, j, k: (k, 0)),
                  dproj, pl.BlockSpec((tk, half * wsh), lambda i, j, k: (k, i)), (0, 0),
                  [(_sds((nsh, D_MODEL, wsh), BF16), pl.BlockSpec((half, D_MODEL, wsh), lambda i, j, k: (i, 0, 0)))],
                  split=wsh)
    deps = emit_in(dwri)
    td = _row_tile(t, 256)
    row = _rows(td, D_MODEL)
    dx, dg = _mm_rows("ret_dhn", td, dproj, wri, 'nt_cols', _norm_bwd_outs(t, td),
                      extras=[(x, row), (dh, row), (norm_g, VEC)], epi=_norm_bwd_epi, deps=deps)
    return dx, jnp.sum(dg, axis=0), dgn.reshape(RET_HEADS, RET_DV)


def _mla_layer_fwd(h, hn, fetch, qa, kva, gq, gk, tabs, next_gain):
    t = h.shape[0]
    tm = _row_tile(t)
    row = pl.BlockSpec((tm, D_MODEL), lambda i, j, k: (i, 0))
    wmi = fetch('mla_in', (h,))['mla_w_in']
    (proj2,) = _mm("mla_in", (t // tm, 1, 1), hn, row,
                   wmi, pl.BlockSpec((D_MODEL, MLA_IN_PAD), lambda i, j, k: (0, 0)), (1, 0),
                   [(_sds((t, MLA_IN_PAD), F32), pl.BlockSpec((tm, MLA_IN_PAD), lambda i, j, k: (i, 0)))])
    cq, ckv = _mla_mid(proj2, qa, kva)
    up = fetch('mla_up', (cq,))
    wuq, wukv = up['mla_w_uq'], up['mla_w_ukv']
    qh, kh, vh = _mla_prep(cq, ckv, wuq, wukv, proj2, gq, gk, tabs)
    o, lse = _attn_fwd(qh, kh, vh)
    wmo = fetch('mla_out', (o,))['mla_w_out']
    more, epi = _residual_epi(next_gain)
    h_out, hn_next = _mm("mla_out", (t // tm, 1, 1), o, row,
                         wmo, pl.BlockSpec((D_MODEL, D_MODEL), lambda i, j, k: (0, 0)), (1, 0),
                         _residual_outs(t, row, next_gain), extras=[(h, row)] + more, epi=epi)
    return h_out, hn_next, (h, hn, proj2, cq, ckv, qh, kh, vh, o, lse), (wmi, wuq, wukv, wmo)


def _mla_layer_bwd(dh, saved, norm_g, wmi, qa, kva, wuq, wukv, gq, gk, wmo, tabs, deps=()):
    h, hn, proj2, cq, ckv, qh, kh, vh, o, lse = saved
    t = h.shape[0]
    tm = _row_tile(t)
    tk = _row_tile(t, 512)
    row = pl.BlockSpec((tm, D_MODEL), lambda i, j, k: (i, 0))
    rowk = pl.BlockSpec((tk, D_MODEL), lambda i, j, k: (k, 0))
    sq = pl.BlockSpec((D_MODEL, D_MODEL), lambda i, j, k: (0, 0))
    (dob,) = _mm("mla_do", (t // tm, 1, 1), dh, row, wmo, sq, (1, 1), [(_sds((t, D_MODEL), BF16), row)], deps=deps)
    (dwmo,) = _mm("mla_dwo", (1, 1, t // tk), o, rowk, dh, rowk, (0, 0), [(_sds((D_MODEL, D_MODEL), BF16), sq)])
    dqt, dkh, dvh = _attn_bwd(qh, kh, vh, dob, o, lse)
    dq, dkv, dkr, dgq, dgk = _mla_prep_bwd(cq, ckv, wuq, wukv, proj2, gq, gk, tabs, dqt, dkh, dvh)

    wide = MLA_HEADS * MLA_HD_PAD
    widek = pl.BlockSpec((tk, wide), lambda i, j, k: (k, 0))
    (dwuq,) = _mm("mla_dwuq", (1, 1, t // tk),
                  cq, pl.BlockSpec((tk, MLA_Q_RANK), lambda i, j, k: (k, 0)), dq, widek, (0, 0),
                  [(_sds((MLA_HEADS, MLA_Q_RANK, MLA_HD_PAD), BF16),
                    pl.BlockSpec((MLA_HEADS, MLA_Q_RANK, MLA_HD_PAD), lambda i, j, k: (0, 0, 0)))], split=MLA_HD_PAD)
    (dwukv,) = _mm("mla_dwukv", (1, 1, t // tk),
                   ckv, pl.BlockSpec((tk, MLA_KV_RANK), lambda i, j, k: (k, 0)), dkv, widek, (0, 0),
                   [(_sds((MLA_HEADS, MLA_KV_RANK, MLA_HD_PAD), BF16),
                     pl.BlockSpec((MLA_HEADS, MLA_KV_RANK, MLA_HD_PAD), lambda i, j, k: (0, 0, 0)))],
                   split=MLA_HD_PAD)
    side_by_side = lambda wg: wg.transpose(1, 0, 2).reshape(wg.shape[1], wide)
    widei = pl.BlockSpec((tm, wide), lambda i, j, k: (i, 0))
    (dcq,) = _mm("mla_dcq", (t // tm, 1, 1), dq, widei,
                 side_by_side(wuq), pl.BlockSpec((MLA_Q_RANK, wide), lambda i, j, k: (0, 0)), (1, 1),
                 [(_sds((t, MLA_Q_RANK), F32), pl.BlockSpec((tm, MLA_Q_RANK), lambda i, j, k: (i, 0)))])
    (dckv,) = _mm("mla_dckv", (t // tm, 1, 1), dkv, widei,
                  side_by_side(wukv), pl.BlockSpec((MLA_KV_RANK, wide), lambda i, j, k: (0, 0)), (1, 1),
                  [(_sds((t, MLA_KV_RANK), F32), pl.BlockSpec((tm, MLA_KV_RANK), lambda i, j, k: (i, 0)))])
    dproj2, dqa, dkva = _mla_mid_bwd(proj2, qa, kva, dcq, dckv, dkr)
    win = pl.BlockSpec((D_MODEL, MLA_IN_PAD), lambda i, j, k: (0, 0))
    (dwmi,) = _mm("mla_dwin", (1, 1, t // tk), hn, rowk,
                  dproj2, pl.BlockSpec((tk, MLA_IN_PAD), lambda i, j, k: (k, 0)), (0, 0),
                  [(_sds((D_MODEL, MLA_IN_PAD), BF16), win)])
    dh_in, dg = _mm("mla_dhn", (t // tm, 1, 1),
                    dproj2, pl.BlockSpec((tm, MLA_IN_PAD), lambda i, j, k: (i, 0)), wmi, win, (1, 1),
                    _norm_bwd_outs(t, tm), extras=[(h, row), (dh, row), (norm_g, VEC)], epi=_norm_bwd_epi)
    return dh_in, dict(mix=jnp.sum(dg, axis=0), wmi=dwmi, qa=dqa, kva=dkva, wuq=dwuq, wukv=dwukv, gq=dgq, gk=dgk,
                       wmo=dwmo)


def _local_step(x, p, target, w, fetch, emit=lambda group: ()):
    t = x.shape[0]
    cos_r, sin_r, tabs = w['tables'] if 'tables' in w else _rope_tables(t, 0.0)
    row = lambda a, i: a[i:i + 1]

    h1, hn1, s_ret = _ret_layer_fwd(x, row(w['mix_norm'], 0), w['ret_w_in'],
                                    lambda after: fetch('ret_out', after)['ret_w_out'], w['ret_gn'], cos_r, sin_r,
                                    row(w['mlp_norm'], 0), hn=w.get('hn0'), deps=w['deps'])
    h2, hn2, s_mlp0 = _mlp_fwd(0, h1, hn1, fetch('mlp_w1_0', (h1,))['mlp_w1'],
                               lambda after: fetch('mlp_w2_0', after)['mlp_w2'], row(w['ple_norm'], 0))
    w0 = fetch('ple_0', (h2,))
    h3, hn3, s_ple0 = _ple_fwd(0, h2, hn2, p, w0['ple_gate_w'], w0['ple_proj_w'], row(w['mix_norm'], 1))
    h4, hn4, s_mla, (wmi, wuq, wukv, wmo) = _mla_layer_fwd(
        h3, hn3, fetch, w['mla_q_a_norm'], w['mla_kv_a_norm'], w['mla_q_norm'], w['mla_k_norm'], tabs,
        row(w['mlp_norm'], 1))
    mla_w = (wmi, w['mla_q_a_norm'], w['mla_kv_a_norm'], wuq, wukv, w['mla_q_norm'], w['mla_k_norm'], wmo, tabs)
    w1 = fetch('layer_1', (h4,))
    h5, hn5, s_mlp1 = _mlp_fwd(1, h4, hn4, w1['mlp_w1'], lambda after: w1['mlp_w2'], row(w['ple_norm'], 1))
    dy, sq_err, s_ple1 = _ple_fwd(1, h5, hn5, p, w1['ple_gate_w'], w1['ple_proj_w'], None, target)

    n = N_DEV
    colsh = lambda a: a.reshape(a.shape[0], n, a.shape[1] // n).transpose(1, 0, 2)
    rowsh = lambda a: a.reshape(n, a.shape[0] // n, a.shape[1])
    big = {}

    def emit_group(group):
        big.update(group)
        return emit(group)

    dh5, dg_ple1, dwg1, dwp1 = _ple_bwd(1, dy, s_ple1, p, row(w['ple_norm'], 1), w1['ple_gate_w'])
    dh4, dg_mlp1, dw1_1, dw2_1 = _mlp_bwd(1, dh5, s_mlp1, row(w['mlp_norm'], 1))
    deps = emit_group({('ple_gate_w', 1): rowsh(dwg1), ('ple_proj_w', 1): colsh(dwp1),
                       ('mlp_w2', 1): dw2_1, ('mlp_w1', 1): dw1_1})
    dh3, gm = _mla_layer_bwd(dh4, s_mla, row(w['mix_norm'], 1), *mla_w, deps=deps)
    deps = emit_group({('mla_w_out', 0): rowsh(gm['wmo']), ('mla_w_uq', 0): _gather_rope(gm['wuq']),
                       ('mla_w_ukv', 0): gm['wukv'], ('mla_w_in', 0): rowsh(_gather_rope(gm['wmi']))})
    dh2, dg_ple0, _, _ = _ple_bwd(
        0, dh3, s_ple0, p, row(w['ple_norm'], 0), w0['ple_gate_w'], deps=deps,
        emit=lambda dwg, dwp: emit_group({('ple_gate_w', 0): rowsh(dwg), ('ple_proj_w', 0): colsh(dwp)}))
    dh1, dg_mlp0, _, _ = _mlp_bwd(0, dh2, s_mlp0, row(w['mlp_norm'], 0),
                                  emit_w2=lambda dw2: emit_group({('mlp_w2', 0): dw2}),
                                  emit_w1=lambda dw1: emit_group({('mlp_w1', 0): dw1}))
    dx, dg_mix0, dgn = _ret_layer_bwd(
        dh1, s_ret, row(w['mix_norm'], 0), w['ret_w_in'], w['ret_gn'], cos_r, sin_r,
        lambda dwro: emit_group({('ret_w_out', 0): rowsh(dwro)}),
        lambda dwri: emit_group({('ret_w_in', 0): dwri}))

    small = dict(
        mix_norm=[dg_mix0, gm['mix']], mlp_norm=[dg_mlp0, dg_mlp1], ple_norm=[dg_ple0, dg_ple1],
        ret_gn=dgn, mla_q_a_norm=gm['qa'], mla_kv_a_norm=gm['kva'], mla_q_norm=gm['gq'], mla_k_norm=gm['gk'],
    )
    return sq_err, dx, big, small


def _my_place():
    x, y, c = lax.axis_index("x"), lax.axis_index("y"), lax.axis_index("c")
    return x, y, c


def _flat(px, py, pc):
    return 4 * px + 2 * py + pc


def _peer(x, y, c, r):
    return (1 - x if r & 4 else x, 1 - y if r & 2 else y, 1 - c if r & 1 else c)


HBM = pl.BlockSpec(memory_space=pltpu.HBM)
SEMS = pl.BlockSpec(memory_space=pltpu.SEMAPHORE)
SIDE_EFFECT = pltpu.SideEffectType.DATAFLOW_SIDE_EFFECTING


def _rs_copies(x, y, c, srcs, lands, send_sems, recv_sems):
    copies = []
    for a in range(len(srcs)):
        for r in range(1, N_DEV):
            peer = _peer(x, y, c, r)
            k = a * (N_DEV - 1) + r - 1
            copies.append(pltpu.make_async_remote_copy(
                src_ref=srcs[a].at[_flat(*peer)], dst_ref=lands[a].at[r - 1],
                send_sem=send_sems.at[k], recv_sem=recv_sems.at[k], device_id=peer, device_id_type=MESH))
    return copies


def _rs_start(name, arrays):
    n = len(arrays)
    hbm = lambda a: pltpu.with_memory_space_constraint(a, pltpu.HBM)
    lands = [hbm(lax.empty((N_DEV - 1,) + a.shape[1:], a.dtype)) for a in arrays]

    def body(*refs):
        srcs, lnd = refs[:n], refs[n:2 * n]
        send_sems, recv_sems = refs[2 * n], refs[2 * n + 1]
        token = refs[-1]
        for cp in _rs_copies(*_my_place(), srcs, lnd, send_sems, recv_sems):
            cp.start()
        token[...] = jnp.zeros_like(token)

    outs = pl.pallas_call(
        body, name=name,
        in_specs=[HBM] * (2 * n),
        out_specs=[SEMS, SEMS] + [HBM] * (2 * n) + [pl.BlockSpec(memory_space=pltpu.VMEM)],
        out_shape=[pltpu.SemaphoreType.DMA((n * (N_DEV - 1),)), pltpu.SemaphoreType.DMA((n * (N_DEV - 1),))]
        + [pltpu.HBM(a.shape, a.dtype) for a in arrays] + [pltpu.HBM(l.shape, l.dtype) for l in lands]
        + [_sds((8, 128), F32)],
        input_output_aliases={i: 2 + i for i in range(2 * n)},
        compiler_params=pltpu.CompilerParams(has_side_effects=SIDE_EFFECT),
    )(*[hbm(a) for a in arrays], *lands)
    return outs[0], outs[1], outs[2:2 + n], outs[2 + n:2 + 2 * n], outs[-1]


def _rs_wait(name, send_sems, recv_sems, srcs, lands, after):
    n = len(srcs)

    def body(*refs):
        src_refs, lnd = refs[:n], refs[n:2 * n]
        send, recv = refs[2 * n], refs[2 * n + 1]
        for cp in _rs_copies(*_my_place(), src_refs, lnd, send, recv):
            cp.wait_send()
            cp.wait_recv()

    outs = pl.pallas_call(
        body, name=name,
        in_specs=[HBM] * (2 * n) + [SEMS, SEMS] + [ANY] * len(after),
        out_specs=[HBM] * (2 * n),
        out_shape=[pltpu.HBM(a.shape, a.dtype) for a in list(srcs) + list(lands)],
        input_output_aliases={i: i for i in range(2 * n)},
        compiler_params=pltpu.CompilerParams(has_side_effects=SIDE_EFFECT),
    )(*srcs, *lands, send_sems, recv_sems, *after)
    return outs[:n], outs[n:]


SMALL_PACK_ROWS = 16


def _all_reduce_small(rows, deps=()):
    n = len(rows)

    def body(*refs):
        ins = refs[:n]
        out_ref, mine, buf, send_sems, recv_sems = refs[n + len(deps):]
        x, y, c = _my_place()
        mine[...] = jnp.zeros_like(mine)
        for (r0, a), ref in zip(rows, ins):
            mine[r0:r0 + a.shape[0], 0:a.shape[1]] = ref[...]
        buf[_flat(x, y, c)] = mine[...]
        copies = []
        for r in range(1, N_DEV):
            peer = _peer(x, y, c, r)
            send = pltpu.make_async_remote_copy(
                src_ref=mine, dst_ref=buf.at[_flat(x, y, c)],
                send_sem=send_sems.at[r - 1], recv_sem=recv_sems.at[r - 1], device_id=peer, device_id_type=MESH)
            send.start()
            recv = pltpu.make_async_remote_copy(
                src_ref=mine, dst_ref=buf.at[_flat(*peer)],
                send_sem=send_sems.at[r - 1], recv_sem=recv_sems.at[r - 1], device_id=peer, device_id_type=MESH)
            copies.append((send, recv))
        for send, recv in copies:
            send.wait_send()
            recv.wait_recv()
        acc = buf[0]
        for s in range(1, N_DEV):
            acc = acc + buf[s]
        out_ref[...] = acc

    vm = pl.BlockSpec(memory_space=pltpu.VMEM)
    shape = (SMALL_PACK_ROWS, D_MODEL)
    return pl.pallas_call(
        body, name="all_reduce_small", in_specs=[vm] * n + [ANY] * len(deps), out_specs=vm,
        out_shape=_sds(shape, F32),
        scratch_shapes=[pltpu.VMEM(shape, F32), pltpu.VMEM((N_DEV,) + shape, F32),
                        pltpu.SemaphoreType.DMA((7,)), pltpu.SemaphoreType.DMA((7,))],
    )(*[a for _, a in rows], *deps)


def _adamw_math(w, g, m, v):
    m = ADAM_B1 * m + (1.0 - ADAM_B1) * g
    v = ADAM_B2 * v + (1.0 - ADAM_B2) * (g * g)
    m_hat = m / (1.0 - ADAM_B1 ** ADAM_STEP)
    v_hat = v / (1.0 - ADAM_B2 ** ADAM_STEP)
    delta = -ADAM_LR * (m_hat / (jnp.sqrt(v_hat) + ADAM_EPS) + ADAM_WD * w)
    return delta, m, v


def _adamw_big(name, w, m, v, srcs, lands, me):
    nl, rows, cols = w.shape
    tr = next(cand for cand in (256, 128, 64, 32, 16, 8) if rows % cand == 0)

    def body(me_ref, w_ref, m_ref, v_ref, *rest):
        src_refs, land_refs = rest[:nl], rest[nl:2 * nl]
        g_ref, d_ref, mo_ref, vo_ref = rest[2 * nl:]
        for layer in range(nl):
            @pl.when(pl.program_id(0) == layer)
            def _():
                g = src_refs[layer][...].astype(F32)
                for s in range(N_DEV - 1):
                    g = g + land_refs[layer][s].astype(F32)
                delta, mn, vn = _adamw_math(w_ref[...], g, m_ref[...], v_ref[...])
                g_ref[...] = g
                d_ref[...] = delta
                mo_ref[...] = mn
                vo_ref[...] = vn

    blk = pl.BlockSpec((None, tr, cols), lambda l, i, me_ref: (l, i, 0))
    at = lambda layer, l, i: jnp.where(l == layer, i, 0)
    own = [pl.BlockSpec((None, tr, cols), functools.partial(lambda layer, l, i, me_ref: (me_ref[0], at(layer, l, i), 0),
                                                            layer)) for layer in range(nl)]
    peers = [pl.BlockSpec((N_DEV - 1, tr, cols), functools.partial(lambda layer, l, i, me_ref: (0, at(layer, l, i), 0),
                                                                   layer)) for layer in range(nl)]
    return pl.pallas_call(
        body, name=name,
        grid_spec=pltpu.PrefetchScalarGridSpec(
            num_scalar_prefetch=1, grid=(nl, rows // tr),
            in_specs=[blk, blk, blk] + own + peers, out_specs=[blk] * 4),
        out_shape=[_sds((nl, rows, cols), F32)] * 4,
        compiler_params=_cparams(("arbitrary", "arbitrary")),
    )(me, w, m, v, *srcs, *lands)


def _adamw_small(ws, gs, ms, vs):
    n = len(ws)

    def body(*refs):
        w_refs, g_refs, m_refs, v_refs = (refs[i * n:(i + 1) * n] for i in range(4))
        d_out, m_out, v_out = (refs[(4 + i) * n:(5 + i) * n] for i in range(3))
        for i in range(n):
            delta, mn, vn = _adamw_math(w_refs[i][...], g_refs[i][...], m_refs[i][...], v_refs[i][...])
            d_out[i][...] = delta
            m_out[i][...] = mn
            v_out[i][...] = vn

    vm = pl.BlockSpec(memory_space=pltpu.VMEM)
    outs = pl.pallas_call(
        body, name="adamw_small", in_specs=[vm] * (4 * n), out_specs=[vm] * (3 * n),
        out_shape=[_sds(a.shape, F32) for a in ws] * 3,
    )(*ws, *gs, *ms, *vs)
    return outs[:n], outs[n:2 * n], outs[2 * n:]


def _pad_to(a, rows, cols):
    return jnp.pad(a, ((0, rows - a.shape[0]), (0, cols - a.shape[1])))


def _place_own(blocks):
    me = _flat(*_my_place())
    return [lax.dynamic_update_slice(lax.empty((N_DEV,) + b.shape, b.dtype), b[None], (me,) + (0,) * b.ndim)
            for b in blocks]


def _ag_copies(x, y, c, blocks, bufs, send_sems, recv_sems, arriving):
    copies = []
    for a in range(len(blocks)):
        for r in range(1, N_DEV):
            peer = _peer(x, y, c, r)
            k = a * (N_DEV - 1) + r - 1
            copies.append(pltpu.make_async_remote_copy(
                src_ref=blocks[a], dst_ref=bufs[a].at[_flat(*(peer if arriving else (x, y, c)))],
                send_sem=send_sems.at[k], recv_sem=recv_sems.at[k], device_id=peer, device_id_type=MESH))
    return copies


def _ag_start(groups, after):
    flat = [pair for g in groups for pair in g]
    n, ng = len(flat), len(groups)
    hbm = lambda a: pltpu.with_memory_space_constraint(a, pltpu.HBM)

    def body(*refs):
        blocks, bufs = refs[:n], refs[n:2 * n]
        sems = refs[2 * n + len(after):2 * n + len(after) + 2 * ng]
        x, y, c = _my_place()
        at = 0
        for gi, g in enumerate(groups):
            for cp in _ag_copies(x, y, c, blocks[at:at + len(g)], bufs[at:at + len(g)], sems[2 * gi],
                                 sems[2 * gi + 1], arriving=False):
                cp.start()
            at += len(g)
        refs[-1][...] = jnp.zeros_like(refs[-1])

    sem_shapes = [pltpu.SemaphoreType.DMA((len(g) * (N_DEV - 1),)) for g in groups for _ in range(2)]
    outs = pl.pallas_call(
        body, name="gather_start",
        in_specs=[HBM] * (2 * n) + [ANY] * len(after),
        out_specs=[SEMS] * (2 * ng) + [HBM] * (2 * n) + [pl.BlockSpec(memory_space=pltpu.VMEM)],
        out_shape=sem_shapes + [pltpu.HBM(b.shape, b.dtype) for b, _ in flat]
        + [pltpu.HBM(u.shape, u.dtype) for _, u in flat] + [_sds((8, 128), F32)],
        input_output_aliases={i: 2 * ng + i for i in range(2 * n)},
        compiler_params=pltpu.CompilerParams(has_side_effects=SIDE_EFFECT),
    )(*[hbm(b) for b, _ in flat], *[hbm(u) for _, u in flat], *after)
    blocks_thru, bufs_thru = outs[2 * ng:2 * ng + n], outs[2 * ng + n:2 * ng + 2 * n]
    started, at = [], 0
    for gi, g in enumerate(groups):
        started.append((outs[2 * gi], outs[2 * gi + 1], blocks_thru[at:at + len(g)], bufs_thru[at:at + len(g)]))
        at += len(g)
    return started, outs[-1]


def _ag_wait(name, send_sems, recv_sems, blocks, bufs, after):
    n = len(blocks)

    def body(*refs):
        for cp in _ag_copies(*_my_place(), refs[:n], refs[n:2 * n], refs[2 * n], refs[2 * n + 1], arriving=True):
            cp.wait_send()
            cp.wait_recv()

    outs = pl.pallas_call(
        body, name=name,
        in_specs=[HBM] * (2 * n) + [SEMS, SEMS] + [ANY] * len(after),
        out_specs=[HBM] * (2 * n),
        out_shape=[pltpu.HBM(a.shape, a.dtype) for a in list(blocks) + list(bufs)],
        input_output_aliases={i: i for i in range(2 * n)},
        compiler_params=pltpu.CompilerParams(has_side_effects=SIDE_EFFECT),
    )(*blocks, *bufs, send_sems, recv_sems, *after)
    return outs[n:]


def _split_call(name, body, thru, sems_in, new_sems, after):
    n, ns, nn = len(thru), len(sems_in), len(new_sems)
    hbm = lambda a: pltpu.with_memory_space_constraint(a, pltpu.HBM)

    def wrapped(*refs):
        body(refs[:n], refs[n:n + ns], refs[n + ns + len(after):n + ns + len(after) + nn])
        refs[-1][...] = jnp.zeros_like(refs[-1])

    outs = pl.pallas_call(
        wrapped, name=name,
        in_specs=[HBM] * n + [SEMS] * ns + [ANY] * len(after),
        out_specs=[SEMS] * nn + [HBM] * n + [pl.BlockSpec(memory_space=pltpu.VMEM)],
        out_shape=[pltpu.SemaphoreType.DMA((k,)) for k in new_sems] + [pltpu.HBM(a.shape, a.dtype) for a in thru]
        + [_sds((8, 128), F32)],
        input_output_aliases={i: nn + i for i in range(n)},
        compiler_params=pltpu.CompilerParams(has_side_effects=SIDE_EFFECT),
    )(*[hbm(a) for a in thru], *sems_in, *after)
    return list(outs[:nn]), list(outs[nn:nn + n]), outs[-1]


def _two_level_gather(name, blocks, bufs, after=()):
    n = len(blocks)

    def copies(refs, s1, r1, s2, r2):
        x, y, c = _my_place()
        me, sibling = (x, y, c), (x, y, 1 - c)
        chips = [(1 - x, y), (x, 1 - y), (1 - x, 1 - y)]
        blk, buf = refs[:n], refs[n:]
        out = dict(send1=[], recv1_sib=[], recv1_ici=[], send2=[], recv2=[])
        for a in range(n):
            place = lambda dev: buf[a].at[_flat(*dev)]
            for k, to in enumerate([sibling] + [(*chip, c) for chip in chips]):
                mk = lambda dst: pltpu.make_async_remote_copy(
                    src_ref=blk[a], dst_ref=dst, send_sem=s1.at[4 * a + k], recv_sem=r1.at[4 * a + k],
                    device_id=to, device_id_type=MESH)
                out['send1'].append(mk(place(me)))
                out['recv1_sib' if k == 0 else 'recv1_ici'].append(mk(place(to)))
            for j, chip in enumerate(chips):
                mk = lambda dev: pltpu.make_async_remote_copy(
                    src_ref=place(dev), dst_ref=place(dev), send_sem=s2.at[3 * a + j], recv_sem=r2.at[3 * a + j],
                    device_id=sibling, device_id_type=MESH)
                out['send2'].append(mk((*chip, c)))
                out['recv2'].append(mk((*chip, 1 - c)))
        return out

    def start(refs, sems_in, new):
        for cp in copies(refs, new[0], new[1], new[0], new[1])['send1']:
            cp.start()

    def forward(refs, sems_in, new):
        cps = copies(refs, sems_in[0], sems_in[1], new[0], new[1])
        for cp in cps['recv1_ici']:
            cp.wait_recv()
        for cp in cps['send2']:
            cp.start()

    def finish(refs, sems_in, new):
        cps = copies(refs, *sems_in)
        for cp in cps['recv1_sib'] + cps['recv2']:
            cp.wait_recv()
        for cp in cps['send1'] + cps['send2']:
            cp.wait_send()

    sems1, thru, token = _split_call(name + "_start", start, list(blocks) + list(bufs), [], [4 * n, 4 * n], after)

    def complete(after):
        sems2, thru2, token2 = _split_call(name + "_forward", forward, thru, sems1, [3 * n, 3 * n], after)
        _, thru3, _ = _split_call(name + "_wait", finish, thru2, sems1 + sems2, [], ())
        return thru3[n:], token2

    return token, complete


def _prepare_weights(p, x):
    n = N_DEV
    bf = lambda a: a.astype(BF16)
    gn_pack = jnp.concatenate([
        _pad_to(p['ret_gn'][0], RET_HEADS, 128), _pad_to(p['mla_q_a_norm'], 1, 128),
        _pad_to(p['mla_kv_a_norm'], 1, 128), jnp.zeros((2, 128), F32)], axis=0)
    ple = lambda l: [bf(p['ple_gate_w'][l]), bf(p['ple_proj_w'][l])]
    names = ('mlp_w1_0', 'mlp_w2_0', 'ple_0', 'mla_in', 'mla_up', 'mla_out', 'layer_1')
    later = [[bf(p['mlp_w1'][0])], [bf(p['mlp_w2'][0])], ple(0),
             [bf(p['mla_w_in'][0])], [bf(p['mla_w_uq'][0]), bf(p['mla_w_ukv'][0])], [bf(p['mla_w_out'][0])],
             [bf(p['mlp_w1'][1]), bf(p['mlp_w2'][1])] + ple(1)]
    first = [gn_pack, bf(p['ret_w_in'][0])]
    second = [bf(p['ret_w_out'][0])]
    token, complete_first = _two_level_gather("first_gather", first, _place_own(first))
    token2, complete_second = _two_level_gather("second_gather", second, _place_own(second), (token,))
    hn0 = _rms_fwd("mix_norm0", x, p['mix_norm'][0:1], deps=(token2,))
    bufs = _place_own([b for g in later for b in g])
    tables = _rope_tables(x.shape[0], token2[0, 0])
    (pack, wri), token = complete_first((hn0, tables[0], tables[1], *tables[2], *bufs))
    groups, at = [], 0
    for g in later:
        groups.append(list(zip(g, bufs[at:at + len(g)])))
        at += len(g)
    started, token = _ag_start(groups, (token,))

    w = {k: p[k] for k in ('mix_norm', 'mlp_norm', 'ple_norm')}
    w['hn0'] = hn0
    w['tables'] = tables
    w['ret_gn'] = pack[:, :RET_HEADS, :RET_DV // n].transpose(1, 0, 2).reshape(RET_HEADS, RET_DV)
    w['mla_q_a_norm'] = pack[:, RET_HEADS, :MLA_Q_RANK // n].reshape(1, MLA_Q_RANK)
    w['mla_kv_a_norm'] = pack[:, RET_HEADS + 1, :MLA_KV_RANK // n].reshape(1, MLA_KV_RANK)
    w['ret_w_in'] = wri
    w['mla_q_norm'] = _spread_rope(p['mla_q_norm'])
    w['mla_k_norm'] = _spread_rope(p['mla_k_norm'])
    w['deps'] = (token,)

    def fetch(name, after):
        if name == 'ret_out':
            return dict(ret_w_out=complete_second(after)[0][0].reshape(RET_V_W, D_MODEL))
        got = list(_ag_wait("gather_wait_" + name, *started[names.index(name)], after))
        if name == 'mla_in':
            return dict(mla_w_in=_spread_rope(got[0].reshape(D_MODEL, MLA_IN)))
        if name == 'mla_up':
            return dict(mla_w_uq=_spread_rope(got[0]), mla_w_ukv=got[1])
        if name == 'mla_out':
            return dict(mla_w_out=got[0].reshape(D_MODEL, D_MODEL))
        out = {}
        if name in ('mlp_w1_0', 'layer_1'):
            out['mlp_w1'] = got.pop(0)
        if name in ('mlp_w2_0', 'layer_1'):
            out['mlp_w2'] = got.pop(0)
        if name in ('ple_0', 'layer_1'):
            out['ple_gate_w'] = got[0].reshape(D_MODEL, D_MODEL)
            out['ple_proj_w'] = got[1].transpose(1, 0, 2).reshape(PLE_DIM, D_MODEL)
        return out

    return w, fetch


def _small_grads(small, after):
    rows = [(0, small['mix_norm'][0]), (1, small['mix_norm'][1]), (2, small['mlp_norm'][0]),
            (3, small['mlp_norm'][1]), (4, small['ple_norm'][0]), (5, small['ple_norm'][1]),
            (6, small['ret_gn']), (10, small['mla_q_a_norm']), (11, small['mla_kv_a_norm']),
            (12, small['mla_q_norm']), (13, small['mla_k_norm']), (14, small['sq_err'])]
    gs = _all_reduce_small(rows, after)
    me = _flat(*_my_place())
    n = N_DEV
    return dict(
        sq_err=gs[14, 0],
        mix_norm=gs[0:2], mlp_norm=gs[2:4], ple_norm=gs[4:6],
        ret_gn=lax.dynamic_slice(gs, (6, me * (RET_DV // n)), (RET_HEADS, RET_DV // n)),
        mla_q_a_norm=lax.dynamic_slice(gs, (10, me * (MLA_Q_RANK // n)), (1, MLA_Q_RANK // n)),
        mla_kv_a_norm=lax.dynamic_slice(gs, (11, me * (MLA_KV_RANK // n)), (1, MLA_KV_RANK // n)),
        mla_q_norm=_gather_rope(gs[12:13, :MLA_HD_PAD]), mla_k_norm=_gather_rope(gs[13:14, :MLA_HD_PAD]))


def kernel(x, p, mix_norm, ret_w_in, ret_gn, ret_w_out, mla_w_in, mla_q_a_norm, mla_kv_a_norm, mla_w_uq, mla_w_ukv, mla_q_norm, mla_k_norm, mla_w_out, mlp_norm, mlp_w1, mlp_w2, ple_norm, ple_gate_w, ple_proj_w, loss_target, m_mix_norm, m_ret_w_in, m_ret_gn, m_ret_w_out, m_mla_w_in, m_mla_q_a_norm, m_mla_kv_a_norm, m_mla_w_uq, m_mla_w_ukv, m_mla_q_norm, m_mla_k_norm, m_mla_w_out, m_mlp_norm, m_mlp_w1, m_mlp_w2, m_ple_norm, m_ple_gate_w, m_ple_proj_w, v_mix_norm, v_ret_w_in, v_ret_gn, v_ret_w_out, v_mla_w_in, v_mla_q_a_norm, v_mla_kv_a_norm, v_mla_w_uq, v_mla_w_ukv, v_mla_q_norm, v_mla_k_norm, v_mla_w_out, v_mlp_norm, v_mlp_w1, v_mlp_w2, v_ple_norm, v_ple_gate_w, v_ple_proj_w):
    given = dict(locals())
    params = {n: given[n] for n in WEIGHTS}
    w, fetch = _prepare_weights(params, x[0])

    started = []

    def emit(group):
        keys = list(group)
        send, recv, srcs, lands, token = _rs_start(f"rs_start{len(started)}", [group[k] for k in keys])
        started.append((keys, send, recv, srcs, lands))
        return (token,)

    sq_err, grad_x, _, small = _local_step(x[0], p, loss_target[0], w, fetch, emit)
    small['sq_err'] = sq_err[0:1]

    grads, deltas, new_m, new_v = {}, {}, {}, {}
    total = {}

    def small_updates(after):
        sg = _small_grads(small, after)
        total['loss'] = 0.5 / D_MODEL * sg['sq_err']
        two_d = lambda a: a.reshape(-1, a.shape[-1])
        d_s, m_s, v_s = _adamw_small(
            [two_d(params[n]) for n in SMALL], [sg[n] for n in SMALL],
            [two_d(given["m_" + n]) for n in SMALL], [two_d(given["v_" + n]) for n in SMALL])
        for i, n in enumerate(SMALL):
            shape = params[n].shape
            grads[n], deltas[n], new_m[n], new_v[n] = (a.reshape(shape) for a in (sg[n], d_s[i], m_s[i], v_s[i]))
        return (d_s[0],)

    me = _flat(*_my_place()).astype(jnp.int32).reshape(1)
    after = (grad_x,)
    src_of, land_of = {}, {}
    for gi, (keys, send, recv, srcs, lands) in enumerate(started):
        if gi == len(started) - 1:
            after = small_updates(after)
        srcs, lands = _rs_wait(f"rs_wait{gi}", send, recv, srcs, lands, after)
        for k, s, l in zip(keys, srcs, lands):
            src_of[k], land_of[k] = s, l
        done = [n for n in BIG if n not in grads and all((n, l) in src_of for l in range(params[n].shape[0]))]
        for n in done:
            layers = range(params[n].shape[0])
            grads[n], deltas[n], new_m[n], new_v[n] = _adamw_big(
                "adamw_" + n, params[n], given["m_" + n], given["v_" + n],
                [src_of[(n, l)] for l in layers], [land_of[(n, l)] for l in layers], me)
        if done:
            after = tuple(deltas[n] for n in done)

    return (total['loss'], grad_x[None], *[grads[n] for n in WEIGHTS], *[deltas[n] for n in WEIGHTS],
            *[new_m[n] for n in WEIGHTS], *[new_v[n] for n in WEIGHTS])
```

```python
import functools

import jax
import jax.numpy as jnp
from jax import lax
from jax.experimental import pallas as pl
from jax.experimental.pallas import tpu as pltpu

F32 = jnp.float32
BF16 = jnp.bfloat16
MESH = pl.DeviceIdType.MESH
ANY = pl.BlockSpec(memory_space=pl.ANY)

N_DEV = 8
D_MODEL = 1024
CHUNK = 64
RET_BLOCK = 4 * CHUNK
EPS = 1e-6
ROPE_THETA = 10000.0
RET_HEADS = 4
RET_DK = 256
RET_DV = 512
RET_QK_W = RET_HEADS * RET_DK
RET_V_W = RET_HEADS * RET_DV
RET_IN = 2 * RET_QK_W + 2 * RET_V_W
MLA_HEADS = 8
MLA_NOPE = 128
MLA_ROPE = 64
MLA_QKD = MLA_NOPE + MLA_ROPE
MLA_VD = 128
MLA_Q_RANK = 384
MLA_KV_RANK = 256
MLA_IN = MLA_Q_RANK + MLA_KV_RANK + MLA_ROPE
MLA_IN_PAD = 768
MLA_HD_PAD = 256
D_FF = 4096
PLE_DIM = 256
ATT_SCALE = MLA_QKD ** -0.5
LOG2E = 1.4426950408889634
ATT_EXP2 = ATT_SCALE * LOG2E

ADAM_LR = 0.001
ADAM_B1 = 0.9
ADAM_B2 = 0.999
ADAM_EPS = 1e-08
ADAM_WD = 0.01
ADAM_STEP = 10

VMEM_LIMIT = 52 * 1024 * 1024
ROW_TILE = 1024
RET_ROWS = 512
ATT_BLOCK = 256
ATT_QROWS = 1024
ATT_KROWS = 1024
ATT_HEADS = 2
PREP_ROWS = 2048

WEIGHTS = ['mix_norm', 'ret_w_in', 'ret_gn', 'ret_w_out', 'mla_w_in', 'mla_q_a_norm', 'mla_kv_a_norm',
           'mla_w_uq', 'mla_w_ukv', 'mla_q_norm', 'mla_k_norm', 'mla_w_out', 'mlp_norm', 'mlp_w1', 'mlp_w2',
           'ple_norm', 'ple_gate_w', 'ple_proj_w']
BIG = ['ret_w_in', 'ret_w_out', 'mla_w_in', 'mla_w_uq', 'mla_w_ukv', 'mla_w_out', 'mlp_w1', 'mlp_w2',
       'ple_gate_w', 'ple_proj_w']
SMALL = [w for w in WEIGHTS if w not in BIG]


def _cparams(sem=None):
    return pltpu.CompilerParams(dimension_semantics=sem, vmem_limit_bytes=VMEM_LIMIT)


def _dot(a, b, ca, cb):
    return lax.dot_general(a, b, (((ca,), (cb,)), ((), ())), preferred_element_type=F32)


def _bf(v):
    return v if v.dtype == BF16 else v.astype(BF16)


def _sigmoid(z):
    return 1.0 / (1.0 + jnp.exp(-z))


def _mm(name, grid, a, a_spec, b, b_spec, contract, outs, extras=(), epi=None, deps=(), split=None):
    nk = grid[2]
    n_ex, n_out, n_dep = len(extras), len(outs), len(deps)
    acc_shape = tuple(d for d in outs[0][1].block_shape if d is not None)
    if split is not None:
        acc_shape = (acc_shape[1], acc_shape[0] * split)

    def body(*refs):
        a_ref, b_ref = refs[:2]
        ex_refs = refs[2:2 + n_ex]
        out_refs = refs[2 + n_ex + n_dep:2 + n_ex + n_dep + n_out]

        def product():
            return _dot(_bf(a_ref[...]), _bf(b_ref[...]), contract[0], contract[1])

        def finish(acc):
            if split is not None:
                for j in range(acc_shape[1] // split):
                    out_refs[0][j] = acc[:, j * split:(j + 1) * split].astype(out_refs[0].dtype)
                return
            acc = acc[...]
            res = epi(acc, *[r[...] for r in ex_refs]) if epi is not None else (acc,)
            for o, r in zip(out_refs, res):
                o[...] = r.astype(o.dtype)

        if nk == 1:
            finish(product())
        else:
            acc_ref = refs[-1]
            k = pl.program_id(2)

            @pl.when(k == 0)
            def _():
                acc_ref[...] = jnp.zeros_like(acc_ref)

            acc_ref[...] += product()

            @pl.when(k == nk - 1)
            def _():
                finish(acc_ref)

    return pl.pallas_call(
        body, name=name, grid=grid,
        in_specs=[a_spec, b_spec] + [s for _, s in extras] + [ANY] * n_dep,
        out_specs=[s for _, s in outs],
        out_shape=[s for s, _ in outs],
        scratch_shapes=[pltpu.VMEM(acc_shape, F32)] if nk > 1 else [],
        compiler_params=_cparams(("parallel", "parallel", "arbitrary")),
    )(a, b, *[x for x, _ in extras], *deps)


def _mm_rows(name, tm, a, w, mode, outs, extras=(), epi=None, deps=()):
    n_sh, rows, cols = w.shape
    n_ex, n_out, n_dep = len(extras), len(outs), len(deps)
    by_cols = mode in ('nn_cols', 'nt_rows')
    width = cols if mode == 'nn_cols' else rows

    def body(*refs):
        a_ref, w_ref = refs[:2]
        ex_refs = refs[2:2 + n_ex]
        out_refs = refs[2 + n_ex + n_dep:2 + n_ex + n_dep + n_out]
        if by_cols:
            av = _bf(a_ref[...])
            for s in range(n_sh):
                cs = slice(s * width, (s + 1) * width)
                acc = _dot(av, w_ref[s], 1, 0 if mode == 'nn_cols' else 1)
                res = epi(acc, *[r[:, cs] for r in ex_refs]) if epi is not None else (acc,)
                for o, r in zip(out_refs, res):
                    o[:, cs] = r.astype(o.dtype)
        else:
            chunk = rows if mode == 'nn_rows' else cols
            acc = None
            for s in range(n_sh):
                part = _dot(_bf(a_ref[:, s * chunk:(s + 1) * chunk]), w_ref[s], 1, 0 if mode == 'nn_rows' else 1)
                acc = part if acc is None else acc + part
            res = epi(acc, *[r[...] for r in ex_refs]) if epi is not None else (acc,)
            for o, r in zip(out_refs, res):
                o[...] = r.astype(o.dtype)

    t, ka = a.shape
    return pl.pallas_call(
        body, name=name, grid=(t // tm, 1, 1),
        in_specs=[pl.BlockSpec((tm, ka), lambda i, j, k: (i, 0)),
                  pl.BlockSpec((n_sh, rows, cols), lambda i, j, k: (0, 0, 0))] + [s for _, s in extras] + [ANY] * n_dep,
        out_specs=[s for _, s in outs],
        out_shape=[s for s, _ in outs],
        compiler_params=_cparams(("parallel", "arbitrary", "arbitrary")),
    )(a, w, *[x for x, _ in extras], *deps)


def _sds(shape, dtype):
    return jax.ShapeDtypeStruct(shape, dtype)


def _row_tile(t, cap=ROW_TILE):
    return min(cap, t)


def _rms_fwd(name, x, g, deps=()):
    t, d = x.shape
    tm = _row_tile(t)

    def body(x_ref, g_ref, *rest):
        o_ref = rest[-1]
        xv = x_ref[...]
        r = lax.rsqrt(jnp.mean(xv * xv, axis=-1, keepdims=True) + EPS)
        o_ref[...] = (xv * r * g_ref[...]).astype(o_ref.dtype)

    return pl.pallas_call(
        body, name=name, grid=(t // tm,),
        in_specs=[pl.BlockSpec((tm, d), lambda i: (i, 0)), pl.BlockSpec((1, d), lambda i: (0, 0))] + [ANY] * len(deps),
        out_specs=pl.BlockSpec((tm, d), lambda i: (i, 0)),
        out_shape=_sds((t, d), BF16),
        compiler_params=_cparams(("parallel",)),
    )(x, g, *deps)


def _rms_bwd_rows(dy, xv, g, n):
    r = lax.rsqrt(jnp.sum(xv * xv, axis=-1, keepdims=True) / n + EPS)
    xh = xv * r
    dxh = dy * g
    dx = r * (dxh - xh * (jnp.sum(dxh * xh, axis=-1, keepdims=True) / n))
    return dx, dy * xh


def _ple_gate_bwd(name, dh, gate, e):
    t, d = dh.shape
    tm = _row_tile(t)

    def body(dh_ref, g_ref, e_ref, de_ref, dz_ref):
        dh_v, gt = dh_ref[...], g_ref[...].astype(F32)
        de_ref[...] = (dh_v * gt).astype(BF16)
        dz_ref[...] = (dh_v * e_ref[...].astype(F32) * (gt * (1.0 - gt))).astype(BF16)

    row = pl.BlockSpec((tm, d), lambda i: (i, 0))
    return pl.pallas_call(
        body, name=name, grid=(t // tm,), in_specs=[row, row, row], out_specs=[row, row],
        out_shape=[_sds((t, d), BF16), _sds((t, d), BF16)],
        compiler_params=_cparams(("parallel",)),
    )(dh, gate, e)


def _rope_half(v, cos, sin):
    half = v.shape[-1] // 2
    v1, v2 = v[:, :half], v[:, half:]
    return jnp.concatenate([v1 * cos - v2 * sin, v2 * cos + v1 * sin], axis=-1)


def _ret_consts():
    lg = jnp.log(1.0 - 2.0 ** (-5.0 - jnp.arange(RET_HEADS, dtype=F32)))
    idx = jnp.arange(RET_BLOCK, dtype=F32)
    chunk = jnp.floor(idx / CHUNK)
    dist = idx[:, None] - idx[None, :]
    same = chunk[:, None] == chunk[None, :]
    seen = jnp.where(same, jnp.abs(dist), jnp.where(chunk[None, :] < chunk[:, None], dist, jnp.inf))
    intra = jnp.exp(lg[:, None, None] * seen)
    qdec = jnp.exp(lg[:, None] * (idx + 1.0))
    kdec = jnp.exp(lg[:, None] * (RET_BLOCK - 1.0 - idx))
    cdec = jnp.exp(lg * RET_BLOCK)
    qdec = jnp.broadcast_to(qdec[:, :, None], (RET_HEADS, RET_BLOCK, RET_DK))
    kdec = jnp.broadcast_to(kdec[:, :, None], (RET_HEADS, RET_BLOCK, RET_DK))
    cdec = jnp.broadcast_to(cdec[:, None, None], (RET_HEADS, 1, RET_DV))
    return intra, qdec, kdec, cdec


def _ret_specs(rb, rev_nb=None):
    blk = (lambda i: i) if rev_nb is None else (lambda i: rev_nb - 1 - i)
    full = lambda shape: pl.BlockSpec(shape, lambda i: (0,) * len(shape))
    return dict(
        proj=pl.BlockSpec((rb, RET_IN), lambda i: (blk(i), 0)),
        tab=pl.BlockSpec((rb, RET_DK // 2), lambda i: (blk(i), 0)),
        vw=pl.BlockSpec((rb, RET_V_W), lambda i: (blk(i), 0)),
        st=pl.BlockSpec((rb // RET_BLOCK, RET_HEADS, RET_DK, RET_DV), lambda i: (blk(i), 0, 0, 0)),
        gn=full((RET_HEADS, 1, RET_DV)),
        intra=full((RET_HEADS, RET_BLOCK, RET_BLOCK)),
        dec=full((RET_HEADS, RET_BLOCK, RET_DK)),
        cdec=full((RET_HEADS, 1, RET_DV)),
    )


def _ret_fwd(proj, cos, sin, gn):
    t = proj.shape[0]
    rb = min(RET_ROWS, t)
    cpb = rb // RET_BLOCK
    intra, qdec, kdec, cdec = _ret_consts()
    sp = _ret_specs(rb)

    def body(proj_ref, cos_ref, sin_ref, gn_ref, intra_ref, qd_ref, kd_ref, cd_ref,
             gated_ref, outp_ref, st_ref, s_ref):
        @pl.when(pl.program_id(0) == 0)
        def _():
            s_ref[...] = jnp.zeros_like(s_ref)

        def chunk(c, carry):
            rows = pl.ds(pl.multiple_of(c * RET_BLOCK, RET_BLOCK), RET_BLOCK)
            cs, sn = cos_ref[rows, :], sin_ref[rows, :]
            for h in range(RET_HEADS):
                q = proj_ref[rows, h * RET_DK:(h + 1) * RET_DK].astype(F32)
                k = proj_ref[rows, RET_QK_W + h * RET_DK:RET_QK_W + (h + 1) * RET_DK].astype(F32)
                v = proj_ref[rows, 2 * RET_QK_W + h * RET_DV:2 * RET_QK_W + (h + 1) * RET_DV]
                g = proj_ref[rows, 2 * RET_QK_W + RET_V_W + h * RET_DV:
                             2 * RET_QK_W + RET_V_W + (h + 1) * RET_DV].astype(F32)
                qr = _rope_half(q, cs, sn)
                kr = _rope_half(k, cs, sn) * (RET_DK ** -0.5)
                qb, kb, vb = qr.astype(BF16), kr.astype(BF16), v
                sc = _dot(qb, kb, 1, 1) * intra_ref[h]
                inner = _dot(sc.astype(BF16), vb, 1, 0)
                s_old = s_ref[h]
                sb = s_old.astype(BF16)
                st_ref[c, h] = sb
                cross = _dot((qr * qd_ref[h]).astype(BF16), sb, 1, 0)
                out = inner + cross
                s_ref[h] = s_old * cd_ref[h] + _dot((kr * kd_ref[h]).astype(BF16), vb, 0, 0)
                r = lax.rsqrt(jnp.mean(out * out, axis=-1, keepdims=True) + EPS)
                y = out * r * gn_ref[h]
                cols = slice(h * RET_DV, (h + 1) * RET_DV)
                gated_ref[rows, cols] = (g * _sigmoid(g) * y).astype(BF16)
                outp_ref[rows, cols] = out.astype(BF16)
            return carry

        lax.fori_loop(0, cpb, chunk, 0)

    return pl.pallas_call(
        body, name="ret_fwd", grid=(t // rb,),
        in_specs=[sp['proj'], sp['tab'], sp['tab'], sp['gn'], sp['intra'], sp['dec'], sp['dec'], sp['cdec']],
        out_specs=[sp['vw'], sp['vw'], sp['st']],
        out_shape=[_sds((t, RET_V_W), BF16), _sds((t, RET_V_W), BF16),
                   _sds((t // RET_BLOCK, RET_HEADS, RET_DK, RET_DV), BF16)],
        scratch_shapes=[pltpu.VMEM((RET_HEADS, RET_DK, RET_DV), F32)],
        compiler_params=_cparams(("arbitrary",)),
    )(proj, cos, sin, gn.reshape(RET_HEADS, 1, RET_DV), intra, qdec, kdec, cdec)


def _ret_gate_bwd_epi(dgt, out, g, gn):
    g = g.astype(F32)
    out = out.astype(F32)
    r = lax.rsqrt(jnp.mean(out * out, axis=-1, keepdims=True) + EPS)
    xh = out * r
    sg = _sigmoid(g)
    dgate = dgt * (xh * gn) * (sg * (1.0 + g * (1.0 - sg)))
    dy = dgt * (g * sg)
    dxh = dy * gn
    dout = r * (dxh - xh * jnp.mean(dxh * xh, axis=-1, keepdims=True))
    return dout, dgate, jnp.sum(dy * xh, axis=0, keepdims=True)


def _ret_bwd(proj, cos, sin, states, dout, dgate, deps=()):
    t = proj.shape[0]
    rb = min(RET_ROWS, t)
    cpb = rb // RET_BLOCK
    nb = t // rb
    intra, qdec, kdec, cdec = _ret_consts()
    sp = _ret_specs(rb, rev_nb=nb)

    def body(proj_ref, cos_ref, sin_ref, intra_ref, qd_ref, kd_ref, cd_ref, st_ref, dout_ref, dgate_ref, *rest):
        dproj_ref, ds_ref = rest[len(deps):]

        @pl.when(pl.program_id(0) == 0)
        def _():
            ds_ref[...] = jnp.zeros_like(ds_ref)

        def chunk(cc, carry):
            c = cpb - 1 - cc
            rows = pl.ds(pl.multiple_of(c * RET_BLOCK, RET_BLOCK), RET_BLOCK)
            cs, sn = cos_ref[rows, :], sin_ref[rows, :]
            for h in range(RET_HEADS):
                q = proj_ref[rows, h * RET_DK:(h + 1) * RET_DK].astype(F32)
                k = proj_ref[rows, RET_QK_W + h * RET_DK:RET_QK_W + (h + 1) * RET_DK].astype(F32)
                v = proj_ref[rows, 2 * RET_QK_W + h * RET_DV:2 * RET_QK_W + (h + 1) * RET_DV]
                cols = slice(h * RET_DV, (h + 1) * RET_DV)
                qr = _rope_half(q, cs, sn)
                kr = _rope_half(k, cs, sn) * (RET_DK ** -0.5)
                qb, kb, vb = qr.astype(BF16), kr.astype(BF16), v
                qdb = (qr * qd_ref[h]).astype(BF16)
                kdb = (kr * kd_ref[h]).astype(BF16)
                doutb = dout_ref[rows, cols]
                itr = intra_ref[h]
                pb = (_dot(qb, kb, 1, 1) * itr).astype(BF16)
                dv = _dot(pb, doutb, 0, 0)
                dsc = (_dot(doutb, vb, 1, 1) * itr).astype(BF16)
                dq = _dot(dsc, kb, 1, 0)
                dk = _dot(dsc, qb, 0, 0)
                dq = dq + _dot(doutb, st_ref[c, h], 1, 1) * qd_ref[h]
                ds_new = ds_ref[h]
                dsb = ds_new.astype(BF16)
                dk = dk + _dot(vb, dsb, 1, 1) * kd_ref[h]
                dv = dv + _dot(kdb, dsb, 1, 0)
                ds_ref[h] = ds_new * cd_ref[h] + _dot(qdb, doutb, 0, 0)
                dproj_ref[rows, h * RET_DK:(h + 1) * RET_DK] = _rope_half(dq, cs, -sn).astype(BF16)
                dproj_ref[rows, RET_QK_W + h * RET_DK:RET_QK_W + (h + 1) * RET_DK] = (
                    _rope_half(dk * (RET_DK ** -0.5), cs, -sn).astype(BF16))
                dproj_ref[rows, 2 * RET_QK_W + h * RET_DV:2 * RET_QK_W + (h + 1) * RET_DV] = dv.astype(BF16)
                dproj_ref[rows, 2 * RET_QK_W + RET_V_W + h * RET_DV:
                          2 * RET_QK_W + RET_V_W + (h + 1) * RET_DV] = dgate_ref[rows, cols]
            return carry

        lax.fori_loop(0, cpb, chunk, 0)

    return pl.pallas_call(
        body, name="ret_bwd", grid=(nb,),
        in_specs=[sp['proj'], sp['tab'], sp['tab'], sp['intra'], sp['dec'], sp['dec'], sp['cdec'],
                  sp['st'], sp['vw'], sp['vw']] + [ANY] * len(deps),
        out_specs=sp['proj'],
        out_shape=_sds((t, RET_IN), BF16),
        scratch_shapes=[pltpu.VMEM((RET_HEADS, RET_DK, RET_DV), F32)],
        compiler_params=_cparams(("arbitrary",)),
    )(proj, cos, sin, intra, qdec, kdec, cdec, states, dout, dgate, *deps)


def _spread_rope(a):
    return jnp.pad(a, [(0, 0)] * (a.ndim - 1) + [(0, MLA_ROPE)])


def _gather_rope(a):
    return a[..., :a.shape[-1] - MLA_ROPE]


def _rope_tables(t, zero):
    pos = jnp.arange(t, dtype=F32)[:, None] + zero
    inv = 1.0 / (ROPE_THETA ** (jnp.arange(0, RET_DK, 2, dtype=F32) / RET_DK))
    ang = pos * inv[None, :]
    return jnp.cos(ang), jnp.sin(ang), _mla_tables(t, pos)


def _mla_tables(t, pos):
    half = MLA_ROPE // 2
    inv = 1.0 / (ROPE_THETA ** (jnp.arange(0, MLA_ROPE, 2, dtype=F32) / MLA_ROPE))
    ang = pos * inv[None, :]
    cos, sin = jnp.cos(ang), jnp.sin(ang)
    z = jnp.zeros((t, half), F32)
    c = jnp.concatenate([cos, cos, z, z], axis=1)
    s1 = jnp.concatenate([-sin, z, z, z], axis=1)
    s2 = jnp.concatenate([z, sin, z, z], axis=1)
    return c, s1, s2


def _rope_tile(r, c, s1, s2):
    return r * c + pltpu.roll(r, 96, 1) * s1 + pltpu.roll(r, 32, 1) * s2


def _mla_mid(proj2, qa, kva):
    t = proj2.shape[0]
    tm = _row_tile(t)

    def body(p_ref, qa_ref, kva_ref, cq_ref, ckv_ref):
        cq = p_ref[:, :MLA_Q_RANK]
        ckv = p_ref[:, MLA_Q_RANK:MLA_Q_RANK + MLA_KV_RANK]
        rq = lax.rsqrt(jnp.mean(cq * cq, axis=-1, keepdims=True) + EPS)
        rkv = lax.rsqrt(jnp.mean(ckv * ckv, axis=-1, keepdims=True) + EPS)
        cq_ref[...] = (cq * rq * qa_ref[...]).astype(BF16)
        ckv_ref[...] = (ckv * rkv * kva_ref[...]).astype(BF16)

    return pl.pallas_call(
        body, name="mla_mid", grid=(t // tm,),
        in_specs=[pl.BlockSpec((tm, MLA_IN_PAD), lambda i: (i, 0)),
                  pl.BlockSpec((1, MLA_Q_RANK), lambda i: (0, 0)),
                  pl.BlockSpec((1, MLA_KV_RANK), lambda i: (0, 0))],
        out_specs=[pl.BlockSpec((tm, MLA_Q_RANK), lambda i: (i, 0)),
                   pl.BlockSpec((tm, MLA_KV_RANK), lambda i: (i, 0))],
        out_shape=[_sds((t, MLA_Q_RANK), BF16), _sds((t, MLA_KV_RANK), BF16)],
        compiler_params=_cparams(("parallel",)),
    )(proj2, qa, kva)


def _mla_mid_bwd(proj2, qa, kva, dcq, dckv, dkr):
    t = proj2.shape[0]
    tm = _row_tile(t)

    def body(p_ref, qa_ref, kva_ref, dcq_ref, dckv_ref, dkr_ref, dp_ref, dqa_ref, dkva_ref):
        @pl.when(pl.program_id(0) == 0)
        def _():
            dqa_ref[...] = jnp.zeros_like(dqa_ref)
            dkva_ref[...] = jnp.zeros_like(dkva_ref)

        dxq, dgq = _rms_bwd_rows(dcq_ref[...], p_ref[:, :MLA_Q_RANK], qa_ref[...], MLA_Q_RANK)
        dxk, dgk = _rms_bwd_rows(dckv_ref[...], p_ref[:, MLA_Q_RANK:MLA_Q_RANK + MLA_KV_RANK], kva_ref[...],
                                 MLA_KV_RANK)
        dp_ref[:, :MLA_Q_RANK] = dxq.astype(BF16)
        dp_ref[:, MLA_Q_RANK:MLA_Q_RANK + MLA_KV_RANK] = dxk.astype(BF16)
        dp_ref[:, MLA_Q_RANK + MLA_KV_RANK:] = dkr_ref[...].astype(BF16)
        dqa_ref[...] += jnp.sum(dgq, axis=0, keepdims=True)
        dkva_ref[...] += jnp.sum(dgk, axis=0, keepdims=True)

    return pl.pallas_call(
        body, name="mla_mid_bwd", grid=(t // tm,),
        in_specs=[pl.BlockSpec((tm, MLA_IN_PAD), lambda i: (i, 0)),
                  pl.BlockSpec((1, MLA_Q_RANK), lambda i: (0, 0)),
                  pl.BlockSpec((1, MLA_KV_RANK), lambda i: (0, 0)),
                  pl.BlockSpec((tm, MLA_Q_RANK), lambda i: (i, 0)),
                  pl.BlockSpec((tm, MLA_KV_RANK), lambda i: (i, 0)),
                  pl.BlockSpec((tm, 128), lambda i: (i, 0))],
        out_specs=[pl.BlockSpec((tm, MLA_IN_PAD), lambda i: (i, 0)),
                   pl.BlockSpec((1, MLA_Q_RANK), lambda i: (0, 0)),
                   pl.BlockSpec((1, MLA_KV_RANK), lambda i: (0, 0))],
        out_shape=[_sds((t, MLA_IN_PAD), BF16), _sds((1, MLA_Q_RANK), F32), _sds((1, MLA_KV_RANK), F32)],
        compiler_params=_cparams(("arbitrary",)),
    )(proj2, qa, kva, dcq, dckv, dkr)


def _mla_prep_specs(t, tm):
    head = lambda w: pl.BlockSpec((None, tm, w), lambda i, h: (h, i, 0))
    return dict(
        head256=head(MLA_HD_PAD), head128=head(MLA_VD),
        cols256=pl.BlockSpec((tm, MLA_HD_PAD), lambda i, h: (i, h)),
        cq=pl.BlockSpec((tm, MLA_Q_RANK), lambda i, h: (i, 0)),
        ckv=pl.BlockSpec((tm, MLA_KV_RANK), lambda i, h: (i, 0)),
        wuq=pl.BlockSpec((None, MLA_Q_RANK, MLA_HD_PAD), lambda i, h: (h, 0, 0)),
        wukv=pl.BlockSpec((None, MLA_KV_RANK, MLA_HD_PAD), lambda i, h: (h, 0, 0)),
        kr=pl.BlockSpec((tm, 128), lambda i, h: (i, (MLA_Q_RANK + MLA_KV_RANK) // 128)),
        gain=pl.BlockSpec((1, MLA_HD_PAD), lambda i, h: (0, 0)),
        tab=pl.BlockSpec((tm, 128), lambda i, h: (i, 0)),
    )


def _mla_prep(cq, ckv, wuq, wukv, proj2, gq, gk, tabs):
    t = cq.shape[0]
    tm = _row_tile(t, PREP_ROWS)
    sp = _mla_prep_specs(t, tm)

    def body(cq_ref, ckv_ref, wuq_ref, wukv_ref, kr_ref, gq_ref, gk_ref, c_ref, s1_ref, s2_ref,
             qh_ref, kh_ref, vh_ref):
        c, s1, s2 = c_ref[...], s1_ref[...], s2_ref[...]

        def norm_rope(xv, gain):
            r = lax.rsqrt(jnp.sum(xv * xv, axis=-1, keepdims=True) / MLA_QKD + EPS)
            y = xv * r * gain
            return jnp.concatenate([y[:, :MLA_NOPE], _rope_tile(y[:, MLA_NOPE:], c, s1, s2)], axis=-1)

        kvv = _dot(ckv_ref[...], wukv_ref[...], 1, 0)
        qh_ref[...] = norm_rope(_dot(cq_ref[...], wuq_ref[...], 1, 0), gq_ref[...]).astype(BF16)
        kf = jnp.concatenate([kvv[:, :MLA_NOPE], kr_ref[...]], axis=-1)
        kh_ref[...] = norm_rope(kf, gk_ref[...]).astype(BF16)
        vh_ref[...] = jnp.concatenate([kvv[:, MLA_NOPE:], jnp.ones((tm, MLA_VD), F32)], axis=-1).astype(BF16)

    return pl.pallas_call(
        body, name="mla_prep", grid=(t // tm, MLA_HEADS),
        in_specs=[sp['cq'], sp['ckv'], sp['wuq'], sp['wukv'], sp['kr'], sp['gain'], sp['gain'],
                  sp['tab'], sp['tab'], sp['tab']],
        out_specs=[sp['head256'], sp['head256'], sp['head256']],
        out_shape=[_sds((MLA_HEADS, t, MLA_HD_PAD), BF16), _sds((MLA_HEADS, t, MLA_HD_PAD), BF16),
                   _sds((MLA_HEADS, t, 2 * MLA_VD), BF16)],
        compiler_params=_cparams(("parallel", "arbitrary")),
    )(cq, ckv, wuq, wukv, proj2, gq, gk, *tabs)


def _mla_prep_bwd(cq, ckv, wuq, wukv, proj2, gq, gk, tabs, dqt, dkh, dvh):
    t = cq.shape[0]
    tm = _row_tile(t, PREP_ROWS)
    ab = dqt.shape[-1]
    sp = _mla_prep_specs(t, tm)

    def body(cq_ref, ckv_ref, wuq_ref, wukv_ref, kr_ref, gq_ref, gk_ref, c_ref, s1_ref, s2_ref,
             dqt_ref, dkh_ref, dvh_ref, dq_ref, dkv_ref, dkr_ref, dgq_ref, dgk_ref):
        dqh = jnp.concatenate([dqt_ref[b].T for b in range(tm // ab)], axis=0)
        i, h = pl.program_id(0), pl.program_id(1)

        @pl.when((i == 0) & (h == 0))
        def _():
            dgq_ref[...] = jnp.zeros_like(dgq_ref)
            dgk_ref[...] = jnp.zeros_like(dgk_ref)

        @pl.when(h == 0)
        def _():
            dkr_ref[...] = jnp.zeros_like(dkr_ref)

        c, s1, s2 = c_ref[...], s1_ref[...], s2_ref[...]

        def back(xv, gain, dout):
            dy = jnp.concatenate([dout[:, :MLA_NOPE], _rope_tile(dout[:, MLA_NOPE:], c, -s1, -s2)], axis=-1)
            return _rms_bwd_rows(dy, xv, gain, MLA_QKD)

        kvv = _dot(ckv_ref[...], wukv_ref[...], 1, 0)
        dxq, dgq = back(_dot(cq_ref[...], wuq_ref[...], 1, 0), gq_ref[...], dqh)
        kf = jnp.concatenate([kvv[:, :MLA_NOPE], kr_ref[...]], axis=-1)
        dxk, dgk = back(kf, gk_ref[...], dkh_ref[...])
        dq_ref[...] = dxq.astype(BF16)
        dkv_ref[...] = jnp.concatenate([dxk[:, :MLA_NOPE], dvh_ref[...]], axis=-1).astype(BF16)
        dkr_ref[...] += dxk[:, MLA_NOPE:]
        dgq_ref[...] += jnp.sum(dgq, axis=0, keepdims=True)
        dgk_ref[...] += jnp.sum(dgk, axis=0, keepdims=True)

    return pl.pallas_call(
        body, name="mla_prep_bwd", grid=(t // tm, MLA_HEADS),
        in_specs=[sp['cq'], sp['ckv'], sp['wuq'], sp['wukv'], sp['kr'], sp['gain'], sp['gain'],
                  sp['tab'], sp['tab'], sp['tab'],
                  pl.BlockSpec((None, tm // ab, MLA_HD_PAD, ab), lambda i, h: (h, i, 0, 0)),
                  sp['head256'], sp['head128']],
        out_specs=[sp['cols256'], sp['cols256'], sp['tab'], sp['gain'], sp['gain']],
        out_shape=[_sds((t, MLA_HEADS * MLA_HD_PAD), BF16), _sds((t, MLA_HEADS * MLA_HD_PAD), BF16),
                   _sds((t, 128), F32), _sds((1, MLA_HD_PAD), F32), _sds((1, MLA_HD_PAD), F32)],
        compiler_params=_cparams(("arbitrary", "arbitrary")),
    )(cq, ckv, wuq, wukv, proj2, gq, gk, *tabs, dqt, dkh, dvh)


def _chunk_visible(rows, cols, row_off, col_off):
    rq = lax.shift_right_logical(lax.broadcasted_iota(jnp.int32, (rows, cols), 0) + row_off, 6)
    ck = lax.shift_right_logical(lax.broadcasted_iota(jnp.int32, (rows, cols), 1) + col_off, 6)
    return ck <= rq


def _rows_to_lanes(col):
    return col.T[:8, :]


def _attn_fwd(qh, kh, vh):
    t = qh.shape[1]
    ab = min(ATT_BLOCK, t)
    tq = min(ATT_QROWS, t)
    r = tq // ab
    hg = ATT_HEADS

    def body(q_ref, k_ref, v_ref, o_ref, lse_ref, acc_ref):
        n_un = pl.program_id(1) * r
        acc_ref[...] = jnp.zeros_like(acc_ref)

        def step(b, ms, diag):
            rows = pl.ds(pl.multiple_of(b * ab, ab), ab)
            out = []
            for hh in range(hg):
                m = ms[hh]
                s = _dot(q_ref[hh], k_ref[hh, rows, :], 1, 1)
                if diag is not None:
                    s = jnp.where(_chunk_visible(tq, ab, 0, diag * ab), s, -1e30)
                m_new = jnp.maximum(m, jnp.max(s, axis=-1, keepdims=True))
                p = jnp.exp2((s - m_new) * ATT_EXP2).astype(BF16)
                acc_ref[hh] = jnp.exp2((m - m_new) * ATT_EXP2) * acc_ref[hh] + _dot(p, v_ref[hh, rows, :], 1, 0)
                out.append(m_new)
            return tuple(out)

        ms = tuple(jnp.full((tq, 1), -1e30, F32) for _ in range(hg))
        ms = lax.fori_loop(0, n_un, lambda b, st: step(b, st, None), ms)
        for d in range(r):
            ms = step(n_un + d, ms, d)
        for hh in range(hg):
            l = acc_ref[hh, :, MLA_VD:]
            o_ref[:, hh * MLA_VD:(hh + 1) * MLA_VD] = (acc_ref[hh, :, :MLA_VD] / l).astype(BF16)
            lse_t = _rows_to_lanes(ms[hh] * ATT_EXP2 + jnp.log(l) * LOG2E)
            for d in range(r):
                lse_ref[hh, d] = lse_t[:, d * ab:(d + 1) * ab]

    return pl.pallas_call(
        body, name="mla_attn", grid=(MLA_HEADS // hg, t // tq),
        in_specs=[pl.BlockSpec((hg, tq, MLA_HD_PAD), lambda g, i: (g, i, 0)),
                  pl.BlockSpec((hg, t, MLA_HD_PAD), lambda g, i: (g, 0, 0)),
                  pl.BlockSpec((hg, t, 2 * MLA_VD), lambda g, i: (g, 0, 0))],
        out_specs=[pl.BlockSpec((tq, hg * MLA_VD), lambda g, i: (i, g)),
                   pl.BlockSpec((hg, r, 8, ab), lambda g, i: (g, i, 0, 0))],
        out_shape=[_sds((t, MLA_HEADS * MLA_VD), BF16), _sds((MLA_HEADS, t // ab, 8, ab), F32)],
        scratch_shapes=[pltpu.VMEM((hg, tq, 2 * MLA_VD), F32)],
        compiler_params=_cparams(("parallel", "arbitrary")),
    )(qh, kh, vh)


def _attn_bwd(qh, kh, vh, dob, o, lse_t):
    t = qh.shape[1]
    ab = min(ATT_BLOCK, t)
    kb = min(ATT_KROWS, t)
    r = kb // ab
    nq = t // ab
    hg = ATT_HEADS

    def body(q_ref, k_ref, v_ref, do_ref, o_ref, lse_ref, dqt_ref, dk_ref, dv_ref, dl_ref):
        j = pl.program_id(1)

        @pl.when(j == 0)
        def _():
            dqt_ref[...] = jnp.zeros_like(dqt_ref)
            ones = jnp.ones((8, MLA_VD), F32)

            def delta(b, carry):
                rows = pl.ds(pl.multiple_of(b * ab, ab), ab)
                for hh in range(hg):
                    cols = slice(hh * MLA_VD, (hh + 1) * MLA_VD)
                    prod = do_ref[rows, cols].astype(F32) * o_ref[rows, cols].astype(F32)
                    dl_ref[hh, b] = lax.dot_general(ones, prod, (((1,), (1,)), ((), ())),
                                                    precision=lax.Precision.HIGHEST, preferred_element_type=F32)
                return carry

            lax.fori_loop(0, nq, delta, 0)

        ks = [k_ref[hh] for hh in range(hg)]
        vs = [v_ref[hh, :, :MLA_VD] for hh in range(hg)]
        kts = [k.T for k in ks]

        dk_ref[...] = jnp.zeros_like(dk_ref)
        dv_ref[...] = jnp.zeros_like(dv_ref)

        def step(b, carry, diag):
            rows = pl.ds(pl.multiple_of(b * ab, ab), ab)
            hi = kb if diag is None else (diag + 1) * ab
            for hh in range(hg):
                q = q_ref[hh, rows, :]
                do = do_ref[rows, hh * MLA_VD:(hh + 1) * MLA_VD]
                s_t = _dot(ks[hh][:hi], q, 1, 1)
                if diag is not None:
                    key_chunk = lax.shift_right_logical(lax.broadcasted_iota(jnp.int32, (hi, ab), 0), 6)
                    query_chunk = lax.shift_right_logical(
                        lax.broadcasted_iota(jnp.int32, (hi, ab), 1) + diag * ab, 6)
                    s_t = jnp.where(key_chunk <= query_chunk, s_t, -1e30)
                p_t = jnp.exp2(s_t * ATT_EXP2 - lse_ref[hh, b][0:1, :])
                dp_t = _dot(vs[hh][:hi], do, 1, 1)
                ds_t = (p_t * (dp_t - dl_ref[hh, b][0:1, :]) * ATT_SCALE).astype(BF16)
                dqt_ref[hh, b] += _dot(kts[hh][:, :hi], ds_t, 1, 0)
                dk_ref[hh, :hi] += _dot(ds_t, q, 1, 0)
                dv_ref[hh, :hi] += _dot(p_t.astype(BF16), do, 1, 0)
            return carry

        for d in range(r):
            step(j * r + d, 0, d)
        lax.fori_loop((j + 1) * r, nq, lambda b, c: step(b, c, None), 0)

    whole = lambda w: pl.BlockSpec((hg, t, w), lambda g, j: (g, 0, 0))
    blk = lambda w: pl.BlockSpec((hg, kb, w), lambda g, j: (g, j, 0))
    stat = pl.BlockSpec((hg, nq, 8, ab), lambda g, j: (g, 0, 0, 0))
    cols = pl.BlockSpec((t, hg * MLA_VD), lambda g, j: (0, g))
    return pl.pallas_call(
        body, name="mla_attn_bwd", grid=(MLA_HEADS // hg, t // kb),
        in_specs=[whole(MLA_HD_PAD), blk(MLA_HD_PAD), blk(2 * MLA_VD),
                  cols, cols, stat],
        out_specs=[pl.BlockSpec((hg, nq, MLA_HD_PAD, ab), lambda g, j: (g, 0, 0, 0)), blk(MLA_HD_PAD), blk(MLA_VD)],
        out_shape=[_sds((MLA_HEADS, nq, MLA_HD_PAD, ab), F32), _sds((MLA_HEADS, t, MLA_HD_PAD), F32),
                   _sds((MLA_HEADS, t, MLA_VD), F32)],
        scratch_shapes=[pltpu.VMEM((hg, nq, 8, ab), F32)],
        compiler_params=_cparams(("parallel", "arbitrary")),
    )(qh, kh, vh, dob, o, lse_t)


VEC = pl.BlockSpec((1, D_MODEL), lambda i, j, k: (0, 0))


def _rows(tm, width):
    return pl.BlockSpec((tm, width), lambda i, j, k: (i, 0))


def _residual_epi(next_gain):
    if next_gain is None:
        return [], lambda acc, hv: (acc + hv,)

    def epi(acc, hv, g):
        h_new = acc + hv
        r = lax.rsqrt(jnp.mean(h_new * h_new, axis=-1, keepdims=True) + EPS)
        return h_new, h_new * r * g

    return [(next_gain, VEC)], epi


def _residual_outs(t, row, next_gain):
    outs = [(_sds((t, D_MODEL), F32), row)]
    return outs + ([(_sds((t, D_MODEL), BF16), row)] if next_gain is not None else [])


def _mlp_fwd(l, h, hn, w1g, fetch_w2, next_gain):
    t = h.shape[0]
    tm = _row_tile(t, 512)

    def relu2(acc):
        r = jnp.maximum(acc, 0.0)
        return (r * r,)

    (u,) = _mm_rows(f"mlp_up{l}", tm, hn, w1g, 'nn_cols', [(_sds((t, D_FF), BF16), _rows(tm, D_FF))], epi=relu2)
    w2g = fetch_w2((u,))
    row = _rows(tm, D_MODEL)
    more, epi = _residual_epi(next_gain)
    h2, hn_next = _mm_rows(f"mlp_down{l}", tm, u, w2g, 'nn_rows', _residual_outs(t, row, next_gain),
                           extras=[(h, row)] + more, epi=epi)
    return h2, hn_next, (h, hn, u, w1g, w2g)


def _norm_bwd_outs(t, tm):
    return [(_sds((t, D_MODEL), F32), pl.BlockSpec((tm, D_MODEL), lambda i, j, k: (i, 0))),
            (_sds((t // tm, 1, D_MODEL), F32), pl.BlockSpec((None, 1, D_MODEL), lambda i, j, k: (i, 0, 0)))]


def _norm_bwd_epi(acc, xv, res, g):
    dx, dgr = _rms_bwd_rows(acc, xv, g, D_MODEL)
    return res + dx, jnp.sum(dgr, axis=0, keepdims=True)


def _mlp_bwd(l, dh, saved, norm_g, emit_w2=None, emit_w1=None):
    h, hn, u, w1g, w2g = saved
    t = h.shape[0]
    tm = _row_tile(t, 512)
    nsh, _, wsh = w1g.shape
    wide = _rows(tm, D_FF)
    (da,) = _mm_rows(f"mlp_du{l}", tm, dh, w2g, 'nt_rows', [(_sds((t, D_FF), BF16), wide)], extras=[(u, wide)],
                     epi=lambda acc, uv: (2.0 * jnp.sqrt(uv.astype(F32)) * acc,))
    tw = _row_tile(t, 512)
    (dw2,) = _mm(f"mlp_dw2{l}", (1, 1, t // tw),
                 u, pl.BlockSpec((tw, D_FF), lambda i, j, k: (k, 0)),
                 dh, pl.BlockSpec((tw, D_MODEL), lambda i, j, k: (k, 0)), (0, 0),
                 [(_sds((D_FF, D_MODEL), BF16), pl.BlockSpec((D_FF, D_MODEL), lambda i, j, k: (0, 0)))])
    dw2 = dw2.reshape(nsh, wsh, D_MODEL)
    (dw1,) = _mm(f"mlp_dw1{l}", (1, 1, t // tw),
                 hn, pl.BlockSpec((tw, D_MODEL), lambda i, j, k: (k, 0)),
                 da, pl.BlockSpec((tw, D_FF), lambda i, j, k: (k, 0)), (0, 0),
                 [(_sds((nsh, D_MODEL, wsh), BF16), pl.BlockSpec((nsh, D_MODEL, wsh), lambda i, j, k: (0, 0, 0)))],
                 split=wsh, deps=emit_w2(dw2) if emit_w2 else ())
    row = _rows(tm, D_MODEL)
    dh_in, dg = _mm_rows(f"mlp_dhn{l}", tm, da, w1g, 'nt_cols', _norm_bwd_outs(t, tm),
                         extras=[(h, row), (dh, row), (norm_g, VEC)], epi=_norm_bwd_epi,
                         deps=emit_w1(dw1) if emit_w1 else ())
    return dh_in, jnp.sum(dg, axis=0), dw1, dw2


def _ple_fwd(l, h, hn, p, wg, wp, next_gain, target=None):
    t = h.shape[0]
    tm = _row_tile(t, 512)
    row = pl.BlockSpec((tm, D_MODEL), lambda i, j, k: (i, 0))
    full = lambda r: pl.BlockSpec((r, D_MODEL), lambda i, j, k: (0, 0))
    f32_row, bf_row = (_sds((t, D_MODEL), F32), row), (_sds((t, D_MODEL), BF16), row)
    common = [(h, row), (p, pl.BlockSpec((None, None, tm, PLE_DIM), lambda i, j, k: (l, 0, i, 0))),
              (wp, full(PLE_DIM))]
    if target is not None:
        def loss_epi(acc, hv, pv, wpv, tv):
            gt = _sigmoid(acc)
            ev = _dot(_bf(pv), wpv, 1, 0)
            err = hv + gt * ev - tv
            sq = jnp.sum(jnp.sum(err * err, axis=-1, keepdims=True), axis=0, keepdims=True)
            return err / D_MODEL, gt, ev, jnp.broadcast_to(sq, (8, 128))

        dy, gate, e, sq = _mm(f"ple_gate{l}", (t // tm, 1, 1), hn, row, wg, full(D_MODEL), (1, 0),
                              [f32_row, bf_row, bf_row, (_sds((t // tm, 8, 128), F32),
                                                         pl.BlockSpec((None, 8, 128), lambda i, j, k: (i, 0, 0)))],
                              extras=common + [(target, row)], epi=loss_epi)
        return dy, jnp.sum(sq, axis=0), (h, hn, gate, e)

    def gate_epi(acc, hv, pv, wpv, *gain):
        gt = _sigmoid(acc)
        ev = _dot(_bf(pv), wpv, 1, 0)
        h_new = hv + gt * ev
        if not gain:
            return h_new, gt, ev
        r = lax.rsqrt(jnp.mean(h_new * h_new, axis=-1, keepdims=True) + EPS)
        return h_new, gt, ev, h_new * r * gain[0]

    res = _mm(f"ple_gate{l}", (t // tm, 1, 1), hn, row, wg, full(D_MODEL), (1, 0),
              [f32_row, bf_row, bf_row] + ([bf_row] if next_gain is not None else []),
              extras=common + ([(next_gain, VEC)] if next_gain is not None else []), epi=gate_epi)
    h_out, gate, e = res[0], res[1], res[2]
    return h_out, (res[3] if next_gain is not None else None), (h, hn, gate, e)


def _ple_bwd(l, dh, saved, p, norm_g, wg, deps=(), emit=None):
    h, hn, gate, e = saved
    t = h.shape[0]
    tm = _row_tile(t)
    tk = _row_tile(t, 512)
    de, dz = _ple_gate_bwd(f"ple_gate_bwd{l}", dh, gate, e)
    full = lambda r: pl.BlockSpec((r, D_MODEL), lambda i, j, k: (0, 0))
    rowk = pl.BlockSpec((tk, D_MODEL), lambda i, j, k: (k, 0))
    (dwp,) = _mm(f"ple_dwp{l}", (1, 1, t // tk),
                 p, pl.BlockSpec((None, None, tk, PLE_DIM), lambda i, j, k: (l, 0, k, 0)),
                 de, rowk, (0, 0), [(_sds((PLE_DIM, D_MODEL), BF16), full(PLE_DIM))], deps=deps)
    (dwg,) = _mm(f"ple_dwg{l}", (1, 1, t // tk), hn, rowk, dz, rowk, (0, 0),
                 [(_sds((D_MODEL, D_MODEL), BF16), full(D_MODEL))])
    row = pl.BlockSpec((tm, D_MODEL), lambda i, j, k: (i, 0))
    dh_in, dg = _mm(f"ple_dhn{l}", (t // tm, 1, 1), dz, row, wg, full(D_MODEL), (1, 1),
                    _norm_bwd_outs(t, tm), extras=[(h, row), (dh, row), (norm_g, VEC)], epi=_norm_bwd_epi,
                    deps=emit(dwg, dwp) if emit else ())
    return dh_in, jnp.sum(dg, axis=0), dwg, dwp


def _ret_layer_fwd(x, norm_g, wri, fetch_wro, gn, cos, sin, next_gain, hn=None, deps=()):
    t = x.shape[0]
    tm = _row_tile(t)
    nsh, _, wsh = wri.shape
    if hn is None:
        hn = _rms_fwd("mix_norm0", x, norm_g)
    tp = _row_tile(t, 512)
    (proj,) = _mm_rows("ret_in", tp, hn, wri, 'nn_cols', [(_sds((t, RET_IN), BF16), _rows(tp, RET_IN))], deps=deps)
    gated, outp, states = _ret_fwd(proj, cos, sin, gn)
    wro = fetch_wro((gated,))
    row = _rows(tp, D_MODEL)
    more, epi = _residual_epi(next_gain)
    h1, hn_next = _mm_rows("ret_out", tp, gated, wro.reshape(RET_HEADS, RET_DV, D_MODEL), 'nn_rows',
                           _residual_outs(t, row, next_gain), extras=[(x, row)] + more, epi=epi)
    return h1, hn_next, (x, hn, proj, gated, outp, states, wro)


def _ret_layer_bwd(dh, saved, norm_g, wri, gn, cos, sin, emit_out, emit_in, deps=()):
    x, hn, proj, gated, outp, states, wro = saved
    t = x.shape[0]
    tm = _row_tile(t)
    tk = _row_tile(t, 512)
    nsh, _, wsh = wri.shape
    tg = _row_tile(t, 512)
    vw = _rows(tg, RET_V_W)
    dout, dgate, dgn = _mm_rows(
        "ret_dgate", tg, dh, wro.reshape(RET_HEADS, RET_DV, D_MODEL), 'nt_rows',
        [(_sds((t, RET_V_W), BF16), vw), (_sds((t, RET_V_W), BF16), vw),
         (_sds((t // tg, 1, RET_V_W), F32), pl.BlockSpec((None, 1, RET_V_W), lambda i, j, k: (i, 0, 0)))],
        extras=[(outp, vw), (proj, pl.BlockSpec((tg, RET_V_W), lambda i, j, k: (i, (RET_IN - RET_V_W) // RET_V_W))),
                (gn.reshape(1, RET_V_W), pl.BlockSpec((1, RET_V_W), lambda i, j, k: (0, 0)))],
        epi=_ret_gate_bwd_epi, deps=deps)
    dgn = jnp.sum(dgn, axis=0)
    (dwro,) = _mm("ret_dwro", (1, 1, t // tk),
                  gated, pl.BlockSpec((tk, RET_V_W), lambda i, j, k: (k, 0)),
                  dh, pl.BlockSpec((tk, D_MODEL), lambda i, j, k: (k, 0)), (0, 0),
                  [(_sds((RET_V_W, D_MODEL), BF16), pl.BlockSpec((RET_V_W, D_MODEL), lambda i, j, k: (0, 0)))])
    dproj = _ret_bwd(proj, cos, sin, states, dout, dgate, deps=emit_out(dwro))
    half = nsh // 2
    (dwri,) = _mm("ret_dwri", (2, 1, t // tk),
                  hn, pl.BlockSpec((tk, D_MODEL), lambda i, j, k: (k, 0)),
                  dproj, pl.BlockSpec((tk, half * wsh), lambda i, j, k: (k, i)), (0, 0),
                  [(_sds((nsh, D_MODEL, wsh), BF16), pl.BlockSpec((half, D_MODEL, wsh), lambda i, j, k: (i, 0, 0)))],
                  split=wsh)
    deps = emit_in(dwri)
    td = _row_tile(t, 256)
    row = _rows(td, D_MODEL)
    dx, dg = _mm_rows("ret_dhn", td, dproj, wri, 'nt_cols', _norm_bwd_outs(t, td),
                      extras=[(x, row), (dh, row), (norm_g, VEC)], epi=_norm_bwd_epi, deps=deps)
    return dx, jnp.sum(dg, axis=0), dgn.reshape(RET_HEADS, RET_DV)


def _mla_layer_fwd(h, hn, fetch, qa, kva, gq, gk, tabs, next_gain):
    t = h.shape[0]
    tm = _row_tile(t)
    row = pl.BlockSpec((tm, D_MODEL), lambda i, j, k: (i, 0))
    wmi = fetch('mla_in', (h,))['mla_w_in']
    (proj2,) = _mm("mla_in", (t // tm, 1, 1), hn, row,
                   wmi, pl.BlockSpec((D_MODEL, MLA_IN_PAD), lambda i, j, k: (0, 0)), (1, 0),
                   [(_sds((t, MLA_IN_PAD), F32), pl.BlockSpec((tm, MLA_IN_PAD), lambda i, j, k: (i, 0)))])
    cq, ckv = _mla_mid(proj2, qa, kva)
    up = fetch('mla_up', (cq,))
    wuq, wukv = up['mla_w_uq'], up['mla_w_ukv']
    qh, kh, vh = _mla_prep(cq, ckv, wuq, wukv, proj2, gq, gk, tabs)
    o, lse = _attn_fwd(qh, kh, vh)
    wmo = fetch('mla_out', (o,))['mla_w_out']
    more, epi = _residual_epi(next_gain)
    h_out, hn_next = _mm("mla_out", (t // tm, 1, 1), o, row,
                         wmo, pl.BlockSpec((D_MODEL, D_MODEL), lambda i, j, k: (0, 0)), (1, 0),
                         _residual_outs(t, row, next_gain), extras=[(h, row)] + more, epi=epi)
    return h_out, hn_next, (h, hn, proj2, cq, ckv, qh, kh, vh, o, lse), (wmi, wuq, wukv, wmo)


def _mla_layer_bwd(dh, saved, norm_g, wmi, qa, kva, wuq, wukv, gq, gk, wmo, tabs, deps=()):
    h, hn, proj2, cq, ckv, qh, kh, vh, o, lse = saved
    t = h.shape[0]
    tm = _row_tile(t)
    tk = _row_tile(t, 512)
    row = pl.BlockSpec((tm, D_MODEL), lambda i, j, k: (i, 0))
    rowk = pl.BlockSpec((tk, D_MODEL), lambda i, j, k: (k, 0))
    sq = pl.BlockSpec((D_MODEL, D_MODEL), lambda i, j, k: (0, 0))
    (dob,) = _mm("mla_do", (t // tm, 1, 1), dh, row, wmo, sq, (1, 1), [(_sds((t, D_MODEL), BF16), row)], deps=deps)
    (dwmo,) = _mm("mla_dwo", (1, 1, t // tk), o, rowk, dh, rowk, (0, 0), [(_sds((D_MODEL, D_MODEL), BF16), sq)])
    dqt, dkh, dvh = _attn_bwd(qh, kh, vh, dob, o, lse)
    dq, dkv, dkr, dgq, dgk = _mla_prep_bwd(cq, ckv, wuq, wukv, proj2, gq, gk, tabs, dqt, dkh, dvh)

    wide = MLA_HEADS * MLA_HD_PAD
    widek = pl.BlockSpec((tk, wide), lambda i, j, k: (k, 0))
    (dwuq,) = _mm("mla_dwuq", (1, 1, t // tk),
                  cq, pl.BlockSpec((tk, MLA_Q_RANK), lambda i, j, k: (k, 0)), dq, widek, (0, 0),
                  [(_sds((MLA_HEADS, MLA_Q_RANK, MLA_HD_PAD), BF16),
                    pl.BlockSpec((MLA_HEADS, MLA_Q_RANK, MLA_HD_PAD), lambda i, j, k: (0, 0, 0)))], split=MLA_HD_PAD)
    (dwukv,) = _mm("mla_dwukv", (1, 1, t // tk),
                   ckv, pl.BlockSpec((tk, MLA_KV_RANK), lambda i, j, k: (k, 0)), dkv, widek, (0, 0),
                   [(_sds((MLA_HEADS, MLA_KV_RANK, MLA_HD_PAD), BF16),
                     pl.BlockSpec((MLA_HEADS, MLA_KV_RANK, MLA_HD_PAD), lambda i, j, k: (0, 0, 0)))],
                   split=MLA_HD_PAD)
    side_by_side = lambda wg: wg.transpose(1, 0, 2).reshape(wg.shape[1], wide)
    widei = pl.BlockSpec((tm, wide), lambda i, j, k: (i, 0))
    (dcq,) = _mm("mla_dcq", (t // tm, 1, 1), dq, widei,
                 side_by_side(wuq), pl.BlockSpec((MLA_Q_RANK, wide), lambda i, j, k: (0, 0)), (1, 1),
                 [(_sds((t, MLA_Q_RANK), F32), pl.BlockSpec((tm, MLA_Q_RANK), lambda i, j, k: (i, 0)))])
    (dckv,) = _mm("mla_dckv", (t // tm, 1, 1), dkv, widei,
                  side_by_side(wukv), pl.BlockSpec((MLA_KV_RANK, wide), lambda i, j, k: (0, 0)), (1, 1),
                  [(_sds((t, MLA_KV_RANK), F32), pl.BlockSpec((tm, MLA_KV_RANK), lambda i, j, k: (i, 0)))])
    dproj2, dqa, dkva = _mla_mid_bwd(proj2, qa, kva, dcq, dckv, dkr)
    win = pl.BlockSpec((D_MODEL, MLA_IN_PAD), lambda i, j, k: (0, 0))
    (dwmi,) = _mm("mla_dwin", (1, 1, t // tk), hn, rowk,
                  dproj2, pl.BlockSpec((tk, MLA_IN_PAD), lambda i, j, k: (k, 0)), (0, 0),
                  [(_sds((D_MODEL, MLA_IN_PAD), BF16), win)])
    dh_in, dg = _mm("mla_dhn", (t // tm, 1, 1),
                    dproj2, pl.BlockSpec((tm, MLA_IN_PAD), lambda i, j, k: (i, 0)), wmi, win, (1, 1),
                    _norm_bwd_outs(t, tm), extras=[(h, row), (dh, row), (norm_g, VEC)], epi=_norm_bwd_epi)
    return dh_in, dict(mix=jnp.sum(dg, axis=0), wmi=dwmi, qa=dqa, kva=dkva, wuq=dwuq, wukv=dwukv, gq=dgq, gk=dgk,
                       wmo=dwmo)


def _local_step(x, p, target, w, fetch, emit=lambda group: ()):
    t = x.shape[0]
    cos_r, sin_r, tabs = w['tables'] if 'tables' in w else _rope_tables(t, 0.0)
    row = lambda a, i: a[i:i + 1]

    h1, hn1, s_ret = _ret_layer_fwd(x, row(w['mix_norm'], 0), w['ret_w_in'],
                                    lambda after: fetch('ret_out', after)['ret_w_out'], w['ret_gn'], cos_r, sin_r,
                                    row(w['mlp_norm'], 0), hn=w.get('hn0'), deps=w['deps'])
    h2, hn2, s_mlp0 = _mlp_fwd(0, h1, hn1, fetch('mlp_w1_0', (h1,))['mlp_w1'],
                               lambda after: fetch('mlp_w2_0', after)['mlp_w2'], row(w['ple_norm'], 0))
    w0 = fetch('ple_0', (h2,))
    h3, hn3, s_ple0 = _ple_fwd(0, h2, hn2, p, w0['ple_gate_w'], w0['ple_proj_w'], row(w['mix_norm'], 1))
    h4, hn4, s_mla, (wmi, wuq, wukv, wmo) = _mla_layer_fwd(
        h3, hn3, fetch, w['mla_q_a_norm'], w['mla_kv_a_norm'], w['mla_q_norm'], w['mla_k_norm'], tabs,
        row(w['mlp_norm'], 1))
    mla_w = (wmi, w['mla_q_a_norm'], w['mla_kv_a_norm'], wuq, wukv, w['mla_q_norm'], w['mla_k_norm'], wmo, tabs)
    w1 = fetch('layer_1', (h4,))
    h5, hn5, s_mlp1 = _mlp_fwd(1, h4, hn4, w1['mlp_w1'], lambda after: w1['mlp_w2'], row(w['ple_norm'], 1))
    dy, sq_err, s_ple1 = _ple_fwd(1, h5, hn5, p, w1['ple_gate_w'], w1['ple_proj_w'], None, target)

    n = N_DEV
    colsh = lambda a: a.reshape(a.shape[0], n, a.shape[1] // n).transpose(1, 0, 2)
    rowsh = lambda a: a.reshape(n, a.shape[0] // n, a.shape[1])
    big = {}

    def emit_group(group):
        big.update(group)
        return emit(group)

    dh5, dg_ple1, dwg1, dwp1 = _ple_bwd(1, dy, s_ple1, p, row(w['ple_norm'], 1), w1['ple_gate_w'])
    dh4, dg_mlp1, dw1_1, dw2_1 = _mlp_bwd(1, dh5, s_mlp1, row(w['mlp_norm'], 1))
    deps = emit_group({('ple_gate_w', 1): rowsh(dwg1), ('ple_proj_w', 1): colsh(dwp1),
                       ('mlp_w2', 1): dw2_1, ('mlp_w1', 1): dw1_1})
    dh3, gm = _mla_layer_bwd(dh4, s_mla, row(w['mix_norm'], 1), *mla_w, deps=deps)
    deps = emit_group({('mla_w_out', 0): rowsh(gm['wmo']), ('mla_w_uq', 0): _gather_rope(gm['wuq']),
                       ('mla_w_ukv', 0): gm['wukv'], ('mla_w_in', 0): rowsh(_gather_rope(gm['wmi']))})
    dh2, dg_ple0, _, _ = _ple_bwd(
        0, dh3, s_ple0, p, row(w['ple_norm'], 0), w0['ple_gate_w'], deps=deps,
        emit=lambda dwg, dwp: emit_group({('ple_gate_w', 0): rowsh(dwg), ('ple_proj_w', 0): colsh(dwp)}))
    dh1, dg_mlp0, _, _ = _mlp_bwd(0, dh2, s_mlp0, row(w['mlp_norm'], 0),
                                  emit_w2=lambda dw2: emit_group({('mlp_w2', 0): dw2}),
                                  emit_w1=lambda dw1: emit_group({('mlp_w1', 0): dw1}))
    dx, dg_mix0, dgn = _ret_layer_bwd(
        dh1, s_ret, row(w['mix_norm'], 0), w['ret_w_in'], w['ret_gn'], cos_r, sin_r,
        lambda dwro: emit_group({('ret_w_out', 0): rowsh(dwro)}),
        lambda dwri: emit_group({('ret_w_in', 0): dwri}))

    small = dict(
        mix_norm=[dg_mix0, gm['mix']], mlp_norm=[dg_mlp0, dg_mlp1], ple_norm=[dg_ple0, dg_ple1],
        ret_gn=dgn, mla_q_a_norm=gm['qa'], mla_kv_a_norm=gm['kva'], mla_q_norm=gm['gq'], mla_k_norm=gm['gk'],
    )
    return sq_err, dx, big, small


def _my_place():
    x, y, c = lax.axis_index("x"), lax.axis_index("y"), lax.axis_index("c")
    return x, y, c


def _flat(px, py, pc):
    return 4 * px + 2 * py + pc


def _peer(x, y, c, r):
    return (1 - x if r & 4 else x, 1 - y if r & 2 else y, 1 - c if r & 1 else c)


HBM = pl.BlockSpec(memory_space=pltpu.HBM)
SEMS = pl.BlockSpec(memory_space=pltpu.SEMAPHORE)
SIDE_EFFECT = pltpu.SideEffectType.DATAFLOW_SIDE_EFFECTING


def _rs_copies(x, y, c, srcs, lands, send_sems, recv_sems):
    copies = []
    for a in range(len(srcs)):
        for r in range(1, N_DEV):
            peer = _peer(x, y, c, r)
            k = a * (N_DEV - 1) + r - 1
            copies.append(pltpu.make_async_remote_copy(
                src_ref=srcs[a].at[_flat(*peer)], dst_ref=lands[a].at[r - 1],
                send_sem=send_sems.at[k], recv_sem=recv_sems.at[k], device_id=peer, device_id_type=MESH))
    return copies


def _rs_start(name, arrays):
    n = len(arrays)
    hbm = lambda a: pltpu.with_memory_space_constraint(a, pltpu.HBM)
    lands = [hbm(lax.empty((N_DEV - 1,) + a.shape[1:], a.dtype)) for a in arrays]

    def body(*refs):
        srcs, lnd = refs[:n], refs[n:2 * n]
        send_sems, recv_sems = refs[2 * n], refs[2 * n + 1]
        token = refs[-1]
        for cp in _rs_copies(*_my_place(), srcs, lnd, send_sems, recv_sems):
            cp.start()
        token[...] = jnp.zeros_like(token)

    outs = pl.pallas_call(
        body, name=name,
        in_specs=[HBM] * (2 * n),
        out_specs=[SEMS, SEMS] + [HBM] * (2 * n) + [pl.BlockSpec(memory_space=pltpu.VMEM)],
        out_shape=[pltpu.SemaphoreType.DMA((n * (N_DEV - 1),)), pltpu.SemaphoreType.DMA((n * (N_DEV - 1),))]
        + [pltpu.HBM(a.shape, a.dtype) for a in arrays] + [pltpu.HBM(l.shape, l.dtype) for l in lands]
        + [_sds((8, 128), F32)],
        input_output_aliases={i: 2 + i for i in range(2 * n)},
        compiler_params=pltpu.CompilerParams(has_side_effects=SIDE_EFFECT),
    )(*[hbm(a) for a in arrays], *lands)
    return outs[0], outs[1], outs[2:2 + n], outs[2 + n:2 + 2 * n], outs[-1]


def _rs_wait(name, send_sems, recv_sems, srcs, lands, after):
    n = len(srcs)

    def body(*refs):
        src_refs, lnd = refs[:n], refs[n:2 * n]
        send, recv = refs[2 * n], refs[2 * n + 1]
        for cp in _rs_copies(*_my_place(), src_refs, lnd, send, recv):
            cp.wait_send()
            cp.wait_recv()

    outs = pl.pallas_call(
        body, name=name,
        in_specs=[HBM] * (2 * n) + [SEMS, SEMS] + [ANY] * len(after),
        out_specs=[HBM] * (2 * n),
        out_shape=[pltpu.HBM(a.shape, a.dtype) for a in list(srcs) + list(lands)],
        input_output_aliases={i: i for i in range(2 * n)},
        compiler_params=pltpu.CompilerParams(has_side_effects=SIDE_EFFECT),
    )(*srcs, *lands, send_sems, recv_sems, *after)
    return outs[:n], outs[n:]


SMALL_PACK_ROWS = 16


def _all_reduce_small(rows, deps=()):
    n = len(rows)

    def body(*refs):
        ins = refs[:n]
        out_ref, mine, buf, send_sems, recv_sems = refs[n + len(deps):]
        x, y, c = _my_place()
        mine[...] = jnp.zeros_like(mine)
        for (r0, a), ref in zip(rows, ins):
            mine[r0:r0 + a.shape[0], 0:a.shape[1]] = ref[...]
        buf[_flat(x, y, c)] = mine[...]
        copies = []
        for r in range(1, N_DEV):
            peer = _peer(x, y, c, r)
            send = pltpu.make_async_remote_copy(
                src_ref=mine, dst_ref=buf.at[_flat(x, y, c)],
                send_sem=send_sems.at[r - 1], recv_sem=recv_sems.at[r - 1], device_id=peer, device_id_type=MESH)
            send.start()
            recv = pltpu.make_async_remote_copy(
                src_ref=mine, dst_ref=buf.at[_flat(*peer)],
                send_sem=send_sems.at[r - 1], recv_sem=recv_sems.at[r - 1], device_id=peer, device_id_type=MESH)
            copies.append((send, recv))
        for send, recv in copies:
            send.wait_send()
            recv.wait_recv()
        acc = buf[0]
        for s in range(1, N_DEV):
            acc = acc + buf[s]
        out_ref[...] = acc

    vm = pl.BlockSpec(memory_space=pltpu.VMEM)
    shape = (SMALL_PACK_ROWS, D_MODEL)
    return pl.pallas_call(
        body, name="all_reduce_small", in_specs=[vm] * n + [ANY] * len(deps), out_specs=vm,
        out_shape=_sds(shape, F32),
        scratch_shapes=[pltpu.VMEM(shape, F32), pltpu.VMEM((N_DEV,) + shape, F32),
                        pltpu.SemaphoreType.DMA((7,)), pltpu.SemaphoreType.DMA((7,))],
    )(*[a for _, a in rows], *deps)


def _adamw_math(w, g, m, v):
    m = ADAM_B1 * m + (1.0 - ADAM_B1) * g
    v = ADAM_B2 * v + (1.0 - ADAM_B2) * (g * g)
    m_hat = m / (1.0 - ADAM_B1 ** ADAM_STEP)
    v_hat = v / (1.0 - ADAM_B2 ** ADAM_STEP)
    delta = -ADAM_LR * (m_hat / (jnp.sqrt(v_hat) + ADAM_EPS) + ADAM_WD * w)
    return delta, m, v


def _adamw_big(name, w, m, v, srcs, lands, me):
    nl, rows, cols = w.shape
    tr = next(cand for cand in (256, 128, 64, 32, 16, 8) if rows % cand == 0)

    def body(me_ref, w_ref, m_ref, v_ref, *rest):
        src_refs, land_refs = rest[:nl], rest[nl:2 * nl]
        g_ref, d_ref, mo_ref, vo_ref = rest[2 * nl:]
        for layer in range(nl):
            @pl.when(pl.program_id(0) == layer)
            def _():
                g = src_refs[layer][...].astype(F32)
                for s in range(N_DEV - 1):
                    g = g + land_refs[layer][s].astype(F32)
                delta, mn, vn = _adamw_math(w_ref[...], g, m_ref[...], v_ref[...])
                g_ref[...] = g
                d_ref[...] = delta
                mo_ref[...] = mn
                vo_ref[...] = vn

    blk = pl.BlockSpec((None, tr, cols), lambda l, i, me_ref: (l, i, 0))
    at = lambda layer, l, i: jnp.where(l == layer, i, 0)
    own = [pl.BlockSpec((None, tr, cols), functools.partial(lambda layer, l, i, me_ref: (me_ref[0], at(layer, l, i), 0),
                                                            layer)) for layer in range(nl)]
    peers = [pl.BlockSpec((N_DEV - 1, tr, cols), functools.partial(lambda layer, l, i, me_ref: (0, at(layer, l, i), 0),
                                                                   layer)) for layer in range(nl)]
    return pl.pallas_call(
        body, name=name,
        grid_spec=pltpu.PrefetchScalarGridSpec(
            num_scalar_prefetch=1, grid=(nl, rows // tr),
            in_specs=[blk, blk, blk] + own + peers, out_specs=[blk] * 4),
        out_shape=[_sds((nl, rows, cols), F32)] * 4,
        compiler_params=_cparams(("arbitrary", "arbitrary")),
    )(me, w, m, v, *srcs, *lands)


def _adamw_small(ws, gs, ms, vs):
    n = len(ws)

    def body(*refs):
        w_refs, g_refs, m_refs, v_refs = (refs[i * n:(i + 1) * n] for i in range(4))
        d_out, m_out, v_out = (refs[(4 + i) * n:(5 + i) * n] for i in range(3))
        for i in range(n):
            delta, mn, vn = _adamw_math(w_refs[i][...], g_refs[i][...], m_refs[i][...], v_refs[i][...])
            d_out[i][...] = delta
            m_out[i][...] = mn
            v_out[i][...] = vn

    vm = pl.BlockSpec(memory_space=pltpu.VMEM)
    outs = pl.pallas_call(
        body, name="adamw_small", in_specs=[vm] * (4 * n), out_specs=[vm] * (3 * n),
        out_shape=[_sds(a.shape, F32) for a in ws] * 3,
    )(*ws, *gs, *ms, *vs)
    return outs[:n], outs[n:2 * n], outs[2 * n:]


def _pad_to(a, rows, cols):
    return jnp.pad(a, ((0, rows - a.shape[0]), (0, cols - a.shape[1])))


def _place_own(blocks):
    me = _flat(*_my_place())
    return [lax.dynamic_update_slice(lax.empty((N_DEV,) + b.shape, b.dtype), b[None], (me,) + (0,) * b.ndim)
            for b in blocks]


def _ag_copies(x, y, c, blocks, bufs, send_sems, recv_sems, arriving):
    copies = []
    for a in range(len(blocks)):
        for r in range(1, N_DEV):
            peer = _peer(x, y, c, r)
            k = a * (N_DEV - 1) + r - 1
            copies.append(pltpu.make_async_remote_copy(
                src_ref=blocks[a], dst_ref=bufs[a].at[_flat(*(peer if arriving else (x, y, c)))],
                send_sem=send_sems.at[k], recv_sem=recv_sems.at[k], device_id=peer, device_id_type=MESH))
    return copies


def _ag_start(groups, after):
    flat = [pair for g in groups for pair in g]
    n, ng = len(flat), len(groups)
    hbm = lambda a: pltpu.with_memory_space_constraint(a, pltpu.HBM)

    def body(*refs):
        blocks, bufs = refs[:n], refs[n:2 * n]
        sems = refs[2 * n + len(after):2 * n + len(after) + 2 * ng]
        x, y, c = _my_place()
        at = 0
        for gi, g in enumerate(groups):
            for cp in _ag_copies(x, y, c, blocks[at:at + len(g)], bufs[at:at + len(g)], sems[2 * gi],
                                 sems[2 * gi + 1], arriving=False):
                cp.start()
            at += len(g)
        refs[-1][...] = jnp.zeros_like(refs[-1])

    sem_shapes = [pltpu.SemaphoreType.DMA((len(g) * (N_DEV - 1),)) for g in groups for _ in range(2)]
    outs = pl.pallas_call(
        body, name="gather_start",
        in_specs=[HBM] * (2 * n) + [ANY] * len(after),
        out_specs=[SEMS] * (2 * ng) + [HBM] * (2 * n) + [pl.BlockSpec(memory_space=pltpu.VMEM)],
        out_shape=sem_shapes + [pltpu.HBM(b.shape, b.dtype) for b, _ in flat]
        + [pltpu.HBM(u.shape, u.dtype) for _, u in flat] + [_sds((8, 128), F32)],
        input_output_aliases={i: 2 * ng + i for i in range(2 * n)},
        compiler_params=pltpu.CompilerParams(has_side_effects=SIDE_EFFECT),
    )(*[hbm(b) for b, _ in flat], *[hbm(u) for _, u in flat], *after)
    blocks_thru, bufs_thru = outs[2 * ng:2 * ng + n], outs[2 * ng + n:2 * ng + 2 * n]
    started, at = [], 0
    for gi, g in enumerate(groups):
        started.append((outs[2 * gi], outs[2 * gi + 1], blocks_thru[at:at + len(g)], bufs_thru[at:at + len(g)]))
        at += len(g)
    return started, outs[-1]


def _ag_wait(name, send_sems, recv_sems, blocks, bufs, after):
    n = len(blocks)

    def body(*refs):
        for cp in _ag_copies(*_my_place(), refs[:n], refs[n:2 * n], refs[2 * n], refs[2 * n + 1], arriving=True):
            cp.wait_send()
            cp.wait_recv()

    outs = pl.pallas_call(
        body, name=name,
        in_specs=[HBM] * (2 * n) + [SEMS, SEMS] + [ANY] * len(after),
        out_specs=[HBM] * (2 * n),
        out_shape=[pltpu.HBM(a.shape, a.dtype) for a in list(blocks) + list(bufs)],
        input_output_aliases={i: i for i in range(2 * n)},
        compiler_params=pltpu.CompilerParams(has_side_effects=SIDE_EFFECT),
    )(*blocks, *bufs, send_sems, recv_sems, *after)
    return outs[n:]


def _split_call(name, body, thru, sems_in, new_sems, after):
    n, ns, nn = len(thru), len(sems_in), len(new_sems)
    hbm = lambda a: pltpu.with_memory_space_constraint(a, pltpu.HBM)

    def wrapped(*refs):
        body(refs[:n], refs[n:n + ns], refs[n + ns + len(after):n + ns + len(after) + nn])
        refs[-1][...] = jnp.zeros_like(refs[-1])

    outs = pl.pallas_call(
        wrapped, name=name,
        in_specs=[HBM] * n + [SEMS] * ns + [ANY] * len(after),
        out_specs=[SEMS] * nn + [HBM] * n + [pl.BlockSpec(memory_space=pltpu.VMEM)],
        out_shape=[pltpu.SemaphoreType.DMA((k,)) for k in new_sems] + [pltpu.HBM(a.shape, a.dtype) for a in thru]
        + [_sds((8, 128), F32)],
        input_output_aliases={i: nn + i for i in range(n)},
        compiler_params=pltpu.CompilerParams(has_side_effects=SIDE_EFFECT),
    )(*[hbm(a) for a in thru], *sems_in, *after)
    return list(outs[:nn]), list(outs[nn:nn + n]), outs[-1]


def _two_level_gather(name, blocks, bufs, after=()):
    n = len(blocks)

    def copies(refs, s1, r1, s2, r2):
        x, y, c = _my_place()
        me, sibling = (x, y, c), (x, y, 1 - c)
        chips = [(1 - x, y), (x, 1 - y), (1 - x, 1 - y)]
        blk, buf = refs[:n], refs[n:]
        out = dict(send1=[], recv1_sib=[], recv1_ici=[], send2=[], recv2=[])
        for a in range(n):
            place = lambda dev: buf[a].at[_flat(*dev)]
            for k, to in enumerate([sibling] + [(*chip, c) for chip in chips]):
                mk = lambda dst: pltpu.make_async_remote_copy(
                    src_ref=blk[a], dst_ref=dst, send_sem=s1.at[4 * a + k], recv_sem=r1.at[4 * a + k],
                    device_id=to, device_id_type=MESH)
                out['send1'].append(mk(place(me)))
                out['recv1_sib' if k == 0 else 'recv1_ici'].append(mk(place(to)))
            for j, chip in enumerate(chips):
                mk = lambda dev: pltpu.make_async_remote_copy(
                    src_ref=place(dev), dst_ref=place(dev), send_sem=s2.at[3 * a + j], recv_sem=r2.at[3 * a + j],
                    device_id=sibling, device_id_type=MESH)
                out['send2'].append(mk((*chip, c)))
                out['recv2'].append(mk((*chip, 1 - c)))
        return out

    def start(refs, sems_in, new):
        for cp in copies(refs, new[0], new[1], new[0], new[1])['send1']:
            cp.start()

    def forward(refs, sems_in, new):
        cps = copies(refs, sems_in[0], sems_in[1], new[0], new[1])
        for cp in cps['recv1_ici']:
            cp.wait_recv()
        for cp in cps['send2']:
            cp.start()

    def finish(refs, sems_in, new):
        cps = copies(refs, *sems_in)
        for cp in cps['recv1_sib'] + cps['recv2']:
            cp.wait_recv()
        for cp in cps['send1'] + cps['send2']:
            cp.wait_send()

    sems1, thru, token = _split_call(name + "_start", start, list(blocks) + list(bufs), [], [4 * n, 4 * n], after)

    def complete(after):
        sems2, thru2, token2 = _split_call(name + "_forward", forward, thru, sems1, [3 * n, 3 * n], after)
        _, thru3, _ = _split_call(name + "_wait", finish, thru2, sems1 + sems2, [], ())
        return thru3[n:], token2

    return token, complete


def _prepare_weights(p, x):
    n = N_DEV
    bf = lambda a: a.astype(BF16)
    gn_pack = jnp.concatenate([
        _pad_to(p['ret_gn'][0], RET_HEADS, 128), _pad_to(p['mla_q_a_norm'], 1, 128),
        _pad_to(p['mla_kv_a_norm'], 1, 128), jnp.zeros((2, 128), F32)], axis=0)
    ple = lambda l: [bf(p['ple_gate_w'][l]), bf(p['ple_proj_w'][l])]
    names = ('mlp_w1_0', 'mlp_w2_0', 'ple_0', 'mla_in', 'mla_up', 'mla_out', 'layer_1')
    later = [[bf(p['mlp_w1'][0])], [bf(p['mlp_w2'][0])], ple(0),
             [bf(p['mla_w_in'][0])], [bf(p['mla_w_uq'][0]), bf(p['mla_w_ukv'][0])], [bf(p['mla_w_out'][0])],
             [bf(p['mlp_w1'][1]), bf(p['mlp_w2'][1])] + ple(1)]
    first = [gn_pack, bf(p['ret_w_in'][0])]
    second = [bf(p['ret_w_out'][0])]
    token, complete_first = _two_level_gather("first_gather", first, _place_own(first))
    token2, complete_second = _two_level_gather("second_gather", second, _place_own(second), (token,))
    hn0 = _rms_fwd("mix_norm0", x, p['mix_norm'][0:1], deps=(token2,))
    bufs = _place_own([b for g in later for b in g])
    tables = _rope_tables(x.shape[0], token2[0, 0])
    (pack, wri), token = complete_first((hn0, tables[0], tables[1], *tables[2], *bufs))
    groups, at = [], 0
    for g in later:
        groups.append(list(zip(g, bufs[at:at + len(g)])))
        at += len(g)
    started, token = _ag_start(groups, (token,))

    w = {k: p[k] for k in ('mix_norm', 'mlp_norm', 'ple_norm')}
    w['hn0'] = hn0
    w['tables'] = tables
    w['ret_gn'] = pack[:, :RET_HEADS, :RET_DV // n].transpose(1, 0, 2).reshape(RET_HEADS, RET_DV)
    w['mla_q_a_norm'] = pack[:, RET_HEADS, :MLA_Q_RANK // n].reshape(1, MLA_Q_RANK)
    w['mla_kv_a_norm'] = pack[:, RET_HEADS + 1, :MLA_KV_RANK // n].reshape(1, MLA_KV_RANK)
    w['ret_w_in'] = wri
    w['mla_q_norm'] = _spread_rope(p['mla_q_norm'])
    w['mla_k_norm'] = _spread_rope(p['mla_k_norm'])
    w['deps'] = (token,)

    def fetch(name, after):
        if name == 'ret_out':
            return dict(ret_w_out=complete_second(after)[0][0].reshape(RET_V_W, D_MODEL))
        got = list(_ag_wait("gather_wait_" + name, *started[names.index(name)], after))
        if name == 'mla_in':
            return dict(mla_w_in=_spread_rope(got[0].reshape(D_MODEL, MLA_IN)))
        if name == 'mla_up':
            return dict(mla_w_uq=_spread_rope(got[0]), mla_w_ukv=got[1])
        if name == 'mla_out':
            return dict(mla_w_out=got[0].reshape(D_MODEL, D_MODEL))
        out = {}
        if name in ('mlp_w1_0', 'layer_1'):
            out['mlp_w1'] = got.pop(0)
        if name in ('mlp_w2_0', 'layer_1'):
            out['mlp_w2'] = got.pop(0)
        if name in ('ple_0', 'layer_1'):
            out['ple_gate_w'] = got[0].reshape(D_MODEL, D_MODEL)
            out['ple_proj_w'] = got[1].transpose(1, 0, 2).reshape(PLE_DIM, D_MODEL)
        return out

    return w, fetch


def _small_grads(small, after):
    rows = [(0, small['mix_norm'][0]), (1, small['mix_norm'][1]), (2, small['mlp_norm'][0]),
            (3, small['mlp_norm'][1]), (4, small['ple_norm'][0]), (5, small['ple_norm'][1]),
            (6, small['ret_gn']), (10, small['mla_q_a_norm']), (11, small['mla_kv_a_norm']),
            (12, small['mla_q_norm']), (13, small['mla_k_norm']), (14, small['sq_err'])]
    gs = _all_reduce_small(rows, after)
    me = _flat(*_my_place())
    n = N_DEV
    return dict(
        sq_err=gs[14, 0],
        mix_norm=gs[0:2], mlp_norm=gs[2:4], ple_norm=gs[4:6],
        ret_gn=lax.dynamic_slice(gs, (6, me * (RET_DV // n)), (RET_HEADS, RET_DV // n)),
        mla_q_a_norm=lax.dynamic_slice(gs, (10, me * (MLA_Q_RANK // n)), (1, MLA_Q_RANK // n)),
        mla_kv_a_norm=lax.dynamic_slice(gs, (11, me * (MLA_KV_RANK // n)), (1, MLA_KV_RANK // n)),
        mla_q_norm=_gather_rope(gs[12:13, :MLA_HD_PAD]), mla_k_norm=_gather_rope(gs[13:14, :MLA_HD_PAD]))


def kernel(x, p, mix_norm, ret_w_in, ret_gn, ret_w_out, mla_w_in, mla_q_a_norm, mla_kv_a_norm, mla_w_uq, mla_w_ukv, mla_q_norm, mla_k_norm, mla_w_out, mlp_norm, mlp_w1, mlp_w2, ple_norm, ple_gate_w, ple_proj_w, loss_target, m_mix_norm, m_ret_w_in, m_ret_gn, m_ret_w_out, m_mla_w_in, m_mla_q_a_norm, m_mla_kv_a_norm, m_mla_w_uq, m_mla_w_ukv, m_mla_q_norm, m_mla_k_norm, m_mla_w_out, m_mlp_norm, m_mlp_w1, m_mlp_w2, m_ple_norm, m_ple_gate_w, m_ple_proj_w, v_mix_norm, v_ret_w_in, v_ret_gn, v_ret_w_out, v_mla_w_in, v_mla_q_a_norm, v_mla_kv_a_norm, v_mla_w_uq, v_mla_w_ukv, v_mla_q_norm, v_mla_k_norm, v_mla_w_out, v_mlp_norm, v_mlp_w1, v_mlp_w2, v_ple_norm, v_ple_gate_w, v_ple_proj_w):
    given = dict(locals())
    params = {n: given[n] for n in WEIGHTS}
    w, fetch = _prepare_weights(params, x[0])

    started = []

    def emit(group):
        keys = list(group)
        send, recv, srcs, lands, token = _rs_start(f"rs_start{len(started)}", [group[k] for k in keys])
        started.append((keys, send, recv, srcs, lands))
        return (token,)

    sq_err, grad_x, _, small = _local_step(x[0], p, loss_target[0], w, fetch, emit)
    small['sq_err'] = sq_err[0:1]

    grads, deltas, new_m, new_v = {}, {}, {}, {}
    total = {}

    def small_updates(after):
        sg = _small_grads(small, after)
        total['loss'] = 0.5 / D_MODEL * sg['sq_err']
        two_d = lambda a: a.reshape(-1, a.shape[-1])
        d_s, m_s, v_s = _adamw_small(
            [two_d(params[n]) for n in SMALL], [sg[n] for n in SMALL],
            [two_d(given["m_" + n]) for n in SMALL], [two_d(given["v_" + n]) for n in SMALL])
        for i, n in enumerate(SMALL):
            shape = params[n].shape
            grads[n], deltas[n], new_m[n], new_v[n] = (a.reshape(shape) for a in (sg[n], d_s[i], m_s[i], v_s[i]))
        return (d_s[0],)

    me = _flat(*_my_place()).astype(jnp.int32).reshape(1)
    after = (grad_x,)
    src_of, land_of = {}, {}
    for gi, (keys, send, recv, srcs, lands) in enumerate(started):
        if gi == len(started) - 1:
            after = small_updates(after)
        srcs, lands = _rs_wait(f"rs_wait{gi}", send, recv, srcs, lands, after)
        for k, s, l in zip(keys, srcs, lands):
            src_of[k], land_of[k] = s, l
        done = [n for n in BIG if n not in grads and all((n, l) in src_of for l in range(params[n].shape[0]))]
        for n in done:
            layers = range(params[n].shape[0])
            grads[n], deltas[n], new_m[n], new_v[n] = _adamw_big(
                "adamw_" + n, params[n], given["m_" + n], given["v_" + n],
                [src_of[(n, l)] for l in layers], [land_of[(n, l)] for l in layers], me)
        if done:
            after = tuple(deltas[n] for n in done)

    return (total['loss'], grad_x[None], *[grads[n] for n in WEIGHTS], *[deltas[n] for n in WEIGHTS],
            *[new_m[n] for n in WEIGHTS], *[new_v[n] for n in WEIGHTS])
```

```python
import functools

import jax
import jax.numpy as jnp
from jax import lax
from jax.experimental import pallas as pl
from jax.experimental.pallas import tpu as pltpu

F32 = jnp.float32
BF16 = jnp.bfloat16
MESH = pl.DeviceIdType.MESH
ANY = pl.BlockSpec(memory_space=pl.ANY)

N_DEV = 8
D_MODEL = 1024
CHUNK = 64
RET_BLOCK = 4 * CHUNK
EPS = 1e-6
ROPE_THETA = 10000.0
RET_HEADS = 4
RET_DK = 256
RET_DV = 512
RET_QK_W = RET_HEADS * RET_DK
RET_V_W = RET_HEADS * RET_DV
RET_IN = 2 * RET_QK_W + 2 * RET_V_W
MLA_HEADS = 8
MLA_NOPE = 128
MLA_ROPE = 64
MLA_QKD = MLA_NOPE + MLA_ROPE
MLA_VD = 128
MLA_Q_RANK = 384
MLA_KV_RANK = 256
MLA_IN = MLA_Q_RANK + MLA_KV_RANK + MLA_ROPE
MLA_IN_PAD = 768
MLA_HD_PAD = 256
D_FF = 4096
PLE_DIM = 256
ATT_SCALE = MLA_QKD ** -0.5
LOG2E = 1.4426950408889634
ATT_EXP2 = ATT_SCALE * LOG2E

ADAM_LR = 0.001
ADAM_B1 = 0.9
ADAM_B2 = 0.999
ADAM_EPS = 1e-08
ADAM_WD = 0.01
ADAM_STEP = 10

VMEM_LIMIT = 52 * 1024 * 1024
ROW_TILE = 1024
RET_ROWS = 512
ATT_BLOCK = 256
ATT_QROWS = 1024
ATT_KROWS = 1024
ATT_HEADS = 2
PREP_ROWS = 2048

WEIGHTS = ['mix_norm', 'ret_w_in', 'ret_gn', 'ret_w_out', 'mla_w_in', 'mla_q_a_norm', 'mla_kv_a_norm',
           'mla_w_uq', 'mla_w_ukv', 'mla_q_norm', 'mla_k_norm', 'mla_w_out', 'mlp_norm', 'mlp_w1', 'mlp_w2',
           'ple_norm', 'ple_gate_w', 'ple_proj_w']
BIG = ['ret_w_in', 'ret_w_out', 'mla_w_in', 'mla_w_uq', 'mla_w_ukv', 'mla_w_out', 'mlp_w1', 'mlp_w2',
       'ple_gate_w', 'ple_proj_w']
SMALL = [w for w in WEIGHTS if w not in BIG]


def _cparams(sem=None):
    return pltpu.CompilerParams(dimension_semantics=sem, vmem_limit_bytes=VMEM_LIMIT)


def _dot(a, b, ca, cb):
    return lax.dot_general(a, b, (((ca,), (cb,)), ((), ())), preferred_element_type=F32)


def _bf(v):
    return v if v.dtype == BF16 else v.astype(BF16)


def _sigmoid(z):
    return 1.0 / (1.0 + jnp.exp(-z))


def _mm(name, grid, a, a_spec, b, b_spec, contract, outs, extras=(), epi=None, deps=(), split=None):
    nk = grid[2]
    n_ex, n_out, n_dep = len(extras), len(outs), len(deps)
    acc_shape = tuple(d for d in outs[0][1].block_shape if d is not None)
    if split is not None:
        acc_shape = (acc_shape[1], acc_shape[0] * split)

    def body(*refs):
        a_ref, b_ref = refs[:2]
        ex_refs = refs[2:2 + n_ex]
        out_refs = refs[2 + n_ex + n_dep:2 + n_ex + n_dep + n_out]

        def product():
            return _dot(_bf(a_ref[...]), _bf(b_ref[...]), contract[0], contract[1])

        def finish(acc):
            if split is not None:
                for j in range(acc_shape[1] // split):
                    out_refs[0][j] = acc[:, j * split:(j + 1) * split].astype(out_refs[0].dtype)
                return
            acc = acc[...]
            res = epi(acc, *[r[...] for r in ex_refs]) if epi is not None else (acc,)
            for o, r in zip(out_refs, res):
                o[...] = r.astype(o.dtype)

        if nk == 1:
            finish(product())
        else:
            acc_ref = refs[-1]
            k = pl.program_id(2)

            @pl.when(k == 0)
            def _():
                acc_ref[...] = jnp.zeros_like(acc_ref)

            acc_ref[...] += product()

            @pl.when(k == nk - 1)
            def _():
                finish(acc_ref)

    return pl.pallas_call(
        body, name=name, grid=grid,
        in_specs=[a_spec, b_spec] + [s for _, s in extras] + [ANY] * n_dep,
        out_specs=[s for _, s in outs],
        out_shape=[s for s, _ in outs],
        scratch_shapes=[pltpu.VMEM(acc_shape, F32)] if nk > 1 else [],
        compiler_params=_cparams(("parallel", "parallel", "arbitrary")),
    )(a, b, *[x for x, _ in extras], *deps)


def _mm_rows(name, tm, a, w, mode, outs, extras=(), epi=None, deps=()):
    n_sh, rows, cols = w.shape
    n_ex, n_out, n_dep = len(extras), len(outs), len(deps)
    by_cols = mode in ('nn_cols', 'nt_rows')
    width = cols if mode == 'nn_cols' else rows

    def body(*refs):
        a_ref, w_hbm = refs[:2]
        ex_refs = refs[2:2 + n_ex]
        out_refs = refs[2 + n_ex + n_dep:2 + n_ex + n_dep + n_out]
        w_ref, w_sem = refs[-2:]
        first = pl.program_id(0) == 0

        def shard_copy(s):
            return pltpu.make_async_copy(w_hbm.at[s], w_ref.at[s], w_sem.at[s])

        @pl.when(first)
        def _():
            for s in range(n_sh):
                shard_copy(s).start()

        def arrived(s):
            @pl.when(first)
            def _():
                shard_copy(s).wait()

        if by_cols:
            av = _bf(a_ref[...])
            for s in range(n_sh):
                cs = slice(s * width, (s + 1) * width)
                arrived(s)
                acc = _dot(av, w_ref[s], 1, 0 if mode == 'nn_cols' else 1)
                res = epi(acc, *[r[:, cs] for r in ex_refs]) if epi is not None else (acc,)
                for o, r in zip(out_refs, res):
                    o[:, cs] = r.astype(o.dtype)
        else:
            chunk = rows if mode == 'nn_rows' else cols
            acc = None
            for s in range(n_sh):
                arrived(s)
                part = _dot(_bf(a_ref[:, s * chunk:(s + 1) * chunk]), w_ref[s], 1, 0 if mode == 'nn_rows' else 1)
                acc = part if acc is None else acc + part
            res = epi(acc, *[r[...] for r in ex_refs]) if epi is not None else (acc,)
            for o, r in zip(out_refs, res):
                o[...] = r.astype(o.dtype)

    t, ka = a.shape
    return pl.pallas_call(
        body, name=name, grid=(t // tm, 1, 1),
        in_specs=[pl.BlockSpec((tm, ka), lambda i, j, k: (i, 0)), ANY] + [s for _, s in extras] + [ANY] * n_dep,
        out_specs=[s for _, s in outs],
        out_shape=[s for s, _ in outs],
        scratch_shapes=[pltpu.VMEM(w.shape, w.dtype), pltpu.SemaphoreType.DMA((n_sh,))],
        compiler_params=_cparams(("arbitrary", "arbitrary", "arbitrary")),
    )(a, w, *[x for x, _ in extras], *deps)


def _sds(shape, dtype):
    return jax.ShapeDtypeStruct(shape, dtype)


def _row_tile(t, cap=ROW_TILE):
    return min(cap, t)


def _rms_fwd(name, x, g, deps=()):
    t, d = x.shape
    tm = _row_tile(t)

    def body(x_ref, g_ref, *rest):
        o_ref = rest[-1]
        xv = x_ref[...]
        r = lax.rsqrt(jnp.mean(xv * xv, axis=-1, keepdims=True) + EPS)
        o_ref[...] = (xv * r * g_ref[...]).astype(o_ref.dtype)

    return pl.pallas_call(
        body, name=name, grid=(t // tm,),
        in_specs=[pl.BlockSpec((tm, d), lambda i: (i, 0)), pl.BlockSpec((1, d), lambda i: (0, 0))] + [ANY] * len(deps),
        out_specs=pl.BlockSpec((tm, d), lambda i: (i, 0)),
        out_shape=_sds((t, d), BF16),
        compiler_params=_cparams(("parallel",)),
    )(x, g, *deps)


def _rms_bwd_rows(dy, xv, g, n):
    r = lax.rsqrt(jnp.sum(xv * xv, axis=-1, keepdims=True) / n + EPS)
    xh = xv * r
    dxh = dy * g
    dx = r * (dxh - xh * (jnp.sum(dxh * xh, axis=-1, keepdims=True) / n))
    return dx, dy * xh


def _ple_gate_bwd(name, dh, gate, e):
    t, d = dh.shape
    tm = _row_tile(t)

    def body(dh_ref, g_ref, e_ref, de_ref, dz_ref):
        dh_v, gt = dh_ref[...], g_ref[...].astype(F32)
        de_ref[...] = (dh_v * gt).astype(BF16)
        dz_ref[...] = (dh_v * e_ref[...].astype(F32) * (gt * (1.0 - gt))).astype(BF16)

    row = pl.BlockSpec((tm, d), lambda i: (i, 0))
    return pl.pallas_call(
        body, name=name, grid=(t // tm,), in_specs=[row, row, row], out_specs=[row, row],
        out_shape=[_sds((t, d), BF16), _sds((t, d), BF16)],
        compiler_params=_cparams(("parallel",)),
    )(dh, gate, e)


def _rope_half(v, cos, sin):
    half = v.shape[-1] // 2
    v1, v2 = v[:, :half], v[:, half:]
    return jnp.concatenate([v1 * cos - v2 * sin, v2 * cos + v1 * sin], axis=-1)


def _ret_consts():
    lg = jnp.log(1.0 - 2.0 ** (-5.0 - jnp.arange(RET_HEADS, dtype=F32)))
    idx = jnp.arange(RET_BLOCK, dtype=F32)
    chunk = jnp.floor(idx / CHUNK)
    dist = idx[:, None] - idx[None, :]
    same = chunk[:, None] == chunk[None, :]
    seen = jnp.where(same, jnp.abs(dist), jnp.where(chunk[None, :] < chunk[:, None], dist, jnp.inf))
    intra = jnp.exp(lg[:, None, None] * seen)
    qdec = jnp.exp(lg[:, None] * (idx + 1.0))
    kdec = jnp.exp(lg[:, None] * (RET_BLOCK - 1.0 - idx))
    cdec = jnp.exp(lg * RET_BLOCK)
    qdec = jnp.broadcast_to(qdec[:, :, None], (RET_HEADS, RET_BLOCK, RET_DK))
    kdec = jnp.broadcast_to(kdec[:, :, None], (RET_HEADS, RET_BLOCK, RET_DK))
    cdec = jnp.broadcast_to(cdec[:, None, None], (RET_HEADS, 1, RET_DV))
    return intra, qdec, kdec, cdec


def _ret_specs(rb, rev_nb=None):
    blk = (lambda i: i) if rev_nb is None else (lambda i: rev_nb - 1 - i)
    full = lambda shape: pl.BlockSpec(shape, lambda i: (0,) * len(shape))
    return dict(
        proj=pl.BlockSpec((rb, RET_IN), lambda i: (blk(i), 0)),
        tab=pl.BlockSpec((rb, RET_DK // 2), lambda i: (blk(i), 0)),
        vw=pl.BlockSpec((rb, RET_V_W), lambda i: (blk(i), 0)),
        st=pl.BlockSpec((rb // RET_BLOCK, RET_HEADS, RET_DK, RET_DV), lambda i: (blk(i), 0, 0, 0)),
        gn=full((RET_HEADS, 1, RET_DV)),
        intra=full((RET_HEADS, RET_BLOCK, RET_BLOCK)),
        dec=full((RET_HEADS, RET_BLOCK, RET_DK)),
        cdec=full((RET_HEADS, 1, RET_DV)),
    )


def _ret_fwd(proj, cos, sin, gn):
    t = proj.shape[0]
    rb = min(RET_ROWS, t)
    cpb = rb // RET_BLOCK
    intra, qdec, kdec, cdec = _ret_consts()
    sp = _ret_specs(rb)

    def body(proj_ref, cos_ref, sin_ref, gn_ref, intra_ref, qd_ref, kd_ref, cd_ref,
             gated_ref, outp_ref, st_ref, s_ref):
        @pl.when(pl.program_id(0) == 0)
        def _():
            s_ref[...] = jnp.zeros_like(s_ref)

        def chunk(c, carry):
            rows = pl.ds(pl.multiple_of(c * RET_BLOCK, RET_BLOCK), RET_BLOCK)
            cs, sn = cos_ref[rows, :], sin_ref[rows, :]
            for h in range(RET_HEADS):
                q = proj_ref[rows, h * RET_DK:(h + 1) * RET_DK].astype(F32)
                k = proj_ref[rows, RET_QK_W + h * RET_DK:RET_QK_W + (h + 1) * RET_DK].astype(F32)
                v = proj_ref[rows, 2 * RET_QK_W + h * RET_DV:2 * RET_QK_W + (h + 1) * RET_DV]
                g = proj_ref[rows, 2 * RET_QK_W + RET_V_W + h * RET_DV:
                             2 * RET_QK_W + RET_V_W + (h + 1) * RET_DV].astype(F32)
                qr = _rope_half(q, cs, sn)
                kr = _rope_half(k, cs, sn) * (RET_DK ** -0.5)
                qb, kb, vb = qr.astype(BF16), kr.astype(BF16), v
                sc = _dot(qb, kb, 1, 1) * intra_ref[h]
                inner = _dot(sc.astype(BF16), vb, 1, 0)
                s_old = s_ref[h]
                sb = s_old.astype(BF16)
                st_ref[c, h] = sb
                cross = _dot((qr * qd_ref[h]).astype(BF16), sb, 1, 0)
                out = inner + cross
                s_ref[h] = s_old * cd_ref[h] + _dot((kr * kd_ref[h]).astype(BF16), vb, 0, 0)
                r = lax.rsqrt(jnp.mean(out * out, axis=-1, keepdims=True) + EPS)
                y = out * r * gn_ref[h]
                cols = slice(h * RET_DV, (h + 1) * RET_DV)
                gated_ref[rows, cols] = (g * _sigmoid(g) * y).astype(BF16)
                outp_ref[rows, cols] = out.astype(BF16)
            return carry

        lax.fori_loop(0, cpb, chunk, 0)

    return pl.pallas_call(
        body, name="ret_fwd", grid=(t // rb,),
        in_specs=[sp['proj'], sp['tab'], sp['tab'], sp['gn'], sp['intra'], sp['dec'], sp['dec'], sp['cdec']],
        out_specs=[sp['vw'], sp['vw'], sp['st']],
        out_shape=[_sds((t, RET_V_W), BF16), _sds((t, RET_V_W), BF16),
                   _sds((t // RET_BLOCK, RET_HEADS, RET_DK, RET_DV), BF16)],
        scratch_shapes=[pltpu.VMEM((RET_HEADS, RET_DK, RET_DV), F32)],
        compiler_params=_cparams(("arbitrary",)),
    )(proj, cos, sin, gn.reshape(RET_HEADS, 1, RET_DV), intra, qdec, kdec, cdec)


def _ret_gate_bwd_epi(dgt, out, g, gn):
    g = g.astype(F32)
    out = out.astype(F32)
    r = lax.rsqrt(jnp.mean(out * out, axis=-1, keepdims=True) + EPS)
    xh = out * r
    sg = _sigmoid(g)
    dgate = dgt * (xh * gn) * (sg * (1.0 + g * (1.0 - sg)))
    dy = dgt * (g * sg)
    dxh = dy * gn
    dout = r * (dxh - xh * jnp.mean(dxh * xh, axis=-1, keepdims=True))
    return dout, dgate, jnp.sum(dy * xh, axis=0, keepdims=True)


def _ret_bwd(proj, cos, sin, states, dout, dgate, deps=()):
    t = proj.shape[0]
    rb = min(RET_ROWS, t)
    cpb = rb // RET_BLOCK
    nb = t // rb
    intra, qdec, kdec, cdec = _ret_consts()
    sp = _ret_specs(rb, rev_nb=nb)

    def body(proj_ref, cos_ref, sin_ref, intra_ref, qd_ref, kd_ref, cd_ref, st_ref, dout_ref, dgate_ref, *rest):
        dproj_ref, ds_ref = rest[len(deps):]

        @pl.when(pl.program_id(0) == 0)
        def _():
            ds_ref[...] = jnp.zeros_like(ds_ref)

        def chunk(cc, carry):
            c = cpb - 1 - cc
            rows = pl.ds(pl.multiple_of(c * RET_BLOCK, RET_BLOCK), RET_BLOCK)
            cs, sn = cos_ref[rows, :], sin_ref[rows, :]
            for h in range(RET_HEADS):
                q = proj_ref[rows, h * RET_DK:(h + 1) * RET_DK].astype(F32)
                k = proj_ref[rows, RET_QK_W + h * RET_DK:RET_QK_W + (h + 1) * RET_DK].astype(F32)
                v = proj_ref[rows, 2 * RET_QK_W + h * RET_DV:2 * RET_QK_W + (h + 1) * RET_DV]
                cols = slice(h * RET_DV, (h + 1) * RET_DV)
                qr = _rope_half(q, cs, sn)
                kr = _rope_half(k, cs, sn) * (RET_DK ** -0.5)
                qb, kb, vb = qr.astype(BF16), kr.astype(BF16), v
                qdb = (qr * qd_ref[h]).astype(BF16)
                kdb = (kr * kd_ref[h]).astype(BF16)
                doutb = dout_ref[rows, cols]
                itr = intra_ref[h]
                pb = (_dot(qb, kb, 1, 1) * itr).astype(BF16)
                dv = _dot(pb, doutb, 0, 0)
                dsc = (_dot(doutb, vb, 1, 1) * itr).astype(BF16)
                dq = _dot(dsc, kb, 1, 0)
                dk = _dot(dsc, qb, 0, 0)
                dq = dq + _dot(doutb, st_ref[c, h], 1, 1) * qd_ref[h]
                ds_new = ds_ref[h]
                dsb = ds_new.astype(BF16)
                dk = dk + _dot(vb, dsb, 1, 1) * kd_ref[h]
                dv = dv + _dot(kdb, dsb, 1, 0)
                ds_ref[h] = ds_new * cd_ref[h] + _dot(qdb, doutb, 0, 0)
                dproj_ref[rows, h * RET_DK:(h + 1) * RET_DK] = _rope_half(dq, cs, -sn).astype(BF16)
                dproj_ref[rows, RET_QK_W + h * RET_DK:RET_QK_W + (h + 1) * RET_DK] = (
                    _rope_half(dk * (RET_DK ** -0.5), cs, -sn).astype(BF16))
                dproj_ref[rows, 2 * RET_QK_W + h * RET_DV:2 * RET_QK_W + (h + 1) * RET_DV] = dv.astype(BF16)
                dproj_ref[rows, 2 * RET_QK_W + RET_V_W + h * RET_DV:
                          2 * RET_QK_W + RET_V_W + (h + 1) * RET_DV] = dgate_ref[rows, cols]
            return carry

        lax.fori_loop(0, cpb, chunk, 0)

    return pl.pallas_call(
        body, name="ret_bwd", grid=(nb,),
        in_specs=[sp['proj'], sp['tab'], sp['tab'], sp['intra'], sp['dec'], sp['dec'], sp['cdec'],
                  sp['st'], sp['vw'], sp['vw']] + [ANY] * len(deps),
        out_specs=sp['proj'],
        out_shape=_sds((t, RET_IN), BF16),
        scratch_shapes=[pltpu.VMEM((RET_HEADS, RET_DK, RET_DV), F32)],
        compiler_params=_cparams(("arbitrary",)),
    )(proj, cos, sin, intra, qdec, kdec, cdec, states, dout, dgate, *deps)


def _spread_rope(a):
    return jnp.pad(a, [(0, 0)] * (a.ndim - 1) + [(0, MLA_ROPE)])


def _gather_rope(a):
    return a[..., :a.shape[-1] - MLA_ROPE]


def _rope_tables(t, zero):
    pos = jnp.arange(t, dtype=F32)[:, None] + zero
    inv = 1.0 / (ROPE_THETA ** (jnp.arange(0, RET_DK, 2, dtype=F32) / RET_DK))
    ang = pos * inv[None, :]
    return jnp.cos(ang), jnp.sin(ang), _mla_tables(t, pos)


def _mla_tables(t, pos):
    half = MLA_ROPE // 2
    inv = 1.0 / (ROPE_THETA ** (jnp.arange(0, MLA_ROPE, 2, dtype=F32) / MLA_ROPE))
    ang = pos * inv[None, :]
    cos, sin = jnp.cos(ang), jnp.sin(ang)
    z = jnp.zeros((t, half), F32)
    c = jnp.concatenate([cos, cos, z, z], axis=1)
    s1 = jnp.concatenate([-sin, z, z, z], axis=1)
    s2 = jnp.concatenate([z, sin, z, z], axis=1)
    return c, s1, s2


def _rope_tile(r, c, s1, s2):
    return r * c + pltpu.roll(r, 96, 1) * s1 + pltpu.roll(r, 32, 1) * s2


def _mla_mid(proj2, qa, kva):
    t = proj2.shape[0]
    tm = _row_tile(t)

    def body(p_ref, qa_ref, kva_ref, cq_ref, ckv_ref):
        cq = p_ref[:, :MLA_Q_RANK]
        ckv = p_ref[:, MLA_Q_RANK:MLA_Q_RANK + MLA_KV_RANK]
        rq = lax.rsqrt(jnp.mean(cq * cq, axis=-1, keepdims=True) + EPS)
        rkv = lax.rsqrt(jnp.mean(ckv * ckv, axis=-1, keepdims=True) + EPS)
        cq_ref[...] = (cq * rq * qa_ref[...]).astype(BF16)
        ckv_ref[...] = (ckv * rkv * kva_ref[...]).astype(BF16)

    return pl.pallas_call(
        body, name="mla_mid", grid=(t // tm,),
        in_specs=[pl.BlockSpec((tm, MLA_IN_PAD), lambda i: (i, 0)),
                  pl.BlockSpec((1, MLA_Q_RANK), lambda i: (0, 0)),
                  pl.BlockSpec((1, MLA_KV_RANK), lambda i: (0, 0))],
        out_specs=[pl.BlockSpec((tm, MLA_Q_RANK), lambda i: (i, 0)),
                   pl.BlockSpec((tm, MLA_KV_RANK), lambda i: (i, 0))],
        out_shape=[_sds((t, MLA_Q_RANK), BF16), _sds((t, MLA_KV_RANK), BF16)],
        compiler_params=_cparams(("parallel",)),
    )(proj2, qa, kva)


def _mla_mid_bwd(proj2, qa, kva, dcq, dckv, dkr):
    t = proj2.shape[0]
    tm = _row_tile(t)

    def body(p_ref, qa_ref, kva_ref, dcq_ref, dckv_ref, dkr_ref, dp_ref, dqa_ref, dkva_ref):
        @pl.when(pl.program_id(0) == 0)
        def _():
            dqa_ref[...] = jnp.zeros_like(dqa_ref)
            dkva_ref[...] = jnp.zeros_like(dkva_ref)

        dxq, dgq = _rms_bwd_rows(dcq_ref[...], p_ref[:, :MLA_Q_RANK], qa_ref[...], MLA_Q_RANK)
        dxk, dgk = _rms_bwd_rows(dckv_ref[...], p_ref[:, MLA_Q_RANK:MLA_Q_RANK + MLA_KV_RANK], kva_ref[...],
                                 MLA_KV_RANK)
        dp_ref[:, :MLA_Q_RANK] = dxq.astype(BF16)
        dp_ref[:, MLA_Q_RANK:MLA_Q_RANK + MLA_KV_RANK] = dxk.astype(BF16)
        dp_ref[:, MLA_Q_RANK + MLA_KV_RANK:] = dkr_ref[...].astype(BF16)
        dqa_ref[...] += jnp.sum(dgq, axis=0, keepdims=True)
        dkva_ref[...] += jnp.sum(dgk, axis=0, keepdims=True)

    return pl.pallas_call(
        body, name="mla_mid_bwd", grid=(t // tm,),
        in_specs=[pl.BlockSpec((tm, MLA_IN_PAD), lambda i: (i, 0)),
                  pl.BlockSpec((1, MLA_Q_RANK), lambda i: (0, 0)),
                  pl.BlockSpec((1, MLA_KV_RANK), lambda i: (0, 0)),
                  pl.BlockSpec((tm, MLA_Q_RANK), lambda i: (i, 0)),
                  pl.BlockSpec((tm, MLA_KV_RANK), lambda i: (i, 0)),
                  pl.BlockSpec((tm, 128), lambda i: (i, 0))],
        out_specs=[pl.BlockSpec((tm, MLA_IN_PAD), lambda i: (i, 0)),
                   pl.BlockSpec((1, MLA_Q_RANK), lambda i: (0, 0)),
                   pl.BlockSpec((1, MLA_KV_RANK), lambda i: (0, 0))],
        out_shape=[_sds((t, MLA_IN_PAD), BF16), _sds((1, MLA_Q_RANK), F32), _sds((1, MLA_KV_RANK), F32)],
        compiler_params=_cparams(("arbitrary",)),
    )(proj2, qa, kva, dcq, dckv, dkr)


def _mla_prep_specs(t, tm):
    head = lambda w: pl.BlockSpec((None, tm, w), lambda i, h: (h, i, 0))
    return dict(
        head256=head(MLA_HD_PAD), head128=head(MLA_VD),
        cols256=pl.BlockSpec((tm, MLA_HD_PAD), lambda i, h: (i, h)),
        cq=pl.BlockSpec((tm, MLA_Q_RANK), lambda i, h: (i, 0)),
        ckv=pl.BlockSpec((tm, MLA_KV_RANK), lambda i, h: (i, 0)),
        wuq=pl.BlockSpec((None, MLA_Q_RANK, MLA_HD_PAD), lambda i, h: (h, 0, 0)),
        wukv=pl.BlockSpec((None, MLA_KV_RANK, MLA_HD_PAD), lambda i, h: (h, 0, 0)),
        kr=pl.BlockSpec((tm, 128), lambda i, h: (i, (MLA_Q_RANK + MLA_KV_RANK) // 128)),
        gain=pl.BlockSpec((1, MLA_HD_PAD), lambda i, h: (0, 0)),
        tab=pl.BlockSpec((tm, 128), lambda i, h: (i, 0)),
    )


def _mla_prep(cq, ckv, wuq, wukv, proj2, gq, gk, tabs):
    t = cq.shape[0]
    tm = _row_tile(t, PREP_ROWS)
    sp = _mla_prep_specs(t, tm)

    def body(cq_ref, ckv_ref, wuq_ref, wukv_ref, kr_ref, gq_ref, gk_ref, c_ref, s1_ref, s2_ref,
             qh_ref, kh_ref, vh_ref):
        c, s1, s2 = c_ref[...], s1_ref[...], s2_ref[...]

        def norm_rope(xv, gain):
            r = lax.rsqrt(jnp.sum(xv * xv, axis=-1, keepdims=True) / MLA_QKD + EPS)
            y = xv * r * gain
            return jnp.concatenate([y[:, :MLA_NOPE], _rope_tile(y[:, MLA_NOPE:], c, s1, s2)], axis=-1)

        kvv = _dot(ckv_ref[...], wukv_ref[...], 1, 0)
        qh_ref[...] = norm_rope(_dot(cq_ref[...], wuq_ref[...], 1, 0), gq_ref[...]).astype(BF16)
        kf = jnp.concatenate([kvv[:, :MLA_NOPE], kr_ref[...]], axis=-1)
        kh_ref[...] = norm_rope(kf, gk_ref[...]).astype(BF16)
        vh_ref[...] = jnp.concatenate([kvv[:, MLA_NOPE:], jnp.ones((tm, MLA_VD), F32)], axis=-1).astype(BF16)

    return pl.pallas_call(
        body, name="mla_prep", grid=(t // tm, MLA_HEADS),
        in_specs=[sp['cq'], sp['ckv'], sp['wuq'], sp['wukv'], sp['kr'], sp['gain'], sp['gain'],
                  sp['tab'], sp['tab'], sp['tab']],
        out_specs=[sp['head256'], sp['head256'], sp['head256']],
        out_shape=[_sds((MLA_HEADS, t, MLA_HD_PAD), BF16), _sds((MLA_HEADS, t, MLA_HD_PAD), BF16),
                   _sds((MLA_HEADS, t, 2 * MLA_VD), BF16)],
        compiler_params=_cparams(("parallel", "arbitrary")),
    )(cq, ckv, wuq, wukv, proj2, gq, gk, *tabs)


def _mla_prep_bwd(cq, ckv, wuq, wukv, proj2, gq, gk, tabs, dqt, dkh, dvh):
    t = cq.shape[0]
    tm = _row_tile(t, PREP_ROWS)
    ab = dqt.shape[-1]
    sp = _mla_prep_specs(t, tm)

    def body(cq_ref, ckv_ref, wuq_ref, wukv_ref, kr_ref, gq_ref, gk_ref, c_ref, s1_ref, s2_ref,
             dqt_ref, dkh_ref, dvh_ref, dq_ref, dkv_ref, dkr_ref, dgq_ref, dgk_ref):
        dqh = jnp.concatenate([dqt_ref[b].T for b in range(tm // ab)], axis=0)
        i, h = pl.program_id(0), pl.program_id(1)

        @pl.when((i == 0) & (h == 0))
        def _():
            dgq_ref[...] = jnp.zeros_like(dgq_ref)
            dgk_ref[...] = jnp.zeros_like(dgk_ref)

        @pl.when(h == 0)
        def _():
            dkr_ref[...] = jnp.zeros_like(dkr_ref)

        c, s1, s2 = c_ref[...], s1_ref[...], s2_ref[...]

        def back(xv, gain, dout):
            dy = jnp.concatenate([dout[:, :MLA_NOPE], _rope_tile(dout[:, MLA_NOPE:], c, -s1, -s2)], axis=-1)
            return _rms_bwd_rows(dy, xv, gain, MLA_QKD)

        kvv = _dot(ckv_ref[...], wukv_ref[...], 1, 0)
        dxq, dgq = back(_dot(cq_ref[...], wuq_ref[...], 1, 0), gq_ref[...], dqh)
        kf = jnp.concatenate([kvv[:, :MLA_NOPE], kr_ref[...]], axis=-1)
        dxk, dgk = back(kf, gk_ref[...], dkh_ref[...])
        dq_ref[...] = dxq.astype(BF16)
        dkv_ref[...] = jnp.concatenate([dxk[:, :MLA_NOPE], dvh_ref[...]], axis=-1).astype(BF16)
        dkr_ref[...] += dxk[:, MLA_NOPE:]
        dgq_ref[...] += jnp.sum(dgq, axis=0, keepdims=True)
        dgk_ref[...] += jnp.sum(dgk, axis=0, keepdims=True)

    return pl.pallas_call(
        body, name="mla_prep_bwd", grid=(t // tm, MLA_HEADS),
        in_specs=[sp['cq'], sp['ckv'], sp['wuq'], sp['wukv'], sp['kr'], sp['gain'], sp['gain'],
                  sp['tab'], sp['tab'], sp['tab'],
                  pl.BlockSpec((None, tm // ab, MLA_HD_PAD, ab), lambda i, h: (h, i, 0, 0)),
                  sp['head256'], sp['head128']],
        out_specs=[sp['cols256'], sp['cols256'], sp['tab'], sp['gain'], sp['gain']],
        out_shape=[_sds((t, MLA_HEADS * MLA_HD_PAD), BF16), _sds((t, MLA_HEADS * MLA_HD_PAD), BF16),
                   _sds((t, 128), F32), _sds((1, MLA_HD_PAD), F32), _sds((1, MLA_HD_PAD), F32)],
        compiler_params=_cparams(("arbitrary", "arbitrary")),
    )(cq, ckv, wuq, wukv, proj2, gq, gk, *tabs, dqt, dkh, dvh)


def _chunk_visible(rows, cols, row_off, col_off):
    rq = lax.shift_right_logical(lax.broadcasted_iota(jnp.int32, (rows, cols), 0) + row_off, 6)
    ck = lax.shift_right_logical(lax.broadcasted_iota(jnp.int32, (rows, cols), 1) + col_off, 6)
    return ck <= rq


def _rows_to_lanes(col):
    return col.T[:8, :]


def _attn_fwd(qh, kh, vh):
    t = qh.shape[1]
    ab = min(ATT_BLOCK, t)
    tq = min(ATT_QROWS, t)
    r = tq // ab
    hg = ATT_HEADS

    def body(q_ref, k_ref, v_ref, o_ref, lse_ref, acc_ref):
        n_un = pl.program_id(1) * r
        acc_ref[...] = jnp.zeros_like(acc_ref)

        def step(b, ms, diag):
            rows = pl.ds(pl.multiple_of(b * ab, ab), ab)
            out = []
            for hh in range(hg):
                m = ms[hh]
                s = _dot(q_ref[hh], k_ref[hh, rows, :], 1, 1)
                if diag is not None:
                    s = jnp.where(_chunk_visible(tq, ab, 0, diag * ab), s, -1e30)
                m_new = jnp.maximum(m, jnp.max(s, axis=-1, keepdims=True))
                p = jnp.exp2((s - m_new) * ATT_EXP2).astype(BF16)
                acc_ref[hh] = jnp.exp2((m - m_new) * ATT_EXP2) * acc_ref[hh] + _dot(p, v_ref[hh, rows, :], 1, 0)
                out.append(m_new)
            return tuple(out)

        ms = tuple(jnp.full((tq, 1), -1e30, F32) for _ in range(hg))
        ms = lax.fori_loop(0, n_un, lambda b, st: step(b, st, None), ms)
        for d in range(r):
            ms = step(n_un + d, ms, d)
        for hh in range(hg):
            l = acc_ref[hh, :, MLA_VD:]
            o_ref[:, hh * MLA_VD:(hh + 1) * MLA_VD] = (acc_ref[hh, :, :MLA_VD] / l).astype(BF16)
            lse_t = _rows_to_lanes(ms[hh] * ATT_EXP2 + jnp.log(l) * LOG2E)
            for d in range(r):
                lse_ref[hh, d] = lse_t[:, d * ab:(d + 1) * ab]

    return pl.pallas_call(
        body, name="mla_attn", grid=(MLA_HEADS // hg, t // tq),
        in_specs=[pl.BlockSpec((hg, tq, MLA_HD_PAD), lambda g, i: (g, i, 0)),
                  pl.BlockSpec((hg, t, MLA_HD_PAD), lambda g, i: (g, 0, 0)),
                  pl.BlockSpec((hg, t, 2 * MLA_VD), lambda g, i: (g, 0, 0))],
        out_specs=[pl.BlockSpec((tq, hg * MLA_VD), lambda g, i: (i, g)),
                   pl.BlockSpec((hg, r, 8, ab), lambda g, i: (g, i, 0, 0))],
        out_shape=[_sds((t, MLA_HEADS * MLA_VD), BF16), _sds((MLA_HEADS, t // ab, 8, ab), F32)],
        scratch_shapes=[pltpu.VMEM((hg, tq, 2 * MLA_VD), F32)],
        compiler_params=_cparams(("parallel", "arbitrary")),
    )(qh, kh, vh)


def _attn_bwd(qh, kh, vh, dob, o, lse_t):
    t = qh.shape[1]
    ab = min(ATT_BLOCK, t)
    kb = min(ATT_KROWS, t)
    r = kb // ab
    nq = t // ab
    hg = ATT_HEADS

    def body(q_ref, k_ref, v_ref, do_ref, o_ref, lse_ref, dqt_ref, dk_ref, dv_ref, dl_ref):
        j = pl.program_id(1)

        @pl.when(j == 0)
        def _():
            dqt_ref[...] = jnp.zeros_like(dqt_ref)
            ones = jnp.ones((8, MLA_VD), F32)

            def delta(b, carry):
                rows = pl.ds(pl.multiple_of(b * ab, ab), ab)
                for hh in range(hg):
                    cols = slice(hh * MLA_VD, (hh + 1) * MLA_VD)
                    prod = do_ref[rows, cols].astype(F32) * o_ref[rows, cols].astype(F32)
                    dl_ref[hh, b] = lax.dot_general(ones, prod, (((1,), (1,)), ((), ())),
                                                    precision=lax.Precision.HIGHEST, preferred_element_type=F32)
                return carry

            lax.fori_loop(0, nq, delta, 0)

        ks = [k_ref[hh] for hh in range(hg)]
        vs = [v_ref[hh, :, :MLA_VD] for hh in range(hg)]
        kts = [k.T for k in ks]

        dk_ref[...] = jnp.zeros_like(dk_ref)
        dv_ref[...] = jnp.zeros_like(dv_ref)

        def step(b, carry, diag):
            rows = pl.ds(pl.multiple_of(b * ab, ab), ab)
            hi = kb if diag is None else (diag + 1) * ab
            for hh in range(hg):
                q = q_ref[hh, rows, :]
                do = do_ref[rows, hh * MLA_VD:(hh + 1) * MLA_VD]
                s_t = _dot(ks[hh][:hi], q, 1, 1)
                if diag is not None:
                    key_chunk = lax.shift_right_logical(lax.broadcasted_iota(jnp.int32, (hi, ab), 0), 6)
                    query_chunk = lax.shift_right_logical(
                        lax.broadcasted_iota(jnp.int32, (hi, ab), 1) + diag * ab, 6)
                    s_t = jnp.where(key_chunk <= query_chunk, s_t, -1e30)
                p_t = jnp.exp2(s_t * ATT_EXP2 - lse_ref[hh, b][0:1, :])
                dp_t = _dot(vs[hh][:hi], do, 1, 1)
                ds_t = (p_t * (dp_t - dl_ref[hh, b][0:1, :]) * ATT_SCALE).astype(BF16)
                dqt_ref[hh, b] += _dot(kts[hh][:, :hi], ds_t, 1, 0)
                dk_ref[hh, :hi] += _dot(ds_t, q, 1, 0)
                dv_ref[hh, :hi] += _dot(p_t.astype(BF16), do, 1, 0)
            return carry

        for d in range(r):
            step(j * r + d, 0, d)
        lax.fori_loop((j + 1) * r, nq, lambda b, c: step(b, c, None), 0)

    whole = lambda w: pl.BlockSpec((hg, t, w), lambda g, j: (g, 0, 0))
    blk = lambda w: pl.BlockSpec((hg, kb, w), lambda g, j: (g, j, 0))
    stat = pl.BlockSpec((hg, nq, 8, ab), lambda g, j: (g, 0, 0, 0))
    cols = pl.BlockSpec((t, hg * MLA_VD), lambda g, j: (0, g))
    return pl.pallas_call(
        body, name="mla_attn_bwd", grid=(MLA_HEADS // hg, t // kb),
        in_specs=[whole(MLA_HD_PAD), blk(MLA_HD_PAD), blk(2 * MLA_VD),
                  cols, cols, stat],
        out_specs=[pl.BlockSpec((hg, nq, MLA_HD_PAD, ab), lambda g, j: (g, 0, 0, 0)), blk(MLA_HD_PAD), blk(MLA_VD)],
        out_shape=[_sds((MLA_HEADS, nq, MLA_HD_PAD, ab), F32), _sds((MLA_HEADS, t, MLA_HD_PAD), F32),
                   _sds((MLA_HEADS, t, MLA_VD), F32)],
        scratch_shapes=[pltpu.VMEM((hg, nq, 8, ab), F32)],
        compiler_params=_cparams(("parallel", "arbitrary")),
    )(qh, kh, vh, dob, o, lse_t)


VEC = pl.BlockSpec((1, D_MODEL), lambda i, j, k: (0, 0))


def _rows(tm, width):
    return pl.BlockSpec((tm, width), lambda i, j, k: (i, 0))


def _residual_epi(next_gain):
    if next_gain is None:
        return [], lambda acc, hv: (acc + hv,)

    def epi(acc, hv, g):
        h_new = acc + hv
        r = lax.rsqrt(jnp.mean(h_new * h_new, axis=-1, keepdims=True) + EPS)
        return h_new, h_new * r * g

    return [(next_gain, VEC)], epi


def _residual_outs(t, row, next_gain):
    outs = [(_sds((t, D_MODEL), F32), row)]
    return outs + ([(_sds((t, D_MODEL), BF16), row)] if next_gain is not None else [])


def _mlp_fwd(l, h, hn, w1g, fetch_w2, next_gain):
    t = h.shape[0]
    tm = _row_tile(t, 512)

    def relu2(acc):
        r = jnp.maximum(acc, 0.0)
        return (r * r,)

    (u,) = _mm_rows(f"mlp_up{l}", tm, hn, w1g, 'nn_cols', [(_sds((t, D_FF), BF16), _rows(tm, D_FF))], epi=relu2)
    w2g = fetch_w2((u,))
    row = _rows(tm, D_MODEL)
    more, epi = _residual_epi(next_gain)
    h2, hn_next = _mm_rows(f"mlp_down{l}", tm, u, w2g, 'nn_rows', _residual_outs(t, row, next_gain),
                           extras=[(h, row)] + more, epi=epi)
    return h2, hn_next, (h, hn, u, w1g, w2g)


def _norm_bwd_outs(t, tm):
    return [(_sds((t, D_MODEL), F32), pl.BlockSpec((tm, D_MODEL), lambda i, j, k: (i, 0))),
            (_sds((t // tm, 1, D_MODEL), F32), pl.BlockSpec((None, 1, D_MODEL), lambda i, j, k: (i, 0, 0)))]


def _norm_bwd_epi(acc, xv, res, g):
    dx, dgr = _rms_bwd_rows(acc, xv, g, D_MODEL)
    return res + dx, jnp.sum(dgr, axis=0, keepdims=True)


def _mlp_bwd(l, dh, saved, norm_g, emit_w2=None, emit_w1=None):
    h, hn, u, w1g, w2g = saved
    t = h.shape[0]
    tm = _row_tile(t, 512)
    nsh, _, wsh = w1g.shape
    wide = _rows(tm, D_FF)
    (da,) = _mm_rows(f"mlp_du{l}", tm, dh, w2g, 'nt_rows', [(_sds((t, D_FF), BF16), wide)], extras=[(u, wide)],
                     epi=lambda acc, uv: (2.0 * jnp.sqrt(uv.astype(F32)) * acc,))
    tw = _row_tile(t, 512)
    (dw2,) = _mm(f"mlp_dw2{l}", (1, 1, t // tw),
                 u, pl.BlockSpec((tw, D_FF), lambda i, j, k: (k, 0)),
                 dh, pl.BlockSpec((tw, D_MODEL), lambda i, j, k: (k, 0)), (0, 0),
                 [(_sds((D_FF, D_MODEL), BF16), pl.BlockSpec((D_FF, D_MODEL), lambda i, j, k: (0, 0)))])
    dw2 = dw2.reshape(nsh, wsh, D_MODEL)
    (dw1,) = _mm(f"mlp_dw1{l}", (1, 1, t // tw),
                 hn, pl.BlockSpec((tw, D_MODEL), lambda i, j, k: (k, 0)),
                 da, pl.BlockSpec((tw, D_FF), lambda i, j, k: (k, 0)), (0, 0),
                 [(_sds((nsh, D_MODEL, wsh), BF16), pl.BlockSpec((nsh, D_MODEL, wsh), lambda i, j, k: (0, 0, 0)))],
                 split=wsh, deps=emit_w2(dw2) if emit_w2 else ())
    row = _rows(tm, D_MODEL)
    dh_in, dg = _mm_rows(f"mlp_dhn{l}", tm, da, w1g, 'nt_cols', _norm_bwd_outs(t, tm),
                         extras=[(h, row), (dh, row), (norm_g, VEC)], epi=_norm_bwd_epi,
                         deps=emit_w1(dw1) if emit_w1 else ())
    return dh_in, jnp.sum(dg, axis=0), dw1, dw2


def _ple_fwd(l, h, hn, p, wg, wp, next_gain, target=None):
    t = h.shape[0]
    tm = _row_tile(t, 512)
    row = pl.BlockSpec((tm, D_MODEL), lambda i, j, k: (i, 0))
    full = lambda r: pl.BlockSpec((r, D_MODEL), lambda i, j, k: (0, 0))
    f32_row, bf_row = (_sds((t, D_MODEL), F32), row), (_sds((t, D_MODEL), BF16), row)
    common = [(h, row), (p, pl.BlockSpec((None, None, tm, PLE_DIM), lambda i, j, k: (l, 0, i, 0))),
              (wp, full(PLE_DIM))]
    if target is not None:
        def loss_epi(acc, hv, pv, wpv, tv):
            gt = _sigmoid(acc)
            ev = _dot(_bf(pv), wpv, 1, 0)
            err = hv + gt * ev - tv
            sq = jnp.sum(jnp.sum(err * err, axis=-1, keepdims=True), axis=0, keepdims=True)
            return err / D_MODEL, gt, ev, jnp.broadcast_to(sq, (8, 128))

        dy, gate, e, sq = _mm(f"ple_gate{l}", (t // tm, 1, 1), hn, row, wg, full(D_MODEL), (1, 0),
                              [f32_row, bf_row, bf_row, (_sds((t // tm, 8, 128), F32),
                                                         pl.BlockSpec((None, 8, 128), lambda i, j, k: (i, 0, 0)))],
                              extras=common + [(target, row)], epi=loss_epi)
        return dy, jnp.sum(sq, axis=0), (h, hn, gate, e)

    def gate_epi(acc, hv, pv, wpv, *gain):
        gt = _sigmoid(acc)
        ev = _dot(_bf(pv), wpv, 1, 0)
        h_new = hv + gt * ev
        if not gain:
            return h_new, gt, ev
        r = lax.rsqrt(jnp.mean(h_new * h_new, axis=-1, keepdims=True) + EPS)
        return h_new, gt, ev, h_new * r * gain[0]

    res = _mm(f"ple_gate{l}", (t // tm, 1, 1), hn, row, wg, full(D_MODEL), (1, 0),
              [f32_row, bf_row, bf_row] + ([bf_row] if next_gain is not None else []),
              extras=common + ([(next_gain, VEC)] if next_gain is not None else []), epi=gate_epi)
    h_out, gate, e = res[0], res[1], res[2]
    return h_out, (res[3] if next_gain is not None else None), (h, hn, gate, e)


def _ple_bwd(l, dh, saved, p, norm_g, wg, deps=(), emit=None):
    h, hn, gate, e = saved
    t = h.shape[0]
    tm = _row_tile(t)
    tk = _row_tile(t, 512)
    de, dz = _ple_gate_bwd(f"ple_gate_bwd{l}", dh, gate, e)
    full = lambda r: pl.BlockSpec((r, D_MODEL), lambda i, j, k: (0, 0))
    rowk = pl.BlockSpec((tk, D_MODEL), lambda i, j, k: (k, 0))
    (dwp,) = _mm(f"ple_dwp{l}", (1, 1, t // tk),
                 p, pl.BlockSpec((None, None, tk, PLE_DIM), lambda i, j, k: (l, 0, k, 0)),
                 de, rowk, (0, 0), [(_sds((PLE_DIM, D_MODEL), BF16), full(PLE_DIM))], deps=deps)
    (dwg,) = _mm(f"ple_dwg{l}", (1, 1, t // tk), hn, rowk, dz, rowk, (0, 0),
                 [(_sds((D_MODEL, D_MODEL), BF16), full(D_MODEL))])
    row = pl.BlockSpec((tm, D_MODEL), lambda i, j, k: (i, 0))
    dh_in, dg = _mm(f"ple_dhn{l}", (t // tm, 1, 1), dz, row, wg, full(D_MODEL), (1, 1),
                    _norm_bwd_outs(t, tm), extras=[(h, row), (dh, row), (norm_g, VEC)], epi=_norm_bwd_epi,
                    deps=emit(dwg, dwp) if emit else ())
    return dh_in, jnp.sum(dg, axis=0), dwg, dwp


def _ret_layer_fwd(x, norm_g, wri, fetch_wro, gn, cos, sin, next_gain, hn=None, deps=()):
    t = x.shape[0]
    tm = _row_tile(t)
    nsh, _, wsh = wri.shape
    if hn is None:
        hn = _rms_fwd("mix_norm0", x, norm_g)
    tp = _row_tile(t, 512)
    (proj,) = _mm_rows("ret_in", tp, hn, wri, 'nn_cols', [(_sds((t, RET_IN), BF16), _rows(tp, RET_IN))], deps=deps)
    gated, outp, states = _ret_fwd(proj, cos, sin, gn)
    wro = fetch_wro((gated,))
    row = _rows(tp, D_MODEL)
    more, epi = _residual_epi(next_gain)
    h1, hn_next = _mm_rows("ret_out", tp, gated, wro.reshape(RET_HEADS, RET_DV, D_MODEL), 'nn_rows',
                           _residual_outs(t, row, next_gain), extras=[(x, row)] + more, epi=epi)
    return h1, hn_next, (x, hn, proj, gated, outp, states, wro)


def _ret_layer_bwd(dh, saved, norm_g, wri, gn, cos, sin, emit_out, emit_in, deps=()):
    x, hn, proj, gated, outp, states, wro = saved
    t = x.shape[0]
    tm = _row_tile(t)
    tk = _row_tile(t, 512)
    nsh, _, wsh = wri.shape
    tg = _row_tile(t, 512)
    vw = _rows(tg, RET_V_W)
    dout, dgate, dgn = _mm_rows(
        "ret_dgate", tg, dh, wro.reshape(RET_HEADS, RET_DV, D_MODEL), 'nt_rows',
        [(_sds((t, RET_V_W), BF16), vw), (_sds((t, RET_V_W), BF16), vw),
         (_sds((t // tg, 1, RET_V_W), F32), pl.BlockSpec((None, 1, RET_V_W), lambda i, j, k: (i, 0, 0)))],
        extras=[(outp, vw), (proj, pl.BlockSpec((tg, RET_V_W), lambda i, j, k: (i, (RET_IN - RET_V_W) // RET_V_W))),
                (gn.reshape(1, RET_V_W), pl.BlockSpec((1, RET_V_W), lambda i, j, k: (0, 0)))],
        epi=_ret_gate_bwd_epi, deps=deps)
    dgn = jnp.sum(dgn, axis=0)
    (dwro,) = _mm("ret_dwro", (1, 1, t // tk),
                  gated, pl.BlockSpec((tk, RET_V_W), lambda i, j, k: (k, 0)),
                  dh, pl.BlockSpec((tk, D_MODEL), lambda i, j, k: (k, 0)), (0, 0),
                  [(_sds((RET_V_W, D_MODEL), BF16), pl.BlockSpec((RET_V_W, D_MODEL), lambda i, j, k: (0, 0)))])
    dproj = _ret_bwd(proj, cos, sin, states, dout, dgate, deps=emit_out(dwro))
    half = nsh // 2
    (dwri,) = _mm("ret_dwri", (2, 1, t // tk),
                  hn, pl.BlockSpec((tk, D_MODEL), lambda i, j, k: (k, 0)),
                  dproj, pl.BlockSpec((tk, half * wsh), lambda i, j, k: (k, i)), (0, 0),
                  [(_sds((nsh, D_MODEL, wsh), BF16), pl.BlockSpec((half, D_MODEL, wsh), lambda i, j, k: (i, 0, 0)))],
                  split=wsh)
    deps = emit_in(dwri)
    td = _row_tile(t, 256)
    row = _rows(td, D_MODEL)
    dx, dg = _mm_rows("ret_dhn", td, dproj, wri, 'nt_cols', _norm_bwd_outs(t, td),
                      extras=[(x, row), (dh, row), (norm_g, VEC)], epi=_norm_bwd_epi, deps=deps)
    return dx, jnp.sum(dg, axis=0), dgn.reshape(RET_HEADS, RET_DV)


def _mla_layer_fwd(h, hn, fetch, qa, kva, gq, gk, tabs, next_gain):
    t = h.shape[0]
    tm = _row_tile(t)
    row = pl.BlockSpec((tm, D_MODEL), lambda i, j, k: (i, 0))
    wmi = fetch('mla_in', (h,))['mla_w_in']
    (proj2,) = _mm("mla_in", (t // tm, 1, 1), hn, row,
                   wmi, pl.BlockSpec((D_MODEL, MLA_IN_PAD), lambda i, j, k: (0, 0)), (1, 0),
                   [(_sds((t, MLA_IN_PAD), F32), pl.BlockSpec((tm, MLA_IN_PAD), lambda i, j, k: (i, 0)))])
    cq, ckv = _mla_mid(proj2, qa, kva)
    up = fetch('mla_up', (cq,))
    wuq, wukv = up['mla_w_uq'], up['mla_w_ukv']
    qh, kh, vh = _mla_prep(cq, ckv, wuq, wukv, proj2, gq, gk, tabs)
    o, lse = _attn_fwd(qh, kh, vh)
    wmo = fetch('mla_out', (o,))['mla_w_out']
    more, epi = _residual_epi(next_gain)
    h_out, hn_next = _mm("mla_out", (t // tm, 1, 1), o, row,
                         wmo, pl.BlockSpec((D_MODEL, D_MODEL), lambda i, j, k: (0, 0)), (1, 0),
                         _residual_outs(t, row, next_gain), extras=[(h, row)] + more, epi=epi)
    return h_out, hn_next, (h, hn, proj2, cq, ckv, qh, kh, vh, o, lse), (wmi, wuq, wukv, wmo)


def _mla_layer_bwd(dh, saved, norm_g, wmi, qa, kva, wuq, wukv, gq, gk, wmo, tabs, deps=()):
    h, hn, proj2, cq, ckv, qh, kh, vh, o, lse = saved
    t = h.shape[0]
    tm = _row_tile(t)
    tk = _row_tile(t, 512)
    row = pl.BlockSpec((tm, D_MODEL), lambda i, j, k: (i, 0))
    rowk = pl.BlockSpec((tk, D_MODEL), lambda i, j, k: (k, 0))
    sq = pl.BlockSpec((D_MODEL, D_MODEL), lambda i, j, k: (0, 0))
    (dob,) = _mm("mla_do", (t // tm, 1, 1), dh, row, wmo, sq, (1, 1), [(_sds((t, D_MODEL), BF16), row)], deps=deps)
    (dwmo,) = _mm("mla_dwo", (1, 1, t // tk), o, rowk, dh, rowk, (0, 0), [(_sds((D_MODEL, D_MODEL), BF16), sq)])
    dqt, dkh, dvh = _attn_bwd(qh, kh, vh, dob, o, lse)
    dq, dkv, dkr, dgq, dgk = _mla_prep_bwd(cq, ckv, wuq, wukv, proj2, gq, gk, tabs, dqt, dkh, dvh)

    wide = MLA_HEADS * MLA_HD_PAD
    widek = pl.BlockSpec((tk, wide), lambda i, j, k: (k, 0))
    (dwuq,) = _mm("mla_dwuq", (1, 1, t // tk),
                  cq, pl.BlockSpec((tk, MLA_Q_RANK), lambda i, j, k: (k, 0)), dq, widek, (0, 0),
                  [(_sds((MLA_HEADS, MLA_Q_RANK, MLA_HD_PAD), BF16),
                    pl.BlockSpec((MLA_HEADS, MLA_Q_RANK, MLA_HD_PAD), lambda i, j, k: (0, 0, 0)))], split=MLA_HD_PAD)
    (dwukv,) = _mm("mla_dwukv", (1, 1, t // tk),
                   ckv, pl.BlockSpec((tk, MLA_KV_RANK), lambda i, j, k: (k, 0)), dkv, widek, (0, 0),
                   [(_sds((MLA_HEADS, MLA_KV_RANK, MLA_HD_PAD), BF16),
                     pl.BlockSpec((MLA_HEADS, MLA_KV_RANK, MLA_HD_PAD), lambda i, j, k: (0, 0, 0)))],
                   split=MLA_HD_PAD)
    side_by_side = lambda wg: wg.transpose(1, 0, 2).reshape(wg.shape[1], wide)
    widei = pl.BlockSpec((tm, wide), lambda i, j, k: (i, 0))
    (dcq,) = _mm("mla_dcq", (t // tm, 1, 1), dq, widei,
                 side_by_side(wuq), pl.BlockSpec((MLA_Q_RANK, wide), lambda i, j, k: (0, 0)), (1, 1),
                 [(_sds((t, MLA_Q_RANK), F32), pl.BlockSpec((tm, MLA_Q_RANK), lambda i, j, k: (i, 0)))])
    (dckv,) = _mm("mla_dckv", (t // tm, 1, 1), dkv, widei,
                  side_by_side(wukv), pl.BlockSpec((MLA_KV_RANK, wide), lambda i, j, k: (0, 0)), (1, 1),
                  [(_sds((t, MLA_KV_RANK), F32), pl.BlockSpec((tm, MLA_KV_RANK), lambda i, j, k: (i, 0)))])
    dproj2, dqa, dkva = _mla_mid_bwd(proj2, qa, kva, dcq, dckv, dkr)
    win = pl.BlockSpec((D_MODEL, MLA_IN_PAD), lambda i, j, k: (0, 0))
    (dwmi,) = _mm("mla_dwin", (1, 1, t // tk), hn, rowk,
                  dproj2, pl.BlockSpec((tk, MLA_IN_PAD), lambda i, j, k: (k, 0)), (0, 0),
                  [(_sds((D_MODEL, MLA_IN_PAD), BF16), win)])
    dh_in, dg = _mm("mla_dhn", (t // tm, 1, 1),
                    dproj2, pl.BlockSpec((tm, MLA_IN_PAD), lambda i, j, k: (i, 0)), wmi, win, (1, 1),
                    _norm_bwd_outs(t, tm), extras=[(h, row), (dh, row), (norm_g, VEC)], epi=_norm_bwd_epi)
    return dh_in, dict(mix=jnp.sum(dg, axis=0), wmi=dwmi, qa=dqa, kva=dkva, wuq=dwuq, wukv=dwukv, gq=dgq, gk=dgk,
                       wmo=dwmo)


def _local_step(x, p, target, w, fetch, emit=lambda group: ()):
    t = x.shape[0]
    cos_r, sin_r, tabs = w['tables'] if 'tables' in w else _rope_tables(t, 0.0)
    row = lambda a, i: a[i:i + 1]

    h1, hn1, s_ret = _ret_layer_fwd(x, row(w['mix_norm'], 0), w['ret_w_in'],
                                    lambda after: fetch('ret_out', after)['ret_w_out'], w['ret_gn'], cos_r, sin_r,
                                    row(w['mlp_norm'], 0), hn=w.get('hn0'), deps=w['deps'])
    h2, hn2, s_mlp0 = _mlp_fwd(0, h1, hn1, fetch('mlp_w1_0', (h1,))['mlp_w1'],
                               lambda after: fetch('mlp_w2_0', after)['mlp_w2'], row(w['ple_norm'], 0))
    w0 = fetch('ple_0', (h2,))
    h3, hn3, s_ple0 = _ple_fwd(0, h2, hn2, p, w0['ple_gate_w'], w0['ple_proj_w'], row(w['mix_norm'], 1))
    h4, hn4, s_mla, (wmi, wuq, wukv, wmo) = _mla_layer_fwd(
        h3, hn3, fetch, w['mla_q_a_norm'], w['mla_kv_a_norm'], w['mla_q_norm'], w['mla_k_norm'], tabs,
        row(w['mlp_norm'], 1))
    mla_w = (wmi, w['mla_q_a_norm'], w['mla_kv_a_norm'], wuq, wukv, w['mla_q_norm'], w['mla_k_norm'], wmo, tabs)
    w1 = fetch('layer_1', (h4,))
    h5, hn5, s_mlp1 = _mlp_fwd(1, h4, hn4, w1['mlp_w1'], lambda after: w1['mlp_w2'], row(w['ple_norm'], 1))
    dy, sq_err, s_ple1 = _ple_fwd(1, h5, hn5, p, w1['ple_gate_w'], w1['ple_proj_w'], None, target)

    n = N_DEV
    colsh = lambda a: a.reshape(a.shape[0], n, a.shape[1] // n).transpose(1, 0, 2)
    rowsh = lambda a: a.reshape(n, a.shape[0] // n, a.shape[1])
    big = {}

    def emit_group(group):
        big.update(group)
        return emit(group)

    dh5, dg_ple1, dwg1, dwp1 = _ple_bwd(1, dy, s_ple1, p, row(w['ple_norm'], 1), w1['ple_gate_w'])
    dh4, dg_mlp1, dw1_1, dw2_1 = _mlp_bwd(1, dh5, s_mlp1, row(w['mlp_norm'], 1))
    deps = emit_group({('ple_gate_w', 1): rowsh(dwg1), ('ple_proj_w', 1): colsh(dwp1),
                       ('mlp_w2', 1): dw2_1, ('mlp_w1', 1): dw1_1})
    dh3, gm = _mla_layer_bwd(dh4, s_mla, row(w['mix_norm'], 1), *mla_w, deps=deps)
    deps = emit_group({('mla_w_out', 0): rowsh(gm['wmo']), ('mla_w_uq', 0): _gather_rope(gm['wuq']),
                       ('mla_w_ukv', 0): gm['wukv'], ('mla_w_in', 0): rowsh(_gather_rope(gm['wmi']))})
    dh2, dg_ple0, _, _ = _ple_bwd(
        0, dh3, s_ple0, p, row(w['ple_norm'], 0), w0['ple_gate_w'], deps=deps,
        emit=lambda dwg, dwp: emit_group({('ple_gate_w', 0): rowsh(dwg), ('ple_proj_w', 0): colsh(dwp)}))
    dh1, dg_mlp0, _, _ = _mlp_bwd(0, dh2, s_mlp0, row(w['mlp_norm'], 0),
                                  emit_w2=lambda dw2: emit_group({('mlp_w2', 0): dw2}),
                                  emit_w1=lambda dw1: emit_group({('mlp_w1', 0): dw1}))
    dx, dg_mix0, dgn = _ret_layer_bwd(
        dh1, s_ret, row(w['mix_norm'], 0), w['ret_w_in'], w['ret_gn'], cos_r, sin_r,
        lambda dwro: emit_group({('ret_w_out', 0): rowsh(dwro)}),
        lambda dwri: emit_group({('ret_w_in', 0): dwri}))

    small = dict(
        mix_norm=[dg_mix0, gm['mix']], mlp_norm=[dg_mlp0, dg_mlp1], ple_norm=[dg_ple0, dg_ple1],
        ret_gn=dgn, mla_q_a_norm=gm['qa'], mla_kv_a_norm=gm['kva'], mla_q_norm=gm['gq'], mla_k_norm=gm['gk'],
    )
    return sq_err, dx, big, small


def _my_place():
    x, y, c = lax.axis_index("x"), lax.axis_index("y"), lax.axis_index("c")
    return x, y, c


def _flat(px, py, pc):
    return 4 * px + 2 * py + pc


def _peer(x, y, c, r):
    return (1 - x if r & 4 else x, 1 - y if r & 2 else y, 1 - c if r & 1 else c)


HBM = pl.BlockSpec(memory_space=pltpu.HBM)
SEMS = pl.BlockSpec(memory_space=pltpu.SEMAPHORE)
SIDE_EFFECT = pltpu.SideEffectType.DATAFLOW_SIDE_EFFECTING


def _rs_copies(x, y, c, srcs, lands, send_sems, recv_sems):
    copies = []
    for a in range(len(srcs)):
        for r in range(1, N_DEV):
            peer = _peer(x, y, c, r)
            k = a * (N_DEV - 1) + r - 1
            copies.append(pltpu.make_async_remote_copy(
                src_ref=srcs[a].at[_flat(*peer)], dst_ref=lands[a].at[r - 1],
                send_sem=send_sems.at[k], recv_sem=recv_sems.at[k], device_id=peer, device_id_type=MESH))
    return copies


def _rs_start(name, arrays):
    n = len(arrays)
    hbm = lambda a: pltpu.with_memory_space_constraint(a, pltpu.HBM)
    lands = [hbm(lax.empty((N_DEV - 1,) + a.shape[1:], a.dtype)) for a in arrays]

    def body(*refs):
        srcs, lnd = refs[:n], refs[n:2 * n]
        send_sems, recv_sems = refs[2 * n], refs[2 * n + 1]
        token = refs[-1]
        for cp in _rs_copies(*_my_place(), srcs, lnd, send_sems, recv_sems):
            cp.start()
        token[...] = jnp.zeros_like(token)

    outs = pl.pallas_call(
        body, name=name,
        in_specs=[HBM] * (2 * n),
        out_specs=[SEMS, SEMS] + [HBM] * (2 * n) + [pl.BlockSpec(memory_space=pltpu.VMEM)],
        out_shape=[pltpu.SemaphoreType.DMA((n * (N_DEV - 1),)), pltpu.SemaphoreType.DMA((n * (N_DEV - 1),))]
        + [pltpu.HBM(a.shape, a.dtype) for a in arrays] + [pltpu.HBM(l.shape, l.dtype) for l in lands]
        + [_sds((8, 128), F32)],
        input_output_aliases={i: 2 + i for i in range(2 * n)},
        compiler_params=pltpu.CompilerParams(has_side_effects=SIDE_EFFECT),
    )(*[hbm(a) for a in arrays], *lands)
    return outs[0], outs[1], outs[2:2 + n], outs[2 + n:2 + 2 * n], outs[-1]


def _rs_wait(name, send_sems, recv_sems, srcs, lands, after):
    n = len(srcs)

    def body(*refs):
        src_refs, lnd = refs[:n], refs[n:2 * n]
        send, recv = refs[2 * n], refs[2 * n + 1]
        for cp in _rs_copies(*_my_place(), src_refs, lnd, send, recv):
            cp.wait_send()
            cp.wait_recv()

    outs = pl.pallas_call(
        body, name=name,
        in_specs=[HBM] * (2 * n) + [SEMS, SEMS] + [ANY] * len(after),
        out_specs=[HBM] * (2 * n),
        out_shape=[pltpu.HBM(a.shape, a.dtype) for a in list(srcs) + list(lands)],
        input_output_aliases={i: i for i in range(2 * n)},
        compiler_params=pltpu.CompilerParams(has_side_effects=SIDE_EFFECT),
    )(*srcs, *lands, send_sems, recv_sems, *after)
    return outs[:n], outs[n:]


SMALL_PACK_ROWS = 16


def _all_reduce_small(rows, deps=()):
    n = len(rows)

    def body(*refs):
        ins = refs[:n]
        out_ref, mine, buf, send_sems, recv_sems = refs[n + len(deps):]
        x, y, c = _my_place()
        mine[...] = jnp.zeros_like(mine)
        for (r0, a), ref in zip(rows, ins):
            mine[r0:r0 + a.shape[0], 0:a.shape[1]] = ref[...]
        buf[_flat(x, y, c)] = mine[...]
        copies = []
        for r in range(1, N_DEV):
            peer = _peer(x, y, c, r)
            send = pltpu.make_async_remote_copy(
                src_ref=mine, dst_ref=buf.at[_flat(x, y, c)],
                send_sem=send_sems.at[r - 1], recv_sem=recv_sems.at[r - 1], device_id=peer, device_id_type=MESH)
            send.start()
            recv = pltpu.make_async_remote_copy(
                src_ref=mine, dst_ref=buf.at[_flat(*peer)],
                send_sem=send_sems.at[r - 1], recv_sem=recv_sems.at[r - 1], device_id=peer, device_id_type=MESH)
            copies.append((send, recv))
        for send, recv in copies:
            send.wait_send()
            recv.wait_recv()
        acc = buf[0]
        for s in range(1, N_DEV):
            acc = acc + buf[s]
        out_ref[...] = acc

    vm = pl.BlockSpec(memory_space=pltpu.VMEM)
    shape = (SMALL_PACK_ROWS, D_MODEL)
    return pl.pallas_call(
        body, name="all_reduce_small", in_specs=[vm] * n + [ANY] * len(deps), out_specs=vm,
        out_shape=_sds(shape, F32),
        scratch_shapes=[pltpu.VMEM(shape, F32), pltpu.VMEM((N_DEV,) + shape, F32),
                        pltpu.SemaphoreType.DMA((7,)), pltpu.SemaphoreType.DMA((7,))],
    )(*[a for _, a in rows], *deps)


def _adamw_math(w, g, m, v):
    m = ADAM_B1 * m + (1.0 - ADAM_B1) * g
    v = ADAM_B2 * v + (1.0 - ADAM_B2) * (g * g)
    m_hat = m / (1.0 - ADAM_B1 ** ADAM_STEP)
    v_hat = v / (1.0 - ADAM_B2 ** ADAM_STEP)
    delta = -ADAM_LR * (m_hat / (jnp.sqrt(v_hat) + ADAM_EPS) + ADAM_WD * w)
    return delta, m, v


def _adamw_big(name, w, m, v, srcs, lands, me):
    nl, rows, cols = w.shape
    tr = next(cand for cand in (256, 128, 64, 32, 16, 8) if rows % cand == 0)

    def body(me_ref, w_ref, m_ref, v_ref, *rest):
        src_refs, land_refs = rest[:nl], rest[nl:2 * nl]
        g_ref, d_ref, mo_ref, vo_ref = rest[2 * nl:]
        for layer in range(nl):
            @pl.when(pl.program_id(0) == layer)
            def _():
                g = src_refs[layer][...].astype(F32)
                for s in range(N_DEV - 1):
                    g = g + land_refs[layer][s].astype(F32)
                delta, mn, vn = _adamw_math(w_ref[...], g, m_ref[...], v_ref[...])
                g_ref[...] = g
                d_ref[...] = delta
                mo_ref[...] = mn
                vo_ref[...] = vn

    blk = pl.BlockSpec((None, tr, cols), lambda l, i, me_ref: (l, i, 0))
    at = lambda layer, l, i: jnp.where(l == layer, i, 0)
    own = [pl.BlockSpec((None, tr, cols), functools.partial(lambda layer, l, i, me_ref: (me_ref[0], at(layer, l, i), 0),
                                                            layer)) for layer in range(nl)]
    peers = [pl.BlockSpec((N_DEV - 1, tr, cols), functools.partial(lambda layer, l, i, me_ref: (0, at(layer, l, i), 0),
                                                                   layer)) for layer in range(nl)]
    return pl.pallas_call(
        body, name=name,
        grid_spec=pltpu.PrefetchScalarGridSpec(
            num_scalar_prefetch=1, grid=(nl, rows // tr),
            in_specs=[blk, blk, blk] + own + peers, out_specs=[blk] * 4),
        out_shape=[_sds((nl, rows, cols), F32)] * 4,
        compiler_params=_cparams(("arbitrary", "arbitrary")),
    )(me, w, m, v, *srcs, *lands)


def _adamw_small(ws, gs, ms, vs):
    n = len(ws)

    def body(*refs):
        w_refs, g_refs, m_refs, v_refs = (refs[i * n:(i + 1) * n] for i in range(4))
        d_out, m_out, v_out = (refs[(4 + i) * n:(5 + i) * n] for i in range(3))
        for i in range(n):
            delta, mn, vn = _adamw_math(w_refs[i][...], g_refs[i][...], m_refs[i][...], v_refs[i][...])
            d_out[i][...] = delta
            m_out[i][...] = mn
            v_out[i][...] = vn

    vm = pl.BlockSpec(memory_space=pltpu.VMEM)
    outs = pl.pallas_call(
        body, name="adamw_small", in_specs=[vm] * (4 * n), out_specs=[vm] * (3 * n),
        out_shape=[_sds(a.shape, F32) for a in ws] * 3,
    )(*ws, *gs, *ms, *vs)
    return outs[:n], outs[n:2 * n], outs[2 * n:]


def _pad_to(a, rows, cols):
    return jnp.pad(a, ((0, rows - a.shape[0]), (0, cols - a.shape[1])))


def _place_own(blocks):
    me = _flat(*_my_place())
    return [lax.dynamic_update_slice(lax.empty((N_DEV,) + b.shape, b.dtype), b[None], (me,) + (0,) * b.ndim)
            for b in blocks]


def _ag_copies(x, y, c, blocks, bufs, send_sems, recv_sems, arriving):
    copies = []
    for a in range(len(blocks)):
        for r in range(1, N_DEV):
            peer = _peer(x, y, c, r)
            k = a * (N_DEV - 1) + r - 1
            copies.append(pltpu.make_async_remote_copy(
                src_ref=blocks[a], dst_ref=bufs[a].at[_flat(*(peer if arriving else (x, y, c)))],
                send_sem=send_sems.at[k], recv_sem=recv_sems.at[k], device_id=peer, device_id_type=MESH))
    return copies


def _ag_start(groups, after):
    flat = [pair for g in groups for pair in g]
    n, ng = len(flat), len(groups)
    hbm = lambda a: pltpu.with_memory_space_constraint(a, pltpu.HBM)

    def body(*refs):
        blocks, bufs = refs[:n], refs[n:2 * n]
        sems = refs[2 * n + len(after):2 * n + len(after) + 2 * ng]
        x, y, c = _my_place()
        at = 0
        for gi, g in enumerate(groups):
            for cp in _ag_copies(x, y, c, blocks[at:at + len(g)], bufs[at:at + len(g)], sems[2 * gi],
                                 sems[2 * gi + 1], arriving=False):
                cp.start()
            at += len(g)
        refs[-1][...] = jnp.zeros_like(refs[-1])

    sem_shapes = [pltpu.SemaphoreType.DMA((len(g) * (N_DEV - 1),)) for g in groups for _ in range(2)]
    outs = pl.pallas_call(
        body, name="gather_start",
        in_specs=[HBM] * (2 * n) + [ANY] * len(after),
        out_specs=[SEMS] * (2 * ng) + [HBM] * (2 * n) + [pl.BlockSpec(memory_space=pltpu.VMEM)],
        out_shape=sem_shapes + [pltpu.HBM(b.shape, b.dtype) for b, _ in flat]
        + [pltpu.HBM(u.shape, u.dtype) for _, u in flat] + [_sds((8, 128), F32)],
        input_output_aliases={i: 2 * ng + i for i in range(2 * n)},
        compiler_params=pltpu.CompilerParams(has_side_effects=SIDE_EFFECT),
    )(*[hbm(b) for b, _ in flat], *[hbm(u) for _, u in flat], *after)
    blocks_thru, bufs_thru = outs[2 * ng:2 * ng + n], outs[2 * ng + n:2 * ng + 2 * n]
    started, at = [], 0
    for gi, g in enumerate(groups):
        started.append((outs[2 * gi], outs[2 * gi + 1], blocks_thru[at:at + len(g)], bufs_thru[at:at + len(g)]))
        at += len(g)
    return started, outs[-1]


def _ag_wait(name, send_sems, recv_sems, blocks, bufs, after):
    n = len(blocks)

    def body(*refs):
        for cp in _ag_copies(*_my_place(), refs[:n], refs[n:2 * n], refs[2 * n], refs[2 * n + 1], arriving=True):
            cp.wait_send()
            cp.wait_recv()

    outs = pl.pallas_call(
        body, name=name,
        in_specs=[HBM] * (2 * n) + [SEMS, SEMS] + [ANY] * len(after),
        out_specs=[HBM] * (2 * n),
        out_shape=[pltpu.HBM(a.shape, a.dtype) for a in list(blocks) + list(bufs)],
        input_output_aliases={i: i for i in range(2 * n)},
        compiler_params=pltpu.CompilerParams(has_side_effects=SIDE_EFFECT),
    )(*blocks, *bufs, send_sems, recv_sems, *after)
    return outs[n:]


def _split_call(name, body, thru, sems_in, new_sems, after):
    n, ns, nn = len(thru), len(sems_in), len(new_sems)
    hbm = lambda a: pltpu.with_memory_space_constraint(a, pltpu.HBM)

    def wrapped(*refs):
        body(refs[:n], refs[n:n + ns], refs[n + ns + len(after):n + ns + len(after) + nn])
        refs[-1][...] = jnp.zeros_like(refs[-1])

    outs = pl.pallas_call(
        wrapped, name=name,
        in_specs=[HBM] * n + [SEMS] * ns + [ANY] * len(after),
        out_specs=[SEMS] * nn + [HBM] * n + [pl.BlockSpec(memory_space=pltpu.VMEM)],
        out_shape=[pltpu.SemaphoreType.DMA((k,)) for k in new_sems] + [pltpu.HBM(a.shape, a.dtype) for a in thru]
        + [_sds((8, 128), F32)],
        input_output_aliases={i: nn + i for i in range(n)},
        compiler_params=pltpu.CompilerParams(has_side_effects=SIDE_EFFECT),
    )(*[hbm(a) for a in thru], *sems_in, *after)
    return list(outs[:nn]), list(outs[nn:nn + n]), outs[-1]


def _two_level_gather(name, blocks, bufs, after=()):
    n = len(blocks)

    def copies(refs, s1, r1, s2, r2):
        x, y, c = _my_place()
        me, sibling = (x, y, c), (x, y, 1 - c)
        chips = [(1 - x, y), (x, 1 - y), (1 - x, 1 - y)]
        blk, buf = refs[:n], refs[n:]
        out = dict(send1=[], recv1_sib=[], recv1_ici=[], send2=[], recv2=[])
        for a in range(n):
            place = lambda dev: buf[a].at[_flat(*dev)]
            for k, to in enumerate([sibling] + [(*chip, c) for chip in chips]):
                mk = lambda dst: pltpu.make_async_remote_copy(
                    src_ref=blk[a], dst_ref=dst, send_sem=s1.at[4 * a + k], recv_sem=r1.at[4 * a + k],
                    device_id=to, device_id_type=MESH)
                out['send1'].append(mk(place(me)))
                out['recv1_sib' if k == 0 else 'recv1_ici'].append(mk(place(to)))
            for j, chip in enumerate(chips):
                mk = lambda dev: pltpu.make_async_remote_copy(
                    src_ref=place(dev), dst_ref=place(dev), send_sem=s2.at[3 * a + j], recv_sem=r2.at[3 * a + j],
                    device_id=sibling, device_id_type=MESH)
                out['send2'].append(mk((*chip, c)))
                out['recv2'].append(mk((*chip, 1 - c)))
        return out

    def start(refs, sems_in, new):
        for cp in copies(refs, new[0], new[1], new[0], new[1])['send1']:
            cp.start()

    def forward(refs, sems_in, new):
        cps = copies(refs, sems_in[0], sems_in[1], new[0], new[1])
        for cp in cps['recv1_ici']:
            cp.wait_recv()
        for cp in cps['send2']:
            cp.start()

    def finish(refs, sems_in, new):
        cps = copies(refs, *sems_in)
        for cp in cps['recv1_sib'] + cps['recv2']:
            cp.wait_recv()
        for cp in cps['send1'] + cps['send2']:
            cp.wait_send()

    sems1, thru, token = _split_call(name + "_start", start, list(blocks) + list(bufs), [], [4 * n, 4 * n], after)

    def complete(after):
        sems2, thru2, token2 = _split_call(name + "_forward", forward, thru, sems1, [3 * n, 3 * n], after)
        _, thru3, _ = _split_call(name + "_wait", finish, thru2, sems1 + sems2, [], ())
        return thru3[n:], token2

    return token, complete


def _prepare_weights(p, x):
    n = N_DEV
    bf = lambda a: a.astype(BF16)
    gn_pack = jnp.concatenate([
        _pad_to(p['ret_gn'][0], RET_HEADS, 128), _pad_to(p['mla_q_a_norm'], 1, 128),
        _pad_to(p['mla_kv_a_norm'], 1, 128), jnp.zeros((2, 128), F32)], axis=0)
    ple = lambda l: [bf(p['ple_gate_w'][l]), bf(p['ple_proj_w'][l])]
    names = ('mlp_w1_0', 'mlp_w2_0', 'ple_0', 'mla_in', 'mla_up', 'mla_out', 'layer_1')
    later = [[bf(p['mlp_w1'][0])], [bf(p['mlp_w2'][0])], ple(0),
             [bf(p['mla_w_in'][0])], [bf(p['mla_w_uq'][0]), bf(p['mla_w_ukv'][0])], [bf(p['mla_w_out'][0])],
             [bf(p['mlp_w1'][1]), bf(p['mlp_w2'][1])] + ple(1)]
    first = [gn_pack, bf(p['ret_w_in'][0])]
    second = [bf(p['ret_w_out'][0])]
    token, complete_first = _two_level_gather("first_gather", first, _place_own(first))
    token2, complete_second = _two_level_gather("second_gather", second, _place_own(second), (token,))
    hn0 = _rms_fwd("mix_norm0", x, p['mix_norm'][0:1], deps=(token2,))
    bufs = _place_own([b for g in later for b in g])
    tables = _rope_tables(x.shape[0], token2[0, 0])
    (pack, wri), token = complete_first((hn0, tables[0], tables[1], *tables[2], *bufs))
    groups, at = [], 0
    for g in later:
        groups.append(list(zip(g, bufs[at:at + len(g)])))
        at += len(g)
    started, token = _ag_start(groups, (token,))

    w = {k: p[k] for k in ('mix_norm', 'mlp_norm', 'ple_norm')}
    w['hn0'] = hn0
    w['tables'] = tables
    w['ret_gn'] = pack[:, :RET_HEADS, :RET_DV // n].transpose(1, 0, 2).reshape(RET_HEADS, RET_DV)
    w['mla_q_a_norm'] = pack[:, RET_HEADS, :MLA_Q_RANK // n].reshape(1, MLA_Q_RANK)
    w['mla_kv_a_norm'] = pack[:, RET_HEADS + 1, :MLA_KV_RANK // n].reshape(1, MLA_KV_RANK)
    w['ret_w_in'] = wri
    w['mla_q_norm'] = _spread_rope(p['mla_q_norm'])
    w['mla_k_norm'] = _spread_rope(p['mla_k_norm'])
    w['deps'] = (token,)

    def fetch(name, after):
        if name == 'ret_out':
            return dict(ret_w_out=complete_second(after)[0][0].reshape(RET_V_W, D_MODEL))
        got = list(_ag_wait("gather_wait_" + name, *started[names.index(name)], after))
        if name == 'mla_in':
            return dict(mla_w_in=_spread_rope(got[0].reshape(D_MODEL, MLA_IN)))
        if name == 'mla_up':
            return dict(mla_w_uq=_spread_rope(got[0]), mla_w_ukv=got[1])
        if name == 'mla_out':
            return dict(mla_w_out=got[0].reshape(D_MODEL, D_MODEL))
        out = {}
        if name in ('mlp_w1_0', 'layer_1'):
            out['mlp_w1'] = got.pop(0)
        if name in ('mlp_w2_0', 'layer_1'):
            out['mlp_w2'] = got.pop(0)
        if name in ('ple_0', 'layer_1'):
            out['ple_gate_w'] = got[0].reshape(D_MODEL, D_MODEL)
            out['ple_proj_w'] = got[1].transpose(1, 0, 2).reshape(PLE_DIM, D_MODEL)
        return out

    return w, fetch


def _small_grads(small, after):
    rows = [(0, small['mix_norm'][0]), (1, small['mix_norm'][1]), (2, small['mlp_norm'][0]),
            (3, small['mlp_norm'][1]), (4, small['ple_norm'][0]), (5, small['ple_norm'][1]),
            (6, small['ret_gn']), (10, small['mla_q_a_norm']), (11, small['mla_kv_a_norm']),
            (12, small['mla_q_norm']), (13, small['mla_k_norm']), (14, small['sq_err'])]
    gs = _all_reduce_small(rows, after)
    me = _flat(*_my_place())
    n = N_DEV
    return dict(
        sq_err=gs[14, 0],
        mix_norm=gs[0:2], mlp_norm=gs[2:4], ple_norm=gs[4:6],
        ret_gn=lax.dynamic_slice(gs, (6, me * (RET_DV // n)), (RET_HEADS, RET_DV // n)),
        mla_q_a_norm=lax.dynamic_slice(gs, (10, me * (MLA_Q_RANK // n)), (1, MLA_Q_RANK // n)),
        mla_kv_a_norm=lax.dynamic_slice(gs, (11, me * (MLA_KV_RANK // n)), (1, MLA_KV_RANK // n)),
        mla_q_norm=_gather_rope(gs[12:13, :MLA_HD_PAD]), mla_k_norm=_gather_rope(gs[13:14, :MLA_HD_PAD]))


def kernel(x, p, mix_norm, ret_w_in, ret_gn, ret_w_out, mla_w_in, mla_q_a_norm, mla_kv_a_norm, mla_w_uq, mla_w_ukv, mla_q_norm, mla_k_norm, mla_w_out, mlp_norm, mlp_w1, mlp_w2, ple_norm, ple_gate_w, ple_proj_w, loss_target, m_mix_norm, m_ret_w_in, m_ret_gn, m_ret_w_out, m_mla_w_in, m_mla_q_a_norm, m_mla_kv_a_norm, m_mla_w_uq, m_mla_w_ukv, m_mla_q_norm, m_mla_k_norm, m_mla_w_out, m_mlp_norm, m_mlp_w1, m_mlp_w2, m_ple_norm, m_ple_gate_w, m_ple_proj_w, v_mix_norm, v_ret_w_in, v_ret_gn, v_ret_w_out, v_mla_w_in, v_mla_q_a_norm, v_mla_kv_a_norm, v_mla_w_uq, v_mla_w_ukv, v_mla_q_norm, v_mla_k_norm, v_mla_w_out, v_mlp_norm, v_mlp_w1, v_mlp_w2, v_ple_norm, v_ple_gate_w, v_ple_proj_w):
    given = dict(locals())
    params = {n: given[n] for n in WEIGHTS}
    w, fetch = _prepare_weights(params, x[0])

    started = []

    def emit(group):
        keys = list(group)
        send, recv, srcs, lands, token = _rs_start(f"rs_start{len(started)}", [group[k] for k in keys])
        started.append((keys, send, recv, srcs, lands))
        return (token,)

    sq_err, grad_x, _, small = _local_step(x[0], p, loss_target[0], w, fetch, emit)
    small['sq_err'] = sq_err[0:1]

    grads, deltas, new_m, new_v = {}, {}, {}, {}
    total = {}

    def small_updates(after):
        sg = _small_grads(small, after)
        total['loss'] = 0.5 / D_MODEL * sg['sq_err']
        two_d = lambda a: a.reshape(-1, a.shape[-1])
        d_s, m_s, v_s = _adamw_small(
            [two_d(params[n]) for n in SMALL], [sg[n] for n in SMALL],
            [two_d(given["m_" + n]) for n in SMALL], [two_d(given["v_" + n]) for n in SMALL])
        for i, n in enumerate(SMALL):
            shape = params[n].shape
            grads[n], deltas[n], new_m[n], new_v[n] = (a.reshape(shape) for a in (sg[n], d_s[i], m_s[i], v_s[i]))
        return (d_s[0],)

    me = _flat(*_my_place()).astype(jnp.int32).reshape(1)
    after = (grad_x,)
    src_of, land_of = {}, {}
    for gi, (keys, send, recv, srcs, lands) in enumerate(started):
        if gi == len(started) - 1:
            after = small_updates(after)
        srcs, lands = _rs_wait(f"rs_wait{gi}", send, recv, srcs, lands, after)
        for k, s, l in zip(keys, srcs, lands):
            src_of[k], land_of[k] = s, l
        done = [n for n in BIG if n not in grads and all((n, l) in src_of for l in range(params[n].shape[0]))]
        for n in done:
            layers = range(params[n].shape[0])
            grads[n], deltas[n], new_m[n], new_v[n] = _adamw_big(
                "adamw_" + n, params[n], given["m_" + n], given["v_" + n],
                [src_of[(n, l)] for l in layers], [land_of[(n, l)] for l in layers], me)
        if done:
            after = tuple(deltas[n] for n in done)

    return (total['loss'], grad_x[None], *[grads[n] for n in WEIGHTS], *[deltas[n] for n in WEIGHTS],
            *[new_m[n] for n in WEIGHTS], *[new_v[n] for n in WEIGHTS])
```

```python
import functools

import jax
import jax.numpy as jnp
from jax import lax
from jax.experimental import pallas as pl
from jax.experimental.pallas import tpu as pltpu

F32 = jnp.float32
BF16 = jnp.bfloat16
MESH = pl.DeviceIdType.MESH
ANY = pl.BlockSpec(memory_space=pl.ANY)

N_DEV = 8
D_MODEL = 1024
CHUNK = 64
RET_BLOCK = 4 * CHUNK
EPS = 1e-6
ROPE_THETA = 10000.0
RET_HEADS = 4
RET_DK = 256
RET_DV = 512
RET_QK_W = RET_HEADS * RET_DK
RET_V_W = RET_HEADS * RET_DV
RET_IN = 2 * RET_QK_W + 2 * RET_V_W
MLA_HEADS = 8
MLA_NOPE = 128
MLA_ROPE = 64
MLA_QKD = MLA_NOPE + MLA_ROPE
MLA_VD = 128
MLA_Q_RANK = 384
MLA_KV_RANK = 256
MLA_IN = MLA_Q_RANK + MLA_KV_RANK + MLA_ROPE
MLA_IN_PAD = 768
MLA_HD_PAD = 256
D_FF = 4096
PLE_DIM = 256
ATT_SCALE = MLA_QKD ** -0.5
LOG2E = 1.4426950408889634
ATT_EXP2 = ATT_SCALE * LOG2E

ADAM_LR = 0.001
ADAM_B1 = 0.9
ADAM_B2 = 0.999
ADAM_EPS = 1e-08
ADAM_WD = 0.01
ADAM_STEP = 10

VMEM_LIMIT = 52 * 1024 * 1024
ROW_TILE = 1024
RET_ROWS = 512
ATT_BLOCK = 256
ATT_QROWS = 1024
ATT_KROWS = 1024
ATT_HEADS = 2
PREP_ROWS = 2048

WEIGHTS = ['mix_norm', 'ret_w_in', 'ret_gn', 'ret_w_out', 'mla_w_in', 'mla_q_a_norm', 'mla_kv_a_norm',
           'mla_w_uq', 'mla_w_ukv', 'mla_q_norm', 'mla_k_norm', 'mla_w_out', 'mlp_norm', 'mlp_w1', 'mlp_w2',
           'ple_norm', 'ple_gate_w', 'ple_proj_w']
BIG = ['ret_w_in', 'ret_w_out', 'mla_w_in', 'mla_w_uq', 'mla_w_ukv', 'mla_w_out', 'mlp_w1', 'mlp_w2',
       'ple_gate_w', 'ple_proj_w']
SMALL = [w for w in WEIGHTS if w not in BIG]


def _cparams(sem=None):
    return pltpu.CompilerParams(dimension_semantics=sem, vmem_limit_bytes=VMEM_LIMIT)


def _dot(a, b, ca, cb):
    return lax.dot_general(a, b, (((ca,), (cb,)), ((), ())), preferred_element_type=F32)


def _bf(v):
    return v if v.dtype == BF16 else v.astype(BF16)


def _sigmoid(z):
    return 1.0 / (1.0 + jnp.exp(-z))


def _mm(name, grid, a, a_spec, b, b_spec, contract, outs, extras=(), epi=None, deps=(), split=None):
    nk = grid[2]
    n_ex, n_out, n_dep = len(extras), len(outs), len(deps)
    acc_shape = tuple(d for d in outs[0][1].block_shape if d is not None)
    if split is not None:
        acc_shape = (acc_shape[1], acc_shape[0] * split)

    def body(*refs):
        a_ref, b_ref = refs[:2]
        ex_refs = refs[2:2 + n_ex]
        out_refs = refs[2 + n_ex + n_dep:2 + n_ex + n_dep + n_out]

        def product():
            return _dot(_bf(a_ref[...]), _bf(b_ref[...]), contract[0], contract[1])

        def finish(acc):
            if split is not None:
                for j in range(acc_shape[1] // split):
                    out_refs[0][j] = acc[:, j * split:(j + 1) * split].astype(out_refs[0].dtype)
                return
            acc = acc[...]
            res = epi(acc, *[r[...] for r in ex_refs]) if epi is not None else (acc,)
            for o, r in zip(out_refs, res):
                o[...] = r.astype(o.dtype)

        if nk == 1:
            finish(product())
        else:
            acc_ref = refs[-1]
            k = pl.program_id(2)

            @pl.when(k == 0)
            def _():
                acc_ref[...] = jnp.zeros_like(acc_ref)

            acc_ref[...] += product()

            @pl.when(k == nk - 1)
            def _():
                finish(acc_ref)

    return pl.pallas_call(
        body, name=name, grid=grid,
        in_specs=[a_spec, b_spec] + [s for _, s in extras] + [ANY] * n_dep,
        out_specs=[s for _, s in outs],
        out_shape=[s for s, _ in outs],
        scratch_shapes=[pltpu.VMEM(acc_shape, F32)] if nk > 1 else [],
        compiler_params=_cparams(("parallel", "parallel", "arbitrary")),
    )(a, b, *[x for x, _ in extras], *deps)


def _mm_rows(name, tm, a, w, mode, outs, extras=(), epi=None, deps=()):
    n_sh, rows, cols = w.shape
    n_ex, n_out, n_dep = len(extras), len(outs), len(deps)
    by_cols = mode in ('nn_cols', 'nt_rows')
    width = cols if mode == 'nn_cols' else rows

    def body(*refs):
        a_ref, w_hbm = refs[:2]
        ex_refs = refs[2:2 + n_ex]
        out_refs = refs[2 + n_ex + n_dep:2 + n_ex + n_dep + n_out]
        w_ref, w_sem = refs[-2:]
        first = pl.program_id(0) == 0

        def shard_copy(s):
            return pltpu.make_async_copy(w_hbm.at[s], w_ref.at[s], w_sem.at[s])

        def compute(arriving):
            if by_cols:
                av = _bf(a_ref[...])
                for s in range(n_sh):
                    cs = slice(s * width, (s + 1) * width)
                    if arriving:
                        shard_copy(s).wait()
                    acc = _dot(av, w_ref[s], 1, 0 if mode == 'nn_cols' else 1)
                    res = epi(acc, *[r[:, cs] for r in ex_refs]) if epi is not None else (acc,)
                    for o, r in zip(out_refs, res):
                        o[:, cs] = r.astype(o.dtype)
            else:
                chunk = rows if mode == 'nn_rows' else cols
                acc = None
                for s in range(n_sh):
                    if arriving:
                        shard_copy(s).wait()
                    part = _dot(_bf(a_ref[:, s * chunk:(s + 1) * chunk]), w_ref[s], 1, 0 if mode == 'nn_rows' else 1)
                    acc = part if acc is None else acc + part
                res = epi(acc, *[r[...] for r in ex_refs]) if epi is not None else (acc,)
                for o, r in zip(out_refs, res):
                    o[...] = r.astype(o.dtype)

        @pl.when(first)
        def _():
            for s in range(n_sh):
                shard_copy(s).start()
            compute(True)

        @pl.when(jnp.logical_not(first))
        def _():
            compute(False)

    t, ka = a.shape
    return pl.pallas_call(
        body, name=name, grid=(t // tm, 1, 1),
        in_specs=[pl.BlockSpec((tm, ka), lambda i, j, k: (i, 0)), ANY] + [s for _, s in extras] + [ANY] * n_dep,
        out_specs=[s for _, s in outs],
        out_shape=[s for s, _ in outs],
        scratch_shapes=[pltpu.VMEM(w.shape, w.dtype), pltpu.SemaphoreType.DMA((n_sh,))],
        compiler_params=_cparams(("arbitrary", "arbitrary", "arbitrary")),
    )(a, w, *[x for x, _ in extras], *deps)


def _sds(shape, dtype):
    return jax.ShapeDtypeStruct(shape, dtype)


def _row_tile(t, cap=ROW_TILE):
    return min(cap, t)


def _rms_fwd(name, x, g, deps=()):
    t, d = x.shape
    tm = _row_tile(t)

    def body(x_ref, g_ref, *rest):
        o_ref = rest[-1]
        xv = x_ref[...]
        r = lax.rsqrt(jnp.mean(xv * xv, axis=-1, keepdims=True) + EPS)
        o_ref[...] = (xv * r * g_ref[...]).astype(o_ref.dtype)

    return pl.pallas_call(
        body, name=name, grid=(t // tm,),
        in_specs=[pl.BlockSpec((tm, d), lambda i: (i, 0)), pl.BlockSpec((1, d), lambda i: (0, 0))] + [ANY] * len(deps),
        out_specs=pl.BlockSpec((tm, d), lambda i: (i, 0)),
        out_shape=_sds((t, d), BF16),
        compiler_params=_cparams(("parallel",)),
    )(x, g, *deps)


def _rms_bwd_rows(dy, xv, g, n):
    r = lax.rsqrt(jnp.sum(xv * xv, axis=-1, keepdims=True) / n + EPS)
    xh = xv * r
    dxh = dy * g
    dx = r * (dxh - xh * (jnp.sum(dxh * xh, axis=-1, keepdims=True) / n))
    return dx, dy * xh


def _ple_gate_bwd(name, dh, gate, e):
    t, d = dh.shape
    tm = _row_tile(t)

    def body(dh_ref, g_ref, e_ref, de_ref, dz_ref):
        dh_v, gt = dh_ref[...], g_ref[...].astype(F32)
        de_ref[...] = (dh_v * gt).astype(BF16)
        dz_ref[...] = (dh_v * e_ref[...].astype(F32) * (gt * (1.0 - gt))).astype(BF16)

    row = pl.BlockSpec((tm, d), lambda i: (i, 0))
    return pl.pallas_call(
        body, name=name, grid=(t // tm,), in_specs=[row, row, row], out_specs=[row, row],
        out_shape=[_sds((t, d), BF16), _sds((t, d), BF16)],
        compiler_params=_cparams(("parallel",)),
    )(dh, gate, e)


def _rope_half(v, cos, sin):
    half = v.shape[-1] // 2
    v1, v2 = v[:, :half], v[:, half:]
    return jnp.concatenate([v1 * cos - v2 * sin, v2 * cos + v1 * sin], axis=-1)


def _ret_consts():
    lg = jnp.log(1.0 - 2.0 ** (-5.0 - jnp.arange(RET_HEADS, dtype=F32)))
    idx = jnp.arange(RET_BLOCK, dtype=F32)
    chunk = jnp.floor(idx / CHUNK)
    dist = idx[:, None] - idx[None, :]
    same = chunk[:, None] == chunk[None, :]
    seen = jnp.where(same, jnp.abs(dist), jnp.where(chunk[None, :] < chunk[:, None], dist, jnp.inf))
    intra = jnp.exp(lg[:, None, None] * seen)
    qdec = jnp.exp(lg[:, None] * (idx + 1.0))
    kdec = jnp.exp(lg[:, None] * (RET_BLOCK - 1.0 - idx))
    cdec = jnp.exp(lg * RET_BLOCK)
    qdec = jnp.broadcast_to(qdec[:, :, None], (RET_HEADS, RET_BLOCK, RET_DK))
    kdec = jnp.broadcast_to(kdec[:, :, None], (RET_HEADS, RET_BLOCK, RET_DK))
    cdec = jnp.broadcast_to(cdec[:, None, None], (RET_HEADS, 1, RET_DV))
    return intra, qdec, kdec, cdec


def _ret_specs(rb, rev_nb=None):
    blk = (lambda i: i) if rev_nb is None else (lambda i: rev_nb - 1 - i)
    full = lambda shape: pl.BlockSpec(shape, lambda i: (0,) * len(shape))
    return dict(
        proj=pl.BlockSpec((rb, RET_IN), lambda i: (blk(i), 0)),
        tab=pl.BlockSpec((rb, RET_DK // 2), lambda i: (blk(i), 0)),
        vw=pl.BlockSpec((rb, RET_V_W), lambda i: (blk(i), 0)),
        st=pl.BlockSpec((rb // RET_BLOCK, RET_HEADS, RET_DK, RET_DV), lambda i: (blk(i), 0, 0, 0)),
        gn=full((RET_HEADS, 1, RET_DV)),
        intra=full((RET_HEADS, RET_BLOCK, RET_BLOCK)),
        dec=full((RET_HEADS, RET_BLOCK, RET_DK)),
        cdec=full((RET_HEADS, 1, RET_DV)),
    )


def _ret_fwd(proj, cos, sin, gn):
    t = proj.shape[0]
    rb = min(RET_ROWS, t)
    cpb = rb // RET_BLOCK
    intra, qdec, kdec, cdec = _ret_consts()
    sp = _ret_specs(rb)

    def body(proj_ref, cos_ref, sin_ref, gn_ref, intra_ref, qd_ref, kd_ref, cd_ref,
             gated_ref, outp_ref, st_ref, s_ref):
        @pl.when(pl.program_id(0) == 0)
        def _():
            s_ref[...] = jnp.zeros_like(s_ref)

        def chunk(c, carry):
            rows = pl.ds(pl.multiple_of(c * RET_BLOCK, RET_BLOCK), RET_BLOCK)
            cs, sn = cos_ref[rows, :], sin_ref[rows, :]
            for h in range(RET_HEADS):
                q = proj_ref[rows, h * RET_DK:(h + 1) * RET_DK].astype(F32)
                k = proj_ref[rows, RET_QK_W + h * RET_DK:RET_QK_W + (h + 1) * RET_DK].astype(F32)
                v = proj_ref[rows, 2 * RET_QK_W + h * RET_DV:2 * RET_QK_W + (h + 1) * RET_DV]
                g = proj_ref[rows, 2 * RET_QK_W + RET_V_W + h * RET_DV:
                             2 * RET_QK_W + RET_V_W + (h + 1) * RET_DV].astype(F32)
                qr = _rope_half(q, cs, sn)
                kr = _rope_half(k, cs, sn) * (RET_DK ** -0.5)
                qb, kb, vb = qr.astype(BF16), kr.astype(BF16), v
                sc = _dot(qb, kb, 1, 1) * intra_ref[h]
                inner = _dot(sc.astype(BF16), vb, 1, 0)
                s_old = s_ref[h]
                sb = s_old.astype(BF16)
                st_ref[c, h] = sb
                cross = _dot((qr * qd_ref[h]).astype(BF16), sb, 1, 0)
                out = inner + cross
                s_ref[h] = s_old * cd_ref[h] + _dot((kr * kd_ref[h]).astype(BF16), vb, 0, 0)
                r = lax.rsqrt(jnp.mean(out * out, axis=-1, keepdims=True) + EPS)
                y = out * r * gn_ref[h]
                cols = slice(h * RET_DV, (h + 1) * RET_DV)
                gated_ref[rows, cols] = (g * _sigmoid(g) * y).astype(BF16)
                outp_ref[rows, cols] = out.astype(BF16)
            return carry

        lax.fori_loop(0, cpb, chunk, 0)

    return pl.pallas_call(
        body, name="ret_fwd", grid=(t // rb,),
        in_specs=[sp['proj'], sp['tab'], sp['tab'], sp['gn'], sp['intra'], sp['dec'], sp['dec'], sp['cdec']],
        out_specs=[sp['vw'], sp['vw'], sp['st']],
        out_shape=[_sds((t, RET_V_W), BF16), _sds((t, RET_V_W), BF16),
                   _sds((t // RET_BLOCK, RET_HEADS, RET_DK, RET_DV), BF16)],
        scratch_shapes=[pltpu.VMEM((RET_HEADS, RET_DK, RET_DV), F32)],
        compiler_params=_cparams(("arbitrary",)),
    )(proj, cos, sin, gn.reshape(RET_HEADS, 1, RET_DV), intra, qdec, kdec, cdec)


def _ret_gate_bwd_epi(dgt, out, g, gn):
    g = g.astype(F32)
    out = out.astype(F32)
    r = lax.rsqrt(jnp.mean(out * out, axis=-1, keepdims=True) + EPS)
    xh = out * r
    sg = _sigmoid(g)
    dgate = dgt * (xh * gn) * (sg * (1.0 + g * (1.0 - sg)))
    dy = dgt * (g * sg)
    dxh = dy * gn
    dout = r * (dxh - xh * jnp.mean(dxh * xh, axis=-1, keepdims=True))
    return dout, dgate, jnp.sum(dy * xh, axis=0, keepdims=True)


def _ret_bwd(proj, cos, sin, states, dout, dgate, deps=()):
    t = proj.shape[0]
    rb = min(RET_ROWS, t)
    cpb = rb // RET_BLOCK
    nb = t // rb
    intra, qdec, kdec, cdec = _ret_consts()
    sp = _ret_specs(rb, rev_nb=nb)

    def body(proj_ref, cos_ref, sin_ref, intra_ref, qd_ref, kd_ref, cd_ref, st_ref, dout_ref, dgate_ref, *rest):
        dproj_ref, ds_ref = rest[len(deps):]

        @pl.when(pl.program_id(0) == 0)
        def _():
            ds_ref[...] = jnp.zeros_like(ds_ref)

        def chunk(cc, carry):
            c = cpb - 1 - cc
            rows = pl.ds(pl.multiple_of(c * RET_BLOCK, RET_BLOCK), RET_BLOCK)
            cs, sn = cos_ref[rows, :], sin_ref[rows, :]
            for h in range(RET_HEADS):
                q = proj_ref[rows, h * RET_DK:(h + 1) * RET_DK].astype(F32)
                k = proj_ref[rows, RET_QK_W + h * RET_DK:RET_QK_W + (h + 1) * RET_DK].astype(F32)
                v = proj_ref[rows, 2 * RET_QK_W + h * RET_DV:2 * RET_QK_W + (h + 1) * RET_DV]
                cols = slice(h * RET_DV, (h + 1) * RET_DV)
                qr = _rope_half(q, cs, sn)
                kr = _rope_half(k, cs, sn) * (RET_DK ** -0.5)
                qb, kb, vb = qr.astype(BF16), kr.astype(BF16), v
                qdb = (qr * qd_ref[h]).astype(BF16)
                kdb = (kr * kd_ref[h]).astype(BF16)
                doutb = dout_ref[rows, cols]
                itr = intra_ref[h]
                pb = (_dot(qb, kb, 1, 1) * itr).astype(BF16)
                dv = _dot(pb, doutb, 0, 0)
                dsc = (_dot(doutb, vb, 1, 1) * itr).astype(BF16)
                dq = _dot(dsc, kb, 1, 0)
                dk = _dot(dsc, qb, 0, 0)
                dq = dq + _dot(doutb, st_ref[c, h], 1, 1) * qd_ref[h]
                ds_new = ds_ref[h]
                dsb = ds_new.astype(BF16)
                dk = dk + _dot(vb, dsb, 1, 1) * kd_ref[h]
                dv = dv + _dot(kdb, dsb, 1, 0)
                ds_ref[h] = ds_new * cd_ref[h] + _dot(qdb, doutb, 0, 0)
                dproj_ref[rows, h * RET_DK:(h + 1) * RET_DK] = _rope_half(dq, cs, -sn).astype(BF16)
                dproj_ref[rows, RET_QK_W + h * RET_DK:RET_QK_W + (h + 1) * RET_DK] = (
                    _rope_half(dk * (RET_DK ** -0.5), cs, -sn).astype(BF16))
                dproj_ref[rows, 2 * RET_QK_W + h * RET_DV:2 * RET_QK_W + (h + 1) * RET_DV] = dv.astype(BF16)
                dproj_ref[rows, 2 * RET_QK_W + RET_V_W + h * RET_DV:
                          2 * RET_QK_W + RET_V_W + (h + 1) * RET_DV] = dgate_ref[rows, cols]
            return carry

        lax.fori_loop(0, cpb, chunk, 0)

    return pl.pallas_call(
        body, name="ret_bwd", grid=(nb,),
        in_specs=[sp['proj'], sp['tab'], sp['tab'], sp['intra'], sp['dec'], sp['dec'], sp['cdec'],
                  sp['st'], sp['vw'], sp['vw']] + [ANY] * len(deps),
        out_specs=sp['proj'],
        out_shape=_sds((t, RET_IN), BF16),
        scratch_shapes=[pltpu.VMEM((RET_HEADS, RET_DK, RET_DV), F32)],
        compiler_params=_cparams(("arbitrary",)),
    )(proj, cos, sin, intra, qdec, kdec, cdec, states, dout, dgate, *deps)


def _spread_rope(a):
    return jnp.pad(a, [(0, 0)] * (a.ndim - 1) + [(0, MLA_ROPE)])


def _gather_rope(a):
    return a[..., :a.shape[-1] - MLA_ROPE]


def _rope_tables(t, zero):
    pos = jnp.arange(t, dtype=F32)[:, None] + zero
    inv = 1.0 / (ROPE_THETA ** (jnp.arange(0, RET_DK, 2, dtype=F32) / RET_DK))
    ang = pos * inv[None, :]
    return jnp.cos(ang), jnp.sin(ang), _mla_tables(t, pos)


def _mla_tables(t, pos):
    half = MLA_ROPE // 2
    inv = 1.0 / (ROPE_THETA ** (jnp.arange(0, MLA_ROPE, 2, dtype=F32) / MLA_ROPE))
    ang = pos * inv[None, :]
    cos, sin = jnp.cos(ang), jnp.sin(ang)
    z = jnp.zeros((t, half), F32)
    c = jnp.concatenate([cos, cos, z, z], axis=1)
    s1 = jnp.concatenate([-sin, z, z, z], axis=1)
    s2 = jnp.concatenate([z, sin, z, z], axis=1)
    return c, s1, s2


def _rope_tile(r, c, s1, s2):
    return r * c + pltpu.roll(r, 96, 1) * s1 + pltpu.roll(r, 32, 1) * s2


def _mla_mid(proj2, qa, kva):
    t = proj2.shape[0]
    tm = _row_tile(t)

    def body(p_ref, qa_ref, kva_ref, cq_ref, ckv_ref):
        cq = p_ref[:, :MLA_Q_RANK]
        ckv = p_ref[:, MLA_Q_RANK:MLA_Q_RANK + MLA_KV_RANK]
        rq = lax.rsqrt(jnp.mean(cq * cq, axis=-1, keepdims=True) + EPS)
        rkv = lax.rsqrt(jnp.mean(ckv * ckv, axis=-1, keepdims=True) + EPS)
        cq_ref[...] = (cq * rq * qa_ref[...]).astype(BF16)
        ckv_ref[...] = (ckv * rkv * kva_ref[...]).astype(BF16)

    return pl.pallas_call(
        body, name="mla_mid", grid=(t // tm,),
        in_specs=[pl.BlockSpec((tm, MLA_IN_PAD), lambda i: (i, 0)),
                  pl.BlockSpec((1, MLA_Q_RANK), lambda i: (0, 0)),
                  pl.BlockSpec((1, MLA_KV_RANK), lambda i: (0, 0))],
        out_specs=[pl.BlockSpec((tm, MLA_Q_RANK), lambda i: (i, 0)),
                   pl.BlockSpec((tm, MLA_KV_RANK), lambda i: (i, 0))],
        out_shape=[_sds((t, MLA_Q_RANK), BF16), _sds((t, MLA_KV_RANK), BF16)],
        compiler_params=_cparams(("parallel",)),
    )(proj2, qa, kva)


def _mla_mid_bwd(proj2, qa, kva, dcq, dckv, dkr):
    t = proj2.shape[0]
    tm = _row_tile(t)

    def body(p_ref, qa_ref, kva_ref, dcq_ref, dckv_ref, dkr_ref, dp_ref, dqa_ref, dkva_ref):
        @pl.when(pl.program_id(0) == 0)
        def _():
            dqa_ref[...] = jnp.zeros_like(dqa_ref)
            dkva_ref[...] = jnp.zeros_like(dkva_ref)

        dxq, dgq = _rms_bwd_rows(dcq_ref[...], p_ref[:, :MLA_Q_RANK], qa_ref[...], MLA_Q_RANK)
        dxk, dgk = _rms_bwd_rows(dckv_ref[...], p_ref[:, MLA_Q_RANK:MLA_Q_RANK + MLA_KV_RANK], kva_ref[...],
                                 MLA_KV_RANK)
        dp_ref[:, :MLA_Q_RANK] = dxq.astype(BF16)
        dp_ref[:, MLA_Q_RANK:MLA_Q_RANK + MLA_KV_RANK] = dxk.astype(BF16)
        dp_ref[:, MLA_Q_RANK + MLA_KV_RANK:] = dkr_ref[...].astype(BF16)
        dqa_ref[...] += jnp.sum(dgq, axis=0, keepdims=True)
        dkva_ref[...] += jnp.sum(dgk, axis=0, keepdims=True)

    return pl.pallas_call(
        body, name="mla_mid_bwd", grid=(t // tm,),
        in_specs=[pl.BlockSpec((tm, MLA_IN_PAD), lambda i: (i, 0)),
                  pl.BlockSpec((1, MLA_Q_RANK), lambda i: (0, 0)),
                  pl.BlockSpec((1, MLA_KV_RANK), lambda i: (0, 0)),
                  pl.BlockSpec((tm, MLA_Q_RANK), lambda i: (i, 0)),
                  pl.BlockSpec((tm, MLA_KV_RANK), lambda i: (i, 0)),
                  pl.BlockSpec((tm, 128), lambda i: (i, 0))],
        out_specs=[pl.BlockSpec((tm, MLA_IN_PAD), lambda i: (i, 0)),
                   pl.BlockSpec((1, MLA_Q_RANK), lambda i: (0, 0)),
                   pl.BlockSpec((1, MLA_KV_RANK), lambda i: (0, 0))],
        out_shape=[_sds((t, MLA_IN_PAD), BF16), _sds((1, MLA_Q_RANK), F32), _sds((1, MLA_KV_RANK), F32)],
        compiler_params=_cparams(("arbitrary",)),
    )(proj2, qa, kva, dcq, dckv, dkr)


def _mla_prep_specs(t, tm):
    head = lambda w: pl.BlockSpec((None, tm, w), lambda i, h: (h, i, 0))
    return dict(
        head256=head(MLA_HD_PAD), head128=head(MLA_VD),
        cols256=pl.BlockSpec((tm, MLA_HD_PAD), lambda i, h: (i, h)),
        cq=pl.BlockSpec((tm, MLA_Q_RANK), lambda i, h: (i, 0)),
        ckv=pl.BlockSpec((tm, MLA_KV_RANK), lambda i, h: (i, 0)),
        wuq=pl.BlockSpec((None, MLA_Q_RANK, MLA_HD_PAD), lambda i, h: (h, 0, 0)),
        wukv=pl.BlockSpec((None, MLA_KV_RANK, MLA_HD_PAD), lambda i, h: (h, 0, 0)),
        kr=pl.BlockSpec((tm, 128), lambda i, h: (i, (MLA_Q_RANK + MLA_KV_RANK) // 128)),
        gain=pl.BlockSpec((1, MLA_HD_PAD), lambda i, h: (0, 0)),
        tab=pl.BlockSpec((tm, 128), lambda i, h: (i, 0)),
    )


def _mla_prep(cq, ckv, wuq, wukv, proj2, gq, gk, tabs):
    t = cq.shape[0]
    tm = _row_tile(t, PREP_ROWS)
    sp = _mla_prep_specs(t, tm)

    def body(cq_ref, ckv_ref, wuq_ref, wukv_ref, kr_ref, gq_ref, gk_ref, c_ref, s1_ref, s2_ref,
             qh_ref, kh_ref, vh_ref):
        c, s1, s2 = c_ref[...], s1_ref[...], s2_ref[...]

        def norm_rope(xv, gain):
            r = lax.rsqrt(jnp.sum(xv * xv, axis=-1, keepdims=True) / MLA_QKD + EPS)
            y = xv * r * gain
            return jnp.concatenate([y[:, :MLA_NOPE], _rope_tile(y[:, MLA_NOPE:], c, s1, s2)], axis=-1)

        kvv = _dot(ckv_ref[...], wukv_ref[...], 1, 0)
        qh_ref[...] = norm_rope(_dot(cq_ref[...], wuq_ref[...], 1, 0), gq_ref[...]).astype(BF16)
        kf = jnp.concatenate([kvv[:, :MLA_NOPE], kr_ref[...]], axis=-1)
        kh_ref[...] = norm_rope(kf, gk_ref[...]).astype(BF16)
        vh_ref[...] = jnp.concatenate([kvv[:, MLA_NOPE:], jnp.ones((tm, MLA_VD), F32)], axis=-1).astype(BF16)

    return pl.pallas_call(
        body, name="mla_prep", grid=(t // tm, MLA_HEADS),
        in_specs=[sp['cq'], sp['ckv'], sp['wuq'], sp['wukv'], sp['kr'], sp['gain'], sp['gain'],
                  sp['tab'], sp['tab'], sp['tab']],
        out_specs=[sp['head256'], sp['head256'], sp['head256']],
        out_shape=[_sds((MLA_HEADS, t, MLA_HD_PAD), BF16), _sds((MLA_HEADS, t, MLA_HD_PAD), BF16),
                   _sds((MLA_HEADS, t, 2 * MLA_VD), BF16)],
        compiler_params=_cparams(("parallel", "arbitrary")),
    )(cq, ckv, wuq, wukv, proj2, gq, gk, *tabs)


def _mla_prep_bwd(cq, ckv, wuq, wukv, proj2, gq, gk, tabs, dqt, dkh, dvh):
    t = cq.shape[0]
    tm = _row_tile(t, PREP_ROWS)
    ab = dqt.shape[-1]
    sp = _mla_prep_specs(t, tm)

    def body(cq_ref, ckv_ref, wuq_ref, wukv_ref, kr_ref, gq_ref, gk_ref, c_ref, s1_ref, s2_ref,
             dqt_ref, dkh_ref, dvh_ref, dq_ref, dkv_ref, dkr_ref, dgq_ref, dgk_ref):
        dqh = jnp.concatenate([dqt_ref[b].T for b in range(tm // ab)], axis=0)
        i, h = pl.program_id(0), pl.program_id(1)

        @pl.when((i == 0) & (h == 0))
        def _():
            dgq_ref[...] = jnp.zeros_like(dgq_ref)
            dgk_ref[...] = jnp.zeros_like(dgk_ref)

        @pl.when(h == 0)
        def _():
            dkr_ref[...] = jnp.zeros_like(dkr_ref)

        c, s1, s2 = c_ref[...], s1_ref[...], s2_ref[...]

        def back(xv, gain, dout):
            dy = jnp.concatenate([dout[:, :MLA_NOPE], _rope_tile(dout[:, MLA_NOPE:], c, -s1, -s2)], axis=-1)
            return _rms_bwd_rows(dy, xv, gain, MLA_QKD)

        kvv = _dot(ckv_ref[...], wukv_ref[...], 1, 0)
        dxq, dgq = back(_dot(cq_ref[...], wuq_ref[...], 1, 0), gq_ref[...], dqh)
        kf = jnp.concatenate([kvv[:, :MLA_NOPE], kr_ref[...]], axis=-1)
        dxk, dgk = back(kf, gk_ref[...], dkh_ref[...])
        dq_ref[...] = dxq.astype(BF16)
        dkv_ref[...] = jnp.concatenate([dxk[:, :MLA_NOPE], dvh_ref[...]], axis=-1).astype(BF16)
        dkr_ref[...] += dxk[:, MLA_NOPE:]
        dgq_ref[...] += jnp.sum(dgq, axis=0, keepdims=True)
        dgk_ref[...] += jnp.sum(dgk, axis=0, keepdims=True)

    return pl.pallas_call(
        body, name="mla_prep_bwd", grid=(t // tm, MLA_HEADS),
        in_specs=[sp['cq'], sp['ckv'], sp['wuq'], sp['wukv'], sp['kr'], sp['gain'], sp['gain'],
                  sp['tab'], sp['tab'], sp['tab'],
                  pl.BlockSpec((None, tm // ab, MLA_HD_PAD, ab), lambda i, h: (h, i, 0, 0)),
                  sp['head256'], sp['head128']],
        out_specs=[sp['cols256'], sp['cols256'], sp['tab'], sp['gain'], sp['gain']],
        out_shape=[_sds((t, MLA_HEADS * MLA_HD_PAD), BF16), _sds((t, MLA_HEADS * MLA_HD_PAD), BF16),
                   _sds((t, 128), F32), _sds((1, MLA_HD_PAD), F32), _sds((1, MLA_HD_PAD), F32)],
        compiler_params=_cparams(("arbitrary", "arbitrary")),
    )(cq, ckv, wuq, wukv, proj2, gq, gk, *tabs, dqt, dkh, dvh)


def _chunk_visible(rows, cols, row_off, col_off):
    rq = lax.shift_right_logical(lax.broadcasted_iota(jnp.int32, (rows, cols), 0) + row_off, 6)
    ck = lax.shift_right_logical(lax.broadcasted_iota(jnp.int32, (rows, cols), 1) + col_off, 6)
    return ck <= rq


def _rows_to_lanes(col):
    return col.T[:8, :]


def _attn_fwd(qh, kh, vh):
    t = qh.shape[1]
    ab = min(ATT_BLOCK, t)
    tq = min(ATT_QROWS, t)
    r = tq // ab
    hg = ATT_HEADS

    def body(q_ref, k_ref, v_ref, o_ref, lse_ref, acc_ref):
        n_un = pl.program_id(1) * r
        acc_ref[...] = jnp.zeros_like(acc_ref)

        def step(b, ms, diag):
            rows = pl.ds(pl.multiple_of(b * ab, ab), ab)
            out = []
            for hh in range(hg):
                m = ms[hh]
                s = _dot(q_ref[hh], k_ref[hh, rows, :], 1, 1)
                if diag is not None:
                    s = jnp.where(_chunk_visible(tq, ab, 0, diag * ab), s, -1e30)
                m_new = jnp.maximum(m, jnp.max(s, axis=-1, keepdims=True))
                p = jnp.exp2((s - m_new) * ATT_EXP2).astype(BF16)
                acc_ref[hh] = jnp.exp2((m - m_new) * ATT_EXP2) * acc_ref[hh] + _dot(p, v_ref[hh, rows, :], 1, 0)
                out.append(m_new)
            return tuple(out)

        ms = tuple(jnp.full((tq, 1), -1e30, F32) for _ in range(hg))
        ms = lax.fori_loop(0, n_un, lambda b, st: step(b, st, None), ms)
        for d in range(r):
            ms = step(n_un + d, ms, d)
        for hh in range(hg):
            l = acc_ref[hh, :, MLA_VD:]
            o_ref[:, hh * MLA_VD:(hh + 1) * MLA_VD] = (acc_ref[hh, :, :MLA_VD] / l).astype(BF16)
            lse_t = _rows_to_lanes(ms[hh] * ATT_EXP2 + jnp.log(l) * LOG2E)
            for d in range(r):
                lse_ref[hh, d] = lse_t[:, d * ab:(d + 1) * ab]

    return pl.pallas_call(
        body, name="mla_attn", grid=(MLA_HEADS // hg, t // tq),
        in_specs=[pl.BlockSpec((hg, tq, MLA_HD_PAD), lambda g, i: (g, i, 0)),
                  pl.BlockSpec((hg, t, MLA_HD_PAD), lambda g, i: (g, 0, 0)),
                  pl.BlockSpec((hg, t, 2 * MLA_VD), lambda g, i: (g, 0, 0))],
        out_specs=[pl.BlockSpec((tq, hg * MLA_VD), lambda g, i: (i, g)),
                   pl.BlockSpec((hg, r, 8, ab), lambda g, i: (g, i, 0, 0))],
        out_shape=[_sds((t, MLA_HEADS * MLA_VD), BF16), _sds((MLA_HEADS, t // ab, 8, ab), F32)],
        scratch_shapes=[pltpu.VMEM((hg, tq, 2 * MLA_VD), F32)],
        compiler_params=_cparams(("parallel", "arbitrary")),
    )(qh, kh, vh)


def _attn_bwd(qh, kh, vh, dob, o, lse_t):
    t = qh.shape[1]
    ab = min(ATT_BLOCK, t)
    kb = min(ATT_KROWS, t)
    r = kb // ab
    nq = t // ab
    hg = ATT_HEADS

    def body(q_ref, k_ref, v_ref, do_ref, o_ref, lse_ref, dqt_ref, dk_ref, dv_ref, dl_ref):
        j = pl.program_id(1)

        @pl.when(j == 0)
        def _():
            dqt_ref[...] = jnp.zeros_like(dqt_ref)
            ones = jnp.ones((8, MLA_VD), F32)

            def delta(b, carry):
                rows = pl.ds(pl.multiple_of(b * ab, ab), ab)
                for hh in range(hg):
                    cols = slice(hh * MLA_VD, (hh + 1) * MLA_VD)
                    prod = do_ref[rows, cols].astype(F32) * o_ref[rows, cols].astype(F32)
                    dl_ref[hh, b] = lax.dot_general(ones, prod, (((1,), (1,)), ((), ())),
                                                    precision=lax.Precision.HIGHEST, preferred_element_type=F32)
                return carry

            lax.fori_loop(0, nq, delta, 0)

        ks = [k_ref[hh] for hh in range(hg)]
        vs = [v_ref[hh, :, :MLA_VD] for hh in range(hg)]
        kts = [k.T for k in ks]

        dk_ref[...] = jnp.zeros_like(dk_ref)
        dv_ref[...] = jnp.zeros_like(dv_ref)

        def step(b, carry, diag):
            rows = pl.ds(pl.multiple_of(b * ab, ab), ab)
            hi = kb if diag is None else (diag + 1) * ab
            for hh in range(hg):
                q = q_ref[hh, rows, :]
                do = do_ref[rows, hh * MLA_VD:(hh + 1) * MLA_VD]
                s_t = _dot(ks[hh][:hi], q, 1, 1)
                if diag is not None:
                    key_chunk = lax.shift_right_logical(lax.broadcasted_iota(jnp.int32, (hi, ab), 0), 6)
                    query_chunk = lax.shift_right_logical(
                        lax.broadcasted_iota(jnp.int32, (hi, ab), 1) + diag * ab, 6)
                    s_t = jnp.where(key_chunk <= query_chunk, s_t, -1e30)
                p_t = jnp.exp2(s_t * ATT_EXP2 - lse_ref[hh, b][0:1, :])
                dp_t = _dot(vs[hh][:hi], do, 1, 1)
                ds_t = (p_t * (dp_t - dl_ref[hh, b][0:1, :]) * ATT_SCALE).astype(BF16)
                dqt_ref[hh, b] += _dot(kts[hh][:, :hi], ds_t, 1, 0)
                dk_ref[hh, :hi] += _dot(ds_t, q, 1, 0)
                dv_ref[hh, :hi] += _dot(p_t.astype(BF16), do, 1, 0)
            return carry

        for d in range(r):
            step(j * r + d, 0, d)
        lax.fori_loop((j + 1) * r, nq, lambda b, c: step(b, c, None), 0)

    whole = lambda w: pl.BlockSpec((hg, t, w), lambda g, j: (g, 0, 0))
    blk = lambda w: pl.BlockSpec((hg, kb, w), lambda g, j: (g, j, 0))
    stat = pl.BlockSpec((hg, nq, 8, ab), lambda g, j: (g, 0, 0, 0))
    cols = pl.BlockSpec((t, hg * MLA_VD), lambda g, j: (0, g))
    return pl.pallas_call(
        body, name="mla_attn_bwd", grid=(MLA_HEADS // hg, t // kb),
        in_specs=[whole(MLA_HD_PAD), blk(MLA_HD_PAD), blk(2 * MLA_VD),
                  cols, cols, stat],
        out_specs=[pl.BlockSpec((hg, nq, MLA_HD_PAD, ab), lambda g, j: (g, 0, 0, 0)), blk(MLA_HD_PAD), blk(MLA_VD)],
        out_shape=[_sds((MLA_HEADS, nq, MLA_HD_PAD, ab), F32), _sds((MLA_HEADS, t, MLA_HD_PAD), F32),
                   _sds((MLA_HEADS, t, MLA_VD), F32)],
        scratch_shapes=[pltpu.VMEM((hg, nq, 8, ab), F32)],
        compiler_params=_cparams(("parallel", "arbitrary")),
    )(qh, kh, vh, dob, o, lse_t)


VEC = pl.BlockSpec((1, D_MODEL), lambda i, j, k: (0, 0))


def _rows(tm, width):
    return pl.BlockSpec((tm, width), lambda i, j, k: (i, 0))


def _residual_epi(next_gain):
    if next_gain is None:
        return [], lambda acc, hv: (acc + hv,)

    def epi(acc, hv, g):
        h_new = acc + hv
        r = lax.rsqrt(jnp.mean(h_new * h_new, axis=-1, keepdims=True) + EPS)
        return h_new, h_new * r * g

    return [(next_gain, VEC)], epi


def _residual_outs(t, row, next_gain):
    outs = [(_sds((t, D_MODEL), F32), row)]
    return outs + ([(_sds((t, D_MODEL), BF16), row)] if next_gain is not None else [])


def _mlp_fwd(l, h, hn, w1g, fetch_w2, next_gain):
    t = h.shape[0]
    tm = _row_tile(t, 512)

    def relu2(acc):
        r = jnp.maximum(acc, 0.0)
        return (r * r,)

    (u,) = _mm_rows(f"mlp_up{l}", tm, hn, w1g, 'nn_cols', [(_sds((t, D_FF), BF16), _rows(tm, D_FF))], epi=relu2)
    w2g = fetch_w2((u,))
    row = _rows(tm, D_MODEL)
    more, epi = _residual_epi(next_gain)
    h2, hn_next = _mm_rows(f"mlp_down{l}", tm, u, w2g, 'nn_rows', _residual_outs(t, row, next_gain),
                           extras=[(h, row)] + more, epi=epi)
    return h2, hn_next, (h, hn, u, w1g, w2g)


def _norm_bwd_outs(t, tm):
    return [(_sds((t, D_MODEL), F32), pl.BlockSpec((tm, D_MODEL), lambda i, j, k: (i, 0))),
            (_sds((t // tm, 1, D_MODEL), F32), pl.BlockSpec((None, 1, D_MODEL), lambda i, j, k: (i, 0, 0)))]


def _norm_bwd_epi(acc, xv, res, g):
    dx, dgr = _rms_bwd_rows(acc, xv, g, D_MODEL)
    return res + dx, jnp.sum(dgr, axis=0, keepdims=True)


def _mlp_bwd(l, dh, saved, norm_g, emit_w2=None, emit_w1=None):
    h, hn, u, w1g, w2g = saved
    t = h.shape[0]
    tm = _row_tile(t, 512)
    nsh, _, wsh = w1g.shape
    wide = _rows(tm, D_FF)
    (da,) = _mm_rows(f"mlp_du{l}", tm, dh, w2g, 'nt_rows', [(_sds((t, D_FF), BF16), wide)], extras=[(u, wide)],
                     epi=lambda acc, uv: (2.0 * jnp.sqrt(uv.astype(F32)) * acc,))
    tw = _row_tile(t, 512)
    (dw2,) = _mm(f"mlp_dw2{l}", (1, 1, t // tw),
                 u, pl.BlockSpec((tw, D_FF), lambda i, j, k: (k, 0)),
                 dh, pl.BlockSpec((tw, D_MODEL), lambda i, j, k: (k, 0)), (0, 0),
                 [(_sds((D_FF, D_MODEL), BF16), pl.BlockSpec((D_FF, D_MODEL), lambda i, j, k: (0, 0)))])
    dw2 = dw2.reshape(nsh, wsh, D_MODEL)
    (dw1,) = _mm(f"mlp_dw1{l}", (1, 1, t // tw),
                 hn, pl.BlockSpec((tw, D_MODEL), lambda i, j, k: (k, 0)),
                 da, pl.BlockSpec((tw, D_FF), lambda i, j, k: (k, 0)), (0, 0),
                 [(_sds((nsh, D_MODEL, wsh), BF16), pl.BlockSpec((nsh, D_MODEL, wsh), lambda i, j, k: (0, 0, 0)))],
                 split=wsh, deps=emit_w2(dw2) if emit_w2 else ())
    row = _rows(tm, D_MODEL)
    dh_in, dg = _mm_rows(f"mlp_dhn{l}", tm, da, w1g, 'nt_cols', _norm_bwd_outs(t, tm),
                         extras=[(h, row), (dh, row), (norm_g, VEC)], epi=_norm_bwd_epi,
                         deps=emit_w1(dw1) if emit_w1 else ())
    return dh_in, jnp.sum(dg, axis=0), dw1, dw2


def _ple_fwd(l, h, hn, p, wg, wp, next_gain, target=None):
    t = h.shape[0]
    tm = _row_tile(t, 512)
    row = pl.BlockSpec((tm, D_MODEL), lambda i, j, k: (i, 0))
    full = lambda r: pl.BlockSpec((r, D_MODEL), lambda i, j, k: (0, 0))
    f32_row, bf_row = (_sds((t, D_MODEL), F32), row), (_sds((t, D_MODEL), BF16), row)
    common = [(h, row), (p, pl.BlockSpec((None, None, tm, PLE_DIM), lambda i, j, k: (l, 0, i, 0))),
              (wp, full(PLE_DIM))]
    if target is not None:
        def loss_epi(acc, hv, pv, wpv, tv):
            gt = _sigmoid(acc)
            ev = _dot(_bf(pv), wpv, 1, 0)
            err = hv + gt * ev - tv
            sq = jnp.sum(jnp.sum(err * err, axis=-1, keepdims=True), axis=0, keepdims=True)
            return err / D_MODEL, gt, ev, jnp.broadcast_to(sq, (8, 128))

        dy, gate, e, sq = _mm(f"ple_gate{l}", (t // tm, 1, 1), hn, row, wg, full(D_MODEL), (1, 0),
                              [f32_row, bf_row, bf_row, (_sds((t // tm, 8, 128), F32),
                                                         pl.BlockSpec((None, 8, 128), lambda i, j, k: (i, 0, 0)))],
                              extras=common + [(target, row)], epi=loss_epi)
        return dy, jnp.sum(sq, axis=0), (h, hn, gate, e)

    def gate_epi(acc, hv, pv, wpv, *gain):
        gt = _sigmoid(acc)
        ev = _dot(_bf(pv), wpv, 1, 0)
        h_new = hv + gt * ev
        if not gain:
            return h_new, gt, ev
        r = lax.rsqrt(jnp.mean(h_new * h_new, axis=-1, keepdims=True) + EPS)
        return h_new, gt, ev, h_new * r * gain[0]

    res = _mm(f"ple_gate{l}", (t // tm, 1, 1), hn, row, wg, full(D_MODEL), (1, 0),
              [f32_row, bf_row, bf_row] + ([bf_row] if next_gain is not None else []),
              extras=common + ([(next_gain, VEC)] if next_gain is not None else []), epi=gate_epi)
    h_out, gate, e = res[0], res[1], res[2]
    return h_out, (res[3] if next_gain is not None else None), (h, hn, gate, e)


def _ple_bwd(l, dh, saved, p, norm_g, wg, deps=(), emit=None):
    h, hn, gate, e = saved
    t = h.shape[0]
    tm = _row_tile(t)
    tk = _row_tile(t, 512)
    de, dz = _ple_gate_bwd(f"ple_gate_bwd{l}", dh, gate, e)
    full = lambda r: pl.BlockSpec((r, D_MODEL), lambda i, j, k: (0, 0))
    rowk = pl.BlockSpec((tk, D_MODEL), lambda i, j, k: (k, 0))
    (dwp,) = _mm(f"ple_dwp{l}", (1, 1, t // tk),
                 p, pl.BlockSpec((None, None, tk, PLE_DIM), lambda i, j, k: (l, 0, k, 0)),
                 de, rowk, (0, 0), [(_sds((PLE_DIM, D_MODEL), BF16), full(PLE_DIM))], deps=deps)
    (dwg,) = _mm(f"ple_dwg{l}", (1, 1, t // tk), hn, rowk, dz, rowk, (0, 0),
                 [(_sds((D_MODEL, D_MODEL), BF16), full(D_MODEL))])
    row = pl.BlockSpec((tm, D_MODEL), lambda i, j, k: (i, 0))
    dh_in, dg = _mm(f"ple_dhn{l}", (t // tm, 1, 1), dz, row, wg, full(D_MODEL), (1, 1),
                    _norm_bwd_outs(t, tm), extras=[(h, row), (dh, row), (norm_g, VEC)], epi=_norm_bwd_epi,
                    deps=emit(dwg, dwp) if emit else ())
    return dh_in, jnp.sum(dg, axis=0), dwg, dwp


def _ret_layer_fwd(x, norm_g, wri, fetch_wro, gn, cos, sin, next_gain, hn=None, deps=()):
    t = x.shape[0]
    tm = _row_tile(t)
    nsh, _, wsh = wri.shape
    if hn is None:
        hn = _rms_fwd("mix_norm0", x, norm_g)
    tp = _row_tile(t, 512)
    (proj,) = _mm_rows("ret_in", tp, hn, wri, 'nn_cols', [(_sds((t, RET_IN), BF16), _rows(tp, RET_IN))], deps=deps)
    gated, outp, states = _ret_fwd(proj, cos, sin, gn)
    wro = fetch_wro((gated,))
    row = _rows(tp, D_MODEL)
    more, epi = _residual_epi(next_gain)
    h1, hn_next = _mm_rows("ret_out", tp, gated, wro.reshape(RET_HEADS, RET_DV, D_MODEL), 'nn_rows',
                           _residual_outs(t, row, next_gain), extras=[(x, row)] + more, epi=epi)
    return h1, hn_next, (x, hn, proj, gated, outp, states, wro)


def _ret_layer_bwd(dh, saved, norm_g, wri, gn, cos, sin, emit_out, emit_in, deps=()):
    x, hn, proj, gated, outp, states, wro = saved
    t = x.shape[0]
    tm = _row_tile(t)
    tk = _row_tile(t, 512)
    nsh, _, wsh = wri.shape
    tg = _row_tile(t, 512)
    vw = _rows(tg, RET_V_W)
    dout, dgate, dgn = _mm_rows(
        "ret_dgate", tg, dh, wro.reshape(RET_HEADS, RET_DV, D_MODEL), 'nt_rows',
        [(_sds((t, RET_V_W), BF16), vw), (_sds((t, RET_V_W), BF16), vw),
         (_sds((t // tg, 1, RET_V_W), F32), pl.BlockSpec((None, 1, RET_V_W), lambda i, j, k: (i, 0, 0)))],
        extras=[(outp, vw), (proj, pl.BlockSpec((tg, RET_V_W), lambda i, j, k: (i, (RET_IN - RET_V_W) // RET_V_W))),
                (gn.reshape(1, RET_V_W), pl.BlockSpec((1, RET_V_W), lambda i, j, k: (0, 0)))],
        epi=_ret_gate_bwd_epi, deps=deps)
    dgn = jnp.sum(dgn, axis=0)
    (dwro,) = _mm("ret_dwro", (1, 1, t // tk),
                  gated, pl.BlockSpec((tk, RET_V_W), lambda i, j, k: (k, 0)),
                  dh, pl.BlockSpec((tk, D_MODEL), lambda i, j, k: (k, 0)), (0, 0),
                  [(_sds((RET_V_W, D_MODEL), BF16), pl.BlockSpec((RET_V_W, D_MODEL), lambda i, j, k: (0, 0)))])
    dproj = _ret_bwd(proj, cos, sin, states, dout, dgate, deps=emit_out(dwro))
    half = nsh // 2
    (dwri,) = _mm("ret_dwri", (2, 1, t // tk),
                  hn, pl.BlockSpec((tk, D_MODEL), lambda i, j, k: (k, 0)),
                  dproj, pl.BlockSpec((tk, half * wsh), lambda i, j, k: (k, i)), (0, 0),
                  [(_sds((nsh, D_MODEL, wsh), BF16), pl.BlockSpec((half, D_MODEL, wsh), lambda i, j, k: (i, 0, 0)))],
                  split=wsh)
    deps = emit_in(dwri)
    td = _row_tile(t, 256)
    row = _rows(td, D_MODEL)
    dx, dg = _mm_rows("ret_dhn", td, dproj, wri, 'nt_cols', _norm_bwd_outs(t, td),
                      extras=[(x, row), (dh, row), (norm_g, VEC)], epi=_norm_bwd_epi, deps=deps)
    return dx, jnp.sum(dg, axis=0), dgn.reshape(RET_HEADS, RET_DV)


def _mla_layer_fwd(h, hn, fetch, qa, kva, gq, gk, tabs, next_gain):
    t = h.shape[0]
    tm = _row_tile(t)
    row = pl.BlockSpec((tm, D_MODEL), lambda i, j, k: (i, 0))
    wmi = fetch('mla_in', (h,))['mla_w_in']
    (proj2,) = _mm("mla_in", (t // tm, 1, 1), hn, row,
                   wmi, pl.BlockSpec((D_MODEL, MLA_IN_PAD), lambda i, j, k: (0, 0)), (1, 0),
                   [(_sds((t, MLA_IN_PAD), F32), pl.BlockSpec((tm, MLA_IN_PAD), lambda i, j, k: (i, 0)))])
    cq, ckv = _mla_mid(proj2, qa, kva)
    up = fetch('mla_up', (cq,))
    wuq, wukv = up['mla_w_uq'], up['mla_w_ukv']
    qh, kh, vh = _mla_prep(cq, ckv, wuq, wukv, proj2, gq, gk, tabs)
    o, lse = _attn_fwd(qh, kh, vh)
    wmo = fetch('mla_out', (o,))['mla_w_out']
    more, epi = _residual_epi(next_gain)
    h_out, hn_next = _mm("mla_out", (t // tm, 1, 1), o, row,
                         wmo, pl.BlockSpec((D_MODEL, D_MODEL), lambda i, j, k: (0, 0)), (1, 0),
                         _residual_outs(t, row, next_gain), extras=[(h, row)] + more, epi=epi)
    return h_out, hn_next, (h, hn, proj2, cq, ckv, qh, kh, vh, o, lse), (wmi, wuq, wukv, wmo)


def _mla_layer_bwd(dh, saved, norm_g, wmi, qa, kva, wuq, wukv, gq, gk, wmo, tabs, deps=()):
    h, hn, proj2, cq, ckv, qh, kh, vh, o, lse = saved
    t = h.shape[0]
    tm = _row_tile(t)
    tk = _row_tile(t, 512)
    row = pl.BlockSpec((tm, D_MODEL), lambda i, j, k: (i, 0))
    rowk = pl.BlockSpec((tk, D_MODEL), lambda i, j, k: (k, 0))
    sq = pl.BlockSpec((D_MODEL, D_MODEL), lambda i, j, k: (0, 0))
    (dob,) = _mm("mla_do", (t // tm, 1, 1), dh, row, wmo, sq, (1, 1), [(_sds((t, D_MODEL), BF16), row)], deps=deps)
    (dwmo,) = _mm("mla_dwo", (1, 1, t // tk), o, rowk, dh, rowk, (0, 0), [(_sds((D_MODEL, D_MODEL), BF16), sq)])
    dqt, dkh, dvh = _attn_bwd(qh, kh, vh, dob, o, lse)
    dq, dkv, dkr, dgq, dgk = _mla_prep_bwd(cq, ckv, wuq, wukv, proj2, gq, gk, tabs, dqt, dkh, dvh)

    wide = MLA_HEADS * MLA_HD_PAD
    widek = pl.BlockSpec((tk, wide), lambda i, j, k: (k, 0))
    (dwuq,) = _mm("mla_dwuq", (1, 1, t // tk),
                  cq, pl.BlockSpec((tk, MLA_Q_RANK), lambda i, j, k: (k, 0)), dq, widek, (0, 0),
                  [(_sds((MLA_HEADS, MLA_Q_RANK, MLA_HD_PAD), BF16),
                    pl.BlockSpec((MLA_HEADS, MLA_Q_RANK, MLA_HD_PAD), lambda i, j, k: (0, 0, 0)))], split=MLA_HD_PAD)
    (dwukv,) = _mm("mla_dwukv", (1, 1, t // tk),
                   ckv, pl.BlockSpec((tk, MLA_KV_RANK), lambda i, j, k: (k, 0)), dkv, widek, (0, 0),
                   [(_sds((MLA_HEADS, MLA_KV_RANK, MLA_HD_PAD), BF16),
                     pl.BlockSpec((MLA_HEADS, MLA_KV_RANK, MLA_HD_PAD), lambda i, j, k: (0, 0, 0)))],
                   split=MLA_HD_PAD)
    side_by_side = lambda wg: wg.transpose(1, 0, 2).reshape(wg.shape[1], wide)
    widei = pl.BlockSpec((tm, wide), lambda i, j, k: (i, 0))
    (dcq,) = _mm("mla_dcq", (t // tm, 1, 1), dq, widei,
                 side_by_side(wuq), pl.BlockSpec((MLA_Q_RANK, wide), lambda i, j, k: (0, 0)), (1, 1),
                 [(_sds((t, MLA_Q_RANK), F32), pl.BlockSpec((tm, MLA_Q_RANK), lambda i, j, k: (i, 0)))])
    (dckv,) = _mm("mla_dckv", (t // tm, 1, 1), dkv, widei,
                  side_by_side(wukv), pl.BlockSpec((MLA_KV_RANK, wide), lambda i, j, k: (0, 0)), (1, 1),
                  [(_sds((t, MLA_KV_RANK), F32), pl.BlockSpec((tm, MLA_KV_RANK), lambda i, j, k: (i, 0)))])
    dproj2, dqa, dkva = _mla_mid_bwd(proj2, qa, kva, dcq, dckv, dkr)
    win = pl.BlockSpec((D_MODEL, MLA_IN_PAD), lambda i, j, k: (0, 0))
    (dwmi,) = _mm("mla_dwin", (1, 1, t // tk), hn, rowk,
                  dproj2, pl.BlockSpec((tk, MLA_IN_PAD), lambda i, j, k: (k, 0)), (0, 0),
                  [(_sds((D_MODEL, MLA_IN_PAD), BF16), win)])
    dh_in, dg = _mm("mla_dhn", (t // tm, 1, 1),
                    dproj2, pl.BlockSpec((tm, MLA_IN_PAD), lambda i, j, k: (i, 0)), wmi, win, (1, 1),
                    _norm_bwd_outs(t, tm), extras=[(h, row), (dh, row), (norm_g, VEC)], epi=_norm_bwd_epi)
    return dh_in, dict(mix=jnp.sum(dg, axis=0), wmi=dwmi, qa=dqa, kva=dkva, wuq=dwuq, wukv=dwukv, gq=dgq, gk=dgk,
                       wmo=dwmo)


def _local_step(x, p, target, w, fetch, emit=lambda group: ()):
    t = x.shape[0]
    cos_r, sin_r, tabs = w['tables'] if 'tables' in w else _rope_tables(t, 0.0)
    row = lambda a, i: a[i:i + 1]

    h1, hn1, s_ret = _ret_layer_fwd(x, row(w['mix_norm'], 0), w['ret_w_in'],
                                    lambda after: fetch('ret_out', after)['ret_w_out'], w['ret_gn'], cos_r, sin_r,
                                    row(w['mlp_norm'], 0), hn=w.get('hn0'), deps=w['deps'])
    h2, hn2, s_mlp0 = _mlp_fwd(0, h1, hn1, fetch('mlp_w1_0', (h1,))['mlp_w1'],
                               lambda after: fetch('mlp_w2_0', after)['mlp_w2'], row(w['ple_norm'], 0))
    w0 = fetch('ple_0', (h2,))
    h3, hn3, s_ple0 = _ple_fwd(0, h2, hn2, p, w0['ple_gate_w'], w0['ple_proj_w'], row(w['mix_norm'], 1))
    h4, hn4, s_mla, (wmi, wuq, wukv, wmo) = _mla_layer_fwd(
        h3, hn3, fetch, w['mla_q_a_norm'], w['mla_kv_a_norm'], w['mla_q_norm'], w['mla_k_norm'], tabs,
        row(w['mlp_norm'], 1))
    mla_w = (wmi, w['mla_q_a_norm'], w['mla_kv_a_norm'], wuq, wukv, w['mla_q_norm'], w['mla_k_norm'], wmo, tabs)
    w1 = fetch('layer_1', (h4,))
    h5, hn5, s_mlp1 = _mlp_fwd(1, h4, hn4, w1['mlp_w1'], lambda after: w1['mlp_w2'], row(w['ple_norm'], 1))
    dy, sq_err, s_ple1 = _ple_fwd(1, h5, hn5, p, w1['ple_gate_w'], w1['ple_proj_w'], None, target)

    n = N_DEV
    colsh = lambda a: a.reshape(a.shape[0], n, a.shape[1] // n).transpose(1, 0, 2)
    rowsh = lambda a: a.reshape(n, a.shape[0] // n, a.shape[1])
    big = {}

    def emit_group(group):
        big.update(group)
        return emit(group)

    dh5, dg_ple1, dwg1, dwp1 = _ple_bwd(1, dy, s_ple1, p, row(w['ple_norm'], 1), w1['ple_gate_w'])
    dh4, dg_mlp1, dw1_1, dw2_1 = _mlp_bwd(1, dh5, s_mlp1, row(w['mlp_norm'], 1))
    deps = emit_group({('ple_gate_w', 1): rowsh(dwg1), ('ple_proj_w', 1): colsh(dwp1),
                       ('mlp_w2', 1): dw2_1, ('mlp_w1', 1): dw1_1})
    dh3, gm = _mla_layer_bwd(dh4, s_mla, row(w['mix_norm'], 1), *mla_w, deps=deps)
    deps = emit_group({('mla_w_out', 0): rowsh(gm['wmo']), ('mla_w_uq', 0): _gather_rope(gm['wuq']),
                       ('mla_w_ukv', 0): gm['wukv'], ('mla_w_in', 0): rowsh(_gather_rope(gm['wmi']))})
    dh2, dg_ple0, _, _ = _ple_bwd(
        0, dh3, s_ple0, p, row(w['ple_norm'], 0), w0['ple_gate_w'], deps=deps,
        emit=lambda dwg, dwp: emit_group({('ple_gate_w', 0): rowsh(dwg), ('ple_proj_w', 0): colsh(dwp)}))
    dh1, dg_mlp0, _, _ = _mlp_bwd(0, dh2, s_mlp0, row(w['mlp_norm'], 0),
                                  emit_w2=lambda dw2: emit_group({('mlp_w2', 0): dw2}),
                                  emit_w1=lambda dw1: emit_group({('mlp_w1', 0): dw1}))
    dx, dg_mix0, dgn = _ret_layer_bwd(
        dh1, s_ret, row(w['mix_norm'], 0), w['ret_w_in'], w['ret_gn'], cos_r, sin_r,
        lambda dwro: emit_group({('ret_w_out', 0): rowsh(dwro)}),
        lambda dwri: emit_group({('ret_w_in', 0): dwri}))

    small = dict(
        mix_norm=[dg_mix0, gm['mix']], mlp_norm=[dg_mlp0, dg_mlp1], ple_norm=[dg_ple0, dg_ple1],
        ret_gn=dgn, mla_q_a_norm=gm['qa'], mla_kv_a_norm=gm['kva'], mla_q_norm=gm['gq'], mla_k_norm=gm['gk'],
    )
    return sq_err, dx, big, small


def _my_place():
    x, y, c = lax.axis_index("x"), lax.axis_index("y"), lax.axis_index("c")
    return x, y, c


def _flat(px, py, pc):
    return 4 * px + 2 * py + pc


def _peer(x, y, c, r):
    return (1 - x if r & 4 else x, 1 - y if r & 2 else y, 1 - c if r & 1 else c)


HBM = pl.BlockSpec(memory_space=pltpu.HBM)
SEMS = pl.BlockSpec(memory_space=pltpu.SEMAPHORE)
SIDE_EFFECT = pltpu.SideEffectType.DATAFLOW_SIDE_EFFECTING


def _rs_copies(x, y, c, srcs, lands, send_sems, recv_sems):
    copies = []
    for a in range(len(srcs)):
        for r in range(1, N_DEV):
            peer = _peer(x, y, c, r)
            k = a * (N_DEV - 1) + r - 1
            copies.append(pltpu.make_async_remote_copy(
                src_ref=srcs[a].at[_flat(*peer)], dst_ref=lands[a].at[r - 1],
                send_sem=send_sems.at[k], recv_sem=recv_sems.at[k], device_id=peer, device_id_type=MESH))
    return copies


def _rs_start(name, arrays):
    n = len(arrays)
    hbm = lambda a: pltpu.with_memory_space_constraint(a, pltpu.HBM)
    lands = [hbm(lax.empty((N_DEV - 1,) + a.shape[1:], a.dtype)) for a in arrays]

    def body(*refs):
        srcs, lnd = refs[:n], refs[n:2 * n]
        send_sems, recv_sems = refs[2 * n], refs[2 * n + 1]
        token = refs[-1]
        for cp in _rs_copies(*_my_place(), srcs, lnd, send_sems, recv_sems):
            cp.start()
        token[...] = jnp.zeros_like(token)

    outs = pl.pallas_call(
        body, name=name,
        in_specs=[HBM] * (2 * n),
        out_specs=[SEMS, SEMS] + [HBM] * (2 * n) + [pl.BlockSpec(memory_space=pltpu.VMEM)],
        out_shape=[pltpu.SemaphoreType.DMA((n * (N_DEV - 1),)), pltpu.SemaphoreType.DMA((n * (N_DEV - 1),))]
        + [pltpu.HBM(a.shape, a.dtype) for a in arrays] + [pltpu.HBM(l.shape, l.dtype) for l in lands]
        + [_sds((8, 128), F32)],
        input_output_aliases={i: 2 + i for i in range(2 * n)},
        compiler_params=pltpu.CompilerParams(has_side_effects=SIDE_EFFECT),
    )(*[hbm(a) for a in arrays], *lands)
    return outs[0], outs[1], outs[2:2 + n], outs[2 + n:2 + 2 * n], outs[-1]


def _rs_wait(name, send_sems, recv_sems, srcs, lands, after):
    n = len(srcs)

    def body(*refs):
        src_refs, lnd = refs[:n], refs[n:2 * n]
        send, recv = refs[2 * n], refs[2 * n + 1]
        for cp in _rs_copies(*_my_place(), src_refs, lnd, send, recv):
            cp.wait_send()
            cp.wait_recv()

    outs = pl.pallas_call(
        body, name=name,
        in_specs=[HBM] * (2 * n) + [SEMS, SEMS] + [ANY] * len(after),
        out_specs=[HBM] * (2 * n),
        out_shape=[pltpu.HBM(a.shape, a.dtype) for a in list(srcs) + list(lands)],
        input_output_aliases={i: i for i in range(2 * n)},
        compiler_params=pltpu.CompilerParams(has_side_effects=SIDE_EFFECT),
    )(*srcs, *lands, send_sems, recv_sems, *after)
    return outs[:n], outs[n:]


SMALL_PACK_ROWS = 16


def _all_reduce_small(rows, deps=()):
    n = len(rows)

    def body(*refs):
        ins = refs[:n]
        out_ref, mine, buf, send_sems, recv_sems = refs[n + len(deps):]
        x, y, c = _my_place()
        mine[...] = jnp.zeros_like(mine)
        for (r0, a), ref in zip(rows, ins):
            mine[r0:r0 + a.shape[0], 0:a.shape[1]] = ref[...]
        buf[_flat(x, y, c)] = mine[...]
        copies = []
        for r in range(1, N_DEV):
            peer = _peer(x, y, c, r)
            send = pltpu.make_async_remote_copy(
                src_ref=mine, dst_ref=buf.at[_flat(x, y, c)],
                send_sem=send_sems.at[r - 1], recv_sem=recv_sems.at[r - 1], device_id=peer, device_id_type=MESH)
            send.start()
            recv = pltpu.make_async_remote_copy(
                src_ref=mine, dst_ref=buf.at[_flat(*peer)],
                send_sem=send_sems.at[r - 1], recv_sem=recv_sems.at[r - 1], device_id=peer, device_id_type=MESH)
            copies.append((send, recv))
        for send, recv in copies:
            send.wait_send()
            recv.wait_recv()
        acc = buf[0]
        for s in range(1, N_DEV):
            acc = acc + buf[s]
        out_ref[...] = acc

    vm = pl.BlockSpec(memory_space=pltpu.VMEM)
    shape = (SMALL_PACK_ROWS, D_MODEL)
    return pl.pallas_call(
        body, name="all_reduce_small", in_specs=[vm] * n + [ANY] * len(deps), out_specs=vm,
        out_shape=_sds(shape, F32),
        scratch_shapes=[pltpu.VMEM(shape, F32), pltpu.VMEM((N_DEV,) + shape, F32),
                        pltpu.SemaphoreType.DMA((7,)), pltpu.SemaphoreType.DMA((7,))],
    )(*[a for _, a in rows], *deps)


def _adamw_math(w, g, m, v):
    m = ADAM_B1 * m + (1.0 - ADAM_B1) * g
    v = ADAM_B2 * v + (1.0 - ADAM_B2) * (g * g)
    m_hat = m / (1.0 - ADAM_B1 ** ADAM_STEP)
    v_hat = v / (1.0 - ADAM_B2 ** ADAM_STEP)
    delta = -ADAM_LR * (m_hat / (jnp.sqrt(v_hat) + ADAM_EPS) + ADAM_WD * w)
    return delta, m, v


def _adamw_big(name, w, m, v, srcs, lands, me):
    nl, rows, cols = w.shape
    tr = next(cand for cand in (256, 128, 64, 32, 16, 8) if rows % cand == 0)

    def body(me_ref, w_ref, m_ref, v_ref, *rest):
        src_refs, land_refs = rest[:nl], rest[nl:2 * nl]
        g_ref, d_ref, mo_ref, vo_ref = rest[2 * nl:]
        for layer in range(nl):
            @pl.when(pl.program_id(0) == layer)
            def _():
                g = src_refs[layer][...].astype(F32)
                for s in range(N_DEV - 1):
                    g = g + land_refs[layer][s].astype(F32)
                delta, mn, vn = _adamw_math(w_ref[...], g, m_ref[...], v_ref[...])
                g_ref[...] = g
                d_ref[...] = delta
                mo_ref[...] = mn
                vo_ref[...] = vn

    blk = pl.BlockSpec((None, tr, cols), lambda l, i, me_ref: (l, i, 0))
    at = lambda layer, l, i: jnp.where(l == layer, i, 0)
    own = [pl.BlockSpec((None, tr, cols), functools.partial(lambda layer, l, i, me_ref: (me_ref[0], at(layer, l, i), 0),
                                                            layer)) for layer in range(nl)]
    peers = [pl.BlockSpec((N_DEV - 1, tr, cols), functools.partial(lambda layer, l, i, me_ref: (0, at(layer, l, i), 0),
                                                                   layer)) for layer in range(nl)]
    return pl.pallas_call(
        body, name=name,
        grid_spec=pltpu.PrefetchScalarGridSpec(
            num_scalar_prefetch=1, grid=(nl, rows // tr),
            in_specs=[blk, blk, blk] + own + peers, out_specs=[blk] * 4),
        out_shape=[_sds((nl, rows, cols), F32)] * 4,
        compiler_params=_cparams(("arbitrary", "arbitrary")),
    )(me, w, m, v, *srcs, *lands)


def _adamw_small(ws, gs, ms, vs):
    n = len(ws)

    def body(*refs):
        w_refs, g_refs, m_refs, v_refs = (refs[i * n:(i + 1) * n] for i in range(4))
        d_out, m_out, v_out = (refs[(4 + i) * n:(5 + i) * n] for i in range(3))
        for i in range(n):
            delta, mn, vn = _adamw_math(w_refs[i][...], g_refs[i][...], m_refs[i][...], v_refs[i][...])
            d_out[i][...] = delta
            m_out[i][...] = mn
            v_out[i][...] = vn

    vm = pl.BlockSpec(memory_space=pltpu.VMEM)
    outs = pl.pallas_call(
        body, name="adamw_small", in_specs=[vm] * (4 * n), out_specs=[vm] * (3 * n),
        out_shape=[_sds(a.shape, F32) for a in ws] * 3,
    )(*ws, *gs, *ms, *vs)
    return outs[:n], outs[n:2 * n], outs[2 * n:]


def _pad_to(a, rows, cols):
    return jnp.pad(a, ((0, rows - a.shape[0]), (0, cols - a.shape[1])))


def _place_own(blocks):
    me = _flat(*_my_place())
    return [lax.dynamic_update_slice(lax.empty((N_DEV,) + b.shape, b.dtype), b[None], (me,) + (0,) * b.ndim)
            for b in blocks]


def _ag_copies(x, y, c, blocks, bufs, send_sems, recv_sems, arriving):
    copies = []
    for a in range(len(blocks)):
        for r in range(1, N_DEV):
            peer = _peer(x, y, c, r)
            k = a * (N_DEV - 1) + r - 1
            copies.append(pltpu.make_async_remote_copy(
                src_ref=blocks[a], dst_ref=bufs[a].at[_flat(*(peer if arriving else (x, y, c)))],
                send_sem=send_sems.at[k], recv_sem=recv_sems.at[k], device_id=peer, device_id_type=MESH))
    return copies


def _ag_start(groups, after):
    flat = [pair for g in groups for pair in g]
    n, ng = len(flat), len(groups)
    hbm = lambda a: pltpu.with_memory_space_constraint(a, pltpu.HBM)

    def body(*refs):
        blocks, bufs = refs[:n], refs[n:2 * n]
        sems = refs[2 * n + len(after):2 * n + len(after) + 2 * ng]
        x, y, c = _my_place()
        at = 0
        for gi, g in enumerate(groups):
            for cp in _ag_copies(x, y, c, blocks[at:at + len(g)], bufs[at:at + len(g)], sems[2 * gi],
                                 sems[2 * gi + 1], arriving=False):
                cp.start()
            at += len(g)
        refs[-1][...] = jnp.zeros_like(refs[-1])

    sem_shapes = [pltpu.SemaphoreType.DMA((len(g) * (N_DEV - 1),)) for g in groups for _ in range(2)]
    outs = pl.pallas_call(
        body, name="gather_start",
        in_specs=[HBM] * (2 * n) + [ANY] * len(after),
        out_specs=[SEMS] * (2 * ng) + [HBM] * (2 * n) + [pl.BlockSpec(memory_space=pltpu.VMEM)],
        out_shape=sem_shapes + [pltpu.HBM(b.shape, b.dtype) for b, _ in flat]
        + [pltpu.HBM(u.shape, u.dtype) for _, u in flat] + [_sds((8, 128), F32)],
        input_output_aliases={i: 2 * ng + i for i in range(2 * n)},
        compiler_params=pltpu.CompilerParams(has_side_effects=SIDE_EFFECT),
    )(*[hbm(b) for b, _ in flat], *[hbm(u) for _, u in flat], *after)
    blocks_thru, bufs_thru = outs[2 * ng:2 * ng + n], outs[2 * ng + n:2 * ng + 2 * n]
    started, at = [], 0
    for gi, g in enumerate(groups):
        started.append((outs[2 * gi], outs[2 * gi + 1], blocks_thru[at:at + len(g)], bufs_thru[at:at + len(g)]))
        at += len(g)
    return started, outs[-1]


def _ag_wait(name, send_sems, recv_sems, blocks, bufs, after):
    n = len(blocks)

    def body(*refs):
        for cp in _ag_copies(*_my_place(), refs[:n], refs[n:2 * n], refs[2 * n], refs[2 * n + 1], arriving=True):
            cp.wait_send()
            cp.wait_recv()

    outs = pl.pallas_call(
        body, name=name,
        in_specs=[HBM] * (2 * n) + [SEMS, SEMS] + [ANY] * len(after),
        out_specs=[HBM] * (2 * n),
        out_shape=[pltpu.HBM(a.shape, a.dtype) for a in list(blocks) + list(bufs)],
        input_output_aliases={i: i for i in range(2 * n)},
        compiler_params=pltpu.CompilerParams(has_side_effects=SIDE_EFFECT),
    )(*blocks, *bufs, send_sems, recv_sems, *after)
    return outs[n:]


def _split_call(name, body, thru, sems_in, new_sems, after):
    n, ns, nn = len(thru), len(sems_in), len(new_sems)
    hbm = lambda a: pltpu.with_memory_space_constraint(a, pltpu.HBM)

    def wrapped(*refs):
        body(refs[:n], refs[n:n + ns], refs[n + ns + len(after):n + ns + len(after) + nn])
        refs[-1][...] = jnp.zeros_like(refs[-1])

    outs = pl.pallas_call(
        wrapped, name=name,
        in_specs=[HBM] * n + [SEMS] * ns + [ANY] * len(after),
        out_specs=[SEMS] * nn + [HBM] * n + [pl.BlockSpec(memory_space=pltpu.VMEM)],
        out_shape=[pltpu.SemaphoreType.DMA((k,)) for k in new_sems] + [pltpu.HBM(a.shape, a.dtype) for a in thru]
        + [_sds((8, 128), F32)],
        input_output_aliases={i: nn + i for i in range(n)},
        compiler_params=pltpu.CompilerParams(has_side_effects=SIDE_EFFECT),
    )(*[hbm(a) for a in thru], *sems_in, *after)
    return list(outs[:nn]), list(outs[nn:nn + n]), outs[-1]


def _two_level_gather(name, blocks, bufs, after=()):
    n = len(blocks)

    def copies(refs, s1, r1, s2, r2):
        x, y, c = _my_place()
        me, sibling = (x, y, c), (x, y, 1 - c)
        chips = [(1 - x, y), (x, 1 - y), (1 - x, 1 - y)]
        blk, buf = refs[:n], refs[n:]
        out = dict(send1=[], recv1_sib=[], recv1_ici=[], send2=[], recv2=[])
        for a in range(n):
            place = lambda dev: buf[a].at[_flat(*dev)]
            for k, to in enumerate([sibling] + [(*chip, c) for chip in chips]):
                mk = lambda dst: pltpu.make_async_remote_copy(
                    src_ref=blk[a], dst_ref=dst, send_sem=s1.at[4 * a + k], recv_sem=r1.at[4 * a + k],
                    device_id=to, device_id_type=MESH)
                out['send1'].append(mk(place(me)))
                out['recv1_sib' if k == 0 else 'recv1_ici'].append(mk(place(to)))
            for j, chip in enumerate(chips):
                mk = lambda dev: pltpu.make_async_remote_copy(
                    src_ref=place(dev), dst_ref=place(dev), send_sem=s2.at[3 * a + j], recv_sem=r2.at[3 * a + j],
                    device_id=sibling, device_id_type=MESH)
                out['send2'].append(mk((*chip, c)))
                out['recv2'].append(mk((*chip, 1 - c)))
        return out

    def start(refs, sems_in, new):
        for cp in copies(refs, new[0], new[1], new[0], new[1])['send1']:
            cp.start()

    def forward(refs, sems_in, new):
        cps = copies(refs, sems_in[0], sems_in[1], new[0], new[1])
        for cp in cps['recv1_ici']:
            cp.wait_recv()
        for cp in cps['send2']:
            cp.start()

    def finish(refs, sems_in, new):
        cps = copies(refs, *sems_in)
        for cp in cps['recv1_sib'] + cps['recv2']:
            cp.wait_recv()
        for cp in cps['send1'] + cps['send2']:
            cp.wait_send()

    sems1, thru, token = _split_call(name + "_start", start, list(blocks) + list(bufs), [], [4 * n, 4 * n], after)

    def complete(after):
        sems2, thru2, token2 = _split_call(name + "_forward", forward, thru, sems1, [3 * n, 3 * n], after)
        _, thru3, _ = _split_call(name + "_wait", finish, thru2, sems1 + sems2, [], ())
        return thru3[n:], token2

    return token, complete


def _prepare_weights(p, x):
    n = N_DEV
    bf = lambda a: a.astype(BF16)
    gn_pack = jnp.concatenate([
        _pad_to(p['ret_gn'][0], RET_HEADS, 128), _pad_to(p['mla_q_a_norm'], 1, 128),
        _pad_to(p['mla_kv_a_norm'], 1, 128), jnp.zeros((2, 128), F32)], axis=0)
    ple = lambda l: [bf(p['ple_gate_w'][l]), bf(p['ple_proj_w'][l])]
    names = ('mlp_w1_0', 'mlp_w2_0', 'ple_0', 'mla_in', 'mla_up', 'mla_out', 'layer_1')
    later = [[bf(p['mlp_w1'][0])], [bf(p['mlp_w2'][0])], ple(0),
             [bf(p['mla_w_in'][0])], [bf(p['mla_w_uq'][0]), bf(p['mla_w_ukv'][0])], [bf(p['mla_w_out'][0])],
             [bf(p['mlp_w1'][1]), bf(p['mlp_w2'][1])] + ple(1)]
    first = [gn_pack, bf(p['ret_w_in'][0])]
    second = [bf(p['ret_w_out'][0])]
    token, complete_first = _two_level_gather("first_gather", first, _place_own(first))
    token2, complete_second = _two_level_gather("second_gather", second, _place_own(second), (token,))
    hn0 = _rms_fwd("mix_norm0", x, p['mix_norm'][0:1], deps=(token2,))
    bufs = _place_own([b for g in later for b in g])
    tables = _rope_tables(x.shape[0], token2[0, 0])
    (pack, wri), token = complete_first((hn0, tables[0], tables[1], *tables[2], *bufs))
    groups, at = [], 0
    for g in later:
        groups.append(list(zip(g, bufs[at:at + len(g)])))
        at += len(g)
    started, token = _ag_start(groups, (token,))

    w = {k: p[k] for k in ('mix_norm', 'mlp_norm', 'ple_norm')}
    w['hn0'] = hn0
    w['tables'] = tables
    w['ret_gn'] = pack[:, :RET_HEADS, :RET_DV // n].transpose(1, 0, 2).reshape(RET_HEADS, RET_DV)
    w['mla_q_a_norm'] = pack[:, RET_HEADS, :MLA_Q_RANK // n].reshape(1, MLA_Q_RANK)
    w['mla_kv_a_norm'] = pack[:, RET_HEADS + 1, :MLA_KV_RANK // n].reshape(1, MLA_KV_RANK)
    w['ret_w_in'] = wri
    w['mla_q_norm'] = _spread_rope(p['mla_q_norm'])
    w['mla_k_norm'] = _spread_rope(p['mla_k_norm'])
    w['deps'] = (token,)

    def fetch(name, after):
        if name == 'ret_out':
            return dict(ret_w_out=complete_second(after)[0][0].reshape(RET_V_W, D_MODEL))
        got = list(_ag_wait("gather_wait_" + name, *started[names.index(name)], after))
        if name == 'mla_in':
            return dict(mla_w_in=_spread_rope(got[0].reshape(D_MODEL, MLA_IN)))
        if name == 'mla_up':
            return dict(mla_w_uq=_spread_rope(got[0]), mla_w_ukv=got[1])
        if name == 'mla_out':
            return dict(mla_w_out=got[0].reshape(D_MODEL, D_MODEL))
        out = {}
        if name in ('mlp_w1_0', 'layer_1'):
            out['mlp_w1'] = got.pop(0)
        if name in ('mlp_w2_0', 'layer_1'):
            out['mlp_w2'] = got.pop(0)
        if name in ('ple_0', 'layer_1'):
            out['ple_gate_w'] = got[0].reshape(D_MODEL, D_MODEL)
            out['ple_proj_w'] = got[1].transpose(1, 0, 2).reshape(PLE_DIM, D_MODEL)
        return out

    return w, fetch


def _small_grads(small, after):
    rows = [(0, small['mix_norm'][0]), (1, small['mix_norm'][1]), (2, small['mlp_norm'][0]),
            (3, small['mlp_norm'][1]), (4, small['ple_norm'][0]), (5, small['ple_norm'][1]),
            (6, small['ret_gn']), (10, small['mla_q_a_norm']), (11, small['mla_kv_a_norm']),
            (12, small['mla_q_norm']), (13, small['mla_k_norm']), (14, small['sq_err'])]
    gs = _all_reduce_small(rows, after)
    me = _flat(*_my_place())
    n = N_DEV
    return dict(
        sq_err=gs[14, 0],
        mix_norm=gs[0:2], mlp_norm=gs[2:4], ple_norm=gs[4:6],
        ret_gn=lax.dynamic_slice(gs, (6, me * (RET_DV // n)), (RET_HEADS, RET_DV // n)),
        mla_q_a_norm=lax.dynamic_slice(gs, (10, me * (MLA_Q_RANK // n)), (1, MLA_Q_RANK // n)),
        mla_kv_a_norm=lax.dynamic_slice(gs, (11, me * (MLA_KV_RANK // n)), (1, MLA_KV_RANK // n)),
        mla_q_norm=_gather_rope(gs[12:13, :MLA_HD_PAD]), mla_k_norm=_gather_rope(gs[13:14, :MLA_HD_PAD]))


def kernel(x, p, mix_norm, ret_w_in, ret_gn, ret_w_out, mla_w_in, mla_q_a_norm, mla_kv_a_norm, mla_w_uq, mla_w_ukv, mla_q_norm, mla_k_norm, mla_w_out, mlp_norm, mlp_w1, mlp_w2, ple_norm, ple_gate_w, ple_proj_w, loss_target, m_mix_norm, m_ret_w_in, m_ret_gn, m_ret_w_out, m_mla_w_in, m_mla_q_a_norm, m_mla_kv_a_norm, m_mla_w_uq, m_mla_w_ukv, m_mla_q_norm, m_mla_k_norm, m_mla_w_out, m_mlp_norm, m_mlp_w1, m_mlp_w2, m_ple_norm, m_ple_gate_w, m_ple_proj_w, v_mix_norm, v_ret_w_in, v_ret_gn, v_ret_w_out, v_mla_w_in, v_mla_q_a_norm, v_mla_kv_a_norm, v_mla_w_uq, v_mla_w_ukv, v_mla_q_norm, v_mla_k_norm, v_mla_w_out, v_mlp_norm, v_mlp_w1, v_mlp_w2, v_ple_norm, v_ple_gate_w, v_ple_proj_w):
    given = dict(locals())
    params = {n: given[n] for n in WEIGHTS}
    w, fetch = _prepare_weights(params, x[0])

    started = []

    def emit(group):
        keys = list(group)
        send, recv, srcs, lands, token = _rs_start(f"rs_start{len(started)}", [group[k] for k in keys])
        started.append((keys, send, recv, srcs, lands))
        return (token,)

    sq_err, grad_x, _, small = _local_step(x[0], p, loss_target[0], w, fetch, emit)
    small['sq_err'] = sq_err[0:1]

    grads, deltas, new_m, new_v = {}, {}, {}, {}
    total = {}

    def small_updates(after):
        sg = _small_grads(small, after)
        total['loss'] = 0.5 / D_MODEL * sg['sq_err']
        two_d = lambda a: a.reshape(-1, a.shape[-1])
        d_s, m_s, v_s = _adamw_small(
            [two_d(params[n]) for n in SMALL], [sg[n] for n in SMALL],
            [two_d(given["m_" + n]) for n in SMALL], [two_d(given["v_" + n]) for n in SMALL])
        for i, n in enumerate(SMALL):
            shape = params[n].shape
            grads[n], deltas[n], new_m[n], new_v[n] = (a.reshape(shape) for a in (sg[n], d_s[i], m_s[i], v_s[i]))
        return (d_s[0],)

    me = _flat(*_my_place()).astype(jnp.int32).reshape(1)
    after = (grad_x,)
    src_of, land_of = {}, {}
    for gi, (keys, send, recv, srcs, lands) in enumerate(started):
        if gi == len(started) - 1:
            after = small_updates(after)
        srcs, lands = _rs_wait(f"rs_wait{gi}", send, recv, srcs, lands, after)
        for k, s, l in zip(keys, srcs, lands):
            src_of[k], land_of[k] = s, l
        done = [n for n in BIG if n not in grads and all((n, l) in src_of for l in range(params[n].shape[0]))]
        for n in done:
            layers = range(params[n].shape[0])
            grads[n], deltas[n], new_m[n], new_v[n] = _adamw_big(
                "adamw_" + n, params[n], given["m_" + n], given["v_" + n],
                [src_of[(n, l)] for l in layers], [land_of[(n, l)] for l in layers], me)
        if done:
            after = tuple(deltas[n] for n in done)

    return (total['loss'], grad_x[None], *[grads[n] for n in WEIGHTS], *[deltas[n] for n in WEIGHTS],
            *[new_m[n] for n in WEIGHTS], *[new_v[n] for n in WEIGHTS])
```
